```python
import math
import jax
import jax.numpy as jnp
from jax import lax
import numpy as np

D_MODEL = 1024
BATCH = 8
SEQ = 2048
DEPTH = 2

CONV_A_GROUPS = 4
CONV_A_GROUP_DIM = 64
D_CONV_A = CONV_A_GROUPS * CONV_A_GROUP_DIM
CONV_A_WIDTH = 3
SSD_HEADS = 6
SSD_HEAD_DIM = 64
D_SSD = SSD_HEADS * SSD_HEAD_DIM
SSD_GROUPS = 2
SSD_STATE = 128
SSD_CONV_WIDTH = 4
SSD_CHUNK = 128
SSD_CONV_DIM = D_SSD + 2 * SSD_GROUPS * SSD_STATE
SSD_NORM_EPS = 1e-5
MLA_HEADS = 6
Q_LORA = 256
KV_LORA = 128
QK_NOPE = 64
QK_ROPE = 32
V_DIM = 64
D_MLA = MLA_HEADS * V_DIM
ROPE_BASE = 10000.0
Q_BLOCK = 128
D_MIX = D_CONV_A + D_SSD + D_MLA
NORM_EPS = 1e-6
POS_OFFSET_MAX = 1024
SPLIT_SIZES = (D_CONV_A, D_CONV_A, D_CONV_A, D_CONV_A,
               D_SSD, D_SSD, SSD_GROUPS * SSD_STATE, SSD_GROUPS * SSD_STATE, SSD_HEADS,
               Q_LORA, KV_LORA, QK_ROPE, D_MLA)
IN_COLS = sum(SPLIT_SIZES)

kernel_name = 'hybrid_conv_ssd_mla_parallel'


def rmsnorm(x, g, eps=NORM_EPS):
    xf = x.astype(jnp.float32)
    y = xf * lax.rsqrt(jnp.mean(xf * xf, axis=-1, keepdims=True) + eps)
    return (y * g.astype(jnp.float32)).astype(x.dtype)


def causal_depthwise_conv(u, w):
    k, c = w.shape
    return lax.conv_general_dilated(
        u, w[:, None, :].astype(u.dtype), window_strides=(1,), padding=[(k - 1, 0)],
        dimension_numbers=('NWC', 'WIO', 'NWC'), feature_group_count=c)


def apply_rope(t, cos, sin):
    tf = t.astype(jnp.float32)
    t1, t2 = jnp.split(tf, 2, axis=-1)
    return jnp.concatenate([t1 * cos - t2 * sin, t2 * cos + t1 * sin], axis=-1).astype(t.dtype)


def short_conv_branch(a_h, a_b, a_c, a_z, conv_w):
    return a_b * causal_depthwise_conv(a_c * a_h, conv_w) * jax.nn.silu(a_z)


def segsum_exp(a_cs):
    l = a_cs.shape[-1]
    diff = a_cs[..., :, None] - a_cs[..., None, :]
    mask = jnp.tril(jnp.ones((l, l), dtype=bool))
    return jnp.exp(jnp.where(mask, diff, -jnp.inf))


def ssd_chunked(xh, dt, a, bh, ch):
    b, s, h, p = xh.shape
    n = bh.shape[-1]
    nc = s // SSD_CHUNK
    la = (dt * a).reshape(b, nc, SSD_CHUNK, h).transpose(0, 3, 1, 2)
    xd = (xh * dt[..., None]).reshape(b, nc, SSD_CHUNK, h, p)
    bc = bh.reshape(b, nc, SSD_CHUNK, h, n)
    cc = ch.reshape(b, nc, SSD_CHUNK, h, n)
    a_cs = jnp.cumsum(la, axis=-1)
    scores = jnp.einsum('bclhn,bcshn->bhcls', cc, bc) * segsum_exp(a_cs)
    y_diag = jnp.einsum('bhcls,bcshp->bclhp', scores, xd)
    decay_states = jnp.exp(a_cs[..., -1:] - a_cs)
    states = jnp.einsum('bclhn,bhcl,bclhp->bchpn', bc, decay_states, xd)
    chunk_decay = jnp.exp(a_cs[..., -1])

    def step(carry, inp):
        st, dec = inp
        return carry * dec[..., None, None] + st, carry

    init = jnp.zeros((b, h, p, n), dtype=xd.dtype)
    _, prev = lax.scan(step, init, (jnp.moveaxis(states, 1, 0), jnp.moveaxis(chunk_decay, 2, 0)))
    prev = jnp.moveaxis(prev, 0, 1)
    y_off = jnp.einsum('bclhn,bchpn,bhcl->bclhp', cc, prev, jnp.exp(a_cs))
    return (y_diag + y_off).reshape(b, s, h, p)


def ssd_branch(s_z, s_x, s_b, s_c, s_dt, conv_w, conv_b, dt_bias, a_log, d_skip, norm_g):
    b, s, _ = s_x.shape
    f32 = jnp.float32
    xbc = jnp.concatenate([s_x, s_b, s_c], axis=-1)
    xbc = jax.nn.silu(causal_depthwise_conv(xbc, conv_w) + conv_b)
    xs, bs, cs = jnp.split(xbc, [D_SSD, D_SSD + SSD_GROUPS * SSD_STATE], axis=-1)
    rep = SSD_HEADS // SSD_GROUPS
    xh = xs.reshape(b, s, SSD_HEADS, SSD_HEAD_DIM).astype(f32)
    bh = jnp.repeat(bs.reshape(b, s, SSD_GROUPS, SSD_STATE), rep, axis=2).astype(f32)
    ch = jnp.repeat(cs.reshape(b, s, SSD_GROUPS, SSD_STATE), rep, axis=2).astype(f32)
    dt = jax.nn.softplus(s_dt.astype(f32) + dt_bias.astype(f32))
    a = -jnp.exp(a_log.astype(f32))
    y = ssd_chunked(xh, dt, a, bh, ch) + xh * d_skip.astype(f32)[:, None]
    y = y.reshape(b, s, D_SSD).astype(s_x.dtype)
    g = (y * jax.nn.silu(s_z)).reshape(b, s, SSD_GROUPS, D_SSD // SSD_GROUPS)
    g = rmsnorm(g, norm_g.reshape(SSD_GROUPS, D_SSD // SSD_GROUPS), SSD_NORM_EPS)
    return g.reshape(b, s, D_SSD)


def causal_block_attention(q_nope, q_rope, k_nope, k_rope, v):
    b, s, h, _ = q_nope.shape
    nb = s // Q_BLOCK
    scale = (QK_NOPE + QK_ROPE) ** -0.5
    kpos = jnp.arange(s)

    def to_blocks(t):
        return jnp.swapaxes(t.reshape(b, nb, Q_BLOCK, *t.shape[2:]), 0, 1)

    def one_block(args):
        qn, qr, start = args
        sc = (jnp.einsum('bqhd,bkhd->bhqk', qn, k_nope)
              + jnp.einsum('bqhr,bkr->bhqk', qr, k_rope)).astype(jnp.float32) * scale
        qpos = start + jnp.arange(Q_BLOCK)
        mask = kpos[None, :] <= qpos[:, None]
        pr = jax.nn.softmax(jnp.where(mask, sc, -jnp.inf), axis=-1).astype(v.dtype)
        return jnp.einsum('bhqk,bkhd->bqhd', pr, v)

    out = lax.map(one_block, (to_blocks(q_nope), to_blocks(q_rope), jnp.arange(nb) * Q_BLOCK))
    return jnp.swapaxes(out, 0, 1).reshape(b, s, h, v.shape[-1])


def mla_branch(c_qa, c_kv, c_kr, c_z, cos, sin, q_norm_g, w_qb, kv_norm_g, w_kvb):
    b, s, _ = c_qa.shape
    q = jnp.einsum('bsr,rc->bsc', rmsnorm(c_qa, q_norm_g), w_qb).reshape(b, s, MLA_HEADS, QK_NOPE + QK_ROPE)
    q_nope, q_rope = jnp.split(q, [QK_NOPE], axis=-1)
    q_rope = apply_rope(q_rope, cos[:, :, None, :], sin[:, :, None, :])
    kv = jnp.einsum('bsr,rc->bsc', rmsnorm(c_kv, kv_norm_g), w_kvb).reshape(b, s, MLA_HEADS, QK_NOPE + V_DIM)
    k_nope, v = jnp.split(kv, [QK_NOPE], axis=-1)
    k_rope = apply_rope(c_kr, cos, sin)
    o = causal_block_attention(q_nope, q_rope, k_nope, k_rope, v)
    return o.reshape(b, s, D_MLA) * jax.nn.silu(c_z)


def hybrid_layer(x, cos, sin, norm_g, w_in, conv_a_w, ssd_conv_w, ssd_conv_b, ssd_dt_bias,
                 ssd_a_log, ssd_d, ssd_norm_g, mla_q_norm_g, w_qb, mla_kv_norm_g, w_kvb, w_out):
    h = rmsnorm(x, norm_g)
    proj = jnp.einsum('bsd,dc->bsc', h, w_in)
    split_at = np.cumsum(SPLIT_SIZES)[:-1].tolist()
    (a_h, a_b, a_c, a_z, s_z, s_x, s_b, s_c, s_dt,
     c_qa, c_kv, c_kr, c_z) = jnp.split(proj, split_at, axis=-1)
    y_a = short_conv_branch(a_h, a_b, a_c, a_z, conv_a_w)
    y_b = ssd_branch(s_z, s_x, s_b, s_c, s_dt, ssd_conv_w, ssd_conv_b, ssd_dt_bias,
                     ssd_a_log, ssd_d, ssd_norm_g)
    y_c = mla_branch(c_qa, c_kv, c_kr, c_z, cos, sin, mla_q_norm_g, w_qb, mla_kv_norm_g, w_kvb)
    y = jnp.concatenate([y_a, y_b, y_c], axis=-1)
    return x + jnp.einsum('bsm,md->bsd', y, w_out)


def _fwd_setup_inputs(seed: int = 0) -> dict:
    key = jax.random.key(seed)
    ks = jax.random.split(key, 20)
    f32 = jnp.float32
    nrm = jax.random.normal
    x = nrm(ks[0], (BATCH, SEQ, D_MODEL), f32)
    offs = jax.random.randint(ks[1], (BATCH, 1), 0, POS_OFFSET_MAX, dtype=jnp.int32)
    positions = (offs + jnp.arange(SEQ, dtype=jnp.int32)[None, :]).astype(jnp.int32)
    norm_g = 1.0 + 0.02 * nrm(ks[2], (DEPTH, D_MODEL), f32)
    w_in = nrm(ks[3], (DEPTH, D_MODEL, IN_COLS), f32) * D_MODEL ** -0.5
    conv_a_w = nrm(ks[4], (DEPTH, CONV_A_WIDTH, D_CONV_A), f32) * CONV_A_WIDTH ** -0.5
    ssd_conv_w = nrm(ks[5], (DEPTH, SSD_CONV_WIDTH, SSD_CONV_DIM), f32) * SSD_CONV_WIDTH ** -0.5
    ssd_conv_b = 0.01 * nrm(ks[6], (DEPTH, SSD_CONV_DIM), f32)
    u = jax.random.uniform(ks[7], (DEPTH, SSD_HEADS), f32)
    dt0 = jnp.exp(u * (math.log(0.1) - math.log(0.001)) + math.log(0.001))
    ssd_dt_bias = dt0 + jnp.log(-jnp.expm1(-dt0))
    ssd_a_log = jnp.log(jax.random.uniform(ks[8], (DEPTH, SSD_HEADS), f32, 1.0, 16.0))
    ssd_d = 1.0 + 0.1 * nrm(ks[9], (DEPTH, SSD_HEADS), f32)
    ssd_norm_g = 1.0 + 0.02 * nrm(ks[10], (DEPTH, D_SSD), f32)
    mla_q_norm_g = 1.0 + 0.02 * nrm(ks[11], (DEPTH, Q_LORA), f32)
    w_qb = nrm(ks[12], (DEPTH, Q_LORA, MLA_HEADS * (QK_NOPE + QK_ROPE)), f32) * Q_LORA ** -0.5
    mla_kv_norm_g = 1.0 + 0.02 * nrm(ks[13], (DEPTH, KV_LORA), f32)
    w_kvb = nrm(ks[14], (DEPTH, KV_LORA, MLA_HEADS * (QK_NOPE + V_DIM)), f32) * KV_LORA ** -0.5
    w_out = nrm(ks[15], (DEPTH, D_MIX, D_MODEL), f32) * D_MIX ** -0.5
    final_norm_g = 1.0 + 0.02 * nrm(ks[16], (D_MODEL,), f32)
    return {'x': x, 'positions': positions, 'norm_g': norm_g, 'w_in': w_in, 'conv_a_w': conv_a_w,
            'ssd_conv_w': ssd_conv_w, 'ssd_conv_b': ssd_conv_b, 'ssd_dt_bias': ssd_dt_bias,
            'ssd_a_log': ssd_a_log, 'ssd_d': ssd_d, 'ssd_norm_g': ssd_norm_g,
            'mla_q_norm_g': mla_q_norm_g, 'w_qb': w_qb, 'mla_kv_norm_g': mla_kv_norm_g,
            'w_kvb': w_kvb, 'w_out': w_out, 'final_norm_g': final_norm_g}


def _fwd_reference(x, positions, norm_g, w_in, conv_a_w, ssd_conv_w, ssd_conv_b, ssd_dt_bias,
              ssd_a_log, ssd_d, ssd_norm_g, mla_q_norm_g, w_qb, mla_kv_norm_g, w_kvb, w_out,
              final_norm_g):
    inv_freq = ROPE_BASE ** (-jnp.arange(0, QK_ROPE, 2, dtype=jnp.float32) / QK_ROPE)
    ang = positions.astype(jnp.float32)[..., None] * inv_freq
    cos, sin = jnp.cos(ang), jnp.sin(ang)
    for l in range(DEPTH):
        x = hybrid_layer(x, cos, sin, norm_g[l], w_in[l], conv_a_w[l], ssd_conv_w[l], ssd_conv_b[l],
                         ssd_dt_bias[l], ssd_a_log[l], ssd_d[l], ssd_norm_g[l], mla_q_norm_g[l],
                         w_qb[l], mla_kv_norm_g[l], w_kvb[l], w_out[l])
    return rmsnorm(x, final_norm_g)


import jax as _jax
import jax.numpy as _jnp

TWIN_FORMAT = 'train_step'
FWD_PARAMS = ['x', 'positions', 'norm_g', 'w_in', 'conv_a_w', 'ssd_conv_w', 'ssd_conv_b', 'ssd_dt_bias', 'ssd_a_log', 'ssd_d', 'ssd_norm_g', 'mla_q_norm_g', 'w_qb', 'mla_kv_norm_g', 'w_kvb', 'w_out', 'final_norm_g']
TWIN_WEIGHTS = ['norm_g', 'w_in', 'conv_a_w', 'ssd_conv_w', 'ssd_conv_b', 'ssd_dt_bias', 'ssd_a_log', 'ssd_d', 'ssd_norm_g', 'mla_q_norm_g', 'w_qb', 'mla_kv_norm_g', 'w_kvb', 'w_out', 'final_norm_g']
TWIN_DIFF_INPUT = 'x'
TWIN_INPUTS = ['x', 'positions', 'norm_g', 'w_in', 'conv_a_w', 'ssd_conv_w', 'ssd_conv_b', 'ssd_dt_bias', 'ssd_a_log', 'ssd_d', 'ssd_norm_g', 'mla_q_norm_g', 'w_qb', 'mla_kv_norm_g', 'w_kvb', 'w_out', 'final_norm_g', 'loss_target', 'm_norm_g', 'm_w_in', 'm_conv_a_w', 'm_ssd_conv_w', 'm_ssd_conv_b', 'm_ssd_dt_bias', 'm_ssd_a_log', 'm_ssd_d', 'm_ssd_norm_g', 'm_mla_q_norm_g', 'm_w_qb', 'm_mla_kv_norm_g', 'm_w_kvb', 'm_w_out', 'm_final_norm_g', 'v_norm_g', 'v_w_in', 'v_conv_a_w', 'v_ssd_conv_w', 'v_ssd_conv_b', 'v_ssd_dt_bias', 'v_ssd_a_log', 'v_ssd_d', 'v_ssd_norm_g', 'v_mla_q_norm_g', 'v_w_qb', 'v_mla_kv_norm_g', 'v_w_kvb', 'v_w_out', 'v_final_norm_g']
TWIN_OUTPUTS = ['loss', 'grad_x', 'grad_norm_g', 'grad_w_in', 'grad_conv_a_w', 'grad_ssd_conv_w', 'grad_ssd_conv_b', 'grad_ssd_dt_bias', 'grad_ssd_a_log', 'grad_ssd_d', 'grad_ssd_norm_g', 'grad_mla_q_norm_g', 'grad_w_qb', 'grad_mla_kv_norm_g', 'grad_w_kvb', 'grad_w_out', 'grad_final_norm_g', 'delta_norm_g', 'delta_w_in', 'delta_conv_a_w', 'delta_ssd_conv_w', 'delta_ssd_conv_b', 'delta_ssd_dt_bias', 'delta_ssd_a_log', 'delta_ssd_d', 'delta_ssd_norm_g', 'delta_mla_q_norm_g', 'delta_w_qb', 'delta_mla_kv_norm_g', 'delta_w_kvb', 'delta_w_out', 'delta_final_norm_g', 'new_m_norm_g', 'new_m_w_in', 'new_m_conv_a_w', 'new_m_ssd_conv_w', 'new_m_ssd_conv_b', 'new_m_ssd_dt_bias', 'new_m_ssd_a_log', 'new_m_ssd_d', 'new_m_ssd_norm_g', 'new_m_mla_q_norm_g', 'new_m_w_qb', 'new_m_mla_kv_norm_g', 'new_m_w_kvb', 'new_m_w_out', 'new_m_final_norm_g', 'new_v_norm_g', 'new_v_w_in', 'new_v_conv_a_w', 'new_v_ssd_conv_w', 'new_v_ssd_conv_b', 'new_v_ssd_dt_bias', 'new_v_ssd_a_log', 'new_v_ssd_d', 'new_v_ssd_norm_g', 'new_v_mla_q_norm_g', 'new_v_w_qb', 'new_v_mla_kv_norm_g', 'new_v_w_kvb', 'new_v_w_out', 'new_v_final_norm_g']
TWIN_LEAF_KINDS = {'loss': 'loss', 'grad_x': 'grad_x', 'grad_norm_g': 'grad_w', 'grad_w_in': 'grad_w', 'grad_conv_a_w': 'grad_w', 'grad_ssd_conv_w': 'grad_w', 'grad_ssd_conv_b': 'grad_w', 'grad_ssd_dt_bias': 'grad_w', 'grad_ssd_a_log': 'grad_w', 'grad_ssd_d': 'grad_w', 'grad_ssd_norm_g': 'grad_w', 'grad_mla_q_norm_g': 'grad_w', 'grad_w_qb': 'grad_w', 'grad_mla_kv_norm_g': 'grad_w', 'grad_w_kvb': 'grad_w', 'grad_w_out': 'grad_w', 'grad_final_norm_g': 'grad_w', 'delta_norm_g': 'delta_w', 'delta_w_in': 'delta_w', 'delta_conv_a_w': 'delta_w', 'delta_ssd_conv_w': 'delta_w', 'delta_ssd_conv_b': 'delta_w', 'delta_ssd_dt_bias': 'delta_w', 'delta_ssd_a_log': 'delta_w', 'delta_ssd_d': 'delta_w', 'delta_ssd_norm_g': 'delta_w', 'delta_mla_q_norm_g': 'delta_w', 'delta_w_qb': 'delta_w', 'delta_mla_kv_norm_g': 'delta_w', 'delta_w_kvb': 'delta_w', 'delta_w_out': 'delta_w', 'delta_final_norm_g': 'delta_w', 'new_m_norm_g': 'new_m', 'new_m_w_in': 'new_m', 'new_m_conv_a_w': 'new_m', 'new_m_ssd_conv_w': 'new_m', 'new_m_ssd_conv_b': 'new_m', 'new_m_ssd_dt_bias': 'new_m', 'new_m_ssd_a_log': 'new_m', 'new_m_ssd_d': 'new_m', 'new_m_ssd_norm_g': 'new_m', 'new_m_mla_q_norm_g': 'new_m', 'new_m_w_qb': 'new_m', 'new_m_mla_kv_norm_g': 'new_m', 'new_m_w_kvb': 'new_m', 'new_m_w_out': 'new_m', 'new_m_final_norm_g': 'new_m', 'new_v_norm_g': 'new_v', 'new_v_w_in': 'new_v', 'new_v_conv_a_w': 'new_v', 'new_v_ssd_conv_w': 'new_v', 'new_v_ssd_conv_b': 'new_v', 'new_v_ssd_dt_bias': 'new_v', 'new_v_ssd_a_log': 'new_v', 'new_v_ssd_d': 'new_v', 'new_v_ssd_norm_g': 'new_v', 'new_v_mla_q_norm_g': 'new_v', 'new_v_w_qb': 'new_v', 'new_v_mla_kv_norm_g': 'new_v', 'new_v_w_kvb': 'new_v', 'new_v_w_out': 'new_v', 'new_v_final_norm_g': 'new_v'}


def _forward(args):
    return _fwd_reference(*[args[k] for k in FWD_PARAMS])


def _output_shape():
    out = _jax.eval_shape(lambda: _forward(_fwd_setup_inputs(0)))
    return out.shape, out.dtype

N_MICROBATCH = 1
ADAM_LR = 0.001
ADAM_B1 = 0.9
ADAM_B2 = 0.999
ADAM_EPS = 1e-08
ADAM_WD = 0.01
ADAM_STEP = 10
PER_EXAMPLE_BATCH_AXIS = {'x': 0, 'positions': 0, 'loss_target': 0}
SHARED_INPUTS = []
_WEIGHT_DTYPES = {'norm_g': _jnp.float32, 'w_in': _jnp.float32, 'conv_a_w': _jnp.float32, 'ssd_conv_w': _jnp.float32, 'ssd_conv_b': _jnp.float32, 'ssd_dt_bias': _jnp.float32, 'ssd_a_log': _jnp.float32, 'ssd_d': _jnp.float32, 'ssd_norm_g': _jnp.float32, 'mla_q_norm_g': _jnp.float32, 'w_qb': _jnp.float32, 'mla_kv_norm_g': _jnp.float32, 'w_kvb': _jnp.float32, 'w_out': _jnp.float32, 'final_norm_g': _jnp.float32}
MOMENT_SCALE = {'norm_g': 1.177991e-01, 'w_in': 6.691499e-02, 'conv_a_w': 6.390699e-02, 'ssd_conv_w': 7.039965e-02, 'ssd_conv_b': 1.074370e-01, 'ssd_dt_bias': 1.428754e-01, 'ssd_a_log': 5.899154e-01, 'ssd_d': 6.594093e-01, 'ssd_norm_g': 1.195062e-01, 'mla_q_norm_g': 1.475661e-02, 'w_qb': 9.759519e-03, 'mla_kv_norm_g': 3.584183e-02, 'w_kvb': 1.293426e-02, 'w_out': 7.373498e-02, 'final_norm_g': 1.602333e+01}


def _to_microbatches(a, axis):
    t = _jnp.moveaxis(a, axis, 0)
    t = t.reshape((N_MICROBATCH, t.shape[0] // N_MICROBATCH) + t.shape[1:])
    return _jnp.moveaxis(t, 1, axis + 1)


def setup_inputs(seed: int = 0) -> dict:
    inp = _fwd_setup_inputs(seed)
    key = _jax.random.fold_in(_jax.random.key(seed), 7919)
    shape, _ = _output_shape()
    out = dict(inp)
    out["loss_target"] = _jax.random.normal(_jax.random.fold_in(key, 0), shape, _jnp.float32)
    for i, name in enumerate(TWIN_WEIGHTS):
        w = inp[name].astype(_jnp.float32)
        if MOMENT_SCALE is None:
            s = _jnp.sqrt(_jnp.mean(_jnp.square(w)) + 1e-30)
        else:
            s = MOMENT_SCALE[name]
        km, kv = _jax.random.split(_jax.random.fold_in(key, i + 1))
        out[name] = w
        out["m_" + name] = s * _jax.random.normal(km, w.shape, _jnp.float32)
        out["v_" + name] = (s * s) * _jax.random.uniform(kv, w.shape, _jnp.float32, 0.5, 1.5)
    if N_MICROBATCH > 1:
        for name, axis in PER_EXAMPLE_BATCH_AXIS.items():
            out[name] = _to_microbatches(out[name], axis)
    return {'x': out['x'], 'positions': out['positions'], 'norm_g': out['norm_g'], 'w_in': out['w_in'], 'conv_a_w': out['conv_a_w'], 'ssd_conv_w': out['ssd_conv_w'], 'ssd_conv_b': out['ssd_conv_b'], 'ssd_dt_bias': out['ssd_dt_bias'], 'ssd_a_log': out['ssd_a_log'], 'ssd_d': out['ssd_d'], 'ssd_norm_g': out['ssd_norm_g'], 'mla_q_norm_g': out['mla_q_norm_g'], 'w_qb': out['w_qb'], 'mla_kv_norm_g': out['mla_kv_norm_g'], 'w_kvb': out['w_kvb'], 'w_out': out['w_out'], 'final_norm_g': out['final_norm_g'], 'loss_target': out['loss_target'], 'm_norm_g': out['m_norm_g'], 'm_w_in': out['m_w_in'], 'm_conv_a_w': out['m_conv_a_w'], 'm_ssd_conv_w': out['m_ssd_conv_w'], 'm_ssd_conv_b': out['m_ssd_conv_b'], 'm_ssd_dt_bias': out['m_ssd_dt_bias'], 'm_ssd_a_log': out['m_ssd_a_log'], 'm_ssd_d': out['m_ssd_d'], 'm_ssd_norm_g': out['m_ssd_norm_g'], 'm_mla_q_norm_g': out['m_mla_q_norm_g'], 'm_w_qb': out['m_w_qb'], 'm_mla_kv_norm_g': out['m_mla_kv_norm_g'], 'm_w_kvb': out['m_w_kvb'], 'm_w_out': out['m_w_out'], 'm_final_norm_g': out['m_final_norm_g'], 'v_norm_g': out['v_norm_g'], 'v_w_in': out['v_w_in'], 'v_conv_a_w': out['v_conv_a_w'], 'v_ssd_conv_w': out['v_ssd_conv_w'], 'v_ssd_conv_b': out['v_ssd_conv_b'], 'v_ssd_dt_bias': out['v_ssd_dt_bias'], 'v_ssd_a_log': out['v_ssd_a_log'], 'v_ssd_d': out['v_ssd_d'], 'v_ssd_norm_g': out['v_ssd_norm_g'], 'v_mla_q_norm_g': out['v_mla_q_norm_g'], 'v_w_qb': out['v_w_qb'], 'v_mla_kv_norm_g': out['v_mla_kv_norm_g'], 'v_w_kvb': out['v_w_kvb'], 'v_w_out': out['v_w_out'], 'v_final_norm_g': out['v_final_norm_g']}


def _loss(weights, diff, rest, loss_target):
    with _jax.named_scope("forward"):
        args = {**rest, TWIN_DIFF_INPUT: diff, **{k: w.astype(_WEIGHT_DTYPES[k]) for k, w in weights.items()}}
        y = _forward(args)
    with _jax.named_scope("loss_head"):
        err = _jnp.square(y.astype(_jnp.float32) - loss_target)
        return 0.5 * _jnp.sum(_jnp.mean(err, axis=-1)) if err.ndim else 0.5 * err


def _adamw(w, g, m, v):
    m = ADAM_B1 * m + (1.0 - ADAM_B1) * g
    v = ADAM_B2 * v + (1.0 - ADAM_B2) * _jnp.square(g)
    m_hat = m / (1.0 - ADAM_B1 ** ADAM_STEP)
    v_hat = v / (1.0 - ADAM_B2 ** ADAM_STEP)
    delta = -ADAM_LR * (m_hat / (_jnp.sqrt(v_hat) + ADAM_EPS) + ADAM_WD * w)
    return delta, m, v


def reference(x, positions, norm_g, w_in, conv_a_w, ssd_conv_w, ssd_conv_b, ssd_dt_bias, ssd_a_log, ssd_d, ssd_norm_g, mla_q_norm_g, w_qb, mla_kv_norm_g, w_kvb, w_out, final_norm_g, loss_target, m_norm_g, m_w_in, m_conv_a_w, m_ssd_conv_w, m_ssd_conv_b, m_ssd_dt_bias, m_ssd_a_log, m_ssd_d, m_ssd_norm_g, m_mla_q_norm_g, m_w_qb, m_mla_kv_norm_g, m_w_kvb, m_w_out, m_final_norm_g, v_norm_g, v_w_in, v_conv_a_w, v_ssd_conv_w, v_ssd_conv_b, v_ssd_dt_bias, v_ssd_a_log, v_ssd_d, v_ssd_norm_g, v_mla_q_norm_g, v_w_qb, v_mla_kv_norm_g, v_w_kvb, v_w_out, v_final_norm_g):
    given = dict(x=x, positions=positions, norm_g=norm_g, w_in=w_in, conv_a_w=conv_a_w, ssd_conv_w=ssd_conv_w, ssd_conv_b=ssd_conv_b, ssd_dt_bias=ssd_dt_bias, ssd_a_log=ssd_a_log, ssd_d=ssd_d, ssd_norm_g=ssd_norm_g, mla_q_norm_g=mla_q_norm_g, w_qb=w_qb, mla_kv_norm_g=mla_kv_norm_g, w_kvb=w_kvb, w_out=w_out, final_norm_g=final_norm_g, loss_target=loss_target, m_norm_g=m_norm_g, m_w_in=m_w_in, m_conv_a_w=m_conv_a_w, m_ssd_conv_w=m_ssd_conv_w, m_ssd_conv_b=m_ssd_conv_b, m_ssd_dt_bias=m_ssd_dt_bias, m_ssd_a_log=m_ssd_a_log, m_ssd_d=m_ssd_d, m_ssd_norm_g=m_ssd_norm_g, m_mla_q_norm_g=m_mla_q_norm_g, m_w_qb=m_w_qb, m_mla_kv_norm_g=m_mla_kv_norm_g, m_w_kvb=m_w_kvb, m_w_out=m_w_out, m_final_norm_g=m_final_norm_g, v_norm_g=v_norm_g, v_w_in=v_w_in, v_conv_a_w=v_conv_a_w, v_ssd_conv_w=v_ssd_conv_w, v_ssd_conv_b=v_ssd_conv_b, v_ssd_dt_bias=v_ssd_dt_bias, v_ssd_a_log=v_ssd_a_log, v_ssd_d=v_ssd_d, v_ssd_norm_g=v_ssd_norm_g, v_mla_q_norm_g=v_mla_q_norm_g, v_w_qb=v_w_qb, v_mla_kv_norm_g=v_mla_kv_norm_g, v_w_kvb=v_w_kvb, v_w_out=v_w_out, v_final_norm_g=v_final_norm_g)
    weights = {n: given[n] for n in TWIN_WEIGHTS}
    shared = {n: given[n] for n in SHARED_INPUTS}
    per_example = {n: given[n] for n in ['x', 'positions']}
    grad_fn = _jax.value_and_grad(_loss, argnums=(0, 1))

    def one_microbatch(ex, loss_target):
        ex = dict(ex)
        diff = ex.pop(TWIN_DIFF_INPUT)
        return grad_fn(weights, diff, {**shared, **ex}, loss_target)

    if N_MICROBATCH == 1:
        loss, (grad_w, grad_x) = one_microbatch(per_example, given["loss_target"])
    else:
        def body(carry, xs):
            loss_sum, grad_sum = carry
            l_k, (gw_k, gx_k) = one_microbatch(xs[0], xs[1])
            with _jax.named_scope("update"):
                return (loss_sum + l_k, _jax.tree.map(_jnp.add, grad_sum, gw_k)), gx_k

        init = (_jnp.zeros((), _jnp.float32), _jax.tree.map(_jnp.zeros_like, weights))
        (loss, grad_w), grad_x = _jax.lax.scan(body, init, (per_example, given["loss_target"]))
    with _jax.named_scope("update"):
        delta_w, new_m, new_v = {}, {}, {}
        for n in TWIN_WEIGHTS:
            delta_w[n], new_m[n], new_v[n] = _adamw(weights[n], grad_w[n], given["m_" + n], given["v_" + n])
    return (loss, grad_x, *[grad_w[n] for n in TWIN_WEIGHTS], *[delta_w[n] for n in TWIN_WEIGHTS],
            *[new_m[n] for n in TWIN_WEIGHTS], *[new_v[n] for n in TWIN_WEIGHTS])
```

```python
import functools
import math

import numpy as np
import jax
import jax.numpy as jnp
from jax import lax
from jax.experimental import pallas as pl
from jax.experimental.pallas import tpu as pltpu

F32 = jnp.float32
MXU = jnp.bfloat16

D_MODEL = 1024
DEPTH = 2
D_CONV_A = 256
D_SSD = 384
SSD_HEADS = 6
SSD_BC = 256
SSD_CHUNK = 128
SSD_NORM_EPS = 1e-5
MLA_HEADS = 6
Q_LORA = 256
KV_LORA = 128
QK_NOPE = 64
QK_ROPE = 32
V_DIM = 64
D_MLA = 384
ROPE_BASE = 10000.0
NORM_EPS = 1e-6
IN_COLS = 3110
LANE = 128

O_AH, O_AB, O_AC, O_AZ = 0, 256, 512, 768
O_SZ = 1024
O_XBC = 1408
O_CQA = 2304
O_CKV = 2560
O_CZ = 2688
O_TAIL = 3072
NCOL = 3200
N_XBC = D_SSD + 2 * SSD_BC
DT_LANE = 32
ROPE_LANE = 64

ADAM_LR, ADAM_B1, ADAM_B2, ADAM_EPS, ADAM_WD, ADAM_STEP = 0.001, 0.9, 0.999, 1e-08, 0.01, 10

VMEM_LIMIT = 56 * 1024 * 1024
MESH_T = pl.DeviceIdType.MESH


def _dot(a, b):
    return jnp.dot(a.astype(MXU), b.astype(MXU), preferred_element_type=F32)


def _dot_nt(a, b):
    return lax.dot_general(a.astype(MXU), b.astype(MXU), (((1,), (1,)), ((), ())), preferred_element_type=F32)


def _dot_tn(a, b):
    return lax.dot_general(a.astype(MXU), b.astype(MXU), (((0,), (0,)), ((), ())), preferred_element_type=F32)


def _dot_hi(a, b):
    return jnp.dot(a, b, precision=lax.Precision.HIGHEST, preferred_element_type=F32)


def _dot_hi_tn(a, b):
    return lax.dot_general(a, b, (((0,), (0,)), ((), ())), precision=lax.Precision.HIGHEST, preferred_element_type=F32)


def _sigmoid(z):
    return 1.0 / (1.0 + jnp.exp(-z))


def _silu(z):
    return z * _sigmoid(z)


def _dsilu(z):
    s = _sigmoid(z)
    return s * (1.0 + z * (1.0 - s))


def _softplus(z):
    e = jnp.exp(-jnp.abs(z))
    return jnp.maximum(z, 0.0) + jnp.where(e < 1e-3, e * (1.0 - 0.5 * e), jnp.log(1.0 + e))


def _iota(shape, dim):
    return lax.broadcasted_iota(jnp.int32, shape, dim)


def _shift_down(u, k):
    if k == 0:
        return u
    return jnp.where(_iota(u.shape, 0) >= k, pltpu.roll(u, k, 0), 0.0)


def _shift_up(u, k):
    if k == 0:
        return u
    n = u.shape[0]
    return jnp.where(_iota(u.shape, 0) < n - k, pltpu.roll(u, n - k, 0), 0.0)


def _rope_swap(t):
    lane = _iota(t.shape, 1)
    lo = (lane >= ROPE_LANE) & (lane < ROPE_LANE + 16)
    hi = (lane >= ROPE_LANE + 16) & (lane < ROPE_LANE + 32)
    return jnp.where(lo, pltpu.roll(t, LANE - 16, 1), jnp.where(hi, pltpu.roll(t, 16, 1), 0.0))


def _params(sem=None):
    return pltpu.CompilerParams(dimension_semantics=sem, vmem_limit_bytes=VMEM_LIMIT)


def _full(shape):
    nd = len(shape)
    return pl.BlockSpec(shape, lambda *_: (0,) * nd)


def _sds(shape, dtype=F32):
    return jax.ShapeDtypeStruct(shape, dtype)


def _tile(s):
    return min(256, s)


def _row(ts, w):
    return pl.BlockSpec((ts, w), lambda i: (i, 0))


def _col(s, off):
    return pl.BlockSpec((s, LANE), lambda j, _o=off // LANE: (0, _o + j))


def _rms(c, g):
    r = lax.rsqrt(jnp.mean(c * c, axis=-1, keepdims=True) + NORM_EPS)
    return c * r * g, r


def _rms_bwd(dn, c, r, g):
    ch = c * r
    dch = dn * g
    dc = r * (dch - ch * jnp.mean(dch * ch, axis=-1, keepdims=True))
    return dc, jnp.sum(dn * ch, axis=0, keepdims=True)


def _inproj_fwd(x, g, w):
    s = x.shape[0]
    ts = _tile(s)

    def body(x_ref, g_ref, w_ref, proj_ref, h_ref, r_ref):
        hn, r = _rms(x_ref[...], g_ref[...])
        h = hn.astype(MXU)
        h_ref[...] = h
        r_ref[...] = r
        proj_ref[...] = jnp.dot(h, w_ref[...], preferred_element_type=F32)

    return pl.pallas_call(
        body, name="inproj_fwd", grid=(s // ts,),
        in_specs=[_row(ts, D_MODEL), _full((1, D_MODEL)), _full((D_MODEL, NCOL))],
        out_specs=[_row(ts, NCOL), _row(ts, D_MODEL), _row(ts, 1)],
        out_shape=[_sds((s, NCOL)), _sds((s, D_MODEL), MXU), _sds((s, 1))],
        compiler_params=_params(("parallel",)),
    )(x, g, w)


def _conva_fwd(proj, w):
    s = proj.shape[0]

    def body(h_ref, b_ref, c_ref, z_ref, w_ref, y_ref):
        u = c_ref[...] * h_ref[...]
        wv = w_ref[...]
        cv = wv[2:3, :] * u + wv[1:2, :] * _shift_down(u, 1) + wv[0:1, :] * _shift_down(u, 2)
        y_ref[...] = b_ref[...] * cv * _silu(z_ref[...])

    return pl.pallas_call(
        body, name="conva_fwd", grid=(D_CONV_A // LANE,),
        in_specs=[_col(s, O_AH), _col(s, O_AB), _col(s, O_AC), _col(s, O_AZ), pl.BlockSpec((3, LANE), lambda j: (0, j))],
        out_specs=pl.BlockSpec((s, LANE), lambda j: (0, j)),
        out_shape=_sds((s, D_CONV_A)),
        compiler_params=_params(("parallel",)),
    )(proj, proj, proj, proj, w)


def _sconv_pre(u, wv, bv):
    return (wv[3:4, :] * u + wv[2:3, :] * _shift_down(u, 1) + wv[1:2, :] * _shift_down(u, 2)
            + wv[0:1, :] * _shift_down(u, 3) + bv)


def _sconv_fwd(proj, w, b):
    s = proj.shape[0]

    def body(u_ref, w_ref, b_ref, o_ref):
        o_ref[...] = _silu(_sconv_pre(u_ref[...], w_ref[...], b_ref[...]))

    return pl.pallas_call(
        body, name="sconv_fwd", grid=(N_XBC // LANE,),
        in_specs=[_col(s, O_XBC), pl.BlockSpec((4, LANE), lambda j: (0, j)), pl.BlockSpec((1, LANE), lambda j: (0, j))],
        out_specs=pl.BlockSpec((s, LANE), lambda j: (0, j)),
        out_shape=_sds((s, N_XBC)),
        compiler_params=_params(("parallel",)),
    )(proj, w, b)


def _ssd_chunk_common(tail, sc):
    l = SSD_CHUNK
    lane = _iota((l, LANE), 1)
    row = _iota((l, LANE), 0)
    tri = (row >= lane).astype(F32)
    a_row = -jnp.exp(sc[1:2, :])
    pre = tail + sc[0:1, :]
    dt = _softplus(pre)
    a_cs = _dot_hi(tri, dt * a_row)
    return lane, row, tri, a_row, pre, dt, a_cs, a_cs.T


def _pick_col(m, lane, k):
    return jnp.sum(jnp.where(lane == k, m, 0.0), axis=1, keepdims=True)


def _pick_row(m, row, k):
    return jnp.sum(jnp.where(row == k, m, 0.0), axis=0, keepdims=True)


def _ssd_fwd(xbc, proj, sc):
    s = xbc.shape[0]
    nc = s // SSD_CHUNK
    l = SSD_CHUNK

    def body(xbc_ref, tail_ref, sc_ref, y_ref, st_ref, state):
        @pl.when(pl.program_id(0) == 0)
        def _():
            state[...] = jnp.zeros_like(state)

        sc_v = sc_ref[...]
        lane, row, _, _, _, dt, a_cs, a_t = _ssd_chunk_common(tail_ref[...], sc_v)
        lane1 = _iota((1, LANE), 1)
        rowp = _iota((LANE, 1), 0)
        d_row = sc_v[2:3, :]
        for j in range(3):
            st_ref[0, j] = state[j]
        for j in range(3):
            xpair = xbc_ref[:, LANE * j:LANE * (j + 1)]
            sp = state[j]
            ypair = jnp.zeros((l, LANE), F32)
            new_s = jnp.zeros((LANE, LANE), F32)
            decay = jnp.zeros((LANE, 1), F32)
            for half in range(2):
                h = 2 * j + half
                g = h // 3
                hm = (lane < 64) if half == 0 else (lane >= 64)
                hrow = (rowp < 64) if half == 0 else (rowp >= 64)
                ac = _pick_col(a_cs, lane, DT_LANE + h)
                ar = _pick_row(a_t, row, DT_LANE + h)
                dtc = _pick_col(dt, lane, DT_LANE + h)
                alast = jnp.sum(jnp.where(lane1 == l - 1, ar, 0.0), axis=1, keepdims=True)
                dh = jnp.sum(jnp.where(lane1 == DT_LANE + h, d_row, 0.0), axis=1, keepdims=True)
                xm = jnp.where(hm, xpair, 0.0)
                xd = xm * dtc
                bm = xbc_ref[:, D_SSD + LANE * g:D_SSD + LANE * (g + 1)]
                cm = xbc_ref[:, D_SSD + SSD_BC + LANE * g:D_SSD + SSD_BC + LANE * (g + 1)]
                lm = jnp.where(row >= lane, jnp.exp(jnp.minimum(ac - ar, 0.0)), 0.0)
                y_diag = _dot(_dot_nt(cm, bm) * lm, xd)
                y_off = jnp.where(hm, _dot_nt(cm, sp), 0.0) * jnp.exp(ac)
                ypair = ypair + y_diag + y_off + xm * dh
                new_s = new_s + _dot_tn(xd * jnp.exp(alast - ac), bm)
                decay = jnp.where(hrow, jnp.exp(alast), decay)
            state[j] = sp * decay + new_s
            y_ref[:, LANE * j:LANE * (j + 1)] = ypair

    return pl.pallas_call(
        body, name="ssd_fwd", grid=(nc,),
        in_specs=[pl.BlockSpec((l, N_XBC), lambda c: (c, 0)),
                  pl.BlockSpec((l, LANE), lambda c: (c, O_TAIL // LANE)), _full((8, LANE))],
        out_specs=[pl.BlockSpec((l, D_SSD), lambda c: (c, 0)), pl.BlockSpec((1, 3, LANE, LANE), lambda c: (c, 0, 0, 0))],
        out_shape=[_sds((s, D_SSD)), _sds((nc, 3, LANE, LANE))],
        scratch_shapes=[pltpu.VMEM((3, LANE, LANE), F32)],
        compiler_params=_params(("arbitrary",)),
    )(xbc, proj, sc)


def _mla_prep_fwd(proj, gq, gkv, wq, wkv, cos, sin):
    s = proj.shape[0]
    ts = _tile(s)
    nh = MLA_HEADS

    def body(cqa_ref, ckv_ref, tail_ref, gq_ref, gkv_ref, wq_ref, wkv_ref, cos_ref, sin_ref,
             q_ref, k_ref, v_ref, qn_ref, kvn_ref, rq_ref, rkv_ref):
        qn, rq = _rms(cqa_ref[...], gq_ref[...])
        kvn, rkv = _rms(ckv_ref[...], gkv_ref[...])
        qn = qn.astype(MXU)
        kvn = kvn.astype(MXU)
        qn_ref[...] = qn
        kvn_ref[...] = kvn
        rq_ref[...] = rq
        rkv_ref[...] = rkv
        q = _dot_nt(qn, wq_ref[...])
        kv = _dot_nt(kvn, wkv_ref[...])
        cosv = cos_ref[...]
        sinv = sin_ref[...]
        lane = _iota((ts, LANE), 1)
        rope_lanes = (lane >= ROPE_LANE) & (lane < ROPE_LANE + QK_ROPE)
        kr = jnp.where(rope_lanes, pltpu.roll(tail_ref[...], ROPE_LANE, 1), 0.0)
        kr = kr * cosv + _rope_swap(kr) * sinv
        for h in range(nh):
            qh = q[:, LANE * h:LANE * (h + 1)]
            q_ref[h] = (qh * cosv + _rope_swap(qh) * sinv).astype(MXU)
            k_ref[h] = (kv[:, LANE * h:LANE * (h + 1)] + kr).astype(MXU)
            v_ref[h] = kv[:, LANE * (nh + h):LANE * (nh + h + 1)].astype(MXU)

    head = pl.BlockSpec((nh, ts, LANE), lambda i: (0, i, 0))
    return pl.pallas_call(
        body, name="mla_prep_fwd", grid=(s // ts,),
        in_specs=[pl.BlockSpec((ts, Q_LORA), lambda i: (i, O_CQA // Q_LORA)),
                  pl.BlockSpec((ts, KV_LORA), lambda i: (i, O_CKV // KV_LORA)),
                  pl.BlockSpec((ts, LANE), lambda i: (i, O_TAIL // LANE)),
                  _full((1, Q_LORA)), _full((1, KV_LORA)), _full((nh * LANE, Q_LORA)), _full((2 * nh * LANE, KV_LORA)),
                  _row(ts, LANE), _row(ts, LANE)],
        out_specs=[head, head, head, _row(ts, Q_LORA), _row(ts, KV_LORA), _row(ts, 1), _row(ts, 1)],
        out_shape=[_sds((nh, s, LANE), MXU)] * 3 + [_sds((s, Q_LORA), MXU), _sds((s, KV_LORA), MXU), _sds((s, 1)), _sds((s, 1))],
        compiler_params=_params(("parallel",)),
    )(proj, proj, proj, gq, gkv, wq, wkv, cos, sin)


ATT_SCALE = (QK_NOPE + QK_ROPE) ** -0.5
NEG = -1e30


def _attn_fwd(q, k, v):
    nh, s, _ = q.shape
    tq = _tile(s)
    nq = s // tq

    def body(q_ref, k_ref, v_ref, o_ref, lse_ref):
        i = pl.program_id(1)
        rowi = _iota((tq, tq), 0)
        coli = _iota((tq, tq), 1)
        out = None
        for hh in range(2):
            qv = q_ref[hh]

            def kv_step(jj, carry, hh=hh, qv=qv):
                m, lsum, acc = carry
                off = pl.multiple_of(jj * tq, tq)
                kb = k_ref[hh, pl.ds(off, tq), :]
                vb = v_ref[hh, pl.ds(off, tq), :]
                sc = _dot_nt(qv, kb) * ATT_SCALE
                sc = jnp.where(jj * tq + coli <= i * tq + rowi, sc, NEG)
                m_new = jnp.maximum(m, jnp.max(sc, axis=1, keepdims=True))
                p = jnp.exp(sc - m_new)
                alpha = jnp.exp(m - m_new)
                return m_new, alpha * lsum + jnp.sum(p, axis=1, keepdims=True), alpha * acc + _dot(p, vb)

            m0 = jnp.full((tq, 1), NEG, F32)
            m, lsum, acc = lax.fori_loop(0, i + 1, kv_step, (m0, jnp.zeros((tq, 1), F32), jnp.zeros((tq, LANE), F32)))
            o = acc / lsum
            lse_ref[hh] = m + jnp.log(lsum)
            out = o if hh == 0 else out + pltpu.roll(o, V_DIM, 1)
        o_ref[...] = out

    return pl.pallas_call(
        body, name="attn_fwd", grid=(nh // 2, nq),
        in_specs=[pl.BlockSpec((2, tq, LANE), lambda j, i: (j, i, 0)), pl.BlockSpec((2, s, LANE), lambda j, i: (j, 0, 0)),
                  pl.BlockSpec((2, s, LANE), lambda j, i: (j, 0, 0))],
        out_specs=[pl.BlockSpec((tq, LANE), lambda j, i: (i, j)), pl.BlockSpec((2, tq, 1), lambda j, i: (j, i, 0))],
        out_shape=[_sds((s, D_MLA)), _sds((nh, s, 1))],
        compiler_params=_params(("parallel", "parallel")),
    )(q, k, v)


def _ssd_gate(y_ssd, s_z, g):
    yz = y_ssd * _silu(s_z)
    g0 = _iota(yz.shape, 1) < D_SSD // 2
    sq = yz * yz
    ms0 = jnp.sum(jnp.where(g0, sq, 0.0), axis=1, keepdims=True) / (D_SSD // 2)
    ms1 = jnp.sum(jnp.where(g0, 0.0, sq), axis=1, keepdims=True) / (D_SSD // 2)
    r = jnp.where(g0, lax.rsqrt(ms0 + SSD_NORM_EPS), lax.rsqrt(ms1 + SSD_NORM_EPS))
    nrm = yz * r
    return nrm * g, nrm, r, g0


def _outproj_fwd(x, proj, ya, y_ssd, o, g_ssd, w):
    s = x.shape[0]
    ts = _tile(s)

    def body(x_ref, p_ref, ya_ref, ys_ref, o_ref, g_ref, w_ref, xo_ref, y_ref):
        yb = _ssd_gate(ys_ref[...], p_ref[:, O_SZ:O_SZ + D_SSD], g_ref[...])[0]
        yc = o_ref[...] * _silu(p_ref[:, O_CZ:O_CZ + D_MLA])
        y = jnp.concatenate([ya_ref[...], yb, yc], axis=1).astype(MXU)
        y_ref[...] = y
        xo_ref[...] = x_ref[...] + jnp.dot(y, w_ref[...], preferred_element_type=F32)

    return pl.pallas_call(
        body, name="outproj_fwd", grid=(s // ts,),
        in_specs=[_row(ts, D_MODEL), _row(ts, NCOL), _row(ts, D_CONV_A), _row(ts, D_SSD), _row(ts, D_MLA),
                  _full((1, D_SSD)), _full((D_MODEL, D_MODEL))],
        out_specs=[_row(ts, D_MODEL), _row(ts, D_MODEL)],
        out_shape=[_sds((s, D_MODEL)), _sds((s, D_MODEL), MXU)],
        compiler_params=_params(("parallel",)),
    )(x, proj, ya, y_ssd, o, g_ssd, w)


def _loss_head(x, g, tgt):
    s = x.shape[0]
    ts = _tile(s)

    def body(x_ref, g_ref, t_ref, dx_ref, dg_ref, loss_ref):
        @pl.when(pl.program_id(0) == 0)
        def _():
            dg_ref[...] = jnp.zeros_like(dg_ref)
            loss_ref[...] = jnp.zeros_like(loss_ref)

        xv = x_ref[...]
        gv = g_ref[...]
        yn, r = _rms(xv, gv)
        e = yn - t_ref[...]
        loss_ref[...] += jnp.sum(jnp.sum(e * e, axis=1, keepdims=True), axis=0, keepdims=True) * (0.5 / D_MODEL)
        dx, dg = _rms_bwd(e * (1.0 / D_MODEL), xv, r, gv)
        dx_ref[...] = dx
        dg_ref[...] += dg

    return pl.pallas_call(
        body, name="loss_head", grid=(s // ts,),
        in_specs=[_row(ts, D_MODEL), _full((1, D_MODEL)), _row(ts, D_MODEL)],
        out_specs=[_row(ts, D_MODEL), _full((1, D_MODEL)), _full((1, LANE))],
        out_shape=[_sds((s, D_MODEL)), _sds((1, D_MODEL)), _sds((1, LANE))],
        compiler_params=_params(("arbitrary",)),
    )(x, g, tgt)


def _outproj_bwd(dout, y, w, proj, y_ssd, o, g_ssd):
    s = dout.shape[0]
    ts = _tile(s)

    def body(dout_ref, y_ref, w_ref, p_ref, ys_ref, o_ref, g_ref,
             dya_ref, dys_ref, dsz_ref, dattn_ref, dcz_ref, dw_ref, dg_ref):
        @pl.when(pl.program_id(0) == 0)
        def _():
            dw_ref[...] = jnp.zeros_like(dw_ref)
            dg_ref[...] = jnp.zeros_like(dg_ref)

        dout_b = dout_ref[...].astype(MXU)
        dw_ref[...] += _dot_tn(y_ref[...], dout_b)
        dy = _dot_nt(dout_b, w_ref[...])
        dya_ref[...] = dy[:, :D_CONV_A]
        dyb = dy[:, D_CONV_A:D_CONV_A + D_SSD]
        sz = p_ref[:, O_SZ:O_SZ + D_SSD]
        ys = ys_ref[...]
        gv = g_ref[...]
        _, nrm, r, g0 = _ssd_gate(ys, sz, gv)
        dg_ref[...] += jnp.sum(dyb * nrm, axis=0, keepdims=True)
        dn = dyb * gv
        t = dn * nrm
        mean = jnp.where(g0, jnp.sum(jnp.where(g0, t, 0.0), axis=1, keepdims=True),
                         jnp.sum(jnp.where(g0, 0.0, t), axis=1, keepdims=True)) / (D_SSD // 2)
        dyz = r * (dn - nrm * mean)
        dys_ref[...] = dyz * _silu(sz)
        dsz_ref[...] = dyz * ys * _dsilu(sz)
        dyc = dy[:, D_CONV_A + D_SSD:]
        cz = p_ref[:, O_CZ:O_CZ + D_MLA]
        dattn_ref[...] = dyc * _silu(cz)
        dcz_ref[...] = dyc * o_ref[...] * _dsilu(cz)

    return pl.pallas_call(
        body, name="outproj_bwd", grid=(s // ts,),
        in_specs=[_row(ts, D_MODEL), _row(ts, D_MODEL), _full((D_MODEL, D_MODEL)), _row(ts, NCOL), _row(ts, D_SSD),
                  _row(ts, D_MLA), _full((1, D_SSD))],
        out_specs=[_row(ts, D_CONV_A), _row(ts, D_SSD), _row(ts, D_SSD), _row(ts, D_MLA), _row(ts, D_MLA),
                   _full((D_MODEL, D_MODEL)), _full((1, D_SSD))],
        out_shape=[_sds((s, D_CONV_A)), _sds((s, D_SSD)), _sds((s, D_SSD)), _sds((s, D_MLA)), _sds((s, D_MLA)),
                   _sds((D_MODEL, D_MODEL)), _sds((1, D_SSD))],
        compiler_params=_params(("arbitrary",)),
    )(dout, y, w, proj, y_ssd, o, g_ssd)


def _attn_bwd(q, k, v, o, d_o, lse):
    nh, s, _ = q.shape
    tq = _tile(s)
    nq = s // tq

    def body(q_ref, k_ref, v_ref, o_ref, do_ref, lse_ref, dq_ref, dk_ref, dv_ref, dop, delta):
        rowi = _iota((tq, tq), 0)
        coli = _iota((tq, tq), 1)
        lane = _iota((s, LANE), 1)
        for hh in range(2):
            dov = do_ref[...]
            ov = o_ref[...]
            if hh == 1:
                dov = pltpu.roll(dov, V_DIM, 1)
                ov = pltpu.roll(ov, V_DIM, 1)
            dov = jnp.where(lane < V_DIM, dov, 0.0)
            dop[...] = dov.astype(MXU)
            delta[...] = jnp.sum(dov * ov, axis=1, keepdims=True)
            dq_ref[hh] = jnp.zeros((s, LANE), F32)

            def kv_loop(jj, _, hh=hh):
                koff = pl.multiple_of(jj * tq, tq)
                kb = k_ref[hh, pl.ds(koff, tq), :]
                vb = v_ref[hh, pl.ds(koff, tq), :]

                def q_loop(ii, carry):
                    dk, dv = carry
                    qoff = pl.multiple_of(ii * tq, tq)
                    qb = q_ref[hh, pl.ds(qoff, tq), :]
                    dob = dop[pl.ds(qoff, tq), :]
                    sc = _dot_nt(qb, kb) * ATT_SCALE
                    sc = jnp.where(jj * tq + coli <= ii * tq + rowi, sc, NEG)
                    p = jnp.exp(sc - lse_ref[hh, pl.ds(qoff, tq), :])
                    dv = dv + _dot_tn(p, dob)
                    dp = _dot_nt(dob, vb)
                    ds = p * (dp - delta[pl.ds(qoff, tq), :]) * ATT_SCALE
                    dq_ref[hh, pl.ds(qoff, tq), :] += _dot(ds, kb)
                    return dk + _dot_tn(ds, qb), dv

                z = jnp.zeros((tq, LANE), F32)
                dk, dv = lax.fori_loop(jj, nq, q_loop, (z, z))
                dk_ref[hh, pl.ds(koff, tq), :] = dk
                dv_ref[hh, pl.ds(koff, tq), :] = dv
                return 0

            lax.fori_loop(0, nq, kv_loop, 0)

    pair = pl.BlockSpec((2, s, LANE), lambda j: (j, 0, 0))
    return pl.pallas_call(
        body, name="attn_bwd", grid=(nh // 2,),
        in_specs=[pair, pair, pair, pl.BlockSpec((s, LANE), lambda j: (0, j)), pl.BlockSpec((s, LANE), lambda j: (0, j)),
                  pl.BlockSpec((2, s, 1), lambda j: (j, 0, 0))],
        out_specs=[pair, pair, pair],
        out_shape=[_sds((nh, s, LANE))] * 3,
        scratch_shapes=[pltpu.VMEM((s, LANE), MXU), pltpu.VMEM((s, 1), F32)],
        compiler_params=_params(("parallel",)),
    )(q, k, v, o, d_o, lse)


def _ssd_bwd(xbc, proj, sc, states, dy):
    s = xbc.shape[0]
    nc = s // SSD_CHUNK
    l = SSD_CHUNK

    def body(xbc_ref, tail_ref, sc_ref, st_ref, dy_ref, dxbc_ref, dtail_ref, dsc_ref, dstate):
        @pl.when(pl.program_id(0) == 0)
        def _():
            dstate[...] = jnp.zeros_like(dstate)
            dsc_ref[...] = jnp.zeros_like(dsc_ref)

        sc_v = sc_ref[...]
        lane, row, tri, a_row, pre, dt, a_cs, a_t = _ssd_chunk_common(tail_ref[...], sc_v)
        lane1 = _iota((1, LANE), 1)
        rowp = _iota((LANE, 1), 0)
        rowl = _iota((l, 1), 0)
        d_row = sc_v[2:3, :]
        da_col = jnp.zeros((l, LANE), F32)
        da_row = jnp.zeros((LANE, l), F32)
        dt_x = jnp.zeros((l, LANE), F32)
        dd_row = jnp.zeros((1, LANE), F32)
        db = [jnp.zeros((l, LANE), F32), jnp.zeros((l, LANE), F32)]
        dc = [jnp.zeros((l, LANE), F32), jnp.zeros((l, LANE), F32)]
        for j in range(3):
            xpair = xbc_ref[:, LANE * j:LANE * (j + 1)]
            dypair = dy_ref[:, LANE * j:LANE * (j + 1)]
            sp = st_ref[0, j]
            dsp = dstate[j]
            dxpair = jnp.zeros((l, LANE), F32)
            ds_new = jnp.zeros((LANE, LANE), F32)
            decay = jnp.zeros((LANE, 1), F32)
            for half in range(2):
                h = 2 * j + half
                g = h // 3
                hm = (lane < 64) if half == 0 else (lane >= 64)
                hrow = (rowp < 64) if half == 0 else (rowp >= 64)
                ac = _pick_col(a_cs, lane, DT_LANE + h)
                ar = _pick_row(a_t, row, DT_LANE + h)
                dtc = _pick_col(dt, lane, DT_LANE + h)
                alast = jnp.sum(jnp.where(lane1 == l - 1, ar, 0.0), axis=1, keepdims=True)
                dh = jnp.sum(jnp.where(lane1 == DT_LANE + h, d_row, 0.0), axis=1, keepdims=True)
                xm = jnp.where(hm, xpair, 0.0)
                xd = xm * dtc
                dym = jnp.where(hm, dypair, 0.0)
                bm = xbc_ref[:, D_SSD + LANE * g:D_SSD + LANE * (g + 1)]
                cm = xbc_ref[:, D_SSD + SSD_BC + LANE * g:D_SSD + SSD_BC + LANE * (g + 1)]
                lm = jnp.where(row >= lane, jnp.exp(jnp.minimum(ac - ar, 0.0)), 0.0)
                e_in = jnp.exp(ac)
                f_out = jnp.exp(alast - ac)
                e_last = jnp.exp(alast)
                m = _dot_nt(cm, bm) * lm
                y_off = jnp.where(hm, _dot_nt(cm, sp), 0.0) * e_in
                dm = _dot_nt(dym, xd)
                dxd = _dot_tn(m, dym)
                dg = dm * lm
                dye = dym * e_in
                dc[g] = dc[g] + _dot(dg, bm) + _dot(dye, sp)
                db[g] = db[g] + _dot_tn(dg, cm)
                qm = dm * m
                dac = jnp.sum(qm, axis=1, keepdims=True) + jnp.sum(dym * y_off, axis=1, keepdims=True)
                dar = -jnp.sum(qm, axis=0, keepdims=True)
                dxf = jnp.where(hm, _dot_nt(bm, dsp), 0.0)
                db[g] = db[g] + _dot(xd * f_out, dsp)
                dxd = dxd + dxf * f_out
                df = jnp.sum(dxf * xd, axis=1, keepdims=True) * f_out
                dac = dac - df
                s_last = jnp.sum(df, axis=0, keepdims=True)
                ss = jnp.sum(jnp.where(hrow, dsp * sp, 0.0), axis=1, keepdims=True)
                s_last = s_last + e_last * jnp.sum(ss, axis=0, keepdims=True)
                dac = dac + jnp.where(rowl == l - 1, s_last, 0.0)
                ds_new = ds_new + _dot_tn(dye, cm)
                decay = jnp.where(hrow, e_last, decay)
                dxpair = dxpair + dxd * dtc + dym * dh
                dt_x = dt_x + jnp.where(lane == DT_LANE + h, jnp.sum(dxd * xm, axis=1, keepdims=True), 0.0)
                dsum = jnp.sum(jnp.sum(dym * xm, axis=1, keepdims=True), axis=0, keepdims=True)
                dd_row = dd_row + jnp.where(lane1 == DT_LANE + h, dsum, 0.0)
                da_col = da_col + jnp.where(lane == DT_LANE + h, dac, 0.0)
                da_row = da_row + jnp.where(row == DT_LANE + h, dar, 0.0)
            dstate[j] = dsp * decay + ds_new
            dxbc_ref[:, LANE * j:LANE * (j + 1)] = dxpair
        for g in range(2):
            dxbc_ref[:, D_SSD + LANE * g:D_SSD + LANE * (g + 1)] = db[g]
            dxbc_ref[:, D_SSD + SSD_BC + LANE * g:D_SSD + SSD_BC + LANE * (g + 1)] = dc[g]
        dla = _dot_hi_tn(tri, da_col + da_row.T)
        ddt = dt_x + dla * a_row
        dpre = ddt * _sigmoid(pre)
        dtm = (lane >= DT_LANE) & (lane < DT_LANE + SSD_HEADS)
        dtail_ref[...] = jnp.where(dtm, dpre, 0.0)
        dtm1 = (lane1 >= DT_LANE) & (lane1 < DT_LANE + SSD_HEADS)
        dsc_ref[0:1, :] += jnp.where(dtm1, jnp.sum(dpre, axis=0, keepdims=True), 0.0)
        dsc_ref[1:2, :] += jnp.where(dtm1, jnp.sum(dla * dt, axis=0, keepdims=True) * a_row, 0.0)
        dsc_ref[2:3, :] += dd_row

    rev = lambda c: nc - 1 - c
    return pl.pallas_call(
        body, name="ssd_bwd", grid=(nc,),
        in_specs=[pl.BlockSpec((l, N_XBC), lambda c: (rev(c), 0)),
                  pl.BlockSpec((l, LANE), lambda c: (rev(c), O_TAIL // LANE)), _full((8, LANE)),
                  pl.BlockSpec((1, 3, LANE, LANE), lambda c: (rev(c), 0, 0, 0)),
                  pl.BlockSpec((l, D_SSD), lambda c: (rev(c), 0))],
        out_specs=[pl.BlockSpec((l, N_XBC), lambda c: (rev(c), 0)), pl.BlockSpec((l, LANE), lambda c: (rev(c), 0)),
                   _full((8, LANE))],
        out_shape=[_sds((s, N_XBC)), _sds((s, LANE)), _sds((8, LANE))],
        scratch_shapes=[pltpu.VMEM((3, LANE, LANE), F32)],
        compiler_params=_params(("arbitrary",)),
    )(xbc, proj, sc, states, dy)


def _sconv_bwd(proj, w, b, dxbc):
    s = proj.shape[0]

    def body(u_ref, w_ref, b_ref, d_ref, du_ref, dw_ref, db_ref):
        u = u_ref[...]
        wv = w_ref[...]
        dpre = d_ref[...] * _dsilu(_sconv_pre(u, wv, b_ref[...]))
        du_ref[...] = (wv[3:4, :] * dpre + wv[2:3, :] * _shift_up(dpre, 1) + wv[1:2, :] * _shift_up(dpre, 2)
                       + wv[0:1, :] * _shift_up(dpre, 3))
        for k in range(4):
            dw_ref[k:k + 1, :] = jnp.sum(dpre * _shift_down(u, 3 - k), axis=0, keepdims=True)
        db_ref[...] = jnp.sum(dpre, axis=0, keepdims=True)

    blk = pl.BlockSpec((s, LANE), lambda j: (0, j))
    return pl.pallas_call(
        body, name="sconv_bwd", grid=(N_XBC // LANE,),
        in_specs=[_col(s, O_XBC), pl.BlockSpec((4, LANE), lambda j: (0, j)), pl.BlockSpec((1, LANE), lambda j: (0, j)), blk],
        out_specs=[blk, pl.BlockSpec((4, LANE), lambda j: (0, j)), pl.BlockSpec((1, LANE), lambda j: (0, j))],
        out_shape=[_sds((s, N_XBC)), _sds((4, N_XBC)), _sds((1, N_XBC))],
        compiler_params=_params(("parallel",)),
    )(proj, w, b, dxbc)


def _conva_bwd(proj, w, dya):
    s = proj.shape[0]

    def body(h_ref, b_ref, c_ref, z_ref, w_ref, d_ref, da_ref, dw_ref):
        ah, ab, acv, az = h_ref[...], b_ref[...], c_ref[...], z_ref[...]
        wv = w_ref[...]
        u = acv * ah
        cv = wv[2:3, :] * u + wv[1:2, :] * _shift_down(u, 1) + wv[0:1, :] * _shift_down(u, 2)
        dy = d_ref[...]
        sz = _silu(az)
        da_ref[1] = dy * cv * sz
        da_ref[3] = dy * ab * cv * _dsilu(az)
        dcv = dy * ab * sz
        du = wv[2:3, :] * dcv + wv[1:2, :] * _shift_up(dcv, 1) + wv[0:1, :] * _shift_up(dcv, 2)
        da_ref[0] = du * acv
        da_ref[2] = du * ah
        for k in range(3):
            dw_ref[k:k + 1, :] = jnp.sum(dcv * _shift_down(u, 2 - k), axis=0, keepdims=True)

    return pl.pallas_call(
        body, name="conva_bwd", grid=(D_CONV_A // LANE,),
        in_specs=[_col(s, O_AH), _col(s, O_AB), _col(s, O_AC), _col(s, O_AZ), pl.BlockSpec((3, LANE), lambda j: (0, j)),
                  pl.BlockSpec((s, LANE), lambda j: (0, j))],
        out_specs=[pl.BlockSpec((4, s, LANE), lambda j: (0, 0, j)), pl.BlockSpec((3, LANE), lambda j: (0, j))],
        out_shape=[_sds((4, s, D_CONV_A)), _sds((3, D_CONV_A))],
        compiler_params=_params(("parallel",)),
    )(proj, proj, proj, proj, w, dya)


def _mla_prep_bwd(dq, dk, dv, proj, qn, kvn, rq, rkv, gq, gkv, wq, wkv, cos, sin):
    s = proj.shape[0]
    ts = _tile(s)
    nh = MLA_HEADS

    def body(dq_ref, dk_ref, dv_ref, cqa_ref, ckv_ref, qn_ref, kvn_ref, rq_ref, rkv_ref, gq_ref, gkv_ref,
             wq_ref, wkv_ref, cos_ref, sin_ref, dcqa_ref, dckv_ref, dtail_ref, dwq_ref, dwkv_ref, dgq_ref, dgkv_ref):
        @pl.when(pl.program_id(0) == 0)
        def _():
            dwq_ref[...] = jnp.zeros_like(dwq_ref)
            dwkv_ref[...] = jnp.zeros_like(dwkv_ref)
            dgq_ref[...] = jnp.zeros_like(dgq_ref)
            dgkv_ref[...] = jnp.zeros_like(dgkv_ref)

        cosv = cos_ref[...]
        sinv = sin_ref[...]
        lane = _iota((ts, LANE), 1)
        rope_lanes = (lane >= ROPE_LANE) & (lane < ROPE_LANE + QK_ROPE)

        def unrope(gr):
            return gr * cosv + _rope_swap(gr * sinv)

        dqs, dks, dvs = [], [], []
        dkr = jnp.zeros((ts, LANE), F32)
        for h in range(nh):
            dqs.append(unrope(dq_ref[h]).astype(MXU))
            dkh = dk_ref[h]
            dks.append(jnp.where(lane < QK_NOPE, dkh, 0.0).astype(MXU))
            dkr = dkr + jnp.where(rope_lanes, dkh, 0.0)
            dvs.append(dv_ref[h].astype(MXU))
        dtail_ref[...] = pltpu.roll(jnp.where(rope_lanes, unrope(dkr), 0.0), ROPE_LANE, 1)
        dq_all = jnp.concatenate(dqs, axis=1)
        dkv_all = jnp.concatenate(dks + dvs, axis=1)
        dwq_ref[...] += _dot_tn(dq_all, qn_ref[...])
        dwkv_ref[...] += _dot_tn(dkv_all, kvn_ref[...])
        dcqa, dgq = _rms_bwd(_dot(dq_all, wq_ref[...]), cqa_ref[...], rq_ref[...], gq_ref[...])
        dckv, dgkv = _rms_bwd(_dot(dkv_all, wkv_ref[...]), ckv_ref[...], rkv_ref[...], gkv_ref[...])
        dcqa_ref[...] = dcqa
        dckv_ref[...] = dckv
        dgq_ref[...] += dgq
        dgkv_ref[...] += dgkv

    head = pl.BlockSpec((nh, ts, LANE), lambda i: (0, i, 0))
    return pl.pallas_call(
        body, name="mla_prep_bwd", grid=(s // ts,),
        in_specs=[head, head, head,
                  pl.BlockSpec((ts, Q_LORA), lambda i: (i, O_CQA // Q_LORA)),
                  pl.BlockSpec((ts, KV_LORA), lambda i: (i, O_CKV // KV_LORA)),
                  _row(ts, Q_LORA), _row(ts, KV_LORA), _row(ts, 1), _row(ts, 1),
                  _full((1, Q_LORA)), _full((1, KV_LORA)), _full((nh * LANE, Q_LORA)), _full((2 * nh * LANE, KV_LORA)),
                  _row(ts, LANE), _row(ts, LANE)],
        out_specs=[_row(ts, Q_LORA), _row(ts, KV_LORA), _row(ts, LANE), _full((nh * LANE, Q_LORA)),
                   _full((2 * nh * LANE, KV_LORA)), _full((1, Q_LORA)), _full((1, KV_LORA))],
        out_shape=[_sds((s, Q_LORA)), _sds((s, KV_LORA)), _sds((s, LANE)), _sds((nh * LANE, Q_LORA)),
                   _sds((2 * nh * LANE, KV_LORA)), _sds((1, Q_LORA)), _sds((1, KV_LORA))],
        compiler_params=_params(("arbitrary",)),
    )(dq, dk, dv, proj, proj, qn, kvn, rq, rkv, gq, gkv, wq, wkv, cos, sin)


def _inproj_bwd(da4, dsz, dxbc_in, dcqa, dckv, dcz, dtail_a, dtail_b, w, x, rstd, g, dout):
    s = x.shape[0]
    ts = _tile(s)

    def body(da_ref, dsz_ref, dxbc_ref, dcqa_ref, dckv_ref, dcz_ref, dta_ref, dtb_ref, w_ref, x_ref, r_ref, g_ref, dout_ref,
             dproj_ref, dx_ref, dg_ref):
        @pl.when(pl.program_id(0) == 0)
        def _():
            dg_ref[...] = jnp.zeros_like(dg_ref)

        dproj = jnp.concatenate(
            [da_ref[0], da_ref[1], da_ref[2], da_ref[3], dsz_ref[...], dxbc_ref[...], dcqa_ref[...], dckv_ref[...],
             dcz_ref[...], dta_ref[...] + dtb_ref[...]], axis=1).astype(MXU)
        dproj_ref[...] = dproj
        dh = _dot_nt(dproj, w_ref[...])
        dx, dg = _rms_bwd(dh, x_ref[...], r_ref[...], g_ref[...])
        dx_ref[...] = dout_ref[...] + dx
        dg_ref[...] += dg

    return pl.pallas_call(
        body, name="inproj_bwd", grid=(s // ts,),
        in_specs=[pl.BlockSpec((4, ts, D_CONV_A), lambda i: (0, i, 0)), _row(ts, D_SSD), _row(ts, N_XBC), _row(ts, Q_LORA),
                  _row(ts, KV_LORA), _row(ts, D_MLA), _row(ts, LANE), _row(ts, LANE), _full((D_MODEL, NCOL)),
                  _row(ts, D_MODEL), _row(ts, 1), _full((1, D_MODEL)), _row(ts, D_MODEL)],
        out_specs=[_row(ts, NCOL), _row(ts, D_MODEL), _full((1, D_MODEL))],
        out_shape=[_sds((s, NCOL), MXU), _sds((s, D_MODEL)), _sds((1, D_MODEL))],
        compiler_params=_params(("arbitrary",)),
    )(da4, dsz, dxbc_in, dcqa, dckv, dcz, dtail_a, dtail_b, w, x, rstd, g, dout)


DWIN_BLOCK = 640


def _dwin(h, dproj):
    s = h.shape[0]

    def body(h_ref, d_ref, o_ref):
        o_ref[...] = _dot_tn(h_ref[...], d_ref[...])

    return pl.pallas_call(
        body, name="dwin", grid=(NCOL // DWIN_BLOCK,),
        in_specs=[_full((s, D_MODEL)), pl.BlockSpec((s, DWIN_BLOCK), lambda j: (0, j))],
        out_specs=pl.BlockSpec((D_MODEL, DWIN_BLOCK), lambda j: (0, j)),
        out_shape=_sds((D_MODEL, NCOL)),
        compiler_params=_params(("parallel",)),
    )(h, dproj)


def _adamw(w, g, m, v):
    r, c = w.shape
    tr = r
    for cand in (256, 128, 64, 32, 16, 8):
        if r % cand == 0:
            tr = cand
            break
    bc1 = 1.0 - ADAM_B1 ** ADAM_STEP
    bc2 = 1.0 - ADAM_B2 ** ADAM_STEP

    def body(w_ref, g_ref, m_ref, v_ref, d_ref, mo_ref, vo_ref):
        gv = g_ref[...]
        mn = ADAM_B1 * m_ref[...] + (1.0 - ADAM_B1) * gv
        vn = ADAM_B2 * v_ref[...] + (1.0 - ADAM_B2) * (gv * gv)
        mo_ref[...] = mn
        vo_ref[...] = vn
        d_ref[...] = -ADAM_LR * ((mn / bc1) / (jnp.sqrt(vn / bc2) + ADAM_EPS) + ADAM_WD * w_ref[...])

    blk = pl.BlockSpec((tr, c), lambda i: (i, 0))
    return pl.pallas_call(
        body, name="adamw", grid=(r // tr,),
        in_specs=[blk] * 4, out_specs=[blk] * 3, out_shape=[_sds((r, c))] * 3,
        compiler_params=_params(("parallel",)),
    )(w, g, m, v)


def _perm_cols(w):
    pad = jnp.zeros(w.shape[:-1] + (NCOL - IN_COLS,), w.dtype)
    return jnp.concatenate([w[..., :2304], w[..., 2310:2566], w[..., 2566:2694], w[..., 2726:3110],
                            w[..., 2694:2726], w[..., 2304:2310], pad], axis=-1)


def _unperm_cols(g):
    return jnp.concatenate([g[..., :2304], g[..., 3104:3110], g[..., 2304:2560], g[..., 2560:2688],
                            g[..., 3072:3104], g[..., 2688:3072]], axis=-1)


def _wq_layout(wt):
    return jnp.pad(wt.reshape(MLA_HEADS, QK_NOPE + QK_ROPE, Q_LORA), ((0, 0), (0, 32), (0, 0))).reshape(MLA_HEADS * LANE, Q_LORA)


def _wq_unlayout(g):
    return g.reshape(MLA_HEADS, LANE, Q_LORA)[:, :QK_NOPE + QK_ROPE].reshape(MLA_HEADS * (QK_NOPE + QK_ROPE), Q_LORA)


def _wkv_layout(wt):
    t = wt.reshape(MLA_HEADS, 2, 64, KV_LORA).transpose(1, 0, 2, 3)
    return jnp.pad(t, ((0, 0), (0, 0), (0, 64), (0, 0))).reshape(2 * MLA_HEADS * LANE, KV_LORA)


def _wkv_unlayout(g):
    t = g.reshape(2, MLA_HEADS, LANE, KV_LORA)[:, :, :64]
    return t.transpose(1, 0, 2, 3).reshape(MLA_HEADS * LANE, KV_LORA)


def _rope_tables(positions):
    inv_freq = ROPE_BASE ** (-jnp.arange(0, QK_ROPE, 2, dtype=F32) / QK_ROPE)
    ang = positions.astype(F32)[:, None] * inv_freq
    cos, sin = jnp.cos(ang), jnp.sin(ang)
    s = positions.shape[0]
    one, zero = jnp.ones((s, ROPE_LANE), F32), jnp.zeros((s, ROPE_LANE), F32)
    cos_t = jnp.concatenate([one, cos, cos, one[:, :32]], axis=1)
    sin_t = jnp.concatenate([zero, -sin, sin, zero[:, :32]], axis=1)
    return cos_t, sin_t


def _ssd_scalars(dt_bias, a_log, d_skip):
    return jnp.pad(jnp.stack([dt_bias, a_log, d_skip]), ((0, 5), (DT_LANE, LANE - DT_LANE - SSD_HEADS)))


def _layer_fwd(x, lw, cos, sin):
    proj, h, rstd = _inproj_fwd(x, lw["norm_g"], lw["w_in"])
    ya = _conva_fwd(proj, lw["conv_a_w"])
    xbc = _sconv_fwd(proj, lw["ssd_conv_w"], lw["ssd_conv_b"])
    y_ssd, states = _ssd_fwd(xbc, proj, lw["sc"])
    q, k, v, qn, kvn, rq, rkv = _mla_prep_fwd(proj, lw["gq"], lw["gkv"], lw["wq"], lw["wkv"], cos, sin)
    o, lse = _attn_fwd(q, k, v)
    x_out, y = _outproj_fwd(x, proj, ya, y_ssd, o, lw["g_ssd"], lw["w_out"])
    saved = dict(x=x, proj=proj, h=h, rstd=rstd, xbc=xbc, y_ssd=y_ssd, states=states, q=q, k=k, v=v, qn=qn, kvn=kvn,
                 rq=rq, rkv=rkv, o=o, lse=lse, y=y)
    return x_out, saved


def _layer_bwd(dout, lw, sv, cos, sin):
    dya, dys, dsz, d_o, dcz, dw_out, dg_ssd = _outproj_bwd(dout, sv["y"], lw["w_out"], sv["proj"], sv["y_ssd"], sv["o"], lw["g_ssd"])
    dq, dk, dv = _attn_bwd(sv["q"], sv["k"], sv["v"], sv["o"], d_o, sv["lse"])
    dxbc, dtail_s, dsc = _ssd_bwd(sv["xbc"], sv["proj"], lw["sc"], sv["states"], dys)
    du, dw_sconv, db_sconv = _sconv_bwd(sv["proj"], lw["ssd_conv_w"], lw["ssd_conv_b"], dxbc)
    da4, dw_conva = _conva_bwd(sv["proj"], lw["conv_a_w"], dya)
    dcqa, dckv, dtail_m, dwq, dwkv, dgq, dgkv = _mla_prep_bwd(
        dq, dk, dv, sv["proj"], sv["qn"], sv["kvn"], sv["rq"], sv["rkv"], lw["gq"], lw["gkv"], lw["wq"], lw["wkv"], cos, sin)
    dproj, dx, dg = _inproj_bwd(da4, dsz, du, dcqa, dckv, dcz, dtail_s, dtail_m, lw["w_in"], sv["x"], sv["rstd"],
                                lw["norm_g"], dout)
    dw_in = _dwin(sv["h"], dproj)
    grads = dict(norm_g=dg, w_in=dw_in, conv_a_w=dw_conva, ssd_conv_w=dw_sconv, ssd_conv_b=db_sconv, sc=dsc,
                 g_ssd=dg_ssd, gq=dgq, wq=dwq, gkv=dgkv, wkv=dwkv, w_out=dw_out)
    return dx, grads


def _local_step(x, positions, tgt, layers, final_g):
    cos, sin = _rope_tables(positions)
    saved = []
    for lw in layers:
        x, sv = _layer_fwd(x, lw, cos, sin)
        saved.append(sv)
    dx, dgf, loss = _loss_head(x, final_g, tgt)
    grads = [None] * len(layers)
    for li in reversed(range(len(layers))):
        dx, grads[li] = _layer_bwd(dx, layers[li], saved[li], cos, sin)
    return loss, dx, grads, dgf


ANY = pl.BlockSpec(memory_space=pl.ANY)
N_CHIPS = 4
N_DEV = 8


def _place():
    return lax.axis_index("x"), lax.axis_index("y"), lax.axis_index("c")


def _gather_weights(arrs):
    n = len(arrs)

    def body(*refs):
        ins, outs = refs[:n], refs[n:2 * n]
        send_sems, recv_sems, loc_sems = refs[2 * n:]
        x, y, c = _place()
        me = 2 * x + y
        sib = (x, y, 1 - c)
        chips = [(1 - x, y), (x, 1 - y), (1 - x, 1 - y)]

        def rcopy(a, k, src, layer, chip, to):
            return pltpu.make_async_remote_copy(
                src_ref=src, dst_ref=outs[a].at[layer, chip], send_sem=send_sems.at[k * n + a],
                recv_sem=recv_sems.at[k * n + a], device_id=to, device_id_type=MESH_T)

        local = []
        for a in range(n):
            for layer in range(2):
                cp = pltpu.make_async_copy(ins[a].at[layer], outs[a].at[layer, me], loc_sems.at[2 * a + layer])
                cp.start()
                local.append(cp)
        sent = []
        for j, (cx, cy) in enumerate(chips):
            for a in range(n):
                cp = rcopy(a, j, ins[a].at[c], c, me, (cx, cy, c))
                cp.start()
                sent.append(cp)
        for j, (cx, cy) in enumerate(chips):
            src = 2 * cx + cy
            for a in range(n):
                rcopy(a, j, outs[a].at[c, src], c, src, (x, y, c)).wait_recv()
                cp = rcopy(a, 3 + j, outs[a].at[c, src], c, src, sib)
                cp.start()
                sent.append(cp)
        for j, (cx, cy) in enumerate(chips):
            src = 2 * cx + cy
            for a in range(n):
                rcopy(a, 3 + j, outs[a].at[1 - c, src], 1 - c, src, (x, y, c)).wait_recv()
        for cp in sent:
            cp.wait_send()
        for cp in local:
            cp.wait()

    return pl.pallas_call(
        body, name="gather_weights",
        in_specs=[ANY] * n, out_specs=[ANY] * n,
        out_shape=[_sds((2, N_CHIPS) + a.shape[1:], a.dtype) for a in arrs],
        scratch_shapes=[pltpu.SemaphoreType.DMA((6 * n,)), pltpu.SemaphoreType.DMA((6 * n,)), pltpu.SemaphoreType.DMA((2 * n,))],
    )(*arrs)


def _swap_layers(gs):
    n = len(gs)

    def body(*refs):
        ins, outs = refs[:n], refs[n:2 * n]
        send_sems, recv_sems = refs[2 * n:]
        x, y, c = _place()
        cps = []
        for a in range(n):
            cp = pltpu.make_async_remote_copy(src_ref=ins[a].at[1 - c], dst_ref=outs[a], send_sem=send_sems.at[a],
                                              recv_sem=recv_sems.at[a], device_id=(x, y, 1 - c), device_id_type=MESH_T)
            cp.start()
            cps.append(cp)
        for cp in cps:
            cp.wait()

    return pl.pallas_call(
        body, name="swap_layers", in_specs=[ANY] * n, out_specs=[ANY] * n,
        out_shape=[_sds(g.shape[1:]) for g in gs],
        scratch_shapes=[pltpu.SemaphoreType.DMA((n,)), pltpu.SemaphoreType.DMA((n,))],
    )(*gs)


def _exchange_chips(ps):
    n = len(ps)

    def body(*refs):
        ins, outs = refs[:n], refs[n:2 * n]
        send_sems, recv_sems, loc_sems = refs[2 * n:]
        x, y, c = _place()
        me = 2 * x + y
        chips = [(1 - x, y), (x, 1 - y), (1 - x, 1 - y)]
        local, sent = [], []
        for a in range(n):
            cp = pltpu.make_async_copy(ins[a].at[me], outs[a].at[me], loc_sems.at[a])
            cp.start()
            local.append(cp)

        def rcopy(a, j, chip_block, slot, to):
            return pltpu.make_async_remote_copy(
                src_ref=ins[a].at[chip_block], dst_ref=outs[a].at[slot], send_sem=send_sems.at[j * n + a],
                recv_sem=recv_sems.at[j * n + a], device_id=to, device_id_type=MESH_T)

        for j, (cx, cy) in enumerate(chips):
            for a in range(n):
                cp = rcopy(a, j, 2 * cx + cy, me, (cx, cy, c))
                cp.start()
                sent.append(cp)
        for j, (cx, cy) in enumerate(chips):
            for a in range(n):
                rcopy(a, j, me, 2 * cx + cy, (x, y, c)).wait_recv()
        for cp in sent:
            cp.wait_send()
        for cp in local:
            cp.wait()

    return pl.pallas_call(
        body, name="exchange_chips", in_specs=[ANY] * n, out_specs=[ANY] * n,
        out_shape=[_sds(p.shape) for p in ps],
        scratch_shapes=[pltpu.SemaphoreType.DMA((3 * n,)), pltpu.SemaphoreType.DMA((3 * n,)), pltpu.SemaphoreType.DMA((n,))],
    )(*ps)


def _share_sibling(fs):
    n = len(fs)

    def body(*refs):
        ins, outs = refs[:n], refs[n:2 * n]
        send_sems, recv_sems, loc_sems = refs[2 * n:]
        x, y, c = _place()
        local, sent = [], []
        for a in range(n):
            cp = pltpu.make_async_copy(ins[a], outs[a].at[c], loc_sems.at[a])
            cp.start()
            local.append(cp)
            cp = pltpu.make_async_remote_copy(src_ref=ins[a], dst_ref=outs[a].at[c], send_sem=send_sems.at[a],
                                              recv_sem=recv_sems.at[a], device_id=(x, y, 1 - c), device_id_type=MESH_T)
            cp.start()
            sent.append(cp)
        for a in range(n):
            pltpu.make_async_remote_copy(src_ref=ins[a], dst_ref=outs[a].at[1 - c], send_sem=send_sems.at[a],
                                         recv_sem=recv_sems.at[a], device_id=(x, y, c), device_id_type=MESH_T).wait_recv()
        for cp in sent:
            cp.wait_send()
        for cp in local:
            cp.wait()

    return pl.pallas_call(
        body, name="share_sibling", in_specs=[ANY] * n, out_specs=[ANY] * n,
        out_shape=[_sds((2,) + f.shape) for f in fs],
        scratch_shapes=[pltpu.SemaphoreType.DMA((n,)), pltpu.SemaphoreType.DMA((n,)), pltpu.SemaphoreType.DMA((n,))],
    )(*fs)


def _allreduce_small(slab):
    r = slab.shape[0]

    def body(s_ref, o_ref, gath, send_sems, recv_sems):
        x, y, c = _place()
        me = 4 * x + 2 * y + c
        gath[me] = s_ref[...]
        cps = []
        for rel in range(1, N_DEV):
            px = 1 - x if rel & 4 else x
            py = 1 - y if rel & 2 else y
            pc = 1 - c if rel & 1 else c
            cp = pltpu.make_async_remote_copy(src_ref=s_ref, dst_ref=gath.at[me], send_sem=send_sems.at[rel - 1],
                                              recv_sem=recv_sems.at[rel - 1], device_id=(px, py, pc), device_id_type=MESH_T)
            cp.start()
            cps.append(cp)
        for cp in cps:
            cp.wait()
        acc = gath[0]
        for d in range(1, N_DEV):
            acc = acc + gath[d]
        o_ref[...] = acc

    vm = pl.BlockSpec(memory_space=pltpu.VMEM)
    return pl.pallas_call(
        body, name="allreduce_small", in_specs=[vm], out_specs=vm, out_shape=_sds((r, LANE)),
        scratch_shapes=[pltpu.VMEM((N_DEV, r, LANE), F32), pltpu.SemaphoreType.DMA((N_DEV - 1,)),
                        pltpu.SemaphoreType.DMA((N_DEV - 1,))],
    )(slab)


def _add_mine(g, recv, layer):
    _, m, c = g.shape
    tm = 256 if m % 256 == 0 else (192 if m % 192 == 0 else 144)

    def body(l_ref, g_ref, r_ref, o_ref):
        o_ref[...] = g_ref[0] + r_ref[...]

    return pl.pallas_call(
        body, name="add_mine",
        grid_spec=pltpu.PrefetchScalarGridSpec(
            num_scalar_prefetch=1, grid=(m // tm,),
            in_specs=[pl.BlockSpec((1, tm, c), lambda i, l: (l[0], i, 0)), pl.BlockSpec((tm, c), lambda i, l: (i, 0))],
            out_specs=pl.BlockSpec((tm, c), lambda i, l: (i, 0))),
        out_shape=_sds((m, c)),
        compiler_params=_params(("parallel",)),
    )(layer, g, recv)


def _add_chips(p):
    _, r, c = p.shape
    tr = 128 if r % 128 == 0 else r // 2 if (r // 2) % 8 == 0 else r

    def body(p_ref, o_ref):
        o_ref[...] = ((p_ref[0] + p_ref[1]) + p_ref[2]) + p_ref[3]

    return pl.pallas_call(
        body, name="add_chips", grid=(r // tr,),
        in_specs=[pl.BlockSpec((N_CHIPS, tr, c), lambda i: (0, i, 0))], out_specs=pl.BlockSpec((tr, c), lambda i: (i, 0)),
        out_shape=_sds((r, c)),
        compiler_params=_params(("parallel",)),
    )(p)


def _reduce_scatter(gs):
    c = lax.axis_index("c")
    layer = jnp.reshape(c, (1,)).astype(jnp.int32)
    recv = _swap_layers(gs)
    ps = [_add_mine(g, rv, layer) for g, rv in zip(gs, recv)]
    ps = [p.reshape(N_CHIPS, p.shape[0] // N_CHIPS, p.shape[1]) for p in ps]
    got = _exchange_chips(ps)
    fs = [_add_chips(p) for p in got]
    return _share_sibling(fs)


WEIGHTS = ["norm_g", "w_in", "conv_a_w", "ssd_conv_w", "ssd_conv_b", "ssd_dt_bias", "ssd_a_log", "ssd_d", "ssd_norm_g",
           "mla_q_norm_g", "w_qb", "mla_kv_norm_g", "w_kvb", "w_out", "final_norm_g"]
BIG = ["w_in", "w_qb", "w_kvb", "w_out"]
SLAB_ROWS = 128
SMALL_ROWS = 72


def _to_slab(parts, rows):
    flat = jnp.concatenate([p.reshape(-1) for p in parts])
    return jnp.pad(flat, (0, rows * LANE - flat.shape[0])).reshape(rows, LANE)


def _from_slab(slab, shapes):
    flat = slab.reshape(-1)
    out, off = [], 0
    for shp in shapes:
        n = int(np.prod(shp))
        out.append(flat[off:off + n].reshape(shp))
        off += n
    return out


def kernel(x, positions, norm_g, w_in, conv_a_w, ssd_conv_w, ssd_conv_b, ssd_dt_bias, ssd_a_log, ssd_d, ssd_norm_g, mla_q_norm_g, w_qb, mla_kv_norm_g, w_kvb, w_out, final_norm_g, loss_target, m_norm_g, m_w_in, m_conv_a_w, m_ssd_conv_w, m_ssd_conv_b, m_ssd_dt_bias, m_ssd_a_log, m_ssd_d, m_ssd_norm_g, m_mla_q_norm_g, m_w_qb, m_mla_kv_norm_g, m_w_kvb, m_w_out, m_final_norm_g, v_norm_g, v_w_in, v_conv_a_w, v_ssd_conv_w, v_ssd_conv_b, v_ssd_dt_bias, v_ssd_a_log, v_ssd_d, v_ssd_norm_g, v_mla_q_norm_g, v_w_qb, v_mla_kv_norm_g, v_w_kvb, v_w_out, v_final_norm_g):
    w = dict(norm_g=norm_g, w_in=w_in, conv_a_w=conv_a_w, ssd_conv_w=ssd_conv_w, ssd_conv_b=ssd_conv_b,
             ssd_dt_bias=ssd_dt_bias, ssd_a_log=ssd_a_log, ssd_d=ssd_d, ssd_norm_g=ssd_norm_g, mla_q_norm_g=mla_q_norm_g,
             w_qb=w_qb, mla_kv_norm_g=mla_kv_norm_g, w_kvb=w_kvb, w_out=w_out, final_norm_g=final_norm_g)
    mom = dict(norm_g=m_norm_g, w_in=m_w_in, conv_a_w=m_conv_a_w, ssd_conv_w=m_ssd_conv_w, ssd_conv_b=m_ssd_conv_b,
               ssd_dt_bias=m_ssd_dt_bias, ssd_a_log=m_ssd_a_log, ssd_d=m_ssd_d, ssd_norm_g=m_ssd_norm_g,
               mla_q_norm_g=m_mla_q_norm_g, w_qb=m_w_qb, mla_kv_norm_g=m_mla_kv_norm_g, w_kvb=m_w_kvb, w_out=m_w_out,
               final_norm_g=m_final_norm_g)
    var = dict(norm_g=v_norm_g, w_in=v_w_in, conv_a_w=v_conv_a_w, ssd_conv_w=v_ssd_conv_w, ssd_conv_b=v_ssd_conv_b,
               ssd_dt_bias=v_ssd_dt_bias, ssd_a_log=v_ssd_a_log, ssd_d=v_ssd_d, ssd_norm_g=v_ssd_norm_g,
               mla_q_norm_g=v_mla_q_norm_g, w_qb=v_w_qb, mla_kv_norm_g=v_mla_kv_norm_g, w_kvb=v_w_kvb, w_out=v_w_out,
               final_norm_g=v_final_norm_g)
    chip = 2 * lax.axis_index("x") + lax.axis_index("y")

    conv_pack = jnp.zeros((DEPTH, 8, 256), F32)
    conv_pack = conv_pack.at[:, 0:3, 0:64].set(conv_a_w).at[:, 3:7, 0:224].set(ssd_conv_w)
    g_in, g_out, g_qb, g_kvb, g_conv = _gather_weights([
        _perm_cols(w_in).astype(MXU), w_out.astype(MXU), jnp.swapaxes(w_qb, 1, 2).astype(MXU),
        jnp.swapaxes(w_kvb, 1, 2).astype(MXU), conv_pack])
    conv_a_full = g_conv[:, :, 0:3, 0:64].transpose(0, 2, 1, 3).reshape(DEPTH, 3, D_CONV_A)
    sconv_full = g_conv[:, :, 3:7, 0:224].transpose(0, 2, 1, 3).reshape(DEPTH, 4, N_XBC)
    layers = []
    for l in range(DEPTH):
        layers.append(dict(
            norm_g=norm_g[l][None], w_in=g_in[l].reshape(D_MODEL, NCOL), conv_a_w=conv_a_full[l], ssd_conv_w=sconv_full[l],
            ssd_conv_b=ssd_conv_b[l][None], sc=_ssd_scalars(ssd_dt_bias[l], ssd_a_log[l], ssd_d[l]),
            g_ssd=ssd_norm_g[l][None], gq=mla_q_norm_g[l][None], gkv=mla_kv_norm_g[l][None],
            wq=_wq_layout(g_qb[l].reshape(MLA_HEADS * 96, Q_LORA)), wkv=_wkv_layout(g_kvb[l].reshape(MLA_HEADS * LANE, KV_LORA)),
            w_out=g_out[l].reshape(D_MODEL, D_MODEL)))

    loss, grad_x, lg, dgf = _local_step(x[0], positions[0], loss_target[0], layers, final_norm_g[None])

    r_in, r_out, r_qb, r_kvb = _reduce_scatter([
        jnp.stack([lg[l]["w_in"] for l in range(DEPTH)]),
        jnp.stack([lg[l]["w_out"] for l in range(DEPTH)]),
        jnp.stack([_wq_unlayout(lg[l]["wq"]) for l in range(DEPTH)]),
        jnp.stack([_wkv_unlayout(lg[l]["wkv"]) for l in range(DEPTH)])])
    grad = dict(w_in=_unperm_cols(r_in), w_out=r_out, w_qb=jnp.swapaxes(r_qb, 1, 2), w_kvb=jnp.swapaxes(r_kvb, 1, 2))

    small_names = ["norm_g", "conv_a_w", "ssd_conv_w", "ssd_conv_b", "sc", "g_ssd", "gq", "gkv"]
    parts = [loss[0, 0:1], dgf]
    for l in range(DEPTH):
        parts += [lg[l][nm][:3, DT_LANE:DT_LANE + SSD_HEADS] if nm == "sc" else lg[l][nm] for nm in small_names]
    shapes = [(1,), (D_MODEL,)] + [(D_MODEL,), (3, D_CONV_A), (4, N_XBC), (N_XBC,), (3, SSD_HEADS), (D_SSD,), (Q_LORA,), (KV_LORA,)] * DEPTH
    red = _from_slab(_allreduce_small(_to_slab(parts, SLAB_ROWS)), shapes)
    loss_out = red[0][0]
    grad["final_norm_g"] = red[1]
    per = [red[2 + 8 * l:10 + 8 * l] for l in range(DEPTH)]
    grad["norm_g"] = jnp.stack([per[l][0] for l in range(DEPTH)])
    grad["conv_a_w"] = lax.dynamic_slice_in_dim(jnp.stack([per[l][1] for l in range(DEPTH)]), chip * 64, 64, axis=2)
    grad["ssd_conv_w"] = lax.dynamic_slice_in_dim(jnp.stack([per[l][2] for l in range(DEPTH)]), chip * 224, 224, axis=2)
    grad["ssd_conv_b"] = jnp.stack([per[l][3] for l in range(DEPTH)])
    grad["ssd_dt_bias"] = jnp.stack([per[l][4][0] for l in range(DEPTH)])
    grad["ssd_a_log"] = jnp.stack([per[l][4][1] for l in range(DEPTH)])
    grad["ssd_d"] = jnp.stack([per[l][4][2] for l in range(DEPTH)])
    grad["ssd_norm_g"] = jnp.stack([per[l][5] for l in range(DEPTH)])
    grad["mla_q_norm_g"] = jnp.stack([per[l][6] for l in range(DEPTH)])
    grad["mla_kv_norm_g"] = jnp.stack([per[l][7] for l in range(DEPTH)])

    delta, new_m, new_v = {}, {}, {}
    for nm in BIG:
        shp = w[nm].shape
        two_d = (shp[0] * shp[1], shp[2])
        d, mo, vo = _adamw(w[nm].reshape(two_d), grad[nm].reshape(two_d), mom[nm].reshape(two_d), var[nm].reshape(two_d))
        delta[nm], new_m[nm], new_v[nm] = d.reshape(shp), mo.reshape(shp), vo.reshape(shp)
    small = [nm for nm in WEIGHTS if nm not in BIG]
    sshapes = [w[nm].shape for nm in small]
    d, mo, vo = _adamw(_to_slab([w[nm] for nm in small], SMALL_ROWS), _to_slab([grad[nm] for nm in small], SMALL_ROWS),
                       _to_slab([mom[nm] for nm in small], SMALL_ROWS), _to_slab([var[nm] for nm in small], SMALL_ROWS))
    for nm, dv, mv, vv in zip(small, _from_slab(d, sshapes), _from_slab(mo, sshapes), _from_slab(vo, sshapes)):
        delta[nm], new_m[nm], new_v[nm] = dv, mv, vv

    return (loss_out, grad_x[None], *[grad[nm] for nm in WEIGHTS], *[delta[nm] for nm in WEIGHTS],
            *[new_m[nm] for nm in WEIGHTS], *[new_v[nm] for nm in WEIGHTS])
```

```python
import functools
import math

import numpy as np
import jax
import jax.numpy as jnp
from jax import lax
from jax.experimental import pallas as pl
from jax.experimental.pallas import tpu as pltpu

F32 = jnp.float32
MXU = jnp.bfloat16

D_MODEL = 1024
DEPTH = 2
D_CONV_A = 256
D_SSD = 384
SSD_HEADS = 6
SSD_BC = 256
SSD_CHUNK = 128
SSD_NORM_EPS = 1e-5
MLA_HEADS = 6
Q_LORA = 256
KV_LORA = 128
QK_NOPE = 64
QK_ROPE = 32
V_DIM = 64
D_MLA = 384
ROPE_BASE = 10000.0
NORM_EPS = 1e-6
IN_COLS = 3110
LANE = 128

O_AH, O_AB, O_AC, O_AZ = 0, 256, 512, 768
O_SZ = 1024
O_XBC = 1408
O_CQA = 2304
O_CKV = 2560
O_CZ = 2688
O_TAIL = 3072
NCOL = 3200
N_XBC = D_SSD + 2 * SSD_BC
DT_LANE = 32
ROPE_LANE = 64

ADAM_LR, ADAM_B1, ADAM_B2, ADAM_EPS, ADAM_WD, ADAM_STEP = 0.001, 0.9, 0.999, 1e-08, 0.01, 10

VMEM_LIMIT = 56 * 1024 * 1024
MESH_T = pl.DeviceIdType.MESH


def _dot(a, b):
    return jnp.dot(a.astype(MXU), b.astype(MXU), preferred_element_type=F32)


def _dot_nt(a, b):
    return lax.dot_general(a.astype(MXU), b.astype(MXU), (((1,), (1,)), ((), ())), preferred_element_type=F32)


def _dot_tn(a, b):
    return lax.dot_general(a.astype(MXU), b.astype(MXU), (((0,), (0,)), ((), ())), preferred_element_type=F32)


def _dot_hi(a, b):
    return jnp.dot(a, b, precision=lax.Precision.HIGHEST, preferred_element_type=F32)


def _dot_hi_tn(a, b):
    return lax.dot_general(a, b, (((0,), (0,)), ((), ())), precision=lax.Precision.HIGHEST, preferred_element_type=F32)


def _sigmoid(z):
    return 1.0 / (1.0 + jnp.exp(-z))


def _silu(z):
    return z * _sigmoid(z)


def _dsilu(z):
    s = _sigmoid(z)
    return s * (1.0 + z * (1.0 - s))


def _softplus(z):
    e = jnp.exp(-jnp.abs(z))
    return jnp.maximum(z, 0.0) + jnp.where(e < 1e-3, e * (1.0 - 0.5 * e), jnp.log(1.0 + e))


def _iota(shape, dim):
    return lax.broadcasted_iota(jnp.int32, shape, dim)


def _shift_down(u, k):
    if k == 0:
        return u
    return jnp.where(_iota(u.shape, 0) >= k, pltpu.roll(u, k, 0), 0.0)


def _shift_up(u, k):
    if k == 0:
        return u
    n = u.shape[0]
    return jnp.where(_iota(u.shape, 0) < n - k, pltpu.roll(u, n - k, 0), 0.0)


def _rope_swap(t):
    lane = _iota(t.shape, 1)
    lo = (lane >= ROPE_LANE) & (lane < ROPE_LANE + 16)
    hi = (lane >= ROPE_LANE + 16) & (lane < ROPE_LANE + 32)
    return jnp.where(lo, pltpu.roll(t, LANE - 16, 1), jnp.where(hi, pltpu.roll(t, 16, 1), 0.0))


def _params(sem=None):
    return pltpu.CompilerParams(dimension_semantics=sem, vmem_limit_bytes=VMEM_LIMIT)


def _full(shape):
    nd = len(shape)
    return pl.BlockSpec(shape, lambda *_: (0,) * nd)


def _sds(shape, dtype=F32):
    return jax.ShapeDtypeStruct(shape, dtype)


def _tile(s):
    return min(256, s)


def _row(ts, w):
    return pl.BlockSpec((ts, w), lambda i: (i, 0))


def _col(s, off):
    return pl.BlockSpec((s, LANE), lambda j, _o=off // LANE: (0, _o + j))


def _rms(c, g):
    r = lax.rsqrt(jnp.mean(c * c, axis=-1, keepdims=True) + NORM_EPS)
    return c * r * g, r


def _rms_bwd(dn, c, r, g):
    ch = c * r
    dch = dn * g
    dc = r * (dch - ch * jnp.mean(dch * ch, axis=-1, keepdims=True))
    return dc, jnp.sum(dn * ch, axis=0, keepdims=True)


def _inproj_fwd(x, g, w):
    s = x.shape[0]
    ts = _tile(s)

    def body(x_ref, g_ref, w_ref, proj_ref, h_ref, r_ref):
        hn, r = _rms(x_ref[...], g_ref[...])
        h = hn.astype(MXU)
        h_ref[...] = h
        r_ref[...] = r
        proj_ref[...] = jnp.dot(h, w_ref[...], preferred_element_type=F32)

    return pl.pallas_call(
        body, name="inproj_fwd", grid=(s // ts,),
        in_specs=[_row(ts, D_MODEL), _full((1, D_MODEL)), _full((D_MODEL, NCOL))],
        out_specs=[_row(ts, NCOL), _row(ts, D_MODEL), _row(ts, 1)],
        out_shape=[_sds((s, NCOL)), _sds((s, D_MODEL), MXU), _sds((s, 1))],
        compiler_params=_params(("parallel",)),
    )(x, g, w)


def _conva_fwd(proj, w):
    s = proj.shape[0]

    def body(h_ref, b_ref, c_ref, z_ref, w_ref, y_ref):
        u = c_ref[...] * h_ref[...]
        wv = w_ref[...]
        cv = wv[2:3, :] * u + wv[1:2, :] * _shift_down(u, 1) + wv[0:1, :] * _shift_down(u, 2)
        y_ref[...] = b_ref[...] * cv * _silu(z_ref[...])

    return pl.pallas_call(
        body, name="conva_fwd", grid=(D_CONV_A // LANE,),
        in_specs=[_col(s, O_AH), _col(s, O_AB), _col(s, O_AC), _col(s, O_AZ), pl.BlockSpec((3, LANE), lambda j: (0, j))],
        out_specs=pl.BlockSpec((s, LANE), lambda j: (0, j)),
        out_shape=_sds((s, D_CONV_A)),
        compiler_params=_params(("parallel",)),
    )(proj, proj, proj, proj, w)


def _sconv_pre(u, wv, bv):
    return (wv[3:4, :] * u + wv[2:3, :] * _shift_down(u, 1) + wv[1:2, :] * _shift_down(u, 2)
            + wv[0:1, :] * _shift_down(u, 3) + bv)


def _sconv_fwd(proj, w, b):
    s = proj.shape[0]

    def body(u_ref, w_ref, b_ref, o_ref):
        o_ref[...] = _silu(_sconv_pre(u_ref[...], w_ref[...], b_ref[...]))

    return pl.pallas_call(
        body, name="sconv_fwd", grid=(N_XBC // LANE,),
        in_specs=[_col(s, O_XBC), pl.BlockSpec((4, LANE), lambda j: (0, j)), pl.BlockSpec((1, LANE), lambda j: (0, j))],
        out_specs=pl.BlockSpec((s, LANE), lambda j: (0, j)),
        out_shape=_sds((s, N_XBC)),
        compiler_params=_params(("parallel",)),
    )(proj, w, b)


def _ssd_chunk_common(tail, sc):
    l = SSD_CHUNK
    lane = _iota((l, LANE), 1)
    row = _iota((l, LANE), 0)
    tri = (row >= lane).astype(F32)
    a_row = -jnp.exp(sc[1:2, :])
    pre = tail + sc[0:1, :]
    dt = _softplus(pre)
    a_cs = _dot_hi(tri, dt * a_row)
    return lane, row, tri, a_row, pre, dt, a_cs, a_cs.T


def _pick_col(m, lane, k):
    return jnp.sum(jnp.where(lane == k, m, 0.0), axis=1, keepdims=True)


def _pick_row(m, row, k):
    return jnp.sum(jnp.where(row == k, m, 0.0), axis=0, keepdims=True)


def _ssd_fwd(xbc, proj, sc):
    s = xbc.shape[0]
    nc = s // SSD_CHUNK
    l = SSD_CHUNK

    def body(xbc_ref, tail_ref, sc_ref, y_ref, st_ref, state):
        @pl.when(pl.program_id(0) == 0)
        def _():
            state[...] = jnp.zeros_like(state)

        sc_v = sc_ref[...]
        lane, row, _, _, _, dt, a_cs, a_t = _ssd_chunk_common(tail_ref[...], sc_v)
        lane1 = _iota((1, LANE), 1)
        rowp = _iota((LANE, 1), 0)
        d_row = sc_v[2:3, :]
        for j in range(3):
            st_ref[0, j] = state[j]
        for j in range(3):
            xpair = xbc_ref[:, LANE * j:LANE * (j + 1)]
            sp = state[j]
            ypair = jnp.zeros((l, LANE), F32)
            new_s = jnp.zeros((LANE, LANE), F32)
            decay = jnp.zeros((LANE, 1), F32)
            for half in range(2):
                h = 2 * j + half
                g = h // 3
                hm = (lane < 64) if half == 0 else (lane >= 64)
                hrow = (rowp < 64) if half == 0 else (rowp >= 64)
                ac = _pick_col(a_cs, lane, DT_LANE + h)
                ar = _pick_row(a_t, row, DT_LANE + h)
                dtc = _pick_col(dt, lane, DT_LANE + h)
                alast = jnp.sum(jnp.where(lane1 == l - 1, ar, 0.0), axis=1, keepdims=True)
                dh = jnp.sum(jnp.where(lane1 == DT_LANE + h, d_row, 0.0), axis=1, keepdims=True)
                xm = jnp.where(hm, xpair, 0.0)
                xd = xm * dtc
                bm = xbc_ref[:, D_SSD + LANE * g:D_SSD + LANE * (g + 1)]
                cm = xbc_ref[:, D_SSD + SSD_BC + LANE * g:D_SSD + SSD_BC + LANE * (g + 1)]
                lm = jnp.where(row >= lane, jnp.exp(jnp.minimum(ac - ar, 0.0)), 0.0)
                y_diag = _dot(_dot_nt(cm, bm) * lm, xd)
                y_off = jnp.where(hm, _dot_nt(cm, sp), 0.0) * jnp.exp(ac)
                ypair = ypair + y_diag + y_off + xm * dh
                new_s = new_s + _dot_tn(xd * jnp.exp(alast - ac), bm)
                decay = jnp.where(hrow, jnp.exp(alast), decay)
            state[j] = sp * decay + new_s
            y_ref[:, LANE * j:LANE * (j + 1)] = ypair

    return pl.pallas_call(
        body, name="ssd_fwd", grid=(nc,),
        in_specs=[pl.BlockSpec((l, N_XBC), lambda c: (c, 0)),
                  pl.BlockSpec((l, LANE), lambda c: (c, O_TAIL // LANE)), _full((8, LANE))],
        out_specs=[pl.BlockSpec((l, D_SSD), lambda c: (c, 0)), pl.BlockSpec((1, 3, LANE, LANE), lambda c: (c, 0, 0, 0))],
        out_shape=[_sds((s, D_SSD)), _sds((nc, 3, LANE, LANE))],
        scratch_shapes=[pltpu.VMEM((3, LANE, LANE), F32)],
        compiler_params=_params(("arbitrary",)),
    )(xbc, proj, sc)


def _mla_prep_fwd(proj, gq, gkv, wq, wkv, cos, sin):
    s = proj.shape[0]
    ts = _tile(s)
    nh = MLA_HEADS

    def body(cqa_ref, ckv_ref, tail_ref, gq_ref, gkv_ref, wq_ref, wkv_ref, cos_ref, sin_ref,
             q_ref, k_ref, v_ref, qn_ref, kvn_ref, rq_ref, rkv_ref):
        qn, rq = _rms(cqa_ref[...], gq_ref[...])
        kvn, rkv = _rms(ckv_ref[...], gkv_ref[...])
        qn = qn.astype(MXU)
        kvn = kvn.astype(MXU)
        qn_ref[...] = qn
        kvn_ref[...] = kvn
        rq_ref[...] = rq
        rkv_ref[...] = rkv
        q = _dot_nt(qn, wq_ref[...])
        kv = _dot_nt(kvn, wkv_ref[...])
        cosv = cos_ref[...]
        sinv = sin_ref[...]
        lane = _iota((ts, LANE), 1)
        rope_lanes = (lane >= ROPE_LANE) & (lane < ROPE_LANE + QK_ROPE)
        kr = jnp.where(rope_lanes, pltpu.roll(tail_ref[...], ROPE_LANE, 1), 0.0)
        kr = kr * cosv + _rope_swap(kr) * sinv
        for h in range(nh):
            qh = q[:, LANE * h:LANE * (h + 1)]
            q_ref[h] = ((qh * cosv + _rope_swap(qh) * sinv) * ATT_SCALE).astype(MXU)
            k_ref[h] = (kv[:, LANE * h:LANE * (h + 1)] + kr).astype(MXU)
            v_ref[h] = kv[:, LANE * (nh + h):LANE * (nh + h + 1)].astype(MXU)

    head = pl.BlockSpec((nh, ts, LANE), lambda i: (0, i, 0))
    return pl.pallas_call(
        body, name="mla_prep_fwd", grid=(s // ts,),
        in_specs=[pl.BlockSpec((ts, Q_LORA), lambda i: (i, O_CQA // Q_LORA)),
                  pl.BlockSpec((ts, KV_LORA), lambda i: (i, O_CKV // KV_LORA)),
                  pl.BlockSpec((ts, LANE), lambda i: (i, O_TAIL // LANE)),
                  _full((1, Q_LORA)), _full((1, KV_LORA)), _full((nh * LANE, Q_LORA)), _full((2 * nh * LANE, KV_LORA)),
                  _row(ts, LANE), _row(ts, LANE)],
        out_specs=[head, head, head, _row(ts, Q_LORA), _row(ts, KV_LORA), _row(ts, 1), _row(ts, 1)],
        out_shape=[_sds((nh, s, LANE), MXU)] * 3 + [_sds((s, Q_LORA), MXU), _sds((s, KV_LORA), MXU), _sds((s, 1)), _sds((s, 1))],
        compiler_params=_params(("parallel",)),
    )(proj, proj, proj, gq, gkv, wq, wkv, cos, sin)


ATT_SCALE = (QK_NOPE + QK_ROPE) ** -0.5
NEG = -1e30


def _att_tile(s):
    return min(256, s // 2)


def _attn_fwd(q, k, v):
    nh, s, _ = q.shape
    tq = _att_tile(s)
    nq = s // tq

    def body(q_ref, k_ref, v_ref, o_ref, lse_ref):
        i = pl.program_id(1)
        rowi = _iota((tq, tq), 0)
        coli = _iota((tq, tq), 1)
        zero = (jnp.full((tq, 1), NEG, F32), jnp.zeros((tq, 1), F32), jnp.zeros((tq, LANE), F32))
        state = [zero, zero]
        done = [zero, zero]
        for t in range(nq + 1):
            first = t <= i
            qblk = jnp.where(first, i, nq - 1 - i)
            kblk = jnp.where(first, t, t - i - 1)
            qoff = pl.multiple_of(qblk * tq, tq)
            koff = pl.multiple_of(kblk * tq, tq)
            keep = coli <= rowi + jnp.where(kblk == qblk, 0, tq)
            restart = t == i + 1
            for hh in range(2):
                m, lsum, acc = state[hh]
                if t > 0:
                    done[hh] = tuple(jnp.where(restart, a, b) for a, b in zip(state[hh], done[hh]))
                    m = jnp.where(restart, NEG, m)
                    lsum = jnp.where(restart, 0.0, lsum)
                    acc = jnp.where(restart, 0.0, acc)
                sc = _dot_nt(q_ref[hh, pl.ds(qoff, tq), :], k_ref[hh, pl.ds(koff, tq), :])
                sc = jnp.where(keep, sc, NEG)
                m_new = jnp.maximum(m, jnp.max(sc, axis=1, keepdims=True))
                p = jnp.exp(sc - m_new)
                alpha = jnp.exp(m - m_new)
                lsum = alpha * lsum + jnp.sum(p, axis=1, keepdims=True)
                acc = alpha * acc + _dot(p, v_ref[hh, pl.ds(koff, tq), :])
                state[hh] = (m_new, lsum, acc)
        for blk, res in ((i, done), (nq - 1 - i, state)):
            off = pl.multiple_of(blk * tq, tq)
            out = None
            for hh in range(2):
                m, lsum, acc = res[hh]
                o = acc * (1.0 / lsum)
                lse_ref[hh, pl.ds(off, tq), :] = m + jnp.log(lsum)
                out = o if hh == 0 else out + pltpu.roll(o, V_DIM, 1)
            o_ref[pl.ds(off, tq), :] = out

    pair = pl.BlockSpec((2, s, LANE), lambda j, i: (j, 0, 0))
    return pl.pallas_call(
        body, name="attn_fwd", grid=(nh // 2, nq // 2),
        in_specs=[pair, pair, pair],
        out_specs=[pl.BlockSpec((s, LANE), lambda j, i: (0, j)), pl.BlockSpec((2, s, 1), lambda j, i: (j, 0, 0))],
        out_shape=[_sds((s, D_MLA)), _sds((nh, s, 1))],
        compiler_params=_params(("parallel", "arbitrary")),
    )(q, k, v)


def _ssd_gate(y_ssd, s_z, g):
    yz = y_ssd * _silu(s_z)
    g0 = _iota(yz.shape, 1) < D_SSD // 2
    sq = yz * yz
    ms0 = jnp.sum(jnp.where(g0, sq, 0.0), axis=1, keepdims=True) / (D_SSD // 2)
    ms1 = jnp.sum(jnp.where(g0, 0.0, sq), axis=1, keepdims=True) / (D_SSD // 2)
    r = jnp.where(g0, lax.rsqrt(ms0 + SSD_NORM_EPS), lax.rsqrt(ms1 + SSD_NORM_EPS))
    nrm = yz * r
    return nrm * g, nrm, r, g0


def _outproj_fwd(x, proj, ya, y_ssd, o, g_ssd, w):
    s = x.shape[0]
    ts = _tile(s)

    def body(x_ref, p_ref, ya_ref, ys_ref, o_ref, g_ref, w_ref, xo_ref, y_ref):
        yb = _ssd_gate(ys_ref[...], p_ref[:, O_SZ:O_SZ + D_SSD], g_ref[...])[0]
        yc = o_ref[...] * _silu(p_ref[:, O_CZ:O_CZ + D_MLA])
        y = jnp.concatenate([ya_ref[...], yb, yc], axis=1).astype(MXU)
        y_ref[...] = y
        xo_ref[...] = x_ref[...] + jnp.dot(y, w_ref[...], preferred_element_type=F32)

    return pl.pallas_call(
        body, name="outproj_fwd", grid=(s // ts,),
        in_specs=[_row(ts, D_MODEL), _row(ts, NCOL), _row(ts, D_CONV_A), _row(ts, D_SSD), _row(ts, D_MLA),
                  _full((1, D_SSD)), _full((D_MODEL, D_MODEL))],
        out_specs=[_row(ts, D_MODEL), _row(ts, D_MODEL)],
        out_shape=[_sds((s, D_MODEL)), _sds((s, D_MODEL), MXU)],
        compiler_params=_params(("parallel",)),
    )(x, proj, ya, y_ssd, o, g_ssd, w)


def _loss_head(x, g, tgt):
    s = x.shape[0]
    ts = _tile(s)

    def body(x_ref, g_ref, t_ref, dx_ref, dg_ref, loss_ref):
        @pl.when(pl.program_id(0) == 0)
        def _():
            dg_ref[...] = jnp.zeros_like(dg_ref)
            loss_ref[...] = jnp.zeros_like(loss_ref)

        xv = x_ref[...]
        gv = g_ref[...]
        yn, r = _rms(xv, gv)
        e = yn - t_ref[...]
        loss_ref[...] += jnp.sum(jnp.sum(e * e, axis=1, keepdims=True), axis=0, keepdims=True) * (0.5 / D_MODEL)
        dx, dg = _rms_bwd(e * (1.0 / D_MODEL), xv, r, gv)
        dx_ref[...] = dx
        dg_ref[...] += dg

    return pl.pallas_call(
        body, name="loss_head", grid=(s // ts,),
        in_specs=[_row(ts, D_MODEL), _full((1, D_MODEL)), _row(ts, D_MODEL)],
        out_specs=[_row(ts, D_MODEL), _full((1, D_MODEL)), _full((1, LANE))],
        out_shape=[_sds((s, D_MODEL)), _sds((1, D_MODEL)), _sds((1, LANE))],
        compiler_params=_params(("arbitrary",)),
    )(x, g, tgt)


def _outproj_bwd(dout, y, w, proj, y_ssd, o, g_ssd, layer, stack):
    s = dout.shape[0]
    ts = _tile(s)

    def body(dout_ref, y_ref, w_ref, p_ref, ys_ref, o_ref, g_ref, *rest):
        dya_ref, dys_ref, dsz_ref, dattn_ref, dcz_ref, dg_ref, dw_ref = rest[-7:]

        @pl.when(pl.program_id(0) == 0)
        def _():
            dw_ref[...] = jnp.zeros_like(dw_ref)
            dg_ref[...] = jnp.zeros_like(dg_ref)

        dout_b = dout_ref[...].astype(MXU)
        dw_ref[0] += _dot_tn(y_ref[...], dout_b)
        dy = _dot_nt(dout_b, w_ref[...])
        dya_ref[...] = dy[:, :D_CONV_A]
        dyb = dy[:, D_CONV_A:D_CONV_A + D_SSD]
        sz = p_ref[:, O_SZ:O_SZ + D_SSD]
        ys = ys_ref[...]
        gv = g_ref[...]
        _, nrm, r, g0 = _ssd_gate(ys, sz, gv)
        dg_ref[...] += jnp.sum(dyb * nrm, axis=0, keepdims=True)
        dn = dyb * gv
        t = dn * nrm
        mean = jnp.where(g0, jnp.sum(jnp.where(g0, t, 0.0), axis=1, keepdims=True),
                         jnp.sum(jnp.where(g0, 0.0, t), axis=1, keepdims=True)) / (D_SSD // 2)
        dyz = r * (dn - nrm * mean)
        dys_ref[...] = dyz * _silu(sz)
        dsz_ref[...] = dyz * ys * _dsilu(sz)
        dyc = dy[:, D_CONV_A + D_SSD:]
        cz = p_ref[:, O_CZ:O_CZ + D_MLA]
        dattn_ref[...] = dyc * _silu(cz)
        dcz_ref[...] = dyc * o_ref[...] * _dsilu(cz)

    extra = [] if stack is None else [stack]
    return pl.pallas_call(
        body, name="outproj_bwd", grid=(s // ts,),
        in_specs=[_row(ts, D_MODEL), _row(ts, D_MODEL), _full((D_MODEL, D_MODEL)), _row(ts, NCOL), _row(ts, D_SSD),
                  _row(ts, D_MLA), _full((1, D_SSD))] + [ANY] * len(extra),
        out_specs=[_row(ts, D_CONV_A), _row(ts, D_SSD), _row(ts, D_SSD), _row(ts, D_MLA), _row(ts, D_MLA),
                   _full((1, D_SSD)), pl.BlockSpec((1, D_MODEL, D_MODEL), lambda i: (layer, 0, 0))],
        out_shape=[_sds((s, D_CONV_A)), _sds((s, D_SSD)), _sds((s, D_SSD)), _sds((s, D_MLA)), _sds((s, D_MLA)),
                   _sds((1, D_SSD)), _sds((DEPTH, D_MODEL, D_MODEL))],
        input_output_aliases={7: 6} if extra else {},
        compiler_params=_params(("arbitrary",)),
    )(dout, y, w, proj, y_ssd, o, g_ssd, *extra)


def _attn_bwd(q, k, v, o, d_o, lse):
    nh, s, _ = q.shape
    tq = _att_tile(s)
    nq = s // tq

    def body(q_ref, k_ref, v_ref, o_ref, do_ref, lse_ref, dq_ref, dk_ref, dv_ref, dop, delta):
        i = pl.program_id(1)

        @pl.when(i == 0)
        def _():
            lane = _iota((s, LANE), 1)
            for hh in range(2):
                dov = do_ref[...]
                ov = o_ref[...]
                if hh == 1:
                    dov = pltpu.roll(dov, V_DIM, 1)
                    ov = pltpu.roll(ov, V_DIM, 1)
                dov = jnp.where(lane < V_DIM, dov, 0.0)
                dop[hh] = dov.astype(MXU)
                delta[hh] = jnp.sum(dov * ov, axis=1, keepdims=True)
                dq_ref[hh] = jnp.zeros((s, LANE), F32)

        rowi = _iota((tq, tq), 0)
        coli = _iota((tq, tq), 1)
        z = jnp.zeros((tq, LANE), F32)
        state = [(z, z), (z, z)]
        done = [(z, z), (z, z)]
        for t in range(nq + 1):
            first = t <= nq - 1 - i
            kblk = jnp.where(first, i, nq - 1 - i)
            qblk = jnp.where(first, i + t, t - 1)
            qoff = pl.multiple_of(qblk * tq, tq)
            koff = pl.multiple_of(kblk * tq, tq)
            keep = coli <= rowi + jnp.where(kblk == qblk, 0, tq)
            restart = t == nq - i
            for hh in range(2):
                dk, dv = state[hh]
                if t > 0:
                    done[hh] = tuple(jnp.where(restart, a, b) for a, b in zip(state[hh], done[hh]))
                    dk = jnp.where(restart, 0.0, dk)
                    dv = jnp.where(restart, 0.0, dv)
                kb = k_ref[hh, pl.ds(koff, tq), :]
                qb = q_ref[hh, pl.ds(qoff, tq), :]
                dob = dop[hh, pl.ds(qoff, tq), :]
                sc = jnp.where(keep, _dot_nt(qb, kb), NEG)
                p = jnp.exp(sc - lse_ref[hh, pl.ds(qoff, tq), :])
                dp = _dot_nt(dob, v_ref[hh, pl.ds(koff, tq), :])
                ds = p * (dp - delta[hh, pl.ds(qoff, tq), :])
                dq_ref[hh, pl.ds(qoff, tq), :] += _dot(ds, kb)
                state[hh] = (dk + _dot_tn(ds, qb), dv + _dot_tn(p, dob))
        for blk, res in ((i, done), (nq - 1 - i, state)):
            off = pl.multiple_of(blk * tq, tq)
            for hh in range(2):
                dk_ref[hh, pl.ds(off, tq), :] = res[hh][0]
                dv_ref[hh, pl.ds(off, tq), :] = res[hh][1]

    pair = pl.BlockSpec((2, s, LANE), lambda j, i: (j, 0, 0))
    return pl.pallas_call(
        body, name="attn_bwd", grid=(nh // 2, nq // 2),
        in_specs=[pair, pair, pair, pl.BlockSpec((s, LANE), lambda j, i: (0, j)), pl.BlockSpec((s, LANE), lambda j, i: (0, j)),
                  pl.BlockSpec((2, s, 1), lambda j, i: (j, 0, 0))],
        out_specs=[pair, pair, pair],
        out_shape=[_sds((nh, s, LANE))] * 3,
        scratch_shapes=[pltpu.VMEM((2, s, LANE), MXU), pltpu.VMEM((2, s, 1), F32)],
        compiler_params=_params(("parallel", "arbitrary")),
    )(q, k, v, o, d_o, lse)


def _ssd_bwd(xbc, proj, sc, states, dy):
    s = xbc.shape[0]
    nc = s // SSD_CHUNK
    l = SSD_CHUNK

    def body(xbc_ref, tail_ref, sc_ref, st_ref, dy_ref, dxbc_ref, dtail_ref, dsc_ref, dstate):
        @pl.when(pl.program_id(0) == 0)
        def _():
            dstate[...] = jnp.zeros_like(dstate)
            dsc_ref[...] = jnp.zeros_like(dsc_ref)

        sc_v = sc_ref[...]
        lane, row, tri, a_row, pre, dt, a_cs, a_t = _ssd_chunk_common(tail_ref[...], sc_v)
        lane1 = _iota((1, LANE), 1)
        rowp = _iota((LANE, 1), 0)
        rowl = _iota((l, 1), 0)
        d_row = sc_v[2:3, :]
        da_col = jnp.zeros((l, LANE), F32)
        da_row = jnp.zeros((LANE, l), F32)
        dt_x = jnp.zeros((l, LANE), F32)
        dd_row = jnp.zeros((1, LANE), F32)
        db = [jnp.zeros((l, LANE), F32), jnp.zeros((l, LANE), F32)]
        dc = [jnp.zeros((l, LANE), F32), jnp.zeros((l, LANE), F32)]
        for j in range(3):
            xpair = xbc_ref[:, LANE * j:LANE * (j + 1)]
            dypair = dy_ref[:, LANE * j:LANE * (j + 1)]
            sp = st_ref[0, j]
            dsp = dstate[j]
            dxpair = jnp.zeros((l, LANE), F32)
            ds_new = jnp.zeros((LANE, LANE), F32)
            decay = jnp.zeros((LANE, 1), F32)
            for half in range(2):
                h = 2 * j + half
                g = h // 3
                hm = (lane < 64) if half == 0 else (lane >= 64)
                hrow = (rowp < 64) if half == 0 else (rowp >= 64)
                ac = _pick_col(a_cs, lane, DT_LANE + h)
                ar = _pick_row(a_t, row, DT_LANE + h)
                dtc = _pick_col(dt, lane, DT_LANE + h)
                alast = jnp.sum(jnp.where(lane1 == l - 1, ar, 0.0), axis=1, keepdims=True)
                dh = jnp.sum(jnp.where(lane1 == DT_LANE + h, d_row, 0.0), axis=1, keepdims=True)
                xm = jnp.where(hm, xpair, 0.0)
                xd = xm * dtc
                dym = jnp.where(hm, dypair, 0.0)
                bm = xbc_ref[:, D_SSD + LANE * g:D_SSD + LANE * (g + 1)]
                cm = xbc_ref[:, D_SSD + SSD_BC + LANE * g:D_SSD + SSD_BC + LANE * (g + 1)]
                lm = jnp.where(row >= lane, jnp.exp(jnp.minimum(ac - ar, 0.0)), 0.0)
                e_in = jnp.exp(ac)
                f_out = jnp.exp(alast - ac)
                e_last = jnp.exp(alast)
                m = _dot_nt(cm, bm) * lm
                y_off = jnp.where(hm, _dot_nt(cm, sp), 0.0) * e_in
                dm = _dot_nt(dym, xd)
                dxd = _dot_tn(m, dym)
                dg = dm * lm
                dye = dym * e_in
                dc[g] = dc[g] + _dot(dg, bm) + _dot(dye, sp)
                db[g] = db[g] + _dot_tn(dg, cm)
                qm = dm * m
                dac = jnp.sum(qm, axis=1, keepdims=True) + jnp.sum(dym * y_off, axis=1, keepdims=True)
                dar = -jnp.sum(qm, axis=0, keepdims=True)
                dxf = jnp.where(hm, _dot_nt(bm, dsp), 0.0)
                db[g] = db[g] + _dot(xd * f_out, dsp)
                dxd = dxd + dxf * f_out
                df = jnp.sum(dxf * xd, axis=1, keepdims=True) * f_out
                dac = dac - df
                s_last = jnp.sum(df, axis=0, keepdims=True)
                ss = jnp.sum(jnp.where(hrow, dsp * sp, 0.0), axis=1, keepdims=True)
                s_last = s_last + e_last * jnp.sum(ss, axis=0, keepdims=True)
                dac = dac + jnp.where(rowl == l - 1, s_last, 0.0)
                ds_new = ds_new + _dot_tn(dye, cm)
                decay = jnp.where(hrow, e_last, decay)
                dxpair = dxpair + dxd * dtc + dym * dh
                dt_x = dt_x + jnp.where(lane == DT_LANE + h, jnp.sum(dxd * xm, axis=1, keepdims=True), 0.0)
                dsum = jnp.sum(jnp.sum(dym * xm, axis=1, keepdims=True), axis=0, keepdims=True)
                dd_row = dd_row + jnp.where(lane1 == DT_LANE + h, dsum, 0.0)
                da_col = da_col + jnp.where(lane == DT_LANE + h, dac, 0.0)
                da_row = da_row + jnp.where(row == DT_LANE + h, dar, 0.0)
            dstate[j] = dsp * decay + ds_new
            dxbc_ref[:, LANE * j:LANE * (j + 1)] = dxpair
        for g in range(2):
            dxbc_ref[:, D_SSD + LANE * g:D_SSD + LANE * (g + 1)] = db[g]
            dxbc_ref[:, D_SSD + SSD_BC + LANE * g:D_SSD + SSD_BC + LANE * (g + 1)] = dc[g]
        dla = _dot_hi_tn(tri, da_col + da_row.T)
        ddt = dt_x + dla * a_row
        dpre = ddt * _sigmoid(pre)
        dtm = (lane >= DT_LANE) & (lane < DT_LANE + SSD_HEADS)
        dtail_ref[...] = jnp.where(dtm, dpre, 0.0)
        dtm1 = (lane1 >= DT_LANE) & (lane1 < DT_LANE + SSD_HEADS)
        dsc_ref[0:1, :] += jnp.where(dtm1, jnp.sum(dpre, axis=0, keepdims=True), 0.0)
        dsc_ref[1:2, :] += jnp.where(dtm1, jnp.sum(dla * dt, axis=0, keepdims=True) * a_row, 0.0)
        dsc_ref[2:3, :] += dd_row

    rev = lambda c: nc - 1 - c
    return pl.pallas_call(
        body, name="ssd_bwd", grid=(nc,),
        in_specs=[pl.BlockSpec((l, N_XBC), lambda c: (rev(c), 0)),
                  pl.BlockSpec((l, LANE), lambda c: (rev(c), O_TAIL // LANE)), _full((8, LANE)),
                  pl.BlockSpec((1, 3, LANE, LANE), lambda c: (rev(c), 0, 0, 0)),
                  pl.BlockSpec((l, D_SSD), lambda c: (rev(c), 0))],
        out_specs=[pl.BlockSpec((l, N_XBC), lambda c: (rev(c), 0)), pl.BlockSpec((l, LANE), lambda c: (rev(c), 0)),
                   _full((8, LANE))],
        out_shape=[_sds((s, N_XBC)), _sds((s, LANE)), _sds((8, LANE))],
        scratch_shapes=[pltpu.VMEM((3, LANE, LANE), F32)],
        compiler_params=_params(("arbitrary",)),
    )(xbc, proj, sc, states, dy)


def _sconv_bwd(proj, w, b, dxbc):
    s = proj.shape[0]

    def body(u_ref, w_ref, b_ref, d_ref, du_ref, dw_ref, db_ref):
        u = u_ref[...]
        wv = w_ref[...]
        dpre = d_ref[...] * _dsilu(_sconv_pre(u, wv, b_ref[...]))
        du_ref[...] = (wv[3:4, :] * dpre + wv[2:3, :] * _shift_up(dpre, 1) + wv[1:2, :] * _shift_up(dpre, 2)
                       + wv[0:1, :] * _shift_up(dpre, 3))
        for k in range(4):
            dw_ref[k:k + 1, :] = jnp.sum(dpre * _shift_down(u, 3 - k), axis=0, keepdims=True)
        db_ref[...] = jnp.sum(dpre, axis=0, keepdims=True)

    blk = pl.BlockSpec((s, LANE), lambda j: (0, j))
    return pl.pallas_call(
        body, name="sconv_bwd", grid=(N_XBC // LANE,),
        in_specs=[_col(s, O_XBC), pl.BlockSpec((4, LANE), lambda j: (0, j)), pl.BlockSpec((1, LANE), lambda j: (0, j)), blk],
        out_specs=[blk, pl.BlockSpec((4, LANE), lambda j: (0, j)), pl.BlockSpec((1, LANE), lambda j: (0, j))],
        out_shape=[_sds((s, N_XBC)), _sds((4, N_XBC)), _sds((1, N_XBC))],
        compiler_params=_params(("parallel",)),
    )(proj, w, b, dxbc)


def _conva_bwd(proj, w, dya):
    s = proj.shape[0]

    def body(h_ref, b_ref, c_ref, z_ref, w_ref, d_ref, da_ref, dw_ref):
        ah, ab, acv, az = h_ref[...], b_ref[...], c_ref[...], z_ref[...]
        wv = w_ref[...]
        u = acv * ah
        cv = wv[2:3, :] * u + wv[1:2, :] * _shift_down(u, 1) + wv[0:1, :] * _shift_down(u, 2)
        dy = d_ref[...]
        sz = _silu(az)
        da_ref[1] = dy * cv * sz
        da_ref[3] = dy * ab * cv * _dsilu(az)
        dcv = dy * ab * sz
        du = wv[2:3, :] * dcv + wv[1:2, :] * _shift_up(dcv, 1) + wv[0:1, :] * _shift_up(dcv, 2)
        da_ref[0] = du * acv
        da_ref[2] = du * ah
        for k in range(3):
            dw_ref[k:k + 1, :] = jnp.sum(dcv * _shift_down(u, 2 - k), axis=0, keepdims=True)

    return pl.pallas_call(
        body, name="conva_bwd", grid=(D_CONV_A // LANE,),
        in_specs=[_col(s, O_AH), _col(s, O_AB), _col(s, O_AC), _col(s, O_AZ), pl.BlockSpec((3, LANE), lambda j: (0, j)),
                  pl.BlockSpec((s, LANE), lambda j: (0, j))],
        out_specs=[pl.BlockSpec((4, s, LANE), lambda j: (0, 0, j)), pl.BlockSpec((3, LANE), lambda j: (0, j))],
        out_shape=[_sds((4, s, D_CONV_A)), _sds((3, D_CONV_A))],
        compiler_params=_params(("parallel",)),
    )(proj, proj, proj, proj, w, dya)


def _mla_prep_bwd(dq, dk, dv, proj, qn, kvn, rq, rkv, gq, gkv, wq, wkv, cos, sin):
    s = proj.shape[0]
    ts = _tile(s)
    nh = MLA_HEADS

    def body(dq_ref, dk_ref, dv_ref, cqa_ref, ckv_ref, qn_ref, kvn_ref, rq_ref, rkv_ref, gq_ref, gkv_ref,
             wq_ref, wkv_ref, cos_ref, sin_ref, dcqa_ref, dckv_ref, dtail_ref, dwq_ref, dwkv_ref, dgq_ref, dgkv_ref):
        @pl.when(pl.program_id(0) == 0)
        def _():
            dwq_ref[...] = jnp.zeros_like(dwq_ref)
            dwkv_ref[...] = jnp.zeros_like(dwkv_ref)
            dgq_ref[...] = jnp.zeros_like(dgq_ref)
            dgkv_ref[...] = jnp.zeros_like(dgkv_ref)

        cosv = cos_ref[...]
        sinv = sin_ref[...]
        lane = _iota((ts, LANE), 1)
        rope_lanes = (lane >= ROPE_LANE) & (lane < ROPE_LANE + QK_ROPE)

        def unrope(gr):
            return gr * cosv + _rope_swap(gr * sinv)

        dqs, dks, dvs = [], [], []
        dkr = jnp.zeros((ts, LANE), F32)
        for h in range(nh):
            dqs.append(unrope(dq_ref[h] * ATT_SCALE).astype(MXU))
            dkh = dk_ref[h]
            dks.append(jnp.where(lane < QK_NOPE, dkh, 0.0).astype(MXU))
            dkr = dkr + jnp.where(rope_lanes, dkh, 0.0)
            dvs.append(dv_ref[h].astype(MXU))
        dtail_ref[...] = pltpu.roll(jnp.where(rope_lanes, unrope(dkr), 0.0), ROPE_LANE, 1)
        dq_all = jnp.concatenate(dqs, axis=1)
        dkv_all = jnp.concatenate(dks + dvs, axis=1)
        dwq_ref[...] += _dot_tn(dq_all, qn_ref[...])
        dwkv_ref[...] += _dot_tn(dkv_all, kvn_ref[...])
        dcqa, dgq = _rms_bwd(_dot(dq_all, wq_ref[...]), cqa_ref[...], rq_ref[...], gq_ref[...])
        dckv, dgkv = _rms_bwd(_dot(dkv_all, wkv_ref[...]), ckv_ref[...], rkv_ref[...], gkv_ref[...])
        dcqa_ref[...] = dcqa
        dckv_ref[...] = dckv
        dgq_ref[...] += dgq
        dgkv_ref[...] += dgkv

    head = pl.BlockSpec((nh, ts, LANE), lambda i: (0, i, 0))
    return pl.pallas_call(
        body, name="mla_prep_bwd", grid=(s // ts,),
        in_specs=[head, head, head,
                  pl.BlockSpec((ts, Q_LORA), lambda i: (i, O_CQA // Q_LORA)),
                  pl.BlockSpec((ts, KV_LORA), lambda i: (i, O_CKV // KV_LORA)),
                  _row(ts, Q_LORA), _row(ts, KV_LORA), _row(ts, 1), _row(ts, 1),
                  _full((1, Q_LORA)), _full((1, KV_LORA)), _full((nh * LANE, Q_LORA)), _full((2 * nh * LANE, KV_LORA)),
                  _row(ts, LANE), _row(ts, LANE)],
        out_specs=[_row(ts, Q_LORA), _row(ts, KV_LORA), _row(ts, LANE), _full((nh * LANE, Q_LORA)),
                   _full((2 * nh * LANE, KV_LORA)), _full((1, Q_LORA)), _full((1, KV_LORA))],
        out_shape=[_sds((s, Q_LORA)), _sds((s, KV_LORA)), _sds((s, LANE)), _sds((nh * LANE, Q_LORA)),
                   _sds((2 * nh * LANE, KV_LORA)), _sds((1, Q_LORA)), _sds((1, KV_LORA))],
        compiler_params=_params(("arbitrary",)),
    )(dq, dk, dv, proj, proj, qn, kvn, rq, rkv, gq, gkv, wq, wkv, cos, sin)


def _inproj_bwd(da4, dsz, dxbc_in, dcqa, dckv, dcz, dtail_a, dtail_b, w, x, rstd, g, dout):
    s = x.shape[0]
    ts = _tile(s)

    def body(da_ref, dsz_ref, dxbc_ref, dcqa_ref, dckv_ref, dcz_ref, dta_ref, dtb_ref, w_ref, x_ref, r_ref, g_ref, dout_ref,
             dproj_ref, dx_ref, dg_ref):
        @pl.when(pl.program_id(0) == 0)
        def _():
            dg_ref[...] = jnp.zeros_like(dg_ref)

        dproj = jnp.concatenate(
            [da_ref[0], da_ref[1], da_ref[2], da_ref[3], dsz_ref[...], dxbc_ref[...], dcqa_ref[...], dckv_ref[...],
             dcz_ref[...], dta_ref[...] + dtb_ref[...]], axis=1).astype(MXU)
        dproj_ref[...] = dproj
        dh = _dot_nt(dproj, w_ref[...])
        dx, dg = _rms_bwd(dh, x_ref[...], r_ref[...], g_ref[...])
        dx_ref[...] = dout_ref[...] + dx
        dg_ref[...] += dg

    return pl.pallas_call(
        body, name="inproj_bwd", grid=(s // ts,),
        in_specs=[pl.BlockSpec((4, ts, D_CONV_A), lambda i: (0, i, 0)), _row(ts, D_SSD), _row(ts, N_XBC), _row(ts, Q_LORA),
                  _row(ts, KV_LORA), _row(ts, D_MLA), _row(ts, LANE), _row(ts, LANE), _full((D_MODEL, NCOL)),
                  _row(ts, D_MODEL), _row(ts, 1), _full((1, D_MODEL)), _row(ts, D_MODEL)],
        out_specs=[_row(ts, NCOL), _row(ts, D_MODEL), _full((1, D_MODEL))],
        out_shape=[_sds((s, NCOL), MXU), _sds((s, D_MODEL)), _sds((1, D_MODEL))],
        compiler_params=_params(("arbitrary",)),
    )(da4, dsz, dxbc_in, dcqa, dckv, dcz, dtail_a, dtail_b, w, x, rstd, g, dout)


DWIN_BLOCK = 640


def _dwin(h, dproj, layer, stack):
    s = h.shape[0]

    def body(h_ref, d_ref, *rest):
        rest[-1][0] = _dot_tn(h_ref[...], d_ref[...])

    extra = [] if stack is None else [stack]
    return pl.pallas_call(
        body, name="dwin", grid=(NCOL // DWIN_BLOCK,),
        in_specs=[_full((s, D_MODEL)), pl.BlockSpec((s, DWIN_BLOCK), lambda j: (0, j))] + [ANY] * len(extra),
        out_specs=pl.BlockSpec((1, D_MODEL, DWIN_BLOCK), lambda j: (layer, 0, j)),
        out_shape=_sds((DEPTH, D_MODEL, NCOL)),
        input_output_aliases={2: 0} if extra else {},
        compiler_params=_params(("parallel",)),
    )(h, dproj, *extra)


def _adamw(w, g, m, v):
    r, c = w.shape
    tr = r
    for cand in (256, 128, 64, 32, 16, 8):
        if r % cand == 0:
            tr = cand
            break
    bc1 = 1.0 - ADAM_B1 ** ADAM_STEP
    bc2 = 1.0 - ADAM_B2 ** ADAM_STEP

    def body(w_ref, g_ref, m_ref, v_ref, d_ref, mo_ref, vo_ref):
        gv = g_ref[...]
        mn = ADAM_B1 * m_ref[...] + (1.0 - ADAM_B1) * gv
        vn = ADAM_B2 * v_ref[...] + (1.0 - ADAM_B2) * (gv * gv)
        mo_ref[...] = mn
        vo_ref[...] = vn
        d_ref[...] = -ADAM_LR * ((mn / bc1) / (jnp.sqrt(vn / bc2) + ADAM_EPS) + ADAM_WD * w_ref[...])

    blk = pl.BlockSpec((tr, c), lambda i: (i, 0))
    return pl.pallas_call(
        body, name="adamw", grid=(r // tr,),
        in_specs=[blk] * 4, out_specs=[blk] * 3, out_shape=[_sds((r, c))] * 3,
        compiler_params=_params(("parallel",)),
    )(w, g, m, v)


def _perm_cols(w):
    pad = jnp.zeros(w.shape[:-1] + (NCOL - IN_COLS,), w.dtype)
    return jnp.concatenate([w[..., :2304], w[..., 2310:2566], w[..., 2566:2694], w[..., 2726:3110],
                            w[..., 2694:2726], w[..., 2304:2310], pad], axis=-1)


def _unperm_cols(g):
    return jnp.concatenate([g[..., :2304], g[..., 3104:3110], g[..., 2304:2560], g[..., 2560:2688],
                            g[..., 3072:3104], g[..., 2688:3072]], axis=-1)


def _wq_layout(wt):
    return jnp.pad(wt.reshape(MLA_HEADS, QK_NOPE + QK_ROPE, Q_LORA), ((0, 0), (0, 32), (0, 0))).reshape(MLA_HEADS * LANE, Q_LORA)


def _wq_unlayout(g):
    return g.reshape(MLA_HEADS, LANE, Q_LORA)[:, :QK_NOPE + QK_ROPE].reshape(MLA_HEADS * (QK_NOPE + QK_ROPE), Q_LORA)


def _wkv_layout(wt):
    t = wt.reshape(MLA_HEADS, 2, 64, KV_LORA).transpose(1, 0, 2, 3)
    return jnp.pad(t, ((0, 0), (0, 0), (0, 64), (0, 0))).reshape(2 * MLA_HEADS * LANE, KV_LORA)


def _wkv_unlayout(g):
    t = g.reshape(2, MLA_HEADS, LANE, KV_LORA)[:, :, :64]
    return t.transpose(1, 0, 2, 3).reshape(MLA_HEADS * LANE, KV_LORA)


def _rope_tables(positions):
    inv_freq = ROPE_BASE ** (-jnp.arange(0, QK_ROPE, 2, dtype=F32) / QK_ROPE)
    ang = positions.astype(F32)[:, None] * inv_freq
    cos, sin = jnp.cos(ang), jnp.sin(ang)
    s = positions.shape[0]
    one, zero = jnp.ones((s, ROPE_LANE), F32), jnp.zeros((s, ROPE_LANE), F32)
    cos_t = jnp.concatenate([one, cos, cos, one[:, :32]], axis=1)
    sin_t = jnp.concatenate([zero, -sin, sin, zero[:, :32]], axis=1)
    return cos_t, sin_t


def _ssd_scalars(dt_bias, a_log, d_skip):
    return jnp.pad(jnp.stack([dt_bias, a_log, d_skip]), ((0, 5), (DT_LANE, LANE - DT_LANE - SSD_HEADS)))


def _layer_fwd(x, lw, cos, sin):
    proj, h, rstd = _inproj_fwd(x, lw["norm_g"], lw["w_in"])
    ya = _conva_fwd(proj, lw["conv_a_w"])
    xbc = _sconv_fwd(proj, lw["ssd_conv_w"], lw["ssd_conv_b"])
    y_ssd, states = _ssd_fwd(xbc, proj, lw["sc"])
    q, k, v, qn, kvn, rq, rkv = _mla_prep_fwd(proj, lw["gq"], lw["gkv"], lw["wq"], lw["wkv"], cos, sin)
    o, lse = _attn_fwd(q, k, v)
    x_out, y = _outproj_fwd(x, proj, ya, y_ssd, o, lw["g_ssd"], lw["w_out"])
    saved = dict(x=x, proj=proj, h=h, rstd=rstd, xbc=xbc, y_ssd=y_ssd, states=states, q=q, k=k, v=v, qn=qn, kvn=kvn,
                 rq=rq, rkv=rkv, o=o, lse=lse, y=y)
    return x_out, saved


def _layer_bwd(dout, lw, sv, cos, sin, layer, stacks):
    dya, dys, dsz, d_o, dcz, dg_ssd, dw_out = _outproj_bwd(dout, sv["y"], lw["w_out"], sv["proj"], sv["y_ssd"], sv["o"],
                                                            lw["g_ssd"], layer, stacks and stacks[1])
    dq, dk, dv = _attn_bwd(sv["q"], sv["k"], sv["v"], sv["o"], d_o, sv["lse"])
    dxbc, dtail_s, dsc = _ssd_bwd(sv["xbc"], sv["proj"], lw["sc"], sv["states"], dys)
    du, dw_sconv, db_sconv = _sconv_bwd(sv["proj"], lw["ssd_conv_w"], lw["ssd_conv_b"], dxbc)
    da4, dw_conva = _conva_bwd(sv["proj"], lw["conv_a_w"], dya)
    dcqa, dckv, dtail_m, dwq, dwkv, dgq, dgkv = _mla_prep_bwd(
        dq, dk, dv, sv["proj"], sv["qn"], sv["kvn"], sv["rq"], sv["rkv"], lw["gq"], lw["gkv"], lw["wq"], lw["wkv"], cos, sin)
    dproj, dx, dg = _inproj_bwd(da4, dsz, du, dcqa, dckv, dcz, dtail_s, dtail_m, lw["w_in"], sv["x"], sv["rstd"],
                                lw["norm_g"], dout)
    dw_in = _dwin(sv["h"], dproj, layer, stacks and stacks[0])
    grads = dict(norm_g=dg, conv_a_w=dw_conva, ssd_conv_w=dw_sconv, ssd_conv_b=db_sconv, sc=dsc,
                 g_ssd=dg_ssd, gq=dgq, wq=dwq, gkv=dgkv, wkv=dwkv)
    return dx, grads, (dw_in, dw_out)


def _local_step(x, positions, tgt, layers, final_g):
    cos, sin = _rope_tables(positions)
    saved = []
    for lw in layers:
        x, sv = _layer_fwd(x, lw, cos, sin)
        saved.append(sv)
    dx, dgf, loss = _loss_head(x, final_g, tgt)
    grads = [None] * len(layers)
    stacks = None
    for li in reversed(range(len(layers))):
        dx, grads[li], stacks = _layer_bwd(dx, layers[li], saved[li], cos, sin, li, stacks)
    return loss, dx, grads, dgf, stacks


ANY = pl.BlockSpec(memory_space=pl.ANY)
N_CHIPS = 4
N_DEV = 8


def _place():
    return lax.axis_index("x"), lax.axis_index("y"), lax.axis_index("c")


def _gather_weights(arrs):
    n = len(arrs)

    def body(*refs):
        ins, outs = refs[:n], refs[n:2 * n]
        send_sems, recv_sems, loc_sems = refs[2 * n:]
        x, y, c = _place()
        me = 2 * x + y
        sib = (x, y, 1 - c)
        chips = [(1 - x, y), (x, 1 - y), (1 - x, 1 - y)]

        def rcopy(a, k, src, layer, chip, to):
            return pltpu.make_async_remote_copy(
                src_ref=src, dst_ref=outs[a].at[layer, chip], send_sem=send_sems.at[k * n + a],
                recv_sem=recv_sems.at[k * n + a], device_id=to, device_id_type=MESH_T)

        local = []
        for a in range(n):
            for layer in range(2):
                cp = pltpu.make_async_copy(ins[a].at[layer], outs[a].at[layer, me], loc_sems.at[2 * a + layer])
                cp.start()
                local.append(cp)
        sent = []
        for j, (cx, cy) in enumerate(chips):
            for a in range(n):
                cp = rcopy(a, j, ins[a].at[c], c, me, (cx, cy, c))
                cp.start()
                sent.append(cp)
        for j, (cx, cy) in enumerate(chips):
            src = 2 * cx + cy
            for a in range(n):
                rcopy(a, j, outs[a].at[c, src], c, src, (x, y, c)).wait_recv()
                cp = rcopy(a, 3 + j, outs[a].at[c, src], c, src, sib)
                cp.start()
                sent.append(cp)
        for j, (cx, cy) in enumerate(chips):
            src = 2 * cx + cy
            for a in range(n):
                rcopy(a, 3 + j, outs[a].at[1 - c, src], 1 - c, src, (x, y, c)).wait_recv()
        for cp in sent:
            cp.wait_send()
        for cp in local:
            cp.wait()

    return pl.pallas_call(
        body, name="gather_weights",
        in_specs=[ANY] * n, out_specs=[ANY] * n,
        out_shape=[_sds((2, N_CHIPS) + a.shape[1:], a.dtype) for a in arrs],
        scratch_shapes=[pltpu.SemaphoreType.DMA((6 * n,)), pltpu.SemaphoreType.DMA((6 * n,)), pltpu.SemaphoreType.DMA((2 * n,))],
    )(*arrs)


def _swap_layers(gs):
    n = len(gs)

    def body(*refs):
        ins, outs = refs[:n], refs[n:2 * n]
        send_sems, recv_sems = refs[2 * n:]
        x, y, c = _place()
        cps = []
        for a in range(n):
            cp = pltpu.make_async_remote_copy(src_ref=ins[a].at[1 - c], dst_ref=outs[a], send_sem=send_sems.at[a],
                                              recv_sem=recv_sems.at[a], device_id=(x, y, 1 - c), device_id_type=MESH_T)
            cp.start()
            cps.append(cp)
        for cp in cps:
            cp.wait()

    return pl.pallas_call(
        body, name="swap_layers", in_specs=[ANY] * n, out_specs=[ANY] * n,
        out_shape=[_sds(g.shape[1:]) for g in gs],
        scratch_shapes=[pltpu.SemaphoreType.DMA((n,)), pltpu.SemaphoreType.DMA((n,))],
    )(*gs)


def _exchange_chips(ps):
    n = len(ps)

    def body(*refs):
        ins, outs = refs[:n], refs[n:2 * n]
        send_sems, recv_sems, loc_sems = refs[2 * n:]
        x, y, c = _place()
        me = 2 * x + y
        chips = [(1 - x, y), (x, 1 - y), (1 - x, 1 - y)]
        local, sent = [], []
        for a in range(n):
            cp = pltpu.make_async_copy(ins[a].at[me], outs[a].at[me], loc_sems.at[a])
            cp.start()
            local.append(cp)

        def rcopy(a, j, chip_block, slot, to):
            return pltpu.make_async_remote_copy(
                src_ref=ins[a].at[chip_block], dst_ref=outs[a].at[slot], send_sem=send_sems.at[j * n + a],
                recv_sem=recv_sems.at[j * n + a], device_id=to, device_id_type=MESH_T)

        for j, (cx, cy) in enumerate(chips):
            for a in range(n):
                cp = rcopy(a, j, 2 * cx + cy, me, (cx, cy, c))
                cp.start()
                sent.append(cp)
        for j, (cx, cy) in enumerate(chips):
            for a in range(n):
                rcopy(a, j, me, 2 * cx + cy, (x, y, c)).wait_recv()
        for cp in sent:
            cp.wait_send()
        for cp in local:
            cp.wait()

    return pl.pallas_call(
        body, name="exchange_chips", in_specs=[ANY] * n, out_specs=[ANY] * n,
        out_shape=[_sds(p.shape) for p in ps],
        scratch_shapes=[pltpu.SemaphoreType.DMA((3 * n,)), pltpu.SemaphoreType.DMA((3 * n,)), pltpu.SemaphoreType.DMA((n,))],
    )(*ps)


def _share_sibling(fs):
    n = len(fs)

    def body(*refs):
        ins, outs = refs[:n], refs[n:2 * n]
        send_sems, recv_sems, loc_sems = refs[2 * n:]
        x, y, c = _place()
        local, sent = [], []
        for a in range(n):
            cp = pltpu.make_async_copy(ins[a], outs[a].at[c], loc_sems.at[a])
            cp.start()
            local.append(cp)
            cp = pltpu.make_async_remote_copy(src_ref=ins[a], dst_ref=outs[a].at[c], send_sem=send_sems.at[a],
                                              recv_sem=recv_sems.at[a], device_id=(x, y, 1 - c), device_id_type=MESH_T)
            cp.start()
            sent.append(cp)
        for a in range(n):
            pltpu.make_async_remote_copy(src_ref=ins[a], dst_ref=outs[a].at[1 - c], send_sem=send_sems.at[a],
                                         recv_sem=recv_sems.at[a], device_id=(x, y, c), device_id_type=MESH_T).wait_recv()
        for cp in sent:
            cp.wait_send()
        for cp in local:
            cp.wait()

    return pl.pallas_call(
        body, name="share_sibling", in_specs=[ANY] * n, out_specs=[ANY] * n,
        out_shape=[_sds((2,) + f.shape) for f in fs],
        scratch_shapes=[pltpu.SemaphoreType.DMA((n,)), pltpu.SemaphoreType.DMA((n,)), pltpu.SemaphoreType.DMA((n,))],
    )(*fs)


def _allreduce_small(slab):
    r = slab.shape[0]

    def body(s_ref, o_ref, gath, send_sems, recv_sems):
        x, y, c = _place()
        me = 4 * x + 2 * y + c
        gath[me] = s_ref[...]
        cps = []
        for rel in range(1, N_DEV):
            px = 1 - x if rel & 4 else x
            py = 1 - y if rel & 2 else y
            pc = 1 - c if rel & 1 else c
            cp = pltpu.make_async_remote_copy(src_ref=s_ref, dst_ref=gath.at[me], send_sem=send_sems.at[rel - 1],
                                              recv_sem=recv_sems.at[rel - 1], device_id=(px, py, pc), device_id_type=MESH_T)
            cp.start()
            cps.append(cp)
        for cp in cps:
            cp.wait()
        acc = gath[0]
        for d in range(1, N_DEV):
            acc = acc + gath[d]
        o_ref[...] = acc

    vm = pl.BlockSpec(memory_space=pltpu.VMEM)
    return pl.pallas_call(
        body, name="allreduce_small", in_specs=[vm], out_specs=vm, out_shape=_sds((r, LANE)),
        scratch_shapes=[pltpu.VMEM((N_DEV, r, LANE), F32), pltpu.SemaphoreType.DMA((N_DEV - 1,)),
                        pltpu.SemaphoreType.DMA((N_DEV - 1,))],
    )(slab)


def _add_mine(g, recv, layer):
    _, m, c = g.shape
    tm = 256 if m % 256 == 0 else (192 if m % 192 == 0 else 144)

    def body(l_ref, g_ref, r_ref, o_ref):
        o_ref[...] = g_ref[0] + r_ref[...]

    return pl.pallas_call(
        body, name="add_mine",
        grid_spec=pltpu.PrefetchScalarGridSpec(
            num_scalar_prefetch=1, grid=(m // tm,),
            in_specs=[pl.BlockSpec((1, tm, c), lambda i, l: (l[0], i, 0)), pl.BlockSpec((tm, c), lambda i, l: (i, 0))],
            out_specs=pl.BlockSpec((tm, c), lambda i, l: (i, 0))),
        out_shape=_sds((m, c)),
        compiler_params=_params(("parallel",)),
    )(layer, g, recv)


def _add_chips(p):
    _, r, c = p.shape
    tr = 128 if r % 128 == 0 else r // 2 if (r // 2) % 8 == 0 else r

    def body(p_ref, o_ref):
        o_ref[...] = ((p_ref[0] + p_ref[1]) + p_ref[2]) + p_ref[3]

    return pl.pallas_call(
        body, name="add_chips", grid=(r // tr,),
        in_specs=[pl.BlockSpec((N_CHIPS, tr, c), lambda i: (0, i, 0))], out_specs=pl.BlockSpec((tr, c), lambda i: (i, 0)),
        out_shape=_sds((r, c)),
        compiler_params=_params(("parallel",)),
    )(p)


def _reduce_scatter(gs):
    c = lax.axis_index("c")
    layer = jnp.reshape(c, (1,)).astype(jnp.int32)
    recv = _swap_layers(gs)
    ps = [_add_mine(g, rv, layer) for g, rv in zip(gs, recv)]
    ps = [p.reshape(N_CHIPS, p.shape[0] // N_CHIPS, p.shape[1]) for p in ps]
    got = _exchange_chips(ps)
    fs = [_add_chips(p) for p in got]
    return _share_sibling(fs)


WEIGHTS = ["norm_g", "w_in", "conv_a_w", "ssd_conv_w", "ssd_conv_b", "ssd_dt_bias", "ssd_a_log", "ssd_d", "ssd_norm_g",
           "mla_q_norm_g", "w_qb", "mla_kv_norm_g", "w_kvb", "w_out", "final_norm_g"]
BIG = ["w_in", "w_qb", "w_kvb", "w_out"]
SLAB_ROWS = 128
SMALL_ROWS = 72


def _to_slab(parts, rows):
    flat = jnp.concatenate([p.reshape(-1) for p in parts])
    return jnp.pad(flat, (0, rows * LANE - flat.shape[0])).reshape(rows, LANE)


def _from_slab(slab, shapes):
    flat = slab.reshape(-1)
    out, off = [], 0
    for shp in shapes:
        n = int(np.prod(shp))
        out.append(flat[off:off + n].reshape(shp))
        off += n
    return out


def kernel(x, positions, norm_g, w_in, conv_a_w, ssd_conv_w, ssd_conv_b, ssd_dt_bias, ssd_a_log, ssd_d, ssd_norm_g, mla_q_norm_g, w_qb, mla_kv_norm_g, w_kvb, w_out, final_norm_g, loss_target, m_norm_g, m_w_in, m_conv_a_w, m_ssd_conv_w, m_ssd_conv_b, m_ssd_dt_bias, m_ssd_a_log, m_ssd_d, m_ssd_norm_g, m_mla_q_norm_g, m_w_qb, m_mla_kv_norm_g, m_w_kvb, m_w_out, m_final_norm_g, v_norm_g, v_w_in, v_conv_a_w, v_ssd_conv_w, v_ssd_conv_b, v_ssd_dt_bias, v_ssd_a_log, v_ssd_d, v_ssd_norm_g, v_mla_q_norm_g, v_w_qb, v_mla_kv_norm_g, v_w_kvb, v_w_out, v_final_norm_g):
    w = dict(norm_g=norm_g, w_in=w_in, conv_a_w=conv_a_w, ssd_conv_w=ssd_conv_w, ssd_conv_b=ssd_conv_b,
             ssd_dt_bias=ssd_dt_bias, ssd_a_log=ssd_a_log, ssd_d=ssd_d, ssd_norm_g=ssd_norm_g, mla_q_norm_g=mla_q_norm_g,
             w_qb=w_qb, mla_kv_norm_g=mla_kv_norm_g, w_kvb=w_kvb, w_out=w_out, final_norm_g=final_norm_g)
    mom = dict(norm_g=m_norm_g, w_in=m_w_in, conv_a_w=m_conv_a_w, ssd_conv_w=m_ssd_conv_w, ssd_conv_b=m_ssd_conv_b,
               ssd_dt_bias=m_ssd_dt_bias, ssd_a_log=m_ssd_a_log, ssd_d=m_ssd_d, ssd_norm_g=m_ssd_norm_g,
               mla_q_norm_g=m_mla_q_norm_g, w_qb=m_w_qb, mla_kv_norm_g=m_mla_kv_norm_g, w_kvb=m_w_kvb, w_out=m_w_out,
               final_norm_g=m_final_norm_g)
    var = dict(norm_g=v_norm_g, w_in=v_w_in, conv_a_w=v_conv_a_w, ssd_conv_w=v_ssd_conv_w, ssd_conv_b=v_ssd_conv_b,
               ssd_dt_bias=v_ssd_dt_bias, ssd_a_log=v_ssd_a_log, ssd_d=v_ssd_d, ssd_norm_g=v_ssd_norm_g,
               mla_q_norm_g=v_mla_q_norm_g, w_qb=v_w_qb, mla_kv_norm_g=v_mla_kv_norm_g, w_kvb=v_w_kvb, w_out=v_w_out,
               final_norm_g=v_final_norm_g)
    chip = 2 * lax.axis_index("x") + lax.axis_index("y")

    conv_pack = jnp.zeros((DEPTH, 8, 256), F32)
    conv_pack = conv_pack.at[:, 0:3, 0:64].set(conv_a_w).at[:, 3:7, 0:224].set(ssd_conv_w)
    g_in, g_out, g_qb, g_kvb, g_conv = _gather_weights([
        _perm_cols(w_in).astype(MXU), w_out.astype(MXU), jnp.swapaxes(w_qb, 1, 2).astype(MXU),
        jnp.swapaxes(w_kvb, 1, 2).astype(MXU), conv_pack])
    conv_a_full = g_conv[:, :, 0:3, 0:64].transpose(0, 2, 1, 3).reshape(DEPTH, 3, D_CONV_A)
    sconv_full = g_conv[:, :, 3:7, 0:224].transpose(0, 2, 1, 3).reshape(DEPTH, 4, N_XBC)
    layers = []
    for l in range(DEPTH):
        layers.append(dict(
            norm_g=norm_g[l][None], w_in=g_in[l].reshape(D_MODEL, NCOL), conv_a_w=conv_a_full[l], ssd_conv_w=sconv_full[l],
            ssd_conv_b=ssd_conv_b[l][None], sc=_ssd_scalars(ssd_dt_bias[l], ssd_a_log[l], ssd_d[l]),
            g_ssd=ssd_norm_g[l][None], gq=mla_q_norm_g[l][None], gkv=mla_kv_norm_g[l][None],
            wq=_wq_layout(g_qb[l].reshape(MLA_HEADS * 96, Q_LORA)), wkv=_wkv_layout(g_kvb[l].reshape(MLA_HEADS * LANE, KV_LORA)),
            w_out=g_out[l].reshape(D_MODEL, D_MODEL)))

    loss, grad_x, lg, dgf, (dw_in, dw_out) = _local_step(x[0], positions[0], loss_target[0], layers, final_norm_g[None])

    r_in, r_out, r_qb, r_kvb = _reduce_scatter([
        dw_in, dw_out,
        jnp.stack([_wq_unlayout(lg[l]["wq"]) for l in range(DEPTH)]),
        jnp.stack([_wkv_unlayout(lg[l]["wkv"]) for l in range(DEPTH)])])
    grad = dict(w_in=_unperm_cols(r_in), w_out=r_out, w_qb=jnp.swapaxes(r_qb, 1, 2), w_kvb=jnp.swapaxes(r_kvb, 1, 2))

    small_names = ["norm_g", "conv_a_w", "ssd_conv_w", "ssd_conv_b", "sc", "g_ssd", "gq", "gkv"]
    parts = [loss[0, 0:1], dgf]
    for l in range(DEPTH):
        parts += [lg[l][nm][:3, DT_LANE:DT_LANE + SSD_HEADS] if nm == "sc" else lg[l][nm] for nm in small_names]
    shapes = [(1,), (D_MODEL,)] + [(D_MODEL,), (3, D_CONV_A), (4, N_XBC), (N_XBC,), (3, SSD_HEADS), (D_SSD,), (Q_LORA,), (KV_LORA,)] * DEPTH
    red = _from_slab(_allreduce_small(_to_slab(parts, SLAB_ROWS)), shapes)
    loss_out = red[0][0]
    grad["final_norm_g"] = red[1]
    per = [red[2 + 8 * l:10 + 8 * l] for l in range(DEPTH)]
    grad["norm_g"] = jnp.stack([per[l][0] for l in range(DEPTH)])
    grad["conv_a_w"] = lax.dynamic_slice_in_dim(jnp.stack([per[l][1] for l in range(DEPTH)]), chip * 64, 64, axis=2)
    grad["ssd_conv_w"] = lax.dynamic_slice_in_dim(jnp.stack([per[l][2] for l in range(DEPTH)]), chip * 224, 224, axis=2)
    grad["ssd_conv_b"] = jnp.stack([per[l][3] for l in range(DEPTH)])
    grad["ssd_dt_bias"] = jnp.stack([per[l][4][0] for l in range(DEPTH)])
    grad["ssd_a_log"] = jnp.stack([per[l][4][1] for l in range(DEPTH)])
    grad["ssd_d"] = jnp.stack([per[l][4][2] for l in range(DEPTH)])
    grad["ssd_norm_g"] = jnp.stack([per[l][5] for l in range(DEPTH)])
    grad["mla_q_norm_g"] = jnp.stack([per[l][6] for l in range(DEPTH)])
    grad["mla_kv_norm_g"] = jnp.stack([per[l][7] for l in range(DEPTH)])

    delta, new_m, new_v = {}, {}, {}
    for nm in BIG:
        shp = w[nm].shape
        two_d = (shp[0] * shp[1], shp[2])
        d, mo, vo = _adamw(w[nm].reshape(two_d), grad[nm].reshape(two_d), mom[nm].reshape(two_d), var[nm].reshape(two_d))
        delta[nm], new_m[nm], new_v[nm] = d.reshape(shp), mo.reshape(shp), vo.reshape(shp)
    small = [nm for nm in WEIGHTS if nm not in BIG]
    sshapes = [w[nm].shape for nm in small]
    d, mo, vo = _adamw(_to_slab([w[nm] for nm in small], SMALL_ROWS), _to_slab([grad[nm] for nm in small], SMALL_ROWS),
                       _to_slab([mom[nm] for nm in small], SMALL_ROWS), _to_slab([var[nm] for nm in small], SMALL_ROWS))
    for nm, dv, mv, vv in zip(small, _from_slab(d, sshapes), _from_slab(mo, sshapes), _from_slab(vo, sshapes)):
        delta[nm], new_m[nm], new_v[nm] = dv, mv, vv

    return (loss_out, grad_x[None], *[grad[nm] for nm in WEIGHTS], *[delta[nm] for nm in WEIGHTS],
            *[new_m[nm] for nm in WEIGHTS], *[new_v[nm] for nm in WEIGHTS])
```

```python
import functools
import math

import numpy as np
import jax
import jax.numpy as jnp
from jax import lax
from jax.experimental import pallas as pl
from jax.experimental.pallas import tpu as pltpu

F32 = jnp.float32
MXU = jnp.bfloat16

D_MODEL = 1024
DEPTH = 2
D_CONV_A = 256
D_SSD = 384
SSD_HEADS = 6
SSD_BC = 256
SSD_CHUNK = 128
SSD_NORM_EPS = 1e-5
MLA_HEADS = 6
Q_LORA = 256
KV_LORA = 128
QK_NOPE = 64
QK_ROPE = 32
V_DIM = 64
D_MLA = 384
ROPE_BASE = 10000.0
NORM_EPS = 1e-6
IN_COLS = 3110
LANE = 128

O_AH, O_AB, O_AC, O_AZ = 0, 256, 512, 768
O_SZ = 1024
O_XBC = 1408
O_CQA = 2304
O_CKV = 2560
O_CZ = 2688
O_TAIL = 3072
NCOL = 3200
N_XBC = D_SSD + 2 * SSD_BC
DT_LANE = 32
ROPE_LANE = 64

ADAM_LR, ADAM_B1, ADAM_B2, ADAM_EPS, ADAM_WD, ADAM_STEP = 0.001, 0.9, 0.999, 1e-08, 0.01, 10

VMEM_LIMIT = 56 * 1024 * 1024
MESH_T = pl.DeviceIdType.MESH


def _dot(a, b):
    return jnp.dot(a.astype(MXU), b.astype(MXU), preferred_element_type=F32)


def _dot_nt(a, b):
    return lax.dot_general(a.astype(MXU), b.astype(MXU), (((1,), (1,)), ((), ())), preferred_element_type=F32)


def _dot_tn(a, b):
    return lax.dot_general(a.astype(MXU), b.astype(MXU), (((0,), (0,)), ((), ())), preferred_element_type=F32)


def _dot_hi(a, b):
    return jnp.dot(a, b, precision=lax.Precision.HIGHEST, preferred_element_type=F32)


def _dot_hi_tn(a, b):
    return lax.dot_general(a, b, (((0,), (0,)), ((), ())), precision=lax.Precision.HIGHEST, preferred_element_type=F32)


def _sigmoid(z):
    return 1.0 / (1.0 + jnp.exp(-z))


def _silu(z):
    return z * _sigmoid(z)


def _dsilu(z):
    s = _sigmoid(z)
    return s * (1.0 + z * (1.0 - s))


def _softplus(z):
    e = jnp.exp(-jnp.abs(z))
    return jnp.maximum(z, 0.0) + jnp.where(e < 1e-3, e * (1.0 - 0.5 * e), jnp.log(1.0 + e))


def _iota(shape, dim):
    return lax.broadcasted_iota(jnp.int32, shape, dim)


def _shift_down(u, k):
    if k == 0:
        return u
    return jnp.where(_iota(u.shape, 0) >= k, pltpu.roll(u, k, 0), 0.0)


def _shift_up(u, k):
    if k == 0:
        return u
    n = u.shape[0]
    return jnp.where(_iota(u.shape, 0) < n - k, pltpu.roll(u, n - k, 0), 0.0)


def _rope_swap(t):
    lane = _iota(t.shape, 1)
    lo = (lane >= ROPE_LANE) & (lane < ROPE_LANE + 16)
    hi = (lane >= ROPE_LANE + 16) & (lane < ROPE_LANE + 32)
    return jnp.where(lo, pltpu.roll(t, LANE - 16, 1), jnp.where(hi, pltpu.roll(t, 16, 1), 0.0))


def _params(sem=None):
    return pltpu.CompilerParams(dimension_semantics=sem, vmem_limit_bytes=VMEM_LIMIT)


def _full(shape):
    nd = len(shape)
    return pl.BlockSpec(shape, lambda *_: (0,) * nd)


def _sds(shape, dtype=F32):
    return jax.ShapeDtypeStruct(shape, dtype)


def _tile(s):
    return min(256, s)


def _row(ts, w):
    return pl.BlockSpec((ts, w), lambda i: (i, 0))


def _col(s, off):
    return pl.BlockSpec((s, LANE), lambda j, _o=off // LANE: (0, _o + j))


def _call_after(dep, body, args, *, in_specs, **kw):
    if dep is None:
        return pl.pallas_call(body, in_specs=in_specs, **kw)(*args)
    n = len(args)

    def body_dep(*refs):
        body(*refs[:n], *refs[n + 1:])

    return pl.pallas_call(body_dep, in_specs=list(in_specs) + [pl.BlockSpec(memory_space=pl.ANY)], **kw)(*args, dep)


def _rms(c, g):
    r = lax.rsqrt(jnp.mean(c * c, axis=-1, keepdims=True) + NORM_EPS)
    return c * r * g, r


def _rms_bwd(dn, c, r, g):
    ch = c * r
    dch = dn * g
    dc = r * (dch - ch * jnp.mean(dch * ch, axis=-1, keepdims=True))
    return dc, jnp.sum(dn * ch, axis=0, keepdims=True)


def _inproj_fwd(x, g, w, dep=None):
    s = x.shape[0]
    ts = _tile(s)

    def body(x_ref, g_ref, w_ref, proj_ref, h_ref, r_ref):
        hn, r = _rms(x_ref[...], g_ref[...])
        h = hn.astype(MXU)
        h_ref[...] = h
        r_ref[...] = r
        proj_ref[...] = jnp.dot(h, w_ref[...], preferred_element_type=F32)

    return _call_after(
        dep, body, (x, g, w), name="inproj_fwd", grid=(s // ts,),
        in_specs=[_row(ts, D_MODEL), _full((1, D_MODEL)), _full((D_MODEL, NCOL))],
        out_specs=[_row(ts, NCOL), _row(ts, D_MODEL), _row(ts, 1)],
        out_shape=[_sds((s, NCOL)), _sds((s, D_MODEL), MXU), _sds((s, 1))],
        compiler_params=_params(("parallel",)),
    )


def _conva_fwd(proj, w):
    s = proj.shape[0]

    def body(h_ref, b_ref, c_ref, z_ref, w_ref, y_ref):
        u = c_ref[...] * h_ref[...]
        wv = w_ref[...]
        cv = wv[2:3, :] * u + wv[1:2, :] * _shift_down(u, 1) + wv[0:1, :] * _shift_down(u, 2)
        y_ref[...] = b_ref[...] * cv * _silu(z_ref[...])

    return pl.pallas_call(
        body, name="conva_fwd", grid=(D_CONV_A // LANE,),
        in_specs=[_col(s, O_AH), _col(s, O_AB), _col(s, O_AC), _col(s, O_AZ), pl.BlockSpec((3, LANE), lambda j: (0, j))],
        out_specs=pl.BlockSpec((s, LANE), lambda j: (0, j)),
        out_shape=_sds((s, D_CONV_A)),
        compiler_params=_params(("parallel",)),
    )(proj, proj, proj, proj, w)


def _sconv_pre(u, wv, bv):
    return (wv[3:4, :] * u + wv[2:3, :] * _shift_down(u, 1) + wv[1:2, :] * _shift_down(u, 2)
            + wv[0:1, :] * _shift_down(u, 3) + bv)


def _sconv_fwd(proj, w, b):
    s = proj.shape[0]

    def body(u_ref, w_ref, b_ref, o_ref):
        o_ref[...] = _silu(_sconv_pre(u_ref[...], w_ref[...], b_ref[...]))

    return pl.pallas_call(
        body, name="sconv_fwd", grid=(N_XBC // LANE,),
        in_specs=[_col(s, O_XBC), pl.BlockSpec((4, LANE), lambda j: (0, j)), pl.BlockSpec((1, LANE), lambda j: (0, j))],
        out_specs=pl.BlockSpec((s, LANE), lambda j: (0, j)),
        out_shape=_sds((s, N_XBC)),
        compiler_params=_params(("parallel",)),
    )(proj, w, b)


def _ssd_chunk_common(tail, sc):
    l = SSD_CHUNK
    lane = _iota((l, LANE), 1)
    row = _iota((l, LANE), 0)
    tri = (row >= lane).astype(F32)
    a_row = -jnp.exp(sc[1:2, :])
    pre = tail + sc[0:1, :]
    dt = _softplus(pre)
    a_cs = _dot_hi(tri, dt * a_row)
    return lane, row, tri, a_row, pre, dt, a_cs, a_cs.T


def _pick_col(m, lane, k):
    return jnp.sum(jnp.where(lane == k, m, 0.0), axis=1, keepdims=True)


def _pick_row(m, row, k):
    return jnp.sum(jnp.where(row == k, m, 0.0), axis=0, keepdims=True)


def _ssd_fwd(xbc, proj, sc):
    s = xbc.shape[0]
    nc = s // SSD_CHUNK
    l = SSD_CHUNK

    def body(xbc_ref, tail_ref, sc_ref, y_ref, st_ref, state):
        @pl.when(pl.program_id(0) == 0)
        def _():
            state[...] = jnp.zeros_like(state)

        sc_v = sc_ref[...]
        lane, row, _, _, _, dt, a_cs, a_t = _ssd_chunk_common(tail_ref[...], sc_v)
        lane1 = _iota((1, LANE), 1)
        rowp = _iota((LANE, 1), 0)
        d_row = sc_v[2:3, :]
        for j in range(3):
            st_ref[0, j] = state[j]
        for j in range(3):
            xpair = xbc_ref[:, LANE * j:LANE * (j + 1)]
            sp = state[j]
            ypair = jnp.zeros((l, LANE), F32)
            new_s = jnp.zeros((LANE, LANE), F32)
            decay = jnp.zeros((LANE, 1), F32)
            for half in range(2):
                h = 2 * j + half
                g = h // 3
                hm = (lane < 64) if half == 0 else (lane >= 64)
                hrow = (rowp < 64) if half == 0 else (rowp >= 64)
                ac = _pick_col(a_cs, lane, DT_LANE + h)
                ar = _pick_row(a_t, row, DT_LANE + h)
                dtc = _pick_col(dt, lane, DT_LANE + h)
                alast = jnp.sum(jnp.where(lane1 == l - 1, ar, 0.0), axis=1, keepdims=True)
                dh = jnp.sum(jnp.where(lane1 == DT_LANE + h, d_row, 0.0), axis=1, keepdims=True)
                xm = jnp.where(hm, xpair, 0.0)
                xd = xm * dtc
                bm = xbc_ref[:, D_SSD + LANE * g:D_SSD + LANE * (g + 1)]
                cm = xbc_ref[:, D_SSD + SSD_BC + LANE * g:D_SSD + SSD_BC + LANE * (g + 1)]
                lm = jnp.where(row >= lane, jnp.exp(jnp.minimum(ac - ar, 0.0)), 0.0)
                y_diag = _dot(_dot_nt(cm, bm) * lm, xd)
                y_off = jnp.where(hm, _dot_nt(cm, sp), 0.0) * jnp.exp(ac)
                ypair = ypair + y_diag + y_off + xm * dh
                new_s = new_s + _dot_tn(xd * jnp.exp(alast - ac), bm)
                decay = jnp.where(hrow, jnp.exp(alast), decay)
            state[j] = sp * decay + new_s
            y_ref[:, LANE * j:LANE * (j + 1)] = ypair

    return pl.pallas_call(
        body, name="ssd_fwd", grid=(nc,),
        in_specs=[pl.BlockSpec((l, N_XBC), lambda c: (c, 0)),
                  pl.BlockSpec((l, LANE), lambda c: (c, O_TAIL // LANE)), _full((8, LANE))],
        out_specs=[pl.BlockSpec((l, D_SSD), lambda c: (c, 0)), pl.BlockSpec((1, 3, LANE, LANE), lambda c: (c, 0, 0, 0))],
        out_shape=[_sds((s, D_SSD)), _sds((nc, 3, LANE, LANE))],
        scratch_shapes=[pltpu.VMEM((3, LANE, LANE), F32)],
        compiler_params=_params(("arbitrary",)),
    )(xbc, proj, sc)


def _mla_prep_fwd(proj, gq, gkv, wq, wkv, cos, sin):
    s = proj.shape[0]
    ts = _tile(s)
    nh = MLA_HEADS

    def body(cqa_ref, ckv_ref, tail_ref, gq_ref, gkv_ref, wq_ref, wkv_ref, cos_ref, sin_ref,
             q_ref, k_ref, v_ref, qn_ref, kvn_ref, rq_ref, rkv_ref):
        qn, rq = _rms(cqa_ref[...], gq_ref[...])
        kvn, rkv = _rms(ckv_ref[...], gkv_ref[...])
        qn = qn.astype(MXU)
        kvn = kvn.astype(MXU)
        qn_ref[...] = qn
        kvn_ref[...] = kvn
        rq_ref[...] = rq
        rkv_ref[...] = rkv
        q = _dot_nt(qn, wq_ref[...])
        kv = _dot_nt(kvn, wkv_ref[...])
        cosv = cos_ref[...]
        sinv = sin_ref[...]
        lane = _iota((ts, LANE), 1)
        rope_lanes = (lane >= ROPE_LANE) & (lane < ROPE_LANE + QK_ROPE)
        kr = jnp.where(rope_lanes, pltpu.roll(tail_ref[...], ROPE_LANE, 1), 0.0)
        kr = kr * cosv + _rope_swap(kr) * sinv
        for h in range(nh):
            qh = q[:, LANE * h:LANE * (h + 1)]
            q_ref[h] = ((qh * cosv + _rope_swap(qh) * sinv) * ATT_SCALE).astype(MXU)
            k_ref[h] = (kv[:, LANE * h:LANE * (h + 1)] + kr).astype(MXU)
            v_ref[h] = kv[:, LANE * (nh + h):LANE * (nh + h + 1)].astype(MXU)

    head = pl.BlockSpec((nh, ts, LANE), lambda i: (0, i, 0))
    return pl.pallas_call(
        body, name="mla_prep_fwd", grid=(s // ts,),
        in_specs=[pl.BlockSpec((ts, Q_LORA), lambda i: (i, O_CQA // Q_LORA)),
                  pl.BlockSpec((ts, KV_LORA), lambda i: (i, O_CKV // KV_LORA)),
                  pl.BlockSpec((ts, LANE), lambda i: (i, O_TAIL // LANE)),
                  _full((1, Q_LORA)), _full((1, KV_LORA)), _full((nh * LANE, Q_LORA)), _full((2 * nh * LANE, KV_LORA)),
                  _row(ts, LANE), _row(ts, LANE)],
        out_specs=[head, head, head, _row(ts, Q_LORA), _row(ts, KV_LORA), _row(ts, 1), _row(ts, 1)],
        out_shape=[_sds((nh, s, LANE), MXU)] * 3 + [_sds((s, Q_LORA), MXU), _sds((s, KV_LORA), MXU), _sds((s, 1)), _sds((s, 1))],
        compiler_params=_params(("parallel",)),
    )(proj, proj, proj, gq, gkv, wq, wkv, cos, sin)


ATT_SCALE = (QK_NOPE + QK_ROPE) ** -0.5
NEG = -1e30


def _att_tile(s):
    return min(256, s // 2)


def _attn_fwd(q, k, v):
    nh, s, _ = q.shape
    tq = _att_tile(s)
    nq = s // tq

    def body(q_ref, k_ref, v_ref, o_ref, lse_ref):
        i = pl.program_id(1)
        rowi = _iota((tq, tq), 0)
        coli = _iota((tq, tq), 1)
        zero = (jnp.full((tq, 1), NEG, F32), jnp.zeros((tq, 1), F32), jnp.zeros((tq, LANE), F32))
        state = [zero, zero]
        done = [zero, zero]
        for t in range(nq + 1):
            first = t <= i
            qblk = jnp.where(first, i, nq - 1 - i)
            kblk = jnp.where(first, t, t - i - 1)
            qoff = pl.multiple_of(qblk * tq, tq)
            koff = pl.multiple_of(kblk * tq, tq)
            keep = coli <= rowi + jnp.where(kblk == qblk, 0, tq)
            restart = t == i + 1
            for hh in range(2):
                m, lsum, acc = state[hh]
                if t > 0:
                    done[hh] = tuple(jnp.where(restart, a, b) for a, b in zip(state[hh], done[hh]))
                    m = jnp.where(restart, NEG, m)
                    lsum = jnp.where(restart, 0.0, lsum)
                    acc = jnp.where(restart, 0.0, acc)
                sc = _dot_nt(q_ref[hh, pl.ds(qoff, tq), :], k_ref[hh, pl.ds(koff, tq), :])
                sc = jnp.where(keep, sc, NEG)
                m_new = jnp.maximum(m, jnp.max(sc, axis=1, keepdims=True))
                p = jnp.exp(sc - m_new)
                alpha = jnp.exp(m - m_new)
                lsum = alpha * lsum + jnp.sum(p, axis=1, keepdims=True)
                acc = alpha * acc + _dot(p, v_ref[hh, pl.ds(koff, tq), :])
                state[hh] = (m_new, lsum, acc)
        for blk, res in ((i, done), (nq - 1 - i, state)):
            off = pl.multiple_of(blk * tq, tq)
            out = None
            for hh in range(2):
                m, lsum, acc = res[hh]
                o = acc * (1.0 / lsum)
                lse_ref[hh, pl.ds(off, tq), :] = m + jnp.log(lsum)
                out = o if hh == 0 else out + pltpu.roll(o, V_DIM, 1)
            o_ref[pl.ds(off, tq), :] = out

    pair = pl.BlockSpec((2, s, LANE), lambda j, i: (j, 0, 0))
    return pl.pallas_call(
        body, name="attn_fwd", grid=(nh // 2, nq // 2),
        in_specs=[pair, pair, pair],
        out_specs=[pl.BlockSpec((s, LANE), lambda j, i: (0, j)), pl.BlockSpec((2, s, 1), lambda j, i: (j, 0, 0))],
        out_shape=[_sds((s, D_MLA)), _sds((nh, s, 1))],
        compiler_params=_params(("parallel", "arbitrary")),
    )(q, k, v)


def _ssd_gate(y_ssd, s_z, g):
    yz = y_ssd * _silu(s_z)
    g0 = _iota(yz.shape, 1) < D_SSD // 2
    sq = yz * yz
    ms0 = jnp.sum(jnp.where(g0, sq, 0.0), axis=1, keepdims=True) / (D_SSD // 2)
    ms1 = jnp.sum(jnp.where(g0, 0.0, sq), axis=1, keepdims=True) / (D_SSD // 2)
    r = jnp.where(g0, lax.rsqrt(ms0 + SSD_NORM_EPS), lax.rsqrt(ms1 + SSD_NORM_EPS))
    nrm = yz * r
    return nrm * g, nrm, r, g0


def _outproj_fwd(x, proj, ya, y_ssd, o, g_ssd, w):
    s = x.shape[0]
    ts = _tile(s)

    def body(x_ref, p_ref, ya_ref, ys_ref, o_ref, g_ref, w_ref, xo_ref, y_ref):
        yb = _ssd_gate(ys_ref[...], p_ref[:, O_SZ:O_SZ + D_SSD], g_ref[...])[0]
        yc = o_ref[...] * _silu(p_ref[:, O_CZ:O_CZ + D_MLA])
        y = jnp.concatenate([ya_ref[...], yb, yc], axis=1).astype(MXU)
        y_ref[...] = y
        xo_ref[...] = x_ref[...] + jnp.dot(y, w_ref[...], preferred_element_type=F32)

    return pl.pallas_call(
        body, name="outproj_fwd", grid=(s // ts,),
        in_specs=[_row(ts, D_MODEL), _row(ts, NCOL), _row(ts, D_CONV_A), _row(ts, D_SSD), _row(ts, D_MLA),
                  _full((1, D_SSD)), _full((D_MODEL, D_MODEL))],
        out_specs=[_row(ts, D_MODEL), _row(ts, D_MODEL)],
        out_shape=[_sds((s, D_MODEL)), _sds((s, D_MODEL), MXU)],
        compiler_params=_params(("parallel",)),
    )(x, proj, ya, y_ssd, o, g_ssd, w)


def _loss_head(x, g, tgt):
    s = x.shape[0]
    ts = _tile(s)

    def body(x_ref, g_ref, t_ref, dx_ref, dg_ref, loss_ref):
        @pl.when(pl.program_id(0) == 0)
        def _():
            dg_ref[...] = jnp.zeros_like(dg_ref)
            loss_ref[...] = jnp.zeros_like(loss_ref)

        xv = x_ref[...]
        gv = g_ref[...]
        yn, r = _rms(xv, gv)
        e = yn - t_ref[...]
        loss_ref[...] += jnp.sum(jnp.sum(e * e, axis=1, keepdims=True), axis=0, keepdims=True) * (0.5 / D_MODEL)
        dx, dg = _rms_bwd(e * (1.0 / D_MODEL), xv, r, gv)
        dx_ref[...] = dx
        dg_ref[...] += dg

    return pl.pallas_call(
        body, name="loss_head", grid=(s // ts,),
        in_specs=[_row(ts, D_MODEL), _full((1, D_MODEL)), _row(ts, D_MODEL)],
        out_specs=[_row(ts, D_MODEL), _full((1, D_MODEL)), _full((1, LANE))],
        out_shape=[_sds((s, D_MODEL)), _sds((1, D_MODEL)), _sds((1, LANE))],
        compiler_params=_params(("arbitrary",)),
    )(x, g, tgt)


def _outproj_bwd(dout, y, w, proj, y_ssd, o, g_ssd, dep=None):
    s = dout.shape[0]
    ts = _tile(s)

    def body(dout_ref, y_ref, w_ref, p_ref, ys_ref, o_ref, g_ref,
             dya_ref, dys_ref, dsz_ref, dattn_ref, dcz_ref, dg_ref, dw_ref):
        @pl.when(pl.program_id(0) == 0)
        def _():
            dw_ref[...] = jnp.zeros_like(dw_ref)
            dg_ref[...] = jnp.zeros_like(dg_ref)

        dout_b = dout_ref[...].astype(MXU)
        dw_ref[...] += _dot_tn(y_ref[...], dout_b)
        dy = _dot_nt(dout_b, w_ref[...])
        dya_ref[...] = dy[:, :D_CONV_A]
        dyb = dy[:, D_CONV_A:D_CONV_A + D_SSD]
        sz = p_ref[:, O_SZ:O_SZ + D_SSD]
        ys = ys_ref[...]
        gv = g_ref[...]
        _, nrm, r, g0 = _ssd_gate(ys, sz, gv)
        dg_ref[...] += jnp.sum(dyb * nrm, axis=0, keepdims=True)
        dn = dyb * gv
        t = dn * nrm
        mean = jnp.where(g0, jnp.sum(jnp.where(g0, t, 0.0), axis=1, keepdims=True),
                         jnp.sum(jnp.where(g0, 0.0, t), axis=1, keepdims=True)) / (D_SSD // 2)
        dyz = r * (dn - nrm * mean)
        dys_ref[...] = dyz * _silu(sz)
        dsz_ref[...] = dyz * ys * _dsilu(sz)
        dyc = dy[:, D_CONV_A + D_SSD:]
        cz = p_ref[:, O_CZ:O_CZ + D_MLA]
        dattn_ref[...] = dyc * _silu(cz)
        dcz_ref[...] = dyc * o_ref[...] * _dsilu(cz)

    return _call_after(
        dep, body, (dout, y, w, proj, y_ssd, o, g_ssd), name="outproj_bwd", grid=(s // ts,),
        in_specs=[_row(ts, D_MODEL), _row(ts, D_MODEL), _full((D_MODEL, D_MODEL)), _row(ts, NCOL), _row(ts, D_SSD),
                  _row(ts, D_MLA), _full((1, D_SSD))],
        out_specs=[_row(ts, D_CONV_A), _row(ts, D_SSD), _row(ts, D_SSD), _row(ts, D_MLA), _row(ts, D_MLA),
                   _full((1, D_SSD)), _full((D_MODEL, D_MODEL))],
        out_shape=[_sds((s, D_CONV_A)), _sds((s, D_SSD)), _sds((s, D_SSD)), _sds((s, D_MLA)), _sds((s, D_MLA)),
                   _sds((1, D_SSD)), _sds((D_MODEL, D_MODEL))],
        compiler_params=_params(("arbitrary",)),
    )


def _attn_bwd(q, k, v, o, d_o, lse, dep=None):
    nh, s, _ = q.shape
    tq = _att_tile(s)
    nq = s // tq

    def body(q_ref, k_ref, v_ref, o_ref, do_ref, lse_ref, dq_ref, dk_ref, dv_ref, dop, delta):
        i = pl.program_id(1)

        @pl.when(i == 0)
        def _():
            lane = _iota((s, LANE), 1)
            for hh in range(2):
                dov = do_ref[...]
                ov = o_ref[...]
                if hh == 1:
                    dov = pltpu.roll(dov, V_DIM, 1)
                    ov = pltpu.roll(ov, V_DIM, 1)
                dov = jnp.where(lane < V_DIM, dov, 0.0)
                dop[hh] = dov.astype(MXU)
                delta[hh] = jnp.sum(dov * ov, axis=1, keepdims=True)
                dq_ref[hh] = jnp.zeros((s, LANE), F32)

        rowi = _iota((tq, tq), 0)
        coli = _iota((tq, tq), 1)
        z = jnp.zeros((tq, LANE), F32)
        state = [(z, z), (z, z)]
        done = [(z, z), (z, z)]
        for t in range(nq + 1):
            first = t <= nq - 1 - i
            kblk = jnp.where(first, i, nq - 1 - i)
            qblk = jnp.where(first, i + t, t - 1)
            qoff = pl.multiple_of(qblk * tq, tq)
            koff = pl.multiple_of(kblk * tq, tq)
            keep = coli <= rowi + jnp.where(kblk == qblk, 0, tq)
            restart = t == nq - i
            for hh in range(2):
                dk, dv = state[hh]
                if t > 0:
                    done[hh] = tuple(jnp.where(restart, a, b) for a, b in zip(state[hh], done[hh]))
                    dk = jnp.where(restart, 0.0, dk)
                    dv = jnp.where(restart, 0.0, dv)
                kb = k_ref[hh, pl.ds(koff, tq), :]
                qb = q_ref[hh, pl.ds(qoff, tq), :]
                dob = dop[hh, pl.ds(qoff, tq), :]
                sc = jnp.where(keep, _dot_nt(qb, kb), NEG)
                p = jnp.exp(sc - lse_ref[hh, pl.ds(qoff, tq), :])
                dp = _dot_nt(dob, v_ref[hh, pl.ds(koff, tq), :])
                ds = p * (dp - delta[hh, pl.ds(qoff, tq), :])
                dq_ref[hh, pl.ds(qoff, tq), :] += _dot(ds, kb)
                state[hh] = (dk + _dot_tn(ds, qb), dv + _dot_tn(p, dob))
        for blk, res in ((i, done), (nq - 1 - i, state)):
            off = pl.multiple_of(blk * tq, tq)
            for hh in range(2):
                dk_ref[hh, pl.ds(off, tq), :] = res[hh][0]
                dv_ref[hh, pl.ds(off, tq), :] = res[hh][1]

    pair = pl.BlockSpec((2, s, LANE), lambda j, i: (j, 0, 0))
    return _call_after(
        dep, body, (q, k, v, o, d_o, lse), name="attn_bwd", grid=(nh // 2, nq // 2),
        in_specs=[pair, pair, pair, pl.BlockSpec((s, LANE), lambda j, i: (0, j)), pl.BlockSpec((s, LANE), lambda j, i: (0, j)),
                  pl.BlockSpec((2, s, 1), lambda j, i: (j, 0, 0))],
        out_specs=[pair, pair, pair],
        out_shape=[_sds((nh, s, LANE))] * 3,
        scratch_shapes=[pltpu.VMEM((2, s, LANE), MXU), pltpu.VMEM((2, s, 1), F32)],
        compiler_params=_params(("parallel", "arbitrary")),
    )


def _ssd_bwd(xbc, proj, sc, states, dy):
    s = xbc.shape[0]
    nc = s // SSD_CHUNK
    l = SSD_CHUNK

    def body(xbc_ref, tail_ref, sc_ref, st_ref, dy_ref, dxbc_ref, dtail_ref, dsc_ref, dstate):
        @pl.when(pl.program_id(0) == 0)
        def _():
            dstate[...] = jnp.zeros_like(dstate)
            dsc_ref[...] = jnp.zeros_like(dsc_ref)

        sc_v = sc_ref[...]
        lane, row, tri, a_row, pre, dt, a_cs, a_t = _ssd_chunk_common(tail_ref[...], sc_v)
        lane1 = _iota((1, LANE), 1)
        rowp = _iota((LANE, 1), 0)
        rowl = _iota((l, 1), 0)
        d_row = sc_v[2:3, :]
        da_col = jnp.zeros((l, LANE), F32)
        da_row = jnp.zeros((LANE, l), F32)
        dt_x = jnp.zeros((l, LANE), F32)
        dd_row = jnp.zeros((1, LANE), F32)
        db = [jnp.zeros((l, LANE), F32), jnp.zeros((l, LANE), F32)]
        dc = [jnp.zeros((l, LANE), F32), jnp.zeros((l, LANE), F32)]
        for j in range(3):
            xpair = xbc_ref[:, LANE * j:LANE * (j + 1)]
            dypair = dy_ref[:, LANE * j:LANE * (j + 1)]
            sp = st_ref[0, j]
            dsp = dstate[j]
            dxpair = jnp.zeros((l, LANE), F32)
            ds_new = jnp.zeros((LANE, LANE), F32)
            decay = jnp.zeros((LANE, 1), F32)
            for half in range(2):
                h = 2 * j + half
                g = h // 3
                hm = (lane < 64) if half == 0 else (lane >= 64)
                hrow = (rowp < 64) if half == 0 else (rowp >= 64)
                ac = _pick_col(a_cs, lane, DT_LANE + h)
                ar = _pick_row(a_t, row, DT_LANE + h)
                dtc = _pick_col(dt, lane, DT_LANE + h)
                alast = jnp.sum(jnp.where(lane1 == l - 1, ar, 0.0), axis=1, keepdims=True)
                dh = jnp.sum(jnp.where(lane1 == DT_LANE + h, d_row, 0.0), axis=1, keepdims=True)
                xm = jnp.where(hm, xpair, 0.0)
                xd = xm * dtc
                dym = jnp.where(hm, dypair, 0.0)
                bm = xbc_ref[:, D_SSD + LANE * g:D_SSD + LANE * (g + 1)]
                cm = xbc_ref[:, D_SSD + SSD_BC + LANE * g:D_SSD + SSD_BC + LANE * (g + 1)]
                lm = jnp.where(row >= lane, jnp.exp(jnp.minimum(ac - ar, 0.0)), 0.0)
                e_in = jnp.exp(ac)
                f_out = jnp.exp(alast - ac)
                e_last = jnp.exp(alast)
                m = _dot_nt(cm, bm) * lm
                y_off = jnp.where(hm, _dot_nt(cm, sp), 0.0) * e_in
                dm = _dot_nt(dym, xd)
                dxd = _dot_tn(m, dym)
                dg = dm * lm
                dye = dym * e_in
                dc[g] = dc[g] + _dot(dg, bm) + _dot(dye, sp)
                db[g] = db[g] + _dot_tn(dg, cm)
                qm = dm * m
                dac = jnp.sum(qm, axis=1, keepdims=True) + jnp.sum(dym * y_off, axis=1, keepdims=True)
                dar = -jnp.sum(qm, axis=0, keepdims=True)
                dxf = jnp.where(hm, _dot_nt(bm, dsp), 0.0)
                db[g] = db[g] + _dot(xd * f_out, dsp)
                dxd = dxd + dxf * f_out
                df = jnp.sum(dxf * xd, axis=1, keepdims=True) * f_out
                dac = dac - df
                s_last = jnp.sum(df, axis=0, keepdims=True)
                ss = jnp.sum(jnp.where(hrow, dsp * sp, 0.0), axis=1, keepdims=True)
                s_last = s_last + e_last * jnp.sum(ss, axis=0, keepdims=True)
                dac = dac + jnp.where(rowl == l - 1, s_last, 0.0)
                ds_new = ds_new + _dot_tn(dye, cm)
                decay = jnp.where(hrow, e_last, decay)
                dxpair = dxpair + dxd * dtc + dym * dh
                dt_x = dt_x + jnp.where(lane == DT_LANE + h, jnp.sum(dxd * xm, axis=1, keepdims=True), 0.0)
                dsum = jnp.sum(jnp.sum(dym * xm, axis=1, keepdims=True), axis=0, keepdims=True)
                dd_row = dd_row + jnp.where(lane1 == DT_LANE + h, dsum, 0.0)
                da_col = da_col + jnp.where(lane == DT_LANE + h, dac, 0.0)
                da_row = da_row + jnp.where(row == DT_LANE + h, dar, 0.0)
            dstate[j] = dsp * decay + ds_new
            dxbc_ref[:, LANE * j:LANE * (j + 1)] = dxpair
        for g in range(2):
            dxbc_ref[:, D_SSD + LANE * g:D_SSD + LANE * (g + 1)] = db[g]
            dxbc_ref[:, D_SSD + SSD_BC + LANE * g:D_SSD + SSD_BC + LANE * (g + 1)] = dc[g]
        dla = _dot_hi_tn(tri, da_col + da_row.T)
        ddt = dt_x + dla * a_row
        dpre = ddt * _sigmoid(pre)
        dtm = (lane >= DT_LANE) & (lane < DT_LANE + SSD_HEADS)
        dtail_ref[...] = jnp.where(dtm, dpre, 0.0)
        dtm1 = (lane1 >= DT_LANE) & (lane1 < DT_LANE + SSD_HEADS)
        dsc_ref[0:1, :] += jnp.where(dtm1, jnp.sum(dpre, axis=0, keepdims=True), 0.0)
        dsc_ref[1:2, :] += jnp.where(dtm1, jnp.sum(dla * dt, axis=0, keepdims=True) * a_row, 0.0)
        dsc_ref[2:3, :] += dd_row

    rev = lambda c: nc - 1 - c
    return pl.pallas_call(
        body, name="ssd_bwd", grid=(nc,),
        in_specs=[pl.BlockSpec((l, N_XBC), lambda c: (rev(c), 0)),
                  pl.BlockSpec((l, LANE), lambda c: (rev(c), O_TAIL // LANE)), _full((8, LANE)),
                  pl.BlockSpec((1, 3, LANE, LANE), lambda c: (rev(c), 0, 0, 0)),
                  pl.BlockSpec((l, D_SSD), lambda c: (rev(c), 0))],
        out_specs=[pl.BlockSpec((l, N_XBC), lambda c: (rev(c), 0)), pl.BlockSpec((l, LANE), lambda c: (rev(c), 0)),
                   _full((8, LANE))],
        out_shape=[_sds((s, N_XBC)), _sds((s, LANE)), _sds((8, LANE))],
        scratch_shapes=[pltpu.VMEM((3, LANE, LANE), F32)],
        compiler_params=_params(("arbitrary",)),
    )(xbc, proj, sc, states, dy)


def _sconv_bwd(proj, w, b, dxbc, dep=None):
    s = proj.shape[0]

    def body(u_ref, w_ref, b_ref, d_ref, du_ref, dw_ref, db_ref):
        u = u_ref[...]
        wv = w_ref[...]
        dpre = d_ref[...] * _dsilu(_sconv_pre(u, wv, b_ref[...]))
        du_ref[...] = (wv[3:4, :] * dpre + wv[2:3, :] * _shift_up(dpre, 1) + wv[1:2, :] * _shift_up(dpre, 2)
                       + wv[0:1, :] * _shift_up(dpre, 3))
        for k in range(4):
            dw_ref[k:k + 1, :] = jnp.sum(dpre * _shift_down(u, 3 - k), axis=0, keepdims=True)
        db_ref[...] = jnp.sum(dpre, axis=0, keepdims=True)

    blk = pl.BlockSpec((s, LANE), lambda j: (0, j))
    return _call_after(
        dep, body, (proj, w, b, dxbc), name="sconv_bwd", grid=(N_XBC // LANE,),
        in_specs=[_col(s, O_XBC), pl.BlockSpec((4, LANE), lambda j: (0, j)), pl.BlockSpec((1, LANE), lambda j: (0, j)), blk],
        out_specs=[blk, pl.BlockSpec((4, LANE), lambda j: (0, j)), pl.BlockSpec((1, LANE), lambda j: (0, j))],
        out_shape=[_sds((s, N_XBC)), _sds((4, N_XBC)), _sds((1, N_XBC))],
        compiler_params=_params(("parallel",)),
    )


def _conva_bwd(proj, w, dya):
    s = proj.shape[0]

    def body(h_ref, b_ref, c_ref, z_ref, w_ref, d_ref, da_ref, dw_ref):
        ah, ab, acv, az = h_ref[...], b_ref[...], c_ref[...], z_ref[...]
        wv = w_ref[...]
        u = acv * ah
        cv = wv[2:3, :] * u + wv[1:2, :] * _shift_down(u, 1) + wv[0:1, :] * _shift_down(u, 2)
        dy = d_ref[...]
        sz = _silu(az)
        da_ref[1] = dy * cv * sz
        da_ref[3] = dy * ab * cv * _dsilu(az)
        dcv = dy * ab * sz
        du = wv[2:3, :] * dcv + wv[1:2, :] * _shift_up(dcv, 1) + wv[0:1, :] * _shift_up(dcv, 2)
        da_ref[0] = du * acv
        da_ref[2] = du * ah
        for k in range(3):
            dw_ref[k:k + 1, :] = jnp.sum(dcv * _shift_down(u, 2 - k), axis=0, keepdims=True)

    return pl.pallas_call(
        body, name="conva_bwd", grid=(D_CONV_A // LANE,),
        in_specs=[_col(s, O_AH), _col(s, O_AB), _col(s, O_AC), _col(s, O_AZ), pl.BlockSpec((3, LANE), lambda j: (0, j)),
                  pl.BlockSpec((s, LANE), lambda j: (0, j))],
        out_specs=[pl.BlockSpec((4, s, LANE), lambda j: (0, 0, j)), pl.BlockSpec((3, LANE), lambda j: (0, j))],
        out_shape=[_sds((4, s, D_CONV_A)), _sds((3, D_CONV_A))],
        compiler_params=_params(("parallel",)),
    )(proj, proj, proj, proj, w, dya)


def _mla_prep_bwd(dq, dk, dv, proj, qn, kvn, rq, rkv, gq, gkv, wq, wkv, cos, sin):
    s = proj.shape[0]
    ts = _tile(s)
    nh = MLA_HEADS

    def body(dq_ref, dk_ref, dv_ref, cqa_ref, ckv_ref, qn_ref, kvn_ref, rq_ref, rkv_ref, gq_ref, gkv_ref,
             wq_ref, wkv_ref, cos_ref, sin_ref, dcqa_ref, dckv_ref, dtail_ref, dwq_ref, dwkv_ref, dgq_ref, dgkv_ref):
        @pl.when(pl.program_id(0) == 0)
        def _():
            dwq_ref[...] = jnp.zeros_like(dwq_ref)
            dwkv_ref[...] = jnp.zeros_like(dwkv_ref)
            dgq_ref[...] = jnp.zeros_like(dgq_ref)
            dgkv_ref[...] = jnp.zeros_like(dgkv_ref)

        cosv = cos_ref[...]
        sinv = sin_ref[...]
        lane = _iota((ts, LANE), 1)
        rope_lanes = (lane >= ROPE_LANE) & (lane < ROPE_LANE + QK_ROPE)

        def unrope(gr):
            return gr * cosv + _rope_swap(gr * sinv)

        dqs, dks, dvs = [], [], []
        dkr = jnp.zeros((ts, LANE), F32)
        for h in range(nh):
            dqs.append(unrope(dq_ref[h] * ATT_SCALE).astype(MXU))
            dkh = dk_ref[h]
            dks.append(jnp.where(lane < QK_NOPE, dkh, 0.0).astype(MXU))
            dkr = dkr + jnp.where(rope_lanes, dkh, 0.0)
            dvs.append(dv_ref[h].astype(MXU))
        dtail_ref[...] = pltpu.roll(jnp.where(rope_lanes, unrope(dkr), 0.0), ROPE_LANE, 1)
        dq_all = jnp.concatenate(dqs, axis=1)
        dkv_all = jnp.concatenate(dks + dvs, axis=1)
        dwq_ref[...] += _dot_tn(dq_all, qn_ref[...])
        dwkv_ref[...] += _dot_tn(dkv_all, kvn_ref[...])
        dcqa, dgq = _rms_bwd(_dot(dq_all, wq_ref[...]), cqa_ref[...], rq_ref[...], gq_ref[...])
        dckv, dgkv = _rms_bwd(_dot(dkv_all, wkv_ref[...]), ckv_ref[...], rkv_ref[...], gkv_ref[...])
        dcqa_ref[...] = dcqa
        dckv_ref[...] = dckv
        dgq_ref[...] += dgq
        dgkv_ref[...] += dgkv

    head = pl.BlockSpec((nh, ts, LANE), lambda i: (0, i, 0))
    return pl.pallas_call(
        body, name="mla_prep_bwd", grid=(s // ts,),
        in_specs=[head, head, head,
                  pl.BlockSpec((ts, Q_LORA), lambda i: (i, O_CQA // Q_LORA)),
                  pl.BlockSpec((ts, KV_LORA), lambda i: (i, O_CKV // KV_LORA)),
                  _row(ts, Q_LORA), _row(ts, KV_LORA), _row(ts, 1), _row(ts, 1),
                  _full((1, Q_LORA)), _full((1, KV_LORA)), _full((nh * LANE, Q_LORA)), _full((2 * nh * LANE, KV_LORA)),
                  _row(ts, LANE), _row(ts, LANE)],
        out_specs=[_row(ts, Q_LORA), _row(ts, KV_LORA), _row(ts, LANE), _full((nh * LANE, Q_LORA)),
                   _full((2 * nh * LANE, KV_LORA)), _full((1, Q_LORA)), _full((1, KV_LORA))],
        out_shape=[_sds((s, Q_LORA)), _sds((s, KV_LORA)), _sds((s, LANE)), _sds((nh * LANE, Q_LORA)),
                   _sds((2 * nh * LANE, KV_LORA)), _sds((1, Q_LORA)), _sds((1, KV_LORA))],
        compiler_params=_params(("arbitrary",)),
    )(dq, dk, dv, proj, proj, qn, kvn, rq, rkv, gq, gkv, wq, wkv, cos, sin)


def _inproj_bwd(da4, dsz, dxbc_in, dcqa, dckv, dcz, dtail_a, dtail_b, w, x, rstd, g, dout):
    s = x.shape[0]
    ts = _tile(s)

    def body(da_ref, dsz_ref, dxbc_ref, dcqa_ref, dckv_ref, dcz_ref, dta_ref, dtb_ref, w_ref, x_ref, r_ref, g_ref, dout_ref,
             dproj_ref, dx_ref, dg_ref):
        @pl.when(pl.program_id(0) == 0)
        def _():
            dg_ref[...] = jnp.zeros_like(dg_ref)

        dproj = jnp.concatenate(
            [da_ref[0], da_ref[1], da_ref[2], da_ref[3], dsz_ref[...], dxbc_ref[...], dcqa_ref[...], dckv_ref[...],
             dcz_ref[...], dta_ref[...] + dtb_ref[...]], axis=1).astype(MXU)
        dproj_ref[...] = dproj
        dh = _dot_nt(dproj, w_ref[...])
        dx, dg = _rms_bwd(dh, x_ref[...], r_ref[...], g_ref[...])
        dx_ref[...] = dout_ref[...] + dx
        dg_ref[...] += dg

    return pl.pallas_call(
        body, name="inproj_bwd", grid=(s // ts,),
        in_specs=[pl.BlockSpec((4, ts, D_CONV_A), lambda i: (0, i, 0)), _row(ts, D_SSD), _row(ts, N_XBC), _row(ts, Q_LORA),
                  _row(ts, KV_LORA), _row(ts, D_MLA), _row(ts, LANE), _row(ts, LANE), _full((D_MODEL, NCOL)),
                  _row(ts, D_MODEL), _row(ts, 1), _full((1, D_MODEL)), _row(ts, D_MODEL)],
        out_specs=[_row(ts, NCOL), _row(ts, D_MODEL), _full((1, D_MODEL))],
        out_shape=[_sds((s, NCOL), MXU), _sds((s, D_MODEL)), _sds((1, D_MODEL))],
        compiler_params=_params(("arbitrary",)),
    )(da4, dsz, dxbc_in, dcqa, dckv, dcz, dtail_a, dtail_b, w, x, rstd, g, dout)


DWIN_BLOCK = 640


def _dwin(h, dproj):
    s = h.shape[0]

    def body(h_ref, d_ref, o_ref):
        o_ref[...] = _dot_tn(h_ref[...], d_ref[...])

    return pl.pallas_call(
        body, name="dwin", grid=(NCOL // DWIN_BLOCK,),
        in_specs=[_full((s, D_MODEL)), pl.BlockSpec((s, DWIN_BLOCK), lambda j: (0, j))],
        out_specs=pl.BlockSpec((D_MODEL, DWIN_BLOCK), lambda j: (0, j)),
        out_shape=_sds((D_MODEL, NCOL)),
        compiler_params=_params(("parallel",)),
    )(h, dproj)


def _adamw(w, g, m, v):
    r, c = w.shape
    tr = r
    for cand in (256, 128, 64, 32, 16, 8):
        if r % cand == 0:
            tr = cand
            break
    bc1 = 1.0 - ADAM_B1 ** ADAM_STEP
    bc2 = 1.0 - ADAM_B2 ** ADAM_STEP

    def body(w_ref, g_ref, m_ref, v_ref, d_ref, mo_ref, vo_ref):
        gv = g_ref[...]
        mn = ADAM_B1 * m_ref[...] + (1.0 - ADAM_B1) * gv
        vn = ADAM_B2 * v_ref[...] + (1.0 - ADAM_B2) * (gv * gv)
        mo_ref[...] = mn
        vo_ref[...] = vn
        d_ref[...] = -ADAM_LR * ((mn / bc1) / (jnp.sqrt(vn / bc2) + ADAM_EPS) + ADAM_WD * w_ref[...])

    blk = pl.BlockSpec((tr, c), lambda i: (i, 0))
    return pl.pallas_call(
        body, name="adamw", grid=(r // tr,),
        in_specs=[blk] * 4, out_specs=[blk] * 3, out_shape=[_sds((r, c))] * 3,
        compiler_params=_params(("parallel",)),
    )(w, g, m, v)


def _perm_cols(w):
    pad = jnp.zeros(w.shape[:-1] + (NCOL - IN_COLS,), w.dtype)
    return jnp.concatenate([w[..., :2304], w[..., 2310:2566], w[..., 2566:2694], w[..., 2726:3110],
                            w[..., 2694:2726], w[..., 2304:2310], pad], axis=-1)


def _unperm_cols(g):
    return jnp.concatenate([g[..., :2304], g[..., 3104:3110], g[..., 2304:2560], g[..., 2560:2688],
                            g[..., 3072:3104], g[..., 2688:3072]], axis=-1)


def _wq_layout(wt):
    return jnp.pad(wt.reshape(MLA_HEADS, QK_NOPE + QK_ROPE, Q_LORA), ((0, 0), (0, 32), (0, 0))).reshape(MLA_HEADS * LANE, Q_LORA)


def _wq_unlayout(g):
    return g.reshape(MLA_HEADS, LANE, Q_LORA)[:, :QK_NOPE + QK_ROPE].reshape(MLA_HEADS * (QK_NOPE + QK_ROPE), Q_LORA)


def _wkv_layout(wt):
    t = wt.reshape(MLA_HEADS, 2, 64, KV_LORA).transpose(1, 0, 2, 3)
    return jnp.pad(t, ((0, 0), (0, 0), (0, 64), (0, 0))).reshape(2 * MLA_HEADS * LANE, KV_LORA)


def _wkv_unlayout(g):
    t = g.reshape(2, MLA_HEADS, LANE, KV_LORA)[:, :, :64]
    return t.transpose(1, 0, 2, 3).reshape(MLA_HEADS * LANE, KV_LORA)


def _rope_tables(positions):
    inv_freq = ROPE_BASE ** (-jnp.arange(0, QK_ROPE, 2, dtype=F32) / QK_ROPE)
    ang = positions.astype(F32)[:, None] * inv_freq
    cos, sin = jnp.cos(ang), jnp.sin(ang)
    s = positions.shape[0]
    one, zero = jnp.ones((s, ROPE_LANE), F32), jnp.zeros((s, ROPE_LANE), F32)
    cos_t = jnp.concatenate([one, cos, cos, one[:, :32]], axis=1)
    sin_t = jnp.concatenate([zero, -sin, sin, zero[:, :32]], axis=1)
    return cos_t, sin_t


def _ssd_scalars(dt_bias, a_log, d_skip):
    return jnp.pad(jnp.stack([dt_bias, a_log, d_skip]), ((0, 5), (DT_LANE, LANE - DT_LANE - SSD_HEADS)))


def _layer_fwd(x, lw, cos, sin, dep=None):
    proj, h, rstd = _inproj_fwd(x, lw["norm_g"], lw["w_in"], dep)
    ya = _conva_fwd(proj, lw["conv_a_w"])
    xbc = _sconv_fwd(proj, lw["ssd_conv_w"], lw["ssd_conv_b"])
    y_ssd, states = _ssd_fwd(xbc, proj, lw["sc"])
    q, k, v, qn, kvn, rq, rkv = _mla_prep_fwd(proj, lw["gq"], lw["gkv"], lw["wq"], lw["wkv"], cos, sin)
    o, lse = _attn_fwd(q, k, v)
    x_out, y = _outproj_fwd(x, proj, ya, y_ssd, o, lw["g_ssd"], lw["w_out"])
    saved = dict(x=x, proj=proj, h=h, rstd=rstd, xbc=xbc, y_ssd=y_ssd, states=states, q=q, k=k, v=v, qn=qn, kvn=kvn,
                 rq=rq, rkv=rkv, o=o, lse=lse, y=y)
    return x_out, saved


def _layer_bwd(dout, lw, sv, cos, sin, rs=None):
    tok = lambda: None if rs is None else rs["h"]["token"]
    dya, dys, dsz, d_o, dcz, dg_ssd, dw_out = _outproj_bwd(dout, sv["y"], lw["w_out"], sv["proj"], sv["y_ssd"], sv["o"],
                                                            lw["g_ssd"], tok())
    if rs is not None:
        rs = _rs_add_mine(rs, dya)
    dq, dk, dv = _attn_bwd(sv["q"], sv["k"], sv["v"], sv["o"], d_o, sv["lse"], tok())
    dxbc, dtail_s, dsc = _ssd_bwd(sv["xbc"], sv["proj"], lw["sc"], sv["states"], dys)
    if rs is not None:
        rs = _rs_add_chips(rs, dxbc)
    du, dw_sconv, db_sconv = _sconv_bwd(sv["proj"], lw["ssd_conv_w"], lw["ssd_conv_b"], dxbc, tok())
    da4, dw_conva = _conva_bwd(sv["proj"], lw["conv_a_w"], dya)
    dcqa, dckv, dtail_m, dwq, dwkv, dgq, dgkv = _mla_prep_bwd(
        dq, dk, dv, sv["proj"], sv["qn"], sv["kvn"], sv["rq"], sv["rkv"], lw["gq"], lw["gkv"], lw["wq"], lw["wkv"], cos, sin)
    dproj, dx, dg = _inproj_bwd(da4, dsz, du, dcqa, dckv, dcz, dtail_s, dtail_m, lw["w_in"], sv["x"], sv["rstd"],
                                lw["norm_g"], dout)
    reduced = None if rs is None else _rs_end(rs, dx)
    dw_in = _dwin(sv["h"], dproj)
    grads = dict(norm_g=dg, w_in=dw_in, conv_a_w=dw_conva, ssd_conv_w=dw_sconv, ssd_conv_b=db_sconv, sc=dsc,
                 g_ssd=dg_ssd, gq=dgq, wq=dwq, gkv=dgkv, wkv=dwkv, w_out=dw_out)
    return dx, grads, reduced


ANY = pl.BlockSpec(memory_space=pl.ANY)
N_CHIPS = 4
N_DEV = 8


def _place():
    return lax.axis_index("x"), lax.axis_index("y"), lax.axis_index("c")


HBM_SPEC = pl.BlockSpec(memory_space=pltpu.HBM)
SEM_SPEC = pl.BlockSpec(memory_space=pltpu.SEMAPHORE)
PAYLOAD = jnp.bfloat16


def _hbm(a):
    return pltpu.with_memory_space_constraint(a, pltpu.HBM)


def _run_plan(plan, srcs, lands, send_sems, recv_sems, start, wait):
    copies = plan(srcs, lands)
    if start:
        for i, (src, dst, _, to) in enumerate(copies):
            pltpu.make_async_remote_copy(src_ref=src, dst_ref=dst, send_sem=send_sems.at[i], recv_sem=recv_sems.at[i],
                                         device_id=to, device_id_type=MESH_T).start()
    if wait:
        for i, (src, _, arrives, to) in enumerate(copies):
            cp = pltpu.make_async_remote_copy(src_ref=src, dst_ref=arrives, send_sem=send_sems.at[i],
                                              recv_sem=recv_sems.at[i], device_id=to, device_id_type=MESH_T)
            cp.wait_send()
            cp.wait_recv()


def _exchange_fused(name, plan, n_copies, srcs, land_shapes):
    ns, nl = len(srcs), len(land_shapes)

    def body(*refs):
        _run_plan(plan, refs[:ns], refs[ns:ns + nl], refs[ns + nl], refs[ns + nl + 1], True, True)

    return pl.pallas_call(
        body, name=name, in_specs=[ANY] * ns, out_specs=[ANY] * nl, out_shape=list(land_shapes),
        scratch_shapes=[pltpu.SemaphoreType.DMA((n_copies,)), pltpu.SemaphoreType.DMA((n_copies,))],
    )(*srcs)


def _exchange_start(name, plan, n_copies, srcs, land_shapes):
    ns, nl = len(srcs), len(land_shapes)

    def body(*refs):
        ins = refs[:ns + nl]
        send_sems, recv_sems = refs[ns + nl], refs[ns + nl + 1]
        token = refs[-1]
        _run_plan(plan, ins[:ns], ins[ns:], send_sems, recv_sems, True, False)
        token[...] = jnp.zeros_like(token)

    thru = [pltpu.HBM(a.shape, a.dtype) for a in srcs] + [pltpu.HBM(a.shape, a.dtype) for a in land_shapes]
    outs = pl.pallas_call(
        body, name=name,
        out_shape=(pltpu.SemaphoreType.DMA((n_copies,)), pltpu.SemaphoreType.DMA((n_copies,)), *thru, _sds((8, LANE))),
        in_specs=[HBM_SPEC] * (ns + nl),
        out_specs=(SEM_SPEC, SEM_SPEC, *[HBM_SPEC] * (ns + nl), pl.BlockSpec(memory_space=pltpu.VMEM)),
        input_output_aliases={i: 2 + i for i in range(ns + nl)},
        compiler_params=pltpu.CompilerParams(has_side_effects=pltpu.SideEffectType.DATAFLOW_SIDE_EFFECTING),
    )(*[_hbm(a) for a in srcs], *[_hbm(lax.empty(a.shape, a.dtype)) for a in land_shapes])
    return (outs[0], outs[1]), list(outs[2:2 + ns]), list(outs[2 + ns:2 + ns + nl]), outs[-1]


def _exchange_wait(name, plan, sems, srcs, lands, after):
    ns, nl = len(srcs), len(lands)

    def body(*refs):
        _run_plan(plan, refs[:ns], refs[ns:ns + nl], refs[ns + nl], refs[ns + nl + 1], False, True)

    outs = pl.pallas_call(
        body, name=name,
        out_shape=[pltpu.HBM(a.shape, a.dtype) for a in list(srcs) + list(lands)],
        in_specs=[HBM_SPEC] * (ns + nl) + [SEM_SPEC, SEM_SPEC, ANY], out_specs=[HBM_SPEC] * (ns + nl),
        input_output_aliases={i: i for i in range(ns + nl)},
        compiler_params=pltpu.CompilerParams(has_side_effects=pltpu.SideEffectType.DATAFLOW_SIDE_EFFECTING),
    )(*srcs, *lands, sems[0], sems[1], after)
    return list(outs[:ns]), list(outs[ns:])


def _xchg_begin(name, plan, n_copies, srcs, land_shapes, split):
    if not split:
        return dict(split=False, srcs=list(srcs), lands=_exchange_fused(name, plan, n_copies, srcs, land_shapes), token=None)
    sems, srcs_t, lands_t, token = _exchange_start(name + "_start", plan, n_copies, srcs, land_shapes)
    return dict(split=True, name=name, plan=plan, sems=sems, srcs=srcs_t, lands=lands_t, token=token)


def _xchg_end(h, after):
    if not h["split"]:
        return h["srcs"], h["lands"]
    return _exchange_wait(h["name"] + "_wait", h["plan"], h["sems"], h["srcs"], h["lands"], after)


def _other_chips():
    x, y, c = _place()
    return [(1 - x, y), (x, 1 - y), (1 - x, 1 - y)]


def _gather_plan(srcs, lands):
    x, y, c = _place()
    me = 2 * x + y
    return [(srcs[a], lands[a].at[me], lands[a].at[2 * cx + cy], (cx, cy, c))
            for (cx, cy) in _other_chips() for a in range(len(srcs))]


def _gather_begin(shards, split, tag):
    shapes = [_sds((N_CHIPS,) + a.shape, a.dtype) for a in shards]
    return _xchg_begin(f"gather_{tag}", _gather_plan, 3 * len(shards), shards, shapes, split)


def _gather_end(h, after):
    shards, lands = _xchg_end(h, after)
    me = 2 * lax.axis_index("x") + lax.axis_index("y")
    return [lax.dynamic_update_index_in_dim(g, s, me, 0) for g, s in zip(lands, shards)]


def _swap_plan(srcs, lands):
    x, y, c = _place()
    return [(srcs[a].at[:, 1 - c], lands[a], lands[a], (x, y, 1 - c)) for a in range(len(srcs))]


def _chips_plan(srcs, lands):
    x, y, c = _place()
    me = 2 * x + y
    return [(srcs[a].at[2 * cx + cy], lands[a].at[me], lands[a].at[2 * cx + cy], (cx, cy, c))
            for (cx, cy) in _other_chips() for a in range(len(srcs))]


def _share_plan(srcs, lands):
    x, y, c = _place()
    return [(srcs[a], lands[a].at[c], lands[a].at[1 - c], (x, y, 1 - c)) for a in range(len(srcs))]


def _allreduce_small(slab):
    r = slab.shape[0]

    def body(s_ref, o_ref, gath, send_sems, recv_sems):
        x, y, c = _place()
        me = 4 * x + 2 * y + c
        gath[me] = s_ref[...]
        cps = []
        for rel in range(1, N_DEV):
            px = 1 - x if rel & 4 else x
            py = 1 - y if rel & 2 else y
            pc = 1 - c if rel & 1 else c
            cp = pltpu.make_async_remote_copy(src_ref=s_ref, dst_ref=gath.at[me], send_sem=send_sems.at[rel - 1],
                                              recv_sem=recv_sems.at[rel - 1], device_id=(px, py, pc), device_id_type=MESH_T)
            cp.start()
            cps.append(cp)
        for cp in cps:
            cp.wait()
        acc = gath[0]
        for d in range(1, N_DEV):
            acc = acc + gath[d]
        o_ref[...] = acc

    vm = pl.BlockSpec(memory_space=pltpu.VMEM)
    return pl.pallas_call(
        body, name="allreduce_small", in_specs=[vm], out_specs=vm, out_shape=_sds((r, LANE)),
        scratch_shapes=[pltpu.VMEM((N_DEV, r, LANE), F32), pltpu.SemaphoreType.DMA((N_DEV - 1,)),
                        pltpu.SemaphoreType.DMA((N_DEV - 1,))],
    )(slab)


def _add_mine(g4, recv, half):
    _, _, rh, c = g4.shape

    def body(h_ref, g_ref, r_ref, o_ref):
        o_ref[0] = (g_ref[0, 0] + r_ref[0]).astype(o_ref.dtype)

    return pl.pallas_call(
        body, name="add_mine",
        grid_spec=pltpu.PrefetchScalarGridSpec(
            num_scalar_prefetch=1, grid=(N_CHIPS,),
            in_specs=[pl.BlockSpec((1, 1, rh, c), lambda j, h: (j, h[0], 0, 0)), pl.BlockSpec((1, rh, c), lambda j, h: (j, 0, 0))],
            out_specs=pl.BlockSpec((1, rh, c), lambda j, h: (j, 0, 0))),
        out_shape=_sds((N_CHIPS, rh, c), PAYLOAD),
        compiler_params=_params(("parallel",)),
    )(half, g4, recv)


def _add_chips(e, p, me):
    _, rh, c = e.shape

    def body(m_ref, e_ref, p_ref, o_ref):
        own = p_ref[0].astype(F32)
        acc = None
        for s in range(N_CHIPS):
            t = jnp.where(m_ref[0] == s, own, e_ref[s].astype(F32))
            acc = t if acc is None else acc + t
        o_ref[...] = acc

    return pl.pallas_call(
        body, name="add_chips",
        grid_spec=pltpu.PrefetchScalarGridSpec(
            num_scalar_prefetch=1, grid=(1,),
            in_specs=[pl.BlockSpec((N_CHIPS, rh, c), lambda i, m: (0, 0, 0)), pl.BlockSpec((1, rh, c), lambda i, m: (m[0], 0, 0))],
            out_specs=pl.BlockSpec((rh, c), lambda i, m: (0, 0))),
        out_shape=_sds((rh, c)),
        compiler_params=_params(("arbitrary",)),
    )(me, e, p)


def _rs_begin(gs, split, tag):
    g4 = [g.reshape(N_CHIPS, 2, g.shape[0] // (2 * N_CHIPS), g.shape[1]) for g in gs]
    h = _xchg_begin(f"rs_swap_{tag}", _swap_plan, len(gs), g4, [_sds((N_CHIPS,) + g.shape[2:]) for g in g4], split)
    return dict(h=h, split=split, tag=tag, shapes=[g.shape for g in gs])


def _rs_add_mine(st, after):
    g4, recv = _xchg_end(st["h"], after)
    half = jnp.reshape(lax.axis_index("c"), (1,)).astype(jnp.int32)
    ps = [_add_mine(g, r, half) for g, r in zip(g4, recv)]
    st["h"] = _xchg_begin(f"rs_chips_{st['tag']}", _chips_plan, 3 * len(ps), ps, [_sds(p.shape, p.dtype) for p in ps], st["split"])
    return st


def _rs_add_chips(st, after):
    ps, es = _xchg_end(st["h"], after)
    me = jnp.reshape(2 * lax.axis_index("x") + lax.axis_index("y"), (1,)).astype(jnp.int32)
    fs = [_add_chips(e, p, me) for e, p in zip(es, ps)]
    st["h"] = _xchg_begin(f"rs_share_{st['tag']}", _share_plan, len(fs), fs, [_sds((2,) + f.shape) for f in fs], st["split"])
    return st


def _rs_end(st, after):
    fs, ss = _xchg_end(st["h"], after)
    c = lax.axis_index("c")
    return [lax.dynamic_update_index_in_dim(s, f, c, 0).reshape(shp[0] // N_CHIPS, shp[1])
            for s, f, shp in zip(ss, fs, st["shapes"])]


WEIGHTS = ["norm_g", "w_in", "conv_a_w", "ssd_conv_w", "ssd_conv_b", "ssd_dt_bias", "ssd_a_log", "ssd_d", "ssd_norm_g",
           "mla_q_norm_g", "w_qb", "mla_kv_norm_g", "w_kvb", "w_out", "final_norm_g"]
BIG = ["w_in", "w_qb", "w_kvb", "w_out"]
SLAB_ROWS = 128
SMALL_ROWS = 72


def _to_slab(parts, rows):
    flat = jnp.concatenate([p.reshape(-1) for p in parts])
    return jnp.pad(flat, (0, rows * LANE - flat.shape[0])).reshape(rows, LANE)


def _from_slab(slab, shapes):
    flat = slab.reshape(-1)
    out, off = [], 0
    for shp in shapes:
        n = int(np.prod(shp))
        out.append(flat[off:off + n].reshape(shp))
        off += n
    return out


def kernel(x, positions, norm_g, w_in, conv_a_w, ssd_conv_w, ssd_conv_b, ssd_dt_bias, ssd_a_log, ssd_d, ssd_norm_g, mla_q_norm_g, w_qb, mla_kv_norm_g, w_kvb, w_out, final_norm_g, loss_target, m_norm_g, m_w_in, m_conv_a_w, m_ssd_conv_w, m_ssd_conv_b, m_ssd_dt_bias, m_ssd_a_log, m_ssd_d, m_ssd_norm_g, m_mla_q_norm_g, m_w_qb, m_mla_kv_norm_g, m_w_kvb, m_w_out, m_final_norm_g, v_norm_g, v_w_in, v_conv_a_w, v_ssd_conv_w, v_ssd_conv_b, v_ssd_dt_bias, v_ssd_a_log, v_ssd_d, v_ssd_norm_g, v_mla_q_norm_g, v_w_qb, v_mla_kv_norm_g, v_w_kvb, v_w_out, v_final_norm_g):
    w = dict(norm_g=norm_g, w_in=w_in, conv_a_w=conv_a_w, ssd_conv_w=ssd_conv_w, ssd_conv_b=ssd_conv_b,
             ssd_dt_bias=ssd_dt_bias, ssd_a_log=ssd_a_log, ssd_d=ssd_d, ssd_norm_g=ssd_norm_g, mla_q_norm_g=mla_q_norm_g,
             w_qb=w_qb, mla_kv_norm_g=mla_kv_norm_g, w_kvb=w_kvb, w_out=w_out, final_norm_g=final_norm_g)
    mom = dict(norm_g=m_norm_g, w_in=m_w_in, conv_a_w=m_conv_a_w, ssd_conv_w=m_ssd_conv_w, ssd_conv_b=m_ssd_conv_b,
               ssd_dt_bias=m_ssd_dt_bias, ssd_a_log=m_ssd_a_log, ssd_d=m_ssd_d, ssd_norm_g=m_ssd_norm_g,
               mla_q_norm_g=m_mla_q_norm_g, w_qb=m_w_qb, mla_kv_norm_g=m_mla_kv_norm_g, w_kvb=m_w_kvb, w_out=m_w_out,
               final_norm_g=m_final_norm_g)
    var = dict(norm_g=v_norm_g, w_in=v_w_in, conv_a_w=v_conv_a_w, ssd_conv_w=v_ssd_conv_w, ssd_conv_b=v_ssd_conv_b,
               ssd_dt_bias=v_ssd_dt_bias, ssd_a_log=v_ssd_a_log, ssd_d=v_ssd_d, ssd_norm_g=v_ssd_norm_g,
               mla_q_norm_g=v_mla_q_norm_g, w_qb=v_w_qb, mla_kv_norm_g=v_mla_kv_norm_g, w_kvb=v_w_kvb, w_out=v_w_out,
               final_norm_g=v_final_norm_g)
    chip = 2 * lax.axis_index("x") + lax.axis_index("y")

    conv_pack = jnp.zeros((DEPTH, 8, 256), F32)
    conv_pack = conv_pack.at[:, 0:3, 0:64].set(conv_a_w).at[:, 3:7, 0:224].set(ssd_conv_w)
    w_in_t, w_out_t = _perm_cols(w_in).astype(MXU), w_out.astype(MXU)
    w_qb_t, w_kvb_t = jnp.swapaxes(w_qb, 1, 2).astype(MXU), jnp.swapaxes(w_kvb, 1, 2).astype(MXU)
    shards = [[w_in_t[l], w_out_t[l], w_qb_t[l], w_kvb_t[l], conv_pack[l]] for l in range(DEPTH)]

    def full_weights(l, gathered):
        g_in, g_out, g_qb, g_kvb, g_conv = gathered
        return dict(
            norm_g=norm_g[l][None], w_in=g_in.reshape(D_MODEL, NCOL),
            conv_a_w=jnp.concatenate([g_conv[j, 0:3, 0:64] for j in range(N_CHIPS)], axis=1),
            ssd_conv_w=jnp.concatenate([g_conv[j, 3:7, 0:224] for j in range(N_CHIPS)], axis=1),
            ssd_conv_b=ssd_conv_b[l][None], sc=_ssd_scalars(ssd_dt_bias[l], ssd_a_log[l], ssd_d[l]),
            g_ssd=ssd_norm_g[l][None], gq=mla_q_norm_g[l][None], gkv=mla_kv_norm_g[l][None],
            wq=_wq_layout(g_qb.reshape(MLA_HEADS * 96, Q_LORA)), wkv=_wkv_layout(g_kvb.reshape(MLA_HEADS * LANE, KV_LORA)),
            w_out=g_out.reshape(D_MODEL, D_MODEL))

    def large_grads(g):
        wq = jnp.pad(_wq_unlayout(g["wq"]).reshape(N_CHIPS, 144, Q_LORA), ((0, 0), (0, 16), (0, 0)))
        return [g["w_in"], g["w_out"], wq.reshape(N_CHIPS * 160, Q_LORA), _wkv_unlayout(g["wkv"])]

    cos, sin = _rope_tables(positions[0])
    lw0 = full_weights(0, _gather_end(_gather_begin(shards[0], False, 0), None))
    gather1 = _gather_begin(shards[1], True, 1)
    x1, sv0 = _layer_fwd(x[0], lw0, cos, sin, gather1["token"])
    lw1 = full_weights(1, _gather_end(gather1, x1))
    x2, sv1 = _layer_fwd(x1, lw1, cos, sin)
    dx, dgf, loss = _loss_head(x2, final_norm_g[None], loss_target[0])

    dx, lg1, _ = _layer_bwd(dx, lw1, sv1, cos, sin)
    grad_x, lg0, red1 = _layer_bwd(dx, lw0, sv0, cos, sin, _rs_begin(large_grads(lg1), True, 1))
    red0 = _rs_end(_rs_add_chips(_rs_add_mine(_rs_begin(large_grads(lg0), False, 0), None), None), None)
    lg = [lg0, lg1]
    r_in, r_out, r_qb, r_kvb = [jnp.stack([a, b]) for a, b in zip(red0, red1)]
    grad = dict(w_in=_unperm_cols(r_in), w_out=r_out, w_qb=jnp.swapaxes(r_qb[:, :144], 1, 2), w_kvb=jnp.swapaxes(r_kvb, 1, 2))

    small_names = ["norm_g", "conv_a_w", "ssd_conv_w", "ssd_conv_b", "sc", "g_ssd", "gq", "gkv"]
    parts = [loss[0, 0:1], dgf]
    for l in range(DEPTH):
        parts += [lg[l][nm][:3, DT_LANE:DT_LANE + SSD_HEADS] if nm == "sc" else lg[l][nm] for nm in small_names]
    shapes = [(1,), (D_MODEL,)] + [(D_MODEL,), (3, D_CONV_A), (4, N_XBC), (N_XBC,), (3, SSD_HEADS), (D_SSD,), (Q_LORA,), (KV_LORA,)] * DEPTH
    red = _from_slab(_allreduce_small(_to_slab(parts, SLAB_ROWS)), shapes)
    loss_out = red[0][0]
    grad["final_norm_g"] = red[1]
    per = [red[2 + 8 * l:10 + 8 * l] for l in range(DEPTH)]
    grad["norm_g"] = jnp.stack([per[l][0] for l in range(DEPTH)])
    grad["conv_a_w"] = lax.dynamic_slice_in_dim(jnp.stack([per[l][1] for l in range(DEPTH)]), chip * 64, 64, axis=2)
    grad["ssd_conv_w"] = lax.dynamic_slice_in_dim(jnp.stack([per[l][2] for l in range(DEPTH)]), chip * 224, 224, axis=2)
    grad["ssd_conv_b"] = jnp.stack([per[l][3] for l in range(DEPTH)])
    grad["ssd_dt_bias"] = jnp.stack([per[l][4][0] for l in range(DEPTH)])
    grad["ssd_a_log"] = jnp.stack([per[l][4][1] for l in range(DEPTH)])
    grad["ssd_d"] = jnp.stack([per[l][4][2] for l in range(DEPTH)])
    grad["ssd_norm_g"] = jnp.stack([per[l][5] for l in range(DEPTH)])
    grad["mla_q_norm_g"] = jnp.stack([per[l][6] for l in range(DEPTH)])
    grad["mla_kv_norm_g"] = jnp.stack([per[l][7] for l in range(DEPTH)])

    delta, new_m, new_v = {}, {}, {}
    for nm in BIG:
        shp = w[nm].shape
        two_d = (shp[0] * shp[1], shp[2])
        d, mo, vo = _adamw(w[nm].reshape(two_d), grad[nm].reshape(two_d), mom[nm].reshape(two_d), var[nm].reshape(two_d))
        delta[nm], new_m[nm], new_v[nm] = d.reshape(shp), mo.reshape(shp), vo.reshape(shp)
    small = [nm for nm in WEIGHTS if nm not in BIG]
    sshapes = [w[nm].shape for nm in small]
    d, mo, vo = _adamw(_to_slab([w[nm] for nm in small], SMALL_ROWS), _to_slab([grad[nm] for nm in small], SMALL_ROWS),
                       _to_slab([mom[nm] for nm in small], SMALL_ROWS), _to_slab([var[nm] for nm in small], SMALL_ROWS))
    for nm, dv, mv, vv in zip(small, _from_slab(d, sshapes), _from_slab(mo, sshapes), _from_slab(vo, sshapes)):
        delta[nm], new_m[nm], new_v[nm] = dv, mv, vv

    return (loss_out, grad_x[None], *[grad[nm] for nm in WEIGHTS], *[delta[nm] for nm in WEIGHTS],
            *[new_m[nm] for nm in WEIGHTS], *[new_v[nm] for nm in WEIGHTS])
```

```python
import functools
import math

import numpy as np
import jax
import jax.numpy as jnp
from jax import lax
from jax.experimental import pallas as pl
from jax.experimental.pallas import tpu as pltpu

F32 = jnp.float32
MXU = jnp.bfloat16

D_MODEL = 1024
DEPTH = 2
D_CONV_A = 256
D_SSD = 384
SSD_HEADS = 6
SSD_BC = 256
SSD_CHUNK = 128
SSD_NORM_EPS = 1e-5
MLA_HEADS = 6
Q_LORA = 256
KV_LORA = 128
QK_NOPE = 64
QK_ROPE = 32
V_DIM = 64
D_MLA = 384
ROPE_BASE = 10000.0
NORM_EPS = 1e-6
IN_COLS = 3110
LANE = 128

O_AH, O_AB, O_AC, O_AZ = 0, 256, 512, 768
O_SZ = 1024
O_XBC = 1408
O_CQA = 2304
O_CKV = 2560
O_CZ = 2688
O_TAIL = 3072
NCOL = 3200
N_XBC = D_SSD + 2 * SSD_BC
DT_LANE = 32
ROPE_LANE = 64

ADAM_LR, ADAM_B1, ADAM_B2, ADAM_EPS, ADAM_WD, ADAM_STEP = 0.001, 0.9, 0.999, 1e-08, 0.01, 10

VMEM_LIMIT = 56 * 1024 * 1024
MESH_T = pl.DeviceIdType.MESH


def _dot(a, b):
    return jnp.dot(a.astype(MXU), b.astype(MXU), preferred_element_type=F32)


def _dot_nt(a, b):
    return lax.dot_general(a.astype(MXU), b.astype(MXU), (((1,), (1,)), ((), ())), preferred_element_type=F32)


def _dot_tn(a, b):
    return lax.dot_general(a.astype(MXU), b.astype(MXU), (((0,), (0,)), ((), ())), preferred_element_type=F32)


def _dot_hi(a, b):
    return jnp.dot(a, b, precision=lax.Precision.HIGHEST, preferred_element_type=F32)


def _dot_hi_tn(a, b):
    return lax.dot_general(a, b, (((0,), (0,)), ((), ())), precision=lax.Precision.HIGHEST, preferred_element_type=F32)


def _sigmoid(z):
    return 1.0 / (1.0 + jnp.exp(-z))


def _silu(z):
    return z * _sigmoid(z)


def _dsilu(z):
    s = _sigmoid(z)
    return s * (1.0 + z * (1.0 - s))


def _softplus(z):
    e = jnp.exp(-jnp.abs(z))
    return jnp.maximum(z, 0.0) + jnp.where(e < 1e-3, e * (1.0 - 0.5 * e), jnp.log(1.0 + e))


def _iota(shape, dim):
    return lax.broadcasted_iota(jnp.int32, shape, dim)


def _shift_down(u, k):
    if k == 0:
        return u
    return jnp.where(_iota(u.shape, 0) >= k, pltpu.roll(u, k, 0), 0.0)


def _shift_up(u, k):
    if k == 0:
        return u
    n = u.shape[0]
    return jnp.where(_iota(u.shape, 0) < n - k, pltpu.roll(u, n - k, 0), 0.0)


def _rope_swap(t):
    lane = _iota(t.shape, 1)
    lo = (lane >= ROPE_LANE) & (lane < ROPE_LANE + 16)
    hi = (lane >= ROPE_LANE + 16) & (lane < ROPE_LANE + 32)
    return jnp.where(lo, pltpu.roll(t, LANE - 16, 1), jnp.where(hi, pltpu.roll(t, 16, 1), 0.0))


def _params(sem=None):
    return pltpu.CompilerParams(dimension_semantics=sem, vmem_limit_bytes=VMEM_LIMIT)


def _full(shape):
    nd = len(shape)
    return pl.BlockSpec(shape, lambda *_: (0,) * nd)


def _sds(shape, dtype=F32):
    return jax.ShapeDtypeStruct(shape, dtype)


def _tile(s):
    return min(256, s)


def _row(ts, w):
    return pl.BlockSpec((ts, w), lambda i: (i, 0))


def _col(s, off):
    return pl.BlockSpec((s, LANE), lambda j, _o=off // LANE: (0, _o + j))


def _call_after(dep, body, args, *, in_specs, **kw):
    if dep is None:
        return pl.pallas_call(body, in_specs=in_specs, **kw)(*args)
    n = len(args)

    def body_dep(*refs):
        body(*refs[:n], *refs[n + 1:])

    return pl.pallas_call(body_dep, in_specs=list(in_specs) + [pl.BlockSpec(memory_space=pl.ANY)], **kw)(*args, dep)


def _rms(c, g):
    r = lax.rsqrt(jnp.mean(c * c, axis=-1, keepdims=True) + NORM_EPS)
    return c * r * g, r


def _rms_bwd(dn, c, r, g):
    ch = c * r
    dch = dn * g
    dc = r * (dch - ch * jnp.mean(dch * ch, axis=-1, keepdims=True))
    return dc, jnp.sum(dn * ch, axis=0, keepdims=True)


def _inproj_fwd(x, g, w, dep=None):
    s = x.shape[0]
    ts = _tile(s)

    def body(x_ref, g_ref, w_ref, proj_ref, h_ref, r_ref):
        hn, r = _rms(x_ref[...], g_ref[...])
        h = hn.astype(MXU)
        h_ref[...] = h
        r_ref[...] = r
        proj_ref[...] = jnp.dot(h, w_ref[...], preferred_element_type=F32)

    return _call_after(
        dep, body, (x, g, w), name="inproj_fwd", grid=(s // ts,),
        in_specs=[_row(ts, D_MODEL), _full((1, D_MODEL)), _full((D_MODEL, NCOL))],
        out_specs=[_row(ts, NCOL), _row(ts, D_MODEL), _row(ts, 1)],
        out_shape=[_sds((s, NCOL)), _sds((s, D_MODEL), MXU), _sds((s, 1))],
        compiler_params=_params(("parallel",)),
    )


def _conva_fwd(proj, w):
    s = proj.shape[0]

    def body(h_ref, b_ref, c_ref, z_ref, w_ref, y_ref):
        u = c_ref[...] * h_ref[...]
        wv = w_ref[...]
        cv = wv[2:3, :] * u + wv[1:2, :] * _shift_down(u, 1) + wv[0:1, :] * _shift_down(u, 2)
        y_ref[...] = b_ref[...] * cv * _silu(z_ref[...])

    return pl.pallas_call(
        body, name="conva_fwd", grid=(D_CONV_A // LANE,),
        in_specs=[_col(s, O_AH), _col(s, O_AB), _col(s, O_AC), _col(s, O_AZ), pl.BlockSpec((3, LANE), lambda j: (0, j))],
        out_specs=pl.BlockSpec((s, LANE), lambda j: (0, j)),
        out_shape=_sds((s, D_CONV_A)),
        compiler_params=_params(("parallel",)),
    )(proj, proj, proj, proj, w)


def _sconv_pre(u, wv, bv):
    return (wv[3:4, :] * u + wv[2:3, :] * _shift_down(u, 1) + wv[1:2, :] * _shift_down(u, 2)
            + wv[0:1, :] * _shift_down(u, 3) + bv)


def _sconv_fwd(proj, w, b):
    s = proj.shape[0]

    def body(u_ref, w_ref, b_ref, o_ref):
        o_ref[...] = _silu(_sconv_pre(u_ref[...], w_ref[...], b_ref[...]))

    return pl.pallas_call(
        body, name="sconv_fwd", grid=(N_XBC // LANE,),
        in_specs=[_col(s, O_XBC), pl.BlockSpec((4, LANE), lambda j: (0, j)), pl.BlockSpec((1, LANE), lambda j: (0, j))],
        out_specs=pl.BlockSpec((s, LANE), lambda j: (0, j)),
        out_shape=_sds((s, N_XBC)),
        compiler_params=_params(("parallel",)),
    )(proj, w, b)


def _ssd_chunk_common(tail, sc):
    l = SSD_CHUNK
    lane = _iota((l, LANE), 1)
    row = _iota((l, LANE), 0)
    tri = (row >= lane).astype(F32)
    a_row = -jnp.exp(sc[1:2, :])
    pre = tail + sc[0:1, :]
    dt = _softplus(pre)
    a_cs = _dot_hi(tri, dt * a_row)
    return lane, row, tri, a_row, pre, dt, a_cs, a_cs.T


def _pick_col(m, lane, k):
    return jnp.sum(jnp.where(lane == k, m, 0.0), axis=1, keepdims=True)


def _pick_row(m, row, k):
    return jnp.sum(jnp.where(row == k, m, 0.0), axis=0, keepdims=True)


def _ssd_fwd(xbc, proj, sc):
    s = xbc.shape[0]
    nc = s // SSD_CHUNK
    l = SSD_CHUNK

    def body(xbc_ref, tail_ref, sc_ref, y_ref, st_ref, state):
        @pl.when(pl.program_id(0) == 0)
        def _():
            state[...] = jnp.zeros_like(state)

        sc_v = sc_ref[...]
        lane, row, _, _, _, dt, a_cs, a_t = _ssd_chunk_common(tail_ref[...], sc_v)
        lane1 = _iota((1, LANE), 1)
        rowp = _iota((LANE, 1), 0)
        d_row = sc_v[2:3, :]
        for j in range(3):
            st_ref[0, j] = state[j]
        for j in range(3):
            xpair = xbc_ref[:, LANE * j:LANE * (j + 1)]
            sp = state[j]
            ypair = jnp.zeros((l, LANE), F32)
            new_s = jnp.zeros((LANE, LANE), F32)
            decay = jnp.zeros((LANE, 1), F32)
            for half in range(2):
                h = 2 * j + half
                g = h // 3
                hm = (lane < 64) if half == 0 else (lane >= 64)
                hrow = (rowp < 64) if half == 0 else (rowp >= 64)
                ac = _pick_col(a_cs, lane, DT_LANE + h)
                ar = _pick_row(a_t, row, DT_LANE + h)
                dtc = _pick_col(dt, lane, DT_LANE + h)
                alast = jnp.sum(jnp.where(lane1 == l - 1, ar, 0.0), axis=1, keepdims=True)
                dh = jnp.sum(jnp.where(lane1 == DT_LANE + h, d_row, 0.0), axis=1, keepdims=True)
                xm = jnp.where(hm, xpair, 0.0)
                xd = xm * dtc
                bm = xbc_ref[:, D_SSD + LANE * g:D_SSD + LANE * (g + 1)]
                cm = xbc_ref[:, D_SSD + SSD_BC + LANE * g:D_SSD + SSD_BC + LANE * (g + 1)]
                lm = jnp.where(row >= lane, jnp.exp(jnp.minimum(ac - ar, 0.0)), 0.0)
                y_diag = _dot(_dot_nt(cm, bm) * lm, xd)
                y_off = jnp.where(hm, _dot_nt(cm, sp), 0.0) * jnp.exp(ac)
                ypair = ypair + y_diag + y_off + xm * dh
                new_s = new_s + _dot_tn(xd * jnp.exp(alast - ac), bm)
                decay = jnp.where(hrow, jnp.exp(alast), decay)
            state[j] = sp * decay + new_s
            y_ref[:, LANE * j:LANE * (j + 1)] = ypair

    return pl.pallas_call(
        body, name="ssd_fwd", grid=(nc,),
        in_specs=[pl.BlockSpec((l, N_XBC), lambda c: (c, 0)),
                  pl.BlockSpec((l, LANE), lambda c: (c, O_TAIL // LANE)), _full((8, LANE))],
        out_specs=[pl.BlockSpec((l, D_SSD), lambda c: (c, 0)), pl.BlockSpec((1, 3, LANE, LANE), lambda c: (c, 0, 0, 0))],
        out_shape=[_sds((s, D_SSD)), _sds((nc, 3, LANE, LANE))],
        scratch_shapes=[pltpu.VMEM((3, LANE, LANE), F32)],
        compiler_params=_params(("arbitrary",)),
    )(xbc, proj, sc)


def _mla_prep_fwd(proj, gq, gkv, wq, wkv, cos, sin):
    s = proj.shape[0]
    ts = _tile(s)
    nh = MLA_HEADS

    def body(cqa_ref, ckv_ref, tail_ref, gq_ref, gkv_ref, wq_ref, wkv_ref, cos_ref, sin_ref,
             q_ref, k_ref, v_ref, qn_ref, kvn_ref, rq_ref, rkv_ref):
        qn, rq = _rms(cqa_ref[...], gq_ref[...])
        kvn, rkv = _rms(ckv_ref[...], gkv_ref[...])
        qn = qn.astype(MXU)
        kvn = kvn.astype(MXU)
        qn_ref[...] = qn
        kvn_ref[...] = kvn
        rq_ref[...] = rq
        rkv_ref[...] = rkv
        q = _dot_nt(qn, wq_ref[...])
        kv = _dot_nt(kvn, wkv_ref[...])
        cosv = cos_ref[...]
        sinv = sin_ref[...]
        lane = _iota((ts, LANE), 1)
        rope_lanes = (lane >= ROPE_LANE) & (lane < ROPE_LANE + QK_ROPE)
        kr = jnp.where(rope_lanes, pltpu.roll(tail_ref[...], ROPE_LANE, 1), 0.0)
        kr = kr * cosv + _rope_swap(kr) * sinv
        for h in range(nh):
            qh = q[:, LANE * h:LANE * (h + 1)]
            q_ref[h] = ((qh * cosv + _rope_swap(qh) * sinv) * ATT_SCALE).astype(MXU)
            k_ref[h] = (kv[:, LANE * h:LANE * (h + 1)] + kr).astype(MXU)
            v_ref[h] = kv[:, LANE * (nh + h):LANE * (nh + h + 1)].astype(MXU)

    head = pl.BlockSpec((nh, ts, LANE), lambda i: (0, i, 0))
    return pl.pallas_call(
        body, name="mla_prep_fwd", grid=(s // ts,),
        in_specs=[pl.BlockSpec((ts, Q_LORA), lambda i: (i, O_CQA // Q_LORA)),
                  pl.BlockSpec((ts, KV_LORA), lambda i: (i, O_CKV // KV_LORA)),
                  pl.BlockSpec((ts, LANE), lambda i: (i, O_TAIL // LANE)),
                  _full((1, Q_LORA)), _full((1, KV_LORA)), _full((nh * LANE, Q_LORA)), _full((2 * nh * LANE, KV_LORA)),
                  _row(ts, LANE), _row(ts, LANE)],
        out_specs=[head, head, head, _row(ts, Q_LORA), _row(ts, KV_LORA), _row(ts, 1), _row(ts, 1)],
        out_shape=[_sds((nh, s, LANE), MXU)] * 3 + [_sds((s, Q_LORA), MXU), _sds((s, KV_LORA), MXU), _sds((s, 1)), _sds((s, 1))],
        compiler_params=_params(("parallel",)),
    )(proj, proj, proj, gq, gkv, wq, wkv, cos, sin)


ATT_SCALE = (QK_NOPE + QK_ROPE) ** -0.5
NEG = -1e30


def _att_tile(s):
    return min(256, s // 2)


def _attn_fwd(q, k, v):
    nh, s, _ = q.shape
    tq = _att_tile(s)
    nq = s // tq

    def body(q_ref, k_ref, v_ref, o_ref, lse_ref):
        i = pl.program_id(1)
        rowi = _iota((tq, tq), 0)
        coli = _iota((tq, tq), 1)
        zero = (jnp.full((tq, 1), NEG, F32), jnp.zeros((tq, 1), F32), jnp.zeros((tq, LANE), F32))
        state = [zero, zero]
        done = [zero, zero]
        for t in range(nq + 1):
            first = t <= i
            qblk = jnp.where(first, i, nq - 1 - i)
            kblk = jnp.where(first, t, t - i - 1)
            qoff = pl.multiple_of(qblk * tq, tq)
            koff = pl.multiple_of(kblk * tq, tq)
            keep = coli <= rowi + jnp.where(kblk == qblk, 0, tq)
            restart = t == i + 1
            for hh in range(2):
                m, lsum, acc = state[hh]
                if t > 0:
                    done[hh] = tuple(jnp.where(restart, a, b) for a, b in zip(state[hh], done[hh]))
                    m = jnp.where(restart, NEG, m)
                    lsum = jnp.where(restart, 0.0, lsum)
                    acc = jnp.where(restart, 0.0, acc)
                sc = _dot_nt(q_ref[hh, pl.ds(qoff, tq), :], k_ref[hh, pl.ds(koff, tq), :])
                sc = jnp.where(keep, sc, NEG)
                m_new = jnp.maximum(m, jnp.max(sc, axis=1, keepdims=True))
                p = jnp.exp(sc - m_new)
                alpha = jnp.exp(m - m_new)
                lsum = alpha * lsum + jnp.sum(p, axis=1, keepdims=True)
                acc = alpha * acc + _dot(p, v_ref[hh, pl.ds(koff, tq), :])
                state[hh] = (m_new, lsum, acc)
        for blk, res in ((i, done), (nq - 1 - i, state)):
            off = pl.multiple_of(blk * tq, tq)
            out = None
            for hh in range(2):
                m, lsum, acc = res[hh]
                o = acc * (1.0 / lsum)
                lse_ref[hh, pl.ds(off, tq), :] = m + jnp.log(lsum)
                out = o if hh == 0 else out + pltpu.roll(o, V_DIM, 1)
            o_ref[pl.ds(off, tq), :] = out

    pair = pl.BlockSpec((2, s, LANE), lambda j, i: (j, 0, 0))
    return pl.pallas_call(
        body, name="attn_fwd", grid=(nh // 2, nq // 2),
        in_specs=[pair, pair, pair],
        out_specs=[pl.BlockSpec((s, LANE), lambda j, i: (0, j)), pl.BlockSpec((2, s, 1), lambda j, i: (j, 0, 0))],
        out_shape=[_sds((s, D_MLA)), _sds((nh, s, 1))],
        compiler_params=_params(("parallel", "arbitrary")),
    )(q, k, v)


def _ssd_gate(y_ssd, s_z, g):
    yz = y_ssd * _silu(s_z)
    g0 = _iota(yz.shape, 1) < D_SSD // 2
    sq = yz * yz
    ms0 = jnp.sum(jnp.where(g0, sq, 0.0), axis=1, keepdims=True) / (D_SSD // 2)
    ms1 = jnp.sum(jnp.where(g0, 0.0, sq), axis=1, keepdims=True) / (D_SSD // 2)
    r = jnp.where(g0, lax.rsqrt(ms0 + SSD_NORM_EPS), lax.rsqrt(ms1 + SSD_NORM_EPS))
    nrm = yz * r
    return nrm * g, nrm, r, g0


def _outproj_fwd(x, proj, ya, y_ssd, o, g_ssd, w):
    s = x.shape[0]
    ts = _tile(s)

    def body(x_ref, p_ref, ya_ref, ys_ref, o_ref, g_ref, w_ref, xo_ref, y_ref):
        yb = _ssd_gate(ys_ref[...], p_ref[:, O_SZ:O_SZ + D_SSD], g_ref[...])[0]
        yc = o_ref[...] * _silu(p_ref[:, O_CZ:O_CZ + D_MLA])
        y = jnp.concatenate([ya_ref[...], yb, yc], axis=1).astype(MXU)
        y_ref[...] = y
        xo_ref[...] = x_ref[...] + jnp.dot(y, w_ref[...], preferred_element_type=F32)

    return pl.pallas_call(
        body, name="outproj_fwd", grid=(s // ts,),
        in_specs=[_row(ts, D_MODEL), _row(ts, NCOL), _row(ts, D_CONV_A), _row(ts, D_SSD), _row(ts, D_MLA),
                  _full((1, D_SSD)), _full((D_MODEL, D_MODEL))],
        out_specs=[_row(ts, D_MODEL), _row(ts, D_MODEL)],
        out_shape=[_sds((s, D_MODEL)), _sds((s, D_MODEL), MXU)],
        compiler_params=_params(("parallel",)),
    )(x, proj, ya, y_ssd, o, g_ssd, w)


def _loss_head(x, g, tgt):
    s = x.shape[0]
    ts = _tile(s)

    def body(x_ref, g_ref, t_ref, dx_ref, dg_ref, loss_ref):
        @pl.when(pl.program_id(0) == 0)
        def _():
            dg_ref[...] = jnp.zeros_like(dg_ref)
            loss_ref[...] = jnp.zeros_like(loss_ref)

        xv = x_ref[...]
        gv = g_ref[...]
        yn, r = _rms(xv, gv)
        e = yn - t_ref[...]
        loss_ref[...] += jnp.sum(jnp.sum(e * e, axis=1, keepdims=True), axis=0, keepdims=True) * (0.5 / D_MODEL)
        dx, dg = _rms_bwd(e * (1.0 / D_MODEL), xv, r, gv)
        dx_ref[...] = dx
        dg_ref[...] += dg

    return pl.pallas_call(
        body, name="loss_head", grid=(s // ts,),
        in_specs=[_row(ts, D_MODEL), _full((1, D_MODEL)), _row(ts, D_MODEL)],
        out_specs=[_row(ts, D_MODEL), _full((1, D_MODEL)), _full((1, LANE))],
        out_shape=[_sds((s, D_MODEL)), _sds((1, D_MODEL)), _sds((1, LANE))],
        compiler_params=_params(("arbitrary",)),
    )(x, g, tgt)


def _outproj_bwd(dout, y, w, proj, y_ssd, o, g_ssd, dep=None):
    s = dout.shape[0]
    ts = _tile(s)

    def body(dout_ref, y_ref, w_ref, p_ref, ys_ref, o_ref, g_ref,
             dya_ref, dys_ref, dsz_ref, dattn_ref, dcz_ref, dg_ref, dw_ref):
        @pl.when(pl.program_id(0) == 0)
        def _():
            dw_ref[...] = jnp.zeros_like(dw_ref)
            dg_ref[...] = jnp.zeros_like(dg_ref)

        dout_b = dout_ref[...].astype(MXU)
        dw_ref[...] += _dot_tn(y_ref[...], dout_b)
        dy = _dot_nt(dout_b, w_ref[...])
        dya_ref[...] = dy[:, :D_CONV_A]
        dyb = dy[:, D_CONV_A:D_CONV_A + D_SSD]
        sz = p_ref[:, O_SZ:O_SZ + D_SSD]
        ys = ys_ref[...]
        gv = g_ref[...]
        _, nrm, r, g0 = _ssd_gate(ys, sz, gv)
        dg_ref[...] += jnp.sum(dyb * nrm, axis=0, keepdims=True)
        dn = dyb * gv
        t = dn * nrm
        mean = jnp.where(g0, jnp.sum(jnp.where(g0, t, 0.0), axis=1, keepdims=True),
                         jnp.sum(jnp.where(g0, 0.0, t), axis=1, keepdims=True)) / (D_SSD // 2)
        dyz = r * (dn - nrm * mean)
        dys_ref[...] = dyz * _silu(sz)
        dsz_ref[...] = dyz * ys * _dsilu(sz)
        dyc = dy[:, D_CONV_A + D_SSD:]
        cz = p_ref[:, O_CZ:O_CZ + D_MLA]
        dattn_ref[...] = dyc * _silu(cz)
        dcz_ref[...] = dyc * o_ref[...] * _dsilu(cz)

    return _call_after(
        dep, body, (dout, y, w, proj, y_ssd, o, g_ssd), name="outproj_bwd", grid=(s // ts,),
        in_specs=[_row(ts, D_MODEL), _row(ts, D_MODEL), _full((D_MODEL, D_MODEL)), _row(ts, NCOL), _row(ts, D_SSD),
                  _row(ts, D_MLA), _full((1, D_SSD))],
        out_specs=[_row(ts, D_CONV_A), _row(ts, D_SSD), _row(ts, D_SSD), _row(ts, D_MLA), _row(ts, D_MLA),
                   _full((1, D_SSD)), _full((D_MODEL, D_MODEL))],
        out_shape=[_sds((s, D_CONV_A)), _sds((s, D_SSD)), _sds((s, D_SSD)), _sds((s, D_MLA)), _sds((s, D_MLA)),
                   _sds((1, D_SSD)), _sds((D_MODEL, D_MODEL))],
        compiler_params=_params(("arbitrary",)),
    )


def _attn_bwd(q, k, v, o, d_o, lse, dep=None):
    nh, s, _ = q.shape
    tq = _att_tile(s)
    nq = s // tq

    def body(q_ref, k_ref, v_ref, o_ref, do_ref, lse_ref, dq_ref, dk_ref, dv_ref, dop, delta):
        i = pl.program_id(1)

        @pl.when(i == 0)
        def _():
            lane = _iota((s, LANE), 1)
            for hh in range(2):
                dov = do_ref[...]
                ov = o_ref[...]
                if hh == 1:
                    dov = pltpu.roll(dov, V_DIM, 1)
                    ov = pltpu.roll(ov, V_DIM, 1)
                dov = jnp.where(lane < V_DIM, dov, 0.0)
                dop[hh] = dov.astype(MXU)
                delta[hh] = jnp.sum(dov * ov, axis=1, keepdims=True)
                dq_ref[hh] = jnp.zeros((s, LANE), F32)

        rowi = _iota((tq, tq), 0)
        coli = _iota((tq, tq), 1)
        z = jnp.zeros((tq, LANE), F32)
        state = [(z, z), (z, z)]
        done = [(z, z), (z, z)]
        for t in range(nq + 1):
            first = t <= nq - 1 - i
            kblk = jnp.where(first, i, nq - 1 - i)
            qblk = jnp.where(first, i + t, t - 1)
            qoff = pl.multiple_of(qblk * tq, tq)
            koff = pl.multiple_of(kblk * tq, tq)
            keep = coli <= rowi + jnp.where(kblk == qblk, 0, tq)
            restart = t == nq - i
            for hh in range(2):
                dk, dv = state[hh]
                if t > 0:
                    done[hh] = tuple(jnp.where(restart, a, b) for a, b in zip(state[hh], done[hh]))
                    dk = jnp.where(restart, 0.0, dk)
                    dv = jnp.where(restart, 0.0, dv)
                kb = k_ref[hh, pl.ds(koff, tq), :]
                qb = q_ref[hh, pl.ds(qoff, tq), :]
                dob = dop[hh, pl.ds(qoff, tq), :]
                sc = jnp.where(keep, _dot_nt(qb, kb), NEG)
                p = jnp.exp(sc - lse_ref[hh, pl.ds(qoff, tq), :])
                dp = _dot_nt(dob, v_ref[hh, pl.ds(koff, tq), :])
                ds = p * (dp - delta[hh, pl.ds(qoff, tq), :])
                dq_ref[hh, pl.ds(qoff, tq), :] += _dot(ds, kb)
                state[hh] = (dk + _dot_tn(ds, qb), dv + _dot_tn(p, dob))
        for blk, res in ((i, done), (nq - 1 - i, state)):
            off = pl.multiple_of(blk * tq, tq)
            for hh in range(2):
                dk_ref[hh, pl.ds(off, tq), :] = res[hh][0]
                dv_ref[hh, pl.ds(off, tq), :] = res[hh][1]

    pair = pl.BlockSpec((2, s, LANE), lambda j, i: (j, 0, 0))
    return _call_after(
        dep, body, (q, k, v, o, d_o, lse), name="attn_bwd", grid=(nh // 2, nq // 2),
        in_specs=[pair, pair, pair, pl.BlockSpec((s, LANE), lambda j, i: (0, j)), pl.BlockSpec((s, LANE), lambda j, i: (0, j)),
                  pl.BlockSpec((2, s, 1), lambda j, i: (j, 0, 0))],
        out_specs=[pair, pair, pair],
        out_shape=[_sds((nh, s, LANE))] * 3,
        scratch_shapes=[pltpu.VMEM((2, s, LANE), MXU), pltpu.VMEM((2, s, 1), F32)],
        compiler_params=_params(("parallel", "arbitrary")),
    )


def _ssd_bwd(xbc, proj, sc, states, dy, dep=None):
    s = xbc.shape[0]
    nc = s // SSD_CHUNK
    l = SSD_CHUNK

    def body(xbc_ref, tail_ref, sc_ref, st_ref, dy_ref, dxbc_ref, dtail_ref, dsc_ref, dstate):
        @pl.when(pl.program_id(0) == 0)
        def _():
            dstate[...] = jnp.zeros_like(dstate)
            dsc_ref[...] = jnp.zeros_like(dsc_ref)

        sc_v = sc_ref[...]
        lane, row, tri, a_row, pre, dt, a_cs, a_t = _ssd_chunk_common(tail_ref[...], sc_v)
        lane1 = _iota((1, LANE), 1)
        rowp = _iota((LANE, 1), 0)
        rowl = _iota((l, 1), 0)
        d_row = sc_v[2:3, :]
        da_col = jnp.zeros((l, LANE), F32)
        da_row = jnp.zeros((LANE, l), F32)
        dt_x = jnp.zeros((l, LANE), F32)
        dd_row = jnp.zeros((1, LANE), F32)
        db = [jnp.zeros((l, LANE), F32), jnp.zeros((l, LANE), F32)]
        dc = [jnp.zeros((l, LANE), F32), jnp.zeros((l, LANE), F32)]
        for j in range(3):
            xpair = xbc_ref[:, LANE * j:LANE * (j + 1)]
            dypair = dy_ref[:, LANE * j:LANE * (j + 1)]
            sp = st_ref[0, j]
            dsp = dstate[j]
            dxpair = jnp.zeros((l, LANE), F32)
            ds_new = jnp.zeros((LANE, LANE), F32)
            decay = jnp.zeros((LANE, 1), F32)
            for half in range(2):
                h = 2 * j + half
                g = h // 3
                hm = (lane < 64) if half == 0 else (lane >= 64)
                hrow = (rowp < 64) if half == 0 else (rowp >= 64)
                ac = _pick_col(a_cs, lane, DT_LANE + h)
                ar = _pick_row(a_t, row, DT_LANE + h)
                dtc = _pick_col(dt, lane, DT_LANE + h)
                alast = jnp.sum(jnp.where(lane1 == l - 1, ar, 0.0), axis=1, keepdims=True)
                dh = jnp.sum(jnp.where(lane1 == DT_LANE + h, d_row, 0.0), axis=1, keepdims=True)
                xm = jnp.where(hm, xpair, 0.0)
                xd = xm * dtc
                dym = jnp.where(hm, dypair, 0.0)
                bm = xbc_ref[:, D_SSD + LANE * g:D_SSD + LANE * (g + 1)]
                cm = xbc_ref[:, D_SSD + SSD_BC + LANE * g:D_SSD + SSD_BC + LANE * (g + 1)]
                lm = jnp.where(row >= lane, jnp.exp(jnp.minimum(ac - ar, 0.0)), 0.0)
                e_in = jnp.exp(ac)
                f_out = jnp.exp(alast - ac)
                e_last = jnp.exp(alast)
                m = _dot_nt(cm, bm) * lm
                y_off = jnp.where(hm, _dot_nt(cm, sp), 0.0) * e_in
                dm = _dot_nt(dym, xd)
                dxd = _dot_tn(m, dym)
                dg = dm * lm
                dye = dym * e_in
                dc[g] = dc[g] + _dot(dg, bm) + _dot(dye, sp)
                db[g] = db[g] + _dot_tn(dg, cm)
                qm = dm * m
                dac = jnp.sum(qm, axis=1, keepdims=True) + jnp.sum(dym * y_off, axis=1, keepdims=True)
                dar = -jnp.sum(qm, axis=0, keepdims=True)
                dxf = jnp.where(hm, _dot_nt(bm, dsp), 0.0)
                db[g] = db[g] + _dot(xd * f_out, dsp)
                dxd = dxd + dxf * f_out
                df = jnp.sum(dxf * xd, axis=1, keepdims=True) * f_out
                dac = dac - df
                s_last = jnp.sum(df, axis=0, keepdims=True)
                ss = jnp.sum(jnp.where(hrow, dsp * sp, 0.0), axis=1, keepdims=True)
                s_last = s_last + e_last * jnp.sum(ss, axis=0, keepdims=True)
                dac = dac + jnp.where(rowl == l - 1, s_last, 0.0)
                ds_new = ds_new + _dot_tn(dye, cm)
                decay = jnp.where(hrow, e_last, decay)
                dxpair = dxpair + dxd * dtc + dym * dh
                dt_x = dt_x + jnp.where(lane == DT_LANE + h, jnp.sum(dxd * xm, axis=1, keepdims=True), 0.0)
                dsum = jnp.sum(jnp.sum(dym * xm, axis=1, keepdims=True), axis=0, keepdims=True)
                dd_row = dd_row + jnp.where(lane1 == DT_LANE + h, dsum, 0.0)
                da_col = da_col + jnp.where(lane == DT_LANE + h, dac, 0.0)
                da_row = da_row + jnp.where(row == DT_LANE + h, dar, 0.0)
            dstate[j] = dsp * decay + ds_new
            dxbc_ref[:, LANE * j:LANE * (j + 1)] = dxpair
        for g in range(2):
            dxbc_ref[:, D_SSD + LANE * g:D_SSD + LANE * (g + 1)] = db[g]
            dxbc_ref[:, D_SSD + SSD_BC + LANE * g:D_SSD + SSD_BC + LANE * (g + 1)] = dc[g]
        dla = _dot_hi_tn(tri, da_col + da_row.T)
        ddt = dt_x + dla * a_row
        dpre = ddt * _sigmoid(pre)
        dtm = (lane >= DT_LANE) & (lane < DT_LANE + SSD_HEADS)
        dtail_ref[...] = jnp.where(dtm, dpre, 0.0)
        dtm1 = (lane1 >= DT_LANE) & (lane1 < DT_LANE + SSD_HEADS)
        dsc_ref[0:1, :] += jnp.where(dtm1, jnp.sum(dpre, axis=0, keepdims=True), 0.0)
        dsc_ref[1:2, :] += jnp.where(dtm1, jnp.sum(dla * dt, axis=0, keepdims=True) * a_row, 0.0)
        dsc_ref[2:3, :] += dd_row

    rev = lambda c: nc - 1 - c
    return _call_after(
        dep, body, (xbc, proj, sc, states, dy), name="ssd_bwd", grid=(nc,),
        in_specs=[pl.BlockSpec((l, N_XBC), lambda c: (rev(c), 0)),
                  pl.BlockSpec((l, LANE), lambda c: (rev(c), O_TAIL // LANE)), _full((8, LANE)),
                  pl.BlockSpec((1, 3, LANE, LANE), lambda c: (rev(c), 0, 0, 0)),
                  pl.BlockSpec((l, D_SSD), lambda c: (rev(c), 0))],
        out_specs=[pl.BlockSpec((l, N_XBC), lambda c: (rev(c), 0)), pl.BlockSpec((l, LANE), lambda c: (rev(c), 0)),
                   _full((8, LANE))],
        out_shape=[_sds((s, N_XBC)), _sds((s, LANE)), _sds((8, LANE))],
        scratch_shapes=[pltpu.VMEM((3, LANE, LANE), F32)],
        compiler_params=_params(("arbitrary",)),
    )


def _sconv_bwd(proj, w, b, dxbc, dep=None):
    s = proj.shape[0]

    def body(u_ref, w_ref, b_ref, d_ref, du_ref, dw_ref, db_ref):
        u = u_ref[...]
        wv = w_ref[...]
        dpre = d_ref[...] * _dsilu(_sconv_pre(u, wv, b_ref[...]))
        du_ref[...] = (wv[3:4, :] * dpre + wv[2:3, :] * _shift_up(dpre, 1) + wv[1:2, :] * _shift_up(dpre, 2)
                       + wv[0:1, :] * _shift_up(dpre, 3))
        for k in range(4):
            dw_ref[k:k + 1, :] = jnp.sum(dpre * _shift_down(u, 3 - k), axis=0, keepdims=True)
        db_ref[...] = jnp.sum(dpre, axis=0, keepdims=True)

    blk = pl.BlockSpec((s, LANE), lambda j: (0, j))
    return _call_after(
        dep, body, (proj, w, b, dxbc), name="sconv_bwd", grid=(N_XBC // LANE,),
        in_specs=[_col(s, O_XBC), pl.BlockSpec((4, LANE), lambda j: (0, j)), pl.BlockSpec((1, LANE), lambda j: (0, j)), blk],
        out_specs=[blk, pl.BlockSpec((4, LANE), lambda j: (0, j)), pl.BlockSpec((1, LANE), lambda j: (0, j))],
        out_shape=[_sds((s, N_XBC)), _sds((4, N_XBC)), _sds((1, N_XBC))],
        compiler_params=_params(("parallel",)),
    )


def _conva_bwd(proj, w, dya, dep=None):
    s = proj.shape[0]

    def body(h_ref, b_ref, c_ref, z_ref, w_ref, d_ref, da_ref, dw_ref):
        ah, ab, acv, az = h_ref[...], b_ref[...], c_ref[...], z_ref[...]
        wv = w_ref[...]
        u = acv * ah
        cv = wv[2:3, :] * u + wv[1:2, :] * _shift_down(u, 1) + wv[0:1, :] * _shift_down(u, 2)
        dy = d_ref[...]
        sz = _silu(az)
        da_ref[1] = dy * cv * sz
        da_ref[3] = dy * ab * cv * _dsilu(az)
        dcv = dy * ab * sz
        du = wv[2:3, :] * dcv + wv[1:2, :] * _shift_up(dcv, 1) + wv[0:1, :] * _shift_up(dcv, 2)
        da_ref[0] = du * acv
        da_ref[2] = du * ah
        for k in range(3):
            dw_ref[k:k + 1, :] = jnp.sum(dcv * _shift_down(u, 2 - k), axis=0, keepdims=True)

    return _call_after(
        dep, body, (proj, proj, proj, proj, w, dya), name="conva_bwd", grid=(D_CONV_A // LANE,),
        in_specs=[_col(s, O_AH), _col(s, O_AB), _col(s, O_AC), _col(s, O_AZ), pl.BlockSpec((3, LANE), lambda j: (0, j)),
                  pl.BlockSpec((s, LANE), lambda j: (0, j))],
        out_specs=[pl.BlockSpec((4, s, LANE), lambda j: (0, 0, j)), pl.BlockSpec((3, LANE), lambda j: (0, j))],
        out_shape=[_sds((4, s, D_CONV_A)), _sds((3, D_CONV_A))],
        compiler_params=_params(("parallel",)),
    )


def _mla_prep_bwd(dq, dk, dv, proj, qn, kvn, rq, rkv, gq, gkv, wq, wkv, cos, sin):
    s = proj.shape[0]
    ts = _tile(s)
    nh = MLA_HEADS

    def body(dq_ref, dk_ref, dv_ref, cqa_ref, ckv_ref, qn_ref, kvn_ref, rq_ref, rkv_ref, gq_ref, gkv_ref,
             wq_ref, wkv_ref, cos_ref, sin_ref, dcqa_ref, dckv_ref, dtail_ref, dwq_ref, dwkv_ref, dgq_ref, dgkv_ref):
        @pl.when(pl.program_id(0) == 0)
        def _():
            dwq_ref[...] = jnp.zeros_like(dwq_ref)
            dwkv_ref[...] = jnp.zeros_like(dwkv_ref)
            dgq_ref[...] = jnp.zeros_like(dgq_ref)
            dgkv_ref[...] = jnp.zeros_like(dgkv_ref)

        cosv = cos_ref[...]
        sinv = sin_ref[...]
        lane = _iota((ts, LANE), 1)
        rope_lanes = (lane >= ROPE_LANE) & (lane < ROPE_LANE + QK_ROPE)

        def unrope(gr):
            return gr * cosv + _rope_swap(gr * sinv)

        dqs, dks, dvs = [], [], []
        dkr = jnp.zeros((ts, LANE), F32)
        for h in range(nh):
            dqs.append(unrope(dq_ref[h] * ATT_SCALE).astype(MXU))
            dkh = dk_ref[h]
            dks.append(jnp.where(lane < QK_NOPE, dkh, 0.0).astype(MXU))
            dkr = dkr + jnp.where(rope_lanes, dkh, 0.0)
            dvs.append(dv_ref[h].astype(MXU))
        dtail_ref[...] = pltpu.roll(jnp.where(rope_lanes, unrope(dkr), 0.0), ROPE_LANE, 1)
        dq_all = jnp.concatenate(dqs, axis=1)
        dkv_all = jnp.concatenate(dks + dvs, axis=1)
        dwq_ref[...] += _dot_tn(dq_all, qn_ref[...])
        dwkv_ref[...] += _dot_tn(dkv_all, kvn_ref[...])
        dcqa, dgq = _rms_bwd(_dot(dq_all, wq_ref[...]), cqa_ref[...], rq_ref[...], gq_ref[...])
        dckv, dgkv = _rms_bwd(_dot(dkv_all, wkv_ref[...]), ckv_ref[...], rkv_ref[...], gkv_ref[...])
        dcqa_ref[...] = dcqa
        dckv_ref[...] = dckv
        dgq_ref[...] += dgq
        dgkv_ref[...] += dgkv

    head = pl.BlockSpec((nh, ts, LANE), lambda i: (0, i, 0))
    return pl.pallas_call(
        body, name="mla_prep_bwd", grid=(s // ts,),
        in_specs=[head, head, head,
                  pl.BlockSpec((ts, Q_LORA), lambda i: (i, O_CQA // Q_LORA)),
                  pl.BlockSpec((ts, KV_LORA), lambda i: (i, O_CKV // KV_LORA)),
                  _row(ts, Q_LORA), _row(ts, KV_LORA), _row(ts, 1), _row(ts, 1),
                  _full((1, Q_LORA)), _full((1, KV_LORA)), _full((nh * LANE, Q_LORA)), _full((2 * nh * LANE, KV_LORA)),
                  _row(ts, LANE), _row(ts, LANE)],
        out_specs=[_row(ts, Q_LORA), _row(ts, KV_LORA), _row(ts, LANE), _full((nh * LANE, Q_LORA)),
                   _full((2 * nh * LANE, KV_LORA)), _full((1, Q_LORA)), _full((1, KV_LORA))],
        out_shape=[_sds((s, Q_LORA)), _sds((s, KV_LORA)), _sds((s, LANE)), _sds((nh * LANE, Q_LORA)),
                   _sds((2 * nh * LANE, KV_LORA)), _sds((1, Q_LORA)), _sds((1, KV_LORA))],
        compiler_params=_params(("arbitrary",)),
    )(dq, dk, dv, proj, proj, qn, kvn, rq, rkv, gq, gkv, wq, wkv, cos, sin)


def _inproj_bwd(da4, dsz, dxbc_in, dcqa, dckv, dcz, dtail_a, dtail_b, w, x, rstd, g, dout):
    s = x.shape[0]
    ts = _tile(s)

    def body(da_ref, dsz_ref, dxbc_ref, dcqa_ref, dckv_ref, dcz_ref, dta_ref, dtb_ref, w_ref, x_ref, r_ref, g_ref, dout_ref,
             dproj_ref, dx_ref, dg_ref):
        @pl.when(pl.program_id(0) == 0)
        def _():
            dg_ref[...] = jnp.zeros_like(dg_ref)

        dproj = jnp.concatenate(
            [da_ref[0], da_ref[1], da_ref[2], da_ref[3], dsz_ref[...], dxbc_ref[...], dcqa_ref[...], dckv_ref[...],
             dcz_ref[...], dta_ref[...] + dtb_ref[...]], axis=1).astype(MXU)
        dproj_ref[...] = dproj
        dh = _dot_nt(dproj, w_ref[...])
        dx, dg = _rms_bwd(dh, x_ref[...], r_ref[...], g_ref[...])
        dx_ref[...] = dout_ref[...] + dx
        dg_ref[...] += dg

    return pl.pallas_call(
        body, name="inproj_bwd", grid=(s // ts,),
        in_specs=[pl.BlockSpec((4, ts, D_CONV_A), lambda i: (0, i, 0)), _row(ts, D_SSD), _row(ts, N_XBC), _row(ts, Q_LORA),
                  _row(ts, KV_LORA), _row(ts, D_MLA), _row(ts, LANE), _row(ts, LANE), _full((D_MODEL, NCOL)),
                  _row(ts, D_MODEL), _row(ts, 1), _full((1, D_MODEL)), _row(ts, D_MODEL)],
        out_specs=[_row(ts, NCOL), _row(ts, D_MODEL), _full((1, D_MODEL))],
        out_shape=[_sds((s, NCOL), MXU), _sds((s, D_MODEL)), _sds((1, D_MODEL))],
        compiler_params=_params(("arbitrary",)),
    )(da4, dsz, dxbc_in, dcqa, dckv, dcz, dtail_a, dtail_b, w, x, rstd, g, dout)


DWIN_BLOCK = 640


def _dwin(h, dproj):
    s = h.shape[0]

    def body(h_ref, d_ref, o_ref):
        o_ref[...] = _dot_tn(h_ref[...], d_ref[...])

    return pl.pallas_call(
        body, name="dwin", grid=(NCOL // DWIN_BLOCK,),
        in_specs=[_full((s, D_MODEL)), pl.BlockSpec((s, DWIN_BLOCK), lambda j: (0, j))],
        out_specs=pl.BlockSpec((D_MODEL, DWIN_BLOCK), lambda j: (0, j)),
        out_shape=_sds((D_MODEL, NCOL)),
        compiler_params=_params(("parallel",)),
    )(h, dproj)


def _adamw(w, g, m, v):
    r, c = w.shape
    tr = r
    for cand in (256, 128, 64, 32, 16, 8):
        if r % cand == 0:
            tr = cand
            break
    bc1 = 1.0 - ADAM_B1 ** ADAM_STEP
    bc2 = 1.0 - ADAM_B2 ** ADAM_STEP

    def body(w_ref, g_ref, m_ref, v_ref, d_ref, mo_ref, vo_ref):
        gv = g_ref[...]
        mn = ADAM_B1 * m_ref[...] + (1.0 - ADAM_B1) * gv
        vn = ADAM_B2 * v_ref[...] + (1.0 - ADAM_B2) * (gv * gv)
        mo_ref[...] = mn
        vo_ref[...] = vn
        d_ref[...] = -ADAM_LR * ((mn / bc1) / (jnp.sqrt(vn / bc2) + ADAM_EPS) + ADAM_WD * w_ref[...])

    blk = pl.BlockSpec((tr, c), lambda i: (i, 0))
    return pl.pallas_call(
        body, name="adamw", grid=(r // tr,),
        in_specs=[blk] * 4, out_specs=[blk] * 3, out_shape=[_sds((r, c))] * 3,
        compiler_params=_params(("parallel",)),
    )(w, g, m, v)


def _perm_cols(w):
    pad = jnp.zeros(w.shape[:-1] + (NCOL - IN_COLS,), w.dtype)
    return jnp.concatenate([w[..., :2304], w[..., 2310:2566], w[..., 2566:2694], w[..., 2726:3110],
                            w[..., 2694:2726], w[..., 2304:2310], pad], axis=-1)


def _unperm_cols(g):
    return jnp.concatenate([g[..., :2304], g[..., 3104:3110], g[..., 2304:2560], g[..., 2560:2688],
                            g[..., 3072:3104], g[..., 2688:3072]], axis=-1)


def _wq_layout(wt):
    return jnp.pad(wt.reshape(MLA_HEADS, QK_NOPE + QK_ROPE, Q_LORA), ((0, 0), (0, 32), (0, 0))).reshape(MLA_HEADS * LANE, Q_LORA)


def _wq_unlayout(g):
    return g.reshape(MLA_HEADS, LANE, Q_LORA)[:, :QK_NOPE + QK_ROPE].reshape(MLA_HEADS * (QK_NOPE + QK_ROPE), Q_LORA)


def _wkv_layout(wt):
    t = wt.reshape(MLA_HEADS, 2, 64, KV_LORA).transpose(1, 0, 2, 3)
    return jnp.pad(t, ((0, 0), (0, 0), (0, 64), (0, 0))).reshape(2 * MLA_HEADS * LANE, KV_LORA)


def _wkv_unlayout(g):
    t = g.reshape(2, MLA_HEADS, LANE, KV_LORA)[:, :, :64]
    return t.transpose(1, 0, 2, 3).reshape(MLA_HEADS * LANE, KV_LORA)


def _rope_tables(positions):
    inv_freq = ROPE_BASE ** (-jnp.arange(0, QK_ROPE, 2, dtype=F32) / QK_ROPE)
    ang = positions.astype(F32)[:, None] * inv_freq
    cos, sin = jnp.cos(ang), jnp.sin(ang)
    s = positions.shape[0]
    one, zero = jnp.ones((s, ROPE_LANE), F32), jnp.zeros((s, ROPE_LANE), F32)
    cos_t = jnp.concatenate([one, cos, cos, one[:, :32]], axis=1)
    sin_t = jnp.concatenate([zero, -sin, sin, zero[:, :32]], axis=1)
    return cos_t, sin_t


def _ssd_scalars(dt_bias, a_log, d_skip):
    return jnp.pad(jnp.stack([dt_bias, a_log, d_skip]), ((0, 5), (DT_LANE, LANE - DT_LANE - SSD_HEADS)))


def _layer_fwd(x, lw, cos, sin, dep=None, late=None):
    proj, h, rstd = _inproj_fwd(x, lw["norm_g"], lw["w_in"], dep)
    ya = _conva_fwd(proj, lw["conv_a_w"])
    xbc = _sconv_fwd(proj, lw["ssd_conv_w"], lw["ssd_conv_b"])
    y_ssd, states = _ssd_fwd(xbc, proj, lw["sc"])
    if late is not None:
        lw = {**lw, **late(ya, y_ssd)}
    q, k, v, qn, kvn, rq, rkv = _mla_prep_fwd(proj, lw["gq"], lw["gkv"], lw["wq"], lw["wkv"], cos, sin)
    o, lse = _attn_fwd(q, k, v)
    x_out, y = _outproj_fwd(x, proj, ya, y_ssd, o, lw["g_ssd"], lw["w_out"])
    saved = dict(x=x, proj=proj, h=h, rstd=rstd, xbc=xbc, y_ssd=y_ssd, states=states, q=q, k=k, v=v, qn=qn, kvn=kvn,
                 rq=rq, rkv=rkv, o=o, lse=lse, y=y)
    return x_out, saved, lw


def _layer_bwd(dout, lw, sv, cos, sin, rs=None):
    tok = lambda: None if rs is None else rs["h"]["token"]
    dya, dys, dsz, d_o, dcz, dg_ssd, dw_out = _outproj_bwd(dout, sv["y"], lw["w_out"], sv["proj"], sv["y_ssd"], sv["o"],
                                                            lw["g_ssd"], tok())
    if rs is not None:
        rs = _rs_add_mine(rs, [dya])
    dq, dk, dv = _attn_bwd(sv["q"], sv["k"], sv["v"], sv["o"], d_o, sv["lse"], tok())
    dxbc, dtail_s, dsc = _ssd_bwd(sv["xbc"], sv["proj"], lw["sc"], sv["states"], dys, tok())
    da4, dw_conva = _conva_bwd(sv["proj"], lw["conv_a_w"], dya, tok())
    if rs is not None:
        rs = _rs_add_chips(rs, [dq, dxbc, da4])
    du, dw_sconv, db_sconv = _sconv_bwd(sv["proj"], lw["ssd_conv_w"], lw["ssd_conv_b"], dxbc, tok())
    dcqa, dckv, dtail_m, dwq, dwkv, dgq, dgkv = _mla_prep_bwd(
        dq, dk, dv, sv["proj"], sv["qn"], sv["kvn"], sv["rq"], sv["rkv"], lw["gq"], lw["gkv"], lw["wq"], lw["wkv"], cos, sin)
    dproj, dx, dg = _inproj_bwd(da4, dsz, du, dcqa, dckv, dcz, dtail_s, dtail_m, lw["w_in"], sv["x"], sv["rstd"],
                                lw["norm_g"], dout)
    reduced = None if rs is None else _rs_end(rs, [du, dcqa, dx])
    dw_in = _dwin(sv["h"], dproj)
    grads = dict(norm_g=dg, w_in=dw_in, conv_a_w=dw_conva, ssd_conv_w=dw_sconv, ssd_conv_b=db_sconv, sc=dsc,
                 g_ssd=dg_ssd, gq=dgq, wq=dwq, gkv=dgkv, wkv=dwkv, w_out=dw_out)
    return dx, grads, reduced


ANY = pl.BlockSpec(memory_space=pl.ANY)
N_CHIPS = 4
N_DEV = 8


def _place():
    return lax.axis_index("x"), lax.axis_index("y"), lax.axis_index("c")


HBM_SPEC = pl.BlockSpec(memory_space=pltpu.HBM)
SEM_SPEC = pl.BlockSpec(memory_space=pltpu.SEMAPHORE)
PAYLOAD = jnp.bfloat16


def _hbm(a):
    return pltpu.with_memory_space_constraint(a, pltpu.HBM)


def _run_plan(plan, srcs, lands, send_sems, recv_sems, start, wait):
    copies = plan(srcs, lands)
    if start:
        for i, (src, dst, _, to) in enumerate(copies):
            pltpu.make_async_remote_copy(src_ref=src, dst_ref=dst, send_sem=send_sems.at[i], recv_sem=recv_sems.at[i],
                                         device_id=to, device_id_type=MESH_T).start()
    if wait:
        for i, (src, _, arrives, to) in enumerate(copies):
            cp = pltpu.make_async_remote_copy(src_ref=src, dst_ref=arrives, send_sem=send_sems.at[i],
                                              recv_sem=recv_sems.at[i], device_id=to, device_id_type=MESH_T)
            cp.wait_send()
            cp.wait_recv()


def _exchange_fused(name, plan, n_copies, srcs, land_shapes):
    ns, nl = len(srcs), len(land_shapes)

    def body(*refs):
        _run_plan(plan, refs[:ns], refs[ns:ns + nl], refs[ns + nl], refs[ns + nl + 1], True, True)

    return pl.pallas_call(
        body, name=name, in_specs=[ANY] * ns, out_specs=[ANY] * nl, out_shape=list(land_shapes),
        scratch_shapes=[pltpu.SemaphoreType.DMA((n_copies,)), pltpu.SemaphoreType.DMA((n_copies,))],
    )(*srcs)


def _exchange_start(name, plan, n_copies, srcs, land_shapes, deps):
    ns, nl = len(srcs), len(land_shapes)
    n_in = ns + nl + len(deps)

    def body(*refs):
        send_sems, recv_sems = refs[n_in], refs[n_in + 1]
        token = refs[-1]
        _run_plan(plan, refs[:ns], refs[ns:ns + nl], send_sems, recv_sems, True, False)
        token[...] = jnp.zeros_like(token)

    thru = [pltpu.HBM(a.shape, a.dtype) for a in srcs] + [pltpu.HBM(a.shape, a.dtype) for a in land_shapes]
    outs = pl.pallas_call(
        body, name=name,
        out_shape=(pltpu.SemaphoreType.DMA((n_copies,)), pltpu.SemaphoreType.DMA((n_copies,)), *thru, _sds((8, LANE))),
        in_specs=[HBM_SPEC] * (ns + nl) + [ANY] * len(deps),
        out_specs=(SEM_SPEC, SEM_SPEC, *[HBM_SPEC] * (ns + nl), pl.BlockSpec(memory_space=pltpu.VMEM)),
        input_output_aliases={i: 2 + i for i in range(ns + nl)},
        compiler_params=pltpu.CompilerParams(has_side_effects=pltpu.SideEffectType.DATAFLOW_SIDE_EFFECTING),
    )(*[_hbm(a) for a in srcs], *[_hbm(lax.empty(a.shape, a.dtype)) for a in land_shapes], *deps)
    return (outs[0], outs[1]), list(outs[2:2 + ns]), list(outs[2 + ns:2 + ns + nl]), outs[-1]


def _exchange_wait(name, plan, sems, srcs, lands, after):
    ns, nl = len(srcs), len(lands)

    def body(*refs):
        _run_plan(plan, refs[:ns], refs[ns:ns + nl], refs[ns + nl], refs[ns + nl + 1], False, True)

    outs = pl.pallas_call(
        body, name=name,
        out_shape=[pltpu.HBM(a.shape, a.dtype) for a in list(srcs) + list(lands)],
        in_specs=[HBM_SPEC] * (ns + nl) + [SEM_SPEC, SEM_SPEC] + [ANY] * len(after), out_specs=[HBM_SPEC] * (ns + nl),
        input_output_aliases={i: i for i in range(ns + nl)},
        compiler_params=pltpu.CompilerParams(has_side_effects=pltpu.SideEffectType.DATAFLOW_SIDE_EFFECTING),
    )(*srcs, *lands, sems[0], sems[1], *after)
    return list(outs[:ns]), list(outs[ns:])


def _xchg_begin(name, plan, n_copies, srcs, land_shapes, split, deps=()):
    if not split:
        return dict(split=False, srcs=list(srcs), lands=_exchange_fused(name, plan, n_copies, srcs, land_shapes), token=None)
    sems, srcs_t, lands_t, token = _exchange_start(name + "_start", plan, n_copies, srcs, land_shapes, list(deps))
    return dict(split=True, name=name, plan=plan, sems=sems, srcs=srcs_t, lands=lands_t, token=token)


def _xchg_end(h, after):
    if not h["split"]:
        return h["srcs"], h["lands"]
    return _exchange_wait(h["name"] + "_wait", h["plan"], h["sems"], h["srcs"], h["lands"], after)


def _other_chips():
    x, y, c = _place()
    return [(1 - x, y), (x, 1 - y), (1 - x, 1 - y)]


def _gather_plan(srcs, lands):
    x, y, c = _place()
    me = 2 * x + y
    return [(srcs[a], lands[a].at[me], lands[a].at[2 * cx + cy], (cx, cy, c))
            for (cx, cy) in _other_chips() for a in range(len(srcs))]


def _gather_begin(shards, split, tag, deps=()):
    shapes = [_sds((N_CHIPS,) + a.shape, a.dtype) for a in shards]
    return _xchg_begin(f"gather_{tag}", _gather_plan, 3 * len(shards), shards, shapes, split, deps)


def _gather_end(h, after):
    shards, lands = _xchg_end(h, after)
    me = 2 * lax.axis_index("x") + lax.axis_index("y")
    return [lax.dynamic_update_index_in_dim(g, s, me, 0) for g, s in zip(lands, shards)]


def _swap_plan(srcs, lands):
    x, y, c = _place()
    return [(srcs[a].at[:, 1 - c], lands[a], lands[a], (x, y, 1 - c)) for a in range(len(srcs))]


def _chips_plan(srcs, lands):
    x, y, c = _place()
    me = 2 * x + y
    return [(srcs[a].at[2 * cx + cy], lands[a].at[me], lands[a].at[2 * cx + cy], (cx, cy, c))
            for (cx, cy) in _other_chips() for a in range(len(srcs))]


def _share_plan(srcs, lands):
    x, y, c = _place()
    return [(srcs[a], lands[a].at[c], lands[a].at[1 - c], (x, y, 1 - c)) for a in range(len(srcs))]


def _allreduce_small(slab):
    r = slab.shape[0]

    def body(s_ref, o_ref, gath, send_sems, recv_sems):
        x, y, c = _place()
        me = 4 * x + 2 * y + c
        gath[me] = s_ref[...]
        cps = []
        for rel in range(1, N_DEV):
            px = 1 - x if rel & 4 else x
            py = 1 - y if rel & 2 else y
            pc = 1 - c if rel & 1 else c
            cp = pltpu.make_async_remote_copy(src_ref=s_ref, dst_ref=gath.at[me], send_sem=send_sems.at[rel - 1],
                                              recv_sem=recv_sems.at[rel - 1], device_id=(px, py, pc), device_id_type=MESH_T)
            cp.start()
            cps.append(cp)
        for cp in cps:
            cp.wait()
        acc = gath[0]
        for d in range(1, N_DEV):
            acc = acc + gath[d]
        o_ref[...] = acc

    vm = pl.BlockSpec(memory_space=pltpu.VMEM)
    return pl.pallas_call(
        body, name="allreduce_small", in_specs=[vm], out_specs=vm, out_shape=_sds((r, LANE)),
        scratch_shapes=[pltpu.VMEM((N_DEV, r, LANE), F32), pltpu.SemaphoreType.DMA((N_DEV - 1,)),
                        pltpu.SemaphoreType.DMA((N_DEV - 1,))],
    )(slab)


def _add_mine(g4, recv, half):
    _, _, rh, c = g4.shape

    def body(h_ref, g_ref, r_ref, o_ref):
        o_ref[0] = (g_ref[0, 0] + r_ref[0]).astype(o_ref.dtype)

    return pl.pallas_call(
        body, name="add_mine",
        grid_spec=pltpu.PrefetchScalarGridSpec(
            num_scalar_prefetch=1, grid=(N_CHIPS,),
            in_specs=[pl.BlockSpec((1, 1, rh, c), lambda j, h: (j, h[0], 0, 0)), pl.BlockSpec((1, rh, c), lambda j, h: (j, 0, 0))],
            out_specs=pl.BlockSpec((1, rh, c), lambda j, h: (j, 0, 0))),
        out_shape=_sds((N_CHIPS, rh, c), PAYLOAD),
        compiler_params=_params(("parallel",)),
    )(half, g4, recv)


def _add_chips(e, p, me):
    _, rh, c = e.shape

    def body(m_ref, e_ref, p_ref, o_ref):
        own = p_ref[0].astype(F32)
        acc = None
        for s in range(N_CHIPS):
            t = jnp.where(m_ref[0] == s, own, e_ref[s].astype(F32))
            acc = t if acc is None else acc + t
        o_ref[...] = acc

    return pl.pallas_call(
        body, name="add_chips",
        grid_spec=pltpu.PrefetchScalarGridSpec(
            num_scalar_prefetch=1, grid=(1,),
            in_specs=[pl.BlockSpec((N_CHIPS, rh, c), lambda i, m: (0, 0, 0)), pl.BlockSpec((1, rh, c), lambda i, m: (m[0], 0, 0))],
            out_specs=pl.BlockSpec((rh, c), lambda i, m: (0, 0))),
        out_shape=_sds((rh, c)),
        compiler_params=_params(("arbitrary",)),
    )(me, e, p)


def _rs_begin(gs, split, tag):
    g4 = [g.reshape(N_CHIPS, 2, g.shape[0] // (2 * N_CHIPS), g.shape[1]) for g in gs]
    h = _xchg_begin(f"rs_swap_{tag}", _swap_plan, len(gs), g4, [_sds((N_CHIPS,) + g.shape[2:]) for g in g4], split)
    return dict(h=h, split=split, tag=tag, shapes=[g.shape for g in gs])


def _rs_add_mine(st, after):
    g4, recv = _xchg_end(st["h"], after)
    half = jnp.reshape(lax.axis_index("c"), (1,)).astype(jnp.int32)
    ps = [_add_mine(g, r, half) for g, r in zip(g4, recv)]
    st["h"] = _xchg_begin(f"rs_chips_{st['tag']}", _chips_plan, 3 * len(ps), ps, [_sds(p.shape, p.dtype) for p in ps], st["split"])
    return st


def _rs_add_chips(st, after):
    ps, es = _xchg_end(st["h"], after)
    me = jnp.reshape(2 * lax.axis_index("x") + lax.axis_index("y"), (1,)).astype(jnp.int32)
    fs = [_add_chips(e, p, me) for e, p in zip(es, ps)]
    st["h"] = _xchg_begin(f"rs_share_{st['tag']}", _share_plan, len(fs), fs, [_sds((2,) + f.shape) for f in fs], st["split"])
    return st


def _rs_end(st, after):
    fs, ss = _xchg_end(st["h"], after)
    c = lax.axis_index("c")
    return [lax.dynamic_update_index_in_dim(s, f, c, 0).reshape(shp[0] // N_CHIPS, shp[1])
            for s, f, shp in zip(ss, fs, st["shapes"])]


WEIGHTS = ["norm_g", "w_in", "conv_a_w", "ssd_conv_w", "ssd_conv_b", "ssd_dt_bias", "ssd_a_log", "ssd_d", "ssd_norm_g",
           "mla_q_norm_g", "w_qb", "mla_kv_norm_g", "w_kvb", "w_out", "final_norm_g"]
BIG = ["w_in", "w_qb", "w_kvb", "w_out"]
SLAB_ROWS = 128
SMALL_ROWS = 72


def _to_slab(parts, rows):
    flat = jnp.concatenate([p.reshape(-1) for p in parts])
    return jnp.pad(flat, (0, rows * LANE - flat.shape[0])).reshape(rows, LANE)


def _from_slab(slab, shapes):
    flat = slab.reshape(-1)
    out, off = [], 0
    for shp in shapes:
        n = int(np.prod(shp))
        out.append(flat[off:off + n].reshape(shp))
        off += n
    return out


def kernel(x, positions, norm_g, w_in, conv_a_w, ssd_conv_w, ssd_conv_b, ssd_dt_bias, ssd_a_log, ssd_d, ssd_norm_g, mla_q_norm_g, w_qb, mla_kv_norm_g, w_kvb, w_out, final_norm_g, loss_target, m_norm_g, m_w_in, m_conv_a_w, m_ssd_conv_w, m_ssd_conv_b, m_ssd_dt_bias, m_ssd_a_log, m_ssd_d, m_ssd_norm_g, m_mla_q_norm_g, m_w_qb, m_mla_kv_norm_g, m_w_kvb, m_w_out, m_final_norm_g, v_norm_g, v_w_in, v_conv_a_w, v_ssd_conv_w, v_ssd_conv_b, v_ssd_dt_bias, v_ssd_a_log, v_ssd_d, v_ssd_norm_g, v_mla_q_norm_g, v_w_qb, v_mla_kv_norm_g, v_w_kvb, v_w_out, v_final_norm_g):
    w = dict(norm_g=norm_g, w_in=w_in, conv_a_w=conv_a_w, ssd_conv_w=ssd_conv_w, ssd_conv_b=ssd_conv_b,
             ssd_dt_bias=ssd_dt_bias, ssd_a_log=ssd_a_log, ssd_d=ssd_d, ssd_norm_g=ssd_norm_g, mla_q_norm_g=mla_q_norm_g,
             w_qb=w_qb, mla_kv_norm_g=mla_kv_norm_g, w_kvb=w_kvb, w_out=w_out, final_norm_g=final_norm_g)
    mom = dict(norm_g=m_norm_g, w_in=m_w_in, conv_a_w=m_conv_a_w, ssd_conv_w=m_ssd_conv_w, ssd_conv_b=m_ssd_conv_b,
               ssd_dt_bias=m_ssd_dt_bias, ssd_a_log=m_ssd_a_log, ssd_d=m_ssd_d, ssd_norm_g=m_ssd_norm_g,
               mla_q_norm_g=m_mla_q_norm_g, w_qb=m_w_qb, mla_kv_norm_g=m_mla_kv_norm_g, w_kvb=m_w_kvb, w_out=m_w_out,
               final_norm_g=m_final_norm_g)
    var = dict(norm_g=v_norm_g, w_in=v_w_in, conv_a_w=v_conv_a_w, ssd_conv_w=v_ssd_conv_w, ssd_conv_b=v_ssd_conv_b,
               ssd_dt_bias=v_ssd_dt_bias, ssd_a_log=v_ssd_a_log, ssd_d=v_ssd_d, ssd_norm_g=v_ssd_norm_g,
               mla_q_norm_g=v_mla_q_norm_g, w_qb=v_w_qb, mla_kv_norm_g=v_mla_kv_norm_g, w_kvb=v_w_kvb, w_out=v_w_out,
               final_norm_g=v_final_norm_g)
    chip = 2 * lax.axis_index("x") + lax.axis_index("y")

    conv_pack = jnp.zeros((DEPTH, 8, 256), F32)
    conv_pack = conv_pack.at[:, 0:3, 0:64].set(conv_a_w).at[:, 3:7, 0:224].set(ssd_conv_w)
    early_shards = [[_perm_cols(w_in[l]).astype(MXU), conv_pack[l]] for l in range(DEPTH)]
    late_shards = [[w_out[l].astype(MXU), w_qb[l].T.astype(MXU), w_kvb[l].T.astype(MXU)] for l in range(DEPTH)]

    def early_weights(l, gathered):
        g_in, g_conv = gathered
        return dict(
            norm_g=norm_g[l][None], w_in=g_in.reshape(D_MODEL, NCOL),
            conv_a_w=jnp.concatenate([g_conv[j, 0:3, 0:64] for j in range(N_CHIPS)], axis=1),
            ssd_conv_w=jnp.concatenate([g_conv[j, 3:7, 0:224] for j in range(N_CHIPS)], axis=1),
            ssd_conv_b=ssd_conv_b[l][None], sc=_ssd_scalars(ssd_dt_bias[l], ssd_a_log[l], ssd_d[l]),
            g_ssd=ssd_norm_g[l][None], gq=mla_q_norm_g[l][None], gkv=mla_kv_norm_g[l][None])

    def late_weights(gathered):
        g_out, g_qb, g_kvb = gathered
        return dict(wq=_wq_layout(g_qb.reshape(MLA_HEADS * 96, Q_LORA)), wkv=_wkv_layout(g_kvb.reshape(MLA_HEADS * LANE, KV_LORA)),
                    w_out=g_out.reshape(D_MODEL, D_MODEL))

    def large_grads(g):
        wq = jnp.pad(_wq_unlayout(g["wq"]).reshape(N_CHIPS, 144, Q_LORA), ((0, 0), (0, 16), (0, 0)))
        return [g["w_in"], g["w_out"], wq.reshape(N_CHIPS * 160, Q_LORA), _wkv_unlayout(g["wkv"])]

    gather_a0 = _gather_begin(early_shards[0], True, "a0")
    cos, sin = _rope_tables(positions[0])
    lw0 = early_weights(0, _gather_end(gather_a0, [cos, sin] + late_shards[0] + early_shards[1] + late_shards[1]))
    gather_b0 = _gather_begin(late_shards[0], True, "b0")
    gather_1 = _gather_begin(early_shards[1] + late_shards[1], True, "1", [gather_b0["token"]])
    x1, sv0, lw0 = _layer_fwd(x[0], lw0, cos, sin, gather_1["token"],
                              lambda ya, y_ssd: late_weights(_gather_end(gather_b0, [ya, y_ssd])))
    g1 = _gather_end(gather_1, [x1])
    x2, sv1, lw1 = _layer_fwd(x1, {**early_weights(1, g1[:2]), **late_weights(g1[2:])}, cos, sin)
    dx, dgf, loss = _loss_head(x2, final_norm_g[None], loss_target[0])

    dx, lg1, _ = _layer_bwd(dx, lw1, sv1, cos, sin)
    grad_x, lg0, red1 = _layer_bwd(dx, lw0, sv0, cos, sin, _rs_begin(large_grads(lg1), True, 1))
    red0 = _rs_end(_rs_add_chips(_rs_add_mine(_rs_begin(large_grads(lg0), False, 0), []), []), [])
    lg = [lg0, lg1]
    r_in, r_out, r_qb, r_kvb = [jnp.stack([a, b]) for a, b in zip(red0, red1)]
    grad = dict(w_in=_unperm_cols(r_in), w_out=r_out, w_qb=jnp.swapaxes(r_qb[:, :144], 1, 2), w_kvb=jnp.swapaxes(r_kvb, 1, 2))

    small_names = ["norm_g", "conv_a_w", "ssd_conv_w", "ssd_conv_b", "sc", "g_ssd", "gq", "gkv"]
    parts = [loss[0, 0:1], dgf]
    for l in range(DEPTH):
        parts += [lg[l][nm][:3, DT_LANE:DT_LANE + SSD_HEADS] if nm == "sc" else lg[l][nm] for nm in small_names]
    shapes = [(1,), (D_MODEL,)] + [(D_MODEL,), (3, D_CONV_A), (4, N_XBC), (N_XBC,), (3, SSD_HEADS), (D_SSD,), (Q_LORA,), (KV_LORA,)] * DEPTH
    red = _from_slab(_allreduce_small(_to_slab(parts, SLAB_ROWS)), shapes)
    loss_out = red[0][0]
    grad["final_norm_g"] = red[1]
    per = [red[2 + 8 * l:10 + 8 * l] for l in range(DEPTH)]
    grad["norm_g"] = jnp.stack([per[l][0] for l in range(DEPTH)])
    grad["conv_a_w"] = lax.dynamic_slice_in_dim(jnp.stack([per[l][1] for l in range(DEPTH)]), chip * 64, 64, axis=2)
    grad["ssd_conv_w"] = lax.dynamic_slice_in_dim(jnp.stack([per[l][2] for l in range(DEPTH)]), chip * 224, 224, axis=2)
    grad["ssd_conv_b"] = jnp.stack([per[l][3] for l in range(DEPTH)])
    grad["ssd_dt_bias"] = jnp.stack([per[l][4][0] for l in range(DEPTH)])
    grad["ssd_a_log"] = jnp.stack([per[l][4][1] for l in range(DEPTH)])
    grad["ssd_d"] = jnp.stack([per[l][4][2] for l in range(DEPTH)])
    grad["ssd_norm_g"] = jnp.stack([per[l][5] for l in range(DEPTH)])
    grad["mla_q_norm_g"] = jnp.stack([per[l][6] for l in range(DEPTH)])
    grad["mla_kv_norm_g"] = jnp.stack([per[l][7] for l in range(DEPTH)])

    delta, new_m, new_v = {}, {}, {}
    for nm in BIG:
        shp = w[nm].shape
        two_d = (shp[0] * shp[1], shp[2])
        d, mo, vo = _adamw(w[nm].reshape(two_d), grad[nm].reshape(two_d), mom[nm].reshape(two_d), var[nm].reshape(two_d))
        delta[nm], new_m[nm], new_v[nm] = d.reshape(shp), mo.reshape(shp), vo.reshape(shp)
    small = [nm for nm in WEIGHTS if nm not in BIG]
    sshapes = [w[nm].shape for nm in small]
    d, mo, vo = _adamw(_to_slab([w[nm] for nm in small], SMALL_ROWS), _to_slab([grad[nm] for nm in small], SMALL_ROWS),
                       _to_slab([mom[nm] for nm in small], SMALL_ROWS), _to_slab([var[nm] for nm in small], SMALL_ROWS))
    for nm, dv, mv, vv in zip(small, _from_slab(d, sshapes), _from_slab(mo, sshapes), _from_slab(vo, sshapes)):
        delta[nm], new_m[nm], new_v[nm] = dv, mv, vv

    return (loss_out, grad_x[None], *[grad[nm] for nm in WEIGHTS], *[delta[nm] for nm in WEIGHTS],
            *[new_m[nm] for nm in WEIGHTS], *[new_v[nm] for nm in WEIGHTS])
```

```python
import functools
import math

import numpy as np
import jax
import jax.numpy as jnp
from jax import lax
from jax.experimental import pallas as pl
from jax.experimental.pallas import tpu as pltpu

F32 = jnp.float32
MXU = jnp.bfloat16

D_MODEL = 1024
DEPTH = 2
D_CONV_A = 256
D_SSD = 384
SSD_HEADS = 6
SSD_BC = 256
SSD_CHUNK = 128
SSD_NORM_EPS = 1e-5
MLA_HEADS = 6
Q_LORA = 256
KV_LORA = 128
QK_NOPE = 64
QK_ROPE = 32
V_DIM = 64
D_MLA = 384
ROPE_BASE = 10000.0
NORM_EPS = 1e-6
IN_COLS = 3110
LANE = 128

O_AH, O_AB, O_AC, O_AZ = 0, 256, 512, 768
O_SZ = 1024
O_XBC = 1408
O_CQA = 2304
O_CKV = 2560
O_CZ = 2688
O_TAIL = 3072
NCOL = 3200
N_XBC = D_SSD + 2 * SSD_BC
DT_LANE = 32
ROPE_LANE = 64

ADAM_LR, ADAM_B1, ADAM_B2, ADAM_EPS, ADAM_WD, ADAM_STEP = 0.001, 0.9, 0.999, 1e-08, 0.01, 10

VMEM_LIMIT = 56 * 1024 * 1024
MESH_T = pl.DeviceIdType.MESH


def _dot(a, b):
    return jnp.dot(a.astype(MXU), b.astype(MXU), preferred_element_type=F32)


def _dot_nt(a, b):
    return lax.dot_general(a.astype(MXU), b.astype(MXU), (((1,), (1,)), ((), ())), preferred_element_type=F32)


def _dot_tn(a, b):
    return lax.dot_general(a.astype(MXU), b.astype(MXU), (((0,), (0,)), ((), ())), preferred_element_type=F32)


def _dot_hi(a, b):
    return jnp.dot(a, b, precision=lax.Precision.HIGHEST, preferred_element_type=F32)


def _dot_hi_tn(a, b):
    return lax.dot_general(a, b, (((0,), (0,)), ((), ())), precision=lax.Precision.HIGHEST, preferred_element_type=F32)


def _sigmoid(z):
    return 1.0 / (1.0 + jnp.exp(-z))


def _silu(z):
    return z * _sigmoid(z)


def _dsilu(z):
    s = _sigmoid(z)
    return s * (1.0 + z * (1.0 - s))


def _softplus(z):
    e = jnp.exp(-jnp.abs(z))
    return jnp.maximum(z, 0.0) + jnp.where(e < 1e-3, e * (1.0 - 0.5 * e), jnp.log(1.0 + e))


def _iota(shape, dim):
    return lax.broadcasted_iota(jnp.int32, shape, dim)


def _shift_down(u, k):
    if k == 0:
        return u
    return jnp.where(_iota(u.shape, 0) >= k, pltpu.roll(u, k, 0), 0.0)


def _shift_up(u, k):
    if k == 0:
        return u
    n = u.shape[0]
    return jnp.where(_iota(u.shape, 0) < n - k, pltpu.roll(u, n - k, 0), 0.0)


def _rope_swap(t):
    lane = _iota(t.shape, 1)
    lo = (lane >= ROPE_LANE) & (lane < ROPE_LANE + 16)
    hi = (lane >= ROPE_LANE + 16) & (lane < ROPE_LANE + 32)
    return jnp.where(lo, pltpu.roll(t, LANE - 16, 1), jnp.where(hi, pltpu.roll(t, 16, 1), 0.0))


def _params(sem=None):
    return pltpu.CompilerParams(dimension_semantics=sem, vmem_limit_bytes=VMEM_LIMIT)


def _full(shape):
    nd = len(shape)
    return pl.BlockSpec(shape, lambda *_: (0,) * nd)


def _sds(shape, dtype=F32):
    return jax.ShapeDtypeStruct(shape, dtype)


def _tile(s):
    return min(256, s)


def _row(ts, w):
    return pl.BlockSpec((ts, w), lambda i: (i, 0))


def _col(s, off):
    return pl.BlockSpec((s, LANE), lambda j, _o=off // LANE: (0, _o + j))


def _call_after(dep, body, args, *, in_specs, **kw):
    if dep is None:
        return pl.pallas_call(body, in_specs=in_specs, **kw)(*args)
    n = len(args)

    def body_dep(*refs):
        body(*refs[:n], *refs[n + 1:])

    return pl.pallas_call(body_dep, in_specs=list(in_specs) + [pl.BlockSpec(memory_space=pl.ANY)], **kw)(*args, dep)


def _rms(c, g):
    r = lax.rsqrt(jnp.mean(c * c, axis=-1, keepdims=True) + NORM_EPS)
    return c * r * g, r


def _rms_bwd(dn, c, r, g):
    ch = c * r
    dch = dn * g
    dc = r * (dch - ch * jnp.mean(dch * ch, axis=-1, keepdims=True))
    return dc, jnp.sum(dn * ch, axis=0, keepdims=True)


def _inproj_fwd(x, g, w, dep=None):
    s = x.shape[0]
    ts = _tile(s)

    def body(x_ref, g_ref, w_ref, proj_ref, h_ref, r_ref):
        hn, r = _rms(x_ref[...], g_ref[...])
        h = hn.astype(MXU)
        h_ref[...] = h
        r_ref[...] = r
        proj_ref[...] = jnp.dot(h, w_ref[...], preferred_element_type=F32)

    return _call_after(
        dep, body, (x, g, w), name="inproj_fwd", grid=(s // ts,),
        in_specs=[_row(ts, D_MODEL), _full((1, D_MODEL)), _full((D_MODEL, NCOL))],
        out_specs=[_row(ts, NCOL), _row(ts, D_MODEL), _row(ts, 1)],
        out_shape=[_sds((s, NCOL)), _sds((s, D_MODEL), MXU), _sds((s, 1))],
        compiler_params=_params(("parallel",)),
    )


def _conva_fwd(proj, w):
    s = proj.shape[0]

    def body(h_ref, b_ref, c_ref, z_ref, w_ref, y_ref):
        u = c_ref[...] * h_ref[...]
        wv = w_ref[...]
        cv = wv[2:3, :] * u + wv[1:2, :] * _shift_down(u, 1) + wv[0:1, :] * _shift_down(u, 2)
        y_ref[...] = b_ref[...] * cv * _silu(z_ref[...])

    return pl.pallas_call(
        body, name="conva_fwd", grid=(D_CONV_A // LANE,),
        in_specs=[_col(s, O_AH), _col(s, O_AB), _col(s, O_AC), _col(s, O_AZ), pl.BlockSpec((3, LANE), lambda j: (0, j))],
        out_specs=pl.BlockSpec((s, LANE), lambda j: (0, j)),
        out_shape=_sds((s, D_CONV_A)),
        compiler_params=_params(("parallel",)),
    )(proj, proj, proj, proj, w)


def _sconv_pre(u, wv, bv):
    return (wv[3:4, :] * u + wv[2:3, :] * _shift_down(u, 1) + wv[1:2, :] * _shift_down(u, 2)
            + wv[0:1, :] * _shift_down(u, 3) + bv)


def _sconv_fwd(proj, w, b):
    s = proj.shape[0]

    def body(u_ref, w_ref, b_ref, o_ref):
        o_ref[...] = _silu(_sconv_pre(u_ref[...], w_ref[...], b_ref[...]))

    return pl.pallas_call(
        body, name="sconv_fwd", grid=(N_XBC // LANE,),
        in_specs=[_col(s, O_XBC), pl.BlockSpec((4, LANE), lambda j: (0, j)), pl.BlockSpec((1, LANE), lambda j: (0, j))],
        out_specs=pl.BlockSpec((s, LANE), lambda j: (0, j)),
        out_shape=_sds((s, N_XBC)),
        compiler_params=_params(("parallel",)),
    )(proj, w, b)


def _ssd_chunk_common(tail, sc):
    l = SSD_CHUNK
    lane = _iota((l, LANE), 1)
    row = _iota((l, LANE), 0)
    tri = (row >= lane).astype(F32)
    a_row = -jnp.exp(sc[1:2, :])
    pre = tail + sc[0:1, :]
    dt = _softplus(pre)
    a_cs = _dot_hi(tri, dt * a_row)
    return lane, row, tri, a_row, pre, dt, a_cs, a_cs.T


def _pick_col(m, lane, k):
    return jnp.sum(jnp.where(lane == k, m, 0.0), axis=1, keepdims=True)


def _pick_row(m, row, k):
    return jnp.sum(jnp.where(row == k, m, 0.0), axis=0, keepdims=True)


def _ssd_fwd(xbc, proj, sc):
    s = xbc.shape[0]
    nc = s // SSD_CHUNK
    l = SSD_CHUNK

    def body(xbc_ref, tail_ref, sc_ref, y_ref, st_ref, state):
        @pl.when(pl.program_id(0) == 0)
        def _():
            state[...] = jnp.zeros_like(state)

        sc_v = sc_ref[...]
        lane, row, _, _, _, dt, a_cs, a_t = _ssd_chunk_common(tail_ref[...], sc_v)
        lane1 = _iota((1, LANE), 1)
        rowp = _iota((LANE, 1), 0)
        d_row = sc_v[2:3, :]
        for j in range(3):
            st_ref[0, j] = state[j]
        for j in range(3):
            xpair = xbc_ref[:, LANE * j:LANE * (j + 1)]
            sp = state[j]
            ypair = jnp.zeros((l, LANE), F32)
            new_s = jnp.zeros((LANE, LANE), F32)
            decay = jnp.zeros((LANE, 1), F32)
            for half in range(2):
                h = 2 * j + half
                g = h // 3
                hm = (lane < 64) if half == 0 else (lane >= 64)
                hrow = (rowp < 64) if half == 0 else (rowp >= 64)
                ac = _pick_col(a_cs, lane, DT_LANE + h)
                ar = _pick_row(a_t, row, DT_LANE + h)
                dtc = _pick_col(dt, lane, DT_LANE + h)
                alast = jnp.sum(jnp.where(lane1 == l - 1, ar, 0.0), axis=1, keepdims=True)
                dh = jnp.sum(jnp.where(lane1 == DT_LANE + h, d_row, 0.0), axis=1, keepdims=True)
                xm = jnp.where(hm, xpair, 0.0)
                xd = xm * dtc
                bm = xbc_ref[:, D_SSD + LANE * g:D_SSD + LANE * (g + 1)]
                cm = xbc_ref[:, D_SSD + SSD_BC + LANE * g:D_SSD + SSD_BC + LANE * (g + 1)]
                lm = jnp.where(row >= lane, jnp.exp(jnp.minimum(ac - ar, 0.0)), 0.0)
                y_diag = _dot(_dot_nt(cm, bm) * lm, xd)
                y_off = jnp.where(hm, _dot_nt(cm, sp), 0.0) * jnp.exp(ac)
                ypair = ypair + y_diag + y_off + xm * dh
                new_s = new_s + _dot_tn(xd * jnp.exp(alast - ac), bm)
                decay = jnp.where(hrow, jnp.exp(alast), decay)
            state[j] = sp * decay + new_s
            y_ref[:, LANE * j:LANE * (j + 1)] = ypair

    return pl.pallas_call(
        body, name="ssd_fwd", grid=(nc,),
        in_specs=[pl.BlockSpec((l, N_XBC), lambda c: (c, 0)),
                  pl.BlockSpec((l, LANE), lambda c: (c, O_TAIL // LANE)), _full((8, LANE))],
        out_specs=[pl.BlockSpec((l, D_SSD), lambda c: (c, 0)), pl.BlockSpec((1, 3, LANE, LANE), lambda c: (c, 0, 0, 0))],
        out_shape=[_sds((s, D_SSD)), _sds((nc, 3, LANE, LANE))],
        scratch_shapes=[pltpu.VMEM((3, LANE, LANE), F32)],
        compiler_params=_params(("arbitrary",)),
    )(xbc, proj, sc)


def _mla_prep_fwd(proj, gq, gkv, wq, wkv, cos, sin):
    s = proj.shape[0]
    ts = _tile(s)
    nh = MLA_HEADS

    def body(cqa_ref, ckv_ref, tail_ref, gq_ref, gkv_ref, wq_ref, wkv_ref, cos_ref, sin_ref,
             q_ref, k_ref, v_ref, qn_ref, kvn_ref, rq_ref, rkv_ref):
        qn, rq = _rms(cqa_ref[...], gq_ref[...])
        kvn, rkv = _rms(ckv_ref[...], gkv_ref[...])
        qn = qn.astype(MXU)
        kvn = kvn.astype(MXU)
        qn_ref[...] = qn
        kvn_ref[...] = kvn
        rq_ref[...] = rq
        rkv_ref[...] = rkv
        q = _dot_nt(qn, wq_ref[...])
        kv = _dot_nt(kvn, wkv_ref[...])
        cosv = cos_ref[...]
        sinv = sin_ref[...]
        lane = _iota((ts, LANE), 1)
        rope_lanes = (lane >= ROPE_LANE) & (lane < ROPE_LANE + QK_ROPE)
        kr = jnp.where(rope_lanes, pltpu.roll(tail_ref[...], ROPE_LANE, 1), 0.0)
        kr = kr * cosv + _rope_swap(kr) * sinv
        for h in range(nh):
            qh = q[:, LANE * h:LANE * (h + 1)]
            q_ref[h] = ((qh * cosv + _rope_swap(qh) * sinv) * ATT_SCALE).astype(MXU)
            k_ref[h] = (kv[:, LANE * h:LANE * (h + 1)] + kr).astype(MXU)
            v_ref[h] = kv[:, LANE * (nh + h):LANE * (nh + h + 1)].astype(MXU)

    head = pl.BlockSpec((nh, ts, LANE), lambda i: (0, i, 0))
    return pl.pallas_call(
        body, name="mla_prep_fwd", grid=(s // ts,),
        in_specs=[pl.BlockSpec((ts, Q_LORA), lambda i: (i, O_CQA // Q_LORA)),
                  pl.BlockSpec((ts, KV_LORA), lambda i: (i, O_CKV // KV_LORA)),
                  pl.BlockSpec((ts, LANE), lambda i: (i, O_TAIL // LANE)),
                  _full((1, Q_LORA)), _full((1, KV_LORA)), _full((nh * LANE, Q_LORA)), _full((2 * nh * LANE, KV_LORA)),
                  _row(ts, LANE), _row(ts, LANE)],
        out_specs=[head, head, head, _row(ts, Q_LORA), _row(ts, KV_LORA), _row(ts, 1), _row(ts, 1)],
        out_shape=[_sds((nh, s, LANE), MXU)] * 3 + [_sds((s, Q_LORA), MXU), _sds((s, KV_LORA), MXU), _sds((s, 1)), _sds((s, 1))],
        compiler_params=_params(("parallel",)),
    )(proj, proj, proj, gq, gkv, wq, wkv, cos, sin)


ATT_SCALE = (QK_NOPE + QK_ROPE) ** -0.5
NEG = -1e30


def _att_tile(s):
    return min(256, s // 2)


def _attn_fwd(q, k, v):
    nh, s, _ = q.shape
    tq = _att_tile(s)
    nq = s // tq

    def body(q_ref, k_ref, v_ref, o_ref, lse_ref):
        i = pl.program_id(1)
        rowi = _iota((tq, tq), 0)
        coli = _iota((tq, tq), 1)
        zero = (jnp.full((tq, 1), NEG, F32), jnp.zeros((tq, 1), F32), jnp.zeros((tq, LANE), F32))
        state = [zero, zero]
        done = [zero, zero]
        for t in range(nq + 1):
            first = t <= i
            qblk = jnp.where(first, i, nq - 1 - i)
            kblk = jnp.where(first, t, t - i - 1)
            qoff = pl.multiple_of(qblk * tq, tq)
            koff = pl.multiple_of(kblk * tq, tq)
            keep = coli <= rowi + jnp.where(kblk == qblk, 0, tq)
            restart = t == i + 1
            for hh in range(2):
                m, lsum, acc = state[hh]
                if t > 0:
                    done[hh] = tuple(jnp.where(restart, a, b) for a, b in zip(state[hh], done[hh]))
                    m = jnp.where(restart, NEG, m)
                    lsum = jnp.where(restart, 0.0, lsum)
                    acc = jnp.where(restart, 0.0, acc)
                sc = _dot_nt(q_ref[hh, pl.ds(qoff, tq), :], k_ref[hh, pl.ds(koff, tq), :])
                sc = jnp.where(keep, sc, NEG)
                m_new = jnp.maximum(m, jnp.max(sc, axis=1, keepdims=True))
                p = jnp.exp(sc - m_new)
                alpha = jnp.exp(m - m_new)
                lsum = alpha * lsum + jnp.sum(p, axis=1, keepdims=True)
                acc = alpha * acc + _dot(p, v_ref[hh, pl.ds(koff, tq), :])
                state[hh] = (m_new, lsum, acc)
        for blk, res in ((i, done), (nq - 1 - i, state)):
            off = pl.multiple_of(blk * tq, tq)
            out = None
            for hh in range(2):
                m, lsum, acc = res[hh]
                o = acc * (1.0 / lsum)
                lse_ref[hh, pl.ds(off, tq), :] = m + jnp.log(lsum)
                out = o if hh == 0 else out + pltpu.roll(o, V_DIM, 1)
            o_ref[pl.ds(off, tq), :] = out

    pair = pl.BlockSpec((2, s, LANE), lambda j, i: (j, 0, 0))
    return pl.pallas_call(
        body, name="attn_fwd", grid=(nh // 2, nq // 2),
        in_specs=[pair, pair, pair],
        out_specs=[pl.BlockSpec((s, LANE), lambda j, i: (0, j)), pl.BlockSpec((2, s, 1), lambda j, i: (j, 0, 0))],
        out_shape=[_sds((s, D_MLA)), _sds((nh, s, 1))],
        compiler_params=_params(("parallel", "arbitrary")),
    )(q, k, v)


def _ssd_gate(y_ssd, s_z, g):
    yz = y_ssd * _silu(s_z)
    g0 = _iota(yz.shape, 1) < D_SSD // 2
    sq = yz * yz
    ms0 = jnp.sum(jnp.where(g0, sq, 0.0), axis=1, keepdims=True) / (D_SSD // 2)
    ms1 = jnp.sum(jnp.where(g0, 0.0, sq), axis=1, keepdims=True) / (D_SSD // 2)
    r = jnp.where(g0, lax.rsqrt(ms0 + SSD_NORM_EPS), lax.rsqrt(ms1 + SSD_NORM_EPS))
    nrm = yz * r
    return nrm * g, nrm, r, g0


def _outproj_fwd(x, proj, ya, y_ssd, o, g_ssd, w):
    s = x.shape[0]
    ts = _tile(s)

    def body(x_ref, p_ref, ya_ref, ys_ref, o_ref, g_ref, w_ref, xo_ref, y_ref):
        yb = _ssd_gate(ys_ref[...], p_ref[:, O_SZ:O_SZ + D_SSD], g_ref[...])[0]
        yc = o_ref[...] * _silu(p_ref[:, O_CZ:O_CZ + D_MLA])
        y = jnp.concatenate([ya_ref[...], yb, yc], axis=1).astype(MXU)
        y_ref[...] = y
        xo_ref[...] = x_ref[...] + jnp.dot(y, w_ref[...], preferred_element_type=F32)

    return pl.pallas_call(
        body, name="outproj_fwd", grid=(s // ts,),
        in_specs=[_row(ts, D_MODEL), _row(ts, NCOL), _row(ts, D_CONV_A), _row(ts, D_SSD), _row(ts, D_MLA),
                  _full((1, D_SSD)), _full((D_MODEL, D_MODEL))],
        out_specs=[_row(ts, D_MODEL), _row(ts, D_MODEL)],
        out_shape=[_sds((s, D_MODEL)), _sds((s, D_MODEL), MXU)],
        compiler_params=_params(("parallel",)),
    )(x, proj, ya, y_ssd, o, g_ssd, w)


def _loss_head(x, g, tgt):
    s = x.shape[0]
    ts = _tile(s)

    def body(x_ref, g_ref, t_ref, dx_ref, dg_ref, loss_ref):
        @pl.when(pl.program_id(0) == 0)
        def _():
            dg_ref[...] = jnp.zeros_like(dg_ref)
            loss_ref[...] = jnp.zeros_like(loss_ref)

        xv = x_ref[...]
        gv = g_ref[...]
        yn, r = _rms(xv, gv)
        e = yn - t_ref[...]
        loss_ref[...] += jnp.sum(jnp.sum(e * e, axis=1, keepdims=True), axis=0, keepdims=True) * (0.5 / D_MODEL)
        dx, dg = _rms_bwd(e * (1.0 / D_MODEL), xv, r, gv)
        dx_ref[...] = dx
        dg_ref[...] += dg

    return pl.pallas_call(
        body, name="loss_head", grid=(s // ts,),
        in_specs=[_row(ts, D_MODEL), _full((1, D_MODEL)), _row(ts, D_MODEL)],
        out_specs=[_row(ts, D_MODEL), _full((1, D_MODEL)), _full((1, LANE))],
        out_shape=[_sds((s, D_MODEL)), _sds((1, D_MODEL)), _sds((1, LANE))],
        compiler_params=_params(("arbitrary",)),
    )(x, g, tgt)


def _outproj_bwd(dout, y, w, proj, y_ssd, o, g_ssd, dep=None):
    s = dout.shape[0]
    ts = _tile(s)

    def body(dout_ref, y_ref, w_ref, p_ref, ys_ref, o_ref, g_ref,
             dya_ref, dys_ref, dsz_ref, dattn_ref, dcz_ref, dg_ref, dw_ref):
        @pl.when(pl.program_id(0) == 0)
        def _():
            dw_ref[...] = jnp.zeros_like(dw_ref)
            dg_ref[...] = jnp.zeros_like(dg_ref)

        dout_b = dout_ref[...].astype(MXU)
        dw_ref[...] += _dot_tn(y_ref[...], dout_b)
        dy = _dot_nt(dout_b, w_ref[...])
        dya_ref[...] = dy[:, :D_CONV_A]
        dyb = dy[:, D_CONV_A:D_CONV_A + D_SSD]
        sz = p_ref[:, O_SZ:O_SZ + D_SSD]
        ys = ys_ref[...]
        gv = g_ref[...]
        _, nrm, r, g0 = _ssd_gate(ys, sz, gv)
        dg_ref[...] += jnp.sum(dyb * nrm, axis=0, keepdims=True)
        dn = dyb * gv
        t = dn * nrm
        mean = jnp.where(g0, jnp.sum(jnp.where(g0, t, 0.0), axis=1, keepdims=True),
                         jnp.sum(jnp.where(g0, 0.0, t), axis=1, keepdims=True)) / (D_SSD // 2)
        dyz = r * (dn - nrm * mean)
        dys_ref[...] = dyz * _silu(sz)
        dsz_ref[...] = dyz * ys * _dsilu(sz)
        dyc = dy[:, D_CONV_A + D_SSD:]
        cz = p_ref[:, O_CZ:O_CZ + D_MLA]
        dattn_ref[...] = dyc * _silu(cz)
        dcz_ref[...] = dyc * o_ref[...] * _dsilu(cz)

    return _call_after(
        dep, body, (dout, y, w, proj, y_ssd, o, g_ssd), name="outproj_bwd", grid=(s // ts,),
        in_specs=[_row(ts, D_MODEL), _row(ts, D_MODEL), _full((D_MODEL, D_MODEL)), _row(ts, NCOL), _row(ts, D_SSD),
                  _row(ts, D_MLA), _full((1, D_SSD))],
        out_specs=[_row(ts, D_CONV_A), _row(ts, D_SSD), _row(ts, D_SSD), _row(ts, D_MLA), _row(ts, D_MLA),
                   _full((1, D_SSD)), _full((D_MODEL, D_MODEL))],
        out_shape=[_sds((s, D_CONV_A)), _sds((s, D_SSD)), _sds((s, D_SSD)), _sds((s, D_MLA)), _sds((s, D_MLA)),
                   _sds((1, D_SSD)), _sds((D_MODEL, D_MODEL))],
        compiler_params=_params(("arbitrary",)),
    )


def _attn_bwd(q, k, v, o, d_o, lse, dep=None):
    nh, s, _ = q.shape
    tq = _att_tile(s)
    nq = s // tq

    def body(q_ref, k_ref, v_ref, o_ref, do_ref, lse_ref, dq_ref, dk_ref, dv_ref, dop, delta):
        i = pl.program_id(1)

        @pl.when(i == 0)
        def _():
            lane = _iota((s, LANE), 1)
            for hh in range(2):
                dov = do_ref[...]
                ov = o_ref[...]
                if hh == 1:
                    dov = pltpu.roll(dov, V_DIM, 1)
                    ov = pltpu.roll(ov, V_DIM, 1)
                dov = jnp.where(lane < V_DIM, dov, 0.0)
                dop[hh] = dov.astype(MXU)
                delta[hh] = jnp.sum(dov * ov, axis=1, keepdims=True)
                dq_ref[hh] = jnp.zeros((s, LANE), F32)

        rowi = _iota((tq, tq), 0)
        coli = _iota((tq, tq), 1)
        z = jnp.zeros((tq, LANE), F32)
        state = [(z, z), (z, z)]
        done = [(z, z), (z, z)]
        for t in range(nq + 1):
            first = t <= nq - 1 - i
            kblk = jnp.where(first, i, nq - 1 - i)
            qblk = jnp.where(first, i + t, t - 1)
            qoff = pl.multiple_of(qblk * tq, tq)
            koff = pl.multiple_of(kblk * tq, tq)
            keep = coli <= rowi + jnp.where(kblk == qblk, 0, tq)
            restart = t == nq - i
            for hh in range(2):
                dk, dv = state[hh]
                if t > 0:
                    done[hh] = tuple(jnp.where(restart, a, b) for a, b in zip(state[hh], done[hh]))
                    dk = jnp.where(restart, 0.0, dk)
                    dv = jnp.where(restart, 0.0, dv)
                kb = k_ref[hh, pl.ds(koff, tq), :]
                qb = q_ref[hh, pl.ds(qoff, tq), :]
                dob = dop[hh, pl.ds(qoff, tq), :]
                sc = jnp.where(keep, _dot_nt(qb, kb), NEG)
                p = jnp.exp(sc - lse_ref[hh, pl.ds(qoff, tq), :])
                dp = _dot_nt(dob, v_ref[hh, pl.ds(koff, tq), :])
                ds = p * (dp - delta[hh, pl.ds(qoff, tq), :])
                dq_ref[hh, pl.ds(qoff, tq), :] += _dot(ds, kb)
                state[hh] = (dk + _dot_tn(ds, qb), dv + _dot_tn(p, dob))
        for blk, res in ((i, done), (nq - 1 - i, state)):
            off = pl.multiple_of(blk * tq, tq)
            for hh in range(2):
                dk_ref[hh, pl.ds(off, tq), :] = res[hh][0]
                dv_ref[hh, pl.ds(off, tq), :] = res[hh][1]

    pair = pl.BlockSpec((2, s, LANE), lambda j, i: (j, 0, 0))
    return _call_after(
        dep, body, (q, k, v, o, d_o, lse), name="attn_bwd", grid=(nh // 2, nq // 2),
        in_specs=[pair, pair, pair, pl.BlockSpec((s, LANE), lambda j, i: (0, j)), pl.BlockSpec((s, LANE), lambda j, i: (0, j)),
                  pl.BlockSpec((2, s, 1), lambda j, i: (j, 0, 0))],
        out_specs=[pair, pair, pair],
        out_shape=[_sds((nh, s, LANE))] * 3,
        scratch_shapes=[pltpu.VMEM((2, s, LANE), MXU), pltpu.VMEM((2, s, 1), F32)],
        compiler_params=_params(("parallel", "arbitrary")),
    )


def _ssd_bwd(xbc, proj, sc, states, dy, dep=None):
    s = xbc.shape[0]
    nc = s // SSD_CHUNK
    l = SSD_CHUNK

    def body(xbc_ref, tail_ref, sc_ref, st_ref, dy_ref, dxbc_ref, dtail_ref, dsc_ref, dstate):
        @pl.when(pl.program_id(0) == 0)
        def _():
            dstate[...] = jnp.zeros_like(dstate)
            dsc_ref[...] = jnp.zeros_like(dsc_ref)

        sc_v = sc_ref[...]
        lane, row, tri, a_row, pre, dt, a_cs, a_t = _ssd_chunk_common(tail_ref[...], sc_v)
        lane1 = _iota((1, LANE), 1)
        rowp = _iota((LANE, 1), 0)
        rowl = _iota((l, 1), 0)
        d_row = sc_v[2:3, :]
        da_col = jnp.zeros((l, LANE), F32)
        da_row = jnp.zeros((LANE, l), F32)
        dt_x = jnp.zeros((l, LANE), F32)
        dd_row = jnp.zeros((1, LANE), F32)
        db = [jnp.zeros((l, LANE), F32), jnp.zeros((l, LANE), F32)]
        dc = [jnp.zeros((l, LANE), F32), jnp.zeros((l, LANE), F32)]
        for j in range(3):
            xpair = xbc_ref[:, LANE * j:LANE * (j + 1)]
            dypair = dy_ref[:, LANE * j:LANE * (j + 1)]
            sp = st_ref[0, j]
            dsp = dstate[j]
            dxpair = jnp.zeros((l, LANE), F32)
            ds_new = jnp.zeros((LANE, LANE), F32)
            decay = jnp.zeros((LANE, 1), F32)
            for half in range(2):
                h = 2 * j + half
                g = h // 3
                hm = (lane < 64) if half == 0 else (lane >= 64)
                hrow = (rowp < 64) if half == 0 else (rowp >= 64)
                ac = _pick_col(a_cs, lane, DT_LANE + h)
                ar = _pick_row(a_t, row, DT_LANE + h)
                dtc = _pick_col(dt, lane, DT_LANE + h)
                alast = jnp.sum(jnp.where(lane1 == l - 1, ar, 0.0), axis=1, keepdims=True)
                dh = jnp.sum(jnp.where(lane1 == DT_LANE + h, d_row, 0.0), axis=1, keepdims=True)
                xm = jnp.where(hm, xpair, 0.0)
                xd = xm * dtc
                dym = jnp.where(hm, dypair, 0.0)
                bm = xbc_ref[:, D_SSD + LANE * g:D_SSD + LANE * (g + 1)]
                cm = xbc_ref[:, D_SSD + SSD_BC + LANE * g:D_SSD + SSD_BC + LANE * (g + 1)]
                lm = jnp.where(row >= lane, jnp.exp(jnp.minimum(ac - ar, 0.0)), 0.0)
                e_in = jnp.exp(ac)
                f_out = jnp.exp(alast - ac)
                e_last = jnp.exp(alast)
                m = _dot_nt(cm, bm) * lm
                y_off = jnp.where(hm, _dot_nt(cm, sp), 0.0) * e_in
                dm = _dot_nt(dym, xd)
                dxd = _dot_tn(m, dym)
                dg = dm * lm
                dye = dym * e_in
                dc[g] = dc[g] + _dot(dg, bm) + _dot(dye, sp)
                db[g] = db[g] + _dot_tn(dg, cm)
                qm = dm * m
                dac = jnp.sum(qm, axis=1, keepdims=True) + jnp.sum(dym * y_off, axis=1, keepdims=True)
                dar = -jnp.sum(qm, axis=0, keepdims=True)
                dxf = jnp.where(hm, _dot_nt(bm, dsp), 0.0)
                db[g] = db[g] + _dot(xd * f_out, dsp)
                dxd = dxd + dxf * f_out
                df = jnp.sum(dxf * xd, axis=1, keepdims=True) * f_out
                dac = dac - df
                s_last = jnp.sum(df, axis=0, keepdims=True)
                ss = jnp.sum(jnp.where(hrow, dsp * sp, 0.0), axis=1, keepdims=True)
                s_last = s_last + e_last * jnp.sum(ss, axis=0, keepdims=True)
                dac = dac + jnp.where(rowl == l - 1, s_last, 0.0)
                ds_new = ds_new + _dot_tn(dye, cm)
                decay = jnp.where(hrow, e_last, decay)
                dxpair = dxpair + dxd * dtc + dym * dh
                dt_x = dt_x + jnp.where(lane == DT_LANE + h, jnp.sum(dxd * xm, axis=1, keepdims=True), 0.0)
                dsum = jnp.sum(jnp.sum(dym * xm, axis=1, keepdims=True), axis=0, keepdims=True)
                dd_row = dd_row + jnp.where(lane1 == DT_LANE + h, dsum, 0.0)
                da_col = da_col + jnp.where(lane == DT_LANE + h, dac, 0.0)
                da_row = da_row + jnp.where(row == DT_LANE + h, dar, 0.0)
            dstate[j] = dsp * decay + ds_new
            dxbc_ref[:, LANE * j:LANE * (j + 1)] = dxpair
        for g in range(2):
            dxbc_ref[:, D_SSD + LANE * g:D_SSD + LANE * (g + 1)] = db[g]
            dxbc_ref[:, D_SSD + SSD_BC + LANE * g:D_SSD + SSD_BC + LANE * (g + 1)] = dc[g]
        dla = _dot_hi_tn(tri, da_col + da_row.T)
        ddt = dt_x + dla * a_row
        dpre = ddt * _sigmoid(pre)
        dtm = (lane >= DT_LANE) & (lane < DT_LANE + SSD_HEADS)
        dtail_ref[...] = jnp.where(dtm, dpre, 0.0)
        dtm1 = (lane1 >= DT_LANE) & (lane1 < DT_LANE + SSD_HEADS)
        dsc_ref[0:1, :] += jnp.where(dtm1, jnp.sum(dpre, axis=0, keepdims=True), 0.0)
        dsc_ref[1:2, :] += jnp.where(dtm1, jnp.sum(dla * dt, axis=0, keepdims=True) * a_row, 0.0)
        dsc_ref[2:3, :] += dd_row

    rev = lambda c: nc - 1 - c
    return _call_after(
        dep, body, (xbc, proj, sc, states, dy), name="ssd_bwd", grid=(nc,),
        in_specs=[pl.BlockSpec((l, N_XBC), lambda c: (rev(c), 0)),
                  pl.BlockSpec((l, LANE), lambda c: (rev(c), O_TAIL // LANE)), _full((8, LANE)),
                  pl.BlockSpec((1, 3, LANE, LANE), lambda c: (rev(c), 0, 0, 0)),
                  pl.BlockSpec((l, D_SSD), lambda c: (rev(c), 0))],
        out_specs=[pl.BlockSpec((l, N_XBC), lambda c: (rev(c), 0)), pl.BlockSpec((l, LANE), lambda c: (rev(c), 0)),
                   _full((8, LANE))],
        out_shape=[_sds((s, N_XBC)), _sds((s, LANE)), _sds((8, LANE))],
        scratch_shapes=[pltpu.VMEM((3, LANE, LANE), F32)],
        compiler_params=_params(("arbitrary",)),
    )


def _sconv_bwd(proj, w, b, dxbc, dep=None):
    s = proj.shape[0]

    def body(u_ref, w_ref, b_ref, d_ref, du_ref, dw_ref, db_ref):
        u = u_ref[...]
        wv = w_ref[...]
        dpre = d_ref[...] * _dsilu(_sconv_pre(u, wv, b_ref[...]))
        du_ref[...] = (wv[3:4, :] * dpre + wv[2:3, :] * _shift_up(dpre, 1) + wv[1:2, :] * _shift_up(dpre, 2)
                       + wv[0:1, :] * _shift_up(dpre, 3))
        for k in range(4):
            dw_ref[k:k + 1, :] = jnp.sum(dpre * _shift_down(u, 3 - k), axis=0, keepdims=True)
        db_ref[...] = jnp.sum(dpre, axis=0, keepdims=True)

    blk = pl.BlockSpec((s, LANE), lambda j: (0, j))
    return _call_after(
        dep, body, (proj, w, b, dxbc), name="sconv_bwd", grid=(N_XBC // LANE,),
        in_specs=[_col(s, O_XBC), pl.BlockSpec((4, LANE), lambda j: (0, j)), pl.BlockSpec((1, LANE), lambda j: (0, j)), blk],
        out_specs=[blk, pl.BlockSpec((4, LANE), lambda j: (0, j)), pl.BlockSpec((1, LANE), lambda j: (0, j))],
        out_shape=[_sds((s, N_XBC)), _sds((4, N_XBC)), _sds((1, N_XBC))],
        compiler_params=_params(("parallel",)),
    )


def _conva_bwd(proj, w, dya, dep=None):
    s = proj.shape[0]

    def body(h_ref, b_ref, c_ref, z_ref, w_ref, d_ref, da_ref, dw_ref):
        ah, ab, acv, az = h_ref[...], b_ref[...], c_ref[...], z_ref[...]
        wv = w_ref[...]
        u = acv * ah
        cv = wv[2:3, :] * u + wv[1:2, :] * _shift_down(u, 1) + wv[0:1, :] * _shift_down(u, 2)
        dy = d_ref[...]
        sz = _silu(az)
        da_ref[1] = dy * cv * sz
        da_ref[3] = dy * ab * cv * _dsilu(az)
        dcv = dy * ab * sz
        du = wv[2:3, :] * dcv + wv[1:2, :] * _shift_up(dcv, 1) + wv[0:1, :] * _shift_up(dcv, 2)
        da_ref[0] = du * acv
        da_ref[2] = du * ah
        for k in range(3):
            dw_ref[k:k + 1, :] = jnp.sum(dcv * _shift_down(u, 2 - k), axis=0, keepdims=True)

    return _call_after(
        dep, body, (proj, proj, proj, proj, w, dya), name="conva_bwd", grid=(D_CONV_A // LANE,),
        in_specs=[_col(s, O_AH), _col(s, O_AB), _col(s, O_AC), _col(s, O_AZ), pl.BlockSpec((3, LANE), lambda j: (0, j)),
                  pl.BlockSpec((s, LANE), lambda j: (0, j))],
        out_specs=[pl.BlockSpec((4, s, LANE), lambda j: (0, 0, j)), pl.BlockSpec((3, LANE), lambda j: (0, j))],
        out_shape=[_sds((4, s, D_CONV_A)), _sds((3, D_CONV_A))],
        compiler_params=_params(("parallel",)),
    )


def _mla_prep_bwd(dq, dk, dv, proj, qn, kvn, rq, rkv, gq, gkv, wq, wkv, cos, sin):
    s = proj.shape[0]
    ts = _tile(s)
    nh = MLA_HEADS

    def body(dq_ref, dk_ref, dv_ref, cqa_ref, ckv_ref, qn_ref, kvn_ref, rq_ref, rkv_ref, gq_ref, gkv_ref,
             wq_ref, wkv_ref, cos_ref, sin_ref, dcqa_ref, dckv_ref, dtail_ref, dwq_ref, dwkv_ref, dgq_ref, dgkv_ref):
        @pl.when(pl.program_id(0) == 0)
        def _():
            dwq_ref[...] = jnp.zeros_like(dwq_ref)
            dwkv_ref[...] = jnp.zeros_like(dwkv_ref)
            dgq_ref[...] = jnp.zeros_like(dgq_ref)
            dgkv_ref[...] = jnp.zeros_like(dgkv_ref)

        cosv = cos_ref[...]
        sinv = sin_ref[...]
        lane = _iota((ts, LANE), 1)
        rope_lanes = (lane >= ROPE_LANE) & (lane < ROPE_LANE + QK_ROPE)

        def unrope(gr):
            return gr * cosv + _rope_swap(gr * sinv)

        dqs, dks, dvs = [], [], []
        dkr = jnp.zeros((ts, LANE), F32)
        for h in range(nh):
            dqs.append(unrope(dq_ref[h] * ATT_SCALE).astype(MXU))
            dkh = dk_ref[h]
            dks.append(jnp.where(lane < QK_NOPE, dkh, 0.0).astype(MXU))
            dkr = dkr + jnp.where(rope_lanes, dkh, 0.0)
            dvs.append(dv_ref[h].astype(MXU))
        dtail_ref[...] = pltpu.roll(jnp.where(rope_lanes, unrope(dkr), 0.0), ROPE_LANE, 1)
        dq_all = jnp.concatenate(dqs, axis=1)
        dkv_all = jnp.concatenate(dks + dvs, axis=1)
        dwq_ref[...] += _dot_tn(dq_all, qn_ref[...])
        dwkv_ref[...] += _dot_tn(dkv_all, kvn_ref[...])
        dcqa, dgq = _rms_bwd(_dot(dq_all, wq_ref[...]), cqa_ref[...], rq_ref[...], gq_ref[...])
        dckv, dgkv = _rms_bwd(_dot(dkv_all, wkv_ref[...]), ckv_ref[...], rkv_ref[...], gkv_ref[...])
        dcqa_ref[...] = dcqa
        dckv_ref[...] = dckv
        dgq_ref[...] += dgq
        dgkv_ref[...] += dgkv

    head = pl.BlockSpec((nh, ts, LANE), lambda i: (0, i, 0))
    return pl.pallas_call(
        body, name="mla_prep_bwd", grid=(s // ts,),
        in_specs=[head, head, head,
                  pl.BlockSpec((ts, Q_LORA), lambda i: (i, O_CQA // Q_LORA)),
                  pl.BlockSpec((ts, KV_LORA), lambda i: (i, O_CKV // KV_LORA)),
                  _row(ts, Q_LORA), _row(ts, KV_LORA), _row(ts, 1), _row(ts, 1),
                  _full((1, Q_LORA)), _full((1, KV_LORA)), _full((nh * LANE, Q_LORA)), _full((2 * nh * LANE, KV_LORA)),
                  _row(ts, LANE), _row(ts, LANE)],
        out_specs=[_row(ts, Q_LORA), _row(ts, KV_LORA), _row(ts, LANE), _full((nh * LANE, Q_LORA)),
                   _full((2 * nh * LANE, KV_LORA)), _full((1, Q_LORA)), _full((1, KV_LORA))],
        out_shape=[_sds((s, Q_LORA)), _sds((s, KV_LORA)), _sds((s, LANE)), _sds((nh * LANE, Q_LORA)),
                   _sds((2 * nh * LANE, KV_LORA)), _sds((1, Q_LORA)), _sds((1, KV_LORA))],
        compiler_params=_params(("arbitrary",)),
    )(dq, dk, dv, proj, proj, qn, kvn, rq, rkv, gq, gkv, wq, wkv, cos, sin)


def _inproj_bwd(da4, dsz, dxbc_in, dcqa, dckv, dcz, dtail_a, dtail_b, w, x, rstd, g, dout):
    s = x.shape[0]
    ts = _tile(s)

    def body(da_ref, dsz_ref, dxbc_ref, dcqa_ref, dckv_ref, dcz_ref, dta_ref, dtb_ref, w_ref, x_ref, r_ref, g_ref, dout_ref,
             dproj_ref, dx_ref, dg_ref):
        @pl.when(pl.program_id(0) == 0)
        def _():
            dg_ref[...] = jnp.zeros_like(dg_ref)

        dproj = jnp.concatenate(
            [da_ref[0], da_ref[1], da_ref[2], da_ref[3], dsz_ref[...], dxbc_ref[...], dcqa_ref[...], dckv_ref[...],
             dcz_ref[...], dta_ref[...] + dtb_ref[...]], axis=1).astype(MXU)
        dproj_ref[...] = dproj
        dh = _dot_nt(dproj, w_ref[...])
        dx, dg = _rms_bwd(dh, x_ref[...], r_ref[...], g_ref[...])
        dx_ref[...] = dout_ref[...] + dx
        dg_ref[...] += dg

    return pl.pallas_call(
        body, name="inproj_bwd", grid=(s // ts,),
        in_specs=[pl.BlockSpec((4, ts, D_CONV_A), lambda i: (0, i, 0)), _row(ts, D_SSD), _row(ts, N_XBC), _row(ts, Q_LORA),
                  _row(ts, KV_LORA), _row(ts, D_MLA), _row(ts, LANE), _row(ts, LANE), _full((D_MODEL, NCOL)),
                  _row(ts, D_MODEL), _row(ts, 1), _full((1, D_MODEL)), _row(ts, D_MODEL)],
        out_specs=[_row(ts, NCOL), _row(ts, D_MODEL), _full((1, D_MODEL))],
        out_shape=[_sds((s, NCOL), MXU), _sds((s, D_MODEL)), _sds((1, D_MODEL))],
        compiler_params=_params(("arbitrary",)),
    )(da4, dsz, dxbc_in, dcqa, dckv, dcz, dtail_a, dtail_b, w, x, rstd, g, dout)


DWIN_BLOCK = 640


def _dwin(h, dproj):
    s = h.shape[0]

    def body(h_ref, d_ref, o_ref):
        o_ref[...] = _dot_tn(h_ref[...], d_ref[...])

    return pl.pallas_call(
        body, name="dwin", grid=(NCOL // DWIN_BLOCK,),
        in_specs=[_full((s, D_MODEL)), pl.BlockSpec((s, DWIN_BLOCK), lambda j: (0, j))],
        out_specs=pl.BlockSpec((D_MODEL, DWIN_BLOCK), lambda j: (0, j)),
        out_shape=_sds((D_MODEL, NCOL)),
        compiler_params=_params(("parallel",)),
    )(h, dproj)


def _adamw(w, g, m, v):
    bc1 = 1.0 - ADAM_B1 ** ADAM_STEP
    bc2 = 1.0 - ADAM_B2 ** ADAM_STEP

    def body(w_ref, g_ref, m_ref, v_ref, d_ref, mo_ref, vo_ref):
        gv = g_ref[...]
        mn = ADAM_B1 * m_ref[...] + (1.0 - ADAM_B1) * gv
        vn = ADAM_B2 * v_ref[...] + (1.0 - ADAM_B2) * (gv * gv)
        mo_ref[...] = mn
        vo_ref[...] = vn
        d_ref[...] = -ADAM_LR * ((mn / bc1) / (jnp.sqrt(vn / bc2) + ADAM_EPS) + ADAM_WD * w_ref[...])

    if w.ndim == 2:
        grid, blk = (1,), pl.BlockSpec(w.shape, lambda i: (0, 0))
    else:
        grid, blk = (w.shape[0],), pl.BlockSpec((1,) + w.shape[1:], lambda i: (i, 0, 0))
    return pl.pallas_call(
        body, name="adamw", grid=grid,
        in_specs=[blk] * 4, out_specs=[blk] * 3, out_shape=[_sds(w.shape)] * 3,
        compiler_params=_params(("parallel",)),
    )(w, g, m, v)


COL_MOVES = ((0, 0, 2304), (2304, 3104, 6), (2310, 2304, 256), (2566, 2560, 128), (2694, 3072, 32), (2726, 2688, 384))


def _move_cols(w, moves, width):
    out = None
    for src, dst, n in moves:
        piece = jnp.pad(w[..., src:src + n], [(0, 0)] * (w.ndim - 1) + [(dst, width - dst - n)])
        out = piece if out is None else out + piece
    return out


def _perm_cols(w):
    return _move_cols(w, COL_MOVES, NCOL)


def _unperm_cols(g):
    return _move_cols(g, [(dst, src, n) for src, dst, n in COL_MOVES], IN_COLS)


def _wq_layout(wt):
    return jnp.pad(wt.reshape(MLA_HEADS, QK_NOPE + QK_ROPE, Q_LORA), ((0, 0), (0, 32), (0, 0))).reshape(MLA_HEADS * LANE, Q_LORA)


def _wq_unlayout(g):
    return g.reshape(MLA_HEADS, LANE, Q_LORA)[:, :QK_NOPE + QK_ROPE].reshape(MLA_HEADS * (QK_NOPE + QK_ROPE), Q_LORA)


def _wkv_layout(wt):
    t = wt.reshape(MLA_HEADS, 2, 64, KV_LORA).transpose(1, 0, 2, 3)
    return jnp.pad(t, ((0, 0), (0, 0), (0, 64), (0, 0))).reshape(2 * MLA_HEADS * LANE, KV_LORA)


def _wkv_unlayout(g):
    t = g.reshape(2, MLA_HEADS, LANE, KV_LORA)[:, :, :64]
    return t.transpose(1, 0, 2, 3).reshape(MLA_HEADS * LANE, KV_LORA)


def _rope_tables(positions):
    inv_freq = ROPE_BASE ** (-jnp.arange(0, QK_ROPE, 2, dtype=F32) / QK_ROPE)
    ang = positions.astype(F32)[:, None] * inv_freq
    cos, sin = jnp.cos(ang), jnp.sin(ang)
    s = positions.shape[0]
    one, zero = jnp.ones((s, ROPE_LANE), F32), jnp.zeros((s, ROPE_LANE), F32)
    cos_t = jnp.concatenate([one, cos, cos, one[:, :32]], axis=1)
    sin_t = jnp.concatenate([zero, -sin, sin, zero[:, :32]], axis=1)
    return cos_t, sin_t


def _ssd_scalars(dt_bias, a_log, d_skip):
    return jnp.pad(jnp.stack([dt_bias, a_log, d_skip]), ((0, 5), (DT_LANE, LANE - DT_LANE - SSD_HEADS)))


def _layer_fwd(x, lw, cos, sin, dep=None, late=None):
    proj, h, rstd = _inproj_fwd(x, lw["norm_g"], lw["w_in"], dep)
    ya = _conva_fwd(proj, lw["conv_a_w"])
    xbc = _sconv_fwd(proj, lw["ssd_conv_w"], lw["ssd_conv_b"])
    y_ssd, states = _ssd_fwd(xbc, proj, lw["sc"])
    if late is not None:
        lw = {**lw, **late(ya, y_ssd)}
    q, k, v, qn, kvn, rq, rkv = _mla_prep_fwd(proj, lw["gq"], lw["gkv"], lw["wq"], lw["wkv"], cos, sin)
    o, lse = _attn_fwd(q, k, v)
    x_out, y = _outproj_fwd(x, proj, ya, y_ssd, o, lw["g_ssd"], lw["w_out"])
    saved = dict(x=x, proj=proj, h=h, rstd=rstd, xbc=xbc, y_ssd=y_ssd, states=states, q=q, k=k, v=v, qn=qn, kvn=kvn,
                 rq=rq, rkv=rkv, o=o, lse=lse, y=y)
    return x_out, saved, lw


def _layer_bwd(dout, lw, sv, cos, sin, rs=None):
    tok = lambda: None if rs is None else rs["h"]["token"]
    dya, dys, dsz, d_o, dcz, dg_ssd, dw_out = _outproj_bwd(dout, sv["y"], lw["w_out"], sv["proj"], sv["y_ssd"], sv["o"],
                                                            lw["g_ssd"], tok())
    if rs is not None:
        rs = _rs_add_mine(rs, [dya])
    dq, dk, dv = _attn_bwd(sv["q"], sv["k"], sv["v"], sv["o"], d_o, sv["lse"], tok())
    dxbc, dtail_s, dsc = _ssd_bwd(sv["xbc"], sv["proj"], lw["sc"], sv["states"], dys, tok())
    da4, dw_conva = _conva_bwd(sv["proj"], lw["conv_a_w"], dya, tok())
    if rs is not None:
        rs = _rs_add_chips(rs, [dq, dxbc, da4])
    du, dw_sconv, db_sconv = _sconv_bwd(sv["proj"], lw["ssd_conv_w"], lw["ssd_conv_b"], dxbc, tok())
    dcqa, dckv, dtail_m, dwq, dwkv, dgq, dgkv = _mla_prep_bwd(
        dq, dk, dv, sv["proj"], sv["qn"], sv["kvn"], sv["rq"], sv["rkv"], lw["gq"], lw["gkv"], lw["wq"], lw["wkv"], cos, sin)
    dproj, dx, dg = _inproj_bwd(da4, dsz, du, dcqa, dckv, dcz, dtail_s, dtail_m, lw["w_in"], sv["x"], sv["rstd"],
                                lw["norm_g"], dout)
    reduced = None if rs is None else _rs_end(rs, [du, dcqa, dx])
    dw_in = _dwin(sv["h"], dproj)
    grads = dict(norm_g=dg, w_in=dw_in, conv_a_w=dw_conva, ssd_conv_w=dw_sconv, ssd_conv_b=db_sconv, sc=dsc,
                 g_ssd=dg_ssd, gq=dgq, wq=dwq, gkv=dgkv, wkv=dwkv, w_out=dw_out)
    return dx, grads, reduced


ANY = pl.BlockSpec(memory_space=pl.ANY)
N_CHIPS = 4
N_DEV = 8


def _place():
    return lax.axis_index("x"), lax.axis_index("y"), lax.axis_index("c")


HBM_SPEC = pl.BlockSpec(memory_space=pltpu.HBM)
SEM_SPEC = pl.BlockSpec(memory_space=pltpu.SEMAPHORE)
PAYLOAD = jnp.bfloat16


def _hbm(a):
    return pltpu.with_memory_space_constraint(a, pltpu.HBM)


def _run_plan(plan, srcs, lands, send_sems, recv_sems, start, wait):
    copies = plan(srcs, lands)
    if start:
        for i, (src, dst, _, to) in enumerate(copies):
            pltpu.make_async_remote_copy(src_ref=src, dst_ref=dst, send_sem=send_sems.at[i], recv_sem=recv_sems.at[i],
                                         device_id=to, device_id_type=MESH_T).start()
    if wait:
        for i, (src, _, arrives, to) in enumerate(copies):
            cp = pltpu.make_async_remote_copy(src_ref=src, dst_ref=arrives, send_sem=send_sems.at[i],
                                              recv_sem=recv_sems.at[i], device_id=to, device_id_type=MESH_T)
            cp.wait_send()
            cp.wait_recv()


def _exchange_fused(name, plan, n_copies, srcs, land_shapes):
    ns, nl = len(srcs), len(land_shapes)

    def body(*refs):
        _run_plan(plan, refs[:ns], refs[ns:ns + nl], refs[ns + nl], refs[ns + nl + 1], True, True)

    return pl.pallas_call(
        body, name=name, in_specs=[ANY] * ns, out_specs=[ANY] * nl, out_shape=list(land_shapes),
        scratch_shapes=[pltpu.SemaphoreType.DMA((n_copies,)), pltpu.SemaphoreType.DMA((n_copies,))],
    )(*srcs)


def _exchange_start(name, plan, n_copies, srcs, land_shapes, deps):
    ns, nl = len(srcs), len(land_shapes)
    n_in = ns + nl + len(deps)

    def body(*refs):
        send_sems, recv_sems = refs[n_in], refs[n_in + 1]
        token = refs[-1]
        _run_plan(plan, refs[:ns], refs[ns:ns + nl], send_sems, recv_sems, True, False)
        token[...] = jnp.zeros_like(token)

    thru = [pltpu.HBM(a.shape, a.dtype) for a in srcs] + [pltpu.HBM(a.shape, a.dtype) for a in land_shapes]
    outs = pl.pallas_call(
        body, name=name,
        out_shape=(pltpu.SemaphoreType.DMA((n_copies,)), pltpu.SemaphoreType.DMA((n_copies,)), *thru, _sds((8, LANE))),
        in_specs=[HBM_SPEC] * (ns + nl) + [ANY] * len(deps),
        out_specs=(SEM_SPEC, SEM_SPEC, *[HBM_SPEC] * (ns + nl), pl.BlockSpec(memory_space=pltpu.VMEM)),
        input_output_aliases={i: 2 + i for i in range(ns + nl)},
        compiler_params=pltpu.CompilerParams(has_side_effects=pltpu.SideEffectType.DATAFLOW_SIDE_EFFECTING),
    )(*[_hbm(a) for a in srcs], *[_hbm(lax.empty(a.shape, a.dtype)) for a in land_shapes], *deps)
    return (outs[0], outs[1]), list(outs[2:2 + ns]), list(outs[2 + ns:2 + ns + nl]), outs[-1]


def _exchange_wait(name, plan, sems, srcs, lands, after):
    ns, nl = len(srcs), len(lands)

    def body(*refs):
        _run_plan(plan, refs[:ns], refs[ns:ns + nl], refs[ns + nl], refs[ns + nl + 1], False, True)

    outs = pl.pallas_call(
        body, name=name,
        out_shape=[pltpu.HBM(a.shape, a.dtype) for a in list(srcs) + list(lands)],
        in_specs=[HBM_SPEC] * (ns + nl) + [SEM_SPEC, SEM_SPEC] + [ANY] * len(after), out_specs=[HBM_SPEC] * (ns + nl),
        input_output_aliases={i: i for i in range(ns + nl)},
        compiler_params=pltpu.CompilerParams(has_side_effects=pltpu.SideEffectType.DATAFLOW_SIDE_EFFECTING),
    )(*srcs, *lands, sems[0], sems[1], *after)
    return list(outs[:ns]), list(outs[ns:])


def _xchg_begin(name, plan, n_copies, srcs, land_shapes, split, deps=()):
    if not split:
        return dict(split=False, srcs=list(srcs), lands=_exchange_fused(name, plan, n_copies, srcs, land_shapes), token=None)
    sems, srcs_t, lands_t, token = _exchange_start(name + "_start", plan, n_copies, srcs, land_shapes, list(deps))
    return dict(split=True, name=name, plan=plan, sems=sems, srcs=srcs_t, lands=lands_t, token=token)


def _xchg_end(h, after):
    if not h["split"]:
        return h["srcs"], h["lands"]
    return _exchange_wait(h["name"] + "_wait", h["plan"], h["sems"], h["srcs"], h["lands"], after)


def _other_chips():
    x, y, c = _place()
    return [(1 - x, y), (x, 1 - y), (1 - x, 1 - y)]


def _gather_plan(srcs, lands):
    x, y, c = _place()
    me = 2 * x + y
    return [(srcs[a], lands[a].at[me], lands[a].at[2 * cx + cy], (cx, cy, c))
            for (cx, cy) in _other_chips() for a in range(len(srcs))]


def _gather_begin(shards, split, tag, deps=()):
    shapes = [_sds((N_CHIPS,) + a.shape, a.dtype) for a in shards]
    return _xchg_begin(f"gather_{tag}", _gather_plan, 3 * len(shards), shards, shapes, split, deps)


def _gather_end(h, after):
    shards, lands = _xchg_end(h, after)
    me = 2 * lax.axis_index("x") + lax.axis_index("y")
    return [lax.dynamic_update_index_in_dim(g, s, me, 0) for g, s in zip(lands, shards)]


def _swap_plan(srcs, lands):
    x, y, c = _place()
    return [(srcs[a].at[:, 1 - c], lands[a], lands[a], (x, y, 1 - c)) for a in range(len(srcs))]


def _chips_plan(srcs, lands):
    x, y, c = _place()
    me = 2 * x + y
    return [(srcs[a].at[2 * cx + cy], lands[a].at[me], lands[a].at[2 * cx + cy], (cx, cy, c))
            for (cx, cy) in _other_chips() for a in range(len(srcs))]


def _share_plan(srcs, lands):
    x, y, c = _place()
    return [(srcs[a], lands[a].at[c], lands[a].at[1 - c], (x, y, 1 - c)) for a in range(len(srcs))]


def _allreduce_small(slab, dep=None):
    r = slab.shape[0]

    def body(s_ref, o_ref, gath, send_sems, recv_sems):
        x, y, c = _place()
        me = 4 * x + 2 * y + c
        gath[me] = s_ref[...]
        cps = []
        for rel in range(1, N_DEV):
            px = 1 - x if rel & 4 else x
            py = 1 - y if rel & 2 else y
            pc = 1 - c if rel & 1 else c
            cp = pltpu.make_async_remote_copy(src_ref=s_ref, dst_ref=gath.at[me], send_sem=send_sems.at[rel - 1],
                                              recv_sem=recv_sems.at[rel - 1], device_id=(px, py, pc), device_id_type=MESH_T)
            cp.start()
            cps.append(cp)
        for cp in cps:
            cp.wait()
        acc = gath[0]
        for d in range(1, N_DEV):
            acc = acc + gath[d]
        o_ref[...] = acc

    vm = pl.BlockSpec(memory_space=pltpu.VMEM)
    return _call_after(
        dep, body, (slab,), name="allreduce_small", in_specs=[vm], out_specs=vm, out_shape=_sds((r, LANE)),
        scratch_shapes=[pltpu.VMEM((N_DEV, r, LANE), F32), pltpu.SemaphoreType.DMA((N_DEV - 1,)),
                        pltpu.SemaphoreType.DMA((N_DEV - 1,))],
    )


def _add_mine(g4, recv, half):
    _, _, rh, c = g4.shape

    def body(h_ref, g_ref, r_ref, o_ref):
        o_ref[0] = (g_ref[0, 0] + r_ref[0]).astype(o_ref.dtype)

    return pl.pallas_call(
        body, name="add_mine",
        grid_spec=pltpu.PrefetchScalarGridSpec(
            num_scalar_prefetch=1, grid=(N_CHIPS,),
            in_specs=[pl.BlockSpec((1, 1, rh, c), lambda j, h: (j, h[0], 0, 0)), pl.BlockSpec((1, rh, c), lambda j, h: (j, 0, 0))],
            out_specs=pl.BlockSpec((1, rh, c), lambda j, h: (j, 0, 0))),
        out_shape=_sds((N_CHIPS, rh, c), PAYLOAD),
        compiler_params=_params(("parallel",)),
    )(half, g4, recv)


def _add_chips(e, p, me):
    _, rh, c = e.shape

    def body(m_ref, e_ref, p_ref, o_ref):
        own = p_ref[0].astype(F32)
        acc = None
        for s in range(N_CHIPS):
            t = jnp.where(m_ref[0] == s, own, e_ref[s].astype(F32))
            acc = t if acc is None else acc + t
        o_ref[...] = acc

    return pl.pallas_call(
        body, name="add_chips",
        grid_spec=pltpu.PrefetchScalarGridSpec(
            num_scalar_prefetch=1, grid=(1,),
            in_specs=[pl.BlockSpec((N_CHIPS, rh, c), lambda i, m: (0, 0, 0)), pl.BlockSpec((1, rh, c), lambda i, m: (m[0], 0, 0))],
            out_specs=pl.BlockSpec((rh, c), lambda i, m: (0, 0))),
        out_shape=_sds((rh, c)),
        compiler_params=_params(("arbitrary",)),
    )(me, e, p)


def _rs_begin(gs, split, tag):
    g4 = [g.reshape(N_CHIPS, 2, g.shape[0] // (2 * N_CHIPS), g.shape[1]) for g in gs]
    h = _xchg_begin(f"rs_swap_{tag}", _swap_plan, len(gs), g4, [_sds((N_CHIPS,) + g.shape[2:]) for g in g4], split)
    return dict(h=h, split=split, tag=tag, shapes=[g.shape for g in gs])


def _rs_add_mine(st, after):
    g4, recv = _xchg_end(st["h"], after)
    half = jnp.reshape(lax.axis_index("c"), (1,)).astype(jnp.int32)
    ps = [_add_mine(g, r, half) for g, r in zip(g4, recv)]
    st["h"] = _xchg_begin(f"rs_chips_{st['tag']}", _chips_plan, 3 * len(ps), ps, [_sds(p.shape, p.dtype) for p in ps], st["split"])
    return st


def _rs_add_chips(st, after):
    ps, es = _xchg_end(st["h"], after)
    me = jnp.reshape(2 * lax.axis_index("x") + lax.axis_index("y"), (1,)).astype(jnp.int32)
    fs = [_add_chips(e, p, me) for e, p in zip(es, ps)]
    st["h"] = _xchg_begin(f"rs_share_{st['tag']}", _share_plan, len(fs), fs, [_sds((2,) + f.shape) for f in fs], st["split"])
    return st


def _rs_end(st, after):
    fs, ss = _xchg_end(st["h"], after)
    c = lax.axis_index("c")
    return [lax.dynamic_update_index_in_dim(s, f, c, 0).reshape(shp[0] // N_CHIPS, shp[1])
            for s, f, shp in zip(ss, fs, st["shapes"])]


WEIGHTS = ["norm_g", "w_in", "conv_a_w", "ssd_conv_w", "ssd_conv_b", "ssd_dt_bias", "ssd_a_log", "ssd_d", "ssd_norm_g",
           "mla_q_norm_g", "w_qb", "mla_kv_norm_g", "w_kvb", "w_out", "final_norm_g"]
BIG = ["w_in", "w_qb", "w_kvb", "w_out"]
SLAB_ROWS = 128
SMALL_ROWS = 72


def _to_slab(parts, rows):
    flat = jnp.concatenate([p.reshape(-1) for p in parts])
    return jnp.pad(flat, (0, rows * LANE - flat.shape[0])).reshape(rows, LANE)


def _from_slab(slab, shapes):
    flat = slab.reshape(-1)
    out, off = [], 0
    for shp in shapes:
        n = int(np.prod(shp))
        out.append(flat[off:off + n].reshape(shp))
        off += n
    return out


def kernel(x, positions, norm_g, w_in, conv_a_w, ssd_conv_w, ssd_conv_b, ssd_dt_bias, ssd_a_log, ssd_d, ssd_norm_g, mla_q_norm_g, w_qb, mla_kv_norm_g, w_kvb, w_out, final_norm_g, loss_target, m_norm_g, m_w_in, m_conv_a_w, m_ssd_conv_w, m_ssd_conv_b, m_ssd_dt_bias, m_ssd_a_log, m_ssd_d, m_ssd_norm_g, m_mla_q_norm_g, m_w_qb, m_mla_kv_norm_g, m_w_kvb, m_w_out, m_final_norm_g, v_norm_g, v_w_in, v_conv_a_w, v_ssd_conv_w, v_ssd_conv_b, v_ssd_dt_bias, v_ssd_a_log, v_ssd_d, v_ssd_norm_g, v_mla_q_norm_g, v_w_qb, v_mla_kv_norm_g, v_w_kvb, v_w_out, v_final_norm_g):
    w = dict(norm_g=norm_g, w_in=w_in, conv_a_w=conv_a_w, ssd_conv_w=ssd_conv_w, ssd_conv_b=ssd_conv_b,
             ssd_dt_bias=ssd_dt_bias, ssd_a_log=ssd_a_log, ssd_d=ssd_d, ssd_norm_g=ssd_norm_g, mla_q_norm_g=mla_q_norm_g,
             w_qb=w_qb, mla_kv_norm_g=mla_kv_norm_g, w_kvb=w_kvb, w_out=w_out, final_norm_g=final_norm_g)
    mom = dict(norm_g=m_norm_g, w_in=m_w_in, conv_a_w=m_conv_a_w, ssd_conv_w=m_ssd_conv_w, ssd_conv_b=m_ssd_conv_b,
               ssd_dt_bias=m_ssd_dt_bias, ssd_a_log=m_ssd_a_log, ssd_d=m_ssd_d, ssd_norm_g=m_ssd_norm_g,
               mla_q_norm_g=m_mla_q_norm_g, w_qb=m_w_qb, mla_kv_norm_g=m_mla_kv_norm_g, w_kvb=m_w_kvb, w_out=m_w_out,
               final_norm_g=m_final_norm_g)
    var = dict(norm_g=v_norm_g, w_in=v_w_in, conv_a_w=v_conv_a_w, ssd_conv_w=v_ssd_conv_w, ssd_conv_b=v_ssd_conv_b,
               ssd_dt_bias=v_ssd_dt_bias, ssd_a_log=v_ssd_a_log, ssd_d=v_ssd_d, ssd_norm_g=v_ssd_norm_g,
               mla_q_norm_g=v_mla_q_norm_g, w_qb=v_w_qb, mla_kv_norm_g=v_mla_kv_norm_g, w_kvb=v_w_kvb, w_out=v_w_out,
               final_norm_g=v_final_norm_g)
    chip = 2 * lax.axis_index("x") + lax.axis_index("y")

    conv_pack = jnp.pad(conv_a_w, ((0, 0), (0, 5), (0, 192))) + jnp.pad(ssd_conv_w, ((0, 0), (3, 1), (0, 32)))
    early_shards = [[_perm_cols(w_in[l]).astype(MXU), conv_pack[l]] for l in range(DEPTH)]
    late_shards = [[w_out[l].astype(MXU), w_qb[l].T.astype(MXU), w_kvb[l].T.astype(MXU)] for l in range(DEPTH)]

    def early_weights(l, gathered):
        g_in, g_conv = gathered
        return dict(
            norm_g=norm_g[l][None], w_in=g_in.reshape(D_MODEL, NCOL),
            conv_a_w=jnp.concatenate([g_conv[j, 0:3, 0:64] for j in range(N_CHIPS)], axis=1),
            ssd_conv_w=jnp.concatenate([g_conv[j, 3:7, 0:224] for j in range(N_CHIPS)], axis=1),
            ssd_conv_b=ssd_conv_b[l][None], sc=_ssd_scalars(ssd_dt_bias[l], ssd_a_log[l], ssd_d[l]),
            g_ssd=ssd_norm_g[l][None], gq=mla_q_norm_g[l][None], gkv=mla_kv_norm_g[l][None])

    def late_weights(gathered):
        g_out, g_qb, g_kvb = gathered
        return dict(wq=_wq_layout(g_qb.reshape(MLA_HEADS * 96, Q_LORA)), wkv=_wkv_layout(g_kvb.reshape(MLA_HEADS * LANE, KV_LORA)),
                    w_out=g_out.reshape(D_MODEL, D_MODEL))

    def large_grads(g):
        wq = jnp.pad(_wq_unlayout(g["wq"]).reshape(N_CHIPS, 144, Q_LORA), ((0, 0), (0, 16), (0, 0)))
        return [g["w_in"], g["w_out"], wq.reshape(N_CHIPS * 160, Q_LORA), _wkv_unlayout(g["wkv"])]

    gather_a0 = _gather_begin(early_shards[0], True, "a0")
    cos, sin = _rope_tables(positions[0])
    idle_work = [cos, sin] + late_shards[0] + early_shards[1] + late_shards[1] + [a[nm] for nm in BIG for a in (w, mom, var)]
    lw0 = early_weights(0, _gather_end(gather_a0, idle_work))
    gather_b0 = _gather_begin(late_shards[0], True, "b0")
    gather_1 = _gather_begin(early_shards[1] + late_shards[1], True, "1", [gather_b0["token"]])
    x1, sv0, lw0 = _layer_fwd(x[0], lw0, cos, sin, gather_1["token"],
                              lambda ya, y_ssd: late_weights(_gather_end(gather_b0, [ya, y_ssd])))
    g1 = _gather_end(gather_1, [x1])
    x2, sv1, lw1 = _layer_fwd(x1, {**early_weights(1, g1[:2]), **late_weights(g1[2:])}, cos, sin)
    dx, dgf, loss = _loss_head(x2, final_norm_g[None], loss_target[0])

    dx, lg1, _ = _layer_bwd(dx, lw1, sv1, cos, sin)
    grad_x, lg0, red1 = _layer_bwd(dx, lw0, sv0, cos, sin, _rs_begin(large_grads(lg1), True, 1))
    rs0 = _rs_add_mine(_rs_begin(large_grads(lg0), True, 0), [])
    lg = [lg0, lg1]
    grad = {}

    small_names = ["norm_g", "conv_a_w", "ssd_conv_w", "ssd_conv_b", "sc", "g_ssd", "gq", "gkv"]
    parts = [loss[0, 0:1], dgf]
    for l in range(DEPTH):
        parts += [lg[l][nm][:3, DT_LANE:DT_LANE + SSD_HEADS] if nm == "sc" else lg[l][nm] for nm in small_names]
    shapes = [(1,), (D_MODEL,)] + [(D_MODEL,), (3, D_CONV_A), (4, N_XBC), (N_XBC,), (3, SSD_HEADS), (D_SSD,), (Q_LORA,), (KV_LORA,)] * DEPTH
    red = _from_slab(_allreduce_small(_to_slab(parts, SLAB_ROWS), rs0["h"]["token"]), shapes)
    loss_out = red[0][0]
    grad["final_norm_g"] = red[1]
    per = [red[2 + 8 * l:10 + 8 * l] for l in range(DEPTH)]
    grad["norm_g"] = jnp.stack([per[l][0] for l in range(DEPTH)])
    grad["conv_a_w"] = lax.dynamic_slice_in_dim(jnp.stack([per[l][1] for l in range(DEPTH)]), chip * 64, 64, axis=2)
    grad["ssd_conv_w"] = lax.dynamic_slice_in_dim(jnp.stack([per[l][2] for l in range(DEPTH)]), chip * 224, 224, axis=2)
    grad["ssd_conv_b"] = jnp.stack([per[l][3] for l in range(DEPTH)])
    grad["ssd_dt_bias"] = jnp.stack([per[l][4][0] for l in range(DEPTH)])
    grad["ssd_a_log"] = jnp.stack([per[l][4][1] for l in range(DEPTH)])
    grad["ssd_d"] = jnp.stack([per[l][4][2] for l in range(DEPTH)])
    grad["ssd_norm_g"] = jnp.stack([per[l][5] for l in range(DEPTH)])
    grad["mla_q_norm_g"] = jnp.stack([per[l][6] for l in range(DEPTH)])
    grad["mla_kv_norm_g"] = jnp.stack([per[l][7] for l in range(DEPTH)])

    delta, new_m, new_v = {}, {}, {}
    small = [nm for nm in WEIGHTS if nm not in BIG]
    sshapes = [w[nm].shape for nm in small]
    d, mo, vo = _adamw(_to_slab([w[nm] for nm in small], SMALL_ROWS), _to_slab([grad[nm] for nm in small], SMALL_ROWS),
                       _to_slab([mom[nm] for nm in small], SMALL_ROWS), _to_slab([var[nm] for nm in small], SMALL_ROWS))
    small_out = list(zip(small, _from_slab(d, sshapes), _from_slab(mo, sshapes), _from_slab(vo, sshapes)))
    for nm, dv, mv, vv in small_out:
        delta[nm], new_m[nm], new_v[nm] = dv, mv, vv

    red0 = _rs_end(_rs_add_chips(rs0, [a for row in small_out for a in row[1:]] + [grad[nm] for nm in small]), [])
    r_in, r_out, r_qb, r_kvb = [jnp.stack([a, b]) for a, b in zip(red0, red1)]
    grad.update(w_in=_unperm_cols(r_in), w_out=r_out, w_qb=jnp.swapaxes(r_qb[:, :144], 1, 2), w_kvb=jnp.swapaxes(r_kvb, 1, 2))
    for nm in BIG:
        delta[nm], new_m[nm], new_v[nm] = _adamw(w[nm], grad[nm], mom[nm], var[nm])

    return (loss_out, grad_x[None], *[grad[nm] for nm in WEIGHTS], *[delta[nm] for nm in WEIGHTS],
            *[new_m[nm] for nm in WEIGHTS], *[new_v[nm] for nm in WEIGHTS])
```

```python
import functools
import math

import numpy as np
import jax
import jax.numpy as jnp
from jax import lax
from jax.experimental import pallas as pl
from jax.experimental.pallas import tpu as pltpu

F32 = jnp.float32
MXU = jnp.bfloat16

D_MODEL = 1024
DEPTH = 2
D_CONV_A = 256
D_SSD = 384
SSD_HEADS = 6
SSD_BC = 256
SSD_CHUNK = 128
SSD_NORM_EPS = 1e-5
MLA_HEADS = 6
Q_LORA = 256
KV_LORA = 128
QK_NOPE = 64
QK_ROPE = 32
V_DIM = 64
D_MLA = 384
ROPE_BASE = 10000.0
NORM_EPS = 1e-6
IN_COLS = 3110
LANE = 128

O_AH, O_AB, O_AC, O_AZ = 0, 256, 512, 768
O_SZ = 1024
O_XBC = 1408
O_CQA = 2304
O_CKV = 2560
O_CZ = 2688
O_TAIL = 3072
NCOL = 3200
N_XBC = D_SSD + 2 * SSD_BC
DT_LANE = 32
ROPE_LANE = 64

ADAM_LR, ADAM_B1, ADAM_B2, ADAM_EPS, ADAM_WD, ADAM_STEP = 0.001, 0.9, 0.999, 1e-08, 0.01, 10

VMEM_LIMIT = 56 * 1024 * 1024
MESH_T = pl.DeviceIdType.MESH


def _dot(a, b):
    return jnp.dot(a.astype(MXU), b.astype(MXU), preferred_element_type=F32)


def _dot_nt(a, b):
    return lax.dot_general(a.astype(MXU), b.astype(MXU), (((1,), (1,)), ((), ())), preferred_element_type=F32)


def _dot_tn(a, b):
    return lax.dot_general(a.astype(MXU), b.astype(MXU), (((0,), (0,)), ((), ())), preferred_element_type=F32)


def _dot_hi(a, b):
    return jnp.dot(a, b, precision=lax.Precision.HIGHEST, preferred_element_type=F32)


def _dot_hi_tn(a, b):
    return lax.dot_general(a, b, (((0,), (0,)), ((), ())), precision=lax.Precision.HIGHEST, preferred_element_type=F32)


def _sigmoid(z):
    return 1.0 / (1.0 + jnp.exp(-z))


def _silu(z):
    return z * _sigmoid(z)


def _dsilu(z):
    s = _sigmoid(z)
    return s * (1.0 + z * (1.0 - s))


def _softplus(z):
    e = jnp.exp(-jnp.abs(z))
    return jnp.maximum(z, 0.0) + jnp.where(e < 1e-3, e * (1.0 - 0.5 * e), jnp.log(1.0 + e))


def _iota(shape, dim):
    return lax.broadcasted_iota(jnp.int32, shape, dim)


def _shift_down(u, k):
    if k == 0:
        return u
    return jnp.where(_iota(u.shape, 0) >= k, pltpu.roll(u, k, 0), 0.0)


def _shift_up(u, k):
    if k == 0:
        return u
    n = u.shape[0]
    return jnp.where(_iota(u.shape, 0) < n - k, pltpu.roll(u, n - k, 0), 0.0)


def _rope_swap(t):
    lane = _iota(t.shape, 1)
    lo = (lane >= ROPE_LANE) & (lane < ROPE_LANE + 16)
    hi = (lane >= ROPE_LANE + 16) & (lane < ROPE_LANE + 32)
    return jnp.where(lo, pltpu.roll(t, LANE - 16, 1), jnp.where(hi, pltpu.roll(t, 16, 1), 0.0))


def _params(sem=None):
    return pltpu.CompilerParams(dimension_semantics=sem, vmem_limit_bytes=VMEM_LIMIT)


def _full(shape):
    nd = len(shape)
    return pl.BlockSpec(shape, lambda *_: (0,) * nd)


def _sds(shape, dtype=F32):
    return jax.ShapeDtypeStruct(shape, dtype)


def _tile(s):
    return min(256, s)


def _row(ts, w):
    return pl.BlockSpec((ts, w), lambda i: (i, 0))


def _col(s, off):
    return pl.BlockSpec((s, LANE), lambda j, _o=off // LANE: (0, _o + j))


def _call_after(dep, body, args, *, in_specs, **kw):
    if dep is None:
        return pl.pallas_call(body, in_specs=in_specs, **kw)(*args)
    n = len(args)

    def body_dep(*refs):
        body(*refs[:n], *refs[n + 1:])

    return pl.pallas_call(body_dep, in_specs=list(in_specs) + [pl.BlockSpec(memory_space=pl.ANY)], **kw)(*args, dep)


def _rms(c, g):
    r = lax.rsqrt(jnp.mean(c * c, axis=-1, keepdims=True) + NORM_EPS)
    return c * r * g, r


def _rms_bwd(dn, c, r, g):
    ch = c * r
    dch = dn * g
    dc = r * (dch - ch * jnp.mean(dch * ch, axis=-1, keepdims=True))
    return dc, jnp.sum(dn * ch, axis=0, keepdims=True)


def _inproj_fwd(x, g, w, dep=None):
    s = x.shape[0]
    ts = _tile(s)

    def body(x_ref, g_ref, w_ref, proj_ref, h_ref, r_ref):
        hn, r = _rms(x_ref[...], g_ref[...])
        h = hn.astype(MXU)
        h_ref[...] = h
        r_ref[...] = r
        proj_ref[...] = jnp.dot(h, w_ref[...], preferred_element_type=F32)

    return _call_after(
        dep, body, (x, g, w), name="inproj_fwd", grid=(s // ts,),
        in_specs=[_row(ts, D_MODEL), _full((1, D_MODEL)), _full((D_MODEL, NCOL))],
        out_specs=[_row(ts, NCOL), _row(ts, D_MODEL), _row(ts, 1)],
        out_shape=[_sds((s, NCOL)), _sds((s, D_MODEL), MXU), _sds((s, 1))],
        compiler_params=_params(("parallel",)),
    )


def _conva_fwd(proj, w):
    s = proj.shape[0]

    def body(h_ref, b_ref, c_ref, z_ref, w_ref, y_ref):
        u = c_ref[...] * h_ref[...]
        wv = w_ref[...]
        cv = wv[2:3, :] * u + wv[1:2, :] * _shift_down(u, 1) + wv[0:1, :] * _shift_down(u, 2)
        y_ref[...] = b_ref[...] * cv * _silu(z_ref[...])

    return pl.pallas_call(
        body, name="conva_fwd", grid=(D_CONV_A // LANE,),
        in_specs=[_col(s, O_AH), _col(s, O_AB), _col(s, O_AC), _col(s, O_AZ), pl.BlockSpec((3, LANE), lambda j: (0, j))],
        out_specs=pl.BlockSpec((s, LANE), lambda j: (0, j)),
        out_shape=_sds((s, D_CONV_A)),
        compiler_params=_params(("parallel",)),
    )(proj, proj, proj, proj, w)


def _sconv_pre(u, wv, bv):
    return (wv[3:4, :] * u + wv[2:3, :] * _shift_down(u, 1) + wv[1:2, :] * _shift_down(u, 2)
            + wv[0:1, :] * _shift_down(u, 3) + bv)


def _sconv_fwd(proj, w, b):
    s = proj.shape[0]

    def body(u_ref, w_ref, b_ref, o_ref):
        o_ref[...] = _silu(_sconv_pre(u_ref[...], w_ref[...], b_ref[...]))

    return pl.pallas_call(
        body, name="sconv_fwd", grid=(N_XBC // LANE,),
        in_specs=[_col(s, O_XBC), pl.BlockSpec((4, LANE), lambda j: (0, j)), pl.BlockSpec((1, LANE), lambda j: (0, j))],
        out_specs=pl.BlockSpec((s, LANE), lambda j: (0, j)),
        out_shape=_sds((s, N_XBC)),
        compiler_params=_params(("parallel",)),
    )(proj, w, b)


def _ssd_chunk_common(tail, sc):
    l = SSD_CHUNK
    lane = _iota((l, LANE), 1)
    row = _iota((l, LANE), 0)
    tri = (row >= lane).astype(F32)
    a_row = -jnp.exp(sc[1:2, :])
    pre = tail + sc[0:1, :]
    dt = _softplus(pre)
    a_cs = _dot_hi(tri, dt * a_row)
    return lane, row, tri, a_row, pre, dt, a_cs, a_cs.T


def _pick_col(m, lane, k):
    return jnp.sum(jnp.where(lane == k, m, 0.0), axis=1, keepdims=True)


def _pick_row(m, row, k):
    return jnp.sum(jnp.where(row == k, m, 0.0), axis=0, keepdims=True)


def _ssd_fwd(xbc, proj, sc):
    s = xbc.shape[0]
    nc = s // SSD_CHUNK
    l = SSD_CHUNK

    def body(xbc_ref, tail_ref, sc_ref, y_ref, st_ref, state):
        @pl.when(pl.program_id(0) == 0)
        def _():
            state[...] = jnp.zeros_like(state)

        sc_v = sc_ref[...]
        lane, row, _, _, _, dt, a_cs, a_t = _ssd_chunk_common(tail_ref[...], sc_v)
        lane1 = _iota((1, LANE), 1)
        rowp = _iota((LANE, 1), 0)
        d_row = sc_v[2:3, :]
        for j in range(3):
            st_ref[0, j] = state[j]
        for j in range(3):
            xpair = xbc_ref[:, LANE * j:LANE * (j + 1)]
            sp = state[j]
            ypair = jnp.zeros((l, LANE), F32)
            new_s = jnp.zeros((LANE, LANE), F32)
            decay = jnp.zeros((LANE, 1), F32)
            for half in range(2):
                h = 2 * j + half
                g = h // 3
                hm = (lane < 64) if half == 0 else (lane >= 64)
                hrow = (rowp < 64) if half == 0 else (rowp >= 64)
                ac = _pick_col(a_cs, lane, DT_LANE + h)
                ar = _pick_row(a_t, row, DT_LANE + h)
                dtc = _pick_col(dt, lane, DT_LANE + h)
                alast = jnp.sum(jnp.where(lane1 == l - 1, ar, 0.0), axis=1, keepdims=True)
                dh = jnp.sum(jnp.where(lane1 == DT_LANE + h, d_row, 0.0), axis=1, keepdims=True)
                xm = jnp.where(hm, xpair, 0.0)
                xd = xm * dtc
                bm = xbc_ref[:, D_SSD + LANE * g:D_SSD + LANE * (g + 1)]
                cm = xbc_ref[:, D_SSD + SSD_BC + LANE * g:D_SSD + SSD_BC + LANE * (g + 1)]
                lm = jnp.where(row >= lane, jnp.exp(jnp.minimum(ac - ar, 0.0)), 0.0)
                y_diag = _dot(_dot_nt(cm, bm) * lm, xd)
                y_off = jnp.where(hm, _dot_nt(cm, sp), 0.0) * jnp.exp(ac)
                ypair = ypair + y_diag + y_off + xm * dh
                new_s = new_s + _dot_tn(xd * jnp.exp(alast - ac), bm)
                decay = jnp.where(hrow, jnp.exp(alast), decay)
            state[j] = sp * decay + new_s
            y_ref[:, LANE * j:LANE * (j + 1)] = ypair

    return pl.pallas_call(
        body, name="ssd_fwd", grid=(nc,),
        in_specs=[pl.BlockSpec((l, N_XBC), lambda c: (c, 0)),
                  pl.BlockSpec((l, LANE), lambda c: (c, O_TAIL // LANE)), _full((8, LANE))],
        out_specs=[pl.BlockSpec((l, D_SSD), lambda c: (c, 0)), pl.BlockSpec((1, 3, LANE, LANE), lambda c: (c, 0, 0, 0))],
        out_shape=[_sds((s, D_SSD)), _sds((nc, 3, LANE, LANE))],
        scratch_shapes=[pltpu.VMEM((3, LANE, LANE), F32)],
        compiler_params=_params(("arbitrary",)),
    )(xbc, proj, sc)


def _mla_prep_fwd(proj, gq, gkv, wq, wkv, cos, sin):
    s = proj.shape[0]
    ts = _tile(s)
    nh = MLA_HEADS

    def body(cqa_ref, ckv_ref, tail_ref, gq_ref, gkv_ref, wq_ref, wkv_ref, cos_ref, sin_ref,
             q_ref, k_ref, v_ref, qn_ref, kvn_ref, rq_ref, rkv_ref):
        qn, rq = _rms(cqa_ref[...], gq_ref[...])
        kvn, rkv = _rms(ckv_ref[...], gkv_ref[...])
        qn = qn.astype(MXU)
        kvn = kvn.astype(MXU)
        qn_ref[...] = qn
        kvn_ref[...] = kvn
        rq_ref[...] = rq
        rkv_ref[...] = rkv
        q = _dot_nt(qn, wq_ref[...])
        kv = _dot_nt(kvn, wkv_ref[...])
        cosv = cos_ref[...]
        sinv = sin_ref[...]
        lane = _iota((ts, LANE), 1)
        rope_lanes = (lane >= ROPE_LANE) & (lane < ROPE_LANE + QK_ROPE)
        kr = jnp.where(rope_lanes, pltpu.roll(tail_ref[...], ROPE_LANE, 1), 0.0)
        kr = kr * cosv + _rope_swap(kr) * sinv
        for h in range(nh):
            qh = q[:, LANE * h:LANE * (h + 1)]
            q_ref[h] = ((qh * cosv + _rope_swap(qh) * sinv) * ATT_SCALE).astype(MXU)
            k_ref[h] = (kv[:, LANE * h:LANE * (h + 1)] + kr).astype(MXU)
            v_ref[h] = kv[:, LANE * (nh + h):LANE * (nh + h + 1)].astype(MXU)

    head = pl.BlockSpec((nh, ts, LANE), lambda i: (0, i, 0))
    return pl.pallas_call(
        body, name="mla_prep_fwd", grid=(s // ts,),
        in_specs=[pl.BlockSpec((ts, Q_LORA), lambda i: (i, O_CQA // Q_LORA)),
                  pl.BlockSpec((ts, KV_LORA), lambda i: (i, O_CKV // KV_LORA)),
                  pl.BlockSpec((ts, LANE), lambda i: (i, O_TAIL // LANE)),
                  _full((1, Q_LORA)), _full((1, KV_LORA)), _full((nh * LANE, Q_LORA)), _full((2 * nh * LANE, KV_LORA)),
                  _row(ts, LANE), _row(ts, LANE)],
        out_specs=[head, head, head, _row(ts, Q_LORA), _row(ts, KV_LORA), _row(ts, 1), _row(ts, 1)],
        out_shape=[_sds((nh, s, LANE), MXU)] * 3 + [_sds((s, Q_LORA), MXU), _sds((s, KV_LORA), MXU), _sds((s, 1)), _sds((s, 1))],
        compiler_params=_params(("parallel",)),
    )(proj, proj, proj, gq, gkv, wq, wkv, cos, sin)


ATT_SCALE = (QK_NOPE + QK_ROPE) ** -0.5
NEG = -1e30


def _att_tile(s):
    return min(256, s // 2)


def _attn_fwd(q, k, v):
    nh, s, _ = q.shape
    tq = _att_tile(s)
    nq = s // tq

    def body(q_ref, k_ref, v_ref, o_ref, lse_ref):
        i = pl.program_id(1)
        rowi = _iota((tq, tq), 0)
        coli = _iota((tq, tq), 1)
        zero = (jnp.full((tq, 1), NEG, F32), jnp.zeros((tq, 1), F32), jnp.zeros((tq, LANE), F32))
        state = [zero, zero]
        done = [zero, zero]
        for t in range(nq + 1):
            first = t <= i
            qblk = jnp.where(first, i, nq - 1 - i)
            kblk = jnp.where(first, t, t - i - 1)
            qoff = pl.multiple_of(qblk * tq, tq)
            koff = pl.multiple_of(kblk * tq, tq)
            keep = coli <= rowi + jnp.where(kblk == qblk, 0, tq)
            restart = t == i + 1
            for hh in range(2):
                m, lsum, acc = state[hh]
                if t > 0:
                    done[hh] = tuple(jnp.where(restart, a, b) for a, b in zip(state[hh], done[hh]))
                    m = jnp.where(restart, NEG, m)
                    lsum = jnp.where(restart, 0.0, lsum)
                    acc = jnp.where(restart, 0.0, acc)
                sc = _dot_nt(q_ref[hh, pl.ds(qoff, tq), :], k_ref[hh, pl.ds(koff, tq), :])
                sc = jnp.where(keep, sc, NEG)
                m_new = jnp.maximum(m, jnp.max(sc, axis=1, keepdims=True))
                p = jnp.exp(sc - m_new)
                alpha = jnp.exp(m - m_new)
                lsum = alpha * lsum + jnp.sum(p, axis=1, keepdims=True)
                acc = alpha * acc + _dot(p, v_ref[hh, pl.ds(koff, tq), :])
                state[hh] = (m_new, lsum, acc)
        for blk, res in ((i, done), (nq - 1 - i, state)):
            off = pl.multiple_of(blk * tq, tq)
            out = None
            for hh in range(2):
                m, lsum, acc = res[hh]
                o = acc * (1.0 / lsum)
                lse_ref[hh, pl.ds(off, tq), :] = m + jnp.log(lsum)
                out = o if hh == 0 else out + pltpu.roll(o, V_DIM, 1)
            o_ref[pl.ds(off, tq), :] = out

    pair = pl.BlockSpec((2, s, LANE), lambda j, i: (j, 0, 0))
    return pl.pallas_call(
        body, name="attn_fwd", grid=(nh // 2, nq // 2),
        in_specs=[pair, pair, pair],
        out_specs=[pl.BlockSpec((s, LANE), lambda j, i: (0, j)), pl.BlockSpec((2, s, 1), lambda j, i: (j, 0, 0))],
        out_shape=[_sds((s, D_MLA)), _sds((nh, s, 1))],
        compiler_params=_params(("parallel", "arbitrary")),
    )(q, k, v)


def _ssd_gate(y_ssd, s_z, g):
    yz = y_ssd * _silu(s_z)
    g0 = _iota(yz.shape, 1) < D_SSD // 2
    sq = yz * yz
    ms0 = jnp.sum(jnp.where(g0, sq, 0.0), axis=1, keepdims=True) / (D_SSD // 2)
    ms1 = jnp.sum(jnp.where(g0, 0.0, sq), axis=1, keepdims=True) / (D_SSD // 2)
    r = jnp.where(g0, lax.rsqrt(ms0 + SSD_NORM_EPS), lax.rsqrt(ms1 + SSD_NORM_EPS))
    nrm = yz * r
    return nrm * g, nrm, r, g0


def _outproj_fwd(x, proj, ya, y_ssd, o, g_ssd, w):
    s = x.shape[0]
    ts = _tile(s)

    def body(x_ref, p_ref, ya_ref, ys_ref, o_ref, g_ref, w_ref, xo_ref, y_ref):
        yb = _ssd_gate(ys_ref[...], p_ref[:, O_SZ:O_SZ + D_SSD], g_ref[...])[0]
        yc = o_ref[...] * _silu(p_ref[:, O_CZ:O_CZ + D_MLA])
        y = jnp.concatenate([ya_ref[...], yb, yc], axis=1).astype(MXU)
        y_ref[...] = y
        xo_ref[...] = x_ref[...] + jnp.dot(y, w_ref[...], preferred_element_type=F32)

    return pl.pallas_call(
        body, name="outproj_fwd", grid=(s // ts,),
        in_specs=[_row(ts, D_MODEL), _row(ts, NCOL), _row(ts, D_CONV_A), _row(ts, D_SSD), _row(ts, D_MLA),
                  _full((1, D_SSD)), _full((D_MODEL, D_MODEL))],
        out_specs=[_row(ts, D_MODEL), _row(ts, D_MODEL)],
        out_shape=[_sds((s, D_MODEL)), _sds((s, D_MODEL), MXU)],
        compiler_params=_params(("parallel",)),
    )(x, proj, ya, y_ssd, o, g_ssd, w)


def _loss_head(x, g, tgt):
    s = x.shape[0]
    ts = _tile(s)

    def body(x_ref, g_ref, t_ref, dx_ref, dg_ref, loss_ref):
        @pl.when(pl.program_id(0) == 0)
        def _():
            dg_ref[...] = jnp.zeros_like(dg_ref)
            loss_ref[...] = jnp.zeros_like(loss_ref)

        xv = x_ref[...]
        gv = g_ref[...]
        yn, r = _rms(xv, gv)
        e = yn - t_ref[...]
        loss_ref[...] += jnp.sum(jnp.sum(e * e, axis=1, keepdims=True), axis=0, keepdims=True) * (0.5 / D_MODEL)
        dx, dg = _rms_bwd(e * (1.0 / D_MODEL), xv, r, gv)
        dx_ref[...] = dx
        dg_ref[...] += dg

    return pl.pallas_call(
        body, name="loss_head", grid=(s // ts,),
        in_specs=[_row(ts, D_MODEL), _full((1, D_MODEL)), _row(ts, D_MODEL)],
        out_specs=[_row(ts, D_MODEL), _full((1, D_MODEL)), _full((1, LANE))],
        out_shape=[_sds((s, D_MODEL)), _sds((1, D_MODEL)), _sds((1, LANE))],
        compiler_params=_params(("arbitrary",)),
    )(x, g, tgt)


def _outproj_bwd(dout, y, w, proj, y_ssd, o, g_ssd, dep=None):
    s = dout.shape[0]
    ts = _tile(s)

    def body(dout_ref, y_ref, w_ref, p_ref, ys_ref, o_ref, g_ref,
             dya_ref, dys_ref, dsz_ref, dattn_ref, dcz_ref, dg_ref, dw_ref):
        @pl.when(pl.program_id(0) == 0)
        def _():
            dw_ref[...] = jnp.zeros_like(dw_ref)
            dg_ref[...] = jnp.zeros_like(dg_ref)

        dout_b = dout_ref[...].astype(MXU)
        dw_ref[...] += _dot_tn(y_ref[...], dout_b)
        dy = _dot_nt(dout_b, w_ref[...])
        dya_ref[...] = dy[:, :D_CONV_A]
        dyb = dy[:, D_CONV_A:D_CONV_A + D_SSD]
        sz = p_ref[:, O_SZ:O_SZ + D_SSD]
        ys = ys_ref[...]
        gv = g_ref[...]
        _, nrm, r, g0 = _ssd_gate(ys, sz, gv)
        dg_ref[...] += jnp.sum(dyb * nrm, axis=0, keepdims=True)
        dn = dyb * gv
        t = dn * nrm
        mean = jnp.where(g0, jnp.sum(jnp.where(g0, t, 0.0), axis=1, keepdims=True),
                         jnp.sum(jnp.where(g0, 0.0, t), axis=1, keepdims=True)) / (D_SSD // 2)
        dyz = r * (dn - nrm * mean)
        dys_ref[...] = dyz * _silu(sz)
        dsz_ref[...] = dyz * ys * _dsilu(sz)
        dyc = dy[:, D_CONV_A + D_SSD:]
        cz = p_ref[:, O_CZ:O_CZ + D_MLA]
        dattn_ref[...] = dyc * _silu(cz)
        dcz_ref[...] = dyc * o_ref[...] * _dsilu(cz)

    return _call_after(
        dep, body, (dout, y, w, proj, y_ssd, o, g_ssd), name="outproj_bwd", grid=(s // ts,),
        in_specs=[_row(ts, D_MODEL), _row(ts, D_MODEL), _full((D_MODEL, D_MODEL)), _row(ts, NCOL), _row(ts, D_SSD),
                  _row(ts, D_MLA), _full((1, D_SSD))],
        out_specs=[_row(ts, D_CONV_A), _row(ts, D_SSD), _row(ts, D_SSD), _row(ts, D_MLA), _row(ts, D_MLA),
                   _full((1, D_SSD)), _full((D_MODEL, D_MODEL))],
        out_shape=[_sds((s, D_CONV_A)), _sds((s, D_SSD)), _sds((s, D_SSD)), _sds((s, D_MLA)), _sds((s, D_MLA)),
                   _sds((1, D_SSD)), _sds((D_MODEL, D_MODEL))],
        compiler_params=_params(("arbitrary",)),
    )


def _attn_bwd(q, k, v, o, d_o, lse, dep=None):
    nh, s, _ = q.shape
    tq = _att_tile(s)
    nq = s // tq

    def body(q_ref, k_ref, v_ref, o_ref, do_ref, lse_ref, dq_ref, dk_ref, dv_ref, dop, delta):
        i = pl.program_id(1)

        @pl.when(i == 0)
        def _():
            lane = _iota((s, LANE), 1)
            for hh in range(2):
                dov = do_ref[...]
                ov = o_ref[...]
                if hh == 1:
                    dov = pltpu.roll(dov, V_DIM, 1)
                    ov = pltpu.roll(ov, V_DIM, 1)
                dov = jnp.where(lane < V_DIM, dov, 0.0)
                dop[hh] = dov.astype(MXU)
                delta[hh] = jnp.sum(dov * ov, axis=1, keepdims=True)
                dq_ref[hh] = jnp.zeros((s, LANE), F32)

        rowi = _iota((tq, tq), 0)
        coli = _iota((tq, tq), 1)
        z = jnp.zeros((tq, LANE), F32)
        state = [(z, z), (z, z)]
        done = [(z, z), (z, z)]
        for t in range(nq + 1):
            first = t <= nq - 1 - i
            kblk = jnp.where(first, i, nq - 1 - i)
            qblk = jnp.where(first, i + t, t - 1)
            qoff = pl.multiple_of(qblk * tq, tq)
            koff = pl.multiple_of(kblk * tq, tq)
            keep = coli <= rowi + jnp.where(kblk == qblk, 0, tq)
            restart = t == nq - i
            for hh in range(2):
                dk, dv = state[hh]
                if t > 0:
                    done[hh] = tuple(jnp.where(restart, a, b) for a, b in zip(state[hh], done[hh]))
                    dk = jnp.where(restart, 0.0, dk)
                    dv = jnp.where(restart, 0.0, dv)
                kb = k_ref[hh, pl.ds(koff, tq), :]
                qb = q_ref[hh, pl.ds(qoff, tq), :]
                dob = dop[hh, pl.ds(qoff, tq), :]
                sc = jnp.where(keep, _dot_nt(qb, kb), NEG)
                p = jnp.exp(sc - lse_ref[hh, pl.ds(qoff, tq), :])
                dp = _dot_nt(dob, v_ref[hh, pl.ds(koff, tq), :])
                ds = p * (dp - delta[hh, pl.ds(qoff, tq), :])
                dq_ref[hh, pl.ds(qoff, tq), :] += _dot(ds, kb)
                state[hh] = (dk + _dot_tn(ds, qb), dv + _dot_tn(p, dob))
        for blk, res in ((i, done), (nq - 1 - i, state)):
            off = pl.multiple_of(blk * tq, tq)
            for hh in range(2):
                dk_ref[hh, pl.ds(off, tq), :] = res[hh][0]
                dv_ref[hh, pl.ds(off, tq), :] = res[hh][1]

    pair = pl.BlockSpec((2, s, LANE), lambda j, i: (j, 0, 0))
    return _call_after(
        dep, body, (q, k, v, o, d_o, lse), name="attn_bwd", grid=(nh // 2, nq // 2),
        in_specs=[pair, pair, pair, pl.BlockSpec((s, LANE), lambda j, i: (0, j)), pl.BlockSpec((s, LANE), lambda j, i: (0, j)),
                  pl.BlockSpec((2, s, 1), lambda j, i: (j, 0, 0))],
        out_specs=[pair, pair, pair],
        out_shape=[_sds((nh, s, LANE))] * 3,
        scratch_shapes=[pltpu.VMEM((2, s, LANE), MXU), pltpu.VMEM((2, s, 1), F32)],
        compiler_params=_params(("parallel", "arbitrary")),
    )


def _ssd_bwd(xbc, proj, sc, states, dy, dep=None):
    s = xbc.shape[0]
    nc = s // SSD_CHUNK
    l = SSD_CHUNK

    def body(xbc_ref, tail_ref, sc_ref, st_ref, dy_ref, dxbc_ref, dtail_ref, dsc_ref, dstate):
        @pl.when(pl.program_id(0) == 0)
        def _():
            dstate[...] = jnp.zeros_like(dstate)
            dsc_ref[...] = jnp.zeros_like(dsc_ref)

        sc_v = sc_ref[...]
        lane, row, tri, a_row, pre, dt, a_cs, a_t = _ssd_chunk_common(tail_ref[...], sc_v)
        lane1 = _iota((1, LANE), 1)
        rowp = _iota((LANE, 1), 0)
        rowl = _iota((l, 1), 0)
        d_row = sc_v[2:3, :]
        da_col = jnp.zeros((l, LANE), F32)
        da_row = jnp.zeros((LANE, l), F32)
        dt_x = jnp.zeros((l, LANE), F32)
        dd_row = jnp.zeros((1, LANE), F32)
        db = [jnp.zeros((l, LANE), F32), jnp.zeros((l, LANE), F32)]
        dc = [jnp.zeros((l, LANE), F32), jnp.zeros((l, LANE), F32)]
        for j in range(3):
            xpair = xbc_ref[:, LANE * j:LANE * (j + 1)]
            dypair = dy_ref[:, LANE * j:LANE * (j + 1)]
            sp = st_ref[0, j]
            dsp = dstate[j]
            dxpair = jnp.zeros((l, LANE), F32)
            ds_new = jnp.zeros((LANE, LANE), F32)
            decay = jnp.zeros((LANE, 1), F32)
            for half in range(2):
                h = 2 * j + half
                g = h // 3
                hm = (lane < 64) if half == 0 else (lane >= 64)
                hrow = (rowp < 64) if half == 0 else (rowp >= 64)
                ac = _pick_col(a_cs, lane, DT_LANE + h)
                ar = _pick_row(a_t, row, DT_LANE + h)
                dtc = _pick_col(dt, lane, DT_LANE + h)
                alast = jnp.sum(jnp.where(lane1 == l - 1, ar, 0.0), axis=1, keepdims=True)
                dh = jnp.sum(jnp.where(lane1 == DT_LANE + h, d_row, 0.0), axis=1, keepdims=True)
                xm = jnp.where(hm, xpair, 0.0)
                xd = xm * dtc
                dym = jnp.where(hm, dypair, 0.0)
                bm = xbc_ref[:, D_SSD + LANE * g:D_SSD + LANE * (g + 1)]
                cm = xbc_ref[:, D_SSD + SSD_BC + LANE * g:D_SSD + SSD_BC + LANE * (g + 1)]
                lm = jnp.where(row >= lane, jnp.exp(jnp.minimum(ac - ar, 0.0)), 0.0)
                e_in = jnp.exp(ac)
                f_out = jnp.exp(alast - ac)
                e_last = jnp.exp(alast)
                m = _dot_nt(cm, bm) * lm
                y_off = jnp.where(hm, _dot_nt(cm, sp), 0.0) * e_in
                dm = _dot_nt(dym, xd)
                dxd = _dot_tn(m, dym)
                dg = dm * lm
                dye = dym * e_in
                dc[g] = dc[g] + _dot(dg, bm) + _dot(dye, sp)
                db[g] = db[g] + _dot_tn(dg, cm)
                qm = dm * m
                dac = jnp.sum(qm, axis=1, keepdims=True) + jnp.sum(dym * y_off, axis=1, keepdims=True)
                dar = -jnp.sum(qm, axis=0, keepdims=True)
                dxf = jnp.where(hm, _dot_nt(bm, dsp), 0.0)
                db[g] = db[g] + _dot(xd * f_out, dsp)
                dxd = dxd + dxf * f_out
                df = jnp.sum(dxf * xd, axis=1, keepdims=True) * f_out
                dac = dac - df
                s_last = jnp.sum(df, axis=0, keepdims=True)
                ss = jnp.sum(jnp.where(hrow, dsp * sp, 0.0), axis=1, keepdims=True)
                s_last = s_last + e_last * jnp.sum(ss, axis=0, keepdims=True)
                dac = dac + jnp.where(rowl == l - 1, s_last, 0.0)
                ds_new = ds_new + _dot_tn(dye, cm)
                decay = jnp.where(hrow, e_last, decay)
                dxpair = dxpair + dxd * dtc + dym * dh
                dt_x = dt_x + jnp.where(lane == DT_LANE + h, jnp.sum(dxd * xm, axis=1, keepdims=True), 0.0)
                dsum = jnp.sum(jnp.sum(dym * xm, axis=1, keepdims=True), axis=0, keepdims=True)
                dd_row = dd_row + jnp.where(lane1 == DT_LANE + h, dsum, 0.0)
                da_col = da_col + jnp.where(lane == DT_LANE + h, dac, 0.0)
                da_row = da_row + jnp.where(row == DT_LANE + h, dar, 0.0)
            dstate[j] = dsp * decay + ds_new
            dxbc_ref[:, LANE * j:LANE * (j + 1)] = dxpair
        for g in range(2):
            dxbc_ref[:, D_SSD + LANE * g:D_SSD + LANE * (g + 1)] = db[g]
            dxbc_ref[:, D_SSD + SSD_BC + LANE * g:D_SSD + SSD_BC + LANE * (g + 1)] = dc[g]
        dla = _dot_hi_tn(tri, da_col + da_row.T)
        ddt = dt_x + dla * a_row
        dpre = ddt * _sigmoid(pre)
        dtm = (lane >= DT_LANE) & (lane < DT_LANE + SSD_HEADS)
        dtail_ref[...] = jnp.where(dtm, dpre, 0.0)
        dtm1 = (lane1 >= DT_LANE) & (lane1 < DT_LANE + SSD_HEADS)
        dsc_ref[0:1, :] += jnp.where(dtm1, jnp.sum(dpre, axis=0, keepdims=True), 0.0)
        dsc_ref[1:2, :] += jnp.where(dtm1, jnp.sum(dla * dt, axis=0, keepdims=True) * a_row, 0.0)
        dsc_ref[2:3, :] += dd_row

    rev = lambda c: nc - 1 - c
    return _call_after(
        dep, body, (xbc, proj, sc, states, dy), name="ssd_bwd", grid=(nc,),
        in_specs=[pl.BlockSpec((l, N_XBC), lambda c: (rev(c), 0)),
                  pl.BlockSpec((l, LANE), lambda c: (rev(c), O_TAIL // LANE)), _full((8, LANE)),
                  pl.BlockSpec((1, 3, LANE, LANE), lambda c: (rev(c), 0, 0, 0)),
                  pl.BlockSpec((l, D_SSD), lambda c: (rev(c), 0))],
        out_specs=[pl.BlockSpec((l, N_XBC), lambda c: (rev(c), 0)), pl.BlockSpec((l, LANE), lambda c: (rev(c), 0)),
                   _full((8, LANE))],
        out_shape=[_sds((s, N_XBC)), _sds((s, LANE)), _sds((8, LANE))],
        scratch_shapes=[pltpu.VMEM((3, LANE, LANE), F32)],
        compiler_params=_params(("arbitrary",)),
    )


def _sconv_bwd(proj, w, b, dxbc, dep=None):
    s = proj.shape[0]

    def body(u_ref, w_ref, b_ref, d_ref, du_ref, dw_ref, db_ref):
        u = u_ref[...]
        wv = w_ref[...]
        dpre = d_ref[...] * _dsilu(_sconv_pre(u, wv, b_ref[...]))
        du_ref[...] = (wv[3:4, :] * dpre + wv[2:3, :] * _shift_up(dpre, 1) + wv[1:2, :] * _shift_up(dpre, 2)
                       + wv[0:1, :] * _shift_up(dpre, 3))
        for k in range(4):
            dw_ref[k:k + 1, :] = jnp.sum(dpre * _shift_down(u, 3 - k), axis=0, keepdims=True)
        db_ref[...] = jnp.sum(dpre, axis=0, keepdims=True)

    blk = pl.BlockSpec((s, LANE), lambda j: (0, j))
    return _call_after(
        dep, body, (proj, w, b, dxbc), name="sconv_bwd", grid=(N_XBC // LANE,),
        in_specs=[_col(s, O_XBC), pl.BlockSpec((4, LANE), lambda j: (0, j)), pl.BlockSpec((1, LANE), lambda j: (0, j)), blk],
        out_specs=[blk, pl.BlockSpec((4, LANE), lambda j: (0, j)), pl.BlockSpec((1, LANE), lambda j: (0, j))],
        out_shape=[_sds((s, N_XBC)), _sds((4, N_XBC)), _sds((1, N_XBC))],
        compiler_params=_params(("parallel",)),
    )


def _conva_bwd(proj, w, dya, dep=None):
    s = proj.shape[0]

    def body(h_ref, b_ref, c_ref, z_ref, w_ref, d_ref, da_ref, dw_ref):
        ah, ab, acv, az = h_ref[...], b_ref[...], c_ref[...], z_ref[...]
        wv = w_ref[...]
        u = acv * ah
        cv = wv[2:3, :] * u + wv[1:2, :] * _shift_down(u, 1) + wv[0:1, :] * _shift_down(u, 2)
        dy = d_ref[...]
        sz = _silu(az)
        da_ref[1] = dy * cv * sz
        da_ref[3] = dy * ab * cv * _dsilu(az)
        dcv = dy * ab * sz
        du = wv[2:3, :] * dcv + wv[1:2, :] * _shift_up(dcv, 1) + wv[0:1, :] * _shift_up(dcv, 2)
        da_ref[0] = du * acv
        da_ref[2] = du * ah
        for k in range(3):
            dw_ref[k:k + 1, :] = jnp.sum(dcv * _shift_down(u, 2 - k), axis=0, keepdims=True)

    return _call_after(
        dep, body, (proj, proj, proj, proj, w, dya), name="conva_bwd", grid=(D_CONV_A // LANE,),
        in_specs=[_col(s, O_AH), _col(s, O_AB), _col(s, O_AC), _col(s, O_AZ), pl.BlockSpec((3, LANE), lambda j: (0, j)),
                  pl.BlockSpec((s, LANE), lambda j: (0, j))],
        out_specs=[pl.BlockSpec((4, s, LANE), lambda j: (0, 0, j)), pl.BlockSpec((3, LANE), lambda j: (0, j))],
        out_shape=[_sds((4, s, D_CONV_A)), _sds((3, D_CONV_A))],
        compiler_params=_params(("parallel",)),
    )


def _mla_prep_bwd(dq, dk, dv, proj, qn, kvn, rq, rkv, gq, gkv, wq, wkv, cos, sin):
    s = proj.shape[0]
    ts = _tile(s)
    nh = MLA_HEADS

    def body(dq_ref, dk_ref, dv_ref, cqa_ref, ckv_ref, qn_ref, kvn_ref, rq_ref, rkv_ref, gq_ref, gkv_ref,
             wq_ref, wkv_ref, cos_ref, sin_ref, dcqa_ref, dckv_ref, dtail_ref, dwq_ref, dwkv_ref, dgq_ref, dgkv_ref):
        @pl.when(pl.program_id(0) == 0)
        def _():
            dwq_ref[...] = jnp.zeros_like(dwq_ref)
            dwkv_ref[...] = jnp.zeros_like(dwkv_ref)
            dgq_ref[...] = jnp.zeros_like(dgq_ref)
            dgkv_ref[...] = jnp.zeros_like(dgkv_ref)

        cosv = cos_ref[...]
        sinv = sin_ref[...]
        lane = _iota((ts, LANE), 1)
        rope_lanes = (lane >= ROPE_LANE) & (lane < ROPE_LANE + QK_ROPE)

        def unrope(gr):
            return gr * cosv + _rope_swap(gr * sinv)

        dqs, dks, dvs = [], [], []
        dkr = jnp.zeros((ts, LANE), F32)
        for h in range(nh):
            dqs.append(unrope(dq_ref[h] * ATT_SCALE).astype(MXU))
            dkh = dk_ref[h]
            dks.append(jnp.where(lane < QK_NOPE, dkh, 0.0).astype(MXU))
            dkr = dkr + jnp.where(rope_lanes, dkh, 0.0)
            dvs.append(dv_ref[h].astype(MXU))
        dtail_ref[...] = pltpu.roll(jnp.where(rope_lanes, unrope(dkr), 0.0), ROPE_LANE, 1)
        dq_all = jnp.concatenate(dqs, axis=1)
        dkv_all = jnp.concatenate(dks + dvs, axis=1)
        dwq_ref[...] += _dot_tn(dq_all, qn_ref[...])
        dwkv_ref[...] += _dot_tn(dkv_all, kvn_ref[...])
        dcqa, dgq = _rms_bwd(_dot(dq_all, wq_ref[...]), cqa_ref[...], rq_ref[...], gq_ref[...])
        dckv, dgkv = _rms_bwd(_dot(dkv_all, wkv_ref[...]), ckv_ref[...], rkv_ref[...], gkv_ref[...])
        dcqa_ref[...] = dcqa
        dckv_ref[...] = dckv
        dgq_ref[...] += dgq
        dgkv_ref[...] += dgkv

    head = pl.BlockSpec((nh, ts, LANE), lambda i: (0, i, 0))
    return pl.pallas_call(
        body, name="mla_prep_bwd", grid=(s // ts,),
        in_specs=[head, head, head,
                  pl.BlockSpec((ts, Q_LORA), lambda i: (i, O_CQA // Q_LORA)),
                  pl.BlockSpec((ts, KV_LORA), lambda i: (i, O_CKV // KV_LORA)),
                  _row(ts, Q_LORA), _row(ts, KV_LORA), _row(ts, 1), _row(ts, 1),
                  _full((1, Q_LORA)), _full((1, KV_LORA)), _full((nh * LANE, Q_LORA)), _full((2 * nh * LANE, KV_LORA)),
                  _row(ts, LANE), _row(ts, LANE)],
        out_specs=[_row(ts, Q_LORA), _row(ts, KV_LORA), _row(ts, LANE), _full((nh * LANE, Q_LORA)),
                   _full((2 * nh * LANE, KV_LORA)), _full((1, Q_LORA)), _full((1, KV_LORA))],
        out_shape=[_sds((s, Q_LORA)), _sds((s, KV_LORA)), _sds((s, LANE)), _sds((nh * LANE, Q_LORA)),
                   _sds((2 * nh * LANE, KV_LORA)), _sds((1, Q_LORA)), _sds((1, KV_LORA))],
        compiler_params=_params(("arbitrary",)),
    )(dq, dk, dv, proj, proj, qn, kvn, rq, rkv, gq, gkv, wq, wkv, cos, sin)


def _inproj_bwd(da4, dsz, dxbc_in, dcqa, dckv, dcz, dtail_a, dtail_b, w, x, rstd, g, dout):
    s = x.shape[0]
    ts = _tile(s)

    def body(da_ref, dsz_ref, dxbc_ref, dcqa_ref, dckv_ref, dcz_ref, dta_ref, dtb_ref, w_ref, x_ref, r_ref, g_ref, dout_ref,
             dproj_ref, dx_ref, dg_ref):
        @pl.when(pl.program_id(0) == 0)
        def _():
            dg_ref[...] = jnp.zeros_like(dg_ref)

        dproj = jnp.concatenate(
            [da_ref[0], da_ref[1], da_ref[2], da_ref[3], dsz_ref[...], dxbc_ref[...], dcqa_ref[...], dckv_ref[...],
             dcz_ref[...], dta_ref[...] + dtb_ref[...]], axis=1).astype(MXU)
        dproj_ref[...] = dproj
        dh = _dot_nt(dproj, w_ref[...])
        dx, dg = _rms_bwd(dh, x_ref[...], r_ref[...], g_ref[...])
        dx_ref[...] = dout_ref[...] + dx
        dg_ref[...] += dg

    return pl.pallas_call(
        body, name="inproj_bwd", grid=(s // ts,),
        in_specs=[pl.BlockSpec((4, ts, D_CONV_A), lambda i: (0, i, 0)), _row(ts, D_SSD), _row(ts, N_XBC), _row(ts, Q_LORA),
                  _row(ts, KV_LORA), _row(ts, D_MLA), _row(ts, LANE), _row(ts, LANE), _full((D_MODEL, NCOL)),
                  _row(ts, D_MODEL), _row(ts, 1), _full((1, D_MODEL)), _row(ts, D_MODEL)],
        out_specs=[_row(ts, NCOL), _row(ts, D_MODEL), _full((1, D_MODEL))],
        out_shape=[_sds((s, NCOL), MXU), _sds((s, D_MODEL)), _sds((1, D_MODEL))],
        compiler_params=_params(("arbitrary",)),
    )(da4, dsz, dxbc_in, dcqa, dckv, dcz, dtail_a, dtail_b, w, x, rstd, g, dout)


DWIN_BLOCK = 640


def _dwin(h, dproj):
    s = h.shape[0]

    def body(h_ref, d_ref, o_ref):
        o_ref[...] = _dot_tn(h_ref[...], d_ref[...])

    return pl.pallas_call(
        body, name="dwin", grid=(NCOL // DWIN_BLOCK,),
        in_specs=[_full((s, D_MODEL)), pl.BlockSpec((s, DWIN_BLOCK), lambda j: (0, j))],
        out_specs=pl.BlockSpec((D_MODEL, DWIN_BLOCK), lambda j: (0, j)),
        out_shape=_sds((D_MODEL, NCOL)),
        compiler_params=_params(("parallel",)),
    )(h, dproj)


def _adamw(w, g, m, v):
    bc1 = 1.0 - ADAM_B1 ** ADAM_STEP
    bc2 = 1.0 - ADAM_B2 ** ADAM_STEP

    def body(w_ref, g_ref, m_ref, v_ref, d_ref, mo_ref, vo_ref):
        gv = g_ref[...]
        mn = ADAM_B1 * m_ref[...] + (1.0 - ADAM_B1) * gv
        vn = ADAM_B2 * v_ref[...] + (1.0 - ADAM_B2) * (gv * gv)
        mo_ref[...] = mn
        vo_ref[...] = vn
        d_ref[...] = -ADAM_LR * ((mn / bc1) / (jnp.sqrt(vn / bc2) + ADAM_EPS) + ADAM_WD * w_ref[...])

    if w.ndim == 2:
        grid, blk = (1,), pl.BlockSpec(w.shape, lambda i: (0, 0))
    else:
        grid, blk = (w.shape[0],), pl.BlockSpec((1,) + w.shape[1:], lambda i: (i, 0, 0))
    return pl.pallas_call(
        body, name="adamw", grid=grid,
        in_specs=[blk] * 4, out_specs=[blk] * 3, out_shape=[_sds(w.shape)] * 3,
        compiler_params=_params(("parallel",)),
    )(w, g, m, v)


COL_MOVES = ((0, 0, 2304), (2304, 3104, 6), (2310, 2304, 256), (2566, 2560, 128), (2694, 3072, 32), (2726, 2688, 384))


def _move_cols(w, moves, width):
    out = None
    for src, dst, n in moves:
        piece = jnp.pad(w[..., src:src + n], [(0, 0)] * (w.ndim - 1) + [(dst, width - dst - n)])
        out = piece if out is None else out + piece
    return out


def _perm_cols(w):
    return _move_cols(w, COL_MOVES, NCOL)


def _unperm_cols(g):
    return _move_cols(g, [(dst, src, n) for src, dst, n in COL_MOVES], IN_COLS)


def _wq_layout(wt):
    return jnp.pad(wt.reshape(MLA_HEADS, QK_NOPE + QK_ROPE, Q_LORA), ((0, 0), (0, 32), (0, 0))).reshape(MLA_HEADS * LANE, Q_LORA)


def _wq_unlayout(g):
    return g.reshape(MLA_HEADS, LANE, Q_LORA)[:, :QK_NOPE + QK_ROPE].reshape(MLA_HEADS * (QK_NOPE + QK_ROPE), Q_LORA)


def _wkv_layout(wt):
    t = wt.reshape(MLA_HEADS, 2, 64, KV_LORA).transpose(1, 0, 2, 3)
    return jnp.pad(t, ((0, 0), (0, 0), (0, 64), (0, 0))).reshape(2 * MLA_HEADS * LANE, KV_LORA)


def _wkv_unlayout(g):
    t = g.reshape(2, MLA_HEADS, LANE, KV_LORA)[:, :, :64]
    return t.transpose(1, 0, 2, 3).reshape(MLA_HEADS * LANE, KV_LORA)


def _rope_tables(positions):
    inv_freq = ROPE_BASE ** (-jnp.arange(0, QK_ROPE, 2, dtype=F32) / QK_ROPE)
    ang = positions.astype(F32)[:, None] * inv_freq
    cos, sin = jnp.cos(ang), jnp.sin(ang)
    s = positions.shape[0]
    one, zero = jnp.ones((s, ROPE_LANE), F32), jnp.zeros((s, ROPE_LANE), F32)
    cos_t = jnp.concatenate([one, cos, cos, one[:, :32]], axis=1)
    sin_t = jnp.concatenate([zero, -sin, sin, zero[:, :32]], axis=1)
    return cos_t, sin_t


def _ssd_scalars(dt_bias, a_log, d_skip):
    return jnp.pad(jnp.stack([dt_bias, a_log, d_skip]), ((0, 5), (DT_LANE, LANE - DT_LANE - SSD_HEADS)))


def _layer_fwd(x, lw, cos, sin, dep=None, late=None):
    proj, h, rstd = _inproj_fwd(x, lw["norm_g"], lw["w_in"], dep)
    ya = _conva_fwd(proj, lw["conv_a_w"])
    xbc = _sconv_fwd(proj, lw["ssd_conv_w"], lw["ssd_conv_b"])
    y_ssd, states = _ssd_fwd(xbc, proj, lw["sc"])
    if late is not None:
        lw = {**lw, **late(ya, y_ssd)}
    q, k, v, qn, kvn, rq, rkv = _mla_prep_fwd(proj, lw["gq"], lw["gkv"], lw["wq"], lw["wkv"], cos, sin)
    o, lse = _attn_fwd(q, k, v)
    x_out, y = _outproj_fwd(x, proj, ya, y_ssd, o, lw["g_ssd"], lw["w_out"])
    saved = dict(x=x, proj=proj, h=h, rstd=rstd, xbc=xbc, y_ssd=y_ssd, states=states, q=q, k=k, v=v, qn=qn, kvn=kvn,
                 rq=rq, rkv=rkv, o=o, lse=lse, y=y)
    return x_out, saved, lw


def _layer_bwd(dout, lw, sv, cos, sin, rs=None):
    tok = lambda: None if rs is None else rs["h"]["token"]
    dya, dys, dsz, d_o, dcz, dg_ssd, dw_out = _outproj_bwd(dout, sv["y"], lw["w_out"], sv["proj"], sv["y_ssd"], sv["o"],
                                                            lw["g_ssd"], tok())
    if rs is not None:
        rs = _rs_add_mine(rs, [dya])
    dq, dk, dv = _attn_bwd(sv["q"], sv["k"], sv["v"], sv["o"], d_o, sv["lse"], tok())
    dxbc, dtail_s, dsc = _ssd_bwd(sv["xbc"], sv["proj"], lw["sc"], sv["states"], dys, tok())
    da4, dw_conva = _conva_bwd(sv["proj"], lw["conv_a_w"], dya, tok())
    if rs is not None:
        rs = _rs_add_chips(rs, [dq, dxbc, da4])
    du, dw_sconv, db_sconv = _sconv_bwd(sv["proj"], lw["ssd_conv_w"], lw["ssd_conv_b"], dxbc, tok())
    dcqa, dckv, dtail_m, dwq, dwkv, dgq, dgkv = _mla_prep_bwd(
        dq, dk, dv, sv["proj"], sv["qn"], sv["kvn"], sv["rq"], sv["rkv"], lw["gq"], lw["gkv"], lw["wq"], lw["wkv"], cos, sin)
    dproj, dx, dg = _inproj_bwd(da4, dsz, du, dcqa, dckv, dcz, dtail_s, dtail_m, lw["w_in"], sv["x"], sv["rstd"],
                                lw["norm_g"], dout)
    reduced = None if rs is None else _rs_end(rs, [du, dcqa, dx])
    dw_in = _dwin(sv["h"], dproj)
    grads = dict(norm_g=dg, w_in=dw_in, conv_a_w=dw_conva, ssd_conv_w=dw_sconv, ssd_conv_b=db_sconv, sc=dsc,
                 g_ssd=dg_ssd, gq=dgq, wq=dwq, gkv=dgkv, wkv=dwkv, w_out=dw_out)
    return dx, grads, reduced


ANY = pl.BlockSpec(memory_space=pl.ANY)
N_CHIPS = 4
N_DEV = 8


def _place():
    return lax.axis_index("x"), lax.axis_index("y"), lax.axis_index("c")


HBM_SPEC = pl.BlockSpec(memory_space=pltpu.HBM)
SEM_SPEC = pl.BlockSpec(memory_space=pltpu.SEMAPHORE)
PAYLOAD = jnp.bfloat16


def _hbm(a):
    return pltpu.with_memory_space_constraint(a, pltpu.HBM)


def _run_plan(plan, srcs, lands, send_sems, recv_sems, start, wait):
    copies = plan(srcs, lands)
    if start:
        for i, (src, dst, _, to) in enumerate(copies):
            pltpu.make_async_remote_copy(src_ref=src, dst_ref=dst, send_sem=send_sems.at[i], recv_sem=recv_sems.at[i],
                                         device_id=to, device_id_type=MESH_T).start()
    if wait:
        for i, (src, _, arrives, to) in enumerate(copies):
            cp = pltpu.make_async_remote_copy(src_ref=src, dst_ref=arrives, send_sem=send_sems.at[i],
                                              recv_sem=recv_sems.at[i], device_id=to, device_id_type=MESH_T)
            cp.wait_send()
            cp.wait_recv()


def _exchange_fused(name, plan, n_copies, srcs, land_shapes):
    ns, nl = len(srcs), len(land_shapes)

    def body(*refs):
        _run_plan(plan, refs[:ns], refs[ns:ns + nl], refs[ns + nl], refs[ns + nl + 1], True, True)

    return pl.pallas_call(
        body, name=name, in_specs=[ANY] * ns, out_specs=[ANY] * nl, out_shape=list(land_shapes),
        scratch_shapes=[pltpu.SemaphoreType.DMA((n_copies,)), pltpu.SemaphoreType.DMA((n_copies,))],
    )(*srcs)


def _exchange_start(name, plan, n_copies, srcs, land_shapes, deps):
    ns, nl = len(srcs), len(land_shapes)
    n_in = ns + nl + len(deps)

    def body(*refs):
        send_sems, recv_sems = refs[n_in], refs[n_in + 1]
        token = refs[-1]
        _run_plan(plan, refs[:ns], refs[ns:ns + nl], send_sems, recv_sems, True, False)
        token[...] = jnp.zeros_like(token)

    thru = [pltpu.HBM(a.shape, a.dtype) for a in srcs] + [pltpu.HBM(a.shape, a.dtype) for a in land_shapes]
    outs = pl.pallas_call(
        body, name=name,
        out_shape=(pltpu.SemaphoreType.DMA((n_copies,)), pltpu.SemaphoreType.DMA((n_copies,)), *thru, _sds((8, LANE))),
        in_specs=[HBM_SPEC] * (ns + nl) + [ANY] * len(deps),
        out_specs=(SEM_SPEC, SEM_SPEC, *[HBM_SPEC] * (ns + nl), pl.BlockSpec(memory_space=pltpu.VMEM)),
        input_output_aliases={i: 2 + i for i in range(ns + nl)},
        compiler_params=pltpu.CompilerParams(has_side_effects=pltpu.SideEffectType.DATAFLOW_SIDE_EFFECTING),
    )(*[_hbm(a) for a in srcs], *[_hbm(lax.empty(a.shape, a.dtype)) for a in land_shapes], *deps)
    return (outs[0], outs[1]), list(outs[2:2 + ns]), list(outs[2 + ns:2 + ns + nl]), outs[-1]


def _exchange_wait(name, plan, sems, srcs, lands, after):
    ns, nl = len(srcs), len(lands)

    def body(*refs):
        _run_plan(plan, refs[:ns], refs[ns:ns + nl], refs[ns + nl], refs[ns + nl + 1], False, True)

    outs = pl.pallas_call(
        body, name=name,
        out_shape=[pltpu.HBM(a.shape, a.dtype) for a in list(srcs) + list(lands)],
        in_specs=[HBM_SPEC] * (ns + nl) + [SEM_SPEC, SEM_SPEC] + [ANY] * len(after), out_specs=[HBM_SPEC] * (ns + nl),
        input_output_aliases={i: i for i in range(ns + nl)},
        compiler_params=pltpu.CompilerParams(has_side_effects=pltpu.SideEffectType.DATAFLOW_SIDE_EFFECTING),
    )(*srcs, *lands, sems[0], sems[1], *after)
    return list(outs[:ns]), list(outs[ns:])


def _xchg_begin(name, plan, n_copies, srcs, land_shapes, split, deps=()):
    if not split:
        return dict(split=False, srcs=list(srcs), lands=_exchange_fused(name, plan, n_copies, srcs, land_shapes),
                    token=jnp.zeros((8, LANE), F32))
    sems, srcs_t, lands_t, token = _exchange_start(name + "_start", plan, n_copies, srcs, land_shapes, list(deps))
    return dict(split=True, name=name, plan=plan, sems=sems, srcs=srcs_t, lands=lands_t, token=token)


def _xchg_end(h, after):
    if not h["split"]:
        return h["srcs"], h["lands"]
    return _exchange_wait(h["name"] + "_wait", h["plan"], h["sems"], h["srcs"], h["lands"], after)


def _other_chips():
    x, y, c = _place()
    return [(1 - x, y), (x, 1 - y), (1 - x, 1 - y)]


def _gather_plan(srcs, lands):
    x, y, c = _place()
    me = 2 * x + y
    return [(srcs[a], lands[a].at[me], lands[a].at[2 * cx + cy], (cx, cy, c))
            for (cx, cy) in _other_chips() for a in range(len(srcs))]


def _gather_begin(shards, split, tag, deps=()):
    shapes = [_sds((N_CHIPS,) + a.shape, a.dtype) for a in shards]
    return _xchg_begin(f"gather_{tag}", _gather_plan, 3 * len(shards), shards, shapes, split, deps)


def _gather_end(h, after):
    shards, lands = _xchg_end(h, after)
    me = 2 * lax.axis_index("x") + lax.axis_index("y")
    return [lax.dynamic_update_index_in_dim(g, s, me, 0) for g, s in zip(lands, shards)]


def _swap_plan(srcs, lands):
    x, y, c = _place()
    return [(srcs[a].at[:, 1 - c], lands[a], lands[a], (x, y, 1 - c)) for a in range(len(srcs))]


def _chips_plan(srcs, lands):
    x, y, c = _place()
    me = 2 * x + y
    return [(srcs[a].at[2 * cx + cy], lands[a].at[me], lands[a].at[2 * cx + cy], (cx, cy, c))
            for (cx, cy) in _other_chips() for a in range(len(srcs))]


def _share_plan(srcs, lands):
    x, y, c = _place()
    return [(srcs[a], lands[a].at[c], lands[a].at[1 - c], (x, y, 1 - c)) for a in range(len(srcs))]


def _allreduce_small(slab, dep=None):
    r = slab.shape[0]

    def body(s_ref, o_ref, gath, send_sems, recv_sems):
        x, y, c = _place()
        me = 4 * x + 2 * y + c
        gath[me] = s_ref[...]
        cps = []
        for rel in range(1, N_DEV):
            px = 1 - x if rel & 4 else x
            py = 1 - y if rel & 2 else y
            pc = 1 - c if rel & 1 else c
            cp = pltpu.make_async_remote_copy(src_ref=s_ref, dst_ref=gath.at[me], send_sem=send_sems.at[rel - 1],
                                              recv_sem=recv_sems.at[rel - 1], device_id=(px, py, pc), device_id_type=MESH_T)
            cp.start()
            cps.append(cp)
        for cp in cps:
            cp.wait()
        acc = gath[0]
        for d in range(1, N_DEV):
            acc = acc + gath[d]
        o_ref[...] = acc

    vm = pl.BlockSpec(memory_space=pltpu.VMEM)
    return _call_after(
        dep, body, (slab,), name="allreduce_small", in_specs=[vm], out_specs=vm, out_shape=_sds((r, LANE)),
        scratch_shapes=[pltpu.VMEM((N_DEV, r, LANE), F32), pltpu.SemaphoreType.DMA((N_DEV - 1,)),
                        pltpu.SemaphoreType.DMA((N_DEV - 1,))],
    )


def _add_mine(g4, recv, half):
    _, _, rh, c = g4.shape

    def body(h_ref, g_ref, r_ref, o_ref):
        o_ref[0] = (g_ref[0, 0] + r_ref[0]).astype(o_ref.dtype)

    return pl.pallas_call(
        body, name="add_mine",
        grid_spec=pltpu.PrefetchScalarGridSpec(
            num_scalar_prefetch=1, grid=(N_CHIPS,),
            in_specs=[pl.BlockSpec((1, 1, rh, c), lambda j, h: (j, h[0], 0, 0)), pl.BlockSpec((1, rh, c), lambda j, h: (j, 0, 0))],
            out_specs=pl.BlockSpec((1, rh, c), lambda j, h: (j, 0, 0))),
        out_shape=_sds((N_CHIPS, rh, c), PAYLOAD),
        compiler_params=_params(("parallel",)),
    )(half, g4, recv)


def _add_chips(e, p, me):
    _, rh, c = e.shape

    def body(m_ref, e_ref, p_ref, o_ref):
        own = p_ref[0].astype(F32)
        acc = None
        for s in range(N_CHIPS):
            t = jnp.where(m_ref[0] == s, own, e_ref[s].astype(F32))
            acc = t if acc is None else acc + t
        o_ref[...] = acc

    return pl.pallas_call(
        body, name="add_chips",
        grid_spec=pltpu.PrefetchScalarGridSpec(
            num_scalar_prefetch=1, grid=(1,),
            in_specs=[pl.BlockSpec((N_CHIPS, rh, c), lambda i, m: (0, 0, 0)), pl.BlockSpec((1, rh, c), lambda i, m: (m[0], 0, 0))],
            out_specs=pl.BlockSpec((rh, c), lambda i, m: (0, 0))),
        out_shape=_sds((rh, c)),
        compiler_params=_params(("arbitrary",)),
    )(me, e, p)


def _rs_begin(gs, split, tag):
    g4 = [g.reshape(N_CHIPS, 2, g.shape[0] // (2 * N_CHIPS), g.shape[1]) for g in gs]
    h = _xchg_begin(f"rs_swap_{tag}", _swap_plan, len(gs), g4, [_sds((N_CHIPS,) + g.shape[2:]) for g in g4], split)
    return dict(h=h, split=split, tag=tag, shapes=[g.shape for g in gs])


def _rs_add_mine(st, after):
    g4, recv = _xchg_end(st["h"], after)
    half = jnp.reshape(lax.axis_index("c"), (1,)).astype(jnp.int32)
    ps = [_add_mine(g, r, half) for g, r in zip(g4, recv)]
    st["h"] = _xchg_begin(f"rs_chips_{st['tag']}", _chips_plan, 3 * len(ps), ps, [_sds(p.shape, p.dtype) for p in ps], st["split"])
    return st


def _rs_add_chips(st, after):
    ps, es = _xchg_end(st["h"], after)
    me = jnp.reshape(2 * lax.axis_index("x") + lax.axis_index("y"), (1,)).astype(jnp.int32)
    fs = [_add_chips(e, p, me) for e, p in zip(es, ps)]
    st["h"] = _xchg_begin(f"rs_share_{st['tag']}", _share_plan, len(fs), fs, [_sds((2,) + f.shape) for f in fs], st["split"])
    return st


def _rs_end(st, after):
    fs, ss = _xchg_end(st["h"], after)
    c = lax.axis_index("c")
    return [lax.dynamic_update_index_in_dim(s, f, c, 0).reshape(shp[0] // N_CHIPS, shp[1])
            for s, f, shp in zip(ss, fs, st["shapes"])]


WEIGHTS = ["norm_g", "w_in", "conv_a_w", "ssd_conv_w", "ssd_conv_b", "ssd_dt_bias", "ssd_a_log", "ssd_d", "ssd_norm_g",
           "mla_q_norm_g", "w_qb", "mla_kv_norm_g", "w_kvb", "w_out", "final_norm_g"]
BIG = ["w_in", "w_qb", "w_kvb", "w_out"]
SLAB_ROWS = 128
SMALL_ROWS = 72


def _to_slab(parts, rows):
    flat = jnp.concatenate([p.reshape(-1) for p in parts])
    return jnp.pad(flat, (0, rows * LANE - flat.shape[0])).reshape(rows, LANE)


def _from_slab(slab, shapes):
    flat = slab.reshape(-1)
    out, off = [], 0
    for shp in shapes:
        n = int(np.prod(shp))
        out.append(flat[off:off + n].reshape(shp))
        off += n
    return out


def kernel(x, positions, norm_g, w_in, conv_a_w, ssd_conv_w, ssd_conv_b, ssd_dt_bias, ssd_a_log, ssd_d, ssd_norm_g, mla_q_norm_g, w_qb, mla_kv_norm_g, w_kvb, w_out, final_norm_g, loss_target, m_norm_g, m_w_in, m_conv_a_w, m_ssd_conv_w, m_ssd_conv_b, m_ssd_dt_bias, m_ssd_a_log, m_ssd_d, m_ssd_norm_g, m_mla_q_norm_g, m_w_qb, m_mla_kv_norm_g, m_w_kvb, m_w_out, m_final_norm_g, v_norm_g, v_w_in, v_conv_a_w, v_ssd_conv_w, v_ssd_conv_b, v_ssd_dt_bias, v_ssd_a_log, v_ssd_d, v_ssd_norm_g, v_mla_q_norm_g, v_w_qb, v_mla_kv_norm_g, v_w_kvb, v_w_out, v_final_norm_g):
    w = dict(norm_g=norm_g, w_in=w_in, conv_a_w=conv_a_w, ssd_conv_w=ssd_conv_w, ssd_conv_b=ssd_conv_b,
             ssd_dt_bias=ssd_dt_bias, ssd_a_log=ssd_a_log, ssd_d=ssd_d, ssd_norm_g=ssd_norm_g, mla_q_norm_g=mla_q_norm_g,
             w_qb=w_qb, mla_kv_norm_g=mla_kv_norm_g, w_kvb=w_kvb, w_out=w_out, final_norm_g=final_norm_g)
    mom = dict(norm_g=m_norm_g, w_in=m_w_in, conv_a_w=m_conv_a_w, ssd_conv_w=m_ssd_conv_w, ssd_conv_b=m_ssd_conv_b,
               ssd_dt_bias=m_ssd_dt_bias, ssd_a_log=m_ssd_a_log, ssd_d=m_ssd_d, ssd_norm_g=m_ssd_norm_g,
               mla_q_norm_g=m_mla_q_norm_g, w_qb=m_w_qb, mla_kv_norm_g=m_mla_kv_norm_g, w_kvb=m_w_kvb, w_out=m_w_out,
               final_norm_g=m_final_norm_g)
    var = dict(norm_g=v_norm_g, w_in=v_w_in, conv_a_w=v_conv_a_w, ssd_conv_w=v_ssd_conv_w, ssd_conv_b=v_ssd_conv_b,
               ssd_dt_bias=v_ssd_dt_bias, ssd_a_log=v_ssd_a_log, ssd_d=v_ssd_d, ssd_norm_g=v_ssd_norm_g,
               mla_q_norm_g=v_mla_q_norm_g, w_qb=v_w_qb, mla_kv_norm_g=v_mla_kv_norm_g, w_kvb=v_w_kvb, w_out=v_w_out,
               final_norm_g=v_final_norm_g)
    chip = 2 * lax.axis_index("x") + lax.axis_index("y")

    def early_shard(l, zero):
        pack = jnp.pad(conv_a_w[l], ((0, 5), (0, 192))) + jnp.pad(ssd_conv_w[l], ((3, 1), (0, 32)))
        return [(_perm_cols(w_in[l]) + zero).astype(MXU), pack + zero]

    def late_shard(l, zero):
        return [(w_out[l] + zero).astype(MXU), (w_qb[l].T + zero).astype(MXU), (w_kvb[l].T + zero).astype(MXU)]

    def early_weights(l, gathered):
        g_in, g_conv = gathered
        return dict(
            norm_g=norm_g[l][None], w_in=g_in.reshape(D_MODEL, NCOL),
            conv_a_w=jnp.concatenate([g_conv[j, 0:3, 0:64] for j in range(N_CHIPS)], axis=1),
            ssd_conv_w=jnp.concatenate([g_conv[j, 3:7, 0:224] for j in range(N_CHIPS)], axis=1),
            ssd_conv_b=ssd_conv_b[l][None], sc=_ssd_scalars(ssd_dt_bias[l], ssd_a_log[l], ssd_d[l]),
            g_ssd=ssd_norm_g[l][None], gq=mla_q_norm_g[l][None], gkv=mla_kv_norm_g[l][None])

    def late_weights(gathered):
        g_out, g_qb, g_kvb = gathered
        return dict(wq=_wq_layout(g_qb.reshape(MLA_HEADS * 96, Q_LORA)), wkv=_wkv_layout(g_kvb.reshape(MLA_HEADS * LANE, KV_LORA)),
                    w_out=g_out.reshape(D_MODEL, D_MODEL))

    def large_grads(g):
        wq = jnp.pad(_wq_unlayout(g["wq"]).reshape(N_CHIPS, 144, Q_LORA), ((0, 0), (0, 16), (0, 0)))
        return [g["w_in"], g["w_out"], wq.reshape(N_CHIPS * 160, Q_LORA), _wkv_unlayout(g["wkv"])]

    gather_a0 = _gather_begin(early_shard(0, 0.0), True, "a0")
    zero = gather_a0["token"][0, 0]
    cos, sin = _rope_tables(positions[0] + zero.astype(jnp.int32))
    late0, shards1 = late_shard(0, zero), early_shard(1, zero) + late_shard(1, zero)
    lw0 = early_weights(0, _gather_end(gather_a0, [cos, sin] + late0 + shards1))
    gather_b0 = _gather_begin(late0, True, "b0")
    gather_1 = _gather_begin(shards1, True, "1", [gather_b0["token"]])
    x1, sv0, lw0 = _layer_fwd(x[0], lw0, cos, sin, gather_1["token"],
                              lambda ya, y_ssd: late_weights(_gather_end(gather_b0, [ya, y_ssd])))
    g1 = _gather_end(gather_1, [x1])
    x2, sv1, lw1 = _layer_fwd(x1, {**early_weights(1, g1[:2]), **late_weights(g1[2:])}, cos, sin)
    dx, dgf, loss = _loss_head(x2, final_norm_g[None], loss_target[0])

    dx, lg1, _ = _layer_bwd(dx, lw1, sv1, cos, sin)
    grad_x, lg0, red1 = _layer_bwd(dx, lw0, sv0, cos, sin, _rs_begin(large_grads(lg1), True, 1))
    rs0 = _rs_begin(large_grads(lg0), True, 0)
    lg = [lg0, lg1]
    grad = {}

    small_names = ["norm_g", "conv_a_w", "ssd_conv_w", "ssd_conv_b", "sc", "g_ssd", "gq", "gkv"]
    parts = [loss[0, 0:1], dgf]
    for l in range(DEPTH):
        parts += [lg[l][nm][:3, DT_LANE:DT_LANE + SSD_HEADS] if nm == "sc" else lg[l][nm] for nm in small_names]
    shapes = [(1,), (D_MODEL,)] + [(D_MODEL,), (3, D_CONV_A), (4, N_XBC), (N_XBC,), (3, SSD_HEADS), (D_SSD,), (Q_LORA,), (KV_LORA,)] * DEPTH
    red_slab = _allreduce_small(_to_slab(parts, SLAB_ROWS), rs0["h"]["token"])
    rs0 = _rs_add_mine(rs0, [red_slab])
    red = _from_slab(red_slab + rs0["h"]["token"][0, 0], shapes)
    loss_out = red[0][0]
    grad["final_norm_g"] = red[1]
    per = [red[2 + 8 * l:10 + 8 * l] for l in range(DEPTH)]
    grad["norm_g"] = jnp.stack([per[l][0] for l in range(DEPTH)])
    grad["conv_a_w"] = lax.dynamic_slice_in_dim(jnp.stack([per[l][1] for l in range(DEPTH)]), chip * 64, 64, axis=2)
    grad["ssd_conv_w"] = lax.dynamic_slice_in_dim(jnp.stack([per[l][2] for l in range(DEPTH)]), chip * 224, 224, axis=2)
    grad["ssd_conv_b"] = jnp.stack([per[l][3] for l in range(DEPTH)])
    grad["ssd_dt_bias"] = jnp.stack([per[l][4][0] for l in range(DEPTH)])
    grad["ssd_a_log"] = jnp.stack([per[l][4][1] for l in range(DEPTH)])
    grad["ssd_d"] = jnp.stack([per[l][4][2] for l in range(DEPTH)])
    grad["ssd_norm_g"] = jnp.stack([per[l][5] for l in range(DEPTH)])
    grad["mla_q_norm_g"] = jnp.stack([per[l][6] for l in range(DEPTH)])
    grad["mla_kv_norm_g"] = jnp.stack([per[l][7] for l in range(DEPTH)])

    delta, new_m, new_v = {}, {}, {}
    small = [nm for nm in WEIGHTS if nm not in BIG]
    sshapes = [w[nm].shape for nm in small]
    d, mo, vo = _adamw(_to_slab([w[nm] for nm in small], SMALL_ROWS), _to_slab([grad[nm] for nm in small], SMALL_ROWS),
                       _to_slab([mom[nm] for nm in small], SMALL_ROWS), _to_slab([var[nm] for nm in small], SMALL_ROWS))
    small_out = list(zip(small, _from_slab(d, sshapes), _from_slab(mo, sshapes), _from_slab(vo, sshapes)))
    for nm, dv, mv, vv in small_out:
        delta[nm], new_m[nm], new_v[nm] = dv, mv, vv

    red0 = _rs_end(_rs_add_chips(rs0, [a for row in small_out for a in row[1:]] + [grad[nm] for nm in small]), [])
    r_in, r_out, r_qb, r_kvb = [jnp.stack([a, b]) for a, b in zip(red0, red1)]
    grad.update(w_in=_unperm_cols(r_in), w_out=r_out, w_qb=jnp.swapaxes(r_qb[:, :144], 1, 2), w_kvb=jnp.swapaxes(r_kvb, 1, 2))
    for nm in BIG:
        delta[nm], new_m[nm], new_v[nm] = _adamw(w[nm], grad[nm], mom[nm], var[nm])

    return (loss_out, grad_x[None], *[grad[nm] for nm in WEIGHTS], *[delta[nm] for nm in WEIGHTS],
            *[new_m[nm] for nm in WEIGHTS], *[new_v[nm] for nm in WEIGHTS])
```

```python
import functools
import math

import numpy as np
import jax
import jax.numpy as jnp
from jax import lax
from jax.experimental import pallas as pl
from jax.experimental.pallas import tpu as pltpu

F32 = jnp.float32
MXU = jnp.bfloat16

D_MODEL = 1024
DEPTH = 2
D_CONV_A = 256
D_SSD = 384
SSD_HEADS = 6
SSD_BC = 256
SSD_CHUNK = 128
SSD_NORM_EPS = 1e-5
MLA_HEADS = 6
Q_LORA = 256
KV_LORA = 128
QK_NOPE = 64
QK_ROPE = 32
V_DIM = 64
D_MLA = 384
ROPE_BASE = 10000.0
NORM_EPS = 1e-6
IN_COLS = 3110
LANE = 128

O_AH, O_AB, O_AC, O_AZ = 0, 256, 512, 768
O_SZ = 1024
O_XBC = 1408
O_CQA = 2304
O_CKV = 2560
O_CZ = 2688
O_TAIL = 3072
NCOL = 3200
N_XBC = D_SSD + 2 * SSD_BC
DT_LANE = 32
ROPE_LANE = 64

ADAM_LR, ADAM_B1, ADAM_B2, ADAM_EPS, ADAM_WD, ADAM_STEP = 0.001, 0.9, 0.999, 1e-08, 0.01, 10

VMEM_LIMIT = 56 * 1024 * 1024
MESH_T = pl.DeviceIdType.MESH


def _dot(a, b):
    return jnp.dot(a.astype(MXU), b.astype(MXU), preferred_element_type=F32)


def _dot_nt(a, b):
    return lax.dot_general(a.astype(MXU), b.astype(MXU), (((1,), (1,)), ((), ())), preferred_element_type=F32)


def _dot_tn(a, b):
    return lax.dot_general(a.astype(MXU), b.astype(MXU), (((0,), (0,)), ((), ())), preferred_element_type=F32)


def _dot_hi(a, b):
    return jnp.dot(a, b, precision=lax.Precision.HIGHEST, preferred_element_type=F32)


def _dot_hi_tn(a, b):
    return lax.dot_general(a, b, (((0,), (0,)), ((), ())), precision=lax.Precision.HIGHEST, preferred_element_type=F32)


def _sigmoid(z):
    return 1.0 / (1.0 + jnp.exp(-z))


def _silu(z):
    return z * _sigmoid(z)


def _dsilu(z):
    s = _sigmoid(z)
    return s * (1.0 + z * (1.0 - s))


def _softplus(z):
    e = jnp.exp(-jnp.abs(z))
    return jnp.maximum(z, 0.0) + jnp.where(e < 1e-3, e * (1.0 - 0.5 * e), jnp.log(1.0 + e))


def _iota(shape, dim):
    return lax.broadcasted_iota(jnp.int32, shape, dim)


def _shift_down(u, k):
    if k == 0:
        return u
    return jnp.where(_iota(u.shape, 0) >= k, pltpu.roll(u, k, 0), 0.0)


def _shift_up(u, k):
    if k == 0:
        return u
    n = u.shape[0]
    return jnp.where(_iota(u.shape, 0) < n - k, pltpu.roll(u, n - k, 0), 0.0)


def _rope_swap(t):
    lane = _iota(t.shape, 1)
    lo = (lane >= ROPE_LANE) & (lane < ROPE_LANE + 16)
    hi = (lane >= ROPE_LANE + 16) & (lane < ROPE_LANE + 32)
    return jnp.where(lo, pltpu.roll(t, LANE - 16, 1), jnp.where(hi, pltpu.roll(t, 16, 1), 0.0))


def _params(sem=None):
    return pltpu.CompilerParams(dimension_semantics=sem, vmem_limit_bytes=VMEM_LIMIT)


def _full(shape):
    nd = len(shape)
    return pl.BlockSpec(shape, lambda *_: (0,) * nd)


def _sds(shape, dtype=F32):
    return jax.ShapeDtypeStruct(shape, dtype)


def _tile(s):
    return min(256, s)


def _row(ts, w):
    return pl.BlockSpec((ts, w), lambda i: (i, 0))


def _col(s, off):
    return pl.BlockSpec((s, LANE), lambda j, _o=off // LANE: (0, _o + j))


def _call_after(dep, body, args, *, in_specs, **kw):
    if dep is None:
        return pl.pallas_call(body, in_specs=in_specs, **kw)(*args)
    n = len(args)

    def body_dep(*refs):
        body(*refs[:n], *refs[n + 1:])

    return pl.pallas_call(body_dep, in_specs=list(in_specs) + [pl.BlockSpec(memory_space=pl.ANY)], **kw)(*args, dep)


def _rms(c, g):
    r = lax.rsqrt(jnp.mean(c * c, axis=-1, keepdims=True) + NORM_EPS)
    return c * r * g, r


def _rms_bwd(dn, c, r, g):
    ch = c * r
    dch = dn * g
    dc = r * (dch - ch * jnp.mean(dch * ch, axis=-1, keepdims=True))
    return dc, jnp.sum(dn * ch, axis=0, keepdims=True)


def _inproj_fwd(x, g, w, dep=None):
    s = x.shape[0]
    ts = _tile(s)

    def body(x_ref, g_ref, w_ref, proj_ref, h_ref, r_ref):
        hn, r = _rms(x_ref[...], g_ref[...])
        h = hn.astype(MXU)
        h_ref[...] = h
        r_ref[...] = r
        proj_ref[...] = jnp.dot(h, w_ref[...], preferred_element_type=F32)

    return _call_after(
        dep, body, (x, g, w), name="inproj_fwd", grid=(s // ts,),
        in_specs=[_row(ts, D_MODEL), _full((1, D_MODEL)), _full((D_MODEL, NCOL))],
        out_specs=[_row(ts, NCOL), _row(ts, D_MODEL), _row(ts, 1)],
        out_shape=[_sds((s, NCOL)), _sds((s, D_MODEL), MXU), _sds((s, 1))],
        compiler_params=_params(("parallel",)),
    )


def _conva_fwd(proj, w):
    s = proj.shape[0]

    def body(h_ref, b_ref, c_ref, z_ref, w_ref, y_ref):
        u = c_ref[...] * h_ref[...]
        wv = w_ref[...]
        cv = wv[2:3, :] * u + wv[1:2, :] * _shift_down(u, 1) + wv[0:1, :] * _shift_down(u, 2)
        y_ref[...] = b_ref[...] * cv * _silu(z_ref[...])

    return pl.pallas_call(
        body, name="conva_fwd", grid=(D_CONV_A // LANE,),
        in_specs=[_col(s, O_AH), _col(s, O_AB), _col(s, O_AC), _col(s, O_AZ), pl.BlockSpec((3, LANE), lambda j: (0, j))],
        out_specs=pl.BlockSpec((s, LANE), lambda j: (0, j)),
        out_shape=_sds((s, D_CONV_A)),
        compiler_params=_params(("parallel",)),
    )(proj, proj, proj, proj, w)


def _sconv_pre(u, wv, bv):
    return (wv[3:4, :] * u + wv[2:3, :] * _shift_down(u, 1) + wv[1:2, :] * _shift_down(u, 2)
            + wv[0:1, :] * _shift_down(u, 3) + bv)


def _sconv_fwd(proj, w, b):
    s = proj.shape[0]

    def body(u_ref, w_ref, b_ref, o_ref):
        o_ref[...] = _silu(_sconv_pre(u_ref[...], w_ref[...], b_ref[...]))

    return pl.pallas_call(
        body, name="sconv_fwd", grid=(N_XBC // LANE,),
        in_specs=[_col(s, O_XBC), pl.BlockSpec((4, LANE), lambda j: (0, j)), pl.BlockSpec((1, LANE), lambda j: (0, j))],
        out_specs=pl.BlockSpec((s, LANE), lambda j: (0, j)),
        out_shape=_sds((s, N_XBC)),
        compiler_params=_params(("parallel",)),
    )(proj, w, b)


def _ssd_chunk_common(tail, sc):
    l = SSD_CHUNK
    lane = _iota((l, LANE), 1)
    row = _iota((l, LANE), 0)
    tri = (row >= lane).astype(F32)
    a_row = -jnp.exp(sc[1:2, :])
    pre = tail + sc[0:1, :]
    dt = _softplus(pre)
    a_cs = _dot_hi(tri, dt * a_row)
    return lane, row, tri, a_row, pre, dt, a_cs, a_cs.T


def _pick_col(m, lane, k):
    return jnp.sum(jnp.where(lane == k, m, 0.0), axis=1, keepdims=True)


def _pick_row(m, row, k):
    return jnp.sum(jnp.where(row == k, m, 0.0), axis=0, keepdims=True)


def _ssd_fwd(xbc, proj, sc):
    s = xbc.shape[0]
    nc = s // SSD_CHUNK
    l = SSD_CHUNK

    def body(xbc_ref, tail_ref, sc_ref, y_ref, st_ref, state):
        @pl.when(pl.program_id(0) == 0)
        def _():
            state[...] = jnp.zeros_like(state)

        sc_v = sc_ref[...]
        lane, row, _, _, _, dt, a_cs, a_t = _ssd_chunk_common(tail_ref[...], sc_v)
        lane1 = _iota((1, LANE), 1)
        rowp = _iota((LANE, 1), 0)
        d_row = sc_v[2:3, :]
        for j in range(3):
            st_ref[0, j] = state[j]
        for j in range(3):
            xpair = xbc_ref[:, LANE * j:LANE * (j + 1)]
            sp = state[j]
            ypair = jnp.zeros((l, LANE), F32)
            new_s = jnp.zeros((LANE, LANE), F32)
            decay = jnp.zeros((LANE, 1), F32)
            for half in range(2):
                h = 2 * j + half
                g = h // 3
                hm = (lane < 64) if half == 0 else (lane >= 64)
                hrow = (rowp < 64) if half == 0 else (rowp >= 64)
                ac = _pick_col(a_cs, lane, DT_LANE + h)
                ar = _pick_row(a_t, row, DT_LANE + h)
                dtc = _pick_col(dt, lane, DT_LANE + h)
                alast = jnp.sum(jnp.where(lane1 == l - 1, ar, 0.0), axis=1, keepdims=True)
                dh = jnp.sum(jnp.where(lane1 == DT_LANE + h, d_row, 0.0), axis=1, keepdims=True)
                xm = jnp.where(hm, xpair, 0.0)
                xd = xm * dtc
                bm = xbc_ref[:, D_SSD + LANE * g:D_SSD + LANE * (g + 1)]
                cm = xbc_ref[:, D_SSD + SSD_BC + LANE * g:D_SSD + SSD_BC + LANE * (g + 1)]
                lm = jnp.where(row >= lane, jnp.exp(jnp.minimum(ac - ar, 0.0)), 0.0)
                y_diag = _dot(_dot_nt(cm, bm) * lm, xd)
                y_off = jnp.where(hm, _dot_nt(cm, sp), 0.0) * jnp.exp(ac)
                ypair = ypair + y_diag + y_off + xm * dh
                new_s = new_s + _dot_tn(xd * jnp.exp(alast - ac), bm)
                decay = jnp.where(hrow, jnp.exp(alast), decay)
            state[j] = sp * decay + new_s
            y_ref[:, LANE * j:LANE * (j + 1)] = ypair

    return pl.pallas_call(
        body, name="ssd_fwd", grid=(nc,),
        in_specs=[pl.BlockSpec((l, N_XBC), lambda c: (c, 0)),
                  pl.BlockSpec((l, LANE), lambda c: (c, O_TAIL // LANE)), _full((8, LANE))],
        out_specs=[pl.BlockSpec((l, D_SSD), lambda c: (c, 0)), pl.BlockSpec((1, 3, LANE, LANE), lambda c: (c, 0, 0, 0))],
        out_shape=[_sds((s, D_SSD)), _sds((nc, 3, LANE, LANE))],
        scratch_shapes=[pltpu.VMEM((3, LANE, LANE), F32)],
        compiler_params=_params(("arbitrary",)),
    )(xbc, proj, sc)


def _mla_prep_fwd(proj, gq, gkv, wq, wkv, cos, sin):
    s = proj.shape[0]
    ts = _tile(s)
    nh = MLA_HEADS

    def body(cqa_ref, ckv_ref, tail_ref, gq_ref, gkv_ref, wq_ref, wkv_ref, cos_ref, sin_ref,
             q_ref, k_ref, v_ref, qn_ref, kvn_ref, rq_ref, rkv_ref):
        qn, rq = _rms(cqa_ref[...], gq_ref[...])
        kvn, rkv = _rms(ckv_ref[...], gkv_ref[...])
        qn = qn.astype(MXU)
        kvn = kvn.astype(MXU)
        qn_ref[...] = qn
        kvn_ref[...] = kvn
        rq_ref[...] = rq
        rkv_ref[...] = rkv
        q = _dot_nt(qn, wq_ref[...])
        kv = _dot_nt(kvn, wkv_ref[...])
        cosv = cos_ref[...]
        sinv = sin_ref[...]
        lane = _iota((ts, LANE), 1)
        rope_lanes = (lane >= ROPE_LANE) & (lane < ROPE_LANE + QK_ROPE)
        kr = jnp.where(rope_lanes, pltpu.roll(tail_ref[...], ROPE_LANE, 1), 0.0)
        kr = kr * cosv + _rope_swap(kr) * sinv
        for h in range(nh):
            qh = q[:, LANE * h:LANE * (h + 1)]
            q_ref[h] = ((qh * cosv + _rope_swap(qh) * sinv) * ATT_SCALE).astype(MXU)
            k_ref[h] = (kv[:, LANE * h:LANE * (h + 1)] + kr).astype(MXU)
            v_ref[h] = kv[:, LANE * (nh + h):LANE * (nh + h + 1)].astype(MXU)

    head = pl.BlockSpec((nh, ts, LANE), lambda i: (0, i, 0))
    return pl.pallas_call(
        body, name="mla_prep_fwd", grid=(s // ts,),
        in_specs=[pl.BlockSpec((ts, Q_LORA), lambda i: (i, O_CQA // Q_LORA)),
                  pl.BlockSpec((ts, KV_LORA), lambda i: (i, O_CKV // KV_LORA)),
                  pl.BlockSpec((ts, LANE), lambda i: (i, O_TAIL // LANE)),
                  _full((1, Q_LORA)), _full((1, KV_LORA)), _full((nh * LANE, Q_LORA)), _full((2 * nh * LANE, KV_LORA)),
                  _row(ts, LANE), _row(ts, LANE)],
        out_specs=[head, head, head, _row(ts, Q_LORA), _row(ts, KV_LORA), _row(ts, 1), _row(ts, 1)],
        out_shape=[_sds((nh, s, LANE), MXU)] * 3 + [_sds((s, Q_LORA), MXU), _sds((s, KV_LORA), MXU), _sds((s, 1)), _sds((s, 1))],
        compiler_params=_params(("parallel",)),
    )(proj, proj, proj, gq, gkv, wq, wkv, cos, sin)


ATT_SCALE = (QK_NOPE + QK_ROPE) ** -0.5
NEG = -1e30


def _att_tile(s):
    return min(256, s // 2)


def _attn_fwd(q, k, v):
    nh, s, _ = q.shape
    tq = _att_tile(s)
    nq = s // tq

    def body(q_ref, k_ref, v_ref, o_ref, lse_ref):
        i = pl.program_id(1)
        rowi = _iota((tq, tq), 0)
        coli = _iota((tq, tq), 1)
        zero = (jnp.full((tq, 1), NEG, F32), jnp.zeros((tq, 1), F32), jnp.zeros((tq, LANE), F32))
        state = [zero, zero]
        done = [zero, zero]
        for t in range(nq + 1):
            first = t <= i
            qblk = jnp.where(first, i, nq - 1 - i)
            kblk = jnp.where(first, t, t - i - 1)
            qoff = pl.multiple_of(qblk * tq, tq)
            koff = pl.multiple_of(kblk * tq, tq)
            keep = coli <= rowi + jnp.where(kblk == qblk, 0, tq)
            restart = t == i + 1
            for hh in range(2):
                m, lsum, acc = state[hh]
                if t > 0:
                    done[hh] = tuple(jnp.where(restart, a, b) for a, b in zip(state[hh], done[hh]))
                    m = jnp.where(restart, NEG, m)
                    lsum = jnp.where(restart, 0.0, lsum)
                    acc = jnp.where(restart, 0.0, acc)
                sc = _dot_nt(q_ref[hh, pl.ds(qoff, tq), :], k_ref[hh, pl.ds(koff, tq), :])
                sc = jnp.where(keep, sc, NEG)
                m_new = jnp.maximum(m, jnp.max(sc, axis=1, keepdims=True))
                p = jnp.exp(sc - m_new)
                alpha = jnp.exp(m - m_new)
                lsum = alpha * lsum + jnp.sum(p, axis=1, keepdims=True)
                acc = alpha * acc + _dot(p, v_ref[hh, pl.ds(koff, tq), :])
                state[hh] = (m_new, lsum, acc)
        for blk, res in ((i, done), (nq - 1 - i, state)):
            off = pl.multiple_of(blk * tq, tq)
            out = None
            for hh in range(2):
                m, lsum, acc = res[hh]
                o = acc * (1.0 / lsum)
                lse_ref[hh, pl.ds(off, tq), :] = m + jnp.log(lsum)
                out = o if hh == 0 else out + pltpu.roll(o, V_DIM, 1)
            o_ref[pl.ds(off, tq), :] = out

    pair = pl.BlockSpec((2, s, LANE), lambda j, i: (j, 0, 0))
    return pl.pallas_call(
        body, name="attn_fwd", grid=(nh // 2, nq // 2),
        in_specs=[pair, pair, pair],
        out_specs=[pl.BlockSpec((s, LANE), lambda j, i: (0, j)), pl.BlockSpec((2, s, 1), lambda j, i: (j, 0, 0))],
        out_shape=[_sds((s, D_MLA)), _sds((nh, s, 1))],
        compiler_params=_params(("parallel", "arbitrary")),
    )(q, k, v)


def _ssd_gate(y_ssd, s_z, g):
    yz = y_ssd * _silu(s_z)
    g0 = _iota(yz.shape, 1) < D_SSD // 2
    sq = yz * yz
    ms0 = jnp.sum(jnp.where(g0, sq, 0.0), axis=1, keepdims=True) / (D_SSD // 2)
    ms1 = jnp.sum(jnp.where(g0, 0.0, sq), axis=1, keepdims=True) / (D_SSD // 2)
    r = jnp.where(g0, lax.rsqrt(ms0 + SSD_NORM_EPS), lax.rsqrt(ms1 + SSD_NORM_EPS))
    nrm = yz * r
    return nrm * g, nrm, r, g0


def _outproj_fwd(x, proj, ya, y_ssd, o, g_ssd, w):
    s = x.shape[0]
    ts = _tile(s)

    def body(x_ref, p_ref, ya_ref, ys_ref, o_ref, g_ref, w_ref, xo_ref, y_ref):
        yb = _ssd_gate(ys_ref[...], p_ref[:, O_SZ:O_SZ + D_SSD], g_ref[...])[0]
        yc = o_ref[...] * _silu(p_ref[:, O_CZ:O_CZ + D_MLA])
        y = jnp.concatenate([ya_ref[...], yb, yc], axis=1).astype(MXU)
        y_ref[...] = y
        xo_ref[...] = x_ref[...] + jnp.dot(y, w_ref[...], preferred_element_type=F32)

    return pl.pallas_call(
        body, name="outproj_fwd", grid=(s // ts,),
        in_specs=[_row(ts, D_MODEL), _row(ts, NCOL), _row(ts, D_CONV_A), _row(ts, D_SSD), _row(ts, D_MLA),
                  _full((1, D_SSD)), _full((D_MODEL, D_MODEL))],
        out_specs=[_row(ts, D_MODEL), _row(ts, D_MODEL)],
        out_shape=[_sds((s, D_MODEL)), _sds((s, D_MODEL), MXU)],
        compiler_params=_params(("parallel",)),
    )(x, proj, ya, y_ssd, o, g_ssd, w)


def _loss_head(x, g, tgt):
    s = x.shape[0]
    ts = _tile(s)

    def body(x_ref, g_ref, t_ref, dx_ref, dg_ref, loss_ref):
        @pl.when(pl.program_id(0) == 0)
        def _():
            dg_ref[...] = jnp.zeros_like(dg_ref)
            loss_ref[...] = jnp.zeros_like(loss_ref)

        xv = x_ref[...]
        gv = g_ref[...]
        yn, r = _rms(xv, gv)
        e = yn - t_ref[...]
        loss_ref[...] += jnp.sum(jnp.sum(e * e, axis=1, keepdims=True), axis=0, keepdims=True) * (0.5 / D_MODEL)
        dx, dg = _rms_bwd(e * (1.0 / D_MODEL), xv, r, gv)
        dx_ref[...] = dx
        dg_ref[...] += dg

    return pl.pallas_call(
        body, name="loss_head", grid=(s // ts,),
        in_specs=[_row(ts, D_MODEL), _full((1, D_MODEL)), _row(ts, D_MODEL)],
        out_specs=[_row(ts, D_MODEL), _full((1, D_MODEL)), _full((1, LANE))],
        out_shape=[_sds((s, D_MODEL)), _sds((1, D_MODEL)), _sds((1, LANE))],
        compiler_params=_params(("arbitrary",)),
    )(x, g, tgt)


def _outproj_bwd(dout, y, w, proj, y_ssd, o, g_ssd, dep=None):
    s = dout.shape[0]
    ts = _tile(s)

    def body(dout_ref, y_ref, w_ref, p_ref, ys_ref, o_ref, g_ref,
             dya_ref, dys_ref, dsz_ref, dattn_ref, dcz_ref, dg_ref, dw_ref):
        @pl.when(pl.program_id(0) == 0)
        def _():
            dw_ref[...] = jnp.zeros_like(dw_ref)
            dg_ref[...] = jnp.zeros_like(dg_ref)

        dout_b = dout_ref[...].astype(MXU)
        dw_ref[...] += _dot_tn(y_ref[...], dout_b)
        dy = _dot_nt(dout_b, w_ref[...])
        dya_ref[...] = dy[:, :D_CONV_A]
        dyb = dy[:, D_CONV_A:D_CONV_A + D_SSD]
        sz = p_ref[:, O_SZ:O_SZ + D_SSD]
        ys = ys_ref[...]
        gv = g_ref[...]
        _, nrm, r, g0 = _ssd_gate(ys, sz, gv)
        dg_ref[...] += jnp.sum(dyb * nrm, axis=0, keepdims=True)
        dn = dyb * gv
        t = dn * nrm
        mean = jnp.where(g0, jnp.sum(jnp.where(g0, t, 0.0), axis=1, keepdims=True),
                         jnp.sum(jnp.where(g0, 0.0, t), axis=1, keepdims=True)) / (D_SSD // 2)
        dyz = r * (dn - nrm * mean)
        dys_ref[...] = dyz * _silu(sz)
        dsz_ref[...] = dyz * ys * _dsilu(sz)
        dyc = dy[:, D_CONV_A + D_SSD:]
        cz = p_ref[:, O_CZ:O_CZ + D_MLA]
        dattn_ref[...] = dyc * _silu(cz)
        dcz_ref[...] = dyc * o_ref[...] * _dsilu(cz)

    return _call_after(
        dep, body, (dout, y, w, proj, y_ssd, o, g_ssd), name="outproj_bwd", grid=(s // ts,),
        in_specs=[_row(ts, D_MODEL), _row(ts, D_MODEL), _full((D_MODEL, D_MODEL)), _row(ts, NCOL), _row(ts, D_SSD),
                  _row(ts, D_MLA), _full((1, D_SSD))],
        out_specs=[_row(ts, D_CONV_A), _row(ts, D_SSD), _row(ts, D_SSD), _row(ts, D_MLA), _row(ts, D_MLA),
                   _full((1, D_SSD)), _full((D_MODEL, D_MODEL))],
        out_shape=[_sds((s, D_CONV_A)), _sds((s, D_SSD)), _sds((s, D_SSD)), _sds((s, D_MLA)), _sds((s, D_MLA)),
                   _sds((1, D_SSD)), _sds((D_MODEL, D_MODEL))],
        compiler_params=_params(("arbitrary",)),
    )


def _attn_bwd(q, k, v, o, d_o, lse, dep=None):
    nh, s, _ = q.shape
    tq = _att_tile(s)
    nq = s // tq

    def body(q_ref, k_ref, v_ref, o_ref, do_ref, lse_ref, dq_ref, dk_ref, dv_ref, dop, delta):
        i = pl.program_id(1)

        @pl.when(i == 0)
        def _():
            lane = _iota((s, LANE), 1)
            for hh in range(2):
                dov = do_ref[...]
                ov = o_ref[...]
                if hh == 1:
                    dov = pltpu.roll(dov, V_DIM, 1)
                    ov = pltpu.roll(ov, V_DIM, 1)
                dov = jnp.where(lane < V_DIM, dov, 0.0)
                dop[hh] = dov.astype(MXU)
                delta[hh] = jnp.sum(dov * ov, axis=1, keepdims=True)
                dq_ref[hh] = jnp.zeros((s, LANE), F32)

        rowi = _iota((tq, tq), 0)
        coli = _iota((tq, tq), 1)
        z = jnp.zeros((tq, LANE), F32)
        state = [(z, z), (z, z)]
        done = [(z, z), (z, z)]
        for t in range(nq + 1):
            first = t <= nq - 1 - i
            kblk = jnp.where(first, i, nq - 1 - i)
            qblk = jnp.where(first, i + t, t - 1)
            qoff = pl.multiple_of(qblk * tq, tq)
            koff = pl.multiple_of(kblk * tq, tq)
            keep = coli <= rowi + jnp.where(kblk == qblk, 0, tq)
            restart = t == nq - i
            for hh in range(2):
                dk, dv = state[hh]
                if t > 0:
                    done[hh] = tuple(jnp.where(restart, a, b) for a, b in zip(state[hh], done[hh]))
                    dk = jnp.where(restart, 0.0, dk)
                    dv = jnp.where(restart, 0.0, dv)
                kb = k_ref[hh, pl.ds(koff, tq), :]
                qb = q_ref[hh, pl.ds(qoff, tq), :]
                dob = dop[hh, pl.ds(qoff, tq), :]
                sc = jnp.where(keep, _dot_nt(qb, kb), NEG)
                p = jnp.exp(sc - lse_ref[hh, pl.ds(qoff, tq), :])
                dp = _dot_nt(dob, v_ref[hh, pl.ds(koff, tq), :])
                ds = p * (dp - delta[hh, pl.ds(qoff, tq), :])
                dq_ref[hh, pl.ds(qoff, tq), :] += _dot(ds, kb)
                state[hh] = (dk + _dot_tn(ds, qb), dv + _dot_tn(p, dob))
        for blk, res in ((i, done), (nq - 1 - i, state)):
            off = pl.multiple_of(blk * tq, tq)
            for hh in range(2):
                dk_ref[hh, pl.ds(off, tq), :] = res[hh][0]
                dv_ref[hh, pl.ds(off, tq), :] = res[hh][1]

    pair = pl.BlockSpec((2, s, LANE), lambda j, i: (j, 0, 0))
    return _call_after(
        dep, body, (q, k, v, o, d_o, lse), name="attn_bwd", grid=(nh // 2, nq // 2),
        in_specs=[pair, pair, pair, pl.BlockSpec((s, LANE), lambda j, i: (0, j)), pl.BlockSpec((s, LANE), lambda j, i: (0, j)),
                  pl.BlockSpec((2, s, 1), lambda j, i: (j, 0, 0))],
        out_specs=[pair, pair, pair],
        out_shape=[_sds((nh, s, LANE))] * 3,
        scratch_shapes=[pltpu.VMEM((2, s, LANE), MXU), pltpu.VMEM((2, s, 1), F32)],
        compiler_params=_params(("parallel", "arbitrary")),
    )


def _ssd_bwd(xbc, proj, sc, states, dy, dep=None):
    s = xbc.shape[0]
    nc = s // SSD_CHUNK
    l = SSD_CHUNK

    def body(xbc_ref, tail_ref, sc_ref, st_ref, dy_ref, dxbc_ref, dtail_ref, dsc_ref, dstate):
        @pl.when(pl.program_id(0) == 0)
        def _():
            dstate[...] = jnp.zeros_like(dstate)
            dsc_ref[...] = jnp.zeros_like(dsc_ref)

        sc_v = sc_ref[...]
        lane, row, tri, a_row, pre, dt, a_cs, a_t = _ssd_chunk_common(tail_ref[...], sc_v)
        lane1 = _iota((1, LANE), 1)
        rowp = _iota((LANE, 1), 0)
        rowl = _iota((l, 1), 0)
        d_row = sc_v[2:3, :]
        da_col = jnp.zeros((l, LANE), F32)
        da_row = jnp.zeros((LANE, l), F32)
        dt_x = jnp.zeros((l, LANE), F32)
        dd_row = jnp.zeros((1, LANE), F32)
        db = [jnp.zeros((l, LANE), F32), jnp.zeros((l, LANE), F32)]
        dc = [jnp.zeros((l, LANE), F32), jnp.zeros((l, LANE), F32)]
        for j in range(3):
            xpair = xbc_ref[:, LANE * j:LANE * (j + 1)]
            dypair = dy_ref[:, LANE * j:LANE * (j + 1)]
            sp = st_ref[0, j]
            dsp = dstate[j]
            dxpair = jnp.zeros((l, LANE), F32)
            ds_new = jnp.zeros((LANE, LANE), F32)
            decay = jnp.zeros((LANE, 1), F32)
            for half in range(2):
                h = 2 * j + half
                g = h // 3
                hm = (lane < 64) if half == 0 else (lane >= 64)
                hrow = (rowp < 64) if half == 0 else (rowp >= 64)
                ac = _pick_col(a_cs, lane, DT_LANE + h)
                ar = _pick_row(a_t, row, DT_LANE + h)
                dtc = _pick_col(dt, lane, DT_LANE + h)
                alast = jnp.sum(jnp.where(lane1 == l - 1, ar, 0.0), axis=1, keepdims=True)
                dh = jnp.sum(jnp.where(lane1 == DT_LANE + h, d_row, 0.0), axis=1, keepdims=True)
                xm = jnp.where(hm, xpair, 0.0)
                xd = xm * dtc
                dym = jnp.where(hm, dypair, 0.0)
                bm = xbc_ref[:, D_SSD + LANE * g:D_SSD + LANE * (g + 1)]
                cm = xbc_ref[:, D_SSD + SSD_BC + LANE * g:D_SSD + SSD_BC + LANE * (g + 1)]
                lm = jnp.where(row >= lane, jnp.exp(jnp.minimum(ac - ar, 0.0)), 0.0)
                e_in = jnp.exp(ac)
                f_out = jnp.exp(alast - ac)
                e_last = jnp.exp(alast)
                m = _dot_nt(cm, bm) * lm
                y_off = jnp.where(hm, _dot_nt(cm, sp), 0.0) * e_in
                dm = _dot_nt(dym, xd)
                dxd = _dot_tn(m, dym)
                dg = dm * lm
                dye = dym * e_in
                dc[g] = dc[g] + _dot(dg, bm) + _dot(dye, sp)
                db[g] = db[g] + _dot_tn(dg, cm)
                qm = dm * m
                dac = jnp.sum(qm, axis=1, keepdims=True) + jnp.sum(dym * y_off, axis=1, keepdims=True)
                dar = -jnp.sum(qm, axis=0, keepdims=True)
                dxf = jnp.where(hm, _dot_nt(bm, dsp), 0.0)
                db[g] = db[g] + _dot(xd * f_out, dsp)
                dxd = dxd + dxf * f_out
                df = jnp.sum(dxf * xd, axis=1, keepdims=True) * f_out
                dac = dac - df
                s_last = jnp.sum(df, axis=0, keepdims=True)
                ss = jnp.sum(jnp.where(hrow, dsp * sp, 0.0), axis=1, keepdims=True)
                s_last = s_last + e_last * jnp.sum(ss, axis=0, keepdims=True)
                dac = dac + jnp.where(rowl == l - 1, s_last, 0.0)
                ds_new = ds_new + _dot_tn(dye, cm)
                decay = jnp.where(hrow, e_last, decay)
                dxpair = dxpair + dxd * dtc + dym * dh
                dt_x = dt_x + jnp.where(lane == DT_LANE + h, jnp.sum(dxd * xm, axis=1, keepdims=True), 0.0)
                dsum = jnp.sum(jnp.sum(dym * xm, axis=1, keepdims=True), axis=0, keepdims=True)
                dd_row = dd_row + jnp.where(lane1 == DT_LANE + h, dsum, 0.0)
                da_col = da_col + jnp.where(lane == DT_LANE + h, dac, 0.0)
                da_row = da_row + jnp.where(row == DT_LANE + h, dar, 0.0)
            dstate[j] = dsp * decay + ds_new
            dxbc_ref[:, LANE * j:LANE * (j + 1)] = dxpair
        for g in range(2):
            dxbc_ref[:, D_SSD + LANE * g:D_SSD + LANE * (g + 1)] = db[g]
            dxbc_ref[:, D_SSD + SSD_BC + LANE * g:D_SSD + SSD_BC + LANE * (g + 1)] = dc[g]
        dla = _dot_hi_tn(tri, da_col + da_row.T)
        ddt = dt_x + dla * a_row
        dpre = ddt * _sigmoid(pre)
        dtm = (lane >= DT_LANE) & (lane < DT_LANE + SSD_HEADS)
        dtail_ref[...] = jnp.where(dtm, dpre, 0.0)
        dtm1 = (lane1 >= DT_LANE) & (lane1 < DT_LANE + SSD_HEADS)
        dsc_ref[0:1, :] += jnp.where(dtm1, jnp.sum(dpre, axis=0, keepdims=True), 0.0)
        dsc_ref[1:2, :] += jnp.where(dtm1, jnp.sum(dla * dt, axis=0, keepdims=True) * a_row, 0.0)
        dsc_ref[2:3, :] += dd_row

    rev = lambda c: nc - 1 - c
    return _call_after(
        dep, body, (xbc, proj, sc, states, dy), name="ssd_bwd", grid=(nc,),
        in_specs=[pl.BlockSpec((l, N_XBC), lambda c: (rev(c), 0)),
                  pl.BlockSpec((l, LANE), lambda c: (rev(c), O_TAIL // LANE)), _full((8, LANE)),
                  pl.BlockSpec((1, 3, LANE, LANE), lambda c: (rev(c), 0, 0, 0)),
                  pl.BlockSpec((l, D_SSD), lambda c: (rev(c), 0))],
        out_specs=[pl.BlockSpec((l, N_XBC), lambda c: (rev(c), 0)), pl.BlockSpec((l, LANE), lambda c: (rev(c), 0)),
                   _full((8, LANE))],
        out_shape=[_sds((s, N_XBC)), _sds((s, LANE)), _sds((8, LANE))],
        scratch_shapes=[pltpu.VMEM((3, LANE, LANE), F32)],
        compiler_params=_params(("arbitrary",)),
    )


def _sconv_bwd(proj, w, b, dxbc, dep=None):
    s = proj.shape[0]

    def body(u_ref, w_ref, b_ref, d_ref, du_ref, dw_ref, db_ref):
        u = u_ref[...]
        wv = w_ref[...]
        dpre = d_ref[...] * _dsilu(_sconv_pre(u, wv, b_ref[...]))
        du_ref[...] = (wv[3:4, :] * dpre + wv[2:3, :] * _shift_up(dpre, 1) + wv[1:2, :] * _shift_up(dpre, 2)
                       + wv[0:1, :] * _shift_up(dpre, 3))
        for k in range(4):
            dw_ref[k:k + 1, :] = jnp.sum(dpre * _shift_down(u, 3 - k), axis=0, keepdims=True)
        db_ref[...] = jnp.sum(dpre, axis=0, keepdims=True)

    blk = pl.BlockSpec((s, LANE), lambda j: (0, j))
    return _call_after(
        dep, body, (proj, w, b, dxbc), name="sconv_bwd", grid=(N_XBC // LANE,),
        in_specs=[_col(s, O_XBC), pl.BlockSpec((4, LANE), lambda j: (0, j)), pl.BlockSpec((1, LANE), lambda j: (0, j)), blk],
        out_specs=[blk, pl.BlockSpec((4, LANE), lambda j: (0, j)), pl.BlockSpec((1, LANE), lambda j: (0, j))],
        out_shape=[_sds((s, N_XBC)), _sds((4, N_XBC)), _sds((1, N_XBC))],
        compiler_params=_params(("parallel",)),
    )


def _conva_bwd(proj, w, dya, dep=None):
    s = proj.shape[0]

    def body(h_ref, b_ref, c_ref, z_ref, w_ref, d_ref, da_ref, dw_ref):
        ah, ab, acv, az = h_ref[...], b_ref[...], c_ref[...], z_ref[...]
        wv = w_ref[...]
        u = acv * ah
        cv = wv[2:3, :] * u + wv[1:2, :] * _shift_down(u, 1) + wv[0:1, :] * _shift_down(u, 2)
        dy = d_ref[...]
        sz = _silu(az)
        da_ref[1] = dy * cv * sz
        da_ref[3] = dy * ab * cv * _dsilu(az)
        dcv = dy * ab * sz
        du = wv[2:3, :] * dcv + wv[1:2, :] * _shift_up(dcv, 1) + wv[0:1, :] * _shift_up(dcv, 2)
        da_ref[0] = du * acv
        da_ref[2] = du * ah
        for k in range(3):
            dw_ref[k:k + 1, :] = jnp.sum(dcv * _shift_down(u, 2 - k), axis=0, keepdims=True)

    return _call_after(
        dep, body, (proj, proj, proj, proj, w, dya), name="conva_bwd", grid=(D_CONV_A // LANE,),
        in_specs=[_col(s, O_AH), _col(s, O_AB), _col(s, O_AC), _col(s, O_AZ), pl.BlockSpec((3, LANE), lambda j: (0, j)),
                  pl.BlockSpec((s, LANE), lambda j: (0, j))],
        out_specs=[pl.BlockSpec((4, s, LANE), lambda j: (0, 0, j)), pl.BlockSpec((3, LANE), lambda j: (0, j))],
        out_shape=[_sds((4, s, D_CONV_A)), _sds((3, D_CONV_A))],
        compiler_params=_params(("parallel",)),
    )


def _mla_prep_bwd(dq, dk, dv, proj, qn, kvn, rq, rkv, gq, gkv, wq, wkv, cos, sin):
    s = proj.shape[0]
    ts = _tile(s)
    nh = MLA_HEADS

    def body(dq_ref, dk_ref, dv_ref, cqa_ref, ckv_ref, qn_ref, kvn_ref, rq_ref, rkv_ref, gq_ref, gkv_ref,
             wq_ref, wkv_ref, cos_ref, sin_ref, dcqa_ref, dckv_ref, dtail_ref, dwq_ref, dwkv_ref, dgq_ref, dgkv_ref):
        @pl.when(pl.program_id(0) == 0)
        def _():
            dwq_ref[...] = jnp.zeros_like(dwq_ref)
            dwkv_ref[...] = jnp.zeros_like(dwkv_ref)
            dgq_ref[...] = jnp.zeros_like(dgq_ref)
            dgkv_ref[...] = jnp.zeros_like(dgkv_ref)

        cosv = cos_ref[...]
        sinv = sin_ref[...]
        lane = _iota((ts, LANE), 1)
        rope_lanes = (lane >= ROPE_LANE) & (lane < ROPE_LANE + QK_ROPE)

        def unrope(gr):
            return gr * cosv + _rope_swap(gr * sinv)

        dqs, dks, dvs = [], [], []
        dkr = jnp.zeros((ts, LANE), F32)
        for h in range(nh):
            dqs.append(unrope(dq_ref[h] * ATT_SCALE).astype(MXU))
            dkh = dk_ref[h]
            dks.append(jnp.where(lane < QK_NOPE, dkh, 0.0).astype(MXU))
            dkr = dkr + jnp.where(rope_lanes, dkh, 0.0)
            dvs.append(dv_ref[h].astype(MXU))
        dtail_ref[...] = pltpu.roll(jnp.where(rope_lanes, unrope(dkr), 0.0), ROPE_LANE, 1)
        dq_all = jnp.concatenate(dqs, axis=1)
        dkv_all = jnp.concatenate(dks + dvs, axis=1)
        dwq_ref[...] += _dot_tn(dq_all, qn_ref[...])
        dwkv_ref[...] += _dot_tn(dkv_all, kvn_ref[...])
        dcqa, dgq = _rms_bwd(_dot(dq_all, wq_ref[...]), cqa_ref[...], rq_ref[...], gq_ref[...])
        dckv, dgkv = _rms_bwd(_dot(dkv_all, wkv_ref[...]), ckv_ref[...], rkv_ref[...], gkv_ref[...])
        dcqa_ref[...] = dcqa
        dckv_ref[...] = dckv
        dgq_ref[...] += dgq
        dgkv_ref[...] += dgkv

    head = pl.BlockSpec((nh, ts, LANE), lambda i: (0, i, 0))
    return pl.pallas_call(
        body, name="mla_prep_bwd", grid=(s // ts,),
        in_specs=[head, head, head,
                  pl.BlockSpec((ts, Q_LORA), lambda i: (i, O_CQA // Q_LORA)),
                  pl.BlockSpec((ts, KV_LORA), lambda i: (i, O_CKV // KV_LORA)),
                  _row(ts, Q_LORA), _row(ts, KV_LORA), _row(ts, 1), _row(ts, 1),
                  _full((1, Q_LORA)), _full((1, KV_LORA)), _full((nh * LANE, Q_LORA)), _full((2 * nh * LANE, KV_LORA)),
                  _row(ts, LANE), _row(ts, LANE)],
        out_specs=[_row(ts, Q_LORA), _row(ts, KV_LORA), _row(ts, LANE), _full((nh * LANE, Q_LORA)),
                   _full((2 * nh * LANE, KV_LORA)), _full((1, Q_LORA)), _full((1, KV_LORA))],
        out_shape=[_sds((s, Q_LORA)), _sds((s, KV_LORA)), _sds((s, LANE)), _sds((nh * LANE, Q_LORA)),
                   _sds((2 * nh * LANE, KV_LORA)), _sds((1, Q_LORA)), _sds((1, KV_LORA))],
        compiler_params=_params(("arbitrary",)),
    )(dq, dk, dv, proj, proj, qn, kvn, rq, rkv, gq, gkv, wq, wkv, cos, sin)


def _inproj_bwd(da4, dsz, dxbc_in, dcqa, dckv, dcz, dtail_a, dtail_b, w, x, rstd, g, dout):
    s = x.shape[0]
    ts = _tile(s)

    def body(da_ref, dsz_ref, dxbc_ref, dcqa_ref, dckv_ref, dcz_ref, dta_ref, dtb_ref, w_ref, x_ref, r_ref, g_ref, dout_ref,
             dproj_ref, dx_ref, dg_ref):
        @pl.when(pl.program_id(0) == 0)
        def _():
            dg_ref[...] = jnp.zeros_like(dg_ref)

        dproj = jnp.concatenate(
            [da_ref[0], da_ref[1], da_ref[2], da_ref[3], dsz_ref[...], dxbc_ref[...], dcqa_ref[...], dckv_ref[...],
             dcz_ref[...], dta_ref[...] + dtb_ref[...]], axis=1).astype(MXU)
        dproj_ref[...] = dproj
        dh = _dot_nt(dproj, w_ref[...])
        dx, dg = _rms_bwd(dh, x_ref[...], r_ref[...], g_ref[...])
        dx_ref[...] = dout_ref[...] + dx
        dg_ref[...] += dg

    return pl.pallas_call(
        body, name="inproj_bwd", grid=(s // ts,),
        in_specs=[pl.BlockSpec((4, ts, D_CONV_A), lambda i: (0, i, 0)), _row(ts, D_SSD), _row(ts, N_XBC), _row(ts, Q_LORA),
                  _row(ts, KV_LORA), _row(ts, D_MLA), _row(ts, LANE), _row(ts, LANE), _full((D_MODEL, NCOL)),
                  _row(ts, D_MODEL), _row(ts, 1), _full((1, D_MODEL)), _row(ts, D_MODEL)],
        out_specs=[_row(ts, NCOL), _row(ts, D_MODEL), _full((1, D_MODEL))],
        out_shape=[_sds((s, NCOL), MXU), _sds((s, D_MODEL)), _sds((1, D_MODEL))],
        compiler_params=_params(("arbitrary",)),
    )(da4, dsz, dxbc_in, dcqa, dckv, dcz, dtail_a, dtail_b, w, x, rstd, g, dout)


DWIN_BLOCK = 640


def _dwin(h, dproj):
    s = h.shape[0]

    def body(h_ref, d_ref, o_ref):
        o_ref[...] = _dot_tn(h_ref[...], d_ref[...])

    return pl.pallas_call(
        body, name="dwin", grid=(NCOL // DWIN_BLOCK,),
        in_specs=[_full((s, D_MODEL)), pl.BlockSpec((s, DWIN_BLOCK), lambda j: (0, j))],
        out_specs=pl.BlockSpec((D_MODEL, DWIN_BLOCK), lambda j: (0, j)),
        out_shape=_sds((D_MODEL, NCOL)),
        compiler_params=_params(("parallel",)),
    )(h, dproj)


def _adamw(w, g, m, v):
    bc1 = 1.0 - ADAM_B1 ** ADAM_STEP
    bc2 = 1.0 - ADAM_B2 ** ADAM_STEP

    def body(w_ref, g_ref, m_ref, v_ref, d_ref, mo_ref, vo_ref):
        gv = g_ref[...]
        mn = ADAM_B1 * m_ref[...] + (1.0 - ADAM_B1) * gv
        vn = ADAM_B2 * v_ref[...] + (1.0 - ADAM_B2) * (gv * gv)
        mo_ref[...] = mn
        vo_ref[...] = vn
        d_ref[...] = -ADAM_LR * ((mn / bc1) / (jnp.sqrt(vn / bc2) + ADAM_EPS) + ADAM_WD * w_ref[...])

    if w.ndim == 2:
        grid, blk = (1,), pl.BlockSpec(w.shape, lambda i: (0, 0))
    else:
        grid, blk = (w.shape[0],), pl.BlockSpec((1,) + w.shape[1:], lambda i: (i, 0, 0))
    return pl.pallas_call(
        body, name="adamw", grid=grid,
        in_specs=[blk] * 4, out_specs=[blk] * 3, out_shape=[_sds(w.shape)] * 3,
        compiler_params=_params(("parallel",)),
    )(w, g, m, v)


COL_MOVES = ((0, 0, 2304), (2304, 3104, 6), (2310, 2304, 256), (2566, 2560, 128), (2694, 3072, 32), (2726, 2688, 384))


def _move_cols(w, moves, width):
    out = None
    for src, dst, n in moves:
        piece = jnp.pad(w[..., src:src + n], [(0, 0)] * (w.ndim - 1) + [(dst, width - dst - n)])
        out = piece if out is None else out + piece
    return out


def _perm_cols(w):
    return _move_cols(w, COL_MOVES, NCOL)


def _unperm_cols(g):
    return _move_cols(g, [(dst, src, n) for src, dst, n in COL_MOVES], IN_COLS)


def _wq_layout(wt):
    return jnp.pad(wt.reshape(MLA_HEADS, QK_NOPE + QK_ROPE, Q_LORA), ((0, 0), (0, 32), (0, 0))).reshape(MLA_HEADS * LANE, Q_LORA)


def _wq_unlayout(g):
    return g.reshape(MLA_HEADS, LANE, Q_LORA)[:, :QK_NOPE + QK_ROPE].reshape(MLA_HEADS * (QK_NOPE + QK_ROPE), Q_LORA)


def _wkv_layout(wt):
    t = wt.reshape(MLA_HEADS, 2, 64, KV_LORA).transpose(1, 0, 2, 3)
    return jnp.pad(t, ((0, 0), (0, 0), (0, 64), (0, 0))).reshape(2 * MLA_HEADS * LANE, KV_LORA)


def _wkv_unlayout(g):
    t = g.reshape(2, MLA_HEADS, LANE, KV_LORA)[:, :, :64]
    return t.transpose(1, 0, 2, 3).reshape(MLA_HEADS * LANE, KV_LORA)


def _rope_tables(positions):
    inv_freq = ROPE_BASE ** (-jnp.arange(0, QK_ROPE, 2, dtype=F32) / QK_ROPE)
    ang = positions.astype(F32)[:, None] * inv_freq
    cos, sin = jnp.cos(ang), jnp.sin(ang)
    s = positions.shape[0]
    one, zero = jnp.ones((s, ROPE_LANE), F32), jnp.zeros((s, ROPE_LANE), F32)
    cos_t = jnp.concatenate([one, cos, cos, one[:, :32]], axis=1)
    sin_t = jnp.concatenate([zero, -sin, sin, zero[:, :32]], axis=1)
    return cos_t, sin_t


def _ssd_scalars(dt_bias, a_log, d_skip):
    return jnp.pad(jnp.stack([dt_bias, a_log, d_skip]), ((0, 5), (DT_LANE, LANE - DT_LANE - SSD_HEADS)))


def _layer_fwd(x, lw, cos, sin, dep=None, late=None):
    proj, h, rstd = _inproj_fwd(x, lw["norm_g"], lw["w_in"], dep)
    ya = _conva_fwd(proj, lw["conv_a_w"])
    xbc = _sconv_fwd(proj, lw["ssd_conv_w"], lw["ssd_conv_b"])
    y_ssd, states = _ssd_fwd(xbc, proj, lw["sc"])
    if late is not None:
        lw = {**lw, **late(ya, y_ssd)}
    q, k, v, qn, kvn, rq, rkv = _mla_prep_fwd(proj, lw["gq"], lw["gkv"], lw["wq"], lw["wkv"], cos, sin)
    o, lse = _attn_fwd(q, k, v)
    x_out, y = _outproj_fwd(x, proj, ya, y_ssd, o, lw["g_ssd"], lw["w_out"])
    saved = dict(x=x, proj=proj, h=h, rstd=rstd, xbc=xbc, y_ssd=y_ssd, states=states, q=q, k=k, v=v, qn=qn, kvn=kvn,
                 rq=rq, rkv=rkv, o=o, lse=lse, y=y)
    return x_out, saved, lw


def _layer_bwd(dout, lw, sv, cos, sin, rs=None):
    tok = lambda: None if rs is None else rs["h"]["token"]
    dya, dys, dsz, d_o, dcz, dg_ssd, dw_out = _outproj_bwd(dout, sv["y"], lw["w_out"], sv["proj"], sv["y_ssd"], sv["o"],
                                                            lw["g_ssd"], tok())
    if rs is not None:
        rs = _rs_add_mine(rs, [dya])
    dq, dk, dv = _attn_bwd(sv["q"], sv["k"], sv["v"], sv["o"], d_o, sv["lse"], tok())
    dxbc, dtail_s, dsc = _ssd_bwd(sv["xbc"], sv["proj"], lw["sc"], sv["states"], dys, tok())
    da4, dw_conva = _conva_bwd(sv["proj"], lw["conv_a_w"], dya, tok())
    if rs is not None:
        rs = _rs_add_chips(rs, [dq, dxbc, da4])
    du, dw_sconv, db_sconv = _sconv_bwd(sv["proj"], lw["ssd_conv_w"], lw["ssd_conv_b"], dxbc, tok())
    dcqa, dckv, dtail_m, dwq, dwkv, dgq, dgkv = _mla_prep_bwd(
        dq, dk, dv, sv["proj"], sv["qn"], sv["kvn"], sv["rq"], sv["rkv"], lw["gq"], lw["gkv"], lw["wq"], lw["wkv"], cos, sin)
    dproj, dx, dg = _inproj_bwd(da4, dsz, du, dcqa, dckv, dcz, dtail_s, dtail_m, lw["w_in"], sv["x"], sv["rstd"],
                                lw["norm_g"], dout)
    reduced = None if rs is None else _rs_end(rs, [du, dcqa, dx])
    dw_in = _dwin(sv["h"], dproj)
    grads = dict(norm_g=dg, w_in=dw_in, conv_a_w=dw_conva, ssd_conv_w=dw_sconv, ssd_conv_b=db_sconv, sc=dsc,
                 g_ssd=dg_ssd, gq=dgq, wq=dwq, gkv=dgkv, wkv=dwkv, w_out=dw_out)
    return dx, grads, reduced


ANY = pl.BlockSpec(memory_space=pl.ANY)
N_CHIPS = 4
N_DEV = 8


def _place():
    return lax.axis_index("x"), lax.axis_index("y"), lax.axis_index("c")


HBM_SPEC = pl.BlockSpec(memory_space=pltpu.HBM)
SEM_SPEC = pl.BlockSpec(memory_space=pltpu.SEMAPHORE)
PAYLOAD = jnp.bfloat16


def _hbm(a):
    return pltpu.with_memory_space_constraint(a, pltpu.HBM)


def _run_plan(plan, srcs, lands, send_sems, recv_sems, start, wait):
    copies = plan(srcs, lands)
    if start:
        for i, (src, dst, _, to) in enumerate(copies):
            pltpu.make_async_remote_copy(src_ref=src, dst_ref=dst, send_sem=send_sems.at[i], recv_sem=recv_sems.at[i],
                                         device_id=to, device_id_type=MESH_T).start()
    if wait:
        for i, (src, _, arrives, to) in enumerate(copies):
            cp = pltpu.make_async_remote_copy(src_ref=src, dst_ref=arrives, send_sem=send_sems.at[i],
                                              recv_sem=recv_sems.at[i], device_id=to, device_id_type=MESH_T)
            cp.wait_send()
            cp.wait_recv()


def _exchange_fused(name, plan, n_copies, srcs, land_shapes):
    ns, nl = len(srcs), len(land_shapes)

    def body(*refs):
        _run_plan(plan, refs[:ns], refs[ns:ns + nl], refs[ns + nl], refs[ns + nl + 1], True, True)

    return pl.pallas_call(
        body, name=name, in_specs=[ANY] * ns, out_specs=[ANY] * nl, out_shape=list(land_shapes),
        scratch_shapes=[pltpu.SemaphoreType.DMA((n_copies,)), pltpu.SemaphoreType.DMA((n_copies,))],
    )(*srcs)


def _exchange_start(name, plan, n_copies, srcs, land_shapes, deps):
    ns, nl = len(srcs), len(land_shapes)
    n_in = ns + nl + len(deps)

    def body(*refs):
        send_sems, recv_sems = refs[n_in], refs[n_in + 1]
        token = refs[-1]
        _run_plan(plan, refs[:ns], refs[ns:ns + nl], send_sems, recv_sems, True, False)
        token[...] = jnp.zeros_like(token)

    thru = [pltpu.HBM(a.shape, a.dtype) for a in srcs] + [pltpu.HBM(a.shape, a.dtype) for a in land_shapes]
    outs = pl.pallas_call(
        body, name=name,
        out_shape=(pltpu.SemaphoreType.DMA((n_copies,)), pltpu.SemaphoreType.DMA((n_copies,)), *thru, _sds((8, LANE))),
        in_specs=[HBM_SPEC] * (ns + nl) + [ANY] * len(deps),
        out_specs=(SEM_SPEC, SEM_SPEC, *[HBM_SPEC] * (ns + nl), pl.BlockSpec(memory_space=pltpu.VMEM)),
        input_output_aliases={i: 2 + i for i in range(ns + nl)},
        compiler_params=pltpu.CompilerParams(has_side_effects=pltpu.SideEffectType.DATAFLOW_SIDE_EFFECTING),
    )(*[_hbm(a) for a in srcs], *[_hbm(lax.empty(a.shape, a.dtype)) for a in land_shapes], *deps)
    return (outs[0], outs[1]), list(outs[2:2 + ns]), list(outs[2 + ns:2 + ns + nl]), outs[-1]


def _exchange_wait(name, plan, sems, srcs, lands, after):
    ns, nl = len(srcs), len(lands)

    def body(*refs):
        _run_plan(plan, refs[:ns], refs[ns:ns + nl], refs[ns + nl], refs[ns + nl + 1], False, True)

    outs = pl.pallas_call(
        body, name=name,
        out_shape=[pltpu.HBM(a.shape, a.dtype) for a in list(srcs) + list(lands)],
        in_specs=[HBM_SPEC] * (ns + nl) + [SEM_SPEC, SEM_SPEC] + [ANY] * len(after), out_specs=[HBM_SPEC] * (ns + nl),
        input_output_aliases={i: i for i in range(ns + nl)},
        compiler_params=pltpu.CompilerParams(has_side_effects=pltpu.SideEffectType.DATAFLOW_SIDE_EFFECTING),
    )(*srcs, *lands, sems[0], sems[1], *after)
    return list(outs[:ns]), list(outs[ns:])


def _xchg_begin(name, plan, n_copies, srcs, land_shapes, split, deps=()):
    if not split:
        return dict(split=False, srcs=list(srcs), lands=_exchange_fused(name, plan, n_copies, srcs, land_shapes),
                    token=jnp.zeros((8, LANE), F32))
    sems, srcs_t, lands_t, token = _exchange_start(name + "_start", plan, n_copies, srcs, land_shapes, list(deps))
    return dict(split=True, name=name, plan=plan, sems=sems, srcs=srcs_t, lands=lands_t, token=token)


def _xchg_end(h, after):
    if not h["split"]:
        return h["srcs"], h["lands"]
    return _exchange_wait(h["name"] + "_wait", h["plan"], h["sems"], h["srcs"], h["lands"], after)


def _other_chips():
    x, y, c = _place()
    return [(1 - x, y), (x, 1 - y), (1 - x, 1 - y)]


def _gather_plan(srcs, lands):
    x, y, c = _place()
    me = 2 * x + y
    return [(srcs[a], lands[a].at[me], lands[a].at[2 * cx + cy], (cx, cy, c))
            for (cx, cy) in _other_chips() for a in range(len(srcs))]


def _gather_begin(shards, split, tag, deps=()):
    shapes = [_sds((N_CHIPS,) + a.shape, a.dtype) for a in shards]
    return _xchg_begin(f"gather_{tag}", _gather_plan, 3 * len(shards), shards, shapes, split, deps)


def _gather_end(h, after):
    shards, lands = _xchg_end(h, after)
    me = 2 * lax.axis_index("x") + lax.axis_index("y")
    return [lax.dynamic_update_index_in_dim(g, s, me, 0) for g, s in zip(lands, shards)]


def _swap_plan(srcs, lands):
    x, y, c = _place()
    return [(srcs[a].at[:, 1 - c], lands[a], lands[a], (x, y, 1 - c)) for a in range(len(srcs))]


def _chips_plan(srcs, lands):
    x, y, c = _place()
    me = 2 * x + y
    return [(srcs[a].at[2 * cx + cy], lands[a].at[me], lands[a].at[2 * cx + cy], (cx, cy, c))
            for (cx, cy) in _other_chips() for a in range(len(srcs))]


def _share_plan(srcs, lands):
    x, y, c = _place()
    return [(srcs[a], lands[a].at[c], lands[a].at[1 - c], (x, y, 1 - c)) for a in range(len(srcs))]


def _allreduce_small(slab, dep=None):
    r = slab.shape[0]

    def body(s_ref, o_ref, gath, send_sems, recv_sems):
        x, y, c = _place()
        me = 4 * x + 2 * y + c
        gath[me] = s_ref[...]
        cps = []
        for rel in range(1, N_DEV):
            px = 1 - x if rel & 4 else x
            py = 1 - y if rel & 2 else y
            pc = 1 - c if rel & 1 else c
            cp = pltpu.make_async_remote_copy(src_ref=s_ref, dst_ref=gath.at[me], send_sem=send_sems.at[rel - 1],
                                              recv_sem=recv_sems.at[rel - 1], device_id=(px, py, pc), device_id_type=MESH_T)
            cp.start()
            cps.append(cp)
        for cp in cps:
            cp.wait()
        acc = gath[0]
        for d in range(1, N_DEV):
            acc = acc + gath[d]
        o_ref[...] = acc

    vm = pl.BlockSpec(memory_space=pltpu.VMEM)
    return _call_after(
        dep, body, (slab,), name="allreduce_small", in_specs=[vm], out_specs=vm, out_shape=_sds((r, LANE)),
        scratch_shapes=[pltpu.VMEM((N_DEV, r, LANE), F32), pltpu.SemaphoreType.DMA((N_DEV - 1,)),
                        pltpu.SemaphoreType.DMA((N_DEV - 1,))],
    )


def _add_mine(g4, recv, half):
    _, _, rh, c = g4.shape

    def body(h_ref, g_ref, r_ref, o_ref):
        o_ref[0] = (g_ref[0, 0] + r_ref[0]).astype(o_ref.dtype)

    return pl.pallas_call(
        body, name="add_mine",
        grid_spec=pltpu.PrefetchScalarGridSpec(
            num_scalar_prefetch=1, grid=(N_CHIPS,),
            in_specs=[pl.BlockSpec((1, 1, rh, c), lambda j, h: (j, h[0], 0, 0)), pl.BlockSpec((1, rh, c), lambda j, h: (j, 0, 0))],
            out_specs=pl.BlockSpec((1, rh, c), lambda j, h: (j, 0, 0))),
        out_shape=_sds((N_CHIPS, rh, c), PAYLOAD),
        compiler_params=_params(("parallel",)),
    )(half, g4, recv)


def _add_chips(e, p, me):
    _, rh, c = e.shape

    def body(m_ref, e_ref, p_ref, o_ref):
        own = p_ref[0].astype(F32)
        acc = None
        for s in range(N_CHIPS):
            t = jnp.where(m_ref[0] == s, own, e_ref[s].astype(F32))
            acc = t if acc is None else acc + t
        o_ref[...] = acc

    return pl.pallas_call(
        body, name="add_chips",
        grid_spec=pltpu.PrefetchScalarGridSpec(
            num_scalar_prefetch=1, grid=(1,),
            in_specs=[pl.BlockSpec((N_CHIPS, rh, c), lambda i, m: (0, 0, 0)), pl.BlockSpec((1, rh, c), lambda i, m: (m[0], 0, 0))],
            out_specs=pl.BlockSpec((rh, c), lambda i, m: (0, 0))),
        out_shape=_sds((rh, c)),
        compiler_params=_params(("arbitrary",)),
    )(me, e, p)


def _rs_begin(gs, split, tag):
    g4 = [g.reshape(N_CHIPS, 2, g.shape[0] // (2 * N_CHIPS), g.shape[1]) for g in gs]
    h = _xchg_begin(f"rs_swap_{tag}", _swap_plan, len(gs), g4, [_sds((N_CHIPS,) + g.shape[2:]) for g in g4], split)
    return dict(h=h, split=split, tag=tag, shapes=[g.shape for g in gs])


def _rs_add_mine(st, after):
    g4, recv = _xchg_end(st["h"], after)
    half = jnp.reshape(lax.axis_index("c"), (1,)).astype(jnp.int32)
    ps = [_add_mine(g, r, half) for g, r in zip(g4, recv)]
    st["h"] = _xchg_begin(f"rs_chips_{st['tag']}", _chips_plan, 3 * len(ps), ps, [_sds(p.shape, p.dtype) for p in ps], st["split"])
    return st


def _rs_add_chips(st, after):
    ps, es = _xchg_end(st["h"], after)
    me = jnp.reshape(2 * lax.axis_index("x") + lax.axis_index("y"), (1,)).astype(jnp.int32)
    fs = [_add_chips(e, p, me) for e, p in zip(es, ps)]
    st["h"] = _xchg_begin(f"rs_share_{st['tag']}", _share_plan, len(fs), fs, [_sds((2,) + f.shape) for f in fs], st["split"])
    return st


def _rs_end(st, after):
    fs, ss = _xchg_end(st["h"], after)
    c = lax.axis_index("c")
    return [lax.dynamic_update_index_in_dim(s, f, c, 0).reshape(shp[0] // N_CHIPS, shp[1])
            for s, f, shp in zip(ss, fs, st["shapes"])]


WEIGHTS = ["norm_g", "w_in", "conv_a_w", "ssd_conv_w", "ssd_conv_b", "ssd_dt_bias", "ssd_a_log", "ssd_d", "ssd_norm_g",
           "mla_q_norm_g", "w_qb", "mla_kv_norm_g", "w_kvb", "w_out", "final_norm_g"]
BIG = ["w_in", "w_qb", "w_kvb", "w_out"]
SLAB_ROWS = 128
SMALL_ROWS = 72


def _to_slab(parts, rows):
    flat = jnp.concatenate([p.reshape(-1) for p in parts])
    return jnp.pad(flat, (0, rows * LANE - flat.shape[0])).reshape(rows, LANE)


def _from_slab(slab, shapes):
    flat = slab.reshape(-1)
    out, off = [], 0
    for shp in shapes:
        n = int(np.prod(shp))
        out.append(flat[off:off + n].reshape(shp))
        off += n
    return out


def kernel(x, positions, norm_g, w_in, conv_a_w, ssd_conv_w, ssd_conv_b, ssd_dt_bias, ssd_a_log, ssd_d, ssd_norm_g, mla_q_norm_g, w_qb, mla_kv_norm_g, w_kvb, w_out, final_norm_g, loss_target, m_norm_g, m_w_in, m_conv_a_w, m_ssd_conv_w, m_ssd_conv_b, m_ssd_dt_bias, m_ssd_a_log, m_ssd_d, m_ssd_norm_g, m_mla_q_norm_g, m_w_qb, m_mla_kv_norm_g, m_w_kvb, m_w_out, m_final_norm_g, v_norm_g, v_w_in, v_conv_a_w, v_ssd_conv_w, v_ssd_conv_b, v_ssd_dt_bias, v_ssd_a_log, v_ssd_d, v_ssd_norm_g, v_mla_q_norm_g, v_w_qb, v_mla_kv_norm_g, v_w_kvb, v_w_out, v_final_norm_g):
    w = dict(norm_g=norm_g, w_in=w_in, conv_a_w=conv_a_w, ssd_conv_w=ssd_conv_w, ssd_conv_b=ssd_conv_b,
             ssd_dt_bias=ssd_dt_bias, ssd_a_log=ssd_a_log, ssd_d=ssd_d, ssd_norm_g=ssd_norm_g, mla_q_norm_g=mla_q_norm_g,
             w_qb=w_qb, mla_kv_norm_g=mla_kv_norm_g, w_kvb=w_kvb, w_out=w_out, final_norm_g=final_norm_g)
    mom = dict(norm_g=m_norm_g, w_in=m_w_in, conv_a_w=m_conv_a_w, ssd_conv_w=m_ssd_conv_w, ssd_conv_b=m_ssd_conv_b,
               ssd_dt_bias=m_ssd_dt_bias, ssd_a_log=m_ssd_a_log, ssd_d=m_ssd_d, ssd_norm_g=m_ssd_norm_g,
               mla_q_norm_g=m_mla_q_norm_g, w_qb=m_w_qb, mla_kv_norm_g=m_mla_kv_norm_g, w_kvb=m_w_kvb, w_out=m_w_out,
               final_norm_g=m_final_norm_g)
    var = dict(norm_g=v_norm_g, w_in=v_w_in, conv_a_w=v_conv_a_w, ssd_conv_w=v_ssd_conv_w, ssd_conv_b=v_ssd_conv_b,
               ssd_dt_bias=v_ssd_dt_bias, ssd_a_log=v_ssd_a_log, ssd_d=v_ssd_d, ssd_norm_g=v_ssd_norm_g,
               mla_q_norm_g=v_mla_q_norm_g, w_qb=v_w_qb, mla_kv_norm_g=v_mla_kv_norm_g, w_kvb=v_w_kvb, w_out=v_w_out,
               final_norm_g=v_final_norm_g)
    chip = 2 * lax.axis_index("x") + lax.axis_index("y")

    def early_shard(l, zero):
        pack = jnp.pad(conv_a_w[l], ((0, 5), (0, 192))) + jnp.pad(ssd_conv_w[l], ((3, 1), (0, 32)))
        return [(_perm_cols(w_in[l]) + zero).astype(MXU), pack + zero]

    def late_shard(l, zero):
        return [(w_out[l] + zero).astype(MXU), (w_qb[l].T + zero).astype(MXU), (w_kvb[l].T + zero).astype(MXU)]

    def early_weights(l, gathered):
        g_in, g_conv = gathered
        return dict(
            norm_g=norm_g[l][None], w_in=g_in.reshape(D_MODEL, NCOL),
            conv_a_w=jnp.concatenate([g_conv[j, 0:3, 0:64] for j in range(N_CHIPS)], axis=1),
            ssd_conv_w=jnp.concatenate([g_conv[j, 3:7, 0:224] for j in range(N_CHIPS)], axis=1),
            ssd_conv_b=ssd_conv_b[l][None], sc=_ssd_scalars(ssd_dt_bias[l], ssd_a_log[l], ssd_d[l]),
            g_ssd=ssd_norm_g[l][None], gq=mla_q_norm_g[l][None], gkv=mla_kv_norm_g[l][None])

    def late_weights(gathered):
        g_out, g_qb, g_kvb = gathered
        return dict(wq=_wq_layout(g_qb.reshape(MLA_HEADS * 96, Q_LORA)), wkv=_wkv_layout(g_kvb.reshape(MLA_HEADS * LANE, KV_LORA)),
                    w_out=g_out.reshape(D_MODEL, D_MODEL))

    def large_grads(g):
        wq = jnp.pad(_wq_unlayout(g["wq"]).reshape(N_CHIPS, 144, Q_LORA), ((0, 0), (0, 16), (0, 0)))
        return [g["w_in"], g["w_out"], wq.reshape(N_CHIPS * 160, Q_LORA), _wkv_unlayout(g["wkv"])]

    gather_a0 = _gather_begin(early_shard(0, 0.0), True, "a0")
    zero = gather_a0["token"][0, 0]
    cos, sin = _rope_tables(positions[0] + zero.astype(jnp.int32))
    late0, shards1 = late_shard(0, zero), early_shard(1, zero) + late_shard(1, zero)
    opt_in = {nm: [a[nm] + zero for a in (w, mom, var)] for nm in BIG}
    lw0 = early_weights(0, _gather_end(gather_a0, [cos, sin] + late0 + shards1 + [a for nm in BIG for a in opt_in[nm]]))
    gather_b0 = _gather_begin(late0, True, "b0")
    gather_1 = _gather_begin(shards1, True, "1", [gather_b0["token"]])
    x1, sv0, lw0 = _layer_fwd(x[0], lw0, cos, sin, gather_1["token"],
                              lambda ya, y_ssd: late_weights(_gather_end(gather_b0, [ya, y_ssd])))
    g1 = _gather_end(gather_1, [x1])
    x2, sv1, lw1 = _layer_fwd(x1, {**early_weights(1, g1[:2]), **late_weights(g1[2:])}, cos, sin)
    dx, dgf, loss = _loss_head(x2, final_norm_g[None], loss_target[0])

    dx, lg1, _ = _layer_bwd(dx, lw1, sv1, cos, sin)
    grad_x, lg0, red1 = _layer_bwd(dx, lw0, sv0, cos, sin, _rs_begin(large_grads(lg1), True, 1))
    rs0 = _rs_begin(large_grads(lg0), True, 0)
    lg = [lg0, lg1]
    grad = {}

    small_names = ["norm_g", "conv_a_w", "ssd_conv_w", "ssd_conv_b", "sc", "g_ssd", "gq", "gkv"]
    parts = [loss[0, 0:1], dgf]
    for l in range(DEPTH):
        parts += [lg[l][nm][:3, DT_LANE:DT_LANE + SSD_HEADS] if nm == "sc" else lg[l][nm] for nm in small_names]
    shapes = [(1,), (D_MODEL,)] + [(D_MODEL,), (3, D_CONV_A), (4, N_XBC), (N_XBC,), (3, SSD_HEADS), (D_SSD,), (Q_LORA,), (KV_LORA,)] * DEPTH
    red_slab = _allreduce_small(_to_slab(parts, SLAB_ROWS), rs0["h"]["token"])
    rs0 = _rs_add_mine(rs0, [red_slab])
    red = _from_slab(red_slab + rs0["h"]["token"][0, 0], shapes)
    loss_out = red[0][0]
    grad["final_norm_g"] = red[1]
    per = [red[2 + 8 * l:10 + 8 * l] for l in range(DEPTH)]
    grad["norm_g"] = jnp.stack([per[l][0] for l in range(DEPTH)])
    grad["conv_a_w"] = lax.dynamic_slice_in_dim(jnp.stack([per[l][1] for l in range(DEPTH)]), chip * 64, 64, axis=2)
    grad["ssd_conv_w"] = lax.dynamic_slice_in_dim(jnp.stack([per[l][2] for l in range(DEPTH)]), chip * 224, 224, axis=2)
    grad["ssd_conv_b"] = jnp.stack([per[l][3] for l in range(DEPTH)])
    grad["ssd_dt_bias"] = jnp.stack([per[l][4][0] for l in range(DEPTH)])
    grad["ssd_a_log"] = jnp.stack([per[l][4][1] for l in range(DEPTH)])
    grad["ssd_d"] = jnp.stack([per[l][4][2] for l in range(DEPTH)])
    grad["ssd_norm_g"] = jnp.stack([per[l][5] for l in range(DEPTH)])
    grad["mla_q_norm_g"] = jnp.stack([per[l][6] for l in range(DEPTH)])
    grad["mla_kv_norm_g"] = jnp.stack([per[l][7] for l in range(DEPTH)])

    delta, new_m, new_v = {}, {}, {}
    small = [nm for nm in WEIGHTS if nm not in BIG]
    sshapes = [w[nm].shape for nm in small]
    d, mo, vo = _adamw(_to_slab([w[nm] for nm in small], SMALL_ROWS), _to_slab([grad[nm] for nm in small], SMALL_ROWS),
                       _to_slab([mom[nm] for nm in small], SMALL_ROWS), _to_slab([var[nm] for nm in small], SMALL_ROWS))
    small_out = list(zip(small, _from_slab(d, sshapes), _from_slab(mo, sshapes), _from_slab(vo, sshapes)))
    for nm, dv, mv, vv in small_out:
        delta[nm], new_m[nm], new_v[nm] = dv, mv, vv

    red0 = _rs_end(_rs_add_chips(rs0, [a for row in small_out for a in row[1:]] + [grad[nm] for nm in small]), [])
    r_in, r_out, r_qb, r_kvb = [jnp.stack([a, b]) for a, b in zip(red0, red1)]
    grad.update(w_in=_unperm_cols(r_in), w_out=r_out, w_qb=jnp.swapaxes(r_qb[:, :144], 1, 2), w_kvb=jnp.swapaxes(r_kvb, 1, 2))
    for nm in BIG:
        delta[nm], new_m[nm], new_v[nm] = _adamw(opt_in[nm][0], grad[nm], opt_in[nm][1], opt_in[nm][2])

    return (loss_out, grad_x[None], *[grad[nm] for nm in WEIGHTS], *[delta[nm] for nm in WEIGHTS],
            *[new_m[nm] for nm in WEIGHTS], *[new_v[nm] for nm in WEIGHTS])
```

```python
import functools
import math

import numpy as np
import jax
import jax.numpy as jnp
from jax import lax
from jax.experimental import pallas as pl
from jax.experimental.pallas import tpu as pltpu

F32 = jnp.float32
MXU = jnp.bfloat16

D_MODEL = 1024
DEPTH = 2
D_CONV_A = 256
D_SSD = 384
SSD_HEADS = 6
SSD_BC = 256
SSD_CHUNK = 128
SSD_NORM_EPS = 1e-5
MLA_HEADS = 6
Q_LORA = 256
KV_LORA = 128
QK_NOPE = 64
QK_ROPE = 32
V_DIM = 64
D_MLA = 384
ROPE_BASE = 10000.0
NORM_EPS = 1e-6
IN_COLS = 3110
LANE = 128

O_AH, O_AB, O_AC, O_AZ = 0, 256, 512, 768
O_XBC = 1024
O_SZ = 1920
O_CQA = 2304
O_CKV = 2560
O_CZ = 2688
O_TAIL = 3072
NCOL = 3200
N_XBC = D_SSD + 2 * SSD_BC
DT_LANE = 32
ROPE_LANE = 64

ADAM_LR, ADAM_B1, ADAM_B2, ADAM_EPS, ADAM_WD, ADAM_STEP = 0.001, 0.9, 0.999, 1e-08, 0.01, 10

VMEM_LIMIT = 56 * 1024 * 1024
MESH_T = pl.DeviceIdType.MESH


def _dot(a, b):
    return jnp.dot(a.astype(MXU), b.astype(MXU), preferred_element_type=F32)


def _dot_nt(a, b):
    return lax.dot_general(a.astype(MXU), b.astype(MXU), (((1,), (1,)), ((), ())), preferred_element_type=F32)


def _dot_tn(a, b):
    return lax.dot_general(a.astype(MXU), b.astype(MXU), (((0,), (0,)), ((), ())), preferred_element_type=F32)


def _dot_hi(a, b):
    return jnp.dot(a, b, precision=lax.Precision.HIGHEST, preferred_element_type=F32)


def _dot_hi_tn(a, b):
    return lax.dot_general(a, b, (((0,), (0,)), ((), ())), precision=lax.Precision.HIGHEST, preferred_element_type=F32)


def _sigmoid(z):
    return 1.0 / (1.0 + jnp.exp(-z))


def _silu(z):
    return z * _sigmoid(z)


def _dsilu(z):
    s = _sigmoid(z)
    return s * (1.0 + z * (1.0 - s))


def _softplus(z):
    e = jnp.exp(-jnp.abs(z))
    return jnp.maximum(z, 0.0) + jnp.where(e < 1e-3, e * (1.0 - 0.5 * e), jnp.log(1.0 + e))


def _iota(shape, dim):
    return lax.broadcasted_iota(jnp.int32, shape, dim)


def _shift_down(u, k):
    if k == 0:
        return u
    return jnp.where(_iota(u.shape, 0) >= k, pltpu.roll(u, k, 0), 0.0)


def _shift_up(u, k):
    if k == 0:
        return u
    n = u.shape[0]
    return jnp.where(_iota(u.shape, 0) < n - k, pltpu.roll(u, n - k, 0), 0.0)


def _rope_swap(t):
    lane = _iota(t.shape, 1)
    lo = (lane >= ROPE_LANE) & (lane < ROPE_LANE + 16)
    hi = (lane >= ROPE_LANE + 16) & (lane < ROPE_LANE + 32)
    return jnp.where(lo, pltpu.roll(t, LANE - 16, 1), jnp.where(hi, pltpu.roll(t, 16, 1), 0.0))


def _params(sem=None):
    return pltpu.CompilerParams(dimension_semantics=sem, vmem_limit_bytes=VMEM_LIMIT)


def _full(shape):
    nd = len(shape)
    return pl.BlockSpec(shape, lambda *_: (0,) * nd)


def _sds(shape, dtype=F32):
    return jax.ShapeDtypeStruct(shape, dtype)


def _tile(s):
    return min(256, s)


def _row(ts, w):
    return pl.BlockSpec((ts, w), lambda i: (i, 0))


def _gate_cols(ts, off):
    return pl.BlockSpec((ts, D_SSD), lambda i, _o=off // D_SSD: (i, _o))


def _col(s, off):
    return pl.BlockSpec((s, LANE), lambda j, _o=off // LANE: (0, _o + j))


def _call_after(dep, body, args, *, in_specs, **kw):
    if dep is None:
        return pl.pallas_call(body, in_specs=in_specs, **kw)(*args)
    n = len(args)

    def body_dep(*refs):
        body(*refs[:n], *refs[n + 1:])

    return pl.pallas_call(body_dep, in_specs=list(in_specs) + [pl.BlockSpec(memory_space=pl.ANY)], **kw)(*args, dep)


def _rms(c, g):
    r = lax.rsqrt(jnp.mean(c * c, axis=-1, keepdims=True) + NORM_EPS)
    return c * r * g, r


def _rms_bwd(dn, c, r, g):
    ch = c * r
    dch = dn * g
    dc = r * (dch - ch * jnp.mean(dch * ch, axis=-1, keepdims=True))
    return dc, jnp.sum(dn * ch, axis=0, keepdims=True)


def _inproj_fwd(x, g, w, dep=None):
    s = x.shape[0]
    ts = _tile(s)

    def body(x_ref, g_ref, w_ref, proj_ref, h_ref, r_ref):
        hn, r = _rms(x_ref[...], g_ref[...])
        h = hn.astype(MXU)
        h_ref[...] = h
        r_ref[...] = r
        proj_ref[...] = jnp.dot(h, w_ref[...], preferred_element_type=F32)

    return _call_after(
        dep, body, (x, g, w), name="inproj_fwd", grid=(s // ts,),
        in_specs=[_row(ts, D_MODEL), _full((1, D_MODEL)), _full((D_MODEL, NCOL))],
        out_specs=[_row(ts, NCOL), _row(ts, D_MODEL), _row(ts, 1)],
        out_shape=[_sds((s, NCOL)), _sds((s, D_MODEL), MXU), _sds((s, 1))],
        compiler_params=_params(("parallel",)),
    )


def _conva_fwd(proj, w):
    s = proj.shape[0]

    def body(h_ref, b_ref, c_ref, z_ref, w_ref, y_ref):
        u = c_ref[...] * h_ref[...]
        wv = w_ref[...]
        cv = wv[2:3, :] * u + wv[1:2, :] * _shift_down(u, 1) + wv[0:1, :] * _shift_down(u, 2)
        y_ref[...] = b_ref[...] * cv * _silu(z_ref[...])

    return pl.pallas_call(
        body, name="conva_fwd", grid=(D_CONV_A // LANE,),
        in_specs=[_col(s, O_AH), _col(s, O_AB), _col(s, O_AC), _col(s, O_AZ), pl.BlockSpec((3, LANE), lambda j: (0, j))],
        out_specs=pl.BlockSpec((s, LANE), lambda j: (0, j)),
        out_shape=_sds((s, D_CONV_A)),
        compiler_params=_params(("parallel",)),
    )(proj, proj, proj, proj, w)


def _sconv_pre(u, wv, bv):
    return (wv[3:4, :] * u + wv[2:3, :] * _shift_down(u, 1) + wv[1:2, :] * _shift_down(u, 2)
            + wv[0:1, :] * _shift_down(u, 3) + bv)


def _sconv_fwd(proj, w, b):
    s = proj.shape[0]

    def body(u_ref, w_ref, b_ref, o_ref):
        o_ref[...] = _silu(_sconv_pre(u_ref[...], w_ref[...], b_ref[...]))

    return pl.pallas_call(
        body, name="sconv_fwd", grid=(N_XBC // LANE,),
        in_specs=[_col(s, O_XBC), pl.BlockSpec((4, LANE), lambda j: (0, j)), pl.BlockSpec((1, LANE), lambda j: (0, j))],
        out_specs=pl.BlockSpec((s, LANE), lambda j: (0, j)),
        out_shape=_sds((s, N_XBC)),
        compiler_params=_params(("parallel",)),
    )(proj, w, b)


def _ssd_chunk_common(tail, sc):
    l = SSD_CHUNK
    lane = _iota((l, LANE), 1)
    row = _iota((l, LANE), 0)
    tri = (row >= lane).astype(F32)
    a_row = -jnp.exp(sc[1:2, :])
    pre = tail + sc[0:1, :]
    dt = _softplus(pre)
    a_cs = _dot_hi(tri, dt * a_row)
    return lane, row, tri, a_row, pre, dt, a_cs, a_cs.T


def _pick_col(m, lane, k):
    return jnp.sum(jnp.where(lane == k, m, 0.0), axis=1, keepdims=True)


def _pick_row(m, row, k):
    return jnp.sum(jnp.where(row == k, m, 0.0), axis=0, keepdims=True)


def _ssd_fwd(xbc, proj, sc):
    s = xbc.shape[0]
    nc = s // SSD_CHUNK
    l = SSD_CHUNK

    def body(xbc_ref, tail_ref, sc_ref, y_ref, st_ref, state):
        @pl.when(pl.program_id(0) == 0)
        def _():
            state[...] = jnp.zeros_like(state)

        sc_v = sc_ref[...]
        lane, row, _, _, _, dt, a_cs, a_t = _ssd_chunk_common(tail_ref[...], sc_v)
        lane1 = _iota((1, LANE), 1)
        rowp = _iota((LANE, 1), 0)
        d_row = sc_v[2:3, :]
        for j in range(3):
            st_ref[0, j] = state[j]
        for j in range(3):
            xpair = xbc_ref[:, LANE * j:LANE * (j + 1)]
            sp = state[j]
            ypair = jnp.zeros((l, LANE), F32)
            new_s = jnp.zeros((LANE, LANE), F32)
            decay = jnp.zeros((LANE, 1), F32)
            for half in range(2):
                h = 2 * j + half
                g = h // 3
                hm = (lane < 64) if half == 0 else (lane >= 64)
                hrow = (rowp < 64) if half == 0 else (rowp >= 64)
                ac = _pick_col(a_cs, lane, DT_LANE + h)
                ar = _pick_row(a_t, row, DT_LANE + h)
                dtc = _pick_col(dt, lane, DT_LANE + h)
                alast = jnp.sum(jnp.where(lane1 == l - 1, ar, 0.0), axis=1, keepdims=True)
                dh = jnp.sum(jnp.where(lane1 == DT_LANE + h, d_row, 0.0), axis=1, keepdims=True)
                xm = jnp.where(hm, xpair, 0.0)
                xd = xm * dtc
                bm = xbc_ref[:, D_SSD + LANE * g:D_SSD + LANE * (g + 1)]
                cm = xbc_ref[:, D_SSD + SSD_BC + LANE * g:D_SSD + SSD_BC + LANE * (g + 1)]
                lm = jnp.where(row >= lane, jnp.exp(jnp.minimum(ac - ar, 0.0)), 0.0)
                y_diag = _dot(_dot_nt(cm, bm) * lm, xd)
                y_off = jnp.where(hm, _dot_nt(cm, sp), 0.0) * jnp.exp(ac)
                ypair = ypair + y_diag + y_off + xm * dh
                new_s = new_s + _dot_tn(xd * jnp.exp(alast - ac), bm)
                decay = jnp.where(hrow, jnp.exp(alast), decay)
            state[j] = sp * decay + new_s
            y_ref[:, LANE * j:LANE * (j + 1)] = ypair

    return pl.pallas_call(
        body, name="ssd_fwd", grid=(nc,),
        in_specs=[pl.BlockSpec((l, N_XBC), lambda c: (c, 0)),
                  pl.BlockSpec((l, LANE), lambda c: (c, O_TAIL // LANE)), _full((8, LANE))],
        out_specs=[pl.BlockSpec((l, D_SSD), lambda c: (c, 0)), pl.BlockSpec((1, 3, LANE, LANE), lambda c: (c, 0, 0, 0))],
        out_shape=[_sds((s, D_SSD)), _sds((nc, 3, LANE, LANE))],
        scratch_shapes=[pltpu.VMEM((3, LANE, LANE), F32)],
        compiler_params=_params(("arbitrary",)),
    )(xbc, proj, sc)


def _mla_prep_fwd(proj, gq, gkv, wq, wkv, cos, sin):
    s = proj.shape[0]
    ts = _tile(s)
    nh = MLA_HEADS

    def body(cqa_ref, ckv_ref, tail_ref, gq_ref, gkv_ref, wq_ref, wkv_ref, cos_ref, sin_ref,
             q_ref, k_ref, v_ref, qn_ref, kvn_ref, rq_ref, rkv_ref):
        qn, rq = _rms(cqa_ref[...], gq_ref[...])
        kvn, rkv = _rms(ckv_ref[...], gkv_ref[...])
        qn = qn.astype(MXU)
        kvn = kvn.astype(MXU)
        qn_ref[...] = qn
        kvn_ref[...] = kvn
        rq_ref[...] = rq
        rkv_ref[...] = rkv
        q = _dot_nt(qn, wq_ref[...])
        kv = _dot_nt(kvn, wkv_ref[...])
        cosv = cos_ref[...]
        sinv = sin_ref[...]
        lane = _iota((ts, LANE), 1)
        rope_lanes = (lane >= ROPE_LANE) & (lane < ROPE_LANE + QK_ROPE)
        kr = jnp.where(rope_lanes, pltpu.roll(tail_ref[...], ROPE_LANE, 1), 0.0)
        kr = kr * cosv + _rope_swap(kr) * sinv
        for h in range(nh):
            qh = q[:, LANE * h:LANE * (h + 1)]
            q_ref[h] = ((qh * cosv + _rope_swap(qh) * sinv) * ATT_SCALE).astype(MXU)
            k_ref[h] = (kv[:, LANE * h:LANE * (h + 1)] + kr).astype(MXU)
            v_ref[h] = kv[:, LANE * (nh + h):LANE * (nh + h + 1)].astype(MXU)

    head = pl.BlockSpec((nh, ts, LANE), lambda i: (0, i, 0))
    return pl.pallas_call(
        body, name="mla_prep_fwd", grid=(s // ts,),
        in_specs=[pl.BlockSpec((ts, Q_LORA), lambda i: (i, O_CQA // Q_LORA)),
                  pl.BlockSpec((ts, KV_LORA), lambda i: (i, O_CKV // KV_LORA)),
                  pl.BlockSpec((ts, LANE), lambda i: (i, O_TAIL // LANE)),
                  _full((1, Q_LORA)), _full((1, KV_LORA)), _full((nh * LANE, Q_LORA)), _full((2 * nh * LANE, KV_LORA)),
                  _row(ts, LANE), _row(ts, LANE)],
        out_specs=[head, head, head, _row(ts, Q_LORA), _row(ts, KV_LORA), _row(ts, 1), _row(ts, 1)],
        out_shape=[_sds((nh, s, LANE), MXU)] * 3 + [_sds((s, Q_LORA), MXU), _sds((s, KV_LORA), MXU), _sds((s, 1)), _sds((s, 1))],
        compiler_params=_params(("parallel",)),
    )(proj, proj, proj, gq, gkv, wq, wkv, cos, sin)


ATT_SCALE = (QK_NOPE + QK_ROPE) ** -0.5
NEG = -1e30


def _att_tile(s):
    return min(256, s // 2)


def _attn_fwd(q, k, v):
    nh, s, _ = q.shape
    tq = _att_tile(s)
    nq = s // tq

    def body(q_ref, k_ref, v_ref, o_ref, lse_ref):
        i = pl.program_id(1)
        rowi = _iota((tq, tq), 0)
        coli = _iota((tq, tq), 1)
        zero = (jnp.full((tq, 1), NEG, F32), jnp.zeros((tq, 1), F32), jnp.zeros((tq, LANE), F32))
        state = [zero, zero]
        done = [zero, zero]
        for t in range(nq + 1):
            first = t <= i
            qblk = jnp.where(first, i, nq - 1 - i)
            kblk = jnp.where(first, t, t - i - 1)
            qoff = pl.multiple_of(qblk * tq, tq)
            koff = pl.multiple_of(kblk * tq, tq)
            keep = coli <= rowi + jnp.where(kblk == qblk, 0, tq)
            restart = t == i + 1
            for hh in range(2):
                m, lsum, acc = state[hh]
                if t > 0:
                    done[hh] = tuple(jnp.where(restart, a, b) for a, b in zip(state[hh], done[hh]))
                    m = jnp.where(restart, NEG, m)
                    lsum = jnp.where(restart, 0.0, lsum)
                    acc = jnp.where(restart, 0.0, acc)
                sc = _dot_nt(q_ref[hh, pl.ds(qoff, tq), :], k_ref[hh, pl.ds(koff, tq), :])
                sc = jnp.where(keep, sc, NEG)
                m_new = jnp.maximum(m, jnp.max(sc, axis=1, keepdims=True))
                p = jnp.exp(sc - m_new)
                alpha = jnp.exp(m - m_new)
                lsum = alpha * lsum + jnp.sum(p, axis=1, keepdims=True)
                acc = alpha * acc + _dot(p, v_ref[hh, pl.ds(koff, tq), :])
                state[hh] = (m_new, lsum, acc)
        for blk, res in ((i, done), (nq - 1 - i, state)):
            off = pl.multiple_of(blk * tq, tq)
            out = None
            for hh in range(2):
                m, lsum, acc = res[hh]
                o = acc * (1.0 / lsum)
                lse_ref[hh, pl.ds(off, tq), :] = m + jnp.log(lsum)
                out = o if hh == 0 else out + pltpu.roll(o, V_DIM, 1)
            o_ref[pl.ds(off, tq), :] = out

    pair = pl.BlockSpec((2, s, LANE), lambda j, i: (j, 0, 0))
    return pl.pallas_call(
        body, name="attn_fwd", grid=(nh // 2, nq // 2),
        in_specs=[pair, pair, pair],
        out_specs=[pl.BlockSpec((s, LANE), lambda j, i: (0, j)), pl.BlockSpec((2, s, 1), lambda j, i: (j, 0, 0))],
        out_shape=[_sds((s, D_MLA)), _sds((nh, s, 1))],
        compiler_params=_params(("parallel", "arbitrary")),
    )(q, k, v)


def _ssd_gate(y_ssd, s_z, g):
    yz = y_ssd * _silu(s_z)
    g0 = _iota(yz.shape, 1) < D_SSD // 2
    sq = yz * yz
    ms0 = jnp.sum(jnp.where(g0, sq, 0.0), axis=1, keepdims=True) / (D_SSD // 2)
    ms1 = jnp.sum(jnp.where(g0, 0.0, sq), axis=1, keepdims=True) / (D_SSD // 2)
    r = jnp.where(g0, lax.rsqrt(ms0 + SSD_NORM_EPS), lax.rsqrt(ms1 + SSD_NORM_EPS))
    nrm = yz * r
    return nrm * g, nrm, r, g0


def _outproj_fwd(x, proj, ya, y_ssd, o, g_ssd, w):
    s = x.shape[0]
    ts = _tile(s)

    def body(x_ref, sz_ref, cz_ref, ya_ref, ys_ref, o_ref, g_ref, w_ref, xo_ref, y_ref):
        yb = _ssd_gate(ys_ref[...], sz_ref[...], g_ref[...])[0]
        yc = o_ref[...] * _silu(cz_ref[...])
        y = jnp.concatenate([ya_ref[...], yb, yc], axis=1).astype(MXU)
        y_ref[...] = y
        xo_ref[...] = x_ref[...] + jnp.dot(y, w_ref[...], preferred_element_type=F32)

    return pl.pallas_call(
        body, name="outproj_fwd", grid=(s // ts,),
        in_specs=[_row(ts, D_MODEL), _gate_cols(ts, O_SZ), _gate_cols(ts, O_CZ), _row(ts, D_CONV_A), _row(ts, D_SSD),
                  _row(ts, D_MLA), _full((1, D_SSD)), _full((D_MODEL, D_MODEL))],
        out_specs=[_row(ts, D_MODEL), _row(ts, D_MODEL)],
        out_shape=[_sds((s, D_MODEL)), _sds((s, D_MODEL), MXU)],
        compiler_params=_params(("parallel",)),
    )(x, proj, proj, ya, y_ssd, o, g_ssd, w)


def _loss_head(x, g, tgt):
    s = x.shape[0]
    ts = _tile(s)

    def body(x_ref, g_ref, t_ref, dx_ref, dg_ref, loss_ref):
        @pl.when(pl.program_id(0) == 0)
        def _():
            dg_ref[...] = jnp.zeros_like(dg_ref)
            loss_ref[...] = jnp.zeros_like(loss_ref)

        xv = x_ref[...]
        gv = g_ref[...]
        yn, r = _rms(xv, gv)
        e = yn - t_ref[...]
        loss_ref[...] += jnp.sum(jnp.sum(e * e, axis=1, keepdims=True), axis=0, keepdims=True) * (0.5 / D_MODEL)
        dx, dg = _rms_bwd(e * (1.0 / D_MODEL), xv, r, gv)
        dx_ref[...] = dx
        dg_ref[...] += dg

    return pl.pallas_call(
        body, name="loss_head", grid=(s // ts,),
        in_specs=[_row(ts, D_MODEL), _full((1, D_MODEL)), _row(ts, D_MODEL)],
        out_specs=[_row(ts, D_MODEL), _full((1, D_MODEL)), _full((1, LANE))],
        out_shape=[_sds((s, D_MODEL)), _sds((1, D_MODEL)), _sds((1, LANE))],
        compiler_params=_params(("arbitrary",)),
    )(x, g, tgt)


def _outproj_bwd(dout, y, w, proj, y_ssd, o, g_ssd, dep=None):
    s = dout.shape[0]
    ts = _tile(s)

    def body(dout_ref, y_ref, w_ref, sz_ref, cz_ref, ys_ref, o_ref, g_ref,
             dya_ref, dys_ref, dsz_ref, dattn_ref, dcz_ref, dg_ref, dw_ref):
        @pl.when(pl.program_id(0) == 0)
        def _():
            dw_ref[...] = jnp.zeros_like(dw_ref)
            dg_ref[...] = jnp.zeros_like(dg_ref)

        dout_b = dout_ref[...].astype(MXU)
        dw_ref[...] += _dot_tn(y_ref[...], dout_b)
        dy = _dot_nt(dout_b, w_ref[...])
        dya_ref[...] = dy[:, :D_CONV_A]
        dyb = dy[:, D_CONV_A:D_CONV_A + D_SSD]
        sz = sz_ref[...]
        ys = ys_ref[...]
        gv = g_ref[...]
        _, nrm, r, g0 = _ssd_gate(ys, sz, gv)
        dg_ref[...] += jnp.sum(dyb * nrm, axis=0, keepdims=True)
        dn = dyb * gv
        t = dn * nrm
        mean = jnp.where(g0, jnp.sum(jnp.where(g0, t, 0.0), axis=1, keepdims=True),
                         jnp.sum(jnp.where(g0, 0.0, t), axis=1, keepdims=True)) / (D_SSD // 2)
        dyz = r * (dn - nrm * mean)
        dys_ref[...] = dyz * _silu(sz)
        dsz_ref[...] = dyz * ys * _dsilu(sz)
        dyc = dy[:, D_CONV_A + D_SSD:]
        cz = cz_ref[...]
        dattn_ref[...] = dyc * _silu(cz)
        dcz_ref[...] = dyc * o_ref[...] * _dsilu(cz)

    return _call_after(
        dep, body, (dout, y, w, proj, proj, y_ssd, o, g_ssd), name="outproj_bwd", grid=(s // ts,),
        in_specs=[_row(ts, D_MODEL), _row(ts, D_MODEL), _full((D_MODEL, D_MODEL)), _gate_cols(ts, O_SZ), _gate_cols(ts, O_CZ),
                  _row(ts, D_SSD), _row(ts, D_MLA), _full((1, D_SSD))],
        out_specs=[_row(ts, D_CONV_A), _row(ts, D_SSD), _row(ts, D_SSD), _row(ts, D_MLA), _row(ts, D_MLA),
                   _full((1, D_SSD)), _full((D_MODEL, D_MODEL))],
        out_shape=[_sds((s, D_CONV_A)), _sds((s, D_SSD)), _sds((s, D_SSD)), _sds((s, D_MLA)), _sds((s, D_MLA)),
                   _sds((1, D_SSD)), _sds((D_MODEL, D_MODEL))],
        compiler_params=_params(("arbitrary",)),
    )


def _attn_bwd(q, k, v, o, d_o, lse, dep=None):
    nh, s, _ = q.shape
    tq = _att_tile(s)
    nq = s // tq

    def body(q_ref, k_ref, v_ref, o_ref, do_ref, lse_ref, dq_ref, dk_ref, dv_ref, dop, delta):
        i = pl.program_id(1)

        @pl.when(i == 0)
        def _():
            lane = _iota((s, LANE), 1)
            for hh in range(2):
                dov = do_ref[...]
                ov = o_ref[...]
                if hh == 1:
                    dov = pltpu.roll(dov, V_DIM, 1)
                    ov = pltpu.roll(ov, V_DIM, 1)
                dov = jnp.where(lane < V_DIM, dov, 0.0)
                dop[hh] = dov.astype(MXU)
                delta[hh] = jnp.sum(dov * ov, axis=1, keepdims=True)
                dq_ref[hh] = jnp.zeros((s, LANE), F32)

        rowi = _iota((tq, tq), 0)
        coli = _iota((tq, tq), 1)
        z = jnp.zeros((tq, LANE), F32)
        state = [(z, z), (z, z)]
        done = [(z, z), (z, z)]
        for t in range(nq + 1):
            first = t <= nq - 1 - i
            kblk = jnp.where(first, i, nq - 1 - i)
            qblk = jnp.where(first, i + t, t - 1)
            qoff = pl.multiple_of(qblk * tq, tq)
            koff = pl.multiple_of(kblk * tq, tq)
            keep = coli <= rowi + jnp.where(kblk == qblk, 0, tq)
            restart = t == nq - i
            for hh in range(2):
                dk, dv = state[hh]
                if t > 0:
                    done[hh] = tuple(jnp.where(restart, a, b) for a, b in zip(state[hh], done[hh]))
                    dk = jnp.where(restart, 0.0, dk)
                    dv = jnp.where(restart, 0.0, dv)
                kb = k_ref[hh, pl.ds(koff, tq), :]
                qb = q_ref[hh, pl.ds(qoff, tq), :]
                dob = dop[hh, pl.ds(qoff, tq), :]
                sc = jnp.where(keep, _dot_nt(qb, kb), NEG)
                p = jnp.exp(sc - lse_ref[hh, pl.ds(qoff, tq), :])
                dp = _dot_nt(dob, v_ref[hh, pl.ds(koff, tq), :])
                ds = p * (dp - delta[hh, pl.ds(qoff, tq), :])
                dq_ref[hh, pl.ds(qoff, tq), :] += _dot(ds, kb)
                state[hh] = (dk + _dot_tn(ds, qb), dv + _dot_tn(p, dob))
        for blk, res in ((i, done), (nq - 1 - i, state)):
            off = pl.multiple_of(blk * tq, tq)
            for hh in range(2):
                dk_ref[hh, pl.ds(off, tq), :] = res[hh][0]
                dv_ref[hh, pl.ds(off, tq), :] = res[hh][1]

    pair = pl.BlockSpec((2, s, LANE), lambda j, i: (j, 0, 0))
    return _call_after(
        dep, body, (q, k, v, o, d_o, lse), name="attn_bwd", grid=(nh // 2, nq // 2),
        in_specs=[pair, pair, pair, pl.BlockSpec((s, LANE), lambda j, i: (0, j)), pl.BlockSpec((s, LANE), lambda j, i: (0, j)),
                  pl.BlockSpec((2, s, 1), lambda j, i: (j, 0, 0))],
        out_specs=[pair, pair, pair],
        out_shape=[_sds((nh, s, LANE))] * 3,
        scratch_shapes=[pltpu.VMEM((2, s, LANE), MXU), pltpu.VMEM((2, s, 1), F32)],
        compiler_params=_params(("parallel", "arbitrary")),
    )


def _ssd_bwd(xbc, proj, sc, states, dy, dep=None):
    s = xbc.shape[0]
    nc = s // SSD_CHUNK
    l = SSD_CHUNK

    def body(xbc_ref, tail_ref, sc_ref, st_ref, dy_ref, dxbc_ref, dtail_ref, dsc_ref, dstate):
        @pl.when(pl.program_id(0) == 0)
        def _():
            dstate[...] = jnp.zeros_like(dstate)
            dsc_ref[...] = jnp.zeros_like(dsc_ref)

        sc_v = sc_ref[...]
        lane, row, tri, a_row, pre, dt, a_cs, a_t = _ssd_chunk_common(tail_ref[...], sc_v)
        lane1 = _iota((1, LANE), 1)
        rowp = _iota((LANE, 1), 0)
        rowl = _iota((l, 1), 0)
        d_row = sc_v[2:3, :]
        da_col = jnp.zeros((l, LANE), F32)
        da_row = jnp.zeros((LANE, l), F32)
        dt_x = jnp.zeros((l, LANE), F32)
        dd_row = jnp.zeros((1, LANE), F32)
        db = [jnp.zeros((l, LANE), F32), jnp.zeros((l, LANE), F32)]
        dc = [jnp.zeros((l, LANE), F32), jnp.zeros((l, LANE), F32)]
        for j in range(3):
            xpair = xbc_ref[:, LANE * j:LANE * (j + 1)]
            dypair = dy_ref[:, LANE * j:LANE * (j + 1)]
            sp = st_ref[0, j]
            dsp = dstate[j]
            dxpair = jnp.zeros((l, LANE), F32)
            ds_new = jnp.zeros((LANE, LANE), F32)
            decay = jnp.zeros((LANE, 1), F32)
            for half in range(2):
                h = 2 * j + half
                g = h // 3
                hm = (lane < 64) if half == 0 else (lane >= 64)
                hrow = (rowp < 64) if half == 0 else (rowp >= 64)
                ac = _pick_col(a_cs, lane, DT_LANE + h)
                ar = _pick_row(a_t, row, DT_LANE + h)
                dtc = _pick_col(dt, lane, DT_LANE + h)
                alast = jnp.sum(jnp.where(lane1 == l - 1, ar, 0.0), axis=1, keepdims=True)
                dh = jnp.sum(jnp.where(lane1 == DT_LANE + h, d_row, 0.0), axis=1, keepdims=True)
                xm = jnp.where(hm, xpair, 0.0)
                xd = xm * dtc
                dym = jnp.where(hm, dypair, 0.0)
                bm = xbc_ref[:, D_SSD + LANE * g:D_SSD + LANE * (g + 1)]
                cm = xbc_ref[:, D_SSD + SSD_BC + LANE * g:D_SSD + SSD_BC + LANE * (g + 1)]
                lm = jnp.where(row >= lane, jnp.exp(jnp.minimum(ac - ar, 0.0)), 0.0)
                e_in = jnp.exp(ac)
                f_out = jnp.exp(alast - ac)
                e_last = jnp.exp(alast)
                m = _dot_nt(cm, bm) * lm
                y_off = jnp.where(hm, _dot_nt(cm, sp), 0.0) * e_in
                dm = _dot_nt(dym, xd)
                dxd = _dot_tn(m, dym)
                dg = dm * lm
                dye = dym * e_in
                dc[g] = dc[g] + _dot(dg, bm) + _dot(dye, sp)
                db[g] = db[g] + _dot_tn(dg, cm)
                qm = dm * m
                dac = jnp.sum(qm, axis=1, keepdims=True) + jnp.sum(dym * y_off, axis=1, keepdims=True)
                dar = -jnp.sum(qm, axis=0, keepdims=True)
                dxf = jnp.where(hm, _dot_nt(bm, dsp), 0.0)
                db[g] = db[g] + _dot(xd * f_out, dsp)
                dxd = dxd + dxf * f_out
                df = jnp.sum(dxf * xd, axis=1, keepdims=True) * f_out
                dac = dac - df
                s_last = jnp.sum(df, axis=0, keepdims=True)
                ss = jnp.sum(jnp.where(hrow, dsp * sp, 0.0), axis=1, keepdims=True)
                s_last = s_last + e_last * jnp.sum(ss, axis=0, keepdims=True)
                dac = dac + jnp.where(rowl == l - 1, s_last, 0.0)
                ds_new = ds_new + _dot_tn(dye, cm)
                decay = jnp.where(hrow, e_last, decay)
                dxpair = dxpair + dxd * dtc + dym * dh
                dt_x = dt_x + jnp.where(lane == DT_LANE + h, jnp.sum(dxd * xm, axis=1, keepdims=True), 0.0)
                dsum = jnp.sum(jnp.sum(dym * xm, axis=1, keepdims=True), axis=0, keepdims=True)
                dd_row = dd_row + jnp.where(lane1 == DT_LANE + h, dsum, 0.0)
                da_col = da_col + jnp.where(lane == DT_LANE + h, dac, 0.0)
                da_row = da_row + jnp.where(row == DT_LANE + h, dar, 0.0)
            dstate[j] = dsp * decay + ds_new
            dxbc_ref[:, LANE * j:LANE * (j + 1)] = dxpair
        for g in range(2):
            dxbc_ref[:, D_SSD + LANE * g:D_SSD + LANE * (g + 1)] = db[g]
            dxbc_ref[:, D_SSD + SSD_BC + LANE * g:D_SSD + SSD_BC + LANE * (g + 1)] = dc[g]
        dla = _dot_hi_tn(tri, da_col + da_row.T)
        ddt = dt_x + dla * a_row
        dpre = ddt * _sigmoid(pre)
        dtm = (lane >= DT_LANE) & (lane < DT_LANE + SSD_HEADS)
        dtail_ref[...] = jnp.where(dtm, dpre, 0.0)
        dtm1 = (lane1 >= DT_LANE) & (lane1 < DT_LANE + SSD_HEADS)
        dsc_ref[0:1, :] += jnp.where(dtm1, jnp.sum(dpre, axis=0, keepdims=True), 0.0)
        dsc_ref[1:2, :] += jnp.where(dtm1, jnp.sum(dla * dt, axis=0, keepdims=True) * a_row, 0.0)
        dsc_ref[2:3, :] += dd_row

    rev = lambda c: nc - 1 - c
    return _call_after(
        dep, body, (xbc, proj, sc, states, dy), name="ssd_bwd", grid=(nc,),
        in_specs=[pl.BlockSpec((l, N_XBC), lambda c: (rev(c), 0)),
                  pl.BlockSpec((l, LANE), lambda c: (rev(c), O_TAIL // LANE)), _full((8, LANE)),
                  pl.BlockSpec((1, 3, LANE, LANE), lambda c: (rev(c), 0, 0, 0)),
                  pl.BlockSpec((l, D_SSD), lambda c: (rev(c), 0))],
        out_specs=[pl.BlockSpec((l, N_XBC), lambda c: (rev(c), 0)), pl.BlockSpec((l, LANE), lambda c: (rev(c), 0)),
                   _full((8, LANE))],
        out_shape=[_sds((s, N_XBC)), _sds((s, LANE)), _sds((8, LANE))],
        scratch_shapes=[pltpu.VMEM((3, LANE, LANE), F32)],
        compiler_params=_params(("arbitrary",)),
    )


def _sconv_bwd(proj, w, b, dxbc, dep=None):
    s = proj.shape[0]

    def body(u_ref, w_ref, b_ref, d_ref, du_ref, dw_ref, db_ref):
        u = u_ref[...]
        wv = w_ref[...]
        dpre = d_ref[...] * _dsilu(_sconv_pre(u, wv, b_ref[...]))
        du_ref[...] = (wv[3:4, :] * dpre + wv[2:3, :] * _shift_up(dpre, 1) + wv[1:2, :] * _shift_up(dpre, 2)
                       + wv[0:1, :] * _shift_up(dpre, 3))
        for k in range(4):
            dw_ref[k:k + 1, :] = jnp.sum(dpre * _shift_down(u, 3 - k), axis=0, keepdims=True)
        db_ref[...] = jnp.sum(dpre, axis=0, keepdims=True)

    blk = pl.BlockSpec((s, LANE), lambda j: (0, j))
    return _call_after(
        dep, body, (proj, w, b, dxbc), name="sconv_bwd", grid=(N_XBC // LANE,),
        in_specs=[_col(s, O_XBC), pl.BlockSpec((4, LANE), lambda j: (0, j)), pl.BlockSpec((1, LANE), lambda j: (0, j)), blk],
        out_specs=[blk, pl.BlockSpec((4, LANE), lambda j: (0, j)), pl.BlockSpec((1, LANE), lambda j: (0, j))],
        out_shape=[_sds((s, N_XBC)), _sds((4, N_XBC)), _sds((1, N_XBC))],
        compiler_params=_params(("parallel",)),
    )


def _conva_bwd(proj, w, dya, dep=None):
    s = proj.shape[0]

    def body(h_ref, b_ref, c_ref, z_ref, w_ref, d_ref, da_ref, dw_ref):
        ah, ab, acv, az = h_ref[...], b_ref[...], c_ref[...], z_ref[...]
        wv = w_ref[...]
        u = acv * ah
        cv = wv[2:3, :] * u + wv[1:2, :] * _shift_down(u, 1) + wv[0:1, :] * _shift_down(u, 2)
        dy = d_ref[...]
        sz = _silu(az)
        da_ref[1] = dy * cv * sz
        da_ref[3] = dy * ab * cv * _dsilu(az)
        dcv = dy * ab * sz
        du = wv[2:3, :] * dcv + wv[1:2, :] * _shift_up(dcv, 1) + wv[0:1, :] * _shift_up(dcv, 2)
        da_ref[0] = du * acv
        da_ref[2] = du * ah
        for k in range(3):
            dw_ref[k:k + 1, :] = jnp.sum(dcv * _shift_down(u, 2 - k), axis=0, keepdims=True)

    return _call_after(
        dep, body, (proj, proj, proj, proj, w, dya), name="conva_bwd", grid=(D_CONV_A // LANE,),
        in_specs=[_col(s, O_AH), _col(s, O_AB), _col(s, O_AC), _col(s, O_AZ), pl.BlockSpec((3, LANE), lambda j: (0, j)),
                  pl.BlockSpec((s, LANE), lambda j: (0, j))],
        out_specs=[pl.BlockSpec((4, s, LANE), lambda j: (0, 0, j)), pl.BlockSpec((3, LANE), lambda j: (0, j))],
        out_shape=[_sds((4, s, D_CONV_A)), _sds((3, D_CONV_A))],
        compiler_params=_params(("parallel",)),
    )


def _mla_prep_bwd(dq, dk, dv, proj, qn, kvn, rq, rkv, gq, gkv, wq, wkv, cos, sin):
    s = proj.shape[0]
    ts = _tile(s)
    nh = MLA_HEADS

    def body(dq_ref, dk_ref, dv_ref, cqa_ref, ckv_ref, qn_ref, kvn_ref, rq_ref, rkv_ref, gq_ref, gkv_ref,
             wq_ref, wkv_ref, cos_ref, sin_ref, dcqa_ref, dckv_ref, dtail_ref, dwq_ref, dwkv_ref, dgq_ref, dgkv_ref):
        @pl.when(pl.program_id(0) == 0)
        def _():
            dwq_ref[...] = jnp.zeros_like(dwq_ref)
            dwkv_ref[...] = jnp.zeros_like(dwkv_ref)
            dgq_ref[...] = jnp.zeros_like(dgq_ref)
            dgkv_ref[...] = jnp.zeros_like(dgkv_ref)

        cosv = cos_ref[...]
        sinv = sin_ref[...]
        lane = _iota((ts, LANE), 1)
        rope_lanes = (lane >= ROPE_LANE) & (lane < ROPE_LANE + QK_ROPE)

        def unrope(gr):
            return gr * cosv + _rope_swap(gr * sinv)

        dqs, dks, dvs = [], [], []
        dkr = jnp.zeros((ts, LANE), F32)
        for h in range(nh):
            dqs.append(unrope(dq_ref[h] * ATT_SCALE).astype(MXU))
            dkh = dk_ref[h]
            dks.append(jnp.where(lane < QK_NOPE, dkh, 0.0).astype(MXU))
            dkr = dkr + jnp.where(rope_lanes, dkh, 0.0)
            dvs.append(dv_ref[h].astype(MXU))
        dtail_ref[...] = pltpu.roll(jnp.where(rope_lanes, unrope(dkr), 0.0), ROPE_LANE, 1)
        dq_all = jnp.concatenate(dqs, axis=1)
        dkv_all = jnp.concatenate(dks + dvs, axis=1)
        dwq_ref[...] += _dot_tn(dq_all, qn_ref[...])
        dwkv_ref[...] += _dot_tn(dkv_all, kvn_ref[...])
        dcqa, dgq = _rms_bwd(_dot(dq_all, wq_ref[...]), cqa_ref[...], rq_ref[...], gq_ref[...])
        dckv, dgkv = _rms_bwd(_dot(dkv_all, wkv_ref[...]), ckv_ref[...], rkv_ref[...], gkv_ref[...])
        dcqa_ref[...] = dcqa
        dckv_ref[...] = dckv
        dgq_ref[...] += dgq
        dgkv_ref[...] += dgkv

    head = pl.BlockSpec((nh, ts, LANE), lambda i: (0, i, 0))
    return pl.pallas_call(
        body, name="mla_prep_bwd", grid=(s // ts,),
        in_specs=[head, head, head,
                  pl.BlockSpec((ts, Q_LORA), lambda i: (i, O_CQA // Q_LORA)),
                  pl.BlockSpec((ts, KV_LORA), lambda i: (i, O_CKV // KV_LORA)),
                  _row(ts, Q_LORA), _row(ts, KV_LORA), _row(ts, 1), _row(ts, 1),
                  _full((1, Q_LORA)), _full((1, KV_LORA)), _full((nh * LANE, Q_LORA)), _full((2 * nh * LANE, KV_LORA)),
                  _row(ts, LANE), _row(ts, LANE)],
        out_specs=[_row(ts, Q_LORA), _row(ts, KV_LORA), _row(ts, LANE), _full((nh * LANE, Q_LORA)),
                   _full((2 * nh * LANE, KV_LORA)), _full((1, Q_LORA)), _full((1, KV_LORA))],
        out_shape=[_sds((s, Q_LORA)), _sds((s, KV_LORA)), _sds((s, LANE)), _sds((nh * LANE, Q_LORA)),
                   _sds((2 * nh * LANE, KV_LORA)), _sds((1, Q_LORA)), _sds((1, KV_LORA))],
        compiler_params=_params(("arbitrary",)),
    )(dq, dk, dv, proj, proj, qn, kvn, rq, rkv, gq, gkv, wq, wkv, cos, sin)


def _inproj_bwd(da4, dsz, dxbc_in, dcqa, dckv, dcz, dtail_a, dtail_b, w, x, rstd, g, dout):
    s = x.shape[0]
    ts = _tile(s)

    def body(da_ref, dsz_ref, dxbc_ref, dcqa_ref, dckv_ref, dcz_ref, dta_ref, dtb_ref, w_ref, x_ref, r_ref, g_ref, dout_ref,
             dproj_ref, dx_ref, dg_ref):
        @pl.when(pl.program_id(0) == 0)
        def _():
            dg_ref[...] = jnp.zeros_like(dg_ref)

        dproj = jnp.concatenate(
            [da_ref[0], da_ref[1], da_ref[2], da_ref[3], dxbc_ref[...], dsz_ref[...], dcqa_ref[...], dckv_ref[...],
             dcz_ref[...], dta_ref[...] + dtb_ref[...]], axis=1).astype(MXU)
        dproj_ref[...] = dproj
        dh = _dot_nt(dproj, w_ref[...])
        dx, dg = _rms_bwd(dh, x_ref[...], r_ref[...], g_ref[...])
        dx_ref[...] = dout_ref[...] + dx
        dg_ref[...] += dg

    return pl.pallas_call(
        body, name="inproj_bwd", grid=(s // ts,),
        in_specs=[pl.BlockSpec((4, ts, D_CONV_A), lambda i: (0, i, 0)), _row(ts, D_SSD), _row(ts, N_XBC), _row(ts, Q_LORA),
                  _row(ts, KV_LORA), _row(ts, D_MLA), _row(ts, LANE), _row(ts, LANE), _full((D_MODEL, NCOL)),
                  _row(ts, D_MODEL), _row(ts, 1), _full((1, D_MODEL)), _row(ts, D_MODEL)],
        out_specs=[_row(ts, NCOL), _row(ts, D_MODEL), _full((1, D_MODEL))],
        out_shape=[_sds((s, NCOL), MXU), _sds((s, D_MODEL)), _sds((1, D_MODEL))],
        compiler_params=_params(("arbitrary",)),
    )(da4, dsz, dxbc_in, dcqa, dckv, dcz, dtail_a, dtail_b, w, x, rstd, g, dout)


DWIN_BLOCK = 640


def _dwin(h, dproj):
    s = h.shape[0]

    def body(h_ref, d_ref, o_ref):
        o_ref[...] = _dot_tn(h_ref[...], d_ref[...])

    return pl.pallas_call(
        body, name="dwin", grid=(NCOL // DWIN_BLOCK,),
        in_specs=[_full((s, D_MODEL)), pl.BlockSpec((s, DWIN_BLOCK), lambda j: (0, j))],
        out_specs=pl.BlockSpec((D_MODEL, DWIN_BLOCK), lambda j: (0, j)),
        out_shape=_sds((D_MODEL, NCOL)),
        compiler_params=_params(("parallel",)),
    )(h, dproj)


def _adamw(w, g, m, v):
    bc1 = 1.0 - ADAM_B1 ** ADAM_STEP
    bc2 = 1.0 - ADAM_B2 ** ADAM_STEP

    def body(w_ref, g_ref, m_ref, v_ref, d_ref, mo_ref, vo_ref):
        gv = g_ref[...]
        mn = ADAM_B1 * m_ref[...] + (1.0 - ADAM_B1) * gv
        vn = ADAM_B2 * v_ref[...] + (1.0 - ADAM_B2) * (gv * gv)
        mo_ref[...] = mn
        vo_ref[...] = vn
        d_ref[...] = -ADAM_LR * ((mn / bc1) / (jnp.sqrt(vn / bc2) + ADAM_EPS) + ADAM_WD * w_ref[...])

    if w.ndim == 2:
        grid, blk = (1,), pl.BlockSpec(w.shape, lambda i: (0, 0))
    else:
        grid, blk = (w.shape[0],), pl.BlockSpec((1,) + w.shape[1:], lambda i: (i, 0, 0))
    return pl.pallas_call(
        body, name="adamw", grid=grid,
        in_specs=[blk] * 4, out_specs=[blk] * 3, out_shape=[_sds(w.shape)] * 3,
        compiler_params=_params(("parallel",)),
    )(w, g, m, v)


COL_MOVES = ((0, 0, 1024), (1024, O_SZ, 384), (1408, O_XBC, 896), (2304, O_TAIL + DT_LANE, 6), (2310, O_CQA, 256),
             (2566, O_CKV, 128), (2694, O_TAIL, 32), (2726, O_CZ, 384))


def _move_cols(w, moves, width):
    out = None
    for src, dst, n in moves:
        piece = jnp.pad(w[..., src:src + n], [(0, 0)] * (w.ndim - 1) + [(dst, width - dst - n)])
        out = piece if out is None else out + piece
    return out


def _perm_cols(w):
    return _move_cols(w, COL_MOVES, NCOL)


def _unperm_cols(g):
    return _move_cols(g, [(dst, src, n) for src, dst, n in COL_MOVES], IN_COLS)


def _wq_layout(wt):
    return jnp.pad(wt.reshape(MLA_HEADS, QK_NOPE + QK_ROPE, Q_LORA), ((0, 0), (0, 32), (0, 0))).reshape(MLA_HEADS * LANE, Q_LORA)


def _wq_unlayout(g):
    return g.reshape(MLA_HEADS, LANE, Q_LORA)[:, :QK_NOPE + QK_ROPE].reshape(MLA_HEADS * (QK_NOPE + QK_ROPE), Q_LORA)


def _wkv_layout(wt):
    t = wt.reshape(MLA_HEADS, 2, 64, KV_LORA).transpose(1, 0, 2, 3)
    return jnp.pad(t, ((0, 0), (0, 0), (0, 64), (0, 0))).reshape(2 * MLA_HEADS * LANE, KV_LORA)


def _wkv_unlayout(g):
    t = g.reshape(2, MLA_HEADS, LANE, KV_LORA)[:, :, :64]
    return t.transpose(1, 0, 2, 3).reshape(MLA_HEADS * LANE, KV_LORA)


def _rope_tables(positions):
    inv_freq = ROPE_BASE ** (-jnp.arange(0, QK_ROPE, 2, dtype=F32) / QK_ROPE)
    ang = positions.astype(F32)[:, None] * inv_freq
    cos, sin = jnp.cos(ang), jnp.sin(ang)
    s = positions.shape[0]
    one, zero = jnp.ones((s, ROPE_LANE), F32), jnp.zeros((s, ROPE_LANE), F32)
    cos_t = jnp.concatenate([one, cos, cos, one[:, :32]], axis=1)
    sin_t = jnp.concatenate([zero, -sin, sin, zero[:, :32]], axis=1)
    return cos_t, sin_t


def _ssd_scalars(dt_bias, a_log, d_skip):
    return jnp.pad(jnp.stack([dt_bias, a_log, d_skip]), ((0, 5), (DT_LANE, LANE - DT_LANE - SSD_HEADS)))


def _layer_fwd(x, lw, cos, sin, dep=None, late=None):
    proj, h, rstd = _inproj_fwd(x, lw["norm_g"], lw["w_in"], dep)
    ya = _conva_fwd(proj, lw["conv_a_w"])
    xbc = _sconv_fwd(proj, lw["ssd_conv_w"], lw["ssd_conv_b"])
    y_ssd, states = _ssd_fwd(xbc, proj, lw["sc"])
    if late is not None:
        lw = {**lw, **late(ya, y_ssd)}
    q, k, v, qn, kvn, rq, rkv = _mla_prep_fwd(proj, lw["gq"], lw["gkv"], lw["wq"], lw["wkv"], cos, sin)
    o, lse = _attn_fwd(q, k, v)
    x_out, y = _outproj_fwd(x, proj, ya, y_ssd, o, lw["g_ssd"], lw["w_out"])
    saved = dict(x=x, proj=proj, h=h, rstd=rstd, xbc=xbc, y_ssd=y_ssd, states=states, q=q, k=k, v=v, qn=qn, kvn=kvn,
                 rq=rq, rkv=rkv, o=o, lse=lse, y=y)
    return x_out, saved, lw


def _layer_bwd(dout, lw, sv, cos, sin, rs=None):
    tok = lambda: None if rs is None else rs["h"]["token"]
    dya, dys, dsz, d_o, dcz, dg_ssd, dw_out = _outproj_bwd(dout, sv["y"], lw["w_out"], sv["proj"], sv["y_ssd"], sv["o"],
                                                            lw["g_ssd"], tok())
    if rs is not None:
        rs = _rs_add_mine(rs, [dya])
    dq, dk, dv = _attn_bwd(sv["q"], sv["k"], sv["v"], sv["o"], d_o, sv["lse"], tok())
    dxbc, dtail_s, dsc = _ssd_bwd(sv["xbc"], sv["proj"], lw["sc"], sv["states"], dys, tok())
    da4, dw_conva = _conva_bwd(sv["proj"], lw["conv_a_w"], dya, tok())
    if rs is not None:
        rs = _rs_add_chips(rs, [dq, dxbc, da4])
    du, dw_sconv, db_sconv = _sconv_bwd(sv["proj"], lw["ssd_conv_w"], lw["ssd_conv_b"], dxbc, tok())
    dcqa, dckv, dtail_m, dwq, dwkv, dgq, dgkv = _mla_prep_bwd(
        dq, dk, dv, sv["proj"], sv["qn"], sv["kvn"], sv["rq"], sv["rkv"], lw["gq"], lw["gkv"], lw["wq"], lw["wkv"], cos, sin)
    dproj, dx, dg = _inproj_bwd(da4, dsz, du, dcqa, dckv, dcz, dtail_s, dtail_m, lw["w_in"], sv["x"], sv["rstd"],
                                lw["norm_g"], dout)
    reduced = None if rs is None else _rs_end(rs, [du, dcqa, dx])
    dw_in = _dwin(sv["h"], dproj)
    grads = dict(norm_g=dg, w_in=dw_in, conv_a_w=dw_conva, ssd_conv_w=dw_sconv, ssd_conv_b=db_sconv, sc=dsc,
                 g_ssd=dg_ssd, gq=dgq, wq=dwq, gkv=dgkv, wkv=dwkv, w_out=dw_out)
    return dx, grads, reduced


ANY = pl.BlockSpec(memory_space=pl.ANY)
N_CHIPS = 4
N_DEV = 8


def _place():
    return lax.axis_index("x"), lax.axis_index("y"), lax.axis_index("c")


HBM_SPEC = pl.BlockSpec(memory_space=pltpu.HBM)
SEM_SPEC = pl.BlockSpec(memory_space=pltpu.SEMAPHORE)
PAYLOAD = jnp.bfloat16


def _hbm(a):
    return pltpu.with_memory_space_constraint(a, pltpu.HBM)


def _run_plan(plan, srcs, lands, send_sems, recv_sems, start, wait):
    copies = plan(srcs, lands)
    if start:
        for i, (src, dst, _, to) in enumerate(copies):
            pltpu.make_async_remote_copy(src_ref=src, dst_ref=dst, send_sem=send_sems.at[i], recv_sem=recv_sems.at[i],
                                         device_id=to, device_id_type=MESH_T).start()
    if wait:
        for i, (src, _, arrives, to) in enumerate(copies):
            cp = pltpu.make_async_remote_copy(src_ref=src, dst_ref=arrives, send_sem=send_sems.at[i],
                                              recv_sem=recv_sems.at[i], device_id=to, device_id_type=MESH_T)
            cp.wait_send()
            cp.wait_recv()


def _exchange_fused(name, plan, n_copies, srcs, land_shapes):
    ns, nl = len(srcs), len(land_shapes)

    def body(*refs):
        _run_plan(plan, refs[:ns], refs[ns:ns + nl], refs[ns + nl], refs[ns + nl + 1], True, True)

    return pl.pallas_call(
        body, name=name, in_specs=[ANY] * ns, out_specs=[ANY] * nl, out_shape=list(land_shapes),
        scratch_shapes=[pltpu.SemaphoreType.DMA((n_copies,)), pltpu.SemaphoreType.DMA((n_copies,))],
    )(*srcs)


def _exchange_start(name, plan, n_copies, srcs, land_shapes, deps):
    ns, nl = len(srcs), len(land_shapes)
    n_in = ns + nl + len(deps)

    def body(*refs):
        send_sems, recv_sems = refs[n_in], refs[n_in + 1]
        token = refs[-1]
        _run_plan(plan, refs[:ns], refs[ns:ns + nl], send_sems, recv_sems, True, False)
        token[...] = jnp.zeros_like(token)

    thru = [pltpu.HBM(a.shape, a.dtype) for a in srcs] + [pltpu.HBM(a.shape, a.dtype) for a in land_shapes]
    outs = pl.pallas_call(
        body, name=name,
        out_shape=(pltpu.SemaphoreType.DMA((n_copies,)), pltpu.SemaphoreType.DMA((n_copies,)), *thru, _sds((8, LANE))),
        in_specs=[HBM_SPEC] * (ns + nl) + [ANY] * len(deps),
        out_specs=(SEM_SPEC, SEM_SPEC, *[HBM_SPEC] * (ns + nl), pl.BlockSpec(memory_space=pltpu.VMEM)),
        input_output_aliases={i: 2 + i for i in range(ns + nl)},
        compiler_params=pltpu.CompilerParams(has_side_effects=pltpu.SideEffectType.DATAFLOW_SIDE_EFFECTING),
    )(*[_hbm(a) for a in srcs], *[_hbm(lax.empty(a.shape, a.dtype)) for a in land_shapes], *deps)
    return (outs[0], outs[1]), list(outs[2:2 + ns]), list(outs[2 + ns:2 + ns + nl]), outs[-1]


def _exchange_wait(name, plan, sems, srcs, lands, after):
    ns, nl = len(srcs), len(lands)

    def body(*refs):
        _run_plan(plan, refs[:ns], refs[ns:ns + nl], refs[ns + nl], refs[ns + nl + 1], False, True)

    outs = pl.pallas_call(
        body, name=name,
        out_shape=[pltpu.HBM(a.shape, a.dtype) for a in list(srcs) + list(lands)],
        in_specs=[HBM_SPEC] * (ns + nl) + [SEM_SPEC, SEM_SPEC] + [ANY] * len(after), out_specs=[HBM_SPEC] * (ns + nl),
        input_output_aliases={i: i for i in range(ns + nl)},
        compiler_params=pltpu.CompilerParams(has_side_effects=pltpu.SideEffectType.DATAFLOW_SIDE_EFFECTING),
    )(*srcs, *lands, sems[0], sems[1], *after)
    return list(outs[:ns]), list(outs[ns:])


def _xchg_begin(name, plan, n_copies, srcs, land_shapes, split, deps=()):
    if not split:
        return dict(split=False, srcs=list(srcs), lands=_exchange_fused(name, plan, n_copies, srcs, land_shapes),
                    token=jnp.zeros((8, LANE), F32))
    sems, srcs_t, lands_t, token = _exchange_start(name + "_start", plan, n_copies, srcs, land_shapes, list(deps))
    return dict(split=True, name=name, plan=plan, sems=sems, srcs=srcs_t, lands=lands_t, token=token)


def _xchg_end(h, after):
    if not h["split"]:
        return h["srcs"], h["lands"]
    return _exchange_wait(h["name"] + "_wait", h["plan"], h["sems"], h["srcs"], h["lands"], after)


def _other_chips():
    x, y, c = _place()
    return [(1 - x, y), (x, 1 - y), (1 - x, 1 - y)]


def _gather_plan(srcs, lands):
    x, y, c = _place()
    me = 2 * x + y
    return [(srcs[a], lands[a].at[me], lands[a].at[2 * cx + cy], (cx, cy, c))
            for (cx, cy) in _other_chips() for a in range(len(srcs))]


def _gather_begin(shards, split, tag, deps=()):
    shapes = [_sds((N_CHIPS,) + a.shape, a.dtype) for a in shards]
    return _xchg_begin(f"gather_{tag}", _gather_plan, 3 * len(shards), shards, shapes, split, deps)


def _gather_end(h, after):
    shards, lands = _xchg_end(h, after)
    me = 2 * lax.axis_index("x") + lax.axis_index("y")
    return [lax.dynamic_update_index_in_dim(g, s, me, 0) for g, s in zip(lands, shards)]


def _swap_plan(srcs, lands):
    x, y, c = _place()
    return [(srcs[a].at[:, 1 - c], lands[a], lands[a], (x, y, 1 - c)) for a in range(len(srcs))]


def _chips_plan(srcs, lands):
    x, y, c = _place()
    me = 2 * x + y
    return [(srcs[a].at[2 * cx + cy], lands[a].at[me], lands[a].at[2 * cx + cy], (cx, cy, c))
            for (cx, cy) in _other_chips() for a in range(len(srcs))]


def _share_plan(srcs, lands):
    x, y, c = _place()
    return [(srcs[a], lands[a].at[c], lands[a].at[1 - c], (x, y, 1 - c)) for a in range(len(srcs))]


def _allreduce_small(slab, dep=None):
    r = slab.shape[0]

    def body(s_ref, o_ref, gath, send_sems, recv_sems):
        x, y, c = _place()
        me = 4 * x + 2 * y + c
        gath[me] = s_ref[...]
        cps = []
        for rel in range(1, N_DEV):
            px = 1 - x if rel & 4 else x
            py = 1 - y if rel & 2 else y
            pc = 1 - c if rel & 1 else c
            cp = pltpu.make_async_remote_copy(src_ref=s_ref, dst_ref=gath.at[me], send_sem=send_sems.at[rel - 1],
                                              recv_sem=recv_sems.at[rel - 1], device_id=(px, py, pc), device_id_type=MESH_T)
            cp.start()
            cps.append(cp)
        for cp in cps:
            cp.wait()
        acc = gath[0]
        for d in range(1, N_DEV):
            acc = acc + gath[d]
        o_ref[...] = acc

    vm = pl.BlockSpec(memory_space=pltpu.VMEM)
    return _call_after(
        dep, body, (slab,), name="allreduce_small", in_specs=[vm], out_specs=vm, out_shape=_sds((r, LANE)),
        scratch_shapes=[pltpu.VMEM((N_DEV, r, LANE), F32), pltpu.SemaphoreType.DMA((N_DEV - 1,)),
                        pltpu.SemaphoreType.DMA((N_DEV - 1,))],
    )


def _add_mine(g4, recv, half):
    _, _, rh, c = g4.shape

    def body(h_ref, g_ref, r_ref, o_ref):
        o_ref[0] = (g_ref[0, 0] + r_ref[0]).astype(o_ref.dtype)

    return pl.pallas_call(
        body, name="add_mine",
        grid_spec=pltpu.PrefetchScalarGridSpec(
            num_scalar_prefetch=1, grid=(N_CHIPS,),
            in_specs=[pl.BlockSpec((1, 1, rh, c), lambda j, h: (j, h[0], 0, 0)), pl.BlockSpec((1, rh, c), lambda j, h: (j, 0, 0))],
            out_specs=pl.BlockSpec((1, rh, c), lambda j, h: (j, 0, 0))),
        out_shape=_sds((N_CHIPS, rh, c), PAYLOAD),
        compiler_params=_params(("parallel",)),
    )(half, g4, recv)


def _add_chips(e, p, me):
    _, rh, c = e.shape

    def body(m_ref, e_ref, p_ref, o_ref):
        own = p_ref[0].astype(F32)
        acc = None
        for s in range(N_CHIPS):
            t = jnp.where(m_ref[0] == s, own, e_ref[s].astype(F32))
            acc = t if acc is None else acc + t
        o_ref[...] = acc

    return pl.pallas_call(
        body, name="add_chips",
        grid_spec=pltpu.PrefetchScalarGridSpec(
            num_scalar_prefetch=1, grid=(1,),
            in_specs=[pl.BlockSpec((N_CHIPS, rh, c), lambda i, m: (0, 0, 0)), pl.BlockSpec((1, rh, c), lambda i, m: (m[0], 0, 0))],
            out_specs=pl.BlockSpec((rh, c), lambda i, m: (0, 0))),
        out_shape=_sds((rh, c)),
        compiler_params=_params(("arbitrary",)),
    )(me, e, p)


def _rs_begin(gs, split, tag):
    g4 = [g.reshape(N_CHIPS, 2, g.shape[0] // (2 * N_CHIPS), g.shape[1]) for g in gs]
    h = _xchg_begin(f"rs_swap_{tag}", _swap_plan, len(gs), g4, [_sds((N_CHIPS,) + g.shape[2:]) for g in g4], split)
    return dict(h=h, split=split, tag=tag, shapes=[g.shape for g in gs])


def _rs_add_mine(st, after):
    g4, recv = _xchg_end(st["h"], after)
    half = jnp.reshape(lax.axis_index("c"), (1,)).astype(jnp.int32)
    ps = [_add_mine(g, r, half) for g, r in zip(g4, recv)]
    st["h"] = _xchg_begin(f"rs_chips_{st['tag']}", _chips_plan, 3 * len(ps), ps, [_sds(p.shape, p.dtype) for p in ps], st["split"])
    return st


def _rs_add_chips(st, after):
    ps, es = _xchg_end(st["h"], after)
    me = jnp.reshape(2 * lax.axis_index("x") + lax.axis_index("y"), (1,)).astype(jnp.int32)
    fs = [_add_chips(e, p, me) for e, p in zip(es, ps)]
    st["h"] = _xchg_begin(f"rs_share_{st['tag']}", _share_plan, len(fs), fs, [_sds((2,) + f.shape) for f in fs], st["split"])
    return st


def _rs_end(st, after):
    fs, ss = _xchg_end(st["h"], after)
    c = lax.axis_index("c")
    return [lax.dynamic_update_index_in_dim(s, f, c, 0).reshape(shp[0] // N_CHIPS, shp[1])
            for s, f, shp in zip(ss, fs, st["shapes"])]


WEIGHTS = ["norm_g", "w_in", "conv_a_w", "ssd_conv_w", "ssd_conv_b", "ssd_dt_bias", "ssd_a_log", "ssd_d", "ssd_norm_g",
           "mla_q_norm_g", "w_qb", "mla_kv_norm_g", "w_kvb", "w_out", "final_norm_g"]
BIG = ["w_in", "w_qb", "w_kvb", "w_out"]
SLAB_ROWS = 128
SMALL_ROWS = 72


def _to_slab(parts, rows):
    flat = jnp.concatenate([p.reshape(-1) for p in parts])
    return jnp.pad(flat, (0, rows * LANE - flat.shape[0])).reshape(rows, LANE)


def _from_slab(slab, shapes):
    flat = slab.reshape(-1)
    out, off = [], 0
    for shp in shapes:
        n = int(np.prod(shp))
        out.append(flat[off:off + n].reshape(shp))
        off += n
    return out


def kernel(x, positions, norm_g, w_in, conv_a_w, ssd_conv_w, ssd_conv_b, ssd_dt_bias, ssd_a_log, ssd_d, ssd_norm_g, mla_q_norm_g, w_qb, mla_kv_norm_g, w_kvb, w_out, final_norm_g, loss_target, m_norm_g, m_w_in, m_conv_a_w, m_ssd_conv_w, m_ssd_conv_b, m_ssd_dt_bias, m_ssd_a_log, m_ssd_d, m_ssd_norm_g, m_mla_q_norm_g, m_w_qb, m_mla_kv_norm_g, m_w_kvb, m_w_out, m_final_norm_g, v_norm_g, v_w_in, v_conv_a_w, v_ssd_conv_w, v_ssd_conv_b, v_ssd_dt_bias, v_ssd_a_log, v_ssd_d, v_ssd_norm_g, v_mla_q_norm_g, v_w_qb, v_mla_kv_norm_g, v_w_kvb, v_w_out, v_final_norm_g):
    w = dict(norm_g=norm_g, w_in=w_in, conv_a_w=conv_a_w, ssd_conv_w=ssd_conv_w, ssd_conv_b=ssd_conv_b,
             ssd_dt_bias=ssd_dt_bias, ssd_a_log=ssd_a_log, ssd_d=ssd_d, ssd_norm_g=ssd_norm_g, mla_q_norm_g=mla_q_norm_g,
             w_qb=w_qb, mla_kv_norm_g=mla_kv_norm_g, w_kvb=w_kvb, w_out=w_out, final_norm_g=final_norm_g)
    mom = dict(norm_g=m_norm_g, w_in=m_w_in, conv_a_w=m_conv_a_w, ssd_conv_w=m_ssd_conv_w, ssd_conv_b=m_ssd_conv_b,
               ssd_dt_bias=m_ssd_dt_bias, ssd_a_log=m_ssd_a_log, ssd_d=m_ssd_d, ssd_norm_g=m_ssd_norm_g,
               mla_q_norm_g=m_mla_q_norm_g, w_qb=m_w_qb, mla_kv_norm_g=m_mla_kv_norm_g, w_kvb=m_w_kvb, w_out=m_w_out,
               final_norm_g=m_final_norm_g)
    var = dict(norm_g=v_norm_g, w_in=v_w_in, conv_a_w=v_conv_a_w, ssd_conv_w=v_ssd_conv_w, ssd_conv_b=v_ssd_conv_b,
               ssd_dt_bias=v_ssd_dt_bias, ssd_a_log=v_ssd_a_log, ssd_d=v_ssd_d, ssd_norm_g=v_ssd_norm_g,
               mla_q_norm_g=v_mla_q_norm_g, w_qb=v_w_qb, mla_kv_norm_g=v_mla_kv_norm_g, w_kvb=v_w_kvb, w_out=v_w_out,
               final_norm_g=v_final_norm_g)
    chip = 2 * lax.axis_index("x") + lax.axis_index("y")

    def early_shard(l, zero):
        pack = jnp.pad(conv_a_w[l], ((0, 5), (0, 192))) + jnp.pad(ssd_conv_w[l], ((3, 1), (0, 32)))
        return [(_perm_cols(w_in[l]) + zero).astype(MXU), pack + zero]

    def late_shard(l, zero):
        return [(w_out[l] + zero).astype(MXU), (w_qb[l].T + zero).astype(MXU), (w_kvb[l].T + zero).astype(MXU)]

    def early_weights(l, gathered):
        g_in, g_conv = gathered
        return dict(
            norm_g=norm_g[l][None], w_in=g_in.reshape(D_MODEL, NCOL),
            conv_a_w=jnp.concatenate([g_conv[j, 0:3, 0:64] for j in range(N_CHIPS)], axis=1),
            ssd_conv_w=jnp.concatenate([g_conv[j, 3:7, 0:224] for j in range(N_CHIPS)], axis=1),
            ssd_conv_b=ssd_conv_b[l][None], sc=_ssd_scalars(ssd_dt_bias[l], ssd_a_log[l], ssd_d[l]),
            g_ssd=ssd_norm_g[l][None], gq=mla_q_norm_g[l][None], gkv=mla_kv_norm_g[l][None])

    def late_weights(gathered):
        g_out, g_qb, g_kvb = gathered
        return dict(wq=_wq_layout(g_qb.reshape(MLA_HEADS * 96, Q_LORA)), wkv=_wkv_layout(g_kvb.reshape(MLA_HEADS * LANE, KV_LORA)),
                    w_out=g_out.reshape(D_MODEL, D_MODEL))

    def large_grads(g):
        wq = jnp.pad(_wq_unlayout(g["wq"]).reshape(N_CHIPS, 144, Q_LORA), ((0, 0), (0, 16), (0, 0)))
        return [g["w_in"], g["w_out"], wq.reshape(N_CHIPS * 160, Q_LORA), _wkv_unlayout(g["wkv"])]

    gather_a0 = _gather_begin(early_shard(0, 0.0), True, "a0")
    zero = gather_a0["token"][0, 0]
    cos, sin = _rope_tables(positions[0] + zero.astype(jnp.int32))
    late0, shards1 = late_shard(0, zero), early_shard(1, zero) + late_shard(1, zero)
    opt_in = {nm: [a[nm] + zero for a in (w, mom, var)] for nm in BIG}
    lw0 = early_weights(0, _gather_end(gather_a0, [cos, sin] + late0 + shards1 + [a for nm in BIG for a in opt_in[nm]]))
    gather_b0 = _gather_begin(late0, True, "b0")
    gather_1 = _gather_begin(shards1, True, "1", [gather_b0["token"]])
    x1, sv0, lw0 = _layer_fwd(x[0], lw0, cos, sin, gather_1["token"],
                              lambda ya, y_ssd: late_weights(_gather_end(gather_b0, [ya, y_ssd])))
    g1 = _gather_end(gather_1, [x1])
    x2, sv1, lw1 = _layer_fwd(x1, {**early_weights(1, g1[:2]), **late_weights(g1[2:])}, cos, sin)
    dx, dgf, loss = _loss_head(x2, final_norm_g[None], loss_target[0])

    dx, lg1, _ = _layer_bwd(dx, lw1, sv1, cos, sin)
    grad_x, lg0, red1 = _layer_bwd(dx, lw0, sv0, cos, sin, _rs_begin(large_grads(lg1), True, 1))
    rs0 = _rs_begin(large_grads(lg0), True, 0)
    lg = [lg0, lg1]
    grad = {}

    small_names = ["norm_g", "conv_a_w", "ssd_conv_w", "ssd_conv_b", "sc", "g_ssd", "gq", "gkv"]
    parts = [loss[0, 0:1], dgf]
    for l in range(DEPTH):
        parts += [lg[l][nm][:3, DT_LANE:DT_LANE + SSD_HEADS] if nm == "sc" else lg[l][nm] for nm in small_names]
    shapes = [(1,), (D_MODEL,)] + [(D_MODEL,), (3, D_CONV_A), (4, N_XBC), (N_XBC,), (3, SSD_HEADS), (D_SSD,), (Q_LORA,), (KV_LORA,)] * DEPTH
    red_slab = _allreduce_small(_to_slab(parts, SLAB_ROWS), rs0["h"]["token"])
    rs0 = _rs_add_mine(rs0, [red_slab])
    red = _from_slab(red_slab + rs0["h"]["token"][0, 0], shapes)
    loss_out = red[0][0]
    grad["final_norm_g"] = red[1]
    per = [red[2 + 8 * l:10 + 8 * l] for l in range(DEPTH)]
    grad["norm_g"] = jnp.stack([per[l][0] for l in range(DEPTH)])
    grad["conv_a_w"] = lax.dynamic_slice_in_dim(jnp.stack([per[l][1] for l in range(DEPTH)]), chip * 64, 64, axis=2)
    grad["ssd_conv_w"] = lax.dynamic_slice_in_dim(jnp.stack([per[l][2] for l in range(DEPTH)]), chip * 224, 224, axis=2)
    grad["ssd_conv_b"] = jnp.stack([per[l][3] for l in range(DEPTH)])
    grad["ssd_dt_bias"] = jnp.stack([per[l][4][0] for l in range(DEPTH)])
    grad["ssd_a_log"] = jnp.stack([per[l][4][1] for l in range(DEPTH)])
    grad["ssd_d"] = jnp.stack([per[l][4][2] for l in range(DEPTH)])
    grad["ssd_norm_g"] = jnp.stack([per[l][5] for l in range(DEPTH)])
    grad["mla_q_norm_g"] = jnp.stack([per[l][6] for l in range(DEPTH)])
    grad["mla_kv_norm_g"] = jnp.stack([per[l][7] for l in range(DEPTH)])

    delta, new_m, new_v = {}, {}, {}
    small = [nm for nm in WEIGHTS if nm not in BIG]
    sshapes = [w[nm].shape for nm in small]
    d, mo, vo = _adamw(_to_slab([w[nm] for nm in small], SMALL_ROWS), _to_slab([grad[nm] for nm in small], SMALL_ROWS),
                       _to_slab([mom[nm] for nm in small], SMALL_ROWS), _to_slab([var[nm] for nm in small], SMALL_ROWS))
    small_out = list(zip(small, _from_slab(d, sshapes), _from_slab(mo, sshapes), _from_slab(vo, sshapes)))
    for nm, dv, mv, vv in small_out:
        delta[nm], new_m[nm], new_v[nm] = dv, mv, vv

    red0 = _rs_end(_rs_add_chips(rs0, [a for row in small_out for a in row[1:]] + [grad[nm] for nm in small]), [])
    r_in, r_out, r_qb, r_kvb = [jnp.stack([a, b]) for a, b in zip(red0, red1)]
    grad.update(w_in=_unperm_cols(r_in), w_out=r_out, w_qb=jnp.swapaxes(r_qb[:, :144], 1, 2), w_kvb=jnp.swapaxes(r_kvb, 1, 2))
    for nm in BIG:
        delta[nm], new_m[nm], new_v[nm] = _adamw(opt_in[nm][0], grad[nm], opt_in[nm][1], opt_in[nm][2])

    return (loss_out, grad_x[None], *[grad[nm] for nm in WEIGHTS], *[delta[nm] for nm in WEIGHTS],
            *[new_m[nm] for nm in WEIGHTS], *[new_v[nm] for nm in WEIGHTS])
```

```python
import functools
import math

import numpy as np
import jax
import jax.numpy as jnp
from jax import lax
from jax.experimental import pallas as pl
from jax.experimental.pallas import tpu as pltpu

F32 = jnp.float32
MXU = jnp.bfloat16

D_MODEL = 1024
DEPTH = 2
D_CONV_A = 256
D_SSD = 384
SSD_HEADS = 6
SSD_BC = 256
SSD_CHUNK = 128
SSD_NORM_EPS = 1e-5
MLA_HEADS = 6
Q_LORA = 256
KV_LORA = 128
QK_NOPE = 64
QK_ROPE = 32
V_DIM = 64
D_MLA = 384
ROPE_BASE = 10000.0
NORM_EPS = 1e-6
IN_COLS = 3110
LANE = 128

O_AH, O_AB, O_AC, O_AZ = 0, 256, 512, 768
O_XBC = 1024
O_SZ = 1920
O_CQA = 2304
O_CKV = 2560
O_CZ = 2688
O_TAIL = 3072
NCOL = 3200
N_XBC = D_SSD + 2 * SSD_BC
DT_LANE = 32
ROPE_LANE = 64

ADAM_LR, ADAM_B1, ADAM_B2, ADAM_EPS, ADAM_WD, ADAM_STEP = 0.001, 0.9, 0.999, 1e-08, 0.01, 10

VMEM_LIMIT = 56 * 1024 * 1024
MESH_T = pl.DeviceIdType.MESH


def _dot(a, b):
    return jnp.dot(a.astype(MXU), b.astype(MXU), preferred_element_type=F32)


def _dot_nt(a, b):
    return lax.dot_general(a.astype(MXU), b.astype(MXU), (((1,), (1,)), ((), ())), preferred_element_type=F32)


def _dot_tn(a, b):
    return lax.dot_general(a.astype(MXU), b.astype(MXU), (((0,), (0,)), ((), ())), preferred_element_type=F32)


def _dot_hi(a, b):
    return jnp.dot(a, b, precision=lax.Precision.HIGHEST, preferred_element_type=F32)


def _dot_hi_tn(a, b):
    return lax.dot_general(a, b, (((0,), (0,)), ((), ())), precision=lax.Precision.HIGHEST, preferred_element_type=F32)


def _sigmoid(z):
    return 1.0 / (1.0 + jnp.exp(-z))


def _silu(z):
    return z * _sigmoid(z)


def _dsilu(z):
    s = _sigmoid(z)
    return s * (1.0 + z * (1.0 - s))


def _softplus(z):
    e = jnp.exp(-jnp.abs(z))
    return jnp.maximum(z, 0.0) + jnp.where(e < 1e-3, e * (1.0 - 0.5 * e), jnp.log(1.0 + e))


def _iota(shape, dim):
    return lax.broadcasted_iota(jnp.int32, shape, dim)


def _shift_down(u, k):
    if k == 0:
        return u
    return jnp.where(_iota(u.shape, 0) >= k, pltpu.roll(u, k, 0), 0.0)


def _shift_up(u, k):
    if k == 0:
        return u
    n = u.shape[0]
    return jnp.where(_iota(u.shape, 0) < n - k, pltpu.roll(u, n - k, 0), 0.0)


def _rope_swap(t):
    lane = _iota(t.shape, 1)
    lo = (lane >= ROPE_LANE) & (lane < ROPE_LANE + 16)
    hi = (lane >= ROPE_LANE + 16) & (lane < ROPE_LANE + 32)
    return jnp.where(lo, pltpu.roll(t, LANE - 16, 1), jnp.where(hi, pltpu.roll(t, 16, 1), 0.0))


def _params(sem=None):
    return pltpu.CompilerParams(dimension_semantics=sem, vmem_limit_bytes=VMEM_LIMIT)


def _full(shape):
    nd = len(shape)
    return pl.BlockSpec(shape, lambda *_: (0,) * nd)


def _sds(shape, dtype=F32):
    return jax.ShapeDtypeStruct(shape, dtype)


def _tile(s):
    return min(256, s)


def _row(ts, w):
    return pl.BlockSpec((ts, w), lambda i: (i, 0))


def _gate_cols(ts, off):
    return pl.BlockSpec((ts, D_SSD), lambda i, _o=off // D_SSD: (i, _o))


def _col(s, off):
    return pl.BlockSpec((s, LANE), lambda j, _o=off // LANE: (0, _o + j))


def _call_after(dep, body, args, *, in_specs, **kw):
    if dep is None:
        return pl.pallas_call(body, in_specs=in_specs, **kw)(*args)
    n = len(args)

    def body_dep(*refs):
        body(*refs[:n], *refs[n + 1:])

    return pl.pallas_call(body_dep, in_specs=list(in_specs) + [pl.BlockSpec(memory_space=pl.ANY)], **kw)(*args, dep)


def _rms(c, g):
    r = lax.rsqrt(jnp.mean(c * c, axis=-1, keepdims=True) + NORM_EPS)
    return c * r * g, r


def _rms_bwd(dn, c, r, g):
    ch = c * r
    dch = dn * g
    dc = r * (dch - ch * jnp.mean(dch * ch, axis=-1, keepdims=True))
    return dc, jnp.sum(dn * ch, axis=0, keepdims=True)


def _inproj_fwd(x, g, w, dep=None):
    s = x.shape[0]
    ts = _tile(s)

    def body(x_ref, g_ref, w_ref, proj_ref, h_ref, r_ref):
        hn, r = _rms(x_ref[...], g_ref[...])
        h = hn.astype(MXU)
        h_ref[...] = h
        r_ref[...] = r
        proj_ref[...] = jnp.dot(h, w_ref[...], preferred_element_type=F32)

    return _call_after(
        dep, body, (x, g, w), name="inproj_fwd", grid=(s // ts,),
        in_specs=[_row(ts, D_MODEL), _full((1, D_MODEL)), _full((D_MODEL, NCOL))],
        out_specs=[_row(ts, NCOL), _row(ts, D_MODEL), _row(ts, 1)],
        out_shape=[_sds((s, NCOL)), _sds((s, D_MODEL), MXU), _sds((s, 1))],
        compiler_params=_params(("parallel",)),
    )


def _conva_fwd(proj, w):
    s = proj.shape[0]

    def body(h_ref, b_ref, c_ref, z_ref, w_ref, y_ref):
        u = c_ref[...] * h_ref[...]
        wv = w_ref[...]
        cv = wv[2:3, :] * u + wv[1:2, :] * _shift_down(u, 1) + wv[0:1, :] * _shift_down(u, 2)
        y_ref[...] = b_ref[...] * cv * _silu(z_ref[...])

    return pl.pallas_call(
        body, name="conva_fwd", grid=(D_CONV_A // LANE,),
        in_specs=[_col(s, O_AH), _col(s, O_AB), _col(s, O_AC), _col(s, O_AZ), pl.BlockSpec((3, LANE), lambda j: (0, j))],
        out_specs=pl.BlockSpec((s, LANE), lambda j: (0, j)),
        out_shape=_sds((s, D_CONV_A)),
        compiler_params=_params(("parallel",)),
    )(proj, proj, proj, proj, w)


def _sconv_pre(u, wv, bv):
    return (wv[3:4, :] * u + wv[2:3, :] * _shift_down(u, 1) + wv[1:2, :] * _shift_down(u, 2)
            + wv[0:1, :] * _shift_down(u, 3) + bv)


def _sconv_fwd(proj, w, b):
    s = proj.shape[0]

    def body(u_ref, w_ref, b_ref, o_ref):
        o_ref[...] = _silu(_sconv_pre(u_ref[...], w_ref[...], b_ref[...]))

    return pl.pallas_call(
        body, name="sconv_fwd", grid=(N_XBC // LANE,),
        in_specs=[_col(s, O_XBC), pl.BlockSpec((4, LANE), lambda j: (0, j)), pl.BlockSpec((1, LANE), lambda j: (0, j))],
        out_specs=pl.BlockSpec((s, LANE), lambda j: (0, j)),
        out_shape=_sds((s, N_XBC)),
        compiler_params=_params(("parallel",)),
    )(proj, w, b)


def _ssd_chunk_common(tail, sc):
    l = SSD_CHUNK
    lane = _iota((l, LANE), 1)
    row = _iota((l, LANE), 0)
    tri = (row >= lane).astype(F32)
    a_row = -jnp.exp(sc[1:2, :])
    pre = tail + sc[0:1, :]
    dt = _softplus(pre)
    a_cs = _dot_hi(tri, dt * a_row)
    return lane, row, tri, a_row, pre, dt, a_cs, a_cs.T


def _pick_col(m, lane, k):
    return jnp.sum(jnp.where(lane == k, m, 0.0), axis=1, keepdims=True)


def _pick_row(m, row, k):
    return jnp.sum(jnp.where(row == k, m, 0.0), axis=0, keepdims=True)


def _ssd_fwd(xbc, proj, sc):
    s = xbc.shape[0]
    nc = s // SSD_CHUNK
    l = SSD_CHUNK

    def body(xbc_ref, tail_ref, sc_ref, y_ref, st_ref, state):
        @pl.when(pl.program_id(0) == 0)
        def _():
            state[...] = jnp.zeros_like(state)

        sc_v = sc_ref[...]
        lane, row, _, _, _, dt, a_cs, a_t = _ssd_chunk_common(tail_ref[...], sc_v)
        lane1 = _iota((1, LANE), 1)
        rowp = _iota((LANE, 1), 0)
        d_row = sc_v[2:3, :]
        for j in range(3):
            st_ref[0, j] = state[j]
        for j in range(3):
            xpair = xbc_ref[:, LANE * j:LANE * (j + 1)]
            sp = state[j]
            ypair = jnp.zeros((l, LANE), F32)
            new_s = jnp.zeros((LANE, LANE), F32)
            decay = jnp.zeros((LANE, 1), F32)
            for half in range(2):
                h = 2 * j + half
                g = h // 3
                hm = (lane < 64) if half == 0 else (lane >= 64)
                hrow = (rowp < 64) if half == 0 else (rowp >= 64)
                ac = _pick_col(a_cs, lane, DT_LANE + h)
                ar = _pick_row(a_t, row, DT_LANE + h)
                dtc = _pick_col(dt, lane, DT_LANE + h)
                alast = jnp.sum(jnp.where(lane1 == l - 1, ar, 0.0), axis=1, keepdims=True)
                dh = jnp.sum(jnp.where(lane1 == DT_LANE + h, d_row, 0.0), axis=1, keepdims=True)
                xm = jnp.where(hm, xpair, 0.0)
                xd = xm * dtc
                bm = xbc_ref[:, D_SSD + LANE * g:D_SSD + LANE * (g + 1)]
                cm = xbc_ref[:, D_SSD + SSD_BC + LANE * g:D_SSD + SSD_BC + LANE * (g + 1)]
                lm = jnp.where(row >= lane, jnp.exp(jnp.minimum(ac - ar, 0.0)), 0.0)
                y_diag = _dot(_dot_nt(cm, bm) * lm, xd)
                y_off = jnp.where(hm, _dot_nt(cm, sp), 0.0) * jnp.exp(ac)
                ypair = ypair + y_diag + y_off + xm * dh
                new_s = new_s + _dot_tn(xd * jnp.exp(alast - ac), bm)
                decay = jnp.where(hrow, jnp.exp(alast), decay)
            state[j] = sp * decay + new_s
            y_ref[:, LANE * j:LANE * (j + 1)] = ypair

    return pl.pallas_call(
        body, name="ssd_fwd", grid=(nc,),
        in_specs=[pl.BlockSpec((l, N_XBC), lambda c: (c, 0)),
                  pl.BlockSpec((l, LANE), lambda c: (c, O_TAIL // LANE)), _full((8, LANE))],
        out_specs=[pl.BlockSpec((l, D_SSD), lambda c: (c, 0)), pl.BlockSpec((1, 3, LANE, LANE), lambda c: (c, 0, 0, 0))],
        out_shape=[_sds((s, D_SSD)), _sds((nc, 3, LANE, LANE))],
        scratch_shapes=[pltpu.VMEM((3, LANE, LANE), F32)],
        compiler_params=_params(("arbitrary",)),
    )(xbc, proj, sc)


def _mla_prep_fwd(proj, gq, gkv, wq, wkv, cos, sin):
    s = proj.shape[0]
    ts = _tile(s)
    nh = MLA_HEADS

    def body(cqa_ref, ckv_ref, tail_ref, gq_ref, gkv_ref, wq_ref, wkv_ref, cos_ref, sin_ref,
             q_ref, k_ref, v_ref, qn_ref, kvn_ref, rq_ref, rkv_ref):
        qn, rq = _rms(cqa_ref[...], gq_ref[...])
        kvn, rkv = _rms(ckv_ref[...], gkv_ref[...])
        qn = qn.astype(MXU)
        kvn = kvn.astype(MXU)
        qn_ref[...] = qn
        kvn_ref[...] = kvn
        rq_ref[...] = rq
        rkv_ref[...] = rkv
        q = _dot_nt(qn, wq_ref[...])
        kv = _dot_nt(kvn, wkv_ref[...])
        cosv = cos_ref[...]
        sinv = sin_ref[...]
        lane = _iota((ts, LANE), 1)
        rope_lanes = (lane >= ROPE_LANE) & (lane < ROPE_LANE + QK_ROPE)
        kr = jnp.where(rope_lanes, pltpu.roll(tail_ref[...], ROPE_LANE, 1), 0.0)
        kr = kr * cosv + _rope_swap(kr) * sinv
        for h in range(nh):
            qh = q[:, LANE * h:LANE * (h + 1)]
            q_ref[h] = ((qh * cosv + _rope_swap(qh) * sinv) * ATT_SCALE).astype(MXU)
            k_ref[h] = (kv[:, LANE * h:LANE * (h + 1)] + kr).astype(MXU)
            v_ref[h] = kv[:, LANE * (nh + h):LANE * (nh + h + 1)].astype(MXU)

    head = pl.BlockSpec((nh, ts, LANE), lambda i: (0, i, 0))
    return pl.pallas_call(
        body, name="mla_prep_fwd", grid=(s // ts,),
        in_specs=[pl.BlockSpec((ts, Q_LORA), lambda i: (i, O_CQA // Q_LORA)),
                  pl.BlockSpec((ts, KV_LORA), lambda i: (i, O_CKV // KV_LORA)),
                  pl.BlockSpec((ts, LANE), lambda i: (i, O_TAIL // LANE)),
                  _full((1, Q_LORA)), _full((1, KV_LORA)), _full((nh * LANE, Q_LORA)), _full((2 * nh * LANE, KV_LORA)),
                  _row(ts, LANE), _row(ts, LANE)],
        out_specs=[head, head, head, _row(ts, Q_LORA), _row(ts, KV_LORA), _row(ts, 1), _row(ts, 1)],
        out_shape=[_sds((nh, s, LANE), MXU)] * 3 + [_sds((s, Q_LORA), MXU), _sds((s, KV_LORA), MXU), _sds((s, 1)), _sds((s, 1))],
        compiler_params=_params(("parallel",)),
    )(proj, proj, proj, gq, gkv, wq, wkv, cos, sin)


ATT_SCALE = (QK_NOPE + QK_ROPE) ** -0.5
NEG = -1e30


def _att_tile(s, most):
    return min(most, s // 2)


ATT_FWD_TILE = 1024
ATT_BWD_TILE = 512


def _attn_fwd(q, k, v):
    nh, s, _ = q.shape
    tq = _att_tile(s, ATT_FWD_TILE)
    nq = s // tq

    def body(q_ref, k_ref, v_ref, o_ref, lse_ref):
        i = pl.program_id(1)
        rowi = _iota((tq, tq), 0)
        coli = _iota((tq, tq), 1)
        zero = (jnp.full((tq, 1), NEG, F32), jnp.zeros((tq, 1), F32), jnp.zeros((tq, LANE), F32))
        state = [zero, zero]
        done = [zero, zero]
        for t in range(nq + 1):
            first = t <= i
            qblk = jnp.where(first, i, nq - 1 - i)
            kblk = jnp.where(first, t, t - i - 1)
            qoff = pl.multiple_of(qblk * tq, tq)
            koff = pl.multiple_of(kblk * tq, tq)
            keep = coli <= rowi + jnp.where(kblk == qblk, 0, tq)
            restart = t == i + 1
            for hh in range(2):
                m, lsum, acc = state[hh]
                if t > 0:
                    done[hh] = tuple(jnp.where(restart, a, b) for a, b in zip(state[hh], done[hh]))
                    m = jnp.where(restart, NEG, m)
                    lsum = jnp.where(restart, 0.0, lsum)
                    acc = jnp.where(restart, 0.0, acc)
                sc = _dot_nt(q_ref[hh, pl.ds(qoff, tq), :], k_ref[hh, pl.ds(koff, tq), :])
                sc = jnp.where(keep, sc, NEG)
                m_new = jnp.maximum(m, jnp.max(sc, axis=1, keepdims=True))
                p = jnp.exp(sc - m_new)
                alpha = jnp.exp(m - m_new)
                lsum = alpha * lsum + jnp.sum(p, axis=1, keepdims=True)
                acc = alpha * acc + _dot(p, v_ref[hh, pl.ds(koff, tq), :])
                state[hh] = (m_new, lsum, acc)
        for blk, res in ((i, done), (nq - 1 - i, state)):
            off = pl.multiple_of(blk * tq, tq)
            out = None
            for hh in range(2):
                m, lsum, acc = res[hh]
                o = acc * (1.0 / lsum)
                lse_ref[hh, pl.ds(off, tq), :] = m + jnp.log(lsum)
                out = o if hh == 0 else out + pltpu.roll(o, V_DIM, 1)
            o_ref[pl.ds(off, tq), :] = out

    pair = pl.BlockSpec((2, s, LANE), lambda j, i: (j, 0, 0))
    return pl.pallas_call(
        body, name="attn_fwd", grid=(nh // 2, nq // 2),
        in_specs=[pair, pair, pair],
        out_specs=[pl.BlockSpec((s, LANE), lambda j, i: (0, j)), pl.BlockSpec((2, s, 1), lambda j, i: (j, 0, 0))],
        out_shape=[_sds((s, D_MLA)), _sds((nh, s, 1))],
        compiler_params=_params(("parallel", "arbitrary")),
    )(q, k, v)


def _ssd_gate(y_ssd, s_z, g):
    yz = y_ssd * _silu(s_z)
    g0 = _iota(yz.shape, 1) < D_SSD // 2
    sq = yz * yz
    ms0 = jnp.sum(jnp.where(g0, sq, 0.0), axis=1, keepdims=True) / (D_SSD // 2)
    ms1 = jnp.sum(jnp.where(g0, 0.0, sq), axis=1, keepdims=True) / (D_SSD // 2)
    r = jnp.where(g0, lax.rsqrt(ms0 + SSD_NORM_EPS), lax.rsqrt(ms1 + SSD_NORM_EPS))
    nrm = yz * r
    return nrm * g, nrm, r, g0


def _outproj_fwd(x, proj, ya, y_ssd, o, g_ssd, w):
    s = x.shape[0]
    ts = _tile(s)

    def body(x_ref, sz_ref, cz_ref, ya_ref, ys_ref, o_ref, g_ref, w_ref, xo_ref, y_ref):
        yb = _ssd_gate(ys_ref[...], sz_ref[...], g_ref[...])[0]
        yc = o_ref[...] * _silu(cz_ref[...])
        y = jnp.concatenate([ya_ref[...], yb, yc], axis=1).astype(MXU)
        y_ref[...] = y
        xo_ref[...] = x_ref[...] + jnp.dot(y, w_ref[...], preferred_element_type=F32)

    return pl.pallas_call(
        body, name="outproj_fwd", grid=(s // ts,),
        in_specs=[_row(ts, D_MODEL), _gate_cols(ts, O_SZ), _gate_cols(ts, O_CZ), _row(ts, D_CONV_A), _row(ts, D_SSD),
                  _row(ts, D_MLA), _full((1, D_SSD)), _full((D_MODEL, D_MODEL))],
        out_specs=[_row(ts, D_MODEL), _row(ts, D_MODEL)],
        out_shape=[_sds((s, D_MODEL)), _sds((s, D_MODEL), MXU)],
        compiler_params=_params(("parallel",)),
    )(x, proj, proj, ya, y_ssd, o, g_ssd, w)


def _loss_head(x, g, tgt):
    s = x.shape[0]
    ts = _tile(s)

    def body(x_ref, g_ref, t_ref, dx_ref, dg_ref, loss_ref):
        @pl.when(pl.program_id(0) == 0)
        def _():
            dg_ref[...] = jnp.zeros_like(dg_ref)
            loss_ref[...] = jnp.zeros_like(loss_ref)

        xv = x_ref[...]
        gv = g_ref[...]
        yn, r = _rms(xv, gv)
        e = yn - t_ref[...]
        loss_ref[...] += jnp.sum(jnp.sum(e * e, axis=1, keepdims=True), axis=0, keepdims=True) * (0.5 / D_MODEL)
        dx, dg = _rms_bwd(e * (1.0 / D_MODEL), xv, r, gv)
        dx_ref[...] = dx
        dg_ref[...] += dg

    return pl.pallas_call(
        body, name="loss_head", grid=(s // ts,),
        in_specs=[_row(ts, D_MODEL), _full((1, D_MODEL)), _row(ts, D_MODEL)],
        out_specs=[_row(ts, D_MODEL), _full((1, D_MODEL)), _full((1, LANE))],
        out_shape=[_sds((s, D_MODEL)), _sds((1, D_MODEL)), _sds((1, LANE))],
        compiler_params=_params(("arbitrary",)),
    )(x, g, tgt)


def _outproj_bwd(dout, y, w, proj, y_ssd, o, g_ssd, dep=None):
    s = dout.shape[0]
    ts = _tile(s)

    def body(dout_ref, y_ref, w_ref, sz_ref, cz_ref, ys_ref, o_ref, g_ref,
             dya_ref, dys_ref, dsz_ref, dattn_ref, dcz_ref, dg_ref, dw_ref):
        @pl.when(pl.program_id(0) == 0)
        def _():
            dw_ref[...] = jnp.zeros_like(dw_ref)
            dg_ref[...] = jnp.zeros_like(dg_ref)

        dout_b = dout_ref[...].astype(MXU)
        dw_ref[...] += _dot_tn(y_ref[...], dout_b)
        dy = _dot_nt(dout_b, w_ref[...])
        dya_ref[...] = dy[:, :D_CONV_A]
        dyb = dy[:, D_CONV_A:D_CONV_A + D_SSD]
        sz = sz_ref[...]
        ys = ys_ref[...]
        gv = g_ref[...]
        _, nrm, r, g0 = _ssd_gate(ys, sz, gv)
        dg_ref[...] += jnp.sum(dyb * nrm, axis=0, keepdims=True)
        dn = dyb * gv
        t = dn * nrm
        mean = jnp.where(g0, jnp.sum(jnp.where(g0, t, 0.0), axis=1, keepdims=True),
                         jnp.sum(jnp.where(g0, 0.0, t), axis=1, keepdims=True)) / (D_SSD // 2)
        dyz = r * (dn - nrm * mean)
        dys_ref[...] = dyz * _silu(sz)
        dsz_ref[...] = dyz * ys * _dsilu(sz)
        dyc = dy[:, D_CONV_A + D_SSD:]
        cz = cz_ref[...]
        dattn_ref[...] = dyc * _silu(cz)
        dcz_ref[...] = dyc * o_ref[...] * _dsilu(cz)

    return _call_after(
        dep, body, (dout, y, w, proj, proj, y_ssd, o, g_ssd), name="outproj_bwd", grid=(s // ts,),
        in_specs=[_row(ts, D_MODEL), _row(ts, D_MODEL), _full((D_MODEL, D_MODEL)), _gate_cols(ts, O_SZ), _gate_cols(ts, O_CZ),
                  _row(ts, D_SSD), _row(ts, D_MLA), _full((1, D_SSD))],
        out_specs=[_row(ts, D_CONV_A), _row(ts, D_SSD), _row(ts, D_SSD), _row(ts, D_MLA), _row(ts, D_MLA),
                   _full((1, D_SSD)), _full((D_MODEL, D_MODEL))],
        out_shape=[_sds((s, D_CONV_A)), _sds((s, D_SSD)), _sds((s, D_SSD)), _sds((s, D_MLA)), _sds((s, D_MLA)),
                   _sds((1, D_SSD)), _sds((D_MODEL, D_MODEL))],
        compiler_params=_params(("arbitrary",)),
    )


def _attn_bwd(q, k, v, o, d_o, lse, dep=None):
    nh, s, _ = q.shape
    tq = _att_tile(s, ATT_BWD_TILE)
    nq = s // tq

    def body(q_ref, k_ref, v_ref, o_ref, do_ref, lse_ref, dq_ref, dk_ref, dv_ref, dop, delta):
        i = pl.program_id(1)

        @pl.when(i == 0)
        def _():
            lane = _iota((s, LANE), 1)
            for hh in range(2):
                dov = do_ref[...]
                ov = o_ref[...]
                if hh == 1:
                    dov = pltpu.roll(dov, V_DIM, 1)
                    ov = pltpu.roll(ov, V_DIM, 1)
                dov = jnp.where(lane < V_DIM, dov, 0.0)
                dop[hh] = dov.astype(MXU)
                delta[hh] = jnp.sum(dov * ov, axis=1, keepdims=True)
                dq_ref[hh] = jnp.zeros((s, LANE), F32)

        rowi = _iota((tq, tq), 0)
        coli = _iota((tq, tq), 1)
        z = jnp.zeros((tq, LANE), F32)
        state = [(z, z), (z, z)]
        done = [(z, z), (z, z)]
        for t in range(nq + 1):
            first = t <= nq - 1 - i
            kblk = jnp.where(first, i, nq - 1 - i)
            qblk = jnp.where(first, i + t, t - 1)
            qoff = pl.multiple_of(qblk * tq, tq)
            koff = pl.multiple_of(kblk * tq, tq)
            keep = coli <= rowi + jnp.where(kblk == qblk, 0, tq)
            restart = t == nq - i
            for hh in range(2):
                dk, dv = state[hh]
                if t > 0:
                    done[hh] = tuple(jnp.where(restart, a, b) for a, b in zip(state[hh], done[hh]))
                    dk = jnp.where(restart, 0.0, dk)
                    dv = jnp.where(restart, 0.0, dv)
                kb = k_ref[hh, pl.ds(koff, tq), :]
                qb = q_ref[hh, pl.ds(qoff, tq), :]
                dob = dop[hh, pl.ds(qoff, tq), :]
                sc = jnp.where(keep, _dot_nt(qb, kb), NEG)
                p = jnp.exp(sc - lse_ref[hh, pl.ds(qoff, tq), :])
                dp = _dot_nt(dob, v_ref[hh, pl.ds(koff, tq), :])
                ds = p * (dp - delta[hh, pl.ds(qoff, tq), :])
                dq_ref[hh, pl.ds(qoff, tq), :] += _dot(ds, kb)
                state[hh] = (dk + _dot_tn(ds, qb), dv + _dot_tn(p, dob))
        for blk, res in ((i, done), (nq - 1 - i, state)):
            off = pl.multiple_of(blk * tq, tq)
            for hh in range(2):
                dk_ref[hh, pl.ds(off, tq), :] = res[hh][0]
                dv_ref[hh, pl.ds(off, tq), :] = res[hh][1]

    pair = pl.BlockSpec((2, s, LANE), lambda j, i: (j, 0, 0))
    return _call_after(
        dep, body, (q, k, v, o, d_o, lse), name="attn_bwd", grid=(nh // 2, nq // 2),
        in_specs=[pair, pair, pair, pl.BlockSpec((s, LANE), lambda j, i: (0, j)), pl.BlockSpec((s, LANE), lambda j, i: (0, j)),
                  pl.BlockSpec((2, s, 1), lambda j, i: (j, 0, 0))],
        out_specs=[pair, pair, pair],
        out_shape=[_sds((nh, s, LANE))] * 3,
        scratch_shapes=[pltpu.VMEM((2, s, LANE), MXU), pltpu.VMEM((2, s, 1), F32)],
        compiler_params=_params(("parallel", "arbitrary")),
    )


def _ssd_bwd(xbc, proj, sc, states, dy, dep=None):
    s = xbc.shape[0]
    nc = s // SSD_CHUNK
    l = SSD_CHUNK

    def body(xbc_ref, tail_ref, sc_ref, st_ref, dy_ref, dxbc_ref, dtail_ref, dsc_ref, dstate):
        @pl.when(pl.program_id(0) == 0)
        def _():
            dstate[...] = jnp.zeros_like(dstate)
            dsc_ref[...] = jnp.zeros_like(dsc_ref)

        sc_v = sc_ref[...]
        lane, row, tri, a_row, pre, dt, a_cs, a_t = _ssd_chunk_common(tail_ref[...], sc_v)
        lane1 = _iota((1, LANE), 1)
        rowp = _iota((LANE, 1), 0)
        rowl = _iota((l, 1), 0)
        d_row = sc_v[2:3, :]
        da_col = jnp.zeros((l, LANE), F32)
        da_row = jnp.zeros((LANE, l), F32)
        dt_x = jnp.zeros((l, LANE), F32)
        dd_row = jnp.zeros((1, LANE), F32)
        db = [jnp.zeros((l, LANE), F32), jnp.zeros((l, LANE), F32)]
        dc = [jnp.zeros((l, LANE), F32), jnp.zeros((l, LANE), F32)]
        for j in range(3):
            xpair = xbc_ref[:, LANE * j:LANE * (j + 1)]
            dypair = dy_ref[:, LANE * j:LANE * (j + 1)]
            sp = st_ref[0, j]
            dsp = dstate[j]
            dxpair = jnp.zeros((l, LANE), F32)
            ds_new = jnp.zeros((LANE, LANE), F32)
            decay = jnp.zeros((LANE, 1), F32)
            for half in range(2):
                h = 2 * j + half
                g = h // 3
                hm = (lane < 64) if half == 0 else (lane >= 64)
                hrow = (rowp < 64) if half == 0 else (rowp >= 64)
                ac = _pick_col(a_cs, lane, DT_LANE + h)
                ar = _pick_row(a_t, row, DT_LANE + h)
                dtc = _pick_col(dt, lane, DT_LANE + h)
                alast = jnp.sum(jnp.where(lane1 == l - 1, ar, 0.0), axis=1, keepdims=True)
                dh = jnp.sum(jnp.where(lane1 == DT_LANE + h, d_row, 0.0), axis=1, keepdims=True)
                xm = jnp.where(hm, xpair, 0.0)
                xd = xm * dtc
                dym = jnp.where(hm, dypair, 0.0)
                bm = xbc_ref[:, D_SSD + LANE * g:D_SSD + LANE * (g + 1)]
                cm = xbc_ref[:, D_SSD + SSD_BC + LANE * g:D_SSD + SSD_BC + LANE * (g + 1)]
                lm = jnp.where(row >= lane, jnp.exp(jnp.minimum(ac - ar, 0.0)), 0.0)
                e_in = jnp.exp(ac)
                f_out = jnp.exp(alast - ac)
                e_last = jnp.exp(alast)
                m = _dot_nt(cm, bm) * lm
                y_off = jnp.where(hm, _dot_nt(cm, sp), 0.0) * e_in
                dm = _dot_nt(dym, xd)
                dxd = _dot_tn(m, dym)
                dg = dm * lm
                dye = dym * e_in
                dc[g] = dc[g] + _dot(dg, bm) + _dot(dye, sp)
                db[g] = db[g] + _dot_tn(dg, cm)
                qm = dm * m
                dac = jnp.sum(qm, axis=1, keepdims=True) + jnp.sum(dym * y_off, axis=1, keepdims=True)
                dar = -jnp.sum(qm, axis=0, keepdims=True)
                dxf = jnp.where(hm, _dot_nt(bm, dsp), 0.0)
                db[g] = db[g] + _dot(xd * f_out, dsp)
                dxd = dxd + dxf * f_out
                df = jnp.sum(dxf * xd, axis=1, keepdims=True) * f_out
                dac = dac - df
                s_last = jnp.sum(df, axis=0, keepdims=True)
                ss = jnp.sum(jnp.where(hrow, dsp * sp, 0.0), axis=1, keepdims=True)
                s_last = s_last + e_last * jnp.sum(ss, axis=0, keepdims=True)
                dac = dac + jnp.where(rowl == l - 1, s_last, 0.0)
                ds_new = ds_new + _dot_tn(dye, cm)
                decay = jnp.where(hrow, e_last, decay)
                dxpair = dxpair + dxd * dtc + dym * dh
                dt_x = dt_x + jnp.where(lane == DT_LANE + h, jnp.sum(dxd * xm, axis=1, keepdims=True), 0.0)
                dsum = jnp.sum(jnp.sum(dym * xm, axis=1, keepdims=True), axis=0, keepdims=True)
                dd_row = dd_row + jnp.where(lane1 == DT_LANE + h, dsum, 0.0)
                da_col = da_col + jnp.where(lane == DT_LANE + h, dac, 0.0)
                da_row = da_row + jnp.where(row == DT_LANE + h, dar, 0.0)
            dstate[j] = dsp * decay + ds_new
            dxbc_ref[:, LANE * j:LANE * (j + 1)] = dxpair
        for g in range(2):
            dxbc_ref[:, D_SSD + LANE * g:D_SSD + LANE * (g + 1)] = db[g]
            dxbc_ref[:, D_SSD + SSD_BC + LANE * g:D_SSD + SSD_BC + LANE * (g + 1)] = dc[g]
        dla = _dot_hi_tn(tri, da_col + da_row.T)
        ddt = dt_x + dla * a_row
        dpre = ddt * _sigmoid(pre)
        dtm = (lane >= DT_LANE) & (lane < DT_LANE + SSD_HEADS)
        dtail_ref[...] = jnp.where(dtm, dpre, 0.0)
        dtm1 = (lane1 >= DT_LANE) & (lane1 < DT_LANE + SSD_HEADS)
        dsc_ref[0:1, :] += jnp.where(dtm1, jnp.sum(dpre, axis=0, keepdims=True), 0.0)
        dsc_ref[1:2, :] += jnp.where(dtm1, jnp.sum(dla * dt, axis=0, keepdims=True) * a_row, 0.0)
        dsc_ref[2:3, :] += dd_row

    rev = lambda c: nc - 1 - c
    return _call_after(
        dep, body, (xbc, proj, sc, states, dy), name="ssd_bwd", grid=(nc,),
        in_specs=[pl.BlockSpec((l, N_XBC), lambda c: (rev(c), 0)),
                  pl.BlockSpec((l, LANE), lambda c: (rev(c), O_TAIL // LANE)), _full((8, LANE)),
                  pl.BlockSpec((1, 3, LANE, LANE), lambda c: (rev(c), 0, 0, 0)),
                  pl.BlockSpec((l, D_SSD), lambda c: (rev(c), 0))],
        out_specs=[pl.BlockSpec((l, N_XBC), lambda c: (rev(c), 0)), pl.BlockSpec((l, LANE), lambda c: (rev(c), 0)),
                   _full((8, LANE))],
        out_shape=[_sds((s, N_XBC)), _sds((s, LANE)), _sds((8, LANE))],
        scratch_shapes=[pltpu.VMEM((3, LANE, LANE), F32)],
        compiler_params=_params(("arbitrary",)),
    )


def _sconv_bwd(proj, w, b, dxbc, dep=None):
    s = proj.shape[0]

    def body(u_ref, w_ref, b_ref, d_ref, du_ref, dw_ref, db_ref):
        u = u_ref[...]
        wv = w_ref[...]
        dpre = d_ref[...] * _dsilu(_sconv_pre(u, wv, b_ref[...]))
        du_ref[...] = (wv[3:4, :] * dpre + wv[2:3, :] * _shift_up(dpre, 1) + wv[1:2, :] * _shift_up(dpre, 2)
                       + wv[0:1, :] * _shift_up(dpre, 3))
        for k in range(4):
            dw_ref[k:k + 1, :] = jnp.sum(dpre * _shift_down(u, 3 - k), axis=0, keepdims=True)
        db_ref[...] = jnp.sum(dpre, axis=0, keepdims=True)

    blk = pl.BlockSpec((s, LANE), lambda j: (0, j))
    return _call_after(
        dep, body, (proj, w, b, dxbc), name="sconv_bwd", grid=(N_XBC // LANE,),
        in_specs=[_col(s, O_XBC), pl.BlockSpec((4, LANE), lambda j: (0, j)), pl.BlockSpec((1, LANE), lambda j: (0, j)), blk],
        out_specs=[blk, pl.BlockSpec((4, LANE), lambda j: (0, j)), pl.BlockSpec((1, LANE), lambda j: (0, j))],
        out_shape=[_sds((s, N_XBC)), _sds((4, N_XBC)), _sds((1, N_XBC))],
        compiler_params=_params(("parallel",)),
    )


def _conva_bwd(proj, w, dya, dep=None):
    s = proj.shape[0]

    def body(h_ref, b_ref, c_ref, z_ref, w_ref, d_ref, da_ref, dw_ref):
        ah, ab, acv, az = h_ref[...], b_ref[...], c_ref[...], z_ref[...]
        wv = w_ref[...]
        u = acv * ah
        cv = wv[2:3, :] * u + wv[1:2, :] * _shift_down(u, 1) + wv[0:1, :] * _shift_down(u, 2)
        dy = d_ref[...]
        sz = _silu(az)
        da_ref[1] = dy * cv * sz
        da_ref[3] = dy * ab * cv * _dsilu(az)
        dcv = dy * ab * sz
        du = wv[2:3, :] * dcv + wv[1:2, :] * _shift_up(dcv, 1) + wv[0:1, :] * _shift_up(dcv, 2)
        da_ref[0] = du * acv
        da_ref[2] = du * ah
        for k in range(3):
            dw_ref[k:k + 1, :] = jnp.sum(dcv * _shift_down(u, 2 - k), axis=0, keepdims=True)

    return _call_after(
        dep, body, (proj, proj, proj, proj, w, dya), name="conva_bwd", grid=(D_CONV_A // LANE,),
        in_specs=[_col(s, O_AH), _col(s, O_AB), _col(s, O_AC), _col(s, O_AZ), pl.BlockSpec((3, LANE), lambda j: (0, j)),
                  pl.BlockSpec((s, LANE), lambda j: (0, j))],
        out_specs=[pl.BlockSpec((4, s, LANE), lambda j: (0, 0, j)), pl.BlockSpec((3, LANE), lambda j: (0, j))],
        out_shape=[_sds((4, s, D_CONV_A)), _sds((3, D_CONV_A))],
        compiler_params=_params(("parallel",)),
    )


def _mla_prep_bwd(dq, dk, dv, proj, qn, kvn, rq, rkv, gq, gkv, wq, wkv, cos, sin):
    s = proj.shape[0]
    ts = _tile(s)
    nh = MLA_HEADS

    def body(dq_ref, dk_ref, dv_ref, cqa_ref, ckv_ref, qn_ref, kvn_ref, rq_ref, rkv_ref, gq_ref, gkv_ref,
             wq_ref, wkv_ref, cos_ref, sin_ref, dcqa_ref, dckv_ref, dtail_ref, dwq_ref, dwkv_ref, dgq_ref, dgkv_ref):
        @pl.when(pl.program_id(0) == 0)
        def _():
            dwq_ref[...] = jnp.zeros_like(dwq_ref)
            dwkv_ref[...] = jnp.zeros_like(dwkv_ref)
            dgq_ref[...] = jnp.zeros_like(dgq_ref)
            dgkv_ref[...] = jnp.zeros_like(dgkv_ref)

        cosv = cos_ref[...]
        sinv = sin_ref[...]
        lane = _iota((ts, LANE), 1)
        rope_lanes = (lane >= ROPE_LANE) & (lane < ROPE_LANE + QK_ROPE)

        def unrope(gr):
            return gr * cosv + _rope_swap(gr * sinv)

        dqs, dks, dvs = [], [], []
        dkr = jnp.zeros((ts, LANE), F32)
        for h in range(nh):
            dqs.append(unrope(dq_ref[h] * ATT_SCALE).astype(MXU))
            dkh = dk_ref[h]
            dks.append(jnp.where(lane < QK_NOPE, dkh, 0.0).astype(MXU))
            dkr = dkr + jnp.where(rope_lanes, dkh, 0.0)
            dvs.append(dv_ref[h].astype(MXU))
        dtail_ref[...] = pltpu.roll(jnp.where(rope_lanes, unrope(dkr), 0.0), ROPE_LANE, 1)
        dq_all = jnp.concatenate(dqs, axis=1)
        dkv_all = jnp.concatenate(dks + dvs, axis=1)
        dwq_ref[...] += _dot_tn(dq_all, qn_ref[...])
        dwkv_ref[...] += _dot_tn(dkv_all, kvn_ref[...])
        dcqa, dgq = _rms_bwd(_dot(dq_all, wq_ref[...]), cqa_ref[...], rq_ref[...], gq_ref[...])
        dckv, dgkv = _rms_bwd(_dot(dkv_all, wkv_ref[...]), ckv_ref[...], rkv_ref[...], gkv_ref[...])
        dcqa_ref[...] = dcqa
        dckv_ref[...] = dckv
        dgq_ref[...] += dgq
        dgkv_ref[...] += dgkv

    head = pl.BlockSpec((nh, ts, LANE), lambda i: (0, i, 0))
    return pl.pallas_call(
        body, name="mla_prep_bwd", grid=(s // ts,),
        in_specs=[head, head, head,
                  pl.BlockSpec((ts, Q_LORA), lambda i: (i, O_CQA // Q_LORA)),
                  pl.BlockSpec((ts, KV_LORA), lambda i: (i, O_CKV // KV_LORA)),
                  _row(ts, Q_LORA), _row(ts, KV_LORA), _row(ts, 1), _row(ts, 1),
                  _full((1, Q_LORA)), _full((1, KV_LORA)), _full((nh * LANE, Q_LORA)), _full((2 * nh * LANE, KV_LORA)),
                  _row(ts, LANE), _row(ts, LANE)],
        out_specs=[_row(ts, Q_LORA), _row(ts, KV_LORA), _row(ts, LANE), _full((nh * LANE, Q_LORA)),
                   _full((2 * nh * LANE, KV_LORA)), _full((1, Q_LORA)), _full((1, KV_LORA))],
        out_shape=[_sds((s, Q_LORA)), _sds((s, KV_LORA)), _sds((s, LANE)), _sds((nh * LANE, Q_LORA)),
                   _sds((2 * nh * LANE, KV_LORA)), _sds((1, Q_LORA)), _sds((1, KV_LORA))],
        compiler_params=_params(("arbitrary",)),
    )(dq, dk, dv, proj, proj, qn, kvn, rq, rkv, gq, gkv, wq, wkv, cos, sin)


def _inproj_bwd(da4, dsz, dxbc_in, dcqa, dckv, dcz, dtail_a, dtail_b, w, x, rstd, g, dout):
    s = x.shape[0]
    ts = _tile(s)

    def body(da_ref, dsz_ref, dxbc_ref, dcqa_ref, dckv_ref, dcz_ref, dta_ref, dtb_ref, w_ref, x_ref, r_ref, g_ref, dout_ref,
             dproj_ref, dx_ref, dg_ref):
        @pl.when(pl.program_id(0) == 0)
        def _():
            dg_ref[...] = jnp.zeros_like(dg_ref)

        dproj = jnp.concatenate(
            [da_ref[0], da_ref[1], da_ref[2], da_ref[3], dxbc_ref[...], dsz_ref[...], dcqa_ref[...], dckv_ref[...],
             dcz_ref[...], dta_ref[...] + dtb_ref[...]], axis=1).astype(MXU)
        dproj_ref[...] = dproj
        dh = _dot_nt(dproj, w_ref[...])
        dx, dg = _rms_bwd(dh, x_ref[...], r_ref[...], g_ref[...])
        dx_ref[...] = dout_ref[...] + dx
        dg_ref[...] += dg

    return pl.pallas_call(
        body, name="inproj_bwd", grid=(s // ts,),
        in_specs=[pl.BlockSpec((4, ts, D_CONV_A), lambda i: (0, i, 0)), _row(ts, D_SSD), _row(ts, N_XBC), _row(ts, Q_LORA),
                  _row(ts, KV_LORA), _row(ts, D_MLA), _row(ts, LANE), _row(ts, LANE), _full((D_MODEL, NCOL)),
                  _row(ts, D_MODEL), _row(ts, 1), _full((1, D_MODEL)), _row(ts, D_MODEL)],
        out_specs=[_row(ts, NCOL), _row(ts, D_MODEL), _full((1, D_MODEL))],
        out_shape=[_sds((s, NCOL), MXU), _sds((s, D_MODEL)), _sds((1, D_MODEL))],
        compiler_params=_params(("arbitrary",)),
    )(da4, dsz, dxbc_in, dcqa, dckv, dcz, dtail_a, dtail_b, w, x, rstd, g, dout)


DWIN_BLOCK = 640


def _dwin(h, dproj):
    s = h.shape[0]

    def body(h_ref, d_ref, o_ref):
        o_ref[...] = _dot_tn(h_ref[...], d_ref[...])

    return pl.pallas_call(
        body, name="dwin", grid=(NCOL // DWIN_BLOCK,),
        in_specs=[_full((s, D_MODEL)), pl.BlockSpec((s, DWIN_BLOCK), lambda j: (0, j))],
        out_specs=pl.BlockSpec((D_MODEL, DWIN_BLOCK), lambda j: (0, j)),
        out_shape=_sds((D_MODEL, NCOL)),
        compiler_params=_params(("parallel",)),
    )(h, dproj)


def _adamw(w, g, m, v):
    bc1 = 1.0 - ADAM_B1 ** ADAM_STEP
    bc2 = 1.0 - ADAM_B2 ** ADAM_STEP

    def body(w_ref, g_ref, m_ref, v_ref, d_ref, mo_ref, vo_ref):
        gv = g_ref[...]
        mn = ADAM_B1 * m_ref[...] + (1.0 - ADAM_B1) * gv
        vn = ADAM_B2 * v_ref[...] + (1.0 - ADAM_B2) * (gv * gv)
        mo_ref[...] = mn
        vo_ref[...] = vn
        d_ref[...] = -ADAM_LR * ((mn / bc1) / (jnp.sqrt(vn / bc2) + ADAM_EPS) + ADAM_WD * w_ref[...])

    if w.ndim == 2:
        grid, blk = (1,), pl.BlockSpec(w.shape, lambda i: (0, 0))
    else:
        grid, blk = (w.shape[0],), pl.BlockSpec((1,) + w.shape[1:], lambda i: (i, 0, 0))
    return pl.pallas_call(
        body, name="adamw", grid=grid,
        in_specs=[blk] * 4, out_specs=[blk] * 3, out_shape=[_sds(w.shape)] * 3,
        compiler_params=_params(("parallel",)),
    )(w, g, m, v)


COL_MOVES = ((0, 0, 1024), (1024, O_SZ, 384), (1408, O_XBC, 896), (2304, O_TAIL + DT_LANE, 6), (2310, O_CQA, 256),
             (2566, O_CKV, 128), (2694, O_TAIL, 32), (2726, O_CZ, 384))


def _move_cols(w, moves, width):
    out = None
    for src, dst, n in moves:
        piece = jnp.pad(w[..., src:src + n], [(0, 0)] * (w.ndim - 1) + [(dst, width - dst - n)])
        out = piece if out is None else out + piece
    return out


def _perm_cols(w):
    return _move_cols(w, COL_MOVES, NCOL)


def _unperm_cols(g):
    return _move_cols(g, [(dst, src, n) for src, dst, n in COL_MOVES], IN_COLS)


def _wq_layout(wt):
    return jnp.pad(wt.reshape(MLA_HEADS, QK_NOPE + QK_ROPE, Q_LORA), ((0, 0), (0, 32), (0, 0))).reshape(MLA_HEADS * LANE, Q_LORA)


def _wq_unlayout(g):
    return g.reshape(MLA_HEADS, LANE, Q_LORA)[:, :QK_NOPE + QK_ROPE].reshape(MLA_HEADS * (QK_NOPE + QK_ROPE), Q_LORA)


def _wkv_layout(wt):
    t = wt.reshape(MLA_HEADS, 2, 64, KV_LORA).transpose(1, 0, 2, 3)
    return jnp.pad(t, ((0, 0), (0, 0), (0, 64), (0, 0))).reshape(2 * MLA_HEADS * LANE, KV_LORA)


def _wkv_unlayout(g):
    t = g.reshape(2, MLA_HEADS, LANE, KV_LORA)[:, :, :64]
    return t.transpose(1, 0, 2, 3).reshape(MLA_HEADS * LANE, KV_LORA)


def _rope_tables(positions):
    inv_freq = ROPE_BASE ** (-jnp.arange(0, QK_ROPE, 2, dtype=F32) / QK_ROPE)
    ang = positions.astype(F32)[:, None] * inv_freq
    cos, sin = jnp.cos(ang), jnp.sin(ang)
    s = positions.shape[0]
    one, zero = jnp.ones((s, ROPE_LANE), F32), jnp.zeros((s, ROPE_LANE), F32)
    cos_t = jnp.concatenate([one, cos, cos, one[:, :32]], axis=1)
    sin_t = jnp.concatenate([zero, -sin, sin, zero[:, :32]], axis=1)
    return cos_t, sin_t


def _ssd_scalars(dt_bias, a_log, d_skip):
    return jnp.pad(jnp.stack([dt_bias, a_log, d_skip]), ((0, 5), (DT_LANE, LANE - DT_LANE - SSD_HEADS)))


def _layer_fwd(x, lw, cos, sin, dep=None, late=None):
    proj, h, rstd = _inproj_fwd(x, lw["norm_g"], lw["w_in"], dep)
    ya = _conva_fwd(proj, lw["conv_a_w"])
    xbc = _sconv_fwd(proj, lw["ssd_conv_w"], lw["ssd_conv_b"])
    y_ssd, states = _ssd_fwd(xbc, proj, lw["sc"])
    if late is not None:
        lw = {**lw, **late(ya, y_ssd)}
    q, k, v, qn, kvn, rq, rkv = _mla_prep_fwd(proj, lw["gq"], lw["gkv"], lw["wq"], lw["wkv"], cos, sin)
    o, lse = _attn_fwd(q, k, v)
    x_out, y = _outproj_fwd(x, proj, ya, y_ssd, o, lw["g_ssd"], lw["w_out"])
    saved = dict(x=x, proj=proj, h=h, rstd=rstd, xbc=xbc, y_ssd=y_ssd, states=states, q=q, k=k, v=v, qn=qn, kvn=kvn,
                 rq=rq, rkv=rkv, o=o, lse=lse, y=y)
    return x_out, saved, lw


def _layer_bwd(dout, lw, sv, cos, sin, rs=None):
    tok = lambda: None if rs is None else rs["h"]["token"]
    dya, dys, dsz, d_o, dcz, dg_ssd, dw_out = _outproj_bwd(dout, sv["y"], lw["w_out"], sv["proj"], sv["y_ssd"], sv["o"],
                                                            lw["g_ssd"], tok())
    if rs is not None:
        rs = _rs_add_mine(rs, [dya])
    dq, dk, dv = _attn_bwd(sv["q"], sv["k"], sv["v"], sv["o"], d_o, sv["lse"], tok())
    dxbc, dtail_s, dsc = _ssd_bwd(sv["xbc"], sv["proj"], lw["sc"], sv["states"], dys, tok())
    da4, dw_conva = _conva_bwd(sv["proj"], lw["conv_a_w"], dya, tok())
    if rs is not None:
        rs = _rs_add_chips(rs, [dq, dxbc, da4])
    du, dw_sconv, db_sconv = _sconv_bwd(sv["proj"], lw["ssd_conv_w"], lw["ssd_conv_b"], dxbc, tok())
    dcqa, dckv, dtail_m, dwq, dwkv, dgq, dgkv = _mla_prep_bwd(
        dq, dk, dv, sv["proj"], sv["qn"], sv["kvn"], sv["rq"], sv["rkv"], lw["gq"], lw["gkv"], lw["wq"], lw["wkv"], cos, sin)
    dproj, dx, dg = _inproj_bwd(da4, dsz, du, dcqa, dckv, dcz, dtail_s, dtail_m, lw["w_in"], sv["x"], sv["rstd"],
                                lw["norm_g"], dout)
    reduced = None if rs is None else _rs_end(rs, [du, dcqa, dx])
    dw_in = _dwin(sv["h"], dproj)
    grads = dict(norm_g=dg, w_in=dw_in, conv_a_w=dw_conva, ssd_conv_w=dw_sconv, ssd_conv_b=db_sconv, sc=dsc,
                 g_ssd=dg_ssd, gq=dgq, wq=dwq, gkv=dgkv, wkv=dwkv, w_out=dw_out)
    return dx, grads, reduced


ANY = pl.BlockSpec(memory_space=pl.ANY)
N_CHIPS = 4
N_DEV = 8


def _place():
    return lax.axis_index("x"), lax.axis_index("y"), lax.axis_index("c")


HBM_SPEC = pl.BlockSpec(memory_space=pltpu.HBM)
SEM_SPEC = pl.BlockSpec(memory_space=pltpu.SEMAPHORE)
PAYLOAD = jnp.bfloat16


def _hbm(a):
    return pltpu.with_memory_space_constraint(a, pltpu.HBM)


def _run_plan(plan, srcs, lands, send_sems, recv_sems, start, wait):
    copies = plan(srcs, lands)
    if start:
        for i, (src, dst, _, to) in enumerate(copies):
            pltpu.make_async_remote_copy(src_ref=src, dst_ref=dst, send_sem=send_sems.at[i], recv_sem=recv_sems.at[i],
                                         device_id=to, device_id_type=MESH_T).start()
    if wait:
        for i, (src, _, arrives, to) in enumerate(copies):
            cp = pltpu.make_async_remote_copy(src_ref=src, dst_ref=arrives, send_sem=send_sems.at[i],
                                              recv_sem=recv_sems.at[i], device_id=to, device_id_type=MESH_T)
            cp.wait_send()
            cp.wait_recv()


def _exchange_fused(name, plan, n_copies, srcs, land_shapes):
    ns, nl = len(srcs), len(land_shapes)

    def body(*refs):
        _run_plan(plan, refs[:ns], refs[ns:ns + nl], refs[ns + nl], refs[ns + nl + 1], True, True)

    return pl.pallas_call(
        body, name=name, in_specs=[ANY] * ns, out_specs=[ANY] * nl, out_shape=list(land_shapes),
        scratch_shapes=[pltpu.SemaphoreType.DMA((n_copies,)), pltpu.SemaphoreType.DMA((n_copies,))],
    )(*srcs)


def _exchange_start(name, plan, n_copies, srcs, land_shapes, deps):
    ns, nl = len(srcs), len(land_shapes)
    n_in = ns + nl + len(deps)

    def body(*refs):
        send_sems, recv_sems = refs[n_in], refs[n_in + 1]
        token = refs[-1]
        _run_plan(plan, refs[:ns], refs[ns:ns + nl], send_sems, recv_sems, True, False)
        token[...] = jnp.zeros_like(token)

    thru = [pltpu.HBM(a.shape, a.dtype) for a in srcs] + [pltpu.HBM(a.shape, a.dtype) for a in land_shapes]
    outs = pl.pallas_call(
        body, name=name,
        out_shape=(pltpu.SemaphoreType.DMA((n_copies,)), pltpu.SemaphoreType.DMA((n_copies,)), *thru, _sds((8, LANE))),
        in_specs=[HBM_SPEC] * (ns + nl) + [ANY] * len(deps),
        out_specs=(SEM_SPEC, SEM_SPEC, *[HBM_SPEC] * (ns + nl), pl.BlockSpec(memory_space=pltpu.VMEM)),
        input_output_aliases={i: 2 + i for i in range(ns + nl)},
        compiler_params=pltpu.CompilerParams(has_side_effects=pltpu.SideEffectType.DATAFLOW_SIDE_EFFECTING),
    )(*[_hbm(a) for a in srcs], *[_hbm(lax.empty(a.shape, a.dtype)) for a in land_shapes], *deps)
    return (outs[0], outs[1]), list(outs[2:2 + ns]), list(outs[2 + ns:2 + ns + nl]), outs[-1]


def _exchange_wait(name, plan, sems, srcs, lands, after):
    ns, nl = len(srcs), len(lands)

    def body(*refs):
        _run_plan(plan, refs[:ns], refs[ns:ns + nl], refs[ns + nl], refs[ns + nl + 1], False, True)

    outs = pl.pallas_call(
        body, name=name,
        out_shape=[pltpu.HBM(a.shape, a.dtype) for a in list(srcs) + list(lands)],
        in_specs=[HBM_SPEC] * (ns + nl) + [SEM_SPEC, SEM_SPEC] + [ANY] * len(after), out_specs=[HBM_SPEC] * (ns + nl),
        input_output_aliases={i: i for i in range(ns + nl)},
        compiler_params=pltpu.CompilerParams(has_side_effects=pltpu.SideEffectType.DATAFLOW_SIDE_EFFECTING),
    )(*srcs, *lands, sems[0], sems[1], *after)
    return list(outs[:ns]), list(outs[ns:])


def _xchg_begin(name, plan, n_copies, srcs, land_shapes, split, deps=()):
    if not split:
        return dict(split=False, srcs=list(srcs), lands=_exchange_fused(name, plan, n_copies, srcs, land_shapes),
                    token=jnp.zeros((8, LANE), F32))
    sems, srcs_t, lands_t, token = _exchange_start(name + "_start", plan, n_copies, srcs, land_shapes, list(deps))
    return dict(split=True, name=name, plan=plan, sems=sems, srcs=srcs_t, lands=lands_t, token=token)


def _xchg_end(h, after):
    if not h["split"]:
        return h["srcs"], h["lands"]
    return _exchange_wait(h["name"] + "_wait", h["plan"], h["sems"], h["srcs"], h["lands"], after)


def _other_chips():
    x, y, c = _place()
    return [(1 - x, y), (x, 1 - y), (1 - x, 1 - y)]


def _gather_plan(srcs, lands):
    x, y, c = _place()
    me = 2 * x + y
    return [(srcs[a], lands[a].at[me], lands[a].at[2 * cx + cy], (cx, cy, c))
            for (cx, cy) in _other_chips() for a in range(len(srcs))]


def _gather_begin(shards, split, tag, deps=()):
    shapes = [_sds((N_CHIPS,) + a.shape, a.dtype) for a in shards]
    return _xchg_begin(f"gather_{tag}", _gather_plan, 3 * len(shards), shards, shapes, split, deps)


def _gather_end(h, after):
    shards, lands = _xchg_end(h, after)
    me = 2 * lax.axis_index("x") + lax.axis_index("y")
    return [lax.dynamic_update_index_in_dim(g, s, me, 0) for g, s in zip(lands, shards)]


def _swap_plan(srcs, lands):
    x, y, c = _place()
    return [(srcs[a].at[:, 1 - c], lands[a], lands[a], (x, y, 1 - c)) for a in range(len(srcs))]


def _chips_plan(srcs, lands):
    x, y, c = _place()
    me = 2 * x + y
    return [(srcs[a].at[2 * cx + cy], lands[a].at[me], lands[a].at[2 * cx + cy], (cx, cy, c))
            for (cx, cy) in _other_chips() for a in range(len(srcs))]


def _share_plan(srcs, lands):
    x, y, c = _place()
    return [(srcs[a], lands[a].at[c], lands[a].at[1 - c], (x, y, 1 - c)) for a in range(len(srcs))]


def _allreduce_small(slab, dep=None):
    r = slab.shape[0]

    def body(s_ref, o_ref, gath, send_sems, recv_sems):
        x, y, c = _place()
        me = 4 * x + 2 * y + c
        gath[me] = s_ref[...]
        cps = []
        for rel in range(1, N_DEV):
            px = 1 - x if rel & 4 else x
            py = 1 - y if rel & 2 else y
            pc = 1 - c if rel & 1 else c
            cp = pltpu.make_async_remote_copy(src_ref=s_ref, dst_ref=gath.at[me], send_sem=send_sems.at[rel - 1],
                                              recv_sem=recv_sems.at[rel - 1], device_id=(px, py, pc), device_id_type=MESH_T)
            cp.start()
            cps.append(cp)
        for cp in cps:
            cp.wait()
        acc = gath[0]
        for d in range(1, N_DEV):
            acc = acc + gath[d]
        o_ref[...] = acc

    vm = pl.BlockSpec(memory_space=pltpu.VMEM)
    return _call_after(
        dep, body, (slab,), name="allreduce_small", in_specs=[vm], out_specs=vm, out_shape=_sds((r, LANE)),
        scratch_shapes=[pltpu.VMEM((N_DEV, r, LANE), F32), pltpu.SemaphoreType.DMA((N_DEV - 1,)),
                        pltpu.SemaphoreType.DMA((N_DEV - 1,))],
    )


def _add_mine(g4, recv, half):
    _, _, rh, c = g4.shape

    def body(h_ref, g_ref, r_ref, o_ref):
        o_ref[0] = (g_ref[0, 0] + r_ref[0]).astype(o_ref.dtype)

    return pl.pallas_call(
        body, name="add_mine",
        grid_spec=pltpu.PrefetchScalarGridSpec(
            num_scalar_prefetch=1, grid=(N_CHIPS,),
            in_specs=[pl.BlockSpec((1, 1, rh, c), lambda j, h: (j, h[0], 0, 0)), pl.BlockSpec((1, rh, c), lambda j, h: (j, 0, 0))],
            out_specs=pl.BlockSpec((1, rh, c), lambda j, h: (j, 0, 0))),
        out_shape=_sds((N_CHIPS, rh, c), PAYLOAD),
        compiler_params=_params(("parallel",)),
    )(half, g4, recv)


def _add_chips(e, p, me):
    _, rh, c = e.shape

    def body(m_ref, e_ref, p_ref, o_ref):
        own = p_ref[0].astype(F32)
        acc = None
        for s in range(N_CHIPS):
            t = jnp.where(m_ref[0] == s, own, e_ref[s].astype(F32))
            acc = t if acc is None else acc + t
        o_ref[...] = acc

    return pl.pallas_call(
        body, name="add_chips",
        grid_spec=pltpu.PrefetchScalarGridSpec(
            num_scalar_prefetch=1, grid=(1,),
            in_specs=[pl.BlockSpec((N_CHIPS, rh, c), lambda i, m: (0, 0, 0)), pl.BlockSpec((1, rh, c), lambda i, m: (m[0], 0, 0))],
            out_specs=pl.BlockSpec((rh, c), lambda i, m: (0, 0))),
        out_shape=_sds((rh, c)),
        compiler_params=_params(("arbitrary",)),
    )(me, e, p)


def _rs_begin(gs, split, tag):
    g4 = [g.reshape(N_CHIPS, 2, g.shape[0] // (2 * N_CHIPS), g.shape[1]) for g in gs]
    h = _xchg_begin(f"rs_swap_{tag}", _swap_plan, len(gs), g4, [_sds((N_CHIPS,) + g.shape[2:]) for g in g4], split)
    return dict(h=h, split=split, tag=tag, shapes=[g.shape for g in gs])


def _rs_add_mine(st, after):
    g4, recv = _xchg_end(st["h"], after)
    half = jnp.reshape(lax.axis_index("c"), (1,)).astype(jnp.int32)
    ps = [_add_mine(g, r, half) for g, r in zip(g4, recv)]
    st["h"] = _xchg_begin(f"rs_chips_{st['tag']}", _chips_plan, 3 * len(ps), ps, [_sds(p.shape, p.dtype) for p in ps], st["split"])
    return st


def _rs_add_chips(st, after):
    ps, es = _xchg_end(st["h"], after)
    me = jnp.reshape(2 * lax.axis_index("x") + lax.axis_index("y"), (1,)).astype(jnp.int32)
    fs = [_add_chips(e, p, me) for e, p in zip(es, ps)]
    st["h"] = _xchg_begin(f"rs_share_{st['tag']}", _share_plan, len(fs), fs, [_sds((2,) + f.shape) for f in fs], st["split"])
    return st


def _rs_end(st, after):
    fs, ss = _xchg_end(st["h"], after)
    c = lax.axis_index("c")
    return [lax.dynamic_update_index_in_dim(s, f, c, 0).reshape(shp[0] // N_CHIPS, shp[1])
            for s, f, shp in zip(ss, fs, st["shapes"])]


WEIGHTS = ["norm_g", "w_in", "conv_a_w", "ssd_conv_w", "ssd_conv_b", "ssd_dt_bias", "ssd_a_log", "ssd_d", "ssd_norm_g",
           "mla_q_norm_g", "w_qb", "mla_kv_norm_g", "w_kvb", "w_out", "final_norm_g"]
BIG = ["w_in", "w_qb", "w_kvb", "w_out"]
SLAB_ROWS = 128
SMALL_ROWS = 72


def _to_slab(parts, rows):
    flat = jnp.concatenate([p.reshape(-1) for p in parts])
    return jnp.pad(flat, (0, rows * LANE - flat.shape[0])).reshape(rows, LANE)


def _from_slab(slab, shapes):
    flat = slab.reshape(-1)
    out, off = [], 0
    for shp in shapes:
        n = int(np.prod(shp))
        out.append(flat[off:off + n].reshape(shp))
        off += n
    return out


def kernel(x, positions, norm_g, w_in, conv_a_w, ssd_conv_w, ssd_conv_b, ssd_dt_bias, ssd_a_log, ssd_d, ssd_norm_g, mla_q_norm_g, w_qb, mla_kv_norm_g, w_kvb, w_out, final_norm_g, loss_target, m_norm_g, m_w_in, m_conv_a_w, m_ssd_conv_w, m_ssd_conv_b, m_ssd_dt_bias, m_ssd_a_log, m_ssd_d, m_ssd_norm_g, m_mla_q_norm_g, m_w_qb, m_mla_kv_norm_g, m_w_kvb, m_w_out, m_final_norm_g, v_norm_g, v_w_in, v_conv_a_w, v_ssd_conv_w, v_ssd_conv_b, v_ssd_dt_bias, v_ssd_a_log, v_ssd_d, v_ssd_norm_g, v_mla_q_norm_g, v_w_qb, v_mla_kv_norm_g, v_w_kvb, v_w_out, v_final_norm_g):
    w = dict(norm_g=norm_g, w_in=w_in, conv_a_w=conv_a_w, ssd_conv_w=ssd_conv_w, ssd_conv_b=ssd_conv_b,
             ssd_dt_bias=ssd_dt_bias, ssd_a_log=ssd_a_log, ssd_d=ssd_d, ssd_norm_g=ssd_norm_g, mla_q_norm_g=mla_q_norm_g,
             w_qb=w_qb, mla_kv_norm_g=mla_kv_norm_g, w_kvb=w_kvb, w_out=w_out, final_norm_g=final_norm_g)
    mom = dict(norm_g=m_norm_g, w_in=m_w_in, conv_a_w=m_conv_a_w, ssd_conv_w=m_ssd_conv_w, ssd_conv_b=m_ssd_conv_b,
               ssd_dt_bias=m_ssd_dt_bias, ssd_a_log=m_ssd_a_log, ssd_d=m_ssd_d, ssd_norm_g=m_ssd_norm_g,
               mla_q_norm_g=m_mla_q_norm_g, w_qb=m_w_qb, mla_kv_norm_g=m_mla_kv_norm_g, w_kvb=m_w_kvb, w_out=m_w_out,
               final_norm_g=m_final_norm_g)
    var = dict(norm_g=v_norm_g, w_in=v_w_in, conv_a_w=v_conv_a_w, ssd_conv_w=v_ssd_conv_w, ssd_conv_b=v_ssd_conv_b,
               ssd_dt_bias=v_ssd_dt_bias, ssd_a_log=v_ssd_a_log, ssd_d=v_ssd_d, ssd_norm_g=v_ssd_norm_g,
               mla_q_norm_g=v_mla_q_norm_g, w_qb=v_w_qb, mla_kv_norm_g=v_mla_kv_norm_g, w_kvb=v_w_kvb, w_out=v_w_out,
               final_norm_g=v_final_norm_g)
    chip = 2 * lax.axis_index("x") + lax.axis_index("y")

    def early_shard(l, zero):
        pack = jnp.pad(conv_a_w[l], ((0, 5), (0, 192))) + jnp.pad(ssd_conv_w[l], ((3, 1), (0, 32)))
        return [(_perm_cols(w_in[l]) + zero).astype(MXU), pack + zero]

    def late_shard(l, zero):
        return [(w_out[l] + zero).astype(MXU), (w_qb[l].T + zero).astype(MXU), (w_kvb[l].T + zero).astype(MXU)]

    def early_weights(l, gathered):
        g_in, g_conv = gathered
        return dict(
            norm_g=norm_g[l][None], w_in=g_in.reshape(D_MODEL, NCOL),
            conv_a_w=jnp.concatenate([g_conv[j, 0:3, 0:64] for j in range(N_CHIPS)], axis=1),
            ssd_conv_w=jnp.concatenate([g_conv[j, 3:7, 0:224] for j in range(N_CHIPS)], axis=1),
            ssd_conv_b=ssd_conv_b[l][None], sc=_ssd_scalars(ssd_dt_bias[l], ssd_a_log[l], ssd_d[l]),
            g_ssd=ssd_norm_g[l][None], gq=mla_q_norm_g[l][None], gkv=mla_kv_norm_g[l][None])

    def late_weights(gathered):
        g_out, g_qb, g_kvb = gathered
        return dict(wq=_wq_layout(g_qb.reshape(MLA_HEADS * 96, Q_LORA)), wkv=_wkv_layout(g_kvb.reshape(MLA_HEADS * LANE, KV_LORA)),
                    w_out=g_out.reshape(D_MODEL, D_MODEL))

    def large_grads(g):
        wq = jnp.pad(_wq_unlayout(g["wq"]).reshape(N_CHIPS, 144, Q_LORA), ((0, 0), (0, 16), (0, 0)))
        return [g["w_in"], g["w_out"], wq.reshape(N_CHIPS * 160, Q_LORA), _wkv_unlayout(g["wkv"])]

    gather_a0 = _gather_begin(early_shard(0, 0.0), True, "a0")
    zero = gather_a0["token"][0, 0]
    cos, sin = _rope_tables(positions[0] + zero.astype(jnp.int32))
    late0, shards1 = late_shard(0, zero), early_shard(1, zero) + late_shard(1, zero)
    opt_in = {nm: [a[nm] + zero for a in (w, mom, var)] for nm in BIG}
    lw0 = early_weights(0, _gather_end(gather_a0, [cos, sin] + late0 + shards1 + [a for nm in BIG for a in opt_in[nm]]))
    gather_b0 = _gather_begin(late0, True, "b0")
    gather_1 = _gather_begin(shards1, True, "1", [gather_b0["token"]])
    x1, sv0, lw0 = _layer_fwd(x[0], lw0, cos, sin, gather_1["token"],
                              lambda ya, y_ssd: late_weights(_gather_end(gather_b0, [ya, y_ssd])))
    g1 = _gather_end(gather_1, [x1])
    x2, sv1, lw1 = _layer_fwd(x1, {**early_weights(1, g1[:2]), **late_weights(g1[2:])}, cos, sin)
    dx, dgf, loss = _loss_head(x2, final_norm_g[None], loss_target[0])

    dx, lg1, _ = _layer_bwd(dx, lw1, sv1, cos, sin)
    grad_x, lg0, red1 = _layer_bwd(dx, lw0, sv0, cos, sin, _rs_begin(large_grads(lg1), True, 1))
    rs0 = _rs_begin(large_grads(lg0), True, 0)
    lg = [lg0, lg1]
    grad = {}

    small_names = ["norm_g", "conv_a_w", "ssd_conv_w", "ssd_conv_b", "sc", "g_ssd", "gq", "gkv"]
    parts = [loss[0, 0:1], dgf]
    for l in range(DEPTH):
        parts += [lg[l][nm][:3, DT_LANE:DT_LANE + SSD_HEADS] if nm == "sc" else lg[l][nm] for nm in small_names]
    shapes = [(1,), (D_MODEL,)] + [(D_MODEL,), (3, D_CONV_A), (4, N_XBC), (N_XBC,), (3, SSD_HEADS), (D_SSD,), (Q_LORA,), (KV_LORA,)] * DEPTH
    red_slab = _allreduce_small(_to_slab(parts, SLAB_ROWS), rs0["h"]["token"])
    rs0 = _rs_add_mine(rs0, [red_slab])
    red = _from_slab(red_slab + rs0["h"]["token"][0, 0], shapes)
    loss_out = red[0][0]
    grad["final_norm_g"] = red[1]
    per = [red[2 + 8 * l:10 + 8 * l] for l in range(DEPTH)]
    grad["norm_g"] = jnp.stack([per[l][0] for l in range(DEPTH)])
    grad["conv_a_w"] = lax.dynamic_slice_in_dim(jnp.stack([per[l][1] for l in range(DEPTH)]), chip * 64, 64, axis=2)
    grad["ssd_conv_w"] = lax.dynamic_slice_in_dim(jnp.stack([per[l][2] for l in range(DEPTH)]), chip * 224, 224, axis=2)
    grad["ssd_conv_b"] = jnp.stack([per[l][3] for l in range(DEPTH)])
    grad["ssd_dt_bias"] = jnp.stack([per[l][4][0] for l in range(DEPTH)])
    grad["ssd_a_log"] = jnp.stack([per[l][4][1] for l in range(DEPTH)])
    grad["ssd_d"] = jnp.stack([per[l][4][2] for l in range(DEPTH)])
    grad["ssd_norm_g"] = jnp.stack([per[l][5] for l in range(DEPTH)])
    grad["mla_q_norm_g"] = jnp.stack([per[l][6] for l in range(DEPTH)])
    grad["mla_kv_norm_g"] = jnp.stack([per[l][7] for l in range(DEPTH)])

    delta, new_m, new_v = {}, {}, {}
    small = [nm for nm in WEIGHTS if nm not in BIG]
    sshapes = [w[nm].shape for nm in small]
    d, mo, vo = _adamw(_to_slab([w[nm] for nm in small], SMALL_ROWS), _to_slab([grad[nm] for nm in small], SMALL_ROWS),
                       _to_slab([mom[nm] for nm in small], SMALL_ROWS), _to_slab([var[nm] for nm in small], SMALL_ROWS))
    small_out = list(zip(small, _from_slab(d, sshapes), _from_slab(mo, sshapes), _from_slab(vo, sshapes)))
    for nm, dv, mv, vv in small_out:
        delta[nm], new_m[nm], new_v[nm] = dv, mv, vv

    red0 = _rs_end(_rs_add_chips(rs0, [a for row in small_out for a in row[1:]] + [grad[nm] for nm in small]), [])
    r_in, r_out, r_qb, r_kvb = [jnp.stack([a, b]) for a, b in zip(red0, red1)]
    grad.update(w_in=_unperm_cols(r_in), w_out=r_out, w_qb=jnp.swapaxes(r_qb[:, :144], 1, 2), w_kvb=jnp.swapaxes(r_kvb, 1, 2))
    for nm in BIG:
        delta[nm], new_m[nm], new_v[nm] = _adamw(opt_in[nm][0], grad[nm], opt_in[nm][1], opt_in[nm][2])

    return (loss_out, grad_x[None], *[grad[nm] for nm in WEIGHTS], *[delta[nm] for nm in WEIGHTS],
            *[new_m[nm] for nm in WEIGHTS], *[new_v[nm] for nm in WEIGHTS])
```

```python
import functools
import math

import numpy as np
import jax
import jax.numpy as jnp
from jax import lax
from jax.experimental import pallas as pl
from jax.experimental.pallas import tpu as pltpu

F32 = jnp.float32
MXU = jnp.bfloat16

D_MODEL = 1024
DEPTH = 2
D_CONV_A = 256
D_SSD = 384
SSD_HEADS = 6
SSD_BC = 256
SSD_CHUNK = 128
SSD_CHUNKS_PER_STEP = 2
SSD_NORM_EPS = 1e-5
MLA_HEADS = 6
Q_LORA = 256
KV_LORA = 128
QK_NOPE = 64
QK_ROPE = 32
V_DIM = 64
D_MLA = 384
ROPE_BASE = 10000.0
NORM_EPS = 1e-6
IN_COLS = 3110
LANE = 128

O_AH, O_AB, O_AC, O_AZ = 0, 256, 512, 768
O_XBC = 1024
O_SZ = 1920
O_CQA = 2304
O_CKV = 2560
O_CZ = 2688
O_TAIL = 3072
NCOL = 3200
N_XBC = D_SSD + 2 * SSD_BC
DT_LANE = 32
ROPE_LANE = 64

ADAM_LR, ADAM_B1, ADAM_B2, ADAM_EPS, ADAM_WD, ADAM_STEP = 0.001, 0.9, 0.999, 1e-08, 0.01, 10

VMEM_LIMIT = 56 * 1024 * 1024
MESH_T = pl.DeviceIdType.MESH


def _dot(a, b):
    return jnp.dot(a.astype(MXU), b.astype(MXU), preferred_element_type=F32)


def _dot_nt(a, b):
    return lax.dot_general(a.astype(MXU), b.astype(MXU), (((1,), (1,)), ((), ())), preferred_element_type=F32)


def _dot_tn(a, b):
    return lax.dot_general(a.astype(MXU), b.astype(MXU), (((0,), (0,)), ((), ())), preferred_element_type=F32)


def _dot_hi(a, b):
    return jnp.dot(a, b, precision=lax.Precision.HIGHEST, preferred_element_type=F32)


def _dot_hi_tn(a, b):
    return lax.dot_general(a, b, (((0,), (0,)), ((), ())), precision=lax.Precision.HIGHEST, preferred_element_type=F32)


def _sigmoid(z):
    return 1.0 / (1.0 + jnp.exp(-z))


def _silu(z):
    return z * _sigmoid(z)


def _dsilu(z):
    s = _sigmoid(z)
    return s * (1.0 + z * (1.0 - s))


def _softplus(z):
    e = jnp.exp(-jnp.abs(z))
    return jnp.maximum(z, 0.0) + jnp.where(e < 1e-3, e * (1.0 - 0.5 * e), jnp.log(1.0 + e))


def _iota(shape, dim):
    return lax.broadcasted_iota(jnp.int32, shape, dim)


def _shift_down(u, k):
    if k == 0:
        return u
    return jnp.where(_iota(u.shape, 0) >= k, pltpu.roll(u, k, 0), 0.0)


def _shift_up(u, k):
    if k == 0:
        return u
    n = u.shape[0]
    return jnp.where(_iota(u.shape, 0) < n - k, pltpu.roll(u, n - k, 0), 0.0)


def _rope_swap(t):
    lane = _iota(t.shape, 1)
    lo = (lane >= ROPE_LANE) & (lane < ROPE_LANE + 16)
    hi = (lane >= ROPE_LANE + 16) & (lane < ROPE_LANE + 32)
    return jnp.where(lo, pltpu.roll(t, LANE - 16, 1), jnp.where(hi, pltpu.roll(t, 16, 1), 0.0))


def _params(sem=None):
    return pltpu.CompilerParams(dimension_semantics=sem, vmem_limit_bytes=VMEM_LIMIT)


def _full(shape):
    nd = len(shape)
    return pl.BlockSpec(shape, lambda *_: (0,) * nd)


def _sds(shape, dtype=F32):
    return jax.ShapeDtypeStruct(shape, dtype)


def _tile(s):
    return min(256, s)


def _row(ts, w):
    return pl.BlockSpec((ts, w), lambda i: (i, 0))


def _gate_cols(ts, off):
    return pl.BlockSpec((ts, D_SSD), lambda i, _o=off // D_SSD: (i, _o))


def _col(s, off):
    return pl.BlockSpec((s, LANE), lambda j, _o=off // LANE: (0, _o + j))


def _call_after(dep, body, args, *, in_specs, **kw):
    if dep is None:
        return pl.pallas_call(body, in_specs=in_specs, **kw)(*args)
    n = len(args)

    def body_dep(*refs):
        body(*refs[:n], *refs[n + 1:])

    return pl.pallas_call(body_dep, in_specs=list(in_specs) + [pl.BlockSpec(memory_space=pl.ANY)], **kw)(*args, dep)


def _rms(c, g):
    r = lax.rsqrt(jnp.mean(c * c, axis=-1, keepdims=True) + NORM_EPS)
    return c * r * g, r


def _rms_bwd(dn, c, r, g):
    ch = c * r
    dch = dn * g
    dc = r * (dch - ch * jnp.mean(dch * ch, axis=-1, keepdims=True))
    return dc, jnp.sum(dn * ch, axis=0, keepdims=True)


def _inproj_fwd(x, g, w, dep=None):
    s = x.shape[0]
    ts = _tile(s)

    def body(x_ref, g_ref, w_ref, proj_ref, h_ref, r_ref):
        hn, r = _rms(x_ref[...], g_ref[...])
        h = hn.astype(MXU)
        h_ref[...] = h
        r_ref[...] = r
        proj_ref[...] = jnp.dot(h, w_ref[...], preferred_element_type=F32)

    return _call_after(
        dep, body, (x, g, w), name="inproj_fwd", grid=(s // ts,),
        in_specs=[_row(ts, D_MODEL), _full((1, D_MODEL)), _full((D_MODEL, NCOL))],
        out_specs=[_row(ts, NCOL), _row(ts, D_MODEL), _row(ts, 1)],
        out_shape=[_sds((s, NCOL)), _sds((s, D_MODEL), MXU), _sds((s, 1))],
        compiler_params=_params(("parallel",)),
    )


def _conva_fwd(proj, w):
    s = proj.shape[0]

    def body(h_ref, b_ref, c_ref, z_ref, w_ref, y_ref):
        u = c_ref[...] * h_ref[...]
        wv = w_ref[...]
        cv = wv[2:3, :] * u + wv[1:2, :] * _shift_down(u, 1) + wv[0:1, :] * _shift_down(u, 2)
        y_ref[...] = b_ref[...] * cv * _silu(z_ref[...])

    return pl.pallas_call(
        body, name="conva_fwd", grid=(D_CONV_A // LANE,),
        in_specs=[_col(s, O_AH), _col(s, O_AB), _col(s, O_AC), _col(s, O_AZ), pl.BlockSpec((3, LANE), lambda j: (0, j))],
        out_specs=pl.BlockSpec((s, LANE), lambda j: (0, j)),
        out_shape=_sds((s, D_CONV_A)),
        compiler_params=_params(("parallel",)),
    )(proj, proj, proj, proj, w)


def _sconv_pre(u, wv, bv):
    return (wv[3:4, :] * u + wv[2:3, :] * _shift_down(u, 1) + wv[1:2, :] * _shift_down(u, 2)
            + wv[0:1, :] * _shift_down(u, 3) + bv)


def _sconv_fwd(proj, w, b):
    s = proj.shape[0]

    def body(u_ref, w_ref, b_ref, o_ref):
        o_ref[...] = _silu(_sconv_pre(u_ref[...], w_ref[...], b_ref[...]))

    return pl.pallas_call(
        body, name="sconv_fwd", grid=(N_XBC // LANE,),
        in_specs=[_col(s, O_XBC), pl.BlockSpec((4, LANE), lambda j: (0, j)), pl.BlockSpec((1, LANE), lambda j: (0, j))],
        out_specs=pl.BlockSpec((s, LANE), lambda j: (0, j)),
        out_shape=_sds((s, N_XBC)),
        compiler_params=_params(("parallel",)),
    )(proj, w, b)


def _ssd_chunk_common(tail, sc):
    l = SSD_CHUNK
    lane = _iota((l, LANE), 1)
    row = _iota((l, LANE), 0)
    tri = (row >= lane).astype(F32)
    a_row = -jnp.exp(sc[1:2, :])
    pre = tail + sc[0:1, :]
    dt = _softplus(pre)
    a_cs = _dot_hi(tri, dt * a_row)
    return lane, row, tri, a_row, pre, dt, a_cs, a_cs.T


def _pick_col(m, lane, k):
    return jnp.sum(jnp.where(lane == k, m, 0.0), axis=1, keepdims=True)


def _pick_row(m, row, k):
    return jnp.sum(jnp.where(row == k, m, 0.0), axis=0, keepdims=True)


def _ssd_fwd(xbc, proj, sc):
    s = xbc.shape[0]
    nc = s // SSD_CHUNK
    l = SSD_CHUNK
    cps = SSD_CHUNKS_PER_STEP

    def body(xbc_ref, tail_ref, sc_ref, y_ref, st_ref, state):
        @pl.when(pl.program_id(0) == 0)
        def _():
            state[...] = jnp.zeros_like(state)

        sc_v = sc_ref[...]
        lane1 = _iota((1, LANE), 1)
        rowp = _iota((LANE, 1), 0)
        d_row = sc_v[2:3, :]
        states = [state[j] for j in range(3)]
        for u in range(cps):
            r = slice(u * l, (u + 1) * l)
            lane, row, _, _, _, dt, a_cs, a_t = _ssd_chunk_common(tail_ref[r, :], sc_v)
            for j in range(3):
                st_ref[u, j] = states[j]
            for j in range(3):
                xpair = xbc_ref[r, LANE * j:LANE * (j + 1)]
                sp = states[j]
                ypair = jnp.zeros((l, LANE), F32)
                new_s = jnp.zeros((LANE, LANE), F32)
                decay = jnp.zeros((LANE, 1), F32)
                for half in range(2):
                    h = 2 * j + half
                    g = h // 3
                    hm = (lane < 64) if half == 0 else (lane >= 64)
                    hrow = (rowp < 64) if half == 0 else (rowp >= 64)
                    ac = _pick_col(a_cs, lane, DT_LANE + h)
                    ar = _pick_row(a_t, row, DT_LANE + h)
                    dtc = _pick_col(dt, lane, DT_LANE + h)
                    alast = jnp.sum(jnp.where(lane1 == l - 1, ar, 0.0), axis=1, keepdims=True)
                    dh = jnp.sum(jnp.where(lane1 == DT_LANE + h, d_row, 0.0), axis=1, keepdims=True)
                    xm = jnp.where(hm, xpair, 0.0)
                    xd = xm * dtc
                    bm = xbc_ref[r, D_SSD + LANE * g:D_SSD + LANE * (g + 1)]
                    cm = xbc_ref[r, D_SSD + SSD_BC + LANE * g:D_SSD + SSD_BC + LANE * (g + 1)]
                    lm = jnp.where(row >= lane, jnp.exp(jnp.minimum(ac - ar, 0.0)), 0.0)
                    y_diag = _dot(_dot_nt(cm, bm) * lm, xd)
                    y_off = jnp.where(hm, _dot_nt(cm, sp), 0.0) * jnp.exp(ac)
                    ypair = ypair + y_diag + y_off + xm * dh
                    new_s = new_s + _dot_tn(xd * jnp.exp(alast - ac), bm)
                    decay = jnp.where(hrow, jnp.exp(alast), decay)
                states[j] = sp * decay + new_s
                y_ref[r, LANE * j:LANE * (j + 1)] = ypair
        for j in range(3):
            state[j] = states[j]

    return pl.pallas_call(
        body, name="ssd_fwd", grid=(nc // cps,),
        in_specs=[pl.BlockSpec((cps * l, N_XBC), lambda c: (c, 0)),
                  pl.BlockSpec((cps * l, LANE), lambda c: (c, O_TAIL // LANE)), _full((8, LANE))],
        out_specs=[pl.BlockSpec((cps * l, D_SSD), lambda c: (c, 0)), pl.BlockSpec((cps, 3, LANE, LANE), lambda c: (c, 0, 0, 0))],
        out_shape=[_sds((s, D_SSD)), _sds((nc, 3, LANE, LANE))],
        scratch_shapes=[pltpu.VMEM((3, LANE, LANE), F32)],
        compiler_params=_params(("arbitrary",)),
    )(xbc, proj, sc)


def _mla_prep_fwd(proj, gq, gkv, wq, wkv, cos, sin):
    s = proj.shape[0]
    ts = _tile(s)
    nh = MLA_HEADS

    def body(cqa_ref, ckv_ref, tail_ref, gq_ref, gkv_ref, wq_ref, wkv_ref, cos_ref, sin_ref,
             q_ref, k_ref, v_ref, qn_ref, kvn_ref, rq_ref, rkv_ref):
        qn, rq = _rms(cqa_ref[...], gq_ref[...])
        kvn, rkv = _rms(ckv_ref[...], gkv_ref[...])
        qn = qn.astype(MXU)
        kvn = kvn.astype(MXU)
        qn_ref[...] = qn
        kvn_ref[...] = kvn
        rq_ref[...] = rq
        rkv_ref[...] = rkv
        q = _dot_nt(qn, wq_ref[...])
        kv = _dot_nt(kvn, wkv_ref[...])
        cosv = cos_ref[...]
        sinv = sin_ref[...]
        lane = _iota((ts, LANE), 1)
        rope_lanes = (lane >= ROPE_LANE) & (lane < ROPE_LANE + QK_ROPE)
        kr = jnp.where(rope_lanes, pltpu.roll(tail_ref[...], ROPE_LANE, 1), 0.0)
        kr = kr * cosv + _rope_swap(kr) * sinv
        for h in range(nh):
            qh = q[:, LANE * h:LANE * (h + 1)]
            q_ref[h] = ((qh * cosv + _rope_swap(qh) * sinv) * ATT_SCALE).astype(MXU)
            k_ref[h] = (kv[:, LANE * h:LANE * (h + 1)] + kr).astype(MXU)
            v_ref[h] = kv[:, LANE * (nh + h):LANE * (nh + h + 1)].astype(MXU)

    head = pl.BlockSpec((nh, ts, LANE), lambda i: (0, i, 0))
    return pl.pallas_call(
        body, name="mla_prep_fwd", grid=(s // ts,),
        in_specs=[pl.BlockSpec((ts, Q_LORA), lambda i: (i, O_CQA // Q_LORA)),
                  pl.BlockSpec((ts, KV_LORA), lambda i: (i, O_CKV // KV_LORA)),
                  pl.BlockSpec((ts, LANE), lambda i: (i, O_TAIL // LANE)),
                  _full((1, Q_LORA)), _full((1, KV_LORA)), _full((nh * LANE, Q_LORA)), _full((2 * nh * LANE, KV_LORA)),
                  _row(ts, LANE), _row(ts, LANE)],
        out_specs=[head, head, head, _row(ts, Q_LORA), _row(ts, KV_LORA), _row(ts, 1), _row(ts, 1)],
        out_shape=[_sds((nh, s, LANE), MXU)] * 3 + [_sds((s, Q_LORA), MXU), _sds((s, KV_LORA), MXU), _sds((s, 1)), _sds((s, 1))],
        compiler_params=_params(("parallel",)),
    )(proj, proj, proj, gq, gkv, wq, wkv, cos, sin)


ATT_SCALE = (QK_NOPE + QK_ROPE) ** -0.5
NEG = -1e30


def _att_tile(s, most):
    return min(most, s // 2)


ATT_FWD_TILE = 1024
ATT_BWD_TILE = 512


def _attn_fwd(q, k, v):
    nh, s, _ = q.shape
    tq = _att_tile(s, ATT_FWD_TILE)
    nq = s // tq

    def body(q_ref, k_ref, v_ref, o_ref, lse_ref):
        i = pl.program_id(1)
        rowi = _iota((tq, tq), 0)
        coli = _iota((tq, tq), 1)
        zero = (jnp.full((tq, 1), NEG, F32), jnp.zeros((tq, 1), F32), jnp.zeros((tq, LANE), F32))
        state = [zero, zero]
        done = [zero, zero]
        for t in range(nq + 1):
            first = t <= i
            qblk = jnp.where(first, i, nq - 1 - i)
            kblk = jnp.where(first, t, t - i - 1)
            qoff = pl.multiple_of(qblk * tq, tq)
            koff = pl.multiple_of(kblk * tq, tq)
            keep = coli <= rowi + jnp.where(kblk == qblk, 0, tq)
            restart = t == i + 1
            for hh in range(2):
                m, lsum, acc = state[hh]
                if t > 0:
                    done[hh] = tuple(jnp.where(restart, a, b) for a, b in zip(state[hh], done[hh]))
                    m = jnp.where(restart, NEG, m)
                    lsum = jnp.where(restart, 0.0, lsum)
                    acc = jnp.where(restart, 0.0, acc)
                sc = _dot_nt(q_ref[hh, pl.ds(qoff, tq), :], k_ref[hh, pl.ds(koff, tq), :])
                sc = jnp.where(keep, sc, NEG)
                m_new = jnp.maximum(m, jnp.max(sc, axis=1, keepdims=True))
                p = jnp.exp(sc - m_new)
                alpha = jnp.exp(m - m_new)
                lsum = alpha * lsum + jnp.sum(p, axis=1, keepdims=True)
                acc = alpha * acc + _dot(p, v_ref[hh, pl.ds(koff, tq), :])
                state[hh] = (m_new, lsum, acc)
        for blk, res in ((i, done), (nq - 1 - i, state)):
            off = pl.multiple_of(blk * tq, tq)
            out = None
            for hh in range(2):
                m, lsum, acc = res[hh]
                o = acc * (1.0 / lsum)
                lse_ref[hh, pl.ds(off, tq), :] = m + jnp.log(lsum)
                out = o if hh == 0 else out + pltpu.roll(o, V_DIM, 1)
            o_ref[pl.ds(off, tq), :] = out

    pair = pl.BlockSpec((2, s, LANE), lambda j, i: (j, 0, 0))
    return pl.pallas_call(
        body, name="attn_fwd", grid=(nh // 2, nq // 2),
        in_specs=[pair, pair, pair],
        out_specs=[pl.BlockSpec((s, LANE), lambda j, i: (0, j)), pl.BlockSpec((2, s, 1), lambda j, i: (j, 0, 0))],
        out_shape=[_sds((s, D_MLA)), _sds((nh, s, 1))],
        compiler_params=_params(("parallel", "arbitrary")),
    )(q, k, v)


def _ssd_gate(y_ssd, s_z, g):
    yz = y_ssd * _silu(s_z)
    g0 = _iota(yz.shape, 1) < D_SSD // 2
    sq = yz * yz
    ms0 = jnp.sum(jnp.where(g0, sq, 0.0), axis=1, keepdims=True) / (D_SSD // 2)
    ms1 = jnp.sum(jnp.where(g0, 0.0, sq), axis=1, keepdims=True) / (D_SSD // 2)
    r = jnp.where(g0, lax.rsqrt(ms0 + SSD_NORM_EPS), lax.rsqrt(ms1 + SSD_NORM_EPS))
    nrm = yz * r
    return nrm * g, nrm, r, g0


def _outproj_fwd(x, proj, ya, y_ssd, o, g_ssd, w):
    s = x.shape[0]
    ts = _tile(s)

    def body(x_ref, sz_ref, cz_ref, ya_ref, ys_ref, o_ref, g_ref, w_ref, xo_ref, y_ref):
        yb = _ssd_gate(ys_ref[...], sz_ref[...], g_ref[...])[0]
        yc = o_ref[...] * _silu(cz_ref[...])
        y = jnp.concatenate([ya_ref[...], yb, yc], axis=1).astype(MXU)
        y_ref[...] = y
        xo_ref[...] = x_ref[...] + jnp.dot(y, w_ref[...], preferred_element_type=F32)

    return pl.pallas_call(
        body, name="outproj_fwd", grid=(s // ts,),
        in_specs=[_row(ts, D_MODEL), _gate_cols(ts, O_SZ), _gate_cols(ts, O_CZ), _row(ts, D_CONV_A), _row(ts, D_SSD),
                  _row(ts, D_MLA), _full((1, D_SSD)), _full((D_MODEL, D_MODEL))],
        out_specs=[_row(ts, D_MODEL), _row(ts, D_MODEL)],
        out_shape=[_sds((s, D_MODEL)), _sds((s, D_MODEL), MXU)],
        compiler_params=_params(("parallel",)),
    )(x, proj, proj, ya, y_ssd, o, g_ssd, w)


def _loss_head(x, g, tgt):
    s = x.shape[0]
    ts = _tile(s)

    def body(x_ref, g_ref, t_ref, dx_ref, dg_ref, loss_ref):
        @pl.when(pl.program_id(0) == 0)
        def _():
            dg_ref[...] = jnp.zeros_like(dg_ref)
            loss_ref[...] = jnp.zeros_like(loss_ref)

        xv = x_ref[...]
        gv = g_ref[...]
        yn, r = _rms(xv, gv)
        e = yn - t_ref[...]
        loss_ref[...] += jnp.sum(jnp.sum(e * e, axis=1, keepdims=True), axis=0, keepdims=True) * (0.5 / D_MODEL)
        dx, dg = _rms_bwd(e * (1.0 / D_MODEL), xv, r, gv)
        dx_ref[...] = dx
        dg_ref[...] += dg

    return pl.pallas_call(
        body, name="loss_head", grid=(s // ts,),
        in_specs=[_row(ts, D_MODEL), _full((1, D_MODEL)), _row(ts, D_MODEL)],
        out_specs=[_row(ts, D_MODEL), _full((1, D_MODEL)), _full((1, LANE))],
        out_shape=[_sds((s, D_MODEL)), _sds((1, D_MODEL)), _sds((1, LANE))],
        compiler_params=_params(("arbitrary",)),
    )(x, g, tgt)


def _outproj_bwd(dout, y, w, proj, y_ssd, o, g_ssd, dep=None):
    s = dout.shape[0]
    ts = _tile(s)

    def body(dout_ref, y_ref, w_ref, sz_ref, cz_ref, ys_ref, o_ref, g_ref,
             dya_ref, dys_ref, dsz_ref, dattn_ref, dcz_ref, dg_ref, dw_ref):
        @pl.when(pl.program_id(0) == 0)
        def _():
            dw_ref[...] = jnp.zeros_like(dw_ref)
            dg_ref[...] = jnp.zeros_like(dg_ref)

        dout_b = dout_ref[...].astype(MXU)
        dw_ref[...] += _dot_tn(y_ref[...], dout_b)
        dy = _dot_nt(dout_b, w_ref[...])
        dya_ref[...] = dy[:, :D_CONV_A]
        dyb = dy[:, D_CONV_A:D_CONV_A + D_SSD]
        sz = sz_ref[...]
        ys = ys_ref[...]
        gv = g_ref[...]
        _, nrm, r, g0 = _ssd_gate(ys, sz, gv)
        dg_ref[...] += jnp.sum(dyb * nrm, axis=0, keepdims=True)
        dn = dyb * gv
        t = dn * nrm
        mean = jnp.where(g0, jnp.sum(jnp.where(g0, t, 0.0), axis=1, keepdims=True),
                         jnp.sum(jnp.where(g0, 0.0, t), axis=1, keepdims=True)) / (D_SSD // 2)
        dyz = r * (dn - nrm * mean)
        dys_ref[...] = dyz * _silu(sz)
        dsz_ref[...] = (dyz * ys * _dsilu(sz)).astype(MXU)
        dyc = dy[:, D_CONV_A + D_SSD:]
        cz = cz_ref[...]
        dattn_ref[...] = dyc * _silu(cz)
        dcz_ref[...] = (dyc * o_ref[...] * _dsilu(cz)).astype(MXU)

    return _call_after(
        dep, body, (dout, y, w, proj, proj, y_ssd, o, g_ssd), name="outproj_bwd", grid=(s // ts,),
        in_specs=[_row(ts, D_MODEL), _row(ts, D_MODEL), _full((D_MODEL, D_MODEL)), _gate_cols(ts, O_SZ), _gate_cols(ts, O_CZ),
                  _row(ts, D_SSD), _row(ts, D_MLA), _full((1, D_SSD))],
        out_specs=[_row(ts, D_CONV_A), _row(ts, D_SSD), _row(ts, D_SSD), _row(ts, D_MLA), _row(ts, D_MLA),
                   _full((1, D_SSD)), _full((D_MODEL, D_MODEL))],
        out_shape=[_sds((s, D_CONV_A)), _sds((s, D_SSD)), _sds((s, D_SSD), MXU), _sds((s, D_MLA)), _sds((s, D_MLA), MXU),
                   _sds((1, D_SSD)), _sds((D_MODEL, D_MODEL))],
        compiler_params=_params(("arbitrary",)),
    )


def _attn_bwd(q, k, v, o, d_o, lse, dep=None):
    nh, s, _ = q.shape
    tq = _att_tile(s, ATT_BWD_TILE)
    nq = s // tq

    def body(q_ref, k_ref, v_ref, o_ref, do_ref, lse_ref, dq_ref, dk_ref, dv_ref, dop, delta):
        i = pl.program_id(1)

        @pl.when(i == 0)
        def _():
            lane = _iota((s, LANE), 1)
            for hh in range(2):
                dov = do_ref[...]
                ov = o_ref[...]
                if hh == 1:
                    dov = pltpu.roll(dov, V_DIM, 1)
                    ov = pltpu.roll(ov, V_DIM, 1)
                dov = jnp.where(lane < V_DIM, dov, 0.0)
                dop[hh] = dov.astype(MXU)
                delta[hh] = jnp.sum(dov * ov, axis=1, keepdims=True)
                dq_ref[hh] = jnp.zeros((s, LANE), F32)

        rowi = _iota((tq, tq), 0)
        coli = _iota((tq, tq), 1)
        z = jnp.zeros((tq, LANE), F32)
        state = [(z, z), (z, z)]
        done = [(z, z), (z, z)]
        for t in range(nq + 1):
            first = t <= nq - 1 - i
            kblk = jnp.where(first, i, nq - 1 - i)
            qblk = jnp.where(first, i + t, t - 1)
            qoff = pl.multiple_of(qblk * tq, tq)
            koff = pl.multiple_of(kblk * tq, tq)
            keep = coli <= rowi + jnp.where(kblk == qblk, 0, tq)
            restart = t == nq - i
            for hh in range(2):
                dk, dv = state[hh]
                if t > 0:
                    done[hh] = tuple(jnp.where(restart, a, b) for a, b in zip(state[hh], done[hh]))
                    dk = jnp.where(restart, 0.0, dk)
                    dv = jnp.where(restart, 0.0, dv)
                kb = k_ref[hh, pl.ds(koff, tq), :]
                qb = q_ref[hh, pl.ds(qoff, tq), :]
                dob = dop[hh, pl.ds(qoff, tq), :]
                sc = jnp.where(keep, _dot_nt(qb, kb), NEG)
                p = jnp.exp(sc - lse_ref[hh, pl.ds(qoff, tq), :])
                dp = _dot_nt(dob, v_ref[hh, pl.ds(koff, tq), :])
                ds = p * (dp - delta[hh, pl.ds(qoff, tq), :])
                dq_ref[hh, pl.ds(qoff, tq), :] += _dot(ds, kb)
                state[hh] = (dk + _dot_tn(ds, qb), dv + _dot_tn(p, dob))
        for blk, res in ((i, done), (nq - 1 - i, state)):
            off = pl.multiple_of(blk * tq, tq)
            for hh in range(2):
                dk_ref[hh, pl.ds(off, tq), :] = res[hh][0]
                dv_ref[hh, pl.ds(off, tq), :] = res[hh][1]

    pair = pl.BlockSpec((2, s, LANE), lambda j, i: (j, 0, 0))
    return _call_after(
        dep, body, (q, k, v, o, d_o, lse), name="attn_bwd", grid=(nh // 2, nq // 2),
        in_specs=[pair, pair, pair, pl.BlockSpec((s, LANE), lambda j, i: (0, j)), pl.BlockSpec((s, LANE), lambda j, i: (0, j)),
                  pl.BlockSpec((2, s, 1), lambda j, i: (j, 0, 0))],
        out_specs=[pair, pair, pair],
        out_shape=[_sds((nh, s, LANE))] * 3,
        scratch_shapes=[pltpu.VMEM((2, s, LANE), MXU), pltpu.VMEM((2, s, 1), F32)],
        compiler_params=_params(("parallel", "arbitrary")),
    )


def _ssd_bwd(xbc, proj, sc, states, dy, dep=None):
    s = xbc.shape[0]
    nc = s // SSD_CHUNK
    l = SSD_CHUNK
    cps = SSD_CHUNKS_PER_STEP

    def body(xbc_ref, tail_ref, sc_ref, st_ref, dy_ref, dxbc_ref, dtail_ref, dsc_ref, dstate):
        @pl.when(pl.program_id(0) == 0)
        def _():
            dstate[...] = jnp.zeros_like(dstate)
            dsc_ref[...] = jnp.zeros_like(dsc_ref)

        sc_v = sc_ref[...]
        lane1 = _iota((1, LANE), 1)
        rowp = _iota((LANE, 1), 0)
        rowl = _iota((l, 1), 0)
        d_row = sc_v[2:3, :]
        dstates = [dstate[j] for j in range(3)]
        for u in reversed(range(cps)):
            dstates = chunk(u, xbc_ref, tail_ref, sc_v, st_ref, dy_ref, dxbc_ref, dtail_ref, dsc_ref, dstates,
                            lane1, rowp, rowl, d_row)
        for j in range(3):
            dstate[j] = dstates[j]

    def chunk(u, xbc_ref, tail_ref, sc_v, st_ref, dy_ref, dxbc_ref, dtail_ref, dsc_ref, dstates, lane1, rowp, rowl, d_row):
        r = slice(u * l, (u + 1) * l)
        dstates = list(dstates)
        lane, row, tri, a_row, pre, dt, a_cs, a_t = _ssd_chunk_common(tail_ref[r, :], sc_v)
        da_col = jnp.zeros((l, LANE), F32)
        da_row = jnp.zeros((LANE, l), F32)
        dt_x = jnp.zeros((l, LANE), F32)
        dd_row = jnp.zeros((1, LANE), F32)
        db = [jnp.zeros((l, LANE), F32), jnp.zeros((l, LANE), F32)]
        dc = [jnp.zeros((l, LANE), F32), jnp.zeros((l, LANE), F32)]
        for j in range(3):
            xpair = xbc_ref[r, LANE * j:LANE * (j + 1)]
            dypair = dy_ref[r, LANE * j:LANE * (j + 1)]
            sp = st_ref[u, j]
            dsp = dstates[j]
            dxpair = jnp.zeros((l, LANE), F32)
            ds_new = jnp.zeros((LANE, LANE), F32)
            decay = jnp.zeros((LANE, 1), F32)
            for half in range(2):
                h = 2 * j + half
                g = h // 3
                hm = (lane < 64) if half == 0 else (lane >= 64)
                hrow = (rowp < 64) if half == 0 else (rowp >= 64)
                ac = _pick_col(a_cs, lane, DT_LANE + h)
                ar = _pick_row(a_t, row, DT_LANE + h)
                dtc = _pick_col(dt, lane, DT_LANE + h)
                alast = jnp.sum(jnp.where(lane1 == l - 1, ar, 0.0), axis=1, keepdims=True)
                dh = jnp.sum(jnp.where(lane1 == DT_LANE + h, d_row, 0.0), axis=1, keepdims=True)
                xm = jnp.where(hm, xpair, 0.0)
                xd = xm * dtc
                dym = jnp.where(hm, dypair, 0.0)
                bm = xbc_ref[r, D_SSD + LANE * g:D_SSD + LANE * (g + 1)]
                cm = xbc_ref[r, D_SSD + SSD_BC + LANE * g:D_SSD + SSD_BC + LANE * (g + 1)]
                lm = jnp.where(row >= lane, jnp.exp(jnp.minimum(ac - ar, 0.0)), 0.0)
                e_in = jnp.exp(ac)
                f_out = jnp.exp(alast - ac)
                e_last = jnp.exp(alast)
                m = _dot_nt(cm, bm) * lm
                y_off = jnp.where(hm, _dot_nt(cm, sp), 0.0) * e_in
                dm = _dot_nt(dym, xd)
                dxd = _dot_tn(m, dym)
                dg = dm * lm
                dye = dym * e_in
                dc[g] = dc[g] + _dot(dg, bm) + _dot(dye, sp)
                db[g] = db[g] + _dot_tn(dg, cm)
                qm = dm * m
                dac = jnp.sum(qm, axis=1, keepdims=True) + jnp.sum(dym * y_off, axis=1, keepdims=True)
                dar = -jnp.sum(qm, axis=0, keepdims=True)
                dxf = jnp.where(hm, _dot_nt(bm, dsp), 0.0)
                db[g] = db[g] + _dot(xd * f_out, dsp)
                dxd = dxd + dxf * f_out
                df = jnp.sum(dxf * xd, axis=1, keepdims=True) * f_out
                dac = dac - df
                s_last = jnp.sum(df, axis=0, keepdims=True)
                ss = jnp.sum(jnp.where(hrow, dsp * sp, 0.0), axis=1, keepdims=True)
                s_last = s_last + e_last * jnp.sum(ss, axis=0, keepdims=True)
                dac = dac + jnp.where(rowl == l - 1, s_last, 0.0)
                ds_new = ds_new + _dot_tn(dye, cm)
                decay = jnp.where(hrow, e_last, decay)
                dxpair = dxpair + dxd * dtc + dym * dh
                dt_x = dt_x + jnp.where(lane == DT_LANE + h, jnp.sum(dxd * xm, axis=1, keepdims=True), 0.0)
                dsum = jnp.sum(jnp.sum(dym * xm, axis=1, keepdims=True), axis=0, keepdims=True)
                dd_row = dd_row + jnp.where(lane1 == DT_LANE + h, dsum, 0.0)
                da_col = da_col + jnp.where(lane == DT_LANE + h, dac, 0.0)
                da_row = da_row + jnp.where(row == DT_LANE + h, dar, 0.0)
            dstates[j] = dsp * decay + ds_new
            dxbc_ref[r, LANE * j:LANE * (j + 1)] = dxpair
        for g in range(2):
            dxbc_ref[r, D_SSD + LANE * g:D_SSD + LANE * (g + 1)] = db[g]
            dxbc_ref[r, D_SSD + SSD_BC + LANE * g:D_SSD + SSD_BC + LANE * (g + 1)] = dc[g]
        dla = _dot_hi_tn(tri, da_col + da_row.T)
        ddt = dt_x + dla * a_row
        dpre = ddt * _sigmoid(pre)
        dtm = (lane >= DT_LANE) & (lane < DT_LANE + SSD_HEADS)
        dtail_ref[r, :] = jnp.where(dtm, dpre, 0.0).astype(MXU)
        dtm1 = (lane1 >= DT_LANE) & (lane1 < DT_LANE + SSD_HEADS)
        dsc_ref[0:1, :] += jnp.where(dtm1, jnp.sum(dpre, axis=0, keepdims=True), 0.0)
        dsc_ref[1:2, :] += jnp.where(dtm1, jnp.sum(dla * dt, axis=0, keepdims=True) * a_row, 0.0)
        dsc_ref[2:3, :] += dd_row
        return dstates

    rev = lambda c: nc // cps - 1 - c
    return _call_after(
        dep, body, (xbc, proj, sc, states, dy), name="ssd_bwd", grid=(nc // cps,),
        in_specs=[pl.BlockSpec((cps * l, N_XBC), lambda c: (rev(c), 0)),
                  pl.BlockSpec((cps * l, LANE), lambda c: (rev(c), O_TAIL // LANE)), _full((8, LANE)),
                  pl.BlockSpec((cps, 3, LANE, LANE), lambda c: (rev(c), 0, 0, 0)),
                  pl.BlockSpec((cps * l, D_SSD), lambda c: (rev(c), 0))],
        out_specs=[pl.BlockSpec((cps * l, N_XBC), lambda c: (rev(c), 0)), pl.BlockSpec((cps * l, LANE), lambda c: (rev(c), 0)),
                   _full((8, LANE))],
        out_shape=[_sds((s, N_XBC)), _sds((s, LANE), MXU), _sds((8, LANE))],
        scratch_shapes=[pltpu.VMEM((3, LANE, LANE), F32)],
        compiler_params=_params(("arbitrary",)),
    )


def _sconv_bwd(proj, w, b, dxbc, dep=None):
    s = proj.shape[0]

    def body(u_ref, w_ref, b_ref, d_ref, du_ref, dw_ref, db_ref):
        u = u_ref[...]
        wv = w_ref[...]
        dpre = d_ref[...] * _dsilu(_sconv_pre(u, wv, b_ref[...]))
        du_ref[...] = (wv[3:4, :] * dpre + wv[2:3, :] * _shift_up(dpre, 1) + wv[1:2, :] * _shift_up(dpre, 2)
                       + wv[0:1, :] * _shift_up(dpre, 3)).astype(MXU)
        for k in range(4):
            dw_ref[k:k + 1, :] = jnp.sum(dpre * _shift_down(u, 3 - k), axis=0, keepdims=True)
        db_ref[...] = jnp.sum(dpre, axis=0, keepdims=True)

    blk = pl.BlockSpec((s, LANE), lambda j: (0, j))
    return _call_after(
        dep, body, (proj, w, b, dxbc), name="sconv_bwd", grid=(N_XBC // LANE,),
        in_specs=[_col(s, O_XBC), pl.BlockSpec((4, LANE), lambda j: (0, j)), pl.BlockSpec((1, LANE), lambda j: (0, j)), blk],
        out_specs=[blk, pl.BlockSpec((4, LANE), lambda j: (0, j)), pl.BlockSpec((1, LANE), lambda j: (0, j))],
        out_shape=[_sds((s, N_XBC), MXU), _sds((4, N_XBC)), _sds((1, N_XBC))],
        compiler_params=_params(("parallel",)),
    )


def _conva_bwd(proj, w, dya, dep=None):
    s = proj.shape[0]

    def body(h_ref, b_ref, c_ref, z_ref, w_ref, d_ref, da_ref, dw_ref):
        ah, ab, acv, az = h_ref[...], b_ref[...], c_ref[...], z_ref[...]
        wv = w_ref[...]
        u = acv * ah
        cv = wv[2:3, :] * u + wv[1:2, :] * _shift_down(u, 1) + wv[0:1, :] * _shift_down(u, 2)
        dy = d_ref[...]
        sz = _silu(az)
        da_ref[1] = (dy * cv * sz).astype(MXU)
        da_ref[3] = (dy * ab * cv * _dsilu(az)).astype(MXU)
        dcv = dy * ab * sz
        du = wv[2:3, :] * dcv + wv[1:2, :] * _shift_up(dcv, 1) + wv[0:1, :] * _shift_up(dcv, 2)
        da_ref[0] = (du * acv).astype(MXU)
        da_ref[2] = (du * ah).astype(MXU)
        for k in range(3):
            dw_ref[k:k + 1, :] = jnp.sum(dcv * _shift_down(u, 2 - k), axis=0, keepdims=True)

    return _call_after(
        dep, body, (proj, proj, proj, proj, w, dya), name="conva_bwd", grid=(D_CONV_A // LANE,),
        in_specs=[_col(s, O_AH), _col(s, O_AB), _col(s, O_AC), _col(s, O_AZ), pl.BlockSpec((3, LANE), lambda j: (0, j)),
                  pl.BlockSpec((s, LANE), lambda j: (0, j))],
        out_specs=[pl.BlockSpec((4, s, LANE), lambda j: (0, 0, j)), pl.BlockSpec((3, LANE), lambda j: (0, j))],
        out_shape=[_sds((4, s, D_CONV_A), MXU), _sds((3, D_CONV_A))],
        compiler_params=_params(("parallel",)),
    )


def _mla_prep_bwd(dq, dk, dv, proj, qn, kvn, rq, rkv, gq, gkv, wq, wkv, cos, sin):
    s = proj.shape[0]
    ts = _tile(s)
    nh = MLA_HEADS

    def body(dq_ref, dk_ref, dv_ref, cqa_ref, ckv_ref, qn_ref, kvn_ref, rq_ref, rkv_ref, gq_ref, gkv_ref,
             wq_ref, wkv_ref, cos_ref, sin_ref, dcqa_ref, dckv_ref, dtail_ref, dwq_ref, dwkv_ref, dgq_ref, dgkv_ref):
        @pl.when(pl.program_id(0) == 0)
        def _():
            dwq_ref[...] = jnp.zeros_like(dwq_ref)
            dwkv_ref[...] = jnp.zeros_like(dwkv_ref)
            dgq_ref[...] = jnp.zeros_like(dgq_ref)
            dgkv_ref[...] = jnp.zeros_like(dgkv_ref)

        cosv = cos_ref[...]
        sinv = sin_ref[...]
        lane = _iota((ts, LANE), 1)
        rope_lanes = (lane >= ROPE_LANE) & (lane < ROPE_LANE + QK_ROPE)

        def unrope(gr):
            return gr * cosv + _rope_swap(gr * sinv)

        dqs, dks, dvs = [], [], []
        dkr = jnp.zeros((ts, LANE), F32)
        for h in range(nh):
            dqs.append(unrope(dq_ref[h] * ATT_SCALE).astype(MXU))
            dkh = dk_ref[h]
            dks.append(jnp.where(lane < QK_NOPE, dkh, 0.0).astype(MXU))
            dkr = dkr + jnp.where(rope_lanes, dkh, 0.0)
            dvs.append(dv_ref[h].astype(MXU))
        dtail_ref[...] = pltpu.roll(jnp.where(rope_lanes, unrope(dkr), 0.0), ROPE_LANE, 1).astype(MXU)
        dq_all = jnp.concatenate(dqs, axis=1)
        dkv_all = jnp.concatenate(dks + dvs, axis=1)
        dwq_ref[...] += _dot_tn(dq_all, qn_ref[...])
        dwkv_ref[...] += _dot_tn(dkv_all, kvn_ref[...])
        dcqa, dgq = _rms_bwd(_dot(dq_all, wq_ref[...]), cqa_ref[...], rq_ref[...], gq_ref[...])
        dckv, dgkv = _rms_bwd(_dot(dkv_all, wkv_ref[...]), ckv_ref[...], rkv_ref[...], gkv_ref[...])
        dcqa_ref[...] = dcqa.astype(MXU)
        dckv_ref[...] = dckv.astype(MXU)
        dgq_ref[...] += dgq
        dgkv_ref[...] += dgkv

    head = pl.BlockSpec((nh, ts, LANE), lambda i: (0, i, 0))
    return pl.pallas_call(
        body, name="mla_prep_bwd", grid=(s // ts,),
        in_specs=[head, head, head,
                  pl.BlockSpec((ts, Q_LORA), lambda i: (i, O_CQA // Q_LORA)),
                  pl.BlockSpec((ts, KV_LORA), lambda i: (i, O_CKV // KV_LORA)),
                  _row(ts, Q_LORA), _row(ts, KV_LORA), _row(ts, 1), _row(ts, 1),
                  _full((1, Q_LORA)), _full((1, KV_LORA)), _full((nh * LANE, Q_LORA)), _full((2 * nh * LANE, KV_LORA)),
                  _row(ts, LANE), _row(ts, LANE)],
        out_specs=[_row(ts, Q_LORA), _row(ts, KV_LORA), _row(ts, LANE), _full((nh * LANE, Q_LORA)),
                   _full((2 * nh * LANE, KV_LORA)), _full((1, Q_LORA)), _full((1, KV_LORA))],
        out_shape=[_sds((s, Q_LORA), MXU), _sds((s, KV_LORA), MXU), _sds((s, LANE), MXU), _sds((nh * LANE, Q_LORA)),
                   _sds((2 * nh * LANE, KV_LORA)), _sds((1, Q_LORA)), _sds((1, KV_LORA))],
        compiler_params=_params(("arbitrary",)),
    )(dq, dk, dv, proj, proj, qn, kvn, rq, rkv, gq, gkv, wq, wkv, cos, sin)


def _inproj_bwd(da4, dsz, dxbc_in, dcqa, dckv, dcz, dtail_a, dtail_b, w, x, rstd, g, dout):
    s = x.shape[0]
    ts = _tile(s)

    def body(da_ref, dsz_ref, dxbc_ref, dcqa_ref, dckv_ref, dcz_ref, dta_ref, dtb_ref, w_ref, x_ref, r_ref, g_ref, dout_ref,
             dproj_ref, dx_ref, dg_ref):
        @pl.when(pl.program_id(0) == 0)
        def _():
            dg_ref[...] = jnp.zeros_like(dg_ref)

        dproj = jnp.concatenate(
            [da_ref[0], da_ref[1], da_ref[2], da_ref[3], dxbc_ref[...], dsz_ref[...], dcqa_ref[...], dckv_ref[...],
             dcz_ref[...], dta_ref[...] + dtb_ref[...]], axis=1)
        dproj_ref[...] = dproj
        dh = _dot_nt(dproj, w_ref[...])
        dx, dg = _rms_bwd(dh, x_ref[...], r_ref[...], g_ref[...])
        dx_ref[...] = dout_ref[...] + dx
        dg_ref[...] += dg

    return pl.pallas_call(
        body, name="inproj_bwd", grid=(s // ts,),
        in_specs=[pl.BlockSpec((4, ts, D_CONV_A), lambda i: (0, i, 0)), _row(ts, D_SSD), _row(ts, N_XBC), _row(ts, Q_LORA),
                  _row(ts, KV_LORA), _row(ts, D_MLA), _row(ts, LANE), _row(ts, LANE), _full((D_MODEL, NCOL)),
                  _row(ts, D_MODEL), _row(ts, 1), _full((1, D_MODEL)), _row(ts, D_MODEL)],
        out_specs=[_row(ts, NCOL), _row(ts, D_MODEL), _full((1, D_MODEL))],
        out_shape=[_sds((s, NCOL), MXU), _sds((s, D_MODEL)), _sds((1, D_MODEL))],
        compiler_params=_params(("arbitrary",)),
    )(da4, dsz, dxbc_in, dcqa, dckv, dcz, dtail_a, dtail_b, w, x, rstd, g, dout)


DWIN_BLOCK = 640


def _dwin(h, dproj):
    s = h.shape[0]

    def body(h_ref, d_ref, o_ref):
        o_ref[...] = _dot_tn(h_ref[...], d_ref[...])

    return pl.pallas_call(
        body, name="dwin", grid=(NCOL // DWIN_BLOCK,),
        in_specs=[_full((s, D_MODEL)), pl.BlockSpec((s, DWIN_BLOCK), lambda j: (0, j))],
        out_specs=pl.BlockSpec((D_MODEL, DWIN_BLOCK), lambda j: (0, j)),
        out_shape=_sds((D_MODEL, NCOL)),
        compiler_params=_params(("parallel",)),
    )(h, dproj)


def _adamw(w, g, m, v):
    bc1 = 1.0 - ADAM_B1 ** ADAM_STEP
    bc2 = 1.0 - ADAM_B2 ** ADAM_STEP

    def body(w_ref, g_ref, m_ref, v_ref, d_ref, mo_ref, vo_ref):
        gv = g_ref[...]
        mn = ADAM_B1 * m_ref[...] + (1.0 - ADAM_B1) * gv
        vn = ADAM_B2 * v_ref[...] + (1.0 - ADAM_B2) * (gv * gv)
        mo_ref[...] = mn
        vo_ref[...] = vn
        d_ref[...] = -ADAM_LR * ((mn / bc1) / (jnp.sqrt(vn / bc2) + ADAM_EPS) + ADAM_WD * w_ref[...])

    if w.ndim == 2:
        grid, blk = (1,), pl.BlockSpec(w.shape, lambda i: (0, 0))
    else:
        grid, blk = (w.shape[0],), pl.BlockSpec((1,) + w.shape[1:], lambda i: (i, 0, 0))
    return pl.pallas_call(
        body, name="adamw", grid=grid,
        in_specs=[blk] * 4, out_specs=[blk] * 3, out_shape=[_sds(w.shape)] * 3,
        compiler_params=_params(("parallel",)),
    )(w, g, m, v)


COL_MOVES = ((0, 0, 1024), (1024, O_SZ, 384), (1408, O_XBC, 896), (2304, O_TAIL + DT_LANE, 6), (2310, O_CQA, 256),
             (2566, O_CKV, 128), (2694, O_TAIL, 32), (2726, O_CZ, 384))


def _move_cols(w, moves, width):
    out = None
    for src, dst, n in moves:
        piece = jnp.pad(w[..., src:src + n], [(0, 0)] * (w.ndim - 1) + [(dst, width - dst - n)])
        out = piece if out is None else out + piece
    return out


def _perm_cols(w):
    return _move_cols(w, COL_MOVES, NCOL)


def _unperm_cols(g):
    return _move_cols(g, [(dst, src, n) for src, dst, n in COL_MOVES], IN_COLS)


def _wq_layout(wt):
    return jnp.pad(wt.reshape(MLA_HEADS, QK_NOPE + QK_ROPE, Q_LORA), ((0, 0), (0, 32), (0, 0))).reshape(MLA_HEADS * LANE, Q_LORA)


def _wq_unlayout(g):
    return g.reshape(MLA_HEADS, LANE, Q_LORA)[:, :QK_NOPE + QK_ROPE].reshape(MLA_HEADS * (QK_NOPE + QK_ROPE), Q_LORA)


def _wkv_layout(wt):
    t = wt.reshape(MLA_HEADS, 2, 64, KV_LORA).transpose(1, 0, 2, 3)
    return jnp.pad(t, ((0, 0), (0, 0), (0, 64), (0, 0))).reshape(2 * MLA_HEADS * LANE, KV_LORA)


def _wkv_unlayout(g):
    t = g.reshape(2, MLA_HEADS, LANE, KV_LORA)[:, :, :64]
    return t.transpose(1, 0, 2, 3).reshape(MLA_HEADS * LANE, KV_LORA)


def _rope_tables(positions):
    inv_freq = ROPE_BASE ** (-jnp.arange(0, QK_ROPE, 2, dtype=F32) / QK_ROPE)
    ang = positions.astype(F32)[:, None] * inv_freq
    cos, sin = jnp.cos(ang), jnp.sin(ang)
    s = positions.shape[0]
    one, zero = jnp.ones((s, ROPE_LANE), F32), jnp.zeros((s, ROPE_LANE), F32)
    cos_t = jnp.concatenate([one, cos, cos, one[:, :32]], axis=1)
    sin_t = jnp.concatenate([zero, -sin, sin, zero[:, :32]], axis=1)
    return cos_t, sin_t


def _ssd_scalars(dt_bias, a_log, d_skip):
    return jnp.pad(jnp.stack([dt_bias, a_log, d_skip]), ((0, 5), (DT_LANE, LANE - DT_LANE - SSD_HEADS)))


def _layer_fwd(x, lw, cos, sin, dep=None, late=None):
    proj, h, rstd = _inproj_fwd(x, lw["norm_g"], lw["w_in"], dep)
    ya = _conva_fwd(proj, lw["conv_a_w"])
    xbc = _sconv_fwd(proj, lw["ssd_conv_w"], lw["ssd_conv_b"])
    y_ssd, states = _ssd_fwd(xbc, proj, lw["sc"])
    if late is not None:
        lw = {**lw, **late(ya, y_ssd)}
    q, k, v, qn, kvn, rq, rkv = _mla_prep_fwd(proj, lw["gq"], lw["gkv"], lw["wq"], lw["wkv"], cos, sin)
    o, lse = _attn_fwd(q, k, v)
    x_out, y = _outproj_fwd(x, proj, ya, y_ssd, o, lw["g_ssd"], lw["w_out"])
    saved = dict(x=x, proj=proj, h=h, rstd=rstd, xbc=xbc, y_ssd=y_ssd, states=states, q=q, k=k, v=v, qn=qn, kvn=kvn,
                 rq=rq, rkv=rkv, o=o, lse=lse, y=y)
    return x_out, saved, lw


def _layer_bwd(dout, lw, sv, cos, sin, rs=None):
    tok = lambda: None if rs is None else rs["h"]["token"]
    dya, dys, dsz, d_o, dcz, dg_ssd, dw_out = _outproj_bwd(dout, sv["y"], lw["w_out"], sv["proj"], sv["y_ssd"], sv["o"],
                                                            lw["g_ssd"], tok())
    if rs is not None:
        rs = _rs_add_mine(rs, [dya])
    dq, dk, dv = _attn_bwd(sv["q"], sv["k"], sv["v"], sv["o"], d_o, sv["lse"], tok())
    dxbc, dtail_s, dsc = _ssd_bwd(sv["xbc"], sv["proj"], lw["sc"], sv["states"], dys, tok())
    da4, dw_conva = _conva_bwd(sv["proj"], lw["conv_a_w"], dya, tok())
    if rs is not None:
        rs = _rs_add_chips(rs, [dq, dxbc, da4])
    du, dw_sconv, db_sconv = _sconv_bwd(sv["proj"], lw["ssd_conv_w"], lw["ssd_conv_b"], dxbc, tok())
    dcqa, dckv, dtail_m, dwq, dwkv, dgq, dgkv = _mla_prep_bwd(
        dq, dk, dv, sv["proj"], sv["qn"], sv["kvn"], sv["rq"], sv["rkv"], lw["gq"], lw["gkv"], lw["wq"], lw["wkv"], cos, sin)
    dproj, dx, dg = _inproj_bwd(da4, dsz, du, dcqa, dckv, dcz, dtail_s, dtail_m, lw["w_in"], sv["x"], sv["rstd"],
                                lw["norm_g"], dout)
    reduced = None if rs is None else _rs_end(rs, [du, dcqa, dx])
    dw_in = _dwin(sv["h"], dproj)
    grads = dict(norm_g=dg, w_in=dw_in, conv_a_w=dw_conva, ssd_conv_w=dw_sconv, ssd_conv_b=db_sconv, sc=dsc,
                 g_ssd=dg_ssd, gq=dgq, wq=dwq, gkv=dgkv, wkv=dwkv, w_out=dw_out)
    return dx, grads, reduced


ANY = pl.BlockSpec(memory_space=pl.ANY)
N_CHIPS = 4
N_DEV = 8


def _place():
    return lax.axis_index("x"), lax.axis_index("y"), lax.axis_index("c")


HBM_SPEC = pl.BlockSpec(memory_space=pltpu.HBM)
SEM_SPEC = pl.BlockSpec(memory_space=pltpu.SEMAPHORE)
PAYLOAD = jnp.bfloat16


def _hbm(a):
    return pltpu.with_memory_space_constraint(a, pltpu.HBM)


def _run_plan(plan, srcs, lands, send_sems, recv_sems, start, wait):
    copies = plan(srcs, lands)
    if start:
        for i, (src, dst, _, to) in enumerate(copies):
            pltpu.make_async_remote_copy(src_ref=src, dst_ref=dst, send_sem=send_sems.at[i], recv_sem=recv_sems.at[i],
                                         device_id=to, device_id_type=MESH_T).start()
    if wait:
        for i, (src, _, arrives, to) in enumerate(copies):
            cp = pltpu.make_async_remote_copy(src_ref=src, dst_ref=arrives, send_sem=send_sems.at[i],
                                              recv_sem=recv_sems.at[i], device_id=to, device_id_type=MESH_T)
            cp.wait_send()
            cp.wait_recv()


def _exchange_fused(name, plan, n_copies, srcs, land_shapes):
    ns, nl = len(srcs), len(land_shapes)

    def body(*refs):
        _run_plan(plan, refs[:ns], refs[ns:ns + nl], refs[ns + nl], refs[ns + nl + 1], True, True)

    return pl.pallas_call(
        body, name=name, in_specs=[ANY] * ns, out_specs=[ANY] * nl, out_shape=list(land_shapes),
        scratch_shapes=[pltpu.SemaphoreType.DMA((n_copies,)), pltpu.SemaphoreType.DMA((n_copies,))],
    )(*srcs)


def _exchange_start(name, plan, n_copies, srcs, land_shapes, deps):
    ns, nl = len(srcs), len(land_shapes)
    n_in = ns + nl + len(deps)

    def body(*refs):
        send_sems, recv_sems = refs[n_in], refs[n_in + 1]
        token = refs[-1]
        _run_plan(plan, refs[:ns], refs[ns:ns + nl], send_sems, recv_sems, True, False)
        token[...] = jnp.zeros_like(token)

    thru = [pltpu.HBM(a.shape, a.dtype) for a in srcs] + [pltpu.HBM(a.shape, a.dtype) for a in land_shapes]
    outs = pl.pallas_call(
        body, name=name,
        out_shape=(pltpu.SemaphoreType.DMA((n_copies,)), pltpu.SemaphoreType.DMA((n_copies,)), *thru, _sds((8, LANE))),
        in_specs=[HBM_SPEC] * (ns + nl) + [ANY] * len(deps),
        out_specs=(SEM_SPEC, SEM_SPEC, *[HBM_SPEC] * (ns + nl), pl.BlockSpec(memory_space=pltpu.VMEM)),
        input_output_aliases={i: 2 + i for i in range(ns + nl)},
        compiler_params=pltpu.CompilerParams(has_side_effects=pltpu.SideEffectType.DATAFLOW_SIDE_EFFECTING),
    )(*[_hbm(a) for a in srcs], *[_hbm(lax.empty(a.shape, a.dtype)) for a in land_shapes], *deps)
    return (outs[0], outs[1]), list(outs[2:2 + ns]), list(outs[2 + ns:2 + ns + nl]), outs[-1]


def _exchange_wait(name, plan, sems, srcs, lands, after):
    ns, nl = len(srcs), len(lands)

    def body(*refs):
        _run_plan(plan, refs[:ns], refs[ns:ns + nl], refs[ns + nl], refs[ns + nl + 1], False, True)

    outs = pl.pallas_call(
        body, name=name,
        out_shape=[pltpu.HBM(a.shape, a.dtype) for a in list(srcs) + list(lands)],
        in_specs=[HBM_SPEC] * (ns + nl) + [SEM_SPEC, SEM_SPEC] + [ANY] * len(after), out_specs=[HBM_SPEC] * (ns + nl),
        input_output_aliases={i: i for i in range(ns + nl)},
        compiler_params=pltpu.CompilerParams(has_side_effects=pltpu.SideEffectType.DATAFLOW_SIDE_EFFECTING),
    )(*srcs, *lands, sems[0], sems[1], *after)
    return list(outs[:ns]), list(outs[ns:])


def _xchg_begin(name, plan, n_copies, srcs, land_shapes, split, deps=()):
    if not split:
        return dict(split=False, srcs=list(srcs), lands=_exchange_fused(name, plan, n_copies, srcs, land_shapes),
                    token=jnp.zeros((8, LANE), F32))
    sems, srcs_t, lands_t, token = _exchange_start(name + "_start", plan, n_copies, srcs, land_shapes, list(deps))
    return dict(split=True, name=name, plan=plan, sems=sems, srcs=srcs_t, lands=lands_t, token=token)


def _xchg_end(h, after):
    if not h["split"]:
        return h["srcs"], h["lands"]
    return _exchange_wait(h["name"] + "_wait", h["plan"], h["sems"], h["srcs"], h["lands"], after)


def _other_chips():
    x, y, c = _place()
    return [(1 - x, y), (x, 1 - y), (1 - x, 1 - y)]


def _gather_plan(srcs, lands):
    x, y, c = _place()
    me = 2 * x + y
    return [(srcs[a], lands[a].at[me], lands[a].at[2 * cx + cy], (cx, cy, c))
            for (cx, cy) in _other_chips() for a in range(len(srcs))]


def _gather_begin(shards, split, tag, deps=()):
    shapes = [_sds((N_CHIPS,) + a.shape, a.dtype) for a in shards]
    return _xchg_begin(f"gather_{tag}", _gather_plan, 3 * len(shards), shards, shapes, split, deps)


def _gather_end(h, after):
    shards, lands = _xchg_end(h, after)
    me = 2 * lax.axis_index("x") + lax.axis_index("y")
    return [lax.dynamic_update_index_in_dim(g, s, me, 0) for g, s in zip(lands, shards)]


def _swap_plan(srcs, lands):
    x, y, c = _place()
    return [(srcs[a].at[:, 1 - c], lands[a], lands[a], (x, y, 1 - c)) for a in range(len(srcs))]


def _chips_plan(srcs, lands):
    x, y, c = _place()
    me = 2 * x + y
    return [(srcs[a].at[2 * cx + cy], lands[a].at[me], lands[a].at[2 * cx + cy], (cx, cy, c))
            for (cx, cy) in _other_chips() for a in range(len(srcs))]


def _share_plan(srcs, lands):
    x, y, c = _place()
    return [(srcs[a], lands[a].at[c], lands[a].at[1 - c], (x, y, 1 - c)) for a in range(len(srcs))]


def _allreduce_small(slab, dep=None):
    r = slab.shape[0]

    def body(s_ref, o_ref, gath, send_sems, recv_sems):
        x, y, c = _place()
        me = 4 * x + 2 * y + c
        gath[me] = s_ref[...]
        cps = []
        for rel in range(1, N_DEV):
            px = 1 - x if rel & 4 else x
            py = 1 - y if rel & 2 else y
            pc = 1 - c if rel & 1 else c
            cp = pltpu.make_async_remote_copy(src_ref=s_ref, dst_ref=gath.at[me], send_sem=send_sems.at[rel - 1],
                                              recv_sem=recv_sems.at[rel - 1], device_id=(px, py, pc), device_id_type=MESH_T)
            cp.start()
            cps.append(cp)
        for cp in cps:
            cp.wait()
        acc = gath[0]
        for d in range(1, N_DEV):
            acc = acc + gath[d]
        o_ref[...] = acc

    vm = pl.BlockSpec(memory_space=pltpu.VMEM)
    return _call_after(
        dep, body, (slab,), name="allreduce_small", in_specs=[vm], out_specs=vm, out_shape=_sds((r, LANE)),
        scratch_shapes=[pltpu.VMEM((N_DEV, r, LANE), F32), pltpu.SemaphoreType.DMA((N_DEV - 1,)),
                        pltpu.SemaphoreType.DMA((N_DEV - 1,))],
    )


def _add_mine(g4, recv, half):
    _, _, rh, c = g4.shape

    def body(h_ref, g_ref, r_ref, o_ref):
        o_ref[0] = (g_ref[0, 0] + r_ref[0]).astype(o_ref.dtype)

    return pl.pallas_call(
        body, name="add_mine",
        grid_spec=pltpu.PrefetchScalarGridSpec(
            num_scalar_prefetch=1, grid=(N_CHIPS,),
            in_specs=[pl.BlockSpec((1, 1, rh, c), lambda j, h: (j, h[0], 0, 0)), pl.BlockSpec((1, rh, c), lambda j, h: (j, 0, 0))],
            out_specs=pl.BlockSpec((1, rh, c), lambda j, h: (j, 0, 0))),
        out_shape=_sds((N_CHIPS, rh, c), PAYLOAD),
        compiler_params=_params(("parallel",)),
    )(half, g4, recv)


def _add_chips(e, p, me):
    _, rh, c = e.shape

    def body(m_ref, e_ref, p_ref, o_ref):
        own = p_ref[0].astype(F32)
        acc = None
        for s in range(N_CHIPS):
            t = jnp.where(m_ref[0] == s, own, e_ref[s].astype(F32))
            acc = t if acc is None else acc + t
        o_ref[...] = acc

    return pl.pallas_call(
        body, name="add_chips",
        grid_spec=pltpu.PrefetchScalarGridSpec(
            num_scalar_prefetch=1, grid=(1,),
            in_specs=[pl.BlockSpec((N_CHIPS, rh, c), lambda i, m: (0, 0, 0)), pl.BlockSpec((1, rh, c), lambda i, m: (m[0], 0, 0))],
            out_specs=pl.BlockSpec((rh, c), lambda i, m: (0, 0))),
        out_shape=_sds((rh, c)),
        compiler_params=_params(("arbitrary",)),
    )(me, e, p)


def _rs_begin(gs, split, tag):
    g4 = [g.reshape(N_CHIPS, 2, g.shape[0] // (2 * N_CHIPS), g.shape[1]) for g in gs]
    h = _xchg_begin(f"rs_swap_{tag}", _swap_plan, len(gs), g4, [_sds((N_CHIPS,) + g.shape[2:]) for g in g4], split)
    return dict(h=h, split=split, tag=tag, shapes=[g.shape for g in gs])


def _rs_add_mine(st, after):
    g4, recv = _xchg_end(st["h"], after)
    half = jnp.reshape(lax.axis_index("c"), (1,)).astype(jnp.int32)
    ps = [_add_mine(g, r, half) for g, r in zip(g4, recv)]
    st["h"] = _xchg_begin(f"rs_chips_{st['tag']}", _chips_plan, 3 * len(ps), ps, [_sds(p.shape, p.dtype) for p in ps], st["split"])
    return st


def _rs_add_chips(st, after):
    ps, es = _xchg_end(st["h"], after)
    me = jnp.reshape(2 * lax.axis_index("x") + lax.axis_index("y"), (1,)).astype(jnp.int32)
    fs = [_add_chips(e, p, me) for e, p in zip(es, ps)]
    st["h"] = _xchg_begin(f"rs_share_{st['tag']}", _share_plan, len(fs), fs, [_sds((2,) + f.shape) for f in fs], st["split"])
    return st


def _rs_end(st, after):
    fs, ss = _xchg_end(st["h"], after)
    c = lax.axis_index("c")
    return [lax.dynamic_update_index_in_dim(s, f, c, 0).reshape(shp[0] // N_CHIPS, shp[1])
            for s, f, shp in zip(ss, fs, st["shapes"])]


WEIGHTS = ["norm_g", "w_in", "conv_a_w", "ssd_conv_w", "ssd_conv_b", "ssd_dt_bias", "ssd_a_log", "ssd_d", "ssd_norm_g",
           "mla_q_norm_g", "w_qb", "mla_kv_norm_g", "w_kvb", "w_out", "final_norm_g"]
BIG = ["w_in", "w_qb", "w_kvb", "w_out"]
SLAB_ROWS = 128
SMALL_ROWS = 72


def _to_slab(parts, rows):
    flat = jnp.concatenate([p.reshape(-1) for p in parts])
    return jnp.pad(flat, (0, rows * LANE - flat.shape[0])).reshape(rows, LANE)


def _from_slab(slab, shapes):
    flat = slab.reshape(-1)
    out, off = [], 0
    for shp in shapes:
        n = int(np.prod(shp))
        out.append(flat[off:off + n].reshape(shp))
        off += n
    return out


def kernel(x, positions, norm_g, w_in, conv_a_w, ssd_conv_w, ssd_conv_b, ssd_dt_bias, ssd_a_log, ssd_d, ssd_norm_g, mla_q_norm_g, w_qb, mla_kv_norm_g, w_kvb, w_out, final_norm_g, loss_target, m_norm_g, m_w_in, m_conv_a_w, m_ssd_conv_w, m_ssd_conv_b, m_ssd_dt_bias, m_ssd_a_log, m_ssd_d, m_ssd_norm_g, m_mla_q_norm_g, m_w_qb, m_mla_kv_norm_g, m_w_kvb, m_w_out, m_final_norm_g, v_norm_g, v_w_in, v_conv_a_w, v_ssd_conv_w, v_ssd_conv_b, v_ssd_dt_bias, v_ssd_a_log, v_ssd_d, v_ssd_norm_g, v_mla_q_norm_g, v_w_qb, v_mla_kv_norm_g, v_w_kvb, v_w_out, v_final_norm_g):
    w = dict(norm_g=norm_g, w_in=w_in, conv_a_w=conv_a_w, ssd_conv_w=ssd_conv_w, ssd_conv_b=ssd_conv_b,
             ssd_dt_bias=ssd_dt_bias, ssd_a_log=ssd_a_log, ssd_d=ssd_d, ssd_norm_g=ssd_norm_g, mla_q_norm_g=mla_q_norm_g,
             w_qb=w_qb, mla_kv_norm_g=mla_kv_norm_g, w_kvb=w_kvb, w_out=w_out, final_norm_g=final_norm_g)
    mom = dict(norm_g=m_norm_g, w_in=m_w_in, conv_a_w=m_conv_a_w, ssd_conv_w=m_ssd_conv_w, ssd_conv_b=m_ssd_conv_b,
               ssd_dt_bias=m_ssd_dt_bias, ssd_a_log=m_ssd_a_log, ssd_d=m_ssd_d, ssd_norm_g=m_ssd_norm_g,
               mla_q_norm_g=m_mla_q_norm_g, w_qb=m_w_qb, mla_kv_norm_g=m_mla_kv_norm_g, w_kvb=m_w_kvb, w_out=m_w_out,
               final_norm_g=m_final_norm_g)
    var = dict(norm_g=v_norm_g, w_in=v_w_in, conv_a_w=v_conv_a_w, ssd_conv_w=v_ssd_conv_w, ssd_conv_b=v_ssd_conv_b,
               ssd_dt_bias=v_ssd_dt_bias, ssd_a_log=v_ssd_a_log, ssd_d=v_ssd_d, ssd_norm_g=v_ssd_norm_g,
               mla_q_norm_g=v_mla_q_norm_g, w_qb=v_w_qb, mla_kv_norm_g=v_mla_kv_norm_g, w_kvb=v_w_kvb, w_out=v_w_out,
               final_norm_g=v_final_norm_g)
    chip = 2 * lax.axis_index("x") + lax.axis_index("y")

    def early_shard(l, zero):
        pack = jnp.pad(conv_a_w[l], ((0, 5), (0, 192))) + jnp.pad(ssd_conv_w[l], ((3, 1), (0, 32)))
        return [(_perm_cols(w_in[l]) + zero).astype(MXU), pack + zero]

    def late_shard(l, zero):
        return [(w_out[l] + zero).astype(MXU), (w_qb[l].T + zero).astype(MXU), (w_kvb[l].T + zero).astype(MXU)]

    def early_weights(l, gathered):
        g_in, g_conv = gathered
        return dict(
            norm_g=norm_g[l][None], w_in=g_in.reshape(D_MODEL, NCOL),
            conv_a_w=jnp.concatenate([g_conv[j, 0:3, 0:64] for j in range(N_CHIPS)], axis=1),
            ssd_conv_w=jnp.concatenate([g_conv[j, 3:7, 0:224] for j in range(N_CHIPS)], axis=1),
            ssd_conv_b=ssd_conv_b[l][None], sc=_ssd_scalars(ssd_dt_bias[l], ssd_a_log[l], ssd_d[l]),
            g_ssd=ssd_norm_g[l][None], gq=mla_q_norm_g[l][None], gkv=mla_kv_norm_g[l][None])

    def late_weights(gathered):
        g_out, g_qb, g_kvb = gathered
        return dict(wq=_wq_layout(g_qb.reshape(MLA_HEADS * 96, Q_LORA)), wkv=_wkv_layout(g_kvb.reshape(MLA_HEADS * LANE, KV_LORA)),
                    w_out=g_out.reshape(D_MODEL, D_MODEL))

    def large_grads(g):
        wq = jnp.pad(_wq_unlayout(g["wq"]).reshape(N_CHIPS, 144, Q_LORA), ((0, 0), (0, 16), (0, 0)))
        return [g["w_in"], g["w_out"], wq.reshape(N_CHIPS * 160, Q_LORA), _wkv_unlayout(g["wkv"])]

    gather_a0 = _gather_begin(early_shard(0, 0.0), True, "a0")
    zero = gather_a0["token"][0, 0]
    cos, sin = _rope_tables(positions[0] + zero.astype(jnp.int32))
    late0, shards1 = late_shard(0, zero), early_shard(1, zero) + late_shard(1, zero)
    opt_in = {nm: [w[nm], mom[nm] + zero, var[nm] + zero] for nm in BIG}
    lw0 = early_weights(0, _gather_end(gather_a0, [cos, sin] + late0 + shards1 + [a for nm in BIG for a in opt_in[nm][1:]]))
    gather_b0 = _gather_begin(late0, True, "b0")
    gather_1 = _gather_begin(shards1, True, "1", [gather_b0["token"]])
    x1, sv0, lw0 = _layer_fwd(x[0], lw0, cos, sin, gather_1["token"],
                              lambda ya, y_ssd: late_weights(_gather_end(gather_b0, [ya, y_ssd])))
    g1 = _gather_end(gather_1, [x1])
    x2, sv1, lw1 = _layer_fwd(x1, {**early_weights(1, g1[:2]), **late_weights(g1[2:])}, cos, sin)
    dx, dgf, loss = _loss_head(x2, final_norm_g[None], loss_target[0])

    dx, lg1, _ = _layer_bwd(dx, lw1, sv1, cos, sin)
    grad_x, lg0, red1 = _layer_bwd(dx, lw0, sv0, cos, sin, _rs_begin(large_grads(lg1), True, 1))
    rs0 = _rs_begin(large_grads(lg0), True, 0)
    lg = [lg0, lg1]
    grad = {}

    small_names = ["norm_g", "conv_a_w", "ssd_conv_w", "ssd_conv_b", "sc", "g_ssd", "gq", "gkv"]
    parts = [loss[0, 0:1], dgf]
    for l in range(DEPTH):
        parts += [lg[l][nm][:3, DT_LANE:DT_LANE + SSD_HEADS] if nm == "sc" else lg[l][nm] for nm in small_names]
    shapes = [(1,), (D_MODEL,)] + [(D_MODEL,), (3, D_CONV_A), (4, N_XBC), (N_XBC,), (3, SSD_HEADS), (D_SSD,), (Q_LORA,), (KV_LORA,)] * DEPTH
    red_slab = _allreduce_small(_to_slab(parts, SLAB_ROWS), rs0["h"]["token"])
    rs0 = _rs_add_mine(rs0, [red_slab])
    red = _from_slab(red_slab + rs0["h"]["token"][0, 0], shapes)
    loss_out = red[0][0]
    grad["final_norm_g"] = red[1]
    per = [red[2 + 8 * l:10 + 8 * l] for l in range(DEPTH)]
    grad["norm_g"] = jnp.stack([per[l][0] for l in range(DEPTH)])
    grad["conv_a_w"] = lax.dynamic_slice_in_dim(jnp.stack([per[l][1] for l in range(DEPTH)]), chip * 64, 64, axis=2)
    grad["ssd_conv_w"] = lax.dynamic_slice_in_dim(jnp.stack([per[l][2] for l in range(DEPTH)]), chip * 224, 224, axis=2)
    grad["ssd_conv_b"] = jnp.stack([per[l][3] for l in range(DEPTH)])
    grad["ssd_dt_bias"] = jnp.stack([per[l][4][0] for l in range(DEPTH)])
    grad["ssd_a_log"] = jnp.stack([per[l][4][1] for l in range(DEPTH)])
    grad["ssd_d"] = jnp.stack([per[l][4][2] for l in range(DEPTH)])
    grad["ssd_norm_g"] = jnp.stack([per[l][5] for l in range(DEPTH)])
    grad["mla_q_norm_g"] = jnp.stack([per[l][6] for l in range(DEPTH)])
    grad["mla_kv_norm_g"] = jnp.stack([per[l][7] for l in range(DEPTH)])

    delta, new_m, new_v = {}, {}, {}
    small = [nm for nm in WEIGHTS if nm not in BIG]
    sshapes = [w[nm].shape for nm in small]
    d, mo, vo = _adamw(_to_slab([w[nm] for nm in small], SMALL_ROWS), _to_slab([grad[nm] for nm in small], SMALL_ROWS),
                       _to_slab([mom[nm] for nm in small], SMALL_ROWS), _to_slab([var[nm] for nm in small], SMALL_ROWS))
    small_out = list(zip(small, _from_slab(d, sshapes), _from_slab(mo, sshapes), _from_slab(vo, sshapes)))
    for nm, dv, mv, vv in small_out:
        delta[nm], new_m[nm], new_v[nm] = dv, mv, vv

    red0 = _rs_end(_rs_add_chips(rs0, [a for row in small_out for a in row[1:]] + [grad[nm] for nm in small]), [])
    r_in, r_out, r_qb, r_kvb = [jnp.stack([a, b]) for a, b in zip(red0, red1)]
    grad.update(w_in=_unperm_cols(r_in), w_out=r_out, w_qb=jnp.swapaxes(r_qb[:, :144], 1, 2), w_kvb=jnp.swapaxes(r_kvb, 1, 2))
    for nm in BIG:
        delta[nm], new_m[nm], new_v[nm] = _adamw(opt_in[nm][0], grad[nm], opt_in[nm][1], opt_in[nm][2])

    return (loss_out, grad_x[None], *[grad[nm] for nm in WEIGHTS], *[delta[nm] for nm in WEIGHTS],
            *[new_m[nm] for nm in WEIGHTS], *[new_v[nm] for nm in WEIGHTS])
```

```python
import functools
import math

import numpy as np
import jax
import jax.numpy as jnp
from jax import lax
from jax.experimental import pallas as pl
from jax.experimental.pallas import tpu as pltpu

F32 = jnp.float32
MXU = jnp.bfloat16

D_MODEL = 1024
DEPTH = 2
D_CONV_A = 256
D_SSD = 384
SSD_HEADS = 6
SSD_BC = 256
SSD_CHUNK = 128
SSD_CHUNKS_PER_STEP = 2
SSD_NORM_EPS = 1e-5
MLA_HEADS = 6
Q_LORA = 256
KV_LORA = 128
QK_NOPE = 64
QK_ROPE = 32
V_DIM = 64
D_MLA = 384
ROPE_BASE = 10000.0
NORM_EPS = 1e-6
IN_COLS = 3110
LANE = 128

O_AH, O_AB, O_AC, O_AZ = 0, 256, 512, 768
O_XBC = 1024
O_SZ = 1920
O_CQA = 2304
O_CKV = 2560
O_CZ = 2688
O_TAIL = 3072
NCOL = 3200
N_XBC = D_SSD + 2 * SSD_BC
DT_LANE = 32
ROPE_LANE = 64

ADAM_LR, ADAM_B1, ADAM_B2, ADAM_EPS, ADAM_WD, ADAM_STEP = 0.001, 0.9, 0.999, 1e-08, 0.01, 10

VMEM_LIMIT = 56 * 1024 * 1024
MESH_T = pl.DeviceIdType.MESH


def _dot(a, b):
    return jnp.dot(a.astype(MXU), b.astype(MXU), preferred_element_type=F32)


def _dot_nt(a, b):
    return lax.dot_general(a.astype(MXU), b.astype(MXU), (((1,), (1,)), ((), ())), preferred_element_type=F32)


def _dot_tn(a, b):
    return lax.dot_general(a.astype(MXU), b.astype(MXU), (((0,), (0,)), ((), ())), preferred_element_type=F32)


def _dot_hi(a, b):
    return jnp.dot(a, b, precision=lax.Precision.HIGHEST, preferred_element_type=F32)


def _dot_hi_tn(a, b):
    return lax.dot_general(a, b, (((0,), (0,)), ((), ())), precision=lax.Precision.HIGHEST, preferred_element_type=F32)


def _sigmoid(z):
    return 1.0 / (1.0 + jnp.exp(-z))


def _silu(z):
    return z * _sigmoid(z)


def _dsilu(z):
    s = _sigmoid(z)
    return s * (1.0 + z * (1.0 - s))


def _softplus(z):
    e = jnp.exp(-jnp.abs(z))
    return jnp.maximum(z, 0.0) + jnp.where(e < 1e-3, e * (1.0 - 0.5 * e), jnp.log(1.0 + e))


def _iota(shape, dim):
    return lax.broadcasted_iota(jnp.int32, shape, dim)


def _shift_down(u, k):
    if k == 0:
        return u
    return jnp.where(_iota(u.shape, 0) >= k, pltpu.roll(u, k, 0), 0.0)


def _shift_up(u, k):
    if k == 0:
        return u
    n = u.shape[0]
    return jnp.where(_iota(u.shape, 0) < n - k, pltpu.roll(u, n - k, 0), 0.0)


def _rope_swap(t):
    lane = _iota(t.shape, 1)
    lo = (lane >= ROPE_LANE) & (lane < ROPE_LANE + 16)
    hi = (lane >= ROPE_LANE + 16) & (lane < ROPE_LANE + 32)
    return jnp.where(lo, pltpu.roll(t, LANE - 16, 1), jnp.where(hi, pltpu.roll(t, 16, 1), 0.0))


def _params(sem=None):
    return pltpu.CompilerParams(dimension_semantics=sem, vmem_limit_bytes=VMEM_LIMIT)


def _full(shape):
    nd = len(shape)
    return pl.BlockSpec(shape, lambda *_: (0,) * nd)


def _sds(shape, dtype=F32):
    return jax.ShapeDtypeStruct(shape, dtype)


def _tile(s):
    return min(256, s)


def _row(ts, w):
    return pl.BlockSpec((ts, w), lambda i: (i, 0))


def _gate_cols(ts, off):
    return pl.BlockSpec((ts, D_SSD), lambda i, _o=off // D_SSD: (i, _o))


def _col(s, off):
    return pl.BlockSpec((s, LANE), lambda j, _o=off // LANE: (0, _o + j))


def _call_after(dep, body, args, *, in_specs, **kw):
    if dep is None:
        return pl.pallas_call(body, in_specs=in_specs, **kw)(*args)
    n = len(args)

    def body_dep(*refs):
        body(*refs[:n], *refs[n + 1:])

    return pl.pallas_call(body_dep, in_specs=list(in_specs) + [pl.BlockSpec(memory_space=pl.ANY)], **kw)(*args, dep)


def _rms(c, g):
    r = lax.rsqrt(jnp.mean(c * c, axis=-1, keepdims=True) + NORM_EPS)
    return c * r * g, r


def _rms_bwd(dn, c, r, g):
    ch = c * r
    dch = dn * g
    dc = r * (dch - ch * jnp.mean(dch * ch, axis=-1, keepdims=True))
    return dc, jnp.sum(dn * ch, axis=0, keepdims=True)


def _inproj_fwd(x, g, w, dep=None):
    s = x.shape[0]
    ts = _tile(s)

    def body(x_ref, g_ref, w_ref, proj_ref, h_ref, r_ref):
        hn, r = _rms(x_ref[...], g_ref[...])
        h = hn.astype(MXU)
        h_ref[...] = h
        r_ref[...] = r
        proj_ref[...] = jnp.dot(h, w_ref[...], preferred_element_type=F32)

    return _call_after(
        dep, body, (x, g, w), name="inproj_fwd", grid=(s // ts,),
        in_specs=[_row(ts, D_MODEL), _full((1, D_MODEL)), _full((D_MODEL, NCOL))],
        out_specs=[_row(ts, NCOL), _row(ts, D_MODEL), _row(ts, 1)],
        out_shape=[_sds((s, NCOL)), _sds((s, D_MODEL), MXU), _sds((s, 1))],
        compiler_params=_params(("parallel",)),
    )


def _conva_fwd(proj, w):
    s = proj.shape[0]

    def body(h_ref, b_ref, c_ref, z_ref, w_ref, y_ref):
        u = c_ref[...] * h_ref[...]
        wv = w_ref[...]
        cv = wv[2:3, :] * u + wv[1:2, :] * _shift_down(u, 1) + wv[0:1, :] * _shift_down(u, 2)
        y_ref[...] = b_ref[...] * cv * _silu(z_ref[...])

    return pl.pallas_call(
        body, name="conva_fwd", grid=(D_CONV_A // LANE,),
        in_specs=[_col(s, O_AH), _col(s, O_AB), _col(s, O_AC), _col(s, O_AZ), pl.BlockSpec((3, LANE), lambda j: (0, j))],
        out_specs=pl.BlockSpec((s, LANE), lambda j: (0, j)),
        out_shape=_sds((s, D_CONV_A)),
        compiler_params=_params(("parallel",)),
    )(proj, proj, proj, proj, w)


def _sconv_pre(u, wv, bv):
    return (wv[3:4, :] * u + wv[2:3, :] * _shift_down(u, 1) + wv[1:2, :] * _shift_down(u, 2)
            + wv[0:1, :] * _shift_down(u, 3) + bv)


def _sconv_fwd(proj, w, b):
    s = proj.shape[0]

    def body(u_ref, w_ref, b_ref, o_ref):
        o_ref[...] = _silu(_sconv_pre(u_ref[...], w_ref[...], b_ref[...]))

    return pl.pallas_call(
        body, name="sconv_fwd", grid=(N_XBC // LANE,),
        in_specs=[_col(s, O_XBC), pl.BlockSpec((4, LANE), lambda j: (0, j)), pl.BlockSpec((1, LANE), lambda j: (0, j))],
        out_specs=pl.BlockSpec((s, LANE), lambda j: (0, j)),
        out_shape=_sds((s, N_XBC)),
        compiler_params=_params(("parallel",)),
    )(proj, w, b)


def _ssd_chunk_common(tail, sc):
    l = SSD_CHUNK
    lane = _iota((l, LANE), 1)
    row = _iota((l, LANE), 0)
    tri = (row >= lane).astype(F32)
    a_row = -jnp.exp(sc[1:2, :])
    pre = tail + sc[0:1, :]
    dt = _softplus(pre)
    a_cs = _dot_hi(tri, dt * a_row)
    return lane, row, tri, a_row, pre, dt, a_cs, a_cs.T


def _pick_col(m, lane, k):
    return jnp.sum(jnp.where(lane == k, m, 0.0), axis=1, keepdims=True)


def _pick_row(m, row, k):
    return jnp.sum(jnp.where(row == k, m, 0.0), axis=0, keepdims=True)


def _ssd_fwd(xbc, proj, sc):
    s = xbc.shape[0]
    nc = s // SSD_CHUNK
    l = SSD_CHUNK
    cps = SSD_CHUNKS_PER_STEP

    def body(xbc_ref, tail_ref, sc_ref, y_ref, st_ref, state):
        @pl.when(pl.program_id(0) == 0)
        def _():
            state[...] = jnp.zeros_like(state)

        sc_v = sc_ref[...]
        lane1 = _iota((1, LANE), 1)
        rowp = _iota((LANE, 1), 0)
        d_row = sc_v[2:3, :]
        states = [state[j] for j in range(3)]
        for u in range(cps):
            r = slice(u * l, (u + 1) * l)
            lane, row, _, _, _, dt, a_cs, a_t = _ssd_chunk_common(tail_ref[r, :], sc_v)
            for j in range(3):
                st_ref[u, j] = states[j]
            for j in range(3):
                xpair = xbc_ref[r, LANE * j:LANE * (j + 1)]
                sp = states[j]
                ypair = jnp.zeros((l, LANE), F32)
                new_s = jnp.zeros((LANE, LANE), F32)
                decay = jnp.zeros((LANE, 1), F32)
                for half in range(2):
                    h = 2 * j + half
                    g = h // 3
                    hm = (lane < 64) if half == 0 else (lane >= 64)
                    hrow = (rowp < 64) if half == 0 else (rowp >= 64)
                    ac = _pick_col(a_cs, lane, DT_LANE + h)
                    ar = _pick_row(a_t, row, DT_LANE + h)
                    dtc = _pick_col(dt, lane, DT_LANE + h)
                    alast = jnp.sum(jnp.where(lane1 == l - 1, ar, 0.0), axis=1, keepdims=True)
                    dh = jnp.sum(jnp.where(lane1 == DT_LANE + h, d_row, 0.0), axis=1, keepdims=True)
                    xm = jnp.where(hm, xpair, 0.0)
                    xd = xm * dtc
                    bm = xbc_ref[r, D_SSD + LANE * g:D_SSD + LANE * (g + 1)]
                    cm = xbc_ref[r, D_SSD + SSD_BC + LANE * g:D_SSD + SSD_BC + LANE * (g + 1)]
                    lm = jnp.where(row >= lane, jnp.exp(jnp.minimum(ac - ar, 0.0)), 0.0)
                    y_diag = _dot(_dot_nt(cm, bm) * lm, xd)
                    y_off = jnp.where(hm, _dot_nt(cm, sp), 0.0) * jnp.exp(ac)
                    ypair = ypair + y_diag + y_off + xm * dh
                    new_s = new_s + _dot_tn(xd * jnp.exp(alast - ac), bm)
                    decay = jnp.where(hrow, jnp.exp(alast), decay)
                states[j] = sp * decay + new_s
                y_ref[r, LANE * j:LANE * (j + 1)] = ypair
        for j in range(3):
            state[j] = states[j]

    return pl.pallas_call(
        body, name="ssd_fwd", grid=(nc // cps,),
        in_specs=[pl.BlockSpec((cps * l, N_XBC), lambda c: (c, 0)),
                  pl.BlockSpec((cps * l, LANE), lambda c: (c, O_TAIL // LANE)), _full((8, LANE))],
        out_specs=[pl.BlockSpec((cps * l, D_SSD), lambda c: (c, 0)), pl.BlockSpec((cps, 3, LANE, LANE), lambda c: (c, 0, 0, 0))],
        out_shape=[_sds((s, D_SSD)), _sds((nc, 3, LANE, LANE))],
        scratch_shapes=[pltpu.VMEM((3, LANE, LANE), F32)],
        compiler_params=_params(("arbitrary",)),
    )(xbc, proj, sc)


def _mla_prep_fwd(proj, gq, gkv, wq, wkv, cos, sin):
    s = proj.shape[0]
    ts = _tile(s)
    nh = MLA_HEADS

    def body(cqa_ref, ckv_ref, tail_ref, gq_ref, gkv_ref, wq_ref, wkv_ref, cos_ref, sin_ref,
             q_ref, k_ref, v_ref, qn_ref, kvn_ref, rq_ref, rkv_ref):
        qn, rq = _rms(cqa_ref[...], gq_ref[...])
        kvn, rkv = _rms(ckv_ref[...], gkv_ref[...])
        qn = qn.astype(MXU)
        kvn = kvn.astype(MXU)
        qn_ref[...] = qn
        kvn_ref[...] = kvn
        rq_ref[...] = rq
        rkv_ref[...] = rkv
        q = _dot_nt(qn, wq_ref[...])
        kv = _dot_nt(kvn, wkv_ref[...])
        cosv = cos_ref[...]
        sinv = sin_ref[...]
        lane = _iota((ts, LANE), 1)
        rope_lanes = (lane >= ROPE_LANE) & (lane < ROPE_LANE + QK_ROPE)
        kr = jnp.where(rope_lanes, pltpu.roll(tail_ref[...], ROPE_LANE, 1), 0.0)
        kr = kr * cosv + _rope_swap(kr) * sinv
        for h in range(nh):
            qh = q[:, LANE * h:LANE * (h + 1)]
            q_ref[h] = ((qh * cosv + _rope_swap(qh) * sinv) * ATT_SCALE).astype(MXU)
            k_ref[h] = (kv[:, LANE * h:LANE * (h + 1)] + kr).astype(MXU)
            v_ref[h] = kv[:, LANE * (nh + h):LANE * (nh + h + 1)].astype(MXU)

    head = pl.BlockSpec((nh, ts, LANE), lambda i: (0, i, 0))
    return pl.pallas_call(
        body, name="mla_prep_fwd", grid=(s // ts,),
        in_specs=[pl.BlockSpec((ts, Q_LORA), lambda i: (i, O_CQA // Q_LORA)),
                  pl.BlockSpec((ts, KV_LORA), lambda i: (i, O_CKV // KV_LORA)),
                  pl.BlockSpec((ts, LANE), lambda i: (i, O_TAIL // LANE)),
                  _full((1, Q_LORA)), _full((1, KV_LORA)), _full((nh * LANE, Q_LORA)), _full((2 * nh * LANE, KV_LORA)),
                  _row(ts, LANE), _row(ts, LANE)],
        out_specs=[head, head, head, _row(ts, Q_LORA), _row(ts, KV_LORA), _row(ts, 1), _row(ts, 1)],
        out_shape=[_sds((nh, s, LANE), MXU)] * 3 + [_sds((s, Q_LORA), MXU), _sds((s, KV_LORA), MXU), _sds((s, 1)), _sds((s, 1))],
        compiler_params=_params(("parallel",)),
    )(proj, proj, proj, gq, gkv, wq, wkv, cos, sin)


ATT_SCALE = (QK_NOPE + QK_ROPE) ** -0.5
NEG = -1e30


def _att_tile(s, most):
    return min(most, s // 2)


ATT_FWD_TILE = 1024
ATT_BWD_TILE = 512


def _attn_fwd(q, k, v):
    nh, s, _ = q.shape
    tq = _att_tile(s, ATT_FWD_TILE)
    nq = s // tq

    def body(q_ref, k_ref, v_ref, o_ref, lse_ref):
        i = pl.program_id(1)
        rowi = _iota((tq, tq), 0)
        coli = _iota((tq, tq), 1)
        zero = (jnp.full((tq, 1), NEG, F32), jnp.zeros((tq, 1), F32), jnp.zeros((tq, LANE), F32))
        state = [zero, zero]
        done = [zero, zero]
        for t in range(nq + 1):
            first = t <= i
            qblk = jnp.where(first, i, nq - 1 - i)
            kblk = jnp.where(first, t, t - i - 1)
            qoff = pl.multiple_of(qblk * tq, tq)
            koff = pl.multiple_of(kblk * tq, tq)
            keep = coli <= rowi + jnp.where(kblk == qblk, 0, tq)
            restart = t == i + 1
            for hh in range(2):
                m, lsum, acc = state[hh]
                if t > 0:
                    done[hh] = tuple(jnp.where(restart, a, b) for a, b in zip(state[hh], done[hh]))
                    m = jnp.where(restart, NEG, m)
                    lsum = jnp.where(restart, 0.0, lsum)
                    acc = jnp.where(restart, 0.0, acc)
                sc = _dot_nt(q_ref[hh, pl.ds(qoff, tq), :], k_ref[hh, pl.ds(koff, tq), :])
                sc = jnp.where(keep, sc, NEG)
                m_new = jnp.maximum(m, jnp.max(sc, axis=1, keepdims=True))
                p = jnp.exp(sc - m_new)
                alpha = jnp.exp(m - m_new)
                lsum = alpha * lsum + jnp.sum(p, axis=1, keepdims=True)
                acc = alpha * acc + _dot(p, v_ref[hh, pl.ds(koff, tq), :])
                state[hh] = (m_new, lsum, acc)
        for blk, res in ((i, done), (nq - 1 - i, state)):
            off = pl.multiple_of(blk * tq, tq)
            out = None
            for hh in range(2):
                m, lsum, acc = res[hh]
                o = acc * (1.0 / lsum)
                lse_ref[hh, pl.ds(off, tq), :] = m + jnp.log(lsum)
                out = o if hh == 0 else out + pltpu.roll(o, V_DIM, 1)
            o_ref[pl.ds(off, tq), :] = out

    pair = pl.BlockSpec((2, s, LANE), lambda j, i: (j, 0, 0))
    return pl.pallas_call(
        body, name="attn_fwd", grid=(nh // 2, nq // 2),
        in_specs=[pair, pair, pair],
        out_specs=[pl.BlockSpec((s, LANE), lambda j, i: (0, j)), pl.BlockSpec((2, s, 1), lambda j, i: (j, 0, 0))],
        out_shape=[_sds((s, D_MLA)), _sds((nh, s, 1))],
        compiler_params=_params(("parallel", "arbitrary")),
    )(q, k, v)


def _ssd_gate(y_ssd, s_z, g):
    yz = y_ssd * _silu(s_z)
    g0 = _iota(yz.shape, 1) < D_SSD // 2
    sq = yz * yz
    ms0 = jnp.sum(jnp.where(g0, sq, 0.0), axis=1, keepdims=True) / (D_SSD // 2)
    ms1 = jnp.sum(jnp.where(g0, 0.0, sq), axis=1, keepdims=True) / (D_SSD // 2)
    r = jnp.where(g0, lax.rsqrt(ms0 + SSD_NORM_EPS), lax.rsqrt(ms1 + SSD_NORM_EPS))
    nrm = yz * r
    return nrm * g, nrm, r, g0


def _outproj_fwd(x, proj, ya, y_ssd, o, g_ssd, w):
    s = x.shape[0]
    ts = _tile(s)

    def body(x_ref, sz_ref, cz_ref, ya_ref, ys_ref, o_ref, g_ref, w_ref, xo_ref, y_ref):
        yb = _ssd_gate(ys_ref[...], sz_ref[...], g_ref[...])[0]
        yc = o_ref[...] * _silu(cz_ref[...])
        y = jnp.concatenate([ya_ref[...], yb, yc], axis=1).astype(MXU)
        y_ref[...] = y
        xo_ref[...] = x_ref[...] + jnp.dot(y, w_ref[...], preferred_element_type=F32)

    return pl.pallas_call(
        body, name="outproj_fwd", grid=(s // ts,),
        in_specs=[_row(ts, D_MODEL), _gate_cols(ts, O_SZ), _gate_cols(ts, O_CZ), _row(ts, D_CONV_A), _row(ts, D_SSD),
                  _row(ts, D_MLA), _full((1, D_SSD)), _full((D_MODEL, D_MODEL))],
        out_specs=[_row(ts, D_MODEL), _row(ts, D_MODEL)],
        out_shape=[_sds((s, D_MODEL)), _sds((s, D_MODEL), MXU)],
        compiler_params=_params(("parallel",)),
    )(x, proj, proj, ya, y_ssd, o, g_ssd, w)


def _loss_head(x, g, tgt):
    s = x.shape[0]
    ts = _tile(s)

    def body(x_ref, g_ref, t_ref, dx_ref, dg_ref, loss_ref):
        @pl.when(pl.program_id(0) == 0)
        def _():
            dg_ref[...] = jnp.zeros_like(dg_ref)
            loss_ref[...] = jnp.zeros_like(loss_ref)

        xv = x_ref[...]
        gv = g_ref[...]
        yn, r = _rms(xv, gv)
        e = yn - t_ref[...]
        loss_ref[...] += jnp.sum(jnp.sum(e * e, axis=1, keepdims=True), axis=0, keepdims=True) * (0.5 / D_MODEL)
        dx, dg = _rms_bwd(e * (1.0 / D_MODEL), xv, r, gv)
        dx_ref[...] = dx
        dg_ref[...] += dg

    return pl.pallas_call(
        body, name="loss_head", grid=(s // ts,),
        in_specs=[_row(ts, D_MODEL), _full((1, D_MODEL)), _row(ts, D_MODEL)],
        out_specs=[_row(ts, D_MODEL), _full((1, D_MODEL)), _full((1, LANE))],
        out_shape=[_sds((s, D_MODEL)), _sds((1, D_MODEL)), _sds((1, LANE))],
        compiler_params=_params(("arbitrary",)),
    )(x, g, tgt)


def _outproj_bwd(dout, y, w, proj, y_ssd, o, g_ssd, dep=None):
    s = dout.shape[0]
    ts = _tile(s)

    def body(dout_ref, y_ref, w_ref, sz_ref, cz_ref, ys_ref, o_ref, g_ref,
             dya_ref, dys_ref, dsz_ref, dattn_ref, dcz_ref, dg_ref, dw_ref):
        @pl.when(pl.program_id(0) == 0)
        def _():
            dw_ref[...] = jnp.zeros_like(dw_ref)
            dg_ref[...] = jnp.zeros_like(dg_ref)

        dout_b = dout_ref[...].astype(MXU)
        dw_ref[...] += _dot_tn(y_ref[...], dout_b)
        dy = _dot_nt(dout_b, w_ref[...])
        dya_ref[...] = dy[:, :D_CONV_A]
        dyb = dy[:, D_CONV_A:D_CONV_A + D_SSD]
        sz = sz_ref[...]
        ys = ys_ref[...]
        gv = g_ref[...]
        _, nrm, r, g0 = _ssd_gate(ys, sz, gv)
        dg_ref[...] += jnp.sum(dyb * nrm, axis=0, keepdims=True)
        dn = dyb * gv
        t = dn * nrm
        mean = jnp.where(g0, jnp.sum(jnp.where(g0, t, 0.0), axis=1, keepdims=True),
                         jnp.sum(jnp.where(g0, 0.0, t), axis=1, keepdims=True)) / (D_SSD // 2)
        dyz = r * (dn - nrm * mean)
        dys_ref[...] = dyz * _silu(sz)
        dsz_ref[...] = (dyz * ys * _dsilu(sz)).astype(MXU)
        dyc = dy[:, D_CONV_A + D_SSD:]
        cz = cz_ref[...]
        dattn_ref[...] = dyc * _silu(cz)
        dcz_ref[...] = (dyc * o_ref[...] * _dsilu(cz)).astype(MXU)

    return _call_after(
        dep, body, (dout, y, w, proj, proj, y_ssd, o, g_ssd), name="outproj_bwd", grid=(s // ts,),
        in_specs=[_row(ts, D_MODEL), _row(ts, D_MODEL), _full((D_MODEL, D_MODEL)), _gate_cols(ts, O_SZ), _gate_cols(ts, O_CZ),
                  _row(ts, D_SSD), _row(ts, D_MLA), _full((1, D_SSD))],
        out_specs=[_row(ts, D_CONV_A), _row(ts, D_SSD), _row(ts, D_SSD), _row(ts, D_MLA), _row(ts, D_MLA),
                   _full((1, D_SSD)), _full((D_MODEL, D_MODEL))],
        out_shape=[_sds((s, D_CONV_A)), _sds((s, D_SSD)), _sds((s, D_SSD), MXU), _sds((s, D_MLA)), _sds((s, D_MLA), MXU),
                   _sds((1, D_SSD)), _sds((D_MODEL, D_MODEL))],
        compiler_params=_params(("arbitrary",)),
    )


def _attn_bwd(q, k, v, o, d_o, lse, dep=None):
    nh, s, _ = q.shape
    tq = _att_tile(s, ATT_BWD_TILE)
    nq = s // tq

    def body(q_ref, k_ref, v_ref, o_ref, do_ref, lse_ref, dq_ref, dk_ref, dv_ref, dop, delta):
        i = pl.program_id(1)

        @pl.when(i == 0)
        def _():
            lane = _iota((s, LANE), 1)
            for hh in range(2):
                dov = do_ref[...]
                ov = o_ref[...]
                if hh == 1:
                    dov = pltpu.roll(dov, V_DIM, 1)
                    ov = pltpu.roll(ov, V_DIM, 1)
                dov = jnp.where(lane < V_DIM, dov, 0.0)
                dop[hh] = dov.astype(MXU)
                delta[hh] = jnp.sum(dov * ov, axis=1, keepdims=True)
                dq_ref[hh] = jnp.zeros((s, LANE), F32)

        rowi = _iota((tq, tq), 0)
        coli = _iota((tq, tq), 1)
        z = jnp.zeros((tq, LANE), F32)
        state = [(z, z), (z, z)]
        done = [(z, z), (z, z)]
        for t in range(nq + 1):
            first = t <= nq - 1 - i
            kblk = jnp.where(first, i, nq - 1 - i)
            qblk = jnp.where(first, i + t, t - 1)
            qoff = pl.multiple_of(qblk * tq, tq)
            koff = pl.multiple_of(kblk * tq, tq)
            keep = coli <= rowi + jnp.where(kblk == qblk, 0, tq)
            restart = t == nq - i
            for hh in range(2):
                dk, dv = state[hh]
                if t > 0:
                    done[hh] = tuple(jnp.where(restart, a, b) for a, b in zip(state[hh], done[hh]))
                    dk = jnp.where(restart, 0.0, dk)
                    dv = jnp.where(restart, 0.0, dv)
                kb = k_ref[hh, pl.ds(koff, tq), :]
                qb = q_ref[hh, pl.ds(qoff, tq), :]
                dob = dop[hh, pl.ds(qoff, tq), :]
                sc = jnp.where(keep, _dot_nt(qb, kb), NEG)
                p = jnp.exp(sc - lse_ref[hh, pl.ds(qoff, tq), :])
                dp = _dot_nt(dob, v_ref[hh, pl.ds(koff, tq), :])
                ds = p * (dp - delta[hh, pl.ds(qoff, tq), :])
                dq_ref[hh, pl.ds(qoff, tq), :] += _dot(ds, kb)
                state[hh] = (dk + _dot_tn(ds, qb), dv + _dot_tn(p, dob))
        for blk, res in ((i, done), (nq - 1 - i, state)):
            off = pl.multiple_of(blk * tq, tq)
            for hh in range(2):
                dk_ref[hh, pl.ds(off, tq), :] = res[hh][0]
                dv_ref[hh, pl.ds(off, tq), :] = res[hh][1]

    pair = pl.BlockSpec((2, s, LANE), lambda j, i: (j, 0, 0))
    return _call_after(
        dep, body, (q, k, v, o, d_o, lse), name="attn_bwd", grid=(nh // 2, nq // 2),
        in_specs=[pair, pair, pair, pl.BlockSpec((s, LANE), lambda j, i: (0, j)), pl.BlockSpec((s, LANE), lambda j, i: (0, j)),
                  pl.BlockSpec((2, s, 1), lambda j, i: (j, 0, 0))],
        out_specs=[pair, pair, pair],
        out_shape=[_sds((nh, s, LANE))] * 3,
        scratch_shapes=[pltpu.VMEM((2, s, LANE), MXU), pltpu.VMEM((2, s, 1), F32)],
        compiler_params=_params(("parallel", "arbitrary")),
    )


def _ssd_bwd(xbc, proj, sc, states, dy, dep=None):
    s = xbc.shape[0]
    nc = s // SSD_CHUNK
    l = SSD_CHUNK
    cps = SSD_CHUNKS_PER_STEP

    def body(xbc_ref, tail_ref, sc_ref, st_ref, dy_ref, dxbc_ref, dtail_ref, dsc_ref, dstate):
        @pl.when(pl.program_id(0) == 0)
        def _():
            dstate[...] = jnp.zeros_like(dstate)
            dsc_ref[...] = jnp.zeros_like(dsc_ref)

        sc_v = sc_ref[...]
        lane1 = _iota((1, LANE), 1)
        rowp = _iota((LANE, 1), 0)
        rowl = _iota((l, 1), 0)
        d_row = sc_v[2:3, :]
        dstates = [dstate[j] for j in range(3)]
        for u in reversed(range(cps)):
            dstates = chunk(u, xbc_ref, tail_ref, sc_v, st_ref, dy_ref, dxbc_ref, dtail_ref, dsc_ref, dstates,
                            lane1, rowp, rowl, d_row)
        for j in range(3):
            dstate[j] = dstates[j]

    def chunk(u, xbc_ref, tail_ref, sc_v, st_ref, dy_ref, dxbc_ref, dtail_ref, dsc_ref, dstates, lane1, rowp, rowl, d_row):
        r = slice(u * l, (u + 1) * l)
        dstates = list(dstates)
        lane, row, tri, a_row, pre, dt, a_cs, a_t = _ssd_chunk_common(tail_ref[r, :], sc_v)
        da_col = jnp.zeros((l, LANE), F32)
        da_row = jnp.zeros((LANE, l), F32)
        dt_x = jnp.zeros((l, LANE), F32)
        dd_row = jnp.zeros((1, LANE), F32)
        db = [jnp.zeros((l, LANE), F32), jnp.zeros((l, LANE), F32)]
        dc = [jnp.zeros((l, LANE), F32), jnp.zeros((l, LANE), F32)]
        for j in range(3):
            xpair = xbc_ref[r, LANE * j:LANE * (j + 1)]
            dypair = dy_ref[r, LANE * j:LANE * (j + 1)]
            sp = st_ref[u, j]
            dsp = dstates[j]
            dxpair = jnp.zeros((l, LANE), F32)
            ds_new = jnp.zeros((LANE, LANE), F32)
            decay = jnp.zeros((LANE, 1), F32)
            for half in range(2):
                h = 2 * j + half
                g = h // 3
                hm = (lane < 64) if half == 0 else (lane >= 64)
                hrow = (rowp < 64) if half == 0 else (rowp >= 64)
                ac = _pick_col(a_cs, lane, DT_LANE + h)
                ar = _pick_row(a_t, row, DT_LANE + h)
                dtc = _pick_col(dt, lane, DT_LANE + h)
                alast = jnp.sum(jnp.where(lane1 == l - 1, ar, 0.0), axis=1, keepdims=True)
                dh = jnp.sum(jnp.where(lane1 == DT_LANE + h, d_row, 0.0), axis=1, keepdims=True)
                xm = jnp.where(hm, xpair, 0.0)
                xd = xm * dtc
                dym = jnp.where(hm, dypair, 0.0)
                bm = xbc_ref[r, D_SSD + LANE * g:D_SSD + LANE * (g + 1)]
                cm = xbc_ref[r, D_SSD + SSD_BC + LANE * g:D_SSD + SSD_BC + LANE * (g + 1)]
                lm = jnp.where(row >= lane, jnp.exp(jnp.minimum(ac - ar, 0.0)), 0.0)
                e_in = jnp.exp(ac)
                f_out = jnp.exp(alast - ac)
                e_last = jnp.exp(alast)
                m = _dot_nt(cm, bm) * lm
                y_off = jnp.where(hm, _dot_nt(cm, sp), 0.0) * e_in
                dm = _dot_nt(dym, xd)
                dxd = _dot_tn(m, dym)
                dg = dm * lm
                dye = dym * e_in
                dc[g] = dc[g] + _dot(dg, bm) + _dot(dye, sp)
                db[g] = db[g] + _dot_tn(dg, cm)
                qm = dm * m
                dac = jnp.sum(qm, axis=1, keepdims=True) + jnp.sum(dym * y_off, axis=1, keepdims=True)
                dar = -jnp.sum(qm, axis=0, keepdims=True)
                dxf = jnp.where(hm, _dot_nt(bm, dsp), 0.0)
                db[g] = db[g] + _dot(xd * f_out, dsp)
                dxd = dxd + dxf * f_out
                df = jnp.sum(dxf * xd, axis=1, keepdims=True) * f_out
                dac = dac - df
                s_last = jnp.sum(df, axis=0, keepdims=True)
                ss = jnp.sum(jnp.where(hrow, dsp * sp, 0.0), axis=1, keepdims=True)
                s_last = s_last + e_last * jnp.sum(ss, axis=0, keepdims=True)
                dac = dac + jnp.where(rowl == l - 1, s_last, 0.0)
                ds_new = ds_new + _dot_tn(dye, cm)
                decay = jnp.where(hrow, e_last, decay)
                dxpair = dxpair + dxd * dtc + dym * dh
                dt_x = dt_x + jnp.where(lane == DT_LANE + h, jnp.sum(dxd * xm, axis=1, keepdims=True), 0.0)
                dsum = jnp.sum(jnp.sum(dym * xm, axis=1, keepdims=True), axis=0, keepdims=True)
                dd_row = dd_row + jnp.where(lane1 == DT_LANE + h, dsum, 0.0)
                da_col = da_col + jnp.where(lane == DT_LANE + h, dac, 0.0)
                da_row = da_row + jnp.where(row == DT_LANE + h, dar, 0.0)
            dstates[j] = dsp * decay + ds_new
            dxbc_ref[r, LANE * j:LANE * (j + 1)] = dxpair
        for g in range(2):
            dxbc_ref[r, D_SSD + LANE * g:D_SSD + LANE * (g + 1)] = db[g]
            dxbc_ref[r, D_SSD + SSD_BC + LANE * g:D_SSD + SSD_BC + LANE * (g + 1)] = dc[g]
        dla = _dot_hi_tn(tri, da_col + da_row.T)
        ddt = dt_x + dla * a_row
        dpre = ddt * _sigmoid(pre)
        dtm = (lane >= DT_LANE) & (lane < DT_LANE + SSD_HEADS)
        dtail_ref[r, :] = jnp.where(dtm, dpre, 0.0).astype(MXU)
        dtm1 = (lane1 >= DT_LANE) & (lane1 < DT_LANE + SSD_HEADS)
        dsc_ref[0:1, :] += jnp.where(dtm1, jnp.sum(dpre, axis=0, keepdims=True), 0.0)
        dsc_ref[1:2, :] += jnp.where(dtm1, jnp.sum(dla * dt, axis=0, keepdims=True) * a_row, 0.0)
        dsc_ref[2:3, :] += dd_row
        return dstates

    rev = lambda c: nc // cps - 1 - c
    return _call_after(
        dep, body, (xbc, proj, sc, states, dy), name="ssd_bwd", grid=(nc // cps,),
        in_specs=[pl.BlockSpec((cps * l, N_XBC), lambda c: (rev(c), 0)),
                  pl.BlockSpec((cps * l, LANE), lambda c: (rev(c), O_TAIL // LANE)), _full((8, LANE)),
                  pl.BlockSpec((cps, 3, LANE, LANE), lambda c: (rev(c), 0, 0, 0)),
                  pl.BlockSpec((cps * l, D_SSD), lambda c: (rev(c), 0))],
        out_specs=[pl.BlockSpec((cps * l, N_XBC), lambda c: (rev(c), 0)), pl.BlockSpec((cps * l, LANE), lambda c: (rev(c), 0)),
                   _full((8, LANE))],
        out_shape=[_sds((s, N_XBC)), _sds((s, LANE), MXU), _sds((8, LANE))],
        scratch_shapes=[pltpu.VMEM((3, LANE, LANE), F32)],
        compiler_params=_params(("arbitrary",)),
    )


def _sconv_bwd(proj, w, b, dxbc, dep=None):
    s = proj.shape[0]

    def body(u_ref, w_ref, b_ref, d_ref, du_ref, dw_ref, db_ref):
        u = u_ref[...]
        wv = w_ref[...]
        dpre = d_ref[...] * _dsilu(_sconv_pre(u, wv, b_ref[...]))
        du_ref[...] = (wv[3:4, :] * dpre + wv[2:3, :] * _shift_up(dpre, 1) + wv[1:2, :] * _shift_up(dpre, 2)
                       + wv[0:1, :] * _shift_up(dpre, 3)).astype(MXU)
        for k in range(4):
            dw_ref[k:k + 1, :] = jnp.sum(dpre * _shift_down(u, 3 - k), axis=0, keepdims=True)
        db_ref[...] = jnp.sum(dpre, axis=0, keepdims=True)

    blk = pl.BlockSpec((s, LANE), lambda j: (0, j))
    return _call_after(
        dep, body, (proj, w, b, dxbc), name="sconv_bwd", grid=(N_XBC // LANE,),
        in_specs=[_col(s, O_XBC), pl.BlockSpec((4, LANE), lambda j: (0, j)), pl.BlockSpec((1, LANE), lambda j: (0, j)), blk],
        out_specs=[blk, pl.BlockSpec((4, LANE), lambda j: (0, j)), pl.BlockSpec((1, LANE), lambda j: (0, j))],
        out_shape=[_sds((s, N_XBC), MXU), _sds((4, N_XBC)), _sds((1, N_XBC))],
        compiler_params=_params(("parallel",)),
    )


def _conva_bwd(proj, w, dya, dep=None):
    s = proj.shape[0]

    def body(h_ref, b_ref, c_ref, z_ref, w_ref, d_ref, da_ref, dw_ref):
        ah, ab, acv, az = h_ref[...], b_ref[...], c_ref[...], z_ref[...]
        wv = w_ref[...]
        u = acv * ah
        cv = wv[2:3, :] * u + wv[1:2, :] * _shift_down(u, 1) + wv[0:1, :] * _shift_down(u, 2)
        dy = d_ref[...]
        sz = _silu(az)
        da_ref[1] = (dy * cv * sz).astype(MXU)
        da_ref[3] = (dy * ab * cv * _dsilu(az)).astype(MXU)
        dcv = dy * ab * sz
        du = wv[2:3, :] * dcv + wv[1:2, :] * _shift_up(dcv, 1) + wv[0:1, :] * _shift_up(dcv, 2)
        da_ref[0] = (du * acv).astype(MXU)
        da_ref[2] = (du * ah).astype(MXU)
        for k in range(3):
            dw_ref[k:k + 1, :] = jnp.sum(dcv * _shift_down(u, 2 - k), axis=0, keepdims=True)

    return _call_after(
        dep, body, (proj, proj, proj, proj, w, dya), name="conva_bwd", grid=(D_CONV_A // LANE,),
        in_specs=[_col(s, O_AH), _col(s, O_AB), _col(s, O_AC), _col(s, O_AZ), pl.BlockSpec((3, LANE), lambda j: (0, j)),
                  pl.BlockSpec((s, LANE), lambda j: (0, j))],
        out_specs=[pl.BlockSpec((4, s, LANE), lambda j: (0, 0, j)), pl.BlockSpec((3, LANE), lambda j: (0, j))],
        out_shape=[_sds((4, s, D_CONV_A), MXU), _sds((3, D_CONV_A))],
        compiler_params=_params(("parallel",)),
    )


def _mla_prep_bwd(dq, dk, dv, proj, qn, kvn, rq, rkv, gq, gkv, wq, wkv, cos, sin):
    s = proj.shape[0]
    ts = _tile(s)
    nh = MLA_HEADS

    def body(dq_ref, dk_ref, dv_ref, cqa_ref, ckv_ref, qn_ref, kvn_ref, rq_ref, rkv_ref, gq_ref, gkv_ref,
             wq_ref, wkv_ref, cos_ref, sin_ref, dcqa_ref, dckv_ref, dtail_ref, dwq_ref, dwkv_ref, dgq_ref, dgkv_ref):
        @pl.when(pl.program_id(0) == 0)
        def _():
            dwq_ref[...] = jnp.zeros_like(dwq_ref)
            dwkv_ref[...] = jnp.zeros_like(dwkv_ref)
            dgq_ref[...] = jnp.zeros_like(dgq_ref)
            dgkv_ref[...] = jnp.zeros_like(dgkv_ref)

        cosv = cos_ref[...]
        sinv = sin_ref[...]
        lane = _iota((ts, LANE), 1)
        rope_lanes = (lane >= ROPE_LANE) & (lane < ROPE_LANE + QK_ROPE)

        def unrope(gr):
            return gr * cosv + _rope_swap(gr * sinv)

        dqs, dks, dvs = [], [], []
        dkr = jnp.zeros((ts, LANE), F32)
        for h in range(nh):
            dqs.append(unrope(dq_ref[h] * ATT_SCALE).astype(MXU))
            dkh = dk_ref[h]
            dks.append(jnp.where(lane < QK_NOPE, dkh, 0.0).astype(MXU))
            dkr = dkr + jnp.where(rope_lanes, dkh, 0.0)
            dvs.append(dv_ref[h].astype(MXU))
        dtail_ref[...] = pltpu.roll(jnp.where(rope_lanes, unrope(dkr), 0.0), ROPE_LANE, 1).astype(MXU)
        dq_all = jnp.concatenate(dqs, axis=1)
        dkv_all = jnp.concatenate(dks + dvs, axis=1)
        dwq_ref[...] += _dot_tn(dq_all, qn_ref[...])
        dwkv_ref[...] += _dot_tn(dkv_all, kvn_ref[...])
        dcqa, dgq = _rms_bwd(_dot(dq_all, wq_ref[...]), cqa_ref[...], rq_ref[...], gq_ref[...])
        dckv, dgkv = _rms_bwd(_dot(dkv_all, wkv_ref[...]), ckv_ref[...], rkv_ref[...], gkv_ref[...])
        dcqa_ref[...] = dcqa.astype(MXU)
        dckv_ref[...] = dckv.astype(MXU)
        dgq_ref[...] += dgq
        dgkv_ref[...] += dgkv

    head = pl.BlockSpec((nh, ts, LANE), lambda i: (0, i, 0))
    return pl.pallas_call(
        body, name="mla_prep_bwd", grid=(s // ts,),
        in_specs=[head, head, head,
                  pl.BlockSpec((ts, Q_LORA), lambda i: (i, O_CQA // Q_LORA)),
                  pl.BlockSpec((ts, KV_LORA), lambda i: (i, O_CKV // KV_LORA)),
                  _row(ts, Q_LORA), _row(ts, KV_LORA), _row(ts, 1), _row(ts, 1),
                  _full((1, Q_LORA)), _full((1, KV_LORA)), _full((nh * LANE, Q_LORA)), _full((2 * nh * LANE, KV_LORA)),
                  _row(ts, LANE), _row(ts, LANE)],
        out_specs=[_row(ts, Q_LORA), _row(ts, KV_LORA), _row(ts, LANE), _full((nh * LANE, Q_LORA)),
                   _full((2 * nh * LANE, KV_LORA)), _full((1, Q_LORA)), _full((1, KV_LORA))],
        out_shape=[_sds((s, Q_LORA), MXU), _sds((s, KV_LORA), MXU), _sds((s, LANE), MXU), _sds((nh * LANE, Q_LORA)),
                   _sds((2 * nh * LANE, KV_LORA)), _sds((1, Q_LORA)), _sds((1, KV_LORA))],
        compiler_params=_params(("arbitrary",)),
    )(dq, dk, dv, proj, proj, qn, kvn, rq, rkv, gq, gkv, wq, wkv, cos, sin)


def _inproj_bwd(da4, dsz, dxbc_in, dcqa, dckv, dcz, dtail_a, dtail_b, w, x, rstd, g, dout, dep=None):
    s = x.shape[0]
    ts = _tile(s)

    def body(da_ref, dsz_ref, dxbc_ref, dcqa_ref, dckv_ref, dcz_ref, dta_ref, dtb_ref, w_ref, x_ref, r_ref, g_ref, dout_ref,
             dproj_ref, dx_ref, dg_ref):
        @pl.when(pl.program_id(0) == 0)
        def _():
            dg_ref[...] = jnp.zeros_like(dg_ref)

        dproj = jnp.concatenate(
            [da_ref[0], da_ref[1], da_ref[2], da_ref[3], dxbc_ref[...], dsz_ref[...], dcqa_ref[...], dckv_ref[...],
             dcz_ref[...], dta_ref[...] + dtb_ref[...]], axis=1)
        dproj_ref[...] = dproj
        dh = _dot_nt(dproj, w_ref[...])
        dx, dg = _rms_bwd(dh, x_ref[...], r_ref[...], g_ref[...])
        dx_ref[...] = dout_ref[...] + dx
        dg_ref[...] += dg

    return _call_after(
        dep, body, (da4, dsz, dxbc_in, dcqa, dckv, dcz, dtail_a, dtail_b, w, x, rstd, g, dout), name="inproj_bwd", grid=(s // ts,),
        in_specs=[pl.BlockSpec((4, ts, D_CONV_A), lambda i: (0, i, 0)), _row(ts, D_SSD), _row(ts, N_XBC), _row(ts, Q_LORA),
                  _row(ts, KV_LORA), _row(ts, D_MLA), _row(ts, LANE), _row(ts, LANE), _full((D_MODEL, NCOL)),
                  _row(ts, D_MODEL), _row(ts, 1), _full((1, D_MODEL)), _row(ts, D_MODEL)],
        out_specs=[_row(ts, NCOL), _row(ts, D_MODEL), _full((1, D_MODEL))],
        out_shape=[_sds((s, NCOL), MXU), _sds((s, D_MODEL)), _sds((1, D_MODEL))],
        compiler_params=_params(("arbitrary",)),
    )


DWIN_BLOCK = 640


def _dwin(h, dproj, dep=None):
    s = h.shape[0]

    def body(h_ref, d_ref, o_ref):
        o_ref[...] = _dot_tn(h_ref[...], d_ref[...])

    return _call_after(
        dep, body, (h, dproj), name="dwin", grid=(NCOL // DWIN_BLOCK,),
        in_specs=[_full((s, D_MODEL)), pl.BlockSpec((s, DWIN_BLOCK), lambda j: (0, j))],
        out_specs=pl.BlockSpec((D_MODEL, DWIN_BLOCK), lambda j: (0, j)),
        out_shape=_sds((D_MODEL, NCOL)),
        compiler_params=_params(("parallel",)),
    )


def _adamw(ws, gs, ms, vs):
    n = len(ws)
    bc1 = 1.0 - ADAM_B1 ** ADAM_STEP
    bc2 = 1.0 - ADAM_B2 ** ADAM_STEP

    def body(*refs):
        ins, outs = refs[:4 * n], refs[4 * n:]
        for a in range(n):
            w_ref, g_ref, m_ref, v_ref = ins[a], ins[n + a], ins[2 * n + a], ins[3 * n + a]
            gv = g_ref[...]
            mn = ADAM_B1 * m_ref[...] + (1.0 - ADAM_B1) * gv
            vn = ADAM_B2 * v_ref[...] + (1.0 - ADAM_B2) * (gv * gv)
            outs[n + a][...] = mn
            outs[2 * n + a][...] = vn
            outs[a][...] = -ADAM_LR * ((mn / bc1) / (jnp.sqrt(vn / bc2) + ADAM_EPS) + ADAM_WD * w_ref[...])

    if ws[0].ndim == 2:
        grid, blks = (1,), [pl.BlockSpec(w.shape, lambda i: (0, 0)) for w in ws]
    else:
        grid = (ws[0].shape[0], 2)
        blks = [pl.BlockSpec((1, w.shape[1] // 2, w.shape[2]), lambda i, k: (i, k, 0)) for w in ws]
    out = pl.pallas_call(
        body, name="adamw", grid=grid,
        in_specs=blks * 4, out_specs=blks * 3, out_shape=[_sds(w.shape) for w in ws] * 3,
        compiler_params=_params(("parallel",) * len(grid)),
    )(*ws, *gs, *ms, *vs)
    return [(out[a], out[n + a], out[2 * n + a]) for a in range(n)]


COL_MOVES = ((0, 0, 1024), (1024, O_SZ, 384), (1408, O_XBC, 896), (2304, O_TAIL + DT_LANE, 6), (2310, O_CQA, 256),
             (2566, O_CKV, 128), (2694, O_TAIL, 32), (2726, O_CZ, 384))


def _move_cols(w, moves, width):
    out = None
    for src, dst, n in moves:
        piece = jnp.pad(w[..., src:src + n], [(0, 0)] * (w.ndim - 1) + [(dst, width - dst - n)])
        out = piece if out is None else out + piece
    return out


def _perm_cols(w):
    return _move_cols(w, COL_MOVES, NCOL)


def _unperm_cols(g):
    return _move_cols(g, [(dst, src, n) for src, dst, n in COL_MOVES], IN_COLS)


def _wq_layout(wt):
    return jnp.pad(wt.reshape(MLA_HEADS, QK_NOPE + QK_ROPE, Q_LORA), ((0, 0), (0, 32), (0, 0))).reshape(MLA_HEADS * LANE, Q_LORA)


def _wq_unlayout(g):
    return g.reshape(MLA_HEADS, LANE, Q_LORA)[:, :QK_NOPE + QK_ROPE].reshape(MLA_HEADS * (QK_NOPE + QK_ROPE), Q_LORA)


def _wkv_layout(wt):
    t = wt.reshape(MLA_HEADS, 2, 64, KV_LORA).transpose(1, 0, 2, 3)
    return jnp.pad(t, ((0, 0), (0, 0), (0, 64), (0, 0))).reshape(2 * MLA_HEADS * LANE, KV_LORA)


def _wkv_unlayout(g):
    t = g.reshape(2, MLA_HEADS, LANE, KV_LORA)[:, :, :64]
    return t.transpose(1, 0, 2, 3).reshape(MLA_HEADS * LANE, KV_LORA)


def _rope_tables(positions):
    inv_freq = ROPE_BASE ** (-jnp.arange(0, QK_ROPE, 2, dtype=F32) / QK_ROPE)
    ang = positions.astype(F32)[:, None] * inv_freq
    cos, sin = jnp.cos(ang), jnp.sin(ang)
    s = positions.shape[0]
    one, zero = jnp.ones((s, ROPE_LANE), F32), jnp.zeros((s, ROPE_LANE), F32)
    cos_t = jnp.concatenate([one, cos, cos, one[:, :32]], axis=1)
    sin_t = jnp.concatenate([zero, -sin, sin, zero[:, :32]], axis=1)
    return cos_t, sin_t


def _ssd_scalars(dt_bias, a_log, d_skip):
    return jnp.pad(jnp.stack([dt_bias, a_log, d_skip]), ((0, 5), (DT_LANE, LANE - DT_LANE - SSD_HEADS)))


def _layer_fwd(x, lw, cos, sin, dep=None, late=None):
    proj, h, rstd = _inproj_fwd(x, lw["norm_g"], lw["w_in"], dep)
    ya = _conva_fwd(proj, lw["conv_a_w"])
    xbc = _sconv_fwd(proj, lw["ssd_conv_w"], lw["ssd_conv_b"])
    y_ssd, states = _ssd_fwd(xbc, proj, lw["sc"])
    if late is not None:
        lw = {**lw, **late(ya, y_ssd)}
    q, k, v, qn, kvn, rq, rkv = _mla_prep_fwd(proj, lw["gq"], lw["gkv"], lw["wq"], lw["wkv"], cos, sin)
    o, lse = _attn_fwd(q, k, v)
    x_out, y = _outproj_fwd(x, proj, ya, y_ssd, o, lw["g_ssd"], lw["w_out"])
    saved = dict(x=x, proj=proj, h=h, rstd=rstd, xbc=xbc, y_ssd=y_ssd, states=states, q=q, k=k, v=v, qn=qn, kvn=kvn,
                 rq=rq, rkv=rkv, o=o, lse=lse, y=y)
    return x_out, saved, lw


def _layer_bwd(dout, lw, sv, cos, sin, rs=None, begin_early=None):
    tok = lambda: None if rs is None else rs["h"]["token"]
    dya, dys, dsz, d_o, dcz, dg_ssd, dw_out = _outproj_bwd(dout, sv["y"], lw["w_out"], sv["proj"], sv["y_ssd"], sv["o"],
                                                            lw["g_ssd"], tok())
    if rs is not None:
        rs = _rs_add_mine(rs, [dya])
    dq, dk, dv = _attn_bwd(sv["q"], sv["k"], sv["v"], sv["o"], d_o, sv["lse"], tok())
    dxbc, dtail_s, dsc = _ssd_bwd(sv["xbc"], sv["proj"], lw["sc"], sv["states"], dys, tok())
    da4, dw_conva = _conva_bwd(sv["proj"], lw["conv_a_w"], dya, tok())
    if rs is not None:
        rs = _rs_add_chips(rs, [dq, dxbc, da4])
    du, dw_sconv, db_sconv = _sconv_bwd(sv["proj"], lw["ssd_conv_w"], lw["ssd_conv_b"], dxbc, tok())
    dcqa, dckv, dtail_m, dwq, dwkv, dgq, dgkv = _mla_prep_bwd(
        dq, dk, dv, sv["proj"], sv["qn"], sv["kvn"], sv["rq"], sv["rkv"], lw["gq"], lw["gkv"], lw["wq"], lw["wkv"], cos, sin)
    early = None if begin_early is None else begin_early(dw_out, dwq, dwkv)
    etok = lambda: None if early is None else early["h"]["token"]
    dproj, dx, dg = _inproj_bwd(da4, dsz, du, dcqa, dckv, dcz, dtail_s, dtail_m, lw["w_in"], sv["x"], sv["rstd"],
                                lw["norm_g"], dout, etok())
    reduced = None if rs is None else _rs_end(rs, [du, dcqa, dx])
    if early is not None:
        early = _rs_add_mine(early, [dx])
    dw_in = _dwin(sv["h"], dproj, etok())
    if early is not None:
        early = _rs_add_chips(early, [dw_in])
    grads = dict(norm_g=dg, w_in=dw_in, conv_a_w=dw_conva, ssd_conv_w=dw_sconv, ssd_conv_b=db_sconv, sc=dsc,
                 g_ssd=dg_ssd, gq=dgq, wq=dwq, gkv=dgkv, wkv=dwkv, w_out=dw_out)
    return dx, grads, reduced, early


ANY = pl.BlockSpec(memory_space=pl.ANY)
N_CHIPS = 4
N_DEV = 8


def _place():
    return lax.axis_index("x"), lax.axis_index("y"), lax.axis_index("c")


HBM_SPEC = pl.BlockSpec(memory_space=pltpu.HBM)
SEM_SPEC = pl.BlockSpec(memory_space=pltpu.SEMAPHORE)
PAYLOAD = jnp.bfloat16


def _hbm(a):
    return pltpu.with_memory_space_constraint(a, pltpu.HBM)


def _run_plan(plan, srcs, lands, send_sems, recv_sems, start, wait):
    copies = plan(srcs, lands)
    if start:
        for i, (src, dst, _, to) in enumerate(copies):
            pltpu.make_async_remote_copy(src_ref=src, dst_ref=dst, send_sem=send_sems.at[i], recv_sem=recv_sems.at[i],
                                         device_id=to, device_id_type=MESH_T).start()
    if wait:
        for i, (src, _, arrives, to) in enumerate(copies):
            cp = pltpu.make_async_remote_copy(src_ref=src, dst_ref=arrives, send_sem=send_sems.at[i],
                                              recv_sem=recv_sems.at[i], device_id=to, device_id_type=MESH_T)
            cp.wait_send()
            cp.wait_recv()


def _exchange_fused(name, plan, n_copies, srcs, land_shapes):
    ns, nl = len(srcs), len(land_shapes)

    def body(*refs):
        _run_plan(plan, refs[:ns], refs[ns:ns + nl], refs[ns + nl], refs[ns + nl + 1], True, True)

    return pl.pallas_call(
        body, name=name, in_specs=[ANY] * ns, out_specs=[ANY] * nl, out_shape=list(land_shapes),
        scratch_shapes=[pltpu.SemaphoreType.DMA((n_copies,)), pltpu.SemaphoreType.DMA((n_copies,))],
    )(*srcs)


def _exchange_start(name, plan, n_copies, srcs, land_shapes, deps):
    ns, nl = len(srcs), len(land_shapes)
    n_in = ns + nl + len(deps)

    def body(*refs):
        send_sems, recv_sems = refs[n_in], refs[n_in + 1]
        token = refs[-1]
        _run_plan(plan, refs[:ns], refs[ns:ns + nl], send_sems, recv_sems, True, False)
        token[...] = jnp.zeros_like(token)

    thru = [pltpu.HBM(a.shape, a.dtype) for a in srcs] + [pltpu.HBM(a.shape, a.dtype) for a in land_shapes]
    outs = pl.pallas_call(
        body, name=name,
        out_shape=(pltpu.SemaphoreType.DMA((n_copies,)), pltpu.SemaphoreType.DMA((n_copies,)), *thru, _sds((8, LANE))),
        in_specs=[HBM_SPEC] * (ns + nl) + [ANY] * len(deps),
        out_specs=(SEM_SPEC, SEM_SPEC, *[HBM_SPEC] * (ns + nl), pl.BlockSpec(memory_space=pltpu.VMEM)),
        input_output_aliases={i: 2 + i for i in range(ns + nl)},
        compiler_params=pltpu.CompilerParams(has_side_effects=pltpu.SideEffectType.DATAFLOW_SIDE_EFFECTING),
    )(*[_hbm(a) for a in srcs], *[_hbm(lax.empty(a.shape, a.dtype)) for a in land_shapes], *deps)
    return (outs[0], outs[1]), list(outs[2:2 + ns]), list(outs[2 + ns:2 + ns + nl]), outs[-1]


def _exchange_wait(name, plan, sems, srcs, lands, after):
    ns, nl = len(srcs), len(lands)

    def body(*refs):
        _run_plan(plan, refs[:ns], refs[ns:ns + nl], refs[ns + nl], refs[ns + nl + 1], False, True)

    outs = pl.pallas_call(
        body, name=name,
        out_shape=[pltpu.HBM(a.shape, a.dtype) for a in list(srcs) + list(lands)],
        in_specs=[HBM_SPEC] * (ns + nl) + [SEM_SPEC, SEM_SPEC] + [ANY] * len(after), out_specs=[HBM_SPEC] * (ns + nl),
        input_output_aliases={i: i for i in range(ns + nl)},
        compiler_params=pltpu.CompilerParams(has_side_effects=pltpu.SideEffectType.DATAFLOW_SIDE_EFFECTING),
    )(*srcs, *lands, sems[0], sems[1], *after)
    return list(outs[:ns]), list(outs[ns:])


def _xchg_begin(name, plan, n_copies, srcs, land_shapes, split, deps=()):
    if not split:
        return dict(split=False, srcs=list(srcs), lands=_exchange_fused(name, plan, n_copies, srcs, land_shapes),
                    token=jnp.zeros((8, LANE), F32))
    sems, srcs_t, lands_t, token = _exchange_start(name + "_start", plan, n_copies, srcs, land_shapes, list(deps))
    return dict(split=True, name=name, plan=plan, sems=sems, srcs=srcs_t, lands=lands_t, token=token)


def _xchg_end(h, after):
    if not h["split"]:
        return h["srcs"], h["lands"]
    return _exchange_wait(h["name"] + "_wait", h["plan"], h["sems"], h["srcs"], h["lands"], after)


def _other_chips():
    x, y, c = _place()
    return [(1 - x, y), (x, 1 - y), (1 - x, 1 - y)]


def _gather_plan(srcs, lands):
    x, y, c = _place()
    me = 2 * x + y
    return [(srcs[a], lands[a].at[me], lands[a].at[2 * cx + cy], (cx, cy, c))
            for (cx, cy) in _other_chips() for a in range(len(srcs))]


def _gather_begin(shards, split, tag, deps=()):
    shapes = [_sds((N_CHIPS,) + a.shape, a.dtype) for a in shards]
    return _xchg_begin(f"gather_{tag}", _gather_plan, 3 * len(shards), shards, shapes, split, deps)


def _gather_end(h, after):
    shards, lands = _xchg_end(h, after)
    me = 2 * lax.axis_index("x") + lax.axis_index("y")
    return [lax.dynamic_update_index_in_dim(g, s, me, 0) for g, s in zip(lands, shards)]


def _swap_plan(srcs, lands):
    x, y, c = _place()
    return [(srcs[a].at[:, 1 - c], lands[a], lands[a], (x, y, 1 - c)) for a in range(len(srcs))]


def _chips_plan(srcs, lands):
    x, y, c = _place()
    me = 2 * x + y
    return [(srcs[a].at[2 * cx + cy], lands[a].at[me], lands[a].at[2 * cx + cy], (cx, cy, c))
            for (cx, cy) in _other_chips() for a in range(len(srcs))]


def _share_plan(srcs, lands):
    x, y, c = _place()
    return [(srcs[a], lands[a].at[c], lands[a].at[1 - c], (x, y, 1 - c)) for a in range(len(srcs))]


def _allreduce_small(slab, dep=None):
    r = slab.shape[0]

    def body(s_ref, o_ref, gath, send_sems, recv_sems):
        x, y, c = _place()
        me = 4 * x + 2 * y + c
        gath[me] = s_ref[...]
        cps = []
        for rel in range(1, N_DEV):
            px = 1 - x if rel & 4 else x
            py = 1 - y if rel & 2 else y
            pc = 1 - c if rel & 1 else c
            cp = pltpu.make_async_remote_copy(src_ref=s_ref, dst_ref=gath.at[me], send_sem=send_sems.at[rel - 1],
                                              recv_sem=recv_sems.at[rel - 1], device_id=(px, py, pc), device_id_type=MESH_T)
            cp.start()
            cps.append(cp)
        for cp in cps:
            cp.wait()
        acc = gath[0]
        for d in range(1, N_DEV):
            acc = acc + gath[d]
        o_ref[...] = acc

    vm = pl.BlockSpec(memory_space=pltpu.VMEM)
    return _call_after(
        dep, body, (slab,), name="allreduce_small", in_specs=[vm], out_specs=vm, out_shape=_sds((r, LANE)),
        scratch_shapes=[pltpu.VMEM((N_DEV, r, LANE), F32), pltpu.SemaphoreType.DMA((N_DEV - 1,)),
                        pltpu.SemaphoreType.DMA((N_DEV - 1,))],
    )


def _add_mine(g4s, recvs, half):
    n = len(g4s)

    def body(h_ref, *refs):
        for g_ref, r_ref, o_ref in zip(refs[:n], refs[n:2 * n], refs[2 * n:]):
            o_ref[0] = (g_ref[0, 0] + r_ref[0]).astype(o_ref.dtype)

    dims = [g.shape[2:] for g in g4s]
    return pl.pallas_call(
        body, name="add_mine",
        grid_spec=pltpu.PrefetchScalarGridSpec(
            num_scalar_prefetch=1, grid=(N_CHIPS,),
            in_specs=[pl.BlockSpec((1, 1) + d, lambda j, h: (j, h[0], 0, 0)) for d in dims]
            + [pl.BlockSpec((1,) + d, lambda j, h: (j, 0, 0)) for d in dims],
            out_specs=[pl.BlockSpec((1,) + d, lambda j, h: (j, 0, 0)) for d in dims]),
        out_shape=[_sds((N_CHIPS,) + d, PAYLOAD) for d in dims],
        compiler_params=_params(("parallel",)),
    )(half, *g4s, *recvs)


def _add_chips(es, ps, me):
    n = len(es)

    def body(m_ref, *refs):
        for e_ref, p_ref, o_ref in zip(refs[:n], refs[n:2 * n], refs[2 * n:]):
            own = p_ref[0].astype(F32)
            acc = None
            for s in range(N_CHIPS):
                t = jnp.where(m_ref[0] == s, own, e_ref[s].astype(F32))
                acc = t if acc is None else acc + t
            o_ref[...] = acc

    dims = [e.shape[1:] for e in es]
    return pl.pallas_call(
        body, name="add_chips",
        grid_spec=pltpu.PrefetchScalarGridSpec(
            num_scalar_prefetch=1, grid=(1,),
            in_specs=[pl.BlockSpec((N_CHIPS,) + d, lambda i, m: (0, 0, 0)) for d in dims]
            + [pl.BlockSpec((1,) + d, lambda i, m: (m[0], 0, 0)) for d in dims],
            out_specs=[pl.BlockSpec(d, lambda i, m: (0, 0)) for d in dims]),
        out_shape=[_sds(d) for d in dims],
        compiler_params=_params(("arbitrary",)),
    )(me, *es, *ps)


def _rs_begin(gs, split, tag):
    g4 = [g.reshape(N_CHIPS, 2, g.shape[0] // (2 * N_CHIPS), g.shape[1]) for g in gs]
    h = _xchg_begin(f"rs_swap_{tag}", _swap_plan, len(gs), g4, [_sds((N_CHIPS,) + g.shape[2:]) for g in g4], split)
    return dict(h=h, split=split, tag=tag, shapes=[g.shape for g in gs])


def _rs_add_mine(st, after):
    g4, recv = _xchg_end(st["h"], after)
    half = jnp.reshape(lax.axis_index("c"), (1,)).astype(jnp.int32)
    ps = _add_mine(g4, recv, half)
    st["h"] = _xchg_begin(f"rs_chips_{st['tag']}", _chips_plan, 3 * len(ps), ps, [_sds(p.shape, p.dtype) for p in ps], st["split"])
    return st


def _rs_add_chips(st, after):
    ps, es = _xchg_end(st["h"], after)
    me = jnp.reshape(2 * lax.axis_index("x") + lax.axis_index("y"), (1,)).astype(jnp.int32)
    fs = _add_chips(es, ps, me)
    st["h"] = _xchg_begin(f"rs_share_{st['tag']}", _share_plan, len(fs), fs, [_sds((2,) + f.shape) for f in fs], st["split"])
    return st


def _rs_end(st, after):
    fs, ss = _xchg_end(st["h"], after)
    c = lax.axis_index("c")
    return [lax.dynamic_update_index_in_dim(s, f, c, 0).reshape(shp[0] // N_CHIPS, shp[1])
            for s, f, shp in zip(ss, fs, st["shapes"])]


WEIGHTS = ["norm_g", "w_in", "conv_a_w", "ssd_conv_w", "ssd_conv_b", "ssd_dt_bias", "ssd_a_log", "ssd_d", "ssd_norm_g",
           "mla_q_norm_g", "w_qb", "mla_kv_norm_g", "w_kvb", "w_out", "final_norm_g"]
BIG = ["w_in", "w_qb", "w_kvb", "w_out"]
SLAB_ROWS = 128
SMALL_ROWS = 72


def _to_slab(parts, rows):
    flat = jnp.concatenate([p.reshape(-1) for p in parts])
    return jnp.pad(flat, (0, rows * LANE - flat.shape[0])).reshape(rows, LANE)


def _from_slab(slab, shapes):
    flat = slab.reshape(-1)
    out, off = [], 0
    for shp in shapes:
        n = int(np.prod(shp))
        out.append(flat[off:off + n].reshape(shp))
        off += n
    return out


def kernel(x, positions, norm_g, w_in, conv_a_w, ssd_conv_w, ssd_conv_b, ssd_dt_bias, ssd_a_log, ssd_d, ssd_norm_g, mla_q_norm_g, w_qb, mla_kv_norm_g, w_kvb, w_out, final_norm_g, loss_target, m_norm_g, m_w_in, m_conv_a_w, m_ssd_conv_w, m_ssd_conv_b, m_ssd_dt_bias, m_ssd_a_log, m_ssd_d, m_ssd_norm_g, m_mla_q_norm_g, m_w_qb, m_mla_kv_norm_g, m_w_kvb, m_w_out, m_final_norm_g, v_norm_g, v_w_in, v_conv_a_w, v_ssd_conv_w, v_ssd_conv_b, v_ssd_dt_bias, v_ssd_a_log, v_ssd_d, v_ssd_norm_g, v_mla_q_norm_g, v_w_qb, v_mla_kv_norm_g, v_w_kvb, v_w_out, v_final_norm_g):
    w = dict(norm_g=norm_g, w_in=w_in, conv_a_w=conv_a_w, ssd_conv_w=ssd_conv_w, ssd_conv_b=ssd_conv_b,
             ssd_dt_bias=ssd_dt_bias, ssd_a_log=ssd_a_log, ssd_d=ssd_d, ssd_norm_g=ssd_norm_g, mla_q_norm_g=mla_q_norm_g,
             w_qb=w_qb, mla_kv_norm_g=mla_kv_norm_g, w_kvb=w_kvb, w_out=w_out, final_norm_g=final_norm_g)
    mom = dict(norm_g=m_norm_g, w_in=m_w_in, conv_a_w=m_conv_a_w, ssd_conv_w=m_ssd_conv_w, ssd_conv_b=m_ssd_conv_b,
               ssd_dt_bias=m_ssd_dt_bias, ssd_a_log=m_ssd_a_log, ssd_d=m_ssd_d, ssd_norm_g=m_ssd_norm_g,
               mla_q_norm_g=m_mla_q_norm_g, w_qb=m_w_qb, mla_kv_norm_g=m_mla_kv_norm_g, w_kvb=m_w_kvb, w_out=m_w_out,
               final_norm_g=m_final_norm_g)
    var = dict(norm_g=v_norm_g, w_in=v_w_in, conv_a_w=v_conv_a_w, ssd_conv_w=v_ssd_conv_w, ssd_conv_b=v_ssd_conv_b,
               ssd_dt_bias=v_ssd_dt_bias, ssd_a_log=v_ssd_a_log, ssd_d=v_ssd_d, ssd_norm_g=v_ssd_norm_g,
               mla_q_norm_g=v_mla_q_norm_g, w_qb=v_w_qb, mla_kv_norm_g=v_mla_kv_norm_g, w_kvb=v_w_kvb, w_out=v_w_out,
               final_norm_g=v_final_norm_g)
    chip = 2 * lax.axis_index("x") + lax.axis_index("y")

    def early_shard(l, zero):
        pack = jnp.pad(conv_a_w[l], ((0, 5), (0, 192))) + jnp.pad(ssd_conv_w[l], ((3, 1), (0, 32)))
        return [(_perm_cols(w_in[l]) + zero).astype(MXU), pack + zero]

    def late_shard(l, zero):
        return [(w_out[l] + zero).astype(MXU), (w_qb[l].T + zero).astype(MXU), (w_kvb[l].T + zero).astype(MXU)]

    def early_weights(l, gathered):
        g_in, g_conv = gathered
        return dict(
            norm_g=norm_g[l][None], w_in=g_in.reshape(D_MODEL, NCOL),
            conv_a_w=jnp.concatenate([g_conv[j, 0:3, 0:64] for j in range(N_CHIPS)], axis=1),
            ssd_conv_w=jnp.concatenate([g_conv[j, 3:7, 0:224] for j in range(N_CHIPS)], axis=1),
            ssd_conv_b=ssd_conv_b[l][None], sc=_ssd_scalars(ssd_dt_bias[l], ssd_a_log[l], ssd_d[l]),
            g_ssd=ssd_norm_g[l][None], gq=mla_q_norm_g[l][None], gkv=mla_kv_norm_g[l][None])

    def late_weights(gathered):
        g_out, g_qb, g_kvb = gathered
        return dict(wq=_wq_layout(g_qb.reshape(MLA_HEADS * 96, Q_LORA)), wkv=_wkv_layout(g_kvb.reshape(MLA_HEADS * LANE, KV_LORA)),
                    w_out=g_out.reshape(D_MODEL, D_MODEL))

    def late_grads(dw_out, dwq, dwkv):
        wq = jnp.pad(_wq_unlayout(dwq).reshape(N_CHIPS, 144, Q_LORA), ((0, 0), (0, 16), (0, 0)))
        return [dw_out, wq.reshape(N_CHIPS * 160, Q_LORA), _wkv_unlayout(dwkv)]

    def large_grads(g):
        return [g["w_in"]] + late_grads(g["w_out"], g["wq"], g["wkv"])

    gather_a0 = _gather_begin(early_shard(0, 0.0), True, "a0")
    zero = gather_a0["token"][0, 0]
    cos, sin = _rope_tables(positions[0] + zero.astype(jnp.int32))
    late0, shards1 = late_shard(0, zero), early_shard(1, zero) + late_shard(1, zero)
    opt_in = {nm: [w[nm], mom[nm] + zero, var[nm] + zero] for nm in BIG}
    lw0 = early_weights(0, _gather_end(gather_a0, [cos, sin] + late0 + shards1 + [a for nm in BIG for a in opt_in[nm][1:]]))
    gather_b0 = _gather_begin(late0, True, "b0")
    gather_1 = _gather_begin(shards1, True, "1", [gather_b0["token"]])
    x1, sv0, lw0 = _layer_fwd(x[0], lw0, cos, sin, gather_1["token"],
                              lambda ya, y_ssd: late_weights(_gather_end(gather_b0, [ya, y_ssd])))
    g1 = _gather_end(gather_1, [x1])
    x2, sv1, lw1 = _layer_fwd(x1, {**early_weights(1, g1[:2]), **late_weights(g1[2:])}, cos, sin)
    dx, dgf, loss = _loss_head(x2, final_norm_g[None], loss_target[0])

    dx, lg1, _, _ = _layer_bwd(dx, lw1, sv1, cos, sin)
    grad_x, lg0, red1, rs0_late = _layer_bwd(dx, lw0, sv0, cos, sin, _rs_begin(large_grads(lg1), True, 1),
                                             lambda *g: _rs_begin(late_grads(*g), True, "0l"))
    rs0 = _rs_begin([lg0["w_in"]], True, 0)
    lg = [lg0, lg1]
    grad = {}

    small_names = ["norm_g", "conv_a_w", "ssd_conv_w", "ssd_conv_b", "sc", "g_ssd", "gq", "gkv"]
    parts = [loss[0, 0:1], dgf]
    for l in range(DEPTH):
        parts += [lg[l][nm][:3, DT_LANE:DT_LANE + SSD_HEADS] if nm == "sc" else lg[l][nm] for nm in small_names]
    shapes = [(1,), (D_MODEL,)] + [(D_MODEL,), (3, D_CONV_A), (4, N_XBC), (N_XBC,), (3, SSD_HEADS), (D_SSD,), (Q_LORA,), (KV_LORA,)] * DEPTH
    red_slab = _allreduce_small(_to_slab(parts, SLAB_ROWS), rs0["h"]["token"])
    rs0 = _rs_add_mine(rs0, [red_slab])
    red = _from_slab(red_slab + rs0["h"]["token"][0, 0], shapes)
    loss_out = red[0][0]
    grad["final_norm_g"] = red[1]
    per = [red[2 + 8 * l:10 + 8 * l] for l in range(DEPTH)]
    grad["norm_g"] = jnp.stack([per[l][0] for l in range(DEPTH)])
    grad["conv_a_w"] = lax.dynamic_slice_in_dim(jnp.stack([per[l][1] for l in range(DEPTH)]), chip * 64, 64, axis=2)
    grad["ssd_conv_w"] = lax.dynamic_slice_in_dim(jnp.stack([per[l][2] for l in range(DEPTH)]), chip * 224, 224, axis=2)
    grad["ssd_conv_b"] = jnp.stack([per[l][3] for l in range(DEPTH)])
    grad["ssd_dt_bias"] = jnp.stack([per[l][4][0] for l in range(DEPTH)])
    grad["ssd_a_log"] = jnp.stack([per[l][4][1] for l in range(DEPTH)])
    grad["ssd_d"] = jnp.stack([per[l][4][2] for l in range(DEPTH)])
    grad["ssd_norm_g"] = jnp.stack([per[l][5] for l in range(DEPTH)])
    grad["mla_q_norm_g"] = jnp.stack([per[l][6] for l in range(DEPTH)])
    grad["mla_kv_norm_g"] = jnp.stack([per[l][7] for l in range(DEPTH)])

    delta, new_m, new_v = {}, {}, {}
    small = [nm for nm in WEIGHTS if nm not in BIG]
    sshapes = [w[nm].shape for nm in small]
    (d, mo, vo), = _adamw(*[[_to_slab([a[nm] for nm in small], SMALL_ROWS)] for a in (w, grad, mom, var)])
    small_out = list(zip(small, _from_slab(d, sshapes), _from_slab(mo, sshapes), _from_slab(vo, sshapes)))
    for nm, dv, mv, vv in small_out:
        delta[nm], new_m[nm], new_v[nm] = dv, mv, vv

    red0_late = _rs_end(rs0_late, [red_slab])
    red0 = _rs_end(_rs_add_chips(rs0, [a for row in small_out for a in row[1:]] + [grad[nm] for nm in small]), []) + red0_late
    r_in, r_out, r_qb, r_kvb = [jnp.stack([a, b]) for a, b in zip(red0, red1)]
    grad.update(w_in=_unperm_cols(r_in), w_out=r_out, w_qb=jnp.swapaxes(r_qb[:, :144], 1, 2), w_kvb=jnp.swapaxes(r_kvb, 1, 2))
    big_out = _adamw([opt_in[nm][0] for nm in BIG], [grad[nm] for nm in BIG], [opt_in[nm][1] for nm in BIG],
                     [opt_in[nm][2] for nm in BIG])
    for nm, (dv, mv, vv) in zip(BIG, big_out):
        delta[nm], new_m[nm], new_v[nm] = dv, mv, vv

    return (loss_out, grad_x[None], *[grad[nm] for nm in WEIGHTS], *[delta[nm] for nm in WEIGHTS],
            *[new_m[nm] for nm in WEIGHTS], *[new_v[nm] for nm in WEIGHTS])
```

```python
import functools
import math

import numpy as np
import jax
import jax.numpy as jnp
from jax import lax
from jax.experimental import pallas as pl
from jax.experimental.pallas import tpu as pltpu

F32 = jnp.float32
MXU = jnp.bfloat16

D_MODEL = 1024
DEPTH = 2
D_CONV_A = 256
D_SSD = 384
SSD_HEADS = 6
SSD_BC = 256
SSD_CHUNK = 128
SSD_CHUNKS_PER_STEP = 2
SSD_NORM_EPS = 1e-5
MLA_HEADS = 6
Q_LORA = 256
KV_LORA = 128
QK_NOPE = 64
QK_ROPE = 32
V_DIM = 64
D_MLA = 384
ROPE_BASE = 10000.0
NORM_EPS = 1e-6
IN_COLS = 3110
LANE = 128

O_AH, O_AB, O_AC, O_AZ = 0, 256, 512, 768
O_XBC = 1024
O_SZ = 1920
O_CQA = 2304
O_CKV = 2560
O_CZ = 2688
O_TAIL = 3072
NCOL = 3200
N_XBC = D_SSD + 2 * SSD_BC
DT_LANE = 32
ROPE_LANE = 64

ADAM_LR, ADAM_B1, ADAM_B2, ADAM_EPS, ADAM_WD, ADAM_STEP = 0.001, 0.9, 0.999, 1e-08, 0.01, 10

VMEM_LIMIT = 56 * 1024 * 1024
MESH_T = pl.DeviceIdType.MESH


def _dot(a, b):
    return jnp.dot(a.astype(MXU), b.astype(MXU), preferred_element_type=F32)


def _dot_nt(a, b):
    return lax.dot_general(a.astype(MXU), b.astype(MXU), (((1,), (1,)), ((), ())), preferred_element_type=F32)


def _dot_tn(a, b):
    return lax.dot_general(a.astype(MXU), b.astype(MXU), (((0,), (0,)), ((), ())), preferred_element_type=F32)


def _dot_hi(a, b):
    return jnp.dot(a, b, precision=lax.Precision.HIGHEST, preferred_element_type=F32)


def _dot_hi_tn(a, b):
    return lax.dot_general(a, b, (((0,), (0,)), ((), ())), precision=lax.Precision.HIGHEST, preferred_element_type=F32)


def _sigmoid(z):
    return 1.0 / (1.0 + jnp.exp(-z))


def _silu(z):
    return z * _sigmoid(z)


def _dsilu(z):
    s = _sigmoid(z)
    return s * (1.0 + z * (1.0 - s))


def _softplus(z):
    e = jnp.exp(-jnp.abs(z))
    return jnp.maximum(z, 0.0) + jnp.where(e < 1e-3, e * (1.0 - 0.5 * e), jnp.log(1.0 + e))


def _iota(shape, dim):
    return lax.broadcasted_iota(jnp.int32, shape, dim)


def _shift_down(u, k):
    if k == 0:
        return u
    return jnp.where(_iota(u.shape, 0) >= k, pltpu.roll(u, k, 0), 0.0)


def _shift_up(u, k):
    if k == 0:
        return u
    n = u.shape[0]
    return jnp.where(_iota(u.shape, 0) < n - k, pltpu.roll(u, n - k, 0), 0.0)


def _rope_swap(t):
    lane = _iota(t.shape, 1)
    lo = (lane >= ROPE_LANE) & (lane < ROPE_LANE + 16)
    hi = (lane >= ROPE_LANE + 16) & (lane < ROPE_LANE + 32)
    return jnp.where(lo, pltpu.roll(t, LANE - 16, 1), jnp.where(hi, pltpu.roll(t, 16, 1), 0.0))


def _params(sem=None):
    return pltpu.CompilerParams(dimension_semantics=sem, vmem_limit_bytes=VMEM_LIMIT)


def _full(shape):
    nd = len(shape)
    return pl.BlockSpec(shape, lambda *_: (0,) * nd)


def _sds(shape, dtype=F32):
    return jax.ShapeDtypeStruct(shape, dtype)


def _tile(s):
    return min(256, s)


def _row(ts, w):
    return pl.BlockSpec((ts, w), lambda i: (i, 0))


def _gate_cols(ts, off):
    return pl.BlockSpec((ts, D_SSD), lambda i, _o=off // D_SSD: (i, _o))


def _col(s, off):
    return pl.BlockSpec((s, LANE), lambda j, _o=off // LANE: (0, _o + j))


def _call_after(dep, body, args, *, in_specs, **kw):
    if dep is None:
        return pl.pallas_call(body, in_specs=in_specs, **kw)(*args)
    n = len(args)

    def body_dep(*refs):
        body(*refs[:n], *refs[n + 1:])

    return pl.pallas_call(body_dep, in_specs=list(in_specs) + [pl.BlockSpec(memory_space=pl.ANY)], **kw)(*args, dep)


def _rms(c, g):
    r = lax.rsqrt(jnp.mean(c * c, axis=-1, keepdims=True) + NORM_EPS)
    return c * r * g, r


def _rms_bwd(dn, c, r, g):
    ch = c * r
    dch = dn * g
    dc = r * (dch - ch * jnp.mean(dch * ch, axis=-1, keepdims=True))
    return dc, jnp.sum(dn * ch, axis=0, keepdims=True)


def _inproj_fwd(x, g, w, dep=None):
    s = x.shape[0]
    ts = _tile(s)

    def body(x_ref, g_ref, w_ref, proj_ref, h_ref, r_ref):
        hn, r = _rms(x_ref[...], g_ref[...])
        h = hn.astype(MXU)
        h_ref[...] = h
        r_ref[...] = r
        proj_ref[...] = jnp.dot(h, w_ref[...], preferred_element_type=F32)

    return _call_after(
        dep, body, (x, g, w), name="inproj_fwd", grid=(s // ts,),
        in_specs=[_row(ts, D_MODEL), _full((1, D_MODEL)), _full((D_MODEL, NCOL))],
        out_specs=[_row(ts, NCOL), _row(ts, D_MODEL), _row(ts, 1)],
        out_shape=[_sds((s, NCOL)), _sds((s, D_MODEL), MXU), _sds((s, 1))],
        compiler_params=_params(("parallel",)),
    )


def _conva_fwd(proj, w):
    s = proj.shape[0]

    def body(h_ref, b_ref, c_ref, z_ref, w_ref, y_ref):
        u = c_ref[...] * h_ref[...]
        wv = w_ref[...]
        cv = wv[2:3, :] * u + wv[1:2, :] * _shift_down(u, 1) + wv[0:1, :] * _shift_down(u, 2)
        y_ref[...] = b_ref[...] * cv * _silu(z_ref[...])

    return pl.pallas_call(
        body, name="conva_fwd", grid=(D_CONV_A // LANE,),
        in_specs=[_col(s, O_AH), _col(s, O_AB), _col(s, O_AC), _col(s, O_AZ), pl.BlockSpec((3, LANE), lambda j: (0, j))],
        out_specs=pl.BlockSpec((s, LANE), lambda j: (0, j)),
        out_shape=_sds((s, D_CONV_A)),
        compiler_params=_params(("parallel",)),
    )(proj, proj, proj, proj, w)


def _sconv_pre(u, wv, bv):
    return (wv[3:4, :] * u + wv[2:3, :] * _shift_down(u, 1) + wv[1:2, :] * _shift_down(u, 2)
            + wv[0:1, :] * _shift_down(u, 3) + bv)


def _sconv_fwd(proj, w, b):
    s = proj.shape[0]

    def body(u_ref, w_ref, b_ref, o_ref):
        o_ref[...] = _silu(_sconv_pre(u_ref[...], w_ref[...], b_ref[...]))

    return pl.pallas_call(
        body, name="sconv_fwd", grid=(N_XBC // LANE,),
        in_specs=[_col(s, O_XBC), pl.BlockSpec((4, LANE), lambda j: (0, j)), pl.BlockSpec((1, LANE), lambda j: (0, j))],
        out_specs=pl.BlockSpec((s, LANE), lambda j: (0, j)),
        out_shape=_sds((s, N_XBC)),
        compiler_params=_params(("parallel",)),
    )(proj, w, b)


def _ssd_chunk_common(tail, sc):
    l = SSD_CHUNK
    lane = _iota((l, LANE), 1)
    row = _iota((l, LANE), 0)
    tri = (row >= lane).astype(F32)
    a_row = -jnp.exp(sc[1:2, :])
    pre = tail + sc[0:1, :]
    dt = _softplus(pre)
    a_cs = _dot_hi(tri, dt * a_row)
    return lane, row, tri, a_row, pre, dt, a_cs, a_cs.T


def _pick_col(m, lane, k):
    return jnp.sum(jnp.where(lane == k, m, 0.0), axis=1, keepdims=True)


def _pick_row(m, row, k):
    return jnp.sum(jnp.where(row == k, m, 0.0), axis=0, keepdims=True)


def _ssd_fwd(xbc, proj, sc):
    s = xbc.shape[0]
    nc = s // SSD_CHUNK
    l = SSD_CHUNK
    cps = SSD_CHUNKS_PER_STEP

    def body(xbc_ref, tail_ref, sc_ref, y_ref, st_ref, state):
        @pl.when(pl.program_id(0) == 0)
        def _():
            state[...] = jnp.zeros_like(state)

        sc_v = sc_ref[...]
        lane1 = _iota((1, LANE), 1)
        rowp = _iota((LANE, 1), 0)
        d_row = sc_v[2:3, :]
        states = [state[j] for j in range(3)]
        for u in range(cps):
            r = slice(u * l, (u + 1) * l)
            lane, row, _, _, _, dt, a_cs, a_t = _ssd_chunk_common(tail_ref[r, :], sc_v)
            for j in range(3):
                st_ref[u, j] = states[j]
            for j in range(3):
                xpair = xbc_ref[r, LANE * j:LANE * (j + 1)]
                sp = states[j]
                ypair = jnp.zeros((l, LANE), F32)
                new_s = jnp.zeros((LANE, LANE), F32)
                decay = jnp.zeros((LANE, 1), F32)
                for half in range(2):
                    h = 2 * j + half
                    g = h // 3
                    hm = (lane < 64) if half == 0 else (lane >= 64)
                    hrow = (rowp < 64) if half == 0 else (rowp >= 64)
                    ac = _pick_col(a_cs, lane, DT_LANE + h)
                    ar = _pick_row(a_t, row, DT_LANE + h)
                    dtc = _pick_col(dt, lane, DT_LANE + h)
                    alast = jnp.sum(jnp.where(lane1 == l - 1, ar, 0.0), axis=1, keepdims=True)
                    dh = jnp.sum(jnp.where(lane1 == DT_LANE + h, d_row, 0.0), axis=1, keepdims=True)
                    xm = jnp.where(hm, xpair, 0.0)
                    xd = xm * dtc
                    bm = xbc_ref[r, D_SSD + LANE * g:D_SSD + LANE * (g + 1)]
                    cm = xbc_ref[r, D_SSD + SSD_BC + LANE * g:D_SSD + SSD_BC + LANE * (g + 1)]
                    lm = jnp.where(row >= lane, jnp.exp(jnp.minimum(ac - ar, 0.0)), 0.0)
                    y_diag = _dot(_dot_nt(cm, bm) * lm, xd)
                    y_off = jnp.where(hm, _dot_nt(cm, sp), 0.0) * jnp.exp(ac)
                    ypair = ypair + y_diag + y_off + xm * dh
                    new_s = new_s + _dot_tn(xd * jnp.exp(alast - ac), bm)
                    decay = jnp.where(hrow, jnp.exp(alast), decay)
                states[j] = sp * decay + new_s
                y_ref[r, LANE * j:LANE * (j + 1)] = ypair
        for j in range(3):
            state[j] = states[j]

    return pl.pallas_call(
        body, name="ssd_fwd", grid=(nc // cps,),
        in_specs=[pl.BlockSpec((cps * l, N_XBC), lambda c: (c, 0)),
                  pl.BlockSpec((cps * l, LANE), lambda c: (c, O_TAIL // LANE)), _full((8, LANE))],
        out_specs=[pl.BlockSpec((cps * l, D_SSD), lambda c: (c, 0)), pl.BlockSpec((cps, 3, LANE, LANE), lambda c: (c, 0, 0, 0))],
        out_shape=[_sds((s, D_SSD)), _sds((nc, 3, LANE, LANE))],
        scratch_shapes=[pltpu.VMEM((3, LANE, LANE), F32)],
        compiler_params=_params(("arbitrary",)),
    )(xbc, proj, sc)


def _mla_prep_fwd(proj, gq, gkv, wq, wkv, cos, sin):
    s = proj.shape[0]
    ts = _tile(s)
    nh = MLA_HEADS

    def body(cqa_ref, ckv_ref, tail_ref, gq_ref, gkv_ref, wq_ref, wkv_ref, cos_ref, sin_ref,
             q_ref, k_ref, v_ref, qn_ref, kvn_ref, rq_ref, rkv_ref):
        qn, rq = _rms(cqa_ref[...], gq_ref[...])
        kvn, rkv = _rms(ckv_ref[...], gkv_ref[...])
        qn = qn.astype(MXU)
        kvn = kvn.astype(MXU)
        qn_ref[...] = qn
        kvn_ref[...] = kvn
        rq_ref[...] = rq
        rkv_ref[...] = rkv
        q = _dot_nt(qn, wq_ref[...])
        kv = _dot_nt(kvn, wkv_ref[...])
        cosv = cos_ref[...]
        sinv = sin_ref[...]
        lane = _iota((ts, LANE), 1)
        rope_lanes = (lane >= ROPE_LANE) & (lane < ROPE_LANE + QK_ROPE)
        kr = jnp.where(rope_lanes, pltpu.roll(tail_ref[...], ROPE_LANE, 1), 0.0)
        kr = kr * cosv + _rope_swap(kr) * sinv
        for h in range(nh):
            qh = q[:, LANE * h:LANE * (h + 1)]
            q_ref[h] = ((qh * cosv + _rope_swap(qh) * sinv) * ATT_SCALE).astype(MXU)
            k_ref[h] = (kv[:, LANE * h:LANE * (h + 1)] + kr).astype(MXU)
            v_ref[h] = kv[:, LANE * (nh + h):LANE * (nh + h + 1)].astype(MXU)

    head = pl.BlockSpec((nh, ts, LANE), lambda i: (0, i, 0))
    return pl.pallas_call(
        body, name="mla_prep_fwd", grid=(s // ts,),
        in_specs=[pl.BlockSpec((ts, Q_LORA), lambda i: (i, O_CQA // Q_LORA)),
                  pl.BlockSpec((ts, KV_LORA), lambda i: (i, O_CKV // KV_LORA)),
                  pl.BlockSpec((ts, LANE), lambda i: (i, O_TAIL // LANE)),
                  _full((1, Q_LORA)), _full((1, KV_LORA)), _full((nh * LANE, Q_LORA)), _full((2 * nh * LANE, KV_LORA)),
                  _row(ts, LANE), _row(ts, LANE)],
        out_specs=[head, head, head, _row(ts, Q_LORA), _row(ts, KV_LORA), _row(ts, 1), _row(ts, 1)],
        out_shape=[_sds((nh, s, LANE), MXU)] * 3 + [_sds((s, Q_LORA), MXU), _sds((s, KV_LORA), MXU), _sds((s, 1)), _sds((s, 1))],
        compiler_params=_params(("parallel",)),
    )(proj, proj, proj, gq, gkv, wq, wkv, cos, sin)


ATT_SCALE = (QK_NOPE + QK_ROPE) ** -0.5
NEG = -1e30


def _att_tile(s, most):
    return min(most, s // 2)


ATT_FWD_TILE = 1024
ATT_BWD_TILE = 512


def _attn_fwd(q, k, v):
    nh, s, _ = q.shape
    tq = _att_tile(s, ATT_FWD_TILE)
    nq = s // tq

    def body(q_ref, k_ref, v_ref, o_ref, lse_ref):
        i = pl.program_id(1)
        rowi = _iota((tq, tq), 0)
        coli = _iota((tq, tq), 1)
        zero = (jnp.full((tq, 1), NEG, F32), jnp.zeros((tq, 1), F32), jnp.zeros((tq, LANE), F32))
        state = [zero, zero]
        done = [zero, zero]
        for t in range(nq + 1):
            first = t <= i
            qblk = jnp.where(first, i, nq - 1 - i)
            kblk = jnp.where(first, t, t - i - 1)
            qoff = pl.multiple_of(qblk * tq, tq)
            koff = pl.multiple_of(kblk * tq, tq)
            keep = coli <= rowi + jnp.where(kblk == qblk, 0, tq)
            restart = t == i + 1
            for hh in range(2):
                m, lsum, acc = state[hh]
                if t > 0:
                    done[hh] = tuple(jnp.where(restart, a, b) for a, b in zip(state[hh], done[hh]))
                    m = jnp.where(restart, NEG, m)
                    lsum = jnp.where(restart, 0.0, lsum)
                    acc = jnp.where(restart, 0.0, acc)
                sc = _dot_nt(q_ref[hh, pl.ds(qoff, tq), :], k_ref[hh, pl.ds(koff, tq), :])
                sc = jnp.where(keep, sc, NEG)
                m_new = jnp.maximum(m, jnp.max(sc, axis=1, keepdims=True))
                p = jnp.exp(sc - m_new)
                alpha = jnp.exp(m - m_new)
                lsum = alpha * lsum + jnp.sum(p, axis=1, keepdims=True)
                acc = alpha * acc + _dot(p, v_ref[hh, pl.ds(koff, tq), :])
                state[hh] = (m_new, lsum, acc)
        for blk, res in ((i, done), (nq - 1 - i, state)):
            off = pl.multiple_of(blk * tq, tq)
            out = None
            for hh in range(2):
                m, lsum, acc = res[hh]
                o = acc * (1.0 / lsum)
                lse_ref[hh, pl.ds(off, tq), :] = m + jnp.log(lsum)
                out = o if hh == 0 else out + pltpu.roll(o, V_DIM, 1)
            o_ref[pl.ds(off, tq), :] = out

    pair = pl.BlockSpec((2, s, LANE), lambda j, i: (j, 0, 0))
    return pl.pallas_call(
        body, name="attn_fwd", grid=(nh // 2, nq // 2),
        in_specs=[pair, pair, pair],
        out_specs=[pl.BlockSpec((s, LANE), lambda j, i: (0, j)), pl.BlockSpec((2, s, 1), lambda j, i: (j, 0, 0))],
        out_shape=[_sds((s, D_MLA)), _sds((nh, s, 1))],
        compiler_params=_params(("parallel", "arbitrary")),
    )(q, k, v)


def _ssd_gate(y_ssd, s_z, g):
    yz = y_ssd * _silu(s_z)
    g0 = _iota(yz.shape, 1) < D_SSD // 2
    sq = yz * yz
    ms0 = jnp.sum(jnp.where(g0, sq, 0.0), axis=1, keepdims=True) / (D_SSD // 2)
    ms1 = jnp.sum(jnp.where(g0, 0.0, sq), axis=1, keepdims=True) / (D_SSD // 2)
    r = jnp.where(g0, lax.rsqrt(ms0 + SSD_NORM_EPS), lax.rsqrt(ms1 + SSD_NORM_EPS))
    nrm = yz * r
    return nrm * g, nrm, r, g0


def _outproj_fwd(x, proj, ya, y_ssd, o, g_ssd, w):
    s = x.shape[0]
    ts = _tile(s)

    def body(x_ref, sz_ref, cz_ref, ya_ref, ys_ref, o_ref, g_ref, w_ref, xo_ref, y_ref):
        yb = _ssd_gate(ys_ref[...], sz_ref[...], g_ref[...])[0]
        yc = o_ref[...] * _silu(cz_ref[...])
        y = jnp.concatenate([ya_ref[...], yb, yc], axis=1).astype(MXU)
        y_ref[...] = y
        xo_ref[...] = x_ref[...] + jnp.dot(y, w_ref[...], preferred_element_type=F32)

    return pl.pallas_call(
        body, name="outproj_fwd", grid=(s // ts,),
        in_specs=[_row(ts, D_MODEL), _gate_cols(ts, O_SZ), _gate_cols(ts, O_CZ), _row(ts, D_CONV_A), _row(ts, D_SSD),
                  _row(ts, D_MLA), _full((1, D_SSD)), _full((D_MODEL, D_MODEL))],
        out_specs=[_row(ts, D_MODEL), _row(ts, D_MODEL)],
        out_shape=[_sds((s, D_MODEL)), _sds((s, D_MODEL), MXU)],
        compiler_params=_params(("parallel",)),
    )(x, proj, proj, ya, y_ssd, o, g_ssd, w)


def _loss_head(x, g, tgt):
    s = x.shape[0]
    ts = _tile(s)

    def body(x_ref, g_ref, t_ref, dx_ref, dg_ref, loss_ref):
        @pl.when(pl.program_id(0) == 0)
        def _():
            dg_ref[...] = jnp.zeros_like(dg_ref)
            loss_ref[...] = jnp.zeros_like(loss_ref)

        xv = x_ref[...]
        gv = g_ref[...]
        yn, r = _rms(xv, gv)
        e = yn - t_ref[...]
        loss_ref[...] += jnp.sum(jnp.sum(e * e, axis=1, keepdims=True), axis=0, keepdims=True) * (0.5 / D_MODEL)
        dx, dg = _rms_bwd(e * (1.0 / D_MODEL), xv, r, gv)
        dx_ref[...] = dx
        dg_ref[...] += dg

    return pl.pallas_call(
        body, name="loss_head", grid=(s // ts,),
        in_specs=[_row(ts, D_MODEL), _full((1, D_MODEL)), _row(ts, D_MODEL)],
        out_specs=[_row(ts, D_MODEL), _full((1, D_MODEL)), _full((1, LANE))],
        out_shape=[_sds((s, D_MODEL)), _sds((1, D_MODEL)), _sds((1, LANE))],
        compiler_params=_params(("arbitrary",)),
    )(x, g, tgt)


def _outproj_bwd(dout, y, w, proj, y_ssd, o, g_ssd, dep=None):
    s = dout.shape[0]
    ts = _tile(s)

    def body(dout_ref, y_ref, w_ref, sz_ref, cz_ref, ys_ref, o_ref, g_ref,
             dya_ref, dys_ref, dsz_ref, dattn_ref, dcz_ref, dg_ref, dw_ref):
        @pl.when(pl.program_id(0) == 0)
        def _():
            dw_ref[...] = jnp.zeros_like(dw_ref)
            dg_ref[...] = jnp.zeros_like(dg_ref)

        dout_b = dout_ref[...].astype(MXU)
        dw_ref[...] += _dot_tn(y_ref[...], dout_b)
        dy = _dot_nt(dout_b, w_ref[...])
        dya_ref[...] = dy[:, :D_CONV_A]
        dyb = dy[:, D_CONV_A:D_CONV_A + D_SSD]
        sz = sz_ref[...]
        ys = ys_ref[...]
        gv = g_ref[...]
        _, nrm, r, g0 = _ssd_gate(ys, sz, gv)
        dg_ref[...] += jnp.sum(dyb * nrm, axis=0, keepdims=True)
        dn = dyb * gv
        t = dn * nrm
        mean = jnp.where(g0, jnp.sum(jnp.where(g0, t, 0.0), axis=1, keepdims=True),
                         jnp.sum(jnp.where(g0, 0.0, t), axis=1, keepdims=True)) / (D_SSD // 2)
        dyz = r * (dn - nrm * mean)
        dys_ref[...] = dyz * _silu(sz)
        dsz_ref[...] = (dyz * ys * _dsilu(sz)).astype(MXU)
        dyc = dy[:, D_CONV_A + D_SSD:]
        cz = cz_ref[...]
        dattn_ref[...] = dyc * _silu(cz)
        dcz_ref[...] = (dyc * o_ref[...] * _dsilu(cz)).astype(MXU)

    return _call_after(
        dep, body, (dout, y, w, proj, proj, y_ssd, o, g_ssd), name="outproj_bwd", grid=(s // ts,),
        in_specs=[_row(ts, D_MODEL), _row(ts, D_MODEL), _full((D_MODEL, D_MODEL)), _gate_cols(ts, O_SZ), _gate_cols(ts, O_CZ),
                  _row(ts, D_SSD), _row(ts, D_MLA), _full((1, D_SSD))],
        out_specs=[_row(ts, D_CONV_A), _row(ts, D_SSD), _row(ts, D_SSD), _row(ts, D_MLA), _row(ts, D_MLA),
                   _full((1, D_SSD)), _full((D_MODEL, D_MODEL))],
        out_shape=[_sds((s, D_CONV_A)), _sds((s, D_SSD)), _sds((s, D_SSD), MXU), _sds((s, D_MLA)), _sds((s, D_MLA), MXU),
                   _sds((1, D_SSD)), _sds((D_MODEL, D_MODEL))],
        compiler_params=_params(("arbitrary",)),
    )


def _attn_bwd(q, k, v, o, d_o, lse, dep=None):
    nh, s, _ = q.shape
    tq = _att_tile(s, ATT_BWD_TILE)
    nq = s // tq

    def body(q_ref, k_ref, v_ref, o_ref, do_ref, lse_ref, dq_ref, dk_ref, dv_ref, dop, delta):
        i = pl.program_id(1)

        @pl.when(i == 0)
        def _():
            lane = _iota((s, LANE), 1)
            for hh in range(2):
                dov = do_ref[...]
                ov = o_ref[...]
                if hh == 1:
                    dov = pltpu.roll(dov, V_DIM, 1)
                    ov = pltpu.roll(ov, V_DIM, 1)
                dov = jnp.where(lane < V_DIM, dov, 0.0)
                dop[hh] = dov.astype(MXU)
                delta[hh] = jnp.sum(dov * ov, axis=1, keepdims=True)
                dq_ref[hh] = jnp.zeros((s, LANE), F32)

        rowi = _iota((tq, tq), 0)
        coli = _iota((tq, tq), 1)
        z = jnp.zeros((tq, LANE), F32)
        state = [(z, z), (z, z)]
        done = [(z, z), (z, z)]
        for t in range(nq + 1):
            first = t <= nq - 1 - i
            kblk = jnp.where(first, i, nq - 1 - i)
            qblk = jnp.where(first, i + t, t - 1)
            qoff = pl.multiple_of(qblk * tq, tq)
            koff = pl.multiple_of(kblk * tq, tq)
            keep = coli <= rowi + jnp.where(kblk == qblk, 0, tq)
            restart = t == nq - i
            for hh in range(2):
                dk, dv = state[hh]
                if t > 0:
                    done[hh] = tuple(jnp.where(restart, a, b) for a, b in zip(state[hh], done[hh]))
                    dk = jnp.where(restart, 0.0, dk)
                    dv = jnp.where(restart, 0.0, dv)
                kb = k_ref[hh, pl.ds(koff, tq), :]
                qb = q_ref[hh, pl.ds(qoff, tq), :]
                dob = dop[hh, pl.ds(qoff, tq), :]
                sc = jnp.where(keep, _dot_nt(qb, kb), NEG)
                p = jnp.exp(sc - lse_ref[hh, pl.ds(qoff, tq), :])
                dp = _dot_nt(dob, v_ref[hh, pl.ds(koff, tq), :])
                ds = p * (dp - delta[hh, pl.ds(qoff, tq), :])
                dq_ref[hh, pl.ds(qoff, tq), :] += _dot(ds, kb)
                state[hh] = (dk + _dot_tn(ds, qb), dv + _dot_tn(p, dob))
        for blk, res in ((i, done), (nq - 1 - i, state)):
            off = pl.multiple_of(blk * tq, tq)
            for hh in range(2):
                dk_ref[hh, pl.ds(off, tq), :] = res[hh][0]
                dv_ref[hh, pl.ds(off, tq), :] = res[hh][1]

    pair = pl.BlockSpec((2, s, LANE), lambda j, i: (j, 0, 0))
    return _call_after(
        dep, body, (q, k, v, o, d_o, lse), name="attn_bwd", grid=(nh // 2, nq // 2),
        in_specs=[pair, pair, pair, pl.BlockSpec((s, LANE), lambda j, i: (0, j)), pl.BlockSpec((s, LANE), lambda j, i: (0, j)),
                  pl.BlockSpec((2, s, 1), lambda j, i: (j, 0, 0))],
        out_specs=[pair, pair, pair],
        out_shape=[_sds((nh, s, LANE))] * 3,
        scratch_shapes=[pltpu.VMEM((2, s, LANE), MXU), pltpu.VMEM((2, s, 1), F32)],
        compiler_params=_params(("parallel", "arbitrary")),
    )


def _ssd_bwd(xbc, proj, sc, states, dy, dep=None):
    s = xbc.shape[0]
    nc = s // SSD_CHUNK
    l = SSD_CHUNK
    cps = SSD_CHUNKS_PER_STEP

    def body(xbc_ref, tail_ref, sc_ref, st_ref, dy_ref, dxbc_ref, dtail_ref, dsc_ref, dstate):
        @pl.when(pl.program_id(0) == 0)
        def _():
            dstate[...] = jnp.zeros_like(dstate)
            dsc_ref[...] = jnp.zeros_like(dsc_ref)

        sc_v = sc_ref[...]
        lane1 = _iota((1, LANE), 1)
        rowp = _iota((LANE, 1), 0)
        rowl = _iota((l, 1), 0)
        d_row = sc_v[2:3, :]
        dstates = [dstate[j] for j in range(3)]
        for u in reversed(range(cps)):
            dstates = chunk(u, xbc_ref, tail_ref, sc_v, st_ref, dy_ref, dxbc_ref, dtail_ref, dsc_ref, dstates,
                            lane1, rowp, rowl, d_row)
        for j in range(3):
            dstate[j] = dstates[j]

    def chunk(u, xbc_ref, tail_ref, sc_v, st_ref, dy_ref, dxbc_ref, dtail_ref, dsc_ref, dstates, lane1, rowp, rowl, d_row):
        r = slice(u * l, (u + 1) * l)
        dstates = list(dstates)
        lane, row, tri, a_row, pre, dt, a_cs, a_t = _ssd_chunk_common(tail_ref[r, :], sc_v)
        da_col = jnp.zeros((l, LANE), F32)
        da_row = jnp.zeros((LANE, l), F32)
        dt_x = jnp.zeros((l, LANE), F32)
        dd_row = jnp.zeros((1, LANE), F32)
        db = [jnp.zeros((l, LANE), F32), jnp.zeros((l, LANE), F32)]
        dc = [jnp.zeros((l, LANE), F32), jnp.zeros((l, LANE), F32)]
        for j in range(3):
            xpair = xbc_ref[r, LANE * j:LANE * (j + 1)]
            dypair = dy_ref[r, LANE * j:LANE * (j + 1)]
            sp = st_ref[u, j]
            dsp = dstates[j]
            dxpair = jnp.zeros((l, LANE), F32)
            ds_new = jnp.zeros((LANE, LANE), F32)
            decay = jnp.zeros((LANE, 1), F32)
            for half in range(2):
                h = 2 * j + half
                g = h // 3
                hm = (lane < 64) if half == 0 else (lane >= 64)
                hrow = (rowp < 64) if half == 0 else (rowp >= 64)
                ac = _pick_col(a_cs, lane, DT_LANE + h)
                ar = _pick_row(a_t, row, DT_LANE + h)
                dtc = _pick_col(dt, lane, DT_LANE + h)
                alast = jnp.sum(jnp.where(lane1 == l - 1, ar, 0.0), axis=1, keepdims=True)
                dh = jnp.sum(jnp.where(lane1 == DT_LANE + h, d_row, 0.0), axis=1, keepdims=True)
                xm = jnp.where(hm, xpair, 0.0)
                xd = xm * dtc
                dym = jnp.where(hm, dypair, 0.0)
                bm = xbc_ref[r, D_SSD + LANE * g:D_SSD + LANE * (g + 1)]
                cm = xbc_ref[r, D_SSD + SSD_BC + LANE * g:D_SSD + SSD_BC + LANE * (g + 1)]
                lm = jnp.where(row >= lane, jnp.exp(jnp.minimum(ac - ar, 0.0)), 0.0)
                e_in = jnp.exp(ac)
                f_out = jnp.exp(alast - ac)
                e_last = jnp.exp(alast)
                m = _dot_nt(cm, bm) * lm
                y_off = jnp.where(hm, _dot_nt(cm, sp), 0.0) * e_in
                dm = _dot_nt(dym, xd)
                dxd = _dot_tn(m, dym)
                dg = dm * lm
                dye = dym * e_in
                dc[g] = dc[g] + _dot(dg, bm) + _dot(dye, sp)
                db[g] = db[g] + _dot_tn(dg, cm)
                qm = dm * m
                dac = jnp.sum(qm, axis=1, keepdims=True) + jnp.sum(dym * y_off, axis=1, keepdims=True)
                dar = -jnp.sum(qm, axis=0, keepdims=True)
                dxf = jnp.where(hm, _dot_nt(bm, dsp), 0.0)
                db[g] = db[g] + _dot(xd * f_out, dsp)
                dxd = dxd + dxf * f_out
                df = jnp.sum(dxf * xd, axis=1, keepdims=True) * f_out
                dac = dac - df
                s_last = jnp.sum(df, axis=0, keepdims=True)
                ss = jnp.sum(jnp.where(hrow, dsp * sp, 0.0), axis=1, keepdims=True)
                s_last = s_last + e_last * jnp.sum(ss, axis=0, keepdims=True)
                dac = dac + jnp.where(rowl == l - 1, s_last, 0.0)
                ds_new = ds_new + _dot_tn(dye, cm)
                decay = jnp.where(hrow, e_last, decay)
                dxpair = dxpair + dxd * dtc + dym * dh
                dt_x = dt_x + jnp.where(lane == DT_LANE + h, jnp.sum(dxd * xm, axis=1, keepdims=True), 0.0)
                dsum = jnp.sum(jnp.sum(dym * xm, axis=1, keepdims=True), axis=0, keepdims=True)
                dd_row = dd_row + jnp.where(lane1 == DT_LANE + h, dsum, 0.0)
                da_col = da_col + jnp.where(lane == DT_LANE + h, dac, 0.0)
                da_row = da_row + jnp.where(row == DT_LANE + h, dar, 0.0)
            dstates[j] = dsp * decay + ds_new
            dxbc_ref[r, LANE * j:LANE * (j + 1)] = dxpair
        for g in range(2):
            dxbc_ref[r, D_SSD + LANE * g:D_SSD + LANE * (g + 1)] = db[g]
            dxbc_ref[r, D_SSD + SSD_BC + LANE * g:D_SSD + SSD_BC + LANE * (g + 1)] = dc[g]
        dla = _dot_hi_tn(tri, da_col + da_row.T)
        ddt = dt_x + dla * a_row
        dpre = ddt * _sigmoid(pre)
        dtm = (lane >= DT_LANE) & (lane < DT_LANE + SSD_HEADS)
        dtail_ref[r, :] = jnp.where(dtm, dpre, 0.0).astype(MXU)
        dtm1 = (lane1 >= DT_LANE) & (lane1 < DT_LANE + SSD_HEADS)
        dsc_ref[0:1, :] += jnp.where(dtm1, jnp.sum(dpre, axis=0, keepdims=True), 0.0)
        dsc_ref[1:2, :] += jnp.where(dtm1, jnp.sum(dla * dt, axis=0, keepdims=True) * a_row, 0.0)
        dsc_ref[2:3, :] += dd_row
        return dstates

    rev = lambda c: nc // cps - 1 - c
    return _call_after(
        dep, body, (xbc, proj, sc, states, dy), name="ssd_bwd", grid=(nc // cps,),
        in_specs=[pl.BlockSpec((cps * l, N_XBC), lambda c: (rev(c), 0)),
                  pl.BlockSpec((cps * l, LANE), lambda c: (rev(c), O_TAIL // LANE)), _full((8, LANE)),
                  pl.BlockSpec((cps, 3, LANE, LANE), lambda c: (rev(c), 0, 0, 0)),
                  pl.BlockSpec((cps * l, D_SSD), lambda c: (rev(c), 0))],
        out_specs=[pl.BlockSpec((cps * l, N_XBC), lambda c: (rev(c), 0)), pl.BlockSpec((cps * l, LANE), lambda c: (rev(c), 0)),
                   _full((8, LANE))],
        out_shape=[_sds((s, N_XBC)), _sds((s, LANE), MXU), _sds((8, LANE))],
        scratch_shapes=[pltpu.VMEM((3, LANE, LANE), F32)],
        compiler_params=_params(("arbitrary",)),
    )


def _sconv_bwd(proj, w, b, dxbc, dep=None):
    s = proj.shape[0]

    def body(u_ref, w_ref, b_ref, d_ref, du_ref, dw_ref, db_ref):
        u = u_ref[...]
        wv = w_ref[...]
        dpre = d_ref[...] * _dsilu(_sconv_pre(u, wv, b_ref[...]))
        du_ref[...] = (wv[3:4, :] * dpre + wv[2:3, :] * _shift_up(dpre, 1) + wv[1:2, :] * _shift_up(dpre, 2)
                       + wv[0:1, :] * _shift_up(dpre, 3)).astype(MXU)
        for k in range(4):
            dw_ref[k:k + 1, :] = jnp.sum(dpre * _shift_down(u, 3 - k), axis=0, keepdims=True)
        db_ref[...] = jnp.sum(dpre, axis=0, keepdims=True)

    blk = pl.BlockSpec((s, LANE), lambda j: (0, j))
    return _call_after(
        dep, body, (proj, w, b, dxbc), name="sconv_bwd", grid=(N_XBC // LANE,),
        in_specs=[_col(s, O_XBC), pl.BlockSpec((4, LANE), lambda j: (0, j)), pl.BlockSpec((1, LANE), lambda j: (0, j)), blk],
        out_specs=[blk, pl.BlockSpec((4, LANE), lambda j: (0, j)), pl.BlockSpec((1, LANE), lambda j: (0, j))],
        out_shape=[_sds((s, N_XBC), MXU), _sds((4, N_XBC)), _sds((1, N_XBC))],
        compiler_params=_params(("parallel",)),
    )


def _conva_bwd(proj, w, dya, dep=None):
    s = proj.shape[0]

    def body(h_ref, b_ref, c_ref, z_ref, w_ref, d_ref, da_ref, dw_ref):
        ah, ab, acv, az = h_ref[...], b_ref[...], c_ref[...], z_ref[...]
        wv = w_ref[...]
        u = acv * ah
        cv = wv[2:3, :] * u + wv[1:2, :] * _shift_down(u, 1) + wv[0:1, :] * _shift_down(u, 2)
        dy = d_ref[...]
        sz = _silu(az)
        da_ref[1] = (dy * cv * sz).astype(MXU)
        da_ref[3] = (dy * ab * cv * _dsilu(az)).astype(MXU)
        dcv = dy * ab * sz
        du = wv[2:3, :] * dcv + wv[1:2, :] * _shift_up(dcv, 1) + wv[0:1, :] * _shift_up(dcv, 2)
        da_ref[0] = (du * acv).astype(MXU)
        da_ref[2] = (du * ah).astype(MXU)
        for k in range(3):
            dw_ref[k:k + 1, :] = jnp.sum(dcv * _shift_down(u, 2 - k), axis=0, keepdims=True)

    return _call_after(
        dep, body, (proj, proj, proj, proj, w, dya), name="conva_bwd", grid=(D_CONV_A // LANE,),
        in_specs=[_col(s, O_AH), _col(s, O_AB), _col(s, O_AC), _col(s, O_AZ), pl.BlockSpec((3, LANE), lambda j: (0, j)),
                  pl.BlockSpec((s, LANE), lambda j: (0, j))],
        out_specs=[pl.BlockSpec((4, s, LANE), lambda j: (0, 0, j)), pl.BlockSpec((3, LANE), lambda j: (0, j))],
        out_shape=[_sds((4, s, D_CONV_A), MXU), _sds((3, D_CONV_A))],
        compiler_params=_params(("parallel",)),
    )


def _mla_prep_bwd(dq, dk, dv, proj, qn, kvn, rq, rkv, gq, gkv, wq, wkv, cos, sin):
    s = proj.shape[0]
    ts = _tile(s)
    nh = MLA_HEADS

    def body(dq_ref, dk_ref, dv_ref, cqa_ref, ckv_ref, qn_ref, kvn_ref, rq_ref, rkv_ref, gq_ref, gkv_ref,
             wq_ref, wkv_ref, cos_ref, sin_ref, dcqa_ref, dckv_ref, dtail_ref, dwq_ref, dwkv_ref, dgq_ref, dgkv_ref):
        @pl.when(pl.program_id(0) == 0)
        def _():
            dwq_ref[...] = jnp.zeros_like(dwq_ref)
            dwkv_ref[...] = jnp.zeros_like(dwkv_ref)
            dgq_ref[...] = jnp.zeros_like(dgq_ref)
            dgkv_ref[...] = jnp.zeros_like(dgkv_ref)

        cosv = cos_ref[...]
        sinv = sin_ref[...]
        lane = _iota((ts, LANE), 1)
        rope_lanes = (lane >= ROPE_LANE) & (lane < ROPE_LANE + QK_ROPE)

        def unrope(gr):
            return gr * cosv + _rope_swap(gr * sinv)

        dqs, dks, dvs = [], [], []
        dkr = jnp.zeros((ts, LANE), F32)
        for h in range(nh):
            dqs.append(unrope(dq_ref[h] * ATT_SCALE).astype(MXU))
            dkh = dk_ref[h]
            dks.append(jnp.where(lane < QK_NOPE, dkh, 0.0).astype(MXU))
            dkr = dkr + jnp.where(rope_lanes, dkh, 0.0)
            dvs.append(dv_ref[h].astype(MXU))
        dtail_ref[...] = pltpu.roll(jnp.where(rope_lanes, unrope(dkr), 0.0), ROPE_LANE, 1).astype(MXU)
        dq_all = jnp.concatenate(dqs, axis=1)
        dkv_all = jnp.concatenate(dks + dvs, axis=1)
        dwq_ref[...] += _dot_tn(dq_all, qn_ref[...])
        dwkv_ref[...] += _dot_tn(dkv_all, kvn_ref[...])
        dcqa, dgq = _rms_bwd(_dot(dq_all, wq_ref[...]), cqa_ref[...], rq_ref[...], gq_ref[...])
        dckv, dgkv = _rms_bwd(_dot(dkv_all, wkv_ref[...]), ckv_ref[...], rkv_ref[...], gkv_ref[...])
        dcqa_ref[...] = dcqa.astype(MXU)
        dckv_ref[...] = dckv.astype(MXU)
        dgq_ref[...] += dgq
        dgkv_ref[...] += dgkv

    head = pl.BlockSpec((nh, ts, LANE), lambda i: (0, i, 0))
    return pl.pallas_call(
        body, name="mla_prep_bwd", grid=(s // ts,),
        in_specs=[head, head, head,
                  pl.BlockSpec((ts, Q_LORA), lambda i: (i, O_CQA // Q_LORA)),
                  pl.BlockSpec((ts, KV_LORA), lambda i: (i, O_CKV // KV_LORA)),
                  _row(ts, Q_LORA), _row(ts, KV_LORA), _row(ts, 1), _row(ts, 1),
                  _full((1, Q_LORA)), _full((1, KV_LORA)), _full((nh * LANE, Q_LORA)), _full((2 * nh * LANE, KV_LORA)),
                  _row(ts, LANE), _row(ts, LANE)],
        out_specs=[_row(ts, Q_LORA), _row(ts, KV_LORA), _row(ts, LANE), _full((nh * LANE, Q_LORA)),
                   _full((2 * nh * LANE, KV_LORA)), _full((1, Q_LORA)), _full((1, KV_LORA))],
        out_shape=[_sds((s, Q_LORA), MXU), _sds((s, KV_LORA), MXU), _sds((s, LANE), MXU), _sds((nh * LANE, Q_LORA)),
                   _sds((2 * nh * LANE, KV_LORA)), _sds((1, Q_LORA)), _sds((1, KV_LORA))],
        compiler_params=_params(("arbitrary",)),
    )(dq, dk, dv, proj, proj, qn, kvn, rq, rkv, gq, gkv, wq, wkv, cos, sin)


def _inproj_bwd(da4, dsz, dxbc_in, dcqa, dckv, dcz, dtail_a, dtail_b, w, x, rstd, g, dout, dep=None):
    s = x.shape[0]
    ts = _tile(s)

    def body(da_ref, dsz_ref, dxbc_ref, dcqa_ref, dckv_ref, dcz_ref, dta_ref, dtb_ref, w_ref, x_ref, r_ref, g_ref, dout_ref,
             dproj_ref, dx_ref, dg_ref):
        @pl.when(pl.program_id(0) == 0)
        def _():
            dg_ref[...] = jnp.zeros_like(dg_ref)

        dproj = jnp.concatenate(
            [da_ref[0], da_ref[1], da_ref[2], da_ref[3], dxbc_ref[...], dsz_ref[...], dcqa_ref[...], dckv_ref[...],
             dcz_ref[...], dta_ref[...] + dtb_ref[...]], axis=1)
        dproj_ref[...] = dproj
        dh = _dot_nt(dproj, w_ref[...])
        dx, dg = _rms_bwd(dh, x_ref[...], r_ref[...], g_ref[...])
        dx_ref[...] = dout_ref[...] + dx
        dg_ref[...] += dg

    return _call_after(
        dep, body, (da4, dsz, dxbc_in, dcqa, dckv, dcz, dtail_a, dtail_b, w, x, rstd, g, dout), name="inproj_bwd", grid=(s // ts,),
        in_specs=[pl.BlockSpec((4, ts, D_CONV_A), lambda i: (0, i, 0)), _row(ts, D_SSD), _row(ts, N_XBC), _row(ts, Q_LORA),
                  _row(ts, KV_LORA), _row(ts, D_MLA), _row(ts, LANE), _row(ts, LANE), _full((D_MODEL, NCOL)),
                  _row(ts, D_MODEL), _row(ts, 1), _full((1, D_MODEL)), _row(ts, D_MODEL)],
        out_specs=[_row(ts, NCOL), _row(ts, D_MODEL), _full((1, D_MODEL))],
        out_shape=[_sds((s, NCOL), MXU), _sds((s, D_MODEL)), _sds((1, D_MODEL))],
        compiler_params=_params(("arbitrary",)),
    )


DWIN_BLOCK = 640


def _dwin(h, dproj, dep=None):
    s = h.shape[0]

    def body(h_ref, d_ref, o_ref):
        o_ref[...] = _dot_tn(h_ref[...], d_ref[...])

    return _call_after(
        dep, body, (h, dproj), name="dwin", grid=(NCOL // DWIN_BLOCK,),
        in_specs=[_full((s, D_MODEL)), pl.BlockSpec((s, DWIN_BLOCK), lambda j: (0, j))],
        out_specs=pl.BlockSpec((D_MODEL, DWIN_BLOCK), lambda j: (0, j)),
        out_shape=_sds((D_MODEL, NCOL)),
        compiler_params=_params(("parallel",)),
    )


def _adamw(ws, gs, ms, vs, whole):
    n = len(ws)
    bc1 = 1.0 - ADAM_B1 ** ADAM_STEP
    bc2 = 1.0 - ADAM_B2 ** ADAM_STEP

    def body(*refs):
        ins, outs = refs[:4 * n], refs[4 * n:]
        for a in range(n):
            w_ref, g_ref, m_ref, v_ref = ins[a], ins[n + a], ins[2 * n + a], ins[3 * n + a]
            gv = g_ref[...]
            mn = ADAM_B1 * m_ref[...] + (1.0 - ADAM_B1) * gv
            vn = ADAM_B2 * v_ref[...] + (1.0 - ADAM_B2) * (gv * gv)
            outs[n + a][...] = mn
            outs[2 * n + a][...] = vn
            outs[a][...] = -ADAM_LR * ((mn / bc1) / (jnp.sqrt(vn / bc2) + ADAM_EPS) + ADAM_WD * w_ref[...])

    if whole:
        grid, blks = (1,), [pl.BlockSpec(w.shape, lambda i, _n=w.ndim: (0,) * _n) for w in ws]
    else:
        grid = (ws[0].shape[0], 2)
        blks = [pl.BlockSpec((1, w.shape[1] // 2, w.shape[2]), lambda i, k: (i, k, 0)) for w in ws]
    out = pl.pallas_call(
        body, name="adamw", grid=grid,
        in_specs=blks * 4, out_specs=blks * 3, out_shape=[_sds(w.shape) for w in ws] * 3,
        compiler_params=_params(("parallel",) * len(grid)),
    )(*ws, *gs, *ms, *vs)
    return [(out[a], out[n + a], out[2 * n + a]) for a in range(n)]


COL_MOVES = ((0, 0, 1024), (1024, O_SZ, 384), (1408, O_XBC, 896), (2304, O_TAIL + DT_LANE, 6), (2310, O_CQA, 256),
             (2566, O_CKV, 128), (2694, O_TAIL, 32), (2726, O_CZ, 384))


def _move_cols(w, moves, width):
    out = None
    for src, dst, n in moves:
        piece = jnp.pad(w[..., src:src + n], [(0, 0)] * (w.ndim - 1) + [(dst, width - dst - n)])
        out = piece if out is None else out + piece
    return out


def _perm_cols(w):
    return _move_cols(w, COL_MOVES, NCOL)


def _unperm_cols(g):
    return _move_cols(g, [(dst, src, n) for src, dst, n in COL_MOVES], IN_COLS)


def _wq_layout(wt):
    return jnp.pad(wt.reshape(MLA_HEADS, QK_NOPE + QK_ROPE, Q_LORA), ((0, 0), (0, 32), (0, 0))).reshape(MLA_HEADS * LANE, Q_LORA)


def _wq_unlayout(g):
    return g.reshape(MLA_HEADS, LANE, Q_LORA)[:, :QK_NOPE + QK_ROPE].reshape(MLA_HEADS * (QK_NOPE + QK_ROPE), Q_LORA)


def _wkv_layout(wt):
    t = wt.reshape(MLA_HEADS, 2, 64, KV_LORA).transpose(1, 0, 2, 3)
    return jnp.pad(t, ((0, 0), (0, 0), (0, 64), (0, 0))).reshape(2 * MLA_HEADS * LANE, KV_LORA)


def _wkv_unlayout(g):
    t = g.reshape(2, MLA_HEADS, LANE, KV_LORA)[:, :, :64]
    return t.transpose(1, 0, 2, 3).reshape(MLA_HEADS * LANE, KV_LORA)


def _rope_tables(positions):
    inv_freq = ROPE_BASE ** (-jnp.arange(0, QK_ROPE, 2, dtype=F32) / QK_ROPE)
    ang = positions.astype(F32)[:, None] * inv_freq
    cos, sin = jnp.cos(ang), jnp.sin(ang)
    s = positions.shape[0]
    one, zero = jnp.ones((s, ROPE_LANE), F32), jnp.zeros((s, ROPE_LANE), F32)
    cos_t = jnp.concatenate([one, cos, cos, one[:, :32]], axis=1)
    sin_t = jnp.concatenate([zero, -sin, sin, zero[:, :32]], axis=1)
    return cos_t, sin_t


def _ssd_scalars(dt_bias, a_log, d_skip):
    return jnp.pad(jnp.stack([dt_bias, a_log, d_skip]), ((0, 5), (DT_LANE, LANE - DT_LANE - SSD_HEADS)))


def _layer_fwd(x, lw, cos, sin, dep=None, late=None):
    proj, h, rstd = _inproj_fwd(x, lw["norm_g"], lw["w_in"], dep)
    ya = _conva_fwd(proj, lw["conv_a_w"])
    xbc = _sconv_fwd(proj, lw["ssd_conv_w"], lw["ssd_conv_b"])
    y_ssd, states = _ssd_fwd(xbc, proj, lw["sc"])
    if late is not None:
        lw = {**lw, **late(ya, y_ssd)}
    q, k, v, qn, kvn, rq, rkv = _mla_prep_fwd(proj, lw["gq"], lw["gkv"], lw["wq"], lw["wkv"], cos, sin)
    o, lse = _attn_fwd(q, k, v)
    x_out, y = _outproj_fwd(x, proj, ya, y_ssd, o, lw["g_ssd"], lw["w_out"])
    saved = dict(x=x, proj=proj, h=h, rstd=rstd, xbc=xbc, y_ssd=y_ssd, states=states, q=q, k=k, v=v, qn=qn, kvn=kvn,
                 rq=rq, rkv=rkv, o=o, lse=lse, y=y)
    return x_out, saved, lw


def _layer_bwd(dout, lw, sv, cos, sin, rs=None, begin_early=None):
    tok = lambda: None if rs is None else rs["h"]["token"]
    dya, dys, dsz, d_o, dcz, dg_ssd, dw_out = _outproj_bwd(dout, sv["y"], lw["w_out"], sv["proj"], sv["y_ssd"], sv["o"],
                                                            lw["g_ssd"], tok())
    if rs is not None:
        rs = _rs_add_mine(rs, [dya])
    dq, dk, dv = _attn_bwd(sv["q"], sv["k"], sv["v"], sv["o"], d_o, sv["lse"], tok())
    dxbc, dtail_s, dsc = _ssd_bwd(sv["xbc"], sv["proj"], lw["sc"], sv["states"], dys, tok())
    da4, dw_conva = _conva_bwd(sv["proj"], lw["conv_a_w"], dya, tok())
    if rs is not None:
        rs = _rs_add_chips(rs, [dq, dxbc, da4])
    du, dw_sconv, db_sconv = _sconv_bwd(sv["proj"], lw["ssd_conv_w"], lw["ssd_conv_b"], dxbc, tok())
    dcqa, dckv, dtail_m, dwq, dwkv, dgq, dgkv = _mla_prep_bwd(
        dq, dk, dv, sv["proj"], sv["qn"], sv["kvn"], sv["rq"], sv["rkv"], lw["gq"], lw["gkv"], lw["wq"], lw["wkv"], cos, sin)
    early = None if begin_early is None else begin_early(dw_out, dwq, dwkv)
    etok = lambda: None if early is None else early["h"]["token"]
    dproj, dx, dg = _inproj_bwd(da4, dsz, du, dcqa, dckv, dcz, dtail_s, dtail_m, lw["w_in"], sv["x"], sv["rstd"],
                                lw["norm_g"], dout, etok())
    reduced = None if rs is None else _rs_end(rs, [du, dcqa, dx])
    if early is not None:
        early = _rs_add_mine(early, [dx])
    dw_in = _dwin(sv["h"], dproj, etok())
    if early is not None:
        early = _rs_add_chips(early, [dw_in])
    grads = dict(norm_g=dg, w_in=dw_in, conv_a_w=dw_conva, ssd_conv_w=dw_sconv, ssd_conv_b=db_sconv, sc=dsc,
                 g_ssd=dg_ssd, gq=dgq, wq=dwq, gkv=dgkv, wkv=dwkv, w_out=dw_out)
    return dx, grads, reduced, early


ANY = pl.BlockSpec(memory_space=pl.ANY)
N_CHIPS = 4
N_DEV = 8


def _place():
    return lax.axis_index("x"), lax.axis_index("y"), lax.axis_index("c")


HBM_SPEC = pl.BlockSpec(memory_space=pltpu.HBM)
SEM_SPEC = pl.BlockSpec(memory_space=pltpu.SEMAPHORE)
PAYLOAD = jnp.bfloat16


def _hbm(a):
    return pltpu.with_memory_space_constraint(a, pltpu.HBM)


def _run_plan(plan, srcs, lands, send_sems, recv_sems, start, wait):
    copies = plan(srcs, lands)
    if start:
        for i, (src, dst, _, to) in enumerate(copies):
            pltpu.make_async_remote_copy(src_ref=src, dst_ref=dst, send_sem=send_sems.at[i], recv_sem=recv_sems.at[i],
                                         device_id=to, device_id_type=MESH_T).start()
    if wait:
        for i, (src, _, arrives, to) in enumerate(copies):
            cp = pltpu.make_async_remote_copy(src_ref=src, dst_ref=arrives, send_sem=send_sems.at[i],
                                              recv_sem=recv_sems.at[i], device_id=to, device_id_type=MESH_T)
            cp.wait_send()
            cp.wait_recv()


def _exchange_fused(name, plan, n_copies, srcs, land_shapes):
    ns, nl = len(srcs), len(land_shapes)

    def body(*refs):
        _run_plan(plan, refs[:ns], refs[ns:ns + nl], refs[ns + nl], refs[ns + nl + 1], True, True)

    return pl.pallas_call(
        body, name=name, in_specs=[ANY] * ns, out_specs=[ANY] * nl, out_shape=list(land_shapes),
        scratch_shapes=[pltpu.SemaphoreType.DMA((n_copies,)), pltpu.SemaphoreType.DMA((n_copies,))],
    )(*srcs)


def _exchange_start(name, plan, n_copies, srcs, land_shapes, deps):
    ns, nl = len(srcs), len(land_shapes)
    n_in = ns + nl + len(deps)

    def body(*refs):
        send_sems, recv_sems = refs[n_in], refs[n_in + 1]
        token = refs[-1]
        _run_plan(plan, refs[:ns], refs[ns:ns + nl], send_sems, recv_sems, True, False)
        token[...] = jnp.zeros_like(token)

    thru = [pltpu.HBM(a.shape, a.dtype) for a in srcs] + [pltpu.HBM(a.shape, a.dtype) for a in land_shapes]
    outs = pl.pallas_call(
        body, name=name,
        out_shape=(pltpu.SemaphoreType.DMA((n_copies,)), pltpu.SemaphoreType.DMA((n_copies,)), *thru, _sds((8, LANE))),
        in_specs=[HBM_SPEC] * (ns + nl) + [ANY] * len(deps),
        out_specs=(SEM_SPEC, SEM_SPEC, *[HBM_SPEC] * (ns + nl), pl.BlockSpec(memory_space=pltpu.VMEM)),
        input_output_aliases={i: 2 + i for i in range(ns + nl)},
        compiler_params=pltpu.CompilerParams(has_side_effects=pltpu.SideEffectType.DATAFLOW_SIDE_EFFECTING),
    )(*[_hbm(a) for a in srcs], *[_hbm(lax.empty(a.shape, a.dtype)) for a in land_shapes], *deps)
    return (outs[0], outs[1]), list(outs[2:2 + ns]), list(outs[2 + ns:2 + ns + nl]), outs[-1]


def _exchange_wait(name, plan, sems, srcs, lands, after):
    ns, nl = len(srcs), len(lands)

    def body(*refs):
        _run_plan(plan, refs[:ns], refs[ns:ns + nl], refs[ns + nl], refs[ns + nl + 1], False, True)

    outs = pl.pallas_call(
        body, name=name,
        out_shape=[pltpu.HBM(a.shape, a.dtype) for a in list(srcs) + list(lands)],
        in_specs=[HBM_SPEC] * (ns + nl) + [SEM_SPEC, SEM_SPEC] + [ANY] * len(after), out_specs=[HBM_SPEC] * (ns + nl),
        input_output_aliases={i: i for i in range(ns + nl)},
        compiler_params=pltpu.CompilerParams(has_side_effects=pltpu.SideEffectType.DATAFLOW_SIDE_EFFECTING),
    )(*srcs, *lands, sems[0], sems[1], *after)
    return list(outs[:ns]), list(outs[ns:])


def _xchg_begin(name, plan, n_copies, srcs, land_shapes, split, deps=()):
    if not split:
        return dict(split=False, srcs=list(srcs), lands=_exchange_fused(name, plan, n_copies, srcs, land_shapes),
                    token=jnp.zeros((8, LANE), F32))
    sems, srcs_t, lands_t, token = _exchange_start(name + "_start", plan, n_copies, srcs, land_shapes, list(deps))
    return dict(split=True, name=name, plan=plan, sems=sems, srcs=srcs_t, lands=lands_t, token=token)


def _xchg_end(h, after):
    if not h["split"]:
        return h["srcs"], h["lands"]
    return _exchange_wait(h["name"] + "_wait", h["plan"], h["sems"], h["srcs"], h["lands"], after)


def _other_chips():
    x, y, c = _place()
    return [(1 - x, y), (x, 1 - y), (1 - x, 1 - y)]


def _gather_plan(srcs, lands):
    x, y, c = _place()
    me = 2 * x + y
    return [(srcs[a], lands[a].at[me], lands[a].at[2 * cx + cy], (cx, cy, c))
            for (cx, cy) in _other_chips() for a in range(len(srcs))]


def _gather_begin(shards, split, tag, deps=()):
    shapes = [_sds((N_CHIPS,) + a.shape, a.dtype) for a in shards]
    return _xchg_begin(f"gather_{tag}", _gather_plan, 3 * len(shards), shards, shapes, split, deps)


def _gather_end(h, after):
    shards, lands = _xchg_end(h, after)
    me = 2 * lax.axis_index("x") + lax.axis_index("y")
    return [lax.dynamic_update_index_in_dim(g, s, me, 0) for g, s in zip(lands, shards)]


def _swap_plan(srcs, lands):
    x, y, c = _place()
    return [(srcs[a].at[:, 1 - c], lands[a], lands[a], (x, y, 1 - c)) for a in range(len(srcs))]


def _chips_plan(srcs, lands):
    x, y, c = _place()
    me = 2 * x + y
    return [(srcs[a].at[2 * cx + cy], lands[a].at[me], lands[a].at[2 * cx + cy], (cx, cy, c))
            for (cx, cy) in _other_chips() for a in range(len(srcs))]


def _share_plan(srcs, lands):
    x, y, c = _place()
    return [(srcs[a], lands[a].at[c], lands[a].at[1 - c], (x, y, 1 - c)) for a in range(len(srcs))]


def _allreduce_small(slab, dep=None):
    r = slab.shape[0]

    def body(s_ref, o_ref, gath, send_sems, recv_sems):
        x, y, c = _place()
        me = 4 * x + 2 * y + c
        gath[me] = s_ref[...]
        cps = []
        for rel in range(1, N_DEV):
            px = 1 - x if rel & 4 else x
            py = 1 - y if rel & 2 else y
            pc = 1 - c if rel & 1 else c
            cp = pltpu.make_async_remote_copy(src_ref=s_ref, dst_ref=gath.at[me], send_sem=send_sems.at[rel - 1],
                                              recv_sem=recv_sems.at[rel - 1], device_id=(px, py, pc), device_id_type=MESH_T)
            cp.start()
            cps.append(cp)
        for cp in cps:
            cp.wait()
        acc = gath[0]
        for d in range(1, N_DEV):
            acc = acc + gath[d]
        o_ref[...] = acc

    vm = pl.BlockSpec(memory_space=pltpu.VMEM)
    return _call_after(
        dep, body, (slab,), name="allreduce_small", in_specs=[vm], out_specs=vm, out_shape=_sds((r, LANE)),
        scratch_shapes=[pltpu.VMEM((N_DEV, r, LANE), F32), pltpu.SemaphoreType.DMA((N_DEV - 1,)),
                        pltpu.SemaphoreType.DMA((N_DEV - 1,))],
    )


def _add_mine(g4s, recvs, half):
    n = len(g4s)

    def body(h_ref, *refs):
        for g_ref, r_ref, o_ref in zip(refs[:n], refs[n:2 * n], refs[2 * n:]):
            o_ref[0] = (g_ref[0, 0] + r_ref[0]).astype(o_ref.dtype)

    dims = [g.shape[2:] for g in g4s]
    return pl.pallas_call(
        body, name="add_mine",
        grid_spec=pltpu.PrefetchScalarGridSpec(
            num_scalar_prefetch=1, grid=(N_CHIPS,),
            in_specs=[pl.BlockSpec((1, 1) + d, lambda j, h: (j, h[0], 0, 0)) for d in dims]
            + [pl.BlockSpec((1,) + d, lambda j, h: (j, 0, 0)) for d in dims],
            out_specs=[pl.BlockSpec((1,) + d, lambda j, h: (j, 0, 0)) for d in dims]),
        out_shape=[_sds((N_CHIPS,) + d, PAYLOAD) for d in dims],
        compiler_params=_params(("parallel",)),
    )(half, *g4s, *recvs)


def _add_chips(es, ps, me):
    n = len(es)

    def body(m_ref, *refs):
        for e_ref, p_ref, o_ref in zip(refs[:n], refs[n:2 * n], refs[2 * n:]):
            own = p_ref[0].astype(F32)
            acc = None
            for s in range(N_CHIPS):
                t = jnp.where(m_ref[0] == s, own, e_ref[s].astype(F32))
                acc = t if acc is None else acc + t
            o_ref[...] = acc

    dims = [e.shape[1:] for e in es]
    return pl.pallas_call(
        body, name="add_chips",
        grid_spec=pltpu.PrefetchScalarGridSpec(
            num_scalar_prefetch=1, grid=(1,),
            in_specs=[pl.BlockSpec((N_CHIPS,) + d, lambda i, m: (0, 0, 0)) for d in dims]
            + [pl.BlockSpec((1,) + d, lambda i, m: (m[0], 0, 0)) for d in dims],
            out_specs=[pl.BlockSpec(d, lambda i, m: (0, 0)) for d in dims]),
        out_shape=[_sds(d) for d in dims],
        compiler_params=_params(("arbitrary",)),
    )(me, *es, *ps)


def _rs_begin(gs, split, tag, deps=()):
    g4 = [g.reshape(N_CHIPS, 2, g.shape[0] // (2 * N_CHIPS), g.shape[1]) for g in gs]
    h = _xchg_begin(f"rs_swap_{tag}", _swap_plan, len(gs), g4, [_sds((N_CHIPS,) + g.shape[2:]) for g in g4], split, deps)
    return dict(h=h, split=split, tag=tag, shapes=[g.shape for g in gs])


def _rs_add_mine(st, after):
    g4, recv = _xchg_end(st["h"], after)
    half = jnp.reshape(lax.axis_index("c"), (1,)).astype(jnp.int32)
    ps = _add_mine(g4, recv, half)
    st["h"] = _xchg_begin(f"rs_chips_{st['tag']}", _chips_plan, 3 * len(ps), ps, [_sds(p.shape, p.dtype) for p in ps], st["split"])
    return st


def _rs_add_chips(st, after):
    ps, es = _xchg_end(st["h"], after)
    me = jnp.reshape(2 * lax.axis_index("x") + lax.axis_index("y"), (1,)).astype(jnp.int32)
    fs = _add_chips(es, ps, me)
    st["h"] = _xchg_begin(f"rs_share_{st['tag']}", _share_plan, len(fs), fs, [_sds((2,) + f.shape) for f in fs], st["split"])
    return st


def _rs_end(st, after):
    fs, ss = _xchg_end(st["h"], after)
    c = lax.axis_index("c")
    return [lax.dynamic_update_index_in_dim(s, f, c, 0).reshape(shp[0] // N_CHIPS, shp[1])
            for s, f, shp in zip(ss, fs, st["shapes"])]


WEIGHTS = ["norm_g", "w_in", "conv_a_w", "ssd_conv_w", "ssd_conv_b", "ssd_dt_bias", "ssd_a_log", "ssd_d", "ssd_norm_g",
           "mla_q_norm_g", "w_qb", "mla_kv_norm_g", "w_kvb", "w_out", "final_norm_g"]
BIG = ["w_in", "w_qb", "w_kvb", "w_out"]
SLAB_ROWS = 128


def _to_slab(parts, rows):
    flat = jnp.concatenate([p.reshape(-1) for p in parts])
    return jnp.pad(flat, (0, rows * LANE - flat.shape[0])).reshape(rows, LANE)


def _from_slab(slab, shapes):
    flat = slab.reshape(-1)
    out, off = [], 0
    for shp in shapes:
        n = int(np.prod(shp))
        out.append(flat[off:off + n].reshape(shp))
        off += n
    return out


def kernel(x, positions, norm_g, w_in, conv_a_w, ssd_conv_w, ssd_conv_b, ssd_dt_bias, ssd_a_log, ssd_d, ssd_norm_g, mla_q_norm_g, w_qb, mla_kv_norm_g, w_kvb, w_out, final_norm_g, loss_target, m_norm_g, m_w_in, m_conv_a_w, m_ssd_conv_w, m_ssd_conv_b, m_ssd_dt_bias, m_ssd_a_log, m_ssd_d, m_ssd_norm_g, m_mla_q_norm_g, m_w_qb, m_mla_kv_norm_g, m_w_kvb, m_w_out, m_final_norm_g, v_norm_g, v_w_in, v_conv_a_w, v_ssd_conv_w, v_ssd_conv_b, v_ssd_dt_bias, v_ssd_a_log, v_ssd_d, v_ssd_norm_g, v_mla_q_norm_g, v_w_qb, v_mla_kv_norm_g, v_w_kvb, v_w_out, v_final_norm_g):
    w = dict(norm_g=norm_g, w_in=w_in, conv_a_w=conv_a_w, ssd_conv_w=ssd_conv_w, ssd_conv_b=ssd_conv_b,
             ssd_dt_bias=ssd_dt_bias, ssd_a_log=ssd_a_log, ssd_d=ssd_d, ssd_norm_g=ssd_norm_g, mla_q_norm_g=mla_q_norm_g,
             w_qb=w_qb, mla_kv_norm_g=mla_kv_norm_g, w_kvb=w_kvb, w_out=w_out, final_norm_g=final_norm_g)
    mom = dict(norm_g=m_norm_g, w_in=m_w_in, conv_a_w=m_conv_a_w, ssd_conv_w=m_ssd_conv_w, ssd_conv_b=m_ssd_conv_b,
               ssd_dt_bias=m_ssd_dt_bias, ssd_a_log=m_ssd_a_log, ssd_d=m_ssd_d, ssd_norm_g=m_ssd_norm_g,
               mla_q_norm_g=m_mla_q_norm_g, w_qb=m_w_qb, mla_kv_norm_g=m_mla_kv_norm_g, w_kvb=m_w_kvb, w_out=m_w_out,
               final_norm_g=m_final_norm_g)
    var = dict(norm_g=v_norm_g, w_in=v_w_in, conv_a_w=v_conv_a_w, ssd_conv_w=v_ssd_conv_w, ssd_conv_b=v_ssd_conv_b,
               ssd_dt_bias=v_ssd_dt_bias, ssd_a_log=v_ssd_a_log, ssd_d=v_ssd_d, ssd_norm_g=v_ssd_norm_g,
               mla_q_norm_g=v_mla_q_norm_g, w_qb=v_w_qb, mla_kv_norm_g=v_mla_kv_norm_g, w_kvb=v_w_kvb, w_out=v_w_out,
               final_norm_g=v_final_norm_g)
    chip = 2 * lax.axis_index("x") + lax.axis_index("y")

    def early_shard(l, zero):
        pack = jnp.pad(conv_a_w[l], ((0, 5), (0, 192))) + jnp.pad(ssd_conv_w[l], ((3, 1), (0, 32)))
        return [(_perm_cols(w_in[l]) + zero).astype(MXU), pack + zero]

    def late_shard(l, zero):
        return [(w_out[l] + zero).astype(MXU), (w_qb[l].T + zero).astype(MXU), (w_kvb[l].T + zero).astype(MXU)]

    def early_weights(l, gathered):
        g_in, g_conv = gathered
        return dict(
            norm_g=norm_g[l][None], w_in=g_in.reshape(D_MODEL, NCOL),
            conv_a_w=jnp.concatenate([g_conv[j, 0:3, 0:64] for j in range(N_CHIPS)], axis=1),
            ssd_conv_w=jnp.concatenate([g_conv[j, 3:7, 0:224] for j in range(N_CHIPS)], axis=1),
            ssd_conv_b=ssd_conv_b[l][None], sc=_ssd_scalars(ssd_dt_bias[l], ssd_a_log[l], ssd_d[l]),
            g_ssd=ssd_norm_g[l][None], gq=mla_q_norm_g[l][None], gkv=mla_kv_norm_g[l][None])

    def late_weights(gathered):
        g_out, g_qb, g_kvb = gathered
        return dict(wq=_wq_layout(g_qb.reshape(MLA_HEADS * 96, Q_LORA)), wkv=_wkv_layout(g_kvb.reshape(MLA_HEADS * LANE, KV_LORA)),
                    w_out=g_out.reshape(D_MODEL, D_MODEL))

    def late_grads(dw_out, dwq, dwkv):
        wq = jnp.pad(_wq_unlayout(dwq).reshape(N_CHIPS, 144, Q_LORA), ((0, 0), (0, 16), (0, 0)))
        return [dw_out, wq.reshape(N_CHIPS * 160, Q_LORA), _wkv_unlayout(dwkv)]

    def large_grads(g):
        return [g["w_in"]] + late_grads(g["w_out"], g["wq"], g["wkv"])

    gather_a0 = _gather_begin(early_shard(0, 0.0), True, "a0")
    zero = gather_a0["token"][0, 0]
    cos, sin = _rope_tables(positions[0] + zero.astype(jnp.int32))
    late0, shards1 = late_shard(0, zero), early_shard(1, zero) + late_shard(1, zero)
    opt_in = {nm: [w[nm], mom[nm] + zero, var[nm] + zero] for nm in BIG}
    lw0 = early_weights(0, _gather_end(gather_a0, [cos, sin] + late0 + shards1 + [a for nm in BIG for a in opt_in[nm][1:]]))
    gather_b0 = _gather_begin(late0, True, "b0")
    gather_1 = _gather_begin(shards1, True, "1", [gather_b0["token"]])
    x1, sv0, lw0 = _layer_fwd(x[0], lw0, cos, sin, gather_1["token"],
                              lambda ya, y_ssd: late_weights(_gather_end(gather_b0, [ya, y_ssd])))
    g1 = _gather_end(gather_1, [x1])
    x2, sv1, lw1 = _layer_fwd(x1, {**early_weights(1, g1[:2]), **late_weights(g1[2:])}, cos, sin)
    dx, dgf, loss = _loss_head(x2, final_norm_g[None], loss_target[0])

    dx, lg1, _, _ = _layer_bwd(dx, lw1, sv1, cos, sin)
    grad_x, lg0, red1, rs0_late = _layer_bwd(dx, lw0, sv0, cos, sin, _rs_begin(large_grads(lg1), True, 1),
                                             lambda *g: _rs_begin(late_grads(*g), True, "0l"))
    rs0 = _rs_begin([lg0["w_in"]], True, 0, [rs0_late["h"]["token"]])
    lg = [lg0, lg1]
    grad = {}

    small_names = ["norm_g", "conv_a_w", "ssd_conv_w", "ssd_conv_b", "sc", "g_ssd", "gq", "gkv"]
    parts = [loss[0, 0:1], dgf]
    for l in range(DEPTH):
        parts += [lg[l][nm][:3, DT_LANE:DT_LANE + SSD_HEADS] if nm == "sc" else lg[l][nm] for nm in small_names]
    shapes = [(1,), (D_MODEL,)] + [(D_MODEL,), (3, D_CONV_A), (4, N_XBC), (N_XBC,), (3, SSD_HEADS), (D_SSD,), (Q_LORA,), (KV_LORA,)] * DEPTH
    red_slab = _allreduce_small(_to_slab(parts, SLAB_ROWS), rs0["h"]["token"])
    rs0 = _rs_add_mine(rs0, [red_slab])
    red = _from_slab(red_slab + rs0["h"]["token"][0, 0], shapes)
    loss_out = red[0][0]
    grad["final_norm_g"] = red[1]
    per = [red[2 + 8 * l:10 + 8 * l] for l in range(DEPTH)]
    grad["norm_g"] = jnp.stack([per[l][0] for l in range(DEPTH)])
    grad["conv_a_w"] = lax.dynamic_slice_in_dim(jnp.stack([per[l][1] for l in range(DEPTH)]), chip * 64, 64, axis=2)
    grad["ssd_conv_w"] = lax.dynamic_slice_in_dim(jnp.stack([per[l][2] for l in range(DEPTH)]), chip * 224, 224, axis=2)
    grad["ssd_conv_b"] = jnp.stack([per[l][3] for l in range(DEPTH)])
    grad["ssd_dt_bias"] = jnp.stack([per[l][4][0] for l in range(DEPTH)])
    grad["ssd_a_log"] = jnp.stack([per[l][4][1] for l in range(DEPTH)])
    grad["ssd_d"] = jnp.stack([per[l][4][2] for l in range(DEPTH)])
    grad["ssd_norm_g"] = jnp.stack([per[l][5] for l in range(DEPTH)])
    grad["mla_q_norm_g"] = jnp.stack([per[l][6] for l in range(DEPTH)])
    grad["mla_kv_norm_g"] = jnp.stack([per[l][7] for l in range(DEPTH)])

    delta, new_m, new_v = {}, {}, {}
    small = [nm for nm in WEIGHTS if nm not in BIG]
    row2 = lambda a: a[None] if a.ndim == 1 else a
    small_out = _adamw(*[[row2(a[nm]) for nm in small] for a in (w, grad, mom, var)], whole=True)
    for nm, (dv, mv, vv) in zip(small, small_out):
        delta[nm], new_m[nm], new_v[nm] = [a.reshape(w[nm].shape) for a in (dv, mv, vv)]

    red0_late = _rs_end(rs0_late, [red_slab])
    red0 = _rs_end(_rs_add_chips(rs0, [a for row in small_out for a in row] + [grad[nm] for nm in small]), []) + red0_late
    r_in, r_out, r_qb, r_kvb = [jnp.stack([a, b]) for a, b in zip(red0, red1)]
    grad.update(w_in=_unperm_cols(r_in), w_out=r_out, w_qb=jnp.swapaxes(r_qb[:, :144], 1, 2), w_kvb=jnp.swapaxes(r_kvb, 1, 2))
    big_out = _adamw([opt_in[nm][0] for nm in BIG], [grad[nm] for nm in BIG], [opt_in[nm][1] for nm in BIG],
                     [opt_in[nm][2] for nm in BIG], whole=False)
    for nm, (dv, mv, vv) in zip(BIG, big_out):
        delta[nm], new_m[nm], new_v[nm] = dv, mv, vv

    return (loss_out, grad_x[None], *[grad[nm] for nm in WEIGHTS], *[delta[nm] for nm in WEIGHTS],
            *[new_m[nm] for nm in WEIGHTS], *[new_v[nm] for nm in WEIGHTS])
```

```python
import functools
import math

import numpy as np
import jax
import jax.numpy as jnp
from jax import lax
from jax.experimental import pallas as pl
from jax.experimental.pallas import tpu as pltpu

F32 = jnp.float32
MXU = jnp.bfloat16

D_MODEL = 1024
DEPTH = 2
D_CONV_A = 256
D_SSD = 384
SSD_HEADS = 6
SSD_BC = 256
SSD_CHUNK = 128
SSD_CHUNKS_PER_STEP = 2
SSD_NORM_EPS = 1e-5
MLA_HEADS = 6
Q_LORA = 256
KV_LORA = 128
QK_NOPE = 64
QK_ROPE = 32
V_DIM = 64
D_MLA = 384
ROPE_BASE = 10000.0
NORM_EPS = 1e-6
IN_COLS = 3110
LANE = 128

O_AH, O_AB, O_AC, O_AZ = 0, 256, 512, 768
O_XBC = 1024
O_SZ = 1920
O_CQA = 2304
O_CKV = 2560
O_CZ = 2688
O_TAIL = 3072
NCOL = 3200
N_XBC = D_SSD + 2 * SSD_BC
DT_LANE = 32
ROPE_LANE = 64

ADAM_LR, ADAM_B1, ADAM_B2, ADAM_EPS, ADAM_WD, ADAM_STEP = 0.001, 0.9, 0.999, 1e-08, 0.01, 10

VMEM_LIMIT = 56 * 1024 * 1024
MESH_T = pl.DeviceIdType.MESH


def _dot(a, b):
    return jnp.dot(a.astype(MXU), b.astype(MXU), preferred_element_type=F32)


def _dot_nt(a, b):
    return lax.dot_general(a.astype(MXU), b.astype(MXU), (((1,), (1,)), ((), ())), preferred_element_type=F32)


def _dot_tn(a, b):
    return lax.dot_general(a.astype(MXU), b.astype(MXU), (((0,), (0,)), ((), ())), preferred_element_type=F32)


def _dot_hi(a, b):
    return jnp.dot(a, b, precision=lax.Precision.HIGHEST, preferred_element_type=F32)


def _dot_hi_tn(a, b):
    return lax.dot_general(a, b, (((0,), (0,)), ((), ())), precision=lax.Precision.HIGHEST, preferred_element_type=F32)


def _sigmoid(z):
    return 1.0 / (1.0 + jnp.exp(-z))


def _silu(z):
    return z * _sigmoid(z)


def _dsilu(z):
    s = _sigmoid(z)
    return s * (1.0 + z * (1.0 - s))


def _softplus(z):
    e = jnp.exp(-jnp.abs(z))
    return jnp.maximum(z, 0.0) + jnp.where(e < 1e-3, e * (1.0 - 0.5 * e), jnp.log(1.0 + e))


def _iota(shape, dim):
    return lax.broadcasted_iota(jnp.int32, shape, dim)


def _shift_down(u, k):
    if k == 0:
        return u
    return jnp.where(_iota(u.shape, 0) >= k, pltpu.roll(u, k, 0), 0.0)


def _shift_up(u, k):
    if k == 0:
        return u
    n = u.shape[0]
    return jnp.where(_iota(u.shape, 0) < n - k, pltpu.roll(u, n - k, 0), 0.0)


def _rope_swap(t):
    lane = _iota(t.shape, 1)
    lo = (lane >= ROPE_LANE) & (lane < ROPE_LANE + 16)
    hi = (lane >= ROPE_LANE + 16) & (lane < ROPE_LANE + 32)
    return jnp.where(lo, pltpu.roll(t, LANE - 16, 1), jnp.where(hi, pltpu.roll(t, 16, 1), 0.0))


def _params(sem=None):
    return pltpu.CompilerParams(dimension_semantics=sem, vmem_limit_bytes=VMEM_LIMIT)


def _full(shape):
    nd = len(shape)
    return pl.BlockSpec(shape, lambda *_: (0,) * nd)


def _sds(shape, dtype=F32):
    return jax.ShapeDtypeStruct(shape, dtype)


def _tile(s):
    return min(256, s)


def _row(ts, w):
    return pl.BlockSpec((ts, w), lambda i: (i, 0))


def _gate_cols(ts, off):
    return pl.BlockSpec((ts, D_SSD), lambda i, _o=off // D_SSD: (i, _o))


def _col(s, off):
    return pl.BlockSpec((s, LANE), lambda j, _o=off // LANE: (0, _o + j))


def _call_after(dep, body, args, *, in_specs, **kw):
    if dep is None:
        return pl.pallas_call(body, in_specs=in_specs, **kw)(*args)
    n = len(args)

    def body_dep(*refs):
        body(*refs[:n], *refs[n + 1:])

    return pl.pallas_call(body_dep, in_specs=list(in_specs) + [pl.BlockSpec(memory_space=pl.ANY)], **kw)(*args, dep)


def _rms(c, g):
    r = lax.rsqrt(jnp.mean(c * c, axis=-1, keepdims=True) + NORM_EPS)
    return c * r * g, r


def _rms_bwd(dn, c, r, g):
    ch = c * r
    dch = dn * g
    dc = r * (dch - ch * jnp.mean(dch * ch, axis=-1, keepdims=True))
    return dc, jnp.sum(dn * ch, axis=0, keepdims=True)


def _inproj_fwd(x, g, w, dep=None):
    s = x.shape[0]
    ts = _tile(s)

    def body(x_ref, g_ref, w_ref, proj_ref, h_ref, r_ref):
        hn, r = _rms(x_ref[...], g_ref[...])
        h = hn.astype(MXU)
        h_ref[...] = h
        r_ref[...] = r
        proj_ref[...] = jnp.dot(h, w_ref[...], preferred_element_type=F32)

    return _call_after(
        dep, body, (x, g, w), name="inproj_fwd", grid=(s // ts,),
        in_specs=[_row(ts, D_MODEL), _full((1, D_MODEL)), _full((D_MODEL, NCOL))],
        out_specs=[_row(ts, NCOL), _row(ts, D_MODEL), _row(ts, 1)],
        out_shape=[_sds((s, NCOL)), _sds((s, D_MODEL), MXU), _sds((s, 1))],
        compiler_params=_params(("parallel",)),
    )


def _conva_fwd(proj, w):
    s = proj.shape[0]

    def body(h_ref, b_ref, c_ref, z_ref, w_ref, y_ref):
        u = c_ref[...] * h_ref[...]
        wv = w_ref[...]
        cv = wv[2:3, :] * u + wv[1:2, :] * _shift_down(u, 1) + wv[0:1, :] * _shift_down(u, 2)
        y_ref[...] = b_ref[...] * cv * _silu(z_ref[...])

    return pl.pallas_call(
        body, name="conva_fwd", grid=(D_CONV_A // LANE,),
        in_specs=[_col(s, O_AH), _col(s, O_AB), _col(s, O_AC), _col(s, O_AZ), pl.BlockSpec((3, LANE), lambda j: (0, j))],
        out_specs=pl.BlockSpec((s, LANE), lambda j: (0, j)),
        out_shape=_sds((s, D_CONV_A)),
        compiler_params=_params(("parallel",)),
    )(proj, proj, proj, proj, w)


def _sconv_pre(u, wv, bv):
    return (wv[3:4, :] * u + wv[2:3, :] * _shift_down(u, 1) + wv[1:2, :] * _shift_down(u, 2)
            + wv[0:1, :] * _shift_down(u, 3) + bv)


def _sconv_fwd(proj, w, b):
    s = proj.shape[0]

    def body(u_ref, w_ref, b_ref, o_ref):
        o_ref[...] = _silu(_sconv_pre(u_ref[...], w_ref[...], b_ref[...]))

    return pl.pallas_call(
        body, name="sconv_fwd", grid=(N_XBC // LANE,),
        in_specs=[_col(s, O_XBC), pl.BlockSpec((4, LANE), lambda j: (0, j)), pl.BlockSpec((1, LANE), lambda j: (0, j))],
        out_specs=pl.BlockSpec((s, LANE), lambda j: (0, j)),
        out_shape=_sds((s, N_XBC)),
        compiler_params=_params(("parallel",)),
    )(proj, w, b)


def _ssd_chunk_common(tail, sc):
    l = SSD_CHUNK
    lane = _iota((l, LANE), 1)
    row = _iota((l, LANE), 0)
    tri = (row >= lane).astype(F32)
    a_row = -jnp.exp(sc[1:2, :])
    pre = tail + sc[0:1, :]
    dt = _softplus(pre)
    a_cs = _dot_hi(tri, dt * a_row)
    return lane, row, tri, a_row, pre, dt, a_cs, a_cs.T


def _pick_col(m, lane, k):
    return jnp.sum(jnp.where(lane == k, m, 0.0), axis=1, keepdims=True)


def _pick_row(m, row, k):
    return jnp.sum(jnp.where(row == k, m, 0.0), axis=0, keepdims=True)


def _ssd_fwd(xbc, proj, sc):
    s = xbc.shape[0]
    nc = s // SSD_CHUNK
    l = SSD_CHUNK
    cps = SSD_CHUNKS_PER_STEP

    def body(xbc_ref, tail_ref, sc_ref, y_ref, st_ref, state):
        @pl.when(pl.program_id(0) == 0)
        def _():
            state[...] = jnp.zeros_like(state)

        sc_v = sc_ref[...]
        lane1 = _iota((1, LANE), 1)
        rowp = _iota((LANE, 1), 0)
        d_row = sc_v[2:3, :]
        states = [state[j] for j in range(3)]
        for u in range(cps):
            r = slice(u * l, (u + 1) * l)
            lane, row, _, _, _, dt, a_cs, a_t = _ssd_chunk_common(tail_ref[r, :], sc_v)
            for j in range(3):
                st_ref[u, j] = states[j]
            for j in range(3):
                xpair = xbc_ref[r, LANE * j:LANE * (j + 1)]
                sp = states[j]
                ypair = jnp.zeros((l, LANE), F32)
                new_s = jnp.zeros((LANE, LANE), F32)
                decay = jnp.zeros((LANE, 1), F32)
                for half in range(2):
                    h = 2 * j + half
                    g = h // 3
                    hm = (lane < 64) if half == 0 else (lane >= 64)
                    hrow = (rowp < 64) if half == 0 else (rowp >= 64)
                    ac = _pick_col(a_cs, lane, DT_LANE + h)
                    ar = _pick_row(a_t, row, DT_LANE + h)
                    dtc = _pick_col(dt, lane, DT_LANE + h)
                    alast = jnp.sum(jnp.where(lane1 == l - 1, ar, 0.0), axis=1, keepdims=True)
                    dh = jnp.sum(jnp.where(lane1 == DT_LANE + h, d_row, 0.0), axis=1, keepdims=True)
                    xm = jnp.where(hm, xpair, 0.0)
                    xd = xm * dtc
                    bm = xbc_ref[r, D_SSD + LANE * g:D_SSD + LANE * (g + 1)]
                    cm = xbc_ref[r, D_SSD + SSD_BC + LANE * g:D_SSD + SSD_BC + LANE * (g + 1)]
                    lm = jnp.where(row >= lane, jnp.exp(jnp.minimum(ac - ar, 0.0)), 0.0)
                    y_diag = _dot(_dot_nt(cm, bm) * lm, xd)
                    y_off = jnp.where(hm, _dot_nt(cm, sp), 0.0) * jnp.exp(ac)
                    ypair = ypair + y_diag + y_off + xm * dh
                    new_s = new_s + _dot_tn(xd * jnp.exp(alast - ac), bm)
                    decay = jnp.where(hrow, jnp.exp(alast), decay)
                states[j] = sp * decay + new_s
                y_ref[r, LANE * j:LANE * (j + 1)] = ypair
        for j in range(3):
            state[j] = states[j]

    return pl.pallas_call(
        body, name="ssd_fwd", grid=(nc // cps,),
        in_specs=[pl.BlockSpec((cps * l, N_XBC), lambda c: (c, 0)),
                  pl.BlockSpec((cps * l, LANE), lambda c: (c, O_TAIL // LANE)), _full((8, LANE))],
        out_specs=[pl.BlockSpec((cps * l, D_SSD), lambda c: (c, 0)), pl.BlockSpec((cps, 3, LANE, LANE), lambda c: (c, 0, 0, 0))],
        out_shape=[_sds((s, D_SSD)), _sds((nc, 3, LANE, LANE))],
        scratch_shapes=[pltpu.VMEM((3, LANE, LANE), F32)],
        compiler_params=_params(("arbitrary",)),
    )(xbc, proj, sc)


def _mla_prep_fwd(proj, gq, gkv, wq, wkv, cos, sin):
    s = proj.shape[0]
    ts = _tile(s)
    nh = MLA_HEADS

    def body(cqa_ref, ckv_ref, tail_ref, gq_ref, gkv_ref, wq_ref, wkv_ref, cos_ref, sin_ref,
             q_ref, k_ref, v_ref, qn_ref, kvn_ref, rq_ref, rkv_ref):
        qn, rq = _rms(cqa_ref[...], gq_ref[...])
        kvn, rkv = _rms(ckv_ref[...], gkv_ref[...])
        qn = qn.astype(MXU)
        kvn = kvn.astype(MXU)
        qn_ref[...] = qn
        kvn_ref[...] = kvn
        rq_ref[...] = rq
        rkv_ref[...] = rkv
        q = _dot_nt(qn, wq_ref[...])
        kv = _dot_nt(kvn, wkv_ref[...])
        cosv = cos_ref[...]
        sinv = sin_ref[...]
        lane = _iota((ts, LANE), 1)
        rope_lanes = (lane >= ROPE_LANE) & (lane < ROPE_LANE + QK_ROPE)
        kr = jnp.where(rope_lanes, pltpu.roll(tail_ref[...], ROPE_LANE, 1), 0.0)
        kr = kr * cosv + _rope_swap(kr) * sinv
        for h in range(nh):
            qh = q[:, LANE * h:LANE * (h + 1)]
            q_ref[h] = ((qh * cosv + _rope_swap(qh) * sinv) * ATT_SCALE).astype(MXU)
            k_ref[h] = (kv[:, LANE * h:LANE * (h + 1)] + kr).astype(MXU)
            v_ref[h] = kv[:, LANE * (nh + h):LANE * (nh + h + 1)].astype(MXU)

    head = pl.BlockSpec((nh, ts, LANE), lambda i: (0, i, 0))
    return pl.pallas_call(
        body, name="mla_prep_fwd", grid=(s // ts,),
        in_specs=[pl.BlockSpec((ts, Q_LORA), lambda i: (i, O_CQA // Q_LORA)),
                  pl.BlockSpec((ts, KV_LORA), lambda i: (i, O_CKV // KV_LORA)),
                  pl.BlockSpec((ts, LANE), lambda i: (i, O_TAIL // LANE)),
                  _full((1, Q_LORA)), _full((1, KV_LORA)), _full((nh * LANE, Q_LORA)), _full((2 * nh * LANE, KV_LORA)),
                  _row(ts, LANE), _row(ts, LANE)],
        out_specs=[head, head, head, _row(ts, Q_LORA), _row(ts, KV_LORA), _row(ts, 1), _row(ts, 1)],
        out_shape=[_sds((nh, s, LANE), MXU)] * 3 + [_sds((s, Q_LORA), MXU), _sds((s, KV_LORA), MXU), _sds((s, 1)), _sds((s, 1))],
        compiler_params=_params(("parallel",)),
    )(proj, proj, proj, gq, gkv, wq, wkv, cos, sin)


ATT_SCALE = (QK_NOPE + QK_ROPE) ** -0.5
NEG = -1e30


def _att_tile(s, most):
    return min(most, s // 2)


ATT_FWD_TILE = 1024
ATT_BWD_TILE = 512


def _attn_fwd(q, k, v):
    nh, s, _ = q.shape
    tq = _att_tile(s, ATT_FWD_TILE)
    nq = s // tq

    def body(q_ref, k_ref, v_ref, o_ref, lse_ref):
        i = pl.program_id(1)
        rowi = _iota((tq, tq), 0)
        coli = _iota((tq, tq), 1)
        zero = (jnp.full((tq, 1), NEG, F32), jnp.zeros((tq, 1), F32), jnp.zeros((tq, LANE), F32))
        state = [zero, zero]
        done = [zero, zero]
        for t in range(nq + 1):
            first = t <= i
            qblk = jnp.where(first, i, nq - 1 - i)
            kblk = jnp.where(first, t, t - i - 1)
            qoff = pl.multiple_of(qblk * tq, tq)
            koff = pl.multiple_of(kblk * tq, tq)
            keep = coli <= rowi + jnp.where(kblk == qblk, 0, tq)
            restart = t == i + 1
            for hh in range(2):
                m, lsum, acc = state[hh]
                if t > 0:
                    done[hh] = tuple(jnp.where(restart, a, b) for a, b in zip(state[hh], done[hh]))
                    m = jnp.where(restart, NEG, m)
                    lsum = jnp.where(restart, 0.0, lsum)
                    acc = jnp.where(restart, 0.0, acc)
                sc = _dot_nt(q_ref[hh, pl.ds(qoff, tq), :], k_ref[hh, pl.ds(koff, tq), :])
                sc = jnp.where(keep, sc, NEG)
                m_new = jnp.maximum(m, jnp.max(sc, axis=1, keepdims=True))
                p = jnp.exp(sc - m_new)
                alpha = jnp.exp(m - m_new)
                lsum = alpha * lsum + jnp.sum(p, axis=1, keepdims=True)
                acc = alpha * acc + _dot(p, v_ref[hh, pl.ds(koff, tq), :])
                state[hh] = (m_new, lsum, acc)
        for blk, res in ((i, done), (nq - 1 - i, state)):
            off = pl.multiple_of(blk * tq, tq)
            out = None
            for hh in range(2):
                m, lsum, acc = res[hh]
                o = acc * (1.0 / lsum)
                lse_ref[hh, pl.ds(off, tq), :] = m + jnp.log(lsum)
                out = o if hh == 0 else out + pltpu.roll(o, V_DIM, 1)
            o_ref[pl.ds(off, tq), :] = out

    pair = pl.BlockSpec((2, s, LANE), lambda j, i: (j, 0, 0))
    return pl.pallas_call(
        body, name="attn_fwd", grid=(nh // 2, nq // 2),
        in_specs=[pair, pair, pair],
        out_specs=[pl.BlockSpec((s, LANE), lambda j, i: (0, j)), pl.BlockSpec((2, s, 1), lambda j, i: (j, 0, 0))],
        out_shape=[_sds((s, D_MLA)), _sds((nh, s, 1))],
        compiler_params=_params(("parallel", "arbitrary")),
    )(q, k, v)


def _ssd_gate(y_ssd, s_z, g):
    yz = y_ssd * _silu(s_z)
    g0 = _iota(yz.shape, 1) < D_SSD // 2
    sq = yz * yz
    ms0 = jnp.sum(jnp.where(g0, sq, 0.0), axis=1, keepdims=True) / (D_SSD // 2)
    ms1 = jnp.sum(jnp.where(g0, 0.0, sq), axis=1, keepdims=True) / (D_SSD // 2)
    r = jnp.where(g0, lax.rsqrt(ms0 + SSD_NORM_EPS), lax.rsqrt(ms1 + SSD_NORM_EPS))
    nrm = yz * r
    return nrm * g, nrm, r, g0


def _outproj_fwd(x, proj, ya, y_ssd, o, g_ssd, w):
    s = x.shape[0]
    ts = _tile(s)

    def body(x_ref, sz_ref, cz_ref, ya_ref, ys_ref, o_ref, g_ref, w_ref, xo_ref, y_ref):
        yb = _ssd_gate(ys_ref[...], sz_ref[...], g_ref[...])[0]
        yc = o_ref[...] * _silu(cz_ref[...])
        y = jnp.concatenate([ya_ref[...], yb, yc], axis=1).astype(MXU)
        y_ref[...] = y
        xo_ref[...] = x_ref[...] + jnp.dot(y, w_ref[...], preferred_element_type=F32)

    return pl.pallas_call(
        body, name="outproj_fwd", grid=(s // ts,),
        in_specs=[_row(ts, D_MODEL), _gate_cols(ts, O_SZ), _gate_cols(ts, O_CZ), _row(ts, D_CONV_A), _row(ts, D_SSD),
                  _row(ts, D_MLA), _full((1, D_SSD)), _full((D_MODEL, D_MODEL))],
        out_specs=[_row(ts, D_MODEL), _row(ts, D_MODEL)],
        out_shape=[_sds((s, D_MODEL)), _sds((s, D_MODEL), MXU)],
        compiler_params=_params(("parallel",)),
    )(x, proj, proj, ya, y_ssd, o, g_ssd, w)


def _loss_head(x, g, tgt):
    s = x.shape[0]
    ts = _tile(s)

    def body(x_ref, g_ref, t_ref, dx_ref, dg_ref, loss_ref):
        @pl.when(pl.program_id(0) == 0)
        def _():
            dg_ref[...] = jnp.zeros_like(dg_ref)
            loss_ref[...] = jnp.zeros_like(loss_ref)

        xv = x_ref[...]
        gv = g_ref[...]
        yn, r = _rms(xv, gv)
        e = yn - t_ref[...]
        loss_ref[...] += jnp.sum(jnp.sum(e * e, axis=1, keepdims=True), axis=0, keepdims=True) * (0.5 / D_MODEL)
        dx, dg = _rms_bwd(e * (1.0 / D_MODEL), xv, r, gv)
        dx_ref[...] = dx
        dg_ref[...] += dg

    return pl.pallas_call(
        body, name="loss_head", grid=(s // ts,),
        in_specs=[_row(ts, D_MODEL), _full((1, D_MODEL)), _row(ts, D_MODEL)],
        out_specs=[_row(ts, D_MODEL), _full((1, D_MODEL)), _full((1, LANE))],
        out_shape=[_sds((s, D_MODEL)), _sds((1, D_MODEL)), _sds((1, LANE))],
        compiler_params=_params(("arbitrary",)),
    )(x, g, tgt)


def _outproj_bwd(dout, y, w, proj, y_ssd, o, g_ssd, dep=None):
    s = dout.shape[0]
    ts = _tile(s)

    def body(dout_ref, y_ref, w_ref, sz_ref, cz_ref, ys_ref, o_ref, g_ref,
             dya_ref, dys_ref, dsz_ref, dattn_ref, dcz_ref, dg_ref, dw_ref):
        @pl.when(pl.program_id(0) == 0)
        def _():
            dw_ref[...] = jnp.zeros_like(dw_ref)
            dg_ref[...] = jnp.zeros_like(dg_ref)

        dout_b = dout_ref[...].astype(MXU)
        dw_ref[...] += _dot_tn(y_ref[...], dout_b)
        dy = _dot_nt(dout_b, w_ref[...])
        dya_ref[...] = dy[:, :D_CONV_A]
        dyb = dy[:, D_CONV_A:D_CONV_A + D_SSD]
        sz = sz_ref[...]
        ys = ys_ref[...]
        gv = g_ref[...]
        _, nrm, r, g0 = _ssd_gate(ys, sz, gv)
        dg_ref[...] += jnp.sum(dyb * nrm, axis=0, keepdims=True)
        dn = dyb * gv
        t = dn * nrm
        mean = jnp.where(g0, jnp.sum(jnp.where(g0, t, 0.0), axis=1, keepdims=True),
                         jnp.sum(jnp.where(g0, 0.0, t), axis=1, keepdims=True)) / (D_SSD // 2)
        dyz = r * (dn - nrm * mean)
        dys_ref[...] = dyz * _silu(sz)
        dsz_ref[...] = (dyz * ys * _dsilu(sz)).astype(MXU)
        dyc = dy[:, D_CONV_A + D_SSD:]
        cz = cz_ref[...]
        dattn_ref[...] = dyc * _silu(cz)
        dcz_ref[...] = (dyc * o_ref[...] * _dsilu(cz)).astype(MXU)

    return _call_after(
        dep, body, (dout, y, w, proj, proj, y_ssd, o, g_ssd), name="outproj_bwd", grid=(s // ts,),
        in_specs=[_row(ts, D_MODEL), _row(ts, D_MODEL), _full((D_MODEL, D_MODEL)), _gate_cols(ts, O_SZ), _gate_cols(ts, O_CZ),
                  _row(ts, D_SSD), _row(ts, D_MLA), _full((1, D_SSD))],
        out_specs=[_row(ts, D_CONV_A), _row(ts, D_SSD), _row(ts, D_SSD), _row(ts, D_MLA), _row(ts, D_MLA),
                   _full((1, D_SSD)), _full((D_MODEL, D_MODEL))],
        out_shape=[_sds((s, D_CONV_A)), _sds((s, D_SSD)), _sds((s, D_SSD), MXU), _sds((s, D_MLA)), _sds((s, D_MLA), MXU),
                   _sds((1, D_SSD)), _sds((D_MODEL, D_MODEL))],
        compiler_params=_params(("arbitrary",)),
    )


def _attn_bwd(q, k, v, o, d_o, lse, dep=None):
    nh, s, _ = q.shape
    tq = _att_tile(s, ATT_BWD_TILE)
    nq = s // tq

    def body(q_ref, k_ref, v_ref, o_ref, do_ref, lse_ref, dq_ref, dk_ref, dv_ref, dop, delta):
        i = pl.program_id(1)

        @pl.when(i == 0)
        def _():
            lane = _iota((s, LANE), 1)
            for hh in range(2):
                dov = do_ref[...]
                ov = o_ref[...]
                if hh == 1:
                    dov = pltpu.roll(dov, V_DIM, 1)
                    ov = pltpu.roll(ov, V_DIM, 1)
                dov = jnp.where(lane < V_DIM, dov, 0.0)
                dop[hh] = dov.astype(MXU)
                delta[hh] = jnp.sum(dov * ov, axis=1, keepdims=True)
                dq_ref[hh] = jnp.zeros((s, LANE), F32)

        rowi = _iota((tq, tq), 0)
        coli = _iota((tq, tq), 1)
        z = jnp.zeros((tq, LANE), F32)
        state = [(z, z), (z, z)]
        done = [(z, z), (z, z)]
        for t in range(nq + 1):
            first = t <= nq - 1 - i
            kblk = jnp.where(first, i, nq - 1 - i)
            qblk = jnp.where(first, i + t, t - 1)
            qoff = pl.multiple_of(qblk * tq, tq)
            koff = pl.multiple_of(kblk * tq, tq)
            keep = coli <= rowi + jnp.where(kblk == qblk, 0, tq)
            restart = t == nq - i
            for hh in range(2):
                dk, dv = state[hh]
                if t > 0:
                    done[hh] = tuple(jnp.where(restart, a, b) for a, b in zip(state[hh], done[hh]))
                    dk = jnp.where(restart, 0.0, dk)
                    dv = jnp.where(restart, 0.0, dv)
                kb = k_ref[hh, pl.ds(koff, tq), :]
                qb = q_ref[hh, pl.ds(qoff, tq), :]
                dob = dop[hh, pl.ds(qoff, tq), :]
                sc = jnp.where(keep, _dot_nt(qb, kb), NEG)
                p = jnp.exp(sc - lse_ref[hh, pl.ds(qoff, tq), :])
                dp = _dot_nt(dob, v_ref[hh, pl.ds(koff, tq), :])
                ds = p * (dp - delta[hh, pl.ds(qoff, tq), :])
                dq_ref[hh, pl.ds(qoff, tq), :] += _dot(ds, kb)
                state[hh] = (dk + _dot_tn(ds, qb), dv + _dot_tn(p, dob))
        for blk, res in ((i, done), (nq - 1 - i, state)):
            off = pl.multiple_of(blk * tq, tq)
            for hh in range(2):
                dk_ref[hh, pl.ds(off, tq), :] = res[hh][0]
                dv_ref[hh, pl.ds(off, tq), :] = res[hh][1]

    pair = pl.BlockSpec((2, s, LANE), lambda j, i: (j, 0, 0))
    return _call_after(
        dep, body, (q, k, v, o, d_o, lse), name="attn_bwd", grid=(nh // 2, nq // 2),
        in_specs=[pair, pair, pair, pl.BlockSpec((s, LANE), lambda j, i: (0, j)), pl.BlockSpec((s, LANE), lambda j, i: (0, j)),
                  pl.BlockSpec((2, s, 1), lambda j, i: (j, 0, 0))],
        out_specs=[pair, pair, pair],
        out_shape=[_sds((nh, s, LANE))] * 3,
        scratch_shapes=[pltpu.VMEM((2, s, LANE), MXU), pltpu.VMEM((2, s, 1), F32)],
        compiler_params=_params(("parallel", "arbitrary")),
    )


def _ssd_bwd(xbc, proj, sc, states, dy, dep=None):
    s = xbc.shape[0]
    nc = s // SSD_CHUNK
    l = SSD_CHUNK
    cps = SSD_CHUNKS_PER_STEP

    def body(xbc_ref, tail_ref, sc_ref, st_ref, dy_ref, dxbc_ref, dtail_ref, dsc_ref, dstate):
        @pl.when(pl.program_id(0) == 0)
        def _():
            dstate[...] = jnp.zeros_like(dstate)
            dsc_ref[...] = jnp.zeros_like(dsc_ref)

        sc_v = sc_ref[...]
        lane1 = _iota((1, LANE), 1)
        rowp = _iota((LANE, 1), 0)
        rowl = _iota((l, 1), 0)
        d_row = sc_v[2:3, :]
        dstates = [dstate[j] for j in range(3)]
        for u in reversed(range(cps)):
            dstates = chunk(u, xbc_ref, tail_ref, sc_v, st_ref, dy_ref, dxbc_ref, dtail_ref, dsc_ref, dstates,
                            lane1, rowp, rowl, d_row)
        for j in range(3):
            dstate[j] = dstates[j]

    def chunk(u, xbc_ref, tail_ref, sc_v, st_ref, dy_ref, dxbc_ref, dtail_ref, dsc_ref, dstates, lane1, rowp, rowl, d_row):
        r = slice(u * l, (u + 1) * l)
        dstates = list(dstates)
        lane, row, tri, a_row, pre, dt, a_cs, a_t = _ssd_chunk_common(tail_ref[r, :], sc_v)
        da_col = jnp.zeros((l, LANE), F32)
        da_row = jnp.zeros((LANE, l), F32)
        dt_x = jnp.zeros((l, LANE), F32)
        dd_row = jnp.zeros((1, LANE), F32)
        db = [jnp.zeros((l, LANE), F32), jnp.zeros((l, LANE), F32)]
        dc = [jnp.zeros((l, LANE), F32), jnp.zeros((l, LANE), F32)]
        for j in range(3):
            xpair = xbc_ref[r, LANE * j:LANE * (j + 1)]
            dypair = dy_ref[r, LANE * j:LANE * (j + 1)]
            sp = st_ref[u, j]
            dsp = dstates[j]
            dxpair = jnp.zeros((l, LANE), F32)
            ds_new = jnp.zeros((LANE, LANE), F32)
            decay = jnp.zeros((LANE, 1), F32)
            for half in range(2):
                h = 2 * j + half
                g = h // 3
                hm = (lane < 64) if half == 0 else (lane >= 64)
                hrow = (rowp < 64) if half == 0 else (rowp >= 64)
                ac = _pick_col(a_cs, lane, DT_LANE + h)
                ar = _pick_row(a_t, row, DT_LANE + h)
                dtc = _pick_col(dt, lane, DT_LANE + h)
                alast = jnp.sum(jnp.where(lane1 == l - 1, ar, 0.0), axis=1, keepdims=True)
                dh = jnp.sum(jnp.where(lane1 == DT_LANE + h, d_row, 0.0), axis=1, keepdims=True)
                xm = jnp.where(hm, xpair, 0.0)
                xd = xm * dtc
                dym = jnp.where(hm, dypair, 0.0)
                bm = xbc_ref[r, D_SSD + LANE * g:D_SSD + LANE * (g + 1)]
                cm = xbc_ref[r, D_SSD + SSD_BC + LANE * g:D_SSD + SSD_BC + LANE * (g + 1)]
                lm = jnp.where(row >= lane, jnp.exp(jnp.minimum(ac - ar, 0.0)), 0.0)
                e_in = jnp.exp(ac)
                f_out = jnp.exp(alast - ac)
                e_last = jnp.exp(alast)
                m = _dot_nt(cm, bm) * lm
                y_off = jnp.where(hm, _dot_nt(cm, sp), 0.0) * e_in
                dm = _dot_nt(dym, xd)
                dxd = _dot_tn(m, dym)
                dg = dm * lm
                dye = dym * e_in
                dc[g] = dc[g] + _dot(dg, bm) + _dot(dye, sp)
                db[g] = db[g] + _dot_tn(dg, cm)
                qm = dm * m
                dac = jnp.sum(qm, axis=1, keepdims=True) + jnp.sum(dym * y_off, axis=1, keepdims=True)
                dar = -jnp.sum(qm, axis=0, keepdims=True)
                dxf = jnp.where(hm, _dot_nt(bm, dsp), 0.0)
                db[g] = db[g] + _dot(xd * f_out, dsp)
                dxd = dxd + dxf * f_out
                df = jnp.sum(dxf * xd, axis=1, keepdims=True) * f_out
                dac = dac - df
                s_last = jnp.sum(df, axis=0, keepdims=True)
                ss = jnp.sum(jnp.where(hrow, dsp * sp, 0.0), axis=1, keepdims=True)
                s_last = s_last + e_last * jnp.sum(ss, axis=0, keepdims=True)
                dac = dac + jnp.where(rowl == l - 1, s_last, 0.0)
                ds_new = ds_new + _dot_tn(dye, cm)
                decay = jnp.where(hrow, e_last, decay)
                dxpair = dxpair + dxd * dtc + dym * dh
                dt_x = dt_x + jnp.where(lane == DT_LANE + h, jnp.sum(dxd * xm, axis=1, keepdims=True), 0.0)
                dsum = jnp.sum(jnp.sum(dym * xm, axis=1, keepdims=True), axis=0, keepdims=True)
                dd_row = dd_row + jnp.where(lane1 == DT_LANE + h, dsum, 0.0)
                da_col = da_col + jnp.where(lane == DT_LANE + h, dac, 0.0)
                da_row = da_row + jnp.where(row == DT_LANE + h, dar, 0.0)
            dstates[j] = dsp * decay + ds_new
            dxbc_ref[r, LANE * j:LANE * (j + 1)] = dxpair
        for g in range(2):
            dxbc_ref[r, D_SSD + LANE * g:D_SSD + LANE * (g + 1)] = db[g]
            dxbc_ref[r, D_SSD + SSD_BC + LANE * g:D_SSD + SSD_BC + LANE * (g + 1)] = dc[g]
        dla = _dot_hi_tn(tri, da_col + da_row.T)
        ddt = dt_x + dla * a_row
        dpre = ddt * _sigmoid(pre)
        dtm = (lane >= DT_LANE) & (lane < DT_LANE + SSD_HEADS)
        dtail_ref[r, :] = jnp.where(dtm, dpre, 0.0).astype(MXU)
        dtm1 = (lane1 >= DT_LANE) & (lane1 < DT_LANE + SSD_HEADS)
        dsc_ref[0:1, :] += jnp.where(dtm1, jnp.sum(dpre, axis=0, keepdims=True), 0.0)
        dsc_ref[1:2, :] += jnp.where(dtm1, jnp.sum(dla * dt, axis=0, keepdims=True) * a_row, 0.0)
        dsc_ref[2:3, :] += dd_row
        return dstates

    rev = lambda c: nc // cps - 1 - c
    return _call_after(
        dep, body, (xbc, proj, sc, states, dy), name="ssd_bwd", grid=(nc // cps,),
        in_specs=[pl.BlockSpec((cps * l, N_XBC), lambda c: (rev(c), 0)),
                  pl.BlockSpec((cps * l, LANE), lambda c: (rev(c), O_TAIL // LANE)), _full((8, LANE)),
                  pl.BlockSpec((cps, 3, LANE, LANE), lambda c: (rev(c), 0, 0, 0)),
                  pl.BlockSpec((cps * l, D_SSD), lambda c: (rev(c), 0))],
        out_specs=[pl.BlockSpec((cps * l, N_XBC), lambda c: (rev(c), 0)), pl.BlockSpec((cps * l, LANE), lambda c: (rev(c), 0)),
                   _full((8, LANE))],
        out_shape=[_sds((s, N_XBC)), _sds((s, LANE), MXU), _sds((8, LANE))],
        scratch_shapes=[pltpu.VMEM((3, LANE, LANE), F32)],
        compiler_params=_params(("arbitrary",)),
    )


def _sconv_bwd(proj, w, b, dxbc, dep=None):
    s = proj.shape[0]

    def body(u_ref, w_ref, b_ref, d_ref, du_ref, dw_ref, db_ref):
        u = u_ref[...]
        wv = w_ref[...]
        dpre = d_ref[...] * _dsilu(_sconv_pre(u, wv, b_ref[...]))
        du_ref[...] = (wv[3:4, :] * dpre + wv[2:3, :] * _shift_up(dpre, 1) + wv[1:2, :] * _shift_up(dpre, 2)
                       + wv[0:1, :] * _shift_up(dpre, 3)).astype(MXU)
        for k in range(4):
            dw_ref[k:k + 1, :] = jnp.sum(dpre * _shift_down(u, 3 - k), axis=0, keepdims=True)
        db_ref[...] = jnp.sum(dpre, axis=0, keepdims=True)

    blk = pl.BlockSpec((s, LANE), lambda j: (0, j))
    return _call_after(
        dep, body, (proj, w, b, dxbc), name="sconv_bwd", grid=(N_XBC // LANE,),
        in_specs=[_col(s, O_XBC), pl.BlockSpec((4, LANE), lambda j: (0, j)), pl.BlockSpec((1, LANE), lambda j: (0, j)), blk],
        out_specs=[blk, pl.BlockSpec((4, LANE), lambda j: (0, j)), pl.BlockSpec((1, LANE), lambda j: (0, j))],
        out_shape=[_sds((s, N_XBC), MXU), _sds((4, N_XBC)), _sds((1, N_XBC))],
        compiler_params=_params(("parallel",)),
    )


def _conva_bwd(proj, w, dya, dep=None):
    s = proj.shape[0]

    def body(h_ref, b_ref, c_ref, z_ref, w_ref, d_ref, da_ref, dw_ref):
        ah, ab, acv, az = h_ref[...], b_ref[...], c_ref[...], z_ref[...]
        wv = w_ref[...]
        u = acv * ah
        cv = wv[2:3, :] * u + wv[1:2, :] * _shift_down(u, 1) + wv[0:1, :] * _shift_down(u, 2)
        dy = d_ref[...]
        sz = _silu(az)
        da_ref[1] = (dy * cv * sz).astype(MXU)
        da_ref[3] = (dy * ab * cv * _dsilu(az)).astype(MXU)
        dcv = dy * ab * sz
        du = wv[2:3, :] * dcv + wv[1:2, :] * _shift_up(dcv, 1) + wv[0:1, :] * _shift_up(dcv, 2)
        da_ref[0] = (du * acv).astype(MXU)
        da_ref[2] = (du * ah).astype(MXU)
        for k in range(3):
            dw_ref[k:k + 1, :] = jnp.sum(dcv * _shift_down(u, 2 - k), axis=0, keepdims=True)

    return _call_after(
        dep, body, (proj, proj, proj, proj, w, dya), name="conva_bwd", grid=(D_CONV_A // LANE,),
        in_specs=[_col(s, O_AH), _col(s, O_AB), _col(s, O_AC), _col(s, O_AZ), pl.BlockSpec((3, LANE), lambda j: (0, j)),
                  pl.BlockSpec((s, LANE), lambda j: (0, j))],
        out_specs=[pl.BlockSpec((4, s, LANE), lambda j: (0, 0, j)), pl.BlockSpec((3, LANE), lambda j: (0, j))],
        out_shape=[_sds((4, s, D_CONV_A), MXU), _sds((3, D_CONV_A))],
        compiler_params=_params(("parallel",)),
    )


def _mla_prep_bwd(dq, dk, dv, proj, qn, kvn, rq, rkv, gq, gkv, wq, wkv, cos, sin):
    s = proj.shape[0]
    ts = _tile(s)
    nh = MLA_HEADS

    def body(dq_ref, dk_ref, dv_ref, cqa_ref, ckv_ref, qn_ref, kvn_ref, rq_ref, rkv_ref, gq_ref, gkv_ref,
             wq_ref, wkv_ref, cos_ref, sin_ref, dcqa_ref, dckv_ref, dtail_ref, dwq_ref, dwkv_ref, dgq_ref, dgkv_ref):
        @pl.when(pl.program_id(0) == 0)
        def _():
            dwq_ref[...] = jnp.zeros_like(dwq_ref)
            dwkv_ref[...] = jnp.zeros_like(dwkv_ref)
            dgq_ref[...] = jnp.zeros_like(dgq_ref)
            dgkv_ref[...] = jnp.zeros_like(dgkv_ref)

        cosv = cos_ref[...]
        sinv = sin_ref[...]
        lane = _iota((ts, LANE), 1)
        rope_lanes = (lane >= ROPE_LANE) & (lane < ROPE_LANE + QK_ROPE)

        def unrope(gr):
            return gr * cosv + _rope_swap(gr * sinv)

        dqs, dks, dvs = [], [], []
        dkr = jnp.zeros((ts, LANE), F32)
        for h in range(nh):
            dqs.append(unrope(dq_ref[h] * ATT_SCALE).astype(MXU))
            dkh = dk_ref[h]
            dks.append(jnp.where(lane < QK_NOPE, dkh, 0.0).astype(MXU))
            dkr = dkr + jnp.where(rope_lanes, dkh, 0.0)
            dvs.append(dv_ref[h].astype(MXU))
        dtail_ref[...] = pltpu.roll(jnp.where(rope_lanes, unrope(dkr), 0.0), ROPE_LANE, 1).astype(MXU)
        dq_all = jnp.concatenate(dqs, axis=1)
        dkv_all = jnp.concatenate(dks + dvs, axis=1)
        dwq_ref[...] += _dot_tn(dq_all, qn_ref[...])
        dwkv_ref[...] += _dot_tn(dkv_all, kvn_ref[...])
        dcqa, dgq = _rms_bwd(_dot(dq_all, wq_ref[...]), cqa_ref[...], rq_ref[...], gq_ref[...])
        dckv, dgkv = _rms_bwd(_dot(dkv_all, wkv_ref[...]), ckv_ref[...], rkv_ref[...], gkv_ref[...])
        dcqa_ref[...] = dcqa.astype(MXU)
        dckv_ref[...] = dckv.astype(MXU)
        dgq_ref[...] += dgq
        dgkv_ref[...] += dgkv

    head = pl.BlockSpec((nh, ts, LANE), lambda i: (0, i, 0))
    return pl.pallas_call(
        body, name="mla_prep_bwd", grid=(s // ts,),
        in_specs=[head, head, head,
                  pl.BlockSpec((ts, Q_LORA), lambda i: (i, O_CQA // Q_LORA)),
                  pl.BlockSpec((ts, KV_LORA), lambda i: (i, O_CKV // KV_LORA)),
                  _row(ts, Q_LORA), _row(ts, KV_LORA), _row(ts, 1), _row(ts, 1),
                  _full((1, Q_LORA)), _full((1, KV_LORA)), _full((nh * LANE, Q_LORA)), _full((2 * nh * LANE, KV_LORA)),
                  _row(ts, LANE), _row(ts, LANE)],
        out_specs=[_row(ts, Q_LORA), _row(ts, KV_LORA), _row(ts, LANE), _full((nh * LANE, Q_LORA)),
                   _full((2 * nh * LANE, KV_LORA)), _full((1, Q_LORA)), _full((1, KV_LORA))],
        out_shape=[_sds((s, Q_LORA), MXU), _sds((s, KV_LORA), MXU), _sds((s, LANE), MXU), _sds((nh * LANE, Q_LORA)),
                   _sds((2 * nh * LANE, KV_LORA)), _sds((1, Q_LORA)), _sds((1, KV_LORA))],
        compiler_params=_params(("arbitrary",)),
    )(dq, dk, dv, proj, proj, qn, kvn, rq, rkv, gq, gkv, wq, wkv, cos, sin)


def _inproj_bwd(da4, dsz, dxbc_in, dcqa, dckv, dcz, dtail_a, dtail_b, w, x, rstd, g, dout, dep=None):
    s = x.shape[0]
    ts = _tile(s)

    def body(da_ref, dsz_ref, dxbc_ref, dcqa_ref, dckv_ref, dcz_ref, dta_ref, dtb_ref, w_ref, x_ref, r_ref, g_ref, dout_ref,
             dproj_ref, dx_ref, dg_ref):
        @pl.when(pl.program_id(0) == 0)
        def _():
            dg_ref[...] = jnp.zeros_like(dg_ref)

        dproj = jnp.concatenate(
            [da_ref[0], da_ref[1], da_ref[2], da_ref[3], dxbc_ref[...], dsz_ref[...], dcqa_ref[...], dckv_ref[...],
             dcz_ref[...], dta_ref[...] + dtb_ref[...]], axis=1)
        dproj_ref[...] = dproj
        dh = _dot_nt(dproj, w_ref[...])
        dx, dg = _rms_bwd(dh, x_ref[...], r_ref[...], g_ref[...])
        dx_ref[...] = dout_ref[...] + dx
        dg_ref[...] += dg

    return _call_after(
        dep, body, (da4, dsz, dxbc_in, dcqa, dckv, dcz, dtail_a, dtail_b, w, x, rstd, g, dout), name="inproj_bwd", grid=(s // ts,),
        in_specs=[pl.BlockSpec((4, ts, D_CONV_A), lambda i: (0, i, 0)), _row(ts, D_SSD), _row(ts, N_XBC), _row(ts, Q_LORA),
                  _row(ts, KV_LORA), _row(ts, D_MLA), _row(ts, LANE), _row(ts, LANE), _full((D_MODEL, NCOL)),
                  _row(ts, D_MODEL), _row(ts, 1), _full((1, D_MODEL)), _row(ts, D_MODEL)],
        out_specs=[_row(ts, NCOL), _row(ts, D_MODEL), _full((1, D_MODEL))],
        out_shape=[_sds((s, NCOL), MXU), _sds((s, D_MODEL)), _sds((1, D_MODEL))],
        compiler_params=_params(("arbitrary",)),
    )


DWIN_BLOCK = 640


def _dwin(h, dproj, dep=None):
    s = h.shape[0]

    def body(h_ref, d_ref, o_ref):
        o_ref[...] = _dot_tn(h_ref[...], d_ref[...])

    return _call_after(
        dep, body, (h, dproj), name="dwin", grid=(NCOL // DWIN_BLOCK,),
        in_specs=[_full((s, D_MODEL)), pl.BlockSpec((s, DWIN_BLOCK), lambda j: (0, j))],
        out_specs=pl.BlockSpec((D_MODEL, DWIN_BLOCK), lambda j: (0, j)),
        out_shape=_sds((D_MODEL, NCOL)),
        compiler_params=_params(("parallel",)),
    )


def _adamw(ws, gs, ms, vs, whole, layer=None, into=None):
    n = len(ws)
    bc1 = 1.0 - ADAM_B1 ** ADAM_STEP
    bc2 = 1.0 - ADAM_B2 ** ADAM_STEP

    def body(*refs):
        ins, outs = refs[:4 * n], refs[4 * n:]
        for a in range(n):
            w_ref, g_ref, m_ref, v_ref = ins[a], ins[n + a], ins[2 * n + a], ins[3 * n + a]
            gv = g_ref[...]
            mn = ADAM_B1 * m_ref[...] + (1.0 - ADAM_B1) * gv
            vn = ADAM_B2 * v_ref[...] + (1.0 - ADAM_B2) * (gv * gv)
            outs[n + a][...] = mn
            outs[2 * n + a][...] = vn
            outs[a][...] = -ADAM_LR * ((mn / bc1) / (jnp.sqrt(vn / bc2) + ADAM_EPS) + ADAM_WD * w_ref[...])

    if whole:
        grid, blks = (1,), [pl.BlockSpec(w.shape, lambda i, _n=w.ndim: (0,) * _n) for w in ws]
    elif layer is not None:
        assert n == 1
        rows, cols = ws[0].shape[1:]
        grid = (2,)
        blk = pl.BlockSpec((1, rows // 2, cols), lambda k: (layer, k, 0))
        gblk = pl.BlockSpec((1, rows // 2, cols), lambda k: (0, k, 0))

        def body1(w_ref, g_ref, m_ref, v_ref, *rest):
            go_ref, d_ref, mo_ref, vo_ref = rest[-4:]
            go_ref[...] = g_ref[...]
            body(w_ref, g_ref, m_ref, v_ref, d_ref, mo_ref, vo_ref)

        extra = list(into) if into is not None else []
        out = pl.pallas_call(
            body1, name=f"adamw_layer{layer}", grid=grid,
            in_specs=[blk, gblk, blk, blk] + [ANY] * len(extra), out_specs=[blk] * 4, out_shape=[_sds(ws[0].shape)] * 4,
            input_output_aliases={4 + i: i for i in range(len(extra))},
            compiler_params=_params(("parallel",)),
        )(ws[0], gs[0][None], ms[0], vs[0], *extra)
        return list(out)
    else:
        grid = (ws[0].shape[0], 2)
        blks = [pl.BlockSpec((1, w.shape[1] // 2, w.shape[2]), lambda i, k: (i, k, 0)) for w in ws]
    out = pl.pallas_call(
        body, name="adamw", grid=grid,
        in_specs=blks * 4, out_specs=blks * 3, out_shape=[_sds(w.shape) for w in ws] * 3,
        compiler_params=_params(("parallel",) * len(grid)),
    )(*ws, *gs, *ms, *vs)
    return [(out[a], out[n + a], out[2 * n + a]) for a in range(n)]


COL_MOVES = ((0, 0, 1024), (1024, O_SZ, 384), (1408, O_XBC, 896), (2304, O_TAIL + DT_LANE, 6), (2310, O_CQA, 256),
             (2566, O_CKV, 128), (2694, O_TAIL, 32), (2726, O_CZ, 384))


def _move_cols(w, moves, width):
    out = None
    for src, dst, n in moves:
        piece = jnp.pad(w[..., src:src + n], [(0, 0)] * (w.ndim - 1) + [(dst, width - dst - n)])
        out = piece if out is None else out + piece
    return out


def _perm_cols(w):
    return _move_cols(w, COL_MOVES, NCOL)


def _unperm_cols(g):
    return _move_cols(g, [(dst, src, n) for src, dst, n in COL_MOVES], IN_COLS)


def _wq_layout(wt):
    return jnp.pad(wt.reshape(MLA_HEADS, QK_NOPE + QK_ROPE, Q_LORA), ((0, 0), (0, 32), (0, 0))).reshape(MLA_HEADS * LANE, Q_LORA)


def _wq_unlayout(g):
    return g.reshape(MLA_HEADS, LANE, Q_LORA)[:, :QK_NOPE + QK_ROPE].reshape(MLA_HEADS * (QK_NOPE + QK_ROPE), Q_LORA)


def _wkv_layout(wt):
    t = wt.reshape(MLA_HEADS, 2, 64, KV_LORA).transpose(1, 0, 2, 3)
    return jnp.pad(t, ((0, 0), (0, 0), (0, 64), (0, 0))).reshape(2 * MLA_HEADS * LANE, KV_LORA)


def _wkv_unlayout(g):
    t = g.reshape(2, MLA_HEADS, LANE, KV_LORA)[:, :, :64]
    return t.transpose(1, 0, 2, 3).reshape(MLA_HEADS * LANE, KV_LORA)


def _rope_tables(positions):
    inv_freq = ROPE_BASE ** (-jnp.arange(0, QK_ROPE, 2, dtype=F32) / QK_ROPE)
    ang = positions.astype(F32)[:, None] * inv_freq
    cos, sin = jnp.cos(ang), jnp.sin(ang)
    s = positions.shape[0]
    one, zero = jnp.ones((s, ROPE_LANE), F32), jnp.zeros((s, ROPE_LANE), F32)
    cos_t = jnp.concatenate([one, cos, cos, one[:, :32]], axis=1)
    sin_t = jnp.concatenate([zero, -sin, sin, zero[:, :32]], axis=1)
    return cos_t, sin_t


def _ssd_scalars(dt_bias, a_log, d_skip):
    return jnp.pad(jnp.stack([dt_bias, a_log, d_skip]), ((0, 5), (DT_LANE, LANE - DT_LANE - SSD_HEADS)))


def _layer_fwd(x, lw, cos, sin, dep=None, late=None):
    proj, h, rstd = _inproj_fwd(x, lw["norm_g"], lw["w_in"], dep)
    ya = _conva_fwd(proj, lw["conv_a_w"])
    xbc = _sconv_fwd(proj, lw["ssd_conv_w"], lw["ssd_conv_b"])
    y_ssd, states = _ssd_fwd(xbc, proj, lw["sc"])
    if late is not None:
        lw = {**lw, **late(ya, y_ssd)}
    q, k, v, qn, kvn, rq, rkv = _mla_prep_fwd(proj, lw["gq"], lw["gkv"], lw["wq"], lw["wkv"], cos, sin)
    o, lse = _attn_fwd(q, k, v)
    x_out, y = _outproj_fwd(x, proj, ya, y_ssd, o, lw["g_ssd"], lw["w_out"])
    saved = dict(x=x, proj=proj, h=h, rstd=rstd, xbc=xbc, y_ssd=y_ssd, states=states, q=q, k=k, v=v, qn=qn, kvn=kvn,
                 rq=rq, rkv=rkv, o=o, lse=lse, y=y)
    return x_out, saved, lw


def _layer_bwd(dout, lw, sv, cos, sin, rs=None, begin_early=None):
    tok = lambda: None if rs is None else rs["h"]["token"]
    dya, dys, dsz, d_o, dcz, dg_ssd, dw_out = _outproj_bwd(dout, sv["y"], lw["w_out"], sv["proj"], sv["y_ssd"], sv["o"],
                                                            lw["g_ssd"], tok())
    if rs is not None:
        rs = _rs_add_mine(rs, [dya])
    dq, dk, dv = _attn_bwd(sv["q"], sv["k"], sv["v"], sv["o"], d_o, sv["lse"], tok())
    dxbc, dtail_s, dsc = _ssd_bwd(sv["xbc"], sv["proj"], lw["sc"], sv["states"], dys, tok())
    da4, dw_conva = _conva_bwd(sv["proj"], lw["conv_a_w"], dya, tok())
    if rs is not None:
        rs = _rs_add_chips(rs, [dq, dxbc, da4])
    du, dw_sconv, db_sconv = _sconv_bwd(sv["proj"], lw["ssd_conv_w"], lw["ssd_conv_b"], dxbc, tok())
    dcqa, dckv, dtail_m, dwq, dwkv, dgq, dgkv = _mla_prep_bwd(
        dq, dk, dv, sv["proj"], sv["qn"], sv["kvn"], sv["rq"], sv["rkv"], lw["gq"], lw["gkv"], lw["wq"], lw["wkv"], cos, sin)
    early = None if begin_early is None else begin_early(dw_out, dwq, dwkv)
    etok = lambda: None if early is None else early["h"]["token"]
    dproj, dx, dg = _inproj_bwd(da4, dsz, du, dcqa, dckv, dcz, dtail_s, dtail_m, lw["w_in"], sv["x"], sv["rstd"],
                                lw["norm_g"], dout, etok())
    reduced = None if rs is None else _rs_end(rs, [du, dcqa, dx])
    if early is not None:
        early = _rs_add_mine(early, [dx])
    dw_in = _dwin(sv["h"], dproj, etok())
    if early is not None:
        early = _rs_add_chips(early, [dw_in])
    grads = dict(norm_g=dg, w_in=dw_in, conv_a_w=dw_conva, ssd_conv_w=dw_sconv, ssd_conv_b=db_sconv, sc=dsc,
                 g_ssd=dg_ssd, gq=dgq, wq=dwq, gkv=dgkv, wkv=dwkv, w_out=dw_out)
    return dx, grads, reduced, early


ANY = pl.BlockSpec(memory_space=pl.ANY)
N_CHIPS = 4
N_DEV = 8


def _place():
    return lax.axis_index("x"), lax.axis_index("y"), lax.axis_index("c")


HBM_SPEC = pl.BlockSpec(memory_space=pltpu.HBM)
SEM_SPEC = pl.BlockSpec(memory_space=pltpu.SEMAPHORE)
PAYLOAD = jnp.bfloat16


def _hbm(a):
    return pltpu.with_memory_space_constraint(a, pltpu.HBM)


def _run_plan(plan, srcs, lands, send_sems, recv_sems, start, wait):
    copies = plan(srcs, lands)
    if start:
        for i, (src, dst, _, to) in enumerate(copies):
            pltpu.make_async_remote_copy(src_ref=src, dst_ref=dst, send_sem=send_sems.at[i], recv_sem=recv_sems.at[i],
                                         device_id=to, device_id_type=MESH_T).start()
    if wait:
        for i, (src, _, arrives, to) in enumerate(copies):
            cp = pltpu.make_async_remote_copy(src_ref=src, dst_ref=arrives, send_sem=send_sems.at[i],
                                              recv_sem=recv_sems.at[i], device_id=to, device_id_type=MESH_T)
            cp.wait_send()
            cp.wait_recv()


def _exchange_fused(name, plan, n_copies, srcs, land_shapes):
    ns, nl = len(srcs), len(land_shapes)

    def body(*refs):
        _run_plan(plan, refs[:ns], refs[ns:ns + nl], refs[ns + nl], refs[ns + nl + 1], True, True)

    return pl.pallas_call(
        body, name=name, in_specs=[ANY] * ns, out_specs=[ANY] * nl, out_shape=list(land_shapes),
        scratch_shapes=[pltpu.SemaphoreType.DMA((n_copies,)), pltpu.SemaphoreType.DMA((n_copies,))],
    )(*srcs)


def _exchange_start(name, plan, n_copies, srcs, land_shapes, deps):
    ns, nl = len(srcs), len(land_shapes)
    n_in = ns + nl + len(deps)

    def body(*refs):
        send_sems, recv_sems = refs[n_in], refs[n_in + 1]
        token = refs[-1]
        _run_plan(plan, refs[:ns], refs[ns:ns + nl], send_sems, recv_sems, True, False)
        token[...] = jnp.zeros_like(token)

    thru = [pltpu.HBM(a.shape, a.dtype) for a in srcs] + [pltpu.HBM(a.shape, a.dtype) for a in land_shapes]
    outs = pl.pallas_call(
        body, name=name,
        out_shape=(pltpu.SemaphoreType.DMA((n_copies,)), pltpu.SemaphoreType.DMA((n_copies,)), *thru, _sds((8, LANE))),
        in_specs=[HBM_SPEC] * (ns + nl) + [ANY] * len(deps),
        out_specs=(SEM_SPEC, SEM_SPEC, *[HBM_SPEC] * (ns + nl), pl.BlockSpec(memory_space=pltpu.VMEM)),
        input_output_aliases={i: 2 + i for i in range(ns + nl)},
        compiler_params=pltpu.CompilerParams(has_side_effects=pltpu.SideEffectType.DATAFLOW_SIDE_EFFECTING),
    )(*[_hbm(a) for a in srcs], *[_hbm(lax.empty(a.shape, a.dtype)) for a in land_shapes], *deps)
    return (outs[0], outs[1]), list(outs[2:2 + ns]), list(outs[2 + ns:2 + ns + nl]), outs[-1]


def _exchange_wait(name, plan, sems, srcs, lands, after):
    ns, nl = len(srcs), len(lands)

    def body(*refs):
        _run_plan(plan, refs[:ns], refs[ns:ns + nl], refs[ns + nl], refs[ns + nl + 1], False, True)

    outs = pl.pallas_call(
        body, name=name,
        out_shape=[pltpu.HBM(a.shape, a.dtype) for a in list(srcs) + list(lands)],
        in_specs=[HBM_SPEC] * (ns + nl) + [SEM_SPEC, SEM_SPEC] + [ANY] * len(after), out_specs=[HBM_SPEC] * (ns + nl),
        input_output_aliases={i: i for i in range(ns + nl)},
        compiler_params=pltpu.CompilerParams(has_side_effects=pltpu.SideEffectType.DATAFLOW_SIDE_EFFECTING),
    )(*srcs, *lands, sems[0], sems[1], *after)
    return list(outs[:ns]), list(outs[ns:])


def _xchg_begin(name, plan, n_copies, srcs, land_shapes, split, deps=()):
    if not split:
        return dict(split=False, srcs=list(srcs), lands=_exchange_fused(name, plan, n_copies, srcs, land_shapes),
                    token=jnp.zeros((8, LANE), F32))
    sems, srcs_t, lands_t, token = _exchange_start(name + "_start", plan, n_copies, srcs, land_shapes, list(deps))
    return dict(split=True, name=name, plan=plan, sems=sems, srcs=srcs_t, lands=lands_t, token=token)


def _xchg_end(h, after):
    if not h["split"]:
        return h["srcs"], h["lands"]
    return _exchange_wait(h["name"] + "_wait", h["plan"], h["sems"], h["srcs"], h["lands"], after)


def _other_chips():
    x, y, c = _place()
    return [(1 - x, y), (x, 1 - y), (1 - x, 1 - y)]


def _gather_plan(srcs, lands):
    x, y, c = _place()
    me = 2 * x + y
    return [(srcs[a], lands[a].at[me], lands[a].at[2 * cx + cy], (cx, cy, c))
            for (cx, cy) in _other_chips() for a in range(len(srcs))]


def _gather_begin(shards, split, tag, deps=()):
    shapes = [_sds((N_CHIPS,) + a.shape, a.dtype) for a in shards]
    return _xchg_begin(f"gather_{tag}", _gather_plan, 3 * len(shards), shards, shapes, split, deps)


def _gather_end(h, after):
    shards, lands = _xchg_end(h, after)
    me = 2 * lax.axis_index("x") + lax.axis_index("y")
    return [lax.dynamic_update_index_in_dim(g, s, me, 0) for g, s in zip(lands, shards)]


def _swap_plan(srcs, lands):
    x, y, c = _place()
    return [(srcs[a].at[:, 1 - c], lands[a], lands[a], (x, y, 1 - c)) for a in range(len(srcs))]


def _chips_plan(srcs, lands):
    x, y, c = _place()
    me = 2 * x + y
    return [(srcs[a].at[2 * cx + cy], lands[a].at[me], lands[a].at[2 * cx + cy], (cx, cy, c))
            for (cx, cy) in _other_chips() for a in range(len(srcs))]


def _share_plan(srcs, lands):
    x, y, c = _place()
    return [(srcs[a], lands[a].at[c], lands[a].at[1 - c], (x, y, 1 - c)) for a in range(len(srcs))]


def _allreduce_small(slab, dep=None):
    r = slab.shape[0]

    def body(s_ref, o_ref, gath, send_sems, recv_sems):
        x, y, c = _place()
        me = 4 * x + 2 * y + c
        gath[me] = s_ref[...]
        cps = []
        for rel in range(1, N_DEV):
            px = 1 - x if rel & 4 else x
            py = 1 - y if rel & 2 else y
            pc = 1 - c if rel & 1 else c
            cp = pltpu.make_async_remote_copy(src_ref=s_ref, dst_ref=gath.at[me], send_sem=send_sems.at[rel - 1],
                                              recv_sem=recv_sems.at[rel - 1], device_id=(px, py, pc), device_id_type=MESH_T)
            cp.start()
            cps.append(cp)
        for cp in cps:
            cp.wait()
        acc = gath[0]
        for d in range(1, N_DEV):
            acc = acc + gath[d]
        o_ref[...] = acc

    vm = pl.BlockSpec(memory_space=pltpu.VMEM)
    return _call_after(
        dep, body, (slab,), name="allreduce_small", in_specs=[vm], out_specs=vm, out_shape=_sds((r, LANE)),
        scratch_shapes=[pltpu.VMEM((N_DEV, r, LANE), F32), pltpu.SemaphoreType.DMA((N_DEV - 1,)),
                        pltpu.SemaphoreType.DMA((N_DEV - 1,))],
    )


def _add_mine(g4s, recvs, half):
    n = len(g4s)

    def body(h_ref, *refs):
        for g_ref, r_ref, o_ref in zip(refs[:n], refs[n:2 * n], refs[2 * n:]):
            o_ref[0] = (g_ref[0, 0] + r_ref[0]).astype(o_ref.dtype)

    dims = [g.shape[2:] for g in g4s]
    return pl.pallas_call(
        body, name="add_mine",
        grid_spec=pltpu.PrefetchScalarGridSpec(
            num_scalar_prefetch=1, grid=(N_CHIPS,),
            in_specs=[pl.BlockSpec((1, 1) + d, lambda j, h: (j, h[0], 0, 0)) for d in dims]
            + [pl.BlockSpec((1,) + d, lambda j, h: (j, 0, 0)) for d in dims],
            out_specs=[pl.BlockSpec((1,) + d, lambda j, h: (j, 0, 0)) for d in dims]),
        out_shape=[_sds((N_CHIPS,) + d, PAYLOAD) for d in dims],
        compiler_params=_params(("parallel",)),
    )(half, *g4s, *recvs)


def _add_chips(es, ps, me):
    n = len(es)

    def body(m_ref, *refs):
        for e_ref, p_ref, o_ref in zip(refs[:n], refs[n:2 * n], refs[2 * n:]):
            own = p_ref[0].astype(F32)
            acc = None
            for s in range(N_CHIPS):
                t = jnp.where(m_ref[0] == s, own, e_ref[s].astype(F32))
                acc = t if acc is None else acc + t
            o_ref[...] = acc

    dims = [e.shape[1:] for e in es]
    return pl.pallas_call(
        body, name="add_chips",
        grid_spec=pltpu.PrefetchScalarGridSpec(
            num_scalar_prefetch=1, grid=(1,),
            in_specs=[pl.BlockSpec((N_CHIPS,) + d, lambda i, m: (0, 0, 0)) for d in dims]
            + [pl.BlockSpec((1,) + d, lambda i, m: (m[0], 0, 0)) for d in dims],
            out_specs=[pl.BlockSpec(d, lambda i, m: (0, 0)) for d in dims]),
        out_shape=[_sds(d) for d in dims],
        compiler_params=_params(("arbitrary",)),
    )(me, *es, *ps)


def _rs_begin(gs, split, tag, deps=()):
    g4 = [g.reshape(N_CHIPS, 2, g.shape[0] // (2 * N_CHIPS), g.shape[1]) for g in gs]
    h = _xchg_begin(f"rs_swap_{tag}", _swap_plan, len(gs), g4, [_sds((N_CHIPS,) + g.shape[2:]) for g in g4], split, deps)
    return dict(h=h, split=split, tag=tag, shapes=[g.shape for g in gs])


def _rs_add_mine(st, after):
    g4, recv = _xchg_end(st["h"], after)
    half = jnp.reshape(lax.axis_index("c"), (1,)).astype(jnp.int32)
    ps = _add_mine(g4, recv, half)
    st["h"] = _xchg_begin(f"rs_chips_{st['tag']}", _chips_plan, 3 * len(ps), ps, [_sds(p.shape, p.dtype) for p in ps], st["split"])
    return st


def _rs_add_chips(st, after):
    ps, es = _xchg_end(st["h"], after)
    me = jnp.reshape(2 * lax.axis_index("x") + lax.axis_index("y"), (1,)).astype(jnp.int32)
    fs = _add_chips(es, ps, me)
    st["h"] = _xchg_begin(f"rs_share_{st['tag']}", _share_plan, len(fs), fs, [_sds((2,) + f.shape) for f in fs], st["split"])
    return st


def _rs_end(st, after):
    fs, ss = _xchg_end(st["h"], after)
    c = lax.axis_index("c")
    return [lax.dynamic_update_index_in_dim(s, f, c, 0).reshape(shp[0] // N_CHIPS, shp[1])
            for s, f, shp in zip(ss, fs, st["shapes"])]


WEIGHTS = ["norm_g", "w_in", "conv_a_w", "ssd_conv_w", "ssd_conv_b", "ssd_dt_bias", "ssd_a_log", "ssd_d", "ssd_norm_g",
           "mla_q_norm_g", "w_qb", "mla_kv_norm_g", "w_kvb", "w_out", "final_norm_g"]
BIG = ["w_in", "w_qb", "w_kvb", "w_out"]
SLAB_ROWS = 128


def _to_slab(parts, rows):
    flat = jnp.concatenate([p.reshape(-1) for p in parts])
    return jnp.pad(flat, (0, rows * LANE - flat.shape[0])).reshape(rows, LANE)


def _from_slab(slab, shapes):
    flat = slab.reshape(-1)
    out, off = [], 0
    for shp in shapes:
        n = int(np.prod(shp))
        out.append(flat[off:off + n].reshape(shp))
        off += n
    return out


def kernel(x, positions, norm_g, w_in, conv_a_w, ssd_conv_w, ssd_conv_b, ssd_dt_bias, ssd_a_log, ssd_d, ssd_norm_g, mla_q_norm_g, w_qb, mla_kv_norm_g, w_kvb, w_out, final_norm_g, loss_target, m_norm_g, m_w_in, m_conv_a_w, m_ssd_conv_w, m_ssd_conv_b, m_ssd_dt_bias, m_ssd_a_log, m_ssd_d, m_ssd_norm_g, m_mla_q_norm_g, m_w_qb, m_mla_kv_norm_g, m_w_kvb, m_w_out, m_final_norm_g, v_norm_g, v_w_in, v_conv_a_w, v_ssd_conv_w, v_ssd_conv_b, v_ssd_dt_bias, v_ssd_a_log, v_ssd_d, v_ssd_norm_g, v_mla_q_norm_g, v_w_qb, v_mla_kv_norm_g, v_w_kvb, v_w_out, v_final_norm_g):
    w = dict(norm_g=norm_g, w_in=w_in, conv_a_w=conv_a_w, ssd_conv_w=ssd_conv_w, ssd_conv_b=ssd_conv_b,
             ssd_dt_bias=ssd_dt_bias, ssd_a_log=ssd_a_log, ssd_d=ssd_d, ssd_norm_g=ssd_norm_g, mla_q_norm_g=mla_q_norm_g,
             w_qb=w_qb, mla_kv_norm_g=mla_kv_norm_g, w_kvb=w_kvb, w_out=w_out, final_norm_g=final_norm_g)
    mom = dict(norm_g=m_norm_g, w_in=m_w_in, conv_a_w=m_conv_a_w, ssd_conv_w=m_ssd_conv_w, ssd_conv_b=m_ssd_conv_b,
               ssd_dt_bias=m_ssd_dt_bias, ssd_a_log=m_ssd_a_log, ssd_d=m_ssd_d, ssd_norm_g=m_ssd_norm_g,
               mla_q_norm_g=m_mla_q_norm_g, w_qb=m_w_qb, mla_kv_norm_g=m_mla_kv_norm_g, w_kvb=m_w_kvb, w_out=m_w_out,
               final_norm_g=m_final_norm_g)
    var = dict(norm_g=v_norm_g, w_in=v_w_in, conv_a_w=v_conv_a_w, ssd_conv_w=v_ssd_conv_w, ssd_conv_b=v_ssd_conv_b,
               ssd_dt_bias=v_ssd_dt_bias, ssd_a_log=v_ssd_a_log, ssd_d=v_ssd_d, ssd_norm_g=v_ssd_norm_g,
               mla_q_norm_g=v_mla_q_norm_g, w_qb=v_w_qb, mla_kv_norm_g=v_mla_kv_norm_g, w_kvb=v_w_kvb, w_out=v_w_out,
               final_norm_g=v_final_norm_g)
    chip = 2 * lax.axis_index("x") + lax.axis_index("y")

    def early_shard(l, zero):
        pack = jnp.pad(conv_a_w[l], ((0, 5), (0, 192))) + jnp.pad(ssd_conv_w[l], ((3, 1), (0, 32)))
        return [(_perm_cols(w_in[l]) + zero).astype(MXU), pack + zero]

    def late_shard(l, zero):
        return [(w_out[l] + zero).astype(MXU), (w_qb[l].T + zero).astype(MXU), (w_kvb[l].T + zero).astype(MXU)]

    def early_weights(l, gathered):
        g_in, g_conv = gathered
        return dict(
            norm_g=norm_g[l][None], w_in=g_in.reshape(D_MODEL, NCOL),
            conv_a_w=jnp.concatenate([g_conv[j, 0:3, 0:64] for j in range(N_CHIPS)], axis=1),
            ssd_conv_w=jnp.concatenate([g_conv[j, 3:7, 0:224] for j in range(N_CHIPS)], axis=1),
            ssd_conv_b=ssd_conv_b[l][None], sc=_ssd_scalars(ssd_dt_bias[l], ssd_a_log[l], ssd_d[l]),
            g_ssd=ssd_norm_g[l][None], gq=mla_q_norm_g[l][None], gkv=mla_kv_norm_g[l][None])

    def late_weights(gathered):
        g_out, g_qb, g_kvb = gathered
        return dict(wq=_wq_layout(g_qb.reshape(MLA_HEADS * 96, Q_LORA)), wkv=_wkv_layout(g_kvb.reshape(MLA_HEADS * LANE, KV_LORA)),
                    w_out=g_out.reshape(D_MODEL, D_MODEL))

    def late_grads(dw_out, dwq, dwkv):
        wq = jnp.pad(_wq_unlayout(dwq).reshape(N_CHIPS, 144, Q_LORA), ((0, 0), (0, 16), (0, 0)))
        return [dw_out, wq.reshape(N_CHIPS * 160, Q_LORA), _wkv_unlayout(dwkv)]

    def large_grads(g):
        return [g["w_in"]] + late_grads(g["w_out"], g["wq"], g["wkv"])

    gather_a0 = _gather_begin(early_shard(0, 0.0), True, "a0")
    zero = gather_a0["token"][0, 0]
    cos, sin = _rope_tables(positions[0] + zero.astype(jnp.int32))
    late0, shards1 = late_shard(0, zero), early_shard(1, zero) + late_shard(1, zero)
    opt_in = {nm: [w[nm], mom[nm], var[nm]] for nm in BIG}
    opt_in["w_in"] = [w["w_in"], mom["w_in"] + zero, var["w_in"] + zero]
    lw0 = early_weights(0, _gather_end(gather_a0, [cos, sin] + late0 + shards1 + opt_in["w_in"][1:]))
    gather_b0 = _gather_begin(late0, True, "b0")
    gather_1 = _gather_begin(shards1, True, "1", [gather_b0["token"]])
    x1, sv0, lw0 = _layer_fwd(x[0], lw0, cos, sin, gather_1["token"],
                              lambda ya, y_ssd: late_weights(_gather_end(gather_b0, [ya, y_ssd])))
    g1 = _gather_end(gather_1, [x1])
    x2, sv1, lw1 = _layer_fwd(x1, {**early_weights(1, g1[:2]), **late_weights(g1[2:])}, cos, sin)
    dx, dgf, loss = _loss_head(x2, final_norm_g[None], loss_target[0])

    dx, lg1, _, _ = _layer_bwd(dx, lw1, sv1, cos, sin)
    grad_x, lg0, red1, rs0_late = _layer_bwd(dx, lw0, sv0, cos, sin, _rs_begin(large_grads(lg1), True, 1),
                                             lambda *g: _rs_begin(late_grads(*g), True, "0l"))
    rs0 = _rs_begin([lg0["w_in"]], True, 0, [rs0_late["h"]["token"]])
    lg = [lg0, lg1]
    grad = {}

    small_names = ["norm_g", "conv_a_w", "ssd_conv_w", "ssd_conv_b", "sc", "g_ssd", "gq", "gkv"]
    parts = [loss[0, 0:1], dgf]
    for l in range(DEPTH):
        parts += [lg[l][nm][:3, DT_LANE:DT_LANE + SSD_HEADS] if nm == "sc" else lg[l][nm] for nm in small_names]
    shapes = [(1,), (D_MODEL,)] + [(D_MODEL,), (3, D_CONV_A), (4, N_XBC), (N_XBC,), (3, SSD_HEADS), (D_SSD,), (Q_LORA,), (KV_LORA,)] * DEPTH
    red_slab = _allreduce_small(_to_slab(parts, SLAB_ROWS), rs0["h"]["token"])
    rs0 = _rs_add_mine(rs0, [red_slab])
    red = _from_slab(red_slab + rs0["h"]["token"][0, 0], shapes)
    loss_out = red[0][0]
    grad["final_norm_g"] = red[1]
    per = [red[2 + 8 * l:10 + 8 * l] for l in range(DEPTH)]
    grad["norm_g"] = jnp.stack([per[l][0] for l in range(DEPTH)])
    grad["conv_a_w"] = lax.dynamic_slice_in_dim(jnp.stack([per[l][1] for l in range(DEPTH)]), chip * 64, 64, axis=2)
    grad["ssd_conv_w"] = lax.dynamic_slice_in_dim(jnp.stack([per[l][2] for l in range(DEPTH)]), chip * 224, 224, axis=2)
    grad["ssd_conv_b"] = jnp.stack([per[l][3] for l in range(DEPTH)])
    grad["ssd_dt_bias"] = jnp.stack([per[l][4][0] for l in range(DEPTH)])
    grad["ssd_a_log"] = jnp.stack([per[l][4][1] for l in range(DEPTH)])
    grad["ssd_d"] = jnp.stack([per[l][4][2] for l in range(DEPTH)])
    grad["ssd_norm_g"] = jnp.stack([per[l][5] for l in range(DEPTH)])
    grad["mla_q_norm_g"] = jnp.stack([per[l][6] for l in range(DEPTH)])
    grad["mla_kv_norm_g"] = jnp.stack([per[l][7] for l in range(DEPTH)])

    delta, new_m, new_v = {}, {}, {}
    small = [nm for nm in WEIGHTS if nm not in BIG]
    row2 = lambda a: a[None] if a.ndim == 1 else a
    small_out = _adamw(*[[row2(a[nm]) for nm in small] for a in (w, grad, mom, var)], whole=True)
    for nm, (dv, mv, vv) in zip(small, small_out):
        delta[nm], new_m[nm], new_v[nm] = [a.reshape(w[nm].shape) for a in (dv, mv, vv)]

    r_out, r_qb, r_kvb = [jnp.stack([a, b]) for a, b in zip(_rs_end(rs0_late, [red_slab]), red1[1:])]
    grad.update(w_out=r_out, w_qb=jnp.swapaxes(r_qb[:, :144], 1, 2), w_kvb=jnp.swapaxes(r_kvb, 1, 2))
    late = [nm for nm in BIG if nm != "w_in"]
    late_out = _adamw([opt_in[nm][0] for nm in late], [grad[nm] for nm in late], [opt_in[nm][1] for nm in late],
                      [opt_in[nm][2] for nm in late], whole=False)
    for nm, (dv, mv, vv) in zip(late, late_out):
        delta[nm], new_m[nm], new_v[nm] = dv, mv, vv
    w_in_opt = [[a] for a in opt_in["w_in"]]
    w_in_l1 = _adamw(w_in_opt[0], [_unperm_cols(red1[0])], w_in_opt[1], w_in_opt[2], whole=False, layer=1)

    shadow_work = [a for row in small_out + late_out for a in row] + [grad[nm] for nm in small] + w_in_l1
    r_in0, = _rs_end(_rs_add_chips(rs0, shadow_work), [])
    grad["w_in"], delta["w_in"], new_m["w_in"], new_v["w_in"] = _adamw(
        w_in_opt[0], [_unperm_cols(r_in0)], w_in_opt[1], w_in_opt[2], whole=False, layer=0, into=w_in_l1)

    return (loss_out, grad_x[None], *[grad[nm] for nm in WEIGHTS], *[delta[nm] for nm in WEIGHTS],
            *[new_m[nm] for nm in WEIGHTS], *[new_v[nm] for nm in WEIGHTS])
```

```python
import functools
import math

import numpy as np
import jax
import jax.numpy as jnp
from jax import lax
from jax.experimental import pallas as pl
from jax.experimental.pallas import tpu as pltpu

F32 = jnp.float32
MXU = jnp.bfloat16

D_MODEL = 1024
DEPTH = 2
D_CONV_A = 256
D_SSD = 384
SSD_HEADS = 6
SSD_BC = 256
SSD_CHUNK = 128
SSD_CHUNKS_PER_STEP = 4
SSD_NORM_EPS = 1e-5
MLA_HEADS = 6
Q_LORA = 256
KV_LORA = 128
QK_NOPE = 64
QK_ROPE = 32
V_DIM = 64
D_MLA = 384
ROPE_BASE = 10000.0
NORM_EPS = 1e-6
IN_COLS = 3110
LANE = 128

O_AH, O_AB, O_AC, O_AZ = 0, 256, 512, 768
O_XBC = 1024
O_SZ = 1920
O_CQA = 2304
O_CKV = 2560
O_CZ = 2688
O_TAIL = 3072
NCOL = 3200
N_XBC = D_SSD + 2 * SSD_BC
DT_LANE = 32
ROPE_LANE = 64

ADAM_LR, ADAM_B1, ADAM_B2, ADAM_EPS, ADAM_WD, ADAM_STEP = 0.001, 0.9, 0.999, 1e-08, 0.01, 10

VMEM_LIMIT = 56 * 1024 * 1024
MESH_T = pl.DeviceIdType.MESH


def _dot(a, b):
    return jnp.dot(a.astype(MXU), b.astype(MXU), preferred_element_type=F32)


def _dot_nt(a, b):
    return lax.dot_general(a.astype(MXU), b.astype(MXU), (((1,), (1,)), ((), ())), preferred_element_type=F32)


def _dot_tn(a, b):
    return lax.dot_general(a.astype(MXU), b.astype(MXU), (((0,), (0,)), ((), ())), preferred_element_type=F32)


def _dot_hi(a, b):
    return jnp.dot(a, b, precision=lax.Precision.HIGHEST, preferred_element_type=F32)


def _dot_hi_tn(a, b):
    return lax.dot_general(a, b, (((0,), (0,)), ((), ())), precision=lax.Precision.HIGHEST, preferred_element_type=F32)


def _sigmoid(z):
    return 1.0 / (1.0 + jnp.exp(-z))


def _silu(z):
    return z * _sigmoid(z)


def _dsilu(z):
    s = _sigmoid(z)
    return s * (1.0 + z * (1.0 - s))


def _softplus(z):
    e = jnp.exp(-jnp.abs(z))
    return jnp.maximum(z, 0.0) + jnp.where(e < 1e-3, e * (1.0 - 0.5 * e), jnp.log(1.0 + e))


def _iota(shape, dim):
    return lax.broadcasted_iota(jnp.int32, shape, dim)


def _shift_down(u, k):
    if k == 0:
        return u
    return jnp.where(_iota(u.shape, 0) >= k, pltpu.roll(u, k, 0), 0.0)


def _shift_up(u, k):
    if k == 0:
        return u
    n = u.shape[0]
    return jnp.where(_iota(u.shape, 0) < n - k, pltpu.roll(u, n - k, 0), 0.0)


def _rope_swap(t):
    lane = _iota(t.shape, 1)
    lo = (lane >= ROPE_LANE) & (lane < ROPE_LANE + 16)
    hi = (lane >= ROPE_LANE + 16) & (lane < ROPE_LANE + 32)
    return jnp.where(lo, pltpu.roll(t, LANE - 16, 1), jnp.where(hi, pltpu.roll(t, 16, 1), 0.0))


def _params(sem=None):
    return pltpu.CompilerParams(dimension_semantics=sem, vmem_limit_bytes=VMEM_LIMIT)


def _full(shape):
    nd = len(shape)
    return pl.BlockSpec(shape, lambda *_: (0,) * nd)


def _sds(shape, dtype=F32):
    return jax.ShapeDtypeStruct(shape, dtype)


def _tile(s):
    return min(512, s)


def _row(ts, w):
    return pl.BlockSpec((ts, w), lambda i: (i, 0))


def _gate_cols(ts, off):
    return pl.BlockSpec((ts, D_SSD), lambda i, _o=off // D_SSD: (i, _o))


def _col(s, off):
    return pl.BlockSpec((s, LANE), lambda j, _o=off // LANE: (0, _o + j))


def _call_after(dep, body, args, *, in_specs, **kw):
    if dep is None:
        return pl.pallas_call(body, in_specs=in_specs, **kw)(*args)
    n = len(args)

    def body_dep(*refs):
        body(*refs[:n], *refs[n + 1:])

    return pl.pallas_call(body_dep, in_specs=list(in_specs) + [pl.BlockSpec(memory_space=pl.ANY)], **kw)(*args, dep)


def _rms(c, g):
    r = lax.rsqrt(jnp.mean(c * c, axis=-1, keepdims=True) + NORM_EPS)
    return c * r * g, r


def _rms_bwd(dn, c, r, g):
    ch = c * r
    dch = dn * g
    dc = r * (dch - ch * jnp.mean(dch * ch, axis=-1, keepdims=True))
    return dc, jnp.sum(dn * ch, axis=0, keepdims=True)


def _inproj_fwd(x, g, w, dep=None):
    s = x.shape[0]
    ts = _tile(s)

    def body(x_ref, g_ref, w_ref, proj_ref, h_ref, r_ref):
        hn, r = _rms(x_ref[...], g_ref[...])
        h = hn.astype(MXU)
        h_ref[...] = h
        r_ref[...] = r
        proj_ref[...] = jnp.dot(h, w_ref[...], preferred_element_type=F32)

    return _call_after(
        dep, body, (x, g, w), name="inproj_fwd", grid=(s // ts,),
        in_specs=[_row(ts, D_MODEL), _full((1, D_MODEL)), _full((D_MODEL, NCOL))],
        out_specs=[_row(ts, NCOL), _row(ts, D_MODEL), _row(ts, 1)],
        out_shape=[_sds((s, NCOL)), _sds((s, D_MODEL), MXU), _sds((s, 1))],
        compiler_params=_params(("parallel",)),
    )


def _conva_fwd(proj, w):
    s = proj.shape[0]

    def body(h_ref, b_ref, c_ref, z_ref, w_ref, y_ref):
        u = c_ref[...] * h_ref[...]
        wv = w_ref[...]
        cv = wv[2:3, :] * u + wv[1:2, :] * _shift_down(u, 1) + wv[0:1, :] * _shift_down(u, 2)
        y_ref[...] = b_ref[...] * cv * _silu(z_ref[...])

    return pl.pallas_call(
        body, name="conva_fwd", grid=(D_CONV_A // LANE,),
        in_specs=[_col(s, O_AH), _col(s, O_AB), _col(s, O_AC), _col(s, O_AZ), pl.BlockSpec((3, LANE), lambda j: (0, j))],
        out_specs=pl.BlockSpec((s, LANE), lambda j: (0, j)),
        out_shape=_sds((s, D_CONV_A)),
        compiler_params=_params(("parallel",)),
    )(proj, proj, proj, proj, w)


def _sconv_pre(u, wv, bv):
    return (wv[3:4, :] * u + wv[2:3, :] * _shift_down(u, 1) + wv[1:2, :] * _shift_down(u, 2)
            + wv[0:1, :] * _shift_down(u, 3) + bv)


def _sconv_fwd(proj, w, b):
    s = proj.shape[0]

    def body(u_ref, w_ref, b_ref, o_ref):
        o_ref[...] = _silu(_sconv_pre(u_ref[...], w_ref[...], b_ref[...]))

    return pl.pallas_call(
        body, name="sconv_fwd", grid=(N_XBC // LANE,),
        in_specs=[_col(s, O_XBC), pl.BlockSpec((4, LANE), lambda j: (0, j)), pl.BlockSpec((1, LANE), lambda j: (0, j))],
        out_specs=pl.BlockSpec((s, LANE), lambda j: (0, j)),
        out_shape=_sds((s, N_XBC)),
        compiler_params=_params(("parallel",)),
    )(proj, w, b)


def _ssd_chunk_common(tail, sc):
    l = SSD_CHUNK
    lane = _iota((l, LANE), 1)
    row = _iota((l, LANE), 0)
    tri = (row >= lane).astype(F32)
    a_row = -jnp.exp(sc[1:2, :])
    pre = tail + sc[0:1, :]
    dt = _softplus(pre)
    a_cs = _dot_hi(tri, dt * a_row)
    return lane, row, tri, a_row, pre, dt, a_cs, a_cs.T


def _pick_col(m, lane, k):
    return jnp.sum(jnp.where(lane == k, m, 0.0), axis=1, keepdims=True)


def _pick_row(m, row, k):
    return jnp.sum(jnp.where(row == k, m, 0.0), axis=0, keepdims=True)


def _ssd_fwd(xbc, proj, sc):
    s = xbc.shape[0]
    nc = s // SSD_CHUNK
    l = SSD_CHUNK
    cps = SSD_CHUNKS_PER_STEP

    def body(xbc_ref, tail_ref, sc_ref, y_ref, st_ref, state):
        @pl.when(pl.program_id(0) == 0)
        def _():
            state[...] = jnp.zeros_like(state)

        sc_v = sc_ref[...]
        lane1 = _iota((1, LANE), 1)
        rowp = _iota((LANE, 1), 0)
        d_row = sc_v[2:3, :]
        states = [state[j] for j in range(3)]
        for u in range(cps):
            r = slice(u * l, (u + 1) * l)
            lane, row, _, _, _, dt, a_cs, a_t = _ssd_chunk_common(tail_ref[r, :], sc_v)
            for j in range(3):
                st_ref[u, j] = states[j]
            for j in range(3):
                xpair = xbc_ref[r, LANE * j:LANE * (j + 1)]
                sp = states[j]
                ypair = jnp.zeros((l, LANE), F32)
                new_s = jnp.zeros((LANE, LANE), F32)
                decay = jnp.zeros((LANE, 1), F32)
                for half in range(2):
                    h = 2 * j + half
                    g = h // 3
                    hm = (lane < 64) if half == 0 else (lane >= 64)
                    hrow = (rowp < 64) if half == 0 else (rowp >= 64)
                    ac = _pick_col(a_cs, lane, DT_LANE + h)
                    ar = _pick_row(a_t, row, DT_LANE + h)
                    dtc = _pick_col(dt, lane, DT_LANE + h)
                    alast = jnp.sum(jnp.where(lane1 == l - 1, ar, 0.0), axis=1, keepdims=True)
                    dh = jnp.sum(jnp.where(lane1 == DT_LANE + h, d_row, 0.0), axis=1, keepdims=True)
                    xm = jnp.where(hm, xpair, 0.0)
                    xd = xm * dtc
                    bm = xbc_ref[r, D_SSD + LANE * g:D_SSD + LANE * (g + 1)]
                    cm = xbc_ref[r, D_SSD + SSD_BC + LANE * g:D_SSD + SSD_BC + LANE * (g + 1)]
                    lm = jnp.where(row >= lane, jnp.exp(jnp.minimum(ac - ar, 0.0)), 0.0)
                    y_diag = _dot(_dot_nt(cm, bm) * lm, xd)
                    y_off = jnp.where(hm, _dot_nt(cm, sp), 0.0) * jnp.exp(ac)
                    ypair = ypair + y_diag + y_off + xm * dh
                    new_s = new_s + _dot_tn(xd * jnp.exp(alast - ac), bm)
                    decay = jnp.where(hrow, jnp.exp(alast), decay)
                states[j] = sp * decay + new_s
                y_ref[r, LANE * j:LANE * (j + 1)] = ypair
        for j in range(3):
            state[j] = states[j]

    return pl.pallas_call(
        body, name="ssd_fwd", grid=(nc // cps,),
        in_specs=[pl.BlockSpec((cps * l, N_XBC), lambda c: (c, 0)),
                  pl.BlockSpec((cps * l, LANE), lambda c: (c, O_TAIL // LANE)), _full((8, LANE))],
        out_specs=[pl.BlockSpec((cps * l, D_SSD), lambda c: (c, 0)), pl.BlockSpec((cps, 3, LANE, LANE), lambda c: (c, 0, 0, 0))],
        out_shape=[_sds((s, D_SSD)), _sds((nc, 3, LANE, LANE))],
        scratch_shapes=[pltpu.VMEM((3, LANE, LANE), F32)],
        compiler_params=_params(("arbitrary",)),
    )(xbc, proj, sc)


def _mla_prep_fwd(proj, gq, gkv, wq, wkv, cos, sin):
    s = proj.shape[0]
    ts = _tile(s)
    nh = MLA_HEADS

    def body(cqa_ref, ckv_ref, tail_ref, gq_ref, gkv_ref, wq_ref, wkv_ref, cos_ref, sin_ref,
             q_ref, k_ref, v_ref, qn_ref, kvn_ref, rq_ref, rkv_ref):
        qn, rq = _rms(cqa_ref[...], gq_ref[...])
        kvn, rkv = _rms(ckv_ref[...], gkv_ref[...])
        qn = qn.astype(MXU)
        kvn = kvn.astype(MXU)
        qn_ref[...] = qn
        kvn_ref[...] = kvn
        rq_ref[...] = rq
        rkv_ref[...] = rkv
        q = _dot_nt(qn, wq_ref[...])
        kv = _dot_nt(kvn, wkv_ref[...])
        cosv = cos_ref[...]
        sinv = sin_ref[...]
        lane = _iota((ts, LANE), 1)
        rope_lanes = (lane >= ROPE_LANE) & (lane < ROPE_LANE + QK_ROPE)
        kr = jnp.where(rope_lanes, pltpu.roll(tail_ref[...], ROPE_LANE, 1), 0.0)
        kr = kr * cosv + _rope_swap(kr) * sinv
        for h in range(nh):
            qh = q[:, LANE * h:LANE * (h + 1)]
            q_ref[h] = ((qh * cosv + _rope_swap(qh) * sinv) * ATT_SCALE).astype(MXU)
            k_ref[h] = (kv[:, LANE * h:LANE * (h + 1)] + kr).astype(MXU)
            v_ref[h] = kv[:, LANE * (nh + h):LANE * (nh + h + 1)].astype(MXU)

    head = pl.BlockSpec((nh, ts, LANE), lambda i: (0, i, 0))
    return pl.pallas_call(
        body, name="mla_prep_fwd", grid=(s // ts,),
        in_specs=[pl.BlockSpec((ts, Q_LORA), lambda i: (i, O_CQA // Q_LORA)),
                  pl.BlockSpec((ts, KV_LORA), lambda i: (i, O_CKV // KV_LORA)),
                  pl.BlockSpec((ts, LANE), lambda i: (i, O_TAIL // LANE)),
                  _full((1, Q_LORA)), _full((1, KV_LORA)), _full((nh * LANE, Q_LORA)), _full((2 * nh * LANE, KV_LORA)),
                  _row(ts, LANE), _row(ts, LANE)],
        out_specs=[head, head, head, _row(ts, Q_LORA), _row(ts, KV_LORA), _row(ts, 1), _row(ts, 1)],
        out_shape=[_sds((nh, s, LANE), MXU)] * 3 + [_sds((s, Q_LORA), MXU), _sds((s, KV_LORA), MXU), _sds((s, 1)), _sds((s, 1))],
        compiler_params=_params(("parallel",)),
    )(proj, proj, proj, gq, gkv, wq, wkv, cos, sin)


ATT_SCALE = (QK_NOPE + QK_ROPE) ** -0.5
NEG = -1e30


def _att_tile(s, most):
    return min(most, s // 2)


ATT_FWD_TILE = 1024
ATT_BWD_TILE = 512


def _attn_fwd(q, k, v):
    nh, s, _ = q.shape
    tq = _att_tile(s, ATT_FWD_TILE)
    nq = s // tq

    def body(q_ref, k_ref, v_ref, o_ref, lse_ref):
        i = pl.program_id(1)
        rowi = _iota((tq, tq), 0)
        coli = _iota((tq, tq), 1)
        zero = (jnp.full((tq, 1), NEG, F32), jnp.zeros((tq, 1), F32), jnp.zeros((tq, LANE), F32))
        state = [zero, zero]
        done = [zero, zero]
        for t in range(nq + 1):
            first = t <= i
            qblk = jnp.where(first, i, nq - 1 - i)
            kblk = jnp.where(first, t, t - i - 1)
            qoff = pl.multiple_of(qblk * tq, tq)
            koff = pl.multiple_of(kblk * tq, tq)
            keep = coli <= rowi + jnp.where(kblk == qblk, 0, tq)
            restart = t == i + 1
            for hh in range(2):
                m, lsum, acc = state[hh]
                if t > 0:
                    done[hh] = tuple(jnp.where(restart, a, b) for a, b in zip(state[hh], done[hh]))
                    m = jnp.where(restart, NEG, m)
                    lsum = jnp.where(restart, 0.0, lsum)
                    acc = jnp.where(restart, 0.0, acc)
                sc = _dot_nt(q_ref[hh, pl.ds(qoff, tq), :], k_ref[hh, pl.ds(koff, tq), :])
                sc = jnp.where(keep, sc, NEG)
                m_new = jnp.maximum(m, jnp.max(sc, axis=1, keepdims=True))
                p = jnp.exp(sc - m_new)
                alpha = jnp.exp(m - m_new)
                lsum = alpha * lsum + jnp.sum(p, axis=1, keepdims=True)
                acc = alpha * acc + _dot(p, v_ref[hh, pl.ds(koff, tq), :])
                state[hh] = (m_new, lsum, acc)
        for blk, res in ((i, done), (nq - 1 - i, state)):
            off = pl.multiple_of(blk * tq, tq)
            out = None
            for hh in range(2):
                m, lsum, acc = res[hh]
                o = acc * (1.0 / lsum)
                lse_ref[hh, pl.ds(off, tq), :] = m + jnp.log(lsum)
                out = o if hh == 0 else out + pltpu.roll(o, V_DIM, 1)
            o_ref[pl.ds(off, tq), :] = out

    pair = pl.BlockSpec((2, s, LANE), lambda j, i: (j, 0, 0))
    return pl.pallas_call(
        body, name="attn_fwd", grid=(nh // 2, nq // 2),
        in_specs=[pair, pair, pair],
        out_specs=[pl.BlockSpec((s, LANE), lambda j, i: (0, j)), pl.BlockSpec((2, s, 1), lambda j, i: (j, 0, 0))],
        out_shape=[_sds((s, D_MLA)), _sds((nh, s, 1))],
        compiler_params=_params(("parallel", "arbitrary")),
    )(q, k, v)


def _ssd_gate(y_ssd, s_z, g):
    yz = y_ssd * _silu(s_z)
    g0 = _iota(yz.shape, 1) < D_SSD // 2
    sq = yz * yz
    ms0 = jnp.sum(jnp.where(g0, sq, 0.0), axis=1, keepdims=True) / (D_SSD // 2)
    ms1 = jnp.sum(jnp.where(g0, 0.0, sq), axis=1, keepdims=True) / (D_SSD // 2)
    r = jnp.where(g0, lax.rsqrt(ms0 + SSD_NORM_EPS), lax.rsqrt(ms1 + SSD_NORM_EPS))
    nrm = yz * r
    return nrm * g, nrm, r, g0


def _outproj_fwd(x, proj, ya, y_ssd, o, g_ssd, w):
    s = x.shape[0]
    ts = _tile(s)

    def body(x_ref, sz_ref, cz_ref, ya_ref, ys_ref, o_ref, g_ref, w_ref, xo_ref, y_ref):
        yb = _ssd_gate(ys_ref[...], sz_ref[...], g_ref[...])[0]
        yc = o_ref[...] * _silu(cz_ref[...])
        y = jnp.concatenate([ya_ref[...], yb, yc], axis=1).astype(MXU)
        y_ref[...] = y
        xo_ref[...] = x_ref[...] + jnp.dot(y, w_ref[...], preferred_element_type=F32)

    return pl.pallas_call(
        body, name="outproj_fwd", grid=(s // ts,),
        in_specs=[_row(ts, D_MODEL), _gate_cols(ts, O_SZ), _gate_cols(ts, O_CZ), _row(ts, D_CONV_A), _row(ts, D_SSD),
                  _row(ts, D_MLA), _full((1, D_SSD)), _full((D_MODEL, D_MODEL))],
        out_specs=[_row(ts, D_MODEL), _row(ts, D_MODEL)],
        out_shape=[_sds((s, D_MODEL)), _sds((s, D_MODEL), MXU)],
        compiler_params=_params(("parallel",)),
    )(x, proj, proj, ya, y_ssd, o, g_ssd, w)


def _loss_head(x, g, tgt):
    s = x.shape[0]
    ts = _tile(s)

    def body(x_ref, g_ref, t_ref, dx_ref, dg_ref, loss_ref):
        @pl.when(pl.program_id(0) == 0)
        def _():
            dg_ref[...] = jnp.zeros_like(dg_ref)
            loss_ref[...] = jnp.zeros_like(loss_ref)

        xv = x_ref[...]
        gv = g_ref[...]
        yn, r = _rms(xv, gv)
        e = yn - t_ref[...]
        loss_ref[...] += jnp.sum(jnp.sum(e * e, axis=1, keepdims=True), axis=0, keepdims=True) * (0.5 / D_MODEL)
        dx, dg = _rms_bwd(e * (1.0 / D_MODEL), xv, r, gv)
        dx_ref[...] = dx
        dg_ref[...] += dg

    return pl.pallas_call(
        body, name="loss_head", grid=(s // ts,),
        in_specs=[_row(ts, D_MODEL), _full((1, D_MODEL)), _row(ts, D_MODEL)],
        out_specs=[_row(ts, D_MODEL), _full((1, D_MODEL)), _full((1, LANE))],
        out_shape=[_sds((s, D_MODEL)), _sds((1, D_MODEL)), _sds((1, LANE))],
        compiler_params=_params(("arbitrary",)),
    )(x, g, tgt)


def _outproj_bwd(dout, y, w, proj, y_ssd, o, g_ssd, dep=None):
    s = dout.shape[0]
    ts = _tile(s)

    def body(dout_ref, y_ref, w_ref, sz_ref, cz_ref, ys_ref, o_ref, g_ref,
             dya_ref, dys_ref, dsz_ref, dattn_ref, dcz_ref, dg_ref, dw_ref):
        @pl.when(pl.program_id(0) == 0)
        def _():
            dw_ref[...] = jnp.zeros_like(dw_ref)
            dg_ref[...] = jnp.zeros_like(dg_ref)

        dout_b = dout_ref[...].astype(MXU)
        dw_ref[...] += _dot_tn(y_ref[...], dout_b)
        dy = _dot_nt(dout_b, w_ref[...])
        dya_ref[...] = dy[:, :D_CONV_A]
        dyb = dy[:, D_CONV_A:D_CONV_A + D_SSD]
        sz = sz_ref[...]
        ys = ys_ref[...]
        gv = g_ref[...]
        _, nrm, r, g0 = _ssd_gate(ys, sz, gv)
        dg_ref[...] += jnp.sum(dyb * nrm, axis=0, keepdims=True)
        dn = dyb * gv
        t = dn * nrm
        mean = jnp.where(g0, jnp.sum(jnp.where(g0, t, 0.0), axis=1, keepdims=True),
                         jnp.sum(jnp.where(g0, 0.0, t), axis=1, keepdims=True)) / (D_SSD // 2)
        dyz = r * (dn - nrm * mean)
        dys_ref[...] = dyz * _silu(sz)
        dsz_ref[...] = (dyz * ys * _dsilu(sz)).astype(MXU)
        dyc = dy[:, D_CONV_A + D_SSD:]
        cz = cz_ref[...]
        dattn_ref[...] = dyc * _silu(cz)
        dcz_ref[...] = (dyc * o_ref[...] * _dsilu(cz)).astype(MXU)

    return _call_after(
        dep, body, (dout, y, w, proj, proj, y_ssd, o, g_ssd), name="outproj_bwd", grid=(s // ts,),
        in_specs=[_row(ts, D_MODEL), _row(ts, D_MODEL), _full((D_MODEL, D_MODEL)), _gate_cols(ts, O_SZ), _gate_cols(ts, O_CZ),
                  _row(ts, D_SSD), _row(ts, D_MLA), _full((1, D_SSD))],
        out_specs=[_row(ts, D_CONV_A), _row(ts, D_SSD), _row(ts, D_SSD), _row(ts, D_MLA), _row(ts, D_MLA),
                   _full((1, D_SSD)), _full((D_MODEL, D_MODEL))],
        out_shape=[_sds((s, D_CONV_A)), _sds((s, D_SSD)), _sds((s, D_SSD), MXU), _sds((s, D_MLA)), _sds((s, D_MLA), MXU),
                   _sds((1, D_SSD)), _sds((D_MODEL, D_MODEL))],
        compiler_params=_params(("arbitrary",)),
    )


def _attn_bwd(q, k, v, o, d_o, lse, dep=None):
    nh, s, _ = q.shape
    tq = _att_tile(s, ATT_BWD_TILE)
    nq = s // tq

    def body(q_ref, k_ref, v_ref, o_ref, do_ref, lse_ref, dq_ref, dk_ref, dv_ref, dop, delta):
        i = pl.program_id(1)

        @pl.when(i == 0)
        def _():
            lane = _iota((s, LANE), 1)
            for hh in range(2):
                dov = do_ref[...]
                ov = o_ref[...]
                if hh == 1:
                    dov = pltpu.roll(dov, V_DIM, 1)
                    ov = pltpu.roll(ov, V_DIM, 1)
                dov = jnp.where(lane < V_DIM, dov, 0.0)
                dop[hh] = dov.astype(MXU)
                delta[hh] = jnp.sum(dov * ov, axis=1, keepdims=True)
                dq_ref[hh] = jnp.zeros((s, LANE), F32)

        rowi = _iota((tq, tq), 0)
        coli = _iota((tq, tq), 1)
        z = jnp.zeros((tq, LANE), F32)
        state = [(z, z), (z, z)]
        done = [(z, z), (z, z)]
        for t in range(nq + 1):
            first = t <= nq - 1 - i
            kblk = jnp.where(first, i, nq - 1 - i)
            qblk = jnp.where(first, i + t, t - 1)
            qoff = pl.multiple_of(qblk * tq, tq)
            koff = pl.multiple_of(kblk * tq, tq)
            keep = coli <= rowi + jnp.where(kblk == qblk, 0, tq)
            restart = t == nq - i
            for hh in range(2):
                dk, dv = state[hh]
                if t > 0:
                    done[hh] = tuple(jnp.where(restart, a, b) for a, b in zip(state[hh], done[hh]))
                    dk = jnp.where(restart, 0.0, dk)
                    dv = jnp.where(restart, 0.0, dv)
                kb = k_ref[hh, pl.ds(koff, tq), :]
                qb = q_ref[hh, pl.ds(qoff, tq), :]
                dob = dop[hh, pl.ds(qoff, tq), :]
                sc = jnp.where(keep, _dot_nt(qb, kb), NEG)
                p = jnp.exp(sc - lse_ref[hh, pl.ds(qoff, tq), :])
                dp = _dot_nt(dob, v_ref[hh, pl.ds(koff, tq), :])
                ds = p * (dp - delta[hh, pl.ds(qoff, tq), :])
                dq_ref[hh, pl.ds(qoff, tq), :] += _dot(ds, kb)
                state[hh] = (dk + _dot_tn(ds, qb), dv + _dot_tn(p, dob))
        for blk, res in ((i, done), (nq - 1 - i, state)):
            off = pl.multiple_of(blk * tq, tq)
            for hh in range(2):
                dk_ref[hh, pl.ds(off, tq), :] = res[hh][0]
                dv_ref[hh, pl.ds(off, tq), :] = res[hh][1]

    pair = pl.BlockSpec((2, s, LANE), lambda j, i: (j, 0, 0))
    return _call_after(
        dep, body, (q, k, v, o, d_o, lse), name="attn_bwd", grid=(nh // 2, nq // 2),
        in_specs=[pair, pair, pair, pl.BlockSpec((s, LANE), lambda j, i: (0, j)), pl.BlockSpec((s, LANE), lambda j, i: (0, j)),
                  pl.BlockSpec((2, s, 1), lambda j, i: (j, 0, 0))],
        out_specs=[pair, pair, pair],
        out_shape=[_sds((nh, s, LANE))] * 3,
        scratch_shapes=[pltpu.VMEM((2, s, LANE), MXU), pltpu.VMEM((2, s, 1), F32)],
        compiler_params=_params(("parallel", "arbitrary")),
    )


def _ssd_bwd(xbc, proj, sc, states, dy, dep=None):
    s = xbc.shape[0]
    nc = s // SSD_CHUNK
    l = SSD_CHUNK
    cps = SSD_CHUNKS_PER_STEP

    def body(xbc_ref, tail_ref, sc_ref, st_ref, dy_ref, dxbc_ref, dtail_ref, dsc_ref, dstate):
        @pl.when(pl.program_id(0) == 0)
        def _():
            dstate[...] = jnp.zeros_like(dstate)
            dsc_ref[...] = jnp.zeros_like(dsc_ref)

        sc_v = sc_ref[...]
        lane1 = _iota((1, LANE), 1)
        rowp = _iota((LANE, 1), 0)
        rowl = _iota((l, 1), 0)
        d_row = sc_v[2:3, :]
        dstates = [dstate[j] for j in range(3)]
        for u in reversed(range(cps)):
            dstates = chunk(u, xbc_ref, tail_ref, sc_v, st_ref, dy_ref, dxbc_ref, dtail_ref, dsc_ref, dstates,
                            lane1, rowp, rowl, d_row)
        for j in range(3):
            dstate[j] = dstates[j]

    def chunk(u, xbc_ref, tail_ref, sc_v, st_ref, dy_ref, dxbc_ref, dtail_ref, dsc_ref, dstates, lane1, rowp, rowl, d_row):
        r = slice(u * l, (u + 1) * l)
        dstates = list(dstates)
        lane, row, tri, a_row, pre, dt, a_cs, a_t = _ssd_chunk_common(tail_ref[r, :], sc_v)
        da_col = jnp.zeros((l, LANE), F32)
        da_row = jnp.zeros((LANE, l), F32)
        dt_x = jnp.zeros((l, LANE), F32)
        dd_row = jnp.zeros((1, LANE), F32)
        db = [jnp.zeros((l, LANE), F32), jnp.zeros((l, LANE), F32)]
        dc = [jnp.zeros((l, LANE), F32), jnp.zeros((l, LANE), F32)]
        for j in range(3):
            xpair = xbc_ref[r, LANE * j:LANE * (j + 1)]
            dypair = dy_ref[r, LANE * j:LANE * (j + 1)]
            sp = st_ref[u, j]
            dsp = dstates[j]
            dxpair = jnp.zeros((l, LANE), F32)
            ds_new = jnp.zeros((LANE, LANE), F32)
            decay = jnp.zeros((LANE, 1), F32)
            for half in range(2):
                h = 2 * j + half
                g = h // 3
                hm = (lane < 64) if half == 0 else (lane >= 64)
                hrow = (rowp < 64) if half == 0 else (rowp >= 64)
                ac = _pick_col(a_cs, lane, DT_LANE + h)
                ar = _pick_row(a_t, row, DT_LANE + h)
                dtc = _pick_col(dt, lane, DT_LANE + h)
                alast = jnp.sum(jnp.where(lane1 == l - 1, ar, 0.0), axis=1, keepdims=True)
                dh = jnp.sum(jnp.where(lane1 == DT_LANE + h, d_row, 0.0), axis=1, keepdims=True)
                xm = jnp.where(hm, xpair, 0.0)
                xd = xm * dtc
                dym = jnp.where(hm, dypair, 0.0)
                bm = xbc_ref[r, D_SSD + LANE * g:D_SSD + LANE * (g + 1)]
                cm = xbc_ref[r, D_SSD + SSD_BC + LANE * g:D_SSD + SSD_BC + LANE * (g + 1)]
                lm = jnp.where(row >= lane, jnp.exp(jnp.minimum(ac - ar, 0.0)), 0.0)
                e_in = jnp.exp(ac)
                f_out = jnp.exp(alast - ac)
                e_last = jnp.exp(alast)
                m = _dot_nt(cm, bm) * lm
                y_off = jnp.where(hm, _dot_nt(cm, sp), 0.0) * e_in
                dm = _dot_nt(dym, xd)
                dxd = _dot_tn(m, dym)
                dg = dm * lm
                dye = dym * e_in
                dc[g] = dc[g] + _dot(dg, bm) + _dot(dye, sp)
                db[g] = db[g] + _dot_tn(dg, cm)
                qm = dm * m
                dac = jnp.sum(qm, axis=1, keepdims=True) + jnp.sum(dym * y_off, axis=1, keepdims=True)
                dar = -jnp.sum(qm, axis=0, keepdims=True)
                dxf = jnp.where(hm, _dot_nt(bm, dsp), 0.0)
                db[g] = db[g] + _dot(xd * f_out, dsp)
                dxd = dxd + dxf * f_out
                df = jnp.sum(dxf * xd, axis=1, keepdims=True) * f_out
                dac = dac - df
                s_last = jnp.sum(df, axis=0, keepdims=True)
                ss = jnp.sum(jnp.where(hrow, dsp * sp, 0.0), axis=1, keepdims=True)
                s_last = s_last + e_last * jnp.sum(ss, axis=0, keepdims=True)
                dac = dac + jnp.where(rowl == l - 1, s_last, 0.0)
                ds_new = ds_new + _dot_tn(dye, cm)
                decay = jnp.where(hrow, e_last, decay)
                dxpair = dxpair + dxd * dtc + dym * dh
                dt_x = dt_x + jnp.where(lane == DT_LANE + h, jnp.sum(dxd * xm, axis=1, keepdims=True), 0.0)
                dsum = jnp.sum(jnp.sum(dym * xm, axis=1, keepdims=True), axis=0, keepdims=True)
                dd_row = dd_row + jnp.where(lane1 == DT_LANE + h, dsum, 0.0)
                da_col = da_col + jnp.where(lane == DT_LANE + h, dac, 0.0)
                da_row = da_row + jnp.where(row == DT_LANE + h, dar, 0.0)
            dstates[j] = dsp * decay + ds_new
            dxbc_ref[r, LANE * j:LANE * (j + 1)] = dxpair
        for g in range(2):
            dxbc_ref[r, D_SSD + LANE * g:D_SSD + LANE * (g + 1)] = db[g]
            dxbc_ref[r, D_SSD + SSD_BC + LANE * g:D_SSD + SSD_BC + LANE * (g + 1)] = dc[g]
        dla = _dot_hi_tn(tri, da_col + da_row.T)
        ddt = dt_x + dla * a_row
        dpre = ddt * _sigmoid(pre)
        dtm = (lane >= DT_LANE) & (lane < DT_LANE + SSD_HEADS)
        dtail_ref[r, :] = jnp.where(dtm, dpre, 0.0).astype(MXU)
        dtm1 = (lane1 >= DT_LANE) & (lane1 < DT_LANE + SSD_HEADS)
        dsc_ref[0:1, :] += jnp.where(dtm1, jnp.sum(dpre, axis=0, keepdims=True), 0.0)
        dsc_ref[1:2, :] += jnp.where(dtm1, jnp.sum(dla * dt, axis=0, keepdims=True) * a_row, 0.0)
        dsc_ref[2:3, :] += dd_row
        return dstates

    rev = lambda c: nc // cps - 1 - c
    return _call_after(
        dep, body, (xbc, proj, sc, states, dy), name="ssd_bwd", grid=(nc // cps,),
        in_specs=[pl.BlockSpec((cps * l, N_XBC), lambda c: (rev(c), 0)),
                  pl.BlockSpec((cps * l, LANE), lambda c: (rev(c), O_TAIL // LANE)), _full((8, LANE)),
                  pl.BlockSpec((cps, 3, LANE, LANE), lambda c: (rev(c), 0, 0, 0)),
                  pl.BlockSpec((cps * l, D_SSD), lambda c: (rev(c), 0))],
        out_specs=[pl.BlockSpec((cps * l, N_XBC), lambda c: (rev(c), 0)), pl.BlockSpec((cps * l, LANE), lambda c: (rev(c), 0)),
                   _full((8, LANE))],
        out_shape=[_sds((s, N_XBC)), _sds((s, LANE), MXU), _sds((8, LANE))],
        scratch_shapes=[pltpu.VMEM((3, LANE, LANE), F32)],
        compiler_params=_params(("arbitrary",)),
    )


def _sconv_bwd(proj, w, b, dxbc, dep=None):
    s = proj.shape[0]

    def body(u_ref, w_ref, b_ref, d_ref, du_ref, dw_ref, db_ref):
        u = u_ref[...]
        wv = w_ref[...]
        dpre = d_ref[...] * _dsilu(_sconv_pre(u, wv, b_ref[...]))
        du_ref[...] = (wv[3:4, :] * dpre + wv[2:3, :] * _shift_up(dpre, 1) + wv[1:2, :] * _shift_up(dpre, 2)
                       + wv[0:1, :] * _shift_up(dpre, 3)).astype(MXU)
        for k in range(4):
            dw_ref[k:k + 1, :] = jnp.sum(dpre * _shift_down(u, 3 - k), axis=0, keepdims=True)
        db_ref[...] = jnp.sum(dpre, axis=0, keepdims=True)

    blk = pl.BlockSpec((s, LANE), lambda j: (0, j))
    return _call_after(
        dep, body, (proj, w, b, dxbc), name="sconv_bwd", grid=(N_XBC // LANE,),
        in_specs=[_col(s, O_XBC), pl.BlockSpec((4, LANE), lambda j: (0, j)), pl.BlockSpec((1, LANE), lambda j: (0, j)), blk],
        out_specs=[blk, pl.BlockSpec((4, LANE), lambda j: (0, j)), pl.BlockSpec((1, LANE), lambda j: (0, j))],
        out_shape=[_sds((s, N_XBC), MXU), _sds((4, N_XBC)), _sds((1, N_XBC))],
        compiler_params=_params(("parallel",)),
    )


def _conva_bwd(proj, w, dya, dep=None):
    s = proj.shape[0]

    def body(h_ref, b_ref, c_ref, z_ref, w_ref, d_ref, da_ref, dw_ref):
        ah, ab, acv, az = h_ref[...], b_ref[...], c_ref[...], z_ref[...]
        wv = w_ref[...]
        u = acv * ah
        cv = wv[2:3, :] * u + wv[1:2, :] * _shift_down(u, 1) + wv[0:1, :] * _shift_down(u, 2)
        dy = d_ref[...]
        sz = _silu(az)
        da_ref[1] = (dy * cv * sz).astype(MXU)
        da_ref[3] = (dy * ab * cv * _dsilu(az)).astype(MXU)
        dcv = dy * ab * sz
        du = wv[2:3, :] * dcv + wv[1:2, :] * _shift_up(dcv, 1) + wv[0:1, :] * _shift_up(dcv, 2)
        da_ref[0] = (du * acv).astype(MXU)
        da_ref[2] = (du * ah).astype(MXU)
        for k in range(3):
            dw_ref[k:k + 1, :] = jnp.sum(dcv * _shift_down(u, 2 - k), axis=0, keepdims=True)

    return _call_after(
        dep, body, (proj, proj, proj, proj, w, dya), name="conva_bwd", grid=(D_CONV_A // LANE,),
        in_specs=[_col(s, O_AH), _col(s, O_AB), _col(s, O_AC), _col(s, O_AZ), pl.BlockSpec((3, LANE), lambda j: (0, j)),
                  pl.BlockSpec((s, LANE), lambda j: (0, j))],
        out_specs=[pl.BlockSpec((4, s, LANE), lambda j: (0, 0, j)), pl.BlockSpec((3, LANE), lambda j: (0, j))],
        out_shape=[_sds((4, s, D_CONV_A), MXU), _sds((3, D_CONV_A))],
        compiler_params=_params(("parallel",)),
    )


def _mla_prep_bwd(dq, dk, dv, proj, qn, kvn, rq, rkv, gq, gkv, wq, wkv, cos, sin):
    s = proj.shape[0]
    ts = _tile(s)
    nh = MLA_HEADS

    def body(dq_ref, dk_ref, dv_ref, cqa_ref, ckv_ref, qn_ref, kvn_ref, rq_ref, rkv_ref, gq_ref, gkv_ref,
             wq_ref, wkv_ref, cos_ref, sin_ref, dcqa_ref, dckv_ref, dtail_ref, dwq_ref, dwkv_ref, dgq_ref, dgkv_ref):
        @pl.when(pl.program_id(0) == 0)
        def _():
            dwq_ref[...] = jnp.zeros_like(dwq_ref)
            dwkv_ref[...] = jnp.zeros_like(dwkv_ref)
            dgq_ref[...] = jnp.zeros_like(dgq_ref)
            dgkv_ref[...] = jnp.zeros_like(dgkv_ref)

        cosv = cos_ref[...]
        sinv = sin_ref[...]
        lane = _iota((ts, LANE), 1)
        rope_lanes = (lane >= ROPE_LANE) & (lane < ROPE_LANE + QK_ROPE)

        def unrope(gr):
            return gr * cosv + _rope_swap(gr * sinv)

        dqs, dks, dvs = [], [], []
        dkr = jnp.zeros((ts, LANE), F32)
        for h in range(nh):
            dqs.append(unrope(dq_ref[h] * ATT_SCALE).astype(MXU))
            dkh = dk_ref[h]
            dks.append(jnp.where(lane < QK_NOPE, dkh, 0.0).astype(MXU))
            dkr = dkr + jnp.where(rope_lanes, dkh, 0.0)
            dvs.append(dv_ref[h].astype(MXU))
        dtail_ref[...] = pltpu.roll(jnp.where(rope_lanes, unrope(dkr), 0.0), ROPE_LANE, 1).astype(MXU)
        dq_all = jnp.concatenate(dqs, axis=1)
        dkv_all = jnp.concatenate(dks + dvs, axis=1)
        dwq_ref[...] += _dot_tn(dq_all, qn_ref[...])
        dwkv_ref[...] += _dot_tn(dkv_all, kvn_ref[...])
        dcqa, dgq = _rms_bwd(_dot(dq_all, wq_ref[...]), cqa_ref[...], rq_ref[...], gq_ref[...])
        dckv, dgkv = _rms_bwd(_dot(dkv_all, wkv_ref[...]), ckv_ref[...], rkv_ref[...], gkv_ref[...])
        dcqa_ref[...] = dcqa.astype(MXU)
        dckv_ref[...] = dckv.astype(MXU)
        dgq_ref[...] += dgq
        dgkv_ref[...] += dgkv

    head = pl.BlockSpec((nh, ts, LANE), lambda i: (0, i, 0))
    return pl.pallas_call(
        body, name="mla_prep_bwd", grid=(s // ts,),
        in_specs=[head, head, head,
                  pl.BlockSpec((ts, Q_LORA), lambda i: (i, O_CQA // Q_LORA)),
                  pl.BlockSpec((ts, KV_LORA), lambda i: (i, O_CKV // KV_LORA)),
                  _row(ts, Q_LORA), _row(ts, KV_LORA), _row(ts, 1), _row(ts, 1),
                  _full((1, Q_LORA)), _full((1, KV_LORA)), _full((nh * LANE, Q_LORA)), _full((2 * nh * LANE, KV_LORA)),
                  _row(ts, LANE), _row(ts, LANE)],
        out_specs=[_row(ts, Q_LORA), _row(ts, KV_LORA), _row(ts, LANE), _full((nh * LANE, Q_LORA)),
                   _full((2 * nh * LANE, KV_LORA)), _full((1, Q_LORA)), _full((1, KV_LORA))],
        out_shape=[_sds((s, Q_LORA), MXU), _sds((s, KV_LORA), MXU), _sds((s, LANE), MXU), _sds((nh * LANE, Q_LORA)),
                   _sds((2 * nh * LANE, KV_LORA)), _sds((1, Q_LORA)), _sds((1, KV_LORA))],
        compiler_params=_params(("arbitrary",)),
    )(dq, dk, dv, proj, proj, qn, kvn, rq, rkv, gq, gkv, wq, wkv, cos, sin)


def _inproj_bwd(da4, dsz, dxbc_in, dcqa, dckv, dcz, dtail_a, dtail_b, w, x, rstd, g, dout, dep=None):
    s = x.shape[0]
    ts = _tile(s)

    def body(da_ref, dsz_ref, dxbc_ref, dcqa_ref, dckv_ref, dcz_ref, dta_ref, dtb_ref, w_ref, x_ref, r_ref, g_ref, dout_ref,
             dproj_ref, dx_ref, dg_ref):
        @pl.when(pl.program_id(0) == 0)
        def _():
            dg_ref[...] = jnp.zeros_like(dg_ref)

        dproj = jnp.concatenate(
            [da_ref[0], da_ref[1], da_ref[2], da_ref[3], dxbc_ref[...], dsz_ref[...], dcqa_ref[...], dckv_ref[...],
             dcz_ref[...], dta_ref[...] + dtb_ref[...]], axis=1)
        dproj_ref[...] = dproj
        dh = _dot_nt(dproj, w_ref[...])
        dx, dg = _rms_bwd(dh, x_ref[...], r_ref[...], g_ref[...])
        dx_ref[...] = dout_ref[...] + dx
        dg_ref[...] += dg

    return _call_after(
        dep, body, (da4, dsz, dxbc_in, dcqa, dckv, dcz, dtail_a, dtail_b, w, x, rstd, g, dout), name="inproj_bwd", grid=(s // ts,),
        in_specs=[pl.BlockSpec((4, ts, D_CONV_A), lambda i: (0, i, 0)), _row(ts, D_SSD), _row(ts, N_XBC), _row(ts, Q_LORA),
                  _row(ts, KV_LORA), _row(ts, D_MLA), _row(ts, LANE), _row(ts, LANE), _full((D_MODEL, NCOL)),
                  _row(ts, D_MODEL), _row(ts, 1), _full((1, D_MODEL)), _row(ts, D_MODEL)],
        out_specs=[_row(ts, NCOL), _row(ts, D_MODEL), _full((1, D_MODEL))],
        out_shape=[_sds((s, NCOL), MXU), _sds((s, D_MODEL)), _sds((1, D_MODEL))],
        compiler_params=_params(("arbitrary",)),
    )


DWIN_BLOCK = 640


def _dwin(h, dproj, dep=None):
    s = h.shape[0]

    def body(h_ref, d_ref, o_ref):
        o_ref[...] = _dot_tn(h_ref[...], d_ref[...])

    return _call_after(
        dep, body, (h, dproj), name="dwin", grid=(NCOL // DWIN_BLOCK,),
        in_specs=[_full((s, D_MODEL)), pl.BlockSpec((s, DWIN_BLOCK), lambda j: (0, j))],
        out_specs=pl.BlockSpec((D_MODEL, DWIN_BLOCK), lambda j: (0, j)),
        out_shape=_sds((D_MODEL, NCOL)),
        compiler_params=_params(("parallel",)),
    )


def _adamw(ws, gs, ms, vs, whole, layer=None, into=None):
    n = len(ws)
    bc1 = 1.0 - ADAM_B1 ** ADAM_STEP
    bc2 = 1.0 - ADAM_B2 ** ADAM_STEP

    def body(*refs):
        ins, outs = refs[:4 * n], refs[4 * n:]
        for a in range(n):
            w_ref, g_ref, m_ref, v_ref = ins[a], ins[n + a], ins[2 * n + a], ins[3 * n + a]
            gv = g_ref[...]
            mn = ADAM_B1 * m_ref[...] + (1.0 - ADAM_B1) * gv
            vn = ADAM_B2 * v_ref[...] + (1.0 - ADAM_B2) * (gv * gv)
            outs[n + a][...] = mn
            outs[2 * n + a][...] = vn
            outs[a][...] = -ADAM_LR * ((mn / bc1) / (jnp.sqrt(vn / bc2) + ADAM_EPS) + ADAM_WD * w_ref[...])

    if whole:
        grid, blks = (1,), [pl.BlockSpec(w.shape, lambda i, _n=w.ndim: (0,) * _n) for w in ws]
    elif layer is not None:
        assert n == 1
        rows, cols = ws[0].shape[1:]
        grid = (2,)
        blk = pl.BlockSpec((1, rows // 2, cols), lambda k: (layer, k, 0))
        gblk = pl.BlockSpec((1, rows // 2, cols), lambda k: (0, k, 0))

        def body1(w_ref, g_ref, m_ref, v_ref, *rest):
            go_ref, d_ref, mo_ref, vo_ref = rest[-4:]
            go_ref[...] = g_ref[...]
            body(w_ref, g_ref, m_ref, v_ref, d_ref, mo_ref, vo_ref)

        extra = list(into) if into is not None else []
        out = pl.pallas_call(
            body1, name=f"adamw_layer{layer}", grid=grid,
            in_specs=[blk, gblk, blk, blk] + [ANY] * len(extra), out_specs=[blk] * 4, out_shape=[_sds(ws[0].shape)] * 4,
            input_output_aliases={4 + i: i for i in range(len(extra))},
            compiler_params=_params(("parallel",)),
        )(ws[0], gs[0][None], ms[0], vs[0], *extra)
        return list(out)
    else:
        grid = (ws[0].shape[0], 2)
        blks = [pl.BlockSpec((1, w.shape[1] // 2, w.shape[2]), lambda i, k: (i, k, 0)) for w in ws]
    out = pl.pallas_call(
        body, name="adamw", grid=grid,
        in_specs=blks * 4, out_specs=blks * 3, out_shape=[_sds(w.shape) for w in ws] * 3,
        compiler_params=_params(("parallel",) * len(grid)),
    )(*ws, *gs, *ms, *vs)
    return [(out[a], out[n + a], out[2 * n + a]) for a in range(n)]


COL_MOVES = ((0, 0, 1024), (1024, O_SZ, 384), (1408, O_XBC, 896), (2304, O_TAIL + DT_LANE, 6), (2310, O_CQA, 256),
             (2566, O_CKV, 128), (2694, O_TAIL, 32), (2726, O_CZ, 384))


def _move_cols(w, moves, width):
    out = None
    for src, dst, n in moves:
        piece = jnp.pad(w[..., src:src + n], [(0, 0)] * (w.ndim - 1) + [(dst, width - dst - n)])
        out = piece if out is None else out + piece
    return out


def _perm_cols(w):
    return _move_cols(w, COL_MOVES, NCOL)


def _unperm_cols(g):
    return _move_cols(g, [(dst, src, n) for src, dst, n in COL_MOVES], IN_COLS)


def _wq_layout(wt):
    return jnp.pad(wt.reshape(MLA_HEADS, QK_NOPE + QK_ROPE, Q_LORA), ((0, 0), (0, 32), (0, 0))).reshape(MLA_HEADS * LANE, Q_LORA)


def _wq_unlayout(g):
    return g.reshape(MLA_HEADS, LANE, Q_LORA)[:, :QK_NOPE + QK_ROPE].reshape(MLA_HEADS * (QK_NOPE + QK_ROPE), Q_LORA)


def _wkv_layout(wt):
    t = wt.reshape(MLA_HEADS, 2, 64, KV_LORA).transpose(1, 0, 2, 3)
    return jnp.pad(t, ((0, 0), (0, 0), (0, 64), (0, 0))).reshape(2 * MLA_HEADS * LANE, KV_LORA)


def _wkv_unlayout(g):
    t = g.reshape(2, MLA_HEADS, LANE, KV_LORA)[:, :, :64]
    return t.transpose(1, 0, 2, 3).reshape(MLA_HEADS * LANE, KV_LORA)


def _rope_tables(positions):
    inv_freq = ROPE_BASE ** (-jnp.arange(0, QK_ROPE, 2, dtype=F32) / QK_ROPE)
    ang = positions.astype(F32)[:, None] * inv_freq
    cos, sin = jnp.cos(ang), jnp.sin(ang)
    s = positions.shape[0]
    one, zero = jnp.ones((s, ROPE_LANE), F32), jnp.zeros((s, ROPE_LANE), F32)
    cos_t = jnp.concatenate([one, cos, cos, one[:, :32]], axis=1)
    sin_t = jnp.concatenate([zero, -sin, sin, zero[:, :32]], axis=1)
    return cos_t, sin_t


def _ssd_scalars(dt_bias, a_log, d_skip):
    return jnp.pad(jnp.stack([dt_bias, a_log, d_skip]), ((0, 5), (DT_LANE, LANE - DT_LANE - SSD_HEADS)))


def _layer_fwd(x, lw, cos, sin, dep=None, late=None):
    proj, h, rstd = _inproj_fwd(x, lw["norm_g"], lw["w_in"], dep)
    ya = _conva_fwd(proj, lw["conv_a_w"])
    xbc = _sconv_fwd(proj, lw["ssd_conv_w"], lw["ssd_conv_b"])
    y_ssd, states = _ssd_fwd(xbc, proj, lw["sc"])
    if late is not None:
        lw = {**lw, **late(ya, y_ssd)}
    q, k, v, qn, kvn, rq, rkv = _mla_prep_fwd(proj, lw["gq"], lw["gkv"], lw["wq"], lw["wkv"], cos, sin)
    o, lse = _attn_fwd(q, k, v)
    x_out, y = _outproj_fwd(x, proj, ya, y_ssd, o, lw["g_ssd"], lw["w_out"])
    saved = dict(x=x, proj=proj, h=h, rstd=rstd, xbc=xbc, y_ssd=y_ssd, states=states, q=q, k=k, v=v, qn=qn, kvn=kvn,
                 rq=rq, rkv=rkv, o=o, lse=lse, y=y)
    return x_out, saved, lw


def _layer_bwd(dout, lw, sv, cos, sin, rs=None, begin_early=None):
    tok = lambda: None if rs is None else rs["h"]["token"]
    dya, dys, dsz, d_o, dcz, dg_ssd, dw_out = _outproj_bwd(dout, sv["y"], lw["w_out"], sv["proj"], sv["y_ssd"], sv["o"],
                                                            lw["g_ssd"], tok())
    if rs is not None:
        rs = _rs_add_mine(rs, [dya])
    dq, dk, dv = _attn_bwd(sv["q"], sv["k"], sv["v"], sv["o"], d_o, sv["lse"], tok())
    dxbc, dtail_s, dsc = _ssd_bwd(sv["xbc"], sv["proj"], lw["sc"], sv["states"], dys, tok())
    da4, dw_conva = _conva_bwd(sv["proj"], lw["conv_a_w"], dya, tok())
    if rs is not None:
        rs = _rs_add_chips(rs, [dq, dxbc, da4])
    du, dw_sconv, db_sconv = _sconv_bwd(sv["proj"], lw["ssd_conv_w"], lw["ssd_conv_b"], dxbc, tok())
    dcqa, dckv, dtail_m, dwq, dwkv, dgq, dgkv = _mla_prep_bwd(
        dq, dk, dv, sv["proj"], sv["qn"], sv["kvn"], sv["rq"], sv["rkv"], lw["gq"], lw["gkv"], lw["wq"], lw["wkv"], cos, sin)
    early = None if begin_early is None else begin_early(dw_out, dwq, dwkv)
    etok = lambda: None if early is None else early["h"]["token"]
    dproj, dx, dg = _inproj_bwd(da4, dsz, du, dcqa, dckv, dcz, dtail_s, dtail_m, lw["w_in"], sv["x"], sv["rstd"],
                                lw["norm_g"], dout, etok())
    reduced = None if rs is None else _rs_end(rs, [du, dcqa, dx])
    if early is not None:
        early = _rs_add_mine(early, [dx])
    dw_in = _dwin(sv["h"], dproj, etok())
    if early is not None:
        early = _rs_add_chips(early, [dw_in])
    grads = dict(norm_g=dg, w_in=dw_in, conv_a_w=dw_conva, ssd_conv_w=dw_sconv, ssd_conv_b=db_sconv, sc=dsc,
                 g_ssd=dg_ssd, gq=dgq, wq=dwq, gkv=dgkv, wkv=dwkv, w_out=dw_out)
    return dx, grads, reduced, early


ANY = pl.BlockSpec(memory_space=pl.ANY)
N_CHIPS = 4
N_DEV = 8


def _place():
    return lax.axis_index("x"), lax.axis_index("y"), lax.axis_index("c")


HBM_SPEC = pl.BlockSpec(memory_space=pltpu.HBM)
SEM_SPEC = pl.BlockSpec(memory_space=pltpu.SEMAPHORE)
PAYLOAD = jnp.bfloat16


def _hbm(a):
    return pltpu.with_memory_space_constraint(a, pltpu.HBM)


def _run_plan(plan, srcs, lands, send_sems, recv_sems, start, wait):
    copies = plan(srcs, lands)
    if start:
        for i, (src, dst, _, to) in enumerate(copies):
            pltpu.make_async_remote_copy(src_ref=src, dst_ref=dst, send_sem=send_sems.at[i], recv_sem=recv_sems.at[i],
                                         device_id=to, device_id_type=MESH_T).start()
    if wait:
        for i, (src, _, arrives, to) in enumerate(copies):
            cp = pltpu.make_async_remote_copy(src_ref=src, dst_ref=arrives, send_sem=send_sems.at[i],
                                              recv_sem=recv_sems.at[i], device_id=to, device_id_type=MESH_T)
            cp.wait_send()
            cp.wait_recv()


def _exchange_fused(name, plan, n_copies, srcs, land_shapes):
    ns, nl = len(srcs), len(land_shapes)

    def body(*refs):
        _run_plan(plan, refs[:ns], refs[ns:ns + nl], refs[ns + nl], refs[ns + nl + 1], True, True)

    return pl.pallas_call(
        body, name=name, in_specs=[ANY] * ns, out_specs=[ANY] * nl, out_shape=list(land_shapes),
        scratch_shapes=[pltpu.SemaphoreType.DMA((n_copies,)), pltpu.SemaphoreType.DMA((n_copies,))],
    )(*srcs)


def _exchange_start(name, plan, n_copies, srcs, land_shapes, deps):
    ns, nl = len(srcs), len(land_shapes)
    n_in = ns + nl + len(deps)

    def body(*refs):
        send_sems, recv_sems = refs[n_in], refs[n_in + 1]
        token = refs[-1]
        _run_plan(plan, refs[:ns], refs[ns:ns + nl], send_sems, recv_sems, True, False)
        token[...] = jnp.zeros_like(token)

    thru = [pltpu.HBM(a.shape, a.dtype) for a in srcs] + [pltpu.HBM(a.shape, a.dtype) for a in land_shapes]
    outs = pl.pallas_call(
        body, name=name,
        out_shape=(pltpu.SemaphoreType.DMA((n_copies,)), pltpu.SemaphoreType.DMA((n_copies,)), *thru, _sds((8, LANE))),
        in_specs=[HBM_SPEC] * (ns + nl) + [ANY] * len(deps),
        out_specs=(SEM_SPEC, SEM_SPEC, *[HBM_SPEC] * (ns + nl), pl.BlockSpec(memory_space=pltpu.VMEM)),
        input_output_aliases={i: 2 + i for i in range(ns + nl)},
        compiler_params=pltpu.CompilerParams(has_side_effects=pltpu.SideEffectType.DATAFLOW_SIDE_EFFECTING),
    )(*[_hbm(a) for a in srcs], *[_hbm(lax.empty(a.shape, a.dtype)) for a in land_shapes], *deps)
    return (outs[0], outs[1]), list(outs[2:2 + ns]), list(outs[2 + ns:2 + ns + nl]), outs[-1]


def _exchange_wait(name, plan, sems, srcs, lands, after):
    ns, nl = len(srcs), len(lands)

    def body(*refs):
        _run_plan(plan, refs[:ns], refs[ns:ns + nl], refs[ns + nl], refs[ns + nl + 1], False, True)

    outs = pl.pallas_call(
        body, name=name,
        out_shape=[pltpu.HBM(a.shape, a.dtype) for a in list(srcs) + list(lands)],
        in_specs=[HBM_SPEC] * (ns + nl) + [SEM_SPEC, SEM_SPEC] + [ANY] * len(after), out_specs=[HBM_SPEC] * (ns + nl),
        input_output_aliases={i: i for i in range(ns + nl)},
        compiler_params=pltpu.CompilerParams(has_side_effects=pltpu.SideEffectType.DATAFLOW_SIDE_EFFECTING),
    )(*srcs, *lands, sems[0], sems[1], *after)
    return list(outs[:ns]), list(outs[ns:])


def _xchg_begin(name, plan, n_copies, srcs, land_shapes, split, deps=()):
    if not split:
        return dict(split=False, srcs=list(srcs), lands=_exchange_fused(name, plan, n_copies, srcs, land_shapes),
                    token=jnp.zeros((8, LANE), F32))
    sems, srcs_t, lands_t, token = _exchange_start(name + "_start", plan, n_copies, srcs, land_shapes, list(deps))
    return dict(split=True, name=name, plan=plan, sems=sems, srcs=srcs_t, lands=lands_t, token=token)


def _xchg_end(h, after):
    if not h["split"]:
        return h["srcs"], h["lands"]
    return _exchange_wait(h["name"] + "_wait", h["plan"], h["sems"], h["srcs"], h["lands"], after)


def _other_chips():
    x, y, c = _place()
    return [(1 - x, y), (x, 1 - y), (1 - x, 1 - y)]


def _gather_plan(srcs, lands):
    x, y, c = _place()
    me = 2 * x + y
    return [(srcs[a], lands[a].at[me], lands[a].at[2 * cx + cy], (cx, cy, c))
            for (cx, cy) in _other_chips() for a in range(len(srcs))]


def _gather_begin(shards, split, tag, deps=()):
    shapes = [_sds((N_CHIPS,) + a.shape, a.dtype) for a in shards]
    return _xchg_begin(f"gather_{tag}", _gather_plan, 3 * len(shards), shards, shapes, split, deps)


def _gather_end(h, after):
    shards, lands = _xchg_end(h, after)
    me = 2 * lax.axis_index("x") + lax.axis_index("y")
    return [lax.dynamic_update_index_in_dim(g, s, me, 0) for g, s in zip(lands, shards)]


def _swap_plan(srcs, lands):
    x, y, c = _place()
    return [(srcs[a].at[:, 1 - c], lands[a], lands[a], (x, y, 1 - c)) for a in range(len(srcs))]


def _chips_plan(srcs, lands):
    x, y, c = _place()
    me = 2 * x + y
    return [(srcs[a].at[2 * cx + cy], lands[a].at[me], lands[a].at[2 * cx + cy], (cx, cy, c))
            for (cx, cy) in _other_chips() for a in range(len(srcs))]


def _share_plan(srcs, lands):
    x, y, c = _place()
    return [(srcs[a], lands[a].at[c], lands[a].at[1 - c], (x, y, 1 - c)) for a in range(len(srcs))]


def _allreduce_small(slab, dep=None):
    r = slab.shape[0]

    def body(s_ref, o_ref, gath, send_sems, recv_sems):
        x, y, c = _place()
        me = 4 * x + 2 * y + c
        gath[me] = s_ref[...]
        cps = []
        for rel in range(1, N_DEV):
            px = 1 - x if rel & 4 else x
            py = 1 - y if rel & 2 else y
            pc = 1 - c if rel & 1 else c
            cp = pltpu.make_async_remote_copy(src_ref=s_ref, dst_ref=gath.at[me], send_sem=send_sems.at[rel - 1],
                                              recv_sem=recv_sems.at[rel - 1], device_id=(px, py, pc), device_id_type=MESH_T)
            cp.start()
            cps.append(cp)
        for cp in cps:
            cp.wait()
        acc = gath[0]
        for d in range(1, N_DEV):
            acc = acc + gath[d]
        o_ref[...] = acc

    vm = pl.BlockSpec(memory_space=pltpu.VMEM)
    return _call_after(
        dep, body, (slab,), name="allreduce_small", in_specs=[vm], out_specs=vm, out_shape=_sds((r, LANE)),
        scratch_shapes=[pltpu.VMEM((N_DEV, r, LANE), F32), pltpu.SemaphoreType.DMA((N_DEV - 1,)),
                        pltpu.SemaphoreType.DMA((N_DEV - 1,))],
    )


def _add_mine(g4s, recvs, half):
    n = len(g4s)

    def body(h_ref, *refs):
        for g_ref, r_ref, o_ref in zip(refs[:n], refs[n:2 * n], refs[2 * n:]):
            o_ref[0] = (g_ref[0, 0] + r_ref[0]).astype(o_ref.dtype)

    dims = [g.shape[2:] for g in g4s]
    return pl.pallas_call(
        body, name="add_mine",
        grid_spec=pltpu.PrefetchScalarGridSpec(
            num_scalar_prefetch=1, grid=(N_CHIPS,),
            in_specs=[pl.BlockSpec((1, 1) + d, lambda j, h: (j, h[0], 0, 0)) for d in dims]
            + [pl.BlockSpec((1,) + d, lambda j, h: (j, 0, 0)) for d in dims],
            out_specs=[pl.BlockSpec((1,) + d, lambda j, h: (j, 0, 0)) for d in dims]),
        out_shape=[_sds((N_CHIPS,) + d, PAYLOAD) for d in dims],
        compiler_params=_params(("parallel",)),
    )(half, *g4s, *recvs)


def _add_chips(es, ps, me):
    n = len(es)

    def body(m_ref, *refs):
        for e_ref, p_ref, o_ref in zip(refs[:n], refs[n:2 * n], refs[2 * n:]):
            own = p_ref[0].astype(F32)
            acc = None
            for s in range(N_CHIPS):
                t = jnp.where(m_ref[0] == s, own, e_ref[s].astype(F32))
                acc = t if acc is None else acc + t
            o_ref[...] = acc

    dims = [e.shape[1:] for e in es]
    return pl.pallas_call(
        body, name="add_chips",
        grid_spec=pltpu.PrefetchScalarGridSpec(
            num_scalar_prefetch=1, grid=(1,),
            in_specs=[pl.BlockSpec((N_CHIPS,) + d, lambda i, m: (0, 0, 0)) for d in dims]
            + [pl.BlockSpec((1,) + d, lambda i, m: (m[0], 0, 0)) for d in dims],
            out_specs=[pl.BlockSpec(d, lambda i, m: (0, 0)) for d in dims]),
        out_shape=[_sds(d) for d in dims],
        compiler_params=_params(("arbitrary",)),
    )(me, *es, *ps)


def _rs_begin(gs, split, tag, deps=()):
    g4 = [g.reshape(N_CHIPS, 2, g.shape[0] // (2 * N_CHIPS), g.shape[1]) for g in gs]
    h = _xchg_begin(f"rs_swap_{tag}", _swap_plan, len(gs), g4, [_sds((N_CHIPS,) + g.shape[2:]) for g in g4], split, deps)
    return dict(h=h, split=split, tag=tag, shapes=[g.shape for g in gs])


def _rs_add_mine(st, after):
    g4, recv = _xchg_end(st["h"], after)
    half = jnp.reshape(lax.axis_index("c"), (1,)).astype(jnp.int32)
    ps = _add_mine(g4, recv, half)
    st["h"] = _xchg_begin(f"rs_chips_{st['tag']}", _chips_plan, 3 * len(ps), ps, [_sds(p.shape, p.dtype) for p in ps], st["split"])
    return st


def _rs_add_chips(st, after):
    ps, es = _xchg_end(st["h"], after)
    me = jnp.reshape(2 * lax.axis_index("x") + lax.axis_index("y"), (1,)).astype(jnp.int32)
    fs = _add_chips(es, ps, me)
    st["h"] = _xchg_begin(f"rs_share_{st['tag']}", _share_plan, len(fs), fs, [_sds((2,) + f.shape) for f in fs], st["split"])
    return st


def _rs_end(st, after):
    fs, ss = _xchg_end(st["h"], after)
    c = lax.axis_index("c")
    return [lax.dynamic_update_index_in_dim(s, f, c, 0).reshape(shp[0] // N_CHIPS, shp[1])
            for s, f, shp in zip(ss, fs, st["shapes"])]


WEIGHTS = ["norm_g", "w_in", "conv_a_w", "ssd_conv_w", "ssd_conv_b", "ssd_dt_bias", "ssd_a_log", "ssd_d", "ssd_norm_g",
           "mla_q_norm_g", "w_qb", "mla_kv_norm_g", "w_kvb", "w_out", "final_norm_g"]
BIG = ["w_in", "w_qb", "w_kvb", "w_out"]
SLAB_ROWS = 128


def _to_slab(parts, rows):
    flat = jnp.concatenate([p.reshape(-1) for p in parts])
    return jnp.pad(flat, (0, rows * LANE - flat.shape[0])).reshape(rows, LANE)


def _from_slab(slab, shapes):
    flat = slab.reshape(-1)
    out, off = [], 0
    for shp in shapes:
        n = int(np.prod(shp))
        out.append(flat[off:off + n].reshape(shp))
        off += n
    return out


def kernel(x, positions, norm_g, w_in, conv_a_w, ssd_conv_w, ssd_conv_b, ssd_dt_bias, ssd_a_log, ssd_d, ssd_norm_g, mla_q_norm_g, w_qb, mla_kv_norm_g, w_kvb, w_out, final_norm_g, loss_target, m_norm_g, m_w_in, m_conv_a_w, m_ssd_conv_w, m_ssd_conv_b, m_ssd_dt_bias, m_ssd_a_log, m_ssd_d, m_ssd_norm_g, m_mla_q_norm_g, m_w_qb, m_mla_kv_norm_g, m_w_kvb, m_w_out, m_final_norm_g, v_norm_g, v_w_in, v_conv_a_w, v_ssd_conv_w, v_ssd_conv_b, v_ssd_dt_bias, v_ssd_a_log, v_ssd_d, v_ssd_norm_g, v_mla_q_norm_g, v_w_qb, v_mla_kv_norm_g, v_w_kvb, v_w_out, v_final_norm_g):
    w = dict(norm_g=norm_g, w_in=w_in, conv_a_w=conv_a_w, ssd_conv_w=ssd_conv_w, ssd_conv_b=ssd_conv_b,
             ssd_dt_bias=ssd_dt_bias, ssd_a_log=ssd_a_log, ssd_d=ssd_d, ssd_norm_g=ssd_norm_g, mla_q_norm_g=mla_q_norm_g,
             w_qb=w_qb, mla_kv_norm_g=mla_kv_norm_g, w_kvb=w_kvb, w_out=w_out, final_norm_g=final_norm_g)
    mom = dict(norm_g=m_norm_g, w_in=m_w_in, conv_a_w=m_conv_a_w, ssd_conv_w=m_ssd_conv_w, ssd_conv_b=m_ssd_conv_b,
               ssd_dt_bias=m_ssd_dt_bias, ssd_a_log=m_ssd_a_log, ssd_d=m_ssd_d, ssd_norm_g=m_ssd_norm_g,
               mla_q_norm_g=m_mla_q_norm_g, w_qb=m_w_qb, mla_kv_norm_g=m_mla_kv_norm_g, w_kvb=m_w_kvb, w_out=m_w_out,
               final_norm_g=m_final_norm_g)
    var = dict(norm_g=v_norm_g, w_in=v_w_in, conv_a_w=v_conv_a_w, ssd_conv_w=v_ssd_conv_w, ssd_conv_b=v_ssd_conv_b,
               ssd_dt_bias=v_ssd_dt_bias, ssd_a_log=v_ssd_a_log, ssd_d=v_ssd_d, ssd_norm_g=v_ssd_norm_g,
               mla_q_norm_g=v_mla_q_norm_g, w_qb=v_w_qb, mla_kv_norm_g=v_mla_kv_norm_g, w_kvb=v_w_kvb, w_out=v_w_out,
               final_norm_g=v_final_norm_g)
    chip = 2 * lax.axis_index("x") + lax.axis_index("y")

    def early_shard(l, zero):
        pack = jnp.pad(conv_a_w[l], ((0, 5), (0, 192))) + jnp.pad(ssd_conv_w[l], ((3, 1), (0, 32)))
        return [(_perm_cols(w_in[l]) + zero).astype(MXU), pack + zero]

    def late_shard(l, zero):
        return [(w_out[l] + zero).astype(MXU), (w_qb[l].T + zero).astype(MXU), (w_kvb[l].T + zero).astype(MXU)]

    def early_weights(l, gathered):
        g_in, g_conv = gathered
        return dict(
            norm_g=norm_g[l][None], w_in=g_in.reshape(D_MODEL, NCOL),
            conv_a_w=jnp.concatenate([g_conv[j, 0:3, 0:64] for j in range(N_CHIPS)], axis=1),
            ssd_conv_w=jnp.concatenate([g_conv[j, 3:7, 0:224] for j in range(N_CHIPS)], axis=1),
            ssd_conv_b=ssd_conv_b[l][None], sc=_ssd_scalars(ssd_dt_bias[l], ssd_a_log[l], ssd_d[l]),
            g_ssd=ssd_norm_g[l][None], gq=mla_q_norm_g[l][None], gkv=mla_kv_norm_g[l][None])

    def late_weights(gathered):
        g_out, g_qb, g_kvb = gathered
        return dict(wq=_wq_layout(g_qb.reshape(MLA_HEADS * 96, Q_LORA)), wkv=_wkv_layout(g_kvb.reshape(MLA_HEADS * LANE, KV_LORA)),
                    w_out=g_out.reshape(D_MODEL, D_MODEL))

    def late_grads(dw_out, dwq, dwkv):
        wq = jnp.pad(_wq_unlayout(dwq).reshape(N_CHIPS, 144, Q_LORA), ((0, 0), (0, 16), (0, 0)))
        return [dw_out, wq.reshape(N_CHIPS * 160, Q_LORA), _wkv_unlayout(dwkv)]

    def large_grads(g):
        return [g["w_in"]] + late_grads(g["w_out"], g["wq"], g["wkv"])

    gather_a0 = _gather_begin(early_shard(0, 0.0), True, "a0")
    zero = gather_a0["token"][0, 0]
    cos, sin = _rope_tables(positions[0] + zero.astype(jnp.int32))
    late0, shards1 = late_shard(0, zero), early_shard(1, zero) + late_shard(1, zero)
    opt_in = {nm: [w[nm], mom[nm], var[nm]] for nm in BIG}
    opt_in["w_in"] = [w["w_in"], mom["w_in"] + zero, var["w_in"] + zero]
    lw0 = early_weights(0, _gather_end(gather_a0, [cos, sin] + late0 + shards1 + opt_in["w_in"][1:]))
    gather_b0 = _gather_begin(late0, True, "b0")
    gather_1 = _gather_begin(shards1, True, "1", [gather_b0["token"]])
    x1, sv0, lw0 = _layer_fwd(x[0], lw0, cos, sin, gather_1["token"],
                              lambda ya, y_ssd: late_weights(_gather_end(gather_b0, [ya, y_ssd])))
    g1 = _gather_end(gather_1, [x1])
    x2, sv1, lw1 = _layer_fwd(x1, {**early_weights(1, g1[:2]), **late_weights(g1[2:])}, cos, sin)
    dx, dgf, loss = _loss_head(x2, final_norm_g[None], loss_target[0])

    dx, lg1, _, _ = _layer_bwd(dx, lw1, sv1, cos, sin)
    grad_x, lg0, red1, rs0_late = _layer_bwd(dx, lw0, sv0, cos, sin, _rs_begin(large_grads(lg1), True, 1),
                                             lambda *g: _rs_begin(late_grads(*g), True, "0l"))
    rs0 = _rs_begin([lg0["w_in"]], True, 0, [rs0_late["h"]["token"]])
    lg = [lg0, lg1]
    grad = {}

    small_names = ["norm_g", "conv_a_w", "ssd_conv_w", "ssd_conv_b", "sc", "g_ssd", "gq", "gkv"]
    parts = [loss[0, 0:1], dgf]
    for l in range(DEPTH):
        parts += [lg[l][nm][:3, DT_LANE:DT_LANE + SSD_HEADS] if nm == "sc" else lg[l][nm] for nm in small_names]
    shapes = [(1,), (D_MODEL,)] + [(D_MODEL,), (3, D_CONV_A), (4, N_XBC), (N_XBC,), (3, SSD_HEADS), (D_SSD,), (Q_LORA,), (KV_LORA,)] * DEPTH
    red_slab = _allreduce_small(_to_slab(parts, SLAB_ROWS), rs0["h"]["token"])
    rs0 = _rs_add_mine(rs0, [red_slab])
    red = _from_slab(red_slab + rs0["h"]["token"][0, 0], shapes)
    loss_out = red[0][0]
    grad["final_norm_g"] = red[1]
    per = [red[2 + 8 * l:10 + 8 * l] for l in range(DEPTH)]
    grad["norm_g"] = jnp.stack([per[l][0] for l in range(DEPTH)])
    grad["conv_a_w"] = lax.dynamic_slice_in_dim(jnp.stack([per[l][1] for l in range(DEPTH)]), chip * 64, 64, axis=2)
    grad["ssd_conv_w"] = lax.dynamic_slice_in_dim(jnp.stack([per[l][2] for l in range(DEPTH)]), chip * 224, 224, axis=2)
    grad["ssd_conv_b"] = jnp.stack([per[l][3] for l in range(DEPTH)])
    grad["ssd_dt_bias"] = jnp.stack([per[l][4][0] for l in range(DEPTH)])
    grad["ssd_a_log"] = jnp.stack([per[l][4][1] for l in range(DEPTH)])
    grad["ssd_d"] = jnp.stack([per[l][4][2] for l in range(DEPTH)])
    grad["ssd_norm_g"] = jnp.stack([per[l][5] for l in range(DEPTH)])
    grad["mla_q_norm_g"] = jnp.stack([per[l][6] for l in range(DEPTH)])
    grad["mla_kv_norm_g"] = jnp.stack([per[l][7] for l in range(DEPTH)])

    delta, new_m, new_v = {}, {}, {}
    small = [nm for nm in WEIGHTS if nm not in BIG]
    row2 = lambda a: a[None] if a.ndim == 1 else a
    small_out = _adamw(*[[row2(a[nm]) for nm in small] for a in (w, grad, mom, var)], whole=True)
    for nm, (dv, mv, vv) in zip(small, small_out):
        delta[nm], new_m[nm], new_v[nm] = [a.reshape(w[nm].shape) for a in (dv, mv, vv)]

    r_out, r_qb, r_kvb = [jnp.stack([a, b]) for a, b in zip(_rs_end(rs0_late, [red_slab]), red1[1:])]
    grad.update(w_out=r_out, w_qb=jnp.swapaxes(r_qb[:, :144], 1, 2), w_kvb=jnp.swapaxes(r_kvb, 1, 2))
    late = [nm for nm in BIG if nm != "w_in"]
    late_out = _adamw([opt_in[nm][0] for nm in late], [grad[nm] for nm in late], [opt_in[nm][1] for nm in late],
                      [opt_in[nm][2] for nm in late], whole=False)
    for nm, (dv, mv, vv) in zip(late, late_out):
        delta[nm], new_m[nm], new_v[nm] = dv, mv, vv
    w_in_opt = [[a] for a in opt_in["w_in"]]
    w_in_l1 = _adamw(w_in_opt[0], [_unperm_cols(red1[0])], w_in_opt[1], w_in_opt[2], whole=False, layer=1)

    shadow_work = [a for row in small_out + late_out for a in row] + [grad[nm] for nm in small] + w_in_l1
    r_in0, = _rs_end(_rs_add_chips(rs0, shadow_work), [])
    grad["w_in"], delta["w_in"], new_m["w_in"], new_v["w_in"] = _adamw(
        w_in_opt[0], [_unperm_cols(r_in0)], w_in_opt[1], w_in_opt[2], whole=False, layer=0, into=w_in_l1)

    return (loss_out, grad_x[None], *[grad[nm] for nm in WEIGHTS], *[delta[nm] for nm in WEIGHTS],
            *[new_m[nm] for nm in WEIGHTS], *[new_v[nm] for nm in WEIGHTS])
```

```python
import functools
import math

import numpy as np
import jax
import jax.numpy as jnp
from jax import lax
from jax.experimental import pallas as pl
from jax.experimental.pallas import tpu as pltpu

F32 = jnp.float32
MXU = jnp.bfloat16

D_MODEL = 1024
DEPTH = 2
D_CONV_A = 256
D_SSD = 384
SSD_HEADS = 6
SSD_BC = 256
SSD_CHUNK = 128
SSD_CHUNKS_PER_STEP = 4
SSD_NORM_EPS = 1e-5
MLA_HEADS = 6
Q_LORA = 256
KV_LORA = 128
QK_NOPE = 64
QK_ROPE = 32
V_DIM = 64
D_MLA = 384
ROPE_BASE = 10000.0
NORM_EPS = 1e-6
IN_COLS = 3110
LANE = 128

O_AH, O_AB, O_AC, O_AZ = 0, 256, 512, 768
O_XBC = 1024
O_SZ = 1920
O_CQA = 2304
O_CKV = 2560
O_CZ = 2688
O_TAIL = 3072
NCOL = 3200
N_XBC = D_SSD + 2 * SSD_BC
DT_LANE = 32
ROPE_LANE = 64

ADAM_LR, ADAM_B1, ADAM_B2, ADAM_EPS, ADAM_WD, ADAM_STEP = 0.001, 0.9, 0.999, 1e-08, 0.01, 10

VMEM_LIMIT = 56 * 1024 * 1024
MESH_T = pl.DeviceIdType.MESH


def _dot(a, b):
    return jnp.dot(a.astype(MXU), b.astype(MXU), preferred_element_type=F32)


def _dot_nt(a, b):
    return lax.dot_general(a.astype(MXU), b.astype(MXU), (((1,), (1,)), ((), ())), preferred_element_type=F32)


def _dot_tn(a, b):
    return lax.dot_general(a.astype(MXU), b.astype(MXU), (((0,), (0,)), ((), ())), preferred_element_type=F32)


def _dot_hi(a, b):
    return jnp.dot(a, b, precision=lax.Precision.HIGHEST, preferred_element_type=F32)


def _dot_hi_tn(a, b):
    return lax.dot_general(a, b, (((0,), (0,)), ((), ())), precision=lax.Precision.HIGHEST, preferred_element_type=F32)


def _sigmoid(z):
    return 1.0 / (1.0 + jnp.exp(-z))


def _silu(z):
    return z * _sigmoid(z)


def _dsilu(z):
    s = _sigmoid(z)
    return s * (1.0 + z * (1.0 - s))


def _softplus(z):
    e = jnp.exp(-jnp.abs(z))
    return jnp.maximum(z, 0.0) + jnp.where(e < 1e-3, e * (1.0 - 0.5 * e), jnp.log(1.0 + e))


def _iota(shape, dim):
    return lax.broadcasted_iota(jnp.int32, shape, dim)


def _shift_down(u, k):
    if k == 0:
        return u
    return jnp.where(_iota(u.shape, 0) >= k, pltpu.roll(u, k, 0), 0.0)


def _shift_up(u, k):
    if k == 0:
        return u
    n = u.shape[0]
    return jnp.where(_iota(u.shape, 0) < n - k, pltpu.roll(u, n - k, 0), 0.0)


def _rope_swap(t):
    lane = _iota(t.shape, 1)
    lo = (lane >= ROPE_LANE) & (lane < ROPE_LANE + 16)
    hi = (lane >= ROPE_LANE + 16) & (lane < ROPE_LANE + 32)
    return jnp.where(lo, pltpu.roll(t, LANE - 16, 1), jnp.where(hi, pltpu.roll(t, 16, 1), 0.0))


def _params(sem=None):
    return pltpu.CompilerParams(dimension_semantics=sem, vmem_limit_bytes=VMEM_LIMIT)


def _full(shape):
    nd = len(shape)
    return pl.BlockSpec(shape, lambda *_: (0,) * nd)


def _sds(shape, dtype=F32):
    return jax.ShapeDtypeStruct(shape, dtype)


def _tile(s):
    return min(512, s)


def _row(ts, w):
    return pl.BlockSpec((ts, w), lambda i: (i, 0))


def _gate_cols(ts, off):
    return pl.BlockSpec((ts, D_SSD), lambda i, _o=off // D_SSD: (i, _o))


def _col(s, off):
    return pl.BlockSpec((s, LANE), lambda j, _o=off // LANE: (0, _o + j))


def _call_after(dep, body, args, *, in_specs, **kw):
    if dep is None:
        return pl.pallas_call(body, in_specs=in_specs, **kw)(*args)
    n = len(args)

    def body_dep(*refs):
        body(*refs[:n], *refs[n + 1:])

    return pl.pallas_call(body_dep, in_specs=list(in_specs) + [pl.BlockSpec(memory_space=pl.ANY)], **kw)(*args, dep)


def _rms(c, g):
    r = lax.rsqrt(jnp.mean(c * c, axis=-1, keepdims=True) + NORM_EPS)
    return c * r * g, r


def _rms_bwd(dn, c, r, g):
    ch = c * r
    dch = dn * g
    dc = r * (dch - ch * jnp.mean(dch * ch, axis=-1, keepdims=True))
    return dc, jnp.sum(dn * ch, axis=0, keepdims=True)


def _inproj_fwd(x, g, w, dep=None):
    s = x.shape[0]
    ts = _tile(s)

    def body(x_ref, g_ref, w_ref, proj_ref, h_ref, r_ref):
        hn, r = _rms(x_ref[...], g_ref[...])
        h = hn.astype(MXU)
        h_ref[...] = h
        r_ref[...] = r
        proj_ref[...] = jnp.dot(h, w_ref[...], preferred_element_type=F32)

    return _call_after(
        dep, body, (x, g, w), name="inproj_fwd", grid=(s // ts,),
        in_specs=[_row(ts, D_MODEL), _full((1, D_MODEL)), _full((D_MODEL, NCOL))],
        out_specs=[_row(ts, NCOL), _row(ts, D_MODEL), _row(ts, 1)],
        out_shape=[_sds((s, NCOL)), _sds((s, D_MODEL), MXU), _sds((s, 1))],
        compiler_params=_params(("parallel",)),
    )


def _conva_fwd(proj, w):
    s = proj.shape[0]

    def body(h_ref, b_ref, c_ref, z_ref, w_ref, y_ref):
        u = c_ref[...] * h_ref[...]
        wv = w_ref[...]
        cv = wv[2:3, :] * u + wv[1:2, :] * _shift_down(u, 1) + wv[0:1, :] * _shift_down(u, 2)
        y_ref[...] = b_ref[...] * cv * _silu(z_ref[...])

    return pl.pallas_call(
        body, name="conva_fwd", grid=(D_CONV_A // LANE,),
        in_specs=[_col(s, O_AH), _col(s, O_AB), _col(s, O_AC), _col(s, O_AZ), pl.BlockSpec((3, LANE), lambda j: (0, j))],
        out_specs=pl.BlockSpec((s, LANE), lambda j: (0, j)),
        out_shape=_sds((s, D_CONV_A)),
        compiler_params=_params(("parallel",)),
    )(proj, proj, proj, proj, w)


def _sconv_pre(u, wv, bv):
    return (wv[3:4, :] * u + wv[2:3, :] * _shift_down(u, 1) + wv[1:2, :] * _shift_down(u, 2)
            + wv[0:1, :] * _shift_down(u, 3) + bv)


def _sconv_fwd(proj, w, b):
    s = proj.shape[0]

    def body(u_ref, w_ref, b_ref, o_ref):
        o_ref[...] = _silu(_sconv_pre(u_ref[...], w_ref[...], b_ref[...]))

    return pl.pallas_call(
        body, name="sconv_fwd", grid=(N_XBC // LANE,),
        in_specs=[_col(s, O_XBC), pl.BlockSpec((4, LANE), lambda j: (0, j)), pl.BlockSpec((1, LANE), lambda j: (0, j))],
        out_specs=pl.BlockSpec((s, LANE), lambda j: (0, j)),
        out_shape=_sds((s, N_XBC)),
        compiler_params=_params(("parallel",)),
    )(proj, w, b)


def _ssd_chunk_common(tail, sc):
    l = SSD_CHUNK
    lane = _iota((l, LANE), 1)
    row = _iota((l, LANE), 0)
    tri = (row >= lane).astype(F32)
    a_row = -jnp.exp(sc[1:2, :])
    pre = tail + sc[0:1, :]
    dt = _softplus(pre)
    a_cs = _dot_hi(tri, dt * a_row)
    return lane, row, tri, a_row, pre, dt, a_cs, a_cs.T


def _pick_col(m, lane, k):
    return jnp.sum(jnp.where(lane == k, m, 0.0), axis=1, keepdims=True)


def _pick_row(m, row, k):
    return jnp.sum(jnp.where(row == k, m, 0.0), axis=0, keepdims=True)


def _ssd_fwd(xbc, proj, sc):
    s = xbc.shape[0]
    nc = s // SSD_CHUNK
    l = SSD_CHUNK
    cps = SSD_CHUNKS_PER_STEP

    def body(xbc_ref, tail_ref, sc_ref, y_ref, st_ref, state):
        @pl.when(pl.program_id(0) == 0)
        def _():
            state[...] = jnp.zeros_like(state)

        sc_v = sc_ref[...]
        lane1 = _iota((1, LANE), 1)
        rowp = _iota((LANE, 1), 0)
        d_row = sc_v[2:3, :]
        states = [state[j] for j in range(3)]
        for u in range(cps):
            r = slice(u * l, (u + 1) * l)
            lane, row, _, _, _, dt, a_cs, a_t = _ssd_chunk_common(tail_ref[r, :], sc_v)
            for j in range(3):
                st_ref[u, j] = states[j]
            for j in range(3):
                xpair = xbc_ref[r, LANE * j:LANE * (j + 1)]
                sp = states[j]
                ypair = jnp.zeros((l, LANE), F32)
                new_s = jnp.zeros((LANE, LANE), F32)
                decay = jnp.zeros((LANE, 1), F32)
                for half in range(2):
                    h = 2 * j + half
                    g = h // 3
                    hm = (lane < 64) if half == 0 else (lane >= 64)
                    hrow = (rowp < 64) if half == 0 else (rowp >= 64)
                    ac = _pick_col(a_cs, lane, DT_LANE + h)
                    ar = _pick_row(a_t, row, DT_LANE + h)
                    dtc = _pick_col(dt, lane, DT_LANE + h)
                    alast = jnp.sum(jnp.where(lane1 == l - 1, ar, 0.0), axis=1, keepdims=True)
                    dh = jnp.sum(jnp.where(lane1 == DT_LANE + h, d_row, 0.0), axis=1, keepdims=True)
                    xm = jnp.where(hm, xpair, 0.0)
                    xd = xm * dtc
                    bm = xbc_ref[r, D_SSD + LANE * g:D_SSD + LANE * (g + 1)]
                    cm = xbc_ref[r, D_SSD + SSD_BC + LANE * g:D_SSD + SSD_BC + LANE * (g + 1)]
                    lm = jnp.where(row >= lane, jnp.exp(jnp.minimum(ac - ar, 0.0)), 0.0)
                    y_diag = _dot(_dot_nt(cm, bm) * lm, xd)
                    y_off = jnp.where(hm, _dot_nt(cm, sp), 0.0) * jnp.exp(ac)
                    ypair = ypair + y_diag + y_off + xm * dh
                    new_s = new_s + _dot_tn(xd * jnp.exp(alast - ac), bm)
                    decay = jnp.where(hrow, jnp.exp(alast), decay)
                states[j] = sp * decay + new_s
                y_ref[r, LANE * j:LANE * (j + 1)] = ypair
        for j in range(3):
            state[j] = states[j]

    return pl.pallas_call(
        body, name="ssd_fwd", grid=(nc // cps,),
        in_specs=[pl.BlockSpec((cps * l, N_XBC), lambda c: (c, 0)),
                  pl.BlockSpec((cps * l, LANE), lambda c: (c, O_TAIL // LANE)), _full((8, LANE))],
        out_specs=[pl.BlockSpec((cps * l, D_SSD), lambda c: (c, 0)), pl.BlockSpec((cps, 3, LANE, LANE), lambda c: (c, 0, 0, 0))],
        out_shape=[_sds((s, D_SSD)), _sds((nc, 3, LANE, LANE))],
        scratch_shapes=[pltpu.VMEM((3, LANE, LANE), F32)],
        compiler_params=_params(("arbitrary",)),
    )(xbc, proj, sc)


def _mla_prep_fwd(proj, gq, gkv, wq, wkv, cos, sin):
    s = proj.shape[0]
    ts = _tile(s)
    nh = MLA_HEADS

    def body(cqa_ref, ckv_ref, tail_ref, gq_ref, gkv_ref, wq_ref, wkv_ref, cos_ref, sin_ref,
             q_ref, k_ref, v_ref, qn_ref, kvn_ref, rq_ref, rkv_ref):
        qn, rq = _rms(cqa_ref[...], gq_ref[...])
        kvn, rkv = _rms(ckv_ref[...], gkv_ref[...])
        qn = qn.astype(MXU)
        kvn = kvn.astype(MXU)
        qn_ref[...] = qn
        kvn_ref[...] = kvn
        rq_ref[...] = rq
        rkv_ref[...] = rkv
        q = _dot_nt(qn, wq_ref[...])
        kv = _dot_nt(kvn, wkv_ref[...])
        cosv = cos_ref[...]
        sinv = sin_ref[...]
        lane = _iota((ts, LANE), 1)
        rope_lanes = (lane >= ROPE_LANE) & (lane < ROPE_LANE + QK_ROPE)
        kr = jnp.where(rope_lanes, pltpu.roll(tail_ref[...], ROPE_LANE, 1), 0.0)
        kr = kr * cosv + _rope_swap(kr) * sinv
        for h in range(nh):
            qh = q[:, LANE * h:LANE * (h + 1)]
            q_ref[h] = ((qh * cosv + _rope_swap(qh) * sinv) * ATT_SCALE).astype(MXU)
            k_ref[h] = (kv[:, LANE * h:LANE * (h + 1)] + kr).astype(MXU)
            v_ref[h] = kv[:, LANE * (nh + h):LANE * (nh + h + 1)].astype(MXU)

    head = pl.BlockSpec((nh, ts, LANE), lambda i: (0, i, 0))
    return pl.pallas_call(
        body, name="mla_prep_fwd", grid=(s // ts,),
        in_specs=[pl.BlockSpec((ts, Q_LORA), lambda i: (i, O_CQA // Q_LORA)),
                  pl.BlockSpec((ts, KV_LORA), lambda i: (i, O_CKV // KV_LORA)),
                  pl.BlockSpec((ts, LANE), lambda i: (i, O_TAIL // LANE)),
                  _full((1, Q_LORA)), _full((1, KV_LORA)), _full((nh * LANE, Q_LORA)), _full((2 * nh * LANE, KV_LORA)),
                  _row(ts, LANE), _row(ts, LANE)],
        out_specs=[head, head, head, _row(ts, Q_LORA), _row(ts, KV_LORA), _row(ts, 1), _row(ts, 1)],
        out_shape=[_sds((nh, s, LANE), MXU)] * 3 + [_sds((s, Q_LORA), MXU), _sds((s, KV_LORA), MXU), _sds((s, 1)), _sds((s, 1))],
        compiler_params=_params(("parallel",)),
    )(proj, proj, proj, gq, gkv, wq, wkv, cos, sin)


ATT_SCALE = (QK_NOPE + QK_ROPE) ** -0.5
NEG = -1e30


def _att_tile(s, most):
    return min(most, s // 2)


ATT_FWD_TILE = 1024
ATT_BWD_TILE = 512


def _attn_fwd(q, k, v):
    nh, s, _ = q.shape
    tq = _att_tile(s, ATT_FWD_TILE)
    nq = s // tq

    def body(q_ref, k_ref, v_ref, o_ref, lse_ref):
        i = pl.program_id(1)
        rowi = _iota((tq, tq), 0)
        coli = _iota((tq, tq), 1)
        zero = (jnp.full((tq, 1), NEG, F32), jnp.zeros((tq, 1), F32), jnp.zeros((tq, LANE), F32))
        state = [zero, zero]
        done = [zero, zero]
        for t in range(nq + 1):
            first = t <= i
            qblk = jnp.where(first, i, nq - 1 - i)
            kblk = jnp.where(first, t, t - i - 1)
            qoff = pl.multiple_of(qblk * tq, tq)
            koff = pl.multiple_of(kblk * tq, tq)
            keep = coli <= rowi + jnp.where(kblk == qblk, 0, tq)
            restart = t == i + 1
            for hh in range(2):
                m, lsum, acc = state[hh]
                if t > 0:
                    done[hh] = tuple(jnp.where(restart, a, b) for a, b in zip(state[hh], done[hh]))
                    m = jnp.where(restart, NEG, m)
                    lsum = jnp.where(restart, 0.0, lsum)
                    acc = jnp.where(restart, 0.0, acc)
                sc = _dot_nt(q_ref[hh, pl.ds(qoff, tq), :], k_ref[hh, pl.ds(koff, tq), :])
                sc = jnp.where(keep, sc, NEG)
                m_new = jnp.maximum(m, jnp.max(sc, axis=1, keepdims=True))
                p = jnp.exp(sc - m_new)
                alpha = jnp.exp(m - m_new)
                lsum = alpha * lsum + jnp.sum(p, axis=1, keepdims=True)
                acc = alpha * acc + _dot(p, v_ref[hh, pl.ds(koff, tq), :])
                state[hh] = (m_new, lsum, acc)
        for blk, res in ((i, done), (nq - 1 - i, state)):
            off = pl.multiple_of(blk * tq, tq)
            out = None
            for hh in range(2):
                m, lsum, acc = res[hh]
                o = acc * (1.0 / lsum)
                lse_ref[hh, pl.ds(off, tq), :] = m + jnp.log(lsum)
                out = o if hh == 0 else out + pltpu.roll(o, V_DIM, 1)
            o_ref[pl.ds(off, tq), :] = out

    pair = pl.BlockSpec((2, s, LANE), lambda j, i: (j, 0, 0))
    return pl.pallas_call(
        body, name="attn_fwd", grid=(nh // 2, nq // 2),
        in_specs=[pair, pair, pair],
        out_specs=[pl.BlockSpec((s, LANE), lambda j, i: (0, j)), pl.BlockSpec((2, s, 1), lambda j, i: (j, 0, 0))],
        out_shape=[_sds((s, D_MLA)), _sds((nh, s, 1))],
        compiler_params=_params(("parallel", "arbitrary")),
    )(q, k, v)


def _ssd_gate(y_ssd, s_z, g):
    yz = y_ssd * _silu(s_z)
    g0 = _iota(yz.shape, 1) < D_SSD // 2
    sq = yz * yz
    ms0 = jnp.sum(jnp.where(g0, sq, 0.0), axis=1, keepdims=True) / (D_SSD // 2)
    ms1 = jnp.sum(jnp.where(g0, 0.0, sq), axis=1, keepdims=True) / (D_SSD // 2)
    r = jnp.where(g0, lax.rsqrt(ms0 + SSD_NORM_EPS), lax.rsqrt(ms1 + SSD_NORM_EPS))
    nrm = yz * r
    return nrm * g, nrm, r, g0


def _outproj_fwd(x, proj, ya, y_ssd, o, g_ssd, w):
    s = x.shape[0]
    ts = _tile(s)

    def body(x_ref, sz_ref, cz_ref, ya_ref, ys_ref, o_ref, g_ref, w_ref, xo_ref, y_ref):
        yb = _ssd_gate(ys_ref[...], sz_ref[...], g_ref[...])[0]
        yc = o_ref[...] * _silu(cz_ref[...])
        y = jnp.concatenate([ya_ref[...], yb, yc], axis=1).astype(MXU)
        y_ref[...] = y
        xo_ref[...] = x_ref[...] + jnp.dot(y, w_ref[...], preferred_element_type=F32)

    return pl.pallas_call(
        body, name="outproj_fwd", grid=(s // ts,),
        in_specs=[_row(ts, D_MODEL), _gate_cols(ts, O_SZ), _gate_cols(ts, O_CZ), _row(ts, D_CONV_A), _row(ts, D_SSD),
                  _row(ts, D_MLA), _full((1, D_SSD)), _full((D_MODEL, D_MODEL))],
        out_specs=[_row(ts, D_MODEL), _row(ts, D_MODEL)],
        out_shape=[_sds((s, D_MODEL)), _sds((s, D_MODEL), MXU)],
        compiler_params=_params(("parallel",)),
    )(x, proj, proj, ya, y_ssd, o, g_ssd, w)


def _loss_head(x, g, tgt):
    s = x.shape[0]
    ts = _tile(s)

    def body(x_ref, g_ref, t_ref, dx_ref, dg_ref, loss_ref):
        @pl.when(pl.program_id(0) == 0)
        def _():
            dg_ref[...] = jnp.zeros_like(dg_ref)
            loss_ref[...] = jnp.zeros_like(loss_ref)

        xv = x_ref[...]
        gv = g_ref[...]
        yn, r = _rms(xv, gv)
        e = yn - t_ref[...]
        loss_ref[...] += jnp.sum(jnp.sum(e * e, axis=1, keepdims=True), axis=0, keepdims=True) * (0.5 / D_MODEL)
        dx, dg = _rms_bwd(e * (1.0 / D_MODEL), xv, r, gv)
        dx_ref[...] = dx
        dg_ref[...] += dg

    return pl.pallas_call(
        body, name="loss_head", grid=(s // ts,),
        in_specs=[_row(ts, D_MODEL), _full((1, D_MODEL)), _row(ts, D_MODEL)],
        out_specs=[_row(ts, D_MODEL), _full((1, D_MODEL)), _full((1, LANE))],
        out_shape=[_sds((s, D_MODEL)), _sds((1, D_MODEL)), _sds((1, LANE))],
        compiler_params=_params(("arbitrary",)),
    )(x, g, tgt)


def _outproj_bwd(dout, y, w, proj, y_ssd, o, g_ssd, dep=None):
    s = dout.shape[0]
    ts = _tile(s)

    def body(dout_ref, y_ref, w_ref, sz_ref, cz_ref, ys_ref, o_ref, g_ref,
             dya_ref, dys_ref, dsz_ref, dattn_ref, dcz_ref, dg_ref, dw_ref):
        @pl.when(pl.program_id(0) == 0)
        def _():
            dw_ref[...] = jnp.zeros_like(dw_ref)
            dg_ref[...] = jnp.zeros_like(dg_ref)

        dout_b = dout_ref[...].astype(MXU)
        dw_ref[...] += _dot_tn(y_ref[...], dout_b)
        dy = _dot_nt(dout_b, w_ref[...])
        dya_ref[...] = dy[:, :D_CONV_A]
        dyb = dy[:, D_CONV_A:D_CONV_A + D_SSD]
        sz = sz_ref[...]
        ys = ys_ref[...]
        gv = g_ref[...]
        _, nrm, r, g0 = _ssd_gate(ys, sz, gv)
        dg_ref[...] += jnp.sum(dyb * nrm, axis=0, keepdims=True)
        dn = dyb * gv
        t = dn * nrm
        mean = jnp.where(g0, jnp.sum(jnp.where(g0, t, 0.0), axis=1, keepdims=True),
                         jnp.sum(jnp.where(g0, 0.0, t), axis=1, keepdims=True)) / (D_SSD // 2)
        dyz = r * (dn - nrm * mean)
        dys_ref[...] = dyz * _silu(sz)
        dsz_ref[...] = (dyz * ys * _dsilu(sz)).astype(MXU)
        dyc = dy[:, D_CONV_A + D_SSD:]
        cz = cz_ref[...]
        dattn_ref[...] = dyc * _silu(cz)
        dcz_ref[...] = (dyc * o_ref[...] * _dsilu(cz)).astype(MXU)

    return _call_after(
        dep, body, (dout, y, w, proj, proj, y_ssd, o, g_ssd), name="outproj_bwd", grid=(s // ts,),
        in_specs=[_row(ts, D_MODEL), _row(ts, D_MODEL), _full((D_MODEL, D_MODEL)), _gate_cols(ts, O_SZ), _gate_cols(ts, O_CZ),
                  _row(ts, D_SSD), _row(ts, D_MLA), _full((1, D_SSD))],
        out_specs=[_row(ts, D_CONV_A), _row(ts, D_SSD), _row(ts, D_SSD), _row(ts, D_MLA), _row(ts, D_MLA),
                   _full((1, D_SSD)), _full((D_MODEL, D_MODEL))],
        out_shape=[_sds((s, D_CONV_A)), _sds((s, D_SSD)), _sds((s, D_SSD), MXU), _sds((s, D_MLA)), _sds((s, D_MLA), MXU),
                   _sds((1, D_SSD)), _sds((D_MODEL, D_MODEL))],
        compiler_params=_params(("arbitrary",)),
    )


def _attn_bwd(q, k, v, o, d_o, lse, dep=None):
    nh, s, _ = q.shape
    tq = _att_tile(s, ATT_BWD_TILE)
    nq = s // tq

    def body(q_ref, k_ref, v_ref, o_ref, do_ref, lse_ref, dq_ref, dk_ref, dv_ref, dop, delta):
        i = pl.program_id(1)

        @pl.when(i == 0)
        def _():
            lane = _iota((s, LANE), 1)
            for hh in range(2):
                dov = do_ref[...]
                ov = o_ref[...]
                if hh == 1:
                    dov = pltpu.roll(dov, V_DIM, 1)
                    ov = pltpu.roll(ov, V_DIM, 1)
                dov = jnp.where(lane < V_DIM, dov, 0.0)
                dop[hh] = dov.astype(MXU)
                delta[hh] = jnp.sum(dov * ov, axis=1, keepdims=True)
                dq_ref[hh] = jnp.zeros((s, LANE), F32)

        rowi = _iota((tq, tq), 0)
        coli = _iota((tq, tq), 1)
        z = jnp.zeros((tq, LANE), F32)
        state = [(z, z), (z, z)]
        done = [(z, z), (z, z)]
        for t in range(nq + 1):
            first = t <= nq - 1 - i
            kblk = jnp.where(first, i, nq - 1 - i)
            qblk = jnp.where(first, i + t, t - 1)
            qoff = pl.multiple_of(qblk * tq, tq)
            koff = pl.multiple_of(kblk * tq, tq)
            keep = coli <= rowi + jnp.where(kblk == qblk, 0, tq)
            restart = t == nq - i
            for hh in range(2):
                dk, dv = state[hh]
                if t > 0:
                    done[hh] = tuple(jnp.where(restart, a, b) for a, b in zip(state[hh], done[hh]))
                    dk = jnp.where(restart, 0.0, dk)
                    dv = jnp.where(restart, 0.0, dv)
                kb = k_ref[hh, pl.ds(koff, tq), :]
                qb = q_ref[hh, pl.ds(qoff, tq), :]
                dob = dop[hh, pl.ds(qoff, tq), :]
                sc = jnp.where(keep, _dot_nt(qb, kb), NEG)
                p = jnp.exp(sc - lse_ref[hh, pl.ds(qoff, tq), :])
                dp = _dot_nt(dob, v_ref[hh, pl.ds(koff, tq), :])
                ds = p * (dp - delta[hh, pl.ds(qoff, tq), :])
                dq_ref[hh, pl.ds(qoff, tq), :] += _dot(ds, kb)
                state[hh] = (dk + _dot_tn(ds, qb), dv + _dot_tn(p, dob))
        for blk, res in ((i, done), (nq - 1 - i, state)):
            off = pl.multiple_of(blk * tq, tq)
            for hh in range(2):
                dk_ref[hh, pl.ds(off, tq), :] = res[hh][0]
                dv_ref[hh, pl.ds(off, tq), :] = res[hh][1]

    pair = pl.BlockSpec((2, s, LANE), lambda j, i: (j, 0, 0))
    return _call_after(
        dep, body, (q, k, v, o, d_o, lse), name="attn_bwd", grid=(nh // 2, nq // 2),
        in_specs=[pair, pair, pair, pl.BlockSpec((s, LANE), lambda j, i: (0, j)), pl.BlockSpec((s, LANE), lambda j, i: (0, j)),
                  pl.BlockSpec((2, s, 1), lambda j, i: (j, 0, 0))],
        out_specs=[pair, pair, pair],
        out_shape=[_sds((nh, s, LANE))] * 3,
        scratch_shapes=[pltpu.VMEM((2, s, LANE), MXU), pltpu.VMEM((2, s, 1), F32)],
        compiler_params=_params(("parallel", "arbitrary")),
    )


def _ssd_bwd(xbc, proj, sc, states, dy, dep=None):
    s = xbc.shape[0]
    nc = s // SSD_CHUNK
    l = SSD_CHUNK
    cps = SSD_CHUNKS_PER_STEP

    def body(xbc_ref, tail_ref, sc_ref, st_ref, dy_ref, dxbc_ref, dtail_ref, dsc_ref, dstate):
        @pl.when(pl.program_id(0) == 0)
        def _():
            dstate[...] = jnp.zeros_like(dstate)
            dsc_ref[...] = jnp.zeros_like(dsc_ref)

        sc_v = sc_ref[...]
        lane1 = _iota((1, LANE), 1)
        rowp = _iota((LANE, 1), 0)
        rowl = _iota((l, 1), 0)
        d_row = sc_v[2:3, :]
        dstates = [dstate[j] for j in range(3)]
        for u in reversed(range(cps)):
            dstates = chunk(u, xbc_ref, tail_ref, sc_v, st_ref, dy_ref, dxbc_ref, dtail_ref, dsc_ref, dstates,
                            lane1, rowp, rowl, d_row)
        for j in range(3):
            dstate[j] = dstates[j]

    def chunk(u, xbc_ref, tail_ref, sc_v, st_ref, dy_ref, dxbc_ref, dtail_ref, dsc_ref, dstates, lane1, rowp, rowl, d_row):
        r = slice(u * l, (u + 1) * l)
        dstates = list(dstates)
        lane, row, tri, a_row, pre, dt, a_cs, a_t = _ssd_chunk_common(tail_ref[r, :], sc_v)
        da_col = jnp.zeros((l, LANE), F32)
        da_row = jnp.zeros((LANE, l), F32)
        dt_x = jnp.zeros((l, LANE), F32)
        dd_row = jnp.zeros((1, LANE), F32)
        db = [jnp.zeros((l, LANE), F32), jnp.zeros((l, LANE), F32)]
        dc = [jnp.zeros((l, LANE), F32), jnp.zeros((l, LANE), F32)]
        for j in range(3):
            xpair = xbc_ref[r, LANE * j:LANE * (j + 1)]
            dypair = dy_ref[r, LANE * j:LANE * (j + 1)]
            sp = st_ref[u, j]
            dsp = dstates[j]
            dxpair = jnp.zeros((l, LANE), F32)
            ds_new = jnp.zeros((LANE, LANE), F32)
            decay = jnp.zeros((LANE, 1), F32)
            for half in range(2):
                h = 2 * j + half
                g = h // 3
                hm = (lane < 64) if half == 0 else (lane >= 64)
                hrow = (rowp < 64) if half == 0 else (rowp >= 64)
                ac = _pick_col(a_cs, lane, DT_LANE + h)
                ar = _pick_row(a_t, row, DT_LANE + h)
                dtc = _pick_col(dt, lane, DT_LANE + h)
                alast = jnp.sum(jnp.where(lane1 == l - 1, ar, 0.0), axis=1, keepdims=True)
                dh = jnp.sum(jnp.where(lane1 == DT_LANE + h, d_row, 0.0), axis=1, keepdims=True)
                xm = jnp.where(hm, xpair, 0.0)
                xd = xm * dtc
                dym = jnp.where(hm, dypair, 0.0)
                bm = xbc_ref[r, D_SSD + LANE * g:D_SSD + LANE * (g + 1)]
                cm = xbc_ref[r, D_SSD + SSD_BC + LANE * g:D_SSD + SSD_BC + LANE * (g + 1)]
                lm = jnp.where(row >= lane, jnp.exp(jnp.minimum(ac - ar, 0.0)), 0.0)
                e_in = jnp.exp(ac)
                f_out = jnp.exp(alast - ac)
                e_last = jnp.exp(alast)
                m = _dot_nt(cm, bm) * lm
                y_off = jnp.where(hm, _dot_nt(cm, sp), 0.0) * e_in
                dm = _dot_nt(dym, xd)
                dxd = _dot_tn(m, dym)
                dg = dm * lm
                dye = dym * e_in
                dc[g] = dc[g] + _dot(dg, bm) + _dot(dye, sp)
                db[g] = db[g] + _dot_tn(dg, cm)
                qm = dm * m
                dac = jnp.sum(qm, axis=1, keepdims=True) + jnp.sum(dym * y_off, axis=1, keepdims=True)
                dar = -jnp.sum(qm, axis=0, keepdims=True)
                dxf = jnp.where(hm, _dot_nt(bm, dsp), 0.0)
                db[g] = db[g] + _dot(xd * f_out, dsp)
                dxd = dxd + dxf * f_out
                df = jnp.sum(dxf * xd, axis=1, keepdims=True) * f_out
                dac = dac - df
                s_last = jnp.sum(df, axis=0, keepdims=True)
                ss = jnp.sum(jnp.where(hrow, dsp * sp, 0.0), axis=1, keepdims=True)
                s_last = s_last + e_last * jnp.sum(ss, axis=0, keepdims=True)
                dac = dac + jnp.where(rowl == l - 1, s_last, 0.0)
                ds_new = ds_new + _dot_tn(dye, cm)
                decay = jnp.where(hrow, e_last, decay)
                dxpair = dxpair + dxd * dtc + dym * dh
                dt_x = dt_x + jnp.where(lane == DT_LANE + h, jnp.sum(dxd * xm, axis=1, keepdims=True), 0.0)
                dsum = jnp.sum(jnp.sum(dym * xm, axis=1, keepdims=True), axis=0, keepdims=True)
                dd_row = dd_row + jnp.where(lane1 == DT_LANE + h, dsum, 0.0)
                da_col = da_col + jnp.where(lane == DT_LANE + h, dac, 0.0)
                da_row = da_row + jnp.where(row == DT_LANE + h, dar, 0.0)
            dstates[j] = dsp * decay + ds_new
            dxbc_ref[r, LANE * j:LANE * (j + 1)] = dxpair
        for g in range(2):
            dxbc_ref[r, D_SSD + LANE * g:D_SSD + LANE * (g + 1)] = db[g]
            dxbc_ref[r, D_SSD + SSD_BC + LANE * g:D_SSD + SSD_BC + LANE * (g + 1)] = dc[g]
        dla = _dot_hi_tn(tri, da_col + da_row.T)
        ddt = dt_x + dla * a_row
        dpre = ddt * _sigmoid(pre)
        dtm = (lane >= DT_LANE) & (lane < DT_LANE + SSD_HEADS)
        dtail_ref[r, :] = jnp.where(dtm, dpre, 0.0).astype(MXU)
        dtm1 = (lane1 >= DT_LANE) & (lane1 < DT_LANE + SSD_HEADS)
        dsc_ref[0:1, :] += jnp.where(dtm1, jnp.sum(dpre, axis=0, keepdims=True), 0.0)
        dsc_ref[1:2, :] += jnp.where(dtm1, jnp.sum(dla * dt, axis=0, keepdims=True) * a_row, 0.0)
        dsc_ref[2:3, :] += dd_row
        return dstates

    rev = lambda c: nc // cps - 1 - c
    return _call_after(
        dep, body, (xbc, proj, sc, states, dy), name="ssd_bwd", grid=(nc // cps,),
        in_specs=[pl.BlockSpec((cps * l, N_XBC), lambda c: (rev(c), 0)),
                  pl.BlockSpec((cps * l, LANE), lambda c: (rev(c), O_TAIL // LANE)), _full((8, LANE)),
                  pl.BlockSpec((cps, 3, LANE, LANE), lambda c: (rev(c), 0, 0, 0)),
                  pl.BlockSpec((cps * l, D_SSD), lambda c: (rev(c), 0))],
        out_specs=[pl.BlockSpec((cps * l, N_XBC), lambda c: (rev(c), 0)), pl.BlockSpec((cps * l, LANE), lambda c: (rev(c), 0)),
                   _full((8, LANE))],
        out_shape=[_sds((s, N_XBC)), _sds((s, LANE), MXU), _sds((8, LANE))],
        scratch_shapes=[pltpu.VMEM((3, LANE, LANE), F32)],
        compiler_params=_params(("arbitrary",)),
    )


def _sconv_bwd(proj, w, b, dxbc, dep=None):
    s = proj.shape[0]

    def body(u_ref, w_ref, b_ref, d_ref, du_ref, dw_ref, db_ref):
        u = u_ref[...]
        wv = w_ref[...]
        dpre = d_ref[...] * _dsilu(_sconv_pre(u, wv, b_ref[...]))
        ahead = [_shift_up(dpre, j) for j in range(4)]
        du_ref[...] = (wv[3:4, :] * ahead[0] + wv[2:3, :] * ahead[1] + wv[1:2, :] * ahead[2]
                       + wv[0:1, :] * ahead[3]).astype(MXU)
        for k in range(4):
            dw_ref[k:k + 1, :] = jnp.sum(ahead[3 - k] * u, axis=0, keepdims=True)
        db_ref[...] = jnp.sum(dpre, axis=0, keepdims=True)

    blk = pl.BlockSpec((s, LANE), lambda j: (0, j))
    return _call_after(
        dep, body, (proj, w, b, dxbc), name="sconv_bwd", grid=(N_XBC // LANE,),
        in_specs=[_col(s, O_XBC), pl.BlockSpec((4, LANE), lambda j: (0, j)), pl.BlockSpec((1, LANE), lambda j: (0, j)), blk],
        out_specs=[blk, pl.BlockSpec((4, LANE), lambda j: (0, j)), pl.BlockSpec((1, LANE), lambda j: (0, j))],
        out_shape=[_sds((s, N_XBC), MXU), _sds((4, N_XBC)), _sds((1, N_XBC))],
        compiler_params=_params(("parallel",)),
    )


def _conva_bwd(proj, w, dya, dep=None):
    s = proj.shape[0]

    def body(h_ref, b_ref, c_ref, z_ref, w_ref, d_ref, da_ref, dw_ref):
        ah, ab, acv, az = h_ref[...], b_ref[...], c_ref[...], z_ref[...]
        wv = w_ref[...]
        u = acv * ah
        cv = wv[2:3, :] * u + wv[1:2, :] * _shift_down(u, 1) + wv[0:1, :] * _shift_down(u, 2)
        dy = d_ref[...]
        sz = _silu(az)
        da_ref[1] = (dy * cv * sz).astype(MXU)
        da_ref[3] = (dy * ab * cv * _dsilu(az)).astype(MXU)
        dcv = dy * ab * sz
        ahead = [_shift_up(dcv, j) for j in range(3)]
        du = wv[2:3, :] * ahead[0] + wv[1:2, :] * ahead[1] + wv[0:1, :] * ahead[2]
        da_ref[0] = (du * acv).astype(MXU)
        da_ref[2] = (du * ah).astype(MXU)
        for k in range(3):
            dw_ref[k:k + 1, :] = jnp.sum(ahead[2 - k] * u, axis=0, keepdims=True)

    return _call_after(
        dep, body, (proj, proj, proj, proj, w, dya), name="conva_bwd", grid=(D_CONV_A // LANE,),
        in_specs=[_col(s, O_AH), _col(s, O_AB), _col(s, O_AC), _col(s, O_AZ), pl.BlockSpec((3, LANE), lambda j: (0, j)),
                  pl.BlockSpec((s, LANE), lambda j: (0, j))],
        out_specs=[pl.BlockSpec((4, s, LANE), lambda j: (0, 0, j)), pl.BlockSpec((3, LANE), lambda j: (0, j))],
        out_shape=[_sds((4, s, D_CONV_A), MXU), _sds((3, D_CONV_A))],
        compiler_params=_params(("parallel",)),
    )


def _mla_prep_bwd(dq, dk, dv, proj, qn, kvn, rq, rkv, gq, gkv, wq, wkv, cos, sin):
    s = proj.shape[0]
    ts = _tile(s)
    nh = MLA_HEADS

    def body(dq_ref, dk_ref, dv_ref, cqa_ref, ckv_ref, qn_ref, kvn_ref, rq_ref, rkv_ref, gq_ref, gkv_ref,
             wq_ref, wkv_ref, cos_ref, sin_ref, dcqa_ref, dckv_ref, dtail_ref, dwq_ref, dwkv_ref, dgq_ref, dgkv_ref):
        @pl.when(pl.program_id(0) == 0)
        def _():
            dwq_ref[...] = jnp.zeros_like(dwq_ref)
            dwkv_ref[...] = jnp.zeros_like(dwkv_ref)
            dgq_ref[...] = jnp.zeros_like(dgq_ref)
            dgkv_ref[...] = jnp.zeros_like(dgkv_ref)

        cosv = cos_ref[...]
        sinv = sin_ref[...]
        lane = _iota((ts, LANE), 1)
        rope_lanes = (lane >= ROPE_LANE) & (lane < ROPE_LANE + QK_ROPE)

        def unrope(gr):
            return gr * cosv + _rope_swap(gr * sinv)

        dqs, dks, dvs = [], [], []
        dkr = jnp.zeros((ts, LANE), F32)
        for h in range(nh):
            dqs.append(unrope(dq_ref[h] * ATT_SCALE).astype(MXU))
            dkh = dk_ref[h]
            dks.append(jnp.where(lane < QK_NOPE, dkh, 0.0).astype(MXU))
            dkr = dkr + jnp.where(rope_lanes, dkh, 0.0)
            dvs.append(dv_ref[h].astype(MXU))
        dtail_ref[...] = pltpu.roll(jnp.where(rope_lanes, unrope(dkr), 0.0), ROPE_LANE, 1).astype(MXU)
        dq_all = jnp.concatenate(dqs, axis=1)
        dkv_all = jnp.concatenate(dks + dvs, axis=1)
        dwq_ref[...] += _dot_tn(dq_all, qn_ref[...])
        dwkv_ref[...] += _dot_tn(dkv_all, kvn_ref[...])
        dcqa, dgq = _rms_bwd(_dot(dq_all, wq_ref[...]), cqa_ref[...], rq_ref[...], gq_ref[...])
        dckv, dgkv = _rms_bwd(_dot(dkv_all, wkv_ref[...]), ckv_ref[...], rkv_ref[...], gkv_ref[...])
        dcqa_ref[...] = dcqa.astype(MXU)
        dckv_ref[...] = dckv.astype(MXU)
        dgq_ref[...] += dgq
        dgkv_ref[...] += dgkv

    head = pl.BlockSpec((nh, ts, LANE), lambda i: (0, i, 0))
    return pl.pallas_call(
        body, name="mla_prep_bwd", grid=(s // ts,),
        in_specs=[head, head, head,
                  pl.BlockSpec((ts, Q_LORA), lambda i: (i, O_CQA // Q_LORA)),
                  pl.BlockSpec((ts, KV_LORA), lambda i: (i, O_CKV // KV_LORA)),
                  _row(ts, Q_LORA), _row(ts, KV_LORA), _row(ts, 1), _row(ts, 1),
                  _full((1, Q_LORA)), _full((1, KV_LORA)), _full((nh * LANE, Q_LORA)), _full((2 * nh * LANE, KV_LORA)),
                  _row(ts, LANE), _row(ts, LANE)],
        out_specs=[_row(ts, Q_LORA), _row(ts, KV_LORA), _row(ts, LANE), _full((nh * LANE, Q_LORA)),
                   _full((2 * nh * LANE, KV_LORA)), _full((1, Q_LORA)), _full((1, KV_LORA))],
        out_shape=[_sds((s, Q_LORA), MXU), _sds((s, KV_LORA), MXU), _sds((s, LANE), MXU), _sds((nh * LANE, Q_LORA)),
                   _sds((2 * nh * LANE, KV_LORA)), _sds((1, Q_LORA)), _sds((1, KV_LORA))],
        compiler_params=_params(("arbitrary",)),
    )(dq, dk, dv, proj, proj, qn, kvn, rq, rkv, gq, gkv, wq, wkv, cos, sin)


def _inproj_bwd(da4, dsz, dxbc_in, dcqa, dckv, dcz, dtail_a, dtail_b, w, x, rstd, g, dout, dep=None):
    s = x.shape[0]
    ts = _tile(s)

    def body(da_ref, dsz_ref, dxbc_ref, dcqa_ref, dckv_ref, dcz_ref, dta_ref, dtb_ref, w_ref, x_ref, r_ref, g_ref, dout_ref,
             dproj_ref, dx_ref, dg_ref):
        @pl.when(pl.program_id(0) == 0)
        def _():
            dg_ref[...] = jnp.zeros_like(dg_ref)

        dproj = jnp.concatenate(
            [da_ref[0], da_ref[1], da_ref[2], da_ref[3], dxbc_ref[...], dsz_ref[...], dcqa_ref[...], dckv_ref[...],
             dcz_ref[...], dta_ref[...] + dtb_ref[...]], axis=1)
        dproj_ref[...] = dproj
        dh = _dot_nt(dproj, w_ref[...])
        dx, dg = _rms_bwd(dh, x_ref[...], r_ref[...], g_ref[...])
        dx_ref[...] = dout_ref[...] + dx
        dg_ref[...] += dg

    return _call_after(
        dep, body, (da4, dsz, dxbc_in, dcqa, dckv, dcz, dtail_a, dtail_b, w, x, rstd, g, dout), name="inproj_bwd", grid=(s // ts,),
        in_specs=[pl.BlockSpec((4, ts, D_CONV_A), lambda i: (0, i, 0)), _row(ts, D_SSD), _row(ts, N_XBC), _row(ts, Q_LORA),
                  _row(ts, KV_LORA), _row(ts, D_MLA), _row(ts, LANE), _row(ts, LANE), _full((D_MODEL, NCOL)),
                  _row(ts, D_MODEL), _row(ts, 1), _full((1, D_MODEL)), _row(ts, D_MODEL)],
        out_specs=[_row(ts, NCOL), _row(ts, D_MODEL), _full((1, D_MODEL))],
        out_shape=[_sds((s, NCOL), MXU), _sds((s, D_MODEL)), _sds((1, D_MODEL))],
        compiler_params=_params(("arbitrary",)),
    )


DWIN_BLOCK = 640


def _dwin(h, dproj, dep=None):
    s = h.shape[0]

    def body(h_ref, d_ref, o_ref):
        o_ref[...] = _dot_tn(h_ref[...], d_ref[...])

    return _call_after(
        dep, body, (h, dproj), name="dwin", grid=(NCOL // DWIN_BLOCK,),
        in_specs=[_full((s, D_MODEL)), pl.BlockSpec((s, DWIN_BLOCK), lambda j: (0, j))],
        out_specs=pl.BlockSpec((D_MODEL, DWIN_BLOCK), lambda j: (0, j)),
        out_shape=_sds((D_MODEL, NCOL)),
        compiler_params=_params(("parallel",)),
    )


def _adamw(ws, gs, ms, vs, whole, layer=None, into=None):
    n = len(ws)
    bc1 = 1.0 - ADAM_B1 ** ADAM_STEP
    bc2 = 1.0 - ADAM_B2 ** ADAM_STEP

    def body(*refs):
        ins, outs = refs[:4 * n], refs[4 * n:]
        for a in range(n):
            w_ref, g_ref, m_ref, v_ref = ins[a], ins[n + a], ins[2 * n + a], ins[3 * n + a]
            gv = g_ref[...]
            mn = ADAM_B1 * m_ref[...] + (1.0 - ADAM_B1) * gv
            vn = ADAM_B2 * v_ref[...] + (1.0 - ADAM_B2) * (gv * gv)
            outs[n + a][...] = mn
            outs[2 * n + a][...] = vn
            outs[a][...] = -ADAM_LR * ((mn / bc1) / (jnp.sqrt(vn / bc2) + ADAM_EPS) + ADAM_WD * w_ref[...])

    if whole:
        grid, blks = (1,), [pl.BlockSpec(w.shape, lambda i, _n=w.ndim: (0,) * _n) for w in ws]
    elif layer is not None:
        assert n == 1
        rows, cols = ws[0].shape[1:]
        grid = (2,)
        blk = pl.BlockSpec((1, rows // 2, cols), lambda k: (layer, k, 0))
        gblk = pl.BlockSpec((1, rows // 2, cols), lambda k: (0, k, 0))

        def body1(w_ref, g_ref, m_ref, v_ref, *rest):
            go_ref, d_ref, mo_ref, vo_ref = rest[-4:]
            go_ref[...] = g_ref[...]
            body(w_ref, g_ref, m_ref, v_ref, d_ref, mo_ref, vo_ref)

        extra = list(into) if into is not None else []
        out = pl.pallas_call(
            body1, name=f"adamw_layer{layer}", grid=grid,
            in_specs=[blk, gblk, blk, blk] + [ANY] * len(extra), out_specs=[blk] * 4, out_shape=[_sds(ws[0].shape)] * 4,
            input_output_aliases={4 + i: i for i in range(len(extra))},
            compiler_params=_params(("parallel",)),
        )(ws[0], gs[0][None], ms[0], vs[0], *extra)
        return list(out)
    else:
        grid = (ws[0].shape[0], 2)
        blks = [pl.BlockSpec((1, w.shape[1] // 2, w.shape[2]), lambda i, k: (i, k, 0)) for w in ws]
    out = pl.pallas_call(
        body, name="adamw", grid=grid,
        in_specs=blks * 4, out_specs=blks * 3, out_shape=[_sds(w.shape) for w in ws] * 3,
        compiler_params=_params(("parallel",) * len(grid)),
    )(*ws, *gs, *ms, *vs)
    return [(out[a], out[n + a], out[2 * n + a]) for a in range(n)]


COL_MOVES = ((0, 0, 1024), (1024, O_SZ, 384), (1408, O_XBC, 896), (2304, O_TAIL + DT_LANE, 6), (2310, O_CQA, 256),
             (2566, O_CKV, 128), (2694, O_TAIL, 32), (2726, O_CZ, 384))


def _move_cols(w, moves, width):
    out = None
    for src, dst, n in moves:
        piece = jnp.pad(w[..., src:src + n], [(0, 0)] * (w.ndim - 1) + [(dst, width - dst - n)])
        out = piece if out is None else out + piece
    return out


def _perm_cols(w):
    return _move_cols(w, COL_MOVES, NCOL)


def _unperm_cols(g):
    return _move_cols(g, [(dst, src, n) for src, dst, n in COL_MOVES], IN_COLS)


def _wq_layout(wt):
    return jnp.pad(wt.reshape(MLA_HEADS, QK_NOPE + QK_ROPE, Q_LORA), ((0, 0), (0, 32), (0, 0))).reshape(MLA_HEADS * LANE, Q_LORA)


def _wq_unlayout(g):
    return g.reshape(MLA_HEADS, LANE, Q_LORA)[:, :QK_NOPE + QK_ROPE].reshape(MLA_HEADS * (QK_NOPE + QK_ROPE), Q_LORA)


def _wkv_layout(wt):
    t = wt.reshape(MLA_HEADS, 2, 64, KV_LORA).transpose(1, 0, 2, 3)
    return jnp.pad(t, ((0, 0), (0, 0), (0, 64), (0, 0))).reshape(2 * MLA_HEADS * LANE, KV_LORA)


def _wkv_unlayout(g):
    t = g.reshape(2, MLA_HEADS, LANE, KV_LORA)[:, :, :64]
    return t.transpose(1, 0, 2, 3).reshape(MLA_HEADS * LANE, KV_LORA)


def _rope_tables(positions):
    inv_freq = ROPE_BASE ** (-jnp.arange(0, QK_ROPE, 2, dtype=F32) / QK_ROPE)
    ang = positions.astype(F32)[:, None] * inv_freq
    cos, sin = jnp.cos(ang), jnp.sin(ang)
    s = positions.shape[0]
    one, zero = jnp.ones((s, ROPE_LANE), F32), jnp.zeros((s, ROPE_LANE), F32)
    cos_t = jnp.concatenate([one, cos, cos, one[:, :32]], axis=1)
    sin_t = jnp.concatenate([zero, -sin, sin, zero[:, :32]], axis=1)
    return cos_t, sin_t


def _ssd_scalars(dt_bias, a_log, d_skip):
    return jnp.pad(jnp.stack([dt_bias, a_log, d_skip]), ((0, 5), (DT_LANE, LANE - DT_LANE - SSD_HEADS)))


def _layer_fwd(x, lw, cos, sin, dep=None, late=None):
    proj, h, rstd = _inproj_fwd(x, lw["norm_g"], lw["w_in"], dep)
    ya = _conva_fwd(proj, lw["conv_a_w"])
    xbc = _sconv_fwd(proj, lw["ssd_conv_w"], lw["ssd_conv_b"])
    y_ssd, states = _ssd_fwd(xbc, proj, lw["sc"])
    if late is not None:
        lw = {**lw, **late(ya, y_ssd)}
    q, k, v, qn, kvn, rq, rkv = _mla_prep_fwd(proj, lw["gq"], lw["gkv"], lw["wq"], lw["wkv"], cos, sin)
    o, lse = _attn_fwd(q, k, v)
    x_out, y = _outproj_fwd(x, proj, ya, y_ssd, o, lw["g_ssd"], lw["w_out"])
    saved = dict(x=x, proj=proj, h=h, rstd=rstd, xbc=xbc, y_ssd=y_ssd, states=states, q=q, k=k, v=v, qn=qn, kvn=kvn,
                 rq=rq, rkv=rkv, o=o, lse=lse, y=y)
    return x_out, saved, lw


def _layer_bwd(dout, lw, sv, cos, sin, rs=None, begin_early=None):
    tok = lambda: None if rs is None else rs["h"]["token"]
    dya, dys, dsz, d_o, dcz, dg_ssd, dw_out = _outproj_bwd(dout, sv["y"], lw["w_out"], sv["proj"], sv["y_ssd"], sv["o"],
                                                            lw["g_ssd"], tok())
    if rs is not None:
        rs = _rs_add_mine(rs, [dya])
    dq, dk, dv = _attn_bwd(sv["q"], sv["k"], sv["v"], sv["o"], d_o, sv["lse"], tok())
    dxbc, dtail_s, dsc = _ssd_bwd(sv["xbc"], sv["proj"], lw["sc"], sv["states"], dys, tok())
    da4, dw_conva = _conva_bwd(sv["proj"], lw["conv_a_w"], dya, tok())
    if rs is not None:
        rs = _rs_add_chips(rs, [dq, dxbc, da4])
    du, dw_sconv, db_sconv = _sconv_bwd(sv["proj"], lw["ssd_conv_w"], lw["ssd_conv_b"], dxbc, tok())
    dcqa, dckv, dtail_m, dwq, dwkv, dgq, dgkv = _mla_prep_bwd(
        dq, dk, dv, sv["proj"], sv["qn"], sv["kvn"], sv["rq"], sv["rkv"], lw["gq"], lw["gkv"], lw["wq"], lw["wkv"], cos, sin)
    early = None if begin_early is None else begin_early(dw_out, dwq, dwkv)
    etok = lambda: None if early is None else early["h"]["token"]
    dproj, dx, dg = _inproj_bwd(da4, dsz, du, dcqa, dckv, dcz, dtail_s, dtail_m, lw["w_in"], sv["x"], sv["rstd"],
                                lw["norm_g"], dout, etok())
    reduced = None if rs is None else _rs_end(rs, [du, dcqa, dx])
    if early is not None:
        early = _rs_add_mine(early, [dx])
    dw_in = _dwin(sv["h"], dproj, etok())
    if early is not None:
        early = _rs_add_chips(early, [dw_in])
    grads = dict(norm_g=dg, w_in=dw_in, conv_a_w=dw_conva, ssd_conv_w=dw_sconv, ssd_conv_b=db_sconv, sc=dsc,
                 g_ssd=dg_ssd, gq=dgq, wq=dwq, gkv=dgkv, wkv=dwkv, w_out=dw_out)
    return dx, grads, reduced, early


ANY = pl.BlockSpec(memory_space=pl.ANY)
N_CHIPS = 4
N_DEV = 8


def _place():
    return lax.axis_index("x"), lax.axis_index("y"), lax.axis_index("c")


HBM_SPEC = pl.BlockSpec(memory_space=pltpu.HBM)
SEM_SPEC = pl.BlockSpec(memory_space=pltpu.SEMAPHORE)
PAYLOAD = jnp.bfloat16


def _hbm(a):
    return pltpu.with_memory_space_constraint(a, pltpu.HBM)


def _run_plan(plan, srcs, lands, send_sems, recv_sems, start, wait):
    copies = plan(srcs, lands)
    if start:
        for i, (src, dst, _, to) in enumerate(copies):
            pltpu.make_async_remote_copy(src_ref=src, dst_ref=dst, send_sem=send_sems.at[i], recv_sem=recv_sems.at[i],
                                         device_id=to, device_id_type=MESH_T).start()
    if wait:
        for i, (src, _, arrives, to) in enumerate(copies):
            cp = pltpu.make_async_remote_copy(src_ref=src, dst_ref=arrives, send_sem=send_sems.at[i],
                                              recv_sem=recv_sems.at[i], device_id=to, device_id_type=MESH_T)
            cp.wait_send()
            cp.wait_recv()


def _exchange_start(name, plan, n_copies, srcs, land_shapes, deps):
    ns, nl = len(srcs), len(land_shapes)
    n_in = ns + nl + len(deps)

    def body(*refs):
        send_sems, recv_sems = refs[n_in], refs[n_in + 1]
        token = refs[-1]
        _run_plan(plan, refs[:ns], refs[ns:ns + nl], send_sems, recv_sems, True, False)
        token[...] = jnp.zeros_like(token)

    thru = [pltpu.HBM(a.shape, a.dtype) for a in srcs] + [pltpu.HBM(a.shape, a.dtype) for a in land_shapes]
    outs = pl.pallas_call(
        body, name=name,
        out_shape=(pltpu.SemaphoreType.DMA((n_copies,)), pltpu.SemaphoreType.DMA((n_copies,)), *thru, _sds((8, LANE))),
        in_specs=[HBM_SPEC] * (ns + nl) + [ANY] * len(deps),
        out_specs=(SEM_SPEC, SEM_SPEC, *[HBM_SPEC] * (ns + nl), pl.BlockSpec(memory_space=pltpu.VMEM)),
        input_output_aliases={i: 2 + i for i in range(ns + nl)},
        compiler_params=pltpu.CompilerParams(has_side_effects=pltpu.SideEffectType.DATAFLOW_SIDE_EFFECTING),
    )(*[_hbm(a) for a in srcs], *[_hbm(lax.empty(a.shape, a.dtype)) for a in land_shapes], *deps)
    return (outs[0], outs[1]), list(outs[2:2 + ns]), list(outs[2 + ns:2 + ns + nl]), outs[-1]


def _exchange_wait(name, plan, sems, srcs, lands, after):
    ns, nl = len(srcs), len(lands)

    def body(*refs):
        _run_plan(plan, refs[:ns], refs[ns:ns + nl], refs[ns + nl], refs[ns + nl + 1], False, True)

    outs = pl.pallas_call(
        body, name=name,
        out_shape=[pltpu.HBM(a.shape, a.dtype) for a in list(srcs) + list(lands)],
        in_specs=[HBM_SPEC] * (ns + nl) + [SEM_SPEC, SEM_SPEC] + [ANY] * len(after), out_specs=[HBM_SPEC] * (ns + nl),
        input_output_aliases={i: i for i in range(ns + nl)},
        compiler_params=pltpu.CompilerParams(has_side_effects=pltpu.SideEffectType.DATAFLOW_SIDE_EFFECTING),
    )(*srcs, *lands, sems[0], sems[1], *after)
    return list(outs[:ns]), list(outs[ns:])


def _xchg_begin(name, plan, n_copies, srcs, land_shapes, deps=()):
    sems, srcs_t, lands_t, token = _exchange_start(name + "_start", plan, n_copies, srcs, land_shapes, list(deps))
    return dict(name=name, plan=plan, sems=sems, srcs=srcs_t, lands=lands_t, token=token)


def _xchg_end(h, after):
    return _exchange_wait(h["name"] + "_wait", h["plan"], h["sems"], h["srcs"], h["lands"], after)


def _other_chips():
    x, y, c = _place()
    return [(1 - x, y), (x, 1 - y), (1 - x, 1 - y)]


def _gather_plan(srcs, lands):
    x, y, c = _place()
    me = 2 * x + y
    return [(srcs[a], lands[a].at[me], lands[a].at[2 * cx + cy], (cx, cy, c))
            for (cx, cy) in _other_chips() for a in range(len(srcs))]


def _gather_begin(shards, tag, deps=()):
    shapes = [_sds((N_CHIPS,) + a.shape, a.dtype) for a in shards]
    return _xchg_begin(f"gather_{tag}", _gather_plan, 3 * len(shards), shards, shapes, deps)


def _gather_end(h, after):
    shards, lands = _xchg_end(h, after)
    me = 2 * lax.axis_index("x") + lax.axis_index("y")
    return [lax.dynamic_update_index_in_dim(g, s, me, 0) for g, s in zip(lands, shards)]


def _gather_half_plan(srcs, lands):
    x, y, c = _place()
    me = 2 * x + y
    out = []
    for (cx, cy) in _other_chips():
        out.append((srcs[0].at[c], lands[0].at[me, c], lands[0].at[2 * cx + cy, c], (cx, cy, c)))
        out += [(srcs[a], lands[a].at[me], lands[a].at[2 * cx + cy], (cx, cy, c)) for a in range(1, len(srcs))]
    return out


def _forward_plan(bufs, _):
    x, y, c = _place()
    return [(bufs[0].at[2 * cx + cy, c], bufs[0].at[2 * cx + cy, c], bufs[0].at[2 * cx + cy, 1 - c], (x, y, 1 - c))
            for (cx, cy) in _other_chips()]


def _swap_plan(srcs, lands):
    x, y, c = _place()
    return [(srcs[a].at[:, 1 - c], lands[a], lands[a], (x, y, 1 - c)) for a in range(len(srcs))]


def _chips_plan(srcs, lands):
    x, y, c = _place()
    me = 2 * x + y
    return [(srcs[a].at[2 * cx + cy], lands[a].at[me], lands[a].at[2 * cx + cy], (cx, cy, c))
            for (cx, cy) in _other_chips() for a in range(len(srcs))]


def _share_plan(srcs, lands):
    x, y, c = _place()
    return [(srcs[a], lands[a].at[c], lands[a].at[1 - c], (x, y, 1 - c)) for a in range(len(srcs))]


def _allreduce_small(slab, dep=None):
    r = slab.shape[0]

    def body(s_ref, o_ref, gath, send_sems, recv_sems):
        x, y, c = _place()
        me = 4 * x + 2 * y + c
        gath[me] = s_ref[...]
        cps = []
        for rel in range(1, N_DEV):
            px = 1 - x if rel & 4 else x
            py = 1 - y if rel & 2 else y
            pc = 1 - c if rel & 1 else c
            cp = pltpu.make_async_remote_copy(src_ref=s_ref, dst_ref=gath.at[me], send_sem=send_sems.at[rel - 1],
                                              recv_sem=recv_sems.at[rel - 1], device_id=(px, py, pc), device_id_type=MESH_T)
            cp.start()
            cps.append(cp)
        for cp in cps:
            cp.wait()
        acc = gath[0]
        for d in range(1, N_DEV):
            acc = acc + gath[d]
        o_ref[...] = acc

    vm = pl.BlockSpec(memory_space=pltpu.VMEM)
    return _call_after(
        dep, body, (slab,), name="allreduce_small", in_specs=[vm], out_specs=vm, out_shape=_sds((r, LANE)),
        scratch_shapes=[pltpu.VMEM((N_DEV, r, LANE), F32), pltpu.SemaphoreType.DMA((N_DEV - 1,)),
                        pltpu.SemaphoreType.DMA((N_DEV - 1,))],
    )


def _add_mine(g4s, recvs, half):
    n = len(g4s)

    def body(h_ref, *refs):
        for g_ref, r_ref, o_ref in zip(refs[:n], refs[n:2 * n], refs[2 * n:]):
            o_ref[0] = (g_ref[0, 0] + r_ref[0]).astype(o_ref.dtype)

    dims = [g.shape[2:] for g in g4s]
    return pl.pallas_call(
        body, name="add_mine",
        grid_spec=pltpu.PrefetchScalarGridSpec(
            num_scalar_prefetch=1, grid=(N_CHIPS,),
            in_specs=[pl.BlockSpec((1, 1) + d, lambda j, h: (j, h[0], 0, 0)) for d in dims]
            + [pl.BlockSpec((1,) + d, lambda j, h: (j, 0, 0)) for d in dims],
            out_specs=[pl.BlockSpec((1,) + d, lambda j, h: (j, 0, 0)) for d in dims]),
        out_shape=[_sds((N_CHIPS,) + d, PAYLOAD) for d in dims],
        compiler_params=_params(("parallel",)),
    )(half, *g4s, *recvs)


def _add_chips(es, ps, me):
    n = len(es)

    def body(m_ref, *refs):
        for e_ref, p_ref, o_ref in zip(refs[:n], refs[n:2 * n], refs[2 * n:]):
            own = p_ref[0].astype(F32)
            acc = None
            for s in range(N_CHIPS):
                t = jnp.where(m_ref[0] == s, own, e_ref[s].astype(F32))
                acc = t if acc is None else acc + t
            o_ref[...] = acc

    dims = [e.shape[1:] for e in es]
    return pl.pallas_call(
        body, name="add_chips",
        grid_spec=pltpu.PrefetchScalarGridSpec(
            num_scalar_prefetch=1, grid=(1,),
            in_specs=[pl.BlockSpec((N_CHIPS,) + d, lambda i, m: (0, 0, 0)) for d in dims]
            + [pl.BlockSpec((1,) + d, lambda i, m: (m[0], 0, 0)) for d in dims],
            out_specs=[pl.BlockSpec(d, lambda i, m: (0, 0)) for d in dims]),
        out_shape=[_sds(d) for d in dims],
        compiler_params=_params(("arbitrary",)),
    )(me, *es, *ps)


def _rs_begin(gs, tag, deps=()):
    g4 = [g.reshape(N_CHIPS, 2, g.shape[0] // (2 * N_CHIPS), g.shape[1]) for g in gs]
    h = _xchg_begin(f"rs_swap_{tag}", _swap_plan, len(gs), g4, [_sds((N_CHIPS,) + g.shape[2:]) for g in g4], deps)
    return dict(h=h, tag=tag, shapes=[g.shape for g in gs])


def _rs_add_mine(st, after):
    g4, recv = _xchg_end(st["h"], after)
    half = jnp.reshape(lax.axis_index("c"), (1,)).astype(jnp.int32)
    ps = _add_mine(g4, recv, half)
    st["h"] = _xchg_begin(f"rs_chips_{st['tag']}", _chips_plan, 3 * len(ps), ps, [_sds(p.shape, p.dtype) for p in ps])
    return st


def _rs_add_chips(st, after):
    ps, es = _xchg_end(st["h"], after)
    me = jnp.reshape(2 * lax.axis_index("x") + lax.axis_index("y"), (1,)).astype(jnp.int32)
    fs = _add_chips(es, ps, me)
    st["h"] = _xchg_begin(f"rs_share_{st['tag']}", _share_plan, len(fs), fs, [_sds((2,) + f.shape) for f in fs])
    return st


def _rs_end(st, after):
    fs, ss = _xchg_end(st["h"], after)
    c = lax.axis_index("c")
    return [lax.dynamic_update_index_in_dim(s, f, c, 0).reshape(shp[0] // N_CHIPS, shp[1])
            for s, f, shp in zip(ss, fs, st["shapes"])]


WEIGHTS = ["norm_g", "w_in", "conv_a_w", "ssd_conv_w", "ssd_conv_b", "ssd_dt_bias", "ssd_a_log", "ssd_d", "ssd_norm_g",
           "mla_q_norm_g", "w_qb", "mla_kv_norm_g", "w_kvb", "w_out", "final_norm_g"]
BIG = ["w_in", "w_qb", "w_kvb", "w_out"]
SLAB_ROWS = 128


def _to_slab(parts, rows):
    flat = jnp.concatenate([p.reshape(-1) for p in parts])
    return jnp.pad(flat, (0, rows * LANE - flat.shape[0])).reshape(rows, LANE)


def _from_slab(slab, shapes):
    flat = slab.reshape(-1)
    out, off = [], 0
    for shp in shapes:
        n = int(np.prod(shp))
        out.append(flat[off:off + n].reshape(shp))
        off += n
    return out


def kernel(x, positions, norm_g, w_in, conv_a_w, ssd_conv_w, ssd_conv_b, ssd_dt_bias, ssd_a_log, ssd_d, ssd_norm_g, mla_q_norm_g, w_qb, mla_kv_norm_g, w_kvb, w_out, final_norm_g, loss_target, m_norm_g, m_w_in, m_conv_a_w, m_ssd_conv_w, m_ssd_conv_b, m_ssd_dt_bias, m_ssd_a_log, m_ssd_d, m_ssd_norm_g, m_mla_q_norm_g, m_w_qb, m_mla_kv_norm_g, m_w_kvb, m_w_out, m_final_norm_g, v_norm_g, v_w_in, v_conv_a_w, v_ssd_conv_w, v_ssd_conv_b, v_ssd_dt_bias, v_ssd_a_log, v_ssd_d, v_ssd_norm_g, v_mla_q_norm_g, v_w_qb, v_mla_kv_norm_g, v_w_kvb, v_w_out, v_final_norm_g):
    w = dict(norm_g=norm_g, w_in=w_in, conv_a_w=conv_a_w, ssd_conv_w=ssd_conv_w, ssd_conv_b=ssd_conv_b,
             ssd_dt_bias=ssd_dt_bias, ssd_a_log=ssd_a_log, ssd_d=ssd_d, ssd_norm_g=ssd_norm_g, mla_q_norm_g=mla_q_norm_g,
             w_qb=w_qb, mla_kv_norm_g=mla_kv_norm_g, w_kvb=w_kvb, w_out=w_out, final_norm_g=final_norm_g)
    mom = dict(norm_g=m_norm_g, w_in=m_w_in, conv_a_w=m_conv_a_w, ssd_conv_w=m_ssd_conv_w, ssd_conv_b=m_ssd_conv_b,
               ssd_dt_bias=m_ssd_dt_bias, ssd_a_log=m_ssd_a_log, ssd_d=m_ssd_d, ssd_norm_g=m_ssd_norm_g,
               mla_q_norm_g=m_mla_q_norm_g, w_qb=m_w_qb, mla_kv_norm_g=m_mla_kv_norm_g, w_kvb=m_w_kvb, w_out=m_w_out,
               final_norm_g=m_final_norm_g)
    var = dict(norm_g=v_norm_g, w_in=v_w_in, conv_a_w=v_conv_a_w, ssd_conv_w=v_ssd_conv_w, ssd_conv_b=v_ssd_conv_b,
               ssd_dt_bias=v_ssd_dt_bias, ssd_a_log=v_ssd_a_log, ssd_d=v_ssd_d, ssd_norm_g=v_ssd_norm_g,
               mla_q_norm_g=v_mla_q_norm_g, w_qb=v_w_qb, mla_kv_norm_g=v_mla_kv_norm_g, w_kvb=v_w_kvb, w_out=v_w_out,
               final_norm_g=v_final_norm_g)
    chip = 2 * lax.axis_index("x") + lax.axis_index("y")

    def early_shard(l, zero):
        pack = jnp.pad(conv_a_w[l], ((0, 5), (0, 192))) + jnp.pad(ssd_conv_w[l], ((3, 1), (0, 32)))
        return [(_perm_cols(w_in[l]) + zero).astype(MXU), pack + zero]

    def late_shard(l, zero):
        return [(w_out[l] + zero).astype(MXU), (w_qb[l].T + zero).astype(MXU), (w_kvb[l].T + zero).astype(MXU)]

    def early_weights(l, gathered):
        g_in, g_conv = gathered
        return dict(
            norm_g=norm_g[l][None], w_in=g_in.reshape(D_MODEL, NCOL),
            conv_a_w=jnp.concatenate([g_conv[j, 0:3, 0:64] for j in range(N_CHIPS)], axis=1),
            ssd_conv_w=jnp.concatenate([g_conv[j, 3:7, 0:224] for j in range(N_CHIPS)], axis=1),
            ssd_conv_b=ssd_conv_b[l][None], sc=_ssd_scalars(ssd_dt_bias[l], ssd_a_log[l], ssd_d[l]),
            g_ssd=ssd_norm_g[l][None], gq=mla_q_norm_g[l][None], gkv=mla_kv_norm_g[l][None])

    def late_weights(gathered):
        g_out, g_qb, g_kvb = gathered
        return dict(wq=_wq_layout(g_qb.reshape(MLA_HEADS * 96, Q_LORA)), wkv=_wkv_layout(g_kvb.reshape(MLA_HEADS * LANE, KV_LORA)),
                    w_out=g_out.reshape(D_MODEL, D_MODEL))

    def late_grads(dw_out, dwq, dwkv):
        wq = jnp.pad(_wq_unlayout(dwq).reshape(N_CHIPS, 144, Q_LORA), ((0, 0), (0, 16), (0, 0)))
        return [dw_out, wq.reshape(N_CHIPS * 160, Q_LORA), _wkv_unlayout(dwkv)]

    def large_grads(g):
        return [g["w_in"]] + late_grads(g["w_out"], g["wq"], g["wkv"])

    w_in0, pack0 = early_shard(0, 0.0)
    half = w_in0.shape[0] // 2
    gather_a0 = _xchg_begin("gather_a0", _gather_half_plan, 6, [w_in0.reshape(2, half, NCOL), pack0],
                            [_sds((N_CHIPS, 2, half, NCOL), MXU), _sds((N_CHIPS,) + pack0.shape)])
    zero = gather_a0["token"][0, 0]
    cos, sin = _rope_tables(positions[0] + zero.astype(jnp.int32))
    late0, shards1 = late_shard(0, zero), early_shard(1, zero) + late_shard(1, zero)
    opt_in = {nm: [w[nm], mom[nm], var[nm]] for nm in BIG}
    opt_in["w_in"] = [w["w_in"], mom["w_in"] + zero, var["w_in"] + zero]
    mine0, (g_in0, g_conv0) = _xchg_end(gather_a0, [cos, sin] + late0 + shards1 + opt_in["w_in"][1:])
    forward_a0 = _xchg_begin("forward_a0", _forward_plan, 3, [g_in0], [])
    gather_b0 = _gather_begin(late0, "b0", [forward_a0["token"]])
    gather_1 = _gather_begin(shards1, "1", [gather_b0["token"]])
    (g_in0,), _ = _xchg_end(forward_a0, [gather_1["token"]])
    lw0 = early_weights(0, [lax.dynamic_update_index_in_dim(g, s_, chip, 0) for g, s_ in zip((g_in0, g_conv0), mine0)])
    x1, sv0, lw0 = _layer_fwd(x[0], lw0, cos, sin, gather_1["token"],
                              lambda ya, y_ssd: late_weights(_gather_end(gather_b0, [ya, y_ssd])))
    g1 = _gather_end(gather_1, [x1])
    x2, sv1, lw1 = _layer_fwd(x1, {**early_weights(1, g1[:2]), **late_weights(g1[2:])}, cos, sin)
    dx, dgf, loss = _loss_head(x2, final_norm_g[None], loss_target[0])

    dx, lg1, _, _ = _layer_bwd(dx, lw1, sv1, cos, sin)
    grad_x, lg0, red1, rs0_late = _layer_bwd(dx, lw0, sv0, cos, sin, _rs_begin(large_grads(lg1), 1),
                                             lambda *g: _rs_begin(late_grads(*g), "0l"))
    rs0 = _rs_begin([lg0["w_in"]], 0, [rs0_late["h"]["token"]])
    lg = [lg0, lg1]
    grad = {}

    small_names = ["norm_g", "conv_a_w", "ssd_conv_w", "ssd_conv_b", "sc", "g_ssd", "gq", "gkv"]
    parts = [loss[0, 0:1], dgf]
    for l in range(DEPTH):
        parts += [lg[l][nm][:3, DT_LANE:DT_LANE + SSD_HEADS] if nm == "sc" else lg[l][nm] for nm in small_names]
    shapes = [(1,), (D_MODEL,)] + [(D_MODEL,), (3, D_CONV_A), (4, N_XBC), (N_XBC,), (3, SSD_HEADS), (D_SSD,), (Q_LORA,), (KV_LORA,)] * DEPTH
    red_slab = _allreduce_small(_to_slab(parts, SLAB_ROWS), rs0["h"]["token"])
    rs0 = _rs_add_mine(rs0, [red_slab])
    red = _from_slab(red_slab + rs0["h"]["token"][0, 0], shapes)
    loss_out = red[0][0]
    grad["final_norm_g"] = red[1]
    per = [red[2 + 8 * l:10 + 8 * l] for l in range(DEPTH)]
    grad["norm_g"] = jnp.stack([per[l][0] for l in range(DEPTH)])
    grad["conv_a_w"] = lax.dynamic_slice_in_dim(jnp.stack([per[l][1] for l in range(DEPTH)]), chip * 64, 64, axis=2)
    grad["ssd_conv_w"] = lax.dynamic_slice_in_dim(jnp.stack([per[l][2] for l in range(DEPTH)]), chip * 224, 224, axis=2)
    grad["ssd_conv_b"] = jnp.stack([per[l][3] for l in range(DEPTH)])
    grad["ssd_dt_bias"] = jnp.stack([per[l][4][0] for l in range(DEPTH)])
    grad["ssd_a_log"] = jnp.stack([per[l][4][1] for l in range(DEPTH)])
    grad["ssd_d"] = jnp.stack([per[l][4][2] for l in range(DEPTH)])
    grad["ssd_norm_g"] = jnp.stack([per[l][5] for l in range(DEPTH)])
    grad["mla_q_norm_g"] = jnp.stack([per[l][6] for l in range(DEPTH)])
    grad["mla_kv_norm_g"] = jnp.stack([per[l][7] for l in range(DEPTH)])

    delta, new_m, new_v = {}, {}, {}
    small = [nm for nm in WEIGHTS if nm not in BIG]
    row2 = lambda a: a[None] if a.ndim == 1 else a
    small_out = _adamw(*[[row2(a[nm]) for nm in small] for a in (w, grad, mom, var)], whole=True)
    for nm, (dv, mv, vv) in zip(small, small_out):
        delta[nm], new_m[nm], new_v[nm] = [a.reshape(w[nm].shape) for a in (dv, mv, vv)]

    r_out, r_qb, r_kvb = [jnp.stack([a, b]) for a, b in zip(_rs_end(rs0_late, [red_slab]), red1[1:])]
    grad.update(w_out=r_out, w_qb=jnp.swapaxes(r_qb[:, :144], 1, 2), w_kvb=jnp.swapaxes(r_kvb, 1, 2))
    late = [nm for nm in BIG if nm != "w_in"]
    late_out = _adamw([opt_in[nm][0] for nm in late], [grad[nm] for nm in late], [opt_in[nm][1] for nm in late],
                      [opt_in[nm][2] for nm in late], whole=False)
    for nm, (dv, mv, vv) in zip(late, late_out):
        delta[nm], new_m[nm], new_v[nm] = dv, mv, vv
    w_in_opt = [[a] for a in opt_in["w_in"]]
    w_in_l1 = _adamw(w_in_opt[0], [_unperm_cols(red1[0])], w_in_opt[1], w_in_opt[2], whole=False, layer=1)

    shadow_work = [a for row in small_out + late_out for a in row] + [grad[nm] for nm in small] + w_in_l1
    r_in0, = _rs_end(_rs_add_chips(rs0, shadow_work), [])
    grad["w_in"], delta["w_in"], new_m["w_in"], new_v["w_in"] = _adamw(
        w_in_opt[0], [_unperm_cols(r_in0)], w_in_opt[1], w_in_opt[2], whole=False, layer=0, into=w_in_l1)

    return (loss_out, grad_x[None], *[grad[nm] for nm in WEIGHTS], *[delta[nm] for nm in WEIGHTS],
            *[new_m[nm] for nm in WEIGHTS], *[new_v[nm] for nm in WEIGHTS])
```

```python
import functools
import math

import numpy as np
import jax
import jax.numpy as jnp
from jax import lax
from jax.experimental import pallas as pl
from jax.experimental.pallas import tpu as pltpu

F32 = jnp.float32
MXU = jnp.bfloat16

D_MODEL = 1024
DEPTH = 2
D_CONV_A = 256
D_SSD = 384
SSD_HEADS = 6
SSD_BC = 256
SSD_CHUNK = 128
SSD_CHUNKS_PER_STEP = 4
SSD_NORM_EPS = 1e-5
MLA_HEADS = 6
Q_LORA = 256
KV_LORA = 128
QK_NOPE = 64
QK_ROPE = 32
V_DIM = 64
D_MLA = 384
ROPE_BASE = 10000.0
NORM_EPS = 1e-6
IN_COLS = 3110
LANE = 128

O_AH, O_AB, O_AC, O_AZ = 0, 256, 512, 768
O_XBC = 1024
O_SZ = 1920
O_CQA = 2304
O_CKV = 2560
O_CZ = 2688
O_TAIL = 3072
NCOL = 3200
N_XBC = D_SSD + 2 * SSD_BC
DT_LANE = 32
ROPE_LANE = 64

ADAM_LR, ADAM_B1, ADAM_B2, ADAM_EPS, ADAM_WD, ADAM_STEP = 0.001, 0.9, 0.999, 1e-08, 0.01, 10

VMEM_LIMIT = 56 * 1024 * 1024
MESH_T = pl.DeviceIdType.MESH


def _dot(a, b):
    return jnp.dot(a.astype(MXU), b.astype(MXU), preferred_element_type=F32)


def _dot_nt(a, b):
    return lax.dot_general(a.astype(MXU), b.astype(MXU), (((1,), (1,)), ((), ())), preferred_element_type=F32)


def _dot_tn(a, b):
    return lax.dot_general(a.astype(MXU), b.astype(MXU), (((0,), (0,)), ((), ())), preferred_element_type=F32)


def _dot_hi(a, b):
    return jnp.dot(a, b, precision=lax.Precision.HIGHEST, preferred_element_type=F32)


def _dot_hi_tn(a, b):
    return lax.dot_general(a, b, (((0,), (0,)), ((), ())), precision=lax.Precision.HIGHEST, preferred_element_type=F32)


def _sigmoid(z):
    return 1.0 / (1.0 + jnp.exp(-z))


def _silu(z):
    return z * _sigmoid(z)


def _dsilu(z):
    s = _sigmoid(z)
    return s * (1.0 + z * (1.0 - s))


def _softplus(z):
    e = jnp.exp(-jnp.abs(z))
    return jnp.maximum(z, 0.0) + jnp.where(e < 1e-3, e * (1.0 - 0.5 * e), jnp.log(1.0 + e))


def _iota(shape, dim):
    return lax.broadcasted_iota(jnp.int32, shape, dim)


def _shift_down(u, k):
    if k == 0:
        return u
    return jnp.where(_iota(u.shape, 0) >= k, pltpu.roll(u, k, 0), 0.0)


def _shift_up(u, k):
    if k == 0:
        return u
    n = u.shape[0]
    return jnp.where(_iota(u.shape, 0) < n - k, pltpu.roll(u, n - k, 0), 0.0)


def _rope_swap(t):
    lane = _iota(t.shape, 1)
    lo = (lane >= ROPE_LANE) & (lane < ROPE_LANE + 16)
    hi = (lane >= ROPE_LANE + 16) & (lane < ROPE_LANE + 32)
    return jnp.where(lo, pltpu.roll(t, LANE - 16, 1), jnp.where(hi, pltpu.roll(t, 16, 1), 0.0))


def _params(sem=None):
    return pltpu.CompilerParams(dimension_semantics=sem, vmem_limit_bytes=VMEM_LIMIT)


def _full(shape):
    nd = len(shape)
    return pl.BlockSpec(shape, lambda *_: (0,) * nd)


def _sds(shape, dtype=F32):
    return jax.ShapeDtypeStruct(shape, dtype)


def _tile(s):
    return min(512, s)


def _row(ts, w):
    return pl.BlockSpec((ts, w), lambda i: (i, 0))


def _gate_cols(ts, off):
    return pl.BlockSpec((ts, D_SSD), lambda i, _o=off // D_SSD: (i, _o))


def _col(s, off):
    return pl.BlockSpec((s, LANE), lambda j, _o=off // LANE: (0, _o + j))


def _call_after(dep, body, args, *, in_specs, **kw):
    if dep is None:
        return pl.pallas_call(body, in_specs=in_specs, **kw)(*args)
    n = len(args)

    def body_dep(*refs):
        body(*refs[:n], *refs[n + 1:])

    return pl.pallas_call(body_dep, in_specs=list(in_specs) + [pl.BlockSpec(memory_space=pl.ANY)], **kw)(*args, dep)


def _rms(c, g):
    r = lax.rsqrt(jnp.mean(c * c, axis=-1, keepdims=True) + NORM_EPS)
    return c * r * g, r


def _rms_bwd(dn, c, r, g):
    ch = c * r
    dch = dn * g
    dc = r * (dch - ch * jnp.mean(dch * ch, axis=-1, keepdims=True))
    return dc, jnp.sum(dn * ch, axis=0, keepdims=True)


def _inproj_fwd(x, g, w, dep=None):
    s = x.shape[0]
    ts = _tile(s)

    def body(x_ref, g_ref, w_ref, proj_ref, h_ref, r_ref):
        hn, r = _rms(x_ref[...], g_ref[...])
        h = hn.astype(MXU)
        h_ref[...] = h
        r_ref[...] = r
        proj_ref[...] = jnp.dot(h, w_ref[...], preferred_element_type=F32)

    return _call_after(
        dep, body, (x, g, w), name="inproj_fwd", grid=(s // ts,),
        in_specs=[_row(ts, D_MODEL), _full((1, D_MODEL)), _full((D_MODEL, NCOL))],
        out_specs=[_row(ts, NCOL), _row(ts, D_MODEL), _row(ts, 1)],
        out_shape=[_sds((s, NCOL)), _sds((s, D_MODEL), MXU), _sds((s, 1))],
        compiler_params=_params(("parallel",)),
    )


def _conva_fwd(proj, w):
    s = proj.shape[0]

    def body(h_ref, b_ref, c_ref, z_ref, w_ref, y_ref):
        u = c_ref[...] * h_ref[...]
        wv = w_ref[...]
        cv = wv[2:3, :] * u + wv[1:2, :] * _shift_down(u, 1) + wv[0:1, :] * _shift_down(u, 2)
        y_ref[...] = b_ref[...] * cv * _silu(z_ref[...])

    return pl.pallas_call(
        body, name="conva_fwd", grid=(D_CONV_A // LANE,),
        in_specs=[_col(s, O_AH), _col(s, O_AB), _col(s, O_AC), _col(s, O_AZ), pl.BlockSpec((3, LANE), lambda j: (0, j))],
        out_specs=pl.BlockSpec((s, LANE), lambda j: (0, j)),
        out_shape=_sds((s, D_CONV_A)),
        compiler_params=_params(("parallel",)),
    )(proj, proj, proj, proj, w)


def _sconv_pre(u, wv, bv):
    return (wv[3:4, :] * u + wv[2:3, :] * _shift_down(u, 1) + wv[1:2, :] * _shift_down(u, 2)
            + wv[0:1, :] * _shift_down(u, 3) + bv)


def _sconv_fwd(proj, w, b):
    s = proj.shape[0]

    def body(u_ref, w_ref, b_ref, o_ref):
        o_ref[...] = _silu(_sconv_pre(u_ref[...], w_ref[...], b_ref[...]))

    return pl.pallas_call(
        body, name="sconv_fwd", grid=(N_XBC // LANE,),
        in_specs=[_col(s, O_XBC), pl.BlockSpec((4, LANE), lambda j: (0, j)), pl.BlockSpec((1, LANE), lambda j: (0, j))],
        out_specs=pl.BlockSpec((s, LANE), lambda j: (0, j)),
        out_shape=_sds((s, N_XBC)),
        compiler_params=_params(("parallel",)),
    )(proj, w, b)


def _ssd_chunk_common(tail, sc):
    l = SSD_CHUNK
    lane = _iota((l, LANE), 1)
    row = _iota((l, LANE), 0)
    tri = (row >= lane).astype(F32)
    a_row = -jnp.exp(sc[1:2, :])
    pre = tail + sc[0:1, :]
    dt = _softplus(pre)
    a_cs = _dot_hi(tri, dt * a_row)
    return lane, row, tri, a_row, pre, dt, a_cs, a_cs.T


def _pick_col(m, lane, k):
    return jnp.sum(jnp.where(lane == k, m, 0.0), axis=1, keepdims=True)


def _pick_row(m, row, k):
    return jnp.sum(jnp.where(row == k, m, 0.0), axis=0, keepdims=True)


def _ssd_fwd(xbc, proj, sc):
    s = xbc.shape[0]
    nc = s // SSD_CHUNK
    l = SSD_CHUNK
    cps = SSD_CHUNKS_PER_STEP

    def body(xbc_ref, tail_ref, sc_ref, y_ref, st_ref, state):
        @pl.when(pl.program_id(0) == 0)
        def _():
            state[...] = jnp.zeros_like(state)

        sc_v = sc_ref[...]
        lane1 = _iota((1, LANE), 1)
        rowp = _iota((LANE, 1), 0)
        d_row = sc_v[2:3, :]
        states = [state[j] for j in range(3)]
        for u in range(cps):
            r = slice(u * l, (u + 1) * l)
            lane, row, _, _, _, dt, a_cs, a_t = _ssd_chunk_common(tail_ref[r, :], sc_v)
            for j in range(3):
                st_ref[u, j] = states[j]
            for j in range(3):
                xpair = xbc_ref[r, LANE * j:LANE * (j + 1)]
                sp = states[j]
                ypair = jnp.zeros((l, LANE), F32)
                new_s = jnp.zeros((LANE, LANE), F32)
                decay = jnp.zeros((LANE, 1), F32)
                for half in range(2):
                    h = 2 * j + half
                    g = h // 3
                    hm = (lane < 64) if half == 0 else (lane >= 64)
                    hrow = (rowp < 64) if half == 0 else (rowp >= 64)
                    ac = _pick_col(a_cs, lane, DT_LANE + h)
                    ar = _pick_row(a_t, row, DT_LANE + h)
                    dtc = _pick_col(dt, lane, DT_LANE + h)
                    alast = jnp.sum(jnp.where(lane1 == l - 1, ar, 0.0), axis=1, keepdims=True)
                    dh = jnp.sum(jnp.where(lane1 == DT_LANE + h, d_row, 0.0), axis=1, keepdims=True)
                    xm = jnp.where(hm, xpair, 0.0)
                    xd = xm * dtc
                    bm = xbc_ref[r, D_SSD + LANE * g:D_SSD + LANE * (g + 1)]
                    cm = xbc_ref[r, D_SSD + SSD_BC + LANE * g:D_SSD + SSD_BC + LANE * (g + 1)]
                    lm = jnp.where(row >= lane, jnp.exp(jnp.minimum(ac - ar, 0.0)), 0.0)
                    y_diag = _dot(_dot_nt(cm, bm) * lm, xd)
                    y_off = jnp.where(hm, _dot_nt(cm, sp), 0.0) * jnp.exp(ac)
                    ypair = ypair + y_diag + y_off + xm * dh
                    new_s = new_s + _dot_tn(xd * jnp.exp(alast - ac), bm)
                    decay = jnp.where(hrow, jnp.exp(alast), decay)
                states[j] = sp * decay + new_s
                y_ref[r, LANE * j:LANE * (j + 1)] = ypair
        for j in range(3):
            state[j] = states[j]

    return pl.pallas_call(
        body, name="ssd_fwd", grid=(nc // cps,),
        in_specs=[pl.BlockSpec((cps * l, N_XBC), lambda c: (c, 0)),
                  pl.BlockSpec((cps * l, LANE), lambda c: (c, O_TAIL // LANE)), _full((8, LANE))],
        out_specs=[pl.BlockSpec((cps * l, D_SSD), lambda c: (c, 0)), pl.BlockSpec((cps, 3, LANE, LANE), lambda c: (c, 0, 0, 0))],
        out_shape=[_sds((s, D_SSD)), _sds((nc, 3, LANE, LANE))],
        scratch_shapes=[pltpu.VMEM((3, LANE, LANE), F32)],
        compiler_params=_params(("arbitrary",)),
    )(xbc, proj, sc)


def _mla_prep_fwd(proj, gq, gkv, wq, wkv, cos, sin):
    s = proj.shape[0]
    ts = _tile(s)
    nh = MLA_HEADS

    def body(cqa_ref, ckv_ref, tail_ref, gq_ref, gkv_ref, wq_ref, wkv_ref, cos_ref, sin_ref,
             q_ref, k_ref, v_ref, qn_ref, kvn_ref, rq_ref, rkv_ref):
        qn, rq = _rms(cqa_ref[...], gq_ref[...])
        kvn, rkv = _rms(ckv_ref[...], gkv_ref[...])
        qn = qn.astype(MXU)
        kvn = kvn.astype(MXU)
        qn_ref[...] = qn
        kvn_ref[...] = kvn
        rq_ref[...] = rq
        rkv_ref[...] = rkv
        q = _dot_nt(qn, wq_ref[...])
        kv = _dot_nt(kvn, wkv_ref[...])
        cosv = cos_ref[...]
        sinv = sin_ref[...]
        lane = _iota((ts, LANE), 1)
        rope_lanes = (lane >= ROPE_LANE) & (lane < ROPE_LANE + QK_ROPE)
        kr = jnp.where(rope_lanes, pltpu.roll(tail_ref[...], ROPE_LANE, 1), 0.0)
        kr = kr * cosv + _rope_swap(kr) * sinv
        for h in range(nh):
            qh = q[:, LANE * h:LANE * (h + 1)]
            q_ref[h] = ((qh * cosv + _rope_swap(qh) * sinv) * ATT_SCALE).astype(MXU)
            k_ref[h] = (kv[:, LANE * h:LANE * (h + 1)] + kr).astype(MXU)
            v_ref[h] = kv[:, LANE * (nh + h):LANE * (nh + h + 1)].astype(MXU)

    head = pl.BlockSpec((nh, ts, LANE), lambda i: (0, i, 0))
    return pl.pallas_call(
        body, name="mla_prep_fwd", grid=(s // ts,),
        in_specs=[pl.BlockSpec((ts, Q_LORA), lambda i: (i, O_CQA // Q_LORA)),
                  pl.BlockSpec((ts, KV_LORA), lambda i: (i, O_CKV // KV_LORA)),
                  pl.BlockSpec((ts, LANE), lambda i: (i, O_TAIL // LANE)),
                  _full((1, Q_LORA)), _full((1, KV_LORA)), _full((nh * LANE, Q_LORA)), _full((2 * nh * LANE, KV_LORA)),
                  _row(ts, LANE), _row(ts, LANE)],
        out_specs=[head, head, head, _row(ts, Q_LORA), _row(ts, KV_LORA), _row(ts, 1), _row(ts, 1)],
        out_shape=[_sds((nh, s, LANE), MXU)] * 3 + [_sds((s, Q_LORA), MXU), _sds((s, KV_LORA), MXU), _sds((s, 1)), _sds((s, 1))],
        compiler_params=_params(("parallel",)),
    )(proj, proj, proj, gq, gkv, wq, wkv, cos, sin)


ATT_SCALE = (QK_NOPE + QK_ROPE) ** -0.5
NEG = -1e30


def _att_tile(s, most):
    return min(most, s // 2)


ATT_FWD_TILE = 1024
ATT_BWD_TILE = 512


def _attn_fwd(q, k, v):
    nh, s, _ = q.shape
    tq = _att_tile(s, ATT_FWD_TILE)
    nq = s // tq

    def body(q_ref, k_ref, v_ref, o_ref, lse_ref):
        i = pl.program_id(1)
        rowi = _iota((tq, tq), 0)
        coli = _iota((tq, tq), 1)
        zero = (jnp.full((tq, 1), NEG, F32), jnp.zeros((tq, 1), F32), jnp.zeros((tq, LANE), F32))
        state = [zero, zero]
        done = [zero, zero]
        for t in range(nq + 1):
            first = t <= i
            qblk = jnp.where(first, i, nq - 1 - i)
            kblk = jnp.where(first, t, t - i - 1)
            qoff = pl.multiple_of(qblk * tq, tq)
            koff = pl.multiple_of(kblk * tq, tq)
            keep = coli <= rowi + jnp.where(kblk == qblk, 0, tq)
            restart = t == i + 1
            for hh in range(2):
                m, lsum, acc = state[hh]
                if t > 0:
                    done[hh] = tuple(jnp.where(restart, a, b) for a, b in zip(state[hh], done[hh]))
                    m = jnp.where(restart, NEG, m)
                    lsum = jnp.where(restart, 0.0, lsum)
                    acc = jnp.where(restart, 0.0, acc)
                sc = _dot_nt(q_ref[hh, pl.ds(qoff, tq), :], k_ref[hh, pl.ds(koff, tq), :])
                sc = jnp.where(keep, sc, NEG)
                m_new = jnp.maximum(m, jnp.max(sc, axis=1, keepdims=True))
                p = jnp.exp(sc - m_new)
                alpha = jnp.exp(m - m_new)
                lsum = alpha * lsum + jnp.sum(p, axis=1, keepdims=True)
                acc = alpha * acc + _dot(p, v_ref[hh, pl.ds(koff, tq), :])
                state[hh] = (m_new, lsum, acc)
        for blk, res in ((i, done), (nq - 1 - i, state)):
            off = pl.multiple_of(blk * tq, tq)
            out = None
            for hh in range(2):
                m, lsum, acc = res[hh]
                o = acc * (1.0 / lsum)
                lse_ref[hh, pl.ds(off, tq), :] = m + jnp.log(lsum)
                out = o if hh == 0 else out + pltpu.roll(o, V_DIM, 1)
            o_ref[pl.ds(off, tq), :] = out

    pair = pl.BlockSpec((2, s, LANE), lambda j, i: (j, 0, 0))
    return pl.pallas_call(
        body, name="attn_fwd", grid=(nh // 2, nq // 2),
        in_specs=[pair, pair, pair],
        out_specs=[pl.BlockSpec((s, LANE), lambda j, i: (0, j)), pl.BlockSpec((2, s, 1), lambda j, i: (j, 0, 0))],
        out_shape=[_sds((s, D_MLA)), _sds((nh, s, 1))],
        compiler_params=_params(("parallel", "arbitrary")),
    )(q, k, v)


def _ssd_gate(y_ssd, s_z, g):
    yz = y_ssd * _silu(s_z)
    g0 = _iota(yz.shape, 1) < D_SSD // 2
    sq = yz * yz
    ms0 = jnp.sum(jnp.where(g0, sq, 0.0), axis=1, keepdims=True) / (D_SSD // 2)
    ms1 = jnp.sum(jnp.where(g0, 0.0, sq), axis=1, keepdims=True) / (D_SSD // 2)
    r = jnp.where(g0, lax.rsqrt(ms0 + SSD_NORM_EPS), lax.rsqrt(ms1 + SSD_NORM_EPS))
    nrm = yz * r
    return nrm * g, nrm, r, g0


def _outproj_fwd(x, proj, ya, y_ssd, o, g_ssd, w):
    s = x.shape[0]
    ts = _tile(s)

    def body(x_ref, sz_ref, cz_ref, ya_ref, ys_ref, o_ref, g_ref, w_ref, xo_ref, y_ref):
        yb = _ssd_gate(ys_ref[...], sz_ref[...], g_ref[...])[0]
        yc = o_ref[...] * _silu(cz_ref[...])
        y = jnp.concatenate([ya_ref[...], yb, yc], axis=1).astype(MXU)
        y_ref[...] = y
        xo_ref[...] = x_ref[...] + jnp.dot(y, w_ref[...], preferred_element_type=F32)

    return pl.pallas_call(
        body, name="outproj_fwd", grid=(s // ts,),
        in_specs=[_row(ts, D_MODEL), _gate_cols(ts, O_SZ), _gate_cols(ts, O_CZ), _row(ts, D_CONV_A), _row(ts, D_SSD),
                  _row(ts, D_MLA), _full((1, D_SSD)), _full((D_MODEL, D_MODEL))],
        out_specs=[_row(ts, D_MODEL), _row(ts, D_MODEL)],
        out_shape=[_sds((s, D_MODEL)), _sds((s, D_MODEL), MXU)],
        compiler_params=_params(("parallel",)),
    )(x, proj, proj, ya, y_ssd, o, g_ssd, w)


def _loss_head(x, g, tgt):
    s = x.shape[0]
    ts = _tile(s)

    def body(x_ref, g_ref, t_ref, dx_ref, dg_ref, loss_ref):
        @pl.when(pl.program_id(0) == 0)
        def _():
            dg_ref[...] = jnp.zeros_like(dg_ref)
            loss_ref[...] = jnp.zeros_like(loss_ref)

        xv = x_ref[...]
        gv = g_ref[...]
        yn, r = _rms(xv, gv)
        e = yn - t_ref[...]
        loss_ref[...] += jnp.sum(jnp.sum(e * e, axis=1, keepdims=True), axis=0, keepdims=True) * (0.5 / D_MODEL)
        dx, dg = _rms_bwd(e * (1.0 / D_MODEL), xv, r, gv)
        dx_ref[...] = dx
        dg_ref[...] += dg

    return pl.pallas_call(
        body, name="loss_head", grid=(s // ts,),
        in_specs=[_row(ts, D_MODEL), _full((1, D_MODEL)), _row(ts, D_MODEL)],
        out_specs=[_row(ts, D_MODEL), _full((1, D_MODEL)), _full((1, LANE))],
        out_shape=[_sds((s, D_MODEL)), _sds((1, D_MODEL)), _sds((1, LANE))],
        compiler_params=_params(("arbitrary",)),
    )(x, g, tgt)


def _outproj_bwd(dout, y, w, proj, y_ssd, o, g_ssd, dep=None):
    s = dout.shape[0]
    ts = _tile(s)

    def body(dout_ref, y_ref, w_ref, sz_ref, cz_ref, ys_ref, o_ref, g_ref,
             dya_ref, dys_ref, dsz_ref, dattn_ref, dcz_ref, dg_ref, dw_ref):
        @pl.when(pl.program_id(0) == 0)
        def _():
            dw_ref[...] = jnp.zeros_like(dw_ref)
            dg_ref[...] = jnp.zeros_like(dg_ref)

        dout_b = dout_ref[...].astype(MXU)
        dw_ref[...] += _dot_tn(y_ref[...], dout_b)
        dy = _dot_nt(dout_b, w_ref[...])
        dya_ref[...] = dy[:, :D_CONV_A]
        dyb = dy[:, D_CONV_A:D_CONV_A + D_SSD]
        sz = sz_ref[...]
        ys = ys_ref[...]
        gv = g_ref[...]
        _, nrm, r, g0 = _ssd_gate(ys, sz, gv)
        dg_ref[...] += jnp.sum(dyb * nrm, axis=0, keepdims=True)
        dn = dyb * gv
        t = dn * nrm
        mean = jnp.where(g0, jnp.sum(jnp.where(g0, t, 0.0), axis=1, keepdims=True),
                         jnp.sum(jnp.where(g0, 0.0, t), axis=1, keepdims=True)) / (D_SSD // 2)
        dyz = r * (dn - nrm * mean)
        dys_ref[...] = dyz * _silu(sz)
        dsz_ref[...] = (dyz * ys * _dsilu(sz)).astype(MXU)
        dyc = dy[:, D_CONV_A + D_SSD:]
        cz = cz_ref[...]
        dattn_ref[...] = dyc * _silu(cz)
        dcz_ref[...] = (dyc * o_ref[...] * _dsilu(cz)).astype(MXU)

    return _call_after(
        dep, body, (dout, y, w, proj, proj, y_ssd, o, g_ssd), name="outproj_bwd", grid=(s // ts,),
        in_specs=[_row(ts, D_MODEL), _row(ts, D_MODEL), _full((D_MODEL, D_MODEL)), _gate_cols(ts, O_SZ), _gate_cols(ts, O_CZ),
                  _row(ts, D_SSD), _row(ts, D_MLA), _full((1, D_SSD))],
        out_specs=[_row(ts, D_CONV_A), _row(ts, D_SSD), _row(ts, D_SSD), _row(ts, D_MLA), _row(ts, D_MLA),
                   _full((1, D_SSD)), _full((D_MODEL, D_MODEL))],
        out_shape=[_sds((s, D_CONV_A)), _sds((s, D_SSD)), _sds((s, D_SSD), MXU), _sds((s, D_MLA)), _sds((s, D_MLA), MXU),
                   _sds((1, D_SSD)), _sds((D_MODEL, D_MODEL))],
        compiler_params=_params(("arbitrary",)),
    )


def _attn_bwd(q, k, v, o, d_o, lse, dep=None):
    nh, s, _ = q.shape
    tq = _att_tile(s, ATT_BWD_TILE)
    nq = s // tq

    def body(q_ref, k_ref, v_ref, o_ref, do_ref, lse_ref, dq_ref, dk_ref, dv_ref, dop, delta):
        i = pl.program_id(1)

        @pl.when(i == 0)
        def _():
            lane = _iota((s, LANE), 1)
            for hh in range(2):
                dov = do_ref[...]
                ov = o_ref[...]
                if hh == 1:
                    dov = pltpu.roll(dov, V_DIM, 1)
                    ov = pltpu.roll(ov, V_DIM, 1)
                dov = jnp.where(lane < V_DIM, dov, 0.0)
                dop[hh] = dov.astype(MXU)
                delta[hh] = jnp.sum(dov * ov, axis=1, keepdims=True)
                dq_ref[hh] = jnp.zeros((s, LANE), F32)

        rowi = _iota((tq, tq), 0)
        coli = _iota((tq, tq), 1)
        z = jnp.zeros((tq, LANE), F32)
        state = [(z, z), (z, z)]
        done = [(z, z), (z, z)]
        for t in range(nq + 1):
            first = t <= nq - 1 - i
            kblk = jnp.where(first, i, nq - 1 - i)
            qblk = jnp.where(first, i + t, t - 1)
            qoff = pl.multiple_of(qblk * tq, tq)
            koff = pl.multiple_of(kblk * tq, tq)
            keep = coli <= rowi + jnp.where(kblk == qblk, 0, tq)
            restart = t == nq - i
            for hh in range(2):
                dk, dv = state[hh]
                if t > 0:
                    done[hh] = tuple(jnp.where(restart, a, b) for a, b in zip(state[hh], done[hh]))
                    dk = jnp.where(restart, 0.0, dk)
                    dv = jnp.where(restart, 0.0, dv)
                kb = k_ref[hh, pl.ds(koff, tq), :]
                qb = q_ref[hh, pl.ds(qoff, tq), :]
                dob = dop[hh, pl.ds(qoff, tq), :]
                sc = jnp.where(keep, _dot_nt(qb, kb), NEG)
                p = jnp.exp(sc - lse_ref[hh, pl.ds(qoff, tq), :])
                dp = _dot_nt(dob, v_ref[hh, pl.ds(koff, tq), :])
                ds = p * (dp - delta[hh, pl.ds(qoff, tq), :])
                dq_ref[hh, pl.ds(qoff, tq), :] += _dot(ds, kb)
                state[hh] = (dk + _dot_tn(ds, qb), dv + _dot_tn(p, dob))
        for blk, res in ((i, done), (nq - 1 - i, state)):
            off = pl.multiple_of(blk * tq, tq)
            for hh in range(2):
                dk_ref[hh, pl.ds(off, tq), :] = res[hh][0]
                dv_ref[hh, pl.ds(off, tq), :] = res[hh][1]

    pair = pl.BlockSpec((2, s, LANE), lambda j, i: (j, 0, 0))
    return _call_after(
        dep, body, (q, k, v, o, d_o, lse), name="attn_bwd", grid=(nh // 2, nq // 2),
        in_specs=[pair, pair, pair, pl.BlockSpec((s, LANE), lambda j, i: (0, j)), pl.BlockSpec((s, LANE), lambda j, i: (0, j)),
                  pl.BlockSpec((2, s, 1), lambda j, i: (j, 0, 0))],
        out_specs=[pair, pair, pair],
        out_shape=[_sds((nh, s, LANE))] * 3,
        scratch_shapes=[pltpu.VMEM((2, s, LANE), MXU), pltpu.VMEM((2, s, 1), F32)],
        compiler_params=_params(("parallel", "arbitrary")),
    )


def _ssd_bwd(xbc, proj, sc, states, dy, dep=None):
    s = xbc.shape[0]
    nc = s // SSD_CHUNK
    l = SSD_CHUNK
    cps = SSD_CHUNKS_PER_STEP

    def body(xbc_ref, tail_ref, sc_ref, st_ref, dy_ref, dxbc_ref, dtail_ref, dsc_ref, dstate):
        @pl.when(pl.program_id(0) == 0)
        def _():
            dstate[...] = jnp.zeros_like(dstate)
            dsc_ref[...] = jnp.zeros_like(dsc_ref)

        sc_v = sc_ref[...]
        lane1 = _iota((1, LANE), 1)
        rowp = _iota((LANE, 1), 0)
        rowl = _iota((l, 1), 0)
        d_row = sc_v[2:3, :]
        dstates = [dstate[j] for j in range(3)]
        for u in reversed(range(cps)):
            dstates = chunk(u, xbc_ref, tail_ref, sc_v, st_ref, dy_ref, dxbc_ref, dtail_ref, dsc_ref, dstates,
                            lane1, rowp, rowl, d_row)
        for j in range(3):
            dstate[j] = dstates[j]

    def chunk(u, xbc_ref, tail_ref, sc_v, st_ref, dy_ref, dxbc_ref, dtail_ref, dsc_ref, dstates, lane1, rowp, rowl, d_row):
        r = slice(u * l, (u + 1) * l)
        dstates = list(dstates)
        lane, row, tri, a_row, pre, dt, a_cs, a_t = _ssd_chunk_common(tail_ref[r, :], sc_v)
        da_col = jnp.zeros((l, LANE), F32)
        da_row = jnp.zeros((LANE, l), F32)
        dt_x = jnp.zeros((l, LANE), F32)
        dd_row = jnp.zeros((1, LANE), F32)
        db = [jnp.zeros((l, LANE), F32), jnp.zeros((l, LANE), F32)]
        dc = [jnp.zeros((l, LANE), F32), jnp.zeros((l, LANE), F32)]
        for j in range(3):
            xpair = xbc_ref[r, LANE * j:LANE * (j + 1)]
            dypair = dy_ref[r, LANE * j:LANE * (j + 1)]
            sp = st_ref[u, j]
            dsp = dstates[j]
            dxpair = jnp.zeros((l, LANE), F32)
            ds_new = jnp.zeros((LANE, LANE), F32)
            decay = jnp.zeros((LANE, 1), F32)
            for half in range(2):
                h = 2 * j + half
                g = h // 3
                hm = (lane < 64) if half == 0 else (lane >= 64)
                hrow = (rowp < 64) if half == 0 else (rowp >= 64)
                ac = _pick_col(a_cs, lane, DT_LANE + h)
                ar = _pick_row(a_t, row, DT_LANE + h)
                dtc = _pick_col(dt, lane, DT_LANE + h)
                alast = jnp.sum(jnp.where(lane1 == l - 1, ar, 0.0), axis=1, keepdims=True)
                dh = jnp.sum(jnp.where(lane1 == DT_LANE + h, d_row, 0.0), axis=1, keepdims=True)
                xm = jnp.where(hm, xpair, 0.0)
                xd = xm * dtc
                dym = jnp.where(hm, dypair, 0.0)
                bm = xbc_ref[r, D_SSD + LANE * g:D_SSD + LANE * (g + 1)]
                cm = xbc_ref[r, D_SSD + SSD_BC + LANE * g:D_SSD + SSD_BC + LANE * (g + 1)]
                lm = jnp.where(row >= lane, jnp.exp(jnp.minimum(ac - ar, 0.0)), 0.0)
                e_in = jnp.exp(ac)
                f_out = jnp.exp(alast - ac)
                e_last = jnp.exp(alast)
                m = _dot_nt(cm, bm) * lm
                y_off = jnp.where(hm, _dot_nt(cm, sp), 0.0) * e_in
                dm = _dot_nt(dym, xd)
                dxd = _dot_tn(m, dym)
                dg = dm * lm
                dye = dym * e_in
                dc[g] = dc[g] + _dot(dg, bm) + _dot(dye, sp)
                db[g] = db[g] + _dot_tn(dg, cm)
                qm = dm * m
                dac = jnp.sum(qm, axis=1, keepdims=True) + jnp.sum(dym * y_off, axis=1, keepdims=True)
                dar = -jnp.sum(qm, axis=0, keepdims=True)
                dxf = jnp.where(hm, _dot_nt(bm, dsp), 0.0)
                db[g] = db[g] + _dot(xd * f_out, dsp)
                dxd = dxd + dxf * f_out
                df = jnp.sum(dxf * xd, axis=1, keepdims=True) * f_out
                dac = dac - df
                s_last = jnp.sum(df, axis=0, keepdims=True)
                ss = jnp.sum(jnp.where(hrow, dsp * sp, 0.0), axis=1, keepdims=True)
                s_last = s_last + e_last * jnp.sum(ss, axis=0, keepdims=True)
                dac = dac + jnp.where(rowl == l - 1, s_last, 0.0)
                ds_new = ds_new + _dot_tn(dye, cm)
                decay = jnp.where(hrow, e_last, decay)
                dxpair = dxpair + dxd * dtc + dym * dh
                dt_x = dt_x + jnp.where(lane == DT_LANE + h, jnp.sum(dxd * xm, axis=1, keepdims=True), 0.0)
                dsum = jnp.sum(jnp.sum(dym * xm, axis=1, keepdims=True), axis=0, keepdims=True)
                dd_row = dd_row + jnp.where(lane1 == DT_LANE + h, dsum, 0.0)
                da_col = da_col + jnp.where(lane == DT_LANE + h, dac, 0.0)
                da_row = da_row + jnp.where(row == DT_LANE + h, dar, 0.0)
            dstates[j] = dsp * decay + ds_new
            dxbc_ref[r, LANE * j:LANE * (j + 1)] = dxpair
        for g in range(2):
            dxbc_ref[r, D_SSD + LANE * g:D_SSD + LANE * (g + 1)] = db[g]
            dxbc_ref[r, D_SSD + SSD_BC + LANE * g:D_SSD + SSD_BC + LANE * (g + 1)] = dc[g]
        dla = _dot_hi_tn(tri, da_col + da_row.T)
        ddt = dt_x + dla * a_row
        dpre = ddt * _sigmoid(pre)
        dtm = (lane >= DT_LANE) & (lane < DT_LANE + SSD_HEADS)
        dtail_ref[r, :] = jnp.where(dtm, dpre, 0.0).astype(MXU)
        dtm1 = (lane1 >= DT_LANE) & (lane1 < DT_LANE + SSD_HEADS)
        dsc_ref[0:1, :] += jnp.where(dtm1, jnp.sum(dpre, axis=0, keepdims=True), 0.0)
        dsc_ref[1:2, :] += jnp.where(dtm1, jnp.sum(dla * dt, axis=0, keepdims=True) * a_row, 0.0)
        dsc_ref[2:3, :] += dd_row
        return dstates

    rev = lambda c: nc // cps - 1 - c
    return _call_after(
        dep, body, (xbc, proj, sc, states, dy), name="ssd_bwd", grid=(nc // cps,),
        in_specs=[pl.BlockSpec((cps * l, N_XBC), lambda c: (rev(c), 0)),
                  pl.BlockSpec((cps * l, LANE), lambda c: (rev(c), O_TAIL // LANE)), _full((8, LANE)),
                  pl.BlockSpec((cps, 3, LANE, LANE), lambda c: (rev(c), 0, 0, 0)),
                  pl.BlockSpec((cps * l, D_SSD), lambda c: (rev(c), 0))],
        out_specs=[pl.BlockSpec((cps * l, N_XBC), lambda c: (rev(c), 0)), pl.BlockSpec((cps * l, LANE), lambda c: (rev(c), 0)),
                   _full((8, LANE))],
        out_shape=[_sds((s, N_XBC)), _sds((s, LANE), MXU), _sds((8, LANE))],
        scratch_shapes=[pltpu.VMEM((3, LANE, LANE), F32)],
        compiler_params=_params(("arbitrary",)),
    )


def _sconv_bwd(proj, w, b, dxbc, dep=None):
    s = proj.shape[0]

    def body(u_ref, w_ref, b_ref, d_ref, du_ref, dw_ref, db_ref):
        u = u_ref[...]
        wv = w_ref[...]
        dpre = d_ref[...] * _dsilu(_sconv_pre(u, wv, b_ref[...]))
        ahead = [_shift_up(dpre, j) for j in range(4)]
        du_ref[...] = (wv[3:4, :] * ahead[0] + wv[2:3, :] * ahead[1] + wv[1:2, :] * ahead[2]
                       + wv[0:1, :] * ahead[3]).astype(MXU)
        for k in range(4):
            dw_ref[k:k + 1, :] = jnp.sum(ahead[3 - k] * u, axis=0, keepdims=True)
        db_ref[...] = jnp.sum(dpre, axis=0, keepdims=True)

    blk = pl.BlockSpec((s, LANE), lambda j: (0, j))
    return _call_after(
        dep, body, (proj, w, b, dxbc), name="sconv_bwd", grid=(N_XBC // LANE,),
        in_specs=[_col(s, O_XBC), pl.BlockSpec((4, LANE), lambda j: (0, j)), pl.BlockSpec((1, LANE), lambda j: (0, j)), blk],
        out_specs=[blk, pl.BlockSpec((4, LANE), lambda j: (0, j)), pl.BlockSpec((1, LANE), lambda j: (0, j))],
        out_shape=[_sds((s, N_XBC), MXU), _sds((4, N_XBC)), _sds((1, N_XBC))],
        compiler_params=_params(("parallel",)),
    )


def _conva_bwd(proj, w, dya, dep=None):
    s = proj.shape[0]

    def body(h_ref, b_ref, c_ref, z_ref, w_ref, d_ref, da_ref, dw_ref):
        ah, ab, acv, az = h_ref[...], b_ref[...], c_ref[...], z_ref[...]
        wv = w_ref[...]
        u = acv * ah
        cv = wv[2:3, :] * u + wv[1:2, :] * _shift_down(u, 1) + wv[0:1, :] * _shift_down(u, 2)
        dy = d_ref[...]
        sz = _silu(az)
        da_ref[1] = (dy * cv * sz).astype(MXU)
        da_ref[3] = (dy * ab * cv * _dsilu(az)).astype(MXU)
        dcv = dy * ab * sz
        ahead = [_shift_up(dcv, j) for j in range(3)]
        du = wv[2:3, :] * ahead[0] + wv[1:2, :] * ahead[1] + wv[0:1, :] * ahead[2]
        da_ref[0] = (du * acv).astype(MXU)
        da_ref[2] = (du * ah).astype(MXU)
        for k in range(3):
            dw_ref[k:k + 1, :] = jnp.sum(ahead[2 - k] * u, axis=0, keepdims=True)

    return _call_after(
        dep, body, (proj, proj, proj, proj, w, dya), name="conva_bwd", grid=(D_CONV_A // LANE,),
        in_specs=[_col(s, O_AH), _col(s, O_AB), _col(s, O_AC), _col(s, O_AZ), pl.BlockSpec((3, LANE), lambda j: (0, j)),
                  pl.BlockSpec((s, LANE), lambda j: (0, j))],
        out_specs=[pl.BlockSpec((4, s, LANE), lambda j: (0, 0, j)), pl.BlockSpec((3, LANE), lambda j: (0, j))],
        out_shape=[_sds((4, s, D_CONV_A), MXU), _sds((3, D_CONV_A))],
        compiler_params=_params(("parallel",)),
    )


def _mla_prep_bwd(dq, dk, dv, proj, qn, kvn, rq, rkv, gq, gkv, wq, wkv, cos, sin):
    s = proj.shape[0]
    ts = _tile(s)
    nh = MLA_HEADS

    def body(dq_ref, dk_ref, dv_ref, cqa_ref, ckv_ref, qn_ref, kvn_ref, rq_ref, rkv_ref, gq_ref, gkv_ref,
             wq_ref, wkv_ref, cos_ref, sin_ref, dcqa_ref, dckv_ref, dtail_ref, dwq_ref, dwkv_ref, dgq_ref, dgkv_ref):
        @pl.when(pl.program_id(0) == 0)
        def _():
            dwq_ref[...] = jnp.zeros_like(dwq_ref)
            dwkv_ref[...] = jnp.zeros_like(dwkv_ref)
            dgq_ref[...] = jnp.zeros_like(dgq_ref)
            dgkv_ref[...] = jnp.zeros_like(dgkv_ref)

        cosv = cos_ref[...]
        sinv = sin_ref[...]
        lane = _iota((ts, LANE), 1)
        rope_lanes = (lane >= ROPE_LANE) & (lane < ROPE_LANE + QK_ROPE)

        def unrope(gr):
            return gr * cosv + _rope_swap(gr * sinv)

        dqs, dks, dvs = [], [], []
        dkr = jnp.zeros((ts, LANE), F32)
        for h in range(nh):
            dqs.append(unrope(dq_ref[h] * ATT_SCALE).astype(MXU))
            dkh = dk_ref[h]
            dks.append(jnp.where(lane < QK_NOPE, dkh, 0.0).astype(MXU))
            dkr = dkr + jnp.where(rope_lanes, dkh, 0.0)
            dvs.append(dv_ref[h].astype(MXU))
        dtail_ref[...] = pltpu.roll(jnp.where(rope_lanes, unrope(dkr), 0.0), ROPE_LANE, 1).astype(MXU)
        dq_all = jnp.concatenate(dqs, axis=1)
        dkv_all = jnp.concatenate(dks + dvs, axis=1)
        dwq_ref[...] += _dot_tn(dq_all, qn_ref[...])
        dwkv_ref[...] += _dot_tn(dkv_all, kvn_ref[...])
        dcqa, dgq = _rms_bwd(_dot(dq_all, wq_ref[...]), cqa_ref[...], rq_ref[...], gq_ref[...])
        dckv, dgkv = _rms_bwd(_dot(dkv_all, wkv_ref[...]), ckv_ref[...], rkv_ref[...], gkv_ref[...])
        dcqa_ref[...] = dcqa.astype(MXU)
        dckv_ref[...] = dckv.astype(MXU)
        dgq_ref[...] += dgq
        dgkv_ref[...] += dgkv

    head = pl.BlockSpec((nh, ts, LANE), lambda i: (0, i, 0))
    return pl.pallas_call(
        body, name="mla_prep_bwd", grid=(s // ts,),
        in_specs=[head, head, head,
                  pl.BlockSpec((ts, Q_LORA), lambda i: (i, O_CQA // Q_LORA)),
                  pl.BlockSpec((ts, KV_LORA), lambda i: (i, O_CKV // KV_LORA)),
                  _row(ts, Q_LORA), _row(ts, KV_LORA), _row(ts, 1), _row(ts, 1),
                  _full((1, Q_LORA)), _full((1, KV_LORA)), _full((nh * LANE, Q_LORA)), _full((2 * nh * LANE, KV_LORA)),
                  _row(ts, LANE), _row(ts, LANE)],
        out_specs=[_row(ts, Q_LORA), _row(ts, KV_LORA), _row(ts, LANE), _full((nh * LANE, Q_LORA)),
                   _full((2 * nh * LANE, KV_LORA)), _full((1, Q_LORA)), _full((1, KV_LORA))],
        out_shape=[_sds((s, Q_LORA), MXU), _sds((s, KV_LORA), MXU), _sds((s, LANE), MXU), _sds((nh * LANE, Q_LORA)),
                   _sds((2 * nh * LANE, KV_LORA)), _sds((1, Q_LORA)), _sds((1, KV_LORA))],
        compiler_params=_params(("arbitrary",)),
    )(dq, dk, dv, proj, proj, qn, kvn, rq, rkv, gq, gkv, wq, wkv, cos, sin)


def _inproj_bwd(da4, dsz, dxbc_in, dcqa, dckv, dcz, dtail_a, dtail_b, w, x, rstd, g, dout, dep=None):
    s = x.shape[0]
    ts = _tile(s)

    def body(da_ref, dsz_ref, dxbc_ref, dcqa_ref, dckv_ref, dcz_ref, dta_ref, dtb_ref, w_ref, x_ref, r_ref, g_ref, dout_ref,
             dproj_ref, dx_ref, dg_ref):
        @pl.when(pl.program_id(0) == 0)
        def _():
            dg_ref[...] = jnp.zeros_like(dg_ref)

        dproj = jnp.concatenate(
            [da_ref[0], da_ref[1], da_ref[2], da_ref[3], dxbc_ref[...], dsz_ref[...], dcqa_ref[...], dckv_ref[...],
             dcz_ref[...], dta_ref[...] + dtb_ref[...]], axis=1)
        dproj_ref[...] = dproj
        dh = _dot_nt(dproj, w_ref[...])
        dx, dg = _rms_bwd(dh, x_ref[...], r_ref[...], g_ref[...])
        dx_ref[...] = dout_ref[...] + dx
        dg_ref[...] += dg

    return _call_after(
        dep, body, (da4, dsz, dxbc_in, dcqa, dckv, dcz, dtail_a, dtail_b, w, x, rstd, g, dout), name="inproj_bwd", grid=(s // ts,),
        in_specs=[pl.BlockSpec((4, ts, D_CONV_A), lambda i: (0, i, 0)), _row(ts, D_SSD), _row(ts, N_XBC), _row(ts, Q_LORA),
                  _row(ts, KV_LORA), _row(ts, D_MLA), _row(ts, LANE), _row(ts, LANE), _full((D_MODEL, NCOL)),
                  _row(ts, D_MODEL), _row(ts, 1), _full((1, D_MODEL)), _row(ts, D_MODEL)],
        out_specs=[_row(ts, NCOL), _row(ts, D_MODEL), _full((1, D_MODEL))],
        out_shape=[_sds((s, NCOL), MXU), _sds((s, D_MODEL)), _sds((1, D_MODEL))],
        compiler_params=_params(("arbitrary",)),
    )


DWIN_BLOCK = 640


def _dwin(h, dproj, dep=None):
    s = h.shape[0]

    def body(h_ref, d_ref, o_ref):
        o_ref[...] = _dot_tn(h_ref[...], d_ref[...])

    return _call_after(
        dep, body, (h, dproj), name="dwin", grid=(NCOL // DWIN_BLOCK,),
        in_specs=[_full((s, D_MODEL)), pl.BlockSpec((s, DWIN_BLOCK), lambda j: (0, j))],
        out_specs=pl.BlockSpec((D_MODEL, DWIN_BLOCK), lambda j: (0, j)),
        out_shape=_sds((D_MODEL, NCOL)),
        compiler_params=_params(("parallel",)),
    )


def _adamw(ws, gs, ms, vs, whole, layer=None, into=None):
    n = len(ws)
    bc1 = 1.0 - ADAM_B1 ** ADAM_STEP
    bc2 = 1.0 - ADAM_B2 ** ADAM_STEP

    def body(*refs):
        ins, outs = refs[:4 * n], refs[4 * n:]
        for a in range(n):
            w_ref, g_ref, m_ref, v_ref = ins[a], ins[n + a], ins[2 * n + a], ins[3 * n + a]
            gv = g_ref[...]
            mn = ADAM_B1 * m_ref[...] + (1.0 - ADAM_B1) * gv
            vn = ADAM_B2 * v_ref[...] + (1.0 - ADAM_B2) * (gv * gv)
            outs[n + a][...] = mn
            outs[2 * n + a][...] = vn
            outs[a][...] = -ADAM_LR * ((mn / bc1) / (jnp.sqrt(vn / bc2) + ADAM_EPS) + ADAM_WD * w_ref[...])

    if whole:
        grid, blks = (1,), [pl.BlockSpec(w.shape, lambda i, _n=w.ndim: (0,) * _n) for w in ws]
    elif layer is not None:
        assert n == 1
        rows, cols = ws[0].shape[1:]
        grid = (2,)
        blk = pl.BlockSpec((1, rows // 2, cols), lambda k: (layer, k, 0))
        gblk = pl.BlockSpec((1, rows // 2, cols), lambda k: (0, k, 0))

        def body1(w_ref, g_ref, m_ref, v_ref, *rest):
            go_ref, d_ref, mo_ref, vo_ref = rest[-4:]
            go_ref[...] = g_ref[...]
            body(w_ref, g_ref, m_ref, v_ref, d_ref, mo_ref, vo_ref)

        extra = list(into) if into is not None else []
        out = pl.pallas_call(
            body1, name=f"adamw_layer{layer}", grid=grid,
            in_specs=[blk, gblk, blk, blk] + [ANY] * len(extra), out_specs=[blk] * 4, out_shape=[_sds(ws[0].shape)] * 4,
            input_output_aliases={4 + i: i for i in range(len(extra))},
            compiler_params=_params(("parallel",)),
        )(ws[0], gs[0][None], ms[0], vs[0], *extra)
        return list(out)
    else:
        grid = (ws[0].shape[0], 2)
        blks = [pl.BlockSpec((1, w.shape[1] // 2, w.shape[2]), lambda i, k: (i, k, 0)) for w in ws]
    out = pl.pallas_call(
        body, name="adamw", grid=grid,
        in_specs=blks * 4, out_specs=blks * 3, out_shape=[_sds(w.shape) for w in ws] * 3,
        compiler_params=_params(("parallel",) * len(grid)),
    )(*ws, *gs, *ms, *vs)
    return [(out[a], out[n + a], out[2 * n + a]) for a in range(n)]


COL_MOVES = ((0, 0, 1024), (1024, O_SZ, 384), (1408, O_XBC, 896), (2304, O_TAIL + DT_LANE, 6), (2310, O_CQA, 256),
             (2566, O_CKV, 128), (2694, O_TAIL, 32), (2726, O_CZ, 384))


def _move_cols(w, moves, width):
    out = None
    for src, dst, n in moves:
        piece = jnp.pad(w[..., src:src + n], [(0, 0)] * (w.ndim - 1) + [(dst, width - dst - n)])
        out = piece if out is None else out + piece
    return out


def _perm_cols(w):
    return _move_cols(w, COL_MOVES, NCOL)


def _unperm_cols(g):
    return _move_cols(g, [(dst, src, n) for src, dst, n in COL_MOVES], IN_COLS)


def _wq_layout(wt):
    return jnp.pad(wt.reshape(MLA_HEADS, QK_NOPE + QK_ROPE, Q_LORA), ((0, 0), (0, 32), (0, 0))).reshape(MLA_HEADS * LANE, Q_LORA)


def _wq_unlayout(g):
    return g.reshape(MLA_HEADS, LANE, Q_LORA)[:, :QK_NOPE + QK_ROPE].reshape(MLA_HEADS * (QK_NOPE + QK_ROPE), Q_LORA)


def _wkv_layout(wt):
    t = wt.reshape(MLA_HEADS, 2, 64, KV_LORA).transpose(1, 0, 2, 3)
    return jnp.pad(t, ((0, 0), (0, 0), (0, 64), (0, 0))).reshape(2 * MLA_HEADS * LANE, KV_LORA)


def _wkv_unlayout(g):
    t = g.reshape(2, MLA_HEADS, LANE, KV_LORA)[:, :, :64]
    return t.transpose(1, 0, 2, 3).reshape(MLA_HEADS * LANE, KV_LORA)


def _rope_tables(positions):
    inv_freq = ROPE_BASE ** (-jnp.arange(0, QK_ROPE, 2, dtype=F32) / QK_ROPE)
    ang = positions.astype(F32)[:, None] * inv_freq
    cos, sin = jnp.cos(ang), jnp.sin(ang)
    s = positions.shape[0]
    one, zero = jnp.ones((s, ROPE_LANE), F32), jnp.zeros((s, ROPE_LANE), F32)
    cos_t = jnp.concatenate([one, cos, cos, one[:, :32]], axis=1)
    sin_t = jnp.concatenate([zero, -sin, sin, zero[:, :32]], axis=1)
    return cos_t, sin_t


def _ssd_scalars(dt_bias, a_log, d_skip):
    return jnp.pad(jnp.stack([dt_bias, a_log, d_skip]), ((0, 5), (DT_LANE, LANE - DT_LANE - SSD_HEADS)))


def _layer_fwd(x, lw, cos, sin, dep=None, late=None):
    proj, h, rstd = _inproj_fwd(x, lw["norm_g"], lw["w_in"], dep)
    ya = _conva_fwd(proj, lw["conv_a_w"])
    xbc = _sconv_fwd(proj, lw["ssd_conv_w"], lw["ssd_conv_b"])
    y_ssd, states = _ssd_fwd(xbc, proj, lw["sc"])
    if late is not None:
        lw = {**lw, **late(ya, y_ssd)}
    q, k, v, qn, kvn, rq, rkv = _mla_prep_fwd(proj, lw["gq"], lw["gkv"], lw["wq"], lw["wkv"], cos, sin)
    o, lse = _attn_fwd(q, k, v)
    x_out, y = _outproj_fwd(x, proj, ya, y_ssd, o, lw["g_ssd"], lw["w_out"])
    saved = dict(x=x, proj=proj, h=h, rstd=rstd, xbc=xbc, y_ssd=y_ssd, states=states, q=q, k=k, v=v, qn=qn, kvn=kvn,
                 rq=rq, rkv=rkv, o=o, lse=lse, y=y)
    return x_out, saved, lw


def _layer_bwd(dout, lw, sv, cos, sin, rs=None, begin_early=None):
    tok = lambda: None if rs is None else rs["h"]["token"]
    dya, dys, dsz, d_o, dcz, dg_ssd, dw_out = _outproj_bwd(dout, sv["y"], lw["w_out"], sv["proj"], sv["y_ssd"], sv["o"],
                                                            lw["g_ssd"], tok())
    if rs is not None:
        rs = _rs_add_mine(rs, [dya])
    dq, dk, dv = _attn_bwd(sv["q"], sv["k"], sv["v"], sv["o"], d_o, sv["lse"], tok())
    dxbc, dtail_s, dsc = _ssd_bwd(sv["xbc"], sv["proj"], lw["sc"], sv["states"], dys, tok())
    da4, dw_conva = _conva_bwd(sv["proj"], lw["conv_a_w"], dya, tok())
    if rs is not None:
        rs = _rs_add_chips(rs, [dq, dxbc, da4])
    du, dw_sconv, db_sconv = _sconv_bwd(sv["proj"], lw["ssd_conv_w"], lw["ssd_conv_b"], dxbc, tok())
    dcqa, dckv, dtail_m, dwq, dwkv, dgq, dgkv = _mla_prep_bwd(
        dq, dk, dv, sv["proj"], sv["qn"], sv["kvn"], sv["rq"], sv["rkv"], lw["gq"], lw["gkv"], lw["wq"], lw["wkv"], cos, sin)
    early = None if begin_early is None else begin_early(dw_out, dwq, dwkv)
    etok = lambda: None if early is None else early["h"]["token"]
    dproj, dx, dg = _inproj_bwd(da4, dsz, du, dcqa, dckv, dcz, dtail_s, dtail_m, lw["w_in"], sv["x"], sv["rstd"],
                                lw["norm_g"], dout, etok())
    reduced = None if rs is None else _rs_end(rs, [du, dcqa, dx])
    if early is not None:
        early = _rs_add_mine(early, [dx])
    dw_in = _dwin(sv["h"], dproj, etok())
    if early is not None:
        early = _rs_add_chips(early, [dw_in])
    grads = dict(norm_g=dg, w_in=dw_in, conv_a_w=dw_conva, ssd_conv_w=dw_sconv, ssd_conv_b=db_sconv, sc=dsc,
                 g_ssd=dg_ssd, gq=dgq, wq=dwq, gkv=dgkv, wkv=dwkv, w_out=dw_out)
    return dx, grads, reduced, early


ANY = pl.BlockSpec(memory_space=pl.ANY)
N_CHIPS = 4
N_DEV = 8


def _place():
    return lax.axis_index("x"), lax.axis_index("y"), lax.axis_index("c")


HBM_SPEC = pl.BlockSpec(memory_space=pltpu.HBM)
SEM_SPEC = pl.BlockSpec(memory_space=pltpu.SEMAPHORE)
PAYLOAD = jnp.bfloat16


def _hbm(a):
    return pltpu.with_memory_space_constraint(a, pltpu.HBM)


def _run_plan(plan, srcs, lands, send_sems, recv_sems, start, wait):
    copies = plan(srcs, lands)
    if start:
        for i, (src, dst, _, to) in enumerate(copies):
            pltpu.make_async_remote_copy(src_ref=src, dst_ref=dst, send_sem=send_sems.at[i], recv_sem=recv_sems.at[i],
                                         device_id=to, device_id_type=MESH_T).start()
    if wait:
        for i, (src, _, arrives, to) in enumerate(copies):
            cp = pltpu.make_async_remote_copy(src_ref=src, dst_ref=arrives, send_sem=send_sems.at[i],
                                              recv_sem=recv_sems.at[i], device_id=to, device_id_type=MESH_T)
            cp.wait_send()
            cp.wait_recv()


def _exchange_start(name, plan, n_copies, srcs, land_shapes, deps):
    ns, nl = len(srcs), len(land_shapes)
    n_in = ns + nl + len(deps)

    def body(*refs):
        send_sems, recv_sems = refs[n_in], refs[n_in + 1]
        token = refs[-1]
        _run_plan(plan, refs[:ns], refs[ns:ns + nl], send_sems, recv_sems, True, False)
        token[...] = jnp.zeros_like(token)

    thru = [pltpu.HBM(a.shape, a.dtype) for a in srcs] + [pltpu.HBM(a.shape, a.dtype) for a in land_shapes]
    outs = pl.pallas_call(
        body, name=name,
        out_shape=(pltpu.SemaphoreType.DMA((n_copies,)), pltpu.SemaphoreType.DMA((n_copies,)), *thru, _sds((8, LANE))),
        in_specs=[HBM_SPEC] * (ns + nl) + [ANY] * len(deps),
        out_specs=(SEM_SPEC, SEM_SPEC, *[HBM_SPEC] * (ns + nl), pl.BlockSpec(memory_space=pltpu.VMEM)),
        input_output_aliases={i: 2 + i for i in range(ns + nl)},
        compiler_params=pltpu.CompilerParams(has_side_effects=pltpu.SideEffectType.DATAFLOW_SIDE_EFFECTING),
    )(*[_hbm(a) for a in srcs], *[_hbm(lax.empty(a.shape, a.dtype)) for a in land_shapes], *deps)
    return (outs[0], outs[1]), list(outs[2:2 + ns]), list(outs[2 + ns:2 + ns + nl]), outs[-1]


def _exchange_wait(name, plan, sems, srcs, lands, after):
    ns, nl = len(srcs), len(lands)

    def body(*refs):
        _run_plan(plan, refs[:ns], refs[ns:ns + nl], refs[ns + nl], refs[ns + nl + 1], False, True)

    outs = pl.pallas_call(
        body, name=name,
        out_shape=[pltpu.HBM(a.shape, a.dtype) for a in list(srcs) + list(lands)],
        in_specs=[HBM_SPEC] * (ns + nl) + [SEM_SPEC, SEM_SPEC] + [ANY] * len(after), out_specs=[HBM_SPEC] * (ns + nl),
        input_output_aliases={i: i for i in range(ns + nl)},
        compiler_params=pltpu.CompilerParams(has_side_effects=pltpu.SideEffectType.DATAFLOW_SIDE_EFFECTING),
    )(*srcs, *lands, sems[0], sems[1], *after)
    return list(outs[:ns]), list(outs[ns:])


def _xchg_begin(name, plan, n_copies, srcs, land_shapes, deps=()):
    sems, srcs_t, lands_t, token = _exchange_start(name + "_start", plan, n_copies, srcs, land_shapes, list(deps))
    return dict(name=name, plan=plan, sems=sems, srcs=srcs_t, lands=lands_t, token=token)


def _xchg_end(h, after):
    return _exchange_wait(h["name"] + "_wait", h["plan"], h["sems"], h["srcs"], h["lands"], after)


def _other_chips():
    x, y, c = _place()
    return [(1 - x, y), (x, 1 - y), (1 - x, 1 - y)]


def _gather_plan(srcs, lands):
    x, y, c = _place()
    me = 2 * x + y
    return [(srcs[a], lands[a].at[me], lands[a].at[2 * cx + cy], (cx, cy, c))
            for (cx, cy) in _other_chips() for a in range(len(srcs))]


def _gather_begin(shards, tag, deps=()):
    shapes = [_sds((N_CHIPS,) + a.shape, a.dtype) for a in shards]
    return _xchg_begin(f"gather_{tag}", _gather_plan, 3 * len(shards), shards, shapes, deps)


def _gather_end(h, after):
    shards, lands = _xchg_end(h, after)
    me = 2 * lax.axis_index("x") + lax.axis_index("y")
    return [lax.dynamic_update_index_in_dim(g, s, me, 0) for g, s in zip(lands, shards)]


def _gather_half_plan(srcs, lands):
    x, y, c = _place()
    me = 2 * x + y
    out = []
    for (cx, cy) in _other_chips():
        out.append((srcs[0].at[c], lands[0].at[me, c], lands[0].at[2 * cx + cy, c], (cx, cy, c)))
        out += [(srcs[a], lands[a].at[me], lands[a].at[2 * cx + cy], (cx, cy, c)) for a in range(1, len(srcs))]
    return out


def _forward_plan(bufs, _):
    x, y, c = _place()
    return [(bufs[0].at[2 * cx + cy, c], bufs[0].at[2 * cx + cy, c], bufs[0].at[2 * cx + cy, 1 - c], (x, y, 1 - c))
            for (cx, cy) in _other_chips()]


def _swap_plan(srcs, lands):
    x, y, c = _place()
    return [(srcs[a].at[:, 1 - c], lands[a], lands[a], (x, y, 1 - c)) for a in range(len(srcs))]


def _chips_plan(srcs, lands):
    x, y, c = _place()
    me = 2 * x + y
    return [(srcs[a].at[2 * cx + cy], lands[a].at[me], lands[a].at[2 * cx + cy], (cx, cy, c))
            for (cx, cy) in _other_chips() for a in range(len(srcs))]


def _share_plan(srcs, lands):
    x, y, c = _place()
    return [(srcs[a], lands[a].at[c], lands[a].at[1 - c], (x, y, 1 - c)) for a in range(len(srcs))]


def _allreduce_small(slab, dep=None):
    r = slab.shape[0]

    def body(s_ref, o_ref, gath, send_sems, recv_sems):
        x, y, c = _place()
        me = 4 * x + 2 * y + c
        gath[me] = s_ref[...]
        cps = []
        for rel in range(1, N_DEV):
            px = 1 - x if rel & 4 else x
            py = 1 - y if rel & 2 else y
            pc = 1 - c if rel & 1 else c
            cp = pltpu.make_async_remote_copy(src_ref=s_ref, dst_ref=gath.at[me], send_sem=send_sems.at[rel - 1],
                                              recv_sem=recv_sems.at[rel - 1], device_id=(px, py, pc), device_id_type=MESH_T)
            cp.start()
            cps.append(cp)
        for cp in cps:
            cp.wait()
        acc = gath[0]
        for d in range(1, N_DEV):
            acc = acc + gath[d]
        o_ref[...] = acc

    vm = pl.BlockSpec(memory_space=pltpu.VMEM)
    return _call_after(
        dep, body, (slab,), name="allreduce_small", in_specs=[vm], out_specs=vm, out_shape=_sds((r, LANE)),
        scratch_shapes=[pltpu.VMEM((N_DEV, r, LANE), F32), pltpu.SemaphoreType.DMA((N_DEV - 1,)),
                        pltpu.SemaphoreType.DMA((N_DEV - 1,))],
    )


def _add_mine(g4s, recvs, half):
    n = len(g4s)

    def body(h_ref, *refs):
        for g_ref, r_ref, o_ref in zip(refs[:n], refs[n:2 * n], refs[2 * n:]):
            o_ref[0] = (g_ref[0, 0] + r_ref[0]).astype(o_ref.dtype)

    dims = [g.shape[2:] for g in g4s]
    return pl.pallas_call(
        body, name="add_mine",
        grid_spec=pltpu.PrefetchScalarGridSpec(
            num_scalar_prefetch=1, grid=(N_CHIPS,),
            in_specs=[pl.BlockSpec((1, 1) + d, lambda j, h: (j, h[0], 0, 0)) for d in dims]
            + [pl.BlockSpec((1,) + d, lambda j, h: (j, 0, 0)) for d in dims],
            out_specs=[pl.BlockSpec((1,) + d, lambda j, h: (j, 0, 0)) for d in dims]),
        out_shape=[_sds((N_CHIPS,) + d, PAYLOAD) for d in dims],
        compiler_params=_params(("parallel",)),
    )(half, *g4s, *recvs)


def _add_chips(es, ps, me):
    n = len(es)

    def body(m_ref, *refs):
        for e_ref, p_ref, o_ref in zip(refs[:n], refs[n:2 * n], refs[2 * n:]):
            own = p_ref[0].astype(F32)
            acc = None
            for s in range(N_CHIPS):
                t = jnp.where(m_ref[0] == s, own, e_ref[s].astype(F32))
                acc = t if acc is None else acc + t
            o_ref[...] = acc

    dims = [e.shape[1:] for e in es]
    return pl.pallas_call(
        body, name="add_chips",
        grid_spec=pltpu.PrefetchScalarGridSpec(
            num_scalar_prefetch=1, grid=(1,),
            in_specs=[pl.BlockSpec((N_CHIPS,) + d, lambda i, m: (0, 0, 0)) for d in dims]
            + [pl.BlockSpec((1,) + d, lambda i, m: (m[0], 0, 0)) for d in dims],
            out_specs=[pl.BlockSpec(d, lambda i, m: (0, 0)) for d in dims]),
        out_shape=[_sds(d) for d in dims],
        compiler_params=_params(("arbitrary",)),
    )(me, *es, *ps)


def _rs_begin(gs, tag, deps=()):
    g4 = [g.reshape(N_CHIPS, 2, g.shape[0] // (2 * N_CHIPS), g.shape[1]) for g in gs]
    h = _xchg_begin(f"rs_swap_{tag}", _swap_plan, len(gs), g4, [_sds((N_CHIPS,) + g.shape[2:]) for g in g4], deps)
    return dict(h=h, tag=tag, shapes=[g.shape for g in gs])


def _rs_add_mine(st, after):
    g4, recv = _xchg_end(st["h"], after)
    half = jnp.reshape(lax.axis_index("c"), (1,)).astype(jnp.int32)
    ps = _add_mine(g4, recv, half)
    st["h"] = _xchg_begin(f"rs_chips_{st['tag']}", _chips_plan, 3 * len(ps), ps, [_sds(p.shape, p.dtype) for p in ps])
    return st


def _rs_add_chips(st, after):
    ps, es = _xchg_end(st["h"], after)
    me = jnp.reshape(2 * lax.axis_index("x") + lax.axis_index("y"), (1,)).astype(jnp.int32)
    fs = _add_chips(es, ps, me)
    st["h"] = _xchg_begin(f"rs_share_{st['tag']}", _share_plan, len(fs), fs, [_sds((2,) + f.shape) for f in fs])
    return st


def _rs_end(st, after):
    fs, ss = _xchg_end(st["h"], after)
    c = lax.axis_index("c")
    return [lax.dynamic_update_index_in_dim(s, f, c, 0).reshape(shp[0] // N_CHIPS, shp[1])
            for s, f, shp in zip(ss, fs, st["shapes"])]


WEIGHTS = ["norm_g", "w_in", "conv_a_w", "ssd_conv_w", "ssd_conv_b", "ssd_dt_bias", "ssd_a_log", "ssd_d", "ssd_norm_g",
           "mla_q_norm_g", "w_qb", "mla_kv_norm_g", "w_kvb", "w_out", "final_norm_g"]
BIG = ["w_in", "w_qb", "w_kvb", "w_out"]
SLAB_ROWS = 128


def _to_slab(parts, rows):
    flat = jnp.concatenate([p.reshape(-1) for p in parts])
    return jnp.pad(flat, (0, rows * LANE - flat.shape[0])).reshape(rows, LANE)


def _from_slab(slab, shapes):
    flat = slab.reshape(-1)
    out, off = [], 0
    for shp in shapes:
        n = int(np.prod(shp))
        out.append(flat[off:off + n].reshape(shp))
        off += n
    return out


def kernel(x, positions, norm_g, w_in, conv_a_w, ssd_conv_w, ssd_conv_b, ssd_dt_bias, ssd_a_log, ssd_d, ssd_norm_g, mla_q_norm_g, w_qb, mla_kv_norm_g, w_kvb, w_out, final_norm_g, loss_target, m_norm_g, m_w_in, m_conv_a_w, m_ssd_conv_w, m_ssd_conv_b, m_ssd_dt_bias, m_ssd_a_log, m_ssd_d, m_ssd_norm_g, m_mla_q_norm_g, m_w_qb, m_mla_kv_norm_g, m_w_kvb, m_w_out, m_final_norm_g, v_norm_g, v_w_in, v_conv_a_w, v_ssd_conv_w, v_ssd_conv_b, v_ssd_dt_bias, v_ssd_a_log, v_ssd_d, v_ssd_norm_g, v_mla_q_norm_g, v_w_qb, v_mla_kv_norm_g, v_w_kvb, v_w_out, v_final_norm_g):
    w = dict(norm_g=norm_g, w_in=w_in, conv_a_w=conv_a_w, ssd_conv_w=ssd_conv_w, ssd_conv_b=ssd_conv_b,
             ssd_dt_bias=ssd_dt_bias, ssd_a_log=ssd_a_log, ssd_d=ssd_d, ssd_norm_g=ssd_norm_g, mla_q_norm_g=mla_q_norm_g,
             w_qb=w_qb, mla_kv_norm_g=mla_kv_norm_g, w_kvb=w_kvb, w_out=w_out, final_norm_g=final_norm_g)
    mom = dict(norm_g=m_norm_g, w_in=m_w_in, conv_a_w=m_conv_a_w, ssd_conv_w=m_ssd_conv_w, ssd_conv_b=m_ssd_conv_b,
               ssd_dt_bias=m_ssd_dt_bias, ssd_a_log=m_ssd_a_log, ssd_d=m_ssd_d, ssd_norm_g=m_ssd_norm_g,
               mla_q_norm_g=m_mla_q_norm_g, w_qb=m_w_qb, mla_kv_norm_g=m_mla_kv_norm_g, w_kvb=m_w_kvb, w_out=m_w_out,
               final_norm_g=m_final_norm_g)
    var = dict(norm_g=v_norm_g, w_in=v_w_in, conv_a_w=v_conv_a_w, ssd_conv_w=v_ssd_conv_w, ssd_conv_b=v_ssd_conv_b,
               ssd_dt_bias=v_ssd_dt_bias, ssd_a_log=v_ssd_a_log, ssd_d=v_ssd_d, ssd_norm_g=v_ssd_norm_g,
               mla_q_norm_g=v_mla_q_norm_g, w_qb=v_w_qb, mla_kv_norm_g=v_mla_kv_norm_g, w_kvb=v_w_kvb, w_out=v_w_out,
               final_norm_g=v_final_norm_g)
    chip = 2 * lax.axis_index("x") + lax.axis_index("y")

    def early_shard(l, zero):
        pack = jnp.pad(conv_a_w[l], ((0, 5), (0, 192))) + jnp.pad(ssd_conv_w[l], ((3, 1), (0, 32)))
        return [(_perm_cols(w_in[l]) + zero).astype(MXU), pack + zero]

    def late_shard(l, zero):
        return [(w_out[l] + zero).astype(MXU), (w_qb[l].T + zero).astype(MXU), (w_kvb[l].T + zero).astype(MXU)]

    def early_weights(l, gathered):
        g_in, g_conv = gathered
        return dict(
            norm_g=norm_g[l][None], w_in=g_in.reshape(D_MODEL, NCOL),
            conv_a_w=jnp.concatenate([g_conv[j, 0:3, 0:64] for j in range(N_CHIPS)], axis=1),
            ssd_conv_w=jnp.concatenate([g_conv[j, 3:7, 0:224] for j in range(N_CHIPS)], axis=1),
            ssd_conv_b=ssd_conv_b[l][None], sc=_ssd_scalars(ssd_dt_bias[l], ssd_a_log[l], ssd_d[l]),
            g_ssd=ssd_norm_g[l][None], gq=mla_q_norm_g[l][None], gkv=mla_kv_norm_g[l][None])

    def late_weights(gathered):
        g_out, g_qb, g_kvb = gathered
        return dict(wq=_wq_layout(g_qb.reshape(MLA_HEADS * 96, Q_LORA)), wkv=_wkv_layout(g_kvb.reshape(MLA_HEADS * LANE, KV_LORA)),
                    w_out=g_out.reshape(D_MODEL, D_MODEL))

    def late_grads(dw_out, dwq, dwkv):
        wq = jnp.pad(_wq_unlayout(dwq).reshape(N_CHIPS, 144, Q_LORA), ((0, 0), (0, 16), (0, 0)))
        return [dw_out, wq.reshape(N_CHIPS * 160, Q_LORA), _wkv_unlayout(dwkv)]

    def large_grads(g):
        return [g["w_in"]] + late_grads(g["w_out"], g["wq"], g["wkv"])

    w_in0, pack0 = early_shard(0, 0.0)
    half = w_in0.shape[0] // 2
    gather_a0 = _xchg_begin("gather_a0", _gather_half_plan, 6, [w_in0.reshape(2, half, NCOL), pack0],
                            [_sds((N_CHIPS, 2, half, NCOL), MXU), _sds((N_CHIPS,) + pack0.shape)])
    zero = gather_a0["token"][0, 0]
    cos, sin = _rope_tables(positions[0] + zero.astype(jnp.int32))
    late0, shards1 = late_shard(0, zero), early_shard(1, zero) + late_shard(1, zero)
    opt_in = {nm: [w[nm], mom[nm], var[nm]] for nm in BIG}
    opt_in["w_in"] = [w["w_in"], mom["w_in"] + zero, var["w_in"] + zero]
    mine0, (g_in0, g_conv0) = _xchg_end(gather_a0, [cos, sin] + late0 + shards1 + opt_in["w_in"][1:])
    forward_a0 = _xchg_begin("forward_a0", _forward_plan, 3, [g_in0], [])
    gather_b0 = _gather_begin(late0, "b0", [forward_a0["token"]])
    gather_1 = _gather_begin(shards1, "1", [gather_b0["token"]])
    (g_in0,), _ = _xchg_end(forward_a0, [gather_1["token"]])
    lw0 = early_weights(0, [lax.dynamic_update_index_in_dim(g, s_, chip, 0) for g, s_ in zip((g_in0, g_conv0), mine0)])
    x1, sv0, lw0 = _layer_fwd(x[0], lw0, cos, sin, gather_1["token"],
                              lambda ya, y_ssd: late_weights(_gather_end(gather_b0, [ya, y_ssd])))
    g1 = _gather_end(gather_1, [x1])
    x2, sv1, lw1 = _layer_fwd(x1, {**early_weights(1, g1[:2]), **late_weights(g1[2:])}, cos, sin)
    dx, dgf, loss = _loss_head(x2, final_norm_g[None], loss_target[0])

    dx, lg1, _, _ = _layer_bwd(dx, lw1, sv1, cos, sin)
    grad_x, lg0, red1, rs0_late = _layer_bwd(dx, lw0, sv0, cos, sin, _rs_begin(large_grads(lg1), 1),
                                             lambda *g: _rs_begin(late_grads(*g), "0l"))
    rs0 = _rs_begin([lg0["w_in"]], 0, [rs0_late["h"]["token"]])
    lg = [lg0, lg1]
    grad = {}

    small_names = ["norm_g", "conv_a_w", "ssd_conv_w", "ssd_conv_b", "sc", "g_ssd", "gq", "gkv"]
    parts = [loss[0, 0:1], dgf]
    for nm in small_names:
        parts += [lg[l][nm][:3, DT_LANE:DT_LANE + SSD_HEADS] if nm == "sc" else lg[l][nm] for l in range(DEPTH)]
    shapes = [(1,), (D_MODEL,)] + [(DEPTH,) + shp for shp in ((D_MODEL,), (3, D_CONV_A), (4, N_XBC), (N_XBC,), (3, SSD_HEADS),
                                                              (D_SSD,), (Q_LORA,), (KV_LORA,))]
    red_slab = _allreduce_small(_to_slab(parts, SLAB_ROWS), rs0["h"]["token"])
    rs0 = _rs_add_mine(rs0, [red_slab])
    red = _from_slab(red_slab + rs0["h"]["token"][0, 0], shapes)
    loss_out = red[0][0]
    grad["final_norm_g"] = red[1]
    grad["norm_g"], conv_a_full, sconv_full, grad["ssd_conv_b"], sc_grads = red[2:7]
    grad["ssd_norm_g"], grad["mla_q_norm_g"], grad["mla_kv_norm_g"] = red[7:10]
    grad["conv_a_w"] = lax.dynamic_slice_in_dim(conv_a_full, chip * 64, 64, axis=2)
    grad["ssd_conv_w"] = lax.dynamic_slice_in_dim(sconv_full, chip * 224, 224, axis=2)
    grad["ssd_dt_bias"], grad["ssd_a_log"], grad["ssd_d"] = sc_grads[:, 0], sc_grads[:, 1], sc_grads[:, 2]

    delta, new_m, new_v = {}, {}, {}
    small = [nm for nm in WEIGHTS if nm not in BIG]
    row2 = lambda a: a[None] if a.ndim == 1 else a
    small_out = _adamw(*[[row2(a[nm]) for nm in small] for a in (w, grad, mom, var)], whole=True)
    for nm, (dv, mv, vv) in zip(small, small_out):
        delta[nm], new_m[nm], new_v[nm] = [a.reshape(w[nm].shape) for a in (dv, mv, vv)]

    r_out, r_qb, r_kvb = [jnp.stack([a, b]) for a, b in zip(_rs_end(rs0_late, [red_slab]), red1[1:])]
    grad.update(w_out=r_out, w_qb=jnp.swapaxes(r_qb[:, :144], 1, 2), w_kvb=jnp.swapaxes(r_kvb, 1, 2))
    late = [nm for nm in BIG if nm != "w_in"]
    late_out = _adamw([opt_in[nm][0] for nm in late], [grad[nm] for nm in late], [opt_in[nm][1] for nm in late],
                      [opt_in[nm][2] for nm in late], whole=False)
    for nm, (dv, mv, vv) in zip(late, late_out):
        delta[nm], new_m[nm], new_v[nm] = dv, mv, vv
    w_in_opt = [[a] for a in opt_in["w_in"]]
    w_in_l1 = _adamw(w_in_opt[0], [_unperm_cols(red1[0])], w_in_opt[1], w_in_opt[2], whole=False, layer=1)

    shadow_work = [a for row in small_out + late_out for a in row] + [grad[nm] for nm in small] + w_in_l1
    r_in0, = _rs_end(_rs_add_chips(rs0, shadow_work), [])
    grad["w_in"], delta["w_in"], new_m["w_in"], new_v["w_in"] = _adamw(
        w_in_opt[0], [_unperm_cols(r_in0)], w_in_opt[1], w_in_opt[2], whole=False, layer=0, into=w_in_l1)

    return (loss_out, grad_x[None], *[grad[nm] for nm in WEIGHTS], *[delta[nm] for nm in WEIGHTS],
            *[new_m[nm] for nm in WEIGHTS], *[new_v[nm] for nm in WEIGHTS])
```

```python
import functools
import math

import numpy as np
import jax
import jax.numpy as jnp
from jax import lax
from jax.experimental import pallas as pl
from jax.experimental.pallas import tpu as pltpu

F32 = jnp.float32
MXU = jnp.bfloat16

D_MODEL = 1024
DEPTH = 2
D_CONV_A = 256
D_SSD = 384
SSD_HEADS = 6
SSD_BC = 256
SSD_CHUNK = 128
SSD_CHUNKS_PER_STEP = 4
SSD_NORM_EPS = 1e-5
MLA_HEADS = 6
Q_LORA = 256
KV_LORA = 128
QK_NOPE = 64
QK_ROPE = 32
V_DIM = 64
D_MLA = 384
ROPE_BASE = 10000.0
NORM_EPS = 1e-6
IN_COLS = 3110
LANE = 128

O_AH, O_AB, O_AC, O_AZ = 0, 256, 512, 768
O_XBC = 1024
O_SZ = 1920
O_CQA = 2304
O_CKV = 2560
O_CZ = 2688
O_TAIL = 3072
NCOL = 3200
N_XBC = D_SSD + 2 * SSD_BC
DT_LANE = 32
ROPE_LANE = 64

ADAM_LR, ADAM_B1, ADAM_B2, ADAM_EPS, ADAM_WD, ADAM_STEP = 0.001, 0.9, 0.999, 1e-08, 0.01, 10

VMEM_LIMIT = 56 * 1024 * 1024
MESH_T = pl.DeviceIdType.MESH


def _dot(a, b):
    return jnp.dot(a.astype(MXU), b.astype(MXU), preferred_element_type=F32)


def _dot_nt(a, b):
    return lax.dot_general(a.astype(MXU), b.astype(MXU), (((1,), (1,)), ((), ())), preferred_element_type=F32)


def _dot_tn(a, b):
    return lax.dot_general(a.astype(MXU), b.astype(MXU), (((0,), (0,)), ((), ())), preferred_element_type=F32)


def _dot_hi(a, b):
    return jnp.dot(a, b, precision=lax.Precision.HIGHEST, preferred_element_type=F32)


def _dot_hi_tn(a, b):
    return lax.dot_general(a, b, (((0,), (0,)), ((), ())), precision=lax.Precision.HIGHEST, preferred_element_type=F32)


def _sigmoid(z):
    return 1.0 / (1.0 + jnp.exp(-z))


def _silu(z):
    return z * _sigmoid(z)


def _dsilu(z):
    s = _sigmoid(z)
    return s * (1.0 + z * (1.0 - s))


def _softplus(z):
    e = jnp.exp(-jnp.abs(z))
    return jnp.maximum(z, 0.0) + jnp.where(e < 1e-3, e * (1.0 - 0.5 * e), jnp.log(1.0 + e))


def _iota(shape, dim):
    return lax.broadcasted_iota(jnp.int32, shape, dim)


def _shift_down(u, k):
    if k == 0:
        return u
    return jnp.where(_iota(u.shape, 0) >= k, pltpu.roll(u, k, 0), 0.0)


def _shift_up(u, k):
    if k == 0:
        return u
    n = u.shape[0]
    return jnp.where(_iota(u.shape, 0) < n - k, pltpu.roll(u, n - k, 0), 0.0)


def _rope_swap(t):
    lane = _iota(t.shape, 1)
    lo = (lane >= ROPE_LANE) & (lane < ROPE_LANE + 16)
    hi = (lane >= ROPE_LANE + 16) & (lane < ROPE_LANE + 32)
    return jnp.where(lo, pltpu.roll(t, LANE - 16, 1), jnp.where(hi, pltpu.roll(t, 16, 1), 0.0))


def _params(sem=None):
    return pltpu.CompilerParams(dimension_semantics=sem, vmem_limit_bytes=VMEM_LIMIT)


def _full(shape):
    nd = len(shape)
    return pl.BlockSpec(shape, lambda *_: (0,) * nd)


def _sds(shape, dtype=F32):
    return jax.ShapeDtypeStruct(shape, dtype)


def _tile(s):
    return min(512, s)


def _row(ts, w):
    return pl.BlockSpec((ts, w), lambda i: (i, 0))


def _gate_cols(ts, off):
    return pl.BlockSpec((ts, D_SSD), lambda i, _o=off // D_SSD: (i, _o))


def _col(s, off):
    return pl.BlockSpec((s, LANE), lambda j, _o=off // LANE: (0, _o + j))


def _call_after(dep, body, args, *, in_specs, **kw):
    if dep is None:
        return pl.pallas_call(body, in_specs=in_specs, **kw)(*args)
    n = len(args)

    def body_dep(*refs):
        body(*refs[:n], *refs[n + 1:])

    return pl.pallas_call(body_dep, in_specs=list(in_specs) + [pl.BlockSpec(memory_space=pl.ANY)], **kw)(*args, dep)


def _rms(c, g):
    r = lax.rsqrt(jnp.mean(c * c, axis=-1, keepdims=True) + NORM_EPS)
    return c * r * g, r


def _rms_bwd(dn, c, r, g):
    ch = c * r
    dch = dn * g
    dc = r * (dch - ch * jnp.mean(dch * ch, axis=-1, keepdims=True))
    return dc, jnp.sum(dn * ch, axis=0, keepdims=True)


def _inproj_fwd(x, g, w, dep=None):
    s = x.shape[0]
    ts = _tile(s)

    def body(x_ref, g_ref, w_ref, proj_ref, h_ref, r_ref):
        hn, r = _rms(x_ref[...], g_ref[...])
        h = hn.astype(MXU)
        h_ref[...] = h
        r_ref[...] = r
        proj_ref[...] = jnp.dot(h, w_ref[...], preferred_element_type=F32)

    return _call_after(
        dep, body, (x, g, w), name="inproj_fwd", grid=(s // ts,),
        in_specs=[_row(ts, D_MODEL), _full((1, D_MODEL)), _full((D_MODEL, NCOL))],
        out_specs=[_row(ts, NCOL), _row(ts, D_MODEL), _row(ts, 1)],
        out_shape=[_sds((s, NCOL)), _sds((s, D_MODEL), MXU), _sds((s, 1))],
        compiler_params=_params(("parallel",)),
    )


def _conva_fwd(proj, w):
    s = proj.shape[0]

    def body(h_ref, b_ref, c_ref, z_ref, w_ref, y_ref):
        u = c_ref[...] * h_ref[...]
        wv = w_ref[...]
        cv = wv[2:3, :] * u + wv[1:2, :] * _shift_down(u, 1) + wv[0:1, :] * _shift_down(u, 2)
        y_ref[...] = b_ref[...] * cv * _silu(z_ref[...])

    return pl.pallas_call(
        body, name="conva_fwd", grid=(D_CONV_A // LANE,),
        in_specs=[_col(s, O_AH), _col(s, O_AB), _col(s, O_AC), _col(s, O_AZ), pl.BlockSpec((3, LANE), lambda j: (0, j))],
        out_specs=pl.BlockSpec((s, LANE), lambda j: (0, j)),
        out_shape=_sds((s, D_CONV_A)),
        compiler_params=_params(("parallel",)),
    )(proj, proj, proj, proj, w)


def _sconv_pre(u, wv, bv):
    return (wv[3:4, :] * u + wv[2:3, :] * _shift_down(u, 1) + wv[1:2, :] * _shift_down(u, 2)
            + wv[0:1, :] * _shift_down(u, 3) + bv)


def _sconv_fwd(proj, w, b):
    s = proj.shape[0]

    def body(u_ref, w_ref, b_ref, o_ref):
        o_ref[...] = _silu(_sconv_pre(u_ref[...], w_ref[...], b_ref[...]))

    return pl.pallas_call(
        body, name="sconv_fwd", grid=(N_XBC // LANE,),
        in_specs=[_col(s, O_XBC), pl.BlockSpec((4, LANE), lambda j: (0, j)), pl.BlockSpec((1, LANE), lambda j: (0, j))],
        out_specs=pl.BlockSpec((s, LANE), lambda j: (0, j)),
        out_shape=_sds((s, N_XBC)),
        compiler_params=_params(("parallel",)),
    )(proj, w, b)


def _ssd_chunk_common(tail, sc):
    l = SSD_CHUNK
    lane = _iota((l, LANE), 1)
    row = _iota((l, LANE), 0)
    tri = (row >= lane).astype(F32)
    a_row = -jnp.exp(sc[1:2, :])
    pre = tail + sc[0:1, :]
    dt = _softplus(pre)
    a_cs = _dot_hi(tri, dt * a_row)
    return lane, row, tri, a_row, pre, dt, a_cs, a_cs.T


def _pick_col(m, lane, k):
    return jnp.sum(jnp.where(lane == k, m, 0.0), axis=1, keepdims=True)


def _pick_row(m, row, k):
    return jnp.sum(jnp.where(row == k, m, 0.0), axis=0, keepdims=True)


def _ssd_fwd(xbc, proj, sc):
    s = xbc.shape[0]
    nc = s // SSD_CHUNK
    l = SSD_CHUNK
    cps = SSD_CHUNKS_PER_STEP

    def body(xbc_ref, tail_ref, sc_ref, y_ref, st_ref, state):
        @pl.when(pl.program_id(0) == 0)
        def _():
            state[...] = jnp.zeros_like(state)

        sc_v = sc_ref[...]
        lane1 = _iota((1, LANE), 1)
        rowp = _iota((LANE, 1), 0)
        d_row = sc_v[2:3, :]
        states = [state[j] for j in range(3)]
        for u in range(cps):
            r = slice(u * l, (u + 1) * l)
            lane, row, _, _, _, dt, a_cs, a_t = _ssd_chunk_common(tail_ref[r, :], sc_v)
            for j in range(3):
                st_ref[u, j] = states[j]
            for j in range(3):
                xpair = xbc_ref[r, LANE * j:LANE * (j + 1)]
                sp = states[j]
                ypair = jnp.zeros((l, LANE), F32)
                new_s = jnp.zeros((LANE, LANE), F32)
                decay = jnp.zeros((LANE, 1), F32)
                for half in range(2):
                    h = 2 * j + half
                    g = h // 3
                    hm = (lane < 64) if half == 0 else (lane >= 64)
                    hrow = (rowp < 64) if half == 0 else (rowp >= 64)
                    ac = _pick_col(a_cs, lane, DT_LANE + h)
                    ar = _pick_row(a_t, row, DT_LANE + h)
                    dtc = _pick_col(dt, lane, DT_LANE + h)
                    alast = jnp.sum(jnp.where(lane1 == l - 1, ar, 0.0), axis=1, keepdims=True)
                    dh = jnp.sum(jnp.where(lane1 == DT_LANE + h, d_row, 0.0), axis=1, keepdims=True)
                    xm = jnp.where(hm, xpair, 0.0)
                    xd = xm * dtc
                    bm = xbc_ref[r, D_SSD + LANE * g:D_SSD + LANE * (g + 1)]
                    cm = xbc_ref[r, D_SSD + SSD_BC + LANE * g:D_SSD + SSD_BC + LANE * (g + 1)]
                    lm = jnp.where(row >= lane, jnp.exp(jnp.minimum(ac - ar, 0.0)), 0.0)
                    y_diag = _dot(_dot_nt(cm, bm) * lm, xd)
                    y_off = jnp.where(hm, _dot_nt(cm, sp), 0.0) * jnp.exp(ac)
                    ypair = ypair + y_diag + y_off + xm * dh
                    new_s = new_s + _dot_tn(xd * jnp.exp(alast - ac), bm)
                    decay = jnp.where(hrow, jnp.exp(alast), decay)
                states[j] = sp * decay + new_s
                y_ref[r, LANE * j:LANE * (j + 1)] = ypair
        for j in range(3):
            state[j] = states[j]

    return pl.pallas_call(
        body, name="ssd_fwd", grid=(nc // cps,),
        in_specs=[pl.BlockSpec((cps * l, N_XBC), lambda c: (c, 0)),
                  pl.BlockSpec((cps * l, LANE), lambda c: (c, O_TAIL // LANE)), _full((8, LANE))],
        out_specs=[pl.BlockSpec((cps * l, D_SSD), lambda c: (c, 0)), pl.BlockSpec((cps, 3, LANE, LANE), lambda c: (c, 0, 0, 0))],
        out_shape=[_sds((s, D_SSD)), _sds((nc, 3, LANE, LANE))],
        scratch_shapes=[pltpu.VMEM((3, LANE, LANE), F32)],
        compiler_params=_params(("arbitrary",)),
    )(xbc, proj, sc)


def _mla_prep_fwd(proj, gq, gkv, wq, wkv, cos, sin):
    s = proj.shape[0]
    ts = _tile(s)
    nh = MLA_HEADS

    def body(cqa_ref, ckv_ref, tail_ref, gq_ref, gkv_ref, wq_ref, wkv_ref, cos_ref, sin_ref,
             q_ref, k_ref, v_ref, qn_ref, kvn_ref, rq_ref, rkv_ref):
        qn, rq = _rms(cqa_ref[...], gq_ref[...])
        kvn, rkv = _rms(ckv_ref[...], gkv_ref[...])
        qn = qn.astype(MXU)
        kvn = kvn.astype(MXU)
        qn_ref[...] = qn
        kvn_ref[...] = kvn
        rq_ref[...] = rq
        rkv_ref[...] = rkv
        q = _dot_nt(qn, wq_ref[...])
        kv = _dot_nt(kvn, wkv_ref[...])
        cosv = cos_ref[...]
        sinv = sin_ref[...]
        lane = _iota((ts, LANE), 1)
        rope_lanes = (lane >= ROPE_LANE) & (lane < ROPE_LANE + QK_ROPE)
        kr = jnp.where(rope_lanes, pltpu.roll(tail_ref[...], ROPE_LANE, 1), 0.0)
        kr = kr * cosv + _rope_swap(kr) * sinv
        for h in range(nh):
            qh = q[:, LANE * h:LANE * (h + 1)]
            q_ref[h] = ((qh * cosv + _rope_swap(qh) * sinv) * ATT_SCALE).astype(MXU)
            k_ref[h] = (kv[:, LANE * h:LANE * (h + 1)] + kr).astype(MXU)
            v_ref[h] = kv[:, LANE * (nh + h):LANE * (nh + h + 1)].astype(MXU)

    head = pl.BlockSpec((nh, ts, LANE), lambda i: (0, i, 0))
    return pl.pallas_call(
        body, name="mla_prep_fwd", grid=(s // ts,),
        in_specs=[pl.BlockSpec((ts, Q_LORA), lambda i: (i, O_CQA // Q_LORA)),
                  pl.BlockSpec((ts, KV_LORA), lambda i: (i, O_CKV // KV_LORA)),
                  pl.BlockSpec((ts, LANE), lambda i: (i, O_TAIL // LANE)),
                  _full((1, Q_LORA)), _full((1, KV_LORA)), _full((nh * LANE, Q_LORA)), _full((2 * nh * LANE, KV_LORA)),
                  _row(ts, LANE), _row(ts, LANE)],
        out_specs=[head, head, head, _row(ts, Q_LORA), _row(ts, KV_LORA), _row(ts, 1), _row(ts, 1)],
        out_shape=[_sds((nh, s, LANE), MXU)] * 3 + [_sds((s, Q_LORA), MXU), _sds((s, KV_LORA), MXU), _sds((s, 1)), _sds((s, 1))],
        compiler_params=_params(("parallel",)),
    )(proj, proj, proj, gq, gkv, wq, wkv, cos, sin)


ATT_SCALE = (QK_NOPE + QK_ROPE) ** -0.5
NEG = -1e30


def _att_tile(s, most):
    return min(most, s // 2)


ATT_FWD_TILE = 1024
ATT_BWD_TILE = 512


def _attn_fwd(q, k, v):
    nh, s, _ = q.shape
    tq = _att_tile(s, ATT_FWD_TILE)
    nq = s // tq

    def body(q_ref, k_ref, v_ref, o_ref, lse_ref):
        i = pl.program_id(1)
        rowi = _iota((tq, tq), 0)
        coli = _iota((tq, tq), 1)
        zero = (jnp.full((tq, 1), NEG, F32), jnp.zeros((tq, 1), F32), jnp.zeros((tq, LANE), F32))
        state = [zero, zero]
        done = [zero, zero]
        for t in range(nq + 1):
            first = t <= i
            qblk = jnp.where(first, i, nq - 1 - i)
            kblk = jnp.where(first, t, t - i - 1)
            qoff = pl.multiple_of(qblk * tq, tq)
            koff = pl.multiple_of(kblk * tq, tq)
            keep = coli <= rowi + jnp.where(kblk == qblk, 0, tq)
            restart = t == i + 1
            for hh in range(2):
                m, lsum, acc = state[hh]
                if t > 0:
                    done[hh] = tuple(jnp.where(restart, a, b) for a, b in zip(state[hh], done[hh]))
                    m = jnp.where(restart, NEG, m)
                    lsum = jnp.where(restart, 0.0, lsum)
                    acc = jnp.where(restart, 0.0, acc)
                sc = _dot_nt(q_ref[hh, pl.ds(qoff, tq), :], k_ref[hh, pl.ds(koff, tq), :])
                sc = jnp.where(keep, sc, NEG)
                m_new = jnp.maximum(m, jnp.max(sc, axis=1, keepdims=True))
                p = jnp.exp(sc - m_new)
                alpha = jnp.exp(m - m_new)
                lsum = alpha * lsum + jnp.sum(p, axis=1, keepdims=True)
                acc = alpha * acc + _dot(p, v_ref[hh, pl.ds(koff, tq), :])
                state[hh] = (m_new, lsum, acc)
        for blk, res in ((i, done), (nq - 1 - i, state)):
            off = pl.multiple_of(blk * tq, tq)
            out = None
            for hh in range(2):
                m, lsum, acc = res[hh]
                o = acc * (1.0 / lsum)
                lse_ref[hh, pl.ds(off, tq), :] = m + jnp.log(lsum)
                out = o if hh == 0 else out + pltpu.roll(o, V_DIM, 1)
            o_ref[pl.ds(off, tq), :] = out

    pair = pl.BlockSpec((2, s, LANE), lambda j, i: (j, 0, 0))
    return pl.pallas_call(
        body, name="attn_fwd", grid=(nh // 2, nq // 2),
        in_specs=[pair, pair, pair],
        out_specs=[pl.BlockSpec((s, LANE), lambda j, i: (0, j)), pl.BlockSpec((2, s, 1), lambda j, i: (j, 0, 0))],
        out_shape=[_sds((s, D_MLA)), _sds((nh, s, 1))],
        compiler_params=_params(("parallel", "arbitrary")),
    )(q, k, v)


def _ssd_gate(y_ssd, s_z, g):
    yz = y_ssd * _silu(s_z)
    g0 = _iota(yz.shape, 1) < D_SSD // 2
    sq = yz * yz
    ms0 = jnp.sum(jnp.where(g0, sq, 0.0), axis=1, keepdims=True) / (D_SSD // 2)
    ms1 = jnp.sum(jnp.where(g0, 0.0, sq), axis=1, keepdims=True) / (D_SSD // 2)
    r = jnp.where(g0, lax.rsqrt(ms0 + SSD_NORM_EPS), lax.rsqrt(ms1 + SSD_NORM_EPS))
    nrm = yz * r
    return nrm * g, nrm, r, g0


def _outproj_fwd(x, proj, ya, y_ssd, o, g_ssd, w):
    s = x.shape[0]
    ts = _tile(s)

    def body(x_ref, sz_ref, cz_ref, ya_ref, ys_ref, o_ref, g_ref, w_ref, xo_ref, y_ref):
        yb = _ssd_gate(ys_ref[...], sz_ref[...], g_ref[...])[0]
        yc = o_ref[...] * _silu(cz_ref[...])
        y = jnp.concatenate([ya_ref[...], yb, yc], axis=1).astype(MXU)
        y_ref[...] = y
        xo_ref[...] = x_ref[...] + jnp.dot(y, w_ref[...], preferred_element_type=F32)

    return pl.pallas_call(
        body, name="outproj_fwd", grid=(s // ts,),
        in_specs=[_row(ts, D_MODEL), _gate_cols(ts, O_SZ), _gate_cols(ts, O_CZ), _row(ts, D_CONV_A), _row(ts, D_SSD),
                  _row(ts, D_MLA), _full((1, D_SSD)), _full((D_MODEL, D_MODEL))],
        out_specs=[_row(ts, D_MODEL), _row(ts, D_MODEL)],
        out_shape=[_sds((s, D_MODEL)), _sds((s, D_MODEL), MXU)],
        compiler_params=_params(("parallel",)),
    )(x, proj, proj, ya, y_ssd, o, g_ssd, w)


def _loss_head(x, g, tgt):
    s = x.shape[0]
    ts = _tile(s)

    def body(x_ref, g_ref, t_ref, dx_ref, dg_ref, loss_ref):
        @pl.when(pl.program_id(0) == 0)
        def _():
            dg_ref[...] = jnp.zeros_like(dg_ref)
            loss_ref[...] = jnp.zeros_like(loss_ref)

        xv = x_ref[...]
        gv = g_ref[...]
        yn, r = _rms(xv, gv)
        e = yn - t_ref[...]
        loss_ref[...] += jnp.sum(jnp.sum(e * e, axis=1, keepdims=True), axis=0, keepdims=True) * (0.5 / D_MODEL)
        dx, dg = _rms_bwd(e * (1.0 / D_MODEL), xv, r, gv)
        dx_ref[...] = dx
        dg_ref[...] += dg

    return pl.pallas_call(
        body, name="loss_head", grid=(s // ts,),
        in_specs=[_row(ts, D_MODEL), _full((1, D_MODEL)), _row(ts, D_MODEL)],
        out_specs=[_row(ts, D_MODEL), _full((1, D_MODEL)), _full((1, LANE))],
        out_shape=[_sds((s, D_MODEL)), _sds((1, D_MODEL)), _sds((1, LANE))],
        compiler_params=_params(("arbitrary",)),
    )(x, g, tgt)


def _outproj_bwd(dout, y, w, proj, y_ssd, o, g_ssd, dep=None):
    s = dout.shape[0]
    ts = _tile(s)

    def body(dout_ref, y_ref, w_ref, sz_ref, cz_ref, ys_ref, o_ref, g_ref,
             dya_ref, dys_ref, dsz_ref, dattn_ref, dcz_ref, dg_ref, dw_ref):
        @pl.when(pl.program_id(0) == 0)
        def _():
            dw_ref[...] = jnp.zeros_like(dw_ref)
            dg_ref[...] = jnp.zeros_like(dg_ref)

        dout_b = dout_ref[...].astype(MXU)
        dw_ref[...] += _dot_tn(y_ref[...], dout_b)
        dy = _dot_nt(dout_b, w_ref[...])
        dya_ref[...] = dy[:, :D_CONV_A]
        dyb = dy[:, D_CONV_A:D_CONV_A + D_SSD]
        sz = sz_ref[...]
        ys = ys_ref[...]
        gv = g_ref[...]
        _, nrm, r, g0 = _ssd_gate(ys, sz, gv)
        dg_ref[...] += jnp.sum(dyb * nrm, axis=0, keepdims=True)
        dn = dyb * gv
        t = dn * nrm
        mean = jnp.where(g0, jnp.sum(jnp.where(g0, t, 0.0), axis=1, keepdims=True),
                         jnp.sum(jnp.where(g0, 0.0, t), axis=1, keepdims=True)) / (D_SSD // 2)
        dyz = r * (dn - nrm * mean)
        dys_ref[...] = dyz * _silu(sz)
        dsz_ref[...] = (dyz * ys * _dsilu(sz)).astype(MXU)
        dyc = dy[:, D_CONV_A + D_SSD:]
        cz = cz_ref[...]
        dattn_ref[...] = dyc * _silu(cz)
        dcz_ref[...] = (dyc * o_ref[...] * _dsilu(cz)).astype(MXU)

    return _call_after(
        dep, body, (dout, y, w, proj, proj, y_ssd, o, g_ssd), name="outproj_bwd", grid=(s // ts,),
        in_specs=[_row(ts, D_MODEL), _row(ts, D_MODEL), _full((D_MODEL, D_MODEL)), _gate_cols(ts, O_SZ), _gate_cols(ts, O_CZ),
                  _row(ts, D_SSD), _row(ts, D_MLA), _full((1, D_SSD))],
        out_specs=[_row(ts, D_CONV_A), _row(ts, D_SSD), _row(ts, D_SSD), _row(ts, D_MLA), _row(ts, D_MLA),
                   _full((1, D_SSD)), _full((D_MODEL, D_MODEL))],
        out_shape=[_sds((s, D_CONV_A)), _sds((s, D_SSD)), _sds((s, D_SSD), MXU), _sds((s, D_MLA)), _sds((s, D_MLA), MXU),
                   _sds((1, D_SSD)), _sds((D_MODEL, D_MODEL))],
        compiler_params=_params(("arbitrary",)),
    )


def _attn_bwd(q, k, v, o, d_o, lse, dep=None):
    nh, s, _ = q.shape
    tq = _att_tile(s, ATT_BWD_TILE)
    nq = s // tq

    def body(q_ref, k_ref, v_ref, o_ref, do_ref, lse_ref, dq_ref, dk_ref, dv_ref, dop, delta):
        i = pl.program_id(1)

        @pl.when(i == 0)
        def _():
            lane = _iota((s, LANE), 1)
            for hh in range(2):
                dov = do_ref[...]
                ov = o_ref[...]
                if hh == 1:
                    dov = pltpu.roll(dov, V_DIM, 1)
                    ov = pltpu.roll(ov, V_DIM, 1)
                dov = jnp.where(lane < V_DIM, dov, 0.0)
                dop[hh] = dov.astype(MXU)
                delta[hh] = jnp.sum(dov * ov, axis=1, keepdims=True)
                dq_ref[hh] = jnp.zeros((s, LANE), F32)

        rowi = _iota((tq, tq), 0)
        coli = _iota((tq, tq), 1)
        z = jnp.zeros((tq, LANE), F32)
        state = [(z, z), (z, z)]
        done = [(z, z), (z, z)]
        for t in range(nq + 1):
            first = t <= nq - 1 - i
            kblk = jnp.where(first, i, nq - 1 - i)
            qblk = jnp.where(first, i + t, t - 1)
            qoff = pl.multiple_of(qblk * tq, tq)
            koff = pl.multiple_of(kblk * tq, tq)
            keep = coli <= rowi + jnp.where(kblk == qblk, 0, tq)
            restart = t == nq - i
            for hh in range(2):
                dk, dv = state[hh]
                if t > 0:
                    done[hh] = tuple(jnp.where(restart, a, b) for a, b in zip(state[hh], done[hh]))
                    dk = jnp.where(restart, 0.0, dk)
                    dv = jnp.where(restart, 0.0, dv)
                kb = k_ref[hh, pl.ds(koff, tq), :]
                qb = q_ref[hh, pl.ds(qoff, tq), :]
                dob = dop[hh, pl.ds(qoff, tq), :]
                sc = jnp.where(keep, _dot_nt(qb, kb), NEG)
                p = jnp.exp(sc - lse_ref[hh, pl.ds(qoff, tq), :])
                dp = _dot_nt(dob, v_ref[hh, pl.ds(koff, tq), :])
                ds = p * (dp - delta[hh, pl.ds(qoff, tq), :])
                dq_ref[hh, pl.ds(qoff, tq), :] += _dot(ds, kb)
                state[hh] = (dk + _dot_tn(ds, qb), dv + _dot_tn(p, dob))
        for blk, res in ((i, done), (nq - 1 - i, state)):
            off = pl.multiple_of(blk * tq, tq)
            for hh in range(2):
                dk_ref[hh, pl.ds(off, tq), :] = res[hh][0]
                dv_ref[hh, pl.ds(off, tq), :] = res[hh][1]

    pair = pl.BlockSpec((2, s, LANE), lambda j, i: (j, 0, 0))
    return _call_after(
        dep, body, (q, k, v, o, d_o, lse), name="attn_bwd", grid=(nh // 2, nq // 2),
        in_specs=[pair, pair, pair, pl.BlockSpec((s, LANE), lambda j, i: (0, j)), pl.BlockSpec((s, LANE), lambda j, i: (0, j)),
                  pl.BlockSpec((2, s, 1), lambda j, i: (j, 0, 0))],
        out_specs=[pair, pair, pair],
        out_shape=[_sds((nh, s, LANE))] * 3,
        scratch_shapes=[pltpu.VMEM((2, s, LANE), MXU), pltpu.VMEM((2, s, 1), F32)],
        compiler_params=_params(("parallel", "arbitrary")),
    )


def _ssd_bwd(xbc, proj, sc, states, dy, dep=None):
    s = xbc.shape[0]
    nc = s // SSD_CHUNK
    l = SSD_CHUNK
    cps = SSD_CHUNKS_PER_STEP

    def body(xbc_ref, tail_ref, sc_ref, st_ref, dy_ref, dxbc_ref, dtail_ref, dsc_ref, dstate):
        @pl.when(pl.program_id(0) == 0)
        def _():
            dstate[...] = jnp.zeros_like(dstate)
            dsc_ref[...] = jnp.zeros_like(dsc_ref)

        sc_v = sc_ref[...]
        lane1 = _iota((1, LANE), 1)
        rowp = _iota((LANE, 1), 0)
        rowl = _iota((l, 1), 0)
        d_row = sc_v[2:3, :]
        dstates = [dstate[j] for j in range(3)]
        for u in reversed(range(cps)):
            dstates = chunk(u, xbc_ref, tail_ref, sc_v, st_ref, dy_ref, dxbc_ref, dtail_ref, dsc_ref, dstates,
                            lane1, rowp, rowl, d_row)
        for j in range(3):
            dstate[j] = dstates[j]

    def chunk(u, xbc_ref, tail_ref, sc_v, st_ref, dy_ref, dxbc_ref, dtail_ref, dsc_ref, dstates, lane1, rowp, rowl, d_row):
        r = slice(u * l, (u + 1) * l)
        dstates = list(dstates)
        lane, row, tri, a_row, pre, dt, a_cs, a_t = _ssd_chunk_common(tail_ref[r, :], sc_v)
        da_col = jnp.zeros((l, LANE), F32)
        da_row = jnp.zeros((LANE, l), F32)
        dt_x = jnp.zeros((l, LANE), F32)
        dd_row = jnp.zeros((1, LANE), F32)
        db = [jnp.zeros((l, LANE), F32), jnp.zeros((l, LANE), F32)]
        dc = [jnp.zeros((l, LANE), F32), jnp.zeros((l, LANE), F32)]
        for j in range(3):
            xpair = xbc_ref[r, LANE * j:LANE * (j + 1)]
            dypair = dy_ref[r, LANE * j:LANE * (j + 1)]
            sp = st_ref[u, j]
            dsp = dstates[j]
            dxpair = jnp.zeros((l, LANE), F32)
            ds_new = jnp.zeros((LANE, LANE), F32)
            decay = jnp.zeros((LANE, 1), F32)
            for half in range(2):
                h = 2 * j + half
                g = h // 3
                hm = (lane < 64) if half == 0 else (lane >= 64)
                hrow = (rowp < 64) if half == 0 else (rowp >= 64)
                ac = _pick_col(a_cs, lane, DT_LANE + h)
                ar = _pick_row(a_t, row, DT_LANE + h)
                dtc = _pick_col(dt, lane, DT_LANE + h)
                alast = jnp.sum(jnp.where(lane1 == l - 1, ar, 0.0), axis=1, keepdims=True)
                dh = jnp.sum(jnp.where(lane1 == DT_LANE + h, d_row, 0.0), axis=1, keepdims=True)
                xm = jnp.where(hm, xpair, 0.0)
                xd = xm * dtc
                dym = jnp.where(hm, dypair, 0.0)
                bm = xbc_ref[r, D_SSD + LANE * g:D_SSD + LANE * (g + 1)]
                cm = xbc_ref[r, D_SSD + SSD_BC + LANE * g:D_SSD + SSD_BC + LANE * (g + 1)]
                lm = jnp.where(row >= lane, jnp.exp(jnp.minimum(ac - ar, 0.0)), 0.0)
                e_in = jnp.exp(ac)
                f_out = jnp.exp(alast - ac)
                e_last = jnp.exp(alast)
                m = _dot_nt(cm, bm) * lm
                y_off = jnp.where(hm, _dot_nt(cm, sp), 0.0) * e_in
                dm = _dot_nt(dym, xd)
                dxd = _dot_tn(m, dym)
                dg = dm * lm
                dye = dym * e_in
                dc[g] = dc[g] + _dot(dg, bm) + _dot(dye, sp)
                db[g] = db[g] + _dot_tn(dg, cm)
                qm = dm * m
                dac = jnp.sum(qm, axis=1, keepdims=True) + jnp.sum(dym * y_off, axis=1, keepdims=True)
                dar = -jnp.sum(qm, axis=0, keepdims=True)
                dxf = jnp.where(hm, _dot_nt(bm, dsp), 0.0)
                db[g] = db[g] + _dot(xd * f_out, dsp)
                dxd = dxd + dxf * f_out
                df = jnp.sum(dxf * xd, axis=1, keepdims=True) * f_out
                dac = dac - df
                s_last = jnp.sum(df, axis=0, keepdims=True)
                ss = jnp.sum(jnp.where(hrow, dsp * sp, 0.0), axis=1, keepdims=True)
                s_last = s_last + e_last * jnp.sum(ss, axis=0, keepdims=True)
                dac = dac + jnp.where(rowl == l - 1, s_last, 0.0)
                ds_new = ds_new + _dot_tn(dye, cm)
                decay = jnp.where(hrow, e_last, decay)
                dxpair = dxpair + dxd * dtc + dym * dh
                dt_x = dt_x + jnp.where(lane == DT_LANE + h, jnp.sum(dxd * xm, axis=1, keepdims=True), 0.0)
                dsum = jnp.sum(jnp.sum(dym * xm, axis=1, keepdims=True), axis=0, keepdims=True)
                dd_row = dd_row + jnp.where(lane1 == DT_LANE + h, dsum, 0.0)
                da_col = da_col + jnp.where(lane == DT_LANE + h, dac, 0.0)
                da_row = da_row + jnp.where(row == DT_LANE + h, dar, 0.0)
            dstates[j] = dsp * decay + ds_new
            dxbc_ref[r, LANE * j:LANE * (j + 1)] = dxpair
        for g in range(2):
            dxbc_ref[r, D_SSD + LANE * g:D_SSD + LANE * (g + 1)] = db[g]
            dxbc_ref[r, D_SSD + SSD_BC + LANE * g:D_SSD + SSD_BC + LANE * (g + 1)] = dc[g]
        dla = _dot_hi_tn(tri, da_col + da_row.T)
        ddt = dt_x + dla * a_row
        dpre = ddt * _sigmoid(pre)
        dtm = (lane >= DT_LANE) & (lane < DT_LANE + SSD_HEADS)
        dtail_ref[r, :] = jnp.where(dtm, dpre, 0.0).astype(MXU)
        dtm1 = (lane1 >= DT_LANE) & (lane1 < DT_LANE + SSD_HEADS)
        dsc_ref[0:1, :] += jnp.where(dtm1, jnp.sum(dpre, axis=0, keepdims=True), 0.0)
        dsc_ref[1:2, :] += jnp.where(dtm1, jnp.sum(dla * dt, axis=0, keepdims=True) * a_row, 0.0)
        dsc_ref[2:3, :] += dd_row
        return dstates

    rev = lambda c: nc // cps - 1 - c
    return _call_after(
        dep, body, (xbc, proj, sc, states, dy), name="ssd_bwd", grid=(nc // cps,),
        in_specs=[pl.BlockSpec((cps * l, N_XBC), lambda c: (rev(c), 0)),
                  pl.BlockSpec((cps * l, LANE), lambda c: (rev(c), O_TAIL // LANE)), _full((8, LANE)),
                  pl.BlockSpec((cps, 3, LANE, LANE), lambda c: (rev(c), 0, 0, 0)),
                  pl.BlockSpec((cps * l, D_SSD), lambda c: (rev(c), 0))],
        out_specs=[pl.BlockSpec((cps * l, N_XBC), lambda c: (rev(c), 0)), pl.BlockSpec((cps * l, LANE), lambda c: (rev(c), 0)),
                   _full((8, LANE))],
        out_shape=[_sds((s, N_XBC)), _sds((s, LANE), MXU), _sds((8, LANE))],
        scratch_shapes=[pltpu.VMEM((3, LANE, LANE), F32)],
        compiler_params=_params(("arbitrary",)),
    )


def _sconv_bwd(proj, w, b, dxbc, dep=None):
    s = proj.shape[0]

    def body(u_ref, w_ref, b_ref, d_ref, du_ref, dw_ref, db_ref):
        u = u_ref[...]
        wv = w_ref[...]
        dpre = d_ref[...] * _dsilu(_sconv_pre(u, wv, b_ref[...]))
        ahead = [_shift_up(dpre, j) for j in range(4)]
        du_ref[...] = (wv[3:4, :] * ahead[0] + wv[2:3, :] * ahead[1] + wv[1:2, :] * ahead[2]
                       + wv[0:1, :] * ahead[3]).astype(MXU)
        for k in range(4):
            dw_ref[k:k + 1, :] = jnp.sum(ahead[3 - k] * u, axis=0, keepdims=True)
        db_ref[...] = jnp.sum(dpre, axis=0, keepdims=True)

    blk = pl.BlockSpec((s, LANE), lambda j: (0, j))
    return _call_after(
        dep, body, (proj, w, b, dxbc), name="sconv_bwd", grid=(N_XBC // LANE,),
        in_specs=[_col(s, O_XBC), pl.BlockSpec((4, LANE), lambda j: (0, j)), pl.BlockSpec((1, LANE), lambda j: (0, j)), blk],
        out_specs=[blk, pl.BlockSpec((4, LANE), lambda j: (0, j)), pl.BlockSpec((1, LANE), lambda j: (0, j))],
        out_shape=[_sds((s, N_XBC), MXU), _sds((4, N_XBC)), _sds((1, N_XBC))],
        compiler_params=_params(("parallel",)),
    )


def _conva_bwd(proj, w, dya, dep=None):
    s = proj.shape[0]

    def body(h_ref, b_ref, c_ref, z_ref, w_ref, d_ref, da_ref, dw_ref):
        ah, ab, acv, az = h_ref[...], b_ref[...], c_ref[...], z_ref[...]
        wv = w_ref[...]
        u = acv * ah
        cv = wv[2:3, :] * u + wv[1:2, :] * _shift_down(u, 1) + wv[0:1, :] * _shift_down(u, 2)
        dy = d_ref[...]
        sz = _silu(az)
        da_ref[1] = (dy * cv * sz).astype(MXU)
        da_ref[3] = (dy * ab * cv * _dsilu(az)).astype(MXU)
        dcv = dy * ab * sz
        ahead = [_shift_up(dcv, j) for j in range(3)]
        du = wv[2:3, :] * ahead[0] + wv[1:2, :] * ahead[1] + wv[0:1, :] * ahead[2]
        da_ref[0] = (du * acv).astype(MXU)
        da_ref[2] = (du * ah).astype(MXU)
        for k in range(3):
            dw_ref[k:k + 1, :] = jnp.sum(ahead[2 - k] * u, axis=0, keepdims=True)

    return _call_after(
        dep, body, (proj, proj, proj, proj, w, dya), name="conva_bwd", grid=(D_CONV_A // LANE,),
        in_specs=[_col(s, O_AH), _col(s, O_AB), _col(s, O_AC), _col(s, O_AZ), pl.BlockSpec((3, LANE), lambda j: (0, j)),
                  pl.BlockSpec((s, LANE), lambda j: (0, j))],
        out_specs=[pl.BlockSpec((4, s, LANE), lambda j: (0, 0, j)), pl.BlockSpec((3, LANE), lambda j: (0, j))],
        out_shape=[_sds((4, s, D_CONV_A), MXU), _sds((3, D_CONV_A))],
        compiler_params=_params(("parallel",)),
    )


def _mla_prep_bwd(dq, dk, dv, proj, qn, kvn, rq, rkv, gq, gkv, wq, wkv, cos, sin):
    s = proj.shape[0]
    ts = _tile(s)
    nh = MLA_HEADS

    def body(dq_ref, dk_ref, dv_ref, cqa_ref, ckv_ref, qn_ref, kvn_ref, rq_ref, rkv_ref, gq_ref, gkv_ref,
             wq_ref, wkv_ref, cos_ref, sin_ref, dcqa_ref, dckv_ref, dtail_ref, dwq_ref, dwkv_ref, dgq_ref, dgkv_ref):
        @pl.when(pl.program_id(0) == 0)
        def _():
            dwq_ref[...] = jnp.zeros_like(dwq_ref)
            dwkv_ref[...] = jnp.zeros_like(dwkv_ref)
            dgq_ref[...] = jnp.zeros_like(dgq_ref)
            dgkv_ref[...] = jnp.zeros_like(dgkv_ref)

        cosv = cos_ref[...]
        sinv = sin_ref[...]
        lane = _iota((ts, LANE), 1)
        rope_lanes = (lane >= ROPE_LANE) & (lane < ROPE_LANE + QK_ROPE)

        def unrope(gr):
            return gr * cosv + _rope_swap(gr * sinv)

        dqs, dks, dvs = [], [], []
        dkr = jnp.zeros((ts, LANE), F32)
        for h in range(nh):
            dqs.append(unrope(dq_ref[h] * ATT_SCALE).astype(MXU))
            dkh = dk_ref[h]
            dks.append(jnp.where(lane < QK_NOPE, dkh, 0.0).astype(MXU))
            dkr = dkr + jnp.where(rope_lanes, dkh, 0.0)
            dvs.append(dv_ref[h].astype(MXU))
        dtail_ref[...] = pltpu.roll(jnp.where(rope_lanes, unrope(dkr), 0.0), ROPE_LANE, 1).astype(MXU)
        dq_all = jnp.concatenate(dqs, axis=1)
        dkv_all = jnp.concatenate(dks + dvs, axis=1)
        dwq_ref[...] += _dot_tn(dq_all, qn_ref[...])
        dwkv_ref[...] += _dot_tn(dkv_all, kvn_ref[...])
        dcqa, dgq = _rms_bwd(_dot(dq_all, wq_ref[...]), cqa_ref[...], rq_ref[...], gq_ref[...])
        dckv, dgkv = _rms_bwd(_dot(dkv_all, wkv_ref[...]), ckv_ref[...], rkv_ref[...], gkv_ref[...])
        dcqa_ref[...] = dcqa.astype(MXU)
        dckv_ref[...] = dckv.astype(MXU)
        dgq_ref[...] += dgq
        dgkv_ref[...] += dgkv

    head = pl.BlockSpec((nh, ts, LANE), lambda i: (0, i, 0))
    return pl.pallas_call(
        body, name="mla_prep_bwd", grid=(s // ts,),
        in_specs=[head, head, head,
                  pl.BlockSpec((ts, Q_LORA), lambda i: (i, O_CQA // Q_LORA)),
                  pl.BlockSpec((ts, KV_LORA), lambda i: (i, O_CKV // KV_LORA)),
                  _row(ts, Q_LORA), _row(ts, KV_LORA), _row(ts, 1), _row(ts, 1),
                  _full((1, Q_LORA)), _full((1, KV_LORA)), _full((nh * LANE, Q_LORA)), _full((2 * nh * LANE, KV_LORA)),
                  _row(ts, LANE), _row(ts, LANE)],
        out_specs=[_row(ts, Q_LORA), _row(ts, KV_LORA), _row(ts, LANE), _full((nh * LANE, Q_LORA)),
                   _full((2 * nh * LANE, KV_LORA)), _full((1, Q_LORA)), _full((1, KV_LORA))],
        out_shape=[_sds((s, Q_LORA), MXU), _sds((s, KV_LORA), MXU), _sds((s, LANE), MXU), _sds((nh * LANE, Q_LORA)),
                   _sds((2 * nh * LANE, KV_LORA)), _sds((1, Q_LORA)), _sds((1, KV_LORA))],
        compiler_params=_params(("arbitrary",)),
    )(dq, dk, dv, proj, proj, qn, kvn, rq, rkv, gq, gkv, wq, wkv, cos, sin)


def _inproj_bwd(da4, dsz, dxbc_in, dcqa, dckv, dcz, dtail_a, dtail_b, w, x, rstd, g, dout, dep=None):
    s = x.shape[0]
    ts = _tile(s)

    def body(da_ref, dsz_ref, dxbc_ref, dcqa_ref, dckv_ref, dcz_ref, dta_ref, dtb_ref, w_ref, x_ref, r_ref, g_ref, dout_ref,
             dproj_ref, dx_ref, dg_ref):
        @pl.when(pl.program_id(0) == 0)
        def _():
            dg_ref[...] = jnp.zeros_like(dg_ref)

        dproj = jnp.concatenate(
            [da_ref[0], da_ref[1], da_ref[2], da_ref[3], dxbc_ref[...], dsz_ref[...], dcqa_ref[...], dckv_ref[...],
             dcz_ref[...], dta_ref[...] + dtb_ref[...]], axis=1)
        dproj_ref[...] = dproj
        dh = _dot_nt(dproj, w_ref[...])
        dx, dg = _rms_bwd(dh, x_ref[...], r_ref[...], g_ref[...])
        dx_ref[...] = dout_ref[...] + dx
        dg_ref[...] += dg

    return _call_after(
        dep, body, (da4, dsz, dxbc_in, dcqa, dckv, dcz, dtail_a, dtail_b, w, x, rstd, g, dout), name="inproj_bwd", grid=(s // ts,),
        in_specs=[pl.BlockSpec((4, ts, D_CONV_A), lambda i: (0, i, 0)), _row(ts, D_SSD), _row(ts, N_XBC), _row(ts, Q_LORA),
                  _row(ts, KV_LORA), _row(ts, D_MLA), _row(ts, LANE), _row(ts, LANE), _full((D_MODEL, NCOL)),
                  _row(ts, D_MODEL), _row(ts, 1), _full((1, D_MODEL)), _row(ts, D_MODEL)],
        out_specs=[_row(ts, NCOL), _row(ts, D_MODEL), _full((1, D_MODEL))],
        out_shape=[_sds((s, NCOL), MXU), _sds((s, D_MODEL)), _sds((1, D_MODEL))],
        compiler_params=_params(("arbitrary",)),
    )


DWIN_BLOCK = 640


def _dwin(h, dproj, dep=None):
    s = h.shape[0]

    def body(h_ref, d_ref, o_ref):
        o_ref[...] = _dot_tn(h_ref[...], d_ref[...])

    return _call_after(
        dep, body, (h, dproj), name="dwin", grid=(NCOL // DWIN_BLOCK,),
        in_specs=[_full((s, D_MODEL)), pl.BlockSpec((s, DWIN_BLOCK), lambda j: (0, j))],
        out_specs=pl.BlockSpec((D_MODEL, DWIN_BLOCK), lambda j: (0, j)),
        out_shape=_sds((D_MODEL, NCOL)),
        compiler_params=_params(("parallel",)),
    )


def _adamw(ws, gs, ms, vs, whole):
    n = len(ws)
    bc1 = 1.0 - ADAM_B1 ** ADAM_STEP
    bc2 = 1.0 - ADAM_B2 ** ADAM_STEP

    def body(*refs):
        ins, outs = refs[:4 * n], refs[4 * n:]
        for a in range(n):
            w_ref, g_ref, m_ref, v_ref = ins[a], ins[n + a], ins[2 * n + a], ins[3 * n + a]
            gv = g_ref[...]
            mn = ADAM_B1 * m_ref[...] + (1.0 - ADAM_B1) * gv
            vn = ADAM_B2 * v_ref[...] + (1.0 - ADAM_B2) * (gv * gv)
            outs[n + a][...] = mn
            outs[2 * n + a][...] = vn
            outs[a][...] = -ADAM_LR * ((mn / bc1) / (jnp.sqrt(vn / bc2) + ADAM_EPS) + ADAM_WD * w_ref[...])

    if whole:
        grid, blks = (1,), [pl.BlockSpec(w.shape, lambda i, _n=w.ndim: (0,) * _n) for w in ws]
    else:
        grid = (ws[0].shape[0], 2)
        blks = [pl.BlockSpec((1, w.shape[1] // 2, w.shape[2]), lambda i, k: (i, k, 0)) for w in ws]
    out = pl.pallas_call(
        body, name="adamw", grid=grid,
        in_specs=blks * 4, out_specs=blks * 3, out_shape=[_sds(w.shape) for w in ws] * 3,
        compiler_params=_params(("parallel",) * len(grid)),
    )(*ws, *gs, *ms, *vs)
    return [(out[a], out[n + a], out[2 * n + a]) for a in range(n)]


def _adamw_cols(w_t, gs, m_t, v_t):
    cols, nl, rows = w_t.shape
    bc1 = 1.0 - ADAM_B1 ** ADAM_STEP
    bc2 = 1.0 - ADAM_B2 ** ADAM_STEP

    def body(w_ref, m_ref, v_ref, *rest):
        g_refs, (go_ref, d_ref, mo_ref, vo_ref) = rest[:nl], rest[nl:]
        for l in range(nl):
            gv = g_refs[l][...].T
            mn = ADAM_B1 * m_ref[:, l, :] + (1.0 - ADAM_B1) * gv
            vn = ADAM_B2 * v_ref[:, l, :] + (1.0 - ADAM_B2) * (gv * gv)
            go_ref[:, l, :] = gv
            mo_ref[:, l, :] = mn
            vo_ref[:, l, :] = vn
            d_ref[:, l, :] = -ADAM_LR * ((mn / bc1) / (jnp.sqrt(vn / bc2) + ADAM_EPS) + ADAM_WD * w_ref[:, l, :])

    blk = pl.BlockSpec((LANE, nl, rows), lambda j: (j, 0, 0))
    gblk = pl.BlockSpec((rows, LANE), lambda j: (0, j))
    return pl.pallas_call(
        body, name="adamw_cols", grid=(pl.cdiv(cols, LANE),),
        in_specs=[blk] * 3 + [gblk] * nl, out_specs=[blk] * 4, out_shape=[_sds(w_t.shape)] * 4,
        compiler_params=_params(("parallel",)),
    )(w_t, m_t, v_t, *gs)


COL_MOVES = ((0, 0, 1024), (1024, O_SZ, 384), (1408, O_XBC, 896), (2304, O_TAIL + DT_LANE, 6), (2310, O_CQA, 256),
             (2566, O_CKV, 128), (2694, O_TAIL, 32), (2726, O_CZ, 384))


def _move_cols(w, moves, width):
    out = None
    for src, dst, n in moves:
        piece = jnp.pad(w[..., src:src + n], [(0, 0)] * (w.ndim - 1) + [(dst, width - dst - n)])
        out = piece if out is None else out + piece
    return out


def _perm_cols(w):
    return _move_cols(w, COL_MOVES, NCOL)


def _unperm_cols(g):
    return _move_cols(g, [(dst, src, n) for src, dst, n in COL_MOVES], IN_COLS)


def _wq_layout(wt):
    return jnp.pad(wt.reshape(MLA_HEADS, QK_NOPE + QK_ROPE, Q_LORA), ((0, 0), (0, 32), (0, 0))).reshape(MLA_HEADS * LANE, Q_LORA)


def _wq_unlayout(g):
    return g.reshape(MLA_HEADS, LANE, Q_LORA)[:, :QK_NOPE + QK_ROPE].reshape(MLA_HEADS * (QK_NOPE + QK_ROPE), Q_LORA)


def _wkv_layout(wt):
    t = wt.reshape(MLA_HEADS, 2, 64, KV_LORA).transpose(1, 0, 2, 3)
    return jnp.pad(t, ((0, 0), (0, 0), (0, 64), (0, 0))).reshape(2 * MLA_HEADS * LANE, KV_LORA)


def _wkv_unlayout(g):
    t = g.reshape(2, MLA_HEADS, LANE, KV_LORA)[:, :, :64]
    return t.transpose(1, 0, 2, 3).reshape(MLA_HEADS * LANE, KV_LORA)


def _rope_tables(positions):
    inv_freq = ROPE_BASE ** (-jnp.arange(0, QK_ROPE, 2, dtype=F32) / QK_ROPE)
    ang = positions.astype(F32)[:, None] * inv_freq
    cos, sin = jnp.cos(ang), jnp.sin(ang)
    s = positions.shape[0]
    one, zero = jnp.ones((s, ROPE_LANE), F32), jnp.zeros((s, ROPE_LANE), F32)
    cos_t = jnp.concatenate([one, cos, cos, one[:, :32]], axis=1)
    sin_t = jnp.concatenate([zero, -sin, sin, zero[:, :32]], axis=1)
    return cos_t, sin_t


def _ssd_scalars(dt_bias, a_log, d_skip):
    return jnp.pad(jnp.stack([dt_bias, a_log, d_skip]), ((0, 5), (DT_LANE, LANE - DT_LANE - SSD_HEADS)))


def _layer_fwd(x, lw, cos, sin, dep=None, late=None):
    proj, h, rstd = _inproj_fwd(x, lw["norm_g"], lw["w_in"], dep)
    ya = _conva_fwd(proj, lw["conv_a_w"])
    xbc = _sconv_fwd(proj, lw["ssd_conv_w"], lw["ssd_conv_b"])
    y_ssd, states = _ssd_fwd(xbc, proj, lw["sc"])
    if late is not None:
        lw = {**lw, **late(ya, y_ssd)}
    q, k, v, qn, kvn, rq, rkv = _mla_prep_fwd(proj, lw["gq"], lw["gkv"], lw["wq"], lw["wkv"], cos, sin)
    o, lse = _attn_fwd(q, k, v)
    x_out, y = _outproj_fwd(x, proj, ya, y_ssd, o, lw["g_ssd"], lw["w_out"])
    saved = dict(x=x, proj=proj, h=h, rstd=rstd, xbc=xbc, y_ssd=y_ssd, states=states, q=q, k=k, v=v, qn=qn, kvn=kvn,
                 rq=rq, rkv=rkv, o=o, lse=lse, y=y)
    return x_out, saved, lw


def _layer_bwd(dout, lw, sv, cos, sin, rs=None, begin_early=None):
    tok = lambda: None if rs is None else rs["h"]["token"]
    dya, dys, dsz, d_o, dcz, dg_ssd, dw_out = _outproj_bwd(dout, sv["y"], lw["w_out"], sv["proj"], sv["y_ssd"], sv["o"],
                                                            lw["g_ssd"], tok())
    if rs is not None:
        rs = _rs_add_mine(rs, [dya])
    dq, dk, dv = _attn_bwd(sv["q"], sv["k"], sv["v"], sv["o"], d_o, sv["lse"], tok())
    dxbc, dtail_s, dsc = _ssd_bwd(sv["xbc"], sv["proj"], lw["sc"], sv["states"], dys, tok())
    da4, dw_conva = _conva_bwd(sv["proj"], lw["conv_a_w"], dya, tok())
    if rs is not None:
        rs = _rs_add_chips(rs, [dq, dxbc, da4])
    du, dw_sconv, db_sconv = _sconv_bwd(sv["proj"], lw["ssd_conv_w"], lw["ssd_conv_b"], dxbc, tok())
    dcqa, dckv, dtail_m, dwq, dwkv, dgq, dgkv = _mla_prep_bwd(
        dq, dk, dv, sv["proj"], sv["qn"], sv["kvn"], sv["rq"], sv["rkv"], lw["gq"], lw["gkv"], lw["wq"], lw["wkv"], cos, sin)
    early = None if begin_early is None else begin_early(dw_out, dwq, dwkv)
    etok = lambda: None if early is None else early["h"]["token"]
    dproj, dx, dg = _inproj_bwd(da4, dsz, du, dcqa, dckv, dcz, dtail_s, dtail_m, lw["w_in"], sv["x"], sv["rstd"],
                                lw["norm_g"], dout, etok())
    reduced = None if rs is None else _rs_end(rs, [du, dcqa, dx])
    if early is not None:
        early = _rs_add_mine(early, [dx])
    dw_in = _dwin(sv["h"], dproj, etok())
    if early is not None:
        early = _rs_add_chips(early, [dw_in])
    grads = dict(norm_g=dg, w_in=dw_in, conv_a_w=dw_conva, ssd_conv_w=dw_sconv, ssd_conv_b=db_sconv, sc=dsc,
                 g_ssd=dg_ssd, gq=dgq, wq=dwq, gkv=dgkv, wkv=dwkv, w_out=dw_out)
    return dx, grads, reduced, early


ANY = pl.BlockSpec(memory_space=pl.ANY)
N_CHIPS = 4
N_DEV = 8


def _place():
    return lax.axis_index("x"), lax.axis_index("y"), lax.axis_index("c")


HBM_SPEC = pl.BlockSpec(memory_space=pltpu.HBM)
SEM_SPEC = pl.BlockSpec(memory_space=pltpu.SEMAPHORE)
PAYLOAD = jnp.bfloat16


def _hbm(a):
    return pltpu.with_memory_space_constraint(a, pltpu.HBM)


def _run_plan(plan, srcs, lands, send_sems, recv_sems, start, wait):
    copies = plan(srcs, lands)
    if start:
        for i, (src, dst, _, to) in enumerate(copies):
            pltpu.make_async_remote_copy(src_ref=src, dst_ref=dst, send_sem=send_sems.at[i], recv_sem=recv_sems.at[i],
                                         device_id=to, device_id_type=MESH_T).start()
    if wait:
        for i, (src, _, arrives, to) in enumerate(copies):
            cp = pltpu.make_async_remote_copy(src_ref=src, dst_ref=arrives, send_sem=send_sems.at[i],
                                              recv_sem=recv_sems.at[i], device_id=to, device_id_type=MESH_T)
            cp.wait_send()
            cp.wait_recv()


def _exchange_start(name, plan, n_copies, srcs, land_shapes, deps):
    ns, nl = len(srcs), len(land_shapes)
    n_in = ns + nl + len(deps)

    def body(*refs):
        send_sems, recv_sems = refs[n_in], refs[n_in + 1]
        token = refs[-1]
        _run_plan(plan, refs[:ns], refs[ns:ns + nl], send_sems, recv_sems, True, False)
        token[...] = jnp.zeros_like(token)

    thru = [pltpu.HBM(a.shape, a.dtype) for a in srcs] + [pltpu.HBM(a.shape, a.dtype) for a in land_shapes]
    outs = pl.pallas_call(
        body, name=name,
        out_shape=(pltpu.SemaphoreType.DMA((n_copies,)), pltpu.SemaphoreType.DMA((n_copies,)), *thru, _sds((8, LANE))),
        in_specs=[HBM_SPEC] * (ns + nl) + [ANY] * len(deps),
        out_specs=(SEM_SPEC, SEM_SPEC, *[HBM_SPEC] * (ns + nl), pl.BlockSpec(memory_space=pltpu.VMEM)),
        input_output_aliases={i: 2 + i for i in range(ns + nl)},
        compiler_params=pltpu.CompilerParams(has_side_effects=pltpu.SideEffectType.DATAFLOW_SIDE_EFFECTING),
    )(*[_hbm(a) for a in srcs], *[_hbm(lax.empty(a.shape, a.dtype)) for a in land_shapes], *deps)
    return (outs[0], outs[1]), list(outs[2:2 + ns]), list(outs[2 + ns:2 + ns + nl]), outs[-1]


def _exchange_wait(name, plan, sems, srcs, lands, after):
    ns, nl = len(srcs), len(lands)

    def body(*refs):
        _run_plan(plan, refs[:ns], refs[ns:ns + nl], refs[ns + nl], refs[ns + nl + 1], False, True)

    outs = pl.pallas_call(
        body, name=name,
        out_shape=[pltpu.HBM(a.shape, a.dtype) for a in list(srcs) + list(lands)],
        in_specs=[HBM_SPEC] * (ns + nl) + [SEM_SPEC, SEM_SPEC] + [ANY] * len(after), out_specs=[HBM_SPEC] * (ns + nl),
        input_output_aliases={i: i for i in range(ns + nl)},
        compiler_params=pltpu.CompilerParams(has_side_effects=pltpu.SideEffectType.DATAFLOW_SIDE_EFFECTING),
    )(*srcs, *lands, sems[0], sems[1], *after)
    return list(outs[:ns]), list(outs[ns:])


def _xchg_begin(name, plan, n_copies, srcs, land_shapes, deps=()):
    sems, srcs_t, lands_t, token = _exchange_start(name + "_start", plan, n_copies, srcs, land_shapes, list(deps))
    return dict(name=name, plan=plan, sems=sems, srcs=srcs_t, lands=lands_t, token=token)


def _xchg_end(h, after):
    return _exchange_wait(h["name"] + "_wait", h["plan"], h["sems"], h["srcs"], h["lands"], after)


def _other_chips():
    x, y, c = _place()
    return [(1 - x, y), (x, 1 - y), (1 - x, 1 - y)]


def _gather_plan(srcs, lands):
    x, y, c = _place()
    me = 2 * x + y
    return [(srcs[a], lands[a].at[me], lands[a].at[2 * cx + cy], (cx, cy, c))
            for (cx, cy) in _other_chips() for a in range(len(srcs))]


def _gather_begin(shards, tag, deps=()):
    shapes = [_sds((N_CHIPS,) + a.shape, a.dtype) for a in shards]
    return _xchg_begin(f"gather_{tag}", _gather_plan, 3 * len(shards), shards, shapes, deps)


def _gather_end(h, after):
    shards, lands = _xchg_end(h, after)
    me = 2 * lax.axis_index("x") + lax.axis_index("y")
    return [lax.dynamic_update_index_in_dim(g, s, me, 0) for g, s in zip(lands, shards)]


def _gather_half_plan(srcs, lands):
    x, y, c = _place()
    me = 2 * x + y
    out = []
    for (cx, cy) in _other_chips():
        out.append((srcs[0].at[c], lands[0].at[me, c], lands[0].at[2 * cx + cy, c], (cx, cy, c)))
        out += [(srcs[a], lands[a].at[me], lands[a].at[2 * cx + cy], (cx, cy, c)) for a in range(1, len(srcs))]
    return out


def _forward_plan(bufs, _):
    x, y, c = _place()
    return [(bufs[0].at[2 * cx + cy, c], bufs[0].at[2 * cx + cy, c], bufs[0].at[2 * cx + cy, 1 - c], (x, y, 1 - c))
            for (cx, cy) in _other_chips()]


def _swap_plan(srcs, lands):
    x, y, c = _place()
    return [(srcs[a].at[:, 1 - c], lands[a], lands[a], (x, y, 1 - c)) for a in range(len(srcs))]


def _chips_plan(srcs, lands):
    x, y, c = _place()
    me = 2 * x + y
    return [(srcs[a].at[2 * cx + cy], lands[a].at[me], lands[a].at[2 * cx + cy], (cx, cy, c))
            for (cx, cy) in _other_chips() for a in range(len(srcs))]


def _share_plan(srcs, lands):
    x, y, c = _place()
    return [(srcs[a], lands[a].at[c], lands[a].at[1 - c], (x, y, 1 - c)) for a in range(len(srcs))]


def _allreduce_small(slab, dep=None):
    r = slab.shape[0]

    def body(s_ref, o_ref, gath, send_sems, recv_sems):
        x, y, c = _place()
        me = 4 * x + 2 * y + c
        gath[me] = s_ref[...]
        cps = []
        for rel in range(1, N_DEV):
            px = 1 - x if rel & 4 else x
            py = 1 - y if rel & 2 else y
            pc = 1 - c if rel & 1 else c
            cp = pltpu.make_async_remote_copy(src_ref=s_ref, dst_ref=gath.at[me], send_sem=send_sems.at[rel - 1],
                                              recv_sem=recv_sems.at[rel - 1], device_id=(px, py, pc), device_id_type=MESH_T)
            cp.start()
            cps.append(cp)
        for cp in cps:
            cp.wait()
        acc = gath[0]
        for d in range(1, N_DEV):
            acc = acc + gath[d]
        o_ref[...] = acc

    vm = pl.BlockSpec(memory_space=pltpu.VMEM)
    return _call_after(
        dep, body, (slab,), name="allreduce_small", in_specs=[vm], out_specs=vm, out_shape=_sds((r, LANE)),
        scratch_shapes=[pltpu.VMEM((N_DEV, r, LANE), F32), pltpu.SemaphoreType.DMA((N_DEV - 1,)),
                        pltpu.SemaphoreType.DMA((N_DEV - 1,))],
    )


def _add_mine(g4s, recvs, half):
    n = len(g4s)

    def body(h_ref, *refs):
        for g_ref, r_ref, o_ref in zip(refs[:n], refs[n:2 * n], refs[2 * n:]):
            o_ref[0] = (g_ref[0, 0] + r_ref[0]).astype(o_ref.dtype)

    dims = [g.shape[2:] for g in g4s]
    return pl.pallas_call(
        body, name="add_mine",
        grid_spec=pltpu.PrefetchScalarGridSpec(
            num_scalar_prefetch=1, grid=(N_CHIPS,),
            in_specs=[pl.BlockSpec((1, 1) + d, lambda j, h: (j, h[0], 0, 0)) for d in dims]
            + [pl.BlockSpec((1,) + d, lambda j, h: (j, 0, 0)) for d in dims],
            out_specs=[pl.BlockSpec((1,) + d, lambda j, h: (j, 0, 0)) for d in dims]),
        out_shape=[_sds((N_CHIPS,) + d, PAYLOAD) for d in dims],
        compiler_params=_params(("parallel",)),
    )(half, *g4s, *recvs)


def _add_chips(es, ps, me):
    n = len(es)

    def body(m_ref, *refs):
        for e_ref, p_ref, o_ref in zip(refs[:n], refs[n:2 * n], refs[2 * n:]):
            own = p_ref[0].astype(F32)
            acc = None
            for s in range(N_CHIPS):
                t = jnp.where(m_ref[0] == s, own, e_ref[s].astype(F32))
                acc = t if acc is None else acc + t
            o_ref[...] = acc

    dims = [e.shape[1:] for e in es]
    return pl.pallas_call(
        body, name="add_chips",
        grid_spec=pltpu.PrefetchScalarGridSpec(
            num_scalar_prefetch=1, grid=(1,),
            in_specs=[pl.BlockSpec((N_CHIPS,) + d, lambda i, m: (0, 0, 0)) for d in dims]
            + [pl.BlockSpec((1,) + d, lambda i, m: (m[0], 0, 0)) for d in dims],
            out_specs=[pl.BlockSpec(d, lambda i, m: (0, 0)) for d in dims]),
        out_shape=[_sds(d) for d in dims],
        compiler_params=_params(("arbitrary",)),
    )(me, *es, *ps)


def _rs_begin(gs, tag, deps=()):
    g4 = [g.reshape(N_CHIPS, 2, g.shape[0] // (2 * N_CHIPS), g.shape[1]) for g in gs]
    h = _xchg_begin(f"rs_swap_{tag}", _swap_plan, len(gs), g4, [_sds((N_CHIPS,) + g.shape[2:]) for g in g4], deps)
    return dict(h=h, tag=tag, shapes=[g.shape for g in gs])


def _rs_add_mine(st, after):
    g4, recv = _xchg_end(st["h"], after)
    half = jnp.reshape(lax.axis_index("c"), (1,)).astype(jnp.int32)
    ps = _add_mine(g4, recv, half)
    st["h"] = _xchg_begin(f"rs_chips_{st['tag']}", _chips_plan, 3 * len(ps), ps, [_sds(p.shape, p.dtype) for p in ps])
    return st


def _rs_add_chips(st, after):
    ps, es = _xchg_end(st["h"], after)
    me = jnp.reshape(2 * lax.axis_index("x") + lax.axis_index("y"), (1,)).astype(jnp.int32)
    fs = _add_chips(es, ps, me)
    st["h"] = _xchg_begin(f"rs_share_{st['tag']}", _share_plan, len(fs), fs, [_sds((2,) + f.shape) for f in fs])
    return st


def _rs_end(st, after):
    fs, ss = _xchg_end(st["h"], after)
    c = lax.axis_index("c")
    return [lax.dynamic_update_index_in_dim(s, f, c, 0).reshape(shp[0] // N_CHIPS, shp[1])
            for s, f, shp in zip(ss, fs, st["shapes"])]


WEIGHTS = ["norm_g", "w_in", "conv_a_w", "ssd_conv_w", "ssd_conv_b", "ssd_dt_bias", "ssd_a_log", "ssd_d", "ssd_norm_g",
           "mla_q_norm_g", "w_qb", "mla_kv_norm_g", "w_kvb", "w_out", "final_norm_g"]
BIG = ["w_in", "w_qb", "w_kvb", "w_out"]
SLAB_ROWS = 128


def _to_slab(parts, rows):
    flat = jnp.concatenate([p.reshape(-1) for p in parts])
    return jnp.pad(flat, (0, rows * LANE - flat.shape[0])).reshape(rows, LANE)


def _from_slab(slab, shapes):
    flat = slab.reshape(-1)
    out, off = [], 0
    for shp in shapes:
        n = int(np.prod(shp))
        out.append(flat[off:off + n].reshape(shp))
        off += n
    return out


def kernel(x, positions, norm_g, w_in, conv_a_w, ssd_conv_w, ssd_conv_b, ssd_dt_bias, ssd_a_log, ssd_d, ssd_norm_g, mla_q_norm_g, w_qb, mla_kv_norm_g, w_kvb, w_out, final_norm_g, loss_target, m_norm_g, m_w_in, m_conv_a_w, m_ssd_conv_w, m_ssd_conv_b, m_ssd_dt_bias, m_ssd_a_log, m_ssd_d, m_ssd_norm_g, m_mla_q_norm_g, m_w_qb, m_mla_kv_norm_g, m_w_kvb, m_w_out, m_final_norm_g, v_norm_g, v_w_in, v_conv_a_w, v_ssd_conv_w, v_ssd_conv_b, v_ssd_dt_bias, v_ssd_a_log, v_ssd_d, v_ssd_norm_g, v_mla_q_norm_g, v_w_qb, v_mla_kv_norm_g, v_w_kvb, v_w_out, v_final_norm_g):
    w = dict(norm_g=norm_g, w_in=w_in, conv_a_w=conv_a_w, ssd_conv_w=ssd_conv_w, ssd_conv_b=ssd_conv_b,
             ssd_dt_bias=ssd_dt_bias, ssd_a_log=ssd_a_log, ssd_d=ssd_d, ssd_norm_g=ssd_norm_g, mla_q_norm_g=mla_q_norm_g,
             w_qb=w_qb, mla_kv_norm_g=mla_kv_norm_g, w_kvb=w_kvb, w_out=w_out, final_norm_g=final_norm_g)
    mom = dict(norm_g=m_norm_g, w_in=m_w_in, conv_a_w=m_conv_a_w, ssd_conv_w=m_ssd_conv_w, ssd_conv_b=m_ssd_conv_b,
               ssd_dt_bias=m_ssd_dt_bias, ssd_a_log=m_ssd_a_log, ssd_d=m_ssd_d, ssd_norm_g=m_ssd_norm_g,
               mla_q_norm_g=m_mla_q_norm_g, w_qb=m_w_qb, mla_kv_norm_g=m_mla_kv_norm_g, w_kvb=m_w_kvb, w_out=m_w_out,
               final_norm_g=m_final_norm_g)
    var = dict(norm_g=v_norm_g, w_in=v_w_in, conv_a_w=v_conv_a_w, ssd_conv_w=v_ssd_conv_w, ssd_conv_b=v_ssd_conv_b,
               ssd_dt_bias=v_ssd_dt_bias, ssd_a_log=v_ssd_a_log, ssd_d=v_ssd_d, ssd_norm_g=v_ssd_norm_g,
               mla_q_norm_g=v_mla_q_norm_g, w_qb=v_w_qb, mla_kv_norm_g=v_mla_kv_norm_g, w_kvb=v_w_kvb, w_out=v_w_out,
               final_norm_g=v_final_norm_g)
    chip = 2 * lax.axis_index("x") + lax.axis_index("y")

    def early_shard(l, zero):
        pack = jnp.pad(conv_a_w[l], ((0, 5), (0, 192))) + jnp.pad(ssd_conv_w[l], ((3, 1), (0, 32)))
        return [(_perm_cols(w_in[l]) + zero).astype(MXU), pack + zero]

    def late_shard(l, zero):
        return [(w_out[l] + zero).astype(MXU), (w_qb[l].T + zero).astype(MXU), (w_kvb[l].T + zero).astype(MXU)]

    def early_weights(l, gathered):
        g_in, g_conv = gathered
        return dict(
            norm_g=norm_g[l][None], w_in=g_in.reshape(D_MODEL, NCOL),
            conv_a_w=jnp.concatenate([g_conv[j, 0:3, 0:64] for j in range(N_CHIPS)], axis=1),
            ssd_conv_w=jnp.concatenate([g_conv[j, 3:7, 0:224] for j in range(N_CHIPS)], axis=1),
            ssd_conv_b=ssd_conv_b[l][None], sc=_ssd_scalars(ssd_dt_bias[l], ssd_a_log[l], ssd_d[l]),
            g_ssd=ssd_norm_g[l][None], gq=mla_q_norm_g[l][None], gkv=mla_kv_norm_g[l][None])

    def late_weights(gathered):
        g_out, g_qb, g_kvb = gathered
        return dict(wq=_wq_layout(g_qb.reshape(MLA_HEADS * 96, Q_LORA)), wkv=_wkv_layout(g_kvb.reshape(MLA_HEADS * LANE, KV_LORA)),
                    w_out=g_out.reshape(D_MODEL, D_MODEL))

    def late_grads(dw_out, dwq, dwkv):
        wq = jnp.pad(_wq_unlayout(dwq).reshape(N_CHIPS, 144, Q_LORA), ((0, 0), (0, 16), (0, 0)))
        return [dw_out, wq.reshape(N_CHIPS * 160, Q_LORA), _wkv_unlayout(dwkv)]

    def large_grads(g):
        return [g["w_in"]] + late_grads(g["w_out"], g["wq"], g["wkv"])

    w_in0, pack0 = early_shard(0, 0.0)
    half = w_in0.shape[0] // 2
    gather_a0 = _xchg_begin("gather_a0", _gather_half_plan, 6, [w_in0.reshape(2, half, NCOL), pack0],
                            [_sds((N_CHIPS, 2, half, NCOL), MXU), _sds((N_CHIPS,) + pack0.shape)])
    zero = gather_a0["token"][0, 0]
    cos, sin = _rope_tables(positions[0] + zero.astype(jnp.int32))
    late0, shards1 = late_shard(0, zero), early_shard(1, zero) + late_shard(1, zero)
    mine0, (g_in0, g_conv0) = _xchg_end(gather_a0, [cos, sin] + late0 + shards1)
    forward_a0 = _xchg_begin("forward_a0", _forward_plan, 3, [g_in0], [])
    gather_b0 = _gather_begin(late0, "b0", [forward_a0["token"]])
    gather_1 = _gather_begin(shards1, "1", [gather_b0["token"]])
    (g_in0,), _ = _xchg_end(forward_a0, [gather_1["token"]])
    lw0 = early_weights(0, [lax.dynamic_update_index_in_dim(g, s_, chip, 0) for g, s_ in zip((g_in0, g_conv0), mine0)])
    x1, sv0, lw0 = _layer_fwd(x[0], lw0, cos, sin, gather_1["token"],
                              lambda ya, y_ssd: late_weights(_gather_end(gather_b0, [ya, y_ssd])))
    g1 = _gather_end(gather_1, [x1])
    x2, sv1, lw1 = _layer_fwd(x1, {**early_weights(1, g1[:2]), **late_weights(g1[2:])}, cos, sin)
    dx, dgf, loss = _loss_head(x2, final_norm_g[None], loss_target[0])

    dx, lg1, _, _ = _layer_bwd(dx, lw1, sv1, cos, sin)
    grad_x, lg0, red1, rs0_late = _layer_bwd(dx, lw0, sv0, cos, sin, _rs_begin(large_grads(lg1), 1),
                                             lambda *g: _rs_begin(late_grads(*g), "0l"))
    rs0 = _rs_begin([lg0["w_in"]], 0, [rs0_late["h"]["token"]])
    lg = [lg0, lg1]
    grad = {}

    small_names = ["norm_g", "conv_a_w", "ssd_conv_w", "ssd_conv_b", "sc", "g_ssd", "gq", "gkv"]
    parts = [loss[0, 0:1], dgf]
    for nm in small_names:
        parts += [lg[l][nm][:3, DT_LANE:DT_LANE + SSD_HEADS] if nm == "sc" else lg[l][nm] for l in range(DEPTH)]
    shapes = [(1,), (D_MODEL,)] + [(DEPTH,) + shp for shp in ((D_MODEL,), (3, D_CONV_A), (4, N_XBC), (N_XBC,), (3, SSD_HEADS),
                                                              (D_SSD,), (Q_LORA,), (KV_LORA,))]
    red_slab = _allreduce_small(_to_slab(parts, SLAB_ROWS), rs0["h"]["token"])
    rs0 = _rs_add_mine(rs0, [red_slab])
    red = _from_slab(red_slab + rs0["h"]["token"][0, 0], shapes)
    loss_out = red[0][0]
    grad["final_norm_g"] = red[1]
    grad["norm_g"], conv_a_full, sconv_full, grad["ssd_conv_b"], sc_grads = red[2:7]
    grad["ssd_norm_g"], grad["mla_q_norm_g"], grad["mla_kv_norm_g"] = red[7:10]
    grad["conv_a_w"] = lax.dynamic_slice_in_dim(conv_a_full, chip * 64, 64, axis=2)
    grad["ssd_conv_w"] = lax.dynamic_slice_in_dim(sconv_full, chip * 224, 224, axis=2)
    grad["ssd_dt_bias"], grad["ssd_a_log"], grad["ssd_d"] = sc_grads[:, 0], sc_grads[:, 1], sc_grads[:, 2]

    delta, new_m, new_v = {}, {}, {}
    small = [nm for nm in WEIGHTS if nm not in BIG]
    row2 = lambda a: a[None] if a.ndim == 1 else a
    small_out = _adamw(*[[row2(a[nm]) for nm in small] for a in (w, grad, mom, var)], whole=True)
    for nm, (dv, mv, vv) in zip(small, small_out):
        delta[nm], new_m[nm], new_v[nm] = [a.reshape(w[nm].shape) for a in (dv, mv, vv)]

    r_out, r_qb, r_kvb = [jnp.stack([a, b]) for a, b in zip(_rs_end(rs0_late, [red_slab]), red1[1:])]
    grad.update(w_out=r_out, w_qb=jnp.swapaxes(r_qb[:, :144], 1, 2), w_kvb=jnp.swapaxes(r_kvb, 1, 2))
    late = [nm for nm in BIG if nm != "w_in"]
    late_out = _adamw(*[[a[nm] for nm in late] for a in (w, grad, mom, var)], whole=False)
    for nm, (dv, mv, vv) in zip(late, late_out):
        delta[nm], new_m[nm], new_v[nm] = dv, mv, vv
    g_in1 = _unperm_cols(red1[0])
    shadow_work = [a for row in small_out + late_out for a in row] + [grad[nm] for nm in small] + [g_in1]
    r_in0, = _rs_end(_rs_add_chips(rs0, shadow_work), [])
    to_cols, from_cols = (lambda a: jnp.transpose(a, (2, 0, 1))), (lambda a: jnp.transpose(a, (1, 2, 0)))
    grad["w_in"], delta["w_in"], new_m["w_in"], new_v["w_in"] = [from_cols(a) for a in _adamw_cols(
        to_cols(w["w_in"]), [_unperm_cols(r_in0), g_in1], to_cols(mom["w_in"]), to_cols(var["w_in"]))]

    return (loss_out, grad_x[None], *[grad[nm] for nm in WEIGHTS], *[delta[nm] for nm in WEIGHTS],
            *[new_m[nm] for nm in WEIGHTS], *[new_v[nm] for nm in WEIGHTS])
```

```python
import functools
import math

import numpy as np
import jax
import jax.numpy as jnp
from jax import lax
from jax.experimental import pallas as pl
from jax.experimental.pallas import tpu as pltpu

F32 = jnp.float32
MXU = jnp.bfloat16

D_MODEL = 1024
DEPTH = 2
D_CONV_A = 256
D_SSD = 384
SSD_HEADS = 6
SSD_BC = 256
SSD_CHUNK = 128
SSD_CHUNKS_PER_STEP = 4
SSD_NORM_EPS = 1e-5
MLA_HEADS = 6
Q_LORA = 256
KV_LORA = 128
QK_NOPE = 64
QK_ROPE = 32
V_DIM = 64
D_MLA = 384
ROPE_BASE = 10000.0
NORM_EPS = 1e-6
IN_COLS = 3110
LANE = 128

O_AH, O_AB, O_AC, O_AZ = 0, 256, 512, 768
O_XBC = 1024
O_SZ = 1920
O_CQA = 2304
O_CKV = 2560
O_CZ = 2688
O_TAIL = 3072
NCOL = 3200
N_XBC = D_SSD + 2 * SSD_BC
DT_LANE = 32
ROPE_LANE = 64

ADAM_LR, ADAM_B1, ADAM_B2, ADAM_EPS, ADAM_WD, ADAM_STEP = 0.001, 0.9, 0.999, 1e-08, 0.01, 10

VMEM_LIMIT = 56 * 1024 * 1024
MESH_T = pl.DeviceIdType.MESH


def _dot(a, b):
    return jnp.dot(a.astype(MXU), b.astype(MXU), preferred_element_type=F32)


def _dot_nt(a, b):
    return lax.dot_general(a.astype(MXU), b.astype(MXU), (((1,), (1,)), ((), ())), preferred_element_type=F32)


def _dot_tn(a, b):
    return lax.dot_general(a.astype(MXU), b.astype(MXU), (((0,), (0,)), ((), ())), preferred_element_type=F32)


def _dot_hi(a, b):
    return jnp.dot(a, b, precision=lax.Precision.HIGHEST, preferred_element_type=F32)


def _dot_hi_tn(a, b):
    return lax.dot_general(a, b, (((0,), (0,)), ((), ())), precision=lax.Precision.HIGHEST, preferred_element_type=F32)


def _sigmoid(z):
    return 1.0 / (1.0 + jnp.exp(-z))


def _silu(z):
    return z * _sigmoid(z)


def _dsilu(z):
    s = _sigmoid(z)
    return s * (1.0 + z * (1.0 - s))


def _softplus(z):
    e = jnp.exp(-jnp.abs(z))
    return jnp.maximum(z, 0.0) + jnp.where(e < 1e-3, e * (1.0 - 0.5 * e), jnp.log(1.0 + e))


def _iota(shape, dim):
    return lax.broadcasted_iota(jnp.int32, shape, dim)


def _shift_down(u, k):
    if k == 0:
        return u
    return jnp.where(_iota(u.shape, 0) >= k, pltpu.roll(u, k, 0), 0.0)


def _shift_up(u, k):
    if k == 0:
        return u
    n = u.shape[0]
    return jnp.where(_iota(u.shape, 0) < n - k, pltpu.roll(u, n - k, 0), 0.0)


def _rope_swap(t):
    lane = _iota(t.shape, 1)
    lo = (lane >= ROPE_LANE) & (lane < ROPE_LANE + 16)
    hi = (lane >= ROPE_LANE + 16) & (lane < ROPE_LANE + 32)
    return jnp.where(lo, pltpu.roll(t, LANE - 16, 1), jnp.where(hi, pltpu.roll(t, 16, 1), 0.0))


def _params(sem=None):
    return pltpu.CompilerParams(dimension_semantics=sem, vmem_limit_bytes=VMEM_LIMIT)


def _full(shape):
    nd = len(shape)
    return pl.BlockSpec(shape, lambda *_: (0,) * nd)


def _sds(shape, dtype=F32):
    return jax.ShapeDtypeStruct(shape, dtype)


def _tile(s):
    return min(512, s)


def _row(ts, w):
    return pl.BlockSpec((ts, w), lambda i: (i, 0))


def _gate_cols(ts, off):
    return pl.BlockSpec((ts, D_SSD), lambda i, _o=off // D_SSD: (i, _o))


def _col(s, off):
    return pl.BlockSpec((s, LANE), lambda j, _o=off // LANE: (0, _o + j))


def _call_after(dep, body, args, *, in_specs, **kw):
    if dep is None:
        return pl.pallas_call(body, in_specs=in_specs, **kw)(*args)
    n = len(args)

    def body_dep(*refs):
        body(*refs[:n], *refs[n + 1:])

    return pl.pallas_call(body_dep, in_specs=list(in_specs) + [pl.BlockSpec(memory_space=pl.ANY)], **kw)(*args, dep)


def _rms(c, g):
    r = lax.rsqrt(jnp.mean(c * c, axis=-1, keepdims=True) + NORM_EPS)
    return c * r * g, r


def _rms_bwd(dn, c, r, g):
    ch = c * r
    dch = dn * g
    dc = r * (dch - ch * jnp.mean(dch * ch, axis=-1, keepdims=True))
    return dc, jnp.sum(dn * ch, axis=0, keepdims=True)


def _inproj_fwd(x, g, w, dep=None):
    s = x.shape[0]
    ts = _tile(s)

    def body(x_ref, g_ref, w_ref, proj_ref, h_ref, r_ref):
        hn, r = _rms(x_ref[...], g_ref[...])
        h = hn.astype(MXU)
        h_ref[...] = h
        r_ref[...] = r
        proj_ref[...] = jnp.dot(h, w_ref[...], preferred_element_type=F32)

    return _call_after(
        dep, body, (x, g, w), name="inproj_fwd", grid=(s // ts,),
        in_specs=[_row(ts, D_MODEL), _full((1, D_MODEL)), _full((D_MODEL, NCOL))],
        out_specs=[_row(ts, NCOL), _row(ts, D_MODEL), _row(ts, 1)],
        out_shape=[_sds((s, NCOL)), _sds((s, D_MODEL), MXU), _sds((s, 1))],
        compiler_params=_params(("parallel",)),
    )


def _conva_fwd(proj, w):
    s = proj.shape[0]

    def body(h_ref, b_ref, c_ref, z_ref, w_ref, y_ref):
        u = c_ref[...] * h_ref[...]
        wv = w_ref[...]
        cv = wv[2:3, :] * u + wv[1:2, :] * _shift_down(u, 1) + wv[0:1, :] * _shift_down(u, 2)
        y_ref[...] = b_ref[...] * cv * _silu(z_ref[...])

    return pl.pallas_call(
        body, name="conva_fwd", grid=(D_CONV_A // LANE,),
        in_specs=[_col(s, O_AH), _col(s, O_AB), _col(s, O_AC), _col(s, O_AZ), pl.BlockSpec((3, LANE), lambda j: (0, j))],
        out_specs=pl.BlockSpec((s, LANE), lambda j: (0, j)),
        out_shape=_sds((s, D_CONV_A)),
        compiler_params=_params(("parallel",)),
    )(proj, proj, proj, proj, w)


def _sconv_pre(u, wv, bv):
    return (wv[3:4, :] * u + wv[2:3, :] * _shift_down(u, 1) + wv[1:2, :] * _shift_down(u, 2)
            + wv[0:1, :] * _shift_down(u, 3) + bv)


def _sconv_fwd(proj, w, b):
    s = proj.shape[0]

    def body(u_ref, w_ref, b_ref, o_ref):
        o_ref[...] = _silu(_sconv_pre(u_ref[...], w_ref[...], b_ref[...]))

    return pl.pallas_call(
        body, name="sconv_fwd", grid=(N_XBC // LANE,),
        in_specs=[_col(s, O_XBC), pl.BlockSpec((4, LANE), lambda j: (0, j)), pl.BlockSpec((1, LANE), lambda j: (0, j))],
        out_specs=pl.BlockSpec((s, LANE), lambda j: (0, j)),
        out_shape=_sds((s, N_XBC)),
        compiler_params=_params(("parallel",)),
    )(proj, w, b)


def _ssd_chunk_common(tail, sc):
    l = SSD_CHUNK
    lane = _iota((l, LANE), 1)
    row = _iota((l, LANE), 0)
    tri = (row >= lane).astype(F32)
    a_row = -jnp.exp(sc[1:2, :])
    pre = tail + sc[0:1, :]
    dt = _softplus(pre)
    a_cs = _dot_hi(tri, dt * a_row)
    return lane, row, tri, a_row, pre, dt, a_cs, a_cs.T


def _pick_col(m, lane, k):
    return jnp.sum(jnp.where(lane == k, m, 0.0), axis=1, keepdims=True)


def _pick_row(m, row, k):
    return jnp.sum(jnp.where(row == k, m, 0.0), axis=0, keepdims=True)


def _ssd_fwd(xbc, proj, sc):
    s = xbc.shape[0]
    nc = s // SSD_CHUNK
    l = SSD_CHUNK
    cps = SSD_CHUNKS_PER_STEP

    def body(xbc_ref, tail_ref, sc_ref, y_ref, st_ref, state):
        @pl.when(pl.program_id(0) == 0)
        def _():
            state[...] = jnp.zeros_like(state)

        sc_v = sc_ref[...]
        lane1 = _iota((1, LANE), 1)
        rowp = _iota((LANE, 1), 0)
        d_row = sc_v[2:3, :]
        states = [state[j] for j in range(3)]
        for u in range(cps):
            r = slice(u * l, (u + 1) * l)
            lane, row, _, _, _, dt, a_cs, a_t = _ssd_chunk_common(tail_ref[r, :], sc_v)
            for j in range(3):
                st_ref[u, j] = states[j]
            for j in range(3):
                xpair = xbc_ref[r, LANE * j:LANE * (j + 1)]
                sp = states[j]
                ypair = jnp.zeros((l, LANE), F32)
                new_s = jnp.zeros((LANE, LANE), F32)
                decay = jnp.zeros((LANE, 1), F32)
                for half in range(2):
                    h = 2 * j + half
                    g = h // 3
                    hm = (lane < 64) if half == 0 else (lane >= 64)
                    hrow = (rowp < 64) if half == 0 else (rowp >= 64)
                    ac = _pick_col(a_cs, lane, DT_LANE + h)
                    ar = _pick_row(a_t, row, DT_LANE + h)
                    dtc = _pick_col(dt, lane, DT_LANE + h)
                    alast = jnp.sum(jnp.where(lane1 == l - 1, ar, 0.0), axis=1, keepdims=True)
                    dh = jnp.sum(jnp.where(lane1 == DT_LANE + h, d_row, 0.0), axis=1, keepdims=True)
                    xm = jnp.where(hm, xpair, 0.0)
                    xd = xm * dtc
                    bm = xbc_ref[r, D_SSD + LANE * g:D_SSD + LANE * (g + 1)]
                    cm = xbc_ref[r, D_SSD + SSD_BC + LANE * g:D_SSD + SSD_BC + LANE * (g + 1)]
                    lm = jnp.where(row >= lane, jnp.exp(jnp.minimum(ac - ar, 0.0)), 0.0)
                    y_diag = _dot(_dot_nt(cm, bm) * lm, xd)
                    y_off = jnp.where(hm, _dot_nt(cm, sp), 0.0) * jnp.exp(ac)
                    ypair = ypair + y_diag + y_off + xm * dh
                    new_s = new_s + _dot_tn(xd * jnp.exp(alast - ac), bm)
                    decay = jnp.where(hrow, jnp.exp(alast), decay)
                states[j] = sp * decay + new_s
                y_ref[r, LANE * j:LANE * (j + 1)] = ypair
        for j in range(3):
            state[j] = states[j]

    return pl.pallas_call(
        body, name="ssd_fwd", grid=(nc // cps,),
        in_specs=[pl.BlockSpec((cps * l, N_XBC), lambda c: (c, 0)),
                  pl.BlockSpec((cps * l, LANE), lambda c: (c, O_TAIL // LANE)), _full((8, LANE))],
        out_specs=[pl.BlockSpec((cps * l, D_SSD), lambda c: (c, 0)), pl.BlockSpec((cps, 3, LANE, LANE), lambda c: (c, 0, 0, 0))],
        out_shape=[_sds((s, D_SSD)), _sds((nc, 3, LANE, LANE))],
        scratch_shapes=[pltpu.VMEM((3, LANE, LANE), F32)],
        compiler_params=_params(("arbitrary",)),
    )(xbc, proj, sc)


def _mla_prep_fwd(proj, gq, gkv, wq, wkv, cos, sin):
    s = proj.shape[0]
    ts = _tile(s)
    nh = MLA_HEADS

    def body(cqa_ref, ckv_ref, tail_ref, gq_ref, gkv_ref, wq_ref, wkv_ref, cos_ref, sin_ref,
             q_ref, k_ref, v_ref, qn_ref, kvn_ref, rq_ref, rkv_ref):
        qn, rq = _rms(cqa_ref[...], gq_ref[...])
        kvn, rkv = _rms(ckv_ref[...], gkv_ref[...])
        qn = qn.astype(MXU)
        kvn = kvn.astype(MXU)
        qn_ref[...] = qn
        kvn_ref[...] = kvn
        rq_ref[...] = rq
        rkv_ref[...] = rkv
        q = _dot_nt(qn, wq_ref[...])
        kv = _dot_nt(kvn, wkv_ref[...])
        cosv = cos_ref[...]
        sinv = sin_ref[...]
        lane = _iota((ts, LANE), 1)
        rope_lanes = (lane >= ROPE_LANE) & (lane < ROPE_LANE + QK_ROPE)
        kr = jnp.where(rope_lanes, pltpu.roll(tail_ref[...], ROPE_LANE, 1), 0.0)
        kr = kr * cosv + _rope_swap(kr) * sinv
        for h in range(nh):
            qh = q[:, LANE * h:LANE * (h + 1)]
            q_ref[h] = ((qh * cosv + _rope_swap(qh) * sinv) * ATT_SCALE).astype(MXU)
            k_ref[h] = (kv[:, LANE * h:LANE * (h + 1)] + kr).astype(MXU)
            v_ref[h] = kv[:, LANE * (nh + h):LANE * (nh + h + 1)].astype(MXU)

    head = pl.BlockSpec((nh, ts, LANE), lambda i: (0, i, 0))
    return pl.pallas_call(
        body, name="mla_prep_fwd", grid=(s // ts,),
        in_specs=[pl.BlockSpec((ts, Q_LORA), lambda i: (i, O_CQA // Q_LORA)),
                  pl.BlockSpec((ts, KV_LORA), lambda i: (i, O_CKV // KV_LORA)),
                  pl.BlockSpec((ts, LANE), lambda i: (i, O_TAIL // LANE)),
                  _full((1, Q_LORA)), _full((1, KV_LORA)), _full((nh * LANE, Q_LORA)), _full((2 * nh * LANE, KV_LORA)),
                  _row(ts, LANE), _row(ts, LANE)],
        out_specs=[head, head, head, _row(ts, Q_LORA), _row(ts, KV_LORA), _row(ts, 1), _row(ts, 1)],
        out_shape=[_sds((nh, s, LANE), MXU)] * 3 + [_sds((s, Q_LORA), MXU), _sds((s, KV_LORA), MXU), _sds((s, 1)), _sds((s, 1))],
        compiler_params=_params(("parallel",)),
    )(proj, proj, proj, gq, gkv, wq, wkv, cos, sin)


ATT_SCALE = (QK_NOPE + QK_ROPE) ** -0.5
NEG = -1e30


def _att_tile(s, most):
    return min(most, s // 2)


ATT_FWD_TILE = 1024
ATT_BWD_TILE = 512


def _attn_fwd(q, k, v):
    nh, s, _ = q.shape
    tq = _att_tile(s, ATT_FWD_TILE)
    nq = s // tq

    def body(q_ref, k_ref, v_ref, o_ref, lse_ref):
        i = pl.program_id(1)
        rowi = _iota((tq, tq), 0)
        coli = _iota((tq, tq), 1)
        zero = (jnp.full((tq, 1), NEG, F32), jnp.zeros((tq, 1), F32), jnp.zeros((tq, LANE), F32))
        state = [zero, zero]
        done = [zero, zero]
        for t in range(nq + 1):
            first = t <= i
            qblk = jnp.where(first, i, nq - 1 - i)
            kblk = jnp.where(first, t, t - i - 1)
            qoff = pl.multiple_of(qblk * tq, tq)
            koff = pl.multiple_of(kblk * tq, tq)
            keep = coli <= rowi + jnp.where(kblk == qblk, 0, tq)
            restart = t == i + 1
            for hh in range(2):
                m, lsum, acc = state[hh]
                if t > 0:
                    done[hh] = tuple(jnp.where(restart, a, b) for a, b in zip(state[hh], done[hh]))
                    m = jnp.where(restart, NEG, m)
                    lsum = jnp.where(restart, 0.0, lsum)
                    acc = jnp.where(restart, 0.0, acc)
                sc = _dot_nt(q_ref[hh, pl.ds(qoff, tq), :], k_ref[hh, pl.ds(koff, tq), :])
                sc = jnp.where(keep, sc, NEG)
                m_new = jnp.maximum(m, jnp.max(sc, axis=1, keepdims=True))
                p = jnp.exp(sc - m_new)
                alpha = jnp.exp(m - m_new)
                lsum = alpha * lsum + jnp.sum(p, axis=1, keepdims=True)
                acc = alpha * acc + _dot(p, v_ref[hh, pl.ds(koff, tq), :])
                state[hh] = (m_new, lsum, acc)
        for blk, res in ((i, done), (nq - 1 - i, state)):
            off = pl.multiple_of(blk * tq, tq)
            out = None
            for hh in range(2):
                m, lsum, acc = res[hh]
                o = acc * (1.0 / lsum)
                lse_ref[hh, pl.ds(off, tq), :] = m + jnp.log(lsum)
                out = o if hh == 0 else out + pltpu.roll(o, V_DIM, 1)
            o_ref[pl.ds(off, tq), :] = out

    pair = pl.BlockSpec((2, s, LANE), lambda j, i: (j, 0, 0))
    return pl.pallas_call(
        body, name="attn_fwd", grid=(nh // 2, nq // 2),
        in_specs=[pair, pair, pair],
        out_specs=[pl.BlockSpec((s, LANE), lambda j, i: (0, j)), pl.BlockSpec((2, s, 1), lambda j, i: (j, 0, 0))],
        out_shape=[_sds((s, D_MLA)), _sds((nh, s, 1))],
        compiler_params=_params(("parallel", "arbitrary")),
    )(q, k, v)


def _ssd_gate(y_ssd, s_z, g):
    yz = y_ssd * _silu(s_z)
    g0 = _iota(yz.shape, 1) < D_SSD // 2
    sq = yz * yz
    ms0 = jnp.sum(jnp.where(g0, sq, 0.0), axis=1, keepdims=True) / (D_SSD // 2)
    ms1 = jnp.sum(jnp.where(g0, 0.0, sq), axis=1, keepdims=True) / (D_SSD // 2)
    r = jnp.where(g0, lax.rsqrt(ms0 + SSD_NORM_EPS), lax.rsqrt(ms1 + SSD_NORM_EPS))
    nrm = yz * r
    return nrm * g, nrm, r, g0


def _outproj_fwd(x, proj, ya, y_ssd, o, g_ssd, w):
    s = x.shape[0]
    ts = _tile(s)

    def body(x_ref, sz_ref, cz_ref, ya_ref, ys_ref, o_ref, g_ref, w_ref, xo_ref, y_ref):
        yb = _ssd_gate(ys_ref[...], sz_ref[...], g_ref[...])[0]
        yc = o_ref[...] * _silu(cz_ref[...])
        y = jnp.concatenate([ya_ref[...], yb, yc], axis=1).astype(MXU)
        y_ref[...] = y
        xo_ref[...] = x_ref[...] + jnp.dot(y, w_ref[...], preferred_element_type=F32)

    return pl.pallas_call(
        body, name="outproj_fwd", grid=(s // ts,),
        in_specs=[_row(ts, D_MODEL), _gate_cols(ts, O_SZ), _gate_cols(ts, O_CZ), _row(ts, D_CONV_A), _row(ts, D_SSD),
                  _row(ts, D_MLA), _full((1, D_SSD)), _full((D_MODEL, D_MODEL))],
        out_specs=[_row(ts, D_MODEL), _row(ts, D_MODEL)],
        out_shape=[_sds((s, D_MODEL)), _sds((s, D_MODEL), MXU)],
        compiler_params=_params(("parallel",)),
    )(x, proj, proj, ya, y_ssd, o, g_ssd, w)


def _loss_head(x, g, tgt):
    s = x.shape[0]
    ts = _tile(s)

    def body(x_ref, g_ref, t_ref, dx_ref, dg_ref, loss_ref):
        @pl.when(pl.program_id(0) == 0)
        def _():
            dg_ref[...] = jnp.zeros_like(dg_ref)
            loss_ref[...] = jnp.zeros_like(loss_ref)

        xv = x_ref[...]
        gv = g_ref[...]
        yn, r = _rms(xv, gv)
        e = yn - t_ref[...]
        loss_ref[...] += jnp.sum(jnp.sum(e * e, axis=1, keepdims=True), axis=0, keepdims=True) * (0.5 / D_MODEL)
        dx, dg = _rms_bwd(e * (1.0 / D_MODEL), xv, r, gv)
        dx_ref[...] = dx
        dg_ref[...] += dg

    return pl.pallas_call(
        body, name="loss_head", grid=(s // ts,),
        in_specs=[_row(ts, D_MODEL), _full((1, D_MODEL)), _row(ts, D_MODEL)],
        out_specs=[_row(ts, D_MODEL), _full((1, D_MODEL)), _full((1, LANE))],
        out_shape=[_sds((s, D_MODEL)), _sds((1, D_MODEL)), _sds((1, LANE))],
        compiler_params=_params(("arbitrary",)),
    )(x, g, tgt)


def _outproj_bwd(dout, y, w, proj, y_ssd, o, g_ssd, dep=None):
    s = dout.shape[0]
    ts = _tile(s)

    def body(dout_ref, y_ref, w_ref, sz_ref, cz_ref, ys_ref, o_ref, g_ref,
             dya_ref, dys_ref, dsz_ref, dattn_ref, dcz_ref, dg_ref, dw_ref):
        @pl.when(pl.program_id(0) == 0)
        def _():
            dw_ref[...] = jnp.zeros_like(dw_ref)
            dg_ref[...] = jnp.zeros_like(dg_ref)

        dout_b = dout_ref[...].astype(MXU)
        dw_ref[...] += _dot_tn(y_ref[...], dout_b)
        dy = _dot_nt(dout_b, w_ref[...])
        dya_ref[...] = dy[:, :D_CONV_A]
        dyb = dy[:, D_CONV_A:D_CONV_A + D_SSD]
        sz = sz_ref[...]
        ys = ys_ref[...]
        gv = g_ref[...]
        _, nrm, r, g0 = _ssd_gate(ys, sz, gv)
        dg_ref[...] += jnp.sum(dyb * nrm, axis=0, keepdims=True)
        dn = dyb * gv
        t = dn * nrm
        mean = jnp.where(g0, jnp.sum(jnp.where(g0, t, 0.0), axis=1, keepdims=True),
                         jnp.sum(jnp.where(g0, 0.0, t), axis=1, keepdims=True)) / (D_SSD // 2)
        dyz = r * (dn - nrm * mean)
        dys_ref[...] = dyz * _silu(sz)
        dsz_ref[...] = (dyz * ys * _dsilu(sz)).astype(MXU)
        dyc = dy[:, D_CONV_A + D_SSD:]
        cz = cz_ref[...]
        dattn_ref[...] = dyc * _silu(cz)
        dcz_ref[...] = (dyc * o_ref[...] * _dsilu(cz)).astype(MXU)

    return _call_after(
        dep, body, (dout, y, w, proj, proj, y_ssd, o, g_ssd), name="outproj_bwd", grid=(s // ts,),
        in_specs=[_row(ts, D_MODEL), _row(ts, D_MODEL), _full((D_MODEL, D_MODEL)), _gate_cols(ts, O_SZ), _gate_cols(ts, O_CZ),
                  _row(ts, D_SSD), _row(ts, D_MLA), _full((1, D_SSD))],
        out_specs=[_row(ts, D_CONV_A), _row(ts, D_SSD), _row(ts, D_SSD), _row(ts, D_MLA), _row(ts, D_MLA),
                   _full((1, D_SSD)), _full((D_MODEL, D_MODEL))],
        out_shape=[_sds((s, D_CONV_A)), _sds((s, D_SSD)), _sds((s, D_SSD), MXU), _sds((s, D_MLA)), _sds((s, D_MLA), MXU),
                   _sds((1, D_SSD)), _sds((D_MODEL, D_MODEL))],
        compiler_params=_params(("arbitrary",)),
    )


def _attn_bwd(q, k, v, o, d_o, lse, dep=None):
    nh, s, _ = q.shape
    tq = _att_tile(s, ATT_BWD_TILE)
    nq = s // tq

    def body(q_ref, k_ref, v_ref, o_ref, do_ref, lse_ref, dq_ref, dk_ref, dv_ref, dop, delta):
        i = pl.program_id(1)

        @pl.when(i == 0)
        def _():
            lane = _iota((s, LANE), 1)
            for hh in range(2):
                dov = do_ref[...]
                ov = o_ref[...]
                if hh == 1:
                    dov = pltpu.roll(dov, V_DIM, 1)
                    ov = pltpu.roll(ov, V_DIM, 1)
                dov = jnp.where(lane < V_DIM, dov, 0.0)
                dop[hh] = dov.astype(MXU)
                delta[hh] = jnp.sum(dov * ov, axis=1, keepdims=True)
                dq_ref[hh] = jnp.zeros((s, LANE), F32)

        rowi = _iota((tq, tq), 0)
        coli = _iota((tq, tq), 1)
        z = jnp.zeros((tq, LANE), F32)
        state = [(z, z), (z, z)]
        done = [(z, z), (z, z)]
        for t in range(nq + 1):
            first = t <= nq - 1 - i
            kblk = jnp.where(first, i, nq - 1 - i)
            qblk = jnp.where(first, i + t, t - 1)
            qoff = pl.multiple_of(qblk * tq, tq)
            koff = pl.multiple_of(kblk * tq, tq)
            keep = coli <= rowi + jnp.where(kblk == qblk, 0, tq)
            restart = t == nq - i
            for hh in range(2):
                dk, dv = state[hh]
                if t > 0:
                    done[hh] = tuple(jnp.where(restart, a, b) for a, b in zip(state[hh], done[hh]))
                    dk = jnp.where(restart, 0.0, dk)
                    dv = jnp.where(restart, 0.0, dv)
                kb = k_ref[hh, pl.ds(koff, tq), :]
                qb = q_ref[hh, pl.ds(qoff, tq), :]
                dob = dop[hh, pl.ds(qoff, tq), :]
                sc = jnp.where(keep, _dot_nt(qb, kb), NEG)
                p = jnp.exp(sc - lse_ref[hh, pl.ds(qoff, tq), :])
                dp = _dot_nt(dob, v_ref[hh, pl.ds(koff, tq), :])
                ds = p * (dp - delta[hh, pl.ds(qoff, tq), :])
                dq_ref[hh, pl.ds(qoff, tq), :] += _dot(ds, kb)
                state[hh] = (dk + _dot_tn(ds, qb), dv + _dot_tn(p, dob))
        for blk, res in ((i, done), (nq - 1 - i, state)):
            off = pl.multiple_of(blk * tq, tq)
            for hh in range(2):
                dk_ref[hh, pl.ds(off, tq), :] = res[hh][0]
                dv_ref[hh, pl.ds(off, tq), :] = res[hh][1]

    pair = pl.BlockSpec((2, s, LANE), lambda j, i: (j, 0, 0))
    return _call_after(
        dep, body, (q, k, v, o, d_o, lse), name="attn_bwd", grid=(nh // 2, nq // 2),
        in_specs=[pair, pair, pair, pl.BlockSpec((s, LANE), lambda j, i: (0, j)), pl.BlockSpec((s, LANE), lambda j, i: (0, j)),
                  pl.BlockSpec((2, s, 1), lambda j, i: (j, 0, 0))],
        out_specs=[pair, pair, pair],
        out_shape=[_sds((nh, s, LANE))] * 3,
        scratch_shapes=[pltpu.VMEM((2, s, LANE), MXU), pltpu.VMEM((2, s, 1), F32)],
        compiler_params=_params(("parallel", "arbitrary")),
    )


def _ssd_bwd(xbc, proj, sc, states, dy, dep=None):
    s = xbc.shape[0]
    nc = s // SSD_CHUNK
    l = SSD_CHUNK
    cps = SSD_CHUNKS_PER_STEP

    def body(xbc_ref, tail_ref, sc_ref, st_ref, dy_ref, dxbc_ref, dtail_ref, dsc_ref, dstate):
        @pl.when(pl.program_id(0) == 0)
        def _():
            dstate[...] = jnp.zeros_like(dstate)
            dsc_ref[...] = jnp.zeros_like(dsc_ref)

        sc_v = sc_ref[...]
        lane1 = _iota((1, LANE), 1)
        rowp = _iota((LANE, 1), 0)
        rowl = _iota((l, 1), 0)
        d_row = sc_v[2:3, :]
        dstates = [dstate[j] for j in range(3)]
        for u in reversed(range(cps)):
            dstates = chunk(u, xbc_ref, tail_ref, sc_v, st_ref, dy_ref, dxbc_ref, dtail_ref, dsc_ref, dstates,
                            lane1, rowp, rowl, d_row)
        for j in range(3):
            dstate[j] = dstates[j]

    def chunk(u, xbc_ref, tail_ref, sc_v, st_ref, dy_ref, dxbc_ref, dtail_ref, dsc_ref, dstates, lane1, rowp, rowl, d_row):
        r = slice(u * l, (u + 1) * l)
        dstates = list(dstates)
        lane, row, tri, a_row, pre, dt, a_cs, a_t = _ssd_chunk_common(tail_ref[r, :], sc_v)
        da_col = jnp.zeros((l, LANE), F32)
        da_row = jnp.zeros((LANE, l), F32)
        dt_x = jnp.zeros((l, LANE), F32)
        dd_row = jnp.zeros((1, LANE), F32)
        db = [jnp.zeros((l, LANE), F32), jnp.zeros((l, LANE), F32)]
        dc = [jnp.zeros((l, LANE), F32), jnp.zeros((l, LANE), F32)]
        for j in range(3):
            xpair = xbc_ref[r, LANE * j:LANE * (j + 1)]
            dypair = dy_ref[r, LANE * j:LANE * (j + 1)]
            sp = st_ref[u, j]
            dsp = dstates[j]
            dxpair = jnp.zeros((l, LANE), F32)
            ds_new = jnp.zeros((LANE, LANE), F32)
            decay = jnp.zeros((LANE, 1), F32)
            for half in range(2):
                h = 2 * j + half
                g = h // 3
                hm = (lane < 64) if half == 0 else (lane >= 64)
                hrow = (rowp < 64) if half == 0 else (rowp >= 64)
                ac = _pick_col(a_cs, lane, DT_LANE + h)
                ar = _pick_row(a_t, row, DT_LANE + h)
                dtc = _pick_col(dt, lane, DT_LANE + h)
                alast = jnp.sum(jnp.where(lane1 == l - 1, ar, 0.0), axis=1, keepdims=True)
                dh = jnp.sum(jnp.where(lane1 == DT_LANE + h, d_row, 0.0), axis=1, keepdims=True)
                xm = jnp.where(hm, xpair, 0.0)
                xd = xm * dtc
                dym = jnp.where(hm, dypair, 0.0)
                bm = xbc_ref[r, D_SSD + LANE * g:D_SSD + LANE * (g + 1)]
                cm = xbc_ref[r, D_SSD + SSD_BC + LANE * g:D_SSD + SSD_BC + LANE * (g + 1)]
                lm = jnp.where(row >= lane, jnp.exp(jnp.minimum(ac - ar, 0.0)), 0.0)
                e_in = jnp.exp(ac)
                f_out = jnp.exp(alast - ac)
                e_last = jnp.exp(alast)
                m = _dot_nt(cm, bm) * lm
                y_off = jnp.where(hm, _dot_nt(cm, sp), 0.0) * e_in
                dm = _dot_nt(dym, xd)
                dxd = _dot_tn(m, dym)
                dg = dm * lm
                dye = dym * e_in
                dc[g] = dc[g] + _dot(dg, bm) + _dot(dye, sp)
                db[g] = db[g] + _dot_tn(dg, cm)
                qm = dm * m
                dac = jnp.sum(qm, axis=1, keepdims=True) + jnp.sum(dym * y_off, axis=1, keepdims=True)
                dar = -jnp.sum(qm, axis=0, keepdims=True)
                dxf = jnp.where(hm, _dot_nt(bm, dsp), 0.0)
                db[g] = db[g] + _dot(xd * f_out, dsp)
                dxd = dxd + dxf * f_out
                df = jnp.sum(dxf * xd, axis=1, keepdims=True) * f_out
                dac = dac - df
                s_last = jnp.sum(df, axis=0, keepdims=True)
                ss = jnp.sum(jnp.where(hrow, dsp * sp, 0.0), axis=1, keepdims=True)
                s_last = s_last + e_last * jnp.sum(ss, axis=0, keepdims=True)
                dac = dac + jnp.where(rowl == l - 1, s_last, 0.0)
                ds_new = ds_new + _dot_tn(dye, cm)
                decay = jnp.where(hrow, e_last, decay)
                dxpair = dxpair + dxd * dtc + dym * dh
                dt_x = dt_x + jnp.where(lane == DT_LANE + h, jnp.sum(dxd * xm, axis=1, keepdims=True), 0.0)
                dsum = jnp.sum(jnp.sum(dym * xm, axis=1, keepdims=True), axis=0, keepdims=True)
                dd_row = dd_row + jnp.where(lane1 == DT_LANE + h, dsum, 0.0)
                da_col = da_col + jnp.where(lane == DT_LANE + h, dac, 0.0)
                da_row = da_row + jnp.where(row == DT_LANE + h, dar, 0.0)
            dstates[j] = dsp * decay + ds_new
            dxbc_ref[r, LANE * j:LANE * (j + 1)] = dxpair
        for g in range(2):
            dxbc_ref[r, D_SSD + LANE * g:D_SSD + LANE * (g + 1)] = db[g]
            dxbc_ref[r, D_SSD + SSD_BC + LANE * g:D_SSD + SSD_BC + LANE * (g + 1)] = dc[g]
        dla = _dot_hi_tn(tri, da_col + da_row.T)
        ddt = dt_x + dla * a_row
        dpre = ddt * _sigmoid(pre)
        dtm = (lane >= DT_LANE) & (lane < DT_LANE + SSD_HEADS)
        dtail_ref[r, :] = jnp.where(dtm, dpre, 0.0).astype(MXU)
        dtm1 = (lane1 >= DT_LANE) & (lane1 < DT_LANE + SSD_HEADS)
        dsc_ref[0:1, :] += jnp.where(dtm1, jnp.sum(dpre, axis=0, keepdims=True), 0.0)
        dsc_ref[1:2, :] += jnp.where(dtm1, jnp.sum(dla * dt, axis=0, keepdims=True) * a_row, 0.0)
        dsc_ref[2:3, :] += dd_row
        return dstates

    rev = lambda c: nc // cps - 1 - c
    return _call_after(
        dep, body, (xbc, proj, sc, states, dy), name="ssd_bwd", grid=(nc // cps,),
        in_specs=[pl.BlockSpec((cps * l, N_XBC), lambda c: (rev(c), 0)),
                  pl.BlockSpec((cps * l, LANE), lambda c: (rev(c), O_TAIL // LANE)), _full((8, LANE)),
                  pl.BlockSpec((cps, 3, LANE, LANE), lambda c: (rev(c), 0, 0, 0)),
                  pl.BlockSpec((cps * l, D_SSD), lambda c: (rev(c), 0))],
        out_specs=[pl.BlockSpec((cps * l, N_XBC), lambda c: (rev(c), 0)), pl.BlockSpec((cps * l, LANE), lambda c: (rev(c), 0)),
                   _full((8, LANE))],
        out_shape=[_sds((s, N_XBC)), _sds((s, LANE), MXU), _sds((8, LANE))],
        scratch_shapes=[pltpu.VMEM((3, LANE, LANE), F32)],
        compiler_params=_params(("arbitrary",)),
    )


def _sconv_bwd(proj, w, b, dxbc, dep=None):
    s = proj.shape[0]

    def body(u_ref, w_ref, b_ref, d_ref, du_ref, dw_ref, db_ref):
        u = u_ref[...]
        wv = w_ref[...]
        dpre = d_ref[...] * _dsilu(_sconv_pre(u, wv, b_ref[...]))
        ahead = [_shift_up(dpre, j) for j in range(4)]
        du_ref[...] = (wv[3:4, :] * ahead[0] + wv[2:3, :] * ahead[1] + wv[1:2, :] * ahead[2]
                       + wv[0:1, :] * ahead[3]).astype(MXU)
        for k in range(4):
            dw_ref[k:k + 1, :] = jnp.sum(ahead[3 - k] * u, axis=0, keepdims=True)
        db_ref[...] = jnp.sum(dpre, axis=0, keepdims=True)

    blk = pl.BlockSpec((s, LANE), lambda j: (0, j))
    return _call_after(
        dep, body, (proj, w, b, dxbc), name="sconv_bwd", grid=(N_XBC // LANE,),
        in_specs=[_col(s, O_XBC), pl.BlockSpec((4, LANE), lambda j: (0, j)), pl.BlockSpec((1, LANE), lambda j: (0, j)), blk],
        out_specs=[blk, pl.BlockSpec((4, LANE), lambda j: (0, j)), pl.BlockSpec((1, LANE), lambda j: (0, j))],
        out_shape=[_sds((s, N_XBC), MXU), _sds((4, N_XBC)), _sds((1, N_XBC))],
        compiler_params=_params(("parallel",)),
    )


def _conva_bwd(proj, w, dya, dep=None):
    s = proj.shape[0]

    def body(h_ref, b_ref, c_ref, z_ref, w_ref, d_ref, da_ref, dw_ref):
        ah, ab, acv, az = h_ref[...], b_ref[...], c_ref[...], z_ref[...]
        wv = w_ref[...]
        u = acv * ah
        cv = wv[2:3, :] * u + wv[1:2, :] * _shift_down(u, 1) + wv[0:1, :] * _shift_down(u, 2)
        dy = d_ref[...]
        sz = _silu(az)
        da_ref[1] = (dy * cv * sz).astype(MXU)
        da_ref[3] = (dy * ab * cv * _dsilu(az)).astype(MXU)
        dcv = dy * ab * sz
        ahead = [_shift_up(dcv, j) for j in range(3)]
        du = wv[2:3, :] * ahead[0] + wv[1:2, :] * ahead[1] + wv[0:1, :] * ahead[2]
        da_ref[0] = (du * acv).astype(MXU)
        da_ref[2] = (du * ah).astype(MXU)
        for k in range(3):
            dw_ref[k:k + 1, :] = jnp.sum(ahead[2 - k] * u, axis=0, keepdims=True)

    return _call_after(
        dep, body, (proj, proj, proj, proj, w, dya), name="conva_bwd", grid=(D_CONV_A // LANE,),
        in_specs=[_col(s, O_AH), _col(s, O_AB), _col(s, O_AC), _col(s, O_AZ), pl.BlockSpec((3, LANE), lambda j: (0, j)),
                  pl.BlockSpec((s, LANE), lambda j: (0, j))],
        out_specs=[pl.BlockSpec((4, s, LANE), lambda j: (0, 0, j)), pl.BlockSpec((3, LANE), lambda j: (0, j))],
        out_shape=[_sds((4, s, D_CONV_A), MXU), _sds((3, D_CONV_A))],
        compiler_params=_params(("parallel",)),
    )


def _mla_prep_bwd(dq, dk, dv, proj, qn, kvn, rq, rkv, gq, gkv, wq, wkv, cos, sin):
    s = proj.shape[0]
    ts = _tile(s)
    nh = MLA_HEADS

    def body(dq_ref, dk_ref, dv_ref, cqa_ref, ckv_ref, qn_ref, kvn_ref, rq_ref, rkv_ref, gq_ref, gkv_ref,
             wq_ref, wkv_ref, cos_ref, sin_ref, dcqa_ref, dckv_ref, dtail_ref, dwq_ref, dwkv_ref, dgq_ref, dgkv_ref):
        @pl.when(pl.program_id(0) == 0)
        def _():
            dwq_ref[...] = jnp.zeros_like(dwq_ref)
            dwkv_ref[...] = jnp.zeros_like(dwkv_ref)
            dgq_ref[...] = jnp.zeros_like(dgq_ref)
            dgkv_ref[...] = jnp.zeros_like(dgkv_ref)

        cosv = cos_ref[...]
        sinv = sin_ref[...]
        lane = _iota((ts, LANE), 1)
        rope_lanes = (lane >= ROPE_LANE) & (lane < ROPE_LANE + QK_ROPE)

        def unrope(gr):
            return gr * cosv + _rope_swap(gr * sinv)

        dqs, dks, dvs = [], [], []
        dkr = jnp.zeros((ts, LANE), F32)
        for h in range(nh):
            dqs.append(unrope(dq_ref[h] * ATT_SCALE).astype(MXU))
            dkh = dk_ref[h]
            dks.append(jnp.where(lane < QK_NOPE, dkh, 0.0).astype(MXU))
            dkr = dkr + jnp.where(rope_lanes, dkh, 0.0)
            dvs.append(dv_ref[h].astype(MXU))
        dtail_ref[...] = pltpu.roll(jnp.where(rope_lanes, unrope(dkr), 0.0), ROPE_LANE, 1).astype(MXU)
        dq_all = jnp.concatenate(dqs, axis=1)
        dkv_all = jnp.concatenate(dks + dvs, axis=1)
        dwq_ref[...] += _dot_tn(dq_all, qn_ref[...])
        dwkv_ref[...] += _dot_tn(dkv_all, kvn_ref[...])
        dcqa, dgq = _rms_bwd(_dot(dq_all, wq_ref[...]), cqa_ref[...], rq_ref[...], gq_ref[...])
        dckv, dgkv = _rms_bwd(_dot(dkv_all, wkv_ref[...]), ckv_ref[...], rkv_ref[...], gkv_ref[...])
        dcqa_ref[...] = dcqa.astype(MXU)
        dckv_ref[...] = dckv.astype(MXU)
        dgq_ref[...] += dgq
        dgkv_ref[...] += dgkv

    head = pl.BlockSpec((nh, ts, LANE), lambda i: (0, i, 0))
    return pl.pallas_call(
        body, name="mla_prep_bwd", grid=(s // ts,),
        in_specs=[head, head, head,
                  pl.BlockSpec((ts, Q_LORA), lambda i: (i, O_CQA // Q_LORA)),
                  pl.BlockSpec((ts, KV_LORA), lambda i: (i, O_CKV // KV_LORA)),
                  _row(ts, Q_LORA), _row(ts, KV_LORA), _row(ts, 1), _row(ts, 1),
                  _full((1, Q_LORA)), _full((1, KV_LORA)), _full((nh * LANE, Q_LORA)), _full((2 * nh * LANE, KV_LORA)),
                  _row(ts, LANE), _row(ts, LANE)],
        out_specs=[_row(ts, Q_LORA), _row(ts, KV_LORA), _row(ts, LANE), _full((nh * LANE, Q_LORA)),
                   _full((2 * nh * LANE, KV_LORA)), _full((1, Q_LORA)), _full((1, KV_LORA))],
        out_shape=[_sds((s, Q_LORA), MXU), _sds((s, KV_LORA), MXU), _sds((s, LANE), MXU), _sds((nh * LANE, Q_LORA)),
                   _sds((2 * nh * LANE, KV_LORA)), _sds((1, Q_LORA)), _sds((1, KV_LORA))],
        compiler_params=_params(("arbitrary",)),
    )(dq, dk, dv, proj, proj, qn, kvn, rq, rkv, gq, gkv, wq, wkv, cos, sin)


def _inproj_bwd(da4, dsz, dxbc_in, dcqa, dckv, dcz, dtail_a, dtail_b, w, x, rstd, g, dout, dep=None):
    s = x.shape[0]
    ts = _tile(s)

    def body(da_ref, dsz_ref, dxbc_ref, dcqa_ref, dckv_ref, dcz_ref, dta_ref, dtb_ref, w_ref, x_ref, r_ref, g_ref, dout_ref,
             dproj_ref, dx_ref, dg_ref):
        @pl.when(pl.program_id(0) == 0)
        def _():
            dg_ref[...] = jnp.zeros_like(dg_ref)

        dproj = jnp.concatenate(
            [da_ref[0], da_ref[1], da_ref[2], da_ref[3], dxbc_ref[...], dsz_ref[...], dcqa_ref[...], dckv_ref[...],
             dcz_ref[...], dta_ref[...] + dtb_ref[...]], axis=1)
        dproj_ref[...] = dproj
        dh = _dot_nt(dproj, w_ref[...])
        dx, dg = _rms_bwd(dh, x_ref[...], r_ref[...], g_ref[...])
        dx_ref[...] = dout_ref[...] + dx
        dg_ref[...] += dg

    return _call_after(
        dep, body, (da4, dsz, dxbc_in, dcqa, dckv, dcz, dtail_a, dtail_b, w, x, rstd, g, dout), name="inproj_bwd", grid=(s // ts,),
        in_specs=[pl.BlockSpec((4, ts, D_CONV_A), lambda i: (0, i, 0)), _row(ts, D_SSD), _row(ts, N_XBC), _row(ts, Q_LORA),
                  _row(ts, KV_LORA), _row(ts, D_MLA), _row(ts, LANE), _row(ts, LANE), _full((D_MODEL, NCOL)),
                  _row(ts, D_MODEL), _row(ts, 1), _full((1, D_MODEL)), _row(ts, D_MODEL)],
        out_specs=[_row(ts, NCOL), _row(ts, D_MODEL), _full((1, D_MODEL))],
        out_shape=[_sds((s, NCOL), MXU), _sds((s, D_MODEL)), _sds((1, D_MODEL))],
        compiler_params=_params(("arbitrary",)),
    )


DWIN_BLOCK = 640


def _dwin(h, dproj, dep=None):
    s = h.shape[0]

    def body(h_ref, d_ref, o_ref):
        o_ref[...] = _dot_tn(h_ref[...], d_ref[...])

    return _call_after(
        dep, body, (h, dproj), name="dwin", grid=(NCOL // DWIN_BLOCK,),
        in_specs=[_full((s, D_MODEL)), pl.BlockSpec((s, DWIN_BLOCK), lambda j: (0, j))],
        out_specs=pl.BlockSpec((D_MODEL, DWIN_BLOCK), lambda j: (0, j)),
        out_shape=_sds((D_MODEL, NCOL)),
        compiler_params=_params(("parallel",)),
    )


def _adamw(ws, gs, ms, vs, whole):
    n = len(ws)
    bc1 = 1.0 - ADAM_B1 ** ADAM_STEP
    bc2 = 1.0 - ADAM_B2 ** ADAM_STEP

    def body(*refs):
        ins, outs = refs[:4 * n], refs[4 * n:]
        for a in range(n):
            w_ref, g_ref, m_ref, v_ref = ins[a], ins[n + a], ins[2 * n + a], ins[3 * n + a]
            gv = g_ref[...]
            mn = ADAM_B1 * m_ref[...] + (1.0 - ADAM_B1) * gv
            vn = ADAM_B2 * v_ref[...] + (1.0 - ADAM_B2) * (gv * gv)
            outs[n + a][...] = mn
            outs[2 * n + a][...] = vn
            outs[a][...] = -ADAM_LR * ((mn / bc1) / (jnp.sqrt(vn / bc2) + ADAM_EPS) + ADAM_WD * w_ref[...])

    if whole:
        grid, blks = (1,), [pl.BlockSpec(w.shape, lambda i, _n=w.ndim: (0,) * _n) for w in ws]
    else:
        grid = (ws[0].shape[0], 2)
        blks = [pl.BlockSpec((1, w.shape[1] // 2, w.shape[2]), lambda i, k: (i, k, 0)) for w in ws]
    out = pl.pallas_call(
        body, name="adamw", grid=grid,
        in_specs=blks * 4, out_specs=blks * 3, out_shape=[_sds(w.shape) for w in ws] * 3,
        compiler_params=_params(("parallel",) * len(grid)),
    )(*ws, *gs, *ms, *vs)
    return [(out[a], out[n + a], out[2 * n + a]) for a in range(n)]


def _adamw_cols(w_t, gs, m_t, v_t):
    cols, nl, rows = w_t.shape
    bc1 = 1.0 - ADAM_B1 ** ADAM_STEP
    bc2 = 1.0 - ADAM_B2 ** ADAM_STEP

    def body(w_ref, m_ref, v_ref, *rest):
        g_refs, (go_ref, d_ref, mo_ref, vo_ref), g_blk = rest[:nl], rest[nl:nl + 4], rest[-1]
        for l in range(nl):
            g_blk[:, l, :] = g_refs[l][...].T
        gv = g_blk[...]
        mn = ADAM_B1 * m_ref[...] + (1.0 - ADAM_B1) * gv
        vn = ADAM_B2 * v_ref[...] + (1.0 - ADAM_B2) * (gv * gv)
        go_ref[...] = gv
        mo_ref[...] = mn
        vo_ref[...] = vn
        d_ref[...] = -ADAM_LR * ((mn / bc1) / (jnp.sqrt(vn / bc2) + ADAM_EPS) + ADAM_WD * w_ref[...])

    blk = pl.BlockSpec((LANE, nl, rows), lambda j: (j, 0, 0))
    gblk = pl.BlockSpec((rows, LANE), lambda j: (0, j))
    return pl.pallas_call(
        body, name="adamw_cols", grid=(pl.cdiv(cols, LANE),),
        in_specs=[blk] * 3 + [gblk] * nl, out_specs=[blk] * 4, out_shape=[_sds(w_t.shape)] * 4,
        scratch_shapes=[pltpu.VMEM((LANE, nl, rows), F32)],
        compiler_params=_params(("parallel",)),
    )(w_t, m_t, v_t, *gs)


COL_MOVES = ((0, 0, 1024), (1024, O_SZ, 384), (1408, O_XBC, 896), (2304, O_TAIL + DT_LANE, 6), (2310, O_CQA, 256),
             (2566, O_CKV, 128), (2694, O_TAIL, 32), (2726, O_CZ, 384))


def _move_cols(w, moves, width):
    out = None
    for src, dst, n in moves:
        piece = jnp.pad(w[..., src:src + n], [(0, 0)] * (w.ndim - 1) + [(dst, width - dst - n)])
        out = piece if out is None else out + piece
    return out


def _perm_cols(w):
    return _move_cols(w, COL_MOVES, NCOL)


def _unperm_cols(g):
    return _move_cols(g, [(dst, src, n) for src, dst, n in COL_MOVES], IN_COLS)


def _wq_layout(wt):
    return jnp.pad(wt.reshape(MLA_HEADS, QK_NOPE + QK_ROPE, Q_LORA), ((0, 0), (0, 32), (0, 0))).reshape(MLA_HEADS * LANE, Q_LORA)


def _wq_unlayout(g):
    return g.reshape(MLA_HEADS, LANE, Q_LORA)[:, :QK_NOPE + QK_ROPE].reshape(MLA_HEADS * (QK_NOPE + QK_ROPE), Q_LORA)


def _wkv_layout(wt):
    t = wt.reshape(MLA_HEADS, 2, 64, KV_LORA).transpose(1, 0, 2, 3)
    return jnp.pad(t, ((0, 0), (0, 0), (0, 64), (0, 0))).reshape(2 * MLA_HEADS * LANE, KV_LORA)


def _wkv_unlayout(g):
    t = g.reshape(2, MLA_HEADS, LANE, KV_LORA)[:, :, :64]
    return t.transpose(1, 0, 2, 3).reshape(MLA_HEADS * LANE, KV_LORA)


def _rope_tables(positions):
    inv_freq = ROPE_BASE ** (-jnp.arange(0, QK_ROPE, 2, dtype=F32) / QK_ROPE)
    ang = positions.astype(F32)[:, None] * inv_freq
    cos, sin = jnp.cos(ang), jnp.sin(ang)
    s = positions.shape[0]
    one, zero = jnp.ones((s, ROPE_LANE), F32), jnp.zeros((s, ROPE_LANE), F32)
    cos_t = jnp.concatenate([one, cos, cos, one[:, :32]], axis=1)
    sin_t = jnp.concatenate([zero, -sin, sin, zero[:, :32]], axis=1)
    return cos_t, sin_t


def _ssd_scalars(dt_bias, a_log, d_skip):
    return jnp.pad(jnp.stack([dt_bias, a_log, d_skip]), ((0, 5), (DT_LANE, LANE - DT_LANE - SSD_HEADS)))


def _layer_fwd(x, lw, cos, sin, dep=None, late=None):
    proj, h, rstd = _inproj_fwd(x, lw["norm_g"], lw["w_in"], dep)
    ya = _conva_fwd(proj, lw["conv_a_w"])
    xbc = _sconv_fwd(proj, lw["ssd_conv_w"], lw["ssd_conv_b"])
    y_ssd, states = _ssd_fwd(xbc, proj, lw["sc"])
    if late is not None:
        lw = {**lw, **late(ya, y_ssd)}
    q, k, v, qn, kvn, rq, rkv = _mla_prep_fwd(proj, lw["gq"], lw["gkv"], lw["wq"], lw["wkv"], cos, sin)
    o, lse = _attn_fwd(q, k, v)
    x_out, y = _outproj_fwd(x, proj, ya, y_ssd, o, lw["g_ssd"], lw["w_out"])
    saved = dict(x=x, proj=proj, h=h, rstd=rstd, xbc=xbc, y_ssd=y_ssd, states=states, q=q, k=k, v=v, qn=qn, kvn=kvn,
                 rq=rq, rkv=rkv, o=o, lse=lse, y=y)
    return x_out, saved, lw


def _layer_bwd(dout, lw, sv, cos, sin, rs=None, begin_early=None):
    tok = lambda: None if rs is None else rs["h"]["token"]
    dya, dys, dsz, d_o, dcz, dg_ssd, dw_out = _outproj_bwd(dout, sv["y"], lw["w_out"], sv["proj"], sv["y_ssd"], sv["o"],
                                                            lw["g_ssd"], tok())
    if rs is not None:
        rs = _rs_add_mine(rs, [dya])
    dq, dk, dv = _attn_bwd(sv["q"], sv["k"], sv["v"], sv["o"], d_o, sv["lse"], tok())
    dxbc, dtail_s, dsc = _ssd_bwd(sv["xbc"], sv["proj"], lw["sc"], sv["states"], dys, tok())
    da4, dw_conva = _conva_bwd(sv["proj"], lw["conv_a_w"], dya, tok())
    if rs is not None:
        rs = _rs_add_chips(rs, [dq, dxbc, da4])
    du, dw_sconv, db_sconv = _sconv_bwd(sv["proj"], lw["ssd_conv_w"], lw["ssd_conv_b"], dxbc, tok())
    dcqa, dckv, dtail_m, dwq, dwkv, dgq, dgkv = _mla_prep_bwd(
        dq, dk, dv, sv["proj"], sv["qn"], sv["kvn"], sv["rq"], sv["rkv"], lw["gq"], lw["gkv"], lw["wq"], lw["wkv"], cos, sin)
    early = None if begin_early is None else begin_early(dw_out, dwq, dwkv)
    etok = lambda: None if early is None else early["h"]["token"]
    dproj, dx, dg = _inproj_bwd(da4, dsz, du, dcqa, dckv, dcz, dtail_s, dtail_m, lw["w_in"], sv["x"], sv["rstd"],
                                lw["norm_g"], dout, etok())
    reduced = None if rs is None else _rs_end(rs, [du, dcqa, dx])
    if early is not None:
        early = _rs_add_mine(early, [dx])
    dw_in = _dwin(sv["h"], dproj, etok())
    if early is not None:
        early = _rs_add_chips(early, [dw_in])
    grads = dict(norm_g=dg, w_in=dw_in, conv_a_w=dw_conva, ssd_conv_w=dw_sconv, ssd_conv_b=db_sconv, sc=dsc,
                 g_ssd=dg_ssd, gq=dgq, wq=dwq, gkv=dgkv, wkv=dwkv, w_out=dw_out)
    return dx, grads, reduced, early


ANY = pl.BlockSpec(memory_space=pl.ANY)
N_CHIPS = 4
N_DEV = 8


def _place():
    return lax.axis_index("x"), lax.axis_index("y"), lax.axis_index("c")


HBM_SPEC = pl.BlockSpec(memory_space=pltpu.HBM)
SEM_SPEC = pl.BlockSpec(memory_space=pltpu.SEMAPHORE)
PAYLOAD = jnp.bfloat16


def _hbm(a):
    return pltpu.with_memory_space_constraint(a, pltpu.HBM)


def _run_plan(plan, srcs, lands, send_sems, recv_sems, start, wait):
    copies = plan(srcs, lands)
    if start:
        for i, (src, dst, _, to) in enumerate(copies):
            pltpu.make_async_remote_copy(src_ref=src, dst_ref=dst, send_sem=send_sems.at[i], recv_sem=recv_sems.at[i],
                                         device_id=to, device_id_type=MESH_T).start()
    if wait:
        for i, (src, _, arrives, to) in enumerate(copies):
            cp = pltpu.make_async_remote_copy(src_ref=src, dst_ref=arrives, send_sem=send_sems.at[i],
                                              recv_sem=recv_sems.at[i], device_id=to, device_id_type=MESH_T)
            cp.wait_send()
            cp.wait_recv()


def _exchange_start(name, plan, n_copies, srcs, land_shapes, deps):
    ns, nl = len(srcs), len(land_shapes)
    n_in = ns + nl + len(deps)

    def body(*refs):
        send_sems, recv_sems = refs[n_in], refs[n_in + 1]
        token = refs[-1]
        _run_plan(plan, refs[:ns], refs[ns:ns + nl], send_sems, recv_sems, True, False)
        token[...] = jnp.zeros_like(token)

    thru = [pltpu.HBM(a.shape, a.dtype) for a in srcs] + [pltpu.HBM(a.shape, a.dtype) for a in land_shapes]
    outs = pl.pallas_call(
        body, name=name,
        out_shape=(pltpu.SemaphoreType.DMA((n_copies,)), pltpu.SemaphoreType.DMA((n_copies,)), *thru, _sds((8, LANE))),
        in_specs=[HBM_SPEC] * (ns + nl) + [ANY] * len(deps),
        out_specs=(SEM_SPEC, SEM_SPEC, *[HBM_SPEC] * (ns + nl), pl.BlockSpec(memory_space=pltpu.VMEM)),
        input_output_aliases={i: 2 + i for i in range(ns + nl)},
        compiler_params=pltpu.CompilerParams(has_side_effects=pltpu.SideEffectType.DATAFLOW_SIDE_EFFECTING),
    )(*[_hbm(a) for a in srcs], *[_hbm(lax.empty(a.shape, a.dtype)) for a in land_shapes], *deps)
    return (outs[0], outs[1]), list(outs[2:2 + ns]), list(outs[2 + ns:2 + ns + nl]), outs[-1]


def _exchange_wait(name, plan, sems, srcs, lands, after):
    ns, nl = len(srcs), len(lands)

    def body(*refs):
        _run_plan(plan, refs[:ns], refs[ns:ns + nl], refs[ns + nl], refs[ns + nl + 1], False, True)

    outs = pl.pallas_call(
        body, name=name,
        out_shape=[pltpu.HBM(a.shape, a.dtype) for a in list(srcs) + list(lands)],
        in_specs=[HBM_SPEC] * (ns + nl) + [SEM_SPEC, SEM_SPEC] + [ANY] * len(after), out_specs=[HBM_SPEC] * (ns + nl),
        input_output_aliases={i: i for i in range(ns + nl)},
        compiler_params=pltpu.CompilerParams(has_side_effects=pltpu.SideEffectType.DATAFLOW_SIDE_EFFECTING),
    )(*srcs, *lands, sems[0], sems[1], *after)
    return list(outs[:ns]), list(outs[ns:])


def _xchg_begin(name, plan, n_copies, srcs, land_shapes, deps=()):
    sems, srcs_t, lands_t, token = _exchange_start(name + "_start", plan, n_copies, srcs, land_shapes, list(deps))
    return dict(name=name, plan=plan, sems=sems, srcs=srcs_t, lands=lands_t, token=token)


def _xchg_end(h, after):
    return _exchange_wait(h["name"] + "_wait", h["plan"], h["sems"], h["srcs"], h["lands"], after)


def _other_chips():
    x, y, c = _place()
    return [(1 - x, y), (x, 1 - y), (1 - x, 1 - y)]


def _gather_plan(srcs, lands):
    x, y, c = _place()
    me = 2 * x + y
    return [(srcs[a], lands[a].at[me], lands[a].at[2 * cx + cy], (cx, cy, c))
            for (cx, cy) in _other_chips() for a in range(len(srcs))]


def _gather_begin(shards, tag, deps=()):
    shapes = [_sds((N_CHIPS,) + a.shape, a.dtype) for a in shards]
    return _xchg_begin(f"gather_{tag}", _gather_plan, 3 * len(shards), shards, shapes, deps)


def _gather_end(h, after):
    shards, lands = _xchg_end(h, after)
    me = 2 * lax.axis_index("x") + lax.axis_index("y")
    return [lax.dynamic_update_index_in_dim(g, s, me, 0) for g, s in zip(lands, shards)]


def _gather_half_plan(srcs, lands):
    x, y, c = _place()
    me = 2 * x + y
    out = []
    for (cx, cy) in _other_chips():
        out.append((srcs[0].at[c], lands[0].at[me, c], lands[0].at[2 * cx + cy, c], (cx, cy, c)))
        out += [(srcs[a], lands[a].at[me], lands[a].at[2 * cx + cy], (cx, cy, c)) for a in range(1, len(srcs))]
    return out


def _forward_plan(bufs, _):
    x, y, c = _place()
    return [(bufs[0].at[2 * cx + cy, c], bufs[0].at[2 * cx + cy, c], bufs[0].at[2 * cx + cy, 1 - c], (x, y, 1 - c))
            for (cx, cy) in _other_chips()]


def _swap_plan(srcs, lands):
    x, y, c = _place()
    return [(srcs[a].at[:, 1 - c], lands[a], lands[a], (x, y, 1 - c)) for a in range(len(srcs))]


def _chips_plan(srcs, lands):
    x, y, c = _place()
    me = 2 * x + y
    return [(srcs[a].at[2 * cx + cy], lands[a].at[me], lands[a].at[2 * cx + cy], (cx, cy, c))
            for (cx, cy) in _other_chips() for a in range(len(srcs))]


def _share_plan(srcs, lands):
    x, y, c = _place()
    return [(srcs[a], lands[a].at[c], lands[a].at[1 - c], (x, y, 1 - c)) for a in range(len(srcs))]


def _allreduce_small(slab, dep=None):
    r = slab.shape[0]

    def body(s_ref, o_ref, gath, send_sems, recv_sems):
        x, y, c = _place()
        me = 4 * x + 2 * y + c
        gath[me] = s_ref[...]
        cps = []
        for rel in range(1, N_DEV):
            px = 1 - x if rel & 4 else x
            py = 1 - y if rel & 2 else y
            pc = 1 - c if rel & 1 else c
            cp = pltpu.make_async_remote_copy(src_ref=s_ref, dst_ref=gath.at[me], send_sem=send_sems.at[rel - 1],
                                              recv_sem=recv_sems.at[rel - 1], device_id=(px, py, pc), device_id_type=MESH_T)
            cp.start()
            cps.append(cp)
        for cp in cps:
            cp.wait()
        acc = gath[0]
        for d in range(1, N_DEV):
            acc = acc + gath[d]
        o_ref[...] = acc

    vm = pl.BlockSpec(memory_space=pltpu.VMEM)
    return _call_after(
        dep, body, (slab,), name="allreduce_small", in_specs=[vm], out_specs=vm, out_shape=_sds((r, LANE)),
        scratch_shapes=[pltpu.VMEM((N_DEV, r, LANE), F32), pltpu.SemaphoreType.DMA((N_DEV - 1,)),
                        pltpu.SemaphoreType.DMA((N_DEV - 1,))],
    )


def _add_mine(g4s, recvs, half):
    n = len(g4s)

    def body(h_ref, *refs):
        for g_ref, r_ref, o_ref in zip(refs[:n], refs[n:2 * n], refs[2 * n:]):
            o_ref[0] = (g_ref[0, 0] + r_ref[0]).astype(o_ref.dtype)

    dims = [g.shape[2:] for g in g4s]
    return pl.pallas_call(
        body, name="add_mine",
        grid_spec=pltpu.PrefetchScalarGridSpec(
            num_scalar_prefetch=1, grid=(N_CHIPS,),
            in_specs=[pl.BlockSpec((1, 1) + d, lambda j, h: (j, h[0], 0, 0)) for d in dims]
            + [pl.BlockSpec((1,) + d, lambda j, h: (j, 0, 0)) for d in dims],
            out_specs=[pl.BlockSpec((1,) + d, lambda j, h: (j, 0, 0)) for d in dims]),
        out_shape=[_sds((N_CHIPS,) + d, PAYLOAD) for d in dims],
        compiler_params=_params(("parallel",)),
    )(half, *g4s, *recvs)


def _add_chips(es, ps, me):
    n = len(es)

    def body(m_ref, *refs):
        for e_ref, p_ref, o_ref in zip(refs[:n], refs[n:2 * n], refs[2 * n:]):
            own = p_ref[0].astype(F32)
            acc = None
            for s in range(N_CHIPS):
                t = jnp.where(m_ref[0] == s, own, e_ref[s].astype(F32))
                acc = t if acc is None else acc + t
            o_ref[...] = acc

    dims = [e.shape[1:] for e in es]
    return pl.pallas_call(
        body, name="add_chips",
        grid_spec=pltpu.PrefetchScalarGridSpec(
            num_scalar_prefetch=1, grid=(1,),
            in_specs=[pl.BlockSpec((N_CHIPS,) + d, lambda i, m: (0, 0, 0)) for d in dims]
            + [pl.BlockSpec((1,) + d, lambda i, m: (m[0], 0, 0)) for d in dims],
            out_specs=[pl.BlockSpec(d, lambda i, m: (0, 0)) for d in dims]),
        out_shape=[_sds(d) for d in dims],
        compiler_params=_params(("arbitrary",)),
    )(me, *es, *ps)


def _rs_begin(gs, tag, deps=()):
    g4 = [g.reshape(N_CHIPS, 2, g.shape[0] // (2 * N_CHIPS), g.shape[1]) for g in gs]
    h = _xchg_begin(f"rs_swap_{tag}", _swap_plan, len(gs), g4, [_sds((N_CHIPS,) + g.shape[2:]) for g in g4], deps)
    return dict(h=h, tag=tag, shapes=[g.shape for g in gs])


def _rs_add_mine(st, after):
    g4, recv = _xchg_end(st["h"], after)
    half = jnp.reshape(lax.axis_index("c"), (1,)).astype(jnp.int32)
    ps = _add_mine(g4, recv, half)
    st["h"] = _xchg_begin(f"rs_chips_{st['tag']}", _chips_plan, 3 * len(ps), ps, [_sds(p.shape, p.dtype) for p in ps])
    return st


def _rs_add_chips(st, after):
    ps, es = _xchg_end(st["h"], after)
    me = jnp.reshape(2 * lax.axis_index("x") + lax.axis_index("y"), (1,)).astype(jnp.int32)
    fs = _add_chips(es, ps, me)
    st["h"] = _xchg_begin(f"rs_share_{st['tag']}", _share_plan, len(fs), fs, [_sds((2,) + f.shape) for f in fs])
    return st


def _rs_end(st, after):
    fs, ss = _xchg_end(st["h"], after)
    c = lax.axis_index("c")
    return [lax.dynamic_update_index_in_dim(s, f, c, 0).reshape(shp[0] // N_CHIPS, shp[1])
            for s, f, shp in zip(ss, fs, st["shapes"])]


WEIGHTS = ["norm_g", "w_in", "conv_a_w", "ssd_conv_w", "ssd_conv_b", "ssd_dt_bias", "ssd_a_log", "ssd_d", "ssd_norm_g",
           "mla_q_norm_g", "w_qb", "mla_kv_norm_g", "w_kvb", "w_out", "final_norm_g"]
BIG = ["w_in", "w_qb", "w_kvb", "w_out"]
SLAB_ROWS = 128


def _to_slab(parts, rows):
    flat = jnp.concatenate([p.reshape(-1) for p in parts])
    return jnp.pad(flat, (0, rows * LANE - flat.shape[0])).reshape(rows, LANE)


def _from_slab(slab, shapes):
    flat = slab.reshape(-1)
    out, off = [], 0
    for shp in shapes:
        n = int(np.prod(shp))
        out.append(flat[off:off + n].reshape(shp))
        off += n
    return out


def kernel(x, positions, norm_g, w_in, conv_a_w, ssd_conv_w, ssd_conv_b, ssd_dt_bias, ssd_a_log, ssd_d, ssd_norm_g, mla_q_norm_g, w_qb, mla_kv_norm_g, w_kvb, w_out, final_norm_g, loss_target, m_norm_g, m_w_in, m_conv_a_w, m_ssd_conv_w, m_ssd_conv_b, m_ssd_dt_bias, m_ssd_a_log, m_ssd_d, m_ssd_norm_g, m_mla_q_norm_g, m_w_qb, m_mla_kv_norm_g, m_w_kvb, m_w_out, m_final_norm_g, v_norm_g, v_w_in, v_conv_a_w, v_ssd_conv_w, v_ssd_conv_b, v_ssd_dt_bias, v_ssd_a_log, v_ssd_d, v_ssd_norm_g, v_mla_q_norm_g, v_w_qb, v_mla_kv_norm_g, v_w_kvb, v_w_out, v_final_norm_g):
    w = dict(norm_g=norm_g, w_in=w_in, conv_a_w=conv_a_w, ssd_conv_w=ssd_conv_w, ssd_conv_b=ssd_conv_b,
             ssd_dt_bias=ssd_dt_bias, ssd_a_log=ssd_a_log, ssd_d=ssd_d, ssd_norm_g=ssd_norm_g, mla_q_norm_g=mla_q_norm_g,
             w_qb=w_qb, mla_kv_norm_g=mla_kv_norm_g, w_kvb=w_kvb, w_out=w_out, final_norm_g=final_norm_g)
    mom = dict(norm_g=m_norm_g, w_in=m_w_in, conv_a_w=m_conv_a_w, ssd_conv_w=m_ssd_conv_w, ssd_conv_b=m_ssd_conv_b,
               ssd_dt_bias=m_ssd_dt_bias, ssd_a_log=m_ssd_a_log, ssd_d=m_ssd_d, ssd_norm_g=m_ssd_norm_g,
               mla_q_norm_g=m_mla_q_norm_g, w_qb=m_w_qb, mla_kv_norm_g=m_mla_kv_norm_g, w_kvb=m_w_kvb, w_out=m_w_out,
               final_norm_g=m_final_norm_g)
    var = dict(norm_g=v_norm_g, w_in=v_w_in, conv_a_w=v_conv_a_w, ssd_conv_w=v_ssd_conv_w, ssd_conv_b=v_ssd_conv_b,
               ssd_dt_bias=v_ssd_dt_bias, ssd_a_log=v_ssd_a_log, ssd_d=v_ssd_d, ssd_norm_g=v_ssd_norm_g,
               mla_q_norm_g=v_mla_q_norm_g, w_qb=v_w_qb, mla_kv_norm_g=v_mla_kv_norm_g, w_kvb=v_w_kvb, w_out=v_w_out,
               final_norm_g=v_final_norm_g)
    chip = 2 * lax.axis_index("x") + lax.axis_index("y")

    def early_shard(l, zero):
        pack = jnp.pad(conv_a_w[l], ((0, 5), (0, 192))) + jnp.pad(ssd_conv_w[l], ((3, 1), (0, 32)))
        return [(_perm_cols(w_in[l]) + zero).astype(MXU), pack + zero]

    def late_shard(l, zero):
        return [(w_out[l] + zero).astype(MXU), (w_qb[l].T + zero).astype(MXU), (w_kvb[l].T + zero).astype(MXU)]

    def early_weights(l, gathered):
        g_in, g_conv = gathered
        return dict(
            norm_g=norm_g[l][None], w_in=g_in.reshape(D_MODEL, NCOL),
            conv_a_w=jnp.concatenate([g_conv[j, 0:3, 0:64] for j in range(N_CHIPS)], axis=1),
            ssd_conv_w=jnp.concatenate([g_conv[j, 3:7, 0:224] for j in range(N_CHIPS)], axis=1),
            ssd_conv_b=ssd_conv_b[l][None], sc=_ssd_scalars(ssd_dt_bias[l], ssd_a_log[l], ssd_d[l]),
            g_ssd=ssd_norm_g[l][None], gq=mla_q_norm_g[l][None], gkv=mla_kv_norm_g[l][None])

    def late_weights(gathered):
        g_out, g_qb, g_kvb = gathered
        return dict(wq=_wq_layout(g_qb.reshape(MLA_HEADS * 96, Q_LORA)), wkv=_wkv_layout(g_kvb.reshape(MLA_HEADS * LANE, KV_LORA)),
                    w_out=g_out.reshape(D_MODEL, D_MODEL))

    def late_grads(dw_out, dwq, dwkv):
        wq = jnp.pad(_wq_unlayout(dwq).reshape(N_CHIPS, 144, Q_LORA), ((0, 0), (0, 16), (0, 0)))
        return [dw_out, wq.reshape(N_CHIPS * 160, Q_LORA), _wkv_unlayout(dwkv)]

    def large_grads(g):
        return [g["w_in"]] + late_grads(g["w_out"], g["wq"], g["wkv"])

    w_in0, pack0 = early_shard(0, 0.0)
    half = w_in0.shape[0] // 2
    gather_a0 = _xchg_begin("gather_a0", _gather_half_plan, 6, [w_in0.reshape(2, half, NCOL), pack0],
                            [_sds((N_CHIPS, 2, half, NCOL), MXU), _sds((N_CHIPS,) + pack0.shape)])
    zero = gather_a0["token"][0, 0]
    cos, sin = _rope_tables(positions[0] + zero.astype(jnp.int32))
    late0, shards1 = late_shard(0, zero), early_shard(1, zero) + late_shard(1, zero)
    mine0, (g_in0, g_conv0) = _xchg_end(gather_a0, [cos, sin] + late0 + shards1)
    forward_a0 = _xchg_begin("forward_a0", _forward_plan, 3, [g_in0], [])
    gather_b0 = _gather_begin(late0, "b0", [forward_a0["token"]])
    gather_1 = _gather_begin(shards1, "1", [gather_b0["token"]])
    (g_in0,), _ = _xchg_end(forward_a0, [gather_1["token"]])
    lw0 = early_weights(0, [lax.dynamic_update_index_in_dim(g, s_, chip, 0) for g, s_ in zip((g_in0, g_conv0), mine0)])
    x1, sv0, lw0 = _layer_fwd(x[0], lw0, cos, sin, gather_1["token"],
                              lambda ya, y_ssd: late_weights(_gather_end(gather_b0, [ya, y_ssd])))
    g1 = _gather_end(gather_1, [x1])
    x2, sv1, lw1 = _layer_fwd(x1, {**early_weights(1, g1[:2]), **late_weights(g1[2:])}, cos, sin)
    dx, dgf, loss = _loss_head(x2, final_norm_g[None], loss_target[0])

    dx, lg1, _, _ = _layer_bwd(dx, lw1, sv1, cos, sin)
    grad_x, lg0, red1, rs0_late = _layer_bwd(dx, lw0, sv0, cos, sin, _rs_begin(large_grads(lg1), 1),
                                             lambda *g: _rs_begin(late_grads(*g), "0l"))
    rs0 = _rs_begin([lg0["w_in"]], 0, [rs0_late["h"]["token"]])
    lg = [lg0, lg1]
    grad = {}

    small_names = ["norm_g", "conv_a_w", "ssd_conv_w", "ssd_conv_b", "sc", "g_ssd", "gq", "gkv"]
    parts = [loss[0, 0:1], dgf]
    for nm in small_names:
        parts += [lg[l][nm][:3, DT_LANE:DT_LANE + SSD_HEADS] if nm == "sc" else lg[l][nm] for l in range(DEPTH)]
    shapes = [(1,), (D_MODEL,)] + [(DEPTH,) + shp for shp in ((D_MODEL,), (3, D_CONV_A), (4, N_XBC), (N_XBC,), (3, SSD_HEADS),
                                                              (D_SSD,), (Q_LORA,), (KV_LORA,))]
    red_slab = _allreduce_small(_to_slab(parts, SLAB_ROWS), rs0["h"]["token"])
    rs0 = _rs_add_mine(rs0, [red_slab])
    red = _from_slab(red_slab + rs0["h"]["token"][0, 0], shapes)
    loss_out = red[0][0]
    grad["final_norm_g"] = red[1]
    grad["norm_g"], conv_a_full, sconv_full, grad["ssd_conv_b"], sc_grads = red[2:7]
    grad["ssd_norm_g"], grad["mla_q_norm_g"], grad["mla_kv_norm_g"] = red[7:10]
    grad["conv_a_w"] = lax.dynamic_slice_in_dim(conv_a_full, chip * 64, 64, axis=2)
    grad["ssd_conv_w"] = lax.dynamic_slice_in_dim(sconv_full, chip * 224, 224, axis=2)
    grad["ssd_dt_bias"], grad["ssd_a_log"], grad["ssd_d"] = sc_grads[:, 0], sc_grads[:, 1], sc_grads[:, 2]

    delta, new_m, new_v = {}, {}, {}
    small = [nm for nm in WEIGHTS if nm not in BIG]
    row2 = lambda a: a[None] if a.ndim == 1 else a
    small_out = _adamw(*[[row2(a[nm]) for nm in small] for a in (w, grad, mom, var)], whole=True)
    for nm, (dv, mv, vv) in zip(small, small_out):
        delta[nm], new_m[nm], new_v[nm] = [a.reshape(w[nm].shape) for a in (dv, mv, vv)]

    r_out, r_qb, r_kvb = [jnp.stack([a, b]) for a, b in zip(_rs_end(rs0_late, [red_slab]), red1[1:])]
    late = [nm for nm in BIG if nm != "w_in"]
    view = {nm: (lambda a: a) if nm == "w_out" else (lambda a: jnp.swapaxes(a, 1, 2)) for nm in late}
    late_g = [dict(w_out=r_out, w_qb=r_qb[:, :144], w_kvb=r_kvb)[nm] for nm in late]
    late_out = _adamw(*[[view[nm](a[nm]) for nm in late] for a in (w,)], late_g,
                      *[[view[nm](a[nm]) for nm in late] for a in (mom, var)], whole=False)
    for nm, gv, (dv, mv, vv) in zip(late, late_g, late_out):
        grad[nm], delta[nm], new_m[nm], new_v[nm] = [view[nm](a) for a in (gv, dv, mv, vv)]
    g_in1 = _unperm_cols(red1[0])
    shadow_work = [a for row in small_out + late_out for a in row] + [grad[nm] for nm in small] + [g_in1]
    r_in0, = _rs_end(_rs_add_chips(rs0, shadow_work), [])
    to_cols, from_cols = (lambda a: jnp.transpose(a, (2, 0, 1))), (lambda a: jnp.transpose(a, (1, 2, 0)))
    grad["w_in"], delta["w_in"], new_m["w_in"], new_v["w_in"] = [from_cols(a) for a in _adamw_cols(
        to_cols(w["w_in"]), [_unperm_cols(r_in0), g_in1], to_cols(mom["w_in"]), to_cols(var["w_in"]))]

    return (loss_out, grad_x[None], *[grad[nm] for nm in WEIGHTS], *[delta[nm] for nm in WEIGHTS],
            *[new_m[nm] for nm in WEIGHTS], *[new_v[nm] for nm in WEIGHTS])
```

```python
import functools
import math

import numpy as np
import jax
import jax.numpy as jnp
from jax import lax
from jax.experimental import pallas as pl
from jax.experimental.pallas import tpu as pltpu

F32 = jnp.float32
MXU = jnp.bfloat16

D_MODEL = 1024
DEPTH = 2
D_CONV_A = 256
D_SSD = 384
SSD_HEADS = 6
SSD_BC = 256
SSD_CHUNK = 128
SSD_CHUNKS_PER_STEP = 4
SSD_NORM_EPS = 1e-5
MLA_HEADS = 6
Q_LORA = 256
KV_LORA = 128
QK_NOPE = 64
QK_ROPE = 32
V_DIM = 64
D_MLA = 384
ROPE_BASE = 10000.0
NORM_EPS = 1e-6
IN_COLS = 3110
LANE = 128

O_AH, O_AB, O_AC, O_AZ = 0, 256, 512, 768
O_XBC = 1024
O_SZ = 1920
O_CQA = 2304
O_CKV = 2560
O_CZ = 2688
O_TAIL = 3072
NCOL = 3200
N_XBC = D_SSD + 2 * SSD_BC
DT_LANE = 32
ROPE_LANE = 64

ADAM_LR, ADAM_B1, ADAM_B2, ADAM_EPS, ADAM_WD, ADAM_STEP = 0.001, 0.9, 0.999, 1e-08, 0.01, 10

VMEM_LIMIT = 56 * 1024 * 1024
MESH_T = pl.DeviceIdType.MESH


def _dot(a, b):
    return jnp.dot(a.astype(MXU), b.astype(MXU), preferred_element_type=F32)


def _dot_nt(a, b):
    return lax.dot_general(a.astype(MXU), b.astype(MXU), (((1,), (1,)), ((), ())), preferred_element_type=F32)


def _dot_tn(a, b):
    return lax.dot_general(a.astype(MXU), b.astype(MXU), (((0,), (0,)), ((), ())), preferred_element_type=F32)


def _dot_hi(a, b):
    return jnp.dot(a, b, precision=lax.Precision.HIGHEST, preferred_element_type=F32)


def _dot_hi_tn(a, b):
    return lax.dot_general(a, b, (((0,), (0,)), ((), ())), precision=lax.Precision.HIGHEST, preferred_element_type=F32)


def _sigmoid(z):
    return 1.0 / (1.0 + jnp.exp(-z))


def _silu(z):
    return z * _sigmoid(z)


def _dsilu(z):
    s = _sigmoid(z)
    return s * (1.0 + z * (1.0 - s))


def _softplus(z):
    e = jnp.exp(-jnp.abs(z))
    return jnp.maximum(z, 0.0) + jnp.where(e < 1e-3, e * (1.0 - 0.5 * e), jnp.log(1.0 + e))


def _iota(shape, dim):
    return lax.broadcasted_iota(jnp.int32, shape, dim)


def _shift_down(u, k):
    if k == 0:
        return u
    return jnp.where(_iota(u.shape, 0) >= k, pltpu.roll(u, k, 0), 0.0)


def _shift_up(u, k):
    if k == 0:
        return u
    n = u.shape[0]
    return jnp.where(_iota(u.shape, 0) < n - k, pltpu.roll(u, n - k, 0), 0.0)


def _rope_swap(t):
    lane = _iota(t.shape, 1)
    lo = (lane >= ROPE_LANE) & (lane < ROPE_LANE + 16)
    hi = (lane >= ROPE_LANE + 16) & (lane < ROPE_LANE + 32)
    return jnp.where(lo, pltpu.roll(t, LANE - 16, 1), jnp.where(hi, pltpu.roll(t, 16, 1), 0.0))


def _params(sem=None):
    return pltpu.CompilerParams(dimension_semantics=sem, vmem_limit_bytes=VMEM_LIMIT)


def _full(shape):
    nd = len(shape)
    return pl.BlockSpec(shape, lambda *_: (0,) * nd)


def _sds(shape, dtype=F32):
    return jax.ShapeDtypeStruct(shape, dtype)


def _tile(s):
    return min(512, s)


def _row(ts, w):
    return pl.BlockSpec((ts, w), lambda i: (i, 0))


def _gate_cols(ts, off):
    return pl.BlockSpec((ts, D_SSD), lambda i, _o=off // D_SSD: (i, _o))


def _col(s, off):
    return pl.BlockSpec((s, LANE), lambda j, _o=off // LANE: (0, _o + j))


def _call_after(dep, body, args, *, in_specs, **kw):
    if dep is None:
        return pl.pallas_call(body, in_specs=in_specs, **kw)(*args)
    n = len(args)

    def body_dep(*refs):
        body(*refs[:n], *refs[n + 1:])

    return pl.pallas_call(body_dep, in_specs=list(in_specs) + [pl.BlockSpec(memory_space=pl.ANY)], **kw)(*args, dep)


def _rms(c, g):
    r = lax.rsqrt(jnp.mean(c * c, axis=-1, keepdims=True) + NORM_EPS)
    return c * r * g, r


def _rms_bwd(dn, c, r, g):
    ch = c * r
    dch = dn * g
    dc = r * (dch - ch * jnp.mean(dch * ch, axis=-1, keepdims=True))
    return dc, jnp.sum(dn * ch, axis=0, keepdims=True)


def _inproj_fwd(x, g, w, dep=None):
    s = x.shape[0]
    ts = _tile(s)

    def body(x_ref, g_ref, w_ref, proj_ref, h_ref, r_ref):
        hn, r = _rms(x_ref[...], g_ref[...])
        h = hn.astype(MXU)
        h_ref[...] = h
        r_ref[...] = r
        proj_ref[...] = jnp.dot(h, w_ref[...], preferred_element_type=F32)

    return _call_after(
        dep, body, (x, g, w), name="inproj_fwd", grid=(s // ts,),
        in_specs=[_row(ts, D_MODEL), _full((1, D_MODEL)), _full((D_MODEL, NCOL))],
        out_specs=[_row(ts, NCOL), _row(ts, D_MODEL), _row(ts, 1)],
        out_shape=[_sds((s, NCOL)), _sds((s, D_MODEL), MXU), _sds((s, 1))],
        compiler_params=_params(("parallel",)),
    )


def _conva_fwd(proj, w):
    s = proj.shape[0]

    def body(h_ref, b_ref, c_ref, z_ref, w_ref, y_ref):
        u = c_ref[...] * h_ref[...]
        wv = w_ref[...]
        cv = wv[2:3, :] * u + wv[1:2, :] * _shift_down(u, 1) + wv[0:1, :] * _shift_down(u, 2)
        y_ref[...] = b_ref[...] * cv * _silu(z_ref[...])

    return pl.pallas_call(
        body, name="conva_fwd", grid=(D_CONV_A // LANE,),
        in_specs=[_col(s, O_AH), _col(s, O_AB), _col(s, O_AC), _col(s, O_AZ), pl.BlockSpec((3, LANE), lambda j: (0, j))],
        out_specs=pl.BlockSpec((s, LANE), lambda j: (0, j)),
        out_shape=_sds((s, D_CONV_A)),
        compiler_params=_params(("parallel",)),
    )(proj, proj, proj, proj, w)


def _sconv_pre(u, wv, bv):
    return (wv[3:4, :] * u + wv[2:3, :] * _shift_down(u, 1) + wv[1:2, :] * _shift_down(u, 2)
            + wv[0:1, :] * _shift_down(u, 3) + bv)


def _sconv_fwd(proj, w, b):
    s = proj.shape[0]

    def body(u_ref, w_ref, b_ref, o_ref):
        o_ref[...] = _silu(_sconv_pre(u_ref[...], w_ref[...], b_ref[...]))

    return pl.pallas_call(
        body, name="sconv_fwd", grid=(N_XBC // LANE,),
        in_specs=[_col(s, O_XBC), pl.BlockSpec((4, LANE), lambda j: (0, j)), pl.BlockSpec((1, LANE), lambda j: (0, j))],
        out_specs=pl.BlockSpec((s, LANE), lambda j: (0, j)),
        out_shape=_sds((s, N_XBC)),
        compiler_params=_params(("parallel",)),
    )(proj, w, b)


def _ssd_chunk_common(tail, sc):
    l = SSD_CHUNK
    lane = _iota((l, LANE), 1)
    row = _iota((l, LANE), 0)
    tri = (row >= lane).astype(F32)
    a_row = -jnp.exp(sc[1:2, :])
    pre = tail + sc[0:1, :]
    dt = _softplus(pre)
    a_cs = _dot_hi(tri, dt * a_row)
    return lane, row, tri, a_row, pre, dt, a_cs, a_cs.T


def _pick_col(m, lane, k):
    return jnp.sum(jnp.where(lane == k, m, 0.0), axis=1, keepdims=True)


def _pick_row(m, row, k):
    return jnp.sum(jnp.where(row == k, m, 0.0), axis=0, keepdims=True)


def _ssd_fwd(xbc, proj, sc):
    s = xbc.shape[0]
    nc = s // SSD_CHUNK
    l = SSD_CHUNK
    cps = SSD_CHUNKS_PER_STEP

    def body(xbc_ref, tail_ref, sc_ref, y_ref, st_ref, state):
        @pl.when(pl.program_id(0) == 0)
        def _():
            state[...] = jnp.zeros_like(state)

        sc_v = sc_ref[...]
        lane1 = _iota((1, LANE), 1)
        rowp = _iota((LANE, 1), 0)
        d_row = sc_v[2:3, :]
        states = [state[j] for j in range(3)]
        for u in range(cps):
            r = slice(u * l, (u + 1) * l)
            lane, row, _, _, _, dt, a_cs, a_t = _ssd_chunk_common(tail_ref[r, :], sc_v)
            for j in range(3):
                st_ref[u, j] = states[j]
            for j in range(3):
                xpair = xbc_ref[r, LANE * j:LANE * (j + 1)]
                sp = states[j]
                ypair = jnp.zeros((l, LANE), F32)
                new_s = jnp.zeros((LANE, LANE), F32)
                decay = jnp.zeros((LANE, 1), F32)
                for half in range(2):
                    h = 2 * j + half
                    g = h // 3
                    hm = (lane < 64) if half == 0 else (lane >= 64)
                    hrow = (rowp < 64) if half == 0 else (rowp >= 64)
                    ac = _pick_col(a_cs, lane, DT_LANE + h)
                    ar = _pick_row(a_t, row, DT_LANE + h)
                    dtc = _pick_col(dt, lane, DT_LANE + h)
                    alast = jnp.sum(jnp.where(lane1 == l - 1, ar, 0.0), axis=1, keepdims=True)
                    dh = jnp.sum(jnp.where(lane1 == DT_LANE + h, d_row, 0.0), axis=1, keepdims=True)
                    xm = jnp.where(hm, xpair, 0.0)
                    xd = xm * dtc
                    bm = xbc_ref[r, D_SSD + LANE * g:D_SSD + LANE * (g + 1)]
                    cm = xbc_ref[r, D_SSD + SSD_BC + LANE * g:D_SSD + SSD_BC + LANE * (g + 1)]
                    lm = jnp.where(row >= lane, jnp.exp(jnp.minimum(ac - ar, 0.0)), 0.0)
                    y_diag = _dot(_dot_nt(cm, bm) * lm, xd)
                    y_off = jnp.where(hm, _dot_nt(cm, sp), 0.0) * jnp.exp(ac)
                    ypair = ypair + y_diag + y_off + xm * dh
                    new_s = new_s + _dot_tn(xd * jnp.exp(alast - ac), bm)
                    decay = jnp.where(hrow, jnp.exp(alast), decay)
                states[j] = sp * decay + new_s
                y_ref[r, LANE * j:LANE * (j + 1)] = ypair
        for j in range(3):
            state[j] = states[j]

    return pl.pallas_call(
        body, name="ssd_fwd", grid=(nc // cps,),
        in_specs=[pl.BlockSpec((cps * l, N_XBC), lambda c: (c, 0)),
                  pl.BlockSpec((cps * l, LANE), lambda c: (c, O_TAIL // LANE)), _full((8, LANE))],
        out_specs=[pl.BlockSpec((cps * l, D_SSD), lambda c: (c, 0)), pl.BlockSpec((cps, 3, LANE, LANE), lambda c: (c, 0, 0, 0))],
        out_shape=[_sds((s, D_SSD)), _sds((nc, 3, LANE, LANE))],
        scratch_shapes=[pltpu.VMEM((3, LANE, LANE), F32)],
        compiler_params=_params(("arbitrary",)),
    )(xbc, proj, sc)


def _mla_prep_fwd(proj, gq, gkv, wq, wkv, cos, sin):
    s = proj.shape[0]
    ts = _tile(s)
    nh = MLA_HEADS

    def body(cqa_ref, ckv_ref, tail_ref, gq_ref, gkv_ref, wq_ref, wkv_ref, cos_ref, sin_ref,
             q_ref, k_ref, v_ref, qn_ref, kvn_ref, rq_ref, rkv_ref):
        qn, rq = _rms(cqa_ref[...], gq_ref[...])
        kvn, rkv = _rms(ckv_ref[...], gkv_ref[...])
        qn = qn.astype(MXU)
        kvn = kvn.astype(MXU)
        qn_ref[...] = qn
        kvn_ref[...] = kvn
        rq_ref[...] = rq
        rkv_ref[...] = rkv
        q = _dot_nt(qn, wq_ref[...])
        kv = _dot_nt(kvn, wkv_ref[...])
        cosv = cos_ref[...]
        sinv = sin_ref[...]
        lane = _iota((ts, LANE), 1)
        rope_lanes = (lane >= ROPE_LANE) & (lane < ROPE_LANE + QK_ROPE)
        kr = jnp.where(rope_lanes, pltpu.roll(tail_ref[...], ROPE_LANE, 1), 0.0)
        kr = kr * cosv + _rope_swap(kr) * sinv
        for h in range(nh):
            qh = q[:, LANE * h:LANE * (h + 1)]
            q_ref[h] = ((qh * cosv + _rope_swap(qh) * sinv) * ATT_SCALE).astype(MXU)
            k_ref[h] = (kv[:, LANE * h:LANE * (h + 1)] + kr).astype(MXU)
            v_ref[h] = kv[:, LANE * (nh + h):LANE * (nh + h + 1)].astype(MXU)

    head = pl.BlockSpec((nh, ts, LANE), lambda i: (0, i, 0))
    return pl.pallas_call(
        body, name="mla_prep_fwd", grid=(s // ts,),
        in_specs=[pl.BlockSpec((ts, Q_LORA), lambda i: (i, O_CQA // Q_LORA)),
                  pl.BlockSpec((ts, KV_LORA), lambda i: (i, O_CKV // KV_LORA)),
                  pl.BlockSpec((ts, LANE), lambda i: (i, O_TAIL // LANE)),
                  _full((1, Q_LORA)), _full((1, KV_LORA)), _full((nh * LANE, Q_LORA)), _full((2 * nh * LANE, KV_LORA)),
                  _row(ts, LANE), _row(ts, LANE)],
        out_specs=[head, head, head, _row(ts, Q_LORA), _row(ts, KV_LORA), _row(ts, 1), _row(ts, 1)],
        out_shape=[_sds((nh, s, LANE), MXU)] * 3 + [_sds((s, Q_LORA), MXU), _sds((s, KV_LORA), MXU), _sds((s, 1)), _sds((s, 1))],
        compiler_params=_params(("parallel",)),
    )(proj, proj, proj, gq, gkv, wq, wkv, cos, sin)


ATT_SCALE = (QK_NOPE + QK_ROPE) ** -0.5
NEG = -1e30


def _att_tile(s, most):
    return min(most, s // 2)


ATT_FWD_TILE = 1024
ATT_BWD_TILE = 512


def _attn_fwd(q, k, v):
    nh, s, _ = q.shape
    tq = _att_tile(s, ATT_FWD_TILE)
    nq = s // tq

    def body(q_ref, k_ref, v_ref, o_ref, lse_ref):
        i = pl.program_id(1)
        rowi = _iota((tq, tq), 0)
        coli = _iota((tq, tq), 1)
        zero = (jnp.full((tq, 1), NEG, F32), jnp.zeros((tq, 1), F32), jnp.zeros((tq, LANE), F32))
        state = [zero, zero]
        done = [zero, zero]
        for t in range(nq + 1):
            first = t <= i
            qblk = jnp.where(first, i, nq - 1 - i)
            kblk = jnp.where(first, t, t - i - 1)
            qoff = pl.multiple_of(qblk * tq, tq)
            koff = pl.multiple_of(kblk * tq, tq)
            keep = coli <= rowi + jnp.where(kblk == qblk, 0, tq)
            restart = t == i + 1
            for hh in range(2):
                m, lsum, acc = state[hh]
                if t > 0:
                    done[hh] = tuple(jnp.where(restart, a, b) for a, b in zip(state[hh], done[hh]))
                    m = jnp.where(restart, NEG, m)
                    lsum = jnp.where(restart, 0.0, lsum)
                    acc = jnp.where(restart, 0.0, acc)
                sc = _dot_nt(q_ref[hh, pl.ds(qoff, tq), :], k_ref[hh, pl.ds(koff, tq), :])
                sc = jnp.where(keep, sc, NEG)
                m_new = jnp.maximum(m, jnp.max(sc, axis=1, keepdims=True))
                p = jnp.exp(sc - m_new)
                alpha = jnp.exp(m - m_new)
                lsum = alpha * lsum + jnp.sum(p, axis=1, keepdims=True)
                acc = alpha * acc + _dot(p, v_ref[hh, pl.ds(koff, tq), :])
                state[hh] = (m_new, lsum, acc)
        for blk, res in ((i, done), (nq - 1 - i, state)):
            off = pl.multiple_of(blk * tq, tq)
            out = None
            for hh in range(2):
                m, lsum, acc = res[hh]
                o = acc * (1.0 / lsum)
                lse_ref[hh, pl.ds(off, tq), :] = m + jnp.log(lsum)
                out = o if hh == 0 else out + pltpu.roll(o, V_DIM, 1)
            o_ref[pl.ds(off, tq), :] = out

    pair = pl.BlockSpec((2, s, LANE), lambda j, i: (j, 0, 0))
    return pl.pallas_call(
        body, name="attn_fwd", grid=(nh // 2, nq // 2),
        in_specs=[pair, pair, pair],
        out_specs=[pl.BlockSpec((s, LANE), lambda j, i: (0, j)), pl.BlockSpec((2, s, 1), lambda j, i: (j, 0, 0))],
        out_shape=[_sds((s, D_MLA)), _sds((nh, s, 1))],
        compiler_params=_params(("parallel", "arbitrary")),
    )(q, k, v)


def _ssd_gate(y_ssd, s_z, g):
    yz = y_ssd * _silu(s_z)
    g0 = _iota(yz.shape, 1) < D_SSD // 2
    sq = yz * yz
    ms0 = jnp.sum(jnp.where(g0, sq, 0.0), axis=1, keepdims=True) / (D_SSD // 2)
    ms1 = jnp.sum(jnp.where(g0, 0.0, sq), axis=1, keepdims=True) / (D_SSD // 2)
    r = jnp.where(g0, lax.rsqrt(ms0 + SSD_NORM_EPS), lax.rsqrt(ms1 + SSD_NORM_EPS))
    nrm = yz * r
    return nrm * g, nrm, r, g0


def _outproj_fwd(x, proj, ya, y_ssd, o, g_ssd, w):
    s = x.shape[0]
    ts = _tile(s)

    def body(x_ref, sz_ref, cz_ref, ya_ref, ys_ref, o_ref, g_ref, w_ref, xo_ref, y_ref):
        yb = _ssd_gate(ys_ref[...], sz_ref[...], g_ref[...])[0]
        yc = o_ref[...] * _silu(cz_ref[...])
        y = jnp.concatenate([ya_ref[...], yb, yc], axis=1).astype(MXU)
        y_ref[...] = y
        xo_ref[...] = x_ref[...] + jnp.dot(y, w_ref[...], preferred_element_type=F32)

    return pl.pallas_call(
        body, name="outproj_fwd", grid=(s // ts,),
        in_specs=[_row(ts, D_MODEL), _gate_cols(ts, O_SZ), _gate_cols(ts, O_CZ), _row(ts, D_CONV_A), _row(ts, D_SSD),
                  _row(ts, D_MLA), _full((1, D_SSD)), _full((D_MODEL, D_MODEL))],
        out_specs=[_row(ts, D_MODEL), _row(ts, D_MODEL)],
        out_shape=[_sds((s, D_MODEL)), _sds((s, D_MODEL), MXU)],
        compiler_params=_params(("parallel",)),
    )(x, proj, proj, ya, y_ssd, o, g_ssd, w)


def _loss_head(x, g, tgt):
    s = x.shape[0]
    ts = _tile(s)

    def body(x_ref, g_ref, t_ref, dx_ref, dg_ref, loss_ref):
        @pl.when(pl.program_id(0) == 0)
        def _():
            dg_ref[...] = jnp.zeros_like(dg_ref)
            loss_ref[...] = jnp.zeros_like(loss_ref)

        xv = x_ref[...]
        gv = g_ref[...]
        yn, r = _rms(xv, gv)
        e = yn - t_ref[...]
        loss_ref[...] += jnp.sum(jnp.sum(e * e, axis=1, keepdims=True), axis=0, keepdims=True) * (0.5 / D_MODEL)
        dx, dg = _rms_bwd(e * (1.0 / D_MODEL), xv, r, gv)
        dx_ref[...] = dx
        dg_ref[...] += dg

    return pl.pallas_call(
        body, name="loss_head", grid=(s // ts,),
        in_specs=[_row(ts, D_MODEL), _full((1, D_MODEL)), _row(ts, D_MODEL)],
        out_specs=[_row(ts, D_MODEL), _full((1, D_MODEL)), _full((1, LANE))],
        out_shape=[_sds((s, D_MODEL)), _sds((1, D_MODEL)), _sds((1, LANE))],
        compiler_params=_params(("arbitrary",)),
    )(x, g, tgt)


def _outproj_bwd(dout, y, w, proj, y_ssd, o, g_ssd, dep=None):
    s = dout.shape[0]
    ts = _tile(s)

    def body(dout_ref, y_ref, w_ref, sz_ref, cz_ref, ys_ref, o_ref, g_ref,
             dya_ref, dys_ref, dsz_ref, dattn_ref, dcz_ref, dg_ref, dw_ref):
        @pl.when(pl.program_id(0) == 0)
        def _():
            dw_ref[...] = jnp.zeros_like(dw_ref)
            dg_ref[...] = jnp.zeros_like(dg_ref)

        dout_b = dout_ref[...].astype(MXU)
        dw_ref[...] += _dot_tn(y_ref[...], dout_b)
        dy = _dot_nt(dout_b, w_ref[...])
        dya_ref[...] = dy[:, :D_CONV_A]
        dyb = dy[:, D_CONV_A:D_CONV_A + D_SSD]
        sz = sz_ref[...]
        ys = ys_ref[...]
        gv = g_ref[...]
        _, nrm, r, g0 = _ssd_gate(ys, sz, gv)
        dg_ref[...] += jnp.sum(dyb * nrm, axis=0, keepdims=True)
        dn = dyb * gv
        t = dn * nrm
        mean = jnp.where(g0, jnp.sum(jnp.where(g0, t, 0.0), axis=1, keepdims=True),
                         jnp.sum(jnp.where(g0, 0.0, t), axis=1, keepdims=True)) / (D_SSD // 2)
        dyz = r * (dn - nrm * mean)
        dys_ref[...] = dyz * _silu(sz)
        dsz_ref[...] = (dyz * ys * _dsilu(sz)).astype(MXU)
        dyc = dy[:, D_CONV_A + D_SSD:]
        cz = cz_ref[...]
        dattn_ref[...] = dyc * _silu(cz)
        dcz_ref[...] = (dyc * o_ref[...] * _dsilu(cz)).astype(MXU)

    return _call_after(
        dep, body, (dout, y, w, proj, proj, y_ssd, o, g_ssd), name="outproj_bwd", grid=(s // ts,),
        in_specs=[_row(ts, D_MODEL), _row(ts, D_MODEL), _full((D_MODEL, D_MODEL)), _gate_cols(ts, O_SZ), _gate_cols(ts, O_CZ),
                  _row(ts, D_SSD), _row(ts, D_MLA), _full((1, D_SSD))],
        out_specs=[_row(ts, D_CONV_A), _row(ts, D_SSD), _row(ts, D_SSD), _row(ts, D_MLA), _row(ts, D_MLA),
                   _full((1, D_SSD)), _full((D_MODEL, D_MODEL))],
        out_shape=[_sds((s, D_CONV_A)), _sds((s, D_SSD)), _sds((s, D_SSD), MXU), _sds((s, D_MLA)), _sds((s, D_MLA), MXU),
                   _sds((1, D_SSD)), _sds((D_MODEL, D_MODEL))],
        compiler_params=_params(("arbitrary",)),
    )


def _attn_bwd(q, k, v, o, d_o, lse, dep=None):
    nh, s, _ = q.shape
    tq = _att_tile(s, ATT_BWD_TILE)
    nq = s // tq

    def body(q_ref, k_ref, v_ref, o_ref, do_ref, lse_ref, dq_ref, dk_ref, dv_ref, dop, delta):
        i = pl.program_id(1)

        @pl.when(i == 0)
        def _():
            lane = _iota((s, LANE), 1)
            for hh in range(2):
                dov = do_ref[...]
                ov = o_ref[...]
                if hh == 1:
                    dov = pltpu.roll(dov, V_DIM, 1)
                    ov = pltpu.roll(ov, V_DIM, 1)
                dov = jnp.where(lane < V_DIM, dov, 0.0)
                dop[hh] = dov.astype(MXU)
                delta[hh] = jnp.sum(dov * ov, axis=1, keepdims=True)
                dq_ref[hh] = jnp.zeros((s, LANE), F32)

        rowi = _iota((tq, tq), 0)
        coli = _iota((tq, tq), 1)
        z = jnp.zeros((tq, LANE), F32)
        state = [(z, z), (z, z)]
        done = [(z, z), (z, z)]
        for t in range(nq + 1):
            first = t <= nq - 1 - i
            kblk = jnp.where(first, i, nq - 1 - i)
            qblk = jnp.where(first, i + t, t - 1)
            qoff = pl.multiple_of(qblk * tq, tq)
            koff = pl.multiple_of(kblk * tq, tq)
            keep = coli <= rowi + jnp.where(kblk == qblk, 0, tq)
            restart = t == nq - i
            for hh in range(2):
                dk, dv = state[hh]
                if t > 0:
                    done[hh] = tuple(jnp.where(restart, a, b) for a, b in zip(state[hh], done[hh]))
                    dk = jnp.where(restart, 0.0, dk)
                    dv = jnp.where(restart, 0.0, dv)
                kb = k_ref[hh, pl.ds(koff, tq), :]
                qb = q_ref[hh, pl.ds(qoff, tq), :]
                dob = dop[hh, pl.ds(qoff, tq), :]
                sc = jnp.where(keep, _dot_nt(qb, kb), NEG)
                p = jnp.exp(sc - lse_ref[hh, pl.ds(qoff, tq), :])
                dp = _dot_nt(dob, v_ref[hh, pl.ds(koff, tq), :])
                ds = p * (dp - delta[hh, pl.ds(qoff, tq), :])
                dq_ref[hh, pl.ds(qoff, tq), :] += _dot(ds, kb)
                state[hh] = (dk + _dot_tn(ds, qb), dv + _dot_tn(p, dob))
        for blk, res in ((i, done), (nq - 1 - i, state)):
            off = pl.multiple_of(blk * tq, tq)
            for hh in range(2):
                dk_ref[hh, pl.ds(off, tq), :] = res[hh][0]
                dv_ref[hh, pl.ds(off, tq), :] = res[hh][1]

    pair = pl.BlockSpec((2, s, LANE), lambda j, i: (j, 0, 0))
    return _call_after(
        dep, body, (q, k, v, o, d_o, lse), name="attn_bwd", grid=(nh // 2, nq // 2),
        in_specs=[pair, pair, pair, pl.BlockSpec((s, LANE), lambda j, i: (0, j)), pl.BlockSpec((s, LANE), lambda j, i: (0, j)),
                  pl.BlockSpec((2, s, 1), lambda j, i: (j, 0, 0))],
        out_specs=[pair, pair, pair],
        out_shape=[_sds((nh, s, LANE))] * 3,
        scratch_shapes=[pltpu.VMEM((2, s, LANE), MXU), pltpu.VMEM((2, s, 1), F32)],
        compiler_params=_params(("parallel", "arbitrary")),
    )


def _ssd_bwd(xbc, proj, sc, states, dy, dep=None):
    s = xbc.shape[0]
    nc = s // SSD_CHUNK
    l = SSD_CHUNK
    cps = SSD_CHUNKS_PER_STEP

    def body(xbc_ref, tail_ref, sc_ref, st_ref, dy_ref, dxbc_ref, dtail_ref, dsc_ref, dstate):
        @pl.when(pl.program_id(0) == 0)
        def _():
            dstate[...] = jnp.zeros_like(dstate)
            dsc_ref[...] = jnp.zeros_like(dsc_ref)

        sc_v = sc_ref[...]
        lane1 = _iota((1, LANE), 1)
        rowp = _iota((LANE, 1), 0)
        rowl = _iota((l, 1), 0)
        d_row = sc_v[2:3, :]
        dstates = [dstate[j] for j in range(3)]
        for u in reversed(range(cps)):
            dstates = chunk(u, xbc_ref, tail_ref, sc_v, st_ref, dy_ref, dxbc_ref, dtail_ref, dsc_ref, dstates,
                            lane1, rowp, rowl, d_row)
        for j in range(3):
            dstate[j] = dstates[j]

    def chunk(u, xbc_ref, tail_ref, sc_v, st_ref, dy_ref, dxbc_ref, dtail_ref, dsc_ref, dstates, lane1, rowp, rowl, d_row):
        r = slice(u * l, (u + 1) * l)
        dstates = list(dstates)
        lane, row, tri, a_row, pre, dt, a_cs, a_t = _ssd_chunk_common(tail_ref[r, :], sc_v)
        da_col = jnp.zeros((l, LANE), F32)
        da_row = jnp.zeros((LANE, l), F32)
        dt_x = jnp.zeros((l, LANE), F32)
        dd_row = jnp.zeros((1, LANE), F32)
        db = [jnp.zeros((l, LANE), F32), jnp.zeros((l, LANE), F32)]
        dc = [jnp.zeros((l, LANE), F32), jnp.zeros((l, LANE), F32)]
        for j in range(3):
            xpair = xbc_ref[r, LANE * j:LANE * (j + 1)]
            dypair = dy_ref[r, LANE * j:LANE * (j + 1)]
            sp = st_ref[u, j]
            dsp = dstates[j]
            dxpair = jnp.zeros((l, LANE), F32)
            ds_new = jnp.zeros((LANE, LANE), F32)
            decay = jnp.zeros((LANE, 1), F32)
            for half in range(2):
                h = 2 * j + half
                g = h // 3
                hm = (lane < 64) if half == 0 else (lane >= 64)
                hrow = (rowp < 64) if half == 0 else (rowp >= 64)
                ac = _pick_col(a_cs, lane, DT_LANE + h)
                ar = _pick_row(a_t, row, DT_LANE + h)
                dtc = _pick_col(dt, lane, DT_LANE + h)
                alast = jnp.sum(jnp.where(lane1 == l - 1, ar, 0.0), axis=1, keepdims=True)
                dh = jnp.sum(jnp.where(lane1 == DT_LANE + h, d_row, 0.0), axis=1, keepdims=True)
                xm = jnp.where(hm, xpair, 0.0)
                xd = xm * dtc
                dym = jnp.where(hm, dypair, 0.0)
                bm = xbc_ref[r, D_SSD + LANE * g:D_SSD + LANE * (g + 1)]
                cm = xbc_ref[r, D_SSD + SSD_BC + LANE * g:D_SSD + SSD_BC + LANE * (g + 1)]
                lm = jnp.where(row >= lane, jnp.exp(jnp.minimum(ac - ar, 0.0)), 0.0)
                e_in = jnp.exp(ac)
                f_out = jnp.exp(alast - ac)
                e_last = jnp.exp(alast)
                m = _dot_nt(cm, bm) * lm
                y_off = jnp.where(hm, _dot_nt(cm, sp), 0.0) * e_in
                dm = _dot_nt(dym, xd)
                dxd = _dot_tn(m, dym)
                dg = dm * lm
                dye = dym * e_in
                dc[g] = dc[g] + _dot(dg, bm) + _dot(dye, sp)
                db[g] = db[g] + _dot_tn(dg, cm)
                qm = dm * m
                dac = jnp.sum(qm, axis=1, keepdims=True) + jnp.sum(dym * y_off, axis=1, keepdims=True)
                dar = -jnp.sum(qm, axis=0, keepdims=True)
                dxf = jnp.where(hm, _dot_nt(bm, dsp), 0.0)
                db[g] = db[g] + _dot(xd * f_out, dsp)
                dxd = dxd + dxf * f_out
                df = jnp.sum(dxf * xd, axis=1, keepdims=True) * f_out
                dac = dac - df
                s_last = jnp.sum(df, axis=0, keepdims=True)
                ss = jnp.sum(jnp.where(hrow, dsp * sp, 0.0), axis=1, keepdims=True)
                s_last = s_last + e_last * jnp.sum(ss, axis=0, keepdims=True)
                dac = dac + jnp.where(rowl == l - 1, s_last, 0.0)
                ds_new = ds_new + _dot_tn(dye, cm)
                decay = jnp.where(hrow, e_last, decay)
                dxpair = dxpair + dxd * dtc + dym * dh
                dt_x = dt_x + jnp.where(lane == DT_LANE + h, jnp.sum(dxd * xm, axis=1, keepdims=True), 0.0)
                dsum = jnp.sum(jnp.sum(dym * xm, axis=1, keepdims=True), axis=0, keepdims=True)
                dd_row = dd_row + jnp.where(lane1 == DT_LANE + h, dsum, 0.0)
                da_col = da_col + jnp.where(lane == DT_LANE + h, dac, 0.0)
                da_row = da_row + jnp.where(row == DT_LANE + h, dar, 0.0)
            dstates[j] = dsp * decay + ds_new
            dxbc_ref[r, LANE * j:LANE * (j + 1)] = dxpair
        for g in range(2):
            dxbc_ref[r, D_SSD + LANE * g:D_SSD + LANE * (g + 1)] = db[g]
            dxbc_ref[r, D_SSD + SSD_BC + LANE * g:D_SSD + SSD_BC + LANE * (g + 1)] = dc[g]
        dla = _dot_hi_tn(tri, da_col + da_row.T)
        ddt = dt_x + dla * a_row
        dpre = ddt * _sigmoid(pre)
        dtm = (lane >= DT_LANE) & (lane < DT_LANE + SSD_HEADS)
        dtail_ref[r, :] = jnp.where(dtm, dpre, 0.0).astype(MXU)
        dtm1 = (lane1 >= DT_LANE) & (lane1 < DT_LANE + SSD_HEADS)
        dsc_ref[0:1, :] += jnp.where(dtm1, jnp.sum(dpre, axis=0, keepdims=True), 0.0)
        dsc_ref[1:2, :] += jnp.where(dtm1, jnp.sum(dla * dt, axis=0, keepdims=True) * a_row, 0.0)
        dsc_ref[2:3, :] += dd_row
        return dstates

    rev = lambda c: nc // cps - 1 - c
    return _call_after(
        dep, body, (xbc, proj, sc, states, dy), name="ssd_bwd", grid=(nc // cps,),
        in_specs=[pl.BlockSpec((cps * l, N_XBC), lambda c: (rev(c), 0)),
                  pl.BlockSpec((cps * l, LANE), lambda c: (rev(c), O_TAIL // LANE)), _full((8, LANE)),
                  pl.BlockSpec((cps, 3, LANE, LANE), lambda c: (rev(c), 0, 0, 0)),
                  pl.BlockSpec((cps * l, D_SSD), lambda c: (rev(c), 0))],
        out_specs=[pl.BlockSpec((cps * l, N_XBC), lambda c: (rev(c), 0)), pl.BlockSpec((cps * l, LANE), lambda c: (rev(c), 0)),
                   _full((8, LANE))],
        out_shape=[_sds((s, N_XBC)), _sds((s, LANE), MXU), _sds((8, LANE))],
        scratch_shapes=[pltpu.VMEM((3, LANE, LANE), F32)],
        compiler_params=_params(("arbitrary",)),
    )


def _sconv_bwd(proj, w, b, dxbc, dep=None):
    s = proj.shape[0]

    def body(u_ref, w_ref, b_ref, d_ref, du_ref, dw_ref, db_ref):
        u = u_ref[...]
        wv = w_ref[...]
        dpre = d_ref[...] * _dsilu(_sconv_pre(u, wv, b_ref[...]))
        ahead = [_shift_up(dpre, j) for j in range(4)]
        du_ref[...] = (wv[3:4, :] * ahead[0] + wv[2:3, :] * ahead[1] + wv[1:2, :] * ahead[2]
                       + wv[0:1, :] * ahead[3]).astype(MXU)
        for k in range(4):
            dw_ref[k:k + 1, :] = jnp.sum(ahead[3 - k] * u, axis=0, keepdims=True)
        db_ref[...] = jnp.sum(dpre, axis=0, keepdims=True)

    blk = pl.BlockSpec((s, LANE), lambda j: (0, j))
    return _call_after(
        dep, body, (proj, w, b, dxbc), name="sconv_bwd", grid=(N_XBC // LANE,),
        in_specs=[_col(s, O_XBC), pl.BlockSpec((4, LANE), lambda j: (0, j)), pl.BlockSpec((1, LANE), lambda j: (0, j)), blk],
        out_specs=[blk, pl.BlockSpec((4, LANE), lambda j: (0, j)), pl.BlockSpec((1, LANE), lambda j: (0, j))],
        out_shape=[_sds((s, N_XBC), MXU), _sds((4, N_XBC)), _sds((1, N_XBC))],
        compiler_params=_params(("parallel",)),
    )


def _conva_bwd(proj, w, dya, dep=None):
    s = proj.shape[0]

    def body(h_ref, b_ref, c_ref, z_ref, w_ref, d_ref, da_ref, dw_ref):
        ah, ab, acv, az = h_ref[...], b_ref[...], c_ref[...], z_ref[...]
        wv = w_ref[...]
        u = acv * ah
        cv = wv[2:3, :] * u + wv[1:2, :] * _shift_down(u, 1) + wv[0:1, :] * _shift_down(u, 2)
        dy = d_ref[...]
        sz = _silu(az)
        da_ref[1] = (dy * cv * sz).astype(MXU)
        da_ref[3] = (dy * ab * cv * _dsilu(az)).astype(MXU)
        dcv = dy * ab * sz
        ahead = [_shift_up(dcv, j) for j in range(3)]
        du = wv[2:3, :] * ahead[0] + wv[1:2, :] * ahead[1] + wv[0:1, :] * ahead[2]
        da_ref[0] = (du * acv).astype(MXU)
        da_ref[2] = (du * ah).astype(MXU)
        for k in range(3):
            dw_ref[k:k + 1, :] = jnp.sum(ahead[2 - k] * u, axis=0, keepdims=True)

    return _call_after(
        dep, body, (proj, proj, proj, proj, w, dya), name="conva_bwd", grid=(D_CONV_A // LANE,),
        in_specs=[_col(s, O_AH), _col(s, O_AB), _col(s, O_AC), _col(s, O_AZ), pl.BlockSpec((3, LANE), lambda j: (0, j)),
                  pl.BlockSpec((s, LANE), lambda j: (0, j))],
        out_specs=[pl.BlockSpec((4, s, LANE), lambda j: (0, 0, j)), pl.BlockSpec((3, LANE), lambda j: (0, j))],
        out_shape=[_sds((4, s, D_CONV_A), MXU), _sds((3, D_CONV_A))],
        compiler_params=_params(("parallel",)),
    )


def _mla_prep_bwd(dq, dk, dv, proj, qn, kvn, rq, rkv, gq, gkv, wq, wkv, cos, sin):
    s = proj.shape[0]
    ts = _tile(s)
    nh = MLA_HEADS

    def body(dq_ref, dk_ref, dv_ref, cqa_ref, ckv_ref, qn_ref, kvn_ref, rq_ref, rkv_ref, gq_ref, gkv_ref,
             wq_ref, wkv_ref, cos_ref, sin_ref, dcqa_ref, dckv_ref, dtail_ref, dwq_ref, dwkv_ref, dgq_ref, dgkv_ref):
        @pl.when(pl.program_id(0) == 0)
        def _():
            dwq_ref[...] = jnp.zeros_like(dwq_ref)
            dwkv_ref[...] = jnp.zeros_like(dwkv_ref)
            dgq_ref[...] = jnp.zeros_like(dgq_ref)
            dgkv_ref[...] = jnp.zeros_like(dgkv_ref)

        cosv = cos_ref[...]
        sinv = sin_ref[...]
        lane = _iota((ts, LANE), 1)
        rope_lanes = (lane >= ROPE_LANE) & (lane < ROPE_LANE + QK_ROPE)

        def unrope(gr):
            return gr * cosv + _rope_swap(gr * sinv)

        dqs, dks, dvs = [], [], []
        dkr = jnp.zeros((ts, LANE), F32)
        for h in range(nh):
            dqs.append(unrope(dq_ref[h] * ATT_SCALE).astype(MXU))
            dkh = dk_ref[h]
            dks.append(jnp.where(lane < QK_NOPE, dkh, 0.0).astype(MXU))
            dkr = dkr + jnp.where(rope_lanes, dkh, 0.0)
            dvs.append(dv_ref[h].astype(MXU))
        dtail_ref[...] = pltpu.roll(jnp.where(rope_lanes, unrope(dkr), 0.0), ROPE_LANE, 1).astype(MXU)
        dq_all = jnp.concatenate(dqs, axis=1)
        dkv_all = jnp.concatenate(dks + dvs, axis=1)
        dwq_ref[...] += _dot_tn(dq_all, qn_ref[...])
        dwkv_ref[...] += _dot_tn(dkv_all, kvn_ref[...])
        dcqa, dgq = _rms_bwd(_dot(dq_all, wq_ref[...]), cqa_ref[...], rq_ref[...], gq_ref[...])
        dckv, dgkv = _rms_bwd(_dot(dkv_all, wkv_ref[...]), ckv_ref[...], rkv_ref[...], gkv_ref[...])
        dcqa_ref[...] = dcqa.astype(MXU)
        dckv_ref[...] = dckv.astype(MXU)
        dgq_ref[...] += dgq
        dgkv_ref[...] += dgkv

    head = pl.BlockSpec((nh, ts, LANE), lambda i: (0, i, 0))
    return pl.pallas_call(
        body, name="mla_prep_bwd", grid=(s // ts,),
        in_specs=[head, head, head,
                  pl.BlockSpec((ts, Q_LORA), lambda i: (i, O_CQA // Q_LORA)),
                  pl.BlockSpec((ts, KV_LORA), lambda i: (i, O_CKV // KV_LORA)),
                  _row(ts, Q_LORA), _row(ts, KV_LORA), _row(ts, 1), _row(ts, 1),
                  _full((1, Q_LORA)), _full((1, KV_LORA)), _full((nh * LANE, Q_LORA)), _full((2 * nh * LANE, KV_LORA)),
                  _row(ts, LANE), _row(ts, LANE)],
        out_specs=[_row(ts, Q_LORA), _row(ts, KV_LORA), _row(ts, LANE), _full((nh * LANE, Q_LORA)),
                   _full((2 * nh * LANE, KV_LORA)), _full((1, Q_LORA)), _full((1, KV_LORA))],
        out_shape=[_sds((s, Q_LORA), MXU), _sds((s, KV_LORA), MXU), _sds((s, LANE), MXU), _sds((nh * LANE, Q_LORA)),
                   _sds((2 * nh * LANE, KV_LORA)), _sds((1, Q_LORA)), _sds((1, KV_LORA))],
        compiler_params=_params(("arbitrary",)),
    )(dq, dk, dv, proj, proj, qn, kvn, rq, rkv, gq, gkv, wq, wkv, cos, sin)


def _inproj_bwd(da4, dsz, dxbc_in, dcqa, dckv, dcz, dtail_a, dtail_b, w, x, rstd, g, dout, dep=None):
    s = x.shape[0]
    ts = _tile(s)

    def body(da_ref, dsz_ref, dxbc_ref, dcqa_ref, dckv_ref, dcz_ref, dta_ref, dtb_ref, w_ref, x_ref, r_ref, g_ref, dout_ref,
             dproj_ref, dx_ref, dg_ref):
        @pl.when(pl.program_id(0) == 0)
        def _():
            dg_ref[...] = jnp.zeros_like(dg_ref)

        dproj = jnp.concatenate(
            [da_ref[0], da_ref[1], da_ref[2], da_ref[3], dxbc_ref[...], dsz_ref[...], dcqa_ref[...], dckv_ref[...],
             dcz_ref[...], dta_ref[...] + dtb_ref[...]], axis=1)
        dproj_ref[...] = dproj
        dh = _dot_nt(dproj, w_ref[...])
        dx, dg = _rms_bwd(dh, x_ref[...], r_ref[...], g_ref[...])
        dx_ref[...] = dout_ref[...] + dx
        dg_ref[...] += dg

    return _call_after(
        dep, body, (da4, dsz, dxbc_in, dcqa, dckv, dcz, dtail_a, dtail_b, w, x, rstd, g, dout), name="inproj_bwd", grid=(s // ts,),
        in_specs=[pl.BlockSpec((4, ts, D_CONV_A), lambda i: (0, i, 0)), _row(ts, D_SSD), _row(ts, N_XBC), _row(ts, Q_LORA),
                  _row(ts, KV_LORA), _row(ts, D_MLA), _row(ts, LANE), _row(ts, LANE), _full((D_MODEL, NCOL)),
                  _row(ts, D_MODEL), _row(ts, 1), _full((1, D_MODEL)), _row(ts, D_MODEL)],
        out_specs=[_row(ts, NCOL), _row(ts, D_MODEL), _full((1, D_MODEL))],
        out_shape=[_sds((s, NCOL), MXU), _sds((s, D_MODEL)), _sds((1, D_MODEL))],
        compiler_params=_params(("arbitrary",)),
    )


DWIN_BLOCK = 640


def _dwin(h, dproj, dep=None):
    s = h.shape[0]

    def body(h_ref, d_ref, o_ref):
        o_ref[...] = _dot_tn(h_ref[...], d_ref[...])

    return _call_after(
        dep, body, (h, dproj), name="dwin", grid=(NCOL // DWIN_BLOCK,),
        in_specs=[_full((s, D_MODEL)), pl.BlockSpec((s, DWIN_BLOCK), lambda j: (0, j))],
        out_specs=pl.BlockSpec((D_MODEL, DWIN_BLOCK), lambda j: (0, j)),
        out_shape=_sds((D_MODEL, NCOL)),
        compiler_params=_params(("parallel",)),
    )


def _adamw(ws, gs, ms, vs, whole):
    n = len(ws)
    bc1 = 1.0 - ADAM_B1 ** ADAM_STEP
    bc2 = 1.0 - ADAM_B2 ** ADAM_STEP

    def body(*refs):
        ins, outs = refs[:4 * n], refs[4 * n:]
        for a in range(n):
            w_ref, g_ref, m_ref, v_ref = ins[a], ins[n + a], ins[2 * n + a], ins[3 * n + a]
            gv = g_ref[...]
            mn = ADAM_B1 * m_ref[...] + (1.0 - ADAM_B1) * gv
            vn = ADAM_B2 * v_ref[...] + (1.0 - ADAM_B2) * (gv * gv)
            outs[n + a][...] = mn
            outs[2 * n + a][...] = vn
            outs[a][...] = -ADAM_LR * ((mn / bc1) / (jnp.sqrt(vn / bc2) + ADAM_EPS) + ADAM_WD * w_ref[...])

    if whole:
        grid, blks = (1,), [pl.BlockSpec(w.shape, lambda i, _n=w.ndim: (0,) * _n) for w in ws]
    else:
        grid = (ws[0].shape[0], 2)
        blks = [pl.BlockSpec((1, w.shape[1] // 2, w.shape[2]), lambda i, k: (i, k, 0)) for w in ws]
    out = pl.pallas_call(
        body, name="adamw", grid=grid,
        in_specs=blks * 4, out_specs=blks * 3, out_shape=[_sds(w.shape) for w in ws] * 3,
        compiler_params=_params(("parallel",) * len(grid)),
    )(*ws, *gs, *ms, *vs)
    return [(out[a], out[n + a], out[2 * n + a]) for a in range(n)]


ADAMW_COLS_BLOCK = 512


def _adamw_cols(w_t, gs, m_t, v_t):
    cols, nl, rows = w_t.shape
    bc1 = 1.0 - ADAM_B1 ** ADAM_STEP
    bc2 = 1.0 - ADAM_B2 ** ADAM_STEP

    def body(w_ref, m_ref, v_ref, *rest):
        g_refs, (go_ref, d_ref, mo_ref, vo_ref), g_blk = rest[:nl], rest[nl:nl + 4], rest[-1]
        for l in range(nl):
            g_blk[:, l, :] = g_refs[l][...].T
        gv = g_blk[...]
        mn = ADAM_B1 * m_ref[...] + (1.0 - ADAM_B1) * gv
        vn = ADAM_B2 * v_ref[...] + (1.0 - ADAM_B2) * (gv * gv)
        go_ref[...] = gv
        mo_ref[...] = mn
        vo_ref[...] = vn
        d_ref[...] = -ADAM_LR * ((mn / bc1) / (jnp.sqrt(vn / bc2) + ADAM_EPS) + ADAM_WD * w_ref[...])

    tc = ADAMW_COLS_BLOCK
    blk = pl.BlockSpec((tc, nl, rows), lambda j: (j, 0, 0))
    gblk = pl.BlockSpec((rows, tc), lambda j: (0, j))
    return pl.pallas_call(
        body, name="adamw_cols", grid=(pl.cdiv(cols, tc),),
        in_specs=[blk] * 3 + [gblk] * nl, out_specs=[blk] * 4, out_shape=[_sds(w_t.shape)] * 4,
        scratch_shapes=[pltpu.VMEM((tc, nl, rows), F32)],
        compiler_params=_params(("parallel",)),
    )(w_t, m_t, v_t, *gs)


COL_MOVES = ((0, 0, 1024), (1024, O_SZ, 384), (1408, O_XBC, 896), (2304, O_TAIL + DT_LANE, 6), (2310, O_CQA, 256),
             (2566, O_CKV, 128), (2694, O_TAIL, 32), (2726, O_CZ, 384))


def _move_cols(w, moves, width):
    out = None
    for src, dst, n in moves:
        piece = jnp.pad(w[..., src:src + n], [(0, 0)] * (w.ndim - 1) + [(dst, width - dst - n)])
        out = piece if out is None else out + piece
    return out


def _perm_cols(w):
    return _move_cols(w, COL_MOVES, NCOL)


def _unperm_cols(g):
    return _move_cols(g, [(dst, src, n) for src, dst, n in COL_MOVES], IN_COLS)


def _wq_layout(wt):
    return jnp.pad(wt.reshape(MLA_HEADS, QK_NOPE + QK_ROPE, Q_LORA), ((0, 0), (0, 32), (0, 0))).reshape(MLA_HEADS * LANE, Q_LORA)


def _wq_unlayout(g):
    return g.reshape(MLA_HEADS, LANE, Q_LORA)[:, :QK_NOPE + QK_ROPE].reshape(MLA_HEADS * (QK_NOPE + QK_ROPE), Q_LORA)


def _wkv_layout(wt):
    t = wt.reshape(MLA_HEADS, 2, 64, KV_LORA).transpose(1, 0, 2, 3)
    return jnp.pad(t, ((0, 0), (0, 0), (0, 64), (0, 0))).reshape(2 * MLA_HEADS * LANE, KV_LORA)


def _wkv_unlayout(g):
    t = g.reshape(2, MLA_HEADS, LANE, KV_LORA)[:, :, :64]
    return t.transpose(1, 0, 2, 3).reshape(MLA_HEADS * LANE, KV_LORA)


def _rope_tables(positions):
    inv_freq = ROPE_BASE ** (-jnp.arange(0, QK_ROPE, 2, dtype=F32) / QK_ROPE)
    ang = positions.astype(F32)[:, None] * inv_freq
    cos, sin = jnp.cos(ang), jnp.sin(ang)
    s = positions.shape[0]
    one, zero = jnp.ones((s, ROPE_LANE), F32), jnp.zeros((s, ROPE_LANE), F32)
    cos_t = jnp.concatenate([one, cos, cos, one[:, :32]], axis=1)
    sin_t = jnp.concatenate([zero, -sin, sin, zero[:, :32]], axis=1)
    return cos_t, sin_t


def _ssd_scalars(dt_bias, a_log, d_skip):
    return jnp.pad(jnp.stack([dt_bias, a_log, d_skip]), ((0, 5), (DT_LANE, LANE - DT_LANE - SSD_HEADS)))


def _layer_fwd(x, lw, cos, sin, dep=None, late=None):
    proj, h, rstd = _inproj_fwd(x, lw["norm_g"], lw["w_in"], dep)
    ya = _conva_fwd(proj, lw["conv_a_w"])
    xbc = _sconv_fwd(proj, lw["ssd_conv_w"], lw["ssd_conv_b"])
    y_ssd, states = _ssd_fwd(xbc, proj, lw["sc"])
    if late is not None:
        lw = {**lw, **late(ya, y_ssd)}
    q, k, v, qn, kvn, rq, rkv = _mla_prep_fwd(proj, lw["gq"], lw["gkv"], lw["wq"], lw["wkv"], cos, sin)
    o, lse = _attn_fwd(q, k, v)
    x_out, y = _outproj_fwd(x, proj, ya, y_ssd, o, lw["g_ssd"], lw["w_out"])
    saved = dict(x=x, proj=proj, h=h, rstd=rstd, xbc=xbc, y_ssd=y_ssd, states=states, q=q, k=k, v=v, qn=qn, kvn=kvn,
                 rq=rq, rkv=rkv, o=o, lse=lse, y=y)
    return x_out, saved, lw


def _layer_bwd(dout, lw, sv, cos, sin, rs=None, begin_early=None):
    tok = lambda: None if rs is None else rs["h"]["token"]
    dya, dys, dsz, d_o, dcz, dg_ssd, dw_out = _outproj_bwd(dout, sv["y"], lw["w_out"], sv["proj"], sv["y_ssd"], sv["o"],
                                                            lw["g_ssd"], tok())
    if rs is not None:
        rs = _rs_add_mine(rs, [dya])
    dq, dk, dv = _attn_bwd(sv["q"], sv["k"], sv["v"], sv["o"], d_o, sv["lse"], tok())
    dxbc, dtail_s, dsc = _ssd_bwd(sv["xbc"], sv["proj"], lw["sc"], sv["states"], dys, tok())
    da4, dw_conva = _conva_bwd(sv["proj"], lw["conv_a_w"], dya, tok())
    if rs is not None:
        rs = _rs_add_chips(rs, [dq, dxbc, da4])
    du, dw_sconv, db_sconv = _sconv_bwd(sv["proj"], lw["ssd_conv_w"], lw["ssd_conv_b"], dxbc, tok())
    dcqa, dckv, dtail_m, dwq, dwkv, dgq, dgkv = _mla_prep_bwd(
        dq, dk, dv, sv["proj"], sv["qn"], sv["kvn"], sv["rq"], sv["rkv"], lw["gq"], lw["gkv"], lw["wq"], lw["wkv"], cos, sin)
    early = None if begin_early is None else begin_early(dw_out, dwq, dwkv)
    etok = lambda: None if early is None else early["h"]["token"]
    dproj, dx, dg = _inproj_bwd(da4, dsz, du, dcqa, dckv, dcz, dtail_s, dtail_m, lw["w_in"], sv["x"], sv["rstd"],
                                lw["norm_g"], dout, etok())
    reduced = None if rs is None else _rs_end(rs, [du, dcqa, dx])
    if early is not None:
        early = _rs_add_mine(early, [dx])
    dw_in = _dwin(sv["h"], dproj, etok())
    if early is not None:
        early = _rs_add_chips(early, [dw_in])
    grads = dict(norm_g=dg, w_in=dw_in, conv_a_w=dw_conva, ssd_conv_w=dw_sconv, ssd_conv_b=db_sconv, sc=dsc,
                 g_ssd=dg_ssd, gq=dgq, wq=dwq, gkv=dgkv, wkv=dwkv, w_out=dw_out)
    return dx, grads, reduced, early


ANY = pl.BlockSpec(memory_space=pl.ANY)
N_CHIPS = 4
N_DEV = 8


def _place():
    return lax.axis_index("x"), lax.axis_index("y"), lax.axis_index("c")


HBM_SPEC = pl.BlockSpec(memory_space=pltpu.HBM)
SEM_SPEC = pl.BlockSpec(memory_space=pltpu.SEMAPHORE)
PAYLOAD = jnp.bfloat16


def _hbm(a):
    return pltpu.with_memory_space_constraint(a, pltpu.HBM)


def _run_plan(plan, srcs, lands, send_sems, recv_sems, start, wait):
    copies = plan(srcs, lands)
    if start:
        for i, (src, dst, _, to) in enumerate(copies):
            pltpu.make_async_remote_copy(src_ref=src, dst_ref=dst, send_sem=send_sems.at[i], recv_sem=recv_sems.at[i],
                                         device_id=to, device_id_type=MESH_T).start()
    if wait:
        for i, (src, _, arrives, to) in enumerate(copies):
            cp = pltpu.make_async_remote_copy(src_ref=src, dst_ref=arrives, send_sem=send_sems.at[i],
                                              recv_sem=recv_sems.at[i], device_id=to, device_id_type=MESH_T)
            cp.wait_send()
            cp.wait_recv()


def _exchange_start(name, plan, n_copies, srcs, land_shapes, deps):
    ns, nl = len(srcs), len(land_shapes)
    n_in = ns + nl + len(deps)

    def body(*refs):
        send_sems, recv_sems = refs[n_in], refs[n_in + 1]
        token = refs[-1]
        _run_plan(plan, refs[:ns], refs[ns:ns + nl], send_sems, recv_sems, True, False)
        token[...] = jnp.zeros_like(token)

    thru = [pltpu.HBM(a.shape, a.dtype) for a in srcs] + [pltpu.HBM(a.shape, a.dtype) for a in land_shapes]
    outs = pl.pallas_call(
        body, name=name,
        out_shape=(pltpu.SemaphoreType.DMA((n_copies,)), pltpu.SemaphoreType.DMA((n_copies,)), *thru, _sds((8, LANE))),
        in_specs=[HBM_SPEC] * (ns + nl) + [ANY] * len(deps),
        out_specs=(SEM_SPEC, SEM_SPEC, *[HBM_SPEC] * (ns + nl), pl.BlockSpec(memory_space=pltpu.VMEM)),
        input_output_aliases={i: 2 + i for i in range(ns + nl)},
        compiler_params=pltpu.CompilerParams(has_side_effects=pltpu.SideEffectType.DATAFLOW_SIDE_EFFECTING),
    )(*[_hbm(a) for a in srcs], *[_hbm(lax.empty(a.shape, a.dtype)) for a in land_shapes], *deps)
    return (outs[0], outs[1]), list(outs[2:2 + ns]), list(outs[2 + ns:2 + ns + nl]), outs[-1]


def _exchange_wait(name, plan, sems, srcs, lands, after):
    ns, nl = len(srcs), len(lands)

    def body(*refs):
        _run_plan(plan, refs[:ns], refs[ns:ns + nl], refs[ns + nl], refs[ns + nl + 1], False, True)

    outs = pl.pallas_call(
        body, name=name,
        out_shape=[pltpu.HBM(a.shape, a.dtype) for a in list(srcs) + list(lands)],
        in_specs=[HBM_SPEC] * (ns + nl) + [SEM_SPEC, SEM_SPEC] + [ANY] * len(after), out_specs=[HBM_SPEC] * (ns + nl),
        input_output_aliases={i: i for i in range(ns + nl)},
        compiler_params=pltpu.CompilerParams(has_side_effects=pltpu.SideEffectType.DATAFLOW_SIDE_EFFECTING),
    )(*srcs, *lands, sems[0], sems[1], *after)
    return list(outs[:ns]), list(outs[ns:])


def _xchg_begin(name, plan, n_copies, srcs, land_shapes, deps=()):
    sems, srcs_t, lands_t, token = _exchange_start(name + "_start", plan, n_copies, srcs, land_shapes, list(deps))
    return dict(name=name, plan=plan, sems=sems, srcs=srcs_t, lands=lands_t, token=token)


def _xchg_end(h, after):
    return _exchange_wait(h["name"] + "_wait", h["plan"], h["sems"], h["srcs"], h["lands"], after)


def _other_chips():
    x, y, c = _place()
    return [(1 - x, y), (x, 1 - y), (1 - x, 1 - y)]


def _gather_plan(srcs, lands):
    x, y, c = _place()
    me = 2 * x + y
    return [(srcs[a], lands[a].at[me], lands[a].at[2 * cx + cy], (cx, cy, c))
            for (cx, cy) in _other_chips() for a in range(len(srcs))]


def _gather_begin(shards, tag, deps=()):
    shapes = [_sds((N_CHIPS,) + a.shape, a.dtype) for a in shards]
    return _xchg_begin(f"gather_{tag}", _gather_plan, 3 * len(shards), shards, shapes, deps)


def _gather_end(h, after):
    shards, lands = _xchg_end(h, after)
    me = 2 * lax.axis_index("x") + lax.axis_index("y")
    return [lax.dynamic_update_index_in_dim(g, s, me, 0) for g, s in zip(lands, shards)]


def _gather_half_plan(srcs, lands):
    x, y, c = _place()
    me = 2 * x + y
    out = []
    for (cx, cy) in _other_chips():
        out.append((srcs[0].at[c], lands[0].at[me, c], lands[0].at[2 * cx + cy, c], (cx, cy, c)))
        out += [(srcs[a], lands[a].at[me], lands[a].at[2 * cx + cy], (cx, cy, c)) for a in range(1, len(srcs))]
    return out


def _forward_plan(bufs, _):
    x, y, c = _place()
    return [(bufs[0].at[2 * cx + cy, c], bufs[0].at[2 * cx + cy, c], bufs[0].at[2 * cx + cy, 1 - c], (x, y, 1 - c))
            for (cx, cy) in _other_chips()]


def _swap_plan(srcs, lands):
    x, y, c = _place()
    return [(srcs[a].at[:, 1 - c], lands[a], lands[a], (x, y, 1 - c)) for a in range(len(srcs))]


def _chips_plan(srcs, lands):
    x, y, c = _place()
    me = 2 * x + y
    return [(srcs[a].at[2 * cx + cy], lands[a].at[me], lands[a].at[2 * cx + cy], (cx, cy, c))
            for (cx, cy) in _other_chips() for a in range(len(srcs))]


def _share_plan(srcs, lands):
    x, y, c = _place()
    return [(srcs[a], lands[a].at[c], lands[a].at[1 - c], (x, y, 1 - c)) for a in range(len(srcs))]


def _allreduce_small(slab, dep=None):
    r = slab.shape[0]

    def body(s_ref, o_ref, gath, send_sems, recv_sems):
        x, y, c = _place()
        me = 4 * x + 2 * y + c
        gath[me] = s_ref[...]
        cps = []
        for rel in range(1, N_DEV):
            px = 1 - x if rel & 4 else x
            py = 1 - y if rel & 2 else y
            pc = 1 - c if rel & 1 else c
            cp = pltpu.make_async_remote_copy(src_ref=s_ref, dst_ref=gath.at[me], send_sem=send_sems.at[rel - 1],
                                              recv_sem=recv_sems.at[rel - 1], device_id=(px, py, pc), device_id_type=MESH_T)
            cp.start()
            cps.append(cp)
        for cp in cps:
            cp.wait()
        acc = gath[0]
        for d in range(1, N_DEV):
            acc = acc + gath[d]
        o_ref[...] = acc

    vm = pl.BlockSpec(memory_space=pltpu.VMEM)
    return _call_after(
        dep, body, (slab,), name="allreduce_small", in_specs=[vm], out_specs=vm, out_shape=_sds((r, LANE)),
        scratch_shapes=[pltpu.VMEM((N_DEV, r, LANE), F32), pltpu.SemaphoreType.DMA((N_DEV - 1,)),
                        pltpu.SemaphoreType.DMA((N_DEV - 1,))],
    )


def _add_mine(g4s, recvs, half):
    n = len(g4s)

    def body(h_ref, *refs):
        for g_ref, r_ref, o_ref in zip(refs[:n], refs[n:2 * n], refs[2 * n:]):
            o_ref[0] = (g_ref[0, 0] + r_ref[0]).astype(o_ref.dtype)

    dims = [g.shape[2:] for g in g4s]
    return pl.pallas_call(
        body, name="add_mine",
        grid_spec=pltpu.PrefetchScalarGridSpec(
            num_scalar_prefetch=1, grid=(N_CHIPS,),
            in_specs=[pl.BlockSpec((1, 1) + d, lambda j, h: (j, h[0], 0, 0)) for d in dims]
            + [pl.BlockSpec((1,) + d, lambda j, h: (j, 0, 0)) for d in dims],
            out_specs=[pl.BlockSpec((1,) + d, lambda j, h: (j, 0, 0)) for d in dims]),
        out_shape=[_sds((N_CHIPS,) + d, PAYLOAD) for d in dims],
        compiler_params=_params(("parallel",)),
    )(half, *g4s, *recvs)


def _add_chips(es, ps, me):
    n = len(es)

    def body(m_ref, *refs):
        for e_ref, p_ref, o_ref in zip(refs[:n], refs[n:2 * n], refs[2 * n:]):
            own = p_ref[0].astype(F32)
            acc = None
            for s in range(N_CHIPS):
                t = jnp.where(m_ref[0] == s, own, e_ref[s].astype(F32))
                acc = t if acc is None else acc + t
            o_ref[...] = acc

    dims = [e.shape[1:] for e in es]
    return pl.pallas_call(
        body, name="add_chips",
        grid_spec=pltpu.PrefetchScalarGridSpec(
            num_scalar_prefetch=1, grid=(1,),
            in_specs=[pl.BlockSpec((N_CHIPS,) + d, lambda i, m: (0, 0, 0)) for d in dims]
            + [pl.BlockSpec((1,) + d, lambda i, m: (m[0], 0, 0)) for d in dims],
            out_specs=[pl.BlockSpec(d, lambda i, m: (0, 0)) for d in dims]),
        out_shape=[_sds(d) for d in dims],
        compiler_params=_params(("arbitrary",)),
    )(me, *es, *ps)


def _rs_begin(gs, tag, deps=()):
    g4 = [g.reshape(N_CHIPS, 2, g.shape[0] // (2 * N_CHIPS), g.shape[1]) for g in gs]
    h = _xchg_begin(f"rs_swap_{tag}", _swap_plan, len(gs), g4, [_sds((N_CHIPS,) + g.shape[2:]) for g in g4], deps)
    return dict(h=h, tag=tag, shapes=[g.shape for g in gs])


def _rs_add_mine(st, after):
    g4, recv = _xchg_end(st["h"], after)
    half = jnp.reshape(lax.axis_index("c"), (1,)).astype(jnp.int32)
    ps = _add_mine(g4, recv, half)
    st["h"] = _xchg_begin(f"rs_chips_{st['tag']}", _chips_plan, 3 * len(ps), ps, [_sds(p.shape, p.dtype) for p in ps])
    return st


def _rs_add_chips(st, after):
    ps, es = _xchg_end(st["h"], after)
    me = jnp.reshape(2 * lax.axis_index("x") + lax.axis_index("y"), (1,)).astype(jnp.int32)
    fs = _add_chips(es, ps, me)
    st["h"] = _xchg_begin(f"rs_share_{st['tag']}", _share_plan, len(fs), fs, [_sds((2,) + f.shape) for f in fs])
    return st


def _rs_end(st, after):
    fs, ss = _xchg_end(st["h"], after)
    c = lax.axis_index("c")
    return [lax.dynamic_update_index_in_dim(s, f, c, 0).reshape(shp[0] // N_CHIPS, shp[1])
            for s, f, shp in zip(ss, fs, st["shapes"])]


WEIGHTS = ["norm_g", "w_in", "conv_a_w", "ssd_conv_w", "ssd_conv_b", "ssd_dt_bias", "ssd_a_log", "ssd_d", "ssd_norm_g",
           "mla_q_norm_g", "w_qb", "mla_kv_norm_g", "w_kvb", "w_out", "final_norm_g"]
BIG = ["w_in", "w_qb", "w_kvb", "w_out"]
SLAB_ROWS = 128


def _to_slab(parts, rows):
    flat = jnp.concatenate([p.reshape(-1) for p in parts])
    return jnp.pad(flat, (0, rows * LANE - flat.shape[0])).reshape(rows, LANE)


def _from_slab(slab, shapes):
    flat = slab.reshape(-1)
    out, off = [], 0
    for shp in shapes:
        n = int(np.prod(shp))
        out.append(flat[off:off + n].reshape(shp))
        off += n
    return out


def kernel(x, positions, norm_g, w_in, conv_a_w, ssd_conv_w, ssd_conv_b, ssd_dt_bias, ssd_a_log, ssd_d, ssd_norm_g, mla_q_norm_g, w_qb, mla_kv_norm_g, w_kvb, w_out, final_norm_g, loss_target, m_norm_g, m_w_in, m_conv_a_w, m_ssd_conv_w, m_ssd_conv_b, m_ssd_dt_bias, m_ssd_a_log, m_ssd_d, m_ssd_norm_g, m_mla_q_norm_g, m_w_qb, m_mla_kv_norm_g, m_w_kvb, m_w_out, m_final_norm_g, v_norm_g, v_w_in, v_conv_a_w, v_ssd_conv_w, v_ssd_conv_b, v_ssd_dt_bias, v_ssd_a_log, v_ssd_d, v_ssd_norm_g, v_mla_q_norm_g, v_w_qb, v_mla_kv_norm_g, v_w_kvb, v_w_out, v_final_norm_g):
    w = dict(norm_g=norm_g, w_in=w_in, conv_a_w=conv_a_w, ssd_conv_w=ssd_conv_w, ssd_conv_b=ssd_conv_b,
             ssd_dt_bias=ssd_dt_bias, ssd_a_log=ssd_a_log, ssd_d=ssd_d, ssd_norm_g=ssd_norm_g, mla_q_norm_g=mla_q_norm_g,
             w_qb=w_qb, mla_kv_norm_g=mla_kv_norm_g, w_kvb=w_kvb, w_out=w_out, final_norm_g=final_norm_g)
    mom = dict(norm_g=m_norm_g, w_in=m_w_in, conv_a_w=m_conv_a_w, ssd_conv_w=m_ssd_conv_w, ssd_conv_b=m_ssd_conv_b,
               ssd_dt_bias=m_ssd_dt_bias, ssd_a_log=m_ssd_a_log, ssd_d=m_ssd_d, ssd_norm_g=m_ssd_norm_g,
               mla_q_norm_g=m_mla_q_norm_g, w_qb=m_w_qb, mla_kv_norm_g=m_mla_kv_norm_g, w_kvb=m_w_kvb, w_out=m_w_out,
               final_norm_g=m_final_norm_g)
    var = dict(norm_g=v_norm_g, w_in=v_w_in, conv_a_w=v_conv_a_w, ssd_conv_w=v_ssd_conv_w, ssd_conv_b=v_ssd_conv_b,
               ssd_dt_bias=v_ssd_dt_bias, ssd_a_log=v_ssd_a_log, ssd_d=v_ssd_d, ssd_norm_g=v_ssd_norm_g,
               mla_q_norm_g=v_mla_q_norm_g, w_qb=v_w_qb, mla_kv_norm_g=v_mla_kv_norm_g, w_kvb=v_w_kvb, w_out=v_w_out,
               final_norm_g=v_final_norm_g)
    chip = 2 * lax.axis_index("x") + lax.axis_index("y")

    def early_shard(l, zero):
        pack = jnp.pad(conv_a_w[l], ((0, 5), (0, 192))) + jnp.pad(ssd_conv_w[l], ((3, 1), (0, 32)))
        return [(_perm_cols(w_in[l]) + zero).astype(MXU), pack + zero]

    def late_shard(l, zero):
        return [(w_out[l] + zero).astype(MXU), (w_qb[l].T + zero).astype(MXU), (w_kvb[l].T + zero).astype(MXU)]

    def early_weights(l, gathered):
        g_in, g_conv = gathered
        return dict(
            norm_g=norm_g[l][None], w_in=g_in.reshape(D_MODEL, NCOL),
            conv_a_w=jnp.concatenate([g_conv[j, 0:3, 0:64] for j in range(N_CHIPS)], axis=1),
            ssd_conv_w=jnp.concatenate([g_conv[j, 3:7, 0:224] for j in range(N_CHIPS)], axis=1),
            ssd_conv_b=ssd_conv_b[l][None], sc=_ssd_scalars(ssd_dt_bias[l], ssd_a_log[l], ssd_d[l]),
            g_ssd=ssd_norm_g[l][None], gq=mla_q_norm_g[l][None], gkv=mla_kv_norm_g[l][None])

    def late_weights(gathered):
        g_out, g_qb, g_kvb = gathered
        return dict(wq=_wq_layout(g_qb.reshape(MLA_HEADS * 96, Q_LORA)), wkv=_wkv_layout(g_kvb.reshape(MLA_HEADS * LANE, KV_LORA)),
                    w_out=g_out.reshape(D_MODEL, D_MODEL))

    def late_grads(dw_out, dwq, dwkv):
        wq = jnp.pad(_wq_unlayout(dwq).reshape(N_CHIPS, 144, Q_LORA), ((0, 0), (0, 16), (0, 0)))
        return [dw_out, wq.reshape(N_CHIPS * 160, Q_LORA), _wkv_unlayout(dwkv)]

    def large_grads(g):
        return [g["w_in"]] + late_grads(g["w_out"], g["wq"], g["wkv"])

    w_in0, pack0 = early_shard(0, 0.0)
    half = w_in0.shape[0] // 2
    gather_a0 = _xchg_begin("gather_a0", _gather_half_plan, 6, [w_in0.reshape(2, half, NCOL), pack0],
                            [_sds((N_CHIPS, 2, half, NCOL), MXU), _sds((N_CHIPS,) + pack0.shape)])
    zero = gather_a0["token"][0, 0]
    cos, sin = _rope_tables(positions[0] + zero.astype(jnp.int32))
    late0, shards1 = late_shard(0, zero), early_shard(1, zero) + late_shard(1, zero)
    mine0, (g_in0, g_conv0) = _xchg_end(gather_a0, [cos, sin] + late0 + shards1)
    forward_a0 = _xchg_begin("forward_a0", _forward_plan, 3, [g_in0], [])
    gather_b0 = _gather_begin(late0, "b0", [forward_a0["token"]])
    gather_1 = _gather_begin(shards1, "1", [gather_b0["token"]])
    (g_in0,), _ = _xchg_end(forward_a0, [gather_1["token"]])
    lw0 = early_weights(0, [lax.dynamic_update_index_in_dim(g, s_, chip, 0) for g, s_ in zip((g_in0, g_conv0), mine0)])
    x1, sv0, lw0 = _layer_fwd(x[0], lw0, cos, sin, gather_1["token"],
                              lambda ya, y_ssd: late_weights(_gather_end(gather_b0, [ya, y_ssd])))
    g1 = _gather_end(gather_1, [x1])
    x2, sv1, lw1 = _layer_fwd(x1, {**early_weights(1, g1[:2]), **late_weights(g1[2:])}, cos, sin)
    dx, dgf, loss = _loss_head(x2, final_norm_g[None], loss_target[0])

    dx, lg1, _, _ = _layer_bwd(dx, lw1, sv1, cos, sin)
    grad_x, lg0, red1, rs0_late = _layer_bwd(dx, lw0, sv0, cos, sin, _rs_begin(large_grads(lg1), 1),
                                             lambda *g: _rs_begin(late_grads(*g), "0l"))
    rs0 = _rs_begin([lg0["w_in"]], 0, [rs0_late["h"]["token"]])
    lg = [lg0, lg1]
    grad = {}

    small_names = ["norm_g", "conv_a_w", "ssd_conv_w", "ssd_conv_b", "sc", "g_ssd", "gq", "gkv"]
    parts = [loss[0, 0:1], dgf]
    for nm in small_names:
        parts += [lg[l][nm][:3, DT_LANE:DT_LANE + SSD_HEADS] if nm == "sc" else lg[l][nm] for l in range(DEPTH)]
    shapes = [(1,), (D_MODEL,)] + [(DEPTH,) + shp for shp in ((D_MODEL,), (3, D_CONV_A), (4, N_XBC), (N_XBC,), (3, SSD_HEADS),
                                                              (D_SSD,), (Q_LORA,), (KV_LORA,))]
    red_slab = _allreduce_small(_to_slab(parts, SLAB_ROWS), rs0["h"]["token"])
    rs0 = _rs_add_mine(rs0, [red_slab])
    red = _from_slab(red_slab + rs0["h"]["token"][0, 0], shapes)
    loss_out = red[0][0]
    grad["final_norm_g"] = red[1]
    grad["norm_g"], conv_a_full, sconv_full, grad["ssd_conv_b"], sc_grads = red[2:7]
    grad["ssd_norm_g"], grad["mla_q_norm_g"], grad["mla_kv_norm_g"] = red[7:10]
    grad["conv_a_w"] = lax.dynamic_slice_in_dim(conv_a_full, chip * 64, 64, axis=2)
    grad["ssd_conv_w"] = lax.dynamic_slice_in_dim(sconv_full, chip * 224, 224, axis=2)
    grad["ssd_dt_bias"], grad["ssd_a_log"], grad["ssd_d"] = sc_grads[:, 0], sc_grads[:, 1], sc_grads[:, 2]

    delta, new_m, new_v = {}, {}, {}
    small = [nm for nm in WEIGHTS if nm not in BIG]
    row2 = lambda a: a[None] if a.ndim == 1 else a
    small_out = _adamw(*[[row2(a[nm]) for nm in small] for a in (w, grad, mom, var)], whole=True)
    for nm, (dv, mv, vv) in zip(small, small_out):
        delta[nm], new_m[nm], new_v[nm] = [a.reshape(w[nm].shape) for a in (dv, mv, vv)]

    r_out, r_qb, r_kvb = [jnp.stack([a, b]) for a, b in zip(_rs_end(rs0_late, [red_slab]), red1[1:])]
    late = [nm for nm in BIG if nm != "w_in"]
    view = {nm: (lambda a: a) if nm == "w_out" else (lambda a: jnp.swapaxes(a, 1, 2)) for nm in late}
    late_g = [dict(w_out=r_out, w_qb=r_qb[:, :144], w_kvb=r_kvb)[nm] for nm in late]
    late_out = _adamw(*[[view[nm](a[nm]) for nm in late] for a in (w,)], late_g,
                      *[[view[nm](a[nm]) for nm in late] for a in (mom, var)], whole=False)
    for nm, gv, (dv, mv, vv) in zip(late, late_g, late_out):
        grad[nm], delta[nm], new_m[nm], new_v[nm] = [view[nm](a) for a in (gv, dv, mv, vv)]
    g_in1 = _unperm_cols(red1[0])
    shadow_work = [a for row in small_out + late_out for a in row] + [grad[nm] for nm in small] + [g_in1]
    r_in0, = _rs_end(_rs_add_chips(rs0, shadow_work), [])
    to_cols, from_cols = (lambda a: jnp.transpose(a, (2, 0, 1))), (lambda a: jnp.transpose(a, (1, 2, 0)))
    grad["w_in"], delta["w_in"], new_m["w_in"], new_v["w_in"] = [from_cols(a) for a in _adamw_cols(
        to_cols(w["w_in"]), [_unperm_cols(r_in0), g_in1], to_cols(mom["w_in"]), to_cols(var["w_in"]))]

    return (loss_out, grad_x[None], *[grad[nm] for nm in WEIGHTS], *[delta[nm] for nm in WEIGHTS],
            *[new_m[nm] for nm in WEIGHTS], *[new_v[nm] for nm in WEIGHTS])
```

```python
import functools
import math

import numpy as np
import jax
import jax.numpy as jnp
from jax import lax
from jax.experimental import pallas as pl
from jax.experimental.pallas import tpu as pltpu

F32 = jnp.float32
MXU = jnp.bfloat16

D_MODEL = 1024
DEPTH = 2
D_CONV_A = 256
D_SSD = 384
SSD_HEADS = 6
SSD_BC = 256
SSD_CHUNK = 128
SSD_CHUNKS_PER_STEP = 4
SSD_NORM_EPS = 1e-5
MLA_HEADS = 6
Q_LORA = 256
KV_LORA = 128
QK_NOPE = 64
QK_ROPE = 32
V_DIM = 64
D_MLA = 384
ROPE_BASE = 10000.0
NORM_EPS = 1e-6
IN_COLS = 3110
LANE = 128

O_AH, O_AB, O_AC, O_AZ = 0, 256, 512, 768
O_XBC = 1024
O_SZ = 1920
O_CQA = 2304
O_CKV = 2560
O_CZ = 2688
O_TAIL = 3072
NCOL = 3200
N_XBC = D_SSD + 2 * SSD_BC
DT_LANE = 32
ROPE_LANE = 64

ADAM_LR, ADAM_B1, ADAM_B2, ADAM_EPS, ADAM_WD, ADAM_STEP = 0.001, 0.9, 0.999, 1e-08, 0.01, 10

VMEM_LIMIT = 56 * 1024 * 1024
MESH_T = pl.DeviceIdType.MESH


def _dot(a, b):
    return jnp.dot(a.astype(MXU), b.astype(MXU), preferred_element_type=F32)


def _dot_nt(a, b):
    return lax.dot_general(a.astype(MXU), b.astype(MXU), (((1,), (1,)), ((), ())), preferred_element_type=F32)


def _dot_tn(a, b):
    return lax.dot_general(a.astype(MXU), b.astype(MXU), (((0,), (0,)), ((), ())), preferred_element_type=F32)


def _dot_hi(a, b):
    return jnp.dot(a, b, precision=lax.Precision.HIGHEST, preferred_element_type=F32)


def _dot_hi_tn(a, b):
    return lax.dot_general(a, b, (((0,), (0,)), ((), ())), precision=lax.Precision.HIGHEST, preferred_element_type=F32)


def _sigmoid(z):
    return 1.0 / (1.0 + jnp.exp(-z))


def _silu(z):
    return z * _sigmoid(z)


def _dsilu(z):
    s = _sigmoid(z)
    return s * (1.0 + z * (1.0 - s))


def _softplus(z):
    e = jnp.exp(-jnp.abs(z))
    return jnp.maximum(z, 0.0) + jnp.where(e < 1e-3, e * (1.0 - 0.5 * e), jnp.log(1.0 + e))


def _iota(shape, dim):
    return lax.broadcasted_iota(jnp.int32, shape, dim)


def _shift_down(u, k):
    if k == 0:
        return u
    return jnp.where(_iota(u.shape, 0) >= k, pltpu.roll(u, k, 0), 0.0)


def _shift_up(u, k):
    if k == 0:
        return u
    n = u.shape[0]
    return jnp.where(_iota(u.shape, 0) < n - k, pltpu.roll(u, n - k, 0), 0.0)


def _rope_swap(t):
    lane = _iota(t.shape, 1)
    lo = (lane >= ROPE_LANE) & (lane < ROPE_LANE + 16)
    hi = (lane >= ROPE_LANE + 16) & (lane < ROPE_LANE + 32)
    return jnp.where(lo, pltpu.roll(t, LANE - 16, 1), jnp.where(hi, pltpu.roll(t, 16, 1), 0.0))


def _params(sem=None):
    return pltpu.CompilerParams(dimension_semantics=sem, vmem_limit_bytes=VMEM_LIMIT)


def _full(shape):
    nd = len(shape)
    return pl.BlockSpec(shape, lambda *_: (0,) * nd)


def _sds(shape, dtype=F32):
    return jax.ShapeDtypeStruct(shape, dtype)


def _tile(s):
    return min(512, s)


def _row(ts, w):
    return pl.BlockSpec((ts, w), lambda i: (i, 0))


def _gate_cols(ts, off):
    return pl.BlockSpec((ts, D_SSD), lambda i, _o=off // D_SSD: (i, _o))


def _col(s, off):
    return pl.BlockSpec((s, LANE), lambda j, _o=off // LANE: (0, _o + j))


def _call_after(dep, body, args, *, in_specs, **kw):
    if dep is None:
        return pl.pallas_call(body, in_specs=in_specs, **kw)(*args)
    n = len(args)

    def body_dep(*refs):
        body(*refs[:n], *refs[n + 1:])

    return pl.pallas_call(body_dep, in_specs=list(in_specs) + [pl.BlockSpec(memory_space=pl.ANY)], **kw)(*args, dep)


def _rms(c, g):
    r = lax.rsqrt(jnp.mean(c * c, axis=-1, keepdims=True) + NORM_EPS)
    return c * r * g, r


def _rms_bwd(dn, c, r, g):
    ch = c * r
    dch = dn * g
    dc = r * (dch - ch * jnp.mean(dch * ch, axis=-1, keepdims=True))
    return dc, jnp.sum(dn * ch, axis=0, keepdims=True)


def _inproj_fwd(x, g, w, dep=None):
    s = x.shape[0]
    ts = _tile(s)

    def body(x_ref, g_ref, w_ref, proj_ref, h_ref, r_ref):
        hn, r = _rms(x_ref[...], g_ref[...])
        h = hn.astype(MXU)
        h_ref[...] = h
        r_ref[...] = r
        proj_ref[...] = jnp.dot(h, w_ref[...], preferred_element_type=F32)

    return _call_after(
        dep, body, (x, g, w), name="inproj_fwd", grid=(s // ts,),
        in_specs=[_row(ts, D_MODEL), _full((1, D_MODEL)), _full((D_MODEL, NCOL))],
        out_specs=[_row(ts, NCOL), _row(ts, D_MODEL), _row(ts, 1)],
        out_shape=[_sds((s, NCOL)), _sds((s, D_MODEL), MXU), _sds((s, 1))],
        compiler_params=_params(("parallel",)),
    )


def _conva_fwd(proj, w):
    s = proj.shape[0]

    def body(h_ref, b_ref, c_ref, z_ref, w_ref, y_ref):
        u = c_ref[...] * h_ref[...]
        wv = w_ref[...]
        cv = wv[2:3, :] * u + wv[1:2, :] * _shift_down(u, 1) + wv[0:1, :] * _shift_down(u, 2)
        y_ref[...] = b_ref[...] * cv * _silu(z_ref[...])

    return pl.pallas_call(
        body, name="conva_fwd", grid=(D_CONV_A // LANE,),
        in_specs=[_col(s, O_AH), _col(s, O_AB), _col(s, O_AC), _col(s, O_AZ), pl.BlockSpec((3, LANE), lambda j: (0, j))],
        out_specs=pl.BlockSpec((s, LANE), lambda j: (0, j)),
        out_shape=_sds((s, D_CONV_A)),
        compiler_params=_params(("parallel",)),
    )(proj, proj, proj, proj, w)


def _sconv_pre(u, wv, bv):
    return (wv[3:4, :] * u + wv[2:3, :] * _shift_down(u, 1) + wv[1:2, :] * _shift_down(u, 2)
            + wv[0:1, :] * _shift_down(u, 3) + bv)


def _sconv_fwd(proj, w, b):
    s = proj.shape[0]

    def body(u_ref, w_ref, b_ref, o_ref):
        o_ref[...] = _silu(_sconv_pre(u_ref[...], w_ref[...], b_ref[...]))

    return pl.pallas_call(
        body, name="sconv_fwd", grid=(N_XBC // LANE,),
        in_specs=[_col(s, O_XBC), pl.BlockSpec((4, LANE), lambda j: (0, j)), pl.BlockSpec((1, LANE), lambda j: (0, j))],
        out_specs=pl.BlockSpec((s, LANE), lambda j: (0, j)),
        out_shape=_sds((s, N_XBC)),
        compiler_params=_params(("parallel",)),
    )(proj, w, b)


def _ssd_chunk_common(tail, sc):
    l = SSD_CHUNK
    lane = _iota((l, LANE), 1)
    row = _iota((l, LANE), 0)
    tri = (row >= lane).astype(F32)
    a_row = -jnp.exp(sc[1:2, :])
    pre = tail + sc[0:1, :]
    dt = _softplus(pre)
    a_cs = _dot_hi(tri, dt * a_row)
    return lane, row, tri, a_row, pre, dt, a_cs, a_cs.T


def _pick_col(m, lane, k):
    return jnp.sum(jnp.where(lane == k, m, 0.0), axis=1, keepdims=True)


def _pick_row(m, row, k):
    return jnp.sum(jnp.where(row == k, m, 0.0), axis=0, keepdims=True)


def _ssd_fwd(xbc, proj, sc):
    s = xbc.shape[0]
    nc = s // SSD_CHUNK
    l = SSD_CHUNK
    cps = SSD_CHUNKS_PER_STEP

    def body(xbc_ref, tail_ref, sc_ref, y_ref, st_ref, state):
        @pl.when(pl.program_id(0) == 0)
        def _():
            state[...] = jnp.zeros_like(state)

        sc_v = sc_ref[...]
        lane1 = _iota((1, LANE), 1)
        rowp = _iota((LANE, 1), 0)
        d_row = sc_v[2:3, :]
        states = [state[j] for j in range(3)]
        for u in range(cps):
            r = slice(u * l, (u + 1) * l)
            lane, row, _, _, _, dt, a_cs, a_t = _ssd_chunk_common(tail_ref[r, :], sc_v)
            for j in range(3):
                st_ref[u, j] = states[j]
            for j in range(3):
                xpair = xbc_ref[r, LANE * j:LANE * (j + 1)]
                sp = states[j]
                ypair = jnp.zeros((l, LANE), F32)
                new_s = jnp.zeros((LANE, LANE), F32)
                decay = jnp.zeros((LANE, 1), F32)
                for half in range(2):
                    h = 2 * j + half
                    g = h // 3
                    hm = (lane < 64) if half == 0 else (lane >= 64)
                    hrow = (rowp < 64) if half == 0 else (rowp >= 64)
                    ac = _pick_col(a_cs, lane, DT_LANE + h)
                    ar = _pick_row(a_t, row, DT_LANE + h)
                    dtc = _pick_col(dt, lane, DT_LANE + h)
                    alast = jnp.sum(jnp.where(lane1 == l - 1, ar, 0.0), axis=1, keepdims=True)
                    dh = jnp.sum(jnp.where(lane1 == DT_LANE + h, d_row, 0.0), axis=1, keepdims=True)
                    xm = jnp.where(hm, xpair, 0.0)
                    xd = xm * dtc
                    bm = xbc_ref[r, D_SSD + LANE * g:D_SSD + LANE * (g + 1)]
                    cm = xbc_ref[r, D_SSD + SSD_BC + LANE * g:D_SSD + SSD_BC + LANE * (g + 1)]
                    lm = jnp.where(row >= lane, jnp.exp(jnp.minimum(ac - ar, 0.0)), 0.0)
                    y_diag = _dot(_dot_nt(cm, bm) * lm, xd)
                    y_off = jnp.where(hm, _dot_nt(cm, sp), 0.0) * jnp.exp(ac)
                    ypair = ypair + y_diag + y_off + xm * dh
                    new_s = new_s + _dot_tn(xd * jnp.exp(alast - ac), bm)
                    decay = jnp.where(hrow, jnp.exp(alast), decay)
                states[j] = sp * decay + new_s
                y_ref[r, LANE * j:LANE * (j + 1)] = ypair
        for j in range(3):
            state[j] = states[j]

    return pl.pallas_call(
        body, name="ssd_fwd", grid=(nc // cps,),
        in_specs=[pl.BlockSpec((cps * l, N_XBC), lambda c: (c, 0)),
                  pl.BlockSpec((cps * l, LANE), lambda c: (c, O_TAIL // LANE)), _full((8, LANE))],
        out_specs=[pl.BlockSpec((cps * l, D_SSD), lambda c: (c, 0)), pl.BlockSpec((cps, 3, LANE, LANE), lambda c: (c, 0, 0, 0))],
        out_shape=[_sds((s, D_SSD)), _sds((nc, 3, LANE, LANE))],
        scratch_shapes=[pltpu.VMEM((3, LANE, LANE), F32)],
        compiler_params=_params(("arbitrary",)),
    )(xbc, proj, sc)


def _mla_prep_fwd(proj, gq, gkv, wq, wkv, cos, sin):
    s = proj.shape[0]
    ts = _tile(s)
    nh = MLA_HEADS

    def body(cqa_ref, ckv_ref, tail_ref, gq_ref, gkv_ref, wq_ref, wkv_ref, cos_ref, sin_ref,
             q_ref, k_ref, v_ref, qn_ref, kvn_ref, rq_ref, rkv_ref):
        qn, rq = _rms(cqa_ref[...], gq_ref[...])
        kvn, rkv = _rms(ckv_ref[...], gkv_ref[...])
        qn = qn.astype(MXU)
        kvn = kvn.astype(MXU)
        qn_ref[...] = qn
        kvn_ref[...] = kvn
        rq_ref[...] = rq
        rkv_ref[...] = rkv
        q = _dot_nt(qn, wq_ref[...])
        kv = _dot_nt(kvn, wkv_ref[...])
        cosv = cos_ref[...]
        sinv = sin_ref[...]
        lane = _iota((ts, LANE), 1)
        rope_lanes = (lane >= ROPE_LANE) & (lane < ROPE_LANE + QK_ROPE)
        kr = jnp.where(rope_lanes, pltpu.roll(tail_ref[...], ROPE_LANE, 1), 0.0)
        kr = kr * cosv + _rope_swap(kr) * sinv
        for h in range(nh):
            qh = q[:, LANE * h:LANE * (h + 1)]
            q_ref[h] = ((qh * cosv + _rope_swap(qh) * sinv) * ATT_SCALE).astype(MXU)
            k_ref[h] = (kv[:, LANE * h:LANE * (h + 1)] + kr).astype(MXU)
            v_ref[h] = kv[:, LANE * (nh + h):LANE * (nh + h + 1)].astype(MXU)

    head = pl.BlockSpec((nh, ts, LANE), lambda i: (0, i, 0))
    return pl.pallas_call(
        body, name="mla_prep_fwd", grid=(s // ts,),
        in_specs=[pl.BlockSpec((ts, Q_LORA), lambda i: (i, O_CQA // Q_LORA)),
                  pl.BlockSpec((ts, KV_LORA), lambda i: (i, O_CKV // KV_LORA)),
                  pl.BlockSpec((ts, LANE), lambda i: (i, O_TAIL // LANE)),
                  _full((1, Q_LORA)), _full((1, KV_LORA)), _full((nh * LANE, Q_LORA)), _full((2 * nh * LANE, KV_LORA)),
                  _row(ts, LANE), _row(ts, LANE)],
        out_specs=[head, head, head, _row(ts, Q_LORA), _row(ts, KV_LORA), _row(ts, 1), _row(ts, 1)],
        out_shape=[_sds((nh, s, LANE), MXU)] * 3 + [_sds((s, Q_LORA), MXU), _sds((s, KV_LORA), MXU), _sds((s, 1)), _sds((s, 1))],
        compiler_params=_params(("parallel",)),
    )(proj, proj, proj, gq, gkv, wq, wkv, cos, sin)


ATT_SCALE = (QK_NOPE + QK_ROPE) ** -0.5
NEG = -1e30


def _att_tile(s, most):
    return min(most, s // 2)


ATT_FWD_TILE = 1024
ATT_BWD_TILE = 512


def _attn_fwd(q, k, v):
    nh, s, _ = q.shape
    tq = _att_tile(s, ATT_FWD_TILE)
    nq = s // tq

    def body(q_ref, k_ref, v_ref, o_ref, lse_ref):
        i = pl.program_id(1)
        rowi = _iota((tq, tq), 0)
        coli = _iota((tq, tq), 1)
        zero = (jnp.full((tq, 1), NEG, F32), jnp.zeros((tq, 1), F32), jnp.zeros((tq, LANE), F32))
        state = [zero, zero]
        done = [zero, zero]
        for t in range(nq + 1):
            first = t <= i
            qblk = jnp.where(first, i, nq - 1 - i)
            kblk = jnp.where(first, t, t - i - 1)
            qoff = pl.multiple_of(qblk * tq, tq)
            koff = pl.multiple_of(kblk * tq, tq)
            keep = coli <= rowi + jnp.where(kblk == qblk, 0, tq)
            restart = t == i + 1
            for hh in range(2):
                m, lsum, acc = state[hh]
                if t > 0:
                    done[hh] = tuple(jnp.where(restart, a, b) for a, b in zip(state[hh], done[hh]))
                    m = jnp.where(restart, NEG, m)
                    lsum = jnp.where(restart, 0.0, lsum)
                    acc = jnp.where(restart, 0.0, acc)
                sc = _dot_nt(q_ref[hh, pl.ds(qoff, tq), :], k_ref[hh, pl.ds(koff, tq), :])
                sc = jnp.where(keep, sc, NEG)
                m_new = jnp.maximum(m, jnp.max(sc, axis=1, keepdims=True))
                p = jnp.exp(sc - m_new)
                alpha = jnp.exp(m - m_new)
                lsum = alpha * lsum + jnp.sum(p, axis=1, keepdims=True)
                acc = alpha * acc + _dot(p, v_ref[hh, pl.ds(koff, tq), :])
                state[hh] = (m_new, lsum, acc)
        for blk, res in ((i, done), (nq - 1 - i, state)):
            off = pl.multiple_of(blk * tq, tq)
            out = None
            for hh in range(2):
                m, lsum, acc = res[hh]
                o = acc * (1.0 / lsum)
                lse_ref[hh, pl.ds(off, tq), :] = m + jnp.log(lsum)
                out = o if hh == 0 else out + pltpu.roll(o, V_DIM, 1)
            o_ref[pl.ds(off, tq), :] = out

    pair = pl.BlockSpec((2, s, LANE), lambda j, i: (j, 0, 0))
    return pl.pallas_call(
        body, name="attn_fwd", grid=(nh // 2, nq // 2),
        in_specs=[pair, pair, pair],
        out_specs=[pl.BlockSpec((s, LANE), lambda j, i: (0, j)), pl.BlockSpec((2, s, 1), lambda j, i: (j, 0, 0))],
        out_shape=[_sds((s, D_MLA)), _sds((nh, s, 1))],
        compiler_params=_params(("parallel", "arbitrary")),
    )(q, k, v)


def _ssd_gate(y_ssd, s_z, g):
    yz = y_ssd * _silu(s_z)
    g0 = _iota(yz.shape, 1) < D_SSD // 2
    sq = yz * yz
    ms0 = jnp.sum(jnp.where(g0, sq, 0.0), axis=1, keepdims=True) / (D_SSD // 2)
    ms1 = jnp.sum(jnp.where(g0, 0.0, sq), axis=1, keepdims=True) / (D_SSD // 2)
    r = jnp.where(g0, lax.rsqrt(ms0 + SSD_NORM_EPS), lax.rsqrt(ms1 + SSD_NORM_EPS))
    nrm = yz * r
    return nrm * g, nrm, r, g0


def _outproj_fwd(x, proj, ya, y_ssd, o, g_ssd, w):
    s = x.shape[0]
    ts = _tile(s)

    def body(x_ref, sz_ref, cz_ref, ya_ref, ys_ref, o_ref, g_ref, w_ref, xo_ref, y_ref):
        yb = _ssd_gate(ys_ref[...], sz_ref[...], g_ref[...])[0]
        yc = o_ref[...] * _silu(cz_ref[...])
        y = jnp.concatenate([ya_ref[...], yb, yc], axis=1).astype(MXU)
        y_ref[...] = y
        xo_ref[...] = x_ref[...] + jnp.dot(y, w_ref[...], preferred_element_type=F32)

    return pl.pallas_call(
        body, name="outproj_fwd", grid=(s // ts,),
        in_specs=[_row(ts, D_MODEL), _gate_cols(ts, O_SZ), _gate_cols(ts, O_CZ), _row(ts, D_CONV_A), _row(ts, D_SSD),
                  _row(ts, D_MLA), _full((1, D_SSD)), _full((D_MODEL, D_MODEL))],
        out_specs=[_row(ts, D_MODEL), _row(ts, D_MODEL)],
        out_shape=[_sds((s, D_MODEL)), _sds((s, D_MODEL), MXU)],
        compiler_params=_params(("parallel",)),
    )(x, proj, proj, ya, y_ssd, o, g_ssd, w)


def _outproj_loss(x, proj, ya, y_ssd, o, g_ssd, w, final_g, tgt):
    s = x.shape[0]
    ts = _tile(s)

    def body(x_ref, sz_ref, cz_ref, ya_ref, ys_ref, o_ref, g_ref, w_ref, fg_ref, t_ref, dx_ref, dg_ref, loss_ref, y_ref):
        @pl.when(pl.program_id(0) == 0)
        def _():
            dg_ref[...] = jnp.zeros_like(dg_ref)
            loss_ref[...] = jnp.zeros_like(loss_ref)

        yb = _ssd_gate(ys_ref[...], sz_ref[...], g_ref[...])[0]
        yc = o_ref[...] * _silu(cz_ref[...])
        y = jnp.concatenate([ya_ref[...], yb, yc], axis=1).astype(MXU)
        y_ref[...] = y
        xv = x_ref[...] + jnp.dot(y, w_ref[...], preferred_element_type=F32)
        gv = fg_ref[...]
        yn, r = _rms(xv, gv)
        e = yn - t_ref[...]
        loss_ref[...] += jnp.sum(jnp.sum(e * e, axis=1, keepdims=True), axis=0, keepdims=True) * (0.5 / D_MODEL)
        dx, dg = _rms_bwd(e * (1.0 / D_MODEL), xv, r, gv)
        dx_ref[...] = dx
        dg_ref[...] += dg

    return pl.pallas_call(
        body, name="outproj_loss", grid=(s // ts,),
        in_specs=[_row(ts, D_MODEL), _gate_cols(ts, O_SZ), _gate_cols(ts, O_CZ), _row(ts, D_CONV_A), _row(ts, D_SSD),
                  _row(ts, D_MLA), _full((1, D_SSD)), _full((D_MODEL, D_MODEL)), _full((1, D_MODEL)), _row(ts, D_MODEL)],
        out_specs=[_row(ts, D_MODEL), _full((1, D_MODEL)), _full((1, LANE)), _row(ts, D_MODEL)],
        out_shape=[_sds((s, D_MODEL)), _sds((1, D_MODEL)), _sds((1, LANE)), _sds((s, D_MODEL), MXU)],
        compiler_params=_params(("arbitrary",)),
    )(x, proj, proj, ya, y_ssd, o, g_ssd, w, final_g, tgt)


def _outproj_bwd(dout, y, w, proj, y_ssd, o, g_ssd, dep=None):
    s = dout.shape[0]
    ts = _tile(s)

    def body(dout_ref, y_ref, w_ref, sz_ref, cz_ref, ys_ref, o_ref, g_ref,
             dya_ref, dys_ref, dsz_ref, dattn_ref, dcz_ref, dg_ref, dw_ref):
        @pl.when(pl.program_id(0) == 0)
        def _():
            dw_ref[...] = jnp.zeros_like(dw_ref)
            dg_ref[...] = jnp.zeros_like(dg_ref)

        dout_b = dout_ref[...].astype(MXU)
        dw_ref[...] += _dot_tn(y_ref[...], dout_b)
        dy = _dot_nt(dout_b, w_ref[...])
        dya_ref[...] = dy[:, :D_CONV_A]
        dyb = dy[:, D_CONV_A:D_CONV_A + D_SSD]
        sz = sz_ref[...]
        ys = ys_ref[...]
        gv = g_ref[...]
        _, nrm, r, g0 = _ssd_gate(ys, sz, gv)
        dg_ref[...] += jnp.sum(dyb * nrm, axis=0, keepdims=True)
        dn = dyb * gv
        t = dn * nrm
        mean = jnp.where(g0, jnp.sum(jnp.where(g0, t, 0.0), axis=1, keepdims=True),
                         jnp.sum(jnp.where(g0, 0.0, t), axis=1, keepdims=True)) / (D_SSD // 2)
        dyz = r * (dn - nrm * mean)
        dys_ref[...] = dyz * _silu(sz)
        dsz_ref[...] = (dyz * ys * _dsilu(sz)).astype(MXU)
        dyc = dy[:, D_CONV_A + D_SSD:]
        cz = cz_ref[...]
        dattn_ref[...] = dyc * _silu(cz)
        dcz_ref[...] = (dyc * o_ref[...] * _dsilu(cz)).astype(MXU)

    return _call_after(
        dep, body, (dout, y, w, proj, proj, y_ssd, o, g_ssd), name="outproj_bwd", grid=(s // ts,),
        in_specs=[_row(ts, D_MODEL), _row(ts, D_MODEL), _full((D_MODEL, D_MODEL)), _gate_cols(ts, O_SZ), _gate_cols(ts, O_CZ),
                  _row(ts, D_SSD), _row(ts, D_MLA), _full((1, D_SSD))],
        out_specs=[_row(ts, D_CONV_A), _row(ts, D_SSD), _row(ts, D_SSD), _row(ts, D_MLA), _row(ts, D_MLA),
                   _full((1, D_SSD)), _full((D_MODEL, D_MODEL))],
        out_shape=[_sds((s, D_CONV_A)), _sds((s, D_SSD)), _sds((s, D_SSD), MXU), _sds((s, D_MLA)), _sds((s, D_MLA), MXU),
                   _sds((1, D_SSD)), _sds((D_MODEL, D_MODEL))],
        compiler_params=_params(("arbitrary",)),
    )


def _attn_bwd(q, k, v, o, d_o, lse, dep=None):
    nh, s, _ = q.shape
    tq = _att_tile(s, ATT_BWD_TILE)
    nq = s // tq

    def body(q_ref, k_ref, v_ref, o_ref, do_ref, lse_ref, dq_ref, dk_ref, dv_ref, dop, delta):
        i = pl.program_id(1)

        @pl.when(i == 0)
        def _():
            lane = _iota((s, LANE), 1)
            for hh in range(2):
                dov = do_ref[...]
                ov = o_ref[...]
                if hh == 1:
                    dov = pltpu.roll(dov, V_DIM, 1)
                    ov = pltpu.roll(ov, V_DIM, 1)
                dov = jnp.where(lane < V_DIM, dov, 0.0)
                dop[hh] = dov.astype(MXU)
                delta[hh] = jnp.sum(dov * ov, axis=1, keepdims=True)
                dq_ref[hh] = jnp.zeros((s, LANE), F32)

        rowi = _iota((tq, tq), 0)
        coli = _iota((tq, tq), 1)
        z = jnp.zeros((tq, LANE), F32)
        state = [(z, z), (z, z)]
        done = [(z, z), (z, z)]
        for t in range(nq + 1):
            first = t <= nq - 1 - i
            kblk = jnp.where(first, i, nq - 1 - i)
            qblk = jnp.where(first, i + t, t - 1)
            qoff = pl.multiple_of(qblk * tq, tq)
            koff = pl.multiple_of(kblk * tq, tq)
            keep = coli <= rowi + jnp.where(kblk == qblk, 0, tq)
            restart = t == nq - i
            for hh in range(2):
                dk, dv = state[hh]
                if t > 0:
                    done[hh] = tuple(jnp.where(restart, a, b) for a, b in zip(state[hh], done[hh]))
                    dk = jnp.where(restart, 0.0, dk)
                    dv = jnp.where(restart, 0.0, dv)
                kb = k_ref[hh, pl.ds(koff, tq), :]
                qb = q_ref[hh, pl.ds(qoff, tq), :]
                dob = dop[hh, pl.ds(qoff, tq), :]
                sc = jnp.where(keep, _dot_nt(qb, kb), NEG)
                p = jnp.exp(sc - lse_ref[hh, pl.ds(qoff, tq), :])
                dp = _dot_nt(dob, v_ref[hh, pl.ds(koff, tq), :])
                ds = p * (dp - delta[hh, pl.ds(qoff, tq), :])
                dq_ref[hh, pl.ds(qoff, tq), :] += _dot(ds, kb)
                state[hh] = (dk + _dot_tn(ds, qb), dv + _dot_tn(p, dob))
        for blk, res in ((i, done), (nq - 1 - i, state)):
            off = pl.multiple_of(blk * tq, tq)
            for hh in range(2):
                dk_ref[hh, pl.ds(off, tq), :] = res[hh][0]
                dv_ref[hh, pl.ds(off, tq), :] = res[hh][1]

    pair = pl.BlockSpec((2, s, LANE), lambda j, i: (j, 0, 0))
    return _call_after(
        dep, body, (q, k, v, o, d_o, lse), name="attn_bwd", grid=(nh // 2, nq // 2),
        in_specs=[pair, pair, pair, pl.BlockSpec((s, LANE), lambda j, i: (0, j)), pl.BlockSpec((s, LANE), lambda j, i: (0, j)),
                  pl.BlockSpec((2, s, 1), lambda j, i: (j, 0, 0))],
        out_specs=[pair, pair, pair],
        out_shape=[_sds((nh, s, LANE))] * 3,
        scratch_shapes=[pltpu.VMEM((2, s, LANE), MXU), pltpu.VMEM((2, s, 1), F32)],
        compiler_params=_params(("parallel", "arbitrary")),
    )


def _ssd_bwd(xbc, proj, sc, states, dy, dep=None):
    s = xbc.shape[0]
    nc = s // SSD_CHUNK
    l = SSD_CHUNK
    cps = SSD_CHUNKS_PER_STEP

    def body(xbc_ref, tail_ref, sc_ref, st_ref, dy_ref, dxbc_ref, dtail_ref, dsc_ref, dstate):
        @pl.when(pl.program_id(0) == 0)
        def _():
            dstate[...] = jnp.zeros_like(dstate)
            dsc_ref[...] = jnp.zeros_like(dsc_ref)

        sc_v = sc_ref[...]
        lane1 = _iota((1, LANE), 1)
        rowp = _iota((LANE, 1), 0)
        rowl = _iota((l, 1), 0)
        d_row = sc_v[2:3, :]
        dstates = [dstate[j] for j in range(3)]
        for u in reversed(range(cps)):
            dstates = chunk(u, xbc_ref, tail_ref, sc_v, st_ref, dy_ref, dxbc_ref, dtail_ref, dsc_ref, dstates,
                            lane1, rowp, rowl, d_row)
        for j in range(3):
            dstate[j] = dstates[j]

    def chunk(u, xbc_ref, tail_ref, sc_v, st_ref, dy_ref, dxbc_ref, dtail_ref, dsc_ref, dstates, lane1, rowp, rowl, d_row):
        r = slice(u * l, (u + 1) * l)
        dstates = list(dstates)
        lane, row, tri, a_row, pre, dt, a_cs, a_t = _ssd_chunk_common(tail_ref[r, :], sc_v)
        da_col = jnp.zeros((l, LANE), F32)
        da_row = jnp.zeros((LANE, l), F32)
        dt_x = jnp.zeros((l, LANE), F32)
        dd_row = jnp.zeros((1, LANE), F32)
        db = [jnp.zeros((l, LANE), F32), jnp.zeros((l, LANE), F32)]
        dc = [jnp.zeros((l, LANE), F32), jnp.zeros((l, LANE), F32)]
        for j in range(3):
            xpair = xbc_ref[r, LANE * j:LANE * (j + 1)]
            dypair = dy_ref[r, LANE * j:LANE * (j + 1)]
            sp = st_ref[u, j]
            dsp = dstates[j]
            dxpair = jnp.zeros((l, LANE), F32)
            ds_new = jnp.zeros((LANE, LANE), F32)
            decay = jnp.zeros((LANE, 1), F32)
            for half in range(2):
                h = 2 * j + half
                g = h // 3
                hm = (lane < 64) if half == 0 else (lane >= 64)
                hrow = (rowp < 64) if half == 0 else (rowp >= 64)
                ac = _pick_col(a_cs, lane, DT_LANE + h)
                ar = _pick_row(a_t, row, DT_LANE + h)
                dtc = _pick_col(dt, lane, DT_LANE + h)
                alast = jnp.sum(jnp.where(lane1 == l - 1, ar, 0.0), axis=1, keepdims=True)
                dh = jnp.sum(jnp.where(lane1 == DT_LANE + h, d_row, 0.0), axis=1, keepdims=True)
                xm = jnp.where(hm, xpair, 0.0)
                xd = xm * dtc
                dym = jnp.where(hm, dypair, 0.0)
                bm = xbc_ref[r, D_SSD + LANE * g:D_SSD + LANE * (g + 1)]
                cm = xbc_ref[r, D_SSD + SSD_BC + LANE * g:D_SSD + SSD_BC + LANE * (g + 1)]
                lm = jnp.where(row >= lane, jnp.exp(jnp.minimum(ac - ar, 0.0)), 0.0)
                e_in = jnp.exp(ac)
                f_out = jnp.exp(alast - ac)
                e_last = jnp.exp(alast)
                m = _dot_nt(cm, bm) * lm
                y_off = jnp.where(hm, _dot_nt(cm, sp), 0.0) * e_in
                dm = _dot_nt(dym, xd)
                dxd = _dot_tn(m, dym)
                dg = dm * lm
                dye = dym * e_in
                dc[g] = dc[g] + _dot(dg, bm) + _dot(dye, sp)
                db[g] = db[g] + _dot_tn(dg, cm)
                qm = dm * m
                dac = jnp.sum(qm, axis=1, keepdims=True) + jnp.sum(dym * y_off, axis=1, keepdims=True)
                dar = -jnp.sum(qm, axis=0, keepdims=True)
                dxf = jnp.where(hm, _dot_nt(bm, dsp), 0.0)
                db[g] = db[g] + _dot(xd * f_out, dsp)
                dxd = dxd + dxf * f_out
                df = jnp.sum(dxf * xd, axis=1, keepdims=True) * f_out
                dac = dac - df
                s_last = jnp.sum(df, axis=0, keepdims=True)
                ss = jnp.sum(jnp.where(hrow, dsp * sp, 0.0), axis=1, keepdims=True)
                s_last = s_last + e_last * jnp.sum(ss, axis=0, keepdims=True)
                dac = dac + jnp.where(rowl == l - 1, s_last, 0.0)
                ds_new = ds_new + _dot_tn(dye, cm)
                decay = jnp.where(hrow, e_last, decay)
                dxpair = dxpair + dxd * dtc + dym * dh
                dt_x = dt_x + jnp.where(lane == DT_LANE + h, jnp.sum(dxd * xm, axis=1, keepdims=True), 0.0)
                dsum = jnp.sum(jnp.sum(dym * xm, axis=1, keepdims=True), axis=0, keepdims=True)
                dd_row = dd_row + jnp.where(lane1 == DT_LANE + h, dsum, 0.0)
                da_col = da_col + jnp.where(lane == DT_LANE + h, dac, 0.0)
                da_row = da_row + jnp.where(row == DT_LANE + h, dar, 0.0)
            dstates[j] = dsp * decay + ds_new
            dxbc_ref[r, LANE * j:LANE * (j + 1)] = dxpair
        for g in range(2):
            dxbc_ref[r, D_SSD + LANE * g:D_SSD + LANE * (g + 1)] = db[g]
            dxbc_ref[r, D_SSD + SSD_BC + LANE * g:D_SSD + SSD_BC + LANE * (g + 1)] = dc[g]
        dla = _dot_hi_tn(tri, da_col + da_row.T)
        ddt = dt_x + dla * a_row
        dpre = ddt * _sigmoid(pre)
        dtm = (lane >= DT_LANE) & (lane < DT_LANE + SSD_HEADS)
        dtail_ref[r, :] = jnp.where(dtm, dpre, 0.0).astype(MXU)
        dtm1 = (lane1 >= DT_LANE) & (lane1 < DT_LANE + SSD_HEADS)
        dsc_ref[0:1, :] += jnp.where(dtm1, jnp.sum(dpre, axis=0, keepdims=True), 0.0)
        dsc_ref[1:2, :] += jnp.where(dtm1, jnp.sum(dla * dt, axis=0, keepdims=True) * a_row, 0.0)
        dsc_ref[2:3, :] += dd_row
        return dstates

    rev = lambda c: nc // cps - 1 - c
    return _call_after(
        dep, body, (xbc, proj, sc, states, dy), name="ssd_bwd", grid=(nc // cps,),
        in_specs=[pl.BlockSpec((cps * l, N_XBC), lambda c: (rev(c), 0)),
                  pl.BlockSpec((cps * l, LANE), lambda c: (rev(c), O_TAIL // LANE)), _full((8, LANE)),
                  pl.BlockSpec((cps, 3, LANE, LANE), lambda c: (rev(c), 0, 0, 0)),
                  pl.BlockSpec((cps * l, D_SSD), lambda c: (rev(c), 0))],
        out_specs=[pl.BlockSpec((cps * l, N_XBC), lambda c: (rev(c), 0)), pl.BlockSpec((cps * l, LANE), lambda c: (rev(c), 0)),
                   _full((8, LANE))],
        out_shape=[_sds((s, N_XBC)), _sds((s, LANE), MXU), _sds((8, LANE))],
        scratch_shapes=[pltpu.VMEM((3, LANE, LANE), F32)],
        compiler_params=_params(("arbitrary",)),
    )


def _sconv_bwd(proj, w, b, dxbc, dep=None):
    s = proj.shape[0]

    def body(u_ref, w_ref, b_ref, d_ref, du_ref, dw_ref, db_ref):
        u = u_ref[...]
        wv = w_ref[...]
        dpre = d_ref[...] * _dsilu(_sconv_pre(u, wv, b_ref[...]))
        ahead = [_shift_up(dpre, j) for j in range(4)]
        du_ref[...] = (wv[3:4, :] * ahead[0] + wv[2:3, :] * ahead[1] + wv[1:2, :] * ahead[2]
                       + wv[0:1, :] * ahead[3]).astype(MXU)
        for k in range(4):
            dw_ref[k:k + 1, :] = jnp.sum(ahead[3 - k] * u, axis=0, keepdims=True)
        db_ref[...] = jnp.sum(dpre, axis=0, keepdims=True)

    blk = pl.BlockSpec((s, LANE), lambda j: (0, j))
    return _call_after(
        dep, body, (proj, w, b, dxbc), name="sconv_bwd", grid=(N_XBC // LANE,),
        in_specs=[_col(s, O_XBC), pl.BlockSpec((4, LANE), lambda j: (0, j)), pl.BlockSpec((1, LANE), lambda j: (0, j)), blk],
        out_specs=[blk, pl.BlockSpec((4, LANE), lambda j: (0, j)), pl.BlockSpec((1, LANE), lambda j: (0, j))],
        out_shape=[_sds((s, N_XBC), MXU), _sds((4, N_XBC)), _sds((1, N_XBC))],
        compiler_params=_params(("parallel",)),
    )


def _conva_bwd(proj, w, dya, dep=None):
    s = proj.shape[0]

    def body(h_ref, b_ref, c_ref, z_ref, w_ref, d_ref, da_ref, dw_ref):
        ah, ab, acv, az = h_ref[...], b_ref[...], c_ref[...], z_ref[...]
        wv = w_ref[...]
        u = acv * ah
        cv = wv[2:3, :] * u + wv[1:2, :] * _shift_down(u, 1) + wv[0:1, :] * _shift_down(u, 2)
        dy = d_ref[...]
        sz = _silu(az)
        da_ref[1] = (dy * cv * sz).astype(MXU)
        da_ref[3] = (dy * ab * cv * _dsilu(az)).astype(MXU)
        dcv = dy * ab * sz
        ahead = [_shift_up(dcv, j) for j in range(3)]
        du = wv[2:3, :] * ahead[0] + wv[1:2, :] * ahead[1] + wv[0:1, :] * ahead[2]
        da_ref[0] = (du * acv).astype(MXU)
        da_ref[2] = (du * ah).astype(MXU)
        for k in range(3):
            dw_ref[k:k + 1, :] = jnp.sum(ahead[2 - k] * u, axis=0, keepdims=True)

    return _call_after(
        dep, body, (proj, proj, proj, proj, w, dya), name="conva_bwd", grid=(D_CONV_A // LANE,),
        in_specs=[_col(s, O_AH), _col(s, O_AB), _col(s, O_AC), _col(s, O_AZ), pl.BlockSpec((3, LANE), lambda j: (0, j)),
                  pl.BlockSpec((s, LANE), lambda j: (0, j))],
        out_specs=[pl.BlockSpec((4, s, LANE), lambda j: (0, 0, j)), pl.BlockSpec((3, LANE), lambda j: (0, j))],
        out_shape=[_sds((4, s, D_CONV_A), MXU), _sds((3, D_CONV_A))],
        compiler_params=_params(("parallel",)),
    )


def _mla_prep_bwd(dq, dk, dv, proj, qn, kvn, rq, rkv, gq, gkv, wq, wkv, cos, sin):
    s = proj.shape[0]
    ts = _tile(s)
    nh = MLA_HEADS

    def body(dq_ref, dk_ref, dv_ref, cqa_ref, ckv_ref, qn_ref, kvn_ref, rq_ref, rkv_ref, gq_ref, gkv_ref,
             wq_ref, wkv_ref, cos_ref, sin_ref, dcqa_ref, dckv_ref, dtail_ref, dwq_ref, dwkv_ref, dgq_ref, dgkv_ref):
        @pl.when(pl.program_id(0) == 0)
        def _():
            dwq_ref[...] = jnp.zeros_like(dwq_ref)
            dwkv_ref[...] = jnp.zeros_like(dwkv_ref)
            dgq_ref[...] = jnp.zeros_like(dgq_ref)
            dgkv_ref[...] = jnp.zeros_like(dgkv_ref)

        cosv = cos_ref[...]
        sinv = sin_ref[...]
        lane = _iota((ts, LANE), 1)
        rope_lanes = (lane >= ROPE_LANE) & (lane < ROPE_LANE + QK_ROPE)

        def unrope(gr):
            return gr * cosv + _rope_swap(gr * sinv)

        dqs, dks, dvs = [], [], []
        dkr = jnp.zeros((ts, LANE), F32)
        for h in range(nh):
            dqs.append(unrope(dq_ref[h] * ATT_SCALE).astype(MXU))
            dkh = dk_ref[h]
            dks.append(jnp.where(lane < QK_NOPE, dkh, 0.0).astype(MXU))
            dkr = dkr + jnp.where(rope_lanes, dkh, 0.0)
            dvs.append(dv_ref[h].astype(MXU))
        dtail_ref[...] = pltpu.roll(jnp.where(rope_lanes, unrope(dkr), 0.0), ROPE_LANE, 1).astype(MXU)
        dq_all = jnp.concatenate(dqs, axis=1)
        dkv_all = jnp.concatenate(dks + dvs, axis=1)
        dwq_ref[...] += _dot_tn(dq_all, qn_ref[...])
        dwkv_ref[...] += _dot_tn(dkv_all, kvn_ref[...])
        dcqa, dgq = _rms_bwd(_dot(dq_all, wq_ref[...]), cqa_ref[...], rq_ref[...], gq_ref[...])
        dckv, dgkv = _rms_bwd(_dot(dkv_all, wkv_ref[...]), ckv_ref[...], rkv_ref[...], gkv_ref[...])
        dcqa_ref[...] = dcqa.astype(MXU)
        dckv_ref[...] = dckv.astype(MXU)
        dgq_ref[...] += dgq
        dgkv_ref[...] += dgkv

    head = pl.BlockSpec((nh, ts, LANE), lambda i: (0, i, 0))
    return pl.pallas_call(
        body, name="mla_prep_bwd", grid=(s // ts,),
        in_specs=[head, head, head,
                  pl.BlockSpec((ts, Q_LORA), lambda i: (i, O_CQA // Q_LORA)),
                  pl.BlockSpec((ts, KV_LORA), lambda i: (i, O_CKV // KV_LORA)),
                  _row(ts, Q_LORA), _row(ts, KV_LORA), _row(ts, 1), _row(ts, 1),
                  _full((1, Q_LORA)), _full((1, KV_LORA)), _full((nh * LANE, Q_LORA)), _full((2 * nh * LANE, KV_LORA)),
                  _row(ts, LANE), _row(ts, LANE)],
        out_specs=[_row(ts, Q_LORA), _row(ts, KV_LORA), _row(ts, LANE), _full((nh * LANE, Q_LORA)),
                   _full((2 * nh * LANE, KV_LORA)), _full((1, Q_LORA)), _full((1, KV_LORA))],
        out_shape=[_sds((s, Q_LORA), MXU), _sds((s, KV_LORA), MXU), _sds((s, LANE), MXU), _sds((nh * LANE, Q_LORA)),
                   _sds((2 * nh * LANE, KV_LORA)), _sds((1, Q_LORA)), _sds((1, KV_LORA))],
        compiler_params=_params(("arbitrary",)),
    )(dq, dk, dv, proj, proj, qn, kvn, rq, rkv, gq, gkv, wq, wkv, cos, sin)


def _inproj_bwd(da4, dsz, dxbc_in, dcqa, dckv, dcz, dtail_a, dtail_b, w, x, rstd, g, dout, dep=None):
    s = x.shape[0]
    ts = _tile(s)

    def body(da_ref, dsz_ref, dxbc_ref, dcqa_ref, dckv_ref, dcz_ref, dta_ref, dtb_ref, w_ref, x_ref, r_ref, g_ref, dout_ref,
             dproj_ref, dx_ref, dg_ref):
        @pl.when(pl.program_id(0) == 0)
        def _():
            dg_ref[...] = jnp.zeros_like(dg_ref)

        dproj = jnp.concatenate(
            [da_ref[0], da_ref[1], da_ref[2], da_ref[3], dxbc_ref[...], dsz_ref[...], dcqa_ref[...], dckv_ref[...],
             dcz_ref[...], dta_ref[...] + dtb_ref[...]], axis=1)
        dproj_ref[...] = dproj
        dh = _dot_nt(dproj, w_ref[...])
        dx, dg = _rms_bwd(dh, x_ref[...], r_ref[...], g_ref[...])
        dx_ref[...] = dout_ref[...] + dx
        dg_ref[...] += dg

    return _call_after(
        dep, body, (da4, dsz, dxbc_in, dcqa, dckv, dcz, dtail_a, dtail_b, w, x, rstd, g, dout), name="inproj_bwd", grid=(s // ts,),
        in_specs=[pl.BlockSpec((4, ts, D_CONV_A), lambda i: (0, i, 0)), _row(ts, D_SSD), _row(ts, N_XBC), _row(ts, Q_LORA),
                  _row(ts, KV_LORA), _row(ts, D_MLA), _row(ts, LANE), _row(ts, LANE), _full((D_MODEL, NCOL)),
                  _row(ts, D_MODEL), _row(ts, 1), _full((1, D_MODEL)), _row(ts, D_MODEL)],
        out_specs=[_row(ts, NCOL), _row(ts, D_MODEL), _full((1, D_MODEL))],
        out_shape=[_sds((s, NCOL), MXU), _sds((s, D_MODEL)), _sds((1, D_MODEL))],
        compiler_params=_params(("arbitrary",)),
    )


DWIN_BLOCK = 640


def _dwin(h, dproj, dep=None):
    s = h.shape[0]

    def body(h_ref, d_ref, o_ref):
        o_ref[...] = _dot_tn(h_ref[...], d_ref[...])

    return _call_after(
        dep, body, (h, dproj), name="dwin", grid=(NCOL // DWIN_BLOCK,),
        in_specs=[_full((s, D_MODEL)), pl.BlockSpec((s, DWIN_BLOCK), lambda j: (0, j))],
        out_specs=pl.BlockSpec((D_MODEL, DWIN_BLOCK), lambda j: (0, j)),
        out_shape=_sds((D_MODEL, NCOL)),
        compiler_params=_params(("parallel",)),
    )


def _adamw(ws, gs, ms, vs, whole):
    n = len(ws)
    bc1 = 1.0 - ADAM_B1 ** ADAM_STEP
    bc2 = 1.0 - ADAM_B2 ** ADAM_STEP

    def body(*refs):
        ins, outs = refs[:4 * n], refs[4 * n:]
        for a in range(n):
            w_ref, g_ref, m_ref, v_ref = ins[a], ins[n + a], ins[2 * n + a], ins[3 * n + a]
            gv = g_ref[...]
            mn = ADAM_B1 * m_ref[...] + (1.0 - ADAM_B1) * gv
            vn = ADAM_B2 * v_ref[...] + (1.0 - ADAM_B2) * (gv * gv)
            outs[n + a][...] = mn
            outs[2 * n + a][...] = vn
            outs[a][...] = -ADAM_LR * ((mn / bc1) / (jnp.sqrt(vn / bc2) + ADAM_EPS) + ADAM_WD * w_ref[...])

    if whole:
        grid, blks = (1,), [pl.BlockSpec(w.shape, lambda i, _n=w.ndim: (0,) * _n) for w in ws]
    else:
        grid = (ws[0].shape[0], 2)
        blks = [pl.BlockSpec((1, w.shape[1] // 2, w.shape[2]), lambda i, k: (i, k, 0)) for w in ws]
    out = pl.pallas_call(
        body, name="adamw", grid=grid,
        in_specs=blks * 4, out_specs=blks * 3, out_shape=[_sds(w.shape) for w in ws] * 3,
        compiler_params=_params(("parallel",) * len(grid)),
    )(*ws, *gs, *ms, *vs)
    return [(out[a], out[n + a], out[2 * n + a]) for a in range(n)]


ADAMW_COLS_BLOCK = 512


def _adamw_cols(w_t, gs, m_t, v_t):
    cols, nl, rows = w_t.shape
    bc1 = 1.0 - ADAM_B1 ** ADAM_STEP
    bc2 = 1.0 - ADAM_B2 ** ADAM_STEP

    def body(w_ref, m_ref, v_ref, *rest):
        g_refs, (go_ref, d_ref, mo_ref, vo_ref), g_blk = rest[:nl], rest[nl:nl + 4], rest[-1]
        for l in range(nl):
            g_blk[:, l, :] = g_refs[l][...].T
        gv = g_blk[...]
        mn = ADAM_B1 * m_ref[...] + (1.0 - ADAM_B1) * gv
        vn = ADAM_B2 * v_ref[...] + (1.0 - ADAM_B2) * (gv * gv)
        go_ref[...] = gv
        mo_ref[...] = mn
        vo_ref[...] = vn
        d_ref[...] = -ADAM_LR * ((mn / bc1) / (jnp.sqrt(vn / bc2) + ADAM_EPS) + ADAM_WD * w_ref[...])

    tc = ADAMW_COLS_BLOCK
    blk = pl.BlockSpec((tc, nl, rows), lambda j: (j, 0, 0))
    gblk = pl.BlockSpec((rows, tc), lambda j: (0, j))
    return pl.pallas_call(
        body, name="adamw_cols", grid=(pl.cdiv(cols, tc),),
        in_specs=[blk] * 3 + [gblk] * nl, out_specs=[blk] * 4, out_shape=[_sds(w_t.shape)] * 4,
        scratch_shapes=[pltpu.VMEM((tc, nl, rows), F32)],
        compiler_params=_params(("parallel",)),
    )(w_t, m_t, v_t, *gs)


COL_MOVES = ((0, 0, 1024), (1024, O_SZ, 384), (1408, O_XBC, 896), (2304, O_TAIL + DT_LANE, 6), (2310, O_CQA, 256),
             (2566, O_CKV, 128), (2694, O_TAIL, 32), (2726, O_CZ, 384))


def _move_cols(w, moves, width):
    out = None
    for src, dst, n in moves:
        piece = jnp.pad(w[..., src:src + n], [(0, 0)] * (w.ndim - 1) + [(dst, width - dst - n)])
        out = piece if out is None else out + piece
    return out


def _perm_cols(w):
    return _move_cols(w, COL_MOVES, NCOL)


def _unperm_cols(g):
    return _move_cols(g, [(dst, src, n) for src, dst, n in COL_MOVES], IN_COLS)


def _wq_layout(wt):
    return jnp.pad(wt.reshape(MLA_HEADS, QK_NOPE + QK_ROPE, Q_LORA), ((0, 0), (0, 32), (0, 0))).reshape(MLA_HEADS * LANE, Q_LORA)


def _wq_unlayout(g):
    return g.reshape(MLA_HEADS, LANE, Q_LORA)[:, :QK_NOPE + QK_ROPE].reshape(MLA_HEADS * (QK_NOPE + QK_ROPE), Q_LORA)


def _wkv_layout(wt):
    t = wt.reshape(MLA_HEADS, 2, 64, KV_LORA).transpose(1, 0, 2, 3)
    return jnp.pad(t, ((0, 0), (0, 0), (0, 64), (0, 0))).reshape(2 * MLA_HEADS * LANE, KV_LORA)


def _wkv_unlayout(g):
    t = g.reshape(2, MLA_HEADS, LANE, KV_LORA)[:, :, :64]
    return t.transpose(1, 0, 2, 3).reshape(MLA_HEADS * LANE, KV_LORA)


def _rope_tables(positions):
    inv_freq = ROPE_BASE ** (-jnp.arange(0, QK_ROPE, 2, dtype=F32) / QK_ROPE)
    ang = positions.astype(F32)[:, None] * inv_freq
    cos, sin = jnp.cos(ang), jnp.sin(ang)
    s = positions.shape[0]
    one, zero = jnp.ones((s, ROPE_LANE), F32), jnp.zeros((s, ROPE_LANE), F32)
    cos_t = jnp.concatenate([one, cos, cos, one[:, :32]], axis=1)
    sin_t = jnp.concatenate([zero, -sin, sin, zero[:, :32]], axis=1)
    return cos_t, sin_t


def _ssd_scalars(dt_bias, a_log, d_skip):
    return jnp.pad(jnp.stack([dt_bias, a_log, d_skip]), ((0, 5), (DT_LANE, LANE - DT_LANE - SSD_HEADS)))


def _layer_fwd(x, lw, cos, sin, dep=None, late=None, head=None):
    proj, h, rstd = _inproj_fwd(x, lw["norm_g"], lw["w_in"], dep)
    ya = _conva_fwd(proj, lw["conv_a_w"])
    xbc = _sconv_fwd(proj, lw["ssd_conv_w"], lw["ssd_conv_b"])
    y_ssd, states = _ssd_fwd(xbc, proj, lw["sc"])
    if late is not None:
        lw = {**lw, **late(ya, y_ssd)}
    q, k, v, qn, kvn, rq, rkv = _mla_prep_fwd(proj, lw["gq"], lw["gkv"], lw["wq"], lw["wkv"], cos, sin)
    o, lse = _attn_fwd(q, k, v)
    if head is None:
        x_out, y = _outproj_fwd(x, proj, ya, y_ssd, o, lw["g_ssd"], lw["w_out"])
    else:
        *x_out, y = _outproj_loss(x, proj, ya, y_ssd, o, lw["g_ssd"], lw["w_out"], *head)
    saved = dict(x=x, proj=proj, h=h, rstd=rstd, xbc=xbc, y_ssd=y_ssd, states=states, q=q, k=k, v=v, qn=qn, kvn=kvn,
                 rq=rq, rkv=rkv, o=o, lse=lse, y=y)
    return x_out, saved, lw


def _layer_bwd(dout, lw, sv, cos, sin, rs=None, begin_early=None):
    tok = lambda: None if rs is None else rs["h"]["token"]
    dya, dys, dsz, d_o, dcz, dg_ssd, dw_out = _outproj_bwd(dout, sv["y"], lw["w_out"], sv["proj"], sv["y_ssd"], sv["o"],
                                                            lw["g_ssd"], tok())
    if rs is not None:
        rs = _rs_add_mine(rs, [dya])
    dq, dk, dv = _attn_bwd(sv["q"], sv["k"], sv["v"], sv["o"], d_o, sv["lse"], tok())
    dxbc, dtail_s, dsc = _ssd_bwd(sv["xbc"], sv["proj"], lw["sc"], sv["states"], dys, tok())
    da4, dw_conva = _conva_bwd(sv["proj"], lw["conv_a_w"], dya, tok())
    if rs is not None:
        rs = _rs_add_chips(rs, [dq, dxbc, da4])
    du, dw_sconv, db_sconv = _sconv_bwd(sv["proj"], lw["ssd_conv_w"], lw["ssd_conv_b"], dxbc, tok())
    dcqa, dckv, dtail_m, dwq, dwkv, dgq, dgkv = _mla_prep_bwd(
        dq, dk, dv, sv["proj"], sv["qn"], sv["kvn"], sv["rq"], sv["rkv"], lw["gq"], lw["gkv"], lw["wq"], lw["wkv"], cos, sin)
    early = None if begin_early is None else begin_early(dw_out, dwq, dwkv)
    etok = lambda: None if early is None else early["h"]["token"]
    dproj, dx, dg = _inproj_bwd(da4, dsz, du, dcqa, dckv, dcz, dtail_s, dtail_m, lw["w_in"], sv["x"], sv["rstd"],
                                lw["norm_g"], dout, etok())
    reduced = None if rs is None else _rs_end(rs, [du, dcqa, dx])
    if early is not None:
        early = _rs_add_mine(early, [dx])
    dw_in = _dwin(sv["h"], dproj, etok())
    if early is not None:
        early = _rs_add_chips(early, [dw_in])
    grads = dict(norm_g=dg, w_in=dw_in, conv_a_w=dw_conva, ssd_conv_w=dw_sconv, ssd_conv_b=db_sconv, sc=dsc,
                 g_ssd=dg_ssd, gq=dgq, wq=dwq, gkv=dgkv, wkv=dwkv, w_out=dw_out)
    return dx, grads, reduced, early


ANY = pl.BlockSpec(memory_space=pl.ANY)
N_CHIPS = 4
N_DEV = 8


def _place():
    return lax.axis_index("x"), lax.axis_index("y"), lax.axis_index("c")


HBM_SPEC = pl.BlockSpec(memory_space=pltpu.HBM)
SEM_SPEC = pl.BlockSpec(memory_space=pltpu.SEMAPHORE)
PAYLOAD = jnp.bfloat16


def _hbm(a):
    return pltpu.with_memory_space_constraint(a, pltpu.HBM)


def _run_plan(plan, srcs, lands, send_sems, recv_sems, start, wait):
    copies = plan(srcs, lands)
    if start:
        for i, (src, dst, _, to) in enumerate(copies):
            pltpu.make_async_remote_copy(src_ref=src, dst_ref=dst, send_sem=send_sems.at[i], recv_sem=recv_sems.at[i],
                                         device_id=to, device_id_type=MESH_T).start()
    if wait:
        for i, (src, _, arrives, to) in enumerate(copies):
            cp = pltpu.make_async_remote_copy(src_ref=src, dst_ref=arrives, send_sem=send_sems.at[i],
                                              recv_sem=recv_sems.at[i], device_id=to, device_id_type=MESH_T)
            cp.wait_send()
            cp.wait_recv()


def _exchange_start(name, plan, n_copies, srcs, land_shapes, deps):
    ns, nl = len(srcs), len(land_shapes)
    n_in = ns + nl + len(deps)

    def body(*refs):
        send_sems, recv_sems = refs[n_in], refs[n_in + 1]
        token = refs[-1]
        _run_plan(plan, refs[:ns], refs[ns:ns + nl], send_sems, recv_sems, True, False)
        token[...] = jnp.zeros_like(token)

    thru = [pltpu.HBM(a.shape, a.dtype) for a in srcs] + [pltpu.HBM(a.shape, a.dtype) for a in land_shapes]
    outs = pl.pallas_call(
        body, name=name,
        out_shape=(pltpu.SemaphoreType.DMA((n_copies,)), pltpu.SemaphoreType.DMA((n_copies,)), *thru, _sds((8, LANE))),
        in_specs=[HBM_SPEC] * (ns + nl) + [ANY] * len(deps),
        out_specs=(SEM_SPEC, SEM_SPEC, *[HBM_SPEC] * (ns + nl), pl.BlockSpec(memory_space=pltpu.VMEM)),
        input_output_aliases={i: 2 + i for i in range(ns + nl)},
        compiler_params=pltpu.CompilerParams(has_side_effects=pltpu.SideEffectType.DATAFLOW_SIDE_EFFECTING),
    )(*[_hbm(a) for a in srcs], *[_hbm(lax.empty(a.shape, a.dtype)) for a in land_shapes], *deps)
    return (outs[0], outs[1]), list(outs[2:2 + ns]), list(outs[2 + ns:2 + ns + nl]), outs[-1]


def _exchange_wait(name, plan, sems, srcs, lands, after):
    ns, nl = len(srcs), len(lands)

    def body(*refs):
        _run_plan(plan, refs[:ns], refs[ns:ns + nl], refs[ns + nl], refs[ns + nl + 1], False, True)

    outs = pl.pallas_call(
        body, name=name,
        out_shape=[pltpu.HBM(a.shape, a.dtype) for a in list(srcs) + list(lands)],
        in_specs=[HBM_SPEC] * (ns + nl) + [SEM_SPEC, SEM_SPEC] + [ANY] * len(after), out_specs=[HBM_SPEC] * (ns + nl),
        input_output_aliases={i: i for i in range(ns + nl)},
        compiler_params=pltpu.CompilerParams(has_side_effects=pltpu.SideEffectType.DATAFLOW_SIDE_EFFECTING),
    )(*srcs, *lands, sems[0], sems[1], *after)
    return list(outs[:ns]), list(outs[ns:])


def _xchg_begin(name, plan, n_copies, srcs, land_shapes, deps=()):
    sems, srcs_t, lands_t, token = _exchange_start(name + "_start", plan, n_copies, srcs, land_shapes, list(deps))
    return dict(name=name, plan=plan, sems=sems, srcs=srcs_t, lands=lands_t, token=token)


def _xchg_end(h, after):
    return _exchange_wait(h["name"] + "_wait", h["plan"], h["sems"], h["srcs"], h["lands"], after)


def _other_chips():
    x, y, c = _place()
    return [(1 - x, y), (x, 1 - y), (1 - x, 1 - y)]


def _gather_plan(srcs, lands):
    x, y, c = _place()
    me = 2 * x + y
    return [(srcs[a], lands[a].at[me], lands[a].at[2 * cx + cy], (cx, cy, c))
            for (cx, cy) in _other_chips() for a in range(len(srcs))]


def _gather_begin(shards, tag, deps=()):
    shapes = [_sds((N_CHIPS,) + a.shape, a.dtype) for a in shards]
    return _xchg_begin(f"gather_{tag}", _gather_plan, 3 * len(shards), shards, shapes, deps)


def _gather_end(h, after):
    shards, lands = _xchg_end(h, after)
    me = 2 * lax.axis_index("x") + lax.axis_index("y")
    return [lax.dynamic_update_index_in_dim(g, s, me, 0) for g, s in zip(lands, shards)]


def _gather_half_plan(srcs, lands):
    x, y, c = _place()
    me = 2 * x + y
    out = []
    for (cx, cy) in _other_chips():
        out.append((srcs[0].at[c], lands[0].at[me, c], lands[0].at[2 * cx + cy, c], (cx, cy, c)))
        out += [(srcs[a], lands[a].at[me], lands[a].at[2 * cx + cy], (cx, cy, c)) for a in range(1, len(srcs))]
    return out


def _forward_plan(bufs, _):
    x, y, c = _place()
    return [(bufs[0].at[2 * cx + cy, c], bufs[0].at[2 * cx + cy, c], bufs[0].at[2 * cx + cy, 1 - c], (x, y, 1 - c))
            for (cx, cy) in _other_chips()]


def _swap_plan(srcs, lands):
    x, y, c = _place()
    return [(srcs[a].at[:, 1 - c], lands[a], lands[a], (x, y, 1 - c)) for a in range(len(srcs))]


def _chips_plan(srcs, lands):
    x, y, c = _place()
    me = 2 * x + y
    return [(srcs[a].at[2 * cx + cy], lands[a].at[me], lands[a].at[2 * cx + cy], (cx, cy, c))
            for (cx, cy) in _other_chips() for a in range(len(srcs))]


def _share_plan(srcs, lands):
    x, y, c = _place()
    return [(srcs[a], lands[a].at[c], lands[a].at[1 - c], (x, y, 1 - c)) for a in range(len(srcs))]


def _allreduce_small(slab, dep=None):
    r = slab.shape[0]

    def body(s_ref, o_ref, gath, send_sems, recv_sems):
        x, y, c = _place()
        me = 4 * x + 2 * y + c
        gath[me] = s_ref[...]
        cps = []
        for rel in range(1, N_DEV):
            px = 1 - x if rel & 4 else x
            py = 1 - y if rel & 2 else y
            pc = 1 - c if rel & 1 else c
            cp = pltpu.make_async_remote_copy(src_ref=s_ref, dst_ref=gath.at[me], send_sem=send_sems.at[rel - 1],
                                              recv_sem=recv_sems.at[rel - 1], device_id=(px, py, pc), device_id_type=MESH_T)
            cp.start()
            cps.append(cp)
        for cp in cps:
            cp.wait()
        acc = gath[0]
        for d in range(1, N_DEV):
            acc = acc + gath[d]
        o_ref[...] = acc

    vm = pl.BlockSpec(memory_space=pltpu.VMEM)
    return _call_after(
        dep, body, (slab,), name="allreduce_small", in_specs=[vm], out_specs=vm, out_shape=_sds((r, LANE)),
        scratch_shapes=[pltpu.VMEM((N_DEV, r, LANE), F32), pltpu.SemaphoreType.DMA((N_DEV - 1,)),
                        pltpu.SemaphoreType.DMA((N_DEV - 1,))],
    )


def _add_mine(g4s, recvs, half):
    n = len(g4s)

    def body(h_ref, *refs):
        for g_ref, r_ref, o_ref in zip(refs[:n], refs[n:2 * n], refs[2 * n:]):
            o_ref[0] = (g_ref[0, 0] + r_ref[0]).astype(o_ref.dtype)

    dims = [g.shape[2:] for g in g4s]
    return pl.pallas_call(
        body, name="add_mine",
        grid_spec=pltpu.PrefetchScalarGridSpec(
            num_scalar_prefetch=1, grid=(N_CHIPS,),
            in_specs=[pl.BlockSpec((1, 1) + d, lambda j, h: (j, h[0], 0, 0)) for d in dims]
            + [pl.BlockSpec((1,) + d, lambda j, h: (j, 0, 0)) for d in dims],
            out_specs=[pl.BlockSpec((1,) + d, lambda j, h: (j, 0, 0)) for d in dims]),
        out_shape=[_sds((N_CHIPS,) + d, PAYLOAD) for d in dims],
        compiler_params=_params(("parallel",)),
    )(half, *g4s, *recvs)


def _add_chips(es, ps, me):
    n = len(es)

    def body(m_ref, *refs):
        for e_ref, p_ref, o_ref in zip(refs[:n], refs[n:2 * n], refs[2 * n:]):
            own = p_ref[0].astype(F32)
            acc = None
            for s in range(N_CHIPS):
                t = jnp.where(m_ref[0] == s, own, e_ref[s].astype(F32))
                acc = t if acc is None else acc + t
            o_ref[...] = acc

    dims = [e.shape[1:] for e in es]
    return pl.pallas_call(
        body, name="add_chips",
        grid_spec=pltpu.PrefetchScalarGridSpec(
            num_scalar_prefetch=1, grid=(1,),
            in_specs=[pl.BlockSpec((N_CHIPS,) + d, lambda i, m: (0, 0, 0)) for d in dims]
            + [pl.BlockSpec((1,) + d, lambda i, m: (m[0], 0, 0)) for d in dims],
            out_specs=[pl.BlockSpec(d, lambda i, m: (0, 0)) for d in dims]),
        out_shape=[_sds(d) for d in dims],
        compiler_params=_params(("arbitrary",)),
    )(me, *es, *ps)


def _rs_begin(gs, tag, deps=()):
    g4 = [g.reshape(N_CHIPS, 2, g.shape[0] // (2 * N_CHIPS), g.shape[1]) for g in gs]
    h = _xchg_begin(f"rs_swap_{tag}", _swap_plan, len(gs), g4, [_sds((N_CHIPS,) + g.shape[2:]) for g in g4], deps)
    return dict(h=h, tag=tag, shapes=[g.shape for g in gs])


def _rs_add_mine(st, after):
    g4, recv = _xchg_end(st["h"], after)
    half = jnp.reshape(lax.axis_index("c"), (1,)).astype(jnp.int32)
    ps = _add_mine(g4, recv, half)
    st["h"] = _xchg_begin(f"rs_chips_{st['tag']}", _chips_plan, 3 * len(ps), ps, [_sds(p.shape, p.dtype) for p in ps])
    return st


def _rs_add_chips(st, after):
    ps, es = _xchg_end(st["h"], after)
    me = jnp.reshape(2 * lax.axis_index("x") + lax.axis_index("y"), (1,)).astype(jnp.int32)
    fs = _add_chips(es, ps, me)
    st["h"] = _xchg_begin(f"rs_share_{st['tag']}", _share_plan, len(fs), fs, [_sds((2,) + f.shape) for f in fs])
    return st


def _rs_end(st, after):
    fs, ss = _xchg_end(st["h"], after)
    c = lax.axis_index("c")
    return [lax.dynamic_update_index_in_dim(s, f, c, 0).reshape(shp[0] // N_CHIPS, shp[1])
            for s, f, shp in zip(ss, fs, st["shapes"])]


WEIGHTS = ["norm_g", "w_in", "conv_a_w", "ssd_conv_w", "ssd_conv_b", "ssd_dt_bias", "ssd_a_log", "ssd_d", "ssd_norm_g",
           "mla_q_norm_g", "w_qb", "mla_kv_norm_g", "w_kvb", "w_out", "final_norm_g"]
BIG = ["w_in", "w_qb", "w_kvb", "w_out"]
SLAB_ROWS = 128


def _to_slab(parts, rows):
    flat = jnp.concatenate([p.reshape(-1) for p in parts])
    return jnp.pad(flat, (0, rows * LANE - flat.shape[0])).reshape(rows, LANE)


def _from_slab(slab, shapes):
    flat = slab.reshape(-1)
    out, off = [], 0
    for shp in shapes:
        n = int(np.prod(shp))
        out.append(flat[off:off + n].reshape(shp))
        off += n
    return out


def kernel(x, positions, norm_g, w_in, conv_a_w, ssd_conv_w, ssd_conv_b, ssd_dt_bias, ssd_a_log, ssd_d, ssd_norm_g, mla_q_norm_g, w_qb, mla_kv_norm_g, w_kvb, w_out, final_norm_g, loss_target, m_norm_g, m_w_in, m_conv_a_w, m_ssd_conv_w, m_ssd_conv_b, m_ssd_dt_bias, m_ssd_a_log, m_ssd_d, m_ssd_norm_g, m_mla_q_norm_g, m_w_qb, m_mla_kv_norm_g, m_w_kvb, m_w_out, m_final_norm_g, v_norm_g, v_w_in, v_conv_a_w, v_ssd_conv_w, v_ssd_conv_b, v_ssd_dt_bias, v_ssd_a_log, v_ssd_d, v_ssd_norm_g, v_mla_q_norm_g, v_w_qb, v_mla_kv_norm_g, v_w_kvb, v_w_out, v_final_norm_g):
    w = dict(norm_g=norm_g, w_in=w_in, conv_a_w=conv_a_w, ssd_conv_w=ssd_conv_w, ssd_conv_b=ssd_conv_b,
             ssd_dt_bias=ssd_dt_bias, ssd_a_log=ssd_a_log, ssd_d=ssd_d, ssd_norm_g=ssd_norm_g, mla_q_norm_g=mla_q_norm_g,
             w_qb=w_qb, mla_kv_norm_g=mla_kv_norm_g, w_kvb=w_kvb, w_out=w_out, final_norm_g=final_norm_g)
    mom = dict(norm_g=m_norm_g, w_in=m_w_in, conv_a_w=m_conv_a_w, ssd_conv_w=m_ssd_conv_w, ssd_conv_b=m_ssd_conv_b,
               ssd_dt_bias=m_ssd_dt_bias, ssd_a_log=m_ssd_a_log, ssd_d=m_ssd_d, ssd_norm_g=m_ssd_norm_g,
               mla_q_norm_g=m_mla_q_norm_g, w_qb=m_w_qb, mla_kv_norm_g=m_mla_kv_norm_g, w_kvb=m_w_kvb, w_out=m_w_out,
               final_norm_g=m_final_norm_g)
    var = dict(norm_g=v_norm_g, w_in=v_w_in, conv_a_w=v_conv_a_w, ssd_conv_w=v_ssd_conv_w, ssd_conv_b=v_ssd_conv_b,
               ssd_dt_bias=v_ssd_dt_bias, ssd_a_log=v_ssd_a_log, ssd_d=v_ssd_d, ssd_norm_g=v_ssd_norm_g,
               mla_q_norm_g=v_mla_q_norm_g, w_qb=v_w_qb, mla_kv_norm_g=v_mla_kv_norm_g, w_kvb=v_w_kvb, w_out=v_w_out,
               final_norm_g=v_final_norm_g)
    chip = 2 * lax.axis_index("x") + lax.axis_index("y")

    def early_shard(l, zero):
        pack = jnp.pad(conv_a_w[l], ((0, 5), (0, 192))) + jnp.pad(ssd_conv_w[l], ((3, 1), (0, 32)))
        return [(_perm_cols(w_in[l]) + zero).astype(MXU), pack + zero]

    def late_shard(l, zero):
        return [(w_out[l] + zero).astype(MXU), (w_qb[l].T + zero).astype(MXU), (w_kvb[l].T + zero).astype(MXU)]

    def early_weights(l, gathered):
        g_in, g_conv = gathered
        return dict(
            norm_g=norm_g[l][None], w_in=g_in.reshape(D_MODEL, NCOL),
            conv_a_w=jnp.concatenate([g_conv[j, 0:3, 0:64] for j in range(N_CHIPS)], axis=1),
            ssd_conv_w=jnp.concatenate([g_conv[j, 3:7, 0:224] for j in range(N_CHIPS)], axis=1),
            ssd_conv_b=ssd_conv_b[l][None], sc=_ssd_scalars(ssd_dt_bias[l], ssd_a_log[l], ssd_d[l]),
            g_ssd=ssd_norm_g[l][None], gq=mla_q_norm_g[l][None], gkv=mla_kv_norm_g[l][None])

    def late_weights(gathered):
        g_out, g_qb, g_kvb = gathered
        return dict(wq=_wq_layout(g_qb.reshape(MLA_HEADS * 96, Q_LORA)), wkv=_wkv_layout(g_kvb.reshape(MLA_HEADS * LANE, KV_LORA)),
                    w_out=g_out.reshape(D_MODEL, D_MODEL))

    def late_grads(dw_out, dwq, dwkv):
        wq = jnp.pad(_wq_unlayout(dwq).reshape(N_CHIPS, 144, Q_LORA), ((0, 0), (0, 16), (0, 0)))
        return [dw_out, wq.reshape(N_CHIPS * 160, Q_LORA), _wkv_unlayout(dwkv)]

    def large_grads(g):
        return [g["w_in"]] + late_grads(g["w_out"], g["wq"], g["wkv"])

    w_in0, pack0 = early_shard(0, 0.0)
    half = w_in0.shape[0] // 2
    gather_a0 = _xchg_begin("gather_a0", _gather_half_plan, 6, [w_in0.reshape(2, half, NCOL), pack0],
                            [_sds((N_CHIPS, 2, half, NCOL), MXU), _sds((N_CHIPS,) + pack0.shape)])
    zero = gather_a0["token"][0, 0]
    cos, sin = _rope_tables(positions[0] + zero.astype(jnp.int32))
    late0, shards1 = late_shard(0, zero), early_shard(1, zero) + late_shard(1, zero)
    mine0, (g_in0, g_conv0) = _xchg_end(gather_a0, [cos, sin] + late0 + shards1)
    forward_a0 = _xchg_begin("forward_a0", _forward_plan, 3, [g_in0], [])
    gather_b0 = _gather_begin(late0, "b0", [forward_a0["token"]])
    gather_1 = _gather_begin(shards1, "1", [gather_b0["token"]])
    (g_in0,), _ = _xchg_end(forward_a0, [gather_1["token"]])
    lw0 = early_weights(0, [lax.dynamic_update_index_in_dim(g, s_, chip, 0) for g, s_ in zip((g_in0, g_conv0), mine0)])
    x1, sv0, lw0 = _layer_fwd(x[0], lw0, cos, sin, gather_1["token"],
                              lambda ya, y_ssd: late_weights(_gather_end(gather_b0, [ya, y_ssd])))
    g1 = _gather_end(gather_1, [x1])
    (dx, dgf, loss), sv1, lw1 = _layer_fwd(x1, {**early_weights(1, g1[:2]), **late_weights(g1[2:])}, cos, sin,
                                           head=(final_norm_g[None], loss_target[0]))

    dx, lg1, _, _ = _layer_bwd(dx, lw1, sv1, cos, sin)
    grad_x, lg0, red1, rs0_late = _layer_bwd(dx, lw0, sv0, cos, sin, _rs_begin(large_grads(lg1), 1),
                                             lambda *g: _rs_begin(late_grads(*g), "0l"))
    rs0 = _rs_begin([lg0["w_in"]], 0, [rs0_late["h"]["token"]])
    lg = [lg0, lg1]
    grad = {}

    small_names = ["norm_g", "conv_a_w", "ssd_conv_w", "ssd_conv_b", "sc", "g_ssd", "gq", "gkv"]
    parts = [loss[0, 0:1], dgf]
    for nm in small_names:
        parts += [lg[l][nm][:3, DT_LANE:DT_LANE + SSD_HEADS] if nm == "sc" else lg[l][nm] for l in range(DEPTH)]
    shapes = [(1,), (D_MODEL,)] + [(DEPTH,) + shp for shp in ((D_MODEL,), (3, D_CONV_A), (4, N_XBC), (N_XBC,), (3, SSD_HEADS),
                                                              (D_SSD,), (Q_LORA,), (KV_LORA,))]
    red_slab = _allreduce_small(_to_slab(parts, SLAB_ROWS), rs0["h"]["token"])
    rs0 = _rs_add_mine(rs0, [red_slab])
    red = _from_slab(red_slab + rs0["h"]["token"][0, 0], shapes)
    loss_out = red[0][0]
    grad["final_norm_g"] = red[1]
    grad["norm_g"], conv_a_full, sconv_full, grad["ssd_conv_b"], sc_grads = red[2:7]
    grad["ssd_norm_g"], grad["mla_q_norm_g"], grad["mla_kv_norm_g"] = red[7:10]
    grad["conv_a_w"] = lax.dynamic_slice_in_dim(conv_a_full, chip * 64, 64, axis=2)
    grad["ssd_conv_w"] = lax.dynamic_slice_in_dim(sconv_full, chip * 224, 224, axis=2)
    grad["ssd_dt_bias"], grad["ssd_a_log"], grad["ssd_d"] = sc_grads[:, 0], sc_grads[:, 1], sc_grads[:, 2]

    delta, new_m, new_v = {}, {}, {}
    small = [nm for nm in WEIGHTS if nm not in BIG]
    row2 = lambda a: a[None] if a.ndim == 1 else a
    small_out = _adamw(*[[row2(a[nm]) for nm in small] for a in (w, grad, mom, var)], whole=True)
    for nm, (dv, mv, vv) in zip(small, small_out):
        delta[nm], new_m[nm], new_v[nm] = [a.reshape(w[nm].shape) for a in (dv, mv, vv)]

    r_out, r_qb, r_kvb = [jnp.stack([a, b]) for a, b in zip(_rs_end(rs0_late, [red_slab]), red1[1:])]
    late = [nm for nm in BIG if nm != "w_in"]
    view = {nm: (lambda a: a) if nm == "w_out" else (lambda a: jnp.swapaxes(a, 1, 2)) for nm in late}
    late_g = [dict(w_out=r_out, w_qb=r_qb[:, :144], w_kvb=r_kvb)[nm] for nm in late]
    late_out = _adamw(*[[view[nm](a[nm]) for nm in late] for a in (w,)], late_g,
                      *[[view[nm](a[nm]) for nm in late] for a in (mom, var)], whole=False)
    for nm, gv, (dv, mv, vv) in zip(late, late_g, late_out):
        grad[nm], delta[nm], new_m[nm], new_v[nm] = [view[nm](a) for a in (gv, dv, mv, vv)]
    g_in1 = _unperm_cols(red1[0])
    shadow_work = [a for row in small_out + late_out for a in row] + [grad[nm] for nm in small] + [g_in1]
    r_in0, = _rs_end(_rs_add_chips(rs0, shadow_work), [])
    to_cols, from_cols = (lambda a: jnp.transpose(a, (2, 0, 1))), (lambda a: jnp.transpose(a, (1, 2, 0)))
    grad["w_in"], delta["w_in"], new_m["w_in"], new_v["w_in"] = [from_cols(a) for a in _adamw_cols(
        to_cols(w["w_in"]), [_unperm_cols(r_in0), g_in1], to_cols(mom["w_in"]), to_cols(var["w_in"]))]

    return (loss_out, grad_x[None], *[grad[nm] for nm in WEIGHTS], *[delta[nm] for nm in WEIGHTS],
            *[new_m[nm] for nm in WEIGHTS], *[new_v[nm] for nm in WEIGHTS])
```

```python
import functools
import math

import numpy as np
import jax
import jax.numpy as jnp
from jax import lax
from jax.experimental import pallas as pl
from jax.experimental.pallas import tpu as pltpu

F32 = jnp.float32
MXU = jnp.bfloat16

D_MODEL = 1024
DEPTH = 2
D_CONV_A = 256
D_SSD = 384
SSD_HEADS = 6
SSD_BC = 256
SSD_CHUNK = 128
SSD_CHUNKS_PER_STEP = 8
SSD_NORM_EPS = 1e-5
MLA_HEADS = 6
Q_LORA = 256
KV_LORA = 128
QK_NOPE = 64
QK_ROPE = 32
V_DIM = 64
D_MLA = 384
ROPE_BASE = 10000.0
NORM_EPS = 1e-6
IN_COLS = 3110
LANE = 128

O_AH, O_AB, O_AC, O_AZ = 0, 256, 512, 768
O_XBC = 1024
O_SZ = 1920
O_CQA = 2304
O_CKV = 2560
O_CZ = 2688
O_TAIL = 3072
NCOL = 3200
N_XBC = D_SSD + 2 * SSD_BC
DT_LANE = 32
ROPE_LANE = 64

ADAM_LR, ADAM_B1, ADAM_B2, ADAM_EPS, ADAM_WD, ADAM_STEP = 0.001, 0.9, 0.999, 1e-08, 0.01, 10

VMEM_LIMIT = 56 * 1024 * 1024
MESH_T = pl.DeviceIdType.MESH


def _dot(a, b):
    return jnp.dot(a.astype(MXU), b.astype(MXU), preferred_element_type=F32)


def _dot_nt(a, b):
    return lax.dot_general(a.astype(MXU), b.astype(MXU), (((1,), (1,)), ((), ())), preferred_element_type=F32)


def _dot_tn(a, b):
    return lax.dot_general(a.astype(MXU), b.astype(MXU), (((0,), (0,)), ((), ())), preferred_element_type=F32)


def _dot_hi(a, b):
    return jnp.dot(a, b, precision=lax.Precision.HIGHEST, preferred_element_type=F32)


def _dot_hi_tn(a, b):
    return lax.dot_general(a, b, (((0,), (0,)), ((), ())), precision=lax.Precision.HIGHEST, preferred_element_type=F32)


def _sigmoid(z):
    return 1.0 / (1.0 + jnp.exp(-z))


def _silu(z):
    return z * _sigmoid(z)


def _dsilu(z):
    s = _sigmoid(z)
    return s * (1.0 + z * (1.0 - s))


def _softplus(z):
    e = jnp.exp(-jnp.abs(z))
    return jnp.maximum(z, 0.0) + jnp.where(e < 1e-3, e * (1.0 - 0.5 * e), jnp.log(1.0 + e))


def _iota(shape, dim):
    return lax.broadcasted_iota(jnp.int32, shape, dim)


def _shift_down(u, k):
    if k == 0:
        return u
    return jnp.where(_iota(u.shape, 0) >= k, pltpu.roll(u, k, 0), 0.0)


def _shift_up(u, k):
    if k == 0:
        return u
    n = u.shape[0]
    return jnp.where(_iota(u.shape, 0) < n - k, pltpu.roll(u, n - k, 0), 0.0)


def _rope_swap(t):
    lane = _iota(t.shape, 1)
    lo = (lane >= ROPE_LANE) & (lane < ROPE_LANE + 16)
    hi = (lane >= ROPE_LANE + 16) & (lane < ROPE_LANE + 32)
    return jnp.where(lo, pltpu.roll(t, LANE - 16, 1), jnp.where(hi, pltpu.roll(t, 16, 1), 0.0))


def _params(sem=None):
    return pltpu.CompilerParams(dimension_semantics=sem, vmem_limit_bytes=VMEM_LIMIT)


def _full(shape):
    nd = len(shape)
    return pl.BlockSpec(shape, lambda *_: (0,) * nd)


def _sds(shape, dtype=F32):
    return jax.ShapeDtypeStruct(shape, dtype)


def _tile(s):
    return min(512, s)


def _row(ts, w):
    return pl.BlockSpec((ts, w), lambda i: (i, 0))


def _gate_cols(ts, off):
    return pl.BlockSpec((ts, D_SSD), lambda i, _o=off // D_SSD: (i, _o))


def _col(s, off):
    return pl.BlockSpec((s, LANE), lambda j, _o=off // LANE: (0, _o + j))


def _call_after(dep, body, args, *, in_specs, **kw):
    if dep is None:
        return pl.pallas_call(body, in_specs=in_specs, **kw)(*args)
    n = len(args)

    def body_dep(*refs):
        body(*refs[:n], *refs[n + 1:])

    return pl.pallas_call(body_dep, in_specs=list(in_specs) + [pl.BlockSpec(memory_space=pl.ANY)], **kw)(*args, dep)


def _rms(c, g):
    r = lax.rsqrt(jnp.mean(c * c, axis=-1, keepdims=True) + NORM_EPS)
    return c * r * g, r


def _rms_bwd(dn, c, r, g):
    ch = c * r
    dch = dn * g
    dc = r * (dch - ch * jnp.mean(dch * ch, axis=-1, keepdims=True))
    return dc, jnp.sum(dn * ch, axis=0, keepdims=True)


def _inproj_fwd(x, g, w, dep=None):
    s = x.shape[0]
    ts = _tile(s)

    def body(x_ref, g_ref, w_ref, proj_ref, h_ref, r_ref):
        hn, r = _rms(x_ref[...], g_ref[...])
        h = hn.astype(MXU)
        h_ref[...] = h
        r_ref[...] = r
        proj_ref[...] = jnp.dot(h, w_ref[...], preferred_element_type=F32)

    return _call_after(
        dep, body, (x, g, w), name="inproj_fwd", grid=(s // ts,),
        in_specs=[_row(ts, D_MODEL), _full((1, D_MODEL)), _full((D_MODEL, NCOL))],
        out_specs=[_row(ts, NCOL), _row(ts, D_MODEL), _row(ts, 1)],
        out_shape=[_sds((s, NCOL)), _sds((s, D_MODEL), MXU), _sds((s, 1))],
        compiler_params=_params(("parallel",)),
    )


def _conva_fwd(proj, w):
    s = proj.shape[0]

    def body(h_ref, b_ref, c_ref, z_ref, w_ref, y_ref):
        u = c_ref[...] * h_ref[...]
        wv = w_ref[...]
        cv = wv[2:3, :] * u + wv[1:2, :] * _shift_down(u, 1) + wv[0:1, :] * _shift_down(u, 2)
        y_ref[...] = b_ref[...] * cv * _silu(z_ref[...])

    return pl.pallas_call(
        body, name="conva_fwd", grid=(D_CONV_A // LANE,),
        in_specs=[_col(s, O_AH), _col(s, O_AB), _col(s, O_AC), _col(s, O_AZ), pl.BlockSpec((3, LANE), lambda j: (0, j))],
        out_specs=pl.BlockSpec((s, LANE), lambda j: (0, j)),
        out_shape=_sds((s, D_CONV_A)),
        compiler_params=_params(("parallel",)),
    )(proj, proj, proj, proj, w)


def _sconv_pre(u, wv, bv):
    return (wv[3:4, :] * u + wv[2:3, :] * _shift_down(u, 1) + wv[1:2, :] * _shift_down(u, 2)
            + wv[0:1, :] * _shift_down(u, 3) + bv)


def _sconv_fwd(proj, w, b):
    s = proj.shape[0]

    def body(u_ref, w_ref, b_ref, o_ref):
        o_ref[...] = _silu(_sconv_pre(u_ref[...], w_ref[...], b_ref[...]))

    return pl.pallas_call(
        body, name="sconv_fwd", grid=(N_XBC // LANE,),
        in_specs=[_col(s, O_XBC), pl.BlockSpec((4, LANE), lambda j: (0, j)), pl.BlockSpec((1, LANE), lambda j: (0, j))],
        out_specs=pl.BlockSpec((s, LANE), lambda j: (0, j)),
        out_shape=_sds((s, N_XBC)),
        compiler_params=_params(("parallel",)),
    )(proj, w, b)


def _ssd_chunk_common(tail, sc):
    l = SSD_CHUNK
    lane = _iota((l, LANE), 1)
    row = _iota((l, LANE), 0)
    tri = (row >= lane).astype(F32)
    a_row = -jnp.exp(sc[1:2, :])
    pre = tail + sc[0:1, :]
    dt = _softplus(pre)
    a_cs = _dot_hi(tri, dt * a_row)
    return lane, row, tri, a_row, pre, dt, a_cs, a_cs.T


def _pick_col(m, lane, k):
    return jnp.sum(jnp.where(lane == k, m, 0.0), axis=1, keepdims=True)


def _pick_row(m, row, k):
    return jnp.sum(jnp.where(row == k, m, 0.0), axis=0, keepdims=True)


def _ssd_fwd(xbc, proj, sc):
    s = xbc.shape[0]
    nc = s // SSD_CHUNK
    l = SSD_CHUNK
    cps = SSD_CHUNKS_PER_STEP

    def body(xbc_ref, tail_ref, sc_ref, y_ref, st_ref, state):
        @pl.when(pl.program_id(0) == 0)
        def _():
            state[...] = jnp.zeros_like(state)

        sc_v = sc_ref[...]
        lane1 = _iota((1, LANE), 1)
        rowp = _iota((LANE, 1), 0)
        d_row = sc_v[2:3, :]
        states = [state[j] for j in range(3)]
        for u in range(cps):
            r = slice(u * l, (u + 1) * l)
            lane, row, _, _, _, dt, a_cs, a_t = _ssd_chunk_common(tail_ref[r, :], sc_v)
            for j in range(3):
                st_ref[u, j] = states[j]
            for j in range(3):
                xpair = xbc_ref[r, LANE * j:LANE * (j + 1)]
                sp = states[j]
                ypair = jnp.zeros((l, LANE), F32)
                new_s = jnp.zeros((LANE, LANE), F32)
                decay = jnp.zeros((LANE, 1), F32)
                for half in range(2):
                    h = 2 * j + half
                    g = h // 3
                    hm = (lane < 64) if half == 0 else (lane >= 64)
                    hrow = (rowp < 64) if half == 0 else (rowp >= 64)
                    ac = _pick_col(a_cs, lane, DT_LANE + h)
                    ar = _pick_row(a_t, row, DT_LANE + h)
                    dtc = _pick_col(dt, lane, DT_LANE + h)
                    alast = jnp.sum(jnp.where(lane1 == l - 1, ar, 0.0), axis=1, keepdims=True)
                    dh = jnp.sum(jnp.where(lane1 == DT_LANE + h, d_row, 0.0), axis=1, keepdims=True)
                    xm = jnp.where(hm, xpair, 0.0)
                    xd = xm * dtc
                    bm = xbc_ref[r, D_SSD + LANE * g:D_SSD + LANE * (g + 1)]
                    cm = xbc_ref[r, D_SSD + SSD_BC + LANE * g:D_SSD + SSD_BC + LANE * (g + 1)]
                    lm = jnp.where(row >= lane, jnp.exp(jnp.minimum(ac - ar, 0.0)), 0.0)
                    y_diag = _dot(_dot_nt(cm, bm) * lm, xd)
                    y_off = jnp.where(hm, _dot_nt(cm, sp), 0.0) * jnp.exp(ac)
                    ypair = ypair + y_diag + y_off + xm * dh
                    new_s = new_s + _dot_tn(xd * jnp.exp(alast - ac), bm)
                    decay = jnp.where(hrow, jnp.exp(alast), decay)
                states[j] = sp * decay + new_s
                y_ref[r, LANE * j:LANE * (j + 1)] = ypair
        for j in range(3):
            state[j] = states[j]

    return pl.pallas_call(
        body, name="ssd_fwd", grid=(nc // cps,),
        in_specs=[pl.BlockSpec((cps * l, N_XBC), lambda c: (c, 0)),
                  pl.BlockSpec((cps * l, LANE), lambda c: (c, O_TAIL // LANE)), _full((8, LANE))],
        out_specs=[pl.BlockSpec((cps * l, D_SSD), lambda c: (c, 0)), pl.BlockSpec((cps, 3, LANE, LANE), lambda c: (c, 0, 0, 0))],
        out_shape=[_sds((s, D_SSD)), _sds((nc, 3, LANE, LANE))],
        scratch_shapes=[pltpu.VMEM((3, LANE, LANE), F32)],
        compiler_params=_params(("arbitrary",)),
    )(xbc, proj, sc)


def _mla_prep_fwd(proj, gq, gkv, wq, wkv, cos, sin):
    s = proj.shape[0]
    ts = _tile(s)
    nh = MLA_HEADS

    def body(cqa_ref, ckv_ref, tail_ref, gq_ref, gkv_ref, wq_ref, wkv_ref, cos_ref, sin_ref,
             q_ref, k_ref, v_ref, qn_ref, kvn_ref, rq_ref, rkv_ref):
        qn, rq = _rms(cqa_ref[...], gq_ref[...])
        kvn, rkv = _rms(ckv_ref[...], gkv_ref[...])
        qn = qn.astype(MXU)
        kvn = kvn.astype(MXU)
        qn_ref[...] = qn
        kvn_ref[...] = kvn
        rq_ref[...] = rq
        rkv_ref[...] = rkv
        q = _dot_nt(qn, wq_ref[...])
        kv = _dot_nt(kvn, wkv_ref[...])
        cosv = cos_ref[...]
        sinv = sin_ref[...]
        lane = _iota((ts, LANE), 1)
        rope_lanes = (lane >= ROPE_LANE) & (lane < ROPE_LANE + QK_ROPE)
        kr = jnp.where(rope_lanes, pltpu.roll(tail_ref[...], ROPE_LANE, 1), 0.0)
        kr = kr * cosv + _rope_swap(kr) * sinv
        for h in range(nh):
            qh = q[:, LANE * h:LANE * (h + 1)]
            q_ref[h] = ((qh * cosv + _rope_swap(qh) * sinv) * ATT_SCALE).astype(MXU)
            k_ref[h] = (kv[:, LANE * h:LANE * (h + 1)] + kr).astype(MXU)
            v_ref[h] = kv[:, LANE * (nh + h):LANE * (nh + h + 1)].astype(MXU)

    head = pl.BlockSpec((nh, ts, LANE), lambda i: (0, i, 0))
    return pl.pallas_call(
        body, name="mla_prep_fwd", grid=(s // ts,),
        in_specs=[pl.BlockSpec((ts, Q_LORA), lambda i: (i, O_CQA // Q_LORA)),
                  pl.BlockSpec((ts, KV_LORA), lambda i: (i, O_CKV // KV_LORA)),
                  pl.BlockSpec((ts, LANE), lambda i: (i, O_TAIL // LANE)),
                  _full((1, Q_LORA)), _full((1, KV_LORA)), _full((nh * LANE, Q_LORA)), _full((2 * nh * LANE, KV_LORA)),
                  _row(ts, LANE), _row(ts, LANE)],
        out_specs=[head, head, head, _row(ts, Q_LORA), _row(ts, KV_LORA), _row(ts, 1), _row(ts, 1)],
        out_shape=[_sds((nh, s, LANE), MXU)] * 3 + [_sds((s, Q_LORA), MXU), _sds((s, KV_LORA), MXU), _sds((s, 1)), _sds((s, 1))],
        compiler_params=_params(("parallel",)),
    )(proj, proj, proj, gq, gkv, wq, wkv, cos, sin)


ATT_SCALE = (QK_NOPE + QK_ROPE) ** -0.5
NEG = -1e30


def _att_tile(s, most):
    return min(most, s // 2)


ATT_FWD_TILE = 1024
ATT_BWD_TILE = 1024


def _attn_fwd(q, k, v):
    nh, s, _ = q.shape
    tq = _att_tile(s, ATT_FWD_TILE)
    nq = s // tq

    def body(q_ref, k_ref, v_ref, o_ref, lse_ref):
        i = pl.program_id(1)
        rowi = _iota((tq, tq), 0)
        coli = _iota((tq, tq), 1)
        zero = (jnp.full((tq, 1), NEG, F32), jnp.zeros((tq, 1), F32), jnp.zeros((tq, LANE), F32))
        state = [zero, zero]
        done = [zero, zero]
        for t in range(nq + 1):
            first = t <= i
            qblk = jnp.where(first, i, nq - 1 - i)
            kblk = jnp.where(first, t, t - i - 1)
            qoff = pl.multiple_of(qblk * tq, tq)
            koff = pl.multiple_of(kblk * tq, tq)
            keep = coli <= rowi + jnp.where(kblk == qblk, 0, tq)
            restart = t == i + 1
            for hh in range(2):
                m, lsum, acc = state[hh]
                if t > 0:
                    done[hh] = tuple(jnp.where(restart, a, b) for a, b in zip(state[hh], done[hh]))
                    m = jnp.where(restart, NEG, m)
                    lsum = jnp.where(restart, 0.0, lsum)
                    acc = jnp.where(restart, 0.0, acc)
                sc = _dot_nt(q_ref[hh, pl.ds(qoff, tq), :], k_ref[hh, pl.ds(koff, tq), :])
                sc = jnp.where(keep, sc, NEG)
                m_new = jnp.maximum(m, jnp.max(sc, axis=1, keepdims=True))
                p = jnp.exp(sc - m_new)
                alpha = jnp.exp(m - m_new)
                lsum = alpha * lsum + jnp.sum(p, axis=1, keepdims=True)
                acc = alpha * acc + _dot(p, v_ref[hh, pl.ds(koff, tq), :])
                state[hh] = (m_new, lsum, acc)
        for blk, res in ((i, done), (nq - 1 - i, state)):
            off = pl.multiple_of(blk * tq, tq)
            out = None
            for hh in range(2):
                m, lsum, acc = res[hh]
                o = acc * (1.0 / lsum)
                lse_ref[hh, pl.ds(off, tq), :] = m + jnp.log(lsum)
                out = o if hh == 0 else out + pltpu.roll(o, V_DIM, 1)
            o_ref[pl.ds(off, tq), :] = out

    pair = pl.BlockSpec((2, s, LANE), lambda j, i: (j, 0, 0))
    return pl.pallas_call(
        body, name="attn_fwd", grid=(nh // 2, nq // 2),
        in_specs=[pair, pair, pair],
        out_specs=[pl.BlockSpec((s, LANE), lambda j, i: (0, j)), pl.BlockSpec((2, s, 1), lambda j, i: (j, 0, 0))],
        out_shape=[_sds((s, D_MLA)), _sds((nh, s, 1))],
        compiler_params=_params(("parallel", "arbitrary")),
    )(q, k, v)


def _ssd_gate(y_ssd, s_z, g):
    yz = y_ssd * _silu(s_z)
    g0 = _iota(yz.shape, 1) < D_SSD // 2
    sq = yz * yz
    ms0 = jnp.sum(jnp.where(g0, sq, 0.0), axis=1, keepdims=True) / (D_SSD // 2)
    ms1 = jnp.sum(jnp.where(g0, 0.0, sq), axis=1, keepdims=True) / (D_SSD // 2)
    r = jnp.where(g0, lax.rsqrt(ms0 + SSD_NORM_EPS), lax.rsqrt(ms1 + SSD_NORM_EPS))
    nrm = yz * r
    return nrm * g, nrm, r, g0


def _outproj_fwd(x, proj, ya, y_ssd, o, g_ssd, w):
    s = x.shape[0]
    ts = _tile(s)

    def body(x_ref, sz_ref, cz_ref, ya_ref, ys_ref, o_ref, g_ref, w_ref, xo_ref, y_ref):
        yb = _ssd_gate(ys_ref[...], sz_ref[...], g_ref[...])[0]
        yc = o_ref[...] * _silu(cz_ref[...])
        y = jnp.concatenate([ya_ref[...], yb, yc], axis=1).astype(MXU)
        y_ref[...] = y
        xo_ref[...] = x_ref[...] + jnp.dot(y, w_ref[...], preferred_element_type=F32)

    return pl.pallas_call(
        body, name="outproj_fwd", grid=(s // ts,),
        in_specs=[_row(ts, D_MODEL), _gate_cols(ts, O_SZ), _gate_cols(ts, O_CZ), _row(ts, D_CONV_A), _row(ts, D_SSD),
                  _row(ts, D_MLA), _full((1, D_SSD)), _full((D_MODEL, D_MODEL))],
        out_specs=[_row(ts, D_MODEL), _row(ts, D_MODEL)],
        out_shape=[_sds((s, D_MODEL)), _sds((s, D_MODEL), MXU)],
        compiler_params=_params(("parallel",)),
    )(x, proj, proj, ya, y_ssd, o, g_ssd, w)


def _outproj_loss(x, proj, ya, y_ssd, o, g_ssd, w, final_g, tgt):
    s = x.shape[0]
    ts = _tile(s)

    def body(x_ref, sz_ref, cz_ref, ya_ref, ys_ref, o_ref, g_ref, w_ref, fg_ref, t_ref, dx_ref, dg_ref, loss_ref, y_ref):
        @pl.when(pl.program_id(0) == 0)
        def _():
            dg_ref[...] = jnp.zeros_like(dg_ref)
            loss_ref[...] = jnp.zeros_like(loss_ref)

        yb = _ssd_gate(ys_ref[...], sz_ref[...], g_ref[...])[0]
        yc = o_ref[...] * _silu(cz_ref[...])
        y = jnp.concatenate([ya_ref[...], yb, yc], axis=1).astype(MXU)
        y_ref[...] = y
        xv = x_ref[...] + jnp.dot(y, w_ref[...], preferred_element_type=F32)
        gv = fg_ref[...]
        yn, r = _rms(xv, gv)
        e = yn - t_ref[...]
        loss_ref[...] += jnp.sum(jnp.sum(e * e, axis=1, keepdims=True), axis=0, keepdims=True) * (0.5 / D_MODEL)
        dx, dg = _rms_bwd(e * (1.0 / D_MODEL), xv, r, gv)
        dx_ref[...] = dx
        dg_ref[...] += dg

    return pl.pallas_call(
        body, name="outproj_loss", grid=(s // ts,),
        in_specs=[_row(ts, D_MODEL), _gate_cols(ts, O_SZ), _gate_cols(ts, O_CZ), _row(ts, D_CONV_A), _row(ts, D_SSD),
                  _row(ts, D_MLA), _full((1, D_SSD)), _full((D_MODEL, D_MODEL)), _full((1, D_MODEL)), _row(ts, D_MODEL)],
        out_specs=[_row(ts, D_MODEL), _full((1, D_MODEL)), _full((1, LANE)), _row(ts, D_MODEL)],
        out_shape=[_sds((s, D_MODEL)), _sds((1, D_MODEL)), _sds((1, LANE)), _sds((s, D_MODEL), MXU)],
        compiler_params=_params(("arbitrary",)),
    )(x, proj, proj, ya, y_ssd, o, g_ssd, w, final_g, tgt)


def _outproj_bwd(dout, y, w, proj, y_ssd, o, g_ssd, dep=None):
    s = dout.shape[0]
    ts = _tile(s)

    def body(dout_ref, y_ref, w_ref, sz_ref, cz_ref, ys_ref, o_ref, g_ref,
             dya_ref, dys_ref, dsz_ref, dattn_ref, dcz_ref, dg_ref, dw_ref):
        @pl.when(pl.program_id(0) == 0)
        def _():
            dw_ref[...] = jnp.zeros_like(dw_ref)
            dg_ref[...] = jnp.zeros_like(dg_ref)

        dout_b = dout_ref[...].astype(MXU)
        dw_ref[...] += _dot_tn(y_ref[...], dout_b)
        dy = _dot_nt(dout_b, w_ref[...])
        dya_ref[...] = dy[:, :D_CONV_A]
        dyb = dy[:, D_CONV_A:D_CONV_A + D_SSD]
        sz = sz_ref[...]
        ys = ys_ref[...]
        gv = g_ref[...]
        _, nrm, r, g0 = _ssd_gate(ys, sz, gv)
        dg_ref[...] += jnp.sum(dyb * nrm, axis=0, keepdims=True)
        dn = dyb * gv
        t = dn * nrm
        mean = jnp.where(g0, jnp.sum(jnp.where(g0, t, 0.0), axis=1, keepdims=True),
                         jnp.sum(jnp.where(g0, 0.0, t), axis=1, keepdims=True)) / (D_SSD // 2)
        dyz = r * (dn - nrm * mean)
        dys_ref[...] = dyz * _silu(sz)
        dsz_ref[...] = (dyz * ys * _dsilu(sz)).astype(MXU)
        dyc = dy[:, D_CONV_A + D_SSD:]
        cz = cz_ref[...]
        dattn_ref[...] = dyc * _silu(cz)
        dcz_ref[...] = (dyc * o_ref[...] * _dsilu(cz)).astype(MXU)

    return _call_after(
        dep, body, (dout, y, w, proj, proj, y_ssd, o, g_ssd), name="outproj_bwd", grid=(s // ts,),
        in_specs=[_row(ts, D_MODEL), _row(ts, D_MODEL), _full((D_MODEL, D_MODEL)), _gate_cols(ts, O_SZ), _gate_cols(ts, O_CZ),
                  _row(ts, D_SSD), _row(ts, D_MLA), _full((1, D_SSD))],
        out_specs=[_row(ts, D_CONV_A), _row(ts, D_SSD), _row(ts, D_SSD), _row(ts, D_MLA), _row(ts, D_MLA),
                   _full((1, D_SSD)), _full((D_MODEL, D_MODEL))],
        out_shape=[_sds((s, D_CONV_A)), _sds((s, D_SSD)), _sds((s, D_SSD), MXU), _sds((s, D_MLA)), _sds((s, D_MLA), MXU),
                   _sds((1, D_SSD)), _sds((D_MODEL, D_MODEL))],
        compiler_params=_params(("arbitrary",)),
    )


def _attn_bwd(q, k, v, o, d_o, lse, dep=None):
    nh, s, _ = q.shape
    tq = _att_tile(s, ATT_BWD_TILE)
    nq = s // tq

    def body(q_ref, k_ref, v_ref, o_ref, do_ref, lse_ref, dq_ref, dk_ref, dv_ref, dop, delta):
        i = pl.program_id(1)

        @pl.when(i == 0)
        def _():
            lane = _iota((s, LANE), 1)
            for hh in range(2):
                dov = do_ref[...]
                ov = o_ref[...]
                if hh == 1:
                    dov = pltpu.roll(dov, V_DIM, 1)
                    ov = pltpu.roll(ov, V_DIM, 1)
                dov = jnp.where(lane < V_DIM, dov, 0.0)
                dop[hh] = dov.astype(MXU)
                delta[hh] = jnp.sum(dov * ov, axis=1, keepdims=True)
                dq_ref[hh] = jnp.zeros((s, LANE), F32)

        rowi = _iota((tq, tq), 0)
        coli = _iota((tq, tq), 1)
        z = jnp.zeros((tq, LANE), F32)
        state = [(z, z), (z, z)]
        done = [(z, z), (z, z)]
        for t in range(nq + 1):
            first = t <= nq - 1 - i
            kblk = jnp.where(first, i, nq - 1 - i)
            qblk = jnp.where(first, i + t, t - 1)
            qoff = pl.multiple_of(qblk * tq, tq)
            koff = pl.multiple_of(kblk * tq, tq)
            keep = coli <= rowi + jnp.where(kblk == qblk, 0, tq)
            restart = t == nq - i
            for hh in range(2):
                dk, dv = state[hh]
                if t > 0:
                    done[hh] = tuple(jnp.where(restart, a, b) for a, b in zip(state[hh], done[hh]))
                    dk = jnp.where(restart, 0.0, dk)
                    dv = jnp.where(restart, 0.0, dv)
                kb = k_ref[hh, pl.ds(koff, tq), :]
                qb = q_ref[hh, pl.ds(qoff, tq), :]
                dob = dop[hh, pl.ds(qoff, tq), :]
                sc = jnp.where(keep, _dot_nt(qb, kb), NEG)
                p = jnp.exp(sc - lse_ref[hh, pl.ds(qoff, tq), :])
                dp = _dot_nt(dob, v_ref[hh, pl.ds(koff, tq), :])
                ds = p * (dp - delta[hh, pl.ds(qoff, tq), :])
                dq_ref[hh, pl.ds(qoff, tq), :] += _dot(ds, kb)
                state[hh] = (dk + _dot_tn(ds, qb), dv + _dot_tn(p, dob))
        for blk, res in ((i, done), (nq - 1 - i, state)):
            off = pl.multiple_of(blk * tq, tq)
            for hh in range(2):
                dk_ref[hh, pl.ds(off, tq), :] = res[hh][0]
                dv_ref[hh, pl.ds(off, tq), :] = res[hh][1]

    pair = pl.BlockSpec((2, s, LANE), lambda j, i: (j, 0, 0))
    return _call_after(
        dep, body, (q, k, v, o, d_o, lse), name="attn_bwd", grid=(nh // 2, nq // 2),
        in_specs=[pair, pair, pair, pl.BlockSpec((s, LANE), lambda j, i: (0, j)), pl.BlockSpec((s, LANE), lambda j, i: (0, j)),
                  pl.BlockSpec((2, s, 1), lambda j, i: (j, 0, 0))],
        out_specs=[pair, pair, pair],
        out_shape=[_sds((nh, s, LANE))] * 3,
        scratch_shapes=[pltpu.VMEM((2, s, LANE), MXU), pltpu.VMEM((2, s, 1), F32)],
        compiler_params=_params(("parallel", "arbitrary")),
    )


def _ssd_bwd(xbc, proj, sc, states, dy, dep=None):
    s = xbc.shape[0]
    nc = s // SSD_CHUNK
    l = SSD_CHUNK
    cps = SSD_CHUNKS_PER_STEP

    def body(xbc_ref, tail_ref, sc_ref, st_ref, dy_ref, dxbc_ref, dtail_ref, dsc_ref, dstate):
        @pl.when(pl.program_id(0) == 0)
        def _():
            dstate[...] = jnp.zeros_like(dstate)
            dsc_ref[...] = jnp.zeros_like(dsc_ref)

        sc_v = sc_ref[...]
        lane1 = _iota((1, LANE), 1)
        rowp = _iota((LANE, 1), 0)
        rowl = _iota((l, 1), 0)
        d_row = sc_v[2:3, :]
        dstates = [dstate[j] for j in range(3)]
        for u in reversed(range(cps)):
            dstates = chunk(u, xbc_ref, tail_ref, sc_v, st_ref, dy_ref, dxbc_ref, dtail_ref, dsc_ref, dstates,
                            lane1, rowp, rowl, d_row)
        for j in range(3):
            dstate[j] = dstates[j]

    def chunk(u, xbc_ref, tail_ref, sc_v, st_ref, dy_ref, dxbc_ref, dtail_ref, dsc_ref, dstates, lane1, rowp, rowl, d_row):
        r = slice(u * l, (u + 1) * l)
        dstates = list(dstates)
        lane, row, tri, a_row, pre, dt, a_cs, a_t = _ssd_chunk_common(tail_ref[r, :], sc_v)
        da_col = jnp.zeros((l, LANE), F32)
        da_row = jnp.zeros((LANE, l), F32)
        dt_x = jnp.zeros((l, LANE), F32)
        dd_row = jnp.zeros((1, LANE), F32)
        db = [jnp.zeros((l, LANE), F32), jnp.zeros((l, LANE), F32)]
        dc = [jnp.zeros((l, LANE), F32), jnp.zeros((l, LANE), F32)]
        for j in range(3):
            xpair = xbc_ref[r, LANE * j:LANE * (j + 1)]
            dypair = dy_ref[r, LANE * j:LANE * (j + 1)]
            sp = st_ref[u, j]
            dsp = dstates[j]
            dxpair = jnp.zeros((l, LANE), F32)
            ds_new = jnp.zeros((LANE, LANE), F32)
            decay = jnp.zeros((LANE, 1), F32)
            for half in range(2):
                h = 2 * j + half
                g = h // 3
                hm = (lane < 64) if half == 0 else (lane >= 64)
                hrow = (rowp < 64) if half == 0 else (rowp >= 64)
                ac = _pick_col(a_cs, lane, DT_LANE + h)
                ar = _pick_row(a_t, row, DT_LANE + h)
                dtc = _pick_col(dt, lane, DT_LANE + h)
                alast = jnp.sum(jnp.where(lane1 == l - 1, ar, 0.0), axis=1, keepdims=True)
                dh = jnp.sum(jnp.where(lane1 == DT_LANE + h, d_row, 0.0), axis=1, keepdims=True)
                xm = jnp.where(hm, xpair, 0.0)
                xd = xm * dtc
                dym = jnp.where(hm, dypair, 0.0)
                bm = xbc_ref[r, D_SSD + LANE * g:D_SSD + LANE * (g + 1)]
                cm = xbc_ref[r, D_SSD + SSD_BC + LANE * g:D_SSD + SSD_BC + LANE * (g + 1)]
                lm = jnp.where(row >= lane, jnp.exp(jnp.minimum(ac - ar, 0.0)), 0.0)
                e_in = jnp.exp(ac)
                f_out = jnp.exp(alast - ac)
                e_last = jnp.exp(alast)
                m = _dot_nt(cm, bm) * lm
                y_off = jnp.where(hm, _dot_nt(cm, sp), 0.0) * e_in
                dm = _dot_nt(dym, xd)
                dxd = _dot_tn(m, dym)
                dg = dm * lm
                dye = dym * e_in
                dc[g] = dc[g] + _dot(dg, bm) + _dot(dye, sp)
                db[g] = db[g] + _dot_tn(dg, cm)
                qm = dm * m
                dac = jnp.sum(qm, axis=1, keepdims=True) + jnp.sum(dym * y_off, axis=1, keepdims=True)
                dar = -jnp.sum(qm, axis=0, keepdims=True)
                dxf = jnp.where(hm, _dot_nt(bm, dsp), 0.0)
                db[g] = db[g] + _dot(xd * f_out, dsp)
                dxd = dxd + dxf * f_out
                df = jnp.sum(dxf * xd, axis=1, keepdims=True) * f_out
                dac = dac - df
                s_last = jnp.sum(df, axis=0, keepdims=True)
                ss = jnp.sum(jnp.where(hrow, dsp * sp, 0.0), axis=1, keepdims=True)
                s_last = s_last + e_last * jnp.sum(ss, axis=0, keepdims=True)
                dac = dac + jnp.where(rowl == l - 1, s_last, 0.0)
                ds_new = ds_new + _dot_tn(dye, cm)
                decay = jnp.where(hrow, e_last, decay)
                dxpair = dxpair + dxd * dtc + dym * dh
                dt_x = dt_x + jnp.where(lane == DT_LANE + h, jnp.sum(dxd * xm, axis=1, keepdims=True), 0.0)
                dsum = jnp.sum(jnp.sum(dym * xm, axis=1, keepdims=True), axis=0, keepdims=True)
                dd_row = dd_row + jnp.where(lane1 == DT_LANE + h, dsum, 0.0)
                da_col = da_col + jnp.where(lane == DT_LANE + h, dac, 0.0)
                da_row = da_row + jnp.where(row == DT_LANE + h, dar, 0.0)
            dstates[j] = dsp * decay + ds_new
            dxbc_ref[r, LANE * j:LANE * (j + 1)] = dxpair
        for g in range(2):
            dxbc_ref[r, D_SSD + LANE * g:D_SSD + LANE * (g + 1)] = db[g]
            dxbc_ref[r, D_SSD + SSD_BC + LANE * g:D_SSD + SSD_BC + LANE * (g + 1)] = dc[g]
        dla = _dot_hi_tn(tri, da_col + da_row.T)
        ddt = dt_x + dla * a_row
        dpre = ddt * _sigmoid(pre)
        dtm = (lane >= DT_LANE) & (lane < DT_LANE + SSD_HEADS)
        dtail_ref[r, :] = jnp.where(dtm, dpre, 0.0).astype(MXU)
        dtm1 = (lane1 >= DT_LANE) & (lane1 < DT_LANE + SSD_HEADS)
        dsc_ref[0:1, :] += jnp.where(dtm1, jnp.sum(dpre, axis=0, keepdims=True), 0.0)
        dsc_ref[1:2, :] += jnp.where(dtm1, jnp.sum(dla * dt, axis=0, keepdims=True) * a_row, 0.0)
        dsc_ref[2:3, :] += dd_row
        return dstates

    rev = lambda c: nc // cps - 1 - c
    return _call_after(
        dep, body, (xbc, proj, sc, states, dy), name="ssd_bwd", grid=(nc // cps,),
        in_specs=[pl.BlockSpec((cps * l, N_XBC), lambda c: (rev(c), 0)),
                  pl.BlockSpec((cps * l, LANE), lambda c: (rev(c), O_TAIL // LANE)), _full((8, LANE)),
                  pl.BlockSpec((cps, 3, LANE, LANE), lambda c: (rev(c), 0, 0, 0)),
                  pl.BlockSpec((cps * l, D_SSD), lambda c: (rev(c), 0))],
        out_specs=[pl.BlockSpec((cps * l, N_XBC), lambda c: (rev(c), 0)), pl.BlockSpec((cps * l, LANE), lambda c: (rev(c), 0)),
                   _full((8, LANE))],
        out_shape=[_sds((s, N_XBC)), _sds((s, LANE), MXU), _sds((8, LANE))],
        scratch_shapes=[pltpu.VMEM((3, LANE, LANE), F32)],
        compiler_params=_params(("arbitrary",)),
    )


def _sconv_bwd(proj, w, b, dxbc, dep=None):
    s = proj.shape[0]

    def body(u_ref, w_ref, b_ref, d_ref, du_ref, dw_ref, db_ref):
        u = u_ref[...]
        wv = w_ref[...]
        dpre = d_ref[...] * _dsilu(_sconv_pre(u, wv, b_ref[...]))
        ahead = [_shift_up(dpre, j) for j in range(4)]
        du_ref[...] = (wv[3:4, :] * ahead[0] + wv[2:3, :] * ahead[1] + wv[1:2, :] * ahead[2]
                       + wv[0:1, :] * ahead[3]).astype(MXU)
        for k in range(4):
            dw_ref[k:k + 1, :] = jnp.sum(ahead[3 - k] * u, axis=0, keepdims=True)
        db_ref[...] = jnp.sum(dpre, axis=0, keepdims=True)

    blk = pl.BlockSpec((s, LANE), lambda j: (0, j))
    return _call_after(
        dep, body, (proj, w, b, dxbc), name="sconv_bwd", grid=(N_XBC // LANE,),
        in_specs=[_col(s, O_XBC), pl.BlockSpec((4, LANE), lambda j: (0, j)), pl.BlockSpec((1, LANE), lambda j: (0, j)), blk],
        out_specs=[blk, pl.BlockSpec((4, LANE), lambda j: (0, j)), pl.BlockSpec((1, LANE), lambda j: (0, j))],
        out_shape=[_sds((s, N_XBC), MXU), _sds((4, N_XBC)), _sds((1, N_XBC))],
        compiler_params=_params(("parallel",)),
    )


def _conva_bwd(proj, w, dya, dep=None):
    s = proj.shape[0]

    def body(h_ref, b_ref, c_ref, z_ref, w_ref, d_ref, da_ref, dw_ref):
        ah, ab, acv, az = h_ref[...], b_ref[...], c_ref[...], z_ref[...]
        wv = w_ref[...]
        u = acv * ah
        cv = wv[2:3, :] * u + wv[1:2, :] * _shift_down(u, 1) + wv[0:1, :] * _shift_down(u, 2)
        dy = d_ref[...]
        sz = _silu(az)
        da_ref[1] = (dy * cv * sz).astype(MXU)
        da_ref[3] = (dy * ab * cv * _dsilu(az)).astype(MXU)
        dcv = dy * ab * sz
        ahead = [_shift_up(dcv, j) for j in range(3)]
        du = wv[2:3, :] * ahead[0] + wv[1:2, :] * ahead[1] + wv[0:1, :] * ahead[2]
        da_ref[0] = (du * acv).astype(MXU)
        da_ref[2] = (du * ah).astype(MXU)
        for k in range(3):
            dw_ref[k:k + 1, :] = jnp.sum(ahead[2 - k] * u, axis=0, keepdims=True)

    return _call_after(
        dep, body, (proj, proj, proj, proj, w, dya), name="conva_bwd", grid=(D_CONV_A // LANE,),
        in_specs=[_col(s, O_AH), _col(s, O_AB), _col(s, O_AC), _col(s, O_AZ), pl.BlockSpec((3, LANE), lambda j: (0, j)),
                  pl.BlockSpec((s, LANE), lambda j: (0, j))],
        out_specs=[pl.BlockSpec((4, s, LANE), lambda j: (0, 0, j)), pl.BlockSpec((3, LANE), lambda j: (0, j))],
        out_shape=[_sds((4, s, D_CONV_A), MXU), _sds((3, D_CONV_A))],
        compiler_params=_params(("parallel",)),
    )


def _mla_prep_bwd(dq, dk, dv, proj, qn, kvn, rq, rkv, gq, gkv, wq, wkv, cos, sin):
    s = proj.shape[0]
    ts = _tile(s)
    nh = MLA_HEADS

    def body(dq_ref, dk_ref, dv_ref, cqa_ref, ckv_ref, qn_ref, kvn_ref, rq_ref, rkv_ref, gq_ref, gkv_ref,
             wq_ref, wkv_ref, cos_ref, sin_ref, dcqa_ref, dckv_ref, dtail_ref, dwq_ref, dwkv_ref, dgq_ref, dgkv_ref):
        @pl.when(pl.program_id(0) == 0)
        def _():
            dwq_ref[...] = jnp.zeros_like(dwq_ref)
            dwkv_ref[...] = jnp.zeros_like(dwkv_ref)
            dgq_ref[...] = jnp.zeros_like(dgq_ref)
            dgkv_ref[...] = jnp.zeros_like(dgkv_ref)

        cosv = cos_ref[...]
        sinv = sin_ref[...]
        lane = _iota((ts, LANE), 1)
        rope_lanes = (lane >= ROPE_LANE) & (lane < ROPE_LANE + QK_ROPE)

        def unrope(gr):
            return gr * cosv + _rope_swap(gr * sinv)

        dqs, dks, dvs = [], [], []
        dkr = jnp.zeros((ts, LANE), F32)
        for h in range(nh):
            dqs.append(unrope(dq_ref[h] * ATT_SCALE).astype(MXU))
            dkh = dk_ref[h]
            dks.append(jnp.where(lane < QK_NOPE, dkh, 0.0).astype(MXU))
            dkr = dkr + jnp.where(rope_lanes, dkh, 0.0)
            dvs.append(dv_ref[h].astype(MXU))
        dtail_ref[...] = pltpu.roll(jnp.where(rope_lanes, unrope(dkr), 0.0), ROPE_LANE, 1).astype(MXU)
        dq_all = jnp.concatenate(dqs, axis=1)
        dkv_all = jnp.concatenate(dks + dvs, axis=1)
        dwq_ref[...] += _dot_tn(dq_all, qn_ref[...])
        dwkv_ref[...] += _dot_tn(dkv_all, kvn_ref[...])
        dcqa, dgq = _rms_bwd(_dot(dq_all, wq_ref[...]), cqa_ref[...], rq_ref[...], gq_ref[...])
        dckv, dgkv = _rms_bwd(_dot(dkv_all, wkv_ref[...]), ckv_ref[...], rkv_ref[...], gkv_ref[...])
        dcqa_ref[...] = dcqa.astype(MXU)
        dckv_ref[...] = dckv.astype(MXU)
        dgq_ref[...] += dgq
        dgkv_ref[...] += dgkv

    head = pl.BlockSpec((nh, ts, LANE), lambda i: (0, i, 0))
    return pl.pallas_call(
        body, name="mla_prep_bwd", grid=(s // ts,),
        in_specs=[head, head, head,
                  pl.BlockSpec((ts, Q_LORA), lambda i: (i, O_CQA // Q_LORA)),
                  pl.BlockSpec((ts, KV_LORA), lambda i: (i, O_CKV // KV_LORA)),
                  _row(ts, Q_LORA), _row(ts, KV_LORA), _row(ts, 1), _row(ts, 1),
                  _full((1, Q_LORA)), _full((1, KV_LORA)), _full((nh * LANE, Q_LORA)), _full((2 * nh * LANE, KV_LORA)),
                  _row(ts, LANE), _row(ts, LANE)],
        out_specs=[_row(ts, Q_LORA), _row(ts, KV_LORA), _row(ts, LANE), _full((nh * LANE, Q_LORA)),
                   _full((2 * nh * LANE, KV_LORA)), _full((1, Q_LORA)), _full((1, KV_LORA))],
        out_shape=[_sds((s, Q_LORA), MXU), _sds((s, KV_LORA), MXU), _sds((s, LANE), MXU), _sds((nh * LANE, Q_LORA)),
                   _sds((2 * nh * LANE, KV_LORA)), _sds((1, Q_LORA)), _sds((1, KV_LORA))],
        compiler_params=_params(("arbitrary",)),
    )(dq, dk, dv, proj, proj, qn, kvn, rq, rkv, gq, gkv, wq, wkv, cos, sin)


def _inproj_bwd(da4, dsz, dxbc_in, dcqa, dckv, dcz, dtail_a, dtail_b, w, x, rstd, g, dout, dep=None):
    s = x.shape[0]
    ts = _tile(s)

    def body(da_ref, dsz_ref, dxbc_ref, dcqa_ref, dckv_ref, dcz_ref, dta_ref, dtb_ref, w_ref, x_ref, r_ref, g_ref, dout_ref,
             dproj_ref, dx_ref, dg_ref):
        @pl.when(pl.program_id(0) == 0)
        def _():
            dg_ref[...] = jnp.zeros_like(dg_ref)

        dproj = jnp.concatenate(
            [da_ref[0], da_ref[1], da_ref[2], da_ref[3], dxbc_ref[...], dsz_ref[...], dcqa_ref[...], dckv_ref[...],
             dcz_ref[...], dta_ref[...] + dtb_ref[...]], axis=1)
        dproj_ref[...] = dproj
        dh = _dot_nt(dproj, w_ref[...])
        dx, dg = _rms_bwd(dh, x_ref[...], r_ref[...], g_ref[...])
        dx_ref[...] = dout_ref[...] + dx
        dg_ref[...] += dg

    return _call_after(
        dep, body, (da4, dsz, dxbc_in, dcqa, dckv, dcz, dtail_a, dtail_b, w, x, rstd, g, dout), name="inproj_bwd", grid=(s // ts,),
        in_specs=[pl.BlockSpec((4, ts, D_CONV_A), lambda i: (0, i, 0)), _row(ts, D_SSD), _row(ts, N_XBC), _row(ts, Q_LORA),
                  _row(ts, KV_LORA), _row(ts, D_MLA), _row(ts, LANE), _row(ts, LANE), _full((D_MODEL, NCOL)),
                  _row(ts, D_MODEL), _row(ts, 1), _full((1, D_MODEL)), _row(ts, D_MODEL)],
        out_specs=[_row(ts, NCOL), _row(ts, D_MODEL), _full((1, D_MODEL))],
        out_shape=[_sds((s, NCOL), MXU), _sds((s, D_MODEL)), _sds((1, D_MODEL))],
        compiler_params=_params(("arbitrary",)),
    )


DWIN_BLOCK = 640


def _dwin(h, dproj, dep=None):
    s = h.shape[0]

    def body(h_ref, d_ref, o_ref):
        o_ref[...] = _dot_tn(h_ref[...], d_ref[...])

    return _call_after(
        dep, body, (h, dproj), name="dwin", grid=(NCOL // DWIN_BLOCK,),
        in_specs=[_full((s, D_MODEL)), pl.BlockSpec((s, DWIN_BLOCK), lambda j: (0, j))],
        out_specs=pl.BlockSpec((D_MODEL, DWIN_BLOCK), lambda j: (0, j)),
        out_shape=_sds((D_MODEL, NCOL)),
        compiler_params=_params(("parallel",)),
    )


def _adamw(ws, gs, ms, vs, whole):
    n = len(ws)
    bc1 = 1.0 - ADAM_B1 ** ADAM_STEP
    bc2 = 1.0 - ADAM_B2 ** ADAM_STEP

    def body(*refs):
        ins, outs = refs[:4 * n], refs[4 * n:]
        for a in range(n):
            w_ref, g_ref, m_ref, v_ref = ins[a], ins[n + a], ins[2 * n + a], ins[3 * n + a]
            gv = g_ref[...]
            mn = ADAM_B1 * m_ref[...] + (1.0 - ADAM_B1) * gv
            vn = ADAM_B2 * v_ref[...] + (1.0 - ADAM_B2) * (gv * gv)
            outs[n + a][...] = mn
            outs[2 * n + a][...] = vn
            outs[a][...] = -ADAM_LR * ((mn / bc1) / (jnp.sqrt(vn / bc2) + ADAM_EPS) + ADAM_WD * w_ref[...])

    if whole:
        grid, blks = (1,), [pl.BlockSpec(w.shape, lambda i, _n=w.ndim: (0,) * _n) for w in ws]
    else:
        grid = (ws[0].shape[0], 2)
        blks = [pl.BlockSpec((1, w.shape[1] // 2, w.shape[2]), lambda i, k: (i, k, 0)) for w in ws]
    out = pl.pallas_call(
        body, name="adamw", grid=grid,
        in_specs=blks * 4, out_specs=blks * 3, out_shape=[_sds(w.shape) for w in ws] * 3,
        compiler_params=_params(("parallel",) * len(grid)),
    )(*ws, *gs, *ms, *vs)
    return [(out[a], out[n + a], out[2 * n + a]) for a in range(n)]


ADAMW_COLS_BLOCK = 512


def _adamw_cols(w_t, gs, m_t, v_t):
    cols, nl, rows = w_t.shape
    bc1 = 1.0 - ADAM_B1 ** ADAM_STEP
    bc2 = 1.0 - ADAM_B2 ** ADAM_STEP

    def body(w_ref, m_ref, v_ref, *rest):
        g_refs, (go_ref, d_ref, mo_ref, vo_ref), g_blk = rest[:nl], rest[nl:nl + 4], rest[-1]
        for l in range(nl):
            g_blk[:, l, :] = g_refs[l][...].T
        gv = g_blk[...]
        mn = ADAM_B1 * m_ref[...] + (1.0 - ADAM_B1) * gv
        vn = ADAM_B2 * v_ref[...] + (1.0 - ADAM_B2) * (gv * gv)
        go_ref[...] = gv
        mo_ref[...] = mn
        vo_ref[...] = vn
        d_ref[...] = -ADAM_LR * ((mn / bc1) / (jnp.sqrt(vn / bc2) + ADAM_EPS) + ADAM_WD * w_ref[...])

    tc = ADAMW_COLS_BLOCK
    blk = pl.BlockSpec((tc, nl, rows), lambda j: (j, 0, 0))
    gblk = pl.BlockSpec((rows, tc), lambda j: (0, j))
    return pl.pallas_call(
        body, name="adamw_cols", grid=(pl.cdiv(cols, tc),),
        in_specs=[blk] * 3 + [gblk] * nl, out_specs=[blk] * 4, out_shape=[_sds(w_t.shape)] * 4,
        scratch_shapes=[pltpu.VMEM((tc, nl, rows), F32)],
        compiler_params=_params(("parallel",)),
    )(w_t, m_t, v_t, *gs)


COL_MOVES = ((0, 0, 1024), (1024, O_SZ, 384), (1408, O_XBC, 896), (2304, O_TAIL + DT_LANE, 6), (2310, O_CQA, 256),
             (2566, O_CKV, 128), (2694, O_TAIL, 32), (2726, O_CZ, 384))


def _move_cols(w, moves, width):
    out = None
    for src, dst, n in moves:
        piece = jnp.pad(w[..., src:src + n], [(0, 0)] * (w.ndim - 1) + [(dst, width - dst - n)])
        out = piece if out is None else out + piece
    return out


def _perm_cols(w):
    return _move_cols(w, COL_MOVES, NCOL)


def _unperm_cols(g):
    return _move_cols(g, [(dst, src, n) for src, dst, n in COL_MOVES], IN_COLS)


def _wq_layout(wt):
    return jnp.pad(wt.reshape(MLA_HEADS, QK_NOPE + QK_ROPE, Q_LORA), ((0, 0), (0, 32), (0, 0))).reshape(MLA_HEADS * LANE, Q_LORA)


def _wq_unlayout(g):
    return g.reshape(MLA_HEADS, LANE, Q_LORA)[:, :QK_NOPE + QK_ROPE].reshape(MLA_HEADS * (QK_NOPE + QK_ROPE), Q_LORA)


def _wkv_layout(wt):
    t = wt.reshape(MLA_HEADS, 2, 64, KV_LORA).transpose(1, 0, 2, 3)
    return jnp.pad(t, ((0, 0), (0, 0), (0, 64), (0, 0))).reshape(2 * MLA_HEADS * LANE, KV_LORA)


def _wkv_unlayout(g):
    t = g.reshape(2, MLA_HEADS, LANE, KV_LORA)[:, :, :64]
    return t.transpose(1, 0, 2, 3).reshape(MLA_HEADS * LANE, KV_LORA)


def _rope_tables(positions):
    inv_freq = ROPE_BASE ** (-jnp.arange(0, QK_ROPE, 2, dtype=F32) / QK_ROPE)
    ang = positions.astype(F32)[:, None] * inv_freq
    cos, sin = jnp.cos(ang), jnp.sin(ang)
    s = positions.shape[0]
    one, zero = jnp.ones((s, ROPE_LANE), F32), jnp.zeros((s, ROPE_LANE), F32)
    cos_t = jnp.concatenate([one, cos, cos, one[:, :32]], axis=1)
    sin_t = jnp.concatenate([zero, -sin, sin, zero[:, :32]], axis=1)
    return cos_t, sin_t


def _ssd_scalars(dt_bias, a_log, d_skip):
    return jnp.pad(jnp.stack([dt_bias, a_log, d_skip]), ((0, 5), (DT_LANE, LANE - DT_LANE - SSD_HEADS)))


def _layer_fwd(x, lw, cos, sin, dep=None, late=None, head=None):
    proj, h, rstd = _inproj_fwd(x, lw["norm_g"], lw["w_in"], dep)
    ya = _conva_fwd(proj, lw["conv_a_w"])
    xbc = _sconv_fwd(proj, lw["ssd_conv_w"], lw["ssd_conv_b"])
    y_ssd, states = _ssd_fwd(xbc, proj, lw["sc"])
    if late is not None:
        lw = {**lw, **late(ya, y_ssd)}
    q, k, v, qn, kvn, rq, rkv = _mla_prep_fwd(proj, lw["gq"], lw["gkv"], lw["wq"], lw["wkv"], cos, sin)
    o, lse = _attn_fwd(q, k, v)
    if head is None:
        x_out, y = _outproj_fwd(x, proj, ya, y_ssd, o, lw["g_ssd"], lw["w_out"])
    else:
        *x_out, y = _outproj_loss(x, proj, ya, y_ssd, o, lw["g_ssd"], lw["w_out"], *head)
    saved = dict(x=x, proj=proj, h=h, rstd=rstd, xbc=xbc, y_ssd=y_ssd, states=states, q=q, k=k, v=v, qn=qn, kvn=kvn,
                 rq=rq, rkv=rkv, o=o, lse=lse, y=y)
    return x_out, saved, lw


def _layer_bwd(dout, lw, sv, cos, sin, rs=None, begin_early=None):
    tok = lambda: None if rs is None else rs["h"]["token"]
    dya, dys, dsz, d_o, dcz, dg_ssd, dw_out = _outproj_bwd(dout, sv["y"], lw["w_out"], sv["proj"], sv["y_ssd"], sv["o"],
                                                            lw["g_ssd"], tok())
    if rs is not None:
        rs = _rs_add_mine(rs, [dya])
    dq, dk, dv = _attn_bwd(sv["q"], sv["k"], sv["v"], sv["o"], d_o, sv["lse"], tok())
    dxbc, dtail_s, dsc = _ssd_bwd(sv["xbc"], sv["proj"], lw["sc"], sv["states"], dys, tok())
    da4, dw_conva = _conva_bwd(sv["proj"], lw["conv_a_w"], dya, tok())
    if rs is not None:
        rs = _rs_add_chips(rs, [dq, dxbc, da4])
    du, dw_sconv, db_sconv = _sconv_bwd(sv["proj"], lw["ssd_conv_w"], lw["ssd_conv_b"], dxbc, tok())
    dcqa, dckv, dtail_m, dwq, dwkv, dgq, dgkv = _mla_prep_bwd(
        dq, dk, dv, sv["proj"], sv["qn"], sv["kvn"], sv["rq"], sv["rkv"], lw["gq"], lw["gkv"], lw["wq"], lw["wkv"], cos, sin)
    early = None if begin_early is None else begin_early(dw_out, dwq, dwkv)
    etok = lambda: None if early is None else early["h"]["token"]
    dproj, dx, dg = _inproj_bwd(da4, dsz, du, dcqa, dckv, dcz, dtail_s, dtail_m, lw["w_in"], sv["x"], sv["rstd"],
                                lw["norm_g"], dout, etok())
    reduced = None if rs is None else _rs_end(rs, [du, dcqa, dx])
    if early is not None:
        early = _rs_add_mine(early, [dx])
    dw_in = _dwin(sv["h"], dproj, etok())
    if early is not None:
        early = _rs_add_chips(early, [dw_in])
    grads = dict(norm_g=dg, w_in=dw_in, conv_a_w=dw_conva, ssd_conv_w=dw_sconv, ssd_conv_b=db_sconv, sc=dsc,
                 g_ssd=dg_ssd, gq=dgq, wq=dwq, gkv=dgkv, wkv=dwkv, w_out=dw_out)
    return dx, grads, reduced, early


ANY = pl.BlockSpec(memory_space=pl.ANY)
N_CHIPS = 4
N_DEV = 8


def _place():
    return lax.axis_index("x"), lax.axis_index("y"), lax.axis_index("c")


HBM_SPEC = pl.BlockSpec(memory_space=pltpu.HBM)
SEM_SPEC = pl.BlockSpec(memory_space=pltpu.SEMAPHORE)
PAYLOAD = jnp.bfloat16


def _hbm(a):
    return pltpu.with_memory_space_constraint(a, pltpu.HBM)


def _run_plan(plan, srcs, lands, send_sems, recv_sems, start, wait):
    copies = plan(srcs, lands)
    if start:
        for i, (src, dst, _, to) in enumerate(copies):
            pltpu.make_async_remote_copy(src_ref=src, dst_ref=dst, send_sem=send_sems.at[i], recv_sem=recv_sems.at[i],
                                         device_id=to, device_id_type=MESH_T).start()
    if wait:
        for i, (src, _, arrives, to) in enumerate(copies):
            cp = pltpu.make_async_remote_copy(src_ref=src, dst_ref=arrives, send_sem=send_sems.at[i],
                                              recv_sem=recv_sems.at[i], device_id=to, device_id_type=MESH_T)
            cp.wait_send()
            cp.wait_recv()


def _exchange_start(name, plan, n_copies, srcs, land_shapes, deps):
    ns, nl = len(srcs), len(land_shapes)
    n_in = ns + nl + len(deps)

    def body(*refs):
        send_sems, recv_sems = refs[n_in], refs[n_in + 1]
        token = refs[-1]
        _run_plan(plan, refs[:ns], refs[ns:ns + nl], send_sems, recv_sems, True, False)
        token[...] = jnp.zeros_like(token)

    thru = [pltpu.HBM(a.shape, a.dtype) for a in srcs] + [pltpu.HBM(a.shape, a.dtype) for a in land_shapes]
    outs = pl.pallas_call(
        body, name=name,
        out_shape=(pltpu.SemaphoreType.DMA((n_copies,)), pltpu.SemaphoreType.DMA((n_copies,)), *thru, _sds((8, LANE))),
        in_specs=[HBM_SPEC] * (ns + nl) + [ANY] * len(deps),
        out_specs=(SEM_SPEC, SEM_SPEC, *[HBM_SPEC] * (ns + nl), pl.BlockSpec(memory_space=pltpu.VMEM)),
        input_output_aliases={i: 2 + i for i in range(ns + nl)},
        compiler_params=pltpu.CompilerParams(has_side_effects=pltpu.SideEffectType.DATAFLOW_SIDE_EFFECTING),
    )(*[_hbm(a) for a in srcs], *[_hbm(lax.empty(a.shape, a.dtype)) for a in land_shapes], *deps)
    return (outs[0], outs[1]), list(outs[2:2 + ns]), list(outs[2 + ns:2 + ns + nl]), outs[-1]


def _exchange_wait(name, plan, sems, srcs, lands, after):
    ns, nl = len(srcs), len(lands)

    def body(*refs):
        _run_plan(plan, refs[:ns], refs[ns:ns + nl], refs[ns + nl], refs[ns + nl + 1], False, True)

    outs = pl.pallas_call(
        body, name=name,
        out_shape=[pltpu.HBM(a.shape, a.dtype) for a in list(srcs) + list(lands)],
        in_specs=[HBM_SPEC] * (ns + nl) + [SEM_SPEC, SEM_SPEC] + [ANY] * len(after), out_specs=[HBM_SPEC] * (ns + nl),
        input_output_aliases={i: i for i in range(ns + nl)},
        compiler_params=pltpu.CompilerParams(has_side_effects=pltpu.SideEffectType.DATAFLOW_SIDE_EFFECTING),
    )(*srcs, *lands, sems[0], sems[1], *after)
    return list(outs[:ns]), list(outs[ns:])


def _xchg_begin(name, plan, n_copies, srcs, land_shapes, deps=()):
    sems, srcs_t, lands_t, token = _exchange_start(name + "_start", plan, n_copies, srcs, land_shapes, list(deps))
    return dict(name=name, plan=plan, sems=sems, srcs=srcs_t, lands=lands_t, token=token)


def _xchg_end(h, after):
    return _exchange_wait(h["name"] + "_wait", h["plan"], h["sems"], h["srcs"], h["lands"], after)


def _other_chips():
    x, y, c = _place()
    return [(1 - x, y), (x, 1 - y), (1 - x, 1 - y)]


def _gather_plan(srcs, lands):
    x, y, c = _place()
    me = 2 * x + y
    return [(srcs[a], lands[a].at[me], lands[a].at[2 * cx + cy], (cx, cy, c))
            for (cx, cy) in _other_chips() for a in range(len(srcs))]


def _gather_begin(shards, tag, deps=()):
    shapes = [_sds((N_CHIPS,) + a.shape, a.dtype) for a in shards]
    return _xchg_begin(f"gather_{tag}", _gather_plan, 3 * len(shards), shards, shapes, deps)


def _gather_end(h, after):
    shards, lands = _xchg_end(h, after)
    me = 2 * lax.axis_index("x") + lax.axis_index("y")
    return [lax.dynamic_update_index_in_dim(g, s, me, 0) for g, s in zip(lands, shards)]


def _gather_half_plan(srcs, lands):
    x, y, c = _place()
    me = 2 * x + y
    out = []
    for (cx, cy) in _other_chips():
        out.append((srcs[0].at[c], lands[0].at[me, c], lands[0].at[2 * cx + cy, c], (cx, cy, c)))
        out += [(srcs[a], lands[a].at[me], lands[a].at[2 * cx + cy], (cx, cy, c)) for a in range(1, len(srcs))]
    return out


def _forward_plan(bufs, _):
    x, y, c = _place()
    return [(bufs[0].at[2 * cx + cy, c], bufs[0].at[2 * cx + cy, c], bufs[0].at[2 * cx + cy, 1 - c], (x, y, 1 - c))
            for (cx, cy) in _other_chips()]


def _swap_plan(srcs, lands):
    x, y, c = _place()
    return [(srcs[a].at[:, 1 - c], lands[a], lands[a], (x, y, 1 - c)) for a in range(len(srcs))]


def _chips_plan(srcs, lands):
    x, y, c = _place()
    me = 2 * x + y
    return [(srcs[a].at[2 * cx + cy], lands[a].at[me], lands[a].at[2 * cx + cy], (cx, cy, c))
            for (cx, cy) in _other_chips() for a in range(len(srcs))]


def _share_plan(srcs, lands):
    x, y, c = _place()
    return [(srcs[a], lands[a].at[c], lands[a].at[1 - c], (x, y, 1 - c)) for a in range(len(srcs))]


def _allreduce_small(slab, dep=None):
    r = slab.shape[0]

    def body(s_ref, o_ref, gath, send_sems, recv_sems):
        x, y, c = _place()
        me = 4 * x + 2 * y + c
        gath[me] = s_ref[...]
        cps = []
        for rel in range(1, N_DEV):
            px = 1 - x if rel & 4 else x
            py = 1 - y if rel & 2 else y
            pc = 1 - c if rel & 1 else c
            cp = pltpu.make_async_remote_copy(src_ref=s_ref, dst_ref=gath.at[me], send_sem=send_sems.at[rel - 1],
                                              recv_sem=recv_sems.at[rel - 1], device_id=(px, py, pc), device_id_type=MESH_T)
            cp.start()
            cps.append(cp)
        for cp in cps:
            cp.wait()
        acc = gath[0]
        for d in range(1, N_DEV):
            acc = acc + gath[d]
        o_ref[...] = acc

    vm = pl.BlockSpec(memory_space=pltpu.VMEM)
    return _call_after(
        dep, body, (slab,), name="allreduce_small", in_specs=[vm], out_specs=vm, out_shape=_sds((r, LANE)),
        scratch_shapes=[pltpu.VMEM((N_DEV, r, LANE), F32), pltpu.SemaphoreType.DMA((N_DEV - 1,)),
                        pltpu.SemaphoreType.DMA((N_DEV - 1,))],
    )


def _add_mine(g4s, recvs, half):
    n = len(g4s)

    def body(h_ref, *refs):
        for g_ref, r_ref, o_ref in zip(refs[:n], refs[n:2 * n], refs[2 * n:]):
            o_ref[0] = (g_ref[0, 0] + r_ref[0]).astype(o_ref.dtype)

    dims = [g.shape[2:] for g in g4s]
    return pl.pallas_call(
        body, name="add_mine",
        grid_spec=pltpu.PrefetchScalarGridSpec(
            num_scalar_prefetch=1, grid=(N_CHIPS,),
            in_specs=[pl.BlockSpec((1, 1) + d, lambda j, h: (j, h[0], 0, 0)) for d in dims]
            + [pl.BlockSpec((1,) + d, lambda j, h: (j, 0, 0)) for d in dims],
            out_specs=[pl.BlockSpec((1,) + d, lambda j, h: (j, 0, 0)) for d in dims]),
        out_shape=[_sds((N_CHIPS,) + d, PAYLOAD) for d in dims],
        compiler_params=_params(("parallel",)),
    )(half, *g4s, *recvs)


def _add_chips(es, ps, me):
    n = len(es)

    def body(m_ref, *refs):
        for e_ref, p_ref, o_ref in zip(refs[:n], refs[n:2 * n], refs[2 * n:]):
            own = p_ref[0].astype(F32)
            acc = None
            for s in range(N_CHIPS):
                t = jnp.where(m_ref[0] == s, own, e_ref[s].astype(F32))
                acc = t if acc is None else acc + t
            o_ref[...] = acc

    dims = [e.shape[1:] for e in es]
    return pl.pallas_call(
        body, name="add_chips",
        grid_spec=pltpu.PrefetchScalarGridSpec(
            num_scalar_prefetch=1, grid=(1,),
            in_specs=[pl.BlockSpec((N_CHIPS,) + d, lambda i, m: (0, 0, 0)) for d in dims]
            + [pl.BlockSpec((1,) + d, lambda i, m: (m[0], 0, 0)) for d in dims],
            out_specs=[pl.BlockSpec(d, lambda i, m: (0, 0)) for d in dims]),
        out_shape=[_sds(d) for d in dims],
        compiler_params=_params(("arbitrary",)),
    )(me, *es, *ps)


def _rs_begin(gs, tag, deps=()):
    g4 = [g.reshape(N_CHIPS, 2, g.shape[0] // (2 * N_CHIPS), g.shape[1]) for g in gs]
    h = _xchg_begin(f"rs_swap_{tag}", _swap_plan, len(gs), g4, [_sds((N_CHIPS,) + g.shape[2:]) for g in g4], deps)
    return dict(h=h, tag=tag, shapes=[g.shape for g in gs])


def _rs_add_mine(st, after):
    g4, recv = _xchg_end(st["h"], after)
    half = jnp.reshape(lax.axis_index("c"), (1,)).astype(jnp.int32)
    ps = _add_mine(g4, recv, half)
    st["h"] = _xchg_begin(f"rs_chips_{st['tag']}", _chips_plan, 3 * len(ps), ps, [_sds(p.shape, p.dtype) for p in ps])
    return st


def _rs_add_chips(st, after):
    ps, es = _xchg_end(st["h"], after)
    me = jnp.reshape(2 * lax.axis_index("x") + lax.axis_index("y"), (1,)).astype(jnp.int32)
    fs = _add_chips(es, ps, me)
    st["h"] = _xchg_begin(f"rs_share_{st['tag']}", _share_plan, len(fs), fs, [_sds((2,) + f.shape) for f in fs])
    return st


def _rs_end(st, after):
    fs, ss = _xchg_end(st["h"], after)
    c = lax.axis_index("c")
    return [lax.dynamic_update_index_in_dim(s, f, c, 0).reshape(shp[0] // N_CHIPS, shp[1])
            for s, f, shp in zip(ss, fs, st["shapes"])]


WEIGHTS = ["norm_g", "w_in", "conv_a_w", "ssd_conv_w", "ssd_conv_b", "ssd_dt_bias", "ssd_a_log", "ssd_d", "ssd_norm_g",
           "mla_q_norm_g", "w_qb", "mla_kv_norm_g", "w_kvb", "w_out", "final_norm_g"]
BIG = ["w_in", "w_qb", "w_kvb", "w_out"]
SLAB_ROWS = 128


def _to_slab(parts, rows):
    flat = jnp.concatenate([p.reshape(-1) for p in parts])
    return jnp.pad(flat, (0, rows * LANE - flat.shape[0])).reshape(rows, LANE)


def _from_slab(slab, shapes):
    flat = slab.reshape(-1)
    out, off = [], 0
    for shp in shapes:
        n = int(np.prod(shp))
        out.append(flat[off:off + n].reshape(shp))
        off += n
    return out


def kernel(x, positions, norm_g, w_in, conv_a_w, ssd_conv_w, ssd_conv_b, ssd_dt_bias, ssd_a_log, ssd_d, ssd_norm_g, mla_q_norm_g, w_qb, mla_kv_norm_g, w_kvb, w_out, final_norm_g, loss_target, m_norm_g, m_w_in, m_conv_a_w, m_ssd_conv_w, m_ssd_conv_b, m_ssd_dt_bias, m_ssd_a_log, m_ssd_d, m_ssd_norm_g, m_mla_q_norm_g, m_w_qb, m_mla_kv_norm_g, m_w_kvb, m_w_out, m_final_norm_g, v_norm_g, v_w_in, v_conv_a_w, v_ssd_conv_w, v_ssd_conv_b, v_ssd_dt_bias, v_ssd_a_log, v_ssd_d, v_ssd_norm_g, v_mla_q_norm_g, v_w_qb, v_mla_kv_norm_g, v_w_kvb, v_w_out, v_final_norm_g):
    w = dict(norm_g=norm_g, w_in=w_in, conv_a_w=conv_a_w, ssd_conv_w=ssd_conv_w, ssd_conv_b=ssd_conv_b,
             ssd_dt_bias=ssd_dt_bias, ssd_a_log=ssd_a_log, ssd_d=ssd_d, ssd_norm_g=ssd_norm_g, mla_q_norm_g=mla_q_norm_g,
             w_qb=w_qb, mla_kv_norm_g=mla_kv_norm_g, w_kvb=w_kvb, w_out=w_out, final_norm_g=final_norm_g)
    mom = dict(norm_g=m_norm_g, w_in=m_w_in, conv_a_w=m_conv_a_w, ssd_conv_w=m_ssd_conv_w, ssd_conv_b=m_ssd_conv_b,
               ssd_dt_bias=m_ssd_dt_bias, ssd_a_log=m_ssd_a_log, ssd_d=m_ssd_d, ssd_norm_g=m_ssd_norm_g,
               mla_q_norm_g=m_mla_q_norm_g, w_qb=m_w_qb, mla_kv_norm_g=m_mla_kv_norm_g, w_kvb=m_w_kvb, w_out=m_w_out,
               final_norm_g=m_final_norm_g)
    var = dict(norm_g=v_norm_g, w_in=v_w_in, conv_a_w=v_conv_a_w, ssd_conv_w=v_ssd_conv_w, ssd_conv_b=v_ssd_conv_b,
               ssd_dt_bias=v_ssd_dt_bias, ssd_a_log=v_ssd_a_log, ssd_d=v_ssd_d, ssd_norm_g=v_ssd_norm_g,
               mla_q_norm_g=v_mla_q_norm_g, w_qb=v_w_qb, mla_kv_norm_g=v_mla_kv_norm_g, w_kvb=v_w_kvb, w_out=v_w_out,
               final_norm_g=v_final_norm_g)
    chip = 2 * lax.axis_index("x") + lax.axis_index("y")

    def early_shard(l, zero):
        pack = jnp.pad(conv_a_w[l], ((0, 5), (0, 192))) + jnp.pad(ssd_conv_w[l], ((3, 1), (0, 32)))
        return [(_perm_cols(w_in[l]) + zero).astype(MXU), pack + zero]

    def late_shard(l, zero):
        return [(w_out[l] + zero).astype(MXU), (w_qb[l].T + zero).astype(MXU), (w_kvb[l].T + zero).astype(MXU)]

    def early_weights(l, gathered):
        g_in, g_conv = gathered
        return dict(
            norm_g=norm_g[l][None], w_in=g_in.reshape(D_MODEL, NCOL),
            conv_a_w=jnp.concatenate([g_conv[j, 0:3, 0:64] for j in range(N_CHIPS)], axis=1),
            ssd_conv_w=jnp.concatenate([g_conv[j, 3:7, 0:224] for j in range(N_CHIPS)], axis=1),
            ssd_conv_b=ssd_conv_b[l][None], sc=_ssd_scalars(ssd_dt_bias[l], ssd_a_log[l], ssd_d[l]),
            g_ssd=ssd_norm_g[l][None], gq=mla_q_norm_g[l][None], gkv=mla_kv_norm_g[l][None])

    def late_weights(gathered):
        g_out, g_qb, g_kvb = gathered
        return dict(wq=_wq_layout(g_qb.reshape(MLA_HEADS * 96, Q_LORA)), wkv=_wkv_layout(g_kvb.reshape(MLA_HEADS * LANE, KV_LORA)),
                    w_out=g_out.reshape(D_MODEL, D_MODEL))

    def late_grads(dw_out, dwq, dwkv):
        wq = jnp.pad(_wq_unlayout(dwq).reshape(N_CHIPS, 144, Q_LORA), ((0, 0), (0, 16), (0, 0)))
        return [dw_out, wq.reshape(N_CHIPS * 160, Q_LORA), _wkv_unlayout(dwkv)]

    def large_grads(g):
        return [g["w_in"]] + late_grads(g["w_out"], g["wq"], g["wkv"])

    w_in0, pack0 = early_shard(0, 0.0)
    half = w_in0.shape[0] // 2
    gather_a0 = _xchg_begin("gather_a0", _gather_half_plan, 6, [w_in0.reshape(2, half, NCOL), pack0],
                            [_sds((N_CHIPS, 2, half, NCOL), MXU), _sds((N_CHIPS,) + pack0.shape)])
    zero = gather_a0["token"][0, 0]
    cos, sin = _rope_tables(positions[0] + zero.astype(jnp.int32))
    late0, shards1 = late_shard(0, zero), early_shard(1, zero) + late_shard(1, zero)
    mine0, (g_in0, g_conv0) = _xchg_end(gather_a0, [cos, sin] + late0 + shards1)
    forward_a0 = _xchg_begin("forward_a0", _forward_plan, 3, [g_in0], [])
    gather_b0 = _gather_begin(late0, "b0", [forward_a0["token"]])
    gather_1 = _gather_begin(shards1, "1", [gather_b0["token"]])
    (g_in0,), _ = _xchg_end(forward_a0, [gather_1["token"]])
    lw0 = early_weights(0, [lax.dynamic_update_index_in_dim(g, s_, chip, 0) for g, s_ in zip((g_in0, g_conv0), mine0)])
    x1, sv0, lw0 = _layer_fwd(x[0], lw0, cos, sin, gather_1["token"],
                              lambda ya, y_ssd: late_weights(_gather_end(gather_b0, [ya, y_ssd])))
    g1 = _gather_end(gather_1, [x1])
    (dx, dgf, loss), sv1, lw1 = _layer_fwd(x1, {**early_weights(1, g1[:2]), **late_weights(g1[2:])}, cos, sin,
                                           head=(final_norm_g[None], loss_target[0]))

    dx, lg1, _, _ = _layer_bwd(dx, lw1, sv1, cos, sin)
    grad_x, lg0, red1, rs0_late = _layer_bwd(dx, lw0, sv0, cos, sin, _rs_begin(large_grads(lg1), 1),
                                             lambda *g: _rs_begin(late_grads(*g), "0l"))
    rs0 = _rs_begin([lg0["w_in"]], 0, [rs0_late["h"]["token"]])
    lg = [lg0, lg1]
    grad = {}

    small_names = ["norm_g", "conv_a_w", "ssd_conv_w", "ssd_conv_b", "sc", "g_ssd", "gq", "gkv"]
    parts = [loss[0, 0:1], dgf]
    for nm in small_names:
        parts += [lg[l][nm][:3, DT_LANE:DT_LANE + SSD_HEADS] if nm == "sc" else lg[l][nm] for l in range(DEPTH)]
    shapes = [(1,), (D_MODEL,)] + [(DEPTH,) + shp for shp in ((D_MODEL,), (3, D_CONV_A), (4, N_XBC), (N_XBC,), (3, SSD_HEADS),
                                                              (D_SSD,), (Q_LORA,), (KV_LORA,))]
    red_slab = _allreduce_small(_to_slab(parts, SLAB_ROWS), rs0["h"]["token"])
    rs0 = _rs_add_mine(rs0, [red_slab])
    red = _from_slab(red_slab + rs0["h"]["token"][0, 0], shapes)
    loss_out = red[0][0]
    grad["final_norm_g"] = red[1]
    grad["norm_g"], conv_a_full, sconv_full, grad["ssd_conv_b"], sc_grads = red[2:7]
    grad["ssd_norm_g"], grad["mla_q_norm_g"], grad["mla_kv_norm_g"] = red[7:10]
    grad["conv_a_w"] = lax.dynamic_slice_in_dim(conv_a_full, chip * 64, 64, axis=2)
    grad["ssd_conv_w"] = lax.dynamic_slice_in_dim(sconv_full, chip * 224, 224, axis=2)
    grad["ssd_dt_bias"], grad["ssd_a_log"], grad["ssd_d"] = sc_grads[:, 0], sc_grads[:, 1], sc_grads[:, 2]

    delta, new_m, new_v = {}, {}, {}
    small = [nm for nm in WEIGHTS if nm not in BIG]
    row2 = lambda a: a[None] if a.ndim == 1 else a
    small_out = _adamw(*[[row2(a[nm]) for nm in small] for a in (w, grad, mom, var)], whole=True)
    for nm, (dv, mv, vv) in zip(small, small_out):
        delta[nm], new_m[nm], new_v[nm] = [a.reshape(w[nm].shape) for a in (dv, mv, vv)]

    r_out, r_qb, r_kvb = [jnp.stack([a, b]) for a, b in zip(_rs_end(rs0_late, [red_slab]), red1[1:])]
    late = [nm for nm in BIG if nm != "w_in"]
    view = {nm: (lambda a: a) if nm == "w_out" else (lambda a: jnp.swapaxes(a, 1, 2)) for nm in late}
    late_g = [dict(w_out=r_out, w_qb=r_qb[:, :144], w_kvb=r_kvb)[nm] for nm in late]
    late_out = _adamw(*[[view[nm](a[nm]) for nm in late] for a in (w,)], late_g,
                      *[[view[nm](a[nm]) for nm in late] for a in (mom, var)], whole=False)
    for nm, gv, (dv, mv, vv) in zip(late, late_g, late_out):
        grad[nm], delta[nm], new_m[nm], new_v[nm] = [view[nm](a) for a in (gv, dv, mv, vv)]
    g_in1 = _unperm_cols(red1[0])
    shadow_work = [a for row in small_out + late_out for a in row] + [grad[nm] for nm in small] + [g_in1]
    r_in0, = _rs_end(_rs_add_chips(rs0, shadow_work), [])
    to_cols, from_cols = (lambda a: jnp.transpose(a, (2, 0, 1))), (lambda a: jnp.transpose(a, (1, 2, 0)))
    grad["w_in"], delta["w_in"], new_m["w_in"], new_v["w_in"] = [from_cols(a) for a in _adamw_cols(
        to_cols(w["w_in"]), [_unperm_cols(r_in0), g_in1], to_cols(mom["w_in"]), to_cols(var["w_in"]))]

    return (loss_out, grad_x[None], *[grad[nm] for nm in WEIGHTS], *[delta[nm] for nm in WEIGHTS],
            *[new_m[nm] for nm in WEIGHTS], *[new_v[nm] for nm in WEIGHTS])
```

```python
import functools
import math

import numpy as np
import jax
import jax.numpy as jnp
from jax import lax
from jax.experimental import pallas as pl
from jax.experimental.pallas import tpu as pltpu

F32 = jnp.float32
MXU = jnp.bfloat16

D_MODEL = 1024
DEPTH = 2
D_CONV_A = 256
D_SSD = 384
SSD_HEADS = 6
SSD_BC = 256
SSD_CHUNK = 128
SSD_CHUNKS_PER_STEP = 4
SSD_NORM_EPS = 1e-5
MLA_HEADS = 6
Q_LORA = 256
KV_LORA = 128
QK_NOPE = 64
QK_ROPE = 32
V_DIM = 64
D_MLA = 384
ROPE_BASE = 10000.0
NORM_EPS = 1e-6
IN_COLS = 3110
LANE = 128

O_AH, O_AB, O_AC, O_AZ = 0, 256, 512, 768
O_XBC = 1024
O_SZ = 1920
O_CQA = 2304
O_CKV = 2560
O_CZ = 2688
O_TAIL = 3072
NCOL = 3200
N_XBC = D_SSD + 2 * SSD_BC
DT_LANE = 32
ROPE_LANE = 64

ADAM_LR, ADAM_B1, ADAM_B2, ADAM_EPS, ADAM_WD, ADAM_STEP = 0.001, 0.9, 0.999, 1e-08, 0.01, 10

VMEM_LIMIT = 56 * 1024 * 1024
MESH_T = pl.DeviceIdType.MESH


def _dot(a, b):
    return jnp.dot(a.astype(MXU), b.astype(MXU), preferred_element_type=F32)


def _dot_nt(a, b):
    return lax.dot_general(a.astype(MXU), b.astype(MXU), (((1,), (1,)), ((), ())), preferred_element_type=F32)


def _dot_tn(a, b):
    return lax.dot_general(a.astype(MXU), b.astype(MXU), (((0,), (0,)), ((), ())), preferred_element_type=F32)


def _dot_hi(a, b):
    return jnp.dot(a, b, precision=lax.Precision.HIGHEST, preferred_element_type=F32)


def _dot_hi_tn(a, b):
    return lax.dot_general(a, b, (((0,), (0,)), ((), ())), precision=lax.Precision.HIGHEST, preferred_element_type=F32)


def _sigmoid(z):
    return 1.0 / (1.0 + jnp.exp(-z))


def _silu(z):
    return z * _sigmoid(z)


def _dsilu(z):
    s = _sigmoid(z)
    return s * (1.0 + z * (1.0 - s))


def _softplus(z):
    e = jnp.exp(-jnp.abs(z))
    return jnp.maximum(z, 0.0) + jnp.where(e < 1e-3, e * (1.0 - 0.5 * e), jnp.log(1.0 + e))


def _iota(shape, dim):
    return lax.broadcasted_iota(jnp.int32, shape, dim)


def _shift_down(u, k):
    if k == 0:
        return u
    return jnp.where(_iota(u.shape, 0) >= k, pltpu.roll(u, k, 0), 0.0)


def _shift_up(u, k):
    if k == 0:
        return u
    n = u.shape[0]
    return jnp.where(_iota(u.shape, 0) < n - k, pltpu.roll(u, n - k, 0), 0.0)


def _rope_swap(t):
    lane = _iota(t.shape, 1)
    lo = (lane >= ROPE_LANE) & (lane < ROPE_LANE + 16)
    hi = (lane >= ROPE_LANE + 16) & (lane < ROPE_LANE + 32)
    return jnp.where(lo, pltpu.roll(t, LANE - 16, 1), jnp.where(hi, pltpu.roll(t, 16, 1), 0.0))


def _params(sem=None):
    return pltpu.CompilerParams(dimension_semantics=sem, vmem_limit_bytes=VMEM_LIMIT)


def _full(shape):
    nd = len(shape)
    return pl.BlockSpec(shape, lambda *_: (0,) * nd)


def _sds(shape, dtype=F32):
    return jax.ShapeDtypeStruct(shape, dtype)


def _tile(s):
    return min(512, s)


def _row(ts, w):
    return pl.BlockSpec((ts, w), lambda i: (i, 0))


def _gate_cols(ts, off):
    return pl.BlockSpec((ts, D_SSD), lambda i, _o=off // D_SSD: (i, _o))


def _col(s, off):
    return pl.BlockSpec((s, LANE), lambda j, _o=off // LANE: (0, _o + j))


def _call_after(dep, body, args, *, in_specs, **kw):
    if dep is None:
        return pl.pallas_call(body, in_specs=in_specs, **kw)(*args)
    n = len(args)

    def body_dep(*refs):
        body(*refs[:n], *refs[n + 1:])

    return pl.pallas_call(body_dep, in_specs=list(in_specs) + [pl.BlockSpec(memory_space=pl.ANY)], **kw)(*args, dep)


def _rms(c, g):
    r = lax.rsqrt(jnp.mean(c * c, axis=-1, keepdims=True) + NORM_EPS)
    return c * r * g, r


def _rms_bwd(dn, c, r, g):
    ch = c * r
    dch = dn * g
    dc = r * (dch - ch * jnp.mean(dch * ch, axis=-1, keepdims=True))
    return dc, jnp.sum(dn * ch, axis=0, keepdims=True)


def _inproj_fwd(x, g, w, dep=None):
    s = x.shape[0]
    ts = _tile(s)

    def body(x_ref, g_ref, w_ref, proj_ref, h_ref, r_ref):
        hn, r = _rms(x_ref[...], g_ref[...])
        h = hn.astype(MXU)
        h_ref[...] = h
        r_ref[...] = r
        proj_ref[...] = jnp.dot(h, w_ref[...], preferred_element_type=F32)

    return _call_after(
        dep, body, (x, g, w), name="inproj_fwd", grid=(s // ts,),
        in_specs=[_row(ts, D_MODEL), _full((1, D_MODEL)), _full((D_MODEL, NCOL))],
        out_specs=[_row(ts, NCOL), _row(ts, D_MODEL), _row(ts, 1)],
        out_shape=[_sds((s, NCOL)), _sds((s, D_MODEL), MXU), _sds((s, 1))],
        compiler_params=_params(("parallel",)),
    )


def _conva_fwd(proj, w):
    s = proj.shape[0]

    def body(h_ref, b_ref, c_ref, z_ref, w_ref, y_ref):
        u = c_ref[...] * h_ref[...]
        wv = w_ref[...]
        cv = wv[2:3, :] * u + wv[1:2, :] * _shift_down(u, 1) + wv[0:1, :] * _shift_down(u, 2)
        y_ref[...] = b_ref[...] * cv * _silu(z_ref[...])

    return pl.pallas_call(
        body, name="conva_fwd", grid=(D_CONV_A // LANE,),
        in_specs=[_col(s, O_AH), _col(s, O_AB), _col(s, O_AC), _col(s, O_AZ), pl.BlockSpec((3, LANE), lambda j: (0, j))],
        out_specs=pl.BlockSpec((s, LANE), lambda j: (0, j)),
        out_shape=_sds((s, D_CONV_A)),
        compiler_params=_params(("parallel",)),
    )(proj, proj, proj, proj, w)


def _sconv_pre(u, wv, bv):
    return (wv[3:4, :] * u + wv[2:3, :] * _shift_down(u, 1) + wv[1:2, :] * _shift_down(u, 2)
            + wv[0:1, :] * _shift_down(u, 3) + bv)


def _sconv_fwd(proj, w, b):
    s = proj.shape[0]

    def body(u_ref, w_ref, b_ref, o_ref):
        o_ref[...] = _silu(_sconv_pre(u_ref[...], w_ref[...], b_ref[...]))

    return pl.pallas_call(
        body, name="sconv_fwd", grid=(N_XBC // LANE,),
        in_specs=[_col(s, O_XBC), pl.BlockSpec((4, LANE), lambda j: (0, j)), pl.BlockSpec((1, LANE), lambda j: (0, j))],
        out_specs=pl.BlockSpec((s, LANE), lambda j: (0, j)),
        out_shape=_sds((s, N_XBC)),
        compiler_params=_params(("parallel",)),
    )(proj, w, b)


def _ssd_chunk_common(tail, sc):
    l = SSD_CHUNK
    lane = _iota((l, LANE), 1)
    row = _iota((l, LANE), 0)
    tri = (row >= lane).astype(F32)
    a_row = -jnp.exp(sc[1:2, :])
    pre = tail + sc[0:1, :]
    dt = _softplus(pre)
    a_cs = _dot_hi(tri, dt * a_row)
    return lane, row, tri, a_row, pre, dt, a_cs, a_cs.T


def _pick_col(m, lane, k):
    return jnp.sum(jnp.where(lane == k, m, 0.0), axis=1, keepdims=True)


def _pick_row(m, row, k):
    return jnp.sum(jnp.where(row == k, m, 0.0), axis=0, keepdims=True)


def _ssd_fwd(xbc, proj, sc):
    s = xbc.shape[0]
    nc = s // SSD_CHUNK
    l = SSD_CHUNK
    cps = SSD_CHUNKS_PER_STEP

    def body(xbc_ref, tail_ref, sc_ref, y_ref, st_ref, state):
        @pl.when(pl.program_id(0) == 0)
        def _():
            state[...] = jnp.zeros_like(state)

        sc_v = sc_ref[...]
        lane1 = _iota((1, LANE), 1)
        rowp = _iota((LANE, 1), 0)
        d_row = sc_v[2:3, :]
        states = [state[j] for j in range(3)]
        for u in range(cps):
            r = slice(u * l, (u + 1) * l)
            lane, row, _, _, _, dt, a_cs, a_t = _ssd_chunk_common(tail_ref[r, :], sc_v)
            for j in range(3):
                st_ref[u, j] = states[j]
            for j in range(3):
                xpair = xbc_ref[r, LANE * j:LANE * (j + 1)]
                sp = states[j]
                ypair = jnp.zeros((l, LANE), F32)
                new_s = jnp.zeros((LANE, LANE), F32)
                decay = jnp.zeros((LANE, 1), F32)
                for half in range(2):
                    h = 2 * j + half
                    g = h // 3
                    hm = (lane < 64) if half == 0 else (lane >= 64)
                    hrow = (rowp < 64) if half == 0 else (rowp >= 64)
                    ac = _pick_col(a_cs, lane, DT_LANE + h)
                    ar = _pick_row(a_t, row, DT_LANE + h)
                    dtc = _pick_col(dt, lane, DT_LANE + h)
                    alast = jnp.sum(jnp.where(lane1 == l - 1, ar, 0.0), axis=1, keepdims=True)
                    dh = jnp.sum(jnp.where(lane1 == DT_LANE + h, d_row, 0.0), axis=1, keepdims=True)
                    xm = jnp.where(hm, xpair, 0.0)
                    xd = xm * dtc
                    bm = xbc_ref[r, D_SSD + LANE * g:D_SSD + LANE * (g + 1)]
                    cm = xbc_ref[r, D_SSD + SSD_BC + LANE * g:D_SSD + SSD_BC + LANE * (g + 1)]
                    lm = jnp.where(row >= lane, jnp.exp(jnp.minimum(ac - ar, 0.0)), 0.0)
                    y_diag = _dot(_dot_nt(cm, bm) * lm, xd)
                    y_off = jnp.where(hm, _dot_nt(cm, sp), 0.0) * jnp.exp(ac)
                    ypair = ypair + y_diag + y_off + xm * dh
                    new_s = new_s + _dot_tn(xd * jnp.exp(alast - ac), bm)
                    decay = jnp.where(hrow, jnp.exp(alast), decay)
                states[j] = sp * decay + new_s
                y_ref[r, LANE * j:LANE * (j + 1)] = ypair
        for j in range(3):
            state[j] = states[j]

    return pl.pallas_call(
        body, name="ssd_fwd", grid=(nc // cps,),
        in_specs=[pl.BlockSpec((cps * l, N_XBC), lambda c: (c, 0)),
                  pl.BlockSpec((cps * l, LANE), lambda c: (c, O_TAIL // LANE)), _full((8, LANE))],
        out_specs=[pl.BlockSpec((cps * l, D_SSD), lambda c: (c, 0)), pl.BlockSpec((cps, 3, LANE, LANE), lambda c: (c, 0, 0, 0))],
        out_shape=[_sds((s, D_SSD)), _sds((nc, 3, LANE, LANE))],
        scratch_shapes=[pltpu.VMEM((3, LANE, LANE), F32)],
        compiler_params=_params(("arbitrary",)),
    )(xbc, proj, sc)


def _mla_prep_fwd(proj, gq, gkv, wq, wkv, cos, sin):
    s = proj.shape[0]
    ts = _tile(s)
    nh = MLA_HEADS

    def body(cqa_ref, ckv_ref, tail_ref, gq_ref, gkv_ref, wq_ref, wkv_ref, cos_ref, sin_ref,
             q_ref, k_ref, v_ref, qn_ref, kvn_ref, rq_ref, rkv_ref):
        qn, rq = _rms(cqa_ref[...], gq_ref[...])
        kvn, rkv = _rms(ckv_ref[...], gkv_ref[...])
        qn = qn.astype(MXU)
        kvn = kvn.astype(MXU)
        qn_ref[...] = qn
        kvn_ref[...] = kvn
        rq_ref[...] = rq
        rkv_ref[...] = rkv
        q = _dot_nt(qn, wq_ref[...])
        kv = _dot_nt(kvn, wkv_ref[...])
        cosv = cos_ref[...]
        sinv = sin_ref[...]
        lane = _iota((ts, LANE), 1)
        rope_lanes = (lane >= ROPE_LANE) & (lane < ROPE_LANE + QK_ROPE)
        kr = jnp.where(rope_lanes, pltpu.roll(tail_ref[...], ROPE_LANE, 1), 0.0)
        kr = kr * cosv + _rope_swap(kr) * sinv
        for h in range(nh):
            qh = q[:, LANE * h:LANE * (h + 1)]
            q_ref[h] = ((qh * cosv + _rope_swap(qh) * sinv) * ATT_SCALE).astype(MXU)
            k_ref[h] = (kv[:, LANE * h:LANE * (h + 1)] + kr).astype(MXU)
            v_ref[h] = kv[:, LANE * (nh + h):LANE * (nh + h + 1)].astype(MXU)

    head = pl.BlockSpec((nh, ts, LANE), lambda i: (0, i, 0))
    return pl.pallas_call(
        body, name="mla_prep_fwd", grid=(s // ts,),
        in_specs=[pl.BlockSpec((ts, Q_LORA), lambda i: (i, O_CQA // Q_LORA)),
                  pl.BlockSpec((ts, KV_LORA), lambda i: (i, O_CKV // KV_LORA)),
                  pl.BlockSpec((ts, LANE), lambda i: (i, O_TAIL // LANE)),
                  _full((1, Q_LORA)), _full((1, KV_LORA)), _full((nh * LANE, Q_LORA)), _full((2 * nh * LANE, KV_LORA)),
                  _row(ts, LANE), _row(ts, LANE)],
        out_specs=[head, head, head, _row(ts, Q_LORA), _row(ts, KV_LORA), _row(ts, 1), _row(ts, 1)],
        out_shape=[_sds((nh, s, LANE), MXU)] * 3 + [_sds((s, Q_LORA), MXU), _sds((s, KV_LORA), MXU), _sds((s, 1)), _sds((s, 1))],
        compiler_params=_params(("parallel",)),
    )(proj, proj, proj, gq, gkv, wq, wkv, cos, sin)


ATT_SCALE = (QK_NOPE + QK_ROPE) ** -0.5
NEG = -1e30


def _att_tile(s, most):
    return min(most, s // 2)


ATT_FWD_TILE = 1024
ATT_BWD_TILE = 512


def _attn_fwd(q, k, v):
    nh, s, _ = q.shape
    tq = _att_tile(s, ATT_FWD_TILE)
    nq = s // tq

    def body(q_ref, k_ref, v_ref, o_ref, lse_ref):
        i = pl.program_id(1)
        rowi = _iota((tq, tq), 0)
        coli = _iota((tq, tq), 1)
        zero = (jnp.full((tq, 1), NEG, F32), jnp.zeros((tq, 1), F32), jnp.zeros((tq, LANE), F32))
        state = [zero, zero]
        done = [zero, zero]
        for t in range(nq + 1):
            first = t <= i
            qblk = jnp.where(first, i, nq - 1 - i)
            kblk = jnp.where(first, t, t - i - 1)
            qoff = pl.multiple_of(qblk * tq, tq)
            koff = pl.multiple_of(kblk * tq, tq)
            keep = coli <= rowi + jnp.where(kblk == qblk, 0, tq)
            restart = t == i + 1
            for hh in range(2):
                m, lsum, acc = state[hh]
                if t > 0:
                    done[hh] = tuple(jnp.where(restart, a, b) for a, b in zip(state[hh], done[hh]))
                    m = jnp.where(restart, NEG, m)
                    lsum = jnp.where(restart, 0.0, lsum)
                    acc = jnp.where(restart, 0.0, acc)
                sc = _dot_nt(q_ref[hh, pl.ds(qoff, tq), :], k_ref[hh, pl.ds(koff, tq), :])
                sc = jnp.where(keep, sc, NEG)
                m_new = jnp.maximum(m, jnp.max(sc, axis=1, keepdims=True))
                p = jnp.exp(sc - m_new)
                alpha = jnp.exp(m - m_new)
                lsum = alpha * lsum + jnp.sum(p, axis=1, keepdims=True)
                acc = alpha * acc + _dot(p, v_ref[hh, pl.ds(koff, tq), :])
                state[hh] = (m_new, lsum, acc)
        for blk, res in ((i, done), (nq - 1 - i, state)):
            off = pl.multiple_of(blk * tq, tq)
            out = None
            for hh in range(2):
                m, lsum, acc = res[hh]
                o = acc * (1.0 / lsum)
                lse_ref[hh, pl.ds(off, tq), :] = m + jnp.log(lsum)
                out = o if hh == 0 else out + pltpu.roll(o, V_DIM, 1)
            o_ref[pl.ds(off, tq), :] = out

    pair = pl.BlockSpec((2, s, LANE), lambda j, i: (j, 0, 0))
    return pl.pallas_call(
        body, name="attn_fwd", grid=(nh // 2, nq // 2),
        in_specs=[pair, pair, pair],
        out_specs=[pl.BlockSpec((s, LANE), lambda j, i: (0, j)), pl.BlockSpec((2, s, 1), lambda j, i: (j, 0, 0))],
        out_shape=[_sds((s, D_MLA)), _sds((nh, s, 1))],
        compiler_params=_params(("parallel", "arbitrary")),
    )(q, k, v)


def _ssd_gate(y_ssd, s_z, g):
    yz = y_ssd * _silu(s_z)
    g0 = _iota(yz.shape, 1) < D_SSD // 2
    sq = yz * yz
    ms0 = jnp.sum(jnp.where(g0, sq, 0.0), axis=1, keepdims=True) / (D_SSD // 2)
    ms1 = jnp.sum(jnp.where(g0, 0.0, sq), axis=1, keepdims=True) / (D_SSD // 2)
    r = jnp.where(g0, lax.rsqrt(ms0 + SSD_NORM_EPS), lax.rsqrt(ms1 + SSD_NORM_EPS))
    nrm = yz * r
    return nrm * g, nrm, r, g0


def _outproj_fwd(x, proj, ya, y_ssd, o, g_ssd, w):
    s = x.shape[0]
    ts = _tile(s)

    def body(x_ref, sz_ref, cz_ref, ya_ref, ys_ref, o_ref, g_ref, w_ref, xo_ref, y_ref):
        yb = _ssd_gate(ys_ref[...], sz_ref[...], g_ref[...])[0]
        yc = o_ref[...] * _silu(cz_ref[...])
        y = jnp.concatenate([ya_ref[...], yb, yc], axis=1).astype(MXU)
        y_ref[...] = y
        xo_ref[...] = x_ref[...] + jnp.dot(y, w_ref[...], preferred_element_type=F32)

    return pl.pallas_call(
        body, name="outproj_fwd", grid=(s // ts,),
        in_specs=[_row(ts, D_MODEL), _gate_cols(ts, O_SZ), _gate_cols(ts, O_CZ), _row(ts, D_CONV_A), _row(ts, D_SSD),
                  _row(ts, D_MLA), _full((1, D_SSD)), _full((D_MODEL, D_MODEL))],
        out_specs=[_row(ts, D_MODEL), _row(ts, D_MODEL)],
        out_shape=[_sds((s, D_MODEL)), _sds((s, D_MODEL), MXU)],
        compiler_params=_params(("parallel",)),
    )(x, proj, proj, ya, y_ssd, o, g_ssd, w)


def _outproj_loss(x, proj, ya, y_ssd, o, g_ssd, w, final_g, tgt):
    s = x.shape[0]
    ts = _tile(s)

    def body(x_ref, sz_ref, cz_ref, ya_ref, ys_ref, o_ref, g_ref, w_ref, fg_ref, t_ref, dx_ref, dg_ref, loss_ref, y_ref):
        @pl.when(pl.program_id(0) == 0)
        def _():
            dg_ref[...] = jnp.zeros_like(dg_ref)
            loss_ref[...] = jnp.zeros_like(loss_ref)

        yb = _ssd_gate(ys_ref[...], sz_ref[...], g_ref[...])[0]
        yc = o_ref[...] * _silu(cz_ref[...])
        y = jnp.concatenate([ya_ref[...], yb, yc], axis=1).astype(MXU)
        y_ref[...] = y
        xv = x_ref[...] + jnp.dot(y, w_ref[...], preferred_element_type=F32)
        gv = fg_ref[...]
        yn, r = _rms(xv, gv)
        e = yn - t_ref[...]
        loss_ref[...] += jnp.sum(jnp.sum(e * e, axis=1, keepdims=True), axis=0, keepdims=True) * (0.5 / D_MODEL)
        dx, dg = _rms_bwd(e * (1.0 / D_MODEL), xv, r, gv)
        dx_ref[...] = dx
        dg_ref[...] += dg

    return pl.pallas_call(
        body, name="outproj_loss", grid=(s // ts,),
        in_specs=[_row(ts, D_MODEL), _gate_cols(ts, O_SZ), _gate_cols(ts, O_CZ), _row(ts, D_CONV_A), _row(ts, D_SSD),
                  _row(ts, D_MLA), _full((1, D_SSD)), _full((D_MODEL, D_MODEL)), _full((1, D_MODEL)), _row(ts, D_MODEL)],
        out_specs=[_row(ts, D_MODEL), _full((1, D_MODEL)), _full((1, LANE)), _row(ts, D_MODEL)],
        out_shape=[_sds((s, D_MODEL)), _sds((1, D_MODEL)), _sds((1, LANE)), _sds((s, D_MODEL), MXU)],
        compiler_params=_params(("arbitrary",)),
    )(x, proj, proj, ya, y_ssd, o, g_ssd, w, final_g, tgt)


def _outproj_bwd(dout, y, w, proj, y_ssd, o, g_ssd, dep=None):
    s = dout.shape[0]
    ts = _tile(s)

    def body(dout_ref, y_ref, w_ref, sz_ref, cz_ref, ys_ref, o_ref, g_ref,
             dya_ref, dys_ref, dsz_ref, dattn_ref, dcz_ref, dg_ref, dw_ref):
        @pl.when(pl.program_id(0) == 0)
        def _():
            dw_ref[...] = jnp.zeros_like(dw_ref)
            dg_ref[...] = jnp.zeros_like(dg_ref)

        dout_b = dout_ref[...].astype(MXU)
        dw_ref[...] += _dot_tn(y_ref[...], dout_b)
        dy = _dot_nt(dout_b, w_ref[...])
        dya_ref[...] = dy[:, :D_CONV_A]
        dyb = dy[:, D_CONV_A:D_CONV_A + D_SSD]
        sz = sz_ref[...]
        ys = ys_ref[...]
        gv = g_ref[...]
        _, nrm, r, g0 = _ssd_gate(ys, sz, gv)
        dg_ref[...] += jnp.sum(dyb * nrm, axis=0, keepdims=True)
        dn = dyb * gv
        t = dn * nrm
        mean = jnp.where(g0, jnp.sum(jnp.where(g0, t, 0.0), axis=1, keepdims=True),
                         jnp.sum(jnp.where(g0, 0.0, t), axis=1, keepdims=True)) / (D_SSD // 2)
        dyz = r * (dn - nrm * mean)
        dys_ref[...] = dyz * _silu(sz)
        dsz_ref[...] = (dyz * ys * _dsilu(sz)).astype(MXU)
        dyc = dy[:, D_CONV_A + D_SSD:]
        cz = cz_ref[...]
        dattn_ref[...] = dyc * _silu(cz)
        dcz_ref[...] = (dyc * o_ref[...] * _dsilu(cz)).astype(MXU)

    return _call_after(
        dep, body, (dout, y, w, proj, proj, y_ssd, o, g_ssd), name="outproj_bwd", grid=(s // ts,),
        in_specs=[_row(ts, D_MODEL), _row(ts, D_MODEL), _full((D_MODEL, D_MODEL)), _gate_cols(ts, O_SZ), _gate_cols(ts, O_CZ),
                  _row(ts, D_SSD), _row(ts, D_MLA), _full((1, D_SSD))],
        out_specs=[_row(ts, D_CONV_A), _row(ts, D_SSD), _row(ts, D_SSD), _row(ts, D_MLA), _row(ts, D_MLA),
                   _full((1, D_SSD)), _full((D_MODEL, D_MODEL))],
        out_shape=[_sds((s, D_CONV_A)), _sds((s, D_SSD)), _sds((s, D_SSD), MXU), _sds((s, D_MLA)), _sds((s, D_MLA), MXU),
                   _sds((1, D_SSD)), _sds((D_MODEL, D_MODEL))],
        compiler_params=_params(("arbitrary",)),
    )


def _attn_bwd(q, k, v, o, d_o, lse, dep=None):
    nh, s, _ = q.shape
    tq = _att_tile(s, ATT_BWD_TILE)
    nq = s // tq

    def body(q_ref, k_ref, v_ref, o_ref, do_ref, lse_ref, dq_ref, dk_ref, dv_ref, dop, delta):
        i = pl.program_id(1)

        @pl.when(i == 0)
        def _():
            lane = _iota((s, LANE), 1)
            for hh in range(2):
                dov = do_ref[...]
                ov = o_ref[...]
                if hh == 1:
                    dov = pltpu.roll(dov, V_DIM, 1)
                    ov = pltpu.roll(ov, V_DIM, 1)
                dov = jnp.where(lane < V_DIM, dov, 0.0)
                dop[hh] = dov.astype(MXU)
                delta[hh] = jnp.sum(dov * ov, axis=1, keepdims=True)
                dq_ref[hh] = jnp.zeros((s, LANE), F32)

        rowi = _iota((tq, tq), 0)
        coli = _iota((tq, tq), 1)
        z = jnp.zeros((tq, LANE), F32)
        state = [(z, z), (z, z)]
        done = [(z, z), (z, z)]
        for t in range(nq + 1):
            first = t <= nq - 1 - i
            kblk = jnp.where(first, i, nq - 1 - i)
            qblk = jnp.where(first, i + t, t - 1)
            qoff = pl.multiple_of(qblk * tq, tq)
            koff = pl.multiple_of(kblk * tq, tq)
            keep = coli <= rowi + jnp.where(kblk == qblk, 0, tq)
            restart = t == nq - i
            for hh in range(2):
                dk, dv = state[hh]
                if t > 0:
                    done[hh] = tuple(jnp.where(restart, a, b) for a, b in zip(state[hh], done[hh]))
                    dk = jnp.where(restart, 0.0, dk)
                    dv = jnp.where(restart, 0.0, dv)
                kb = k_ref[hh, pl.ds(koff, tq), :]
                qb = q_ref[hh, pl.ds(qoff, tq), :]
                dob = dop[hh, pl.ds(qoff, tq), :]
                sc = jnp.where(keep, _dot_nt(qb, kb), NEG)
                p = jnp.exp(sc - lse_ref[hh, pl.ds(qoff, tq), :])
                dp = _dot_nt(dob, v_ref[hh, pl.ds(koff, tq), :])
                ds = p * (dp - delta[hh, pl.ds(qoff, tq), :])
                dq_ref[hh, pl.ds(qoff, tq), :] += _dot(ds, kb)
                state[hh] = (dk + _dot_tn(ds, qb), dv + _dot_tn(p, dob))
        for blk, res in ((i, done), (nq - 1 - i, state)):
            off = pl.multiple_of(blk * tq, tq)
            for hh in range(2):
                dk_ref[hh, pl.ds(off, tq), :] = res[hh][0]
                dv_ref[hh, pl.ds(off, tq), :] = res[hh][1]

    pair = pl.BlockSpec((2, s, LANE), lambda j, i: (j, 0, 0))
    return _call_after(
        dep, body, (q, k, v, o, d_o, lse), name="attn_bwd", grid=(nh // 2, nq // 2),
        in_specs=[pair, pair, pair, pl.BlockSpec((s, LANE), lambda j, i: (0, j)), pl.BlockSpec((s, LANE), lambda j, i: (0, j)),
                  pl.BlockSpec((2, s, 1), lambda j, i: (j, 0, 0))],
        out_specs=[pair, pair, pair],
        out_shape=[_sds((nh, s, LANE))] * 3,
        scratch_shapes=[pltpu.VMEM((2, s, LANE), MXU), pltpu.VMEM((2, s, 1), F32)],
        compiler_params=_params(("parallel", "arbitrary")),
    )


def _ssd_bwd(xbc, proj, sc, states, dy, dep=None):
    s = xbc.shape[0]
    nc = s // SSD_CHUNK
    l = SSD_CHUNK
    cps = SSD_CHUNKS_PER_STEP

    def body(xbc_ref, tail_ref, sc_ref, st_ref, dy_ref, dxbc_ref, dtail_ref, dsc_ref, dstate):
        @pl.when(pl.program_id(0) == 0)
        def _():
            dstate[...] = jnp.zeros_like(dstate)
            dsc_ref[...] = jnp.zeros_like(dsc_ref)

        sc_v = sc_ref[...]
        lane1 = _iota((1, LANE), 1)
        rowp = _iota((LANE, 1), 0)
        rowl = _iota((l, 1), 0)
        d_row = sc_v[2:3, :]
        dstates = [dstate[j] for j in range(3)]
        for u in reversed(range(cps)):
            dstates = chunk(u, xbc_ref, tail_ref, sc_v, st_ref, dy_ref, dxbc_ref, dtail_ref, dsc_ref, dstates,
                            lane1, rowp, rowl, d_row)
        for j in range(3):
            dstate[j] = dstates[j]

    def chunk(u, xbc_ref, tail_ref, sc_v, st_ref, dy_ref, dxbc_ref, dtail_ref, dsc_ref, dstates, lane1, rowp, rowl, d_row):
        r = slice(u * l, (u + 1) * l)
        dstates = list(dstates)
        lane, row, tri, a_row, pre, dt, a_cs, a_t = _ssd_chunk_common(tail_ref[r, :], sc_v)
        da_col = jnp.zeros((l, LANE), F32)
        da_row = jnp.zeros((LANE, l), F32)
        dt_x = jnp.zeros((l, LANE), F32)
        dd_row = jnp.zeros((1, LANE), F32)
        db = [jnp.zeros((l, LANE), F32), jnp.zeros((l, LANE), F32)]
        dc = [jnp.zeros((l, LANE), F32), jnp.zeros((l, LANE), F32)]
        for j in range(3):
            xpair = xbc_ref[r, LANE * j:LANE * (j + 1)]
            dypair = dy_ref[r, LANE * j:LANE * (j + 1)]
            sp = st_ref[u, j]
            dsp = dstates[j]
            dxpair = jnp.zeros((l, LANE), F32)
            ds_new = jnp.zeros((LANE, LANE), F32)
            decay = jnp.zeros((LANE, 1), F32)
            for half in range(2):
                h = 2 * j + half
                g = h // 3
                hm = (lane < 64) if half == 0 else (lane >= 64)
                hrow = (rowp < 64) if half == 0 else (rowp >= 64)
                ac = _pick_col(a_cs, lane, DT_LANE + h)
                ar = _pick_row(a_t, row, DT_LANE + h)
                dtc = _pick_col(dt, lane, DT_LANE + h)
                alast = jnp.sum(jnp.where(lane1 == l - 1, ar, 0.0), axis=1, keepdims=True)
                dh = jnp.sum(jnp.where(lane1 == DT_LANE + h, d_row, 0.0), axis=1, keepdims=True)
                xm = jnp.where(hm, xpair, 0.0)
                xd = xm * dtc
                dym = jnp.where(hm, dypair, 0.0)
                bm = xbc_ref[r, D_SSD + LANE * g:D_SSD + LANE * (g + 1)]
                cm = xbc_ref[r, D_SSD + SSD_BC + LANE * g:D_SSD + SSD_BC + LANE * (g + 1)]
                lm = jnp.where(row >= lane, jnp.exp(jnp.minimum(ac - ar, 0.0)), 0.0)
                e_in = jnp.exp(ac)
                f_out = jnp.exp(alast - ac)
                e_last = jnp.exp(alast)
                m = _dot_nt(cm, bm) * lm
                y_off = jnp.where(hm, _dot_nt(cm, sp), 0.0) * e_in
                dm = _dot_nt(dym, xd)
                dxd = _dot_tn(m, dym)
                dg = dm * lm
                dye = dym * e_in
                dc[g] = dc[g] + _dot(dg, bm) + _dot(dye, sp)
                db[g] = db[g] + _dot_tn(dg, cm)
                qm = dm * m
                dac = jnp.sum(qm, axis=1, keepdims=True) + jnp.sum(dym * y_off, axis=1, keepdims=True)
                dar = -jnp.sum(qm, axis=0, keepdims=True)
                dxf = jnp.where(hm, _dot_nt(bm, dsp), 0.0)
                db[g] = db[g] + _dot(xd * f_out, dsp)
                dxd = dxd + dxf * f_out
                df = jnp.sum(dxf * xd, axis=1, keepdims=True) * f_out
                dac = dac - df
                s_last = jnp.sum(df, axis=0, keepdims=True)
                ss = jnp.sum(jnp.where(hrow, dsp * sp, 0.0), axis=1, keepdims=True)
                s_last = s_last + e_last * jnp.sum(ss, axis=0, keepdims=True)
                dac = dac + jnp.where(rowl == l - 1, s_last, 0.0)
                ds_new = ds_new + _dot_tn(dye, cm)
                decay = jnp.where(hrow, e_last, decay)
                dxpair = dxpair + dxd * dtc + dym * dh
                dt_x = dt_x + jnp.where(lane == DT_LANE + h, jnp.sum(dxd * xm, axis=1, keepdims=True), 0.0)
                dsum = jnp.sum(jnp.sum(dym * xm, axis=1, keepdims=True), axis=0, keepdims=True)
                dd_row = dd_row + jnp.where(lane1 == DT_LANE + h, dsum, 0.0)
                da_col = da_col + jnp.where(lane == DT_LANE + h, dac, 0.0)
                da_row = da_row + jnp.where(row == DT_LANE + h, dar, 0.0)
            dstates[j] = dsp * decay + ds_new
            dxbc_ref[r, LANE * j:LANE * (j + 1)] = dxpair
        for g in range(2):
            dxbc_ref[r, D_SSD + LANE * g:D_SSD + LANE * (g + 1)] = db[g]
            dxbc_ref[r, D_SSD + SSD_BC + LANE * g:D_SSD + SSD_BC + LANE * (g + 1)] = dc[g]
        dla = _dot_hi_tn(tri, da_col + da_row.T)
        ddt = dt_x + dla * a_row
        dpre = ddt * _sigmoid(pre)
        dtm = (lane >= DT_LANE) & (lane < DT_LANE + SSD_HEADS)
        dtail_ref[r, :] = jnp.where(dtm, dpre, 0.0).astype(MXU)
        dtm1 = (lane1 >= DT_LANE) & (lane1 < DT_LANE + SSD_HEADS)
        dsc_ref[0:1, :] += jnp.where(dtm1, jnp.sum(dpre, axis=0, keepdims=True), 0.0)
        dsc_ref[1:2, :] += jnp.where(dtm1, jnp.sum(dla * dt, axis=0, keepdims=True) * a_row, 0.0)
        dsc_ref[2:3, :] += dd_row
        return dstates

    rev = lambda c: nc // cps - 1 - c
    return _call_after(
        dep, body, (xbc, proj, sc, states, dy), name="ssd_bwd", grid=(nc // cps,),
        in_specs=[pl.BlockSpec((cps * l, N_XBC), lambda c: (rev(c), 0)),
                  pl.BlockSpec((cps * l, LANE), lambda c: (rev(c), O_TAIL // LANE)), _full((8, LANE)),
                  pl.BlockSpec((cps, 3, LANE, LANE), lambda c: (rev(c), 0, 0, 0)),
                  pl.BlockSpec((cps * l, D_SSD), lambda c: (rev(c), 0))],
        out_specs=[pl.BlockSpec((cps * l, N_XBC), lambda c: (rev(c), 0)), pl.BlockSpec((cps * l, LANE), lambda c: (rev(c), 0)),
                   _full((8, LANE))],
        out_shape=[_sds((s, N_XBC)), _sds((s, LANE), MXU), _sds((8, LANE))],
        scratch_shapes=[pltpu.VMEM((3, LANE, LANE), F32)],
        compiler_params=_params(("arbitrary",)),
    )


def _sconv_bwd(proj, w, b, dxbc, dep=None):
    s = proj.shape[0]

    def body(u_ref, w_ref, b_ref, d_ref, du_ref, dw_ref, db_ref):
        u = u_ref[...]
        wv = w_ref[...]
        dpre = d_ref[...] * _dsilu(_sconv_pre(u, wv, b_ref[...]))
        ahead = [_shift_up(dpre, j) for j in range(4)]
        du_ref[...] = (wv[3:4, :] * ahead[0] + wv[2:3, :] * ahead[1] + wv[1:2, :] * ahead[2]
                       + wv[0:1, :] * ahead[3]).astype(MXU)
        for k in range(4):
            dw_ref[k:k + 1, :] = jnp.sum(ahead[3 - k] * u, axis=0, keepdims=True)
        db_ref[...] = jnp.sum(dpre, axis=0, keepdims=True)

    blk = pl.BlockSpec((s, LANE), lambda j: (0, j))
    return _call_after(
        dep, body, (proj, w, b, dxbc), name="sconv_bwd", grid=(N_XBC // LANE,),
        in_specs=[_col(s, O_XBC), pl.BlockSpec((4, LANE), lambda j: (0, j)), pl.BlockSpec((1, LANE), lambda j: (0, j)), blk],
        out_specs=[blk, pl.BlockSpec((4, LANE), lambda j: (0, j)), pl.BlockSpec((1, LANE), lambda j: (0, j))],
        out_shape=[_sds((s, N_XBC), MXU), _sds((4, N_XBC)), _sds((1, N_XBC))],
        compiler_params=_params(("parallel",)),
    )


def _conva_bwd(proj, w, dya, dep=None):
    s = proj.shape[0]

    def body(h_ref, b_ref, c_ref, z_ref, w_ref, d_ref, da_ref, dw_ref):
        ah, ab, acv, az = h_ref[...], b_ref[...], c_ref[...], z_ref[...]
        wv = w_ref[...]
        u = acv * ah
        cv = wv[2:3, :] * u + wv[1:2, :] * _shift_down(u, 1) + wv[0:1, :] * _shift_down(u, 2)
        dy = d_ref[...]
        sz = _silu(az)
        da_ref[1] = (dy * cv * sz).astype(MXU)
        da_ref[3] = (dy * ab * cv * _dsilu(az)).astype(MXU)
        dcv = dy * ab * sz
        ahead = [_shift_up(dcv, j) for j in range(3)]
        du = wv[2:3, :] * ahead[0] + wv[1:2, :] * ahead[1] + wv[0:1, :] * ahead[2]
        da_ref[0] = (du * acv).astype(MXU)
        da_ref[2] = (du * ah).astype(MXU)
        for k in range(3):
            dw_ref[k:k + 1, :] = jnp.sum(ahead[2 - k] * u, axis=0, keepdims=True)

    return _call_after(
        dep, body, (proj, proj, proj, proj, w, dya), name="conva_bwd", grid=(D_CONV_A // LANE,),
        in_specs=[_col(s, O_AH), _col(s, O_AB), _col(s, O_AC), _col(s, O_AZ), pl.BlockSpec((3, LANE), lambda j: (0, j)),
                  pl.BlockSpec((s, LANE), lambda j: (0, j))],
        out_specs=[pl.BlockSpec((4, s, LANE), lambda j: (0, 0, j)), pl.BlockSpec((3, LANE), lambda j: (0, j))],
        out_shape=[_sds((4, s, D_CONV_A), MXU), _sds((3, D_CONV_A))],
        compiler_params=_params(("parallel",)),
    )


def _mla_prep_bwd(dq, dk, dv, proj, qn, kvn, rq, rkv, gq, gkv, wq, wkv, cos, sin):
    s = proj.shape[0]
    ts = _tile(s)
    nh = MLA_HEADS

    def body(dq_ref, dk_ref, dv_ref, cqa_ref, ckv_ref, qn_ref, kvn_ref, rq_ref, rkv_ref, gq_ref, gkv_ref,
             wq_ref, wkv_ref, cos_ref, sin_ref, dcqa_ref, dckv_ref, dtail_ref, dwq_ref, dwkv_ref, dgq_ref, dgkv_ref):
        @pl.when(pl.program_id(0) == 0)
        def _():
            dwq_ref[...] = jnp.zeros_like(dwq_ref)
            dwkv_ref[...] = jnp.zeros_like(dwkv_ref)
            dgq_ref[...] = jnp.zeros_like(dgq_ref)
            dgkv_ref[...] = jnp.zeros_like(dgkv_ref)

        cosv = cos_ref[...]
        sinv = sin_ref[...]
        lane = _iota((ts, LANE), 1)
        rope_lanes = (lane >= ROPE_LANE) & (lane < ROPE_LANE + QK_ROPE)

        def unrope(gr):
            return gr * cosv + _rope_swap(gr * sinv)

        dqs, dks, dvs = [], [], []
        dkr = jnp.zeros((ts, LANE), F32)
        for h in range(nh):
            dqs.append(unrope(dq_ref[h] * ATT_SCALE).astype(MXU))
            dkh = dk_ref[h]
            dks.append(jnp.where(lane < QK_NOPE, dkh, 0.0).astype(MXU))
            dkr = dkr + jnp.where(rope_lanes, dkh, 0.0)
            dvs.append(dv_ref[h].astype(MXU))
        dtail_ref[...] = pltpu.roll(jnp.where(rope_lanes, unrope(dkr), 0.0), ROPE_LANE, 1).astype(MXU)
        dq_all = jnp.concatenate(dqs, axis=1)
        dkv_all = jnp.concatenate(dks + dvs, axis=1)
        dwq_ref[...] += _dot_tn(dq_all, qn_ref[...])
        dwkv_ref[...] += _dot_tn(dkv_all, kvn_ref[...])
        dcqa, dgq = _rms_bwd(_dot(dq_all, wq_ref[...]), cqa_ref[...], rq_ref[...], gq_ref[...])
        dckv, dgkv = _rms_bwd(_dot(dkv_all, wkv_ref[...]), ckv_ref[...], rkv_ref[...], gkv_ref[...])
        dcqa_ref[...] = dcqa.astype(MXU)
        dckv_ref[...] = dckv.astype(MXU)
        dgq_ref[...] += dgq
        dgkv_ref[...] += dgkv

    head = pl.BlockSpec((nh, ts, LANE), lambda i: (0, i, 0))
    return pl.pallas_call(
        body, name="mla_prep_bwd", grid=(s // ts,),
        in_specs=[head, head, head,
                  pl.BlockSpec((ts, Q_LORA), lambda i: (i, O_CQA // Q_LORA)),
                  pl.BlockSpec((ts, KV_LORA), lambda i: (i, O_CKV // KV_LORA)),
                  _row(ts, Q_LORA), _row(ts, KV_LORA), _row(ts, 1), _row(ts, 1),
                  _full((1, Q_LORA)), _full((1, KV_LORA)), _full((nh * LANE, Q_LORA)), _full((2 * nh * LANE, KV_LORA)),
                  _row(ts, LANE), _row(ts, LANE)],
        out_specs=[_row(ts, Q_LORA), _row(ts, KV_LORA), _row(ts, LANE), _full((nh * LANE, Q_LORA)),
                   _full((2 * nh * LANE, KV_LORA)), _full((1, Q_LORA)), _full((1, KV_LORA))],
        out_shape=[_sds((s, Q_LORA), MXU), _sds((s, KV_LORA), MXU), _sds((s, LANE), MXU), _sds((nh * LANE, Q_LORA)),
                   _sds((2 * nh * LANE, KV_LORA)), _sds((1, Q_LORA)), _sds((1, KV_LORA))],
        compiler_params=_params(("arbitrary",)),
    )(dq, dk, dv, proj, proj, qn, kvn, rq, rkv, gq, gkv, wq, wkv, cos, sin)


def _inproj_bwd(da4, dsz, dxbc_in, dcqa, dckv, dcz, dtail_a, dtail_b, w, x, rstd, g, dout, dep=None):
    s = x.shape[0]
    ts = _tile(s)

    def body(da_ref, dsz_ref, dxbc_ref, dcqa_ref, dckv_ref, dcz_ref, dta_ref, dtb_ref, w_ref, x_ref, r_ref, g_ref, dout_ref,
             dproj_ref, dx_ref, dg_ref):
        @pl.when(pl.program_id(0) == 0)
        def _():
            dg_ref[...] = jnp.zeros_like(dg_ref)

        dproj = jnp.concatenate(
            [da_ref[0], da_ref[1], da_ref[2], da_ref[3], dxbc_ref[...], dsz_ref[...], dcqa_ref[...], dckv_ref[...],
             dcz_ref[...], dta_ref[...] + dtb_ref[...]], axis=1)
        dproj_ref[...] = dproj
        dh = _dot_nt(dproj, w_ref[...])
        dx, dg = _rms_bwd(dh, x_ref[...], r_ref[...], g_ref[...])
        dx_ref[...] = dout_ref[...] + dx
        dg_ref[...] += dg

    return _call_after(
        dep, body, (da4, dsz, dxbc_in, dcqa, dckv, dcz, dtail_a, dtail_b, w, x, rstd, g, dout), name="inproj_bwd", grid=(s // ts,),
        in_specs=[pl.BlockSpec((4, ts, D_CONV_A), lambda i: (0, i, 0)), _row(ts, D_SSD), _row(ts, N_XBC), _row(ts, Q_LORA),
                  _row(ts, KV_LORA), _row(ts, D_MLA), _row(ts, LANE), _row(ts, LANE), _full((D_MODEL, NCOL)),
                  _row(ts, D_MODEL), _row(ts, 1), _full((1, D_MODEL)), _row(ts, D_MODEL)],
        out_specs=[_row(ts, NCOL), _row(ts, D_MODEL), _full((1, D_MODEL))],
        out_shape=[_sds((s, NCOL), MXU), _sds((s, D_MODEL)), _sds((1, D_MODEL))],
        compiler_params=_params(("arbitrary",)),
    )


DWIN_BLOCK = 640


def _dwin(h, dproj, dep=None):
    s = h.shape[0]

    def body(h_ref, d_ref, o_ref):
        o_ref[...] = _dot_tn(h_ref[...], d_ref[...])

    return _call_after(
        dep, body, (h, dproj), name="dwin", grid=(NCOL // DWIN_BLOCK,),
        in_specs=[_full((s, D_MODEL)), pl.BlockSpec((s, DWIN_BLOCK), lambda j: (0, j))],
        out_specs=pl.BlockSpec((D_MODEL, DWIN_BLOCK), lambda j: (0, j)),
        out_shape=_sds((D_MODEL, NCOL)),
        compiler_params=_params(("parallel",)),
    )


def _adamw(ws, gs, ms, vs, whole):
    n = len(ws)
    bc1 = 1.0 - ADAM_B1 ** ADAM_STEP
    bc2 = 1.0 - ADAM_B2 ** ADAM_STEP

    def body(*refs):
        ins, outs = refs[:4 * n], refs[4 * n:]
        for a in range(n):
            w_ref, g_ref, m_ref, v_ref = ins[a], ins[n + a], ins[2 * n + a], ins[3 * n + a]
            gv = g_ref[...]
            mn = ADAM_B1 * m_ref[...] + (1.0 - ADAM_B1) * gv
            vn = ADAM_B2 * v_ref[...] + (1.0 - ADAM_B2) * (gv * gv)
            outs[n + a][...] = mn
            outs[2 * n + a][...] = vn
            outs[a][...] = -ADAM_LR * ((mn / bc1) / (jnp.sqrt(vn / bc2) + ADAM_EPS) + ADAM_WD * w_ref[...])

    if whole:
        grid, blks = (1,), [pl.BlockSpec(w.shape, lambda i, _n=w.ndim: (0,) * _n) for w in ws]
    else:
        grid = (ws[0].shape[0], 2)
        blks = [pl.BlockSpec((1, w.shape[1] // 2, w.shape[2]), lambda i, k: (i, k, 0)) for w in ws]
    out = pl.pallas_call(
        body, name="adamw", grid=grid,
        in_specs=blks * 4, out_specs=blks * 3, out_shape=[_sds(w.shape) for w in ws] * 3,
        compiler_params=_params(("parallel",) * len(grid)),
    )(*ws, *gs, *ms, *vs)
    return [(out[a], out[n + a], out[2 * n + a]) for a in range(n)]


ADAMW_COLS_BLOCK = 512


def _adamw_cols(w_t, gs, m_t, v_t):
    cols, nl, rows = w_t.shape
    bc1 = 1.0 - ADAM_B1 ** ADAM_STEP
    bc2 = 1.0 - ADAM_B2 ** ADAM_STEP

    def body(w_ref, m_ref, v_ref, *rest):
        g_refs, (go_ref, d_ref, mo_ref, vo_ref), g_blk = rest[:nl], rest[nl:nl + 4], rest[-1]
        for l in range(nl):
            g_blk[:, l, :] = g_refs[l][...].T
        gv = g_blk[...]
        mn = ADAM_B1 * m_ref[...] + (1.0 - ADAM_B1) * gv
        vn = ADAM_B2 * v_ref[...] + (1.0 - ADAM_B2) * (gv * gv)
        go_ref[...] = gv
        mo_ref[...] = mn
        vo_ref[...] = vn
        d_ref[...] = -ADAM_LR * ((mn / bc1) / (jnp.sqrt(vn / bc2) + ADAM_EPS) + ADAM_WD * w_ref[...])

    tc = ADAMW_COLS_BLOCK
    blk = pl.BlockSpec((tc, nl, rows), lambda j: (j, 0, 0))
    gblk = pl.BlockSpec((rows, tc), lambda j: (0, j))
    return pl.pallas_call(
        body, name="adamw_cols", grid=(pl.cdiv(cols, tc),),
        in_specs=[blk] * 3 + [gblk] * nl, out_specs=[blk] * 4, out_shape=[_sds(w_t.shape)] * 4,
        scratch_shapes=[pltpu.VMEM((tc, nl, rows), F32)],
        compiler_params=_params(("parallel",)),
    )(w_t, m_t, v_t, *gs)


COL_MOVES = ((0, 0, 1024), (1024, O_SZ, 384), (1408, O_XBC, 896), (2304, O_TAIL + DT_LANE, 6), (2310, O_CQA, 256),
             (2566, O_CKV, 128), (2694, O_TAIL, 32), (2726, O_CZ, 384))


def _move_cols(w, moves, width):
    out = None
    for src, dst, n in moves:
        piece = jnp.pad(w[..., src:src + n], [(0, 0)] * (w.ndim - 1) + [(dst, width - dst - n)])
        out = piece if out is None else out + piece
    return out


def _perm_cols(w):
    return _move_cols(w, COL_MOVES, NCOL)


def _unperm_cols(g):
    return _move_cols(g, [(dst, src, n) for src, dst, n in COL_MOVES], IN_COLS)


def _wq_layout(wt):
    return jnp.pad(wt.reshape(MLA_HEADS, QK_NOPE + QK_ROPE, Q_LORA), ((0, 0), (0, 32), (0, 0))).reshape(MLA_HEADS * LANE, Q_LORA)


def _wq_unlayout(g):
    return g.reshape(MLA_HEADS, LANE, Q_LORA)[:, :QK_NOPE + QK_ROPE].reshape(MLA_HEADS * (QK_NOPE + QK_ROPE), Q_LORA)


def _wkv_layout(wt):
    t = wt.reshape(MLA_HEADS, 2, 64, KV_LORA).transpose(1, 0, 2, 3)
    return jnp.pad(t, ((0, 0), (0, 0), (0, 64), (0, 0))).reshape(2 * MLA_HEADS * LANE, KV_LORA)


def _wkv_unlayout(g):
    t = g.reshape(2, MLA_HEADS, LANE, KV_LORA)[:, :, :64]
    return t.transpose(1, 0, 2, 3).reshape(MLA_HEADS * LANE, KV_LORA)


def _rope_tables(positions):
    inv_freq = ROPE_BASE ** (-jnp.arange(0, QK_ROPE, 2, dtype=F32) / QK_ROPE)
    ang = positions.astype(F32)[:, None] * inv_freq
    cos, sin = jnp.cos(ang), jnp.sin(ang)
    s = positions.shape[0]
    one, zero = jnp.ones((s, ROPE_LANE), F32), jnp.zeros((s, ROPE_LANE), F32)
    cos_t = jnp.concatenate([one, cos, cos, one[:, :32]], axis=1)
    sin_t = jnp.concatenate([zero, -sin, sin, zero[:, :32]], axis=1)
    return cos_t, sin_t


def _ssd_scalars(dt_bias, a_log, d_skip):
    return jnp.pad(jnp.stack([dt_bias, a_log, d_skip]), ((0, 5), (DT_LANE, LANE - DT_LANE - SSD_HEADS)))


def _layer_fwd(x, lw, cos, sin, dep=None, late=None, head=None):
    proj, h, rstd = _inproj_fwd(x, lw["norm_g"], lw["w_in"], dep)
    ya = _conva_fwd(proj, lw["conv_a_w"])
    xbc = _sconv_fwd(proj, lw["ssd_conv_w"], lw["ssd_conv_b"])
    y_ssd, states = _ssd_fwd(xbc, proj, lw["sc"])
    if late is not None:
        lw = {**lw, **late(ya, y_ssd)}
    q, k, v, qn, kvn, rq, rkv = _mla_prep_fwd(proj, lw["gq"], lw["gkv"], lw["wq"], lw["wkv"], cos, sin)
    o, lse = _attn_fwd(q, k, v)
    if head is None:
        x_out, y = _outproj_fwd(x, proj, ya, y_ssd, o, lw["g_ssd"], lw["w_out"])
    else:
        *x_out, y = _outproj_loss(x, proj, ya, y_ssd, o, lw["g_ssd"], lw["w_out"], *head)
    saved = dict(x=x, proj=proj, h=h, rstd=rstd, xbc=xbc, y_ssd=y_ssd, states=states, q=q, k=k, v=v, qn=qn, kvn=kvn,
                 rq=rq, rkv=rkv, o=o, lse=lse, y=y)
    return x_out, saved, lw


def _layer_bwd(dout, lw, sv, cos, sin, rs=None, begin_early=None):
    tok = lambda: None if rs is None else rs["h"]["token"]
    dya, dys, dsz, d_o, dcz, dg_ssd, dw_out = _outproj_bwd(dout, sv["y"], lw["w_out"], sv["proj"], sv["y_ssd"], sv["o"],
                                                            lw["g_ssd"], tok())
    if rs is not None:
        rs = _rs_add_mine(rs, [dya])
    dq, dk, dv = _attn_bwd(sv["q"], sv["k"], sv["v"], sv["o"], d_o, sv["lse"], tok())
    dxbc, dtail_s, dsc = _ssd_bwd(sv["xbc"], sv["proj"], lw["sc"], sv["states"], dys, tok())
    da4, dw_conva = _conva_bwd(sv["proj"], lw["conv_a_w"], dya, tok())
    if rs is not None:
        rs = _rs_add_chips(rs, [dq, dxbc, da4])
    du, dw_sconv, db_sconv = _sconv_bwd(sv["proj"], lw["ssd_conv_w"], lw["ssd_conv_b"], dxbc, tok())
    dcqa, dckv, dtail_m, dwq, dwkv, dgq, dgkv = _mla_prep_bwd(
        dq, dk, dv, sv["proj"], sv["qn"], sv["kvn"], sv["rq"], sv["rkv"], lw["gq"], lw["gkv"], lw["wq"], lw["wkv"], cos, sin)
    early = None if begin_early is None else begin_early(dw_out, dwq, dwkv)
    etok = lambda: None if early is None else early["h"]["token"]
    dproj, dx, dg = _inproj_bwd(da4, dsz, du, dcqa, dckv, dcz, dtail_s, dtail_m, lw["w_in"], sv["x"], sv["rstd"],
                                lw["norm_g"], dout, etok())
    reduced = None if rs is None else _rs_end(rs, [du, dcqa, dx])
    if early is not None:
        early = _rs_add_mine(early, [dx])
    dw_in = _dwin(sv["h"], dproj, etok())
    own = None
    if early is not None:
        early, own = _rs_add_chips(early, [dw_in], also=([dw_in], "own"))
    grads = dict(norm_g=dg, w_in=dw_in, conv_a_w=dw_conva, ssd_conv_w=dw_sconv, ssd_conv_b=db_sconv, sc=dsc,
                 g_ssd=dg_ssd, gq=dgq, wq=dwq, gkv=dgkv, wkv=dwkv, w_out=dw_out)
    return dx, grads, reduced, early, own


ANY = pl.BlockSpec(memory_space=pl.ANY)
N_CHIPS = 4
N_DEV = 8


def _place():
    return lax.axis_index("x"), lax.axis_index("y"), lax.axis_index("c")


HBM_SPEC = pl.BlockSpec(memory_space=pltpu.HBM)
SEM_SPEC = pl.BlockSpec(memory_space=pltpu.SEMAPHORE)
PAYLOAD = jnp.bfloat16


def _hbm(a):
    return pltpu.with_memory_space_constraint(a, pltpu.HBM)


def _run_plan(plan, srcs, lands, send_sems, recv_sems, start, wait):
    copies = plan(srcs, lands)
    if start:
        for i, (src, dst, _, to) in enumerate(copies):
            pltpu.make_async_remote_copy(src_ref=src, dst_ref=dst, send_sem=send_sems.at[i], recv_sem=recv_sems.at[i],
                                         device_id=to, device_id_type=MESH_T).start()
    if wait:
        for i, (src, _, arrives, to) in enumerate(copies):
            cp = pltpu.make_async_remote_copy(src_ref=src, dst_ref=arrives, send_sem=send_sems.at[i],
                                              recv_sem=recv_sems.at[i], device_id=to, device_id_type=MESH_T)
            cp.wait_send()
            cp.wait_recv()


def _exchange_start_many(name, groups, deps):
    g = len(groups)
    sizes = [(len(srcs), len(shapes)) for _, _, srcs, shapes in groups]
    n_arr = sum(ns + nl for ns, nl in sizes)
    n_in = n_arr + len(deps)

    def body(*refs):
        at = 0
        for k, ((plan, _, _, _), (ns, nl)) in enumerate(zip(groups, sizes)):
            _run_plan(plan, refs[at:at + ns], refs[at + ns:at + ns + nl], refs[n_in + 2 * k], refs[n_in + 2 * k + 1], True, False)
            at += ns + nl
        refs[-1][...] = jnp.zeros_like(refs[-1])

    arrs, thru, sems = [], [], []
    for _, n_copies, srcs, shapes in groups:
        arrs += [_hbm(a) for a in srcs] + [_hbm(lax.empty(a.shape, a.dtype)) for a in shapes]
        thru += [pltpu.HBM(a.shape, a.dtype) for a in list(srcs) + list(shapes)]
        sems += [pltpu.SemaphoreType.DMA((n_copies,))] * 2
    outs = pl.pallas_call(
        body, name=name,
        out_shape=(*sems, *thru, _sds((8, LANE))),
        in_specs=[HBM_SPEC] * n_arr + [ANY] * len(deps),
        out_specs=(*[SEM_SPEC] * (2 * g), *[HBM_SPEC] * n_arr, pl.BlockSpec(memory_space=pltpu.VMEM)),
        input_output_aliases={i: 2 * g + i for i in range(n_arr)},
        compiler_params=pltpu.CompilerParams(has_side_effects=pltpu.SideEffectType.DATAFLOW_SIDE_EFFECTING),
    )(*arrs, *deps)
    res, at = [], 2 * g
    for k, (ns, nl) in enumerate(sizes):
        res.append(((outs[2 * k], outs[2 * k + 1]), list(outs[at:at + ns]), list(outs[at + ns:at + ns + nl])))
        at += ns + nl
    return res, outs[-1]


def _exchange_start(name, plan, n_copies, srcs, land_shapes, deps):
    (one,), token = _exchange_start_many(name, [(plan, n_copies, srcs, land_shapes)], deps)
    return (*one, token)


def _exchange_wait(name, plan, sems, srcs, lands, after):
    ns, nl = len(srcs), len(lands)

    def body(*refs):
        _run_plan(plan, refs[:ns], refs[ns:ns + nl], refs[ns + nl], refs[ns + nl + 1], False, True)

    outs = pl.pallas_call(
        body, name=name,
        out_shape=[pltpu.HBM(a.shape, a.dtype) for a in list(srcs) + list(lands)],
        in_specs=[HBM_SPEC] * (ns + nl) + [SEM_SPEC, SEM_SPEC] + [ANY] * len(after), out_specs=[HBM_SPEC] * (ns + nl),
        input_output_aliases={i: i for i in range(ns + nl)},
        compiler_params=pltpu.CompilerParams(has_side_effects=pltpu.SideEffectType.DATAFLOW_SIDE_EFFECTING),
    )(*srcs, *lands, sems[0], sems[1], *after)
    return list(outs[:ns]), list(outs[ns:])


def _xchg_begin(name, plan, n_copies, srcs, land_shapes, deps=()):
    sems, srcs_t, lands_t, token = _exchange_start(name + "_start", plan, n_copies, srcs, land_shapes, list(deps))
    return dict(name=name, plan=plan, sems=sems, srcs=srcs_t, lands=lands_t, token=token)


def _xchg_begin_many(name, specs, deps=()):
    res, token = _exchange_start_many(name + "_start", [s[1:] for s in specs], list(deps))
    return [dict(name=s[0], plan=s[1], sems=sems, srcs=srcs_t, lands=lands_t, token=token)
            for s, (sems, srcs_t, lands_t) in zip(specs, res)]


def _xchg_end(h, after):
    return _exchange_wait(h["name"] + "_wait", h["plan"], h["sems"], h["srcs"], h["lands"], after)


def _other_chips():
    x, y, c = _place()
    return [(1 - x, y), (x, 1 - y), (1 - x, 1 - y)]


def _gather_plan(srcs, lands):
    x, y, c = _place()
    me = 2 * x + y
    return [(srcs[a], lands[a].at[me], lands[a].at[2 * cx + cy], (cx, cy, c))
            for (cx, cy) in _other_chips() for a in range(len(srcs))]


def _gather_spec(shards, tag):
    return (f"gather_{tag}", _gather_plan, 3 * len(shards), shards, [_sds((N_CHIPS,) + a.shape, a.dtype) for a in shards])


def _gather_end(h, after):
    shards, lands = _xchg_end(h, after)
    me = 2 * lax.axis_index("x") + lax.axis_index("y")
    return [lax.dynamic_update_index_in_dim(g, s, me, 0) for g, s in zip(lands, shards)]


def _gather_half_plan(srcs, lands):
    x, y, c = _place()
    me = 2 * x + y
    out = []
    for (cx, cy) in _other_chips():
        out.append((srcs[0].at[c], lands[0].at[me, c], lands[0].at[2 * cx + cy, c], (cx, cy, c)))
        out += [(srcs[a], lands[a].at[me], lands[a].at[2 * cx + cy], (cx, cy, c)) for a in range(1, len(srcs))]
    return out


def _forward_plan(bufs, _):
    x, y, c = _place()
    return [(bufs[0].at[2 * cx + cy, c], bufs[0].at[2 * cx + cy, c], bufs[0].at[2 * cx + cy, 1 - c], (x, y, 1 - c))
            for (cx, cy) in _other_chips()]


def _swap_plan(srcs, lands):
    x, y, c = _place()
    return [(srcs[a].at[:, 1 - c], lands[a], lands[a], (x, y, 1 - c)) for a in range(len(srcs))]


def _chips_plan(srcs, lands):
    x, y, c = _place()
    me = 2 * x + y
    return [(srcs[a].at[2 * cx + cy], lands[a].at[me], lands[a].at[2 * cx + cy], (cx, cy, c))
            for (cx, cy) in _other_chips() for a in range(len(srcs))]


def _share_plan(srcs, lands):
    x, y, c = _place()
    return [(srcs[a], lands[a].at[c], lands[a].at[1 - c], (x, y, 1 - c)) for a in range(len(srcs))]


def _allreduce_small(slab, dep=None):
    r = slab.shape[0]

    def body(s_ref, o_ref, gath, send_sems, recv_sems):
        x, y, c = _place()
        me = 4 * x + 2 * y + c
        gath[me] = s_ref[...]
        cps = []
        for rel in range(1, N_DEV):
            px = 1 - x if rel & 4 else x
            py = 1 - y if rel & 2 else y
            pc = 1 - c if rel & 1 else c
            cp = pltpu.make_async_remote_copy(src_ref=s_ref, dst_ref=gath.at[me], send_sem=send_sems.at[rel - 1],
                                              recv_sem=recv_sems.at[rel - 1], device_id=(px, py, pc), device_id_type=MESH_T)
            cp.start()
            cps.append(cp)
        for cp in cps:
            cp.wait()
        acc = gath[0]
        for d in range(1, N_DEV):
            acc = acc + gath[d]
        o_ref[...] = acc

    vm = pl.BlockSpec(memory_space=pltpu.VMEM)
    return _call_after(
        dep, body, (slab,), name="allreduce_small", in_specs=[vm], out_specs=vm, out_shape=_sds((r, LANE)),
        scratch_shapes=[pltpu.VMEM((N_DEV, r, LANE), F32), pltpu.SemaphoreType.DMA((N_DEV - 1,)),
                        pltpu.SemaphoreType.DMA((N_DEV - 1,))],
    )


def _add_mine(g4s, recvs, half):
    n = len(g4s)

    def body(h_ref, *refs):
        for g_ref, r_ref, o_ref in zip(refs[:n], refs[n:2 * n], refs[2 * n:]):
            o_ref[0] = (g_ref[0, 0] + r_ref[0]).astype(o_ref.dtype)

    dims = [g.shape[2:] for g in g4s]
    return pl.pallas_call(
        body, name="add_mine",
        grid_spec=pltpu.PrefetchScalarGridSpec(
            num_scalar_prefetch=1, grid=(N_CHIPS,),
            in_specs=[pl.BlockSpec((1, 1) + d, lambda j, h: (j, h[0], 0, 0)) for d in dims]
            + [pl.BlockSpec((1,) + d, lambda j, h: (j, 0, 0)) for d in dims],
            out_specs=[pl.BlockSpec((1,) + d, lambda j, h: (j, 0, 0)) for d in dims]),
        out_shape=[_sds((N_CHIPS,) + d, PAYLOAD) for d in dims],
        compiler_params=_params(("parallel",)),
    )(half, *g4s, *recvs)


def _add_chips(es, ps, me):
    n = len(es)

    def body(m_ref, *refs):
        for e_ref, p_ref, o_ref in zip(refs[:n], refs[n:2 * n], refs[2 * n:]):
            own = p_ref[0].astype(F32)
            acc = None
            for s in range(N_CHIPS):
                t = jnp.where(m_ref[0] == s, own, e_ref[s].astype(F32))
                acc = t if acc is None else acc + t
            o_ref[...] = acc

    dims = [e.shape[1:] for e in es]
    return pl.pallas_call(
        body, name="add_chips",
        grid_spec=pltpu.PrefetchScalarGridSpec(
            num_scalar_prefetch=1, grid=(1,),
            in_specs=[pl.BlockSpec((N_CHIPS,) + d, lambda i, m: (0, 0, 0)) for d in dims]
            + [pl.BlockSpec((1,) + d, lambda i, m: (m[0], 0, 0)) for d in dims],
            out_specs=[pl.BlockSpec(d, lambda i, m: (0, 0)) for d in dims]),
        out_shape=[_sds(d) for d in dims],
        compiler_params=_params(("arbitrary",)),
    )(me, *es, *ps)


def _rs_begin(gs, tag, deps=()):
    return dict(h=_xchg_begin(*_swap_spec(gs, tag), deps), tag=tag, shapes=[g.shape for g in gs])


def _swap_spec(gs, tag):
    g4 = [g.reshape(N_CHIPS, 2, g.shape[0] // (2 * N_CHIPS), g.shape[1]) for g in gs]
    return (f"rs_swap_{tag}", _swap_plan, len(gs), g4, [_sds((N_CHIPS,) + g.shape[2:]) for g in g4])


def _rs_add_mine(st, after):
    g4, recv = _xchg_end(st["h"], after)
    half = jnp.reshape(lax.axis_index("c"), (1,)).astype(jnp.int32)
    ps = _add_mine(g4, recv, half)
    st["h"] = _xchg_begin(f"rs_chips_{st['tag']}", _chips_plan, 3 * len(ps), ps, [_sds(p.shape, p.dtype) for p in ps])
    return st


def _rs_add_chips(st, after, also=None):
    ps, es = _xchg_end(st["h"], after)
    me = jnp.reshape(2 * lax.axis_index("x") + lax.axis_index("y"), (1,)).astype(jnp.int32)
    fs = _add_chips(es, ps, me)
    share = (f"rs_share_{st['tag']}", _share_plan, len(fs), fs, [_sds((2,) + f.shape) for f in fs])
    if also is None:
        st["h"] = _xchg_begin(*share)
        return st
    gs, tag = also
    st["h"], h = _xchg_begin_many(f"rs_share_{st['tag']}_swap_{tag}", [share, _swap_spec(gs, tag)])
    return st, dict(h=h, tag=tag, shapes=[g.shape for g in gs])


def _rs_end(st, after):
    fs, ss = _xchg_end(st["h"], after)
    c = lax.axis_index("c")
    return [lax.dynamic_update_index_in_dim(s, f, c, 0).reshape(shp[0] // N_CHIPS, shp[1])
            for s, f, shp in zip(ss, fs, st["shapes"])]


WEIGHTS = ["norm_g", "w_in", "conv_a_w", "ssd_conv_w", "ssd_conv_b", "ssd_dt_bias", "ssd_a_log", "ssd_d", "ssd_norm_g",
           "mla_q_norm_g", "w_qb", "mla_kv_norm_g", "w_kvb", "w_out", "final_norm_g"]
BIG = ["w_in", "w_qb", "w_kvb", "w_out"]
SLAB_ROWS = 128


def _to_slab(parts, rows):
    flat = jnp.concatenate([p.reshape(-1) for p in parts])
    return jnp.pad(flat, (0, rows * LANE - flat.shape[0])).reshape(rows, LANE)


def _from_slab(slab, shapes):
    flat = slab.reshape(-1)
    out, off = [], 0
    for shp in shapes:
        n = int(np.prod(shp))
        out.append(flat[off:off + n].reshape(shp))
        off += n
    return out


def kernel(x, positions, norm_g, w_in, conv_a_w, ssd_conv_w, ssd_conv_b, ssd_dt_bias, ssd_a_log, ssd_d, ssd_norm_g, mla_q_norm_g, w_qb, mla_kv_norm_g, w_kvb, w_out, final_norm_g, loss_target, m_norm_g, m_w_in, m_conv_a_w, m_ssd_conv_w, m_ssd_conv_b, m_ssd_dt_bias, m_ssd_a_log, m_ssd_d, m_ssd_norm_g, m_mla_q_norm_g, m_w_qb, m_mla_kv_norm_g, m_w_kvb, m_w_out, m_final_norm_g, v_norm_g, v_w_in, v_conv_a_w, v_ssd_conv_w, v_ssd_conv_b, v_ssd_dt_bias, v_ssd_a_log, v_ssd_d, v_ssd_norm_g, v_mla_q_norm_g, v_w_qb, v_mla_kv_norm_g, v_w_kvb, v_w_out, v_final_norm_g):
    w = dict(norm_g=norm_g, w_in=w_in, conv_a_w=conv_a_w, ssd_conv_w=ssd_conv_w, ssd_conv_b=ssd_conv_b,
             ssd_dt_bias=ssd_dt_bias, ssd_a_log=ssd_a_log, ssd_d=ssd_d, ssd_norm_g=ssd_norm_g, mla_q_norm_g=mla_q_norm_g,
             w_qb=w_qb, mla_kv_norm_g=mla_kv_norm_g, w_kvb=w_kvb, w_out=w_out, final_norm_g=final_norm_g)
    mom = dict(norm_g=m_norm_g, w_in=m_w_in, conv_a_w=m_conv_a_w, ssd_conv_w=m_ssd_conv_w, ssd_conv_b=m_ssd_conv_b,
               ssd_dt_bias=m_ssd_dt_bias, ssd_a_log=m_ssd_a_log, ssd_d=m_ssd_d, ssd_norm_g=m_ssd_norm_g,
               mla_q_norm_g=m_mla_q_norm_g, w_qb=m_w_qb, mla_kv_norm_g=m_mla_kv_norm_g, w_kvb=m_w_kvb, w_out=m_w_out,
               final_norm_g=m_final_norm_g)
    var = dict(norm_g=v_norm_g, w_in=v_w_in, conv_a_w=v_conv_a_w, ssd_conv_w=v_ssd_conv_w, ssd_conv_b=v_ssd_conv_b,
               ssd_dt_bias=v_ssd_dt_bias, ssd_a_log=v_ssd_a_log, ssd_d=v_ssd_d, ssd_norm_g=v_ssd_norm_g,
               mla_q_norm_g=v_mla_q_norm_g, w_qb=v_w_qb, mla_kv_norm_g=v_mla_kv_norm_g, w_kvb=v_w_kvb, w_out=v_w_out,
               final_norm_g=v_final_norm_g)
    chip = 2 * lax.axis_index("x") + lax.axis_index("y")

    def early_shard(l, zero):
        pack = jnp.pad(conv_a_w[l], ((0, 5), (0, 192))) + jnp.pad(ssd_conv_w[l], ((3, 1), (0, 32)))
        return [(_perm_cols(w_in[l]) + zero).astype(MXU), pack + zero]

    def late_shard(l, zero):
        return [(w_out[l] + zero).astype(MXU), (w_qb[l].T + zero).astype(MXU), (w_kvb[l].T + zero).astype(MXU)]

    def early_weights(l, gathered):
        g_in, g_conv = gathered
        return dict(
            norm_g=norm_g[l][None], w_in=g_in.reshape(D_MODEL, NCOL),
            conv_a_w=jnp.concatenate([g_conv[j, 0:3, 0:64] for j in range(N_CHIPS)], axis=1),
            ssd_conv_w=jnp.concatenate([g_conv[j, 3:7, 0:224] for j in range(N_CHIPS)], axis=1),
            ssd_conv_b=ssd_conv_b[l][None], sc=_ssd_scalars(ssd_dt_bias[l], ssd_a_log[l], ssd_d[l]),
            g_ssd=ssd_norm_g[l][None], gq=mla_q_norm_g[l][None], gkv=mla_kv_norm_g[l][None])

    def late_weights(gathered):
        g_out, g_qb, g_kvb = gathered
        return dict(wq=_wq_layout(g_qb.reshape(MLA_HEADS * 96, Q_LORA)), wkv=_wkv_layout(g_kvb.reshape(MLA_HEADS * LANE, KV_LORA)),
                    w_out=g_out.reshape(D_MODEL, D_MODEL))

    def late_grads(dw_out, dwq, dwkv):
        wq = jnp.pad(_wq_unlayout(dwq).reshape(N_CHIPS, 144, Q_LORA), ((0, 0), (0, 16), (0, 0)))
        return [dw_out, wq.reshape(N_CHIPS * 160, Q_LORA), _wkv_unlayout(dwkv)]

    def large_grads(g):
        return [g["w_in"]] + late_grads(g["w_out"], g["wq"], g["wkv"])

    w_in0, pack0 = early_shard(0, 0.0)
    half = w_in0.shape[0] // 2
    gather_a0 = _xchg_begin("gather_a0", _gather_half_plan, 6, [w_in0.reshape(2, half, NCOL), pack0],
                            [_sds((N_CHIPS, 2, half, NCOL), MXU), _sds((N_CHIPS,) + pack0.shape)])
    zero = gather_a0["token"][0, 0]
    cos, sin = _rope_tables(positions[0] + zero.astype(jnp.int32))
    late0, shards1 = late_shard(0, zero), early_shard(1, zero) + late_shard(1, zero)
    mine0, (g_in0, g_conv0) = _xchg_end(gather_a0, [cos, sin] + late0 + shards1)
    forward_a0 = _xchg_begin("forward_a0", _forward_plan, 3, [g_in0], [])
    gather_b0, gather_1 = _xchg_begin_many("gather_b0_1", [_gather_spec(late0, "b0"), _gather_spec(shards1, "1")],
                                           [forward_a0["token"]])
    (g_in0,), _ = _xchg_end(forward_a0, [gather_1["token"]])
    lw0 = early_weights(0, [lax.dynamic_update_index_in_dim(g, s_, chip, 0) for g, s_ in zip((g_in0, g_conv0), mine0)])
    x1, sv0, lw0 = _layer_fwd(x[0], lw0, cos, sin, gather_1["token"],
                              lambda ya, y_ssd: late_weights(_gather_end(gather_b0, [ya, y_ssd])))
    g1 = _gather_end(gather_1, [x1])
    (dx, dgf, loss), sv1, lw1 = _layer_fwd(x1, {**early_weights(1, g1[:2]), **late_weights(g1[2:])}, cos, sin,
                                           head=(final_norm_g[None], loss_target[0]))

    dx, lg1, _, _, _ = _layer_bwd(dx, lw1, sv1, cos, sin)
    grad_x, lg0, red1, rs0_late, rs0 = _layer_bwd(dx, lw0, sv0, cos, sin, _rs_begin(large_grads(lg1), 1),
                                                  lambda *g: _rs_begin(late_grads(*g), "0l"))
    lg = [lg0, lg1]
    grad = {}

    small_names = ["norm_g", "conv_a_w", "ssd_conv_w", "ssd_conv_b", "sc", "g_ssd", "gq", "gkv"]
    parts = [loss[0, 0:1], dgf]
    for nm in small_names:
        parts += [lg[l][nm][:3, DT_LANE:DT_LANE + SSD_HEADS] if nm == "sc" else lg[l][nm] for l in range(DEPTH)]
    shapes = [(1,), (D_MODEL,)] + [(DEPTH,) + shp for shp in ((D_MODEL,), (3, D_CONV_A), (4, N_XBC), (N_XBC,), (3, SSD_HEADS),
                                                              (D_SSD,), (Q_LORA,), (KV_LORA,))]
    red_slab = _allreduce_small(_to_slab(parts, SLAB_ROWS), rs0["h"]["token"])
    rs0 = _rs_add_mine(rs0, [red_slab])
    red = _from_slab(red_slab + rs0["h"]["token"][0, 0], shapes)
    loss_out = red[0][0]
    grad["final_norm_g"] = red[1]
    grad["norm_g"], conv_a_full, sconv_full, grad["ssd_conv_b"], sc_grads = red[2:7]
    grad["ssd_norm_g"], grad["mla_q_norm_g"], grad["mla_kv_norm_g"] = red[7:10]
    grad["conv_a_w"] = lax.dynamic_slice_in_dim(conv_a_full, chip * 64, 64, axis=2)
    grad["ssd_conv_w"] = lax.dynamic_slice_in_dim(sconv_full, chip * 224, 224, axis=2)
    grad["ssd_dt_bias"], grad["ssd_a_log"], grad["ssd_d"] = sc_grads[:, 0], sc_grads[:, 1], sc_grads[:, 2]

    delta, new_m, new_v = {}, {}, {}
    small = [nm for nm in WEIGHTS if nm not in BIG]
    row2 = lambda a: a[None] if a.ndim == 1 else a
    small_out = _adamw(*[[row2(a[nm]) for nm in small] for a in (w, grad, mom, var)], whole=True)
    for nm, (dv, mv, vv) in zip(small, small_out):
        delta[nm], new_m[nm], new_v[nm] = [a.reshape(w[nm].shape) for a in (dv, mv, vv)]

    r_out, r_qb, r_kvb = [jnp.stack([a, b]) for a, b in zip(_rs_end(rs0_late, [red_slab]), red1[1:])]
    late = [nm for nm in BIG if nm != "w_in"]
    view = {nm: (lambda a: a) if nm == "w_out" else (lambda a: jnp.swapaxes(a, 1, 2)) for nm in late}
    late_g = [dict(w_out=r_out, w_qb=r_qb[:, :144], w_kvb=r_kvb)[nm] for nm in late]
    late_out = _adamw(*[[view[nm](a[nm]) for nm in late] for a in (w,)], late_g,
                      *[[view[nm](a[nm]) for nm in late] for a in (mom, var)], whole=False)
    for nm, gv, (dv, mv, vv) in zip(late, late_g, late_out):
        grad[nm], delta[nm], new_m[nm], new_v[nm] = [view[nm](a) for a in (gv, dv, mv, vv)]
    g_in1 = _unperm_cols(red1[0])
    shadow_work = [a for row in small_out + late_out for a in row] + [grad[nm] for nm in small] + [g_in1]
    r_in0, = _rs_end(_rs_add_chips(rs0, shadow_work), [])
    to_cols, from_cols = (lambda a: jnp.transpose(a, (2, 0, 1))), (lambda a: jnp.transpose(a, (1, 2, 0)))
    grad["w_in"], delta["w_in"], new_m["w_in"], new_v["w_in"] = [from_cols(a) for a in _adamw_cols(
        to_cols(w["w_in"]), [_unperm_cols(r_in0), g_in1], to_cols(mom["w_in"]), to_cols(var["w_in"]))]

    return (loss_out, grad_x[None], *[grad[nm] for nm in WEIGHTS], *[delta[nm] for nm in WEIGHTS],
            *[new_m[nm] for nm in WEIGHTS], *[new_v[nm] for nm in WEIGHTS])
```

```python
import functools
import math

import numpy as np
import jax
import jax.numpy as jnp
from jax import lax
from jax.experimental import pallas as pl
from jax.experimental.pallas import tpu as pltpu

F32 = jnp.float32
MXU = jnp.bfloat16

D_MODEL = 1024
DEPTH = 2
D_CONV_A = 256
D_SSD = 384
SSD_HEADS = 6
SSD_BC = 256
SSD_CHUNK = 128
SSD_CHUNKS_PER_STEP = 4
SSD_NORM_EPS = 1e-5
MLA_HEADS = 6
Q_LORA = 256
KV_LORA = 128
QK_NOPE = 64
QK_ROPE = 32
V_DIM = 64
D_MLA = 384
ROPE_BASE = 10000.0
NORM_EPS = 1e-6
IN_COLS = 3110
LANE = 128

O_AH, O_AB, O_AC, O_AZ = 0, 256, 512, 768
O_XBC = 1024
O_SZ = 1920
O_CQA = 2304
O_CKV = 2560
O_CZ = 2688
O_TAIL = 3072
NCOL = 3200
N_XBC = D_SSD + 2 * SSD_BC
DT_LANE = 32
ROPE_LANE = 64

ADAM_LR, ADAM_B1, ADAM_B2, ADAM_EPS, ADAM_WD, ADAM_STEP = 0.001, 0.9, 0.999, 1e-08, 0.01, 10

VMEM_LIMIT = 56 * 1024 * 1024
MESH_T = pl.DeviceIdType.MESH


def _dot(a, b):
    return jnp.dot(a.astype(MXU), b.astype(MXU), preferred_element_type=F32)


def _dot_nt(a, b):
    return lax.dot_general(a.astype(MXU), b.astype(MXU), (((1,), (1,)), ((), ())), preferred_element_type=F32)


def _dot_tn(a, b):
    return lax.dot_general(a.astype(MXU), b.astype(MXU), (((0,), (0,)), ((), ())), preferred_element_type=F32)


def _dot_hi(a, b):
    return jnp.dot(a, b, precision=lax.Precision.HIGHEST, preferred_element_type=F32)


def _dot_hi_tn(a, b):
    return lax.dot_general(a, b, (((0,), (0,)), ((), ())), precision=lax.Precision.HIGHEST, preferred_element_type=F32)


def _sigmoid(z):
    return 1.0 / (1.0 + jnp.exp(-z))


def _silu(z):
    return z * _sigmoid(z)


def _dsilu(z):
    s = _sigmoid(z)
    return s * (1.0 + z * (1.0 - s))


def _softplus(z):
    e = jnp.exp(-jnp.abs(z))
    return jnp.maximum(z, 0.0) + jnp.where(e < 1e-3, e * (1.0 - 0.5 * e), jnp.log(1.0 + e))


def _iota(shape, dim):
    return lax.broadcasted_iota(jnp.int32, shape, dim)


def _shift_down(u, k):
    if k == 0:
        return u
    return jnp.where(_iota(u.shape, 0) >= k, pltpu.roll(u, k, 0), 0.0)


def _shift_up(u, k):
    if k == 0:
        return u
    n = u.shape[0]
    return jnp.where(_iota(u.shape, 0) < n - k, pltpu.roll(u, n - k, 0), 0.0)


def _rope_swap(t):
    lane = _iota(t.shape, 1)
    lo = (lane >= ROPE_LANE) & (lane < ROPE_LANE + 16)
    hi = (lane >= ROPE_LANE + 16) & (lane < ROPE_LANE + 32)
    return jnp.where(lo, pltpu.roll(t, LANE - 16, 1), jnp.where(hi, pltpu.roll(t, 16, 1), 0.0))


def _params(sem=None):
    return pltpu.CompilerParams(dimension_semantics=sem, vmem_limit_bytes=VMEM_LIMIT)


def _full(shape):
    nd = len(shape)
    return pl.BlockSpec(shape, lambda *_: (0,) * nd)


def _sds(shape, dtype=F32):
    return jax.ShapeDtypeStruct(shape, dtype)


def _tile(s):
    return min(512, s)


def _row(ts, w):
    return pl.BlockSpec((ts, w), lambda i: (i, 0))


def _gate_cols(ts, off):
    return pl.BlockSpec((ts, D_SSD), lambda i, _o=off // D_SSD: (i, _o))


def _col(s, off):
    return pl.BlockSpec((s, LANE), lambda j, _o=off // LANE: (0, _o + j))


def _call_after(dep, body, args, *, in_specs, **kw):
    if dep is None:
        return pl.pallas_call(body, in_specs=in_specs, **kw)(*args)
    n = len(args)

    def body_dep(*refs):
        body(*refs[:n], *refs[n + 1:])

    return pl.pallas_call(body_dep, in_specs=list(in_specs) + [pl.BlockSpec(memory_space=pl.ANY)], **kw)(*args, dep)


def _rms(c, g):
    r = lax.rsqrt(jnp.mean(c * c, axis=-1, keepdims=True) + NORM_EPS)
    return c * r * g, r


def _rms_bwd(dn, c, r, g):
    ch = c * r
    dch = dn * g
    dc = r * (dch - ch * jnp.mean(dch * ch, axis=-1, keepdims=True))
    return dc, jnp.sum(dn * ch, axis=0, keepdims=True)


def _inproj_fwd(x, g, w, dep=None):
    s = x.shape[0]
    ts = _tile(s)

    def body(x_ref, g_ref, w_ref, proj_ref, h_ref, r_ref):
        hn, r = _rms(x_ref[...], g_ref[...])
        h = hn.astype(MXU)
        h_ref[...] = h
        r_ref[...] = r
        proj_ref[...] = jnp.dot(h, w_ref[...], preferred_element_type=F32)

    return _call_after(
        dep, body, (x, g, w), name="inproj_fwd", grid=(s // ts,),
        in_specs=[_row(ts, D_MODEL), _full((1, D_MODEL)), _full((D_MODEL, NCOL))],
        out_specs=[_row(ts, NCOL), _row(ts, D_MODEL), _row(ts, 1)],
        out_shape=[_sds((s, NCOL)), _sds((s, D_MODEL), MXU), _sds((s, 1))],
        compiler_params=_params(("parallel",)),
    )


def _conva_fwd(proj, w):
    s = proj.shape[0]

    def body(h_ref, b_ref, c_ref, z_ref, w_ref, y_ref):
        u = c_ref[...] * h_ref[...]
        wv = w_ref[...]
        cv = wv[2:3, :] * u + wv[1:2, :] * _shift_down(u, 1) + wv[0:1, :] * _shift_down(u, 2)
        y_ref[...] = b_ref[...] * cv * _silu(z_ref[...])

    return pl.pallas_call(
        body, name="conva_fwd", grid=(D_CONV_A // LANE,),
        in_specs=[_col(s, O_AH), _col(s, O_AB), _col(s, O_AC), _col(s, O_AZ), pl.BlockSpec((3, LANE), lambda j: (0, j))],
        out_specs=pl.BlockSpec((s, LANE), lambda j: (0, j)),
        out_shape=_sds((s, D_CONV_A)),
        compiler_params=_params(("parallel",)),
    )(proj, proj, proj, proj, w)


def _sconv_pre(u, wv, bv):
    return (wv[3:4, :] * u + wv[2:3, :] * _shift_down(u, 1) + wv[1:2, :] * _shift_down(u, 2)
            + wv[0:1, :] * _shift_down(u, 3) + bv)


def _sconv_fwd(proj, w, b):
    s = proj.shape[0]

    def body(u_ref, w_ref, b_ref, o_ref):
        o_ref[...] = _silu(_sconv_pre(u_ref[...], w_ref[...], b_ref[...]))

    return pl.pallas_call(
        body, name="sconv_fwd", grid=(N_XBC // LANE,),
        in_specs=[_col(s, O_XBC), pl.BlockSpec((4, LANE), lambda j: (0, j)), pl.BlockSpec((1, LANE), lambda j: (0, j))],
        out_specs=pl.BlockSpec((s, LANE), lambda j: (0, j)),
        out_shape=_sds((s, N_XBC)),
        compiler_params=_params(("parallel",)),
    )(proj, w, b)


def _ssd_chunk_common(tail, sc):
    l = SSD_CHUNK
    lane = _iota((l, LANE), 1)
    row = _iota((l, LANE), 0)
    tri = (row >= lane).astype(F32)
    a_row = -jnp.exp(sc[1:2, :])
    pre = tail + sc[0:1, :]
    dt = _softplus(pre)
    a_cs = _dot_hi(tri, dt * a_row)
    return lane, row, tri, a_row, pre, dt, a_cs, a_cs.T


def _pick_col(m, lane, k):
    return jnp.sum(jnp.where(lane == k, m, 0.0), axis=1, keepdims=True)


def _pick_row(m, row, k):
    return jnp.sum(jnp.where(row == k, m, 0.0), axis=0, keepdims=True)


def _ssd_fwd(xbc, proj, sc):
    s = xbc.shape[0]
    nc = s // SSD_CHUNK
    l = SSD_CHUNK
    cps = SSD_CHUNKS_PER_STEP

    def body(xbc_ref, tail_ref, sc_ref, y_ref, st_ref, state):
        @pl.when(pl.program_id(0) == 0)
        def _():
            state[...] = jnp.zeros_like(state)

        sc_v = sc_ref[...]
        lane1 = _iota((1, LANE), 1)
        rowp = _iota((LANE, 1), 0)
        d_row = sc_v[2:3, :]
        states = [state[j] for j in range(3)]
        for u in range(cps):
            r = slice(u * l, (u + 1) * l)
            lane, row, _, _, _, dt, a_cs, a_t = _ssd_chunk_common(tail_ref[r, :], sc_v)
            for j in range(3):
                st_ref[u, j] = states[j]
            for j in range(3):
                xpair = xbc_ref[r, LANE * j:LANE * (j + 1)]
                sp = states[j]
                ypair = jnp.zeros((l, LANE), F32)
                new_s = jnp.zeros((LANE, LANE), F32)
                decay = jnp.zeros((LANE, 1), F32)
                for half in range(2):
                    h = 2 * j + half
                    g = h // 3
                    hm = (lane < 64) if half == 0 else (lane >= 64)
                    hrow = (rowp < 64) if half == 0 else (rowp >= 64)
                    ac = _pick_col(a_cs, lane, DT_LANE + h)
                    ar = _pick_row(a_t, row, DT_LANE + h)
                    dtc = _pick_col(dt, lane, DT_LANE + h)
                    alast = jnp.sum(jnp.where(lane1 == l - 1, ar, 0.0), axis=1, keepdims=True)
                    dh = jnp.sum(jnp.where(lane1 == DT_LANE + h, d_row, 0.0), axis=1, keepdims=True)
                    xm = jnp.where(hm, xpair, 0.0)
                    xd = xm * dtc
                    bm = xbc_ref[r, D_SSD + LANE * g:D_SSD + LANE * (g + 1)]
                    cm = xbc_ref[r, D_SSD + SSD_BC + LANE * g:D_SSD + SSD_BC + LANE * (g + 1)]
                    lm = jnp.where(row >= lane, jnp.exp(jnp.minimum(ac - ar, 0.0)), 0.0)
                    y_diag = _dot(_dot_nt(cm, bm) * lm, xd)
                    y_off = jnp.where(hm, _dot_nt(cm, sp), 0.0) * jnp.exp(ac)
                    ypair = ypair + y_diag + y_off + xm * dh
                    new_s = new_s + _dot_tn(xd * jnp.exp(alast - ac), bm)
                    decay = jnp.where(hrow, jnp.exp(alast), decay)
                states[j] = sp * decay + new_s
                y_ref[r, LANE * j:LANE * (j + 1)] = ypair
        for j in range(3):
            state[j] = states[j]

    return pl.pallas_call(
        body, name="ssd_fwd", grid=(nc // cps,),
        in_specs=[pl.BlockSpec((cps * l, N_XBC), lambda c: (c, 0)),
                  pl.BlockSpec((cps * l, LANE), lambda c: (c, O_TAIL // LANE)), _full((8, LANE))],
        out_specs=[pl.BlockSpec((cps * l, D_SSD), lambda c: (c, 0)), pl.BlockSpec((cps, 3, LANE, LANE), lambda c: (c, 0, 0, 0))],
        out_shape=[_sds((s, D_SSD)), _sds((nc, 3, LANE, LANE))],
        scratch_shapes=[pltpu.VMEM((3, LANE, LANE), F32)],
        compiler_params=_params(("arbitrary",)),
    )(xbc, proj, sc)


def _mla_prep_fwd(proj, gq, gkv, wq, wkv, cos, sin):
    s = proj.shape[0]
    ts = _tile(s)
    nh = MLA_HEADS

    def body(cqa_ref, ckv_ref, tail_ref, gq_ref, gkv_ref, wq_ref, wkv_ref, cos_ref, sin_ref,
             q_ref, k_ref, v_ref, qn_ref, kvn_ref, rq_ref, rkv_ref):
        qn, rq = _rms(cqa_ref[...], gq_ref[...])
        kvn, rkv = _rms(ckv_ref[...], gkv_ref[...])
        qn = qn.astype(MXU)
        kvn = kvn.astype(MXU)
        qn_ref[...] = qn
        kvn_ref[...] = kvn
        rq_ref[...] = rq
        rkv_ref[...] = rkv
        q = _dot_nt(qn, wq_ref[...])
        kv = _dot_nt(kvn, wkv_ref[...])
        cosv = cos_ref[...]
        sinv = sin_ref[...]
        lane = _iota((ts, LANE), 1)
        rope_lanes = (lane >= ROPE_LANE) & (lane < ROPE_LANE + QK_ROPE)
        kr = jnp.where(rope_lanes, pltpu.roll(tail_ref[...], ROPE_LANE, 1), 0.0)
        kr = kr * cosv + _rope_swap(kr) * sinv
        for h in range(nh):
            qh = q[:, LANE * h:LANE * (h + 1)]
            q_ref[h] = ((qh * cosv + _rope_swap(qh) * sinv) * ATT_SCALE).astype(MXU)
            k_ref[h] = (kv[:, LANE * h:LANE * (h + 1)] + kr).astype(MXU)
            v_ref[h] = kv[:, LANE * (nh + h):LANE * (nh + h + 1)].astype(MXU)

    head = pl.BlockSpec((nh, ts, LANE), lambda i: (0, i, 0))
    return pl.pallas_call(
        body, name="mla_prep_fwd", grid=(s // ts,),
        in_specs=[pl.BlockSpec((ts, Q_LORA), lambda i: (i, O_CQA // Q_LORA)),
                  pl.BlockSpec((ts, KV_LORA), lambda i: (i, O_CKV // KV_LORA)),
                  pl.BlockSpec((ts, LANE), lambda i: (i, O_TAIL // LANE)),
                  _full((1, Q_LORA)), _full((1, KV_LORA)), _full((nh * LANE, Q_LORA)), _full((2 * nh * LANE, KV_LORA)),
                  _row(ts, LANE), _row(ts, LANE)],
        out_specs=[head, head, head, _row(ts, Q_LORA), _row(ts, KV_LORA), _row(ts, 1), _row(ts, 1)],
        out_shape=[_sds((nh, s, LANE), MXU)] * 3 + [_sds((s, Q_LORA), MXU), _sds((s, KV_LORA), MXU), _sds((s, 1)), _sds((s, 1))],
        compiler_params=_params(("parallel",)),
    )(proj, proj, proj, gq, gkv, wq, wkv, cos, sin)


ATT_SCALE = (QK_NOPE + QK_ROPE) ** -0.5
NEG = -1e30


def _att_tile(s, most):
    return min(most, s // 2)


ATT_FWD_TILE = 1024
ATT_BWD_TILE = 512


def _attn_fwd(q, k, v):
    nh, s, _ = q.shape
    tq = _att_tile(s, ATT_FWD_TILE)
    nq = s // tq

    def body(q_ref, k_ref, v_ref, o_ref, lse_ref):
        i = pl.program_id(1)
        rowi = _iota((tq, tq), 0)
        coli = _iota((tq, tq), 1)
        zero = (jnp.full((tq, 1), NEG, F32), jnp.zeros((tq, 1), F32), jnp.zeros((tq, LANE), F32))
        state = [zero, zero]
        done = [zero, zero]
        for t in range(nq + 1):
            first = t <= i
            qblk = jnp.where(first, i, nq - 1 - i)
            kblk = jnp.where(first, t, t - i - 1)
            qoff = pl.multiple_of(qblk * tq, tq)
            koff = pl.multiple_of(kblk * tq, tq)
            keep = coli <= rowi + jnp.where(kblk == qblk, 0, tq)
            restart = t == i + 1
            for hh in range(2):
                m, lsum, acc = state[hh]
                if t > 0:
                    done[hh] = tuple(jnp.where(restart, a, b) for a, b in zip(state[hh], done[hh]))
                    m = jnp.where(restart, NEG, m)
                    lsum = jnp.where(restart, 0.0, lsum)
                    acc = jnp.where(restart, 0.0, acc)
                sc = _dot_nt(q_ref[hh, pl.ds(qoff, tq), :], k_ref[hh, pl.ds(koff, tq), :])
                sc = jnp.where(keep, sc, NEG)
                m_new = jnp.maximum(m, jnp.max(sc, axis=1, keepdims=True))
                p = jnp.exp(sc - m_new)
                alpha = jnp.exp(m - m_new)
                lsum = alpha * lsum + jnp.sum(p, axis=1, keepdims=True)
                acc = alpha * acc + _dot(p, v_ref[hh, pl.ds(koff, tq), :])
                state[hh] = (m_new, lsum, acc)
        for blk, res in ((i, done), (nq - 1 - i, state)):
            off = pl.multiple_of(blk * tq, tq)
            out = None
            for hh in range(2):
                m, lsum, acc = res[hh]
                o = acc * (1.0 / lsum)
                lse_ref[hh, pl.ds(off, tq), :] = m + jnp.log(lsum)
                out = o if hh == 0 else out + pltpu.roll(o, V_DIM, 1)
            o_ref[pl.ds(off, tq), :] = out

    pair = pl.BlockSpec((2, s, LANE), lambda j, i: (j, 0, 0))
    return pl.pallas_call(
        body, name="attn_fwd", grid=(nh // 2, nq // 2),
        in_specs=[pair, pair, pair],
        out_specs=[pl.BlockSpec((s, LANE), lambda j, i: (0, j)), pl.BlockSpec((2, s, 1), lambda j, i: (j, 0, 0))],
        out_shape=[_sds((s, D_MLA)), _sds((nh, s, 1))],
        compiler_params=_params(("parallel", "arbitrary")),
    )(q, k, v)


def _ssd_gate(y_ssd, s_z, g):
    yz = y_ssd * _silu(s_z)
    g0 = _iota(yz.shape, 1) < D_SSD // 2
    sq = yz * yz
    ms0 = jnp.sum(jnp.where(g0, sq, 0.0), axis=1, keepdims=True) / (D_SSD // 2)
    ms1 = jnp.sum(jnp.where(g0, 0.0, sq), axis=1, keepdims=True) / (D_SSD // 2)
    r = jnp.where(g0, lax.rsqrt(ms0 + SSD_NORM_EPS), lax.rsqrt(ms1 + SSD_NORM_EPS))
    nrm = yz * r
    return nrm * g, nrm, r, g0


def _outproj_fwd(x, proj, ya, y_ssd, o, g_ssd, w):
    s = x.shape[0]
    ts = _tile(s)

    def body(x_ref, sz_ref, cz_ref, ya_ref, ys_ref, o_ref, g_ref, w_ref, xo_ref, y_ref):
        yb = _ssd_gate(ys_ref[...], sz_ref[...], g_ref[...])[0]
        yc = o_ref[...] * _silu(cz_ref[...])
        y = jnp.concatenate([ya_ref[...], yb, yc], axis=1).astype(MXU)
        y_ref[...] = y
        xo_ref[...] = x_ref[...] + jnp.dot(y, w_ref[...], preferred_element_type=F32)

    return pl.pallas_call(
        body, name="outproj_fwd", grid=(s // ts,),
        in_specs=[_row(ts, D_MODEL), _gate_cols(ts, O_SZ), _gate_cols(ts, O_CZ), _row(ts, D_CONV_A), _row(ts, D_SSD),
                  _row(ts, D_MLA), _full((1, D_SSD)), _full((D_MODEL, D_MODEL))],
        out_specs=[_row(ts, D_MODEL), _row(ts, D_MODEL)],
        out_shape=[_sds((s, D_MODEL)), _sds((s, D_MODEL), MXU)],
        compiler_params=_params(("parallel",)),
    )(x, proj, proj, ya, y_ssd, o, g_ssd, w)


def _outproj_loss(x, proj, ya, y_ssd, o, g_ssd, w, final_g, tgt):
    s = x.shape[0]
    ts = _tile(s)

    def body(x_ref, sz_ref, cz_ref, ya_ref, ys_ref, o_ref, g_ref, w_ref, fg_ref, t_ref, dx_ref, dg_ref, loss_ref, y_ref):
        @pl.when(pl.program_id(0) == 0)
        def _():
            dg_ref[...] = jnp.zeros_like(dg_ref)
            loss_ref[...] = jnp.zeros_like(loss_ref)

        yb = _ssd_gate(ys_ref[...], sz_ref[...], g_ref[...])[0]
        yc = o_ref[...] * _silu(cz_ref[...])
        y = jnp.concatenate([ya_ref[...], yb, yc], axis=1).astype(MXU)
        y_ref[...] = y
        xv = x_ref[...] + jnp.dot(y, w_ref[...], preferred_element_type=F32)
        gv = fg_ref[...]
        yn, r = _rms(xv, gv)
        e = yn - t_ref[...]
        loss_ref[...] += jnp.sum(jnp.sum(e * e, axis=1, keepdims=True), axis=0, keepdims=True) * (0.5 / D_MODEL)
        dx, dg = _rms_bwd(e * (1.0 / D_MODEL), xv, r, gv)
        dx_ref[...] = dx
        dg_ref[...] += dg

    return pl.pallas_call(
        body, name="outproj_loss", grid=(s // ts,),
        in_specs=[_row(ts, D_MODEL), _gate_cols(ts, O_SZ), _gate_cols(ts, O_CZ), _row(ts, D_CONV_A), _row(ts, D_SSD),
                  _row(ts, D_MLA), _full((1, D_SSD)), _full((D_MODEL, D_MODEL)), _full((1, D_MODEL)), _row(ts, D_MODEL)],
        out_specs=[_row(ts, D_MODEL), _full((1, D_MODEL)), _full((1, LANE)), _row(ts, D_MODEL)],
        out_shape=[_sds((s, D_MODEL)), _sds((1, D_MODEL)), _sds((1, LANE)), _sds((s, D_MODEL), MXU)],
        compiler_params=_params(("arbitrary",)),
    )(x, proj, proj, ya, y_ssd, o, g_ssd, w, final_g, tgt)


def _outproj_bwd(dout, y, w, proj, y_ssd, o, g_ssd, dep=None):
    s = dout.shape[0]
    ts = _tile(s)

    def body(dout_ref, y_ref, w_ref, sz_ref, cz_ref, ys_ref, o_ref, g_ref,
             dya_ref, dys_ref, dsz_ref, dattn_ref, dcz_ref, dg_ref, dw_ref):
        @pl.when(pl.program_id(0) == 0)
        def _():
            dw_ref[...] = jnp.zeros_like(dw_ref)
            dg_ref[...] = jnp.zeros_like(dg_ref)

        dout_b = dout_ref[...].astype(MXU)
        dw_ref[...] += _dot_tn(y_ref[...], dout_b)
        dy = _dot_nt(dout_b, w_ref[...])
        dya_ref[...] = dy[:, :D_CONV_A]
        dyb = dy[:, D_CONV_A:D_CONV_A + D_SSD]
        sz = sz_ref[...]
        ys = ys_ref[...]
        gv = g_ref[...]
        _, nrm, r, g0 = _ssd_gate(ys, sz, gv)
        dg_ref[...] += jnp.sum(dyb * nrm, axis=0, keepdims=True)
        dn = dyb * gv
        t = dn * nrm
        mean = jnp.where(g0, jnp.sum(jnp.where(g0, t, 0.0), axis=1, keepdims=True),
                         jnp.sum(jnp.where(g0, 0.0, t), axis=1, keepdims=True)) / (D_SSD // 2)
        dyz = r * (dn - nrm * mean)
        dys_ref[...] = dyz * _silu(sz)
        dsz_ref[...] = (dyz * ys * _dsilu(sz)).astype(MXU)
        dyc = dy[:, D_CONV_A + D_SSD:]
        cz = cz_ref[...]
        dattn_ref[...] = dyc * _silu(cz)
        dcz_ref[...] = (dyc * o_ref[...] * _dsilu(cz)).astype(MXU)

    return _call_after(
        dep, body, (dout, y, w, proj, proj, y_ssd, o, g_ssd), name="outproj_bwd", grid=(s // ts,),
        in_specs=[_row(ts, D_MODEL), _row(ts, D_MODEL), _full((D_MODEL, D_MODEL)), _gate_cols(ts, O_SZ), _gate_cols(ts, O_CZ),
                  _row(ts, D_SSD), _row(ts, D_MLA), _full((1, D_SSD))],
        out_specs=[_row(ts, D_CONV_A), _row(ts, D_SSD), _row(ts, D_SSD), _row(ts, D_MLA), _row(ts, D_MLA),
                   _full((1, D_SSD)), _full((D_MODEL, D_MODEL))],
        out_shape=[_sds((s, D_CONV_A)), _sds((s, D_SSD)), _sds((s, D_SSD), MXU), _sds((s, D_MLA)), _sds((s, D_MLA), MXU),
                   _sds((1, D_SSD)), _sds((D_MODEL, D_MODEL))],
        compiler_params=_params(("arbitrary",)),
    )


def _attn_bwd(q, k, v, o, d_o, lse, dep=None):
    nh, s, _ = q.shape
    tq = _att_tile(s, ATT_BWD_TILE)
    nq = s // tq

    def body(q_ref, k_ref, v_ref, o_ref, do_ref, lse_ref, dq_ref, dk_ref, dv_ref, dop, delta):
        i = pl.program_id(1)

        @pl.when(i == 0)
        def _():
            lane = _iota((s, LANE), 1)
            for hh in range(2):
                dov = do_ref[...]
                ov = o_ref[...]
                if hh == 1:
                    dov = pltpu.roll(dov, V_DIM, 1)
                    ov = pltpu.roll(ov, V_DIM, 1)
                dov = jnp.where(lane < V_DIM, dov, 0.0)
                dop[hh] = dov.astype(MXU)
                delta[hh] = jnp.sum(dov * ov, axis=1, keepdims=True)
                dq_ref[hh] = jnp.zeros((s, LANE), F32)

        rowi = _iota((tq, tq), 0)
        coli = _iota((tq, tq), 1)
        z = jnp.zeros((tq, LANE), F32)
        state = [(z, z), (z, z)]
        done = [(z, z), (z, z)]
        for t in range(nq + 1):
            first = t <= nq - 1 - i
            kblk = jnp.where(first, i, nq - 1 - i)
            qblk = jnp.where(first, i + t, t - 1)
            qoff = pl.multiple_of(qblk * tq, tq)
            koff = pl.multiple_of(kblk * tq, tq)
            keep = coli <= rowi + jnp.where(kblk == qblk, 0, tq)
            restart = t == nq - i
            for hh in range(2):
                dk, dv = state[hh]
                if t > 0:
                    done[hh] = tuple(jnp.where(restart, a, b) for a, b in zip(state[hh], done[hh]))
                    dk = jnp.where(restart, 0.0, dk)
                    dv = jnp.where(restart, 0.0, dv)
                kb = k_ref[hh, pl.ds(koff, tq), :]
                qb = q_ref[hh, pl.ds(qoff, tq), :]
                dob = dop[hh, pl.ds(qoff, tq), :]
                sc = jnp.where(keep, _dot_nt(qb, kb), NEG)
                p = jnp.exp(sc - lse_ref[hh, pl.ds(qoff, tq), :])
                dp = _dot_nt(dob, v_ref[hh, pl.ds(koff, tq), :])
                ds = p * (dp - delta[hh, pl.ds(qoff, tq), :])
                dq_ref[hh, pl.ds(qoff, tq), :] += _dot(ds, kb)
                state[hh] = (dk + _dot_tn(ds, qb), dv + _dot_tn(p, dob))
        for blk, res in ((i, done), (nq - 1 - i, state)):
            off = pl.multiple_of(blk * tq, tq)
            for hh in range(2):
                dk_ref[hh, pl.ds(off, tq), :] = res[hh][0]
                dv_ref[hh, pl.ds(off, tq), :] = res[hh][1]

    pair = pl.BlockSpec((2, s, LANE), lambda j, i: (j, 0, 0))
    return _call_after(
        dep, body, (q, k, v, o, d_o, lse), name="attn_bwd", grid=(nh // 2, nq // 2),
        in_specs=[pair, pair, pair, pl.BlockSpec((s, LANE), lambda j, i: (0, j)), pl.BlockSpec((s, LANE), lambda j, i: (0, j)),
                  pl.BlockSpec((2, s, 1), lambda j, i: (j, 0, 0))],
        out_specs=[pair, pair, pair],
        out_shape=[_sds((nh, s, LANE))] * 3,
        scratch_shapes=[pltpu.VMEM((2, s, LANE), MXU), pltpu.VMEM((2, s, 1), F32)],
        compiler_params=_params(("parallel", "arbitrary")),
    )


def _ssd_bwd(xbc, proj, sc, states, dy, dep=None):
    s = xbc.shape[0]
    nc = s // SSD_CHUNK
    l = SSD_CHUNK
    cps = SSD_CHUNKS_PER_STEP

    def body(xbc_ref, tail_ref, sc_ref, st_ref, dy_ref, dxbc_ref, dtail_ref, dsc_ref, dstate):
        @pl.when(pl.program_id(0) == 0)
        def _():
            dstate[...] = jnp.zeros_like(dstate)
            dsc_ref[...] = jnp.zeros_like(dsc_ref)

        sc_v = sc_ref[...]
        lane1 = _iota((1, LANE), 1)
        rowp = _iota((LANE, 1), 0)
        rowl = _iota((l, 1), 0)
        d_row = sc_v[2:3, :]
        dstates = [dstate[j] for j in range(3)]
        for u in reversed(range(cps)):
            dstates = chunk(u, xbc_ref, tail_ref, sc_v, st_ref, dy_ref, dxbc_ref, dtail_ref, dsc_ref, dstates,
                            lane1, rowp, rowl, d_row)
        for j in range(3):
            dstate[j] = dstates[j]

    def chunk(u, xbc_ref, tail_ref, sc_v, st_ref, dy_ref, dxbc_ref, dtail_ref, dsc_ref, dstates, lane1, rowp, rowl, d_row):
        r = slice(u * l, (u + 1) * l)
        dstates = list(dstates)
        lane, row, tri, a_row, pre, dt, a_cs, a_t = _ssd_chunk_common(tail_ref[r, :], sc_v)
        da_col = jnp.zeros((l, LANE), F32)
        da_row = jnp.zeros((LANE, l), F32)
        dt_x = jnp.zeros((l, LANE), F32)
        dd_row = jnp.zeros((1, LANE), F32)
        db = [jnp.zeros((l, LANE), F32), jnp.zeros((l, LANE), F32)]
        dc = [jnp.zeros((l, LANE), F32), jnp.zeros((l, LANE), F32)]
        for j in range(3):
            xpair = xbc_ref[r, LANE * j:LANE * (j + 1)]
            dypair = dy_ref[r, LANE * j:LANE * (j + 1)]
            sp = st_ref[u, j]
            dsp = dstates[j]
            dxpair = jnp.zeros((l, LANE), F32)
            ds_new = jnp.zeros((LANE, LANE), F32)
            decay = jnp.zeros((LANE, 1), F32)
            for half in range(2):
                h = 2 * j + half
                g = h // 3
                hm = (lane < 64) if half == 0 else (lane >= 64)
                hrow = (rowp < 64) if half == 0 else (rowp >= 64)
                ac = _pick_col(a_cs, lane, DT_LANE + h)
                ar = _pick_row(a_t, row, DT_LANE + h)
                dtc = _pick_col(dt, lane, DT_LANE + h)
                alast = jnp.sum(jnp.where(lane1 == l - 1, ar, 0.0), axis=1, keepdims=True)
                dh = jnp.sum(jnp.where(lane1 == DT_LANE + h, d_row, 0.0), axis=1, keepdims=True)
                xm = jnp.where(hm, xpair, 0.0)
                xd = xm * dtc
                dym = jnp.where(hm, dypair, 0.0)
                bm = xbc_ref[r, D_SSD + LANE * g:D_SSD + LANE * (g + 1)]
                cm = xbc_ref[r, D_SSD + SSD_BC + LANE * g:D_SSD + SSD_BC + LANE * (g + 1)]
                lm = jnp.where(row >= lane, jnp.exp(jnp.minimum(ac - ar, 0.0)), 0.0)
                e_in = jnp.exp(ac)
                f_out = jnp.exp(alast - ac)
                e_last = jnp.exp(alast)
                m = _dot_nt(cm, bm) * lm
                y_off = jnp.where(hm, _dot_nt(cm, sp), 0.0) * e_in
                dm = _dot_nt(dym, xd)
                dxd = _dot_tn(m, dym)
                dg = dm * lm
                dye = dym * e_in
                dc[g] = dc[g] + _dot(dg, bm) + _dot(dye, sp)
                db[g] = db[g] + _dot_tn(dg, cm)
                qm = dm * m
                dac = jnp.sum(qm, axis=1, keepdims=True) + jnp.sum(dym * y_off, axis=1, keepdims=True)
                dar = -jnp.sum(qm, axis=0, keepdims=True)
                dxf = jnp.where(hm, _dot_nt(bm, dsp), 0.0)
                db[g] = db[g] + _dot(xd * f_out, dsp)
                dxd = dxd + dxf * f_out
                df = jnp.sum(dxf * xd, axis=1, keepdims=True) * f_out
                dac = dac - df
                s_last = jnp.sum(df, axis=0, keepdims=True)
                ss = jnp.sum(jnp.where(hrow, dsp * sp, 0.0), axis=1, keepdims=True)
                s_last = s_last + e_last * jnp.sum(ss, axis=0, keepdims=True)
                dac = dac + jnp.where(rowl == l - 1, s_last, 0.0)
                ds_new = ds_new + _dot_tn(dye, cm)
                decay = jnp.where(hrow, e_last, decay)
                dxpair = dxpair + dxd * dtc + dym * dh
                dt_x = dt_x + jnp.where(lane == DT_LANE + h, jnp.sum(dxd * xm, axis=1, keepdims=True), 0.0)
                dsum = jnp.sum(jnp.sum(dym * xm, axis=1, keepdims=True), axis=0, keepdims=True)
                dd_row = dd_row + jnp.where(lane1 == DT_LANE + h, dsum, 0.0)
                da_col = da_col + jnp.where(lane == DT_LANE + h, dac, 0.0)
                da_row = da_row + jnp.where(row == DT_LANE + h, dar, 0.0)
            dstates[j] = dsp * decay + ds_new
            dxbc_ref[r, LANE * j:LANE * (j + 1)] = dxpair
        for g in range(2):
            dxbc_ref[r, D_SSD + LANE * g:D_SSD + LANE * (g + 1)] = db[g]
            dxbc_ref[r, D_SSD + SSD_BC + LANE * g:D_SSD + SSD_BC + LANE * (g + 1)] = dc[g]
        dla = _dot_hi_tn(tri, da_col + da_row.T)
        ddt = dt_x + dla * a_row
        dpre = ddt * _sigmoid(pre)
        dtm = (lane >= DT_LANE) & (lane < DT_LANE + SSD_HEADS)
        dtail_ref[r, :] = jnp.where(dtm, dpre, 0.0).astype(MXU)
        dtm1 = (lane1 >= DT_LANE) & (lane1 < DT_LANE + SSD_HEADS)
        dsc_ref[0:1, :] += jnp.where(dtm1, jnp.sum(dpre, axis=0, keepdims=True), 0.0)
        dsc_ref[1:2, :] += jnp.where(dtm1, jnp.sum(dla * dt, axis=0, keepdims=True) * a_row, 0.0)
        dsc_ref[2:3, :] += dd_row
        return dstates

    rev = lambda c: nc // cps - 1 - c
    return _call_after(
        dep, body, (xbc, proj, sc, states, dy), name="ssd_bwd", grid=(nc // cps,),
        in_specs=[pl.BlockSpec((cps * l, N_XBC), lambda c: (rev(c), 0)),
                  pl.BlockSpec((cps * l, LANE), lambda c: (rev(c), O_TAIL // LANE)), _full((8, LANE)),
                  pl.BlockSpec((cps, 3, LANE, LANE), lambda c: (rev(c), 0, 0, 0)),
                  pl.BlockSpec((cps * l, D_SSD), lambda c: (rev(c), 0))],
        out_specs=[pl.BlockSpec((cps * l, N_XBC), lambda c: (rev(c), 0)), pl.BlockSpec((cps * l, LANE), lambda c: (rev(c), 0)),
                   _full((8, LANE))],
        out_shape=[_sds((s, N_XBC)), _sds((s, LANE), MXU), _sds((8, LANE))],
        scratch_shapes=[pltpu.VMEM((3, LANE, LANE), F32)],
        compiler_params=_params(("arbitrary",)),
    )


def _sconv_bwd(proj, w, b, dxbc, dep=None):
    s = proj.shape[0]

    def body(u_ref, w_ref, b_ref, d_ref, du_ref, dw_ref, db_ref):
        u = u_ref[...]
        wv = w_ref[...]
        dpre = d_ref[...] * _dsilu(_sconv_pre(u, wv, b_ref[...]))
        ahead = [_shift_up(dpre, j) for j in range(4)]
        du_ref[...] = (wv[3:4, :] * ahead[0] + wv[2:3, :] * ahead[1] + wv[1:2, :] * ahead[2]
                       + wv[0:1, :] * ahead[3]).astype(MXU)
        for k in range(4):
            dw_ref[k:k + 1, :] = jnp.sum(ahead[3 - k] * u, axis=0, keepdims=True)
        db_ref[...] = jnp.sum(dpre, axis=0, keepdims=True)

    blk = pl.BlockSpec((s, LANE), lambda j: (0, j))
    return _call_after(
        dep, body, (proj, w, b, dxbc), name="sconv_bwd", grid=(N_XBC // LANE,),
        in_specs=[_col(s, O_XBC), pl.BlockSpec((4, LANE), lambda j: (0, j)), pl.BlockSpec((1, LANE), lambda j: (0, j)), blk],
        out_specs=[blk, pl.BlockSpec((4, LANE), lambda j: (0, j)), pl.BlockSpec((1, LANE), lambda j: (0, j))],
        out_shape=[_sds((s, N_XBC), MXU), _sds((4, N_XBC)), _sds((1, N_XBC))],
        compiler_params=_params(("parallel",)),
    )


def _conva_bwd(proj, w, dya, dep=None):
    s = proj.shape[0]

    def body(h_ref, b_ref, c_ref, z_ref, w_ref, d_ref, da_ref, dw_ref):
        ah, ab, acv, az = h_ref[...], b_ref[...], c_ref[...], z_ref[...]
        wv = w_ref[...]
        u = acv * ah
        cv = wv[2:3, :] * u + wv[1:2, :] * _shift_down(u, 1) + wv[0:1, :] * _shift_down(u, 2)
        dy = d_ref[...]
        sz = _silu(az)
        da_ref[1] = (dy * cv * sz).astype(MXU)
        da_ref[3] = (dy * ab * cv * _dsilu(az)).astype(MXU)
        dcv = dy * ab * sz
        ahead = [_shift_up(dcv, j) for j in range(3)]
        du = wv[2:3, :] * ahead[0] + wv[1:2, :] * ahead[1] + wv[0:1, :] * ahead[2]
        da_ref[0] = (du * acv).astype(MXU)
        da_ref[2] = (du * ah).astype(MXU)
        for k in range(3):
            dw_ref[k:k + 1, :] = jnp.sum(ahead[2 - k] * u, axis=0, keepdims=True)

    return _call_after(
        dep, body, (proj, proj, proj, proj, w, dya), name="conva_bwd", grid=(D_CONV_A // LANE,),
        in_specs=[_col(s, O_AH), _col(s, O_AB), _col(s, O_AC), _col(s, O_AZ), pl.BlockSpec((3, LANE), lambda j: (0, j)),
                  pl.BlockSpec((s, LANE), lambda j: (0, j))],
        out_specs=[pl.BlockSpec((4, s, LANE), lambda j: (0, 0, j)), pl.BlockSpec((3, LANE), lambda j: (0, j))],
        out_shape=[_sds((4, s, D_CONV_A), MXU), _sds((3, D_CONV_A))],
        compiler_params=_params(("parallel",)),
    )


def _mla_prep_bwd(dq, dk, dv, proj, qn, kvn, rq, rkv, gq, gkv, wq, wkv, cos, sin):
    s = proj.shape[0]
    ts = _tile(s)
    nh = MLA_HEADS

    def body(dq_ref, dk_ref, dv_ref, cqa_ref, ckv_ref, qn_ref, kvn_ref, rq_ref, rkv_ref, gq_ref, gkv_ref,
             wq_ref, wkv_ref, cos_ref, sin_ref, dcqa_ref, dckv_ref, dtail_ref, dwq_ref, dwkv_ref, dgq_ref, dgkv_ref):
        @pl.when(pl.program_id(0) == 0)
        def _():
            dwq_ref[...] = jnp.zeros_like(dwq_ref)
            dwkv_ref[...] = jnp.zeros_like(dwkv_ref)
            dgq_ref[...] = jnp.zeros_like(dgq_ref)
            dgkv_ref[...] = jnp.zeros_like(dgkv_ref)

        cosv = cos_ref[...]
        sinv = sin_ref[...]
        lane = _iota((ts, LANE), 1)
        rope_lanes = (lane >= ROPE_LANE) & (lane < ROPE_LANE + QK_ROPE)

        def unrope(gr):
            return gr * cosv + _rope_swap(gr * sinv)

        dqs, dks, dvs = [], [], []
        dkr = jnp.zeros((ts, LANE), F32)
        for h in range(nh):
            dqs.append(unrope(dq_ref[h] * ATT_SCALE).astype(MXU))
            dkh = dk_ref[h]
            dks.append(jnp.where(lane < QK_NOPE, dkh, 0.0).astype(MXU))
            dkr = dkr + jnp.where(rope_lanes, dkh, 0.0)
            dvs.append(dv_ref[h].astype(MXU))
        dtail_ref[...] = pltpu.roll(jnp.where(rope_lanes, unrope(dkr), 0.0), ROPE_LANE, 1).astype(MXU)
        dq_all = jnp.concatenate(dqs, axis=1)
        dkv_all = jnp.concatenate(dks + dvs, axis=1)
        dwq_ref[...] += _dot_tn(dq_all, qn_ref[...])
        dwkv_ref[...] += _dot_tn(dkv_all, kvn_ref[...])
        dcqa, dgq = _rms_bwd(_dot(dq_all, wq_ref[...]), cqa_ref[...], rq_ref[...], gq_ref[...])
        dckv, dgkv = _rms_bwd(_dot(dkv_all, wkv_ref[...]), ckv_ref[...], rkv_ref[...], gkv_ref[...])
        dcqa_ref[...] = dcqa.astype(MXU)
        dckv_ref[...] = dckv.astype(MXU)
        dgq_ref[...] += dgq
        dgkv_ref[...] += dgkv

    head = pl.BlockSpec((nh, ts, LANE), lambda i: (0, i, 0))
    return pl.pallas_call(
        body, name="mla_prep_bwd", grid=(s // ts,),
        in_specs=[head, head, head,
                  pl.BlockSpec((ts, Q_LORA), lambda i: (i, O_CQA // Q_LORA)),
                  pl.BlockSpec((ts, KV_LORA), lambda i: (i, O_CKV // KV_LORA)),
                  _row(ts, Q_LORA), _row(ts, KV_LORA), _row(ts, 1), _row(ts, 1),
                  _full((1, Q_LORA)), _full((1, KV_LORA)), _full((nh * LANE, Q_LORA)), _full((2 * nh * LANE, KV_LORA)),
                  _row(ts, LANE), _row(ts, LANE)],
        out_specs=[_row(ts, Q_LORA), _row(ts, KV_LORA), _row(ts, LANE), _full((nh * LANE, Q_LORA)),
                   _full((2 * nh * LANE, KV_LORA)), _full((1, Q_LORA)), _full((1, KV_LORA))],
        out_shape=[_sds((s, Q_LORA), MXU), _sds((s, KV_LORA), MXU), _sds((s, LANE), MXU), _sds((nh * LANE, Q_LORA)),
                   _sds((2 * nh * LANE, KV_LORA)), _sds((1, Q_LORA)), _sds((1, KV_LORA))],
        compiler_params=_params(("arbitrary",)),
    )(dq, dk, dv, proj, proj, qn, kvn, rq, rkv, gq, gkv, wq, wkv, cos, sin)


def _inproj_bwd(da4, dsz, dxbc_in, dcqa, dckv, dcz, dtail_a, dtail_b, w, x, rstd, g, dout, dep=None):
    s = x.shape[0]
    ts = _tile(s)

    def body(da_ref, dsz_ref, dxbc_ref, dcqa_ref, dckv_ref, dcz_ref, dta_ref, dtb_ref, w_ref, x_ref, r_ref, g_ref, dout_ref,
             dproj_ref, dx_ref, dg_ref):
        @pl.when(pl.program_id(0) == 0)
        def _():
            dg_ref[...] = jnp.zeros_like(dg_ref)

        dproj = jnp.concatenate(
            [da_ref[0], da_ref[1], da_ref[2], da_ref[3], dxbc_ref[...], dsz_ref[...], dcqa_ref[...], dckv_ref[...],
             dcz_ref[...], dta_ref[...] + dtb_ref[...]], axis=1)
        dproj_ref[...] = dproj
        dh = _dot_nt(dproj, w_ref[...])
        dx, dg = _rms_bwd(dh, x_ref[...], r_ref[...], g_ref[...])
        dx_ref[...] = dout_ref[...] + dx
        dg_ref[...] += dg

    return _call_after(
        dep, body, (da4, dsz, dxbc_in, dcqa, dckv, dcz, dtail_a, dtail_b, w, x, rstd, g, dout), name="inproj_bwd", grid=(s // ts,),
        in_specs=[pl.BlockSpec((4, ts, D_CONV_A), lambda i: (0, i, 0)), _row(ts, D_SSD), _row(ts, N_XBC), _row(ts, Q_LORA),
                  _row(ts, KV_LORA), _row(ts, D_MLA), _row(ts, LANE), _row(ts, LANE), _full((D_MODEL, NCOL)),
                  _row(ts, D_MODEL), _row(ts, 1), _full((1, D_MODEL)), _row(ts, D_MODEL)],
        out_specs=[_row(ts, NCOL), _row(ts, D_MODEL), _full((1, D_MODEL))],
        out_shape=[_sds((s, NCOL), MXU), _sds((s, D_MODEL)), _sds((1, D_MODEL))],
        compiler_params=_params(("arbitrary",)),
    )


DWIN_BLOCK = 640


def _dwin(h, dproj, dep=None):
    s = h.shape[0]

    def body(h_ref, d_ref, o_ref):
        o_ref[...] = _dot_tn(h_ref[...], d_ref[...])

    return _call_after(
        dep, body, (h, dproj), name="dwin", grid=(NCOL // DWIN_BLOCK,),
        in_specs=[_full((s, D_MODEL)), pl.BlockSpec((s, DWIN_BLOCK), lambda j: (0, j))],
        out_specs=pl.BlockSpec((D_MODEL, DWIN_BLOCK), lambda j: (0, j)),
        out_shape=_sds((D_MODEL, NCOL)),
        compiler_params=_params(("parallel",)),
    )


def _adamw(ws, gs, ms, vs, whole):
    n = len(ws)
    bc1 = 1.0 - ADAM_B1 ** ADAM_STEP
    bc2 = 1.0 - ADAM_B2 ** ADAM_STEP

    def body(*refs):
        ins, outs = refs[:4 * n], refs[4 * n:]
        for a in range(n):
            w_ref, g_ref, m_ref, v_ref = ins[a], ins[n + a], ins[2 * n + a], ins[3 * n + a]
            gv = g_ref[...]
            mn = ADAM_B1 * m_ref[...] + (1.0 - ADAM_B1) * gv
            vn = ADAM_B2 * v_ref[...] + (1.0 - ADAM_B2) * (gv * gv)
            outs[n + a][...] = mn
            outs[2 * n + a][...] = vn
            outs[a][...] = -ADAM_LR * ((mn / bc1) / (jnp.sqrt(vn / bc2) + ADAM_EPS) + ADAM_WD * w_ref[...])

    if whole:
        grid, blks = (1,), [pl.BlockSpec(w.shape, lambda i, _n=w.ndim: (0,) * _n) for w in ws]
    else:
        grid = (ws[0].shape[0], 2)
        blks = [pl.BlockSpec((1, w.shape[1] // 2, w.shape[2]), lambda i, k: (i, k, 0)) for w in ws]
    out = pl.pallas_call(
        body, name="adamw", grid=grid,
        in_specs=blks * 4, out_specs=blks * 3, out_shape=[_sds(w.shape) for w in ws] * 3,
        compiler_params=_params(("parallel",) * len(grid)),
    )(*ws, *gs, *ms, *vs)
    return [(out[a], out[n + a], out[2 * n + a]) for a in range(n)]


ADAMW_COLS_BLOCK = 512


def _adamw_cols(w_t, gs, m_t, v_t):
    cols, nl, rows = w_t.shape
    bc1 = 1.0 - ADAM_B1 ** ADAM_STEP
    bc2 = 1.0 - ADAM_B2 ** ADAM_STEP

    def body(w_ref, m_ref, v_ref, *rest):
        g_refs, (go_ref, d_ref, mo_ref, vo_ref), g_blk = rest[:nl], rest[nl:nl + 4], rest[-1]
        for l in range(nl):
            g_blk[:, l, :] = g_refs[l][...].T
        gv = g_blk[...]
        mn = ADAM_B1 * m_ref[...] + (1.0 - ADAM_B1) * gv
        vn = ADAM_B2 * v_ref[...] + (1.0 - ADAM_B2) * (gv * gv)
        go_ref[...] = gv
        mo_ref[...] = mn
        vo_ref[...] = vn
        d_ref[...] = -ADAM_LR * ((mn / bc1) / (jnp.sqrt(vn / bc2) + ADAM_EPS) + ADAM_WD * w_ref[...])

    tc = ADAMW_COLS_BLOCK
    blk = pl.BlockSpec((tc, nl, rows), lambda j: (j, 0, 0))
    gblk = pl.BlockSpec((rows, tc), lambda j: (0, j))
    return pl.pallas_call(
        body, name="adamw_cols", grid=(pl.cdiv(cols, tc),),
        in_specs=[blk] * 3 + [gblk] * nl, out_specs=[blk] * 4, out_shape=[_sds(w_t.shape)] * 4,
        scratch_shapes=[pltpu.VMEM((tc, nl, rows), F32)],
        compiler_params=_params(("parallel",)),
    )(w_t, m_t, v_t, *gs)


COL_MOVES = ((0, 0, 1024), (1024, O_SZ, 384), (1408, O_XBC, 896), (2304, O_TAIL + DT_LANE, 6), (2310, O_CQA, 256),
             (2566, O_CKV, 128), (2694, O_TAIL, 32), (2726, O_CZ, 384))


def _move_cols(w, moves, width):
    out = None
    for src, dst, n in moves:
        piece = jnp.pad(w[..., src:src + n], [(0, 0)] * (w.ndim - 1) + [(dst, width - dst - n)])
        out = piece if out is None else out + piece
    return out


def _perm_cols(w):
    return _move_cols(w, COL_MOVES, NCOL)


def _unperm_cols(g):
    return _move_cols(g, [(dst, src, n) for src, dst, n in COL_MOVES], IN_COLS)


def _wq_layout(wt):
    return jnp.pad(wt.reshape(MLA_HEADS, QK_NOPE + QK_ROPE, Q_LORA), ((0, 0), (0, 32), (0, 0))).reshape(MLA_HEADS * LANE, Q_LORA)


def _wq_unlayout(g):
    return g.reshape(MLA_HEADS, LANE, Q_LORA)[:, :QK_NOPE + QK_ROPE].reshape(MLA_HEADS * (QK_NOPE + QK_ROPE), Q_LORA)


def _wkv_layout(wt):
    t = wt.reshape(MLA_HEADS, 2, 64, KV_LORA).transpose(1, 0, 2, 3)
    return jnp.pad(t, ((0, 0), (0, 0), (0, 64), (0, 0))).reshape(2 * MLA_HEADS * LANE, KV_LORA)


def _wkv_unlayout(g):
    t = g.reshape(2, MLA_HEADS, LANE, KV_LORA)[:, :, :64]
    return t.transpose(1, 0, 2, 3).reshape(MLA_HEADS * LANE, KV_LORA)


def _rope_tables(positions):
    inv_freq = ROPE_BASE ** (-jnp.arange(0, QK_ROPE, 2, dtype=F32) / QK_ROPE)
    ang = positions.astype(F32)[:, None] * inv_freq
    cos, sin = jnp.cos(ang), jnp.sin(ang)
    s = positions.shape[0]
    one, zero = jnp.ones((s, ROPE_LANE), F32), jnp.zeros((s, ROPE_LANE), F32)
    cos_t = jnp.concatenate([one, cos, cos, one[:, :32]], axis=1)
    sin_t = jnp.concatenate([zero, -sin, sin, zero[:, :32]], axis=1)
    return cos_t, sin_t


def _ssd_scalars(dt_bias, a_log, d_skip):
    return jnp.pad(jnp.stack([dt_bias, a_log, d_skip]), ((0, 5), (DT_LANE, LANE - DT_LANE - SSD_HEADS)))


def _layer_fwd(x, lw, cos, sin, dep=None, late=None, head=None):
    proj, h, rstd = _inproj_fwd(x, lw["norm_g"], lw["w_in"], dep)
    ya = _conva_fwd(proj, lw["conv_a_w"])
    xbc = _sconv_fwd(proj, lw["ssd_conv_w"], lw["ssd_conv_b"])
    y_ssd, states = _ssd_fwd(xbc, proj, lw["sc"])
    if late is not None:
        lw = {**lw, **late(ya, y_ssd)}
    q, k, v, qn, kvn, rq, rkv = _mla_prep_fwd(proj, lw["gq"], lw["gkv"], lw["wq"], lw["wkv"], cos, sin)
    o, lse = _attn_fwd(q, k, v)
    if head is None:
        x_out, y = _outproj_fwd(x, proj, ya, y_ssd, o, lw["g_ssd"], lw["w_out"])
    else:
        *x_out, y = _outproj_loss(x, proj, ya, y_ssd, o, lw["g_ssd"], lw["w_out"], *head)
    saved = dict(x=x, proj=proj, h=h, rstd=rstd, xbc=xbc, y_ssd=y_ssd, states=states, q=q, k=k, v=v, qn=qn, kvn=kvn,
                 rq=rq, rkv=rkv, o=o, lse=lse, y=y)
    return x_out, saved, lw


def _layer_bwd(dout, lw, sv, cos, sin, rs=None, early_grads=None):
    tok = lambda: None if rs is None else rs["h"]["token"]
    dya, dys, dsz, d_o, dcz, dg_ssd, dw_out = _outproj_bwd(dout, sv["y"], lw["w_out"], sv["proj"], sv["y_ssd"], sv["o"],
                                                            lw["g_ssd"], tok())
    if rs is not None:
        rs = _rs_add_mine(rs, [dya])
    dq, dk, dv = _attn_bwd(sv["q"], sv["k"], sv["v"], sv["o"], d_o, sv["lse"], tok())
    dxbc, dtail_s, dsc = _ssd_bwd(sv["xbc"], sv["proj"], lw["sc"], sv["states"], dys, tok())
    da4, dw_conva = _conva_bwd(sv["proj"], lw["conv_a_w"], dya, tok())
    dcqa, dckv, dtail_m, dwq, dwkv, dgq, dgkv = _mla_prep_bwd(
        dq, dk, dv, sv["proj"], sv["qn"], sv["kvn"], sv["rq"], sv["rkv"], lw["gq"], lw["gkv"], lw["wq"], lw["wkv"], cos, sin)
    early = None
    if rs is not None and early_grads is not None:
        rs, early = _rs_add_chips(rs, [dq, dxbc, da4, dcqa], also=(early_grads(dw_out, dwq, dwkv), "0l"))
    elif rs is not None:
        rs = _rs_add_chips(rs, [dq, dxbc, da4, dcqa])
    du, dw_sconv, db_sconv = _sconv_bwd(sv["proj"], lw["ssd_conv_w"], lw["ssd_conv_b"], dxbc, tok())
    etok = lambda: None if early is None else early["h"]["token"]
    dproj, dx, dg = _inproj_bwd(da4, dsz, du, dcqa, dckv, dcz, dtail_s, dtail_m, lw["w_in"], sv["x"], sv["rstd"],
                                lw["norm_g"], dout, etok())
    reduced = None if rs is None else _rs_end(rs, [du, dcqa, dx])
    if early is not None:
        early = _rs_add_mine(early, [dx])
    dw_in = _dwin(sv["h"], dproj, etok())
    own = None
    if early is not None:
        early, own = _rs_add_chips(early, [dw_in], also=([dw_in], "own"))
    grads = dict(norm_g=dg, w_in=dw_in, conv_a_w=dw_conva, ssd_conv_w=dw_sconv, ssd_conv_b=db_sconv, sc=dsc,
                 g_ssd=dg_ssd, gq=dgq, wq=dwq, gkv=dgkv, wkv=dwkv, w_out=dw_out)
    return dx, grads, reduced, early, own


ANY = pl.BlockSpec(memory_space=pl.ANY)
N_CHIPS = 4
N_DEV = 8


def _place():
    return lax.axis_index("x"), lax.axis_index("y"), lax.axis_index("c")


HBM_SPEC = pl.BlockSpec(memory_space=pltpu.HBM)
SEM_SPEC = pl.BlockSpec(memory_space=pltpu.SEMAPHORE)
PAYLOAD = jnp.bfloat16


def _hbm(a):
    return pltpu.with_memory_space_constraint(a, pltpu.HBM)


def _run_plan(plan, srcs, lands, send_sems, recv_sems, start, wait):
    copies = plan(srcs, lands)
    if start:
        for i, (src, dst, _, to) in enumerate(copies):
            pltpu.make_async_remote_copy(src_ref=src, dst_ref=dst, send_sem=send_sems.at[i], recv_sem=recv_sems.at[i],
                                         device_id=to, device_id_type=MESH_T).start()
    if wait:
        for i, (src, _, arrives, to) in enumerate(copies):
            cp = pltpu.make_async_remote_copy(src_ref=src, dst_ref=arrives, send_sem=send_sems.at[i],
                                              recv_sem=recv_sems.at[i], device_id=to, device_id_type=MESH_T)
            cp.wait_send()
            cp.wait_recv()


def _exchange_start_many(name, groups, deps):
    g = len(groups)
    sizes = [(len(srcs), len(shapes)) for _, _, srcs, shapes in groups]
    n_arr = sum(ns + nl for ns, nl in sizes)
    n_in = n_arr + len(deps)

    def body(*refs):
        at = 0
        for k, ((plan, _, _, _), (ns, nl)) in enumerate(zip(groups, sizes)):
            _run_plan(plan, refs[at:at + ns], refs[at + ns:at + ns + nl], refs[n_in + 2 * k], refs[n_in + 2 * k + 1], True, False)
            at += ns + nl
        refs[-1][...] = jnp.zeros_like(refs[-1])

    arrs, thru, sems = [], [], []
    for _, n_copies, srcs, shapes in groups:
        arrs += [_hbm(a) for a in srcs] + [_hbm(lax.empty(a.shape, a.dtype)) for a in shapes]
        thru += [pltpu.HBM(a.shape, a.dtype) for a in list(srcs) + list(shapes)]
        sems += [pltpu.SemaphoreType.DMA((n_copies,))] * 2
    outs = pl.pallas_call(
        body, name=name,
        out_shape=(*sems, *thru, _sds((8, LANE))),
        in_specs=[HBM_SPEC] * n_arr + [ANY] * len(deps),
        out_specs=(*[SEM_SPEC] * (2 * g), *[HBM_SPEC] * n_arr, pl.BlockSpec(memory_space=pltpu.VMEM)),
        input_output_aliases={i: 2 * g + i for i in range(n_arr)},
        compiler_params=pltpu.CompilerParams(has_side_effects=pltpu.SideEffectType.DATAFLOW_SIDE_EFFECTING),
    )(*arrs, *deps)
    res, at = [], 2 * g
    for k, (ns, nl) in enumerate(sizes):
        res.append(((outs[2 * k], outs[2 * k + 1]), list(outs[at:at + ns]), list(outs[at + ns:at + ns + nl])))
        at += ns + nl
    return res, outs[-1]


def _exchange_start(name, plan, n_copies, srcs, land_shapes, deps):
    (one,), token = _exchange_start_many(name, [(plan, n_copies, srcs, land_shapes)], deps)
    return (*one, token)


def _exchange_wait(name, plan, sems, srcs, lands, after):
    ns, nl = len(srcs), len(lands)

    def body(*refs):
        _run_plan(plan, refs[:ns], refs[ns:ns + nl], refs[ns + nl], refs[ns + nl + 1], False, True)

    outs = pl.pallas_call(
        body, name=name,
        out_shape=[pltpu.HBM(a.shape, a.dtype) for a in list(srcs) + list(lands)],
        in_specs=[HBM_SPEC] * (ns + nl) + [SEM_SPEC, SEM_SPEC] + [ANY] * len(after), out_specs=[HBM_SPEC] * (ns + nl),
        input_output_aliases={i: i for i in range(ns + nl)},
        compiler_params=pltpu.CompilerParams(has_side_effects=pltpu.SideEffectType.DATAFLOW_SIDE_EFFECTING),
    )(*srcs, *lands, sems[0], sems[1], *after)
    return list(outs[:ns]), list(outs[ns:])


def _xchg_begin(name, plan, n_copies, srcs, land_shapes, deps=()):
    sems, srcs_t, lands_t, token = _exchange_start(name + "_start", plan, n_copies, srcs, land_shapes, list(deps))
    return dict(name=name, plan=plan, sems=sems, srcs=srcs_t, lands=lands_t, token=token)


def _xchg_begin_many(name, specs, deps=()):
    res, token = _exchange_start_many(name + "_start", [s[1:] for s in specs], list(deps))
    return [dict(name=s[0], plan=s[1], sems=sems, srcs=srcs_t, lands=lands_t, token=token)
            for s, (sems, srcs_t, lands_t) in zip(specs, res)]


def _xchg_end(h, after):
    return _exchange_wait(h["name"] + "_wait", h["plan"], h["sems"], h["srcs"], h["lands"], after)


def _other_chips():
    x, y, c = _place()
    return [(1 - x, y), (x, 1 - y), (1 - x, 1 - y)]


def _gather_plan(srcs, lands):
    x, y, c = _place()
    me = 2 * x + y
    return [(srcs[a], lands[a].at[me], lands[a].at[2 * cx + cy], (cx, cy, c))
            for (cx, cy) in _other_chips() for a in range(len(srcs))]


def _gather_spec(shards, tag):
    return (f"gather_{tag}", _gather_plan, 3 * len(shards), shards, [_sds((N_CHIPS,) + a.shape, a.dtype) for a in shards])


def _gather_end(h, after):
    shards, lands = _xchg_end(h, after)
    me = 2 * lax.axis_index("x") + lax.axis_index("y")
    return [lax.dynamic_update_index_in_dim(g, s, me, 0) for g, s in zip(lands, shards)]


def _gather_half_plan(srcs, lands):
    x, y, c = _place()
    me = 2 * x + y
    out = []
    for (cx, cy) in _other_chips():
        out.append((srcs[0].at[c], lands[0].at[me, c], lands[0].at[2 * cx + cy, c], (cx, cy, c)))
        out += [(srcs[a], lands[a].at[me], lands[a].at[2 * cx + cy], (cx, cy, c)) for a in range(1, len(srcs))]
    return out


def _forward_plan(bufs, _):
    x, y, c = _place()
    return [(bufs[0].at[2 * cx + cy, c], bufs[0].at[2 * cx + cy, c], bufs[0].at[2 * cx + cy, 1 - c], (x, y, 1 - c))
            for (cx, cy) in _other_chips()]


def _swap_plan(srcs, lands):
    x, y, c = _place()
    return [(srcs[a].at[:, 1 - c], lands[a], lands[a], (x, y, 1 - c)) for a in range(len(srcs))]


def _chips_plan(srcs, lands):
    x, y, c = _place()
    me = 2 * x + y
    return [(srcs[a].at[2 * cx + cy], lands[a].at[me], lands[a].at[2 * cx + cy], (cx, cy, c))
            for (cx, cy) in _other_chips() for a in range(len(srcs))]


def _share_plan(srcs, lands):
    x, y, c = _place()
    return [(srcs[a], lands[a].at[c], lands[a].at[1 - c], (x, y, 1 - c)) for a in range(len(srcs))]


def _allreduce_small(slab, dep=None):
    r = slab.shape[0]

    def body(s_ref, o_ref, gath, send_sems, recv_sems):
        x, y, c = _place()
        me = 4 * x + 2 * y + c
        gath[me] = s_ref[...]
        cps = []
        for rel in range(1, N_DEV):
            px = 1 - x if rel & 4 else x
            py = 1 - y if rel & 2 else y
            pc = 1 - c if rel & 1 else c
            cp = pltpu.make_async_remote_copy(src_ref=s_ref, dst_ref=gath.at[me], send_sem=send_sems.at[rel - 1],
                                              recv_sem=recv_sems.at[rel - 1], device_id=(px, py, pc), device_id_type=MESH_T)
            cp.start()
            cps.append(cp)
        for cp in cps:
            cp.wait()
        acc = gath[0]
        for d in range(1, N_DEV):
            acc = acc + gath[d]
        o_ref[...] = acc

    vm = pl.BlockSpec(memory_space=pltpu.VMEM)
    return _call_after(
        dep, body, (slab,), name="allreduce_small", in_specs=[vm], out_specs=vm, out_shape=_sds((r, LANE)),
        scratch_shapes=[pltpu.VMEM((N_DEV, r, LANE), F32), pltpu.SemaphoreType.DMA((N_DEV - 1,)),
                        pltpu.SemaphoreType.DMA((N_DEV - 1,))],
    )


def _add_mine(g4s, recvs, half):
    n = len(g4s)

    def body(h_ref, *refs):
        for g_ref, r_ref, o_ref in zip(refs[:n], refs[n:2 * n], refs[2 * n:]):
            o_ref[0] = (g_ref[0, 0] + r_ref[0]).astype(o_ref.dtype)

    dims = [g.shape[2:] for g in g4s]
    return pl.pallas_call(
        body, name="add_mine",
        grid_spec=pltpu.PrefetchScalarGridSpec(
            num_scalar_prefetch=1, grid=(N_CHIPS,),
            in_specs=[pl.BlockSpec((1, 1) + d, lambda j, h: (j, h[0], 0, 0)) for d in dims]
            + [pl.BlockSpec((1,) + d, lambda j, h: (j, 0, 0)) for d in dims],
            out_specs=[pl.BlockSpec((1,) + d, lambda j, h: (j, 0, 0)) for d in dims]),
        out_shape=[_sds((N_CHIPS,) + d, PAYLOAD) for d in dims],
        compiler_params=_params(("parallel",)),
    )(half, *g4s, *recvs)


def _add_chips(es, ps, me):
    n = len(es)

    def body(m_ref, *refs):
        for e_ref, p_ref, o_ref in zip(refs[:n], refs[n:2 * n], refs[2 * n:]):
            own = p_ref[0].astype(F32)
            acc = None
            for s in range(N_CHIPS):
                t = jnp.where(m_ref[0] == s, own, e_ref[s].astype(F32))
                acc = t if acc is None else acc + t
            o_ref[...] = acc

    dims = [e.shape[1:] for e in es]
    return pl.pallas_call(
        body, name="add_chips",
        grid_spec=pltpu.PrefetchScalarGridSpec(
            num_scalar_prefetch=1, grid=(1,),
            in_specs=[pl.BlockSpec((N_CHIPS,) + d, lambda i, m: (0, 0, 0)) for d in dims]
            + [pl.BlockSpec((1,) + d, lambda i, m: (m[0], 0, 0)) for d in dims],
            out_specs=[pl.BlockSpec(d, lambda i, m: (0, 0)) for d in dims]),
        out_shape=[_sds(d) for d in dims],
        compiler_params=_params(("arbitrary",)),
    )(me, *es, *ps)


def _rs_begin(gs, tag, deps=()):
    return dict(h=_xchg_begin(*_swap_spec(gs, tag), deps), tag=tag, shapes=[g.shape for g in gs])


def _swap_spec(gs, tag):
    g4 = [g.reshape(N_CHIPS, 2, g.shape[0] // (2 * N_CHIPS), g.shape[1]) for g in gs]
    return (f"rs_swap_{tag}", _swap_plan, len(gs), g4, [_sds((N_CHIPS,) + g.shape[2:]) for g in g4])


def _rs_add_mine(st, after):
    g4, recv = _xchg_end(st["h"], after)
    half = jnp.reshape(lax.axis_index("c"), (1,)).astype(jnp.int32)
    ps = _add_mine(g4, recv, half)
    st["h"] = _xchg_begin(f"rs_chips_{st['tag']}", _chips_plan, 3 * len(ps), ps, [_sds(p.shape, p.dtype) for p in ps])
    return st


def _rs_add_chips(st, after, also=None):
    ps, es = _xchg_end(st["h"], after)
    me = jnp.reshape(2 * lax.axis_index("x") + lax.axis_index("y"), (1,)).astype(jnp.int32)
    fs = _add_chips(es, ps, me)
    share = (f"rs_share_{st['tag']}", _share_plan, len(fs), fs, [_sds((2,) + f.shape) for f in fs])
    if also is None:
        st["h"] = _xchg_begin(*share)
        return st
    gs, tag = also
    st["h"], h = _xchg_begin_many(f"rs_share_{st['tag']}_swap_{tag}", [share, _swap_spec(gs, tag)])
    return st, dict(h=h, tag=tag, shapes=[g.shape for g in gs])


def _rs_end(st, after):
    fs, ss = _xchg_end(st["h"], after)
    c = lax.axis_index("c")
    return [lax.dynamic_update_index_in_dim(s, f, c, 0).reshape(shp[0] // N_CHIPS, shp[1])
            for s, f, shp in zip(ss, fs, st["shapes"])]


WEIGHTS = ["norm_g", "w_in", "conv_a_w", "ssd_conv_w", "ssd_conv_b", "ssd_dt_bias", "ssd_a_log", "ssd_d", "ssd_norm_g",
           "mla_q_norm_g", "w_qb", "mla_kv_norm_g", "w_kvb", "w_out", "final_norm_g"]
BIG = ["w_in", "w_qb", "w_kvb", "w_out"]
SLAB_ROWS = 128


def _to_slab(parts, rows):
    flat = jnp.concatenate([p.reshape(-1) for p in parts])
    return jnp.pad(flat, (0, rows * LANE - flat.shape[0])).reshape(rows, LANE)


def _from_slab(slab, shapes):
    flat = slab.reshape(-1)
    out, off = [], 0
    for shp in shapes:
        n = int(np.prod(shp))
        out.append(flat[off:off + n].reshape(shp))
        off += n
    return out


def kernel(x, positions, norm_g, w_in, conv_a_w, ssd_conv_w, ssd_conv_b, ssd_dt_bias, ssd_a_log, ssd_d, ssd_norm_g, mla_q_norm_g, w_qb, mla_kv_norm_g, w_kvb, w_out, final_norm_g, loss_target, m_norm_g, m_w_in, m_conv_a_w, m_ssd_conv_w, m_ssd_conv_b, m_ssd_dt_bias, m_ssd_a_log, m_ssd_d, m_ssd_norm_g, m_mla_q_norm_g, m_w_qb, m_mla_kv_norm_g, m_w_kvb, m_w_out, m_final_norm_g, v_norm_g, v_w_in, v_conv_a_w, v_ssd_conv_w, v_ssd_conv_b, v_ssd_dt_bias, v_ssd_a_log, v_ssd_d, v_ssd_norm_g, v_mla_q_norm_g, v_w_qb, v_mla_kv_norm_g, v_w_kvb, v_w_out, v_final_norm_g):
    w = dict(norm_g=norm_g, w_in=w_in, conv_a_w=conv_a_w, ssd_conv_w=ssd_conv_w, ssd_conv_b=ssd_conv_b,
             ssd_dt_bias=ssd_dt_bias, ssd_a_log=ssd_a_log, ssd_d=ssd_d, ssd_norm_g=ssd_norm_g, mla_q_norm_g=mla_q_norm_g,
             w_qb=w_qb, mla_kv_norm_g=mla_kv_norm_g, w_kvb=w_kvb, w_out=w_out, final_norm_g=final_norm_g)
    mom = dict(norm_g=m_norm_g, w_in=m_w_in, conv_a_w=m_conv_a_w, ssd_conv_w=m_ssd_conv_w, ssd_conv_b=m_ssd_conv_b,
               ssd_dt_bias=m_ssd_dt_bias, ssd_a_log=m_ssd_a_log, ssd_d=m_ssd_d, ssd_norm_g=m_ssd_norm_g,
               mla_q_norm_g=m_mla_q_norm_g, w_qb=m_w_qb, mla_kv_norm_g=m_mla_kv_norm_g, w_kvb=m_w_kvb, w_out=m_w_out,
               final_norm_g=m_final_norm_g)
    var = dict(norm_g=v_norm_g, w_in=v_w_in, conv_a_w=v_conv_a_w, ssd_conv_w=v_ssd_conv_w, ssd_conv_b=v_ssd_conv_b,
               ssd_dt_bias=v_ssd_dt_bias, ssd_a_log=v_ssd_a_log, ssd_d=v_ssd_d, ssd_norm_g=v_ssd_norm_g,
               mla_q_norm_g=v_mla_q_norm_g, w_qb=v_w_qb, mla_kv_norm_g=v_mla_kv_norm_g, w_kvb=v_w_kvb, w_out=v_w_out,
               final_norm_g=v_final_norm_g)
    chip = 2 * lax.axis_index("x") + lax.axis_index("y")

    def early_shard(l, zero):
        pack = jnp.pad(conv_a_w[l], ((0, 5), (0, 192))) + jnp.pad(ssd_conv_w[l], ((3, 1), (0, 32)))
        return [(_perm_cols(w_in[l]) + zero).astype(MXU), pack + zero]

    def late_shard(l, zero):
        return [(w_out[l] + zero).astype(MXU), (w_qb[l].T + zero).astype(MXU), (w_kvb[l].T + zero).astype(MXU)]

    def early_weights(l, gathered):
        g_in, g_conv = gathered
        return dict(
            norm_g=norm_g[l][None], w_in=g_in.reshape(D_MODEL, NCOL),
            conv_a_w=jnp.concatenate([g_conv[j, 0:3, 0:64] for j in range(N_CHIPS)], axis=1),
            ssd_conv_w=jnp.concatenate([g_conv[j, 3:7, 0:224] for j in range(N_CHIPS)], axis=1),
            ssd_conv_b=ssd_conv_b[l][None], sc=_ssd_scalars(ssd_dt_bias[l], ssd_a_log[l], ssd_d[l]),
            g_ssd=ssd_norm_g[l][None], gq=mla_q_norm_g[l][None], gkv=mla_kv_norm_g[l][None])

    def late_weights(gathered):
        g_out, g_qb, g_kvb = gathered
        return dict(wq=_wq_layout(g_qb.reshape(MLA_HEADS * 96, Q_LORA)), wkv=_wkv_layout(g_kvb.reshape(MLA_HEADS * LANE, KV_LORA)),
                    w_out=g_out.reshape(D_MODEL, D_MODEL))

    def late_grads(dw_out, dwq, dwkv):
        wq = jnp.pad(_wq_unlayout(dwq).reshape(N_CHIPS, 144, Q_LORA), ((0, 0), (0, 16), (0, 0)))
        return [dw_out, wq.reshape(N_CHIPS * 160, Q_LORA), _wkv_unlayout(dwkv)]

    def large_grads(g):
        return [g["w_in"]] + late_grads(g["w_out"], g["wq"], g["wkv"])

    w_in0, pack0 = early_shard(0, 0.0)
    half = w_in0.shape[0] // 2
    gather_a0 = _xchg_begin("gather_a0", _gather_half_plan, 6, [w_in0.reshape(2, half, NCOL), pack0],
                            [_sds((N_CHIPS, 2, half, NCOL), MXU), _sds((N_CHIPS,) + pack0.shape)])
    zero = gather_a0["token"][0, 0]
    cos, sin = _rope_tables(positions[0] + zero.astype(jnp.int32))
    late0, shards1 = late_shard(0, zero), early_shard(1, zero) + late_shard(1, zero)
    mine0, (g_in0, g_conv0) = _xchg_end(gather_a0, [cos, sin] + late0 + shards1)
    forward_a0 = _xchg_begin("forward_a0", _forward_plan, 3, [g_in0], [])
    gather_b0, gather_1 = _xchg_begin_many("gather_b0_1", [_gather_spec(late0, "b0"), _gather_spec(shards1, "1")],
                                           [forward_a0["token"]])
    (g_in0,), _ = _xchg_end(forward_a0, [gather_1["token"]])
    lw0 = early_weights(0, [lax.dynamic_update_index_in_dim(g, s_, chip, 0) for g, s_ in zip((g_in0, g_conv0), mine0)])
    x1, sv0, lw0 = _layer_fwd(x[0], lw0, cos, sin, gather_1["token"],
                              lambda ya, y_ssd: late_weights(_gather_end(gather_b0, [ya, y_ssd])))
    g1 = _gather_end(gather_1, [x1])
    (dx, dgf, loss), sv1, lw1 = _layer_fwd(x1, {**early_weights(1, g1[:2]), **late_weights(g1[2:])}, cos, sin,
                                           head=(final_norm_g[None], loss_target[0]))

    dx, lg1, _, _, _ = _layer_bwd(dx, lw1, sv1, cos, sin)
    grad_x, lg0, red1, rs0_late, rs0 = _layer_bwd(dx, lw0, sv0, cos, sin, _rs_begin(large_grads(lg1), 1),
                                                  late_grads)
    lg = [lg0, lg1]
    grad = {}

    small_names = ["norm_g", "conv_a_w", "ssd_conv_w", "ssd_conv_b", "sc", "g_ssd", "gq", "gkv"]
    parts = [loss[0, 0:1], dgf]
    for nm in small_names:
        parts += [lg[l][nm][:3, DT_LANE:DT_LANE + SSD_HEADS] if nm == "sc" else lg[l][nm] for l in range(DEPTH)]
    shapes = [(1,), (D_MODEL,)] + [(DEPTH,) + shp for shp in ((D_MODEL,), (3, D_CONV_A), (4, N_XBC), (N_XBC,), (3, SSD_HEADS),
                                                              (D_SSD,), (Q_LORA,), (KV_LORA,))]
    red_slab = _allreduce_small(_to_slab(parts, SLAB_ROWS), rs0["h"]["token"])
    rs0 = _rs_add_mine(rs0, [red_slab])
    red = _from_slab(red_slab + rs0["h"]["token"][0, 0], shapes)
    loss_out = red[0][0]
    grad["final_norm_g"] = red[1]
    grad["norm_g"], conv_a_full, sconv_full, grad["ssd_conv_b"], sc_grads = red[2:7]
    grad["ssd_norm_g"], grad["mla_q_norm_g"], grad["mla_kv_norm_g"] = red[7:10]
    grad["conv_a_w"] = lax.dynamic_slice_in_dim(conv_a_full, chip * 64, 64, axis=2)
    grad["ssd_conv_w"] = lax.dynamic_slice_in_dim(sconv_full, chip * 224, 224, axis=2)
    grad["ssd_dt_bias"], grad["ssd_a_log"], grad["ssd_d"] = sc_grads[:, 0], sc_grads[:, 1], sc_grads[:, 2]

    delta, new_m, new_v = {}, {}, {}
    small = [nm for nm in WEIGHTS if nm not in BIG]
    row2 = lambda a: a[None] if a.ndim == 1 else a
    small_out = _adamw(*[[row2(a[nm]) for nm in small] for a in (w, grad, mom, var)], whole=True)
    for nm, (dv, mv, vv) in zip(small, small_out):
        delta[nm], new_m[nm], new_v[nm] = [a.reshape(w[nm].shape) for a in (dv, mv, vv)]

    r_out, r_qb, r_kvb = [jnp.stack([a, b]) for a, b in zip(_rs_end(rs0_late, [red_slab]), red1[1:])]
    late = [nm for nm in BIG if nm != "w_in"]
    view = {nm: (lambda a: a) if nm == "w_out" else (lambda a: jnp.swapaxes(a, 1, 2)) for nm in late}
    late_g = [dict(w_out=r_out, w_qb=r_qb[:, :144], w_kvb=r_kvb)[nm] for nm in late]
    late_out = _adamw(*[[view[nm](a[nm]) for nm in late] for a in (w,)], late_g,
                      *[[view[nm](a[nm]) for nm in late] for a in (mom, var)], whole=False)
    for nm, gv, (dv, mv, vv) in zip(late, late_g, late_out):
        grad[nm], delta[nm], new_m[nm], new_v[nm] = [view[nm](a) for a in (gv, dv, mv, vv)]
    g_in1 = _unperm_cols(red1[0])
    shadow_work = [a for row in small_out + late_out for a in row] + [grad[nm] for nm in small] + [g_in1]
    r_in0, = _rs_end(_rs_add_chips(rs0, shadow_work), [])
    to_cols, from_cols = (lambda a: jnp.transpose(a, (2, 0, 1))), (lambda a: jnp.transpose(a, (1, 2, 0)))
    grad["w_in"], delta["w_in"], new_m["w_in"], new_v["w_in"] = [from_cols(a) for a in _adamw_cols(
        to_cols(w["w_in"]), [_unperm_cols(r_in0), g_in1], to_cols(mom["w_in"]), to_cols(var["w_in"]))]

    return (loss_out, grad_x[None], *[grad[nm] for nm in WEIGHTS], *[delta[nm] for nm in WEIGHTS],
            *[new_m[nm] for nm in WEIGHTS], *[new_v[nm] for nm in WEIGHTS])
```

```python
import functools
import math

import numpy as np
import jax
import jax.numpy as jnp
from jax import lax
from jax.experimental import pallas as pl
from jax.experimental.pallas import tpu as pltpu

F32 = jnp.float32
MXU = jnp.bfloat16

D_MODEL = 1024
DEPTH = 2
D_CONV_A = 256
D_SSD = 384
SSD_HEADS = 6
SSD_BC = 256
SSD_CHUNK = 128
SSD_CHUNKS_PER_STEP = 4
SSD_NORM_EPS = 1e-5
MLA_HEADS = 6
Q_LORA = 256
KV_LORA = 128
QK_NOPE = 64
QK_ROPE = 32
V_DIM = 64
D_MLA = 384
ROPE_BASE = 10000.0
NORM_EPS = 1e-6
IN_COLS = 3110
LANE = 128

O_AH, O_AB, O_AC, O_AZ = 0, 256, 512, 768
O_XBC = 1024
O_SZ = 1920
O_CQA = 2304
O_CKV = 2560
O_CZ = 2688
O_TAIL = 3072
NCOL = 3200
N_XBC = D_SSD + 2 * SSD_BC
DT_LANE = 32
ROPE_LANE = 64

ADAM_LR, ADAM_B1, ADAM_B2, ADAM_EPS, ADAM_WD, ADAM_STEP = 0.001, 0.9, 0.999, 1e-08, 0.01, 10

VMEM_LIMIT = 56 * 1024 * 1024
MESH_T = pl.DeviceIdType.MESH


def _dot(a, b):
    return jnp.dot(a.astype(MXU), b.astype(MXU), preferred_element_type=F32)


def _dot_nt(a, b):
    return lax.dot_general(a.astype(MXU), b.astype(MXU), (((1,), (1,)), ((), ())), preferred_element_type=F32)


def _dot_tn(a, b):
    return lax.dot_general(a.astype(MXU), b.astype(MXU), (((0,), (0,)), ((), ())), preferred_element_type=F32)


def _dot_hi(a, b):
    return jnp.dot(a, b, precision=lax.Precision.HIGHEST, preferred_element_type=F32)


def _dot_hi_tn(a, b):
    return lax.dot_general(a, b, (((0,), (0,)), ((), ())), precision=lax.Precision.HIGHEST, preferred_element_type=F32)


def _sigmoid(z):
    return 1.0 / (1.0 + jnp.exp(-z))


def _silu(z):
    return z * _sigmoid(z)


def _dsilu(z):
    s = _sigmoid(z)
    return s * (1.0 + z * (1.0 - s))


def _softplus(z):
    e = jnp.exp(-jnp.abs(z))
    return jnp.maximum(z, 0.0) + jnp.where(e < 1e-3, e * (1.0 - 0.5 * e), jnp.log(1.0 + e))


def _iota(shape, dim):
    return lax.broadcasted_iota(jnp.int32, shape, dim)


def _shift_down(u, k):
    if k == 0:
        return u
    return jnp.where(_iota(u.shape, 0) >= k, pltpu.roll(u, k, 0), 0.0)


def _shift_up(u, k):
    if k == 0:
        return u
    n = u.shape[0]
    return jnp.where(_iota(u.shape, 0) < n - k, pltpu.roll(u, n - k, 0), 0.0)


def _rope_swap(t):
    lane = _iota(t.shape, 1)
    lo = (lane >= ROPE_LANE) & (lane < ROPE_LANE + 16)
    hi = (lane >= ROPE_LANE + 16) & (lane < ROPE_LANE + 32)
    return jnp.where(lo, pltpu.roll(t, LANE - 16, 1), jnp.where(hi, pltpu.roll(t, 16, 1), 0.0))


def _params(sem=None):
    return pltpu.CompilerParams(dimension_semantics=sem, vmem_limit_bytes=VMEM_LIMIT)


def _full(shape):
    nd = len(shape)
    return pl.BlockSpec(shape, lambda *_: (0,) * nd)


def _sds(shape, dtype=F32):
    return jax.ShapeDtypeStruct(shape, dtype)


def _tile(s):
    return min(512, s)


def _row(ts, w):
    return pl.BlockSpec((ts, w), lambda i: (i, 0))


def _gate_cols(ts, off):
    return pl.BlockSpec((ts, D_SSD), lambda i, _o=off // D_SSD: (i, _o))


def _col(s, off):
    return pl.BlockSpec((s, LANE), lambda j, _o=off // LANE: (0, _o + j))


def _call_after(dep, body, args, *, in_specs, **kw):
    if dep is None:
        return pl.pallas_call(body, in_specs=in_specs, **kw)(*args)
    n = len(args)

    def body_dep(*refs):
        body(*refs[:n], *refs[n + 1:])

    return pl.pallas_call(body_dep, in_specs=list(in_specs) + [pl.BlockSpec(memory_space=pl.ANY)], **kw)(*args, dep)


def _rms(c, g):
    r = lax.rsqrt(jnp.mean(c * c, axis=-1, keepdims=True) + NORM_EPS)
    return c * r * g, r


def _rms_bwd(dn, c, r, g):
    ch = c * r
    dch = dn * g
    dc = r * (dch - ch * jnp.mean(dch * ch, axis=-1, keepdims=True))
    return dc, jnp.sum(dn * ch, axis=0, keepdims=True)


def _inproj_fwd(x, g, w, dep=None):
    s = x.shape[0]
    ts = _tile(s)

    def body(x_ref, g_ref, w_ref, proj_ref, h_ref, r_ref):
        hn, r = _rms(x_ref[...], g_ref[...])
        h = hn.astype(MXU)
        h_ref[...] = h
        r_ref[...] = r
        proj_ref[...] = jnp.dot(h, w_ref[...], preferred_element_type=F32)

    return _call_after(
        dep, body, (x, g, w), name="inproj_fwd", grid=(s // ts,),
        in_specs=[_row(ts, D_MODEL), _full((1, D_MODEL)), _full((D_MODEL, NCOL))],
        out_specs=[_row(ts, NCOL), _row(ts, D_MODEL), _row(ts, 1)],
        out_shape=[_sds((s, NCOL)), _sds((s, D_MODEL), MXU), _sds((s, 1))],
        compiler_params=_params(("parallel",)),
    )


def _conva_fwd(proj, w):
    s = proj.shape[0]

    def body(h_ref, b_ref, c_ref, z_ref, w_ref, y_ref):
        u = c_ref[...] * h_ref[...]
        wv = w_ref[...]
        cv = wv[2:3, :] * u + wv[1:2, :] * _shift_down(u, 1) + wv[0:1, :] * _shift_down(u, 2)
        y_ref[...] = b_ref[...] * cv * _silu(z_ref[...])

    return pl.pallas_call(
        body, name="conva_fwd", grid=(D_CONV_A // LANE,),
        in_specs=[_col(s, O_AH), _col(s, O_AB), _col(s, O_AC), _col(s, O_AZ), pl.BlockSpec((3, LANE), lambda j: (0, j))],
        out_specs=pl.BlockSpec((s, LANE), lambda j: (0, j)),
        out_shape=_sds((s, D_CONV_A)),
        compiler_params=_params(("parallel",)),
    )(proj, proj, proj, proj, w)


def _sconv_pre(u, wv, bv):
    return (wv[3:4, :] * u + wv[2:3, :] * _shift_down(u, 1) + wv[1:2, :] * _shift_down(u, 2)
            + wv[0:1, :] * _shift_down(u, 3) + bv)


def _sconv_fwd(proj, w, b):
    s = proj.shape[0]

    def body(u_ref, w_ref, b_ref, o_ref):
        o_ref[...] = _silu(_sconv_pre(u_ref[...], w_ref[...], b_ref[...]))

    return pl.pallas_call(
        body, name="sconv_fwd", grid=(N_XBC // LANE,),
        in_specs=[_col(s, O_XBC), pl.BlockSpec((4, LANE), lambda j: (0, j)), pl.BlockSpec((1, LANE), lambda j: (0, j))],
        out_specs=pl.BlockSpec((s, LANE), lambda j: (0, j)),
        out_shape=_sds((s, N_XBC)),
        compiler_params=_params(("parallel",)),
    )(proj, w, b)


def _ssd_chunk_common(tail, sc):
    l = SSD_CHUNK
    lane = _iota((l, LANE), 1)
    row = _iota((l, LANE), 0)
    tri = (row >= lane).astype(F32)
    a_row = -jnp.exp(sc[1:2, :])
    pre = tail + sc[0:1, :]
    dt = _softplus(pre)
    a_cs = _dot_hi(tri, dt * a_row)
    return lane, row, tri, a_row, pre, dt, a_cs, a_cs.T


def _pick_col(m, lane, k):
    return jnp.sum(jnp.where(lane == k, m, 0.0), axis=1, keepdims=True)


def _pick_row(m, row, k):
    return jnp.sum(jnp.where(row == k, m, 0.0), axis=0, keepdims=True)


def _ssd_fwd(xbc, proj, sc):
    s = xbc.shape[0]
    nc = s // SSD_CHUNK
    l = SSD_CHUNK
    cps = SSD_CHUNKS_PER_STEP

    def body(xbc_ref, tail_ref, sc_ref, y_ref, st_ref, state):
        @pl.when(pl.program_id(0) == 0)
        def _():
            state[...] = jnp.zeros_like(state)

        sc_v = sc_ref[...]
        lane1 = _iota((1, LANE), 1)
        rowp = _iota((LANE, 1), 0)
        d_row = sc_v[2:3, :]
        states = [state[j] for j in range(3)]
        for u in range(cps):
            r = slice(u * l, (u + 1) * l)
            lane, row, _, _, _, dt, a_cs, a_t = _ssd_chunk_common(tail_ref[r, :], sc_v)
            for j in range(3):
                st_ref[u, j] = states[j]
            for j in range(3):
                xpair = xbc_ref[r, LANE * j:LANE * (j + 1)]
                sp = states[j]
                ypair = jnp.zeros((l, LANE), F32)
                new_s = jnp.zeros((LANE, LANE), F32)
                decay = jnp.zeros((LANE, 1), F32)
                for half in range(2):
                    h = 2 * j + half
                    g = h // 3
                    hm = (lane < 64) if half == 0 else (lane >= 64)
                    hrow = (rowp < 64) if half == 0 else (rowp >= 64)
                    ac = _pick_col(a_cs, lane, DT_LANE + h)
                    ar = _pick_row(a_t, row, DT_LANE + h)
                    dtc = _pick_col(dt, lane, DT_LANE + h)
                    alast = jnp.sum(jnp.where(lane1 == l - 1, ar, 0.0), axis=1, keepdims=True)
                    dh = jnp.sum(jnp.where(lane1 == DT_LANE + h, d_row, 0.0), axis=1, keepdims=True)
                    xm = jnp.where(hm, xpair, 0.0)
                    xd = xm * dtc
                    bm = xbc_ref[r, D_SSD + LANE * g:D_SSD + LANE * (g + 1)]
                    cm = xbc_ref[r, D_SSD + SSD_BC + LANE * g:D_SSD + SSD_BC + LANE * (g + 1)]
                    lm = jnp.where(row >= lane, jnp.exp(jnp.minimum(ac - ar, 0.0)), 0.0)
                    y_diag = _dot(_dot_nt(cm, bm) * lm, xd)
                    y_off = jnp.where(hm, _dot_nt(cm, sp), 0.0) * jnp.exp(ac)
                    ypair = ypair + y_diag + y_off + xm * dh
                    new_s = new_s + _dot_tn(xd * jnp.exp(alast - ac), bm)
                    decay = jnp.where(hrow, jnp.exp(alast), decay)
                states[j] = sp * decay + new_s
                y_ref[r, LANE * j:LANE * (j + 1)] = ypair
        for j in range(3):
            state[j] = states[j]

    return pl.pallas_call(
        body, name="ssd_fwd", grid=(nc // cps,),
        in_specs=[pl.BlockSpec((cps * l, N_XBC), lambda c: (c, 0)),
                  pl.BlockSpec((cps * l, LANE), lambda c: (c, O_TAIL // LANE)), _full((8, LANE))],
        out_specs=[pl.BlockSpec((cps * l, D_SSD), lambda c: (c, 0)), pl.BlockSpec((cps, 3, LANE, LANE), lambda c: (c, 0, 0, 0))],
        out_shape=[_sds((s, D_SSD)), _sds((nc, 3, LANE, LANE))],
        scratch_shapes=[pltpu.VMEM((3, LANE, LANE), F32)],
        compiler_params=_params(("arbitrary",)),
    )(xbc, proj, sc)


def _mla_prep_fwd(proj, gq, gkv, wq, wkv, cos, sin):
    s = proj.shape[0]
    ts = _tile(s)
    nh = MLA_HEADS

    def body(cqa_ref, ckv_ref, tail_ref, gq_ref, gkv_ref, wq_ref, wkv_ref, cos_ref, sin_ref,
             q_ref, k_ref, v_ref, qn_ref, kvn_ref, rq_ref, rkv_ref):
        qn, rq = _rms(cqa_ref[...], gq_ref[...])
        kvn, rkv = _rms(ckv_ref[...], gkv_ref[...])
        qn = qn.astype(MXU)
        kvn = kvn.astype(MXU)
        qn_ref[...] = qn
        kvn_ref[...] = kvn
        rq_ref[...] = rq
        rkv_ref[...] = rkv
        q = _dot_nt(qn, wq_ref[...])
        kv = _dot_nt(kvn, wkv_ref[...])
        cosv = cos_ref[...]
        sinv = sin_ref[...]
        lane = _iota((ts, LANE), 1)
        rope_lanes = (lane >= ROPE_LANE) & (lane < ROPE_LANE + QK_ROPE)
        kr = jnp.where(rope_lanes, pltpu.roll(tail_ref[...], ROPE_LANE, 1), 0.0)
        kr = kr * cosv + _rope_swap(kr) * sinv
        for h in range(nh):
            qh = q[:, LANE * h:LANE * (h + 1)]
            q_ref[h] = ((qh * cosv + _rope_swap(qh) * sinv) * ATT_SCALE).astype(MXU)
            k_ref[h] = (kv[:, LANE * h:LANE * (h + 1)] + kr).astype(MXU)
            v_ref[h] = kv[:, LANE * (nh + h):LANE * (nh + h + 1)].astype(MXU)

    head = pl.BlockSpec((nh, ts, LANE), lambda i: (0, i, 0))
    return pl.pallas_call(
        body, name="mla_prep_fwd", grid=(s // ts,),
        in_specs=[pl.BlockSpec((ts, Q_LORA), lambda i: (i, O_CQA // Q_LORA)),
                  pl.BlockSpec((ts, KV_LORA), lambda i: (i, O_CKV // KV_LORA)),
                  pl.BlockSpec((ts, LANE), lambda i: (i, O_TAIL // LANE)),
                  _full((1, Q_LORA)), _full((1, KV_LORA)), _full((nh * LANE, Q_LORA)), _full((2 * nh * LANE, KV_LORA)),
                  _row(ts, LANE), _row(ts, LANE)],
        out_specs=[head, head, head, _row(ts, Q_LORA), _row(ts, KV_LORA), _row(ts, 1), _row(ts, 1)],
        out_shape=[_sds((nh, s, LANE), MXU)] * 3 + [_sds((s, Q_LORA), MXU), _sds((s, KV_LORA), MXU), _sds((s, 1)), _sds((s, 1))],
        compiler_params=_params(("parallel",)),
    )(proj, proj, proj, gq, gkv, wq, wkv, cos, sin)


ATT_SCALE = (QK_NOPE + QK_ROPE) ** -0.5
NEG = -1e30


def _att_tile(s, most):
    return min(most, s // 2)


ATT_FWD_TILE = 1024
ATT_BWD_TILE = 512


def _attn_fwd(q, k, v):
    nh, s, _ = q.shape
    tq = _att_tile(s, ATT_FWD_TILE)
    nq = s // tq

    def body(q_ref, k_ref, v_ref, o_ref, lse_ref):
        i = pl.program_id(1)
        rowi = _iota((tq, tq), 0)
        coli = _iota((tq, tq), 1)
        zero = (jnp.full((tq, 1), NEG, F32), jnp.zeros((tq, 1), F32), jnp.zeros((tq, LANE), F32))
        state = [zero, zero]
        done = [zero, zero]
        for t in range(nq + 1):
            first = t <= i
            qblk = jnp.where(first, i, nq - 1 - i)
            kblk = jnp.where(first, t, t - i - 1)
            qoff = pl.multiple_of(qblk * tq, tq)
            koff = pl.multiple_of(kblk * tq, tq)
            keep = coli <= rowi + jnp.where(kblk == qblk, 0, tq)
            restart = t == i + 1
            for hh in range(2):
                m, lsum, acc = state[hh]
                if t > 0:
                    done[hh] = tuple(jnp.where(restart, a, b) for a, b in zip(state[hh], done[hh]))
                    m = jnp.where(restart, NEG, m)
                    lsum = jnp.where(restart, 0.0, lsum)
                    acc = jnp.where(restart, 0.0, acc)
                sc = _dot_nt(q_ref[hh, pl.ds(qoff, tq), :], k_ref[hh, pl.ds(koff, tq), :])
                sc = jnp.where(keep, sc, NEG)
                m_new = jnp.maximum(m, jnp.max(sc, axis=1, keepdims=True))
                p = jnp.exp(sc - m_new)
                alpha = jnp.exp(m - m_new)
                lsum = alpha * lsum + jnp.sum(p, axis=1, keepdims=True)
                acc = alpha * acc + _dot(p, v_ref[hh, pl.ds(koff, tq), :])
                state[hh] = (m_new, lsum, acc)
        for blk, res in ((i, done), (nq - 1 - i, state)):
            off = pl.multiple_of(blk * tq, tq)
            out = None
            for hh in range(2):
                m, lsum, acc = res[hh]
                o = acc * (1.0 / lsum)
                lse_ref[hh, pl.ds(off, tq), :] = m + jnp.log(lsum)
                out = o if hh == 0 else out + pltpu.roll(o, V_DIM, 1)
            o_ref[pl.ds(off, tq), :] = out

    pair = pl.BlockSpec((2, s, LANE), lambda j, i: (j, 0, 0))
    return pl.pallas_call(
        body, name="attn_fwd", grid=(nh // 2, nq // 2),
        in_specs=[pair, pair, pair],
        out_specs=[pl.BlockSpec((s, LANE), lambda j, i: (0, j)), pl.BlockSpec((2, s, 1), lambda j, i: (j, 0, 0))],
        out_shape=[_sds((s, D_MLA)), _sds((nh, s, 1))],
        compiler_params=_params(("parallel", "arbitrary")),
    )(q, k, v)


def _ssd_gate(y_ssd, s_z, g):
    yz = y_ssd * _silu(s_z)
    g0 = _iota(yz.shape, 1) < D_SSD // 2
    sq = yz * yz
    ms0 = jnp.sum(jnp.where(g0, sq, 0.0), axis=1, keepdims=True) / (D_SSD // 2)
    ms1 = jnp.sum(jnp.where(g0, 0.0, sq), axis=1, keepdims=True) / (D_SSD // 2)
    r = jnp.where(g0, lax.rsqrt(ms0 + SSD_NORM_EPS), lax.rsqrt(ms1 + SSD_NORM_EPS))
    nrm = yz * r
    return nrm * g, nrm, r, g0


def _outproj_fwd(x, proj, ya, y_ssd, o, g_ssd, w):
    s = x.shape[0]
    ts = _tile(s)

    def body(x_ref, sz_ref, cz_ref, ya_ref, ys_ref, o_ref, g_ref, w_ref, xo_ref, y_ref):
        yb = _ssd_gate(ys_ref[...], sz_ref[...], g_ref[...])[0]
        yc = o_ref[...] * _silu(cz_ref[...])
        y = jnp.concatenate([ya_ref[...], yb, yc], axis=1).astype(MXU)
        y_ref[...] = y
        xo_ref[...] = x_ref[...] + jnp.dot(y, w_ref[...], preferred_element_type=F32)

    return pl.pallas_call(
        body, name="outproj_fwd", grid=(s // ts,),
        in_specs=[_row(ts, D_MODEL), _gate_cols(ts, O_SZ), _gate_cols(ts, O_CZ), _row(ts, D_CONV_A), _row(ts, D_SSD),
                  _row(ts, D_MLA), _full((1, D_SSD)), _full((D_MODEL, D_MODEL))],
        out_specs=[_row(ts, D_MODEL), _row(ts, D_MODEL)],
        out_shape=[_sds((s, D_MODEL)), _sds((s, D_MODEL), MXU)],
        compiler_params=_params(("parallel",)),
    )(x, proj, proj, ya, y_ssd, o, g_ssd, w)


def _outproj_loss(x, proj, ya, y_ssd, o, g_ssd, w, final_g, tgt):
    s = x.shape[0]
    ts = _tile(s)

    def body(x_ref, sz_ref, cz_ref, ya_ref, ys_ref, o_ref, g_ref, w_ref, fg_ref, t_ref, dx_ref, dg_ref, loss_ref, y_ref):
        @pl.when(pl.program_id(0) == 0)
        def _():
            dg_ref[...] = jnp.zeros_like(dg_ref)
            loss_ref[...] = jnp.zeros_like(loss_ref)

        yb = _ssd_gate(ys_ref[...], sz_ref[...], g_ref[...])[0]
        yc = o_ref[...] * _silu(cz_ref[...])
        y = jnp.concatenate([ya_ref[...], yb, yc], axis=1).astype(MXU)
        y_ref[...] = y
        xv = x_ref[...] + jnp.dot(y, w_ref[...], preferred_element_type=F32)
        gv = fg_ref[...]
        yn, r = _rms(xv, gv)
        e = yn - t_ref[...]
        loss_ref[...] += jnp.sum(jnp.sum(e * e, axis=1, keepdims=True), axis=0, keepdims=True) * (0.5 / D_MODEL)
        dx, dg = _rms_bwd(e * (1.0 / D_MODEL), xv, r, gv)
        dx_ref[...] = dx
        dg_ref[...] += dg

    return pl.pallas_call(
        body, name="outproj_loss", grid=(s // ts,),
        in_specs=[_row(ts, D_MODEL), _gate_cols(ts, O_SZ), _gate_cols(ts, O_CZ), _row(ts, D_CONV_A), _row(ts, D_SSD),
                  _row(ts, D_MLA), _full((1, D_SSD)), _full((D_MODEL, D_MODEL)), _full((1, D_MODEL)), _row(ts, D_MODEL)],
        out_specs=[_row(ts, D_MODEL), _full((1, D_MODEL)), _full((1, LANE)), _row(ts, D_MODEL)],
        out_shape=[_sds((s, D_MODEL)), _sds((1, D_MODEL)), _sds((1, LANE)), _sds((s, D_MODEL), MXU)],
        compiler_params=_params(("arbitrary",)),
    )(x, proj, proj, ya, y_ssd, o, g_ssd, w, final_g, tgt)


def _outproj_bwd(dout, y, w, proj, y_ssd, o, g_ssd, dep=None):
    s = dout.shape[0]
    ts = _tile(s)

    def body(dout_ref, y_ref, w_ref, sz_ref, cz_ref, ys_ref, o_ref, g_ref,
             dya_ref, dys_ref, dsz_ref, dattn_ref, dcz_ref, dg_ref, dw_ref):
        @pl.when(pl.program_id(0) == 0)
        def _():
            dw_ref[...] = jnp.zeros_like(dw_ref)
            dg_ref[...] = jnp.zeros_like(dg_ref)

        dout_b = dout_ref[...].astype(MXU)
        dw_ref[...] += _dot_tn(y_ref[...], dout_b)
        dy = _dot_nt(dout_b, w_ref[...])
        dya_ref[...] = dy[:, :D_CONV_A]
        dyb = dy[:, D_CONV_A:D_CONV_A + D_SSD]
        sz = sz_ref[...]
        ys = ys_ref[...]
        gv = g_ref[...]
        _, nrm, r, g0 = _ssd_gate(ys, sz, gv)
        dg_ref[...] += jnp.sum(dyb * nrm, axis=0, keepdims=True)
        dn = dyb * gv
        t = dn * nrm
        mean = jnp.where(g0, jnp.sum(jnp.where(g0, t, 0.0), axis=1, keepdims=True),
                         jnp.sum(jnp.where(g0, 0.0, t), axis=1, keepdims=True)) / (D_SSD // 2)
        dyz = r * (dn - nrm * mean)
        dys_ref[...] = dyz * _silu(sz)
        dsz_ref[...] = (dyz * ys * _dsilu(sz)).astype(MXU)
        dyc = dy[:, D_CONV_A + D_SSD:]
        cz = cz_ref[...]
        dattn_ref[...] = dyc * _silu(cz)
        dcz_ref[...] = (dyc * o_ref[...] * _dsilu(cz)).astype(MXU)

    return _call_after(
        dep, body, (dout, y, w, proj, proj, y_ssd, o, g_ssd), name="outproj_bwd", grid=(s // ts,),
        in_specs=[_row(ts, D_MODEL), _row(ts, D_MODEL), _full((D_MODEL, D_MODEL)), _gate_cols(ts, O_SZ), _gate_cols(ts, O_CZ),
                  _row(ts, D_SSD), _row(ts, D_MLA), _full((1, D_SSD))],
        out_specs=[_row(ts, D_CONV_A), _row(ts, D_SSD), _row(ts, D_SSD), _row(ts, D_MLA), _row(ts, D_MLA),
                   _full((1, D_SSD)), _full((D_MODEL, D_MODEL))],
        out_shape=[_sds((s, D_CONV_A)), _sds((s, D_SSD)), _sds((s, D_SSD), MXU), _sds((s, D_MLA)), _sds((s, D_MLA), MXU),
                   _sds((1, D_SSD)), _sds((D_MODEL, D_MODEL))],
        compiler_params=_params(("arbitrary",)),
    )


def _attn_bwd(q, k, v, o, d_o, lse, dep=None):
    nh, s, _ = q.shape
    tq = _att_tile(s, ATT_BWD_TILE)
    nq = s // tq

    def body(q_ref, k_ref, v_ref, o_ref, do_ref, lse_ref, dq_ref, dk_ref, dv_ref, dop, delta):
        i = pl.program_id(1)

        @pl.when(i == 0)
        def _():
            lane = _iota((s, LANE), 1)
            for hh in range(2):
                dov = do_ref[...]
                ov = o_ref[...]
                if hh == 1:
                    dov = pltpu.roll(dov, V_DIM, 1)
                    ov = pltpu.roll(ov, V_DIM, 1)
                dov = jnp.where(lane < V_DIM, dov, 0.0)
                dop[hh] = dov.astype(MXU)
                delta[hh] = jnp.sum(dov * ov, axis=1, keepdims=True)
                dq_ref[hh] = jnp.zeros((s, LANE), F32)

        rowi = _iota((tq, tq), 0)
        coli = _iota((tq, tq), 1)
        z = jnp.zeros((tq, LANE), F32)
        state = [(z, z), (z, z)]
        done = [(z, z), (z, z)]
        for t in range(nq + 1):
            first = t <= nq - 1 - i
            kblk = jnp.where(first, i, nq - 1 - i)
            qblk = jnp.where(first, i + t, t - 1)
            qoff = pl.multiple_of(qblk * tq, tq)
            koff = pl.multiple_of(kblk * tq, tq)
            keep = coli <= rowi + jnp.where(kblk == qblk, 0, tq)
            restart = t == nq - i
            for hh in range(2):
                dk, dv = state[hh]
                if t > 0:
                    done[hh] = tuple(jnp.where(restart, a, b) for a, b in zip(state[hh], done[hh]))
                    dk = jnp.where(restart, 0.0, dk)
                    dv = jnp.where(restart, 0.0, dv)
                kb = k_ref[hh, pl.ds(koff, tq), :]
                qb = q_ref[hh, pl.ds(qoff, tq), :]
                dob = dop[hh, pl.ds(qoff, tq), :]
                sc = jnp.where(keep, _dot_nt(qb, kb), NEG)
                p = jnp.exp(sc - lse_ref[hh, pl.ds(qoff, tq), :])
                dp = _dot_nt(dob, v_ref[hh, pl.ds(koff, tq), :])
                ds = p * (dp - delta[hh, pl.ds(qoff, tq), :])
                dq_ref[hh, pl.ds(qoff, tq), :] += _dot(ds, kb)
                state[hh] = (dk + _dot_tn(ds, qb), dv + _dot_tn(p, dob))
        for blk, res in ((i, done), (nq - 1 - i, state)):
            off = pl.multiple_of(blk * tq, tq)
            for hh in range(2):
                dk_ref[hh, pl.ds(off, tq), :] = res[hh][0]
                dv_ref[hh, pl.ds(off, tq), :] = res[hh][1]

    pair = pl.BlockSpec((2, s, LANE), lambda j, i: (j, 0, 0))
    return _call_after(
        dep, body, (q, k, v, o, d_o, lse), name="attn_bwd", grid=(nh // 2, nq // 2),
        in_specs=[pair, pair, pair, pl.BlockSpec((s, LANE), lambda j, i: (0, j)), pl.BlockSpec((s, LANE), lambda j, i: (0, j)),
                  pl.BlockSpec((2, s, 1), lambda j, i: (j, 0, 0))],
        out_specs=[pair, pair, pair],
        out_shape=[_sds((nh, s, LANE))] * 3,
        scratch_shapes=[pltpu.VMEM((2, s, LANE), MXU), pltpu.VMEM((2, s, 1), F32)],
        compiler_params=_params(("parallel", "arbitrary")),
    )


def _ssd_bwd(xbc, proj, sc, states, dy, dep=None):
    s = xbc.shape[0]
    nc = s // SSD_CHUNK
    l = SSD_CHUNK
    cps = SSD_CHUNKS_PER_STEP

    def body(xbc_ref, tail_ref, sc_ref, st_ref, dy_ref, dxbc_ref, dtail_ref, dsc_ref, dstate):
        @pl.when(pl.program_id(0) == 0)
        def _():
            dstate[...] = jnp.zeros_like(dstate)
            dsc_ref[...] = jnp.zeros_like(dsc_ref)

        sc_v = sc_ref[...]
        lane1 = _iota((1, LANE), 1)
        rowp = _iota((LANE, 1), 0)
        rowl = _iota((l, 1), 0)
        d_row = sc_v[2:3, :]
        dstates = [dstate[j] for j in range(3)]
        for u in reversed(range(cps)):
            dstates = chunk(u, xbc_ref, tail_ref, sc_v, st_ref, dy_ref, dxbc_ref, dtail_ref, dsc_ref, dstates,
                            lane1, rowp, rowl, d_row)
        for j in range(3):
            dstate[j] = dstates[j]

    def chunk(u, xbc_ref, tail_ref, sc_v, st_ref, dy_ref, dxbc_ref, dtail_ref, dsc_ref, dstates, lane1, rowp, rowl, d_row):
        r = slice(u * l, (u + 1) * l)
        dstates = list(dstates)
        lane, row, tri, a_row, pre, dt, a_cs, a_t = _ssd_chunk_common(tail_ref[r, :], sc_v)
        da_col = jnp.zeros((l, LANE), F32)
        da_row = jnp.zeros((LANE, l), F32)
        dt_x = jnp.zeros((l, LANE), F32)
        dd_row = jnp.zeros((1, LANE), F32)
        db = [jnp.zeros((l, LANE), F32), jnp.zeros((l, LANE), F32)]
        dc = [jnp.zeros((l, LANE), F32), jnp.zeros((l, LANE), F32)]
        for j in range(3):
            xpair = xbc_ref[r, LANE * j:LANE * (j + 1)]
            dypair = dy_ref[r, LANE * j:LANE * (j + 1)]
            sp = st_ref[u, j]
            dsp = dstates[j]
            dxpair = jnp.zeros((l, LANE), F32)
            ds_new = jnp.zeros((LANE, LANE), F32)
            decay = jnp.zeros((LANE, 1), F32)
            for half in range(2):
                h = 2 * j + half
                g = h // 3
                hm = (lane < 64) if half == 0 else (lane >= 64)
                hrow = (rowp < 64) if half == 0 else (rowp >= 64)
                ac = _pick_col(a_cs, lane, DT_LANE + h)
                ar = _pick_row(a_t, row, DT_LANE + h)
                dtc = _pick_col(dt, lane, DT_LANE + h)
                alast = jnp.sum(jnp.where(lane1 == l - 1, ar, 0.0), axis=1, keepdims=True)
                dh = jnp.sum(jnp.where(lane1 == DT_LANE + h, d_row, 0.0), axis=1, keepdims=True)
                xm = jnp.where(hm, xpair, 0.0)
                xd = xm * dtc
                dym = jnp.where(hm, dypair, 0.0)
                bm = xbc_ref[r, D_SSD + LANE * g:D_SSD + LANE * (g + 1)]
                cm = xbc_ref[r, D_SSD + SSD_BC + LANE * g:D_SSD + SSD_BC + LANE * (g + 1)]
                lm = jnp.where(row >= lane, jnp.exp(jnp.minimum(ac - ar, 0.0)), 0.0)
                e_in = jnp.exp(ac)
                f_out = jnp.exp(alast - ac)
                e_last = jnp.exp(alast)
                m = _dot_nt(cm, bm) * lm
                y_off = jnp.where(hm, _dot_nt(cm, sp), 0.0) * e_in
                dm = _dot_nt(dym, xd)
                dxd = _dot_tn(m, dym)
                dg = dm * lm
                dye = dym * e_in
                dc[g] = dc[g] + _dot(dg, bm) + _dot(dye, sp)
                db[g] = db[g] + _dot_tn(dg, cm)
                qm = dm * m
                dac = jnp.sum(qm, axis=1, keepdims=True) + jnp.sum(dym * y_off, axis=1, keepdims=True)
                dar = -jnp.sum(qm, axis=0, keepdims=True)
                dxf = jnp.where(hm, _dot_nt(bm, dsp), 0.0)
                db[g] = db[g] + _dot(xd * f_out, dsp)
                dxd = dxd + dxf * f_out
                df = jnp.sum(dxf * xd, axis=1, keepdims=True) * f_out
                dac = dac - df
                s_last = jnp.sum(df, axis=0, keepdims=True)
                ss = jnp.sum(jnp.where(hrow, dsp * sp, 0.0), axis=1, keepdims=True)
                s_last = s_last + e_last * jnp.sum(ss, axis=0, keepdims=True)
                dac = dac + jnp.where(rowl == l - 1, s_last, 0.0)
                ds_new = ds_new + _dot_tn(dye, cm)
                decay = jnp.where(hrow, e_last, decay)
                dxpair = dxpair + dxd * dtc + dym * dh
                dt_x = dt_x + jnp.where(lane == DT_LANE + h, jnp.sum(dxd * xm, axis=1, keepdims=True), 0.0)
                dsum = jnp.sum(jnp.sum(dym * xm, axis=1, keepdims=True), axis=0, keepdims=True)
                dd_row = dd_row + jnp.where(lane1 == DT_LANE + h, dsum, 0.0)
                da_col = da_col + jnp.where(lane == DT_LANE + h, dac, 0.0)
                da_row = da_row + jnp.where(row == DT_LANE + h, dar, 0.0)
            dstates[j] = dsp * decay + ds_new
            dxbc_ref[r, LANE * j:LANE * (j + 1)] = dxpair
        for g in range(2):
            dxbc_ref[r, D_SSD + LANE * g:D_SSD + LANE * (g + 1)] = db[g]
            dxbc_ref[r, D_SSD + SSD_BC + LANE * g:D_SSD + SSD_BC + LANE * (g + 1)] = dc[g]
        dla = _dot_hi_tn(tri, da_col + da_row.T)
        ddt = dt_x + dla * a_row
        dpre = ddt * _sigmoid(pre)
        dtm = (lane >= DT_LANE) & (lane < DT_LANE + SSD_HEADS)
        dtail_ref[r, :] = jnp.where(dtm, dpre, 0.0).astype(MXU)
        dtm1 = (lane1 >= DT_LANE) & (lane1 < DT_LANE + SSD_HEADS)
        dsc_ref[0:1, :] += jnp.where(dtm1, jnp.sum(dpre, axis=0, keepdims=True), 0.0)
        dsc_ref[1:2, :] += jnp.where(dtm1, jnp.sum(dla * dt, axis=0, keepdims=True) * a_row, 0.0)
        dsc_ref[2:3, :] += dd_row
        return dstates

    rev = lambda c: nc // cps - 1 - c
    return _call_after(
        dep, body, (xbc, proj, sc, states, dy), name="ssd_bwd", grid=(nc // cps,),
        in_specs=[pl.BlockSpec((cps * l, N_XBC), lambda c: (rev(c), 0)),
                  pl.BlockSpec((cps * l, LANE), lambda c: (rev(c), O_TAIL // LANE)), _full((8, LANE)),
                  pl.BlockSpec((cps, 3, LANE, LANE), lambda c: (rev(c), 0, 0, 0)),
                  pl.BlockSpec((cps * l, D_SSD), lambda c: (rev(c), 0))],
        out_specs=[pl.BlockSpec((cps * l, N_XBC), lambda c: (rev(c), 0)), pl.BlockSpec((cps * l, LANE), lambda c: (rev(c), 0)),
                   _full((8, LANE))],
        out_shape=[_sds((s, N_XBC)), _sds((s, LANE), MXU), _sds((8, LANE))],
        scratch_shapes=[pltpu.VMEM((3, LANE, LANE), F32)],
        compiler_params=_params(("arbitrary",)),
    )


def _sconv_bwd(proj, w, b, dxbc, dep=None):
    s = proj.shape[0]

    def body(u_ref, w_ref, b_ref, d_ref, du_ref, dw_ref, db_ref):
        u = u_ref[...]
        wv = w_ref[...]
        dpre = d_ref[...] * _dsilu(_sconv_pre(u, wv, b_ref[...]))
        ahead = [_shift_up(dpre, j) for j in range(4)]
        du_ref[...] = (wv[3:4, :] * ahead[0] + wv[2:3, :] * ahead[1] + wv[1:2, :] * ahead[2]
                       + wv[0:1, :] * ahead[3]).astype(MXU)
        for k in range(4):
            dw_ref[k:k + 1, :] = jnp.sum(ahead[3 - k] * u, axis=0, keepdims=True)
        db_ref[...] = jnp.sum(dpre, axis=0, keepdims=True)

    blk = pl.BlockSpec((s, LANE), lambda j: (0, j))
    return _call_after(
        dep, body, (proj, w, b, dxbc), name="sconv_bwd", grid=(N_XBC // LANE,),
        in_specs=[_col(s, O_XBC), pl.BlockSpec((4, LANE), lambda j: (0, j)), pl.BlockSpec((1, LANE), lambda j: (0, j)), blk],
        out_specs=[blk, pl.BlockSpec((4, LANE), lambda j: (0, j)), pl.BlockSpec((1, LANE), lambda j: (0, j))],
        out_shape=[_sds((s, N_XBC), MXU), _sds((4, N_XBC)), _sds((1, N_XBC))],
        compiler_params=_params(("parallel",)),
    )


def _conva_bwd(proj, w, dya, dep=None):
    s = proj.shape[0]

    def body(h_ref, b_ref, c_ref, z_ref, w_ref, d_ref, da_ref, dw_ref):
        ah, ab, acv, az = h_ref[...], b_ref[...], c_ref[...], z_ref[...]
        wv = w_ref[...]
        u = acv * ah
        cv = wv[2:3, :] * u + wv[1:2, :] * _shift_down(u, 1) + wv[0:1, :] * _shift_down(u, 2)
        dy = d_ref[...]
        sz = _silu(az)
        da_ref[1] = (dy * cv * sz).astype(MXU)
        da_ref[3] = (dy * ab * cv * _dsilu(az)).astype(MXU)
        dcv = dy * ab * sz
        ahead = [_shift_up(dcv, j) for j in range(3)]
        du = wv[2:3, :] * ahead[0] + wv[1:2, :] * ahead[1] + wv[0:1, :] * ahead[2]
        da_ref[0] = (du * acv).astype(MXU)
        da_ref[2] = (du * ah).astype(MXU)
        for k in range(3):
            dw_ref[k:k + 1, :] = jnp.sum(ahead[2 - k] * u, axis=0, keepdims=True)

    return _call_after(
        dep, body, (proj, proj, proj, proj, w, dya), name="conva_bwd", grid=(D_CONV_A // LANE,),
        in_specs=[_col(s, O_AH), _col(s, O_AB), _col(s, O_AC), _col(s, O_AZ), pl.BlockSpec((3, LANE), lambda j: (0, j)),
                  pl.BlockSpec((s, LANE), lambda j: (0, j))],
        out_specs=[pl.BlockSpec((4, s, LANE), lambda j: (0, 0, j)), pl.BlockSpec((3, LANE), lambda j: (0, j))],
        out_shape=[_sds((4, s, D_CONV_A), MXU), _sds((3, D_CONV_A))],
        compiler_params=_params(("parallel",)),
    )


def _mla_prep_bwd(dq, dk, dv, proj, qn, kvn, rq, rkv, gq, gkv, wq, wkv, cos, sin):
    s = proj.shape[0]
    ts = _tile(s)
    nh = MLA_HEADS

    def body(dq_ref, dk_ref, dv_ref, cqa_ref, ckv_ref, qn_ref, kvn_ref, rq_ref, rkv_ref, gq_ref, gkv_ref,
             wq_ref, wkv_ref, cos_ref, sin_ref, dcqa_ref, dckv_ref, dtail_ref, dwq_ref, dwkv_ref, dgq_ref, dgkv_ref):
        @pl.when(pl.program_id(0) == 0)
        def _():
            dwq_ref[...] = jnp.zeros_like(dwq_ref)
            dwkv_ref[...] = jnp.zeros_like(dwkv_ref)
            dgq_ref[...] = jnp.zeros_like(dgq_ref)
            dgkv_ref[...] = jnp.zeros_like(dgkv_ref)

        cosv = cos_ref[...]
        sinv = sin_ref[...]
        lane = _iota((ts, LANE), 1)
        rope_lanes = (lane >= ROPE_LANE) & (lane < ROPE_LANE + QK_ROPE)

        def unrope(gr):
            return gr * cosv + _rope_swap(gr * sinv)

        dqs, dks, dvs = [], [], []
        dkr = jnp.zeros((ts, LANE), F32)
        for h in range(nh):
            dqs.append(unrope(dq_ref[h] * ATT_SCALE).astype(MXU))
            dkh = dk_ref[h]
            dks.append(jnp.where(lane < QK_NOPE, dkh, 0.0).astype(MXU))
            dkr = dkr + jnp.where(rope_lanes, dkh, 0.0)
            dvs.append(dv_ref[h].astype(MXU))
        dtail_ref[...] = pltpu.roll(jnp.where(rope_lanes, unrope(dkr), 0.0), ROPE_LANE, 1).astype(MXU)
        dq_all = jnp.concatenate(dqs, axis=1)
        dkv_all = jnp.concatenate(dks + dvs, axis=1)
        dwq_ref[...] += _dot_tn(dq_all, qn_ref[...])
        dwkv_ref[...] += _dot_tn(dkv_all, kvn_ref[...])
        dcqa, dgq = _rms_bwd(_dot(dq_all, wq_ref[...]), cqa_ref[...], rq_ref[...], gq_ref[...])
        dckv, dgkv = _rms_bwd(_dot(dkv_all, wkv_ref[...]), ckv_ref[...], rkv_ref[...], gkv_ref[...])
        dcqa_ref[...] = dcqa.astype(MXU)
        dckv_ref[...] = dckv.astype(MXU)
        dgq_ref[...] += dgq
        dgkv_ref[...] += dgkv

    head = pl.BlockSpec((nh, ts, LANE), lambda i: (0, i, 0))
    return pl.pallas_call(
        body, name="mla_prep_bwd", grid=(s // ts,),
        in_specs=[head, head, head,
                  pl.BlockSpec((ts, Q_LORA), lambda i: (i, O_CQA // Q_LORA)),
                  pl.BlockSpec((ts, KV_LORA), lambda i: (i, O_CKV // KV_LORA)),
                  _row(ts, Q_LORA), _row(ts, KV_LORA), _row(ts, 1), _row(ts, 1),
                  _full((1, Q_LORA)), _full((1, KV_LORA)), _full((nh * LANE, Q_LORA)), _full((2 * nh * LANE, KV_LORA)),
                  _row(ts, LANE), _row(ts, LANE)],
        out_specs=[_row(ts, Q_LORA), _row(ts, KV_LORA), _row(ts, LANE), _full((nh * LANE, Q_LORA)),
                   _full((2 * nh * LANE, KV_LORA)), _full((1, Q_LORA)), _full((1, KV_LORA))],
        out_shape=[_sds((s, Q_LORA), MXU), _sds((s, KV_LORA), MXU), _sds((s, LANE), MXU), _sds((nh * LANE, Q_LORA)),
                   _sds((2 * nh * LANE, KV_LORA)), _sds((1, Q_LORA)), _sds((1, KV_LORA))],
        compiler_params=_params(("arbitrary",)),
    )(dq, dk, dv, proj, proj, qn, kvn, rq, rkv, gq, gkv, wq, wkv, cos, sin)


def _inproj_bwd(da4, dsz, dxbc_in, dcqa, dckv, dcz, dtail_a, dtail_b, w, x, rstd, g, dout, dep=None):
    s = x.shape[0]
    ts = _tile(s)

    def body(da_ref, dsz_ref, dxbc_ref, dcqa_ref, dckv_ref, dcz_ref, dta_ref, dtb_ref, w_ref, x_ref, r_ref, g_ref, dout_ref,
             dproj_ref, dx_ref, dg_ref):
        @pl.when(pl.program_id(0) == 0)
        def _():
            dg_ref[...] = jnp.zeros_like(dg_ref)

        dproj = jnp.concatenate(
            [da_ref[0], da_ref[1], da_ref[2], da_ref[3], dxbc_ref[...], dsz_ref[...], dcqa_ref[...], dckv_ref[...],
             dcz_ref[...], dta_ref[...] + dtb_ref[...]], axis=1)
        dproj_ref[...] = dproj
        dh = _dot_nt(dproj, w_ref[...])
        dx, dg = _rms_bwd(dh, x_ref[...], r_ref[...], g_ref[...])
        dx_ref[...] = dout_ref[...] + dx
        dg_ref[...] += dg

    return _call_after(
        dep, body, (da4, dsz, dxbc_in, dcqa, dckv, dcz, dtail_a, dtail_b, w, x, rstd, g, dout), name="inproj_bwd", grid=(s // ts,),
        in_specs=[pl.BlockSpec((4, ts, D_CONV_A), lambda i: (0, i, 0)), _row(ts, D_SSD), _row(ts, N_XBC), _row(ts, Q_LORA),
                  _row(ts, KV_LORA), _row(ts, D_MLA), _row(ts, LANE), _row(ts, LANE), _full((D_MODEL, NCOL)),
                  _row(ts, D_MODEL), _row(ts, 1), _full((1, D_MODEL)), _row(ts, D_MODEL)],
        out_specs=[_row(ts, NCOL), _row(ts, D_MODEL), _full((1, D_MODEL))],
        out_shape=[_sds((s, NCOL), MXU), _sds((s, D_MODEL)), _sds((1, D_MODEL))],
        compiler_params=_params(("arbitrary",)),
    )


DWIN_BLOCK = 640


def _dwin(h, dproj, dep=None):
    s = h.shape[0]

    def body(h_ref, d_ref, o_ref):
        o_ref[...] = _dot_tn(h_ref[...], d_ref[...])

    return _call_after(
        dep, body, (h, dproj), name="dwin", grid=(NCOL // DWIN_BLOCK,),
        in_specs=[_full((s, D_MODEL)), pl.BlockSpec((s, DWIN_BLOCK), lambda j: (0, j))],
        out_specs=pl.BlockSpec((D_MODEL, DWIN_BLOCK), lambda j: (0, j)),
        out_shape=_sds((D_MODEL, NCOL)),
        compiler_params=_params(("parallel",)),
    )


def _adamw(ws, gs, ms, vs, whole):
    n = len(ws)
    bc1 = 1.0 - ADAM_B1 ** ADAM_STEP
    bc2 = 1.0 - ADAM_B2 ** ADAM_STEP

    def body(*refs):
        ins, outs = refs[:4 * n], refs[4 * n:]
        for a in range(n):
            w_ref, g_ref, m_ref, v_ref = ins[a], ins[n + a], ins[2 * n + a], ins[3 * n + a]
            gv = g_ref[...]
            mn = ADAM_B1 * m_ref[...] + (1.0 - ADAM_B1) * gv
            vn = ADAM_B2 * v_ref[...] + (1.0 - ADAM_B2) * (gv * gv)
            outs[n + a][...] = mn
            outs[2 * n + a][...] = vn
            outs[a][...] = -ADAM_LR * ((mn / bc1) / (jnp.sqrt(vn / bc2) + ADAM_EPS) + ADAM_WD * w_ref[...])

    if whole:
        grid, blks = (1,), [pl.BlockSpec(w.shape, lambda i, _n=w.ndim: (0,) * _n) for w in ws]
    else:
        grid = (ws[0].shape[0], 2)
        blks = [pl.BlockSpec((1, w.shape[1] // 2, w.shape[2]), lambda i, k: (i, k, 0)) for w in ws]
    out = pl.pallas_call(
        body, name="adamw", grid=grid,
        in_specs=blks * 4, out_specs=blks * 3, out_shape=[_sds(w.shape) for w in ws] * 3,
        compiler_params=_params(("parallel",) * len(grid)),
    )(*ws, *gs, *ms, *vs)
    return [(out[a], out[n + a], out[2 * n + a]) for a in range(n)]


ADAMW_COLS_BLOCK = 512


def _adamw_cols(w_t, gs, m_t, v_t):
    cols, nl, rows = w_t.shape
    bc1 = 1.0 - ADAM_B1 ** ADAM_STEP
    bc2 = 1.0 - ADAM_B2 ** ADAM_STEP

    def body(w_ref, m_ref, v_ref, *rest):
        g_refs, (go_ref, d_ref, mo_ref, vo_ref), g_blk = rest[:nl], rest[nl:nl + 4], rest[-1]
        for l in range(nl):
            g_blk[:, l, :] = g_refs[l][...].T
        gv = g_blk[...]
        mn = ADAM_B1 * m_ref[...] + (1.0 - ADAM_B1) * gv
        vn = ADAM_B2 * v_ref[...] + (1.0 - ADAM_B2) * (gv * gv)
        go_ref[...] = gv
        mo_ref[...] = mn
        vo_ref[...] = vn
        d_ref[...] = -ADAM_LR * ((mn / bc1) / (jnp.sqrt(vn / bc2) + ADAM_EPS) + ADAM_WD * w_ref[...])

    tc = ADAMW_COLS_BLOCK
    blk = pl.BlockSpec((tc, nl, rows), lambda j: (j, 0, 0))
    gblk = pl.BlockSpec((rows, tc), lambda j: (0, j))
    return pl.pallas_call(
        body, name="adamw_cols", grid=(pl.cdiv(cols, tc),),
        in_specs=[blk] * 3 + [gblk] * nl, out_specs=[blk] * 4, out_shape=[_sds(w_t.shape)] * 4,
        scratch_shapes=[pltpu.VMEM((tc, nl, rows), F32)],
        compiler_params=_params(("parallel",)),
    )(w_t, m_t, v_t, *gs)


COL_MOVES = ((0, 0, 1024), (1024, O_SZ, 384), (1408, O_XBC, 896), (2304, O_TAIL + DT_LANE, 6), (2310, O_CQA, 256),
             (2566, O_CKV, 128), (2694, O_TAIL, 32), (2726, O_CZ, 384))


def _move_cols(w, moves, width):
    out = None
    for src, dst, n in moves:
        piece = jnp.pad(w[..., src:src + n], [(0, 0)] * (w.ndim - 1) + [(dst, width - dst - n)])
        out = piece if out is None else out + piece
    return out


def _perm_cols(w):
    return _move_cols(w, COL_MOVES, NCOL)


def _unperm_cols(g):
    return _move_cols(g, [(dst, src, n) for src, dst, n in COL_MOVES], IN_COLS)


def _wq_layout(wt):
    return jnp.pad(wt.reshape(MLA_HEADS, QK_NOPE + QK_ROPE, Q_LORA), ((0, 0), (0, 32), (0, 0))).reshape(MLA_HEADS * LANE, Q_LORA)


def _wq_unlayout(g):
    return g.reshape(MLA_HEADS, LANE, Q_LORA)[:, :QK_NOPE + QK_ROPE].reshape(MLA_HEADS * (QK_NOPE + QK_ROPE), Q_LORA)


def _wkv_layout(wt):
    t = wt.reshape(MLA_HEADS, 2, 64, KV_LORA).transpose(1, 0, 2, 3)
    return jnp.pad(t, ((0, 0), (0, 0), (0, 64), (0, 0))).reshape(2 * MLA_HEADS * LANE, KV_LORA)


def _wkv_unlayout(g):
    t = g.reshape(2, MLA_HEADS, LANE, KV_LORA)[:, :, :64]
    return t.transpose(1, 0, 2, 3).reshape(MLA_HEADS * LANE, KV_LORA)


def _rope_tables(positions):
    inv_freq = ROPE_BASE ** (-jnp.arange(0, QK_ROPE, 2, dtype=F32) / QK_ROPE)
    ang = positions.astype(F32)[:, None] * inv_freq
    cos, sin = jnp.cos(ang), jnp.sin(ang)
    s = positions.shape[0]
    one, zero = jnp.ones((s, ROPE_LANE), F32), jnp.zeros((s, ROPE_LANE), F32)
    cos_t = jnp.concatenate([one, cos, cos, one[:, :32]], axis=1)
    sin_t = jnp.concatenate([zero, -sin, sin, zero[:, :32]], axis=1)
    return cos_t, sin_t


def _ssd_scalars(dt_bias, a_log, d_skip):
    return jnp.pad(jnp.stack([dt_bias, a_log, d_skip]), ((0, 5), (DT_LANE, LANE - DT_LANE - SSD_HEADS)))


def _layer_fwd(x, lw, cos, sin, dep=None, late=None, head=None):
    proj, h, rstd = _inproj_fwd(x, lw["norm_g"], lw["w_in"], dep)
    ya = _conva_fwd(proj, lw["conv_a_w"])
    xbc = _sconv_fwd(proj, lw["ssd_conv_w"], lw["ssd_conv_b"])
    y_ssd, states = _ssd_fwd(xbc, proj, lw["sc"])
    if late is not None:
        lw = {**lw, **late(ya, y_ssd)}
    q, k, v, qn, kvn, rq, rkv = _mla_prep_fwd(proj, lw["gq"], lw["gkv"], lw["wq"], lw["wkv"], cos, sin)
    o, lse = _attn_fwd(q, k, v)
    if head is None:
        x_out, y = _outproj_fwd(x, proj, ya, y_ssd, o, lw["g_ssd"], lw["w_out"])
    else:
        *x_out, y = _outproj_loss(x, proj, ya, y_ssd, o, lw["g_ssd"], lw["w_out"], *head)
    saved = dict(x=x, proj=proj, h=h, rstd=rstd, xbc=xbc, y_ssd=y_ssd, states=states, q=q, k=k, v=v, qn=qn, kvn=kvn,
                 rq=rq, rkv=rkv, o=o, lse=lse, y=y)
    return x_out, saved, lw


def _layer_bwd(dout, lw, sv, cos, sin, rs=None, early_grads=None):
    tok = lambda: None if rs is None else rs["h"]["token"]
    dya, dys, dsz, d_o, dcz, dg_ssd, dw_out = _outproj_bwd(dout, sv["y"], lw["w_out"], sv["proj"], sv["y_ssd"], sv["o"],
                                                            lw["g_ssd"], tok())
    if rs is not None:
        rs = _rs_add_mine(rs, [dya])
    dq, dk, dv = _attn_bwd(sv["q"], sv["k"], sv["v"], sv["o"], d_o, sv["lse"], tok())
    dxbc, dtail_s, dsc = _ssd_bwd(sv["xbc"], sv["proj"], lw["sc"], sv["states"], dys, tok())
    da4, dw_conva = _conva_bwd(sv["proj"], lw["conv_a_w"], dya, tok())
    dcqa, dckv, dtail_m, dwq, dwkv, dgq, dgkv = _mla_prep_bwd(
        dq, dk, dv, sv["proj"], sv["qn"], sv["kvn"], sv["rq"], sv["rkv"], lw["gq"], lw["gkv"], lw["wq"], lw["wkv"], cos, sin)
    early = None
    if rs is not None and early_grads is not None:
        rs, early = _rs_add_chips(rs, [dq, dxbc, da4, dcqa], also=(early_grads(dw_out, dwq, dwkv), "0l"))
    elif rs is not None:
        rs = _rs_add_chips(rs, [dq, dxbc, da4, dcqa])
    du, dw_sconv, db_sconv = _sconv_bwd(sv["proj"], lw["ssd_conv_w"], lw["ssd_conv_b"], dxbc, tok())
    etok = lambda: None if early is None else early["h"]["token"]
    dproj, dx, dg = _inproj_bwd(da4, dsz, du, dcqa, dckv, dcz, dtail_s, dtail_m, lw["w_in"], sv["x"], sv["rstd"],
                                lw["norm_g"], dout, etok())
    reduced = None if rs is None else _rs_end(rs, [du, dcqa, dx])
    if early is not None:
        early = _rs_add_mine(early, [dx])
    dw_in = _dwin(sv["h"], dproj, etok())
    own = None
    if early is not None:
        early, own = _rs_add_chips(early, [dw_in], also=([dw_in], "own"))
    grads = dict(norm_g=dg, w_in=dw_in, conv_a_w=dw_conva, ssd_conv_w=dw_sconv, ssd_conv_b=db_sconv, sc=dsc,
                 g_ssd=dg_ssd, gq=dgq, wq=dwq, gkv=dgkv, wkv=dwkv, w_out=dw_out)
    return dx, grads, reduced, early, own


ANY = pl.BlockSpec(memory_space=pl.ANY)
N_CHIPS = 4
N_DEV = 8


def _place():
    return lax.axis_index("x"), lax.axis_index("y"), lax.axis_index("c")


HBM_SPEC = pl.BlockSpec(memory_space=pltpu.HBM)
SEM_SPEC = pl.BlockSpec(memory_space=pltpu.SEMAPHORE)
PAYLOAD = jnp.bfloat16


def _hbm(a):
    return pltpu.with_memory_space_constraint(a, pltpu.HBM)


def _run_plan(plan, srcs, lands, send_sems, recv_sems, start, wait):
    copies = plan(srcs, lands)
    if start:
        for i, (src, dst, _, to) in enumerate(copies):
            if to is None:
                pltpu.make_async_copy(src, dst, recv_sems.at[i]).start()
                continue
            pltpu.make_async_remote_copy(src_ref=src, dst_ref=dst, send_sem=send_sems.at[i], recv_sem=recv_sems.at[i],
                                         device_id=to, device_id_type=MESH_T).start()
    if wait:
        for i, (src, _, arrives, to) in enumerate(copies):
            if to is None:
                pltpu.make_async_copy(src, arrives, recv_sems.at[i]).wait()
                continue
            cp = pltpu.make_async_remote_copy(src_ref=src, dst_ref=arrives, send_sem=send_sems.at[i],
                                              recv_sem=recv_sems.at[i], device_id=to, device_id_type=MESH_T)
            cp.wait_send()
            cp.wait_recv()


def _exchange_start_many(name, groups, deps):
    g = len(groups)
    sizes = [(len(srcs), len(shapes)) for _, _, srcs, shapes in groups]
    n_arr = sum(ns + nl for ns, nl in sizes)
    n_in = n_arr + len(deps)

    def body(*refs):
        at = 0
        for k, ((plan, _, _, _), (ns, nl)) in enumerate(zip(groups, sizes)):
            _run_plan(plan, refs[at:at + ns], refs[at + ns:at + ns + nl], refs[n_in + 2 * k], refs[n_in + 2 * k + 1], True, False)
            at += ns + nl
        refs[-1][...] = jnp.zeros_like(refs[-1])

    arrs, thru, sems = [], [], []
    for _, n_copies, srcs, shapes in groups:
        arrs += [_hbm(a) for a in srcs] + [_hbm(lax.empty(a.shape, a.dtype)) for a in shapes]
        thru += [pltpu.HBM(a.shape, a.dtype) for a in list(srcs) + list(shapes)]
        sems += [pltpu.SemaphoreType.DMA((n_copies,))] * 2
    outs = pl.pallas_call(
        body, name=name,
        out_shape=(*sems, *thru, _sds((8, LANE))),
        in_specs=[HBM_SPEC] * n_arr + [ANY] * len(deps),
        out_specs=(*[SEM_SPEC] * (2 * g), *[HBM_SPEC] * n_arr, pl.BlockSpec(memory_space=pltpu.VMEM)),
        input_output_aliases={i: 2 * g + i for i in range(n_arr)},
        compiler_params=pltpu.CompilerParams(has_side_effects=pltpu.SideEffectType.DATAFLOW_SIDE_EFFECTING),
    )(*arrs, *deps)
    res, at = [], 2 * g
    for k, (ns, nl) in enumerate(sizes):
        res.append(((outs[2 * k], outs[2 * k + 1]), list(outs[at:at + ns]), list(outs[at + ns:at + ns + nl])))
        at += ns + nl
    return res, outs[-1]


def _exchange_start(name, plan, n_copies, srcs, land_shapes, deps):
    (one,), token = _exchange_start_many(name, [(plan, n_copies, srcs, land_shapes)], deps)
    return (*one, token)


def _exchange_wait(name, plan, sems, srcs, lands, after):
    ns, nl = len(srcs), len(lands)

    def body(*refs):
        _run_plan(plan, refs[:ns], refs[ns:ns + nl], refs[ns + nl], refs[ns + nl + 1], False, True)

    outs = pl.pallas_call(
        body, name=name,
        out_shape=[pltpu.HBM(a.shape, a.dtype) for a in list(srcs) + list(lands)],
        in_specs=[HBM_SPEC] * (ns + nl) + [SEM_SPEC, SEM_SPEC] + [ANY] * len(after), out_specs=[HBM_SPEC] * (ns + nl),
        input_output_aliases={i: i for i in range(ns + nl)},
        compiler_params=pltpu.CompilerParams(has_side_effects=pltpu.SideEffectType.DATAFLOW_SIDE_EFFECTING),
    )(*srcs, *lands, sems[0], sems[1], *after)
    return list(outs[:ns]), list(outs[ns:])


def _xchg_begin(name, plan, n_copies, srcs, land_shapes, deps=()):
    sems, srcs_t, lands_t, token = _exchange_start(name + "_start", plan, n_copies, srcs, land_shapes, list(deps))
    return dict(name=name, plan=plan, sems=sems, srcs=srcs_t, lands=lands_t, token=token)


def _xchg_begin_many(name, specs, deps=()):
    res, token = _exchange_start_many(name + "_start", [s[1:] for s in specs], list(deps))
    return [dict(name=s[0], plan=s[1], sems=sems, srcs=srcs_t, lands=lands_t, token=token)
            for s, (sems, srcs_t, lands_t) in zip(specs, res)]


def _xchg_end(h, after):
    return _exchange_wait(h["name"] + "_wait", h["plan"], h["sems"], h["srcs"], h["lands"], after)


def _other_chips():
    x, y, c = _place()
    return [(1 - x, y), (x, 1 - y), (1 - x, 1 - y)]


def _gather_plan(srcs, lands):
    x, y, c = _place()
    me = 2 * x + y
    return [(srcs[a], lands[a].at[me], lands[a].at[2 * cx + cy], (cx, cy, c))
            for (cx, cy) in _other_chips() for a in range(len(srcs))]


def _gather_spec(shards, tag):
    return (f"gather_{tag}", _gather_plan, 3 * len(shards), shards, [_sds((N_CHIPS,) + a.shape, a.dtype) for a in shards])


def _gather_end(h, after):
    shards, lands = _xchg_end(h, after)
    me = 2 * lax.axis_index("x") + lax.axis_index("y")
    return [lax.dynamic_update_index_in_dim(g, s, me, 0) for g, s in zip(lands, shards)]


def _gather_half_plan(srcs, lands):
    x, y, c = _place()
    me = 2 * x + y
    out = []
    for (cx, cy) in _other_chips():
        out.append((srcs[0].at[c], lands[0].at[me, c], lands[0].at[2 * cx + cy, c], (cx, cy, c)))
        out += [(srcs[a], lands[a].at[me], lands[a].at[2 * cx + cy], (cx, cy, c)) for a in range(1, len(srcs))]
    return out


def _forward_plan(bufs, _):
    x, y, c = _place()
    return [(bufs[0].at[2 * cx + cy, c], bufs[0].at[2 * cx + cy, c], bufs[0].at[2 * cx + cy, 1 - c], (x, y, 1 - c))
            for (cx, cy) in _other_chips()]


def _swap_plan(srcs, lands):
    x, y, c = _place()
    return [(srcs[a].at[:, 1 - c], lands[a], lands[a], (x, y, 1 - c)) for a in range(len(srcs))]


def _chips_plan(srcs, lands):
    x, y, c = _place()
    me = 2 * x + y
    return [(srcs[a].at[2 * cx + cy], lands[a].at[me], lands[a].at[2 * cx + cy], (cx, cy, c))
            for (cx, cy) in _other_chips() for a in range(len(srcs))]


def _share_plan(srcs, lands):
    x, y, c = _place()
    return ([(srcs[a], lands[a].at[c], lands[a].at[1 - c], (x, y, 1 - c)) for a in range(len(srcs))]
            + [(srcs[a], lands[a].at[c], lands[a].at[c], None) for a in range(len(srcs))])


def _allreduce_small(slab, dep=None):
    r = slab.shape[0]

    def body(s_ref, o_ref, gath, send_sems, recv_sems):
        x, y, c = _place()
        me = 4 * x + 2 * y + c
        gath[me] = s_ref[...]
        cps = []
        for rel in range(1, N_DEV):
            px = 1 - x if rel & 4 else x
            py = 1 - y if rel & 2 else y
            pc = 1 - c if rel & 1 else c
            cp = pltpu.make_async_remote_copy(src_ref=s_ref, dst_ref=gath.at[me], send_sem=send_sems.at[rel - 1],
                                              recv_sem=recv_sems.at[rel - 1], device_id=(px, py, pc), device_id_type=MESH_T)
            cp.start()
            cps.append(cp)
        for cp in cps:
            cp.wait()
        acc = gath[0]
        for d in range(1, N_DEV):
            acc = acc + gath[d]
        o_ref[...] = acc

    vm = pl.BlockSpec(memory_space=pltpu.VMEM)
    return _call_after(
        dep, body, (slab,), name="allreduce_small", in_specs=[vm], out_specs=vm, out_shape=_sds((r, LANE)),
        scratch_shapes=[pltpu.VMEM((N_DEV, r, LANE), F32), pltpu.SemaphoreType.DMA((N_DEV - 1,)),
                        pltpu.SemaphoreType.DMA((N_DEV - 1,))],
    )


def _add_mine(g4s, recvs, half):
    n = len(g4s)

    def body(h_ref, *refs):
        for g_ref, r_ref, o_ref in zip(refs[:n], refs[n:2 * n], refs[2 * n:]):
            o_ref[0] = (g_ref[0, 0] + r_ref[0]).astype(o_ref.dtype)

    dims = [g.shape[2:] for g in g4s]
    return pl.pallas_call(
        body, name="add_mine",
        grid_spec=pltpu.PrefetchScalarGridSpec(
            num_scalar_prefetch=1, grid=(N_CHIPS,),
            in_specs=[pl.BlockSpec((1, 1) + d, lambda j, h: (j, h[0], 0, 0)) for d in dims]
            + [pl.BlockSpec((1,) + d, lambda j, h: (j, 0, 0)) for d in dims],
            out_specs=[pl.BlockSpec((1,) + d, lambda j, h: (j, 0, 0)) for d in dims]),
        out_shape=[_sds((N_CHIPS,) + d, PAYLOAD) for d in dims],
        compiler_params=_params(("parallel",)),
    )(half, *g4s, *recvs)


def _add_chips(es, ps, me):
    n = len(es)

    def body(m_ref, *refs):
        for e_ref, p_ref, o_ref in zip(refs[:n], refs[n:2 * n], refs[2 * n:]):
            own = p_ref[0].astype(F32)
            acc = None
            for s in range(N_CHIPS):
                t = jnp.where(m_ref[0] == s, own, e_ref[s].astype(F32))
                acc = t if acc is None else acc + t
            o_ref[...] = acc

    dims = [e.shape[1:] for e in es]
    return pl.pallas_call(
        body, name="add_chips",
        grid_spec=pltpu.PrefetchScalarGridSpec(
            num_scalar_prefetch=1, grid=(1,),
            in_specs=[pl.BlockSpec((N_CHIPS,) + d, lambda i, m: (0, 0, 0)) for d in dims]
            + [pl.BlockSpec((1,) + d, lambda i, m: (m[0], 0, 0)) for d in dims],
            out_specs=[pl.BlockSpec(d, lambda i, m: (0, 0)) for d in dims]),
        out_shape=[_sds(d) for d in dims],
        compiler_params=_params(("arbitrary",)),
    )(me, *es, *ps)


def _rs_begin(gs, tag, deps=()):
    return dict(h=_xchg_begin(*_swap_spec(gs, tag), deps), tag=tag, shapes=[g.shape for g in gs])


def _swap_spec(gs, tag):
    g4 = [g.reshape(N_CHIPS, 2, g.shape[0] // (2 * N_CHIPS), g.shape[1]) for g in gs]
    return (f"rs_swap_{tag}", _swap_plan, len(gs), g4, [_sds((N_CHIPS,) + g.shape[2:]) for g in g4])


def _rs_add_mine(st, after):
    g4, recv = _xchg_end(st["h"], after)
    half = jnp.reshape(lax.axis_index("c"), (1,)).astype(jnp.int32)
    ps = _add_mine(g4, recv, half)
    st["h"] = _xchg_begin(f"rs_chips_{st['tag']}", _chips_plan, 3 * len(ps), ps, [_sds(p.shape, p.dtype) for p in ps])
    return st


def _rs_add_chips(st, after, also=None):
    ps, es = _xchg_end(st["h"], after)
    me = jnp.reshape(2 * lax.axis_index("x") + lax.axis_index("y"), (1,)).astype(jnp.int32)
    fs = _add_chips(es, ps, me)
    share = (f"rs_share_{st['tag']}", _share_plan, 2 * len(fs), fs, [_sds((2,) + f.shape) for f in fs])
    if also is None:
        st["h"] = _xchg_begin(*share)
        return st
    gs, tag = also
    st["h"], h = _xchg_begin_many(f"rs_share_{st['tag']}_swap_{tag}", [share, _swap_spec(gs, tag)])
    return st, dict(h=h, tag=tag, shapes=[g.shape for g in gs])


def _rs_end(st, after):
    _, ss = _xchg_end(st["h"], after)
    return [s.reshape(shp[0] // N_CHIPS, shp[1]) for s, shp in zip(ss, st["shapes"])]


WEIGHTS = ["norm_g", "w_in", "conv_a_w", "ssd_conv_w", "ssd_conv_b", "ssd_dt_bias", "ssd_a_log", "ssd_d", "ssd_norm_g",
           "mla_q_norm_g", "w_qb", "mla_kv_norm_g", "w_kvb", "w_out", "final_norm_g"]
BIG = ["w_in", "w_qb", "w_kvb", "w_out"]
SLAB_ROWS = 128


def _to_slab(parts, rows):
    flat = jnp.concatenate([p.reshape(-1) for p in parts])
    return jnp.pad(flat, (0, rows * LANE - flat.shape[0])).reshape(rows, LANE)


def _from_slab(slab, shapes):
    flat = slab.reshape(-1)
    out, off = [], 0
    for shp in shapes:
        n = int(np.prod(shp))
        out.append(flat[off:off + n].reshape(shp))
        off += n
    return out


def kernel(x, positions, norm_g, w_in, conv_a_w, ssd_conv_w, ssd_conv_b, ssd_dt_bias, ssd_a_log, ssd_d, ssd_norm_g, mla_q_norm_g, w_qb, mla_kv_norm_g, w_kvb, w_out, final_norm_g, loss_target, m_norm_g, m_w_in, m_conv_a_w, m_ssd_conv_w, m_ssd_conv_b, m_ssd_dt_bias, m_ssd_a_log, m_ssd_d, m_ssd_norm_g, m_mla_q_norm_g, m_w_qb, m_mla_kv_norm_g, m_w_kvb, m_w_out, m_final_norm_g, v_norm_g, v_w_in, v_conv_a_w, v_ssd_conv_w, v_ssd_conv_b, v_ssd_dt_bias, v_ssd_a_log, v_ssd_d, v_ssd_norm_g, v_mla_q_norm_g, v_w_qb, v_mla_kv_norm_g, v_w_kvb, v_w_out, v_final_norm_g):
    w = dict(norm_g=norm_g, w_in=w_in, conv_a_w=conv_a_w, ssd_conv_w=ssd_conv_w, ssd_conv_b=ssd_conv_b,
             ssd_dt_bias=ssd_dt_bias, ssd_a_log=ssd_a_log, ssd_d=ssd_d, ssd_norm_g=ssd_norm_g, mla_q_norm_g=mla_q_norm_g,
             w_qb=w_qb, mla_kv_norm_g=mla_kv_norm_g, w_kvb=w_kvb, w_out=w_out, final_norm_g=final_norm_g)
    mom = dict(norm_g=m_norm_g, w_in=m_w_in, conv_a_w=m_conv_a_w, ssd_conv_w=m_ssd_conv_w, ssd_conv_b=m_ssd_conv_b,
               ssd_dt_bias=m_ssd_dt_bias, ssd_a_log=m_ssd_a_log, ssd_d=m_ssd_d, ssd_norm_g=m_ssd_norm_g,
               mla_q_norm_g=m_mla_q_norm_g, w_qb=m_w_qb, mla_kv_norm_g=m_mla_kv_norm_g, w_kvb=m_w_kvb, w_out=m_w_out,
               final_norm_g=m_final_norm_g)
    var = dict(norm_g=v_norm_g, w_in=v_w_in, conv_a_w=v_conv_a_w, ssd_conv_w=v_ssd_conv_w, ssd_conv_b=v_ssd_conv_b,
               ssd_dt_bias=v_ssd_dt_bias, ssd_a_log=v_ssd_a_log, ssd_d=v_ssd_d, ssd_norm_g=v_ssd_norm_g,
               mla_q_norm_g=v_mla_q_norm_g, w_qb=v_w_qb, mla_kv_norm_g=v_mla_kv_norm_g, w_kvb=v_w_kvb, w_out=v_w_out,
               final_norm_g=v_final_norm_g)
    chip = 2 * lax.axis_index("x") + lax.axis_index("y")

    def early_shard(l, zero):
        pack = jnp.pad(conv_a_w[l], ((0, 5), (0, 192))) + jnp.pad(ssd_conv_w[l], ((3, 1), (0, 32)))
        return [(_perm_cols(w_in[l]) + zero).astype(MXU), pack + zero]

    def late_shard(l, zero):
        return [(w_out[l] + zero).astype(MXU), (w_qb[l].T + zero).astype(MXU), (w_kvb[l].T + zero).astype(MXU)]

    def early_weights(l, gathered):
        g_in, g_conv = gathered
        return dict(
            norm_g=norm_g[l][None], w_in=g_in.reshape(D_MODEL, NCOL),
            conv_a_w=jnp.concatenate([g_conv[j, 0:3, 0:64] for j in range(N_CHIPS)], axis=1),
            ssd_conv_w=jnp.concatenate([g_conv[j, 3:7, 0:224] for j in range(N_CHIPS)], axis=1),
            ssd_conv_b=ssd_conv_b[l][None], sc=_ssd_scalars(ssd_dt_bias[l], ssd_a_log[l], ssd_d[l]),
            g_ssd=ssd_norm_g[l][None], gq=mla_q_norm_g[l][None], gkv=mla_kv_norm_g[l][None])

    def late_weights(gathered):
        g_out, g_qb, g_kvb = gathered
        return dict(wq=_wq_layout(g_qb.reshape(MLA_HEADS * 96, Q_LORA)), wkv=_wkv_layout(g_kvb.reshape(MLA_HEADS * LANE, KV_LORA)),
                    w_out=g_out.reshape(D_MODEL, D_MODEL))

    def late_grads(dw_out, dwq, dwkv):
        wq = jnp.pad(_wq_unlayout(dwq).reshape(N_CHIPS, 144, Q_LORA), ((0, 0), (0, 16), (0, 0)))
        return [dw_out, wq.reshape(N_CHIPS * 160, Q_LORA), _wkv_unlayout(dwkv)]

    def large_grads(g):
        return [g["w_in"]] + late_grads(g["w_out"], g["wq"], g["wkv"])

    w_in0, pack0 = early_shard(0, 0.0)
    half = w_in0.shape[0] // 2
    gather_a0 = _xchg_begin("gather_a0", _gather_half_plan, 6, [w_in0.reshape(2, half, NCOL), pack0],
                            [_sds((N_CHIPS, 2, half, NCOL), MXU), _sds((N_CHIPS,) + pack0.shape)])
    zero = gather_a0["token"][0, 0]
    cos, sin = _rope_tables(positions[0] + zero.astype(jnp.int32))
    late0, shards1 = late_shard(0, zero), early_shard(1, zero) + late_shard(1, zero)
    mine0, (g_in0, g_conv0) = _xchg_end(gather_a0, [cos, sin] + late0 + shards1)
    forward_a0 = _xchg_begin("forward_a0", _forward_plan, 3, [g_in0], [])
    gather_b0, gather_1 = _xchg_begin_many("gather_b0_1", [_gather_spec(late0, "b0"), _gather_spec(shards1, "1")],
                                           [forward_a0["token"]])
    (g_in0,), _ = _xchg_end(forward_a0, [gather_1["token"]])
    lw0 = early_weights(0, [lax.dynamic_update_index_in_dim(g, s_, chip, 0) for g, s_ in zip((g_in0, g_conv0), mine0)])
    x1, sv0, lw0 = _layer_fwd(x[0], lw0, cos, sin, gather_1["token"],
                              lambda ya, y_ssd: late_weights(_gather_end(gather_b0, [ya, y_ssd])))
    g1 = _gather_end(gather_1, [x1])
    (dx, dgf, loss), sv1, lw1 = _layer_fwd(x1, {**early_weights(1, g1[:2]), **late_weights(g1[2:])}, cos, sin,
                                           head=(final_norm_g[None], loss_target[0]))

    dx, lg1, _, _, _ = _layer_bwd(dx, lw1, sv1, cos, sin)
    grad_x, lg0, red1, rs0_late, rs0 = _layer_bwd(dx, lw0, sv0, cos, sin, _rs_begin(large_grads(lg1), 1),
                                                  late_grads)
    lg = [lg0, lg1]
    grad = {}

    small_names = ["norm_g", "conv_a_w", "ssd_conv_w", "ssd_conv_b", "sc", "g_ssd", "gq", "gkv"]
    parts = [loss[0, 0:1], dgf]
    for nm in small_names:
        parts += [lg[l][nm][:3, DT_LANE:DT_LANE + SSD_HEADS] if nm == "sc" else lg[l][nm] for l in range(DEPTH)]
    shapes = [(1,), (D_MODEL,)] + [(DEPTH,) + shp for shp in ((D_MODEL,), (3, D_CONV_A), (4, N_XBC), (N_XBC,), (3, SSD_HEADS),
                                                              (D_SSD,), (Q_LORA,), (KV_LORA,))]
    red_slab = _allreduce_small(_to_slab(parts, SLAB_ROWS), rs0["h"]["token"])
    rs0 = _rs_add_mine(rs0, [red_slab])
    red = _from_slab(red_slab + rs0["h"]["token"][0, 0], shapes)
    loss_out = red[0][0]
    grad["final_norm_g"] = red[1]
    grad["norm_g"], conv_a_full, sconv_full, grad["ssd_conv_b"], sc_grads = red[2:7]
    grad["ssd_norm_g"], grad["mla_q_norm_g"], grad["mla_kv_norm_g"] = red[7:10]
    grad["conv_a_w"] = lax.dynamic_slice_in_dim(conv_a_full, chip * 64, 64, axis=2)
    grad["ssd_conv_w"] = lax.dynamic_slice_in_dim(sconv_full, chip * 224, 224, axis=2)
    grad["ssd_dt_bias"], grad["ssd_a_log"], grad["ssd_d"] = sc_grads[:, 0], sc_grads[:, 1], sc_grads[:, 2]

    delta, new_m, new_v = {}, {}, {}
    small = [nm for nm in WEIGHTS if nm not in BIG]
    row2 = lambda a: a[None] if a.ndim == 1 else a
    small_out = _adamw(*[[row2(a[nm]) for nm in small] for a in (w, grad, mom, var)], whole=True)
    for nm, (dv, mv, vv) in zip(small, small_out):
        delta[nm], new_m[nm], new_v[nm] = [a.reshape(w[nm].shape) for a in (dv, mv, vv)]

    r_out, r_qb, r_kvb = [jnp.stack([a, b]) for a, b in zip(_rs_end(rs0_late, [red_slab]), red1[1:])]
    late = [nm for nm in BIG if nm != "w_in"]
    view = {nm: (lambda a: a) if nm == "w_out" else (lambda a: jnp.swapaxes(a, 1, 2)) for nm in late}
    late_g = [dict(w_out=r_out, w_qb=r_qb[:, :144], w_kvb=r_kvb)[nm] for nm in late]
    late_out = _adamw(*[[view[nm](a[nm]) for nm in late] for a in (w,)], late_g,
                      *[[view[nm](a[nm]) for nm in late] for a in (mom, var)], whole=False)
    for nm, gv, (dv, mv, vv) in zip(late, late_g, late_out):
        grad[nm], delta[nm], new_m[nm], new_v[nm] = [view[nm](a) for a in (gv, dv, mv, vv)]
    g_in1 = _unperm_cols(red1[0])
    shadow_work = [a for row in small_out + late_out for a in row] + [grad[nm] for nm in small] + [g_in1]
    r_in0, = _rs_end(_rs_add_chips(rs0, shadow_work), [])
    to_cols, from_cols = (lambda a: jnp.transpose(a, (2, 0, 1))), (lambda a: jnp.transpose(a, (1, 2, 0)))
    grad["w_in"], delta["w_in"], new_m["w_in"], new_v["w_in"] = [from_cols(a) for a in _adamw_cols(
        to_cols(w["w_in"]), [_unperm_cols(r_in0), g_in1], to_cols(mom["w_in"]), to_cols(var["w_in"]))]

    return (loss_out, grad_x[None], *[grad[nm] for nm in WEIGHTS], *[delta[nm] for nm in WEIGHTS],
            *[new_m[nm] for nm in WEIGHTS], *[new_v[nm] for nm in WEIGHTS])
```

```python
import functools
import math

import numpy as np
import jax
import jax.numpy as jnp
from jax import lax
from jax.experimental import pallas as pl
from jax.experimental.pallas import tpu as pltpu

F32 = jnp.float32
MXU = jnp.bfloat16

D_MODEL = 1024
DEPTH = 2
D_CONV_A = 256
D_SSD = 384
SSD_HEADS = 6
SSD_BC = 256
SSD_CHUNK = 128
SSD_CHUNKS_PER_STEP = 4
SSD_NORM_EPS = 1e-5
MLA_HEADS = 6
Q_LORA = 256
KV_LORA = 128
QK_NOPE = 64
QK_ROPE = 32
V_DIM = 64
D_MLA = 384
ROPE_BASE = 10000.0
NORM_EPS = 1e-6
IN_COLS = 3110
LANE = 128

O_AH, O_AB, O_AC, O_AZ = 0, 256, 512, 768
O_XBC = 1024
O_SZ = 1920
O_CQA = 2304
O_CKV = 2560
O_CZ = 2688
O_TAIL = 3072
NCOL = 3200
N_XBC = D_SSD + 2 * SSD_BC
DT_LANE = 32
ROPE_LANE = 64

ADAM_LR, ADAM_B1, ADAM_B2, ADAM_EPS, ADAM_WD, ADAM_STEP = 0.001, 0.9, 0.999, 1e-08, 0.01, 10

VMEM_LIMIT = 56 * 1024 * 1024
MESH_T = pl.DeviceIdType.MESH


def _dot(a, b):
    return jnp.dot(a.astype(MXU), b.astype(MXU), preferred_element_type=F32)


def _dot_nt(a, b):
    return lax.dot_general(a.astype(MXU), b.astype(MXU), (((1,), (1,)), ((), ())), preferred_element_type=F32)


def _dot_tn(a, b):
    return lax.dot_general(a.astype(MXU), b.astype(MXU), (((0,), (0,)), ((), ())), preferred_element_type=F32)


def _dot_hi(a, b):
    return jnp.dot(a, b, precision=lax.Precision.HIGHEST, preferred_element_type=F32)


def _dot_hi_tn(a, b):
    return lax.dot_general(a, b, (((0,), (0,)), ((), ())), precision=lax.Precision.HIGHEST, preferred_element_type=F32)


def _sigmoid(z):
    return 1.0 / (1.0 + jnp.exp(-z))


def _silu(z):
    return z * _sigmoid(z)


def _dsilu(z):
    s = _sigmoid(z)
    return s * (1.0 + z * (1.0 - s))


def _softplus(z):
    e = jnp.exp(-jnp.abs(z))
    return jnp.maximum(z, 0.0) + jnp.where(e < 1e-3, e * (1.0 - 0.5 * e), jnp.log(1.0 + e))


def _iota(shape, dim):
    return lax.broadcasted_iota(jnp.int32, shape, dim)


def _shift_down(u, k):
    if k == 0:
        return u
    return jnp.where(_iota(u.shape, 0) >= k, pltpu.roll(u, k, 0), 0.0)


def _shift_up(u, k):
    if k == 0:
        return u
    n = u.shape[0]
    return jnp.where(_iota(u.shape, 0) < n - k, pltpu.roll(u, n - k, 0), 0.0)


def _rope_swap(t):
    lane = _iota(t.shape, 1)
    lo = (lane >= ROPE_LANE) & (lane < ROPE_LANE + 16)
    hi = (lane >= ROPE_LANE + 16) & (lane < ROPE_LANE + 32)
    return jnp.where(lo, pltpu.roll(t, LANE - 16, 1), jnp.where(hi, pltpu.roll(t, 16, 1), 0.0))


def _params(sem=None):
    return pltpu.CompilerParams(dimension_semantics=sem, vmem_limit_bytes=VMEM_LIMIT)


def _full(shape):
    nd = len(shape)
    return pl.BlockSpec(shape, lambda *_: (0,) * nd)


def _sds(shape, dtype=F32):
    return jax.ShapeDtypeStruct(shape, dtype)


def _tile(s):
    return min(512, s)


def _row(ts, w):
    return pl.BlockSpec((ts, w), lambda i: (i, 0))


def _gate_cols(ts, off):
    return pl.BlockSpec((ts, D_SSD), lambda i, _o=off // D_SSD: (i, _o))


def _col(s, off):
    return pl.BlockSpec((s, LANE), lambda j, _o=off // LANE: (0, _o + j))


def _call_after(dep, body, args, *, in_specs, **kw):
    if dep is None:
        return pl.pallas_call(body, in_specs=in_specs, **kw)(*args)
    n = len(args)

    def body_dep(*refs):
        body(*refs[:n], *refs[n + 1:])

    return pl.pallas_call(body_dep, in_specs=list(in_specs) + [pl.BlockSpec(memory_space=pl.ANY)], **kw)(*args, dep)


def _rms(c, g):
    r = lax.rsqrt(jnp.mean(c * c, axis=-1, keepdims=True) + NORM_EPS)
    return c * r * g, r


def _rms_bwd(dn, c, r, g):
    ch = c * r
    dch = dn * g
    dc = r * (dch - ch * jnp.mean(dch * ch, axis=-1, keepdims=True))
    return dc, jnp.sum(dn * ch, axis=0, keepdims=True)


def _inproj_fwd(x, g, w, dep=None):
    s = x.shape[0]
    ts = _tile(s)

    def body(x_ref, g_ref, w_ref, proj_ref, h_ref, r_ref):
        hn, r = _rms(x_ref[...], g_ref[...])
        h = hn.astype(MXU)
        h_ref[...] = h
        r_ref[...] = r
        proj_ref[...] = jnp.dot(h, w_ref[...], preferred_element_type=F32)

    return _call_after(
        dep, body, (x, g, w), name="inproj_fwd", grid=(s // ts,),
        in_specs=[_row(ts, D_MODEL), _full((1, D_MODEL)), _full((D_MODEL, NCOL))],
        out_specs=[_row(ts, NCOL), _row(ts, D_MODEL), _row(ts, 1)],
        out_shape=[_sds((s, NCOL)), _sds((s, D_MODEL), MXU), _sds((s, 1))],
        compiler_params=_params(("parallel",)),
    )


def _conva_fwd(proj, w):
    s = proj.shape[0]

    def body(h_ref, b_ref, c_ref, z_ref, w_ref, y_ref):
        u = c_ref[...] * h_ref[...]
        wv = w_ref[...]
        cv = wv[2:3, :] * u + wv[1:2, :] * _shift_down(u, 1) + wv[0:1, :] * _shift_down(u, 2)
        y_ref[...] = b_ref[...] * cv * _silu(z_ref[...])

    return pl.pallas_call(
        body, name="conva_fwd", grid=(D_CONV_A // LANE,),
        in_specs=[_col(s, O_AH), _col(s, O_AB), _col(s, O_AC), _col(s, O_AZ), pl.BlockSpec((3, LANE), lambda j: (0, j))],
        out_specs=pl.BlockSpec((s, LANE), lambda j: (0, j)),
        out_shape=_sds((s, D_CONV_A)),
        compiler_params=_params(("parallel",)),
    )(proj, proj, proj, proj, w)


def _sconv_pre(u, wv, bv):
    return (wv[3:4, :] * u + wv[2:3, :] * _shift_down(u, 1) + wv[1:2, :] * _shift_down(u, 2)
            + wv[0:1, :] * _shift_down(u, 3) + bv)


def _sconv_fwd(proj, w, b):
    s = proj.shape[0]

    def body(u_ref, w_ref, b_ref, o_ref):
        o_ref[...] = _silu(_sconv_pre(u_ref[...], w_ref[...], b_ref[...]))

    return pl.pallas_call(
        body, name="sconv_fwd", grid=(N_XBC // LANE,),
        in_specs=[_col(s, O_XBC), pl.BlockSpec((4, LANE), lambda j: (0, j)), pl.BlockSpec((1, LANE), lambda j: (0, j))],
        out_specs=pl.BlockSpec((s, LANE), lambda j: (0, j)),
        out_shape=_sds((s, N_XBC)),
        compiler_params=_params(("parallel",)),
    )(proj, w, b)


def _ssd_chunk_common(tail, sc):
    l = SSD_CHUNK
    lane = _iota((l, LANE), 1)
    row = _iota((l, LANE), 0)
    tri = (row >= lane).astype(F32)
    a_row = -jnp.exp(sc[1:2, :])
    pre = tail + sc[0:1, :]
    dt = _softplus(pre)
    a_cs = _dot_hi(tri, dt * a_row)
    return lane, row, tri, a_row, pre, dt, a_cs, a_cs.T


def _pick_col(m, lane, k):
    return jnp.sum(jnp.where(lane == k, m, 0.0), axis=1, keepdims=True)


def _pick_row(m, row, k):
    return jnp.sum(jnp.where(row == k, m, 0.0), axis=0, keepdims=True)


def _ssd_fwd(xbc, proj, sc):
    s = xbc.shape[0]
    nc = s // SSD_CHUNK
    l = SSD_CHUNK
    cps = SSD_CHUNKS_PER_STEP

    def body(xbc_ref, tail_ref, sc_ref, y_ref, st_ref, state):
        @pl.when(pl.program_id(0) == 0)
        def _():
            state[...] = jnp.zeros_like(state)

        sc_v = sc_ref[...]
        lane1 = _iota((1, LANE), 1)
        rowp = _iota((LANE, 1), 0)
        d_row = sc_v[2:3, :]
        states = [state[j] for j in range(3)]
        for u in range(cps):
            r = slice(u * l, (u + 1) * l)
            lane, row, _, _, _, dt, a_cs, a_t = _ssd_chunk_common(tail_ref[r, :], sc_v)
            for j in range(3):
                st_ref[u, j] = states[j]
            for j in range(3):
                xpair = xbc_ref[r, LANE * j:LANE * (j + 1)]
                sp = states[j]
                ypair = jnp.zeros((l, LANE), F32)
                new_s = jnp.zeros((LANE, LANE), F32)
                decay = jnp.zeros((LANE, 1), F32)
                for half in range(2):
                    h = 2 * j + half
                    g = h // 3
                    hm = (lane < 64) if half == 0 else (lane >= 64)
                    hrow = (rowp < 64) if half == 0 else (rowp >= 64)
                    ac = _pick_col(a_cs, lane, DT_LANE + h)
                    ar = _pick_row(a_t, row, DT_LANE + h)
                    dtc = _pick_col(dt, lane, DT_LANE + h)
                    alast = jnp.sum(jnp.where(lane1 == l - 1, ar, 0.0), axis=1, keepdims=True)
                    dh = jnp.sum(jnp.where(lane1 == DT_LANE + h, d_row, 0.0), axis=1, keepdims=True)
                    xm = jnp.where(hm, xpair, 0.0)
                    xd = xm * dtc
                    bm = xbc_ref[r, D_SSD + LANE * g:D_SSD + LANE * (g + 1)]
                    cm = xbc_ref[r, D_SSD + SSD_BC + LANE * g:D_SSD + SSD_BC + LANE * (g + 1)]
                    lm = jnp.where(row >= lane, jnp.exp(jnp.minimum(ac - ar, 0.0)), 0.0)
                    y_diag = _dot(_dot_nt(cm, bm) * lm, xd)
                    y_off = jnp.where(hm, _dot_nt(cm, sp), 0.0) * jnp.exp(ac)
                    ypair = ypair + y_diag + y_off + xm * dh
                    new_s = new_s + _dot_tn(xd * jnp.exp(alast - ac), bm)
                    decay = jnp.where(hrow, jnp.exp(alast), decay)
                states[j] = sp * decay + new_s
                y_ref[r, LANE * j:LANE * (j + 1)] = ypair
        for j in range(3):
            state[j] = states[j]

    return pl.pallas_call(
        body, name="ssd_fwd", grid=(nc // cps,),
        in_specs=[pl.BlockSpec((cps * l, N_XBC), lambda c: (c, 0)),
                  pl.BlockSpec((cps * l, LANE), lambda c: (c, O_TAIL // LANE)), _full((8, LANE))],
        out_specs=[pl.BlockSpec((cps * l, D_SSD), lambda c: (c, 0)), pl.BlockSpec((cps, 3, LANE, LANE), lambda c: (c, 0, 0, 0))],
        out_shape=[_sds((s, D_SSD)), _sds((nc, 3, LANE, LANE))],
        scratch_shapes=[pltpu.VMEM((3, LANE, LANE), F32)],
        compiler_params=_params(("arbitrary",)),
    )(xbc, proj, sc)


def _mla_prep_fwd(proj, gq, gkv, wq, wkv, cos, sin):
    s = proj.shape[0]
    ts = _tile(s)
    nh = MLA_HEADS

    def body(cqa_ref, ckv_ref, tail_ref, gq_ref, gkv_ref, wq_ref, wkv_ref, cos_ref, sin_ref,
             q_ref, k_ref, v_ref, qn_ref, kvn_ref, rq_ref, rkv_ref):
        qn, rq = _rms(cqa_ref[...], gq_ref[...])
        kvn, rkv = _rms(ckv_ref[...], gkv_ref[...])
        qn = qn.astype(MXU)
        kvn = kvn.astype(MXU)
        qn_ref[...] = qn
        kvn_ref[...] = kvn
        rq_ref[...] = rq
        rkv_ref[...] = rkv
        q = _dot_nt(qn, wq_ref[...])
        kv = _dot_nt(kvn, wkv_ref[...])
        cosv = cos_ref[...]
        sinv = sin_ref[...]
        lane = _iota((ts, LANE), 1)
        rope_lanes = (lane >= ROPE_LANE) & (lane < ROPE_LANE + QK_ROPE)
        kr = jnp.where(rope_lanes, pltpu.roll(tail_ref[...], ROPE_LANE, 1), 0.0)
        kr = kr * cosv + _rope_swap(kr) * sinv
        for h in range(nh):
            qh = q[:, LANE * h:LANE * (h + 1)]
            q_ref[h] = ((qh * cosv + _rope_swap(qh) * sinv) * ATT_SCALE).astype(MXU)
            k_ref[h] = (kv[:, LANE * h:LANE * (h + 1)] + kr).astype(MXU)
            v_ref[h] = kv[:, LANE * (nh + h):LANE * (nh + h + 1)].astype(MXU)

    head = pl.BlockSpec((nh, ts, LANE), lambda i: (0, i, 0))
    return pl.pallas_call(
        body, name="mla_prep_fwd", grid=(s // ts,),
        in_specs=[pl.BlockSpec((ts, Q_LORA), lambda i: (i, O_CQA // Q_LORA)),
                  pl.BlockSpec((ts, KV_LORA), lambda i: (i, O_CKV // KV_LORA)),
                  pl.BlockSpec((ts, LANE), lambda i: (i, O_TAIL // LANE)),
                  _full((1, Q_LORA)), _full((1, KV_LORA)), _full((nh * LANE, Q_LORA)), _full((2 * nh * LANE, KV_LORA)),
                  _row(ts, LANE), _row(ts, LANE)],
        out_specs=[head, head, head, _row(ts, Q_LORA), _row(ts, KV_LORA), _row(ts, 1), _row(ts, 1)],
        out_shape=[_sds((nh, s, LANE), MXU)] * 3 + [_sds((s, Q_LORA), MXU), _sds((s, KV_LORA), MXU), _sds((s, 1)), _sds((s, 1))],
        compiler_params=_params(("parallel",)),
    )(proj, proj, proj, gq, gkv, wq, wkv, cos, sin)


ATT_SCALE = (QK_NOPE + QK_ROPE) ** -0.5
NEG = -1e30


def _att_tile(s, most):
    return min(most, s // 2)


ATT_FWD_TILE = 1024
ATT_BWD_TILE = 512


def _attn_fwd(q, k, v):
    nh, s, _ = q.shape
    tq = _att_tile(s, ATT_FWD_TILE)
    nq = s // tq

    def body(q_ref, k_ref, v_ref, o_ref, lse_ref):
        i = pl.program_id(1)
        rowi = _iota((tq, tq), 0)
        coli = _iota((tq, tq), 1)
        zero = (jnp.full((tq, 1), NEG, F32), jnp.zeros((tq, 1), F32), jnp.zeros((tq, LANE), F32))
        state = [zero, zero]
        done = [zero, zero]
        for t in range(nq + 1):
            first = t <= i
            qblk = jnp.where(first, i, nq - 1 - i)
            kblk = jnp.where(first, t, t - i - 1)
            qoff = pl.multiple_of(qblk * tq, tq)
            koff = pl.multiple_of(kblk * tq, tq)
            keep = coli <= rowi + jnp.where(kblk == qblk, 0, tq)
            restart = t == i + 1
            for hh in range(2):
                m, lsum, acc = state[hh]
                if t > 0:
                    done[hh] = tuple(jnp.where(restart, a, b) for a, b in zip(state[hh], done[hh]))
                    m = jnp.where(restart, NEG, m)
                    lsum = jnp.where(restart, 0.0, lsum)
                    acc = jnp.where(restart, 0.0, acc)
                sc = _dot_nt(q_ref[hh, pl.ds(qoff, tq), :], k_ref[hh, pl.ds(koff, tq), :])
                sc = jnp.where(keep, sc, NEG)
                m_new = jnp.maximum(m, jnp.max(sc, axis=1, keepdims=True))
                p = jnp.exp(sc - m_new)
                alpha = jnp.exp(m - m_new)
                lsum = alpha * lsum + jnp.sum(p, axis=1, keepdims=True)
                acc = alpha * acc + _dot(p, v_ref[hh, pl.ds(koff, tq), :])
                state[hh] = (m_new, lsum, acc)
        for blk, res in ((i, done), (nq - 1 - i, state)):
            off = pl.multiple_of(blk * tq, tq)
            out = None
            for hh in range(2):
                m, lsum, acc = res[hh]
                o = acc * (1.0 / lsum)
                lse_ref[hh, pl.ds(off, tq), :] = m + jnp.log(lsum)
                out = o if hh == 0 else out + pltpu.roll(o, V_DIM, 1)
            o_ref[pl.ds(off, tq), :] = out

    pair = pl.BlockSpec((2, s, LANE), lambda j, i: (j, 0, 0))
    return pl.pallas_call(
        body, name="attn_fwd", grid=(nh // 2, nq // 2),
        in_specs=[pair, pair, pair],
        out_specs=[pl.BlockSpec((s, LANE), lambda j, i: (0, j)), pl.BlockSpec((2, s, 1), lambda j, i: (j, 0, 0))],
        out_shape=[_sds((s, D_MLA)), _sds((nh, s, 1))],
        compiler_params=_params(("parallel", "arbitrary")),
    )(q, k, v)


def _ssd_gate(y_ssd, s_z, g):
    yz = y_ssd * _silu(s_z)
    g0 = _iota(yz.shape, 1) < D_SSD // 2
    sq = yz * yz
    ms0 = jnp.sum(jnp.where(g0, sq, 0.0), axis=1, keepdims=True) / (D_SSD // 2)
    ms1 = jnp.sum(jnp.where(g0, 0.0, sq), axis=1, keepdims=True) / (D_SSD // 2)
    r = jnp.where(g0, lax.rsqrt(ms0 + SSD_NORM_EPS), lax.rsqrt(ms1 + SSD_NORM_EPS))
    nrm = yz * r
    return nrm * g, nrm, r, g0


def _outproj_fwd(x, proj, ya, y_ssd, o, g_ssd, w):
    s = x.shape[0]
    ts = _tile(s)

    def body(x_ref, sz_ref, cz_ref, ya_ref, ys_ref, o_ref, g_ref, w_ref, xo_ref, y_ref):
        yb = _ssd_gate(ys_ref[...], sz_ref[...], g_ref[...])[0]
        yc = o_ref[...] * _silu(cz_ref[...])
        y = jnp.concatenate([ya_ref[...], yb, yc], axis=1).astype(MXU)
        y_ref[...] = y
        xo_ref[...] = x_ref[...] + jnp.dot(y, w_ref[...], preferred_element_type=F32)

    return pl.pallas_call(
        body, name="outproj_fwd", grid=(s // ts,),
        in_specs=[_row(ts, D_MODEL), _gate_cols(ts, O_SZ), _gate_cols(ts, O_CZ), _row(ts, D_CONV_A), _row(ts, D_SSD),
                  _row(ts, D_MLA), _full((1, D_SSD)), _full((D_MODEL, D_MODEL))],
        out_specs=[_row(ts, D_MODEL), _row(ts, D_MODEL)],
        out_shape=[_sds((s, D_MODEL)), _sds((s, D_MODEL), MXU)],
        compiler_params=_params(("parallel",)),
    )(x, proj, proj, ya, y_ssd, o, g_ssd, w)


def _outproj_loss(x, proj, ya, y_ssd, o, g_ssd, w, final_g, tgt):
    s = x.shape[0]
    ts = _tile(s)

    def body(x_ref, sz_ref, cz_ref, ya_ref, ys_ref, o_ref, g_ref, w_ref, fg_ref, t_ref, dx_ref, dg_ref, loss_ref, y_ref):
        @pl.when(pl.program_id(0) == 0)
        def _():
            dg_ref[...] = jnp.zeros_like(dg_ref)
            loss_ref[...] = jnp.zeros_like(loss_ref)

        yb = _ssd_gate(ys_ref[...], sz_ref[...], g_ref[...])[0]
        yc = o_ref[...] * _silu(cz_ref[...])
        y = jnp.concatenate([ya_ref[...], yb, yc], axis=1).astype(MXU)
        y_ref[...] = y
        xv = x_ref[...] + jnp.dot(y, w_ref[...], preferred_element_type=F32)
        gv = fg_ref[...]
        yn, r = _rms(xv, gv)
        e = yn - t_ref[...]
        loss_ref[...] += jnp.sum(jnp.sum(e * e, axis=1, keepdims=True), axis=0, keepdims=True) * (0.5 / D_MODEL)
        dx, dg = _rms_bwd(e * (1.0 / D_MODEL), xv, r, gv)
        dx_ref[...] = dx
        dg_ref[...] += dg

    return pl.pallas_call(
        body, name="outproj_loss", grid=(s // ts,),
        in_specs=[_row(ts, D_MODEL), _gate_cols(ts, O_SZ), _gate_cols(ts, O_CZ), _row(ts, D_CONV_A), _row(ts, D_SSD),
                  _row(ts, D_MLA), _full((1, D_SSD)), _full((D_MODEL, D_MODEL)), _full((1, D_MODEL)), _row(ts, D_MODEL)],
        out_specs=[_row(ts, D_MODEL), _full((1, D_MODEL)), _full((1, LANE)), _row(ts, D_MODEL)],
        out_shape=[_sds((s, D_MODEL)), _sds((1, D_MODEL)), _sds((1, LANE)), _sds((s, D_MODEL), MXU)],
        compiler_params=_params(("arbitrary",)),
    )(x, proj, proj, ya, y_ssd, o, g_ssd, w, final_g, tgt)


def _outproj_bwd(dout, y, w, proj, y_ssd, o, g_ssd, dep=None):
    s = dout.shape[0]
    ts = _tile(s)

    def body(dout_ref, y_ref, w_ref, sz_ref, cz_ref, ys_ref, o_ref, g_ref,
             dya_ref, dys_ref, dsz_ref, dattn_ref, dcz_ref, dg_ref, dw_ref):
        @pl.when(pl.program_id(0) == 0)
        def _():
            dw_ref[...] = jnp.zeros_like(dw_ref)
            dg_ref[...] = jnp.zeros_like(dg_ref)

        dout_b = dout_ref[...].astype(MXU)
        dw_ref[...] += _dot_tn(y_ref[...], dout_b)
        dy = _dot_nt(dout_b, w_ref[...])
        dya_ref[...] = dy[:, :D_CONV_A]
        dyb = dy[:, D_CONV_A:D_CONV_A + D_SSD]
        sz = sz_ref[...]
        ys = ys_ref[...]
        gv = g_ref[...]
        _, nrm, r, g0 = _ssd_gate(ys, sz, gv)
        dg_ref[...] += jnp.sum(dyb * nrm, axis=0, keepdims=True)
        dn = dyb * gv
        t = dn * nrm
        mean = jnp.where(g0, jnp.sum(jnp.where(g0, t, 0.0), axis=1, keepdims=True),
                         jnp.sum(jnp.where(g0, 0.0, t), axis=1, keepdims=True)) / (D_SSD // 2)
        dyz = r * (dn - nrm * mean)
        dys_ref[...] = dyz * _silu(sz)
        dsz_ref[...] = (dyz * ys * _dsilu(sz)).astype(MXU)
        dyc = dy[:, D_CONV_A + D_SSD:]
        cz = cz_ref[...]
        dattn_ref[...] = dyc * _silu(cz)
        dcz_ref[...] = (dyc * o_ref[...] * _dsilu(cz)).astype(MXU)

    return _call_after(
        dep, body, (dout, y, w, proj, proj, y_ssd, o, g_ssd), name="outproj_bwd", grid=(s // ts,),
        in_specs=[_row(ts, D_MODEL), _row(ts, D_MODEL), _full((D_MODEL, D_MODEL)), _gate_cols(ts, O_SZ), _gate_cols(ts, O_CZ),
                  _row(ts, D_SSD), _row(ts, D_MLA), _full((1, D_SSD))],
        out_specs=[_row(ts, D_CONV_A), _row(ts, D_SSD), _row(ts, D_SSD), _row(ts, D_MLA), _row(ts, D_MLA),
                   _full((1, D_SSD)), _full((D_MODEL, D_MODEL))],
        out_shape=[_sds((s, D_CONV_A)), _sds((s, D_SSD)), _sds((s, D_SSD), MXU), _sds((s, D_MLA)), _sds((s, D_MLA), MXU),
                   _sds((1, D_SSD)), _sds((D_MODEL, D_MODEL))],
        compiler_params=_params(("arbitrary",)),
    )


def _attn_bwd(q, k, v, o, d_o, lse, dep=None):
    nh, s, _ = q.shape
    tq = _att_tile(s, ATT_BWD_TILE)
    nq = s // tq

    def body(q_ref, k_ref, v_ref, o_ref, do_ref, lse_ref, dq_ref, dk_ref, dv_ref, dop, delta):
        i = pl.program_id(1)

        @pl.when(i == 0)
        def _():
            lane = _iota((s, LANE), 1)
            for hh in range(2):
                dov = do_ref[...]
                ov = o_ref[...]
                if hh == 1:
                    dov = pltpu.roll(dov, V_DIM, 1)
                    ov = pltpu.roll(ov, V_DIM, 1)
                dov = jnp.where(lane < V_DIM, dov, 0.0)
                dop[hh] = dov.astype(MXU)
                delta[hh] = jnp.sum(dov * ov, axis=1, keepdims=True)
                dq_ref[hh] = jnp.zeros((s, LANE), F32)

        rowi = _iota((tq, tq), 0)
        coli = _iota((tq, tq), 1)
        z = jnp.zeros((tq, LANE), F32)
        state = [(z, z), (z, z)]
        done = [(z, z), (z, z)]
        for t in range(nq + 1):
            first = t <= nq - 1 - i
            kblk = jnp.where(first, i, nq - 1 - i)
            qblk = jnp.where(first, i + t, t - 1)
            qoff = pl.multiple_of(qblk * tq, tq)
            koff = pl.multiple_of(kblk * tq, tq)
            keep = coli <= rowi + jnp.where(kblk == qblk, 0, tq)
            restart = t == nq - i
            for hh in range(2):
                dk, dv = state[hh]
                if t > 0:
                    done[hh] = tuple(jnp.where(restart, a, b) for a, b in zip(state[hh], done[hh]))
                    dk = jnp.where(restart, 0.0, dk)
                    dv = jnp.where(restart, 0.0, dv)
                kb = k_ref[hh, pl.ds(koff, tq), :]
                qb = q_ref[hh, pl.ds(qoff, tq), :]
                dob = dop[hh, pl.ds(qoff, tq), :]
                sc = jnp.where(keep, _dot_nt(qb, kb), NEG)
                p = jnp.exp(sc - lse_ref[hh, pl.ds(qoff, tq), :])
                dp = _dot_nt(dob, v_ref[hh, pl.ds(koff, tq), :])
                ds = p * (dp - delta[hh, pl.ds(qoff, tq), :])
                dq_ref[hh, pl.ds(qoff, tq), :] += _dot(ds, kb)
                state[hh] = (dk + _dot_tn(ds, qb), dv + _dot_tn(p, dob))
        for blk, res in ((i, done), (nq - 1 - i, state)):
            off = pl.multiple_of(blk * tq, tq)
            for hh in range(2):
                dk_ref[hh, pl.ds(off, tq), :] = res[hh][0]
                dv_ref[hh, pl.ds(off, tq), :] = res[hh][1]

    pair = pl.BlockSpec((2, s, LANE), lambda j, i: (j, 0, 0))
    return _call_after(
        dep, body, (q, k, v, o, d_o, lse), name="attn_bwd", grid=(nh // 2, nq // 2),
        in_specs=[pair, pair, pair, pl.BlockSpec((s, LANE), lambda j, i: (0, j)), pl.BlockSpec((s, LANE), lambda j, i: (0, j)),
                  pl.BlockSpec((2, s, 1), lambda j, i: (j, 0, 0))],
        out_specs=[pair, pair, pair],
        out_shape=[_sds((nh, s, LANE))] * 3,
        scratch_shapes=[pltpu.VMEM((2, s, LANE), MXU), pltpu.VMEM((2, s, 1), F32)],
        compiler_params=_params(("parallel", "arbitrary")),
    )


def _ssd_bwd(xbc, proj, sc, states, dy, dep=None):
    s = xbc.shape[0]
    nc = s // SSD_CHUNK
    l = SSD_CHUNK
    cps = SSD_CHUNKS_PER_STEP

    def body(xbc_ref, tail_ref, sc_ref, st_ref, dy_ref, dxbc_ref, dtail_ref, dsc_ref, dstate):
        @pl.when(pl.program_id(0) == 0)
        def _():
            dstate[...] = jnp.zeros_like(dstate)
            dsc_ref[...] = jnp.zeros_like(dsc_ref)

        sc_v = sc_ref[...]
        lane1 = _iota((1, LANE), 1)
        rowp = _iota((LANE, 1), 0)
        rowl = _iota((l, 1), 0)
        d_row = sc_v[2:3, :]
        dstates = [dstate[j] for j in range(3)]
        for u in reversed(range(cps)):
            dstates = chunk(u, xbc_ref, tail_ref, sc_v, st_ref, dy_ref, dxbc_ref, dtail_ref, dsc_ref, dstates,
                            lane1, rowp, rowl, d_row)
        for j in range(3):
            dstate[j] = dstates[j]

    def chunk(u, xbc_ref, tail_ref, sc_v, st_ref, dy_ref, dxbc_ref, dtail_ref, dsc_ref, dstates, lane1, rowp, rowl, d_row):
        r = slice(u * l, (u + 1) * l)
        dstates = list(dstates)
        lane, row, tri, a_row, pre, dt, a_cs, a_t = _ssd_chunk_common(tail_ref[r, :], sc_v)
        da_col = jnp.zeros((l, LANE), F32)
        da_row = jnp.zeros((LANE, l), F32)
        dt_x = jnp.zeros((l, LANE), F32)
        dd_row = jnp.zeros((1, LANE), F32)
        db = [jnp.zeros((l, LANE), F32), jnp.zeros((l, LANE), F32)]
        dc = [jnp.zeros((l, LANE), F32), jnp.zeros((l, LANE), F32)]
        for j in range(3):
            xpair = xbc_ref[r, LANE * j:LANE * (j + 1)]
            dypair = dy_ref[r, LANE * j:LANE * (j + 1)]
            sp = st_ref[u, j]
            dsp = dstates[j]
            dxpair = jnp.zeros((l, LANE), F32)
            ds_new = jnp.zeros((LANE, LANE), F32)
            decay = jnp.zeros((LANE, 1), F32)
            for half in range(2):
                h = 2 * j + half
                g = h // 3
                hm = (lane < 64) if half == 0 else (lane >= 64)
                hrow = (rowp < 64) if half == 0 else (rowp >= 64)
                ac = _pick_col(a_cs, lane, DT_LANE + h)
                ar = _pick_row(a_t, row, DT_LANE + h)
                dtc = _pick_col(dt, lane, DT_LANE + h)
                alast = jnp.sum(jnp.where(lane1 == l - 1, ar, 0.0), axis=1, keepdims=True)
                dh = jnp.sum(jnp.where(lane1 == DT_LANE + h, d_row, 0.0), axis=1, keepdims=True)
                xm = jnp.where(hm, xpair, 0.0)
                xd = xm * dtc
                dym = jnp.where(hm, dypair, 0.0)
                bm = xbc_ref[r, D_SSD + LANE * g:D_SSD + LANE * (g + 1)]
                cm = xbc_ref[r, D_SSD + SSD_BC + LANE * g:D_SSD + SSD_BC + LANE * (g + 1)]
                lm = jnp.where(row >= lane, jnp.exp(jnp.minimum(ac - ar, 0.0)), 0.0)
                e_in = jnp.exp(ac)
                f_out = jnp.exp(alast - ac)
                e_last = jnp.exp(alast)
                m = _dot_nt(cm, bm) * lm
                y_off = jnp.where(hm, _dot_nt(cm, sp), 0.0) * e_in
                dm = _dot_nt(dym, xd)
                dxd = _dot_tn(m, dym)
                dg = dm * lm
                dye = dym * e_in
                dc[g] = dc[g] + _dot(dg, bm) + _dot(dye, sp)
                db[g] = db[g] + _dot_tn(dg, cm)
                qm = dm * m
                dac = jnp.sum(qm, axis=1, keepdims=True) + jnp.sum(dym * y_off, axis=1, keepdims=True)
                dar = -jnp.sum(qm, axis=0, keepdims=True)
                dxf = jnp.where(hm, _dot_nt(bm, dsp), 0.0)
                db[g] = db[g] + _dot(xd * f_out, dsp)
                dxd = dxd + dxf * f_out
                df = jnp.sum(dxf * xd, axis=1, keepdims=True) * f_out
                dac = dac - df
                s_last = jnp.sum(df, axis=0, keepdims=True)
                ss = jnp.sum(jnp.where(hrow, dsp * sp, 0.0), axis=1, keepdims=True)
                s_last = s_last + e_last * jnp.sum(ss, axis=0, keepdims=True)
                dac = dac + jnp.where(rowl == l - 1, s_last, 0.0)
                ds_new = ds_new + _dot_tn(dye, cm)
                decay = jnp.where(hrow, e_last, decay)
                dxpair = dxpair + dxd * dtc + dym * dh
                dt_x = dt_x + jnp.where(lane == DT_LANE + h, jnp.sum(dxd * xm, axis=1, keepdims=True), 0.0)
                dsum = jnp.sum(jnp.sum(dym * xm, axis=1, keepdims=True), axis=0, keepdims=True)
                dd_row = dd_row + jnp.where(lane1 == DT_LANE + h, dsum, 0.0)
                da_col = da_col + jnp.where(lane == DT_LANE + h, dac, 0.0)
                da_row = da_row + jnp.where(row == DT_LANE + h, dar, 0.0)
            dstates[j] = dsp * decay + ds_new
            dxbc_ref[r, LANE * j:LANE * (j + 1)] = dxpair
        for g in range(2):
            dxbc_ref[r, D_SSD + LANE * g:D_SSD + LANE * (g + 1)] = db[g]
            dxbc_ref[r, D_SSD + SSD_BC + LANE * g:D_SSD + SSD_BC + LANE * (g + 1)] = dc[g]
        dla = _dot_hi_tn(tri, da_col + da_row.T)
        ddt = dt_x + dla * a_row
        dpre = ddt * _sigmoid(pre)
        dtm = (lane >= DT_LANE) & (lane < DT_LANE + SSD_HEADS)
        dtail_ref[r, :] = jnp.where(dtm, dpre, 0.0).astype(MXU)
        dtm1 = (lane1 >= DT_LANE) & (lane1 < DT_LANE + SSD_HEADS)
        dsc_ref[0:1, :] += jnp.where(dtm1, jnp.sum(dpre, axis=0, keepdims=True), 0.0)
        dsc_ref[1:2, :] += jnp.where(dtm1, jnp.sum(dla * dt, axis=0, keepdims=True) * a_row, 0.0)
        dsc_ref[2:3, :] += dd_row
        return dstates

    rev = lambda c: nc // cps - 1 - c
    return _call_after(
        dep, body, (xbc, proj, sc, states, dy), name="ssd_bwd", grid=(nc // cps,),
        in_specs=[pl.BlockSpec((cps * l, N_XBC), lambda c: (rev(c), 0)),
                  pl.BlockSpec((cps * l, LANE), lambda c: (rev(c), O_TAIL // LANE)), _full((8, LANE)),
                  pl.BlockSpec((cps, 3, LANE, LANE), lambda c: (rev(c), 0, 0, 0)),
                  pl.BlockSpec((cps * l, D_SSD), lambda c: (rev(c), 0))],
        out_specs=[pl.BlockSpec((cps * l, N_XBC), lambda c: (rev(c), 0)), pl.BlockSpec((cps * l, LANE), lambda c: (rev(c), 0)),
                   _full((8, LANE))],
        out_shape=[_sds((s, N_XBC)), _sds((s, LANE), MXU), _sds((8, LANE))],
        scratch_shapes=[pltpu.VMEM((3, LANE, LANE), F32)],
        compiler_params=_params(("arbitrary",)),
    )


def _sconv_bwd(proj, w, b, dxbc, dep=None):
    s = proj.shape[0]

    def body(u_ref, w_ref, b_ref, d_ref, du_ref, dw_ref, db_ref):
        u = u_ref[...]
        wv = w_ref[...]
        dpre = d_ref[...] * _dsilu(_sconv_pre(u, wv, b_ref[...]))
        ahead = [_shift_up(dpre, j) for j in range(4)]
        du_ref[...] = (wv[3:4, :] * ahead[0] + wv[2:3, :] * ahead[1] + wv[1:2, :] * ahead[2]
                       + wv[0:1, :] * ahead[3]).astype(MXU)
        for k in range(4):
            dw_ref[k:k + 1, :] = jnp.sum(ahead[3 - k] * u, axis=0, keepdims=True)
        db_ref[...] = jnp.sum(dpre, axis=0, keepdims=True)

    blk = pl.BlockSpec((s, LANE), lambda j: (0, j))
    return _call_after(
        dep, body, (proj, w, b, dxbc), name="sconv_bwd", grid=(N_XBC // LANE,),
        in_specs=[_col(s, O_XBC), pl.BlockSpec((4, LANE), lambda j: (0, j)), pl.BlockSpec((1, LANE), lambda j: (0, j)), blk],
        out_specs=[blk, pl.BlockSpec((4, LANE), lambda j: (0, j)), pl.BlockSpec((1, LANE), lambda j: (0, j))],
        out_shape=[_sds((s, N_XBC), MXU), _sds((4, N_XBC)), _sds((1, N_XBC))],
        compiler_params=_params(("parallel",)),
    )


def _conva_bwd(proj, w, dya, dep=None):
    s = proj.shape[0]

    def body(h_ref, b_ref, c_ref, z_ref, w_ref, d_ref, da_ref, dw_ref):
        ah, ab, acv, az = h_ref[...], b_ref[...], c_ref[...], z_ref[...]
        wv = w_ref[...]
        u = acv * ah
        cv = wv[2:3, :] * u + wv[1:2, :] * _shift_down(u, 1) + wv[0:1, :] * _shift_down(u, 2)
        dy = d_ref[...]
        sz = _silu(az)
        da_ref[1] = (dy * cv * sz).astype(MXU)
        da_ref[3] = (dy * ab * cv * _dsilu(az)).astype(MXU)
        dcv = dy * ab * sz
        ahead = [_shift_up(dcv, j) for j in range(3)]
        du = wv[2:3, :] * ahead[0] + wv[1:2, :] * ahead[1] + wv[0:1, :] * ahead[2]
        da_ref[0] = (du * acv).astype(MXU)
        da_ref[2] = (du * ah).astype(MXU)
        for k in range(3):
            dw_ref[k:k + 1, :] = jnp.sum(ahead[2 - k] * u, axis=0, keepdims=True)

    return _call_after(
        dep, body, (proj, proj, proj, proj, w, dya), name="conva_bwd", grid=(D_CONV_A // LANE,),
        in_specs=[_col(s, O_AH), _col(s, O_AB), _col(s, O_AC), _col(s, O_AZ), pl.BlockSpec((3, LANE), lambda j: (0, j)),
                  pl.BlockSpec((s, LANE), lambda j: (0, j))],
        out_specs=[pl.BlockSpec((4, s, LANE), lambda j: (0, 0, j)), pl.BlockSpec((3, LANE), lambda j: (0, j))],
        out_shape=[_sds((4, s, D_CONV_A), MXU), _sds((3, D_CONV_A))],
        compiler_params=_params(("parallel",)),
    )


def _mla_prep_bwd(dq, dk, dv, proj, qn, kvn, rq, rkv, gq, gkv, wq, wkv, cos, sin):
    s = proj.shape[0]
    ts = _tile(s)
    nh = MLA_HEADS

    def body(dq_ref, dk_ref, dv_ref, cqa_ref, ckv_ref, qn_ref, kvn_ref, rq_ref, rkv_ref, gq_ref, gkv_ref,
             wq_ref, wkv_ref, cos_ref, sin_ref, dcqa_ref, dckv_ref, dtail_ref, dwq_ref, dwkv_ref, dgq_ref, dgkv_ref):
        @pl.when(pl.program_id(0) == 0)
        def _():
            dwq_ref[...] = jnp.zeros_like(dwq_ref)
            dwkv_ref[...] = jnp.zeros_like(dwkv_ref)
            dgq_ref[...] = jnp.zeros_like(dgq_ref)
            dgkv_ref[...] = jnp.zeros_like(dgkv_ref)

        cosv = cos_ref[...]
        sinv = sin_ref[...]
        lane = _iota((ts, LANE), 1)
        rope_lanes = (lane >= ROPE_LANE) & (lane < ROPE_LANE + QK_ROPE)

        def unrope(gr):
            return gr * cosv + _rope_swap(gr * sinv)

        dqs, dks, dvs = [], [], []
        dkr = jnp.zeros((ts, LANE), F32)
        for h in range(nh):
            dqs.append(unrope(dq_ref[h] * ATT_SCALE).astype(MXU))
            dkh = dk_ref[h]
            dks.append(jnp.where(lane < QK_NOPE, dkh, 0.0).astype(MXU))
            dkr = dkr + jnp.where(rope_lanes, dkh, 0.0)
            dvs.append(dv_ref[h].astype(MXU))
        dtail_ref[...] = pltpu.roll(jnp.where(rope_lanes, unrope(dkr), 0.0), ROPE_LANE, 1).astype(MXU)
        dq_all = jnp.concatenate(dqs, axis=1)
        dkv_all = jnp.concatenate(dks + dvs, axis=1)
        dwq_ref[...] += _dot_tn(dq_all, qn_ref[...])
        dwkv_ref[...] += _dot_tn(dkv_all, kvn_ref[...])
        dcqa, dgq = _rms_bwd(_dot(dq_all, wq_ref[...]), cqa_ref[...], rq_ref[...], gq_ref[...])
        dckv, dgkv = _rms_bwd(_dot(dkv_all, wkv_ref[...]), ckv_ref[...], rkv_ref[...], gkv_ref[...])
        dcqa_ref[...] = dcqa.astype(MXU)
        dckv_ref[...] = dckv.astype(MXU)
        dgq_ref[...] += dgq
        dgkv_ref[...] += dgkv

    head = pl.BlockSpec((nh, ts, LANE), lambda i: (0, i, 0))
    return pl.pallas_call(
        body, name="mla_prep_bwd", grid=(s // ts,),
        in_specs=[head, head, head,
                  pl.BlockSpec((ts, Q_LORA), lambda i: (i, O_CQA // Q_LORA)),
                  pl.BlockSpec((ts, KV_LORA), lambda i: (i, O_CKV // KV_LORA)),
                  _row(ts, Q_LORA), _row(ts, KV_LORA), _row(ts, 1), _row(ts, 1),
                  _full((1, Q_LORA)), _full((1, KV_LORA)), _full((nh * LANE, Q_LORA)), _full((2 * nh * LANE, KV_LORA)),
                  _row(ts, LANE), _row(ts, LANE)],
        out_specs=[_row(ts, Q_LORA), _row(ts, KV_LORA), _row(ts, LANE), _full((nh * LANE, Q_LORA)),
                   _full((2 * nh * LANE, KV_LORA)), _full((1, Q_LORA)), _full((1, KV_LORA))],
        out_shape=[_sds((s, Q_LORA), MXU), _sds((s, KV_LORA), MXU), _sds((s, LANE), MXU), _sds((nh * LANE, Q_LORA)),
                   _sds((2 * nh * LANE, KV_LORA)), _sds((1, Q_LORA)), _sds((1, KV_LORA))],
        compiler_params=_params(("arbitrary",)),
    )(dq, dk, dv, proj, proj, qn, kvn, rq, rkv, gq, gkv, wq, wkv, cos, sin)


def _inproj_bwd(da4, dsz, dxbc_in, dcqa, dckv, dcz, dtail_a, dtail_b, w, x, rstd, g, dout, dep=None):
    s = x.shape[0]
    ts = _tile(s)

    def body(da_ref, dsz_ref, dxbc_ref, dcqa_ref, dckv_ref, dcz_ref, dta_ref, dtb_ref, w_ref, x_ref, r_ref, g_ref, dout_ref,
             dproj_ref, dx_ref, dg_ref):
        @pl.when(pl.program_id(0) == 0)
        def _():
            dg_ref[...] = jnp.zeros_like(dg_ref)

        dproj = jnp.concatenate(
            [da_ref[0], da_ref[1], da_ref[2], da_ref[3], dxbc_ref[...], dsz_ref[...], dcqa_ref[...], dckv_ref[...],
             dcz_ref[...], dta_ref[...] + dtb_ref[...]], axis=1)
        dproj_ref[...] = dproj
        dh = _dot_nt(dproj, w_ref[...])
        dx, dg = _rms_bwd(dh, x_ref[...], r_ref[...], g_ref[...])
        dx_ref[...] = dout_ref[...] + dx
        dg_ref[...] += dg

    return _call_after(
        dep, body, (da4, dsz, dxbc_in, dcqa, dckv, dcz, dtail_a, dtail_b, w, x, rstd, g, dout), name="inproj_bwd", grid=(s // ts,),
        in_specs=[pl.BlockSpec((4, ts, D_CONV_A), lambda i: (0, i, 0)), _row(ts, D_SSD), _row(ts, N_XBC), _row(ts, Q_LORA),
                  _row(ts, KV_LORA), _row(ts, D_MLA), _row(ts, LANE), _row(ts, LANE), _full((D_MODEL, NCOL)),
                  _row(ts, D_MODEL), _row(ts, 1), _full((1, D_MODEL)), _row(ts, D_MODEL)],
        out_specs=[_row(ts, NCOL), _row(ts, D_MODEL), _full((1, D_MODEL))],
        out_shape=[_sds((s, NCOL), MXU), _sds((s, D_MODEL)), _sds((1, D_MODEL))],
        compiler_params=_params(("arbitrary",)),
    )


DWIN_BLOCK = 640


def _dwin(h, dproj, dep=None):
    s = h.shape[0]

    def body(h_ref, d_ref, o_ref):
        o_ref[...] = _dot_tn(h_ref[...], d_ref[...])

    return _call_after(
        dep, body, (h, dproj), name="dwin", grid=(NCOL // DWIN_BLOCK,),
        in_specs=[_full((s, D_MODEL)), pl.BlockSpec((s, DWIN_BLOCK), lambda j: (0, j))],
        out_specs=pl.BlockSpec((D_MODEL, DWIN_BLOCK), lambda j: (0, j)),
        out_shape=_sds((D_MODEL, NCOL)),
        compiler_params=_params(("parallel",)),
    )


def _adamw(ws, gs, ms, vs, whole):
    n = len(ws)
    bc1 = 1.0 - ADAM_B1 ** ADAM_STEP
    bc2 = 1.0 - ADAM_B2 ** ADAM_STEP

    def body(*refs):
        ins, outs = refs[:4 * n], refs[4 * n:]
        for a in range(n):
            w_ref, g_ref, m_ref, v_ref = ins[a], ins[n + a], ins[2 * n + a], ins[3 * n + a]
            gv = g_ref[...]
            mn = ADAM_B1 * m_ref[...] + (1.0 - ADAM_B1) * gv
            vn = ADAM_B2 * v_ref[...] + (1.0 - ADAM_B2) * (gv * gv)
            outs[n + a][...] = mn
            outs[2 * n + a][...] = vn
            outs[a][...] = -ADAM_LR * ((mn / bc1) / (jnp.sqrt(vn / bc2) + ADAM_EPS) + ADAM_WD * w_ref[...])

    if whole:
        grid, blks = (1,), [pl.BlockSpec(w.shape, lambda i, _n=w.ndim: (0,) * _n) for w in ws]
    else:
        grid = (ws[0].shape[0], 2)
        blks = [pl.BlockSpec((1, w.shape[1] // 2, w.shape[2]), lambda i, k: (i, k, 0)) for w in ws]
    out = pl.pallas_call(
        body, name="adamw", grid=grid,
        in_specs=blks * 4, out_specs=blks * 3, out_shape=[_sds(w.shape) for w in ws] * 3,
        compiler_params=_params(("parallel",) * len(grid)),
    )(*ws, *gs, *ms, *vs)
    return [(out[a], out[n + a], out[2 * n + a]) for a in range(n)]


ADAMW_COLS_BLOCK = 512


def _adamw_cols(w_t, gs, m_t, v_t):
    cols, nl, rows = w_t.shape
    bc1 = 1.0 - ADAM_B1 ** ADAM_STEP
    bc2 = 1.0 - ADAM_B2 ** ADAM_STEP

    def body(w_ref, m_ref, v_ref, *rest):
        g_refs, (go_ref, d_ref, mo_ref, vo_ref), g_blk = rest[:nl], rest[nl:nl + 4], rest[-1]
        for l in range(nl):
            g_blk[:, l, :] = g_refs[l][...].T
        gv = g_blk[...]
        mn = ADAM_B1 * m_ref[...] + (1.0 - ADAM_B1) * gv
        vn = ADAM_B2 * v_ref[...] + (1.0 - ADAM_B2) * (gv * gv)
        go_ref[...] = gv
        mo_ref[...] = mn
        vo_ref[...] = vn
        d_ref[...] = -ADAM_LR * ((mn / bc1) / (jnp.sqrt(vn / bc2) + ADAM_EPS) + ADAM_WD * w_ref[...])

    tc = ADAMW_COLS_BLOCK
    blk = pl.BlockSpec((tc, nl, rows), lambda j: (j, 0, 0))
    gblk = pl.BlockSpec((rows, tc), lambda j: (0, j))
    return pl.pallas_call(
        body, name="adamw_cols", grid=(pl.cdiv(cols, tc),),
        in_specs=[blk] * 3 + [gblk] * nl, out_specs=[blk] * 4, out_shape=[_sds(w_t.shape)] * 4,
        scratch_shapes=[pltpu.VMEM((tc, nl, rows), F32)],
        compiler_params=_params(("parallel",)),
    )(w_t, m_t, v_t, *gs)


COL_MOVES = ((0, 0, 1024), (1024, O_SZ, 384), (1408, O_XBC, 896), (2304, O_TAIL + DT_LANE, 6), (2310, O_CQA, 256),
             (2566, O_CKV, 128), (2694, O_TAIL, 32), (2726, O_CZ, 384))


def _move_cols(w, moves, width):
    out = None
    for src, dst, n in moves:
        piece = jnp.pad(w[..., src:src + n], [(0, 0)] * (w.ndim - 1) + [(dst, width - dst - n)])
        out = piece if out is None else out + piece
    return out


def _perm_cols(w):
    return _move_cols(w, COL_MOVES, NCOL)


def _unperm_cols(g):
    return _move_cols(g, [(dst, src, n) for src, dst, n in COL_MOVES], IN_COLS)


def _wq_layout(wt):
    return jnp.pad(wt.reshape(MLA_HEADS, QK_NOPE + QK_ROPE, Q_LORA), ((0, 0), (0, 32), (0, 0))).reshape(MLA_HEADS * LANE, Q_LORA)


def _wq_unlayout(g):
    return g.reshape(MLA_HEADS, LANE, Q_LORA)[:, :QK_NOPE + QK_ROPE].reshape(MLA_HEADS * (QK_NOPE + QK_ROPE), Q_LORA)


def _wkv_layout(wt):
    t = wt.reshape(MLA_HEADS, 2, 64, KV_LORA).transpose(1, 0, 2, 3)
    return jnp.pad(t, ((0, 0), (0, 0), (0, 64), (0, 0))).reshape(2 * MLA_HEADS * LANE, KV_LORA)


def _wkv_unlayout(g):
    t = g.reshape(2, MLA_HEADS, LANE, KV_LORA)[:, :, :64]
    return t.transpose(1, 0, 2, 3).reshape(MLA_HEADS * LANE, KV_LORA)


def _rope_tables(positions):
    inv_freq = ROPE_BASE ** (-jnp.arange(0, QK_ROPE, 2, dtype=F32) / QK_ROPE)
    ang = positions.astype(F32)[:, None] * inv_freq
    cos, sin = jnp.cos(ang), jnp.sin(ang)
    s = positions.shape[0]
    one, zero = jnp.ones((s, ROPE_LANE), F32), jnp.zeros((s, ROPE_LANE), F32)
    cos_t = jnp.concatenate([one, cos, cos, one[:, :32]], axis=1)
    sin_t = jnp.concatenate([zero, -sin, sin, zero[:, :32]], axis=1)
    return cos_t, sin_t


def _ssd_scalars(dt_bias, a_log, d_skip):
    return jnp.pad(jnp.stack([dt_bias, a_log, d_skip]), ((0, 5), (DT_LANE, LANE - DT_LANE - SSD_HEADS)))


def _layer_fwd(x, lw, cos, sin, dep=None, late=None, head=None):
    proj, h, rstd = _inproj_fwd(x, lw["norm_g"], lw["w_in"], dep)
    ya = _conva_fwd(proj, lw["conv_a_w"])
    xbc = _sconv_fwd(proj, lw["ssd_conv_w"], lw["ssd_conv_b"])
    y_ssd, states = _ssd_fwd(xbc, proj, lw["sc"])
    if late is not None:
        lw = {**lw, **late(ya, y_ssd)}
    q, k, v, qn, kvn, rq, rkv = _mla_prep_fwd(proj, lw["gq"], lw["gkv"], lw["wq"], lw["wkv"], cos, sin)
    o, lse = _attn_fwd(q, k, v)
    if head is None:
        x_out, y = _outproj_fwd(x, proj, ya, y_ssd, o, lw["g_ssd"], lw["w_out"])
    else:
        *x_out, y = _outproj_loss(x, proj, ya, y_ssd, o, lw["g_ssd"], lw["w_out"], *head)
    saved = dict(x=x, proj=proj, h=h, rstd=rstd, xbc=xbc, y_ssd=y_ssd, states=states, q=q, k=k, v=v, qn=qn, kvn=kvn,
                 rq=rq, rkv=rkv, o=o, lse=lse, y=y)
    return x_out, saved, lw


def _layer_bwd(dout, lw, sv, cos, sin, rs=None, early_grads=None):
    tok = lambda: None if rs is None else rs["h"]["token"]
    dya, dys, dsz, d_o, dcz, dg_ssd, dw_out = _outproj_bwd(dout, sv["y"], lw["w_out"], sv["proj"], sv["y_ssd"], sv["o"],
                                                            lw["g_ssd"], tok())
    if rs is not None:
        rs = _rs_add_mine(rs, [dya])
    dq, dk, dv = _attn_bwd(sv["q"], sv["k"], sv["v"], sv["o"], d_o, sv["lse"], tok())
    dxbc, dtail_s, dsc = _ssd_bwd(sv["xbc"], sv["proj"], lw["sc"], sv["states"], dys, tok())
    da4, dw_conva = _conva_bwd(sv["proj"], lw["conv_a_w"], dya, tok())
    dcqa, dckv, dtail_m, dwq, dwkv, dgq, dgkv = _mla_prep_bwd(
        dq, dk, dv, sv["proj"], sv["qn"], sv["kvn"], sv["rq"], sv["rkv"], lw["gq"], lw["gkv"], lw["wq"], lw["wkv"], cos, sin)
    early = None
    if rs is not None and early_grads is not None:
        rs, early = _rs_add_chips(rs, [dq, dxbc, da4, dcqa], also=(early_grads(dw_out, dwq, dwkv), "0l"))
    elif rs is not None:
        rs = _rs_add_chips(rs, [dq, dxbc, da4, dcqa])
    du, dw_sconv, db_sconv = _sconv_bwd(sv["proj"], lw["ssd_conv_w"], lw["ssd_conv_b"], dxbc, tok())
    etok = lambda: None if early is None else early["h"]["token"]
    dproj, dx, dg = _inproj_bwd(da4, dsz, du, dcqa, dckv, dcz, dtail_s, dtail_m, lw["w_in"], sv["x"], sv["rstd"],
                                lw["norm_g"], dout, etok())
    reduced = None if rs is None else _rs_end(rs, [du, dcqa, dx])
    if early is not None:
        early = _rs_add_mine(early, [dx])
    dw_in = _dwin(sv["h"], dproj, etok())
    own = None
    if early is not None:
        early, own = _rs_add_chips(early, [dw_in], also=([dw_in], "own"))
    grads = dict(norm_g=dg, w_in=dw_in, conv_a_w=dw_conva, ssd_conv_w=dw_sconv, ssd_conv_b=db_sconv, sc=dsc,
                 g_ssd=dg_ssd, gq=dgq, wq=dwq, gkv=dgkv, wkv=dwkv, w_out=dw_out)
    return dx, grads, reduced, early, own


ANY = pl.BlockSpec(memory_space=pl.ANY)
N_CHIPS = 4
N_DEV = 8


def _place():
    return lax.axis_index("x"), lax.axis_index("y"), lax.axis_index("c")


HBM_SPEC = pl.BlockSpec(memory_space=pltpu.HBM)
SEM_SPEC = pl.BlockSpec(memory_space=pltpu.SEMAPHORE)
PAYLOAD = jnp.bfloat16


def _hbm(a):
    return pltpu.with_memory_space_constraint(a, pltpu.HBM)


def _run_plan(plan, srcs, lands, send_sems, recv_sems, start, wait):
    copies = plan(srcs, lands)
    if start:
        for i, (src, dst, _, to) in enumerate(copies):
            pltpu.make_async_remote_copy(src_ref=src, dst_ref=dst, send_sem=send_sems.at[i], recv_sem=recv_sems.at[i],
                                         device_id=to, device_id_type=MESH_T).start()
    if wait:
        for i, (src, _, arrives, to) in enumerate(copies):
            cp = pltpu.make_async_remote_copy(src_ref=src, dst_ref=arrives, send_sem=send_sems.at[i],
                                              recv_sem=recv_sems.at[i], device_id=to, device_id_type=MESH_T)
            cp.wait_send()
            cp.wait_recv()


def _exchange_start_many(name, groups, deps):
    g = len(groups)
    sizes = [(len(srcs), len(shapes)) for _, _, srcs, shapes in groups]
    n_arr = sum(ns + nl for ns, nl in sizes)
    n_in = n_arr + len(deps)

    def body(*refs):
        at = 0
        for k, ((plan, _, _, _), (ns, nl)) in enumerate(zip(groups, sizes)):
            _run_plan(plan, refs[at:at + ns], refs[at + ns:at + ns + nl], refs[n_in + 2 * k], refs[n_in + 2 * k + 1], True, False)
            at += ns + nl
        refs[-1][...] = jnp.zeros_like(refs[-1])

    arrs, thru, sems = [], [], []
    for _, n_copies, srcs, shapes in groups:
        arrs += [_hbm(a) for a in srcs] + [_hbm(lax.empty(a.shape, a.dtype)) for a in shapes]
        thru += [pltpu.HBM(a.shape, a.dtype) for a in list(srcs) + list(shapes)]
        sems += [pltpu.SemaphoreType.DMA((n_copies,))] * 2
    outs = pl.pallas_call(
        body, name=name,
        out_shape=(*sems, *thru, _sds((8, LANE))),
        in_specs=[HBM_SPEC] * n_arr + [ANY] * len(deps),
        out_specs=(*[SEM_SPEC] * (2 * g), *[HBM_SPEC] * n_arr, pl.BlockSpec(memory_space=pltpu.VMEM)),
        input_output_aliases={i: 2 * g + i for i in range(n_arr)},
        compiler_params=pltpu.CompilerParams(has_side_effects=pltpu.SideEffectType.DATAFLOW_SIDE_EFFECTING),
    )(*arrs, *deps)
    res, at = [], 2 * g
    for k, (ns, nl) in enumerate(sizes):
        res.append(((outs[2 * k], outs[2 * k + 1]), list(outs[at:at + ns]), list(outs[at + ns:at + ns + nl])))
        at += ns + nl
    return res, outs[-1]


def _exchange_start(name, plan, n_copies, srcs, land_shapes, deps):
    (one,), token = _exchange_start_many(name, [(plan, n_copies, srcs, land_shapes)], deps)
    return (*one, token)


def _exchange_wait(name, plan, sems, srcs, lands, after):
    ns, nl = len(srcs), len(lands)

    def body(*refs):
        _run_plan(plan, refs[:ns], refs[ns:ns + nl], refs[ns + nl], refs[ns + nl + 1], False, True)

    outs = pl.pallas_call(
        body, name=name,
        out_shape=[pltpu.HBM(a.shape, a.dtype) for a in list(srcs) + list(lands)],
        in_specs=[HBM_SPEC] * (ns + nl) + [SEM_SPEC, SEM_SPEC] + [ANY] * len(after), out_specs=[HBM_SPEC] * (ns + nl),
        input_output_aliases={i: i for i in range(ns + nl)},
        compiler_params=pltpu.CompilerParams(has_side_effects=pltpu.SideEffectType.DATAFLOW_SIDE_EFFECTING),
    )(*srcs, *lands, sems[0], sems[1], *after)
    return list(outs[:ns]), list(outs[ns:])


def _xchg_begin(name, plan, n_copies, srcs, land_shapes, deps=()):
    sems, srcs_t, lands_t, token = _exchange_start(name + "_start", plan, n_copies, srcs, land_shapes, list(deps))
    return dict(name=name, plan=plan, sems=sems, srcs=srcs_t, lands=lands_t, token=token)


def _xchg_begin_many(name, specs, deps=()):
    res, token = _exchange_start_many(name + "_start", [s[1:] for s in specs], list(deps))
    return [dict(name=s[0], plan=s[1], sems=sems, srcs=srcs_t, lands=lands_t, token=token)
            for s, (sems, srcs_t, lands_t) in zip(specs, res)]


def _xchg_end(h, after):
    return _exchange_wait(h["name"] + "_wait", h["plan"], h["sems"], h["srcs"], h["lands"], after)


def _other_chips():
    x, y, c = _place()
    return [(1 - x, y), (x, 1 - y), (1 - x, 1 - y)]


def _gather_plan(srcs, lands):
    x, y, c = _place()
    me = 2 * x + y
    return [(srcs[a], lands[a].at[me], lands[a].at[2 * cx + cy], (cx, cy, c))
            for (cx, cy) in _other_chips() for a in range(len(srcs))]


def _gather_spec(shards, tag):
    return (f"gather_{tag}", _gather_plan, 3 * len(shards), shards, [_sds((N_CHIPS,) + a.shape, a.dtype) for a in shards])


def _gather_end(h, after):
    shards, lands = _xchg_end(h, after)
    me = 2 * lax.axis_index("x") + lax.axis_index("y")
    return [lax.dynamic_update_index_in_dim(g, s, me, 0) for g, s in zip(lands, shards)]


def _gather_half_plan(srcs, lands):
    x, y, c = _place()
    me = 2 * x + y
    out = []
    for (cx, cy) in _other_chips():
        out.append((srcs[0].at[c], lands[0].at[me, c], lands[0].at[2 * cx + cy, c], (cx, cy, c)))
        out += [(srcs[a], lands[a].at[me], lands[a].at[2 * cx + cy], (cx, cy, c)) for a in range(1, len(srcs))]
    return out


def _forward_plan(bufs, _):
    x, y, c = _place()
    return [(bufs[0].at[2 * cx + cy, c], bufs[0].at[2 * cx + cy, c], bufs[0].at[2 * cx + cy, 1 - c], (x, y, 1 - c))
            for (cx, cy) in _other_chips()]


def _swap_plan(srcs, lands):
    x, y, c = _place()
    return [(srcs[a].at[:, 1 - c], lands[a], lands[a], (x, y, 1 - c)) for a in range(len(srcs))]


def _chips_plan(srcs, lands):
    x, y, c = _place()
    me = 2 * x + y
    return [(srcs[a].at[2 * cx + cy], lands[a].at[me], lands[a].at[2 * cx + cy], (cx, cy, c))
            for (cx, cy) in _other_chips() for a in range(len(srcs))]


def _share_plan(srcs, lands):
    x, y, c = _place()
    return [(srcs[a].at[c], srcs[a].at[c], srcs[a].at[1 - c], (x, y, 1 - c)) for a in range(len(srcs))]


def _allreduce_small(slab, dep=None):
    r = slab.shape[0]

    def body(s_ref, o_ref, gath, send_sems, recv_sems):
        x, y, c = _place()
        me = 4 * x + 2 * y + c
        gath[me] = s_ref[...]
        cps = []
        for rel in range(1, N_DEV):
            px = 1 - x if rel & 4 else x
            py = 1 - y if rel & 2 else y
            pc = 1 - c if rel & 1 else c
            cp = pltpu.make_async_remote_copy(src_ref=s_ref, dst_ref=gath.at[me], send_sem=send_sems.at[rel - 1],
                                              recv_sem=recv_sems.at[rel - 1], device_id=(px, py, pc), device_id_type=MESH_T)
            cp.start()
            cps.append(cp)
        for cp in cps:
            cp.wait()
        acc = gath[0]
        for d in range(1, N_DEV):
            acc = acc + gath[d]
        o_ref[...] = acc

    vm = pl.BlockSpec(memory_space=pltpu.VMEM)
    return _call_after(
        dep, body, (slab,), name="allreduce_small", in_specs=[vm], out_specs=vm, out_shape=_sds((r, LANE)),
        scratch_shapes=[pltpu.VMEM((N_DEV, r, LANE), F32), pltpu.SemaphoreType.DMA((N_DEV - 1,)),
                        pltpu.SemaphoreType.DMA((N_DEV - 1,))],
    )


def _add_mine(g4s, recvs, half):
    n = len(g4s)

    def body(h_ref, *refs):
        for g_ref, r_ref, o_ref in zip(refs[:n], refs[n:2 * n], refs[2 * n:]):
            o_ref[0] = (g_ref[0, 0] + r_ref[0]).astype(o_ref.dtype)

    dims = [g.shape[2:] for g in g4s]
    return pl.pallas_call(
        body, name="add_mine",
        grid_spec=pltpu.PrefetchScalarGridSpec(
            num_scalar_prefetch=1, grid=(N_CHIPS,),
            in_specs=[pl.BlockSpec((1, 1) + d, lambda j, h: (j, h[0], 0, 0)) for d in dims]
            + [pl.BlockSpec((1,) + d, lambda j, h: (j, 0, 0)) for d in dims],
            out_specs=[pl.BlockSpec((1,) + d, lambda j, h: (j, 0, 0)) for d in dims]),
        out_shape=[_sds((N_CHIPS,) + d, PAYLOAD) for d in dims],
        compiler_params=_params(("parallel",)),
    )(half, *g4s, *recvs)


def _add_chips(es, ps, me):
    n = len(es)

    def body(m_ref, *refs):
        for e_ref, p_ref, o_ref in zip(refs[:n], refs[n:2 * n], refs[2 * n:]):
            own = p_ref[0].astype(F32)
            acc = None
            for s in range(N_CHIPS):
                t = jnp.where(m_ref[0] == s, own, e_ref[s].astype(F32))
                acc = t if acc is None else acc + t
            o_ref[0] = acc

    dims = [e.shape[1:] for e in es]
    return pl.pallas_call(
        body, name="add_chips",
        grid_spec=pltpu.PrefetchScalarGridSpec(
            num_scalar_prefetch=1, grid=(1,),
            in_specs=[pl.BlockSpec((N_CHIPS,) + d, lambda i, m: (0, 0, 0)) for d in dims]
            + [pl.BlockSpec((1,) + d, lambda i, m: (m[0], 0, 0)) for d in dims],
            out_specs=[pl.BlockSpec((1,) + d, lambda i, m: (m[1], 0, 0)) for d in dims]),
        out_shape=[_sds((2,) + d) for d in dims],
        compiler_params=_params(("arbitrary",)),
    )(me, *es, *ps)


def _rs_begin(gs, tag, deps=()):
    return dict(h=_xchg_begin(*_swap_spec(gs, tag), deps), tag=tag, shapes=[g.shape for g in gs])


def _swap_spec(gs, tag):
    g4 = [g.reshape(N_CHIPS, 2, g.shape[0] // (2 * N_CHIPS), g.shape[1]) for g in gs]
    return (f"rs_swap_{tag}", _swap_plan, len(gs), g4, [_sds((N_CHIPS,) + g.shape[2:]) for g in g4])


def _rs_add_mine(st, after):
    g4, recv = _xchg_end(st["h"], after)
    half = jnp.reshape(lax.axis_index("c"), (1,)).astype(jnp.int32)
    ps = _add_mine(g4, recv, half)
    st["h"] = _xchg_begin(f"rs_chips_{st['tag']}", _chips_plan, 3 * len(ps), ps, [_sds(p.shape, p.dtype) for p in ps])
    return st


def _rs_add_chips(st, after, also=None):
    ps, es = _xchg_end(st["h"], after)
    me = jnp.stack([2 * lax.axis_index("x") + lax.axis_index("y"), lax.axis_index("c")]).astype(jnp.int32)
    fs = _add_chips(es, ps, me)
    share = (f"rs_share_{st['tag']}", _share_plan, len(fs), fs, [])
    if also is None:
        st["h"] = _xchg_begin(*share)
        return st
    gs, tag = also
    st["h"], h = _xchg_begin_many(f"rs_share_{st['tag']}_swap_{tag}", [share, _swap_spec(gs, tag)])
    return st, dict(h=h, tag=tag, shapes=[g.shape for g in gs])


def _rs_end(st, after):
    fs, _ = _xchg_end(st["h"], after)
    return [f.reshape(shp[0] // N_CHIPS, shp[1]) for f, shp in zip(fs, st["shapes"])]


WEIGHTS = ["norm_g", "w_in", "conv_a_w", "ssd_conv_w", "ssd_conv_b", "ssd_dt_bias", "ssd_a_log", "ssd_d", "ssd_norm_g",
           "mla_q_norm_g", "w_qb", "mla_kv_norm_g", "w_kvb", "w_out", "final_norm_g"]
BIG = ["w_in", "w_qb", "w_kvb", "w_out"]
SLAB_ROWS = 128


def _to_slab(parts, rows):
    flat = jnp.concatenate([p.reshape(-1) for p in parts])
    return jnp.pad(flat, (0, rows * LANE - flat.shape[0])).reshape(rows, LANE)


def _from_slab(slab, shapes):
    flat = slab.reshape(-1)
    out, off = [], 0
    for shp in shapes:
        n = int(np.prod(shp))
        out.append(flat[off:off + n].reshape(shp))
        off += n
    return out


def kernel(x, positions, norm_g, w_in, conv_a_w, ssd_conv_w, ssd_conv_b, ssd_dt_bias, ssd_a_log, ssd_d, ssd_norm_g, mla_q_norm_g, w_qb, mla_kv_norm_g, w_kvb, w_out, final_norm_g, loss_target, m_norm_g, m_w_in, m_conv_a_w, m_ssd_conv_w, m_ssd_conv_b, m_ssd_dt_bias, m_ssd_a_log, m_ssd_d, m_ssd_norm_g, m_mla_q_norm_g, m_w_qb, m_mla_kv_norm_g, m_w_kvb, m_w_out, m_final_norm_g, v_norm_g, v_w_in, v_conv_a_w, v_ssd_conv_w, v_ssd_conv_b, v_ssd_dt_bias, v_ssd_a_log, v_ssd_d, v_ssd_norm_g, v_mla_q_norm_g, v_w_qb, v_mla_kv_norm_g, v_w_kvb, v_w_out, v_final_norm_g):
    w = dict(norm_g=norm_g, w_in=w_in, conv_a_w=conv_a_w, ssd_conv_w=ssd_conv_w, ssd_conv_b=ssd_conv_b,
             ssd_dt_bias=ssd_dt_bias, ssd_a_log=ssd_a_log, ssd_d=ssd_d, ssd_norm_g=ssd_norm_g, mla_q_norm_g=mla_q_norm_g,
             w_qb=w_qb, mla_kv_norm_g=mla_kv_norm_g, w_kvb=w_kvb, w_out=w_out, final_norm_g=final_norm_g)
    mom = dict(norm_g=m_norm_g, w_in=m_w_in, conv_a_w=m_conv_a_w, ssd_conv_w=m_ssd_conv_w, ssd_conv_b=m_ssd_conv_b,
               ssd_dt_bias=m_ssd_dt_bias, ssd_a_log=m_ssd_a_log, ssd_d=m_ssd_d, ssd_norm_g=m_ssd_norm_g,
               mla_q_norm_g=m_mla_q_norm_g, w_qb=m_w_qb, mla_kv_norm_g=m_mla_kv_norm_g, w_kvb=m_w_kvb, w_out=m_w_out,
               final_norm_g=m_final_norm_g)
    var = dict(norm_g=v_norm_g, w_in=v_w_in, conv_a_w=v_conv_a_w, ssd_conv_w=v_ssd_conv_w, ssd_conv_b=v_ssd_conv_b,
               ssd_dt_bias=v_ssd_dt_bias, ssd_a_log=v_ssd_a_log, ssd_d=v_ssd_d, ssd_norm_g=v_ssd_norm_g,
               mla_q_norm_g=v_mla_q_norm_g, w_qb=v_w_qb, mla_kv_norm_g=v_mla_kv_norm_g, w_kvb=v_w_kvb, w_out=v_w_out,
               final_norm_g=v_final_norm_g)
    chip = 2 * lax.axis_index("x") + lax.axis_index("y")

    def early_shard(l, zero):
        pack = jnp.pad(conv_a_w[l], ((0, 5), (0, 192))) + jnp.pad(ssd_conv_w[l], ((3, 1), (0, 32)))
        return [(_perm_cols(w_in[l]) + zero).astype(MXU), pack + zero]

    def late_shard(l, zero):
        return [(w_out[l] + zero).astype(MXU), (w_qb[l].T + zero).astype(MXU), (w_kvb[l].T + zero).astype(MXU)]

    def early_weights(l, gathered):
        g_in, g_conv = gathered
        return dict(
            norm_g=norm_g[l][None], w_in=g_in.reshape(D_MODEL, NCOL),
            conv_a_w=jnp.concatenate([g_conv[j, 0:3, 0:64] for j in range(N_CHIPS)], axis=1),
            ssd_conv_w=jnp.concatenate([g_conv[j, 3:7, 0:224] for j in range(N_CHIPS)], axis=1),
            ssd_conv_b=ssd_conv_b[l][None], sc=_ssd_scalars(ssd_dt_bias[l], ssd_a_log[l], ssd_d[l]),
            g_ssd=ssd_norm_g[l][None], gq=mla_q_norm_g[l][None], gkv=mla_kv_norm_g[l][None])

    def late_weights(gathered):
        g_out, g_qb, g_kvb = gathered
        return dict(wq=_wq_layout(g_qb.reshape(MLA_HEADS * 96, Q_LORA)), wkv=_wkv_layout(g_kvb.reshape(MLA_HEADS * LANE, KV_LORA)),
                    w_out=g_out.reshape(D_MODEL, D_MODEL))

    def late_grads(dw_out, dwq, dwkv):
        wq = jnp.pad(_wq_unlayout(dwq).reshape(N_CHIPS, 144, Q_LORA), ((0, 0), (0, 16), (0, 0)))
        return [dw_out, wq.reshape(N_CHIPS * 160, Q_LORA), _wkv_unlayout(dwkv)]

    def large_grads(g):
        return [g["w_in"]] + late_grads(g["w_out"], g["wq"], g["wkv"])

    w_in0, pack0 = early_shard(0, 0.0)
    half = w_in0.shape[0] // 2
    gather_a0 = _xchg_begin("gather_a0", _gather_half_plan, 6, [w_in0.reshape(2, half, NCOL), pack0],
                            [_sds((N_CHIPS, 2, half, NCOL), MXU), _sds((N_CHIPS,) + pack0.shape)])
    zero = gather_a0["token"][0, 0]
    cos, sin = _rope_tables(positions[0] + zero.astype(jnp.int32))
    late0, shards1 = late_shard(0, zero), early_shard(1, zero) + late_shard(1, zero)
    mine0, (g_in0, g_conv0) = _xchg_end(gather_a0, [cos, sin] + late0 + shards1)
    forward_a0 = _xchg_begin("forward_a0", _forward_plan, 3, [g_in0], [])
    gather_b0, gather_1 = _xchg_begin_many("gather_b0_1", [_gather_spec(late0, "b0"), _gather_spec(shards1, "1")],
                                           [forward_a0["token"]])
    (g_in0,), _ = _xchg_end(forward_a0, [gather_1["token"]])
    lw0 = early_weights(0, [lax.dynamic_update_index_in_dim(g, s_, chip, 0) for g, s_ in zip((g_in0, g_conv0), mine0)])
    x1, sv0, lw0 = _layer_fwd(x[0], lw0, cos, sin, gather_1["token"],
                              lambda ya, y_ssd: late_weights(_gather_end(gather_b0, [ya, y_ssd])))
    g1 = _gather_end(gather_1, [x1])
    (dx, dgf, loss), sv1, lw1 = _layer_fwd(x1, {**early_weights(1, g1[:2]), **late_weights(g1[2:])}, cos, sin,
                                           head=(final_norm_g[None], loss_target[0]))

    dx, lg1, _, _, _ = _layer_bwd(dx, lw1, sv1, cos, sin)
    grad_x, lg0, red1, rs0_late, rs0 = _layer_bwd(dx, lw0, sv0, cos, sin, _rs_begin(large_grads(lg1), 1),
                                                  late_grads)
    lg = [lg0, lg1]
    grad = {}

    small_names = ["norm_g", "conv_a_w", "ssd_conv_w", "ssd_conv_b", "sc", "g_ssd", "gq", "gkv"]
    parts = [loss[0, 0:1], dgf]
    for nm in small_names:
        parts += [lg[l][nm][:3, DT_LANE:DT_LANE + SSD_HEADS] if nm == "sc" else lg[l][nm] for l in range(DEPTH)]
    shapes = [(1,), (D_MODEL,)] + [(DEPTH,) + shp for shp in ((D_MODEL,), (3, D_CONV_A), (4, N_XBC), (N_XBC,), (3, SSD_HEADS),
                                                              (D_SSD,), (Q_LORA,), (KV_LORA,))]
    red_slab = _allreduce_small(_to_slab(parts, SLAB_ROWS), rs0["h"]["token"])
    rs0 = _rs_add_mine(rs0, [red_slab])
    red = _from_slab(red_slab + rs0["h"]["token"][0, 0], shapes)
    loss_out = red[0][0]
    grad["final_norm_g"] = red[1]
    grad["norm_g"], conv_a_full, sconv_full, grad["ssd_conv_b"], sc_grads = red[2:7]
    grad["ssd_norm_g"], grad["mla_q_norm_g"], grad["mla_kv_norm_g"] = red[7:10]
    grad["conv_a_w"] = lax.dynamic_slice_in_dim(conv_a_full, chip * 64, 64, axis=2)
    grad["ssd_conv_w"] = lax.dynamic_slice_in_dim(sconv_full, chip * 224, 224, axis=2)
    grad["ssd_dt_bias"], grad["ssd_a_log"], grad["ssd_d"] = sc_grads[:, 0], sc_grads[:, 1], sc_grads[:, 2]

    delta, new_m, new_v = {}, {}, {}
    small = [nm for nm in WEIGHTS if nm not in BIG]
    row2 = lambda a: a[None] if a.ndim == 1 else a
    small_out = _adamw(*[[row2(a[nm]) for nm in small] for a in (w, grad, mom, var)], whole=True)
    for nm, (dv, mv, vv) in zip(small, small_out):
        delta[nm], new_m[nm], new_v[nm] = [a.reshape(w[nm].shape) for a in (dv, mv, vv)]

    r_out, r_qb, r_kvb = [jnp.stack([a, b]) for a, b in zip(_rs_end(rs0_late, [red_slab]), red1[1:])]
    late = [nm for nm in BIG if nm != "w_in"]
    view = {nm: (lambda a: a) if nm == "w_out" else (lambda a: jnp.swapaxes(a, 1, 2)) for nm in late}
    late_g = [dict(w_out=r_out, w_qb=r_qb[:, :144], w_kvb=r_kvb)[nm] for nm in late]
    late_out = _adamw(*[[view[nm](a[nm]) for nm in late] for a in (w,)], late_g,
                      *[[view[nm](a[nm]) for nm in late] for a in (mom, var)], whole=False)
    for nm, gv, (dv, mv, vv) in zip(late, late_g, late_out):
        grad[nm], delta[nm], new_m[nm], new_v[nm] = [view[nm](a) for a in (gv, dv, mv, vv)]
    g_in1 = _unperm_cols(red1[0])
    shadow_work = [a for row in small_out + late_out for a in row] + [grad[nm] for nm in small] + [g_in1]
    r_in0, = _rs_end(_rs_add_chips(rs0, shadow_work), [])
    to_cols, from_cols = (lambda a: jnp.transpose(a, (2, 0, 1))), (lambda a: jnp.transpose(a, (1, 2, 0)))
    grad["w_in"], delta["w_in"], new_m["w_in"], new_v["w_in"] = [from_cols(a) for a in _adamw_cols(
        to_cols(w["w_in"]), [_unperm_cols(r_in0), g_in1], to_cols(mom["w_in"]), to_cols(var["w_in"]))]

    return (loss_out, grad_x[None], *[grad[nm] for nm in WEIGHTS], *[delta[nm] for nm in WEIGHTS],
            *[new_m[nm] for nm in WEIGHTS], *[new_v[nm] for nm in WEIGHTS])
```
